```python
import math
import jax, jax.numpy as jnp
from jax import lax
import numpy as np

D_MODEL = 1024
BATCH = 16
SEQ = 2048
DEPTH = 1

DN_HEAD_DIM = 128
DN_WIDTH = D_MODEL // 2
DN_HEADS = DN_WIDTH // DN_HEAD_DIM
SHORT_CONV = 4
GLA_VAL_DIM = 128
GLA_WIDTH = D_MODEL - DN_WIDTH
GLA_HEADS = GLA_WIDTH // GLA_VAL_DIM
GLA_KEY_DIM = GLA_VAL_DIM // 2
GLA_GATE_RANK = 16
GLA_TAU = 16.0
MIX_WIDTH = DN_WIDTH + GLA_WIDTH
IN_SPLITS = (DN_WIDTH, DN_WIDTH, DN_WIDTH, DN_WIDTH, DN_HEADS, DN_HEADS,
             GLA_HEADS * GLA_KEY_DIM, GLA_HEADS * GLA_KEY_DIM, GLA_WIDTH, GLA_WIDTH, GLA_GATE_RANK)
IN_WIDTH = sum(IN_SPLITS)
CHUNK = 64
D_FF = 2816
FFN_CONV = 3
ALPHA = (2.0 * DEPTH) ** 0.25
BETA_INIT = (8.0 * DEPTH) ** -0.25
EPS = 1e-6

kernel_name = "hybrid_deltanet_gla_convffn_deepnorm_adaln"


def layer_norm(x, g, b):
    xf = x.astype(jnp.float32)
    mu = xf.mean(-1, keepdims=True)
    var = jnp.square(xf - mu).mean(-1, keepdims=True)
    return ((xf - mu) * lax.rsqrt(var + EPS) * g.astype(jnp.float32) + b.astype(jnp.float32)).astype(x.dtype)


def rms_norm(x, g):
    xf = x.astype(jnp.float32)
    return (xf * lax.rsqrt(jnp.mean(xf * xf, -1, keepdims=True) + EPS) * g.astype(jnp.float32)).astype(x.dtype)


def l2_norm(x):
    return x * lax.rsqrt(jnp.sum(x * x, -1, keepdims=True) + EPS)


def causal_dwconv(x, w):
    k_w, ch = w.shape
    return lax.conv_general_dilated(x, w[:, None, :].astype(x.dtype), window_strides=(1,),
                                    padding=[(k_w - 1, 0)], dimension_numbers=('NWC', 'WIO', 'NWC'),
                                    feature_group_count=ch)


def to_chunks(x):
    b, t, h, d = x.shape
    return x.reshape(b, t // CHUNK, CHUNK, h, d).transpose(1, 0, 3, 2, 4)


def from_chunks(x):
    n, b, h, c, d = x.shape
    return x.transpose(1, 0, 3, 2, 4).reshape(b, n * c, h, d)


def gated_delta_rule(q, k, v, log_a, beta):
    dt = v.dtype
    q, k, v, log_a, beta = (a.astype(jnp.float32) for a in (q, k, v, log_a, beta))
    bsz, _, nh, dk = q.shape
    dv = v.shape[-1]
    qc, kc, vc = to_chunks(q * dk ** -0.5), to_chunks(k), to_chunks(v)
    G = jnp.cumsum(to_chunks(log_a[..., None])[..., 0], axis=-1)
    bc = to_chunks(beta[..., None])[..., 0]
    causal = jnp.tril(jnp.ones((CHUNK, CHUNK), dtype=bool))
    strict = jnp.tril(jnp.ones((CHUNK, CHUNK), dtype=bool), -1)
    decay = jnp.exp(jnp.where(causal, G[..., :, None] - G[..., None, :], -jnp.inf))
    kb = kc * bc[..., None]
    m_low = jnp.where(strict, jnp.einsum('nbhid,nbhjd->nbhij', kb, kc) * decay, 0.0)
    lhs = m_low + jnp.eye(CHUNK, dtype=jnp.float32)
    rhs = jnp.concatenate([vc * bc[..., None], kb * jnp.exp(G)[..., None]], axis=-1)
    sol = lax.linalg.triangular_solve(lhs, rhs, left_side=True, lower=True, unit_diagonal=True)
    u, w = sol[..., :dv], sol[..., dv:]
    attn = jnp.einsum('nbhid,nbhjd->nbhij', qc, kc) * decay
    q_dec = qc * jnp.exp(G)[..., None]
    k_dec = kc * jnp.exp(G[..., -1:] - G)[..., None]
    g_last = jnp.exp(G[..., -1])

    def step(S, inp):
        u_i, w_i, attn_i, qd_i, kd_i, gl_i = inp
        v_new = u_i - jnp.einsum('bhcd,bhde->bhce', w_i, S)
        o = jnp.einsum('bhcd,bhde->bhce', qd_i, S) + jnp.einsum('bhij,bhje->bhie', attn_i, v_new)
        S = S * gl_i[..., None, None] + jnp.einsum('bhcd,bhce->bhde', kd_i, v_new)
        return S, o

    S0 = jnp.zeros((bsz, nh, dk, dv), jnp.float32)
    _, o = lax.scan(step, S0, (u, w, attn, q_dec, k_dec, g_last))
    return from_chunks(o).astype(dt)


def gla_attention(q, k, v, log_alpha):
    dt = v.dtype
    q, k, v, log_alpha = (a.astype(jnp.float32) for a in (q, k, v, log_alpha))
    bsz, _, nh, dk = q.shape
    dv = v.shape[-1]
    qc, kc, vc = to_chunks(q * dk ** -0.5), to_chunks(k), to_chunks(v)
    b = jnp.cumsum(to_chunks(log_alpha), axis=-2)
    causal = jnp.tril(jnp.ones((CHUNK, CHUNK), dtype=bool))
    q_dec = qc * jnp.exp(b)
    attn = jnp.where(causal, jnp.einsum('nbhid,nbhjd->nbhij', q_dec, kc * jnp.exp(-b)), 0.0)
    o_intra = jnp.einsum('nbhij,nbhje->nbhie', attn, vc)
    k_dec = kc * jnp.exp(b[..., -1:, :] - b)
    g_last = jnp.exp(b[..., -1, :])

    def step(S, inp):
        qd_i, kd_i, v_i, oi_i, gl_i = inp
        o = jnp.einsum('bhcd,bhde->bhce', qd_i, S) + oi_i
        S = S * gl_i[..., :, None] + jnp.einsum('bhcd,bhce->bhde', kd_i, v_i)
        return S, o

    S0 = jnp.zeros((bsz, nh, dk, dv), jnp.float32)
    _, o = lax.scan(step, S0, (q_dec, k_dec, vc, o_intra, g_last))
    return from_chunks(o).astype(dt)


def hybrid_mixer(h, w_in, dn_conv, dn_a_log, dn_dt_bias, dn_norm_g, gla_w_gate2, gla_b_gate, gla_norm_g, w_o):
    bsz, t, _ = h.shape
    proj = h @ w_in
    offsets, acc = [], 0
    for s in IN_SPLITS[:-1]:
        acc += s
        offsets.append(acc)
    (dn_q, dn_k, dn_v, dn_z, dn_a, dn_b, gl_q, gl_k, gl_v, gl_g, gl_r) = jnp.split(proj, offsets, axis=-1)
    qkv = jax.nn.silu(causal_dwconv(jnp.concatenate([dn_q, dn_k, dn_v], -1), dn_conv))
    q, k, v = (a.reshape(bsz, t, DN_HEADS, DN_HEAD_DIM) for a in jnp.split(qkv, 3, axis=-1))
    log_a = -jnp.exp(dn_a_log) * jax.nn.softplus(dn_a + dn_dt_bias)
    beta = jax.nn.sigmoid(dn_b)
    o_dn = gated_delta_rule(l2_norm(q), l2_norm(k), v, log_a, beta)
    o_dn = rms_norm(o_dn, dn_norm_g) * jax.nn.silu(dn_z.reshape(bsz, t, DN_HEADS, DN_HEAD_DIM))
    log_alpha = jax.nn.log_sigmoid(gl_r @ gla_w_gate2 + gla_b_gate) / GLA_TAU
    o_gla = gla_attention(gl_q.reshape(bsz, t, GLA_HEADS, GLA_KEY_DIM),
                          gl_k.reshape(bsz, t, GLA_HEADS, GLA_KEY_DIM),
                          gl_v.reshape(bsz, t, GLA_HEADS, GLA_VAL_DIM),
                          log_alpha.reshape(bsz, t, GLA_HEADS, GLA_KEY_DIM))
    o_gla = rms_norm(o_gla, gla_norm_g) * jax.nn.silu(gl_g.reshape(bsz, t, GLA_HEADS, GLA_VAL_DIM))
    o = jnp.concatenate([o_dn.reshape(bsz, t, DN_WIDTH), o_gla.reshape(bsz, t, GLA_WIDTH)], axis=-1)
    return o @ w_o


def conv_ffn(h, w_up, conv_w, conv_b, w_down):
    u = causal_dwconv(h @ w_up, conv_w) + conv_b
    gate, val = jnp.split(u, 2, axis=-1)
    return (jax.nn.silu(gate) * val) @ w_down


def _fwd_setup_inputs(seed: int = 0) -> dict:
    key = jax.random.key(seed)
    ks = jax.random.split(key, 24)
    nrm = lambda k, shape, s: jax.random.normal(k, shape, jnp.float32) * s
    L, D = DEPTH, D_MODEL
    dt = jnp.exp(jax.random.uniform(ks[9], (L, DN_HEADS), jnp.float32, math.log(1e-3), math.log(1e-1)))
    return {
        "x": nrm(ks[0], (BATCH, SEQ, D), 1.0),
        "c": nrm(ks[1], (BATCH, D), 1.0),
        "ln0_g": 1.0 + nrm(ks[2], (D,), 0.02),
        "ln0_b": nrm(ks[3], (D,), 0.02),
        "w_ada": nrm(ks[4], (L, D, 6 * D), 0.1 * D ** -0.5),
        "b_ada": nrm(ks[5], (L, 6 * D), 0.01),
        "w_in": nrm(ks[6], (L, D, IN_WIDTH), D ** -0.5),
        "dn_conv": nrm(ks[7], (L, SHORT_CONV, 3 * DN_WIDTH), SHORT_CONV ** -0.5),
        "dn_a_log": jnp.log(jax.random.uniform(ks[8], (L, DN_HEADS), jnp.float32, 1.0, 16.0)),
        "dn_dt_bias": dt + jnp.log(-jnp.expm1(-dt)),
        "dn_norm_g": 1.0 + nrm(ks[10], (L, DN_HEAD_DIM), 0.02),
        "gla_w_gate2": nrm(ks[11], (L, GLA_GATE_RANK, GLA_HEADS * GLA_KEY_DIM), GLA_GATE_RANK ** -0.5),
        "gla_b_gate": nrm(ks[12], (L, GLA_HEADS * GLA_KEY_DIM), 0.01),
        "gla_norm_g": 1.0 + nrm(ks[13], (L, GLA_VAL_DIM), 0.02),
        "w_o": nrm(ks[14], (L, MIX_WIDTH, D), MIX_WIDTH ** -0.5 * BETA_INIT),
        "ln1_g": 1.0 + nrm(ks[15], (L, D), 0.02),
        "ln1_b": nrm(ks[16], (L, D), 0.02),
        "ffn_w_up": nrm(ks[17], (L, D, 2 * D_FF), D ** -0.5),
        "ffn_conv": nrm(ks[18], (L, FFN_CONV, 2 * D_FF), FFN_CONV ** -0.5),
        "ffn_conv_b": nrm(ks[19], (L, 2 * D_FF), 0.02),
        "ffn_w_down": nrm(ks[20], (L, D_FF, D), D_FF ** -0.5 * BETA_INIT),
        "ln2_g": 1.0 + nrm(ks[21], (L, D), 0.02),
        "ln2_b": nrm(ks[22], (L, D), 0.02),
    }


def _fwd_reference(x, c, ln0_g, ln0_b, w_ada, b_ada, w_in, dn_conv, dn_a_log, dn_dt_bias, dn_norm_g,
              gla_w_gate2, gla_b_gate, gla_norm_g, w_o, ln1_g, ln1_b, ffn_w_up, ffn_conv, ffn_conv_b,
              ffn_w_down, ln2_g, ln2_b):
    x = layer_norm(x, ln0_g, ln0_b)
    cond = jax.nn.silu(c)
    for l in range(DEPTH):
        mod = cond @ w_ada[l] + b_ada[l]
        sh_a, sc_a, gt_a, sh_f, sc_f, gt_f = jnp.split(mod[:, None, :], 6, axis=-1)
        h = x * (1.0 + sc_a) + sh_a
        y = hybrid_mixer(h, w_in[l], dn_conv[l], dn_a_log[l], dn_dt_bias[l], dn_norm_g[l],
                         gla_w_gate2[l], gla_b_gate[l], gla_norm_g[l], w_o[l])
        x = layer_norm(ALPHA * x + (1.0 + gt_a) * y, ln1_g[l], ln1_b[l])
        h = x * (1.0 + sc_f) + sh_f
        y = conv_ffn(h, ffn_w_up[l], ffn_conv[l], ffn_conv_b[l], ffn_w_down[l])
        x = layer_norm(ALPHA * x + (1.0 + gt_f) * y, ln2_g[l], ln2_b[l])
    return x


import jax as _jax
import jax.numpy as _jnp

TWIN_FORMAT = 'train_step'
FWD_PARAMS = ['x', 'c', 'ln0_g', 'ln0_b', 'w_ada', 'b_ada', 'w_in', 'dn_conv', 'dn_a_log', 'dn_dt_bias', 'dn_norm_g', 'gla_w_gate2', 'gla_b_gate', 'gla_norm_g', 'w_o', 'ln1_g', 'ln1_b', 'ffn_w_up', 'ffn_conv', 'ffn_conv_b', 'ffn_w_down', 'ln2_g', 'ln2_b']
TWIN_WEIGHTS = ['ln0_g', 'ln0_b', 'w_ada', 'b_ada', 'w_in', 'dn_conv', 'dn_a_log', 'dn_dt_bias', 'dn_norm_g', 'gla_w_gate2', 'gla_b_gate', 'gla_norm_g', 'w_o', 'ln1_g', 'ln1_b', 'ffn_w_up', 'ffn_conv', 'ffn_conv_b', 'ffn_w_down', 'ln2_g', 'ln2_b']
TWIN_DIFF_INPUT = 'x'
TWIN_INPUTS = ['x', 'c', 'ln0_g', 'ln0_b', 'w_ada', 'b_ada', 'w_in', 'dn_conv', 'dn_a_log', 'dn_dt_bias', 'dn_norm_g', 'gla_w_gate2', 'gla_b_gate', 'gla_norm_g', 'w_o', 'ln1_g', 'ln1_b', 'ffn_w_up', 'ffn_conv', 'ffn_conv_b', 'ffn_w_down', 'ln2_g', 'ln2_b', 'loss_target', 'm_ln0_g', 'm_ln0_b', 'm_w_ada', 'm_b_ada', 'm_w_in', 'm_dn_conv', 'm_dn_a_log', 'm_dn_dt_bias', 'm_dn_norm_g', 'm_gla_w_gate2', 'm_gla_b_gate', 'm_gla_norm_g', 'm_w_o', 'm_ln1_g', 'm_ln1_b', 'm_ffn_w_up', 'm_ffn_conv', 'm_ffn_conv_b', 'm_ffn_w_down', 'm_ln2_g', 'm_ln2_b', 'v_ln0_g', 'v_ln0_b', 'v_w_ada', 'v_b_ada', 'v_w_in', 'v_dn_conv', 'v_dn_a_log', 'v_dn_dt_bias', 'v_dn_norm_g', 'v_gla_w_gate2', 'v_gla_b_gate', 'v_gla_norm_g', 'v_w_o', 'v_ln1_g', 'v_ln1_b', 'v_ffn_w_up', 'v_ffn_conv', 'v_ffn_conv_b', 'v_ffn_w_down', 'v_ln2_g', 'v_ln2_b']
TWIN_OUTPUTS = ['loss', 'grad_x', 'grad_ln0_g', 'grad_ln0_b', 'grad_w_ada', 'grad_b_ada', 'grad_w_in', 'grad_dn_conv', 'grad_dn_a_log', 'grad_dn_dt_bias', 'grad_dn_norm_g', 'grad_gla_w_gate2', 'grad_gla_b_gate', 'grad_gla_norm_g', 'grad_w_o', 'grad_ln1_g', 'grad_ln1_b', 'grad_ffn_w_up', 'grad_ffn_conv', 'grad_ffn_conv_b', 'grad_ffn_w_down', 'grad_ln2_g', 'grad_ln2_b', 'delta_ln0_g', 'delta_ln0_b', 'delta_w_ada', 'delta_b_ada', 'delta_w_in', 'delta_dn_conv', 'delta_dn_a_log', 'delta_dn_dt_bias', 'delta_dn_norm_g', 'delta_gla_w_gate2', 'delta_gla_b_gate', 'delta_gla_norm_g', 'delta_w_o', 'delta_ln1_g', 'delta_ln1_b', 'delta_ffn_w_up', 'delta_ffn_conv', 'delta_ffn_conv_b', 'delta_ffn_w_down', 'delta_ln2_g', 'delta_ln2_b', 'new_m_ln0_g', 'new_m_ln0_b', 'new_m_w_ada', 'new_m_b_ada', 'new_m_w_in', 'new_m_dn_conv', 'new_m_dn_a_log', 'new_m_dn_dt_bias', 'new_m_dn_norm_g', 'new_m_gla_w_gate2', 'new_m_gla_b_gate', 'new_m_gla_norm_g', 'new_m_w_o', 'new_m_ln1_g', 'new_m_ln1_b', 'new_m_ffn_w_up', 'new_m_ffn_conv', 'new_m_ffn_conv_b', 'new_m_ffn_w_down', 'new_m_ln2_g', 'new_m_ln2_b', 'new_v_ln0_g', 'new_v_ln0_b', 'new_v_w_ada', 'new_v_b_ada', 'new_v_w_in', 'new_v_dn_conv', 'new_v_dn_a_log', 'new_v_dn_dt_bias', 'new_v_dn_norm_g', 'new_v_gla_w_gate2', 'new_v_gla_b_gate', 'new_v_gla_norm_g', 'new_v_w_o', 'new_v_ln1_g', 'new_v_ln1_b', 'new_v_ffn_w_up', 'new_v_ffn_conv', 'new_v_ffn_conv_b', 'new_v_ffn_w_down', 'new_v_ln2_g', 'new_v_ln2_b']
TWIN_LEAF_KINDS = {'loss': 'loss', 'grad_x': 'grad_x', 'grad_ln0_g': 'grad_w', 'grad_ln0_b': 'grad_w', 'grad_w_ada': 'grad_w', 'grad_b_ada': 'grad_w', 'grad_w_in': 'grad_w', 'grad_dn_conv': 'grad_w', 'grad_dn_a_log': 'grad_w', 'grad_dn_dt_bias': 'grad_w', 'grad_dn_norm_g': 'grad_w', 'grad_gla_w_gate2': 'grad_w', 'grad_gla_b_gate': 'grad_w', 'grad_gla_norm_g': 'grad_w', 'grad_w_o': 'grad_w', 'grad_ln1_g': 'grad_w', 'grad_ln1_b': 'grad_w', 'grad_ffn_w_up': 'grad_w', 'grad_ffn_conv': 'grad_w', 'grad_ffn_conv_b': 'grad_w', 'grad_ffn_w_down': 'grad_w', 'grad_ln2_g': 'grad_w', 'grad_ln2_b': 'grad_w', 'delta_ln0_g': 'delta_w', 'delta_ln0_b': 'delta_w', 'delta_w_ada': 'delta_w', 'delta_b_ada': 'delta_w', 'delta_w_in': 'delta_w', 'delta_dn_conv': 'delta_w', 'delta_dn_a_log': 'delta_w', 'delta_dn_dt_bias': 'delta_w', 'delta_dn_norm_g': 'delta_w', 'delta_gla_w_gate2': 'delta_w', 'delta_gla_b_gate': 'delta_w', 'delta_gla_norm_g': 'delta_w', 'delta_w_o': 'delta_w', 'delta_ln1_g': 'delta_w', 'delta_ln1_b': 'delta_w', 'delta_ffn_w_up': 'delta_w', 'delta_ffn_conv': 'delta_w', 'delta_ffn_conv_b': 'delta_w', 'delta_ffn_w_down': 'delta_w', 'delta_ln2_g': 'delta_w', 'delta_ln2_b': 'delta_w', 'new_m_ln0_g': 'new_m', 'new_m_ln0_b': 'new_m', 'new_m_w_ada': 'new_m', 'new_m_b_ada': 'new_m', 'new_m_w_in': 'new_m', 'new_m_dn_conv': 'new_m', 'new_m_dn_a_log': 'new_m', 'new_m_dn_dt_bias': 'new_m', 'new_m_dn_norm_g': 'new_m', 'new_m_gla_w_gate2': 'new_m', 'new_m_gla_b_gate': 'new_m', 'new_m_gla_norm_g': 'new_m', 'new_m_w_o': 'new_m', 'new_m_ln1_g': 'new_m', 'new_m_ln1_b': 'new_m', 'new_m_ffn_w_up': 'new_m', 'new_m_ffn_conv': 'new_m', 'new_m_ffn_conv_b': 'new_m', 'new_m_ffn_w_down': 'new_m', 'new_m_ln2_g': 'new_m', 'new_m_ln2_b': 'new_m', 'new_v_ln0_g': 'new_v', 'new_v_ln0_b': 'new_v', 'new_v_w_ada': 'new_v', 'new_v_b_ada': 'new_v', 'new_v_w_in': 'new_v', 'new_v_dn_conv': 'new_v', 'new_v_dn_a_log': 'new_v', 'new_v_dn_dt_bias': 'new_v', 'new_v_dn_norm_g': 'new_v', 'new_v_gla_w_gate2': 'new_v', 'new_v_gla_b_gate': 'new_v', 'new_v_gla_norm_g': 'new_v', 'new_v_w_o': 'new_v', 'new_v_ln1_g': 'new_v', 'new_v_ln1_b': 'new_v', 'new_v_ffn_w_up': 'new_v', 'new_v_ffn_conv': 'new_v', 'new_v_ffn_conv_b': 'new_v', 'new_v_ffn_w_down': 'new_v', 'new_v_ln2_g': 'new_v', 'new_v_ln2_b': 'new_v'}


def _forward(args):
    return _fwd_reference(*[args[k] for k in FWD_PARAMS])


def _output_shape():
    out = _jax.eval_shape(lambda: _forward(_fwd_setup_inputs(0)))
    return out.shape, out.dtype

N_MICROBATCH = 1
ADAM_LR = 0.001
ADAM_B1 = 0.9
ADAM_B2 = 0.999
ADAM_EPS = 1e-08
ADAM_WD = 0.01
ADAM_STEP = 10
PER_EXAMPLE_BATCH_AXIS = {'x': 0, 'c': 0, 'loss_target': 0}
SHARED_INPUTS = []
_WEIGHT_DTYPES = {'ln0_g': _jnp.float32, 'ln0_b': _jnp.float32, 'w_ada': _jnp.float32, 'b_ada': _jnp.float32, 'w_in': _jnp.float32, 'dn_conv': _jnp.float32, 'dn_a_log': _jnp.float32, 'dn_dt_bias': _jnp.float32, 'dn_norm_g': _jnp.float32, 'gla_w_gate2': _jnp.float32, 'gla_b_gate': _jnp.float32, 'gla_norm_g': _jnp.float32, 'w_o': _jnp.float32, 'ln1_g': _jnp.float32, 'ln1_b': _jnp.float32, 'ffn_w_up': _jnp.float32, 'ffn_conv': _jnp.float32, 'ffn_conv_b': _jnp.float32, 'ffn_w_down': _jnp.float32, 'ln2_g': _jnp.float32, 'ln2_b': _jnp.float32}
MOMENT_SCALE = {'ln0_g': 8.076109e-01, 'ln0_b': 4.247208e-01, 'w_ada': 5.235454e-02, 'b_ada': 8.996501e-02, 'w_in': 5.329968e-02, 'dn_conv': 4.344068e-02, 'dn_a_log': 5.533799e-01, 'dn_dt_bias': 5.526547e-01, 'dn_norm_g': 1.472841e-01, 'gla_w_gate2': 9.327260e-03, 'gla_b_gate': 4.165695e-02, 'gla_norm_g': 1.048270e-01, 'w_o': 9.229439e-02, 'ln1_g': 9.444955e-01, 'ln1_b': 4.311951e-01, 'ffn_w_up': 3.149934e-02, 'ffn_conv': 3.172831e-02, 'ffn_conv_b': 3.441782e-02, 'ffn_w_down': 8.697393e-02, 'ln2_g': 3.204731e+01, 'ln2_b': 1.438956e+00}


def _to_microbatches(a, axis):
    t = _jnp.moveaxis(a, axis, 0)
    t = t.reshape((N_MICROBATCH, t.shape[0] // N_MICROBATCH) + t.shape[1:])
    return _jnp.moveaxis(t, 1, axis + 1)


def setup_inputs(seed: int = 0) -> dict:
    inp = _fwd_setup_inputs(seed)
    key = _jax.random.fold_in(_jax.random.key(seed), 7919)
    shape, _ = _output_shape()
    out = dict(inp)
    out["loss_target"] = _jax.random.normal(_jax.random.fold_in(key, 0), shape, _jnp.float32)
    for i, name in enumerate(TWIN_WEIGHTS):
        w = inp[name].astype(_jnp.float32)
        if MOMENT_SCALE is None:
            s = _jnp.sqrt(_jnp.mean(_jnp.square(w)) + 1e-30)
        else:
            s = MOMENT_SCALE[name]
        km, kv = _jax.random.split(_jax.random.fold_in(key, i + 1))
        out[name] = w
        out["m_" + name] = s * _jax.random.normal(km, w.shape, _jnp.float32)
        out["v_" + name] = (s * s) * _jax.random.uniform(kv, w.shape, _jnp.float32, 0.5, 1.5)
    if N_MICROBATCH > 1:
        for name, axis in PER_EXAMPLE_BATCH_AXIS.items():
            out[name] = _to_microbatches(out[name], axis)
    return {'x': out['x'], 'c': out['c'], 'ln0_g': out['ln0_g'], 'ln0_b': out['ln0_b'], 'w_ada': out['w_ada'], 'b_ada': out['b_ada'], 'w_in': out['w_in'], 'dn_conv': out['dn_conv'], 'dn_a_log': out['dn_a_log'], 'dn_dt_bias': out['dn_dt_bias'], 'dn_norm_g': out['dn_norm_g'], 'gla_w_gate2': out['gla_w_gate2'], 'gla_b_gate': out['gla_b_gate'], 'gla_norm_g': out['gla_norm_g'], 'w_o': out['w_o'], 'ln1_g': out['ln1_g'], 'ln1_b': out['ln1_b'], 'ffn_w_up': out['ffn_w_up'], 'ffn_conv': out['ffn_conv'], 'ffn_conv_b': out['ffn_conv_b'], 'ffn_w_down': out['ffn_w_down'], 'ln2_g': out['ln2_g'], 'ln2_b': out['ln2_b'], 'loss_target': out['loss_target'], 'm_ln0_g': out['m_ln0_g'], 'm_ln0_b': out['m_ln0_b'], 'm_w_ada': out['m_w_ada'], 'm_b_ada': out['m_b_ada'], 'm_w_in': out['m_w_in'], 'm_dn_conv': out['m_dn_conv'], 'm_dn_a_log': out['m_dn_a_log'], 'm_dn_dt_bias': out['m_dn_dt_bias'], 'm_dn_norm_g': out['m_dn_norm_g'], 'm_gla_w_gate2': out['m_gla_w_gate2'], 'm_gla_b_gate': out['m_gla_b_gate'], 'm_gla_norm_g': out['m_gla_norm_g'], 'm_w_o': out['m_w_o'], 'm_ln1_g': out['m_ln1_g'], 'm_ln1_b': out['m_ln1_b'], 'm_ffn_w_up': out['m_ffn_w_up'], 'm_ffn_conv': out['m_ffn_conv'], 'm_ffn_conv_b': out['m_ffn_conv_b'], 'm_ffn_w_down': out['m_ffn_w_down'], 'm_ln2_g': out['m_ln2_g'], 'm_ln2_b': out['m_ln2_b'], 'v_ln0_g': out['v_ln0_g'], 'v_ln0_b': out['v_ln0_b'], 'v_w_ada': out['v_w_ada'], 'v_b_ada': out['v_b_ada'], 'v_w_in': out['v_w_in'], 'v_dn_conv': out['v_dn_conv'], 'v_dn_a_log': out['v_dn_a_log'], 'v_dn_dt_bias': out['v_dn_dt_bias'], 'v_dn_norm_g': out['v_dn_norm_g'], 'v_gla_w_gate2': out['v_gla_w_gate2'], 'v_gla_b_gate': out['v_gla_b_gate'], 'v_gla_norm_g': out['v_gla_norm_g'], 'v_w_o': out['v_w_o'], 'v_ln1_g': out['v_ln1_g'], 'v_ln1_b': out['v_ln1_b'], 'v_ffn_w_up': out['v_ffn_w_up'], 'v_ffn_conv': out['v_ffn_conv'], 'v_ffn_conv_b': out['v_ffn_conv_b'], 'v_ffn_w_down': out['v_ffn_w_down'], 'v_ln2_g': out['v_ln2_g'], 'v_ln2_b': out['v_ln2_b']}


def _loss(weights, diff, rest, loss_target):
    with _jax.named_scope("forward"):
        args = {**rest, TWIN_DIFF_INPUT: diff, **{k: w.astype(_WEIGHT_DTYPES[k]) for k, w in weights.items()}}
        y = _forward(args)
    with _jax.named_scope("loss_head"):
        err = _jnp.square(y.astype(_jnp.float32) - loss_target)
        return 0.5 * _jnp.sum(_jnp.mean(err, axis=-1)) if err.ndim else 0.5 * err


def _adamw(w, g, m, v):
    m = ADAM_B1 * m + (1.0 - ADAM_B1) * g
    v = ADAM_B2 * v + (1.0 - ADAM_B2) * _jnp.square(g)
    m_hat = m / (1.0 - ADAM_B1 ** ADAM_STEP)
    v_hat = v / (1.0 - ADAM_B2 ** ADAM_STEP)
    delta = -ADAM_LR * (m_hat / (_jnp.sqrt(v_hat) + ADAM_EPS) + ADAM_WD * w)
    return delta, m, v


def reference(x, c, ln0_g, ln0_b, w_ada, b_ada, w_in, dn_conv, dn_a_log, dn_dt_bias, dn_norm_g, gla_w_gate2, gla_b_gate, gla_norm_g, w_o, ln1_g, ln1_b, ffn_w_up, ffn_conv, ffn_conv_b, ffn_w_down, ln2_g, ln2_b, loss_target, m_ln0_g, m_ln0_b, m_w_ada, m_b_ada, m_w_in, m_dn_conv, m_dn_a_log, m_dn_dt_bias, m_dn_norm_g, m_gla_w_gate2, m_gla_b_gate, m_gla_norm_g, m_w_o, m_ln1_g, m_ln1_b, m_ffn_w_up, m_ffn_conv, m_ffn_conv_b, m_ffn_w_down, m_ln2_g, m_ln2_b, v_ln0_g, v_ln0_b, v_w_ada, v_b_ada, v_w_in, v_dn_conv, v_dn_a_log, v_dn_dt_bias, v_dn_norm_g, v_gla_w_gate2, v_gla_b_gate, v_gla_norm_g, v_w_o, v_ln1_g, v_ln1_b, v_ffn_w_up, v_ffn_conv, v_ffn_conv_b, v_ffn_w_down, v_ln2_g, v_ln2_b):
    given = dict(x=x, c=c, ln0_g=ln0_g, ln0_b=ln0_b, w_ada=w_ada, b_ada=b_ada, w_in=w_in, dn_conv=dn_conv, dn_a_log=dn_a_log, dn_dt_bias=dn_dt_bias, dn_norm_g=dn_norm_g, gla_w_gate2=gla_w_gate2, gla_b_gate=gla_b_gate, gla_norm_g=gla_norm_g, w_o=w_o, ln1_g=ln1_g, ln1_b=ln1_b, ffn_w_up=ffn_w_up, ffn_conv=ffn_conv, ffn_conv_b=ffn_conv_b, ffn_w_down=ffn_w_down, ln2_g=ln2_g, ln2_b=ln2_b, loss_target=loss_target, m_ln0_g=m_ln0_g, m_ln0_b=m_ln0_b, m_w_ada=m_w_ada, m_b_ada=m_b_ada, m_w_in=m_w_in, m_dn_conv=m_dn_conv, m_dn_a_log=m_dn_a_log, m_dn_dt_bias=m_dn_dt_bias, m_dn_norm_g=m_dn_norm_g, m_gla_w_gate2=m_gla_w_gate2, m_gla_b_gate=m_gla_b_gate, m_gla_norm_g=m_gla_norm_g, m_w_o=m_w_o, m_ln1_g=m_ln1_g, m_ln1_b=m_ln1_b, m_ffn_w_up=m_ffn_w_up, m_ffn_conv=m_ffn_conv, m_ffn_conv_b=m_ffn_conv_b, m_ffn_w_down=m_ffn_w_down, m_ln2_g=m_ln2_g, m_ln2_b=m_ln2_b, v_ln0_g=v_ln0_g, v_ln0_b=v_ln0_b, v_w_ada=v_w_ada, v_b_ada=v_b_ada, v_w_in=v_w_in, v_dn_conv=v_dn_conv, v_dn_a_log=v_dn_a_log, v_dn_dt_bias=v_dn_dt_bias, v_dn_norm_g=v_dn_norm_g, v_gla_w_gate2=v_gla_w_gate2, v_gla_b_gate=v_gla_b_gate, v_gla_norm_g=v_gla_norm_g, v_w_o=v_w_o, v_ln1_g=v_ln1_g, v_ln1_b=v_ln1_b, v_ffn_w_up=v_ffn_w_up, v_ffn_conv=v_ffn_conv, v_ffn_conv_b=v_ffn_conv_b, v_ffn_w_down=v_ffn_w_down, v_ln2_g=v_ln2_g, v_ln2_b=v_ln2_b)
    weights = {n: given[n] for n in TWIN_WEIGHTS}
    shared = {n: given[n] for n in SHARED_INPUTS}
    per_example = {n: given[n] for n in ['x', 'c']}
    grad_fn = _jax.value_and_grad(_loss, argnums=(0, 1))

    def one_microbatch(ex, loss_target):
        ex = dict(ex)
        diff = ex.pop(TWIN_DIFF_INPUT)
        return grad_fn(weights, diff, {**shared, **ex}, loss_target)

    if N_MICROBATCH == 1:
        loss, (grad_w, grad_x) = one_microbatch(per_example, given["loss_target"])
    else:
        def body(carry, xs):
            loss_sum, grad_sum = carry
            l_k, (gw_k, gx_k) = one_microbatch(xs[0], xs[1])
            with _jax.named_scope("update"):
                return (loss_sum + l_k, _jax.tree.map(_jnp.add, grad_sum, gw_k)), gx_k

        init = (_jnp.zeros((), _jnp.float32), _jax.tree.map(_jnp.zeros_like, weights))
        (loss, grad_w), grad_x = _jax.lax.scan(body, init, (per_example, given["loss_target"]))
    with _jax.named_scope("update"):
        delta_w, new_m, new_v = {}, {}, {}
        for n in TWIN_WEIGHTS:
            delta_w[n], new_m[n], new_v[n] = _adamw(weights[n], grad_w[n], given["m_" + n], given["v_" + n])
    return (loss, grad_x, *[grad_w[n] for n in TWIN_WEIGHTS], *[delta_w[n] for n in TWIN_WEIGHTS],
            *[new_m[n] for n in TWIN_WEIGHTS], *[new_v[n] for n in TWIN_WEIGHTS])
```

```python
import functools

import jax
import jax.numpy as jnp
from jax import lax
from jax.experimental import pallas as pl
from jax.experimental.pallas import tpu as pltpu

F32 = jnp.float32
BF16 = jnp.bfloat16
HI = lax.Precision.HIGHEST
MESH = pl.DeviceIdType.MESH

D_MODEL = 1024
HEADS = 4
HEAD_DIM = 128
GLA_KEY = 64
GATE_RANK = 16
CHUNK = 64
D_FF = 2816
ALPHA = 2.0 ** 0.25
EPS = 1e-6
N_CHIPS = 4
N_DEV = 8

PROJ_W = 3840
OFF_Z, OFF_GQ, OFF_GK, OFF_GV, OFF_GG, OFF_SMALL = 1536, 2048, 2304, 2560, 3072, 3584
W_IN_COLS = 3608

ADAM_LR, ADAM_B1, ADAM_B2, ADAM_EPS, ADAM_WD, ADAM_STEP = 0.001, 0.9, 0.999, 1e-08, 0.01, 10

VMEM_LIMIT = 56 * 1024 * 1024
ROW_TILE = 256


def _cparams(sem):
    return pltpu.CompilerParams(dimension_semantics=sem, vmem_limit_bytes=VMEM_LIMIT)


def _pick(n, prefs):
    for p in prefs:
        if n % p == 0:
            return p
    return n


def _mm(a, b, *, ta=False, tb=False, out_dtype=F32, name):
    if ta:
        k_dim, m_dim = a.shape
    else:
        m_dim, k_dim = a.shape
    n_dim = b.shape[0] if tb else b.shape[1]
    tm = _pick(m_dim, (512, 256, 128))
    tn = _pick(n_dim, (768, 512, 384, 256, 128))
    tk = k_dim if k_dim <= 2048 else _pick(k_dim, (1408, 1280, 1024, 512, 256, 128))
    nk = k_dim // tk
    dims = (((0 if ta else 1,), (1 if tb else 0,)), ((), ()))

    def body(a_ref, b_ref, o_ref, acc_ref):
        k = pl.program_id(2)

        @pl.when(k == 0)
        def _():
            acc_ref[...] = jnp.zeros_like(acc_ref)

        acc_ref[...] += lax.dot_general(a_ref[...].astype(BF16), b_ref[...].astype(BF16), dims,
                                        preferred_element_type=F32)

        @pl.when(k == nk - 1)
        def _():
            o_ref[...] = acc_ref[...].astype(o_ref.dtype)

    a_spec = pl.BlockSpec((tk, tm), lambda i, j, k: (k, i)) if ta else pl.BlockSpec((tm, tk), lambda i, j, k: (i, k))
    b_spec = pl.BlockSpec((tn, tk), lambda i, j, k: (j, k)) if tb else pl.BlockSpec((tk, tn), lambda i, j, k: (k, j))
    return pl.pallas_call(
        body, name=name, grid=(m_dim // tm, n_dim // tn, nk),
        in_specs=[a_spec, b_spec], out_specs=pl.BlockSpec((tm, tn), lambda i, j, k: (i, j)),
        out_shape=jax.ShapeDtypeStruct((m_dim, n_dim), out_dtype),
        scratch_shapes=[pltpu.VMEM((tm, tn), F32)],
        compiler_params=_cparams(("parallel", "parallel", "arbitrary")),
    )(a, b)


def _ln(x, g, b):
    mu = jnp.mean(x, -1, keepdims=True)
    xc = x - mu
    var = jnp.mean(xc * xc, -1, keepdims=True)
    return xc * lax.rsqrt(var + EPS) * g + b


def _softplus(x):
    return jnp.maximum(x, 0.0) + jnp.log(1.0 + jnp.exp(-jnp.abs(x)))


def _silu(x):
    return x * jax.nn.sigmoid(x)


def _dsilu(x):
    s = jax.nn.sigmoid(x)
    return s * (1.0 + x * (1.0 - s))


def _f_ln0(x, g, b, sc, sh):
    x0 = _ln(x, g, b)
    return x0, x0 * (1.0 + sc) + sh


def _f_ln1(x0, y, gt, g, b, sc, sh):
    x1 = _ln(ALPHA * x0 + (1.0 + gt) * y, g, b)
    return x1, x1 * (1.0 + sc) + sh


def _f_ln2_loss(x1, y2, gt, g, b, tgt):
    x2 = _ln(ALPHA * x1 + (1.0 + gt) * y2, g, b)
    err = x2 - tgt
    per_row = jnp.sum(err * err, -1, keepdims=True) * (0.5 / D_MODEL)
    return jnp.sum(per_row, 0, keepdims=True)


def _row_specs(t_len):
    nt = t_len // ROW_TILE
    row = pl.BlockSpec((ROW_TILE, D_MODEL), lambda b, i: (b * nt + i, 0))
    vec = pl.BlockSpec((1, D_MODEL), lambda b, i: (0, 0))
    mod = pl.BlockSpec((None, 6, D_MODEL), lambda b, i: (b, 0, 0))
    return nt, row, vec, mod


def _first_step():
    return jnp.logical_and(pl.program_id(0) == 0, pl.program_id(1) == 0)


def _acc(ref, val, first, at=(Ellipsis,)):
    @pl.when(first)
    def _():
        ref[at] = val

    @pl.when(jnp.logical_not(first))
    def _():
        ref[at] += val


def _acc_rows(ref, rows, first):
    for i, r in enumerate(rows):
        _acc(ref, r, first, at=(slice(i, i + 1), slice(None)))


def _ln0_fwd(x, g, b, mod, n_b, t_len):
    nt, row, vec, mods = _row_specs(t_len)

    def body(x_ref, g_ref, b_ref, mod_ref, x0_ref, h_ref):
        x0, h = _f_ln0(x_ref[...], g_ref[...], b_ref[...], mod_ref[1:2, :], mod_ref[0:1, :])
        x0_ref[...] = x0
        h_ref[...] = h.astype(BF16)

    return pl.pallas_call(
        body, name="ln0_fwd", grid=(n_b, nt), in_specs=[row, vec, vec, mods], out_specs=[row, row],
        out_shape=[jax.ShapeDtypeStruct(x.shape, F32), jax.ShapeDtypeStruct(x.shape, BF16)],
        compiler_params=_cparams(("parallel", "parallel")),
    )(x, g, b, mod)


def _ln0_bwd(x, g, b, mod, dx0, dh, n_b, t_len):
    nt, row, vec, mods = _row_specs(t_len)
    dmod_spec = pl.BlockSpec((None, 2, D_MODEL), lambda bb, i: (bb, 0, 0))

    def body(x_ref, g_ref, b_ref, mod_ref, dx0_ref, dh_ref, dx_ref, dg_ref, db_ref, dmod_ref):
        _, pull = jax.vjp(_f_ln0, x_ref[...], g_ref[...], b_ref[...], mod_ref[1:2, :], mod_ref[0:1, :])
        dx, dg, db, dsc, dsh = pull((dx0_ref[...], dh_ref[...]))
        dx_ref[...] = dx
        _acc(dg_ref, dg, _first_step())
        _acc(db_ref, db, _first_step())
        _acc_rows(dmod_ref, [dsh, dsc], pl.program_id(1) == 0)

    return pl.pallas_call(
        body, name="ln0_bwd", grid=(n_b, nt), in_specs=[row, vec, vec, mods, row, row],
        out_specs=[row, vec, vec, dmod_spec],
        out_shape=[jax.ShapeDtypeStruct(x.shape, F32), jax.ShapeDtypeStruct((1, D_MODEL), F32),
                   jax.ShapeDtypeStruct((1, D_MODEL), F32), jax.ShapeDtypeStruct((n_b, 2, D_MODEL), F32)],
        compiler_params=_cparams(("arbitrary", "arbitrary")),
    )(x, g, b, mod, dx0, dh)


def _ln1_fwd(x0, y, g, b, mod, n_b, t_len):
    nt, row, vec, mods = _row_specs(t_len)

    def body(x0_ref, y_ref, g_ref, b_ref, mod_ref, x1_ref, h_ref):
        x1, h = _f_ln1(x0_ref[...], y_ref[...], mod_ref[2:3, :], g_ref[...], b_ref[...],
                       mod_ref[4:5, :], mod_ref[3:4, :])
        x1_ref[...] = x1
        h_ref[...] = h.astype(BF16)

    return pl.pallas_call(
        body, name="ln1_fwd", grid=(n_b, nt), in_specs=[row, row, vec, vec, mods], out_specs=[row, row],
        out_shape=[jax.ShapeDtypeStruct(x0.shape, F32), jax.ShapeDtypeStruct(x0.shape, BF16)],
        compiler_params=_cparams(("parallel", "parallel")),
    )(x0, y, g, b, mod)


def _ln1_bwd(x0, y, g, b, mod, dx1, dh, n_b, t_len):
    nt, row, vec, mods = _row_specs(t_len)
    dmod_spec = pl.BlockSpec((None, 3, D_MODEL), lambda bb, i: (bb, 0, 0))

    def body(x0_ref, y_ref, g_ref, b_ref, mod_ref, dx1_ref, dh_ref, dx0_ref, dy_ref, dg_ref, db_ref, dmod_ref):
        _, pull = jax.vjp(_f_ln1, x0_ref[...], y_ref[...], mod_ref[2:3, :], g_ref[...], b_ref[...],
                          mod_ref[4:5, :], mod_ref[3:4, :])
        dx0, dy, dgt, dg, db, dsc, dsh = pull((dx1_ref[...], dh_ref[...]))
        dx0_ref[...] = dx0
        dy_ref[...] = dy.astype(BF16)
        _acc(dg_ref, dg, _first_step())
        _acc(db_ref, db, _first_step())
        _acc_rows(dmod_ref, [dgt, dsh, dsc], pl.program_id(1) == 0)

    return pl.pallas_call(
        body, name="ln1_bwd", grid=(n_b, nt), in_specs=[row, row, vec, vec, mods, row, row],
        out_specs=[row, row, vec, vec, dmod_spec],
        out_shape=[jax.ShapeDtypeStruct(x0.shape, F32), jax.ShapeDtypeStruct(x0.shape, BF16),
                   jax.ShapeDtypeStruct((1, D_MODEL), F32), jax.ShapeDtypeStruct((1, D_MODEL), F32),
                   jax.ShapeDtypeStruct((n_b, 3, D_MODEL), F32)],
        compiler_params=_cparams(("arbitrary", "arbitrary")),
    )(x0, y, g, b, mod, dx1, dh)


def _ln2_loss_bwd(x1, y2, g, b, mod, tgt, n_b, t_len):
    nt, row, vec, mods = _row_specs(t_len)
    one = pl.BlockSpec((1, 128), lambda bb, i: (0, 0))
    dmod_spec = pl.BlockSpec((None, 1, D_MODEL), lambda bb, i: (bb, 0, 0))

    def body(x1_ref, y2_ref, g_ref, b_ref, mod_ref, t_ref, loss_ref, dx1_ref, dy2_ref, dg_ref, db_ref, dgt_ref):
        loss, pull = jax.vjp(functools.partial(_f_ln2_loss, tgt=t_ref[...]), x1_ref[...], y2_ref[...],
                             mod_ref[5:6, :], g_ref[...], b_ref[...])
        dx1, dy2, dgt, dg, db = pull(jnp.ones((1, 1), F32))
        dx1_ref[...] = dx1
        dy2_ref[...] = dy2.astype(BF16)
        _acc(loss_ref, jnp.broadcast_to(loss, (1, 128)), _first_step())
        _acc(dg_ref, dg, _first_step())
        _acc(db_ref, db, _first_step())
        _acc(dgt_ref, dgt, pl.program_id(1) == 0)

    return pl.pallas_call(
        body, name="ln2_loss_bwd", grid=(n_b, nt), in_specs=[row, row, vec, vec, mods, row],
        out_specs=[one, row, row, vec, vec, dmod_spec],
        out_shape=[jax.ShapeDtypeStruct((1, 128), F32), jax.ShapeDtypeStruct(x1.shape, F32),
                   jax.ShapeDtypeStruct(x1.shape, BF16), jax.ShapeDtypeStruct((1, D_MODEL), F32),
                   jax.ShapeDtypeStruct((1, D_MODEL), F32), jax.ShapeDtypeStruct((n_b, 1, D_MODEL), F32)],
        compiler_params=_cparams(("arbitrary", "arbitrary")),
    )(x1, y2, g, b, mod, tgt)


def _shift_down(x, s):
    if s == 0:
        return x
    rows = lax.broadcasted_iota(jnp.int32, x.shape, 0)
    return jnp.where(rows >= s, pltpu.roll(x, s, 0), 0.0)


def _shift_up(x, s):
    if s == 0:
        return x
    t_len = x.shape[0]
    rows = lax.broadcasted_iota(jnp.int32, x.shape, 0)
    return jnp.where(rows < t_len - s, pltpu.roll(x, t_len - s, 0), 0.0)


def _conv(x, w):
    k_w = w.shape[0]
    out = w[k_w - 1:k_w, :] * x
    for k in range(k_w - 1):
        out = out + w[k:k + 1, :] * _shift_down(x, k_w - 1 - k)
    return out


def _conv_bwd(x, w, du):
    k_w = w.shape[0]
    dx = w[k_w - 1:k_w, :] * du
    dws = []
    for k in range(k_w):
        s = k_w - 1 - k
        if s:
            dx = dx + w[k:k + 1, :] * _shift_up(du, s)
        dws.append(jnp.sum(du * _shift_down(x, s), 0, keepdims=True))
    return dx, dws


def _dn_pre_fwd(proj, conv_w, n_b, t_len):
    n_ct = 3 * HEADS
    k_w = conv_w.shape[0]

    def body(x_ref, w_ref, o_ref):
        o_ref[...] = _silu(_conv(x_ref[...], w_ref[...]))

    return pl.pallas_call(
        body, name="dn_pre_fwd", grid=(n_ct, n_b),
        in_specs=[pl.BlockSpec((t_len, 128), lambda j, b: (b, j)), pl.BlockSpec((k_w, 128), lambda j, b: (0, j))],
        out_specs=pl.BlockSpec((t_len, 128), lambda j, b: (b, j)),
        out_shape=jax.ShapeDtypeStruct((n_b * t_len, n_ct * 128), F32),
        compiler_params=_cparams(("parallel", "parallel")),
    )(proj, conv_w)


def _dn_pre_bwd(proj, conv_w, dqkv, n_b, t_len):
    n_ct = 3 * HEADS
    k_w = conv_w.shape[0]

    def body(x_ref, w_ref, d_ref, dx_ref, dw_ref):
        x, w = x_ref[...], w_ref[...]
        du = d_ref[...] * _dsilu(_conv(x, w))
        dx, dw = _conv_bwd(x, w, du)
        dx_ref[...] = dx.astype(BF16)
        _acc_rows(dw_ref, dw, pl.program_id(1) == 0)

    return pl.pallas_call(
        body, name="dn_pre_bwd", grid=(n_ct, n_b),
        in_specs=[pl.BlockSpec((t_len, 128), lambda j, b: (b, j)), pl.BlockSpec((k_w, 128), lambda j, b: (0, j)),
                  pl.BlockSpec((t_len, 128), lambda j, b: (b, j))],
        out_specs=[pl.BlockSpec((t_len, 128), lambda j, b: (b, j)), pl.BlockSpec((k_w, 128), lambda j, b: (0, j))],
        out_shape=[jax.ShapeDtypeStruct((n_b * t_len, n_ct * 128), BF16),
                   jax.ShapeDtypeStruct((k_w, n_ct * 128), F32)],
        compiler_params=_cparams(("parallel", "arbitrary")),
    )(proj, conv_w, dqkv)


FFN_TC = 256
FFN_NT = D_FF // FFN_TC


def _ffn_act_fwd(up, conv_w, conv_b, n_b, t_len):
    def body(g_ref, v_ref, wg_ref, wv_ref, bg_ref, bv_ref, o_ref):
        ug = _conv(g_ref[...], wg_ref[...]) + bg_ref[...]
        uv = _conv(v_ref[...], wv_ref[...]) + bv_ref[...]
        o_ref[...] = (_silu(ug) * uv).astype(BF16)

    blk = lambda off: pl.BlockSpec((t_len, FFN_TC), lambda j, b: (b, j + off))
    wblk = lambda off: pl.BlockSpec((3, FFN_TC), lambda j, b: (0, j + off))
    bblk = lambda off: pl.BlockSpec((1, FFN_TC), lambda j, b: (0, j + off))
    return pl.pallas_call(
        body, name="ffn_act_fwd", grid=(FFN_NT, n_b),
        in_specs=[blk(0), blk(FFN_NT), wblk(0), wblk(FFN_NT), bblk(0), bblk(FFN_NT)],
        out_specs=pl.BlockSpec((t_len, FFN_TC), lambda j, b: (b, j)),
        out_shape=jax.ShapeDtypeStruct((n_b * t_len, D_FF), BF16),
        compiler_params=_cparams(("parallel", "parallel")),
    )(up, up, conv_w, conv_w, conv_b, conv_b)


def _ffn_act_bwd(up, conv_w, conv_b, da, n_b, t_len):
    n_all = 2 * FFN_NT
    partner = lambda j: (j + FFN_NT) % n_all

    def body(xo_ref, xp_ref, wo_ref, wp_ref, bo_ref, bp_ref, da_ref, dup_ref, dw_ref, db_ref):
        is_gate = pl.program_id(0) < FFN_NT
        xo, wo = xo_ref[...], wo_ref[...]
        uo = _conv(xo, wo) + bo_ref[...]
        upar = _conv(xp_ref[...], wp_ref[...]) + bp_ref[...]
        du = da_ref[...] * jnp.where(is_gate, upar * _dsilu(uo), _silu(upar))
        dx, dw = _conv_bwd(xo, wo, du)
        dup_ref[...] = dx.astype(BF16)
        _acc_rows(dw_ref, dw, pl.program_id(1) == 0)
        _acc(db_ref, jnp.sum(du, 0, keepdims=True), pl.program_id(1) == 0)

    return pl.pallas_call(
        body, name="ffn_act_bwd", grid=(n_all, n_b),
        in_specs=[pl.BlockSpec((t_len, FFN_TC), lambda j, b: (b, j)),
                  pl.BlockSpec((t_len, FFN_TC), lambda j, b: (b, partner(j))),
                  pl.BlockSpec((3, FFN_TC), lambda j, b: (0, j)),
                  pl.BlockSpec((3, FFN_TC), lambda j, b: (0, partner(j))),
                  pl.BlockSpec((1, FFN_TC), lambda j, b: (0, j)),
                  pl.BlockSpec((1, FFN_TC), lambda j, b: (0, partner(j))),
                  pl.BlockSpec((t_len, FFN_TC), lambda j, b: (b, j % FFN_NT))],
        out_specs=[pl.BlockSpec((t_len, FFN_TC), lambda j, b: (b, j)),
                   pl.BlockSpec((3, FFN_TC), lambda j, b: (0, j)),
                   pl.BlockSpec((1, FFN_TC), lambda j, b: (0, j))],
        out_shape=[jax.ShapeDtypeStruct((n_b * t_len, 2 * D_FF), BF16),
                   jax.ShapeDtypeStruct((3, 2 * D_FF), F32), jax.ShapeDtypeStruct((1, 2 * D_FF), F32)],
        compiler_params=_cparams(("parallel", "arbitrary")),
    )(up, up, conv_w, conv_w, conv_b, conv_b, da)


def _iota2(n, m, axis):
    return lax.broadcasted_iota(jnp.int32, (n, m), axis)


def _dot(a, b, precision=None):
    return lax.dot_general(a, b, (((1,), (0,)), ((), ())), precision=precision, preferred_element_type=F32)


def _dot_nt(a, b, precision=None):
    return lax.dot_general(a, b, (((1,), (1,)), ((), ())), precision=precision, preferred_element_type=F32)


def _dot_tn(a, b, precision=None):
    return lax.dot_general(a, b, (((0,), (0,)), ((), ())), precision=precision, preferred_element_type=F32)


@jax.custom_vjp
def _unit_lower_inv(m):
    n = m.shape[0]
    p = -m
    a = jnp.where(_iota2(n, n, 0) == _iota2(n, n, 1), 1.0, 0.0) + p
    span = 2
    while span < n:
        p = _dot(p, p, HI)
        a = a + _dot(a, p, HI)
        span *= 2
    return a


def _unit_lower_inv_fwd(m):
    a = _unit_lower_inv(m)
    return a, a


def _unit_lower_inv_bwd(a, da):
    return (-_dot_tn(a, _dot_nt(da, a, HI), HI),)


_unit_lower_inv.defvjp(_unit_lower_inv_fwd, _unit_lower_inv_bwd)


def _rms_gate(o, gn, gate):
    return o * lax.rsqrt(jnp.mean(o * o, -1, keepdims=True) + EPS) * gn * _silu(gate)


def _dn_head(h, q, k, v, z, small, s_in, a_log, dt_bias, gn):
    c_len = q.shape[0]
    row, col = _iota2(c_len, c_len, 0), _iota2(c_len, c_len, 1)
    causal, strict, eye = row >= col, row > col, row == col
    qn = q * lax.rsqrt(jnp.sum(q * q, -1, keepdims=True) + EPS) * (HEAD_DIM ** -0.5)
    kn = k * lax.rsqrt(jnp.sum(k * k, -1, keepdims=True) + EPS)
    lane = _iota2(c_len, 128, 1)
    la_all = -jnp.exp(a_log) * _softplus(small + dt_bias)
    la_c = jnp.sum(jnp.where(lane == h, la_all, 0.0), 1, keepdims=True)
    beta = jnp.sum(jnp.where(lane == HEADS + h, jax.nn.sigmoid(small), 0.0), 1, keepdims=True)
    la_b = jnp.broadcast_to(la_c, (c_len, c_len))
    la_r = jnp.sum(jnp.where(eye, la_b, 0.0), 0, keepdims=True)
    g_c = jnp.sum(jnp.where(causal, jnp.broadcast_to(la_r, (c_len, c_len)), 0.0), 1, keepdims=True)
    g_r = jnp.sum(jnp.where(row <= col, la_b, 0.0), 0, keepdims=True)
    g_last = jnp.sum(la_c, 0, keepdims=True)
    decay = jnp.exp(jnp.where(causal, g_c - g_r, -1e30))
    e_g = jnp.exp(g_c)
    kb = kn * beta
    m_low = jnp.where(strict, _dot_nt(kb, kn) * decay, 0.0)
    a_inv = _unit_lower_inv(m_low)
    u = _dot(a_inv, v * beta, HI)
    w = _dot(a_inv, kb * e_g, HI)
    attn = _dot_nt(qn, kn) * decay
    v_new = u - _dot(w, s_in)
    o = _dot(qn * e_g, s_in) + _dot(attn, v_new)
    s_out = s_in * jnp.exp(g_last) + _dot_tn(kn * jnp.exp(g_last - g_c), v_new)
    return _rms_gate(o, gn, z), s_out


def _gla_head(q, k, v, gate, small, s_in, w2, b2, gn):
    c_len = q.shape[0]
    row, col = _iota2(c_len, c_len, 0), _iota2(c_len, c_len, 1)
    causal = row >= col
    la = -_softplus(-(_dot(small, w2) + b2)) * (1.0 / 16.0)
    b = _dot(jnp.where(causal, 1.0, 0.0), la, HI)
    b_last = jnp.sum(jnp.where(_iota2(c_len, GLA_KEY, 0) == c_len - 1, b, 0.0), 0, keepdims=True)
    q_dec = q * (GLA_KEY ** -0.5) * jnp.exp(b)
    attn = jnp.where(causal, _dot_nt(q_dec, k * jnp.exp(-b)), 0.0)
    o = _dot(q_dec, s_in) + _dot(attn, v)
    g_row = jnp.exp(b_last)
    g_col = jnp.sum(jnp.where(_iota2(GLA_KEY, GLA_KEY, 0) == _iota2(GLA_KEY, GLA_KEY, 1),
                              jnp.broadcast_to(g_row, (GLA_KEY, GLA_KEY)), 0.0), 1, keepdims=True)
    s_out = s_in * g_col + _dot_tn(k * jnp.exp(b_last - b), v)
    return _rms_gate(o, gn, gate), s_out


def _chunk_spec(width, col_block, n_c, reverse=False):
    if reverse:
        return pl.BlockSpec((CHUNK, width), lambda b, n: (b * n_c + n_c - 1 - n, col_block))
    return pl.BlockSpec((CHUNK, width), lambda b, n: (b * n_c + n, col_block))


def _const_spec(shape):
    return pl.BlockSpec(shape, lambda b, n: (0,) * len(shape))


def _dn_scan_fwd(qkv, proj, a_log, dt_bias, gn, n_b, t_len):
    n_c = t_len // CHUNK
    hs = [slice(h * HEAD_DIM, (h + 1) * HEAD_DIM) for h in range(HEADS)]

    def body(q_ref, k_ref, v_ref, z_ref, sm_ref, al_ref, dt_ref, gn_ref, o_ref, hist_ref, s_ref):
        @pl.when(pl.program_id(1) == 0)
        def _():
            s_ref[...] = jnp.zeros_like(s_ref)

        small = sm_ref[...]
        for h in range(HEADS):
            s_in = s_ref[h]
            hist_ref[h] = s_in
            og, s_out = _dn_head(h, q_ref[:, hs[h]], k_ref[:, hs[h]], v_ref[:, hs[h]], z_ref[:, hs[h]], small,
                                 s_in, al_ref[...], dt_ref[...], gn_ref[...])
            o_ref[:, hs[h]] = og.astype(BF16)
            s_ref[h] = s_out

    return pl.pallas_call(
        body, name="dn_scan_fwd", grid=(n_b, n_c),
        in_specs=[_chunk_spec(512, 0, n_c), _chunk_spec(512, 1, n_c), _chunk_spec(512, 2, n_c),
                  _chunk_spec(512, OFF_Z // 512, n_c), _chunk_spec(128, OFF_SMALL // 128, n_c),
                  _const_spec((1, 128)), _const_spec((1, 128)), _const_spec((1, 128))],
        out_specs=[_chunk_spec(512, 0, n_c),
                   pl.BlockSpec((None, HEADS, HEAD_DIM, HEAD_DIM), lambda b, n: (b * n_c + n, 0, 0, 0))],
        out_shape=[jax.ShapeDtypeStruct((n_b * t_len, 512), BF16),
                   jax.ShapeDtypeStruct((n_b * n_c, HEADS, HEAD_DIM, HEAD_DIM), F32)],
        scratch_shapes=[pltpu.VMEM((HEADS, HEAD_DIM, HEAD_DIM), F32)],
        compiler_params=_cparams(("parallel", "arbitrary")),
    )(qkv, qkv, qkv, proj, proj, a_log, dt_bias, gn)


def _dn_scan_bwd(qkv, proj, a_log, dt_bias, gn, hist, d_o, n_b, t_len):
    n_c = t_len // CHUNK
    hs = [slice(h * HEAD_DIM, (h + 1) * HEAD_DIM) for h in range(HEADS)]

    def body(q_ref, k_ref, v_ref, z_ref, sm_ref, al_ref, dt_ref, gn_ref, hist_ref, do_ref,
             dqkv_ref, dz_ref, dsm_ref, dal_ref, ddt_ref, dgn_ref, ds_ref):
        @pl.when(pl.program_id(1) == 0)
        def _():
            ds_ref[...] = jnp.zeros_like(ds_ref)

        small = sm_ref[...]
        dsm = jnp.zeros_like(small)
        dal = jnp.zeros((1, 128), F32)
        ddt = jnp.zeros((1, 128), F32)
        dgn = jnp.zeros((1, 128), F32)
        for h in range(HEADS):
            _, pull = jax.vjp(functools.partial(_dn_head, h), q_ref[:, hs[h]], k_ref[:, hs[h]], v_ref[:, hs[h]],
                              z_ref[:, hs[h]], small, hist_ref[h], al_ref[...], dt_ref[...], gn_ref[...])
            dq, dk, dv, dz, dsm_h, ds_in, dal_h, ddt_h, dgn_h = pull((do_ref[:, hs[h]], ds_ref[h]))
            dqkv_ref[:, hs[h]] = dq
            dqkv_ref[:, slice(512 + h * HEAD_DIM, 512 + (h + 1) * HEAD_DIM)] = dk
            dqkv_ref[:, slice(1024 + h * HEAD_DIM, 1024 + (h + 1) * HEAD_DIM)] = dv
            dz_ref[:, hs[h]] = dz.astype(BF16)
            ds_ref[h] = ds_in
            dsm, dal, ddt, dgn = dsm + dsm_h, dal + dal_h, ddt + ddt_h, dgn + dgn_h
        dsm_ref[...] = dsm
        _acc(dal_ref, dal, _first_step())
        _acc(ddt_ref, ddt, _first_step())
        _acc(dgn_ref, dgn, _first_step())

    rev = functools.partial(_chunk_spec, n_c=n_c, reverse=True)
    return pl.pallas_call(
        body, name="dn_scan_bwd", grid=(n_b, n_c),
        in_specs=[rev(512, 0), rev(512, 1), rev(512, 2), rev(512, OFF_Z // 512), rev(128, OFF_SMALL // 128),
                  _const_spec((1, 128)), _const_spec((1, 128)), _const_spec((1, 128)),
                  pl.BlockSpec((None, HEADS, HEAD_DIM, HEAD_DIM), lambda b, n: (b * n_c + n_c - 1 - n, 0, 0, 0)),
                  rev(512, 0)],
        out_specs=[rev(1536, 0), rev(512, 0), rev(128, 0),
                   _const_spec((1, 128)), _const_spec((1, 128)), _const_spec((1, 128))],
        out_shape=[jax.ShapeDtypeStruct((n_b * t_len, 1536), F32), jax.ShapeDtypeStruct((n_b * t_len, 512), BF16),
                   jax.ShapeDtypeStruct((n_b * t_len, 128), F32), jax.ShapeDtypeStruct((1, 128), F32),
                   jax.ShapeDtypeStruct((1, 128), F32), jax.ShapeDtypeStruct((1, 128), F32)],
        scratch_shapes=[pltpu.VMEM((HEADS, HEAD_DIM, HEAD_DIM), F32)],
        compiler_params=_cparams(("arbitrary", "arbitrary")),
    )(qkv, qkv, qkv, proj, proj, a_log, dt_bias, gn, hist, d_o)


def _gla_scan_fwd(proj, w2, b2, gn, n_b, t_len):
    n_c = t_len // CHUNK
    ks = [slice(h * GLA_KEY, (h + 1) * GLA_KEY) for h in range(HEADS)]
    vs = [slice(h * HEAD_DIM, (h + 1) * HEAD_DIM) for h in range(HEADS)]

    def body(q_ref, k_ref, v_ref, g_ref, sm_ref, w2_ref, b2_ref, gn_ref, o_ref, hist_ref, s_ref):
        @pl.when(pl.program_id(1) == 0)
        def _():
            s_ref[...] = jnp.zeros_like(s_ref)

        small = sm_ref[...]
        for h in range(HEADS):
            s_in = s_ref[h]
            hist_ref[h] = s_in
            og, s_out = _gla_head(q_ref[:, ks[h]], k_ref[:, ks[h]], v_ref[:, vs[h]], g_ref[:, vs[h]], small,
                                  s_in, w2_ref[:, ks[h]], b2_ref[:, ks[h]], gn_ref[...])
            o_ref[:, vs[h]] = og.astype(BF16)
            s_ref[h] = s_out

    return pl.pallas_call(
        body, name="gla_scan_fwd", grid=(n_b, n_c),
        in_specs=[_chunk_spec(256, OFF_GQ // 256, n_c), _chunk_spec(256, OFF_GK // 256, n_c),
                  _chunk_spec(512, OFF_GV // 512, n_c), _chunk_spec(512, OFF_GG // 512, n_c),
                  _chunk_spec(128, OFF_SMALL // 128, n_c),
                  _const_spec((128, 256)), _const_spec((1, 256)), _const_spec((1, 128))],
        out_specs=[_chunk_spec(512, 0, n_c),
                   pl.BlockSpec((None, HEADS, GLA_KEY, HEAD_DIM), lambda b, n: (b * n_c + n, 0, 0, 0))],
        out_shape=[jax.ShapeDtypeStruct((n_b * t_len, 512), BF16),
                   jax.ShapeDtypeStruct((n_b * n_c, HEADS, GLA_KEY, HEAD_DIM), F32)],
        scratch_shapes=[pltpu.VMEM((HEADS, GLA_KEY, HEAD_DIM), F32)],
        compiler_params=_cparams(("parallel", "arbitrary")),
    )(proj, proj, proj, proj, proj, w2, b2, gn)


def _gla_scan_bwd(proj, w2, b2, gn, hist, d_o, n_b, t_len):
    n_c = t_len // CHUNK
    ks = [slice(h * GLA_KEY, (h + 1) * GLA_KEY) for h in range(HEADS)]
    vs = [slice(h * HEAD_DIM, (h + 1) * HEAD_DIM) for h in range(HEADS)]

    def body(q_ref, k_ref, v_ref, g_ref, sm_ref, w2_ref, b2_ref, gn_ref, hist_ref, do_ref,
             dq_ref, dk_ref, dv_ref, dg_ref, dsm_ref, dw2_ref, db2_ref, dgn_ref, ds_ref):
        @pl.when(pl.program_id(1) == 0)
        def _():
            ds_ref[...] = jnp.zeros_like(ds_ref)

        small = sm_ref[...]
        dsm = jnp.zeros_like(small)
        dgn = jnp.zeros((1, 128), F32)
        first = _first_step()
        for h in range(HEADS):
            _, pull = jax.vjp(_gla_head, q_ref[:, ks[h]], k_ref[:, ks[h]], v_ref[:, vs[h]], g_ref[:, vs[h]], small,
                              hist_ref[h], w2_ref[:, ks[h]], b2_ref[:, ks[h]], gn_ref[...])
            dq, dk, dv, dg, dsm_h, ds_in, dw2_h, db2_h, dgn_h = pull((do_ref[:, vs[h]], ds_ref[h]))
            dq_ref[:, ks[h]] = dq.astype(BF16)
            dk_ref[:, ks[h]] = dk.astype(BF16)
            dv_ref[:, vs[h]] = dv.astype(BF16)
            dg_ref[:, vs[h]] = dg.astype(BF16)
            ds_ref[h] = ds_in
            dsm, dgn = dsm + dsm_h, dgn + dgn_h

            @pl.when(first)
            def _():
                dw2_ref[:, ks[h]] = dw2_h
                db2_ref[:, ks[h]] = db2_h

            @pl.when(jnp.logical_not(first))
            def _():
                dw2_ref[:, ks[h]] += dw2_h
                db2_ref[:, ks[h]] += db2_h

        dsm_ref[...] = dsm
        _acc(dgn_ref, dgn, first)

    rev = functools.partial(_chunk_spec, n_c=n_c, reverse=True)
    n_rows = n_b * t_len
    return pl.pallas_call(
        body, name="gla_scan_bwd", grid=(n_b, n_c),
        in_specs=[rev(256, OFF_GQ // 256), rev(256, OFF_GK // 256), rev(512, OFF_GV // 512), rev(512, OFF_GG // 512),
                  rev(128, OFF_SMALL // 128),
                  _const_spec((128, 256)), _const_spec((1, 256)), _const_spec((1, 128)),
                  pl.BlockSpec((None, HEADS, GLA_KEY, HEAD_DIM), lambda b, n: (b * n_c + n_c - 1 - n, 0, 0, 0)),
                  rev(512, 1)],
        out_specs=[rev(256, 0), rev(256, 0), rev(512, 0), rev(512, 0), rev(128, 0),
                   _const_spec((128, 256)), _const_spec((1, 256)), _const_spec((1, 128))],
        out_shape=[jax.ShapeDtypeStruct((n_rows, 256), BF16), jax.ShapeDtypeStruct((n_rows, 256), BF16),
                   jax.ShapeDtypeStruct((n_rows, 512), BF16), jax.ShapeDtypeStruct((n_rows, 512), BF16),
                   jax.ShapeDtypeStruct((n_rows, 128), F32), jax.ShapeDtypeStruct((128, 256), F32),
                   jax.ShapeDtypeStruct((1, 256), F32), jax.ShapeDtypeStruct((1, 128), F32)],
        scratch_shapes=[pltpu.VMEM((HEADS, GLA_KEY, HEAD_DIM), F32)],
        compiler_params=_cparams(("arbitrary", "arbitrary")),
    )(proj, proj, proj, proj, proj, w2, b2, gn, hist, d_o)


def _pad_w_in(w_in):
    return jnp.concatenate([w_in[:, 0:2048], w_in[:, 2056:3592], w_in[:, 2048:2056], w_in[:, 3592:3608],
                            jnp.zeros((w_in.shape[0], PROJ_W - W_IN_COLS), w_in.dtype)], axis=1)


def _unpad_w_in(g):
    return jnp.concatenate([g[:, 0:2048], g[:, 3584:3592], g[:, 2048:3584], g[:, 3592:3608]], axis=1)


def _lane_vec(v, offset=0):
    return jnp.zeros((1, 128), F32).at[0, offset:offset + v.shape[0]].set(v)


def _local_step(x, tgt, mod, p, n_b, t_len):
    row1 = lambda v: v.reshape(1, -1)
    a_log, dt_bias = _lane_vec(p["dn_a_log"]), _lane_vec(p["dn_dt_bias"])
    dn_gn, gla_gn = row1(p["dn_norm_g"]), row1(p["gla_norm_g"])
    w2 = jnp.zeros((128, 256), F32).at[8:8 + GATE_RANK].set(p["gla_w_gate2"])
    b2 = row1(p["gla_b_gate"])
    ln0_g, ln0_b, ln1_g, ln1_b, ln2_g, ln2_b = (row1(p[k]) for k in ("ln0_g", "ln0_b", "ln1_g", "ln1_b", "ln2_g", "ln2_b"))
    conv_b = row1(p["ffn_conv_b"])

    x0, h1 = _ln0_fwd(x, ln0_g, ln0_b, mod, n_b, t_len)
    proj = _mm(h1, p["w_in_p"], name="mm_proj")
    qkv = _dn_pre_fwd(proj, p["dn_conv"], n_b, t_len)
    o_dn, hist_dn = _dn_scan_fwd(qkv, proj, a_log, dt_bias, dn_gn, n_b, t_len)
    o_gla, hist_gla = _gla_scan_fwd(proj, w2, b2, gla_gn, n_b, t_len)
    o_mix = jnp.concatenate([o_dn, o_gla], axis=1)
    y = _mm(o_mix, p["w_o"], name="mm_wo")
    x1, h2 = _ln1_fwd(x0, y, ln1_g, ln1_b, mod, n_b, t_len)
    up = _mm(h2, p["w_up"], name="mm_up")
    act = _ffn_act_fwd(up, p["ffn_conv"], conv_b, n_b, t_len)
    y2 = _mm(act, p["w_down"], name="mm_down")

    loss, dx1, dy2, g_ln2_g, g_ln2_b, dgt_f = _ln2_loss_bwd(x1, y2, ln2_g, ln2_b, mod, tgt, n_b, t_len)
    g_w_down = _mm(act, dy2, ta=True, name="mm_g_down")
    d_act = _mm(dy2, p["w_down"], tb=True, name="mm_d_act")
    d_up, g_ffn_conv, g_conv_b = _ffn_act_bwd(up, p["ffn_conv"], conv_b, d_act, n_b, t_len)
    g_w_up = _mm(h2, d_up, ta=True, name="mm_g_up")
    dh2 = _mm(d_up, p["w_up"], tb=True, name="mm_d_h2")
    dx0, dy, g_ln1_g, g_ln1_b, dmod_1 = _ln1_bwd(x0, y, ln1_g, ln1_b, mod, dx1, dh2, n_b, t_len)
    g_w_o = _mm(o_mix, dy, ta=True, name="mm_g_wo")
    d_o = _mm(dy, p["w_o"], tb=True, name="mm_d_o")
    dqkv, dz, dsm_dn, g_a_log, g_dt_bias, g_dn_gn = _dn_scan_bwd(qkv, proj, a_log, dt_bias, dn_gn, hist_dn, d_o,
                                                                n_b, t_len)
    dgq, dgk, dgv, dgg, dsm_gla, g_w2, g_b2, g_gla_gn = _gla_scan_bwd(proj, w2, b2, gla_gn, hist_gla, d_o, n_b, t_len)
    d_pre, g_dn_conv = _dn_pre_bwd(proj, p["dn_conv"], dqkv, n_b, t_len)
    d_small = (dsm_dn + dsm_gla).astype(BF16)
    d_proj = jnp.concatenate([d_pre, dz, dgq, dgk, dgv, dgg, d_small, jnp.zeros_like(d_small)], axis=1)
    g_w_in_p = _mm(h1, d_proj, ta=True, name="mm_g_win")
    dh1 = _mm(d_proj, p["w_in_p"], tb=True, name="mm_d_h1")
    grad_x, g_ln0_g, g_ln0_b, dmod_0 = _ln0_bwd(x, ln0_g, ln0_b, mod, dx0, dh1, n_b, t_len)

    dmod = jnp.concatenate([dmod_0, dmod_1[:, 0:1], dmod_1[:, 1:3], dgt_f], axis=1)
    grads = {
        "ln0_g": g_ln0_g[0], "ln0_b": g_ln0_b[0], "w_in_p": g_w_in_p, "dn_conv": g_dn_conv,
        "dn_a_log": g_a_log[0, 0:HEADS], "dn_dt_bias": g_dt_bias[0, 0:HEADS], "dn_norm_g": g_dn_gn[0],
        "gla_w_gate2": g_w2[8:8 + GATE_RANK], "gla_b_gate": g_b2[0], "gla_norm_g": g_gla_gn[0],
        "w_o": g_w_o, "ln1_g": g_ln1_g[0], "ln1_b": g_ln1_b[0], "w_up": g_w_up, "ffn_conv": g_ffn_conv,
        "ffn_conv_b": g_conv_b[0], "w_down": g_w_down, "ln2_g": g_ln2_g[0], "ln2_b": g_ln2_b[0],
    }
    return loss, grad_x, grads, dmod


def _ada_fwd(c_all, w_shard, b_shard):
    n_all, n_col = c_all.shape[0], w_shard.shape[1]
    tn = 512

    def body(c_ref, w_ref, b_ref, cond_ref, mod_ref):
        cond = _silu(c_ref[...])
        cond_ref[...] = cond
        mod_ref[...] = _dot(cond.astype(BF16), w_ref[...].astype(BF16)) + b_ref[...]

    return pl.pallas_call(
        body, name="ada_fwd", grid=(n_col // tn,),
        in_specs=[pl.BlockSpec((n_all, D_MODEL), lambda j: (0, 0)), pl.BlockSpec((D_MODEL, tn), lambda j: (0, j)),
                  pl.BlockSpec((1, tn), lambda j: (0, j))],
        out_specs=[pl.BlockSpec((n_all, D_MODEL), lambda j: (0, 0)), pl.BlockSpec((n_all, tn), lambda j: (0, j))],
        out_shape=[jax.ShapeDtypeStruct((n_all, D_MODEL), F32), jax.ShapeDtypeStruct((n_all, n_col), F32)],
        compiler_params=_cparams(("arbitrary",)),
    )(c_all, w_shard, b_shard)


def _col_sum(a):
    def body(a_ref, o_ref):
        o_ref[...] = jnp.sum(a_ref[...], 0, keepdims=True)

    return pl.pallas_call(body, name="col_sum", out_shape=jax.ShapeDtypeStruct((1, a.shape[1]), F32))(a)


def _adamw(w, g, m, v, name):
    n_r, n_c = w.shape
    tr = _pick(n_r, (256, 64, 32, 16, 8))

    def body(w_ref, g_ref, m_ref, v_ref, d_ref, nm_ref, nv_ref):
        grad = g_ref[...]
        new_m = ADAM_B1 * m_ref[...] + (1.0 - ADAM_B1) * grad
        new_v = ADAM_B2 * v_ref[...] + (1.0 - ADAM_B2) * (grad * grad)
        m_hat = new_m / (1.0 - ADAM_B1 ** ADAM_STEP)
        v_hat = new_v / (1.0 - ADAM_B2 ** ADAM_STEP)
        d_ref[...] = -ADAM_LR * (m_hat / (jnp.sqrt(v_hat) + ADAM_EPS) + ADAM_WD * w_ref[...])
        nm_ref[...] = new_m
        nv_ref[...] = new_v

    blk = pl.BlockSpec((tr, n_c), lambda i: (i, 0))
    out = jax.ShapeDtypeStruct(w.shape, F32)
    return pl.pallas_call(
        body, name=name, grid=(n_r // tr,), in_specs=[blk] * 4, out_specs=[blk] * 3, out_shape=[out] * 3,
        compiler_params=_cparams(("parallel",)),
    )(w, g, m, v)


HBM_SPEC = pl.BlockSpec(memory_space=pltpu.HBM)
VMEM_SPEC = pl.BlockSpec(memory_space=pltpu.VMEM)
CHIP_FLIPS = ((1, 0), (0, 1), (1, 1))


def _place():
    return lax.axis_index("x"), lax.axis_index("y"), lax.axis_index("c")


def _flip(v, f):
    return 1 - v if f else v


def _all_gather8(slab, name):
    n_r, n_w = slab.shape

    def body(x_ref, o_ref, s_ref, send_sems, recv_sems, local_sem):
        x, y, c = _place()
        me = 4 * x + 2 * y + c
        mine = pltpu.make_async_copy(x_ref, o_ref.at[me], local_sem)
        mine.start()
        peers = [(_flip(x, k & 4), _flip(y, k & 2), _flip(c, k & 1)) for k in range(1, N_DEV)]
        sends = []
        for k, peer in enumerate(peers):
            cp = pltpu.make_async_remote_copy(src_ref=x_ref, dst_ref=o_ref.at[me], send_sem=send_sems.at[k],
                                              recv_sem=recv_sems.at[k], device_id=peer, device_id_type=MESH)
            cp.start()
            sends.append(cp)
        for k, (px, py, pc) in enumerate(peers):
            pltpu.make_async_remote_copy(src_ref=x_ref, dst_ref=o_ref.at[4 * px + 2 * py + pc],
                                         send_sem=send_sems.at[k], recv_sem=recv_sems.at[k],
                                         device_id=(px, py, pc), device_id_type=MESH).wait_recv()
        for cp in sends:
            cp.wait_send()
        mine.wait()
        total = o_ref[0]
        for d in range(1, N_DEV):
            total = total + o_ref[d]
        s_ref[...] = total

    return pl.pallas_call(
        body, name=name, in_specs=[VMEM_SPEC], out_specs=[VMEM_SPEC, VMEM_SPEC],
        out_shape=[jax.ShapeDtypeStruct((N_DEV, n_r, n_w), F32), jax.ShapeDtypeStruct((n_r, n_w), F32)],
        scratch_shapes=[pltpu.SemaphoreType.DMA((N_DEV - 1,)), pltpu.SemaphoreType.DMA((N_DEV - 1,)),
                        pltpu.SemaphoreType.DMA],
    )(slab)


def _gather_weights(shards):
    n_a = len(shards)

    def body(*refs):
        ins, outs = refs[:n_a], refs[n_a:2 * n_a]
        send_sems, recv_sems, local_sems = refs[2 * n_a:]
        x, y, c = _place()
        me_chip = 2 * x + y
        sibling = (x, y, 1 - c)
        chips = [(_flip(x, fx), _flip(y, fy)) for fx, fy in CHIP_FLIPS]
        local = [pltpu.make_async_copy(ins[k], outs[k].at[me_chip], local_sems.at[k]) for k in range(n_a)]
        for cp in local:
            cp.start()

        def copy(k, slot, chip_of_block, half, to, src=None):
            dst = outs[k].at[chip_of_block, half]
            return pltpu.make_async_remote_copy(src_ref=dst if src is None else src, dst_ref=dst,
                                                send_sem=send_sems.at[k * 6 + slot], recv_sem=recv_sems.at[k * 6 + slot],
                                                device_id=to, device_id_type=MESH)

        first = [copy(k, r, me_chip, c, (*chips[r], c), src=ins[k].at[c]) for k in range(n_a) for r in range(3)]
        for cp in first:
            cp.start()
        passed = []
        for k in range(n_a):
            for r, (px, py) in enumerate(chips):
                copy(k, r, 2 * px + py, c, (x, y, c)).wait_recv()
                fwd = copy(k, 3 + r, 2 * px + py, c, sibling)
                fwd.start()
                passed.append(fwd)
        for k in range(n_a):
            for r, (px, py) in enumerate(chips):
                copy(k, 3 + r, 2 * px + py, 1 - c, (x, y, c)).wait_recv()
        for cp in first + passed:
            cp.wait_send()
        for cp in local:
            cp.wait()

    return pl.pallas_call(
        body, name="gather_weights", in_specs=[HBM_SPEC] * n_a, out_specs=[HBM_SPEC] * n_a,
        out_shape=[jax.ShapeDtypeStruct((N_CHIPS,) + s.shape, s.dtype) for s in shards],
        scratch_shapes=[pltpu.SemaphoreType.DMA((6 * n_a,)), pltpu.SemaphoreType.DMA((6 * n_a,)),
                        pltpu.SemaphoreType.DMA((n_a,))],
    )(*shards)


def _rs_pair(parts):
    n_a = len(parts)

    def body(*refs):
        ins, outs = refs[:n_a], refs[n_a:2 * n_a]
        send_sems, recv_sems = refs[2 * n_a:]
        x, y, c = _place()
        cps = [pltpu.make_async_remote_copy(src_ref=ins[k].at[1 - c], dst_ref=outs[k], send_sem=send_sems.at[k],
                                            recv_sem=recv_sems.at[k], device_id=(x, y, 1 - c), device_id_type=MESH)
               for k in range(n_a)]
        for cp in cps:
            cp.start()
        for cp in cps:
            cp.wait()

    return pl.pallas_call(
        body, name="rs_pair", in_specs=[HBM_SPEC] * n_a, out_specs=[HBM_SPEC] * n_a,
        out_shape=[jax.ShapeDtypeStruct(p.shape[1:], F32) for p in parts],
        scratch_shapes=[pltpu.SemaphoreType.DMA((n_a,)), pltpu.SemaphoreType.DMA((n_a,))],
    )(*parts)


def _rs_chips(sums):
    n_a = len(sums)

    def body(*refs):
        ins, outs = refs[:n_a], refs[n_a:2 * n_a]
        send_sems, recv_sems = refs[2 * n_a:]
        x, y, c = _place()
        cps = []
        for k in range(n_a):
            for r, (fx, fy) in enumerate(CHIP_FLIPS):
                px, py = _flip(x, fx), _flip(y, fy)
                cps.append(pltpu.make_async_remote_copy(
                    src_ref=ins[k].at[2 * px + py], dst_ref=outs[k].at[r], send_sem=send_sems.at[3 * k + r],
                    recv_sem=recv_sems.at[3 * k + r], device_id=(px, py, c), device_id_type=MESH))
        for cp in cps:
            cp.start()
        for cp in cps:
            cp.wait()

    return pl.pallas_call(
        body, name="rs_chips", in_specs=[HBM_SPEC] * n_a, out_specs=[HBM_SPEC] * n_a,
        out_shape=[jax.ShapeDtypeStruct((3,) + s.shape[1:], s.dtype) for s in sums],
        scratch_shapes=[pltpu.SemaphoreType.DMA((3 * n_a,)), pltpu.SemaphoreType.DMA((3 * n_a,))],
    )(*sums)


def _rs_share(halves):
    n_a = len(halves)

    def body(*refs):
        ins, outs = refs[:n_a], refs[n_a:2 * n_a]
        send_sems, recv_sems, local_sems = refs[2 * n_a:]
        x, y, c = _place()
        local = [pltpu.make_async_copy(ins[k], outs[k].at[c], local_sems.at[k]) for k in range(n_a)]
        for cp in local:
            cp.start()
        sends = [pltpu.make_async_remote_copy(src_ref=ins[k], dst_ref=outs[k].at[c], send_sem=send_sems.at[k],
                                              recv_sem=recv_sems.at[k], device_id=(x, y, 1 - c), device_id_type=MESH)
                 for k in range(n_a)]
        for cp in sends:
            cp.start()
        for k in range(n_a):
            pltpu.make_async_remote_copy(src_ref=ins[k], dst_ref=outs[k].at[1 - c], send_sem=send_sems.at[k],
                                         recv_sem=recv_sems.at[k], device_id=(x, y, 1 - c),
                                         device_id_type=MESH).wait_recv()
        for cp in sends:
            cp.wait_send()
        for cp in local:
            cp.wait()

    return pl.pallas_call(
        body, name="rs_share", in_specs=[HBM_SPEC] * n_a, out_specs=[HBM_SPEC] * n_a,
        out_shape=[jax.ShapeDtypeStruct((2,) + s.shape, F32) for s in halves],
        scratch_shapes=[pltpu.SemaphoreType.DMA((n_a,)), pltpu.SemaphoreType.DMA((n_a,)),
                        pltpu.SemaphoreType.DMA((n_a,))],
    )(*halves)


def _pair_add(part, recv, core, name):
    _, _, n_h, n_c = part.shape
    th = _pick(n_h, (256, 176, 128))

    def body(sel_ref, p_ref, r_ref, o_ref):
        o_ref[...] = (p_ref[...] + r_ref[...]).astype(BF16)

    grid_spec = pltpu.PrefetchScalarGridSpec(
        num_scalar_prefetch=1, grid=(N_CHIPS, n_h // th),
        in_specs=[pl.BlockSpec((None, None, th, n_c), lambda j, i, sel: (sel[0], j, i, 0)),
                  pl.BlockSpec((None, th, n_c), lambda j, i, sel: (j, i, 0))],
        out_specs=pl.BlockSpec((None, th, n_c), lambda j, i, sel: (j, i, 0)))
    return pl.pallas_call(
        body, name=name, grid_spec=grid_spec, out_shape=jax.ShapeDtypeStruct(recv.shape, BF16),
        compiler_params=_cparams(("parallel", "parallel")),
    )(core.reshape(1), part, recv)


def _chip_add(sums, recv, chip, name):
    _, n_h, n_c = sums.shape
    th = _pick(n_h, (256, 176, 128))

    def body(sel_ref, s_ref, r_ref, o_ref):
        total = s_ref[...].astype(F32)
        for r in range(3):
            total = total + r_ref[r].astype(F32)
        o_ref[...] = total

    grid_spec = pltpu.PrefetchScalarGridSpec(
        num_scalar_prefetch=1, grid=(n_h // th,),
        in_specs=[pl.BlockSpec((None, th, n_c), lambda i, sel: (sel[0], i, 0)),
                  pl.BlockSpec((3, th, n_c), lambda i, sel: (0, i, 0))],
        out_specs=pl.BlockSpec((th, n_c), lambda i, sel: (i, 0)))
    return pl.pallas_call(
        body, name=name, grid_spec=grid_spec, out_shape=jax.ShapeDtypeStruct((n_h, n_c), F32),
        compiler_params=_cparams(("parallel",)),
    )(chip.reshape(1), sums, recv)


def _reduce_scatter(parts, core, chip):
    names = ("w_in", "w_o", "w_up", "w_down")
    from_sibling = _rs_pair(parts)
    pair_sums = [_pair_add(p, r, core, "pair_add_" + n) for p, r, n in zip(parts, from_sibling, names)]
    from_chips = _rs_chips(pair_sums)
    halves = [_chip_add(s, r, chip, "chip_add_" + n) for s, r, n in zip(pair_sums, from_chips, names)]
    return [f.reshape(-1, f.shape[-1]) for f in _rs_share(halves)]


SLAB_W = 1024


def _pack(arrays, rows):
    flat = jnp.concatenate([a.reshape(-1).astype(F32) for a in arrays])
    return jnp.pad(flat, (0, rows * SLAB_W - flat.shape[0])).reshape(rows, SLAB_W)


def _unpack(flat, shapes):
    out, off = [], 0
    for s in shapes:
        n = 1
        for d in s:
            n *= d
        out.append(flat[off:off + n].reshape(s))
        off += n
    return out


def _rows_for(arrays_or_shapes):
    n = 0
    for a in arrays_or_shapes:
        s = a if isinstance(a, tuple) else a.shape
        k = 1
        for d in s:
            k *= d
        n += k
    return -(-n // (8 * SLAB_W)) * 8


def _by_cols(a, n_cols):
    n_r = a.shape[0]
    return a.reshape(2, n_r // 2, N_CHIPS, n_cols).transpose(0, 2, 1, 3)


def _by_rows(a):
    n_r = a.shape[0] // N_CHIPS
    return a.reshape(N_CHIPS, 2, n_r // 2, a.shape[1]).transpose(1, 0, 2, 3)


def kernel(x, c, ln0_g, ln0_b, w_ada, b_ada, w_in, dn_conv, dn_a_log, dn_dt_bias, dn_norm_g, gla_w_gate2, gla_b_gate, gla_norm_g, w_o, ln1_g, ln1_b, ffn_w_up, ffn_conv, ffn_conv_b, ffn_w_down, ln2_g, ln2_b, loss_target, m_ln0_g, m_ln0_b, m_w_ada, m_b_ada, m_w_in, m_dn_conv, m_dn_a_log, m_dn_dt_bias, m_dn_norm_g, m_gla_w_gate2, m_gla_b_gate, m_gla_norm_g, m_w_o, m_ln1_g, m_ln1_b, m_ffn_w_up, m_ffn_conv, m_ffn_conv_b, m_ffn_w_down, m_ln2_g, m_ln2_b, v_ln0_g, v_ln0_b, v_w_ada, v_b_ada, v_w_in, v_dn_conv, v_dn_a_log, v_dn_dt_bias, v_dn_norm_g, v_gla_w_gate2, v_gla_b_gate, v_gla_norm_g, v_w_o, v_ln1_g, v_ln1_b, v_ffn_w_up, v_ffn_conv, v_ffn_conv_b, v_ffn_w_down, v_ln2_g, v_ln2_b):
    n_b, t_len, _ = x.shape
    xi, yi, ci = _place()
    chip = (2 * xi + yi).astype(jnp.int32)
    core = ci.astype(jnp.int32)
    me = 2 * chip + core
    n_all = N_DEV * n_b
    ada_cols = w_ada.shape[2]

    sharded_small = [dn_conv[0], gla_w_gate2[0], ffn_conv[0]]
    slab = _pack([c] + sharded_small, _rows_for([c] + sharded_small))
    gathered, _ = _all_gather8(slab, "gather_small")
    flat = gathered.reshape(N_DEV, -1)
    per_dev = [_unpack(flat[d], [c.shape] + [a.shape for a in sharded_small]) for d in range(N_DEV)]
    c_all = jnp.concatenate([per_dev[d][0] for d in range(N_DEV)], axis=0)
    dn_conv_f, gate2_f, ffn_conv_f = (jnp.concatenate([per_dev[2 * j][i] for j in range(N_CHIPS)], axis=1)
                                      for i in (1, 2, 3))

    b_ada_shard = lax.dynamic_slice(b_ada, (0, chip * ada_cols), (1, ada_cols))
    cond_all, mod_cols = _ada_fwd(c_all, w_ada[0], b_ada_shard)
    mod_g, _ = _all_gather8(mod_cols, "gather_mod")
    mod_full = jnp.concatenate([mod_g[2 * j] for j in range(N_CHIPS)], axis=1)
    mod = lax.dynamic_slice(mod_full, (me * n_b, 0), (n_b, 6 * D_MODEL)).reshape(n_b, 6, D_MODEL)

    halves = lambda a: a.astype(BF16).reshape(2, a.shape[0] // 2, a.shape[1])
    g_in, g_o, g_up, g_down = _gather_weights([halves(w_in[0]), halves(w_o[0]), halves(ffn_w_up[0]),
                                               halves(ffn_w_down[0])])
    cols = lambda g: g.reshape(N_CHIPS, -1, g.shape[-1]).transpose(1, 0, 2).reshape(-1, N_CHIPS * g.shape[-1])
    params = {
        "w_in_p": _pad_w_in(cols(g_in)), "w_o": g_o.reshape(-1, D_MODEL), "w_up": cols(g_up),
        "w_down": g_down.reshape(-1, D_MODEL),
        "dn_conv": dn_conv_f, "dn_a_log": dn_a_log[0], "dn_dt_bias": dn_dt_bias[0], "dn_norm_g": dn_norm_g[0],
        "gla_w_gate2": gate2_f, "gla_b_gate": gla_b_gate[0], "gla_norm_g": gla_norm_g[0],
        "ln0_g": ln0_g, "ln0_b": ln0_b, "ln1_g": ln1_g[0], "ln1_b": ln1_b[0], "ln2_g": ln2_g[0], "ln2_b": ln2_b[0],
        "ffn_conv": ffn_conv_f, "ffn_conv_b": ffn_conv_b[0],
    }

    loss_row, grad_x, gp, dmod = _local_step(x.reshape(n_b * t_len, D_MODEL), loss_target.reshape(n_b * t_len, D_MODEL),
                                             mod, params, n_b, t_len)
    loss = lax.psum(loss_row[0, 0], ("x", "y", "c"))

    summed_names = ["ln0_g", "ln0_b", "dn_conv", "dn_a_log", "dn_dt_bias", "dn_norm_g", "gla_w_gate2", "gla_b_gate",
                    "gla_norm_g", "ln1_g", "ln1_b", "ffn_conv", "ffn_conv_b", "ln2_g", "ln2_b"]
    summed_parts = [gp[n] for n in summed_names]
    sum_rows = _rows_for(summed_parts)
    slab = jnp.concatenate([_pack(summed_parts, sum_rows), _pack([dmod], _rows_for([dmod]))], axis=0)
    gathered, total = _all_gather8(slab, "reduce_small")
    small_g = dict(zip(summed_names, _unpack(total.reshape(-1), [a.shape for a in summed_parts])))
    dmod_rows = n_b * 6 * D_MODEL // SLAB_W
    dmod_all = gathered[:, sum_rows:sum_rows + dmod_rows, :].reshape(n_all, 6 * D_MODEL)

    g_b_ada = _col_sum(dmod_all)
    dmod_cols = lax.dynamic_slice(dmod_all, (0, chip * ada_cols), (n_all, ada_cols))
    g_w_ada = _mm(cond_all, dmod_cols, ta=True, name="mm_g_ada")

    g_w_in, g_w_o, g_w_up, g_w_down = _reduce_scatter(
        [_by_cols(_unpad_w_in(gp["w_in_p"]), w_in.shape[2]), _by_rows(gp["w_o"]),
         _by_cols(gp["w_up"], ffn_w_up.shape[2]), _by_rows(gp["w_down"])], core, chip)

    col_block = lambda a: lax.dynamic_slice(a, (0, chip * (a.shape[1] // N_CHIPS)), (a.shape[0], a.shape[1] // N_CHIPS))
    grads = {
        "ln0_g": small_g["ln0_g"], "ln0_b": small_g["ln0_b"], "w_ada": g_w_ada[None], "b_ada": g_b_ada,
        "w_in": g_w_in[None], "dn_conv": col_block(small_g["dn_conv"])[None], "dn_a_log": small_g["dn_a_log"][None],
        "dn_dt_bias": small_g["dn_dt_bias"][None], "dn_norm_g": small_g["dn_norm_g"][None],
        "gla_w_gate2": col_block(small_g["gla_w_gate2"])[None], "gla_b_gate": small_g["gla_b_gate"][None],
        "gla_norm_g": small_g["gla_norm_g"][None], "w_o": g_w_o[None], "ln1_g": small_g["ln1_g"][None],
        "ln1_b": small_g["ln1_b"][None], "ffn_w_up": g_w_up[None], "ffn_conv": col_block(small_g["ffn_conv"])[None],
        "ffn_conv_b": small_g["ffn_conv_b"][None], "ffn_w_down": g_w_down[None], "ln2_g": small_g["ln2_g"][None],
        "ln2_b": small_g["ln2_b"][None],
    }
    names = ["ln0_g", "ln0_b", "w_ada", "b_ada", "w_in", "dn_conv", "dn_a_log", "dn_dt_bias", "dn_norm_g",
             "gla_w_gate2", "gla_b_gate", "gla_norm_g", "w_o", "ln1_g", "ln1_b", "ffn_w_up", "ffn_conv", "ffn_conv_b",
             "ffn_w_down", "ln2_g", "ln2_b"]
    weights = dict(zip(names, [ln0_g, ln0_b, w_ada, b_ada, w_in, dn_conv, dn_a_log, dn_dt_bias, dn_norm_g, gla_w_gate2,
                               gla_b_gate, gla_norm_g, w_o, ln1_g, ln1_b, ffn_w_up, ffn_conv, ffn_conv_b, ffn_w_down,
                               ln2_g, ln2_b]))
    m_in = dict(zip(names, [m_ln0_g, m_ln0_b, m_w_ada, m_b_ada, m_w_in, m_dn_conv, m_dn_a_log, m_dn_dt_bias,
                            m_dn_norm_g, m_gla_w_gate2, m_gla_b_gate, m_gla_norm_g, m_w_o, m_ln1_g, m_ln1_b,
                            m_ffn_w_up, m_ffn_conv, m_ffn_conv_b, m_ffn_w_down, m_ln2_g, m_ln2_b]))
    v_in = dict(zip(names, [v_ln0_g, v_ln0_b, v_w_ada, v_b_ada, v_w_in, v_dn_conv, v_dn_a_log, v_dn_dt_bias,
                            v_dn_norm_g, v_gla_w_gate2, v_gla_b_gate, v_gla_norm_g, v_w_o, v_ln1_g, v_ln1_b,
                            v_ffn_w_up, v_ffn_conv, v_ffn_conv_b, v_ffn_w_down, v_ln2_g, v_ln2_b]))

    big = ("w_ada", "w_in", "w_o", "ffn_w_up", "ffn_w_down")
    delta, new_m, new_v = {}, {}, {}
    for n in big:
        d_n, m_n, v_n = _adamw(weights[n][0], grads[n][0], m_in[n][0], v_in[n][0], "adamw_" + n)
        delta[n], new_m[n], new_v[n] = d_n[None], m_n[None], v_n[None]
    small = [n for n in names if n not in big]
    shapes = [weights[n].shape for n in small]
    rows = _rows_for(shapes)
    d_s, m_s, v_s = _adamw(_pack([weights[n] for n in small], rows), _pack([grads[n] for n in small], rows),
                           _pack([m_in[n] for n in small], rows), _pack([v_in[n] for n in small], rows), "adamw_small")
    for out, slab_out in ((delta, d_s), (new_m, m_s), (new_v, v_s)):
        out.update(zip(small, _unpack(slab_out.reshape(-1), shapes)))

    return (loss, grad_x.reshape(x.shape), *[grads[n] for n in names], *[delta[n] for n in names],
            *[new_m[n] for n in names], *[new_v[n] for n in names])
```

```python
import functools

import jax
import jax.numpy as jnp
from jax import lax
from jax.experimental import pallas as pl
from jax.experimental.pallas import tpu as pltpu

F32 = jnp.float32
BF16 = jnp.bfloat16
MESH = pl.DeviceIdType.MESH

D_MODEL = 1024
HEADS = 4
HEAD_DIM = 128
GLA_KEY = 64
GATE_RANK = 16
CHUNK = 64
D_FF = 2816
ALPHA = 2.0 ** 0.25
EPS = 1e-6
N_CHIPS = 4
N_DEV = 8

PROJ_W = 3840
OFF_Z, OFF_GQ, OFF_GK, OFF_GV, OFF_GG, OFF_SMALL = 1536, 2048, 2304, 2560, 3072, 3584
W_IN_COLS = 3608

ADAM_LR, ADAM_B1, ADAM_B2, ADAM_EPS, ADAM_WD, ADAM_STEP = 0.001, 0.9, 0.999, 1e-08, 0.01, 10

VMEM_LIMIT = 56 * 1024 * 1024
ROW_TILE = 256


def _cparams(sem):
    return pltpu.CompilerParams(dimension_semantics=sem, vmem_limit_bytes=VMEM_LIMIT)


def _pick(n, prefs):
    for p in prefs:
        if n % p == 0:
            return p
    return n


def _mm(a, b, *, ta=False, tb=False, out_dtype=F32, name):
    a_slabs = a.shape[0] if a.ndim == 3 else 1
    b_slabs = b.shape[0] if b.ndim == 3 else 1
    assert not (ta and a_slabs > 1) and not (tb and b_slabs > 1)
    a2, b2 = a.shape[-2:], b.shape[-2:]
    if ta:
        k_dim, m_dim = a2
    else:
        m_dim, k_dim = a2[0], a2[1] * a_slabs
    n_dim = b2[0] if tb else b2[1] * b_slabs
    tm = _pick(m_dim, (1024, 1408, 512, 256, 128))
    tn = _pick(n_dim // b_slabs, (1536, 1408, 1280, 1024, 768, 512, 384, 256, 128))
    tk = _pick(k_dim // a_slabs, (1408, 1280, 1024, 512, 256, 128))
    nk = k_dim // tk
    nk_slab, nj_slab = nk // a_slabs, n_dim // tn // b_slabs
    dims = (((0 if ta else 1,), (1 if tb else 0,)), ((), ()))

    def body(a_ref, b_ref, o_ref, acc_ref):
        k = pl.program_id(2)

        @pl.when(k == 0)
        def _():
            acc_ref[...] = jnp.zeros_like(acc_ref)

        acc_ref[...] += lax.dot_general(a_ref[...].astype(BF16), b_ref[...].astype(BF16), dims,
                                        preferred_element_type=F32)

        @pl.when(k == nk - 1)
        def _():
            o_ref[...] = acc_ref[...].astype(o_ref.dtype)

    if ta:
        a_spec = pl.BlockSpec((tk, tm), lambda i, j, k: (k, i))
    elif a_slabs > 1:
        a_spec = pl.BlockSpec((None, tm, tk), lambda i, j, k: (k // nk_slab, i, k % nk_slab))
    else:
        a_spec = pl.BlockSpec((tm, tk), lambda i, j, k: (i, k))
    if tb:
        b_spec = pl.BlockSpec((tn, tk), lambda i, j, k: (j, k))
    elif b_slabs > 1:
        b_spec = pl.BlockSpec((None, tk, tn), lambda i, j, k: (j // nj_slab, k, j % nj_slab))
    else:
        b_spec = pl.BlockSpec((tk, tn), lambda i, j, k: (k, j))
    return pl.pallas_call(
        body, name=name, grid=(m_dim // tm, n_dim // tn, nk),
        in_specs=[a_spec, b_spec], out_specs=pl.BlockSpec((tm, tn), lambda i, j, k: (i, j)),
        out_shape=jax.ShapeDtypeStruct((m_dim, n_dim), out_dtype),
        scratch_shapes=[pltpu.VMEM((tm, tn), F32)],
        compiler_params=_cparams(("parallel", "parallel", "arbitrary")),
    )(a, b)


def _ln(x, g, b):
    mu = jnp.mean(x, -1, keepdims=True)
    xc = x - mu
    var = jnp.mean(xc * xc, -1, keepdims=True)
    return xc * lax.rsqrt(var + EPS) * g + b


def _softplus(x):
    return jnp.maximum(x, 0.0) + jnp.log(1.0 + jnp.exp(-jnp.abs(x)))


def _silu(x):
    return x * jax.nn.sigmoid(x)


def _dsilu(x):
    s = jax.nn.sigmoid(x)
    return s * (1.0 + x * (1.0 - s))


def _f_ln0(x, g, b, sc, sh):
    x0 = _ln(x, g, b)
    return x0, x0 * (1.0 + sc) + sh


def _f_ln1(x0, y, gt, g, b, sc, sh):
    x1 = _ln(ALPHA * x0 + (1.0 + gt) * y, g, b)
    return x1, x1 * (1.0 + sc) + sh


def _f_ln2_loss(x1, y2, gt, g, b, tgt):
    x2 = _ln(ALPHA * x1 + (1.0 + gt) * y2, g, b)
    err = x2 - tgt
    per_row = jnp.sum(err * err, -1, keepdims=True) * (0.5 / D_MODEL)
    return jnp.sum(per_row, 0, keepdims=True)


def _row_specs(t_len):
    nt = t_len // ROW_TILE
    row = pl.BlockSpec((ROW_TILE, D_MODEL), lambda b, i: (b * nt + i, 0))
    vec = pl.BlockSpec((1, D_MODEL), lambda b, i: (0, 0))
    mod = pl.BlockSpec((None, 6, D_MODEL), lambda b, i: (b, 0, 0))
    return nt, row, vec, mod


def _first_step():
    return jnp.logical_and(pl.program_id(0) == 0, pl.program_id(1) == 0)


def _acc(ref, val, first, at=(Ellipsis,)):
    @pl.when(first)
    def _():
        ref[at] = val

    @pl.when(jnp.logical_not(first))
    def _():
        ref[at] += val


def _acc_rows(ref, rows, first):
    for i, r in enumerate(rows):
        _acc(ref, r, first, at=(slice(i, i + 1), slice(None)))


def _ln0_fwd(x, g, b, mod, n_b, t_len):
    nt, row, vec, mods = _row_specs(t_len)

    def body(x_ref, g_ref, b_ref, mod_ref, x0_ref, h_ref):
        x0, h = _f_ln0(x_ref[...], g_ref[...], b_ref[...], mod_ref[1:2, :], mod_ref[0:1, :])
        x0_ref[...] = x0
        h_ref[...] = h.astype(BF16)

    return pl.pallas_call(
        body, name="ln0_fwd", grid=(n_b, nt), in_specs=[row, vec, vec, mods], out_specs=[row, row],
        out_shape=[jax.ShapeDtypeStruct(x.shape, F32), jax.ShapeDtypeStruct(x.shape, BF16)],
        compiler_params=_cparams(("parallel", "parallel")),
    )(x, g, b, mod)


def _ln0_bwd(x, g, b, mod, dx0, dh, n_b, t_len):
    nt, row, vec, mods = _row_specs(t_len)
    dmod_spec = pl.BlockSpec((None, 2, D_MODEL), lambda bb, i: (bb, 0, 0))

    def body(x_ref, g_ref, b_ref, mod_ref, dx0_ref, dh_ref, dx_ref, dg_ref, db_ref, dmod_ref):
        _, pull = jax.vjp(_f_ln0, x_ref[...], g_ref[...], b_ref[...], mod_ref[1:2, :], mod_ref[0:1, :])
        dx, dg, db, dsc, dsh = pull((dx0_ref[...], dh_ref[...]))
        dx_ref[...] = dx
        _acc(dg_ref, dg, _first_step())
        _acc(db_ref, db, _first_step())
        _acc_rows(dmod_ref, [dsh, dsc], pl.program_id(1) == 0)

    return pl.pallas_call(
        body, name="ln0_bwd", grid=(n_b, nt), in_specs=[row, vec, vec, mods, row, row],
        out_specs=[row, vec, vec, dmod_spec],
        out_shape=[jax.ShapeDtypeStruct(x.shape, F32), jax.ShapeDtypeStruct((1, D_MODEL), F32),
                   jax.ShapeDtypeStruct((1, D_MODEL), F32), jax.ShapeDtypeStruct((n_b, 2, D_MODEL), F32)],
        compiler_params=_cparams(("arbitrary", "arbitrary")),
    )(x, g, b, mod, dx0, dh)


def _ln1_fwd(x0, y, g, b, mod, n_b, t_len):
    nt, row, vec, mods = _row_specs(t_len)

    def body(x0_ref, y_ref, g_ref, b_ref, mod_ref, x1_ref, h_ref):
        x1, h = _f_ln1(x0_ref[...], y_ref[...], mod_ref[2:3, :], g_ref[...], b_ref[...],
                       mod_ref[4:5, :], mod_ref[3:4, :])
        x1_ref[...] = x1
        h_ref[...] = h.astype(BF16)

    return pl.pallas_call(
        body, name="ln1_fwd", grid=(n_b, nt), in_specs=[row, row, vec, vec, mods], out_specs=[row, row],
        out_shape=[jax.ShapeDtypeStruct(x0.shape, F32), jax.ShapeDtypeStruct(x0.shape, BF16)],
        compiler_params=_cparams(("parallel", "parallel")),
    )(x0, y, g, b, mod)


def _ln1_bwd(x0, y, g, b, mod, dx1, dh, n_b, t_len):
    nt, row, vec, mods = _row_specs(t_len)
    dmod_spec = pl.BlockSpec((None, 3, D_MODEL), lambda bb, i: (bb, 0, 0))

    def body(x0_ref, y_ref, g_ref, b_ref, mod_ref, dx1_ref, dh_ref, dx0_ref, dy_ref, dg_ref, db_ref, dmod_ref):
        _, pull = jax.vjp(_f_ln1, x0_ref[...], y_ref[...], mod_ref[2:3, :], g_ref[...], b_ref[...],
                          mod_ref[4:5, :], mod_ref[3:4, :])
        dx0, dy, dgt, dg, db, dsc, dsh = pull((dx1_ref[...], dh_ref[...]))
        dx0_ref[...] = dx0
        dy_ref[...] = dy.astype(BF16)
        _acc(dg_ref, dg, _first_step())
        _acc(db_ref, db, _first_step())
        _acc_rows(dmod_ref, [dgt, dsh, dsc], pl.program_id(1) == 0)

    return pl.pallas_call(
        body, name="ln1_bwd", grid=(n_b, nt), in_specs=[row, row, vec, vec, mods, row, row],
        out_specs=[row, row, vec, vec, dmod_spec],
        out_shape=[jax.ShapeDtypeStruct(x0.shape, F32), jax.ShapeDtypeStruct(x0.shape, BF16),
                   jax.ShapeDtypeStruct((1, D_MODEL), F32), jax.ShapeDtypeStruct((1, D_MODEL), F32),
                   jax.ShapeDtypeStruct((n_b, 3, D_MODEL), F32)],
        compiler_params=_cparams(("arbitrary", "arbitrary")),
    )(x0, y, g, b, mod, dx1, dh)


def _ln2_loss_bwd(x1, y2, g, b, mod, tgt, n_b, t_len):
    nt, row, vec, mods = _row_specs(t_len)
    one = pl.BlockSpec((1, 128), lambda bb, i: (0, 0))
    dmod_spec = pl.BlockSpec((None, 1, D_MODEL), lambda bb, i: (bb, 0, 0))

    def body(x1_ref, y2_ref, g_ref, b_ref, mod_ref, t_ref, loss_ref, dx1_ref, dy2_ref, dg_ref, db_ref, dgt_ref):
        loss, pull = jax.vjp(functools.partial(_f_ln2_loss, tgt=t_ref[...]), x1_ref[...], y2_ref[...],
                             mod_ref[5:6, :], g_ref[...], b_ref[...])
        dx1, dy2, dgt, dg, db = pull(jnp.ones((1, 1), F32))
        dx1_ref[...] = dx1
        dy2_ref[...] = dy2.astype(BF16)
        _acc(loss_ref, jnp.broadcast_to(loss, (1, 128)), _first_step())
        _acc(dg_ref, dg, _first_step())
        _acc(db_ref, db, _first_step())
        _acc(dgt_ref, dgt, pl.program_id(1) == 0)

    return pl.pallas_call(
        body, name="ln2_loss_bwd", grid=(n_b, nt), in_specs=[row, row, vec, vec, mods, row],
        out_specs=[one, row, row, vec, vec, dmod_spec],
        out_shape=[jax.ShapeDtypeStruct((1, 128), F32), jax.ShapeDtypeStruct(x1.shape, F32),
                   jax.ShapeDtypeStruct(x1.shape, BF16), jax.ShapeDtypeStruct((1, D_MODEL), F32),
                   jax.ShapeDtypeStruct((1, D_MODEL), F32), jax.ShapeDtypeStruct((n_b, 1, D_MODEL), F32)],
        compiler_params=_cparams(("arbitrary", "arbitrary")),
    )(x1, y2, g, b, mod, tgt)


def _shift_down(x, s):
    if s == 0:
        return x
    rows = lax.broadcasted_iota(jnp.int32, x.shape, 0)
    return jnp.where(rows >= s, pltpu.roll(x, s, 0), 0.0)


def _shift_up(x, s):
    if s == 0:
        return x
    t_len = x.shape[0]
    rows = lax.broadcasted_iota(jnp.int32, x.shape, 0)
    return jnp.where(rows < t_len - s, pltpu.roll(x, t_len - s, 0), 0.0)


def _conv(x, w):
    k_w = w.shape[0]
    out = w[k_w - 1:k_w, :] * x
    for k in range(k_w - 1):
        out = out + w[k:k + 1, :] * _shift_down(x, k_w - 1 - k)
    return out


def _conv_bwd(x, w, du):
    k_w = w.shape[0]
    dx = w[k_w - 1:k_w, :] * du
    dws = []
    for k in range(k_w):
        s = k_w - 1 - k
        if s:
            dx = dx + w[k:k + 1, :] * _shift_up(du, s)
        dws.append(jnp.sum(du * _shift_down(x, s), 0, keepdims=True))
    return dx, dws


def _dn_pre_fwd(proj, conv_w, n_b, t_len):
    n_ct = 3 * HEADS
    k_w = conv_w.shape[0]

    def body(x_ref, w_ref, o_ref):
        o_ref[...] = _silu(_conv(x_ref[...], w_ref[...]))

    return pl.pallas_call(
        body, name="dn_pre_fwd", grid=(n_ct, n_b),
        in_specs=[pl.BlockSpec((t_len, 128), lambda j, b: (b, j)), pl.BlockSpec((k_w, 128), lambda j, b: (0, j))],
        out_specs=pl.BlockSpec((t_len, 128), lambda j, b: (b, j)),
        out_shape=jax.ShapeDtypeStruct((n_b * t_len, n_ct * 128), F32),
        compiler_params=_cparams(("parallel", "parallel")),
    )(proj, conv_w)


def _dn_pre_bwd(proj, conv_w, dqkv, n_b, t_len):
    n_ct = 3 * HEADS
    k_w = conv_w.shape[0]

    def body(x_ref, w_ref, d_ref, dx_ref, dw_ref):
        x, w = x_ref[...], w_ref[...]
        du = d_ref[...] * _dsilu(_conv(x, w))
        dx, dw = _conv_bwd(x, w, du)
        dx_ref[...] = dx.astype(BF16)
        _acc_rows(dw_ref, dw, pl.program_id(1) == 0)

    return pl.pallas_call(
        body, name="dn_pre_bwd", grid=(n_ct, n_b),
        in_specs=[pl.BlockSpec((t_len, 128), lambda j, b: (b, j)), pl.BlockSpec((k_w, 128), lambda j, b: (0, j)),
                  pl.BlockSpec((t_len, 128), lambda j, b: (b, j))],
        out_specs=[pl.BlockSpec((t_len, 128), lambda j, b: (b, j)), pl.BlockSpec((k_w, 128), lambda j, b: (0, j))],
        out_shape=[jax.ShapeDtypeStruct((n_b * t_len, n_ct * 128), BF16),
                   jax.ShapeDtypeStruct((k_w, n_ct * 128), F32)],
        compiler_params=_cparams(("parallel", "arbitrary")),
    )(proj, conv_w, dqkv)


FFN_TC = 256
FFN_NT = D_FF // FFN_TC


def _ffn_specs(t_len):
    blk = lambda off: pl.BlockSpec((t_len, FFN_TC), lambda j, b: (b, j + off))
    wblk = lambda off: pl.BlockSpec((3, FFN_TC), lambda j, b: (0, j + off))
    bblk = lambda off: pl.BlockSpec((1, FFN_TC), lambda j, b: (0, j + off))
    return [blk(0), blk(FFN_NT), wblk(0), wblk(FFN_NT), bblk(0), bblk(FFN_NT)]


def _ffn_act_fwd(up, conv_w, conv_b, n_b, t_len):
    def body(g_ref, v_ref, wg_ref, wv_ref, bg_ref, bv_ref, o_ref):
        ug = _conv(g_ref[...], wg_ref[...]) + bg_ref[...]
        uv = _conv(v_ref[...], wv_ref[...]) + bv_ref[...]
        o_ref[...] = (_silu(ug) * uv).astype(BF16)

    return pl.pallas_call(
        body, name="ffn_act_fwd", grid=(FFN_NT, n_b), in_specs=_ffn_specs(t_len),
        out_specs=pl.BlockSpec((t_len, FFN_TC), lambda j, b: (b, j)),
        out_shape=jax.ShapeDtypeStruct((n_b * t_len, D_FF), BF16),
        compiler_params=_cparams(("parallel", "parallel")),
    )(up, up, conv_w, conv_w, conv_b, conv_b)


def _ffn_act_bwd(up, conv_w, conv_b, da, n_b, t_len):
    def body(g_ref, v_ref, wg_ref, wv_ref, bg_ref, bv_ref, da_ref, dup_ref, dw_ref, db_ref):
        first = pl.program_id(1) == 0
        xg, xv, wg, wv = g_ref[...], v_ref[...], wg_ref[...], wv_ref[...]
        ug = _conv(xg, wg) + bg_ref[...]
        uv = _conv(xv, wv) + bv_ref[...]
        d_act = da_ref[...]
        sig = jax.nn.sigmoid(ug)
        d_v = d_act * (ug * sig)
        d_g = d_act * uv * (sig * (1.0 + ug * (1.0 - sig)))
        for slab, (x, w, du) in enumerate(((xg, wg, d_g), (xv, wv, d_v))):
            dx, dw = _conv_bwd(x, w, du)
            dup_ref[slab] = dx.astype(BF16)
            for k, dw_k in enumerate(dw):
                _acc(dw_ref, dw_k, first, at=(slab, slice(k, k + 1), slice(None)))
            _acc(db_ref, jnp.sum(du, 0, keepdims=True), first, at=(slab, slice(None), slice(None)))

    return pl.pallas_call(
        body, name="ffn_act_bwd", grid=(FFN_NT, n_b),
        in_specs=_ffn_specs(t_len) + [pl.BlockSpec((t_len, FFN_TC), lambda j, b: (b, j))],
        out_specs=[pl.BlockSpec((2, t_len, FFN_TC), lambda j, b: (0, b, j)),
                   pl.BlockSpec((2, 3, FFN_TC), lambda j, b: (0, 0, j)),
                   pl.BlockSpec((2, 1, FFN_TC), lambda j, b: (0, 0, j))],
        out_shape=[jax.ShapeDtypeStruct((2, n_b * t_len, D_FF), BF16),
                   jax.ShapeDtypeStruct((2, 3, D_FF), F32), jax.ShapeDtypeStruct((2, 1, D_FF), F32)],
        compiler_params=_cparams(("parallel", "arbitrary")),
    )(up, up, conv_w, conv_w, conv_b, conv_b, da)


NN = (((2,), (1,)), ((0,), (0,)))
NT = (((2,), (2,)), ((0,), (0,)))
TN = (((1,), (1,)), ((0,), (0,)))


def _iota3(shape, axis):
    return lax.broadcasted_iota(jnp.int32, shape, axis)


def _dg(a, b, dims):
    return lax.dot_general(a, b, dims, preferred_element_type=F32)


def _dot(a, b):
    return _dg(a, b, NN)


def _dot_nt(a, b):
    return _dg(a, b, NT)


def _dot_tn(a, b):
    return _dg(a, b, TN)


def _split(a):
    hi = a.astype(BF16)
    return hi, (a - hi.astype(F32)).astype(BF16)


def _dg3(a, b, dims):
    ah, al = _split(a)
    bh, bl = _split(b)
    return _dg(ah, bh, dims) + (_dg(ah, bl, dims) + _dg(al, bh, dims))


@jax.custom_vjp
def _dot3(a, b):
    return _dg3(a, b, NN)


def _dot3_fwd(a, b):
    return _dg3(a, b, NN), (a, b)


def _dot3_bwd(res, g):
    a, b = res
    return _dg3(g, b, NT), _dg3(a, g, TN)


_dot3.defvjp(_dot3_fwd, _dot3_bwd)


def _lower_ones(g_n, n):
    shape = (g_n, n, n)
    return jnp.where(_iota3(shape, 1) >= _iota3(shape, 2), 1.0, 0.0).astype(BF16)


@jax.custom_vjp
def _chunk_cumsum(x):
    hi, lo = _split(x)
    tri = _lower_ones(x.shape[0], x.shape[1])
    return _dg(tri, hi, NN) + _dg(tri, lo, NN)


def _chunk_cumsum_fwd(x):
    return _chunk_cumsum(x), None


def _chunk_cumsum_bwd(_, g):
    hi, lo = _split(g)
    tri = _lower_ones(g.shape[0], g.shape[1])
    return (_dg(tri, hi, TN) + _dg(tri, lo, TN),)


_chunk_cumsum.defvjp(_chunk_cumsum_fwd, _chunk_cumsum_bwd)


@jax.custom_vjp
def _unit_lower_inv(m):
    n = m.shape[1]
    p = -m
    a = jnp.where(_iota3(m.shape, 1) == _iota3(m.shape, 2), 1.0, 0.0) + p
    span = 2
    while span < n:
        p = _dg3(p, p, NN)
        a = a + _dg3(a, p, NN)
        span *= 2
    return a


def _unit_lower_inv_fwd(m):
    a = _unit_lower_inv(m)
    return a, a


def _unit_lower_inv_bwd(a, da):
    return (-_dg3(a, _dg3(da, a, NT), TN),)


_unit_lower_inv.defvjp(_unit_lower_inv_fwd, _unit_lower_inv_bwd)


def _rms_gate(o, gn, gate):
    return o * lax.rsqrt(jnp.mean(o * o, -1, keepdims=True) + EPS) * gn * _silu(gate)


def _dn_chains(q, k, v, z, small, s_in, a_log, dt_bias, gn):
    g_n, c_len = q.shape[0], q.shape[1]
    sq = (g_n, c_len, c_len)
    row, col = _iota3(sq, 1), _iota3(sq, 2)
    causal, strict, eye = row >= col, row > col, row == col
    qn = q * lax.rsqrt(jnp.sum(q * q, -1, keepdims=True) + EPS) * (HEAD_DIM ** -0.5)
    kn = k * lax.rsqrt(jnp.sum(k * k, -1, keepdims=True) + EPS)
    lane = _iota3(small.shape, 2)
    head = jnp.bitwise_and(_iota3(small.shape, 0), HEADS - 1)
    la_all = -jnp.exp(a_log) * _softplus(small + dt_bias)
    la_c = jnp.sum(jnp.where(lane == head, la_all, 0.0), 2, keepdims=True)
    beta = jnp.sum(jnp.where(lane == head + HEADS, jax.nn.sigmoid(small), 0.0), 2, keepdims=True)
    la_b = jnp.broadcast_to(la_c, sq)
    la_r = jnp.sum(jnp.where(eye, la_b, 0.0), 1, keepdims=True)
    g_c = jnp.sum(jnp.where(causal, jnp.broadcast_to(la_r, sq), 0.0), 2, keepdims=True)
    g_r = jnp.sum(jnp.where(row <= col, la_b, 0.0), 1, keepdims=True)
    g_last = jnp.sum(la_c, 1, keepdims=True)
    decay = jnp.exp(jnp.where(causal, g_c - g_r, -1e30))
    e_g = jnp.exp(g_c)
    kb = kn * beta
    m_low = jnp.where(strict, _dot_nt(kb, kn) * decay, 0.0)
    a_inv = _unit_lower_inv(m_low)
    u = _dot3(a_inv, v * beta)
    w = _dot3(a_inv, kb * e_g)
    attn = _dot_nt(qn, kn) * decay
    v_new = u - _dot(w, s_in)
    o = _dot(qn * e_g, s_in) + _dot(attn, v_new)
    s_out = s_in * jnp.exp(g_last) + _dot_tn(kn * jnp.exp(g_last - g_c), v_new)
    return _rms_gate(o, gn, z), s_out


def _gla_chains(q, k, v, gate, small, s_in, w2, b2, gn):
    g_n, c_len = q.shape[0], q.shape[1]
    sq, kk = (g_n, c_len, c_len), (g_n, GLA_KEY, GLA_KEY)
    causal = _iota3(sq, 1) >= _iota3(sq, 2)
    la = -_softplus(-(_dot(small, w2) + b2)) * (1.0 / 16.0)
    b = _chunk_cumsum(la)
    b_last = jnp.sum(jnp.where(_iota3(b.shape, 1) == c_len - 1, b, 0.0), 1, keepdims=True)
    q_dec = q * (GLA_KEY ** -0.5) * jnp.exp(b)
    attn = jnp.where(causal, _dot_nt(q_dec, k * jnp.exp(-b)), 0.0)
    o = _dot(q_dec, s_in) + _dot(attn, v)
    g_row = jnp.exp(b_last)
    g_col = jnp.sum(jnp.where(_iota3(kk, 1) == _iota3(kk, 2), jnp.broadcast_to(g_row, kk), 0.0), 2, keepdims=True)
    s_out = s_in * g_col + _dot_tn(k * jnp.exp(b_last - b), v)
    return _rms_gate(o, gn, gate), s_out


def _chunk_spec(n_b, width, col_block, n_c, reverse=False):
    if reverse:
        return pl.BlockSpec((n_b, CHUNK, width), lambda n: (0, n_c - 1 - n, col_block))
    return pl.BlockSpec((n_b, CHUNK, width), lambda n: (0, n, col_block))


def _hist_spec(n_b, d_k, n_c, reverse=False):
    if reverse:
        return pl.BlockSpec((None, n_b * HEADS, d_k, HEAD_DIM), lambda n: (n_c - 1 - n, 0, 0, 0))
    return pl.BlockSpec((None, n_b * HEADS, d_k, HEAD_DIM), lambda n: (n, 0, 0, 0))


def _stack_chains(ref, n_b, slices):
    return jnp.stack([ref[b, :, sl] for b in range(n_b) for sl in slices], axis=0)


def _per_chain(ref, n_b):
    return jnp.stack([ref[b] for b in range(n_b) for _ in range(HEADS)], axis=0)


def _unstack_chains(ref, val, n_b, slices, offset=0):
    for b in range(n_b):
        for h, sl in enumerate(slices):
            ref[b, :, slice(offset + sl.start, offset + sl.stop)] = val[b * HEADS + h].astype(ref.dtype)


def _gate_weights(w2_ref, b2_ref, n_b):
    w2 = jnp.stack([w2_ref[:, ks] for _ in range(n_b) for ks in GLA_KSL], axis=0)
    b2 = jnp.stack([b2_ref[:, ks] for _ in range(n_b) for ks in GLA_KSL], axis=0)
    return w2, b2


def _sum_heads(val, n_b):
    return [sum(val[b * HEADS + h] for h in range(HEADS)) for b in range(n_b)]


def _const_spec(shape):
    return pl.BlockSpec(shape, lambda n: (0,) * len(shape))


DN_SL = [slice(h * HEAD_DIM, (h + 1) * HEAD_DIM) for h in range(HEADS)]
GLA_KSL = [slice(h * GLA_KEY, (h + 1) * GLA_KEY) for h in range(HEADS)]


def _dn_scan_fwd(qkv, proj, a_log, dt_bias, gn, n_b, t_len):
    n_c = t_len // CHUNK
    spec = functools.partial(_chunk_spec, n_b, n_c=n_c)

    def body(q_ref, k_ref, v_ref, z_ref, sm_ref, al_ref, dt_ref, gn_ref, o_ref, hist_ref, s_ref):
        @pl.when(pl.program_id(0) == 0)
        def _():
            s_ref[...] = jnp.zeros_like(s_ref)

        s_in = s_ref[...]
        hist_ref[...] = s_in
        og, s_out = _dn_chains(*(_stack_chains(r, n_b, DN_SL) for r in (q_ref, k_ref, v_ref, z_ref)),
                               _per_chain(sm_ref, n_b), s_in, al_ref[...], dt_ref[...], gn_ref[...])
        _unstack_chains(o_ref, og, n_b, DN_SL)
        s_ref[...] = s_out

    qkv3, proj3 = qkv.reshape(n_b, t_len, -1), proj.reshape(n_b, t_len, -1)
    o, hist = pl.pallas_call(
        body, name="dn_scan_fwd", grid=(n_c,),
        in_specs=[spec(512, 0), spec(512, 1), spec(512, 2), spec(512, OFF_Z // 512), spec(128, OFF_SMALL // 128),
                  _const_spec((1, 128)), _const_spec((1, 128)), _const_spec((1, 128))],
        out_specs=[spec(512, 0), _hist_spec(n_b, HEAD_DIM, n_c)],
        out_shape=[jax.ShapeDtypeStruct((n_b, t_len, 512), BF16),
                   jax.ShapeDtypeStruct((n_c, n_b * HEADS,HEAD_DIM, HEAD_DIM), F32)],
        scratch_shapes=[pltpu.VMEM((n_b * HEADS,HEAD_DIM, HEAD_DIM), F32)],
        compiler_params=_cparams(("arbitrary",)),
    )(qkv3, qkv3, qkv3, proj3, proj3, a_log, dt_bias, gn)
    return o.reshape(n_b * t_len, 512), hist


def _dn_scan_bwd(qkv, proj, a_log, dt_bias, gn, hist, d_o, n_b, t_len):
    n_c = t_len // CHUNK
    rev = functools.partial(_chunk_spec, n_b, n_c=n_c, reverse=True)

    def body(q_ref, k_ref, v_ref, z_ref, sm_ref, al_ref, dt_ref, gn_ref, hist_ref, do_ref,
             dqkv_ref, dz_ref, dsm_ref, dal_ref, ddt_ref, dgn_ref, ds_ref):
        first = pl.program_id(0) == 0

        @pl.when(first)
        def _():
            ds_ref[...] = jnp.zeros_like(ds_ref)

        _, pull = jax.vjp(_dn_chains, *(_stack_chains(r, n_b, DN_SL) for r in (q_ref, k_ref, v_ref, z_ref)),
                          _per_chain(sm_ref, n_b), hist_ref[...], al_ref[...], dt_ref[...], gn_ref[...])
        dq, dk, dv, dz, dsm, ds_in, dal, ddt, dgn = pull((_stack_chains(do_ref, n_b, DN_SL), ds_ref[...]))
        _unstack_chains(dqkv_ref, dq, n_b, DN_SL)
        _unstack_chains(dqkv_ref, dk, n_b, DN_SL, offset=512)
        _unstack_chains(dqkv_ref, dv, n_b, DN_SL, offset=1024)
        _unstack_chains(dz_ref, dz, n_b, DN_SL)
        ds_ref[...] = ds_in
        for b, dsm_b in enumerate(_sum_heads(dsm, n_b)):
            dsm_ref[b] = dsm_b
        _acc(dal_ref, dal, first)
        _acc(ddt_ref, ddt, first)
        _acc(dgn_ref, dgn, first)

    qkv3, proj3, do3 = (a.reshape(n_b, t_len, -1) for a in (qkv, proj, d_o))
    vec = jax.ShapeDtypeStruct((1, 128), F32)
    dqkv, dz, dsm, dal, ddt, dgn = pl.pallas_call(
        body, name="dn_scan_bwd", grid=(n_c,),
        in_specs=[rev(512, 0), rev(512, 1), rev(512, 2), rev(512, OFF_Z // 512), rev(128, OFF_SMALL // 128),
                  _const_spec((1, 128)), _const_spec((1, 128)), _const_spec((1, 128)),
                  _hist_spec(n_b, HEAD_DIM, n_c, reverse=True), rev(512, 0)],
        out_specs=[rev(1536, 0), rev(512, 0), rev(128, 0),
                   _const_spec((1, 128)), _const_spec((1, 128)), _const_spec((1, 128))],
        out_shape=[jax.ShapeDtypeStruct((n_b, t_len, 1536), F32), jax.ShapeDtypeStruct((n_b, t_len, 512), BF16),
                   jax.ShapeDtypeStruct((n_b, t_len, 128), F32), vec, vec, vec],
        scratch_shapes=[pltpu.VMEM((n_b * HEADS,HEAD_DIM, HEAD_DIM), F32)],
        compiler_params=_cparams(("arbitrary",)),
    )(qkv3, qkv3, qkv3, proj3, proj3, a_log, dt_bias, gn, hist, do3)
    n_rows = n_b * t_len
    return dqkv.reshape(n_rows, 1536), dz.reshape(n_rows, 512), dsm.reshape(n_rows, 128), dal, ddt, dgn


def _gla_scan_fwd(proj, w2, b2, gn, n_b, t_len):
    n_c = t_len // CHUNK
    spec = functools.partial(_chunk_spec, n_b, n_c=n_c)

    def body(q_ref, k_ref, v_ref, g_ref, sm_ref, w2_ref, b2_ref, gn_ref, o_ref, hist_ref, s_ref):
        @pl.when(pl.program_id(0) == 0)
        def _():
            s_ref[...] = jnp.zeros_like(s_ref)

        s_in = s_ref[...]
        hist_ref[...] = s_in
        og, s_out = _gla_chains(_stack_chains(q_ref, n_b, GLA_KSL), _stack_chains(k_ref, n_b, GLA_KSL),
                                _stack_chains(v_ref, n_b, DN_SL), _stack_chains(g_ref, n_b, DN_SL),
                                _per_chain(sm_ref, n_b), s_in, *_gate_weights(w2_ref, b2_ref, n_b), gn_ref[...])
        _unstack_chains(o_ref, og, n_b, DN_SL)
        s_ref[...] = s_out

    proj3 = proj.reshape(n_b, t_len, -1)
    o, hist = pl.pallas_call(
        body, name="gla_scan_fwd", grid=(n_c,),
        in_specs=[spec(256, OFF_GQ // 256), spec(256, OFF_GK // 256), spec(512, OFF_GV // 512),
                  spec(512, OFF_GG // 512), spec(128, OFF_SMALL // 128),
                  _const_spec((128, 256)), _const_spec((1, 256)), _const_spec((1, 128))],
        out_specs=[spec(512, 0), _hist_spec(n_b, GLA_KEY, n_c)],
        out_shape=[jax.ShapeDtypeStruct((n_b, t_len, 512), BF16),
                   jax.ShapeDtypeStruct((n_c, n_b * HEADS,GLA_KEY, HEAD_DIM), F32)],
        scratch_shapes=[pltpu.VMEM((n_b * HEADS,GLA_KEY, HEAD_DIM), F32)],
        compiler_params=_cparams(("arbitrary",)),
    )(proj3, proj3, proj3, proj3, proj3, w2, b2, gn)
    return o.reshape(n_b * t_len, 512), hist


def _gla_scan_bwd(proj, w2, b2, gn, hist, d_o, n_b, t_len):
    n_c = t_len // CHUNK
    rev = functools.partial(_chunk_spec, n_b, n_c=n_c, reverse=True)

    def body(q_ref, k_ref, v_ref, g_ref, sm_ref, w2_ref, b2_ref, gn_ref, hist_ref, do_ref,
             dq_ref, dk_ref, dv_ref, dg_ref, dsm_ref, dw2_ref, db2_ref, dgn_ref, ds_ref):
        first = pl.program_id(0) == 0

        @pl.when(first)
        def _():
            ds_ref[...] = jnp.zeros_like(ds_ref)

        _, pull = jax.vjp(_gla_chains, _stack_chains(q_ref, n_b, GLA_KSL), _stack_chains(k_ref, n_b, GLA_KSL),
                          _stack_chains(v_ref, n_b, DN_SL), _stack_chains(g_ref, n_b, DN_SL),
                          _per_chain(sm_ref, n_b), hist_ref[...], *_gate_weights(w2_ref, b2_ref, n_b), gn_ref[...])
        dq, dk, dv, dg, dsm, ds_in, dw2, db2, dgn = pull((_stack_chains(do_ref, n_b, DN_SL), ds_ref[...]))
        _unstack_chains(dq_ref, dq, n_b, GLA_KSL)
        _unstack_chains(dk_ref, dk, n_b, GLA_KSL)
        _unstack_chains(dv_ref, dv, n_b, DN_SL)
        _unstack_chains(dg_ref, dg, n_b, DN_SL)
        ds_ref[...] = ds_in
        for b, dsm_b in enumerate(_sum_heads(dsm, n_b)):
            dsm_ref[b] = dsm_b
        for h, ks in enumerate(GLA_KSL):
            _acc(dw2_ref, sum(dw2[b * HEADS + h] for b in range(n_b)), first, at=(slice(None), ks))
            _acc(db2_ref, sum(db2[b * HEADS + h] for b in range(n_b)), first, at=(slice(None), ks))
        _acc(dgn_ref, dgn, first)

    proj3, do3 = proj.reshape(n_b, t_len, -1), d_o.reshape(n_b, t_len, -1)
    dq, dk, dv, dg, dsm, dw2, db2, dgn = pl.pallas_call(
        body, name="gla_scan_bwd", grid=(n_c,),
        in_specs=[rev(256, OFF_GQ // 256), rev(256, OFF_GK // 256), rev(512, OFF_GV // 512), rev(512, OFF_GG // 512),
                  rev(128, OFF_SMALL // 128),
                  _const_spec((128, 256)), _const_spec((1, 256)), _const_spec((1, 128)),
                  _hist_spec(n_b, GLA_KEY, n_c, reverse=True), rev(512, 1)],
        out_specs=[rev(256, 0), rev(256, 0), rev(512, 0), rev(512, 0), rev(128, 0),
                   _const_spec((128, 256)), _const_spec((1, 256)), _const_spec((1, 128))],
        out_shape=[jax.ShapeDtypeStruct((n_b, t_len, 256), BF16), jax.ShapeDtypeStruct((n_b, t_len, 256), BF16),
                   jax.ShapeDtypeStruct((n_b, t_len, 512), BF16), jax.ShapeDtypeStruct((n_b, t_len, 512), BF16),
                   jax.ShapeDtypeStruct((n_b, t_len, 128), F32), jax.ShapeDtypeStruct((128, 256), F32),
                   jax.ShapeDtypeStruct((1, 256), F32), jax.ShapeDtypeStruct((1, 128), F32)],
        scratch_shapes=[pltpu.VMEM((n_b * HEADS,GLA_KEY, HEAD_DIM), F32)],
        compiler_params=_cparams(("arbitrary",)),
    )(proj3, proj3, proj3, proj3, proj3, w2, b2, gn, hist, do3)
    n_rows = n_b * t_len
    return (dq.reshape(n_rows, 256), dk.reshape(n_rows, 256), dv.reshape(n_rows, 512), dg.reshape(n_rows, 512),
            dsm.reshape(n_rows, 128), dw2, db2, dgn)


def _pad_w_in(w_in):
    return jnp.concatenate([w_in[:, 0:2048], w_in[:, 2056:3592], w_in[:, 2048:2056], w_in[:, 3592:3608],
                            jnp.zeros((w_in.shape[0], PROJ_W - W_IN_COLS), w_in.dtype)], axis=1)


def _unpad_w_in(g):
    return jnp.concatenate([g[:, 0:2048], g[:, 3584:3592], g[:, 2048:3584], g[:, 3592:3608]], axis=1)


def _lane_vec(v, offset=0):
    return jnp.zeros((1, 128), F32).at[0, offset:offset + v.shape[0]].set(v)


def _local_step(x, tgt, mod, p, n_b, t_len):
    row1 = lambda v: v.reshape(1, -1)
    a_log, dt_bias = _lane_vec(p["dn_a_log"]), _lane_vec(p["dn_dt_bias"])
    dn_gn, gla_gn = row1(p["dn_norm_g"]), row1(p["gla_norm_g"])
    w2 = jnp.zeros((128, 256), F32).at[8:8 + GATE_RANK].set(p["gla_w_gate2"])
    b2 = row1(p["gla_b_gate"])
    ln0_g, ln0_b, ln1_g, ln1_b, ln2_g, ln2_b = (row1(p[k]) for k in ("ln0_g", "ln0_b", "ln1_g", "ln1_b", "ln2_g", "ln2_b"))
    conv_b = row1(p["ffn_conv_b"])

    x0, h1 = _ln0_fwd(x, ln0_g, ln0_b, mod, n_b, t_len)
    proj = _mm(h1, p["w_in_p"], name="mm_proj")
    qkv = _dn_pre_fwd(proj, p["dn_conv"], n_b, t_len)
    o_dn, hist_dn = _dn_scan_fwd(qkv, proj, a_log, dt_bias, dn_gn, n_b, t_len)
    o_gla, hist_gla = _gla_scan_fwd(proj, w2, b2, gla_gn, n_b, t_len)
    o_mix = jnp.concatenate([o_dn, o_gla], axis=1)
    y = _mm(o_mix, p["w_o"], name="mm_wo")
    x1, h2 = _ln1_fwd(x0, y, ln1_g, ln1_b, mod, n_b, t_len)
    up = _mm(h2, p["w_up"], name="mm_up")
    act = _ffn_act_fwd(up, p["ffn_conv"], conv_b, n_b, t_len)
    y2 = _mm(act, p["w_down"], name="mm_down")

    loss, dx1, dy2, g_ln2_g, g_ln2_b, dgt_f = _ln2_loss_bwd(x1, y2, ln2_g, ln2_b, mod, tgt, n_b, t_len)
    g_w_down = _mm(act, dy2, ta=True, name="mm_g_down")
    d_act = _mm(dy2, p["w_down"], tb=True, name="mm_d_act")
    d_up, g_ffn_conv, g_conv_b = _ffn_act_bwd(up, p["ffn_conv"], conv_b, d_act, n_b, t_len)
    g_w_up = _mm(h2, d_up, ta=True, name="mm_g_up")
    dh2 = _mm(d_up, p["w_up"], tb=True, name="mm_d_h2")
    dx0, dy, g_ln1_g, g_ln1_b, dmod_1 = _ln1_bwd(x0, y, ln1_g, ln1_b, mod, dx1, dh2, n_b, t_len)
    g_w_o = _mm(o_mix, dy, ta=True, name="mm_g_wo")
    d_o = _mm(dy, p["w_o"], tb=True, name="mm_d_o")
    dqkv, dz, dsm_dn, g_a_log, g_dt_bias, g_dn_gn = _dn_scan_bwd(qkv, proj, a_log, dt_bias, dn_gn, hist_dn, d_o,
                                                                n_b, t_len)
    dgq, dgk, dgv, dgg, dsm_gla, g_w2, g_b2, g_gla_gn = _gla_scan_bwd(proj, w2, b2, gla_gn, hist_gla, d_o, n_b, t_len)
    d_pre, g_dn_conv = _dn_pre_bwd(proj, p["dn_conv"], dqkv, n_b, t_len)
    d_small = (dsm_dn + dsm_gla).astype(BF16)
    d_proj = jnp.concatenate([d_pre, dz, dgq, dgk, dgv, dgg, d_small, jnp.zeros_like(d_small)], axis=1)
    g_w_in_p = _mm(h1, d_proj, ta=True, name="mm_g_win")
    dh1 = _mm(d_proj, p["w_in_p"], tb=True, name="mm_d_h1")
    grad_x, g_ln0_g, g_ln0_b, dmod_0 = _ln0_bwd(x, ln0_g, ln0_b, mod, dx0, dh1, n_b, t_len)

    dmod = jnp.concatenate([dmod_0, dmod_1[:, 0:1], dmod_1[:, 1:3], dgt_f], axis=1)
    grads = {
        "ln0_g": g_ln0_g[0], "ln0_b": g_ln0_b[0], "w_in_p": g_w_in_p, "dn_conv": g_dn_conv,
        "dn_a_log": g_a_log[0, 0:HEADS], "dn_dt_bias": g_dt_bias[0, 0:HEADS], "dn_norm_g": g_dn_gn[0],
        "gla_w_gate2": g_w2[8:8 + GATE_RANK], "gla_b_gate": g_b2[0], "gla_norm_g": g_gla_gn[0],
        "w_o": g_w_o, "ln1_g": g_ln1_g[0], "ln1_b": g_ln1_b[0], "w_up": g_w_up,
        "ffn_conv": jnp.concatenate([g_ffn_conv[0], g_ffn_conv[1]], axis=1),
        "ffn_conv_b": jnp.concatenate([g_conv_b[0, 0], g_conv_b[1, 0]]), "w_down": g_w_down,
        "ln2_g": g_ln2_g[0], "ln2_b": g_ln2_b[0],
    }
    return loss, grad_x, grads, dmod


def _ada_fwd(c_all, w_shard, b_shard):
    n_all, n_col = c_all.shape[0], w_shard.shape[1]
    tn = 512

    def body(c_ref, w_ref, b_ref, cond_ref, mod_ref):
        cond = _silu(c_ref[...])
        cond_ref[...] = cond
        mod_ref[...] = jnp.dot(cond.astype(BF16), w_ref[...].astype(BF16), preferred_element_type=F32) + b_ref[...]

    return pl.pallas_call(
        body, name="ada_fwd", grid=(n_col // tn,),
        in_specs=[pl.BlockSpec((n_all, D_MODEL), lambda j: (0, 0)), pl.BlockSpec((D_MODEL, tn), lambda j: (0, j)),
                  pl.BlockSpec((1, tn), lambda j: (0, j))],
        out_specs=[pl.BlockSpec((n_all, D_MODEL), lambda j: (0, 0)), pl.BlockSpec((n_all, tn), lambda j: (0, j))],
        out_shape=[jax.ShapeDtypeStruct((n_all, D_MODEL), F32), jax.ShapeDtypeStruct((n_all, n_col), F32)],
        compiler_params=_cparams(("arbitrary",)),
    )(c_all, w_shard, b_shard)


def _col_sum(a):
    def body(a_ref, o_ref):
        o_ref[...] = jnp.sum(a_ref[...], 0, keepdims=True)

    return pl.pallas_call(body, name="col_sum", out_shape=jax.ShapeDtypeStruct((1, a.shape[1]), F32))(a)


def _adamw(w, g, m, v, name):
    n_r, n_c = w.shape
    tr = _pick(n_r, (256, 64, 32, 16, 8))

    def body(w_ref, g_ref, m_ref, v_ref, d_ref, nm_ref, nv_ref):
        grad = g_ref[...]
        new_m = ADAM_B1 * m_ref[...] + (1.0 - ADAM_B1) * grad
        new_v = ADAM_B2 * v_ref[...] + (1.0 - ADAM_B2) * (grad * grad)
        m_hat = new_m / (1.0 - ADAM_B1 ** ADAM_STEP)
        v_hat = new_v / (1.0 - ADAM_B2 ** ADAM_STEP)
        d_ref[...] = -ADAM_LR * (m_hat / (jnp.sqrt(v_hat) + ADAM_EPS) + ADAM_WD * w_ref[...])
        nm_ref[...] = new_m
        nv_ref[...] = new_v

    blk = pl.BlockSpec((tr, n_c), lambda i: (i, 0))
    out = jax.ShapeDtypeStruct(w.shape, F32)
    return pl.pallas_call(
        body, name=name, grid=(n_r // tr,), in_specs=[blk] * 4, out_specs=[blk] * 3, out_shape=[out] * 3,
        compiler_params=_cparams(("parallel",)),
    )(w, g, m, v)


HBM_SPEC = pl.BlockSpec(memory_space=pltpu.HBM)
VMEM_SPEC = pl.BlockSpec(memory_space=pltpu.VMEM)
CHIP_FLIPS = ((1, 0), (0, 1), (1, 1))


def _place():
    return lax.axis_index("x"), lax.axis_index("y"), lax.axis_index("c")


def _flip(v, f):
    return 1 - v if f else v


def _all_gather8(slab, name):
    n_r, n_w = slab.shape

    def body(x_ref, o_ref, s_ref, send_sems, recv_sems, local_sem):
        x, y, c = _place()
        me = 4 * x + 2 * y + c
        mine = pltpu.make_async_copy(x_ref, o_ref.at[me], local_sem)
        mine.start()
        peers = [(_flip(x, k & 4), _flip(y, k & 2), _flip(c, k & 1)) for k in range(1, N_DEV)]
        sends = []
        for k, peer in enumerate(peers):
            cp = pltpu.make_async_remote_copy(src_ref=x_ref, dst_ref=o_ref.at[me], send_sem=send_sems.at[k],
                                              recv_sem=recv_sems.at[k], device_id=peer, device_id_type=MESH)
            cp.start()
            sends.append(cp)
        for k, (px, py, pc) in enumerate(peers):
            pltpu.make_async_remote_copy(src_ref=x_ref, dst_ref=o_ref.at[4 * px + 2 * py + pc],
                                         send_sem=send_sems.at[k], recv_sem=recv_sems.at[k],
                                         device_id=(px, py, pc), device_id_type=MESH).wait_recv()
        for cp in sends:
            cp.wait_send()
        mine.wait()
        total = o_ref[0]
        for d in range(1, N_DEV):
            total = total + o_ref[d]
        s_ref[...] = total

    return pl.pallas_call(
        body, name=name, in_specs=[VMEM_SPEC], out_specs=[VMEM_SPEC, VMEM_SPEC],
        out_shape=[jax.ShapeDtypeStruct((N_DEV, n_r, n_w), F32), jax.ShapeDtypeStruct((n_r, n_w), F32)],
        scratch_shapes=[pltpu.SemaphoreType.DMA((N_DEV - 1,)), pltpu.SemaphoreType.DMA((N_DEV - 1,)),
                        pltpu.SemaphoreType.DMA],
    )(slab)


def _gather_weights(shards):
    n_a = len(shards)

    def body(*refs):
        ins, outs, stage = refs[:n_a], refs[n_a:2 * n_a], refs[2 * n_a:3 * n_a]
        send_sems, recv_sems, local_sems = refs[3 * n_a:]
        x, y, c = _place()
        me_chip = 2 * x + y
        sibling = (x, y, 1 - c)
        chips = [(_flip(x, fx), _flip(y, fy)) for fx, fy in CHIP_FLIPS]
        stage_in = [pltpu.make_async_copy(ins[k], stage[k], local_sems.at[k]) for k in range(n_a)]
        for cp in stage_in:
            cp.start()

        def copy(k, slot, chip_of_block, half, to, src=None):
            dst = outs[k].at[chip_of_block, half]
            return pltpu.make_async_remote_copy(src_ref=dst if src is None else src, dst_ref=dst,
                                                send_sem=send_sems.at[k * 6 + slot], recv_sem=recv_sems.at[k * 6 + slot],
                                                device_id=to, device_id_type=MESH)

        first = [copy(k, r, me_chip, c, (*chips[r], c), src=ins[k].at[c]) for k in range(n_a) for r in range(3)]
        for cp in first:
            cp.start()
        stage_out = []
        for k in range(n_a):
            stage_in[k].wait()
            cp = pltpu.make_async_copy(stage[k], outs[k].at[me_chip], local_sems.at[n_a + k])
            cp.start()
            stage_out.append(cp)
        passed = []
        for k in range(n_a):
            for r, (px, py) in enumerate(chips):
                copy(k, r, 2 * px + py, c, (x, y, c)).wait_recv()
                fwd = copy(k, 3 + r, 2 * px + py, c, sibling)
                fwd.start()
                passed.append(fwd)
        for k in range(n_a):
            for r, (px, py) in enumerate(chips):
                copy(k, 3 + r, 2 * px + py, 1 - c, (x, y, c)).wait_recv()
        for cp in first + passed:
            cp.wait_send()
        for cp in stage_out:
            cp.wait()

    return pl.pallas_call(
        body, name="gather_weights", in_specs=[HBM_SPEC] * n_a, out_specs=[HBM_SPEC] * n_a,
        out_shape=[jax.ShapeDtypeStruct((N_CHIPS,) + s.shape, s.dtype) for s in shards],
        scratch_shapes=[pltpu.VMEM(s.shape, s.dtype) for s in shards]
        + [pltpu.SemaphoreType.DMA((6 * n_a,)), pltpu.SemaphoreType.DMA((6 * n_a,)),
           pltpu.SemaphoreType.DMA((2 * n_a,))],
        compiler_params=pltpu.CompilerParams(vmem_limit_bytes=VMEM_LIMIT),
    )(*shards)


def _rs_pair(parts):
    n_a = len(parts)

    def body(*refs):
        ins, outs = refs[:n_a], refs[n_a:2 * n_a]
        send_sems, recv_sems = refs[2 * n_a:]
        x, y, c = _place()
        cps = [pltpu.make_async_remote_copy(src_ref=ins[k].at[1 - c], dst_ref=outs[k], send_sem=send_sems.at[k],
                                            recv_sem=recv_sems.at[k], device_id=(x, y, 1 - c), device_id_type=MESH)
               for k in range(n_a)]
        for cp in cps:
            cp.start()
        for cp in cps:
            cp.wait()

    return pl.pallas_call(
        body, name="rs_pair", in_specs=[HBM_SPEC] * n_a, out_specs=[HBM_SPEC] * n_a,
        out_shape=[jax.ShapeDtypeStruct(p.shape[1:], F32) for p in parts],
        scratch_shapes=[pltpu.SemaphoreType.DMA((n_a,)), pltpu.SemaphoreType.DMA((n_a,))],
    )(*parts)


def _rs_chips(sums):
    n_a = len(sums)

    def body(*refs):
        ins, outs = refs[:n_a], refs[n_a:2 * n_a]
        send_sems, recv_sems = refs[2 * n_a:]
        x, y, c = _place()
        cps = []
        for k in range(n_a):
            for r, (fx, fy) in enumerate(CHIP_FLIPS):
                px, py = _flip(x, fx), _flip(y, fy)
                cps.append(pltpu.make_async_remote_copy(
                    src_ref=ins[k].at[2 * px + py], dst_ref=outs[k].at[r], send_sem=send_sems.at[3 * k + r],
                    recv_sem=recv_sems.at[3 * k + r], device_id=(px, py, c), device_id_type=MESH))
        for cp in cps:
            cp.start()
        for cp in cps:
            cp.wait()

    return pl.pallas_call(
        body, name="rs_chips", in_specs=[HBM_SPEC] * n_a, out_specs=[HBM_SPEC] * n_a,
        out_shape=[jax.ShapeDtypeStruct((3,) + s.shape[1:], s.dtype) for s in sums],
        scratch_shapes=[pltpu.SemaphoreType.DMA((3 * n_a,)), pltpu.SemaphoreType.DMA((3 * n_a,))],
    )(*sums)


def _rs_share(bufs):
    n_a = len(bufs)

    def body(*refs):
        ins, outs = refs[:n_a], refs[n_a:2 * n_a]
        send_sems, recv_sems = refs[2 * n_a:]
        x, y, c = _place()
        sends = [pltpu.make_async_remote_copy(src_ref=ins[k].at[c], dst_ref=outs[k].at[c], send_sem=send_sems.at[k],
                                              recv_sem=recv_sems.at[k], device_id=(x, y, 1 - c), device_id_type=MESH)
                 for k in range(n_a)]
        for cp in sends:
            cp.start()
        for k in range(n_a):
            pltpu.make_async_remote_copy(src_ref=ins[k].at[c], dst_ref=outs[k].at[1 - c], send_sem=send_sems.at[k],
                                         recv_sem=recv_sems.at[k], device_id=(x, y, 1 - c),
                                         device_id_type=MESH).wait_recv()
        for cp in sends:
            cp.wait_send()

    return pl.pallas_call(
        body, name="rs_share", in_specs=[HBM_SPEC] * n_a, out_specs=[HBM_SPEC] * n_a,
        out_shape=[jax.ShapeDtypeStruct(s.shape, F32) for s in bufs],
        input_output_aliases={k: k for k in range(n_a)},
        scratch_shapes=[pltpu.SemaphoreType.DMA((n_a,)), pltpu.SemaphoreType.DMA((n_a,))],
    )(*bufs)


def _pair_add(part, recv, core, name):
    _, _, n_h, n_c = part.shape
    th = _pick(n_h, (256, 176, 128))

    def body(sel_ref, p_ref, r_ref, o_ref):
        o_ref[...] = (p_ref[...] + r_ref[...]).astype(BF16)

    grid_spec = pltpu.PrefetchScalarGridSpec(
        num_scalar_prefetch=1, grid=(N_CHIPS, n_h // th),
        in_specs=[pl.BlockSpec((None, None, th, n_c), lambda j, i, sel: (sel[0], j, i, 0)),
                  pl.BlockSpec((None, th, n_c), lambda j, i, sel: (j, i, 0))],
        out_specs=pl.BlockSpec((None, th, n_c), lambda j, i, sel: (j, i, 0)))
    return pl.pallas_call(
        body, name=name, grid_spec=grid_spec, out_shape=jax.ShapeDtypeStruct(recv.shape, BF16),
        compiler_params=_cparams(("parallel", "parallel")),
    )(core.reshape(1), part, recv)


def _chip_add(sums, recv, chip, core, name):
    _, n_h, n_c = sums.shape
    th = _pick(n_h, (256, 176, 128))

    def body(sel_ref, s_ref, r_ref, o_ref):
        total = s_ref[...].astype(F32)
        for r in range(3):
            total = total + r_ref[r].astype(F32)
        o_ref[...] = total

    grid_spec = pltpu.PrefetchScalarGridSpec(
        num_scalar_prefetch=1, grid=(n_h // th,),
        in_specs=[pl.BlockSpec((None, th, n_c), lambda i, sel: (sel[0], i, 0)),
                  pl.BlockSpec((3, th, n_c), lambda i, sel: (0, i, 0))],
        out_specs=pl.BlockSpec((None, th, n_c), lambda i, sel: (sel[1], i, 0)))
    return pl.pallas_call(
        body, name=name, grid_spec=grid_spec, out_shape=jax.ShapeDtypeStruct((2, n_h, n_c), F32),
        compiler_params=_cparams(("parallel",)),
    )(jnp.stack([chip, core]), sums, recv)


def _reduce_scatter(parts, core, chip):
    names = ("w_in", "w_o", "w_up", "w_down")
    from_sibling = _rs_pair(parts)
    pair_sums = [_pair_add(p, r, core, "pair_add_" + n) for p, r, n in zip(parts, from_sibling, names)]
    from_chips = _rs_chips(pair_sums)
    halves = [_chip_add(s, r, chip, core, "chip_add_" + n) for s, r, n in zip(pair_sums, from_chips, names)]
    return [f.reshape(-1, f.shape[-1]) for f in _rs_share(halves)]


SLAB_W = 1024


def _pack(arrays, rows):
    flat = jnp.concatenate([a.reshape(-1).astype(F32) for a in arrays])
    return jnp.pad(flat, (0, rows * SLAB_W - flat.shape[0])).reshape(rows, SLAB_W)


def _unpack(flat, shapes):
    out, off = [], 0
    for s in shapes:
        n = 1
        for d in s:
            n *= d
        out.append(flat[off:off + n].reshape(s))
        off += n
    return out


def _rows_for(arrays_or_shapes):
    n = 0
    for a in arrays_or_shapes:
        s = a if isinstance(a, tuple) else a.shape
        k = 1
        for d in s:
            k *= d
        n += k
    return -(-n // (8 * SLAB_W)) * 8


def _by_cols(a, n_cols):
    n_r = a.shape[0]
    return a.reshape(2, n_r // 2, N_CHIPS, n_cols).transpose(0, 2, 1, 3)


def _by_rows(a):
    n_r = a.shape[0] // N_CHIPS
    return a.reshape(N_CHIPS, 2, n_r // 2, a.shape[1]).transpose(1, 0, 2, 3)


def kernel(x, c, ln0_g, ln0_b, w_ada, b_ada, w_in, dn_conv, dn_a_log, dn_dt_bias, dn_norm_g, gla_w_gate2, gla_b_gate, gla_norm_g, w_o, ln1_g, ln1_b, ffn_w_up, ffn_conv, ffn_conv_b, ffn_w_down, ln2_g, ln2_b, loss_target, m_ln0_g, m_ln0_b, m_w_ada, m_b_ada, m_w_in, m_dn_conv, m_dn_a_log, m_dn_dt_bias, m_dn_norm_g, m_gla_w_gate2, m_gla_b_gate, m_gla_norm_g, m_w_o, m_ln1_g, m_ln1_b, m_ffn_w_up, m_ffn_conv, m_ffn_conv_b, m_ffn_w_down, m_ln2_g, m_ln2_b, v_ln0_g, v_ln0_b, v_w_ada, v_b_ada, v_w_in, v_dn_conv, v_dn_a_log, v_dn_dt_bias, v_dn_norm_g, v_gla_w_gate2, v_gla_b_gate, v_gla_norm_g, v_w_o, v_ln1_g, v_ln1_b, v_ffn_w_up, v_ffn_conv, v_ffn_conv_b, v_ffn_w_down, v_ln2_g, v_ln2_b):
    n_b, t_len, _ = x.shape
    xi, yi, ci = _place()
    chip = (2 * xi + yi).astype(jnp.int32)
    core = ci.astype(jnp.int32)
    me = 2 * chip + core
    n_all = N_DEV * n_b
    ada_cols = w_ada.shape[2]

    sharded_small = [dn_conv[0], gla_w_gate2[0], ffn_conv[0]]
    slab = _pack([c] + sharded_small, _rows_for([c] + sharded_small))
    gathered, _ = _all_gather8(slab, "gather_small")
    flat = gathered.reshape(N_DEV, -1)
    per_dev = [_unpack(flat[d], [c.shape] + [a.shape for a in sharded_small]) for d in range(N_DEV)]
    c_all = jnp.concatenate([per_dev[d][0] for d in range(N_DEV)], axis=0)
    dn_conv_f, gate2_f, ffn_conv_f = (jnp.concatenate([per_dev[2 * j][i] for j in range(N_CHIPS)], axis=1)
                                      for i in (1, 2, 3))

    b_ada_shard = lax.dynamic_slice(b_ada, (0, chip * ada_cols), (1, ada_cols))
    cond_all, mod_cols = _ada_fwd(c_all, w_ada[0], b_ada_shard)
    mod_g, _ = _all_gather8(mod_cols, "gather_mod")
    mod_full = jnp.concatenate([mod_g[2 * j] for j in range(N_CHIPS)], axis=1)
    mod = lax.dynamic_slice(mod_full, (me * n_b, 0), (n_b, 6 * D_MODEL)).reshape(n_b, 6, D_MODEL)

    halves = lambda a: a.astype(BF16).reshape(2, a.shape[0] // 2, a.shape[1])
    g_in, g_o, g_up, g_down = _gather_weights([halves(w_in[0]), halves(w_o[0]), halves(ffn_w_up[0]),
                                               halves(ffn_w_down[0])])
    cols = lambda g: g.reshape(N_CHIPS, -1, g.shape[-1]).transpose(1, 0, 2).reshape(-1, N_CHIPS * g.shape[-1])
    params = {
        "w_in_p": _pad_w_in(cols(g_in)), "w_o": g_o.reshape(-1, D_MODEL), "w_up": cols(g_up),
        "w_down": g_down.reshape(-1, D_MODEL),
        "dn_conv": dn_conv_f, "dn_a_log": dn_a_log[0], "dn_dt_bias": dn_dt_bias[0], "dn_norm_g": dn_norm_g[0],
        "gla_w_gate2": gate2_f, "gla_b_gate": gla_b_gate[0], "gla_norm_g": gla_norm_g[0],
        "ln0_g": ln0_g, "ln0_b": ln0_b, "ln1_g": ln1_g[0], "ln1_b": ln1_b[0], "ln2_g": ln2_g[0], "ln2_b": ln2_b[0],
        "ffn_conv": ffn_conv_f, "ffn_conv_b": ffn_conv_b[0],
    }

    loss_row, grad_x, gp, dmod = _local_step(x.reshape(n_b * t_len, D_MODEL), loss_target.reshape(n_b * t_len, D_MODEL),
                                             mod, params, n_b, t_len)
    loss = lax.psum(loss_row[0, 0], ("x", "y", "c"))

    summed_names = ["ln0_g", "ln0_b", "dn_conv", "dn_a_log", "dn_dt_bias", "dn_norm_g", "gla_w_gate2", "gla_b_gate",
                    "gla_norm_g", "ln1_g", "ln1_b", "ffn_conv", "ffn_conv_b", "ln2_g", "ln2_b"]
    summed_parts = [gp[n] for n in summed_names]
    sum_rows = _rows_for(summed_parts)
    slab = jnp.concatenate([_pack(summed_parts, sum_rows), _pack([dmod], _rows_for([dmod]))], axis=0)
    gathered, total = _all_gather8(slab, "reduce_small")
    small_g = dict(zip(summed_names, _unpack(total.reshape(-1), [a.shape for a in summed_parts])))
    dmod_rows = n_b * 6 * D_MODEL // SLAB_W
    dmod_all = gathered[:, sum_rows:sum_rows + dmod_rows, :].reshape(n_all, 6 * D_MODEL)

    g_b_ada = _col_sum(dmod_all)
    dmod_cols = lax.dynamic_slice(dmod_all, (0, chip * ada_cols), (n_all, ada_cols))
    g_w_ada = _mm(cond_all, dmod_cols, ta=True, name="mm_g_ada")

    g_w_in, g_w_o, g_w_up, g_w_down = _reduce_scatter(
        [_by_cols(_unpad_w_in(gp["w_in_p"]), w_in.shape[2]), _by_rows(gp["w_o"]),
         _by_cols(gp["w_up"], ffn_w_up.shape[2]), _by_rows(gp["w_down"])], core, chip)

    col_block = lambda a: lax.dynamic_slice(a, (0, chip * (a.shape[1] // N_CHIPS)), (a.shape[0], a.shape[1] // N_CHIPS))
    grads = {
        "ln0_g": small_g["ln0_g"], "ln0_b": small_g["ln0_b"], "w_ada": g_w_ada[None], "b_ada": g_b_ada,
        "w_in": g_w_in[None], "dn_conv": col_block(small_g["dn_conv"])[None], "dn_a_log": small_g["dn_a_log"][None],
        "dn_dt_bias": small_g["dn_dt_bias"][None], "dn_norm_g": small_g["dn_norm_g"][None],
        "gla_w_gate2": col_block(small_g["gla_w_gate2"])[None], "gla_b_gate": small_g["gla_b_gate"][None],
        "gla_norm_g": small_g["gla_norm_g"][None], "w_o": g_w_o[None], "ln1_g": small_g["ln1_g"][None],
        "ln1_b": small_g["ln1_b"][None], "ffn_w_up": g_w_up[None], "ffn_conv": col_block(small_g["ffn_conv"])[None],
        "ffn_conv_b": small_g["ffn_conv_b"][None], "ffn_w_down": g_w_down[None], "ln2_g": small_g["ln2_g"][None],
        "ln2_b": small_g["ln2_b"][None],
    }
    names = ["ln0_g", "ln0_b", "w_ada", "b_ada", "w_in", "dn_conv", "dn_a_log", "dn_dt_bias", "dn_norm_g",
             "gla_w_gate2", "gla_b_gate", "gla_norm_g", "w_o", "ln1_g", "ln1_b", "ffn_w_up", "ffn_conv", "ffn_conv_b",
             "ffn_w_down", "ln2_g", "ln2_b"]
    weights = dict(zip(names, [ln0_g, ln0_b, w_ada, b_ada, w_in, dn_conv, dn_a_log, dn_dt_bias, dn_norm_g, gla_w_gate2,
                               gla_b_gate, gla_norm_g, w_o, ln1_g, ln1_b, ffn_w_up, ffn_conv, ffn_conv_b, ffn_w_down,
                               ln2_g, ln2_b]))
    m_in = dict(zip(names, [m_ln0_g, m_ln0_b, m_w_ada, m_b_ada, m_w_in, m_dn_conv, m_dn_a_log, m_dn_dt_bias,
                            m_dn_norm_g, m_gla_w_gate2, m_gla_b_gate, m_gla_norm_g, m_w_o, m_ln1_g, m_ln1_b,
                            m_ffn_w_up, m_ffn_conv, m_ffn_conv_b, m_ffn_w_down, m_ln2_g, m_ln2_b]))
    v_in = dict(zip(names, [v_ln0_g, v_ln0_b, v_w_ada, v_b_ada, v_w_in, v_dn_conv, v_dn_a_log, v_dn_dt_bias,
                            v_dn_norm_g, v_gla_w_gate2, v_gla_b_gate, v_gla_norm_g, v_w_o, v_ln1_g, v_ln1_b,
                            v_ffn_w_up, v_ffn_conv, v_ffn_conv_b, v_ffn_w_down, v_ln2_g, v_ln2_b]))

    big = ("w_ada", "w_in", "w_o", "ffn_w_up", "ffn_w_down")
    delta, new_m, new_v = {}, {}, {}
    for n in big:
        d_n, m_n, v_n = _adamw(weights[n][0], grads[n][0], m_in[n][0], v_in[n][0], "adamw_" + n)
        delta[n], new_m[n], new_v[n] = d_n[None], m_n[None], v_n[None]
    small = [n for n in names if n not in big]
    shapes = [weights[n].shape for n in small]
    rows = _rows_for(shapes)
    d_s, m_s, v_s = _adamw(_pack([weights[n] for n in small], rows), _pack([grads[n] for n in small], rows),
                           _pack([m_in[n] for n in small], rows), _pack([v_in[n] for n in small], rows), "adamw_small")
    for out, slab_out in ((delta, d_s), (new_m, m_s), (new_v, v_s)):
        out.update(zip(small, _unpack(slab_out.reshape(-1), shapes)))

    return (loss, grad_x.reshape(x.shape), *[grads[n] for n in names], *[delta[n] for n in names],
            *[new_m[n] for n in names], *[new_v[n] for n in names])
```

```python
import functools

import jax
import jax.numpy as jnp
from jax import lax
from jax.experimental import pallas as pl
from jax.experimental.pallas import tpu as pltpu

F32 = jnp.float32
BF16 = jnp.bfloat16
MESH = pl.DeviceIdType.MESH

D_MODEL = 1024
HEADS = 4
HEAD_DIM = 128
GLA_KEY = 64
GATE_RANK = 16
CHUNK = 64
D_FF = 2816
ALPHA = 2.0 ** 0.25
EPS = 1e-6
N_CHIPS = 4
N_DEV = 8

PROJ_W = 3840
OFF_GQ, OFF_GK, OFF_GV, OFF_GG, OFF_SMALL, GLA_W = 0, 256, 512, 1024, 1536, 1792
OFF_Z = 2048
W_IN_COLS = 3608


def _qkv_block(j):
    return jnp.where(j < 2, GLA_W // 128 + j, (OFF_Z + 512) // 128 - 2 + j)

ADAM_LR, ADAM_B1, ADAM_B2, ADAM_EPS, ADAM_WD, ADAM_STEP = 0.001, 0.9, 0.999, 1e-08, 0.01, 10

VMEM_LIMIT = 56 * 1024 * 1024
ROW_TILE = 256


def _cparams(sem):
    return pltpu.CompilerParams(dimension_semantics=sem, vmem_limit_bytes=VMEM_LIMIT)


def _pick(n, prefs):
    for p in prefs:
        if n % p == 0:
            return p
    return n


def _mm(a, b, *, ta=False, tb=False, out_slabs=1, out_dtype=F32, name):
    a_slabs = a.shape[0] if a.ndim == 3 else 1
    b_slabs = b.shape[0] if b.ndim == 3 else 1
    assert not (ta and a_slabs > 1)
    a2, b2 = a.shape[-2:], b.shape[-2:]
    if ta:
        k_dim, m_dim = a2
    else:
        m_dim, k_dim = a2[0], a2[1] * a_slabs
    n_dim = b2[0] if tb else b2[1] * b_slabs
    k_slabs = max(a_slabs, b_slabs if tb else 1)
    n_slabs = max(out_slabs, 1 if tb else b_slabs)
    tm = _pick(m_dim, (1024, 1408, 512, 256, 128))
    tn = _pick(n_dim // n_slabs, (1536, 1408, 1280, 1024, 768, 512, 384, 256, 128))
    tk = _pick(k_dim // k_slabs, (1408, 1280, 1024, 512, 256, 128))
    nk, nj = k_dim // tk, n_dim // tn
    nk_a, nk_b, nj_b, nj_o = nk // a_slabs, nk // b_slabs, nj // b_slabs, nj // out_slabs
    dims = (((0 if ta else 1,), (1 if tb else 0,)), ((), ()))

    def body(a_ref, b_ref, o_ref, acc_ref):
        k = pl.program_id(2)

        @pl.when(k == 0)
        def _():
            acc_ref[...] = jnp.zeros_like(acc_ref)

        acc_ref[...] += lax.dot_general(a_ref[...].astype(BF16), b_ref[...].astype(BF16), dims,
                                        preferred_element_type=F32)

        @pl.when(k == nk - 1)
        def _():
            o_ref[...] = acc_ref[...].astype(o_ref.dtype)

    if ta:
        a_spec = pl.BlockSpec((tk, tm), lambda i, j, k: (k, i))
    elif a_slabs > 1:
        a_spec = pl.BlockSpec((None, tm, tk), lambda i, j, k: (k // nk_a, i, k % nk_a))
    else:
        a_spec = pl.BlockSpec((tm, tk), lambda i, j, k: (i, k))
    if tb and b_slabs > 1:
        b_spec = pl.BlockSpec((None, tn, tk), lambda i, j, k: (k // nk_b, j, k % nk_b))
    elif tb:
        b_spec = pl.BlockSpec((tn, tk), lambda i, j, k: (j, k))
    elif b_slabs > 1:
        b_spec = pl.BlockSpec((None, tk, tn), lambda i, j, k: (j // nj_b, k, j % nj_b))
    else:
        b_spec = pl.BlockSpec((tk, tn), lambda i, j, k: (k, j))
    if out_slabs > 1:
        o_spec = pl.BlockSpec((None, tm, tn), lambda i, j, k: (j // nj_o, i, j % nj_o))
        o_shape = (out_slabs, m_dim, n_dim // out_slabs)
    else:
        o_spec, o_shape = pl.BlockSpec((tm, tn), lambda i, j, k: (i, j)), (m_dim, n_dim)
    return pl.pallas_call(
        body, name=name, grid=(m_dim // tm, nj, nk),
        in_specs=[a_spec, b_spec], out_specs=o_spec,
        out_shape=jax.ShapeDtypeStruct(o_shape, out_dtype),
        scratch_shapes=[pltpu.VMEM((tm, tn), F32)],
        compiler_params=_cparams(("parallel", "parallel", "arbitrary")),
    )(a, b)


def _ln(x, g, b):
    mu = jnp.mean(x, -1, keepdims=True)
    xc = x - mu
    var = jnp.mean(xc * xc, -1, keepdims=True)
    return xc * lax.rsqrt(var + EPS) * g + b


def _softplus(x):
    return jnp.maximum(x, 0.0) + jnp.log(1.0 + jnp.exp(-jnp.abs(x)))


def _silu(x):
    return x * jax.nn.sigmoid(x)


def _dsilu(x):
    s = jax.nn.sigmoid(x)
    return s * (1.0 + x * (1.0 - s))


def _f_ln0(x, g, b, sc, sh):
    x0 = _ln(x, g, b)
    return x0, x0 * (1.0 + sc) + sh


def _f_ln1(x0, y, gt, g, b, sc, sh):
    x1 = _ln(ALPHA * x0 + (1.0 + gt) * y, g, b)
    return x1, x1 * (1.0 + sc) + sh


def _f_ln2_loss(x1, y2, gt, g, b, tgt):
    x2 = _ln(ALPHA * x1 + (1.0 + gt) * y2, g, b)
    err = x2 - tgt
    per_row = jnp.sum(err * err, -1, keepdims=True) * (0.5 / D_MODEL)
    return jnp.sum(per_row, 0, keepdims=True)


def _row_specs(t_len):
    nt = t_len // ROW_TILE
    row = pl.BlockSpec((ROW_TILE, D_MODEL), lambda b, i: (b * nt + i, 0))
    vec = pl.BlockSpec((1, D_MODEL), lambda b, i: (0, 0))
    mod = pl.BlockSpec((None, 6, D_MODEL), lambda b, i: (b, 0, 0))
    return nt, row, vec, mod


def _first_step():
    return jnp.logical_and(pl.program_id(0) == 0, pl.program_id(1) == 0)


def _acc(ref, val, first, at=(Ellipsis,)):
    @pl.when(first)
    def _():
        ref[at] = val

    @pl.when(jnp.logical_not(first))
    def _():
        ref[at] += val


def _acc_rows(ref, rows, first):
    for i, r in enumerate(rows):
        _acc(ref, r, first, at=(slice(i, i + 1), slice(None)))


def _ln0_fwd(x, g, b, mod, n_b, t_len):
    nt, row, vec, mods = _row_specs(t_len)

    def body(x_ref, g_ref, b_ref, mod_ref, x0_ref, h_ref):
        x0, h = _f_ln0(x_ref[...], g_ref[...], b_ref[...], mod_ref[1:2, :], mod_ref[0:1, :])
        x0_ref[...] = x0
        h_ref[...] = h.astype(BF16)

    return pl.pallas_call(
        body, name="ln0_fwd", grid=(n_b, nt), in_specs=[row, vec, vec, mods], out_specs=[row, row],
        out_shape=[jax.ShapeDtypeStruct(x.shape, F32), jax.ShapeDtypeStruct(x.shape, BF16)],
        compiler_params=_cparams(("parallel", "parallel")),
    )(x, g, b, mod)


def _ln0_bwd(x, g, b, mod, dx0, dh, n_b, t_len):
    nt, row, vec, mods = _row_specs(t_len)
    dmod_spec = pl.BlockSpec((None, 2, D_MODEL), lambda bb, i: (bb, 0, 0))

    def body(x_ref, g_ref, b_ref, mod_ref, dx0_ref, dh_ref, dx_ref, dg_ref, db_ref, dmod_ref):
        _, pull = jax.vjp(_f_ln0, x_ref[...], g_ref[...], b_ref[...], mod_ref[1:2, :], mod_ref[0:1, :])
        dx, dg, db, dsc, dsh = pull((dx0_ref[...], dh_ref[...]))
        dx_ref[...] = dx
        _acc(dg_ref, dg, _first_step())
        _acc(db_ref, db, _first_step())
        _acc_rows(dmod_ref, [dsh, dsc], pl.program_id(1) == 0)

    return pl.pallas_call(
        body, name="ln0_bwd", grid=(n_b, nt), in_specs=[row, vec, vec, mods, row, row],
        out_specs=[row, vec, vec, dmod_spec],
        out_shape=[jax.ShapeDtypeStruct(x.shape, F32), jax.ShapeDtypeStruct((1, D_MODEL), F32),
                   jax.ShapeDtypeStruct((1, D_MODEL), F32), jax.ShapeDtypeStruct((n_b, 2, D_MODEL), F32)],
        compiler_params=_cparams(("arbitrary", "arbitrary")),
    )(x, g, b, mod, dx0, dh)


def _ln1_fwd(x0, y, g, b, mod, n_b, t_len):
    nt, row, vec, mods = _row_specs(t_len)

    def body(x0_ref, y_ref, g_ref, b_ref, mod_ref, x1_ref, h_ref):
        x1, h = _f_ln1(x0_ref[...], y_ref[...], mod_ref[2:3, :], g_ref[...], b_ref[...],
                       mod_ref[4:5, :], mod_ref[3:4, :])
        x1_ref[...] = x1
        h_ref[...] = h.astype(BF16)

    return pl.pallas_call(
        body, name="ln1_fwd", grid=(n_b, nt), in_specs=[row, row, vec, vec, mods], out_specs=[row, row],
        out_shape=[jax.ShapeDtypeStruct(x0.shape, F32), jax.ShapeDtypeStruct(x0.shape, BF16)],
        compiler_params=_cparams(("parallel", "parallel")),
    )(x0, y, g, b, mod)


def _ln1_bwd(x0, y, g, b, mod, dx1, dh, n_b, t_len):
    nt, row, vec, mods = _row_specs(t_len)
    dmod_spec = pl.BlockSpec((None, 3, D_MODEL), lambda bb, i: (bb, 0, 0))

    def body(x0_ref, y_ref, g_ref, b_ref, mod_ref, dx1_ref, dh_ref, dx0_ref, dy_ref, dg_ref, db_ref, dmod_ref):
        _, pull = jax.vjp(_f_ln1, x0_ref[...], y_ref[...], mod_ref[2:3, :], g_ref[...], b_ref[...],
                          mod_ref[4:5, :], mod_ref[3:4, :])
        dx0, dy, dgt, dg, db, dsc, dsh = pull((dx1_ref[...], dh_ref[...]))
        dx0_ref[...] = dx0
        dy_ref[...] = dy.astype(BF16)
        _acc(dg_ref, dg, _first_step())
        _acc(db_ref, db, _first_step())
        _acc_rows(dmod_ref, [dgt, dsh, dsc], pl.program_id(1) == 0)

    return pl.pallas_call(
        body, name="ln1_bwd", grid=(n_b, nt), in_specs=[row, row, vec, vec, mods, row, row],
        out_specs=[row, row, vec, vec, dmod_spec],
        out_shape=[jax.ShapeDtypeStruct(x0.shape, F32), jax.ShapeDtypeStruct(x0.shape, BF16),
                   jax.ShapeDtypeStruct((1, D_MODEL), F32), jax.ShapeDtypeStruct((1, D_MODEL), F32),
                   jax.ShapeDtypeStruct((n_b, 3, D_MODEL), F32)],
        compiler_params=_cparams(("arbitrary", "arbitrary")),
    )(x0, y, g, b, mod, dx1, dh)


def _ln2_loss_bwd(x1, y2, g, b, mod, tgt, n_b, t_len):
    nt, row, vec, mods = _row_specs(t_len)
    one = pl.BlockSpec((1, 128), lambda bb, i: (0, 0))
    dmod_spec = pl.BlockSpec((None, 1, D_MODEL), lambda bb, i: (bb, 0, 0))

    def body(x1_ref, y2_ref, g_ref, b_ref, mod_ref, t_ref, loss_ref, dx1_ref, dy2_ref, dg_ref, db_ref, dgt_ref):
        loss, pull = jax.vjp(functools.partial(_f_ln2_loss, tgt=t_ref[...]), x1_ref[...], y2_ref[...],
                             mod_ref[5:6, :], g_ref[...], b_ref[...])
        dx1, dy2, dgt, dg, db = pull(jnp.ones((1, 1), F32))
        dx1_ref[...] = dx1
        dy2_ref[...] = dy2.astype(BF16)
        _acc(loss_ref, jnp.broadcast_to(loss, (1, 128)), _first_step())
        _acc(dg_ref, dg, _first_step())
        _acc(db_ref, db, _first_step())
        _acc(dgt_ref, dgt, pl.program_id(1) == 0)

    return pl.pallas_call(
        body, name="ln2_loss_bwd", grid=(n_b, nt), in_specs=[row, row, vec, vec, mods, row],
        out_specs=[one, row, row, vec, vec, dmod_spec],
        out_shape=[jax.ShapeDtypeStruct((1, 128), F32), jax.ShapeDtypeStruct(x1.shape, F32),
                   jax.ShapeDtypeStruct(x1.shape, BF16), jax.ShapeDtypeStruct((1, D_MODEL), F32),
                   jax.ShapeDtypeStruct((1, D_MODEL), F32), jax.ShapeDtypeStruct((n_b, 1, D_MODEL), F32)],
        compiler_params=_cparams(("arbitrary", "arbitrary")),
    )(x1, y2, g, b, mod, tgt)


def _shift_down(x, s):
    if s == 0:
        return x
    rows = lax.broadcasted_iota(jnp.int32, x.shape, 0)
    return jnp.where(rows >= s, pltpu.roll(x, s, 0), 0.0)


def _shift_up(x, s):
    if s == 0:
        return x
    t_len = x.shape[0]
    rows = lax.broadcasted_iota(jnp.int32, x.shape, 0)
    return jnp.where(rows < t_len - s, pltpu.roll(x, t_len - s, 0), 0.0)


def _conv(x, w):
    k_w = w.shape[0]
    out = w[k_w - 1:k_w, :] * x
    for k in range(k_w - 1):
        out = out + w[k:k + 1, :] * _shift_down(x, k_w - 1 - k)
    return out


def _conv_bwd(x, w, du):
    k_w = w.shape[0]
    dx = w[k_w - 1:k_w, :] * du
    dws = []
    for k in range(k_w):
        s = k_w - 1 - k
        if s:
            dx = dx + w[k:k + 1, :] * _shift_up(du, s)
        dws.append(jnp.sum(du * _shift_down(x, s), 0, keepdims=True))
    return dx, dws


def _dn_pre_fwd(proj, conv_w, n_b, t_len):
    n_ct = 3 * HEADS
    k_w = conv_w.shape[0]

    def body(x_ref, w_ref, o_ref):
        o_ref[...] = _silu(_conv(x_ref[...], w_ref[...]))

    return pl.pallas_call(
        body, name="dn_pre_fwd", grid=(n_ct, n_b),
        in_specs=[pl.BlockSpec((t_len, 128), lambda j, b: (b, _qkv_block(j))),
                  pl.BlockSpec((k_w, 128), lambda j, b: (0, j))],
        out_specs=pl.BlockSpec((t_len, 128), lambda j, b: (b, j)),
        out_shape=jax.ShapeDtypeStruct((n_b * t_len, n_ct * 128), F32),
        compiler_params=_cparams(("parallel", "parallel")),
    )(proj, conv_w)


def _dn_pre_bwd(proj, conv_w, dqkv, d_proj, n_b, t_len):
    n_ct = 3 * HEADS
    k_w = conv_w.shape[0]

    def body(x_ref, w_ref, d_ref, _, dx_ref, dw_ref):
        x, w = x_ref[...], w_ref[...]
        du = d_ref[...] * _dsilu(_conv(x, w))
        dx, dw = _conv_bwd(x, w, du)
        dx_ref[...] = dx.astype(BF16)
        _acc_rows(dw_ref, dw, pl.program_id(1) == 0)

    return pl.pallas_call(
        body, name="dn_pre_bwd", grid=(n_ct, n_b),
        in_specs=[pl.BlockSpec((t_len, 128), lambda j, b: (b, _qkv_block(j))),
                  pl.BlockSpec((k_w, 128), lambda j, b: (0, j)),
                  pl.BlockSpec((t_len, 128), lambda j, b: (b, j)), pl.BlockSpec(memory_space=pl.ANY)],
        out_specs=[pl.BlockSpec((t_len, 128), lambda j, b: (b, _qkv_block(j))),
                   pl.BlockSpec((k_w, 128), lambda j, b: (0, j))],
        out_shape=[jax.ShapeDtypeStruct(d_proj.shape, BF16), jax.ShapeDtypeStruct((k_w, n_ct * 128), F32)],
        input_output_aliases={3: 0},
        compiler_params=_cparams(("parallel", "arbitrary")),
    )(proj, conv_w, dqkv, d_proj)


FFN_TC = 256
FFN_NT = D_FF // FFN_TC


def _ffn_specs(t_len):
    blk = lambda off: pl.BlockSpec((t_len, FFN_TC), lambda j, b: (b, j + off))
    wblk = lambda off: pl.BlockSpec((3, FFN_TC), lambda j, b: (0, j + off))
    bblk = lambda off: pl.BlockSpec((1, FFN_TC), lambda j, b: (0, j + off))
    return [blk(0), blk(FFN_NT), wblk(0), wblk(FFN_NT), bblk(0), bblk(FFN_NT)]


def _ffn_act_fwd(up, conv_w, conv_b, n_b, t_len):
    def body(g_ref, v_ref, wg_ref, wv_ref, bg_ref, bv_ref, o_ref):
        ug = _conv(g_ref[...], wg_ref[...]) + bg_ref[...]
        uv = _conv(v_ref[...], wv_ref[...]) + bv_ref[...]
        o_ref[...] = (_silu(ug) * uv).astype(BF16)

    return pl.pallas_call(
        body, name="ffn_act_fwd", grid=(FFN_NT, n_b), in_specs=_ffn_specs(t_len),
        out_specs=pl.BlockSpec((t_len, FFN_TC), lambda j, b: (b, j)),
        out_shape=jax.ShapeDtypeStruct((n_b * t_len, D_FF), BF16),
        compiler_params=_cparams(("parallel", "parallel")),
    )(up, up, conv_w, conv_w, conv_b, conv_b)


def _ffn_act_bwd(up, conv_w, conv_b, da, n_b, t_len):
    def body(g_ref, v_ref, wg_ref, wv_ref, bg_ref, bv_ref, da_ref, dup_ref, dw_ref, db_ref):
        first = pl.program_id(1) == 0
        xg, xv, wg, wv = g_ref[...], v_ref[...], wg_ref[...], wv_ref[...]
        ug = _conv(xg, wg) + bg_ref[...]
        uv = _conv(xv, wv) + bv_ref[...]
        d_act = da_ref[...]
        sig = jax.nn.sigmoid(ug)
        d_v = d_act * (ug * sig)
        d_g = d_act * uv * (sig * (1.0 + ug * (1.0 - sig)))
        for slab, (x, w, du) in enumerate(((xg, wg, d_g), (xv, wv, d_v))):
            dx, dw = _conv_bwd(x, w, du)
            dup_ref[slab] = dx.astype(BF16)
            for k, dw_k in enumerate(dw):
                _acc(dw_ref, dw_k, first, at=(slab, slice(k, k + 1), slice(None)))
            _acc(db_ref, jnp.sum(du, 0, keepdims=True), first, at=(slab, slice(None), slice(None)))

    return pl.pallas_call(
        body, name="ffn_act_bwd", grid=(FFN_NT, n_b),
        in_specs=_ffn_specs(t_len) + [pl.BlockSpec((t_len, FFN_TC), lambda j, b: (b, j))],
        out_specs=[pl.BlockSpec((2, t_len, FFN_TC), lambda j, b: (0, b, j)),
                   pl.BlockSpec((2, 3, FFN_TC), lambda j, b: (0, 0, j)),
                   pl.BlockSpec((2, 1, FFN_TC), lambda j, b: (0, 0, j))],
        out_shape=[jax.ShapeDtypeStruct((2, n_b * t_len, D_FF), BF16),
                   jax.ShapeDtypeStruct((2, 3, D_FF), F32), jax.ShapeDtypeStruct((2, 1, D_FF), F32)],
        compiler_params=_cparams(("parallel", "arbitrary")),
    )(up, up, conv_w, conv_w, conv_b, conv_b, da)


NN = (((2,), (1,)), ((0,), (0,)))
NT = (((2,), (2,)), ((0,), (0,)))
TN = (((1,), (1,)), ((0,), (0,)))


def _iota3(shape, axis):
    return lax.broadcasted_iota(jnp.int32, shape, axis)


def _dg(a, b, dims):
    return lax.dot_general(a, b, dims, preferred_element_type=F32)


def _dot(a, b):
    return _dg(a, b, NN)


def _dot_nt(a, b):
    return _dg(a, b, NT)


def _dot_tn(a, b):
    return _dg(a, b, TN)


def _split(a):
    hi = a.astype(BF16)
    return hi, (a - hi.astype(F32)).astype(BF16)


def _dg3(a, b, dims):
    ah, al = _split(a)
    bh, bl = _split(b)
    return _dg(ah, bh, dims) + (_dg(ah, bl, dims) + _dg(al, bh, dims))


@jax.custom_vjp
def _dot3(a, b):
    return _dg3(a, b, NN)


def _dot3_fwd(a, b):
    return _dg3(a, b, NN), (a, b)


def _dot3_bwd(res, g):
    a, b = res
    return _dg3(g, b, NT), _dg3(a, g, TN)


_dot3.defvjp(_dot3_fwd, _dot3_bwd)


def _lower_ones(g_n, n):
    shape = (g_n, n, n)
    return jnp.where(_iota3(shape, 1) >= _iota3(shape, 2), 1.0, 0.0).astype(BF16)


@jax.custom_vjp
def _chunk_cumsum(x):
    hi, lo = _split(x)
    tri = _lower_ones(x.shape[0], x.shape[1])
    return _dg(tri, hi, NN) + _dg(tri, lo, NN)


def _chunk_cumsum_fwd(x):
    return _chunk_cumsum(x), None


def _chunk_cumsum_bwd(_, g):
    hi, lo = _split(g)
    tri = _lower_ones(g.shape[0], g.shape[1])
    return (_dg(tri, hi, TN) + _dg(tri, lo, TN),)


_chunk_cumsum.defvjp(_chunk_cumsum_fwd, _chunk_cumsum_bwd)


@jax.custom_vjp
def _unit_lower_inv(m):
    n = m.shape[1]
    p = -m
    a = jnp.where(_iota3(m.shape, 1) == _iota3(m.shape, 2), 1.0, 0.0) + p
    span = 2
    while span < n:
        p = _dg3(p, p, NN)
        a = a + _dg3(a, p, NN)
        span *= 2
    return a


def _unit_lower_inv_fwd(m):
    a = _unit_lower_inv(m)
    return a, a


def _unit_lower_inv_bwd(a, da):
    return (-_dg3(a, _dg3(da, a, NT), TN),)


_unit_lower_inv.defvjp(_unit_lower_inv_fwd, _unit_lower_inv_bwd)


def _rms_gate(o, gn, gate):
    return o * lax.rsqrt(jnp.mean(o * o, -1, keepdims=True) + EPS) * gn * _silu(gate)


def _dn_chains(q, k, v, z, small, s_in, a_log, dt_bias, gn):
    g_n, c_len = q.shape[0], q.shape[1]
    sq = (g_n, c_len, c_len)
    row, col = _iota3(sq, 1), _iota3(sq, 2)
    causal, strict, eye = row >= col, row > col, row == col
    qn = q * lax.rsqrt(jnp.sum(q * q, -1, keepdims=True) + EPS) * (HEAD_DIM ** -0.5)
    kn = k * lax.rsqrt(jnp.sum(k * k, -1, keepdims=True) + EPS)
    lane = _iota3(small.shape, 2)
    head = jnp.bitwise_and(_iota3(small.shape, 0), HEADS - 1)
    la_all = -jnp.exp(a_log) * _softplus(small + dt_bias)
    la_c = jnp.sum(jnp.where(lane == head, la_all, 0.0), 2, keepdims=True)
    beta = jnp.sum(jnp.where(lane == head + HEADS, jax.nn.sigmoid(small), 0.0), 2, keepdims=True)
    la_b = jnp.broadcast_to(la_c, sq)
    la_r = jnp.sum(jnp.where(eye, la_b, 0.0), 1, keepdims=True)
    g_c = jnp.sum(jnp.where(causal, jnp.broadcast_to(la_r, sq), 0.0), 2, keepdims=True)
    g_r = jnp.sum(jnp.where(row <= col, la_b, 0.0), 1, keepdims=True)
    g_last = jnp.sum(la_c, 1, keepdims=True)
    decay = jnp.exp(jnp.where(causal, g_c - g_r, -1e30))
    e_g = jnp.exp(g_c)
    kb = kn * beta
    m_low = jnp.where(strict, _dot_nt(kb, kn) * decay, 0.0)
    a_inv = _unit_lower_inv(m_low)
    u = _dot3(a_inv, v * beta)
    w = _dot3(a_inv, kb * e_g)
    attn = _dot_nt(qn, kn) * decay
    v_new = u - _dot(w, s_in)
    o = _dot(qn * e_g, s_in) + _dot(attn, v_new)
    s_out = s_in * jnp.exp(g_last) + _dot_tn(kn * jnp.exp(g_last - g_c), v_new)
    return _rms_gate(o, gn, z), s_out


def _gla_chains(q, k, v, gate, small, s_in, w2, b2, gn):
    g_n, c_len = q.shape[0], q.shape[1]
    sq, kk = (g_n, c_len, c_len), (g_n, GLA_KEY, GLA_KEY)
    causal = _iota3(sq, 1) >= _iota3(sq, 2)
    la = -_softplus(-(_dot(small, w2) + b2)) * (1.0 / 16.0)
    b = _chunk_cumsum(la)
    b_last = jnp.sum(jnp.where(_iota3(b.shape, 1) == c_len - 1, b, 0.0), 1, keepdims=True)
    q_dec = q * (GLA_KEY ** -0.5) * jnp.exp(b)
    attn = jnp.where(causal, _dot_nt(q_dec, k * jnp.exp(-b)), 0.0)
    o = _dot(q_dec, s_in) + _dot(attn, v)
    g_row = jnp.exp(b_last)
    g_col = jnp.sum(jnp.where(_iota3(kk, 1) == _iota3(kk, 2), jnp.broadcast_to(g_row, kk), 0.0), 2, keepdims=True)
    s_out = s_in * g_col + _dot_tn(k * jnp.exp(b_last - b), v)
    return _rms_gate(o, gn, gate), s_out


def _chunk_spec(n_b, width, col_block, n_c, reverse=False):
    if reverse:
        return pl.BlockSpec((n_b, CHUNK, width), lambda n: (0, n_c - 1 - n, col_block))
    return pl.BlockSpec((n_b, CHUNK, width), lambda n: (0, n, col_block))


def _hist_spec(n_b, d_k, n_c, reverse=False):
    if reverse:
        return pl.BlockSpec((None, n_b * HEADS, d_k, HEAD_DIM), lambda n: (n_c - 1 - n, 0, 0, 0))
    return pl.BlockSpec((None, n_b * HEADS, d_k, HEAD_DIM), lambda n: (n, 0, 0, 0))


def _stack_chains(ref, n_b, slices):
    return jnp.stack([ref[b, :, sl] for b in range(n_b) for sl in slices], axis=0)


def _per_chain(ref, n_b):
    return jnp.stack([ref[b] for b in range(n_b) for _ in range(HEADS)], axis=0)


def _unstack_chains(ref, val, n_b, slices, offset=0):
    for b in range(n_b):
        for h, sl in enumerate(slices):
            ref[b, :, slice(offset + sl.start, offset + sl.stop)] = val[b * HEADS + h].astype(ref.dtype)


def _gate_weights(w2_ref, b2_ref, n_b):
    w2 = jnp.stack([w2_ref[:, ks] for _ in range(n_b) for ks in GLA_KSL], axis=0)
    b2 = jnp.stack([b2_ref[:, ks] for _ in range(n_b) for ks in GLA_KSL], axis=0)
    return w2, b2


def _sum_heads(val, n_b):
    return [sum(val[b * HEADS + h] for h in range(HEADS)) for b in range(n_b)]


def _const_spec(shape):
    return pl.BlockSpec(shape, lambda n: (0,) * len(shape))


DN_SL = [slice(h * HEAD_DIM, (h + 1) * HEAD_DIM) for h in range(HEADS)]
GLA_KSL = [slice(h * GLA_KEY, (h + 1) * GLA_KEY) for h in range(HEADS)]


def _dn_scan_fwd(qkv, proj, a_log, dt_bias, gn, n_b, t_len):
    n_c = t_len // CHUNK
    spec = functools.partial(_chunk_spec, n_b, n_c=n_c)

    def body(q_ref, k_ref, v_ref, z_ref, sm_ref, al_ref, dt_ref, gn_ref, o_ref, hist_ref, s_ref):
        @pl.when(pl.program_id(0) == 0)
        def _():
            s_ref[...] = jnp.zeros_like(s_ref)

        s_in = s_ref[...]
        hist_ref[...] = s_in
        og, s_out = _dn_chains(*(_stack_chains(r, n_b, DN_SL) for r in (q_ref, k_ref, v_ref, z_ref)),
                               _per_chain(sm_ref, n_b), s_in, al_ref[...], dt_ref[...], gn_ref[...])
        _unstack_chains(o_ref, og, n_b, DN_SL)
        s_ref[...] = s_out

    qkv3, proj3 = qkv.reshape(n_b, t_len, -1), proj.reshape(n_b, t_len, -1)
    o, hist = pl.pallas_call(
        body, name="dn_scan_fwd", grid=(n_c,),
        in_specs=[spec(512, 0), spec(512, 1), spec(512, 2), spec(512, OFF_Z // 512), spec(128, OFF_SMALL // 128),
                  _const_spec((1, 128)), _const_spec((1, 128)), _const_spec((1, 128))],
        out_specs=[spec(512, 0), _hist_spec(n_b, HEAD_DIM, n_c)],
        out_shape=[jax.ShapeDtypeStruct((n_b, t_len, 2 * 512), BF16),
                   jax.ShapeDtypeStruct((n_c, n_b * HEADS, HEAD_DIM, HEAD_DIM), F32)],
        scratch_shapes=[pltpu.VMEM((n_b * HEADS, HEAD_DIM, HEAD_DIM), F32)],
        compiler_params=_cparams(("arbitrary",)),
    )(qkv3, qkv3, qkv3, proj3, proj3, a_log, dt_bias, gn)
    return o, hist


def _dn_scan_bwd(qkv, proj, a_log, dt_bias, gn, hist, d_o, n_b, t_len):
    n_c = t_len // CHUNK
    rev = functools.partial(_chunk_spec, n_b, n_c=n_c, reverse=True)

    def body(q_ref, k_ref, v_ref, z_ref, sm_ref, al_ref, dt_ref, gn_ref, hist_ref, do_ref,
             dqkv_ref, dz_ref, dsm_ref, dal_ref, ddt_ref, dgn_ref, ds_ref):
        first = pl.program_id(0) == 0

        @pl.when(first)
        def _():
            ds_ref[...] = jnp.zeros_like(ds_ref)

        _, pull = jax.vjp(_dn_chains, *(_stack_chains(r, n_b, DN_SL) for r in (q_ref, k_ref, v_ref, z_ref)),
                          _per_chain(sm_ref, n_b), hist_ref[...], al_ref[...], dt_ref[...], gn_ref[...])
        dq, dk, dv, dz, dsm, ds_in, dal, ddt, dgn = pull((_stack_chains(do_ref, n_b, DN_SL), ds_ref[...]))
        _unstack_chains(dqkv_ref, dq, n_b, DN_SL)
        _unstack_chains(dqkv_ref, dk, n_b, DN_SL, offset=512)
        _unstack_chains(dqkv_ref, dv, n_b, DN_SL, offset=1024)
        _unstack_chains(dz_ref, dz, n_b, DN_SL)
        ds_ref[...] = ds_in
        for b, dsm_b in enumerate(_sum_heads(dsm, n_b)):
            dsm_ref[b] = dsm_b
        _acc(dal_ref, dal, first)
        _acc(ddt_ref, ddt, first)
        _acc(dgn_ref, dgn, first)

    qkv3, proj3, do3 = (a.reshape(n_b, t_len, -1) for a in (qkv, proj, d_o))
    vec = jax.ShapeDtypeStruct((1, 128), F32)
    dqkv, d_proj, dsm, dal, ddt, dgn = pl.pallas_call(
        body, name="dn_scan_bwd", grid=(n_c,),
        in_specs=[rev(512, 0), rev(512, 1), rev(512, 2), rev(512, OFF_Z // 512), rev(128, OFF_SMALL // 128),
                  _const_spec((1, 128)), _const_spec((1, 128)), _const_spec((1, 128)),
                  _hist_spec(n_b, HEAD_DIM, n_c, reverse=True), rev(512, 0)],
        out_specs=[rev(1536, 0), rev(512, OFF_Z // 512), rev(128, 0),
                   _const_spec((1, 128)), _const_spec((1, 128)), _const_spec((1, 128))],
        out_shape=[jax.ShapeDtypeStruct((n_b, t_len, 1536), F32), jax.ShapeDtypeStruct((n_b, t_len, PROJ_W), BF16),
                   jax.ShapeDtypeStruct((n_b, t_len, 128), F32), vec, vec, vec],
        scratch_shapes=[pltpu.VMEM((n_b * HEADS, HEAD_DIM, HEAD_DIM), F32)],
        compiler_params=_cparams(("arbitrary",)),
    )(qkv3, qkv3, qkv3, proj3, proj3, a_log, dt_bias, gn, hist, do3)
    return dqkv.reshape(n_b * t_len, 1536), d_proj, dsm, dal, ddt, dgn


def _gla_scan_fwd(proj, w2, b2, gn, o_mix, n_b, t_len):
    n_c = t_len // CHUNK
    spec = functools.partial(_chunk_spec, n_b, n_c=n_c)

    def body(q_ref, k_ref, v_ref, g_ref, sm_ref, w2_ref, b2_ref, gn_ref, _, o_ref, hist_ref, s_ref):
        @pl.when(pl.program_id(0) == 0)
        def _():
            s_ref[...] = jnp.zeros_like(s_ref)

        s_in = s_ref[...]
        hist_ref[...] = s_in
        og, s_out = _gla_chains(_stack_chains(q_ref, n_b, GLA_KSL), _stack_chains(k_ref, n_b, GLA_KSL),
                                _stack_chains(v_ref, n_b, DN_SL), _stack_chains(g_ref, n_b, DN_SL),
                                _per_chain(sm_ref, n_b), s_in, *_gate_weights(w2_ref, b2_ref, n_b), gn_ref[...])
        _unstack_chains(o_ref, og, n_b, DN_SL)
        s_ref[...] = s_out

    proj3 = proj.reshape(n_b, t_len, -1)
    o, hist = pl.pallas_call(
        body, name="gla_scan_fwd", grid=(n_c,),
        in_specs=[spec(256, OFF_GQ // 256), spec(256, OFF_GK // 256), spec(512, OFF_GV // 512),
                  spec(512, OFF_GG // 512), spec(128, OFF_SMALL // 128),
                  _const_spec((128, 256)), _const_spec((1, 256)), _const_spec((1, 128)),
                  pl.BlockSpec(memory_space=pl.ANY)],
        out_specs=[spec(512, 1), _hist_spec(n_b, GLA_KEY, n_c)],
        out_shape=[jax.ShapeDtypeStruct(o_mix.shape, BF16),
                   jax.ShapeDtypeStruct((n_c, n_b * HEADS, GLA_KEY, HEAD_DIM), F32)],
        input_output_aliases={8: 0},
        scratch_shapes=[pltpu.VMEM((n_b * HEADS, GLA_KEY, HEAD_DIM), F32)],
        compiler_params=_cparams(("arbitrary",)),
    )(proj3, proj3, proj3, proj3, proj3, w2, b2, gn, o_mix)
    return o.reshape(n_b * t_len, 2 * 512), hist


def _gla_scan_bwd(proj, w2, b2, gn, hist, d_o, dsm_dn, d_proj, n_b, t_len):
    n_c = t_len // CHUNK
    rev = functools.partial(_chunk_spec, n_b, n_c=n_c, reverse=True)

    def body(q_ref, k_ref, v_ref, g_ref, sm_ref, w2_ref, b2_ref, gn_ref, hist_ref, do_ref, dsm_dn_ref, _,
             dp_ref, dw2_ref, db2_ref, dgn_ref, ds_ref):
        first = pl.program_id(0) == 0

        @pl.when(first)
        def _():
            ds_ref[...] = jnp.zeros_like(ds_ref)

        _, pull = jax.vjp(_gla_chains, _stack_chains(q_ref, n_b, GLA_KSL), _stack_chains(k_ref, n_b, GLA_KSL),
                          _stack_chains(v_ref, n_b, DN_SL), _stack_chains(g_ref, n_b, DN_SL),
                          _per_chain(sm_ref, n_b), hist_ref[...], *_gate_weights(w2_ref, b2_ref, n_b), gn_ref[...])
        dq, dk, dv, dg, dsm, ds_in, dw2, db2, dgn = pull((_stack_chains(do_ref, n_b, DN_SL), ds_ref[...]))
        _unstack_chains(dp_ref, dq, n_b, GLA_KSL, offset=OFF_GQ)
        _unstack_chains(dp_ref, dk, n_b, GLA_KSL, offset=OFF_GK)
        _unstack_chains(dp_ref, dv, n_b, DN_SL, offset=OFF_GV)
        _unstack_chains(dp_ref, dg, n_b, DN_SL, offset=OFF_GG)
        ds_ref[...] = ds_in
        for b, dsm_b in enumerate(_sum_heads(dsm, n_b)):
            dp_ref[b, :, OFF_SMALL:OFF_SMALL + 128] = (dsm_b + dsm_dn_ref[b]).astype(BF16)
            dp_ref[b, :, OFF_SMALL + 128:GLA_W] = jnp.zeros((CHUNK, GLA_W - OFF_SMALL - 128), BF16)
        for h, ks in enumerate(GLA_KSL):
            _acc(dw2_ref, sum(dw2[b * HEADS + h] for b in range(n_b)), first, at=(slice(None), ks))
            _acc(db2_ref, sum(db2[b * HEADS + h] for b in range(n_b)), first, at=(slice(None), ks))
        _acc(dgn_ref, dgn, first)

    proj3, do3 = proj.reshape(n_b, t_len, -1), d_o.reshape(n_b, t_len, -1)
    return pl.pallas_call(
        body, name="gla_scan_bwd", grid=(n_c,),
        in_specs=[rev(256, OFF_GQ // 256), rev(256, OFF_GK // 256), rev(512, OFF_GV // 512), rev(512, OFF_GG // 512),
                  rev(128, OFF_SMALL // 128),
                  _const_spec((128, 256)), _const_spec((1, 256)), _const_spec((1, 128)),
                  _hist_spec(n_b, GLA_KEY, n_c, reverse=True), rev(512, 1), rev(128, 0),
                  pl.BlockSpec(memory_space=pl.ANY)],
        out_specs=[rev(GLA_W, 0), _const_spec((128, 256)), _const_spec((1, 256)), _const_spec((1, 128))],
        out_shape=[jax.ShapeDtypeStruct(d_proj.shape, BF16), jax.ShapeDtypeStruct((128, 256), F32),
                   jax.ShapeDtypeStruct((1, 256), F32), jax.ShapeDtypeStruct((1, 128), F32)],
        input_output_aliases={11: 0},
        scratch_shapes=[pltpu.VMEM((n_b * HEADS, GLA_KEY, HEAD_DIM), F32)],
        compiler_params=_cparams(("arbitrary",)),
    )(proj3, proj3, proj3, proj3, proj3, w2, b2, gn, hist, do3, dsm_dn, d_proj)


def _pad_w_in(w_in):
    zeros = jnp.zeros((w_in.shape[0], GLA_W - OFF_SMALL - 8 - GATE_RANK), w_in.dtype)
    return jnp.concatenate([w_in[:, 2056:3592], w_in[:, 2048:2056], w_in[:, 3592:3608], zeros,
                            w_in[:, 0:256], w_in[:, 1536:2048], w_in[:, 256:1536]], axis=1)


def _unpad_w_in(g):
    return jnp.concatenate([g[:, GLA_W:OFF_Z], g[:, OFF_Z + 512:PROJ_W], g[:, OFF_Z:OFF_Z + 512],
                            g[:, OFF_SMALL:OFF_SMALL + 8], g[:, 0:OFF_SMALL],
                            g[:, OFF_SMALL + 8:OFF_SMALL + 8 + GATE_RANK]], axis=1)


def _lane_vec(v, offset=0):
    return jnp.zeros((1, 128), F32).at[0, offset:offset + v.shape[0]].set(v)


def _local_step(x, tgt, mod, p, n_b, t_len):
    row1 = lambda v: v.reshape(1, -1)
    a_log, dt_bias = _lane_vec(p["dn_a_log"]), _lane_vec(p["dn_dt_bias"])
    dn_gn, gla_gn = row1(p["dn_norm_g"]), row1(p["gla_norm_g"])
    w2 = jnp.zeros((128, 256), F32).at[8:8 + GATE_RANK].set(p["gla_w_gate2"])
    b2 = row1(p["gla_b_gate"])
    ln0_g, ln0_b, ln1_g, ln1_b, ln2_g, ln2_b = (row1(p[k]) for k in ("ln0_g", "ln0_b", "ln1_g", "ln1_b", "ln2_g", "ln2_b"))
    conv_b = row1(p["ffn_conv_b"])

    x0, h1 = _ln0_fwd(x, ln0_g, ln0_b, mod, n_b, t_len)
    proj = _mm(h1, p["w_in_p"], name="mm_proj")
    qkv = _dn_pre_fwd(proj, p["dn_conv"], n_b, t_len)
    o_half, hist_dn = _dn_scan_fwd(qkv, proj, a_log, dt_bias, dn_gn, n_b, t_len)
    o_mix, hist_gla = _gla_scan_fwd(proj, w2, b2, gla_gn, o_half, n_b, t_len)
    y = _mm(o_mix, p["w_o"], name="mm_wo")
    x1, h2 = _ln1_fwd(x0, y, ln1_g, ln1_b, mod, n_b, t_len)
    up = _mm(h2, p["w_up"], name="mm_up")
    act = _ffn_act_fwd(up, p["ffn_conv"], conv_b, n_b, t_len)
    y2 = _mm(act, p["w_down"], name="mm_down")

    loss, dx1, dy2, g_ln2_g, g_ln2_b, dgt_f = _ln2_loss_bwd(x1, y2, ln2_g, ln2_b, mod, tgt, n_b, t_len)
    g_w_down = _mm(act, dy2, ta=True, name="mm_g_down")
    d_act = _mm(dy2, p["w_down"], tb=True, name="mm_d_act")
    d_up, g_ffn_conv, g_conv_b = _ffn_act_bwd(up, p["ffn_conv"], conv_b, d_act, n_b, t_len)
    g_w_up = _mm(h2, d_up, ta=True, out_slabs=N_CHIPS, name="mm_g_up")
    dh2 = _mm(d_up, p["w_up"], tb=True, name="mm_d_h2")
    dx0, dy, g_ln1_g, g_ln1_b, dmod_1 = _ln1_bwd(x0, y, ln1_g, ln1_b, mod, dx1, dh2, n_b, t_len)
    g_w_o = _mm(o_mix, dy, ta=True, name="mm_g_wo")
    d_o = _mm(dy, p["w_o"], tb=True, name="mm_d_o")
    dqkv, d_proj, dsm_dn, g_a_log, g_dt_bias, g_dn_gn = _dn_scan_bwd(qkv, proj, a_log, dt_bias, dn_gn, hist_dn, d_o,
                                                                    n_b, t_len)
    d_proj, g_w2, g_b2, g_gla_gn = _gla_scan_bwd(proj, w2, b2, gla_gn, hist_gla, d_o, dsm_dn, d_proj, n_b, t_len)
    d_proj, g_dn_conv = _dn_pre_bwd(proj, p["dn_conv"], dqkv, d_proj.reshape(n_b * t_len, PROJ_W), n_b, t_len)
    g_w_in_p = _mm(h1, d_proj, ta=True, name="mm_g_win")
    dh1 = _mm(d_proj, p["w_in_p"], tb=True, name="mm_d_h1")
    grad_x, g_ln0_g, g_ln0_b, dmod_0 = _ln0_bwd(x, ln0_g, ln0_b, mod, dx0, dh1, n_b, t_len)

    dmod = jnp.concatenate([dmod_0, dmod_1[:, 0:1], dmod_1[:, 1:3], dgt_f], axis=1)
    grads = {
        "ln0_g": g_ln0_g[0], "ln0_b": g_ln0_b[0], "w_in_p": g_w_in_p, "dn_conv": g_dn_conv,
        "dn_a_log": g_a_log[0, 0:HEADS], "dn_dt_bias": g_dt_bias[0, 0:HEADS], "dn_norm_g": g_dn_gn[0],
        "gla_w_gate2": g_w2[8:8 + GATE_RANK], "gla_b_gate": g_b2[0], "gla_norm_g": g_gla_gn[0],
        "w_o": g_w_o, "ln1_g": g_ln1_g[0], "ln1_b": g_ln1_b[0], "w_up": g_w_up,
        "ffn_conv": jnp.concatenate([g_ffn_conv[0], g_ffn_conv[1]], axis=1),
        "ffn_conv_b": jnp.concatenate([g_conv_b[0, 0], g_conv_b[1, 0]]), "w_down": g_w_down,
        "ln2_g": g_ln2_g[0], "ln2_b": g_ln2_b[0],
    }
    return loss, grad_x, grads, dmod


def _ada_fwd(c_all, w_shard, b_shard):
    n_all, n_col = c_all.shape[0], w_shard.shape[1]
    tn = 512

    def body(c_ref, w_ref, b_ref, cond_ref, mod_ref):
        cond = _silu(c_ref[...])
        cond_ref[...] = cond
        mod_ref[...] = jnp.dot(cond.astype(BF16), w_ref[...].astype(BF16), preferred_element_type=F32) + b_ref[...]

    return pl.pallas_call(
        body, name="ada_fwd", grid=(n_col // tn,),
        in_specs=[pl.BlockSpec((n_all, D_MODEL), lambda j: (0, 0)), pl.BlockSpec((D_MODEL, tn), lambda j: (0, j)),
                  pl.BlockSpec((1, tn), lambda j: (0, j))],
        out_specs=[pl.BlockSpec((n_all, D_MODEL), lambda j: (0, 0)), pl.BlockSpec((n_all, tn), lambda j: (0, j))],
        out_shape=[jax.ShapeDtypeStruct((n_all, D_MODEL), F32), jax.ShapeDtypeStruct((n_all, n_col), F32)],
        compiler_params=_cparams(("arbitrary",)),
    )(c_all, w_shard, b_shard)


def _col_sum(a):
    def body(a_ref, o_ref):
        o_ref[...] = jnp.sum(a_ref[...], 0, keepdims=True)

    return pl.pallas_call(body, name="col_sum", out_shape=jax.ShapeDtypeStruct((1, a.shape[1]), F32))(a)


def _adamw(w, g, m, v, name):
    n_r, n_c = w.shape
    tr = _pick(n_r, (256, 64, 32, 16, 8))

    def body(w_ref, g_ref, m_ref, v_ref, d_ref, nm_ref, nv_ref):
        grad = g_ref[...]
        new_m = ADAM_B1 * m_ref[...] + (1.0 - ADAM_B1) * grad
        new_v = ADAM_B2 * v_ref[...] + (1.0 - ADAM_B2) * (grad * grad)
        m_hat = new_m / (1.0 - ADAM_B1 ** ADAM_STEP)
        v_hat = new_v / (1.0 - ADAM_B2 ** ADAM_STEP)
        d_ref[...] = -ADAM_LR * (m_hat / (jnp.sqrt(v_hat) + ADAM_EPS) + ADAM_WD * w_ref[...])
        nm_ref[...] = new_m
        nv_ref[...] = new_v

    blk = pl.BlockSpec((tr, n_c), lambda i: (i, 0))
    out = jax.ShapeDtypeStruct(w.shape, F32)
    return pl.pallas_call(
        body, name=name, grid=(n_r // tr,), in_specs=[blk] * 4, out_specs=[blk] * 3, out_shape=[out] * 3,
        compiler_params=_cparams(("parallel",)),
    )(w, g, m, v)


HBM_SPEC = pl.BlockSpec(memory_space=pltpu.HBM)
VMEM_SPEC = pl.BlockSpec(memory_space=pltpu.VMEM)
CHIP_FLIPS = ((1, 0), (0, 1), (1, 1))


def _place():
    return lax.axis_index("x"), lax.axis_index("y"), lax.axis_index("c")


def _flip(v, f):
    return 1 - v if f else v


def _all_gather8(slab, name):
    n_r, n_w = slab.shape

    def body(x_ref, o_ref, s_ref, send_sems, recv_sems, local_sem):
        x, y, c = _place()
        me = 4 * x + 2 * y + c
        mine = pltpu.make_async_copy(x_ref, o_ref.at[me], local_sem)
        mine.start()
        peers = [(_flip(x, k & 4), _flip(y, k & 2), _flip(c, k & 1)) for k in range(1, N_DEV)]
        sends = []
        for k, peer in enumerate(peers):
            cp = pltpu.make_async_remote_copy(src_ref=x_ref, dst_ref=o_ref.at[me], send_sem=send_sems.at[k],
                                              recv_sem=recv_sems.at[k], device_id=peer, device_id_type=MESH)
            cp.start()
            sends.append(cp)
        for k, (px, py, pc) in enumerate(peers):
            pltpu.make_async_remote_copy(src_ref=x_ref, dst_ref=o_ref.at[4 * px + 2 * py + pc],
                                         send_sem=send_sems.at[k], recv_sem=recv_sems.at[k],
                                         device_id=(px, py, pc), device_id_type=MESH).wait_recv()
        for cp in sends:
            cp.wait_send()
        mine.wait()
        total = o_ref[0]
        for d in range(1, N_DEV):
            total = total + o_ref[d]
        s_ref[...] = total

    return pl.pallas_call(
        body, name=name, in_specs=[VMEM_SPEC], out_specs=[VMEM_SPEC, VMEM_SPEC],
        out_shape=[jax.ShapeDtypeStruct((N_DEV, n_r, n_w), F32), jax.ShapeDtypeStruct((n_r, n_w), F32)],
        scratch_shapes=[pltpu.SemaphoreType.DMA((N_DEV - 1,)), pltpu.SemaphoreType.DMA((N_DEV - 1,)),
                        pltpu.SemaphoreType.DMA],
    )(slab)


def _gather_weights(shards):
    n_a = len(shards)

    def body(*refs):
        ins, outs, stage = refs[:n_a], refs[n_a:2 * n_a], refs[2 * n_a:3 * n_a]
        send_sems, recv_sems, local_sems = refs[3 * n_a:]
        x, y, c = _place()
        me_chip = 2 * x + y
        sibling = (x, y, 1 - c)
        chips = [(_flip(x, fx), _flip(y, fy)) for fx, fy in CHIP_FLIPS]
        stage_in = [pltpu.make_async_copy(ins[k], stage[k], local_sems.at[k]) for k in range(n_a)]
        for cp in stage_in:
            cp.start()

        def copy(k, slot, chip_of_block, half, to, src=None):
            dst = outs[k].at[chip_of_block, half]
            return pltpu.make_async_remote_copy(src_ref=dst if src is None else src, dst_ref=dst,
                                                send_sem=send_sems.at[k * 6 + slot], recv_sem=recv_sems.at[k * 6 + slot],
                                                device_id=to, device_id_type=MESH)

        first = [copy(k, r, me_chip, c, (*chips[r], c), src=ins[k].at[c]) for k in range(n_a) for r in range(3)]
        for cp in first:
            cp.start()
        stage_out = []
        for k in range(n_a):
            stage_in[k].wait()
            cp = pltpu.make_async_copy(stage[k], outs[k].at[me_chip], local_sems.at[n_a + k])
            cp.start()
            stage_out.append(cp)
        passed = []
        for k in range(n_a):
            for r, (px, py) in enumerate(chips):
                copy(k, r, 2 * px + py, c, (x, y, c)).wait_recv()
                fwd = copy(k, 3 + r, 2 * px + py, c, sibling)
                fwd.start()
                passed.append(fwd)
        for k in range(n_a):
            for r, (px, py) in enumerate(chips):
                copy(k, 3 + r, 2 * px + py, 1 - c, (x, y, c)).wait_recv()
        for cp in first + passed:
            cp.wait_send()
        for cp in stage_out:
            cp.wait()

    return pl.pallas_call(
        body, name="gather_weights", in_specs=[HBM_SPEC] * n_a, out_specs=[HBM_SPEC] * n_a,
        out_shape=[jax.ShapeDtypeStruct((N_CHIPS,) + s.shape, s.dtype) for s in shards],
        scratch_shapes=[pltpu.VMEM(s.shape, s.dtype) for s in shards]
        + [pltpu.SemaphoreType.DMA((6 * n_a,)), pltpu.SemaphoreType.DMA((6 * n_a,)),
           pltpu.SemaphoreType.DMA((2 * n_a,))],
        compiler_params=pltpu.CompilerParams(vmem_limit_bytes=VMEM_LIMIT),
    )(*shards)


def _rs_pair(parts):
    n_a = len(parts)

    def body(*refs):
        ins, outs = refs[:n_a], refs[n_a:2 * n_a]
        send_sems, recv_sems = refs[2 * n_a:]
        x, y, c = _place()
        cps = [pltpu.make_async_remote_copy(src_ref=ins[k].at[:, 1 - c], dst_ref=outs[k], send_sem=send_sems.at[k],
                                            recv_sem=recv_sems.at[k], device_id=(x, y, 1 - c), device_id_type=MESH)
               for k in range(n_a)]
        for cp in cps:
            cp.start()
        for cp in cps:
            cp.wait()

    return pl.pallas_call(
        body, name="rs_pair", in_specs=[HBM_SPEC] * n_a, out_specs=[HBM_SPEC] * n_a,
        out_shape=[jax.ShapeDtypeStruct((N_CHIPS,) + p.shape[2:], F32) for p in parts],
        scratch_shapes=[pltpu.SemaphoreType.DMA((n_a,)), pltpu.SemaphoreType.DMA((n_a,))],
    )(*parts)


def _rs_chips(sums):
    n_a = len(sums)

    def body(*refs):
        ins, outs = refs[:n_a], refs[n_a:2 * n_a]
        send_sems, recv_sems = refs[2 * n_a:]
        x, y, c = _place()
        cps = []
        for k in range(n_a):
            for r, (fx, fy) in enumerate(CHIP_FLIPS):
                px, py = _flip(x, fx), _flip(y, fy)
                cps.append(pltpu.make_async_remote_copy(
                    src_ref=ins[k].at[2 * px + py], dst_ref=outs[k].at[r], send_sem=send_sems.at[3 * k + r],
                    recv_sem=recv_sems.at[3 * k + r], device_id=(px, py, c), device_id_type=MESH))
        for cp in cps:
            cp.start()
        for cp in cps:
            cp.wait()

    return pl.pallas_call(
        body, name="rs_chips", in_specs=[HBM_SPEC] * n_a, out_specs=[HBM_SPEC] * n_a,
        out_shape=[jax.ShapeDtypeStruct((3,) + s.shape[1:], s.dtype) for s in sums],
        scratch_shapes=[pltpu.SemaphoreType.DMA((3 * n_a,)), pltpu.SemaphoreType.DMA((3 * n_a,))],
    )(*sums)


def _rs_share(bufs):
    n_a = len(bufs)

    def body(*refs):
        ins, outs = refs[:n_a], refs[n_a:2 * n_a]
        send_sems, recv_sems = refs[2 * n_a:]
        x, y, c = _place()
        sends = [pltpu.make_async_remote_copy(src_ref=ins[k].at[c], dst_ref=outs[k].at[c], send_sem=send_sems.at[k],
                                              recv_sem=recv_sems.at[k], device_id=(x, y, 1 - c), device_id_type=MESH)
                 for k in range(n_a)]
        for cp in sends:
            cp.start()
        for k in range(n_a):
            pltpu.make_async_remote_copy(src_ref=ins[k].at[c], dst_ref=outs[k].at[1 - c], send_sem=send_sems.at[k],
                                         recv_sem=recv_sems.at[k], device_id=(x, y, 1 - c),
                                         device_id_type=MESH).wait_recv()
        for cp in sends:
            cp.wait_send()

    return pl.pallas_call(
        body, name="rs_share", in_specs=[HBM_SPEC] * n_a, out_specs=[HBM_SPEC] * n_a,
        out_shape=[jax.ShapeDtypeStruct(s.shape, F32) for s in bufs],
        input_output_aliases={k: k for k in range(n_a)},
        scratch_shapes=[pltpu.SemaphoreType.DMA((n_a,)), pltpu.SemaphoreType.DMA((n_a,))],
    )(*bufs)


def _pair_add(part, recv, core, name):
    _, _, n_h, n_c = part.shape
    th = _pick(n_h, (256, 176, 128))

    def body(sel_ref, p_ref, r_ref, o_ref):
        o_ref[...] = (p_ref[...] + r_ref[...]).astype(BF16)

    grid_spec = pltpu.PrefetchScalarGridSpec(
        num_scalar_prefetch=1, grid=(N_CHIPS, n_h // th),
        in_specs=[pl.BlockSpec((None, None, th, n_c), lambda j, i, sel: (j, sel[0], i, 0)),
                  pl.BlockSpec((None, th, n_c), lambda j, i, sel: (j, i, 0))],
        out_specs=pl.BlockSpec((None, th, n_c), lambda j, i, sel: (j, i, 0)))
    return pl.pallas_call(
        body, name=name, grid_spec=grid_spec, out_shape=jax.ShapeDtypeStruct(recv.shape, BF16),
        compiler_params=_cparams(("parallel", "parallel")),
    )(core.reshape(1), part, recv)


def _chip_add(sums, recv, chip, core, name):
    _, n_h, n_c = sums.shape
    th = _pick(n_h, (256, 176, 128))

    def body(sel_ref, s_ref, r_ref, o_ref):
        total = s_ref[...].astype(F32)
        for r in range(3):
            total = total + r_ref[r].astype(F32)
        o_ref[...] = total

    grid_spec = pltpu.PrefetchScalarGridSpec(
        num_scalar_prefetch=1, grid=(n_h // th,),
        in_specs=[pl.BlockSpec((None, th, n_c), lambda i, sel: (sel[0], i, 0)),
                  pl.BlockSpec((3, th, n_c), lambda i, sel: (0, i, 0))],
        out_specs=pl.BlockSpec((None, th, n_c), lambda i, sel: (sel[1], i, 0)))
    return pl.pallas_call(
        body, name=name, grid_spec=grid_spec, out_shape=jax.ShapeDtypeStruct((2, n_h, n_c), F32),
        compiler_params=_cparams(("parallel",)),
    )(jnp.stack([chip, core]), sums, recv)


def _reduce_scatter(parts, core, chip):
    names = ("w_in", "w_o", "w_up", "w_down")
    from_sibling = _rs_pair(parts)
    pair_sums = [_pair_add(p, r, core, "pair_add_" + n) for p, r, n in zip(parts, from_sibling, names)]
    from_chips = _rs_chips(pair_sums)
    halves = [_chip_add(s, r, chip, core, "chip_add_" + n) for s, r, n in zip(pair_sums, from_chips, names)]
    return [f.reshape(-1, f.shape[-1]) for f in _rs_share(halves)]


SLAB_W = 1024


def _pack(arrays, rows):
    flat = jnp.concatenate([a.reshape(-1).astype(F32) for a in arrays])
    return jnp.pad(flat, (0, rows * SLAB_W - flat.shape[0])).reshape(rows, SLAB_W)


def _unpack(flat, shapes):
    out, off = [], 0
    for s in shapes:
        n = 1
        for d in s:
            n *= d
        out.append(flat[off:off + n].reshape(s))
        off += n
    return out


def _rows_for(arrays_or_shapes):
    n = 0
    for a in arrays_or_shapes:
        s = a if isinstance(a, tuple) else a.shape
        k = 1
        for d in s:
            k *= d
        n += k
    return -(-n // (8 * SLAB_W)) * 8


def _row_halves(a):
    n_c = a.shape[-1]
    return a.reshape(N_CHIPS, 2, -1, n_c)


def _by_cols(a, n_cols):
    return a.reshape(a.shape[0], N_CHIPS, n_cols).transpose(1, 0, 2)


def kernel(x, c, ln0_g, ln0_b, w_ada, b_ada, w_in, dn_conv, dn_a_log, dn_dt_bias, dn_norm_g, gla_w_gate2, gla_b_gate, gla_norm_g, w_o, ln1_g, ln1_b, ffn_w_up, ffn_conv, ffn_conv_b, ffn_w_down, ln2_g, ln2_b, loss_target, m_ln0_g, m_ln0_b, m_w_ada, m_b_ada, m_w_in, m_dn_conv, m_dn_a_log, m_dn_dt_bias, m_dn_norm_g, m_gla_w_gate2, m_gla_b_gate, m_gla_norm_g, m_w_o, m_ln1_g, m_ln1_b, m_ffn_w_up, m_ffn_conv, m_ffn_conv_b, m_ffn_w_down, m_ln2_g, m_ln2_b, v_ln0_g, v_ln0_b, v_w_ada, v_b_ada, v_w_in, v_dn_conv, v_dn_a_log, v_dn_dt_bias, v_dn_norm_g, v_gla_w_gate2, v_gla_b_gate, v_gla_norm_g, v_w_o, v_ln1_g, v_ln1_b, v_ffn_w_up, v_ffn_conv, v_ffn_conv_b, v_ffn_w_down, v_ln2_g, v_ln2_b):
    n_b, t_len, _ = x.shape
    xi, yi, ci = _place()
    chip = (2 * xi + yi).astype(jnp.int32)
    core = ci.astype(jnp.int32)
    me = 2 * chip + core
    n_all = N_DEV * n_b
    ada_cols = w_ada.shape[2]

    sharded_small = [dn_conv[0], gla_w_gate2[0], ffn_conv[0]]
    slab = _pack([c] + sharded_small, _rows_for([c] + sharded_small))
    gathered, _ = _all_gather8(slab, "gather_small")
    flat = gathered.reshape(N_DEV, -1)
    per_dev = [_unpack(flat[d], [c.shape] + [a.shape for a in sharded_small]) for d in range(N_DEV)]
    c_all = jnp.concatenate([per_dev[d][0] for d in range(N_DEV)], axis=0)
    dn_conv_f, gate2_f, ffn_conv_f = (jnp.concatenate([per_dev[2 * j][i] for j in range(N_CHIPS)], axis=1)
                                      for i in (1, 2, 3))

    b_ada_shard = lax.dynamic_slice(b_ada, (0, chip * ada_cols), (1, ada_cols))
    cond_all, mod_cols = _ada_fwd(c_all, w_ada[0], b_ada_shard)
    mod_g, _ = _all_gather8(mod_cols, "gather_mod")
    mod_full = jnp.concatenate([mod_g[2 * j] for j in range(N_CHIPS)], axis=1)
    mod = lax.dynamic_slice(mod_full, (me * n_b, 0), (n_b, 6 * D_MODEL)).reshape(n_b, 6, D_MODEL)

    halves = lambda a: a.astype(BF16).reshape(2, a.shape[0] // 2, a.shape[1])
    g_in, g_o, g_up, g_down = _gather_weights([halves(w_in[0]), halves(w_o[0]), halves(ffn_w_up[0]),
                                               halves(ffn_w_down[0])])
    cols = lambda g: g.reshape(N_CHIPS, -1, g.shape[-1]).transpose(1, 0, 2).reshape(-1, N_CHIPS * g.shape[-1])
    params = {
        "w_in_p": _pad_w_in(cols(g_in)), "w_o": g_o.reshape(-1, D_MODEL),
        "w_up": g_up.reshape(N_CHIPS, -1, g_up.shape[-1]),
        "w_down": g_down.reshape(-1, D_MODEL),
        "dn_conv": dn_conv_f, "dn_a_log": dn_a_log[0], "dn_dt_bias": dn_dt_bias[0], "dn_norm_g": dn_norm_g[0],
        "gla_w_gate2": gate2_f, "gla_b_gate": gla_b_gate[0], "gla_norm_g": gla_norm_g[0],
        "ln0_g": ln0_g, "ln0_b": ln0_b, "ln1_g": ln1_g[0], "ln1_b": ln1_b[0], "ln2_g": ln2_g[0], "ln2_b": ln2_b[0],
        "ffn_conv": ffn_conv_f, "ffn_conv_b": ffn_conv_b[0],
    }

    loss_row, grad_x, gp, dmod = _local_step(x.reshape(n_b * t_len, D_MODEL), loss_target.reshape(n_b * t_len, D_MODEL),
                                             mod, params, n_b, t_len)
    loss = lax.psum(loss_row[0, 0], ("x", "y", "c"))

    summed_names = ["ln0_g", "ln0_b", "dn_conv", "dn_a_log", "dn_dt_bias", "dn_norm_g", "gla_w_gate2", "gla_b_gate",
                    "gla_norm_g", "ln1_g", "ln1_b", "ffn_conv", "ffn_conv_b", "ln2_g", "ln2_b"]
    summed_parts = [gp[n] for n in summed_names]
    sum_rows = _rows_for(summed_parts)
    slab = jnp.concatenate([_pack(summed_parts, sum_rows), _pack([dmod], _rows_for([dmod]))], axis=0)
    gathered, total = _all_gather8(slab, "reduce_small")
    small_g = dict(zip(summed_names, _unpack(total.reshape(-1), [a.shape for a in summed_parts])))
    dmod_rows = n_b * 6 * D_MODEL // SLAB_W
    dmod_all = gathered[:, sum_rows:sum_rows + dmod_rows, :].reshape(n_all, 6 * D_MODEL)

    g_b_ada = _col_sum(dmod_all)
    dmod_cols = lax.dynamic_slice(dmod_all, (0, chip * ada_cols), (n_all, ada_cols))
    g_w_ada = _mm(cond_all, dmod_cols, ta=True, name="mm_g_ada")

    g_w_in, g_w_o, g_w_up, g_w_down = _reduce_scatter(
        [_row_halves(_by_cols(_unpad_w_in(gp["w_in_p"]), w_in.shape[2])), _row_halves(gp["w_o"]),
         _row_halves(gp["w_up"]), _row_halves(gp["w_down"])], core, chip)

    col_block = lambda a: lax.dynamic_slice(a, (0, chip * (a.shape[1] // N_CHIPS)), (a.shape[0], a.shape[1] // N_CHIPS))
    grads = {
        "ln0_g": small_g["ln0_g"], "ln0_b": small_g["ln0_b"], "w_ada": g_w_ada[None], "b_ada": g_b_ada,
        "w_in": g_w_in[None], "dn_conv": col_block(small_g["dn_conv"])[None], "dn_a_log": small_g["dn_a_log"][None],
        "dn_dt_bias": small_g["dn_dt_bias"][None], "dn_norm_g": small_g["dn_norm_g"][None],
        "gla_w_gate2": col_block(small_g["gla_w_gate2"])[None], "gla_b_gate": small_g["gla_b_gate"][None],
        "gla_norm_g": small_g["gla_norm_g"][None], "w_o": g_w_o[None], "ln1_g": small_g["ln1_g"][None],
        "ln1_b": small_g["ln1_b"][None], "ffn_w_up": g_w_up[None], "ffn_conv": col_block(small_g["ffn_conv"])[None],
        "ffn_conv_b": small_g["ffn_conv_b"][None], "ffn_w_down": g_w_down[None], "ln2_g": small_g["ln2_g"][None],
        "ln2_b": small_g["ln2_b"][None],
    }
    names = ["ln0_g", "ln0_b", "w_ada", "b_ada", "w_in", "dn_conv", "dn_a_log", "dn_dt_bias", "dn_norm_g",
             "gla_w_gate2", "gla_b_gate", "gla_norm_g", "w_o", "ln1_g", "ln1_b", "ffn_w_up", "ffn_conv", "ffn_conv_b",
             "ffn_w_down", "ln2_g", "ln2_b"]
    weights = dict(zip(names, [ln0_g, ln0_b, w_ada, b_ada, w_in, dn_conv, dn_a_log, dn_dt_bias, dn_norm_g, gla_w_gate2,
                               gla_b_gate, gla_norm_g, w_o, ln1_g, ln1_b, ffn_w_up, ffn_conv, ffn_conv_b, ffn_w_down,
                               ln2_g, ln2_b]))
    m_in = dict(zip(names, [m_ln0_g, m_ln0_b, m_w_ada, m_b_ada, m_w_in, m_dn_conv, m_dn_a_log, m_dn_dt_bias,
                            m_dn_norm_g, m_gla_w_gate2, m_gla_b_gate, m_gla_norm_g, m_w_o, m_ln1_g, m_ln1_b,
                            m_ffn_w_up, m_ffn_conv, m_ffn_conv_b, m_ffn_w_down, m_ln2_g, m_ln2_b]))
    v_in = dict(zip(names, [v_ln0_g, v_ln0_b, v_w_ada, v_b_ada, v_w_in, v_dn_conv, v_dn_a_log, v_dn_dt_bias,
                            v_dn_norm_g, v_gla_w_gate2, v_gla_b_gate, v_gla_norm_g, v_w_o, v_ln1_g, v_ln1_b,
                            v_ffn_w_up, v_ffn_conv, v_ffn_conv_b, v_ffn_w_down, v_ln2_g, v_ln2_b]))

    big = ("w_ada", "w_in", "w_o", "ffn_w_up", "ffn_w_down")
    delta, new_m, new_v = {}, {}, {}
    for n in big:
        d_n, m_n, v_n = _adamw(weights[n][0], grads[n][0], m_in[n][0], v_in[n][0], "adamw_" + n)
        delta[n], new_m[n], new_v[n] = d_n[None], m_n[None], v_n[None]
    small = [n for n in names if n not in big]
    shapes = [weights[n].shape for n in small]
    rows = _rows_for(shapes)
    d_s, m_s, v_s = _adamw(_pack([weights[n] for n in small], rows), _pack([grads[n] for n in small], rows),
                           _pack([m_in[n] for n in small], rows), _pack([v_in[n] for n in small], rows), "adamw_small")
    for out, slab_out in ((delta, d_s), (new_m, m_s), (new_v, v_s)):
        out.update(zip(small, _unpack(slab_out.reshape(-1), shapes)))

    return (loss, grad_x.reshape(x.shape), *[grads[n] for n in names], *[delta[n] for n in names],
            *[new_m[n] for n in names], *[new_v[n] for n in names])
```

```python
import functools

import jax
import jax.numpy as jnp
from jax import lax
from jax.experimental import pallas as pl
from jax.experimental.pallas import tpu as pltpu

F32 = jnp.float32
BF16 = jnp.bfloat16
MESH = pl.DeviceIdType.MESH

D_MODEL = 1024
HEADS = 4
HEAD_DIM = 128
GLA_KEY = 64
GATE_RANK = 16
CHUNK = 64
D_FF = 2816
ALPHA = 2.0 ** 0.25
EPS = 1e-6
N_CHIPS = 4
N_DEV = 8

PROJ_W = 3840
OFF_GQ, OFF_GK, OFF_GV, OFF_GG, OFF_SMALL, GLA_W = 0, 256, 512, 1024, 1536, 1792
OFF_Z = 2048
W_IN_COLS = 3608


def _qkv_block(j):
    return jnp.where(j < 2, GLA_W // 128 + j, (OFF_Z + 512) // 128 - 2 + j)

ADAM_LR, ADAM_B1, ADAM_B2, ADAM_EPS, ADAM_WD, ADAM_STEP = 0.001, 0.9, 0.999, 1e-08, 0.01, 10

VMEM_LIMIT = 56 * 1024 * 1024
ROW_TILE = 256


def _cparams(sem):
    return pltpu.CompilerParams(dimension_semantics=sem, vmem_limit_bytes=VMEM_LIMIT)


def _pick(n, prefs):
    for p in prefs:
        if n % p == 0:
            return p
    return n


def _mm(a, b, *, ta=False, tb=False, out_slabs=1, out_dtype=F32, name):
    a_slabs = a.shape[0] if a.ndim == 3 else 1
    b_slabs = b.shape[0] if b.ndim == 3 else 1
    assert not (ta and a_slabs > 1)
    a2, b2 = a.shape[-2:], b.shape[-2:]
    if ta:
        k_dim, m_dim = a2
    else:
        m_dim, k_dim = a2[0], a2[1] * a_slabs
    n_dim = b2[0] if tb else b2[1] * b_slabs
    k_slabs = max(a_slabs, b_slabs if tb else 1)
    n_slabs = max(out_slabs, 1 if tb else b_slabs)
    tm = _pick(m_dim, (1024, 1408, 512, 256, 128))
    tn = _pick(n_dim // n_slabs, (1536, 1408, 1280, 1024, 768, 512, 384, 256, 128))
    tk = _pick(k_dim // k_slabs, (1408, 1280, 1024, 512, 256, 128))
    nk, nj = k_dim // tk, n_dim // tn
    nk_a, nk_b, nj_b, nj_o = nk // a_slabs, nk // b_slabs, nj // b_slabs, nj // out_slabs
    dims = (((0 if ta else 1,), (1 if tb else 0,)), ((), ()))

    def body(a_ref, b_ref, o_ref, acc_ref):
        k = pl.program_id(2)

        @pl.when(k == 0)
        def _():
            acc_ref[...] = jnp.zeros_like(acc_ref)

        acc_ref[...] += lax.dot_general(a_ref[...].astype(BF16), b_ref[...].astype(BF16), dims,
                                        preferred_element_type=F32)

        @pl.when(k == nk - 1)
        def _():
            o_ref[...] = acc_ref[...].astype(o_ref.dtype)

    if ta:
        a_spec = pl.BlockSpec((tk, tm), lambda i, j, k: (k, i))
    elif a_slabs > 1:
        a_spec = pl.BlockSpec((None, tm, tk), lambda i, j, k: (k // nk_a, i, k % nk_a))
    else:
        a_spec = pl.BlockSpec((tm, tk), lambda i, j, k: (i, k))
    if tb and b_slabs > 1:
        b_spec = pl.BlockSpec((None, tn, tk), lambda i, j, k: (k // nk_b, j, k % nk_b))
    elif tb:
        b_spec = pl.BlockSpec((tn, tk), lambda i, j, k: (j, k))
    elif b_slabs > 1:
        b_spec = pl.BlockSpec((None, tk, tn), lambda i, j, k: (j // nj_b, k, j % nj_b))
    else:
        b_spec = pl.BlockSpec((tk, tn), lambda i, j, k: (k, j))
    if out_slabs > 1:
        o_spec = pl.BlockSpec((None, tm, tn), lambda i, j, k: (j // nj_o, i, j % nj_o))
        o_shape = (out_slabs, m_dim, n_dim // out_slabs)
    else:
        o_spec, o_shape = pl.BlockSpec((tm, tn), lambda i, j, k: (i, j)), (m_dim, n_dim)
    return pl.pallas_call(
        body, name=name, grid=(m_dim // tm, nj, nk),
        in_specs=[a_spec, b_spec], out_specs=o_spec,
        out_shape=jax.ShapeDtypeStruct(o_shape, out_dtype),
        scratch_shapes=[pltpu.VMEM((tm, tn), F32)],
        compiler_params=_cparams(("parallel", "parallel", "arbitrary")),
    )(a, b)


def _ln(x, g, b):
    mu = jnp.mean(x, -1, keepdims=True)
    xc = x - mu
    var = jnp.mean(xc * xc, -1, keepdims=True)
    return xc * lax.rsqrt(var + EPS) * g + b


def _softplus(x):
    return jnp.maximum(x, 0.0) + jnp.log(1.0 + jnp.exp(-jnp.abs(x)))


def _silu(x):
    return x * jax.nn.sigmoid(x)


def _dsilu(x):
    s = jax.nn.sigmoid(x)
    return s * (1.0 + x * (1.0 - s))


def _f_ln0(x, g, b, sc, sh):
    x0 = _ln(x, g, b)
    return x0, x0 * (1.0 + sc) + sh


def _f_ln1(x0, y, gt, g, b, sc, sh):
    x1 = _ln(ALPHA * x0 + (1.0 + gt) * y, g, b)
    return x1, x1 * (1.0 + sc) + sh


def _f_ln2_loss(x1, y2, gt, g, b, tgt):
    x2 = _ln(ALPHA * x1 + (1.0 + gt) * y2, g, b)
    err = x2 - tgt
    per_row = jnp.sum(err * err, -1, keepdims=True) * (0.5 / D_MODEL)
    return jnp.sum(per_row, 0, keepdims=True)


def _row_specs(t_len):
    nt = t_len // ROW_TILE
    row = pl.BlockSpec((ROW_TILE, D_MODEL), lambda b, i: (b * nt + i, 0))
    vec = pl.BlockSpec((1, D_MODEL), lambda b, i: (0, 0))
    mod = pl.BlockSpec((None, 6, D_MODEL), lambda b, i: (b, 0, 0))
    return nt, row, vec, mod


def _first_step():
    return jnp.logical_and(pl.program_id(0) == 0, pl.program_id(1) == 0)


def _acc(ref, val, first, at=(Ellipsis,)):
    @pl.when(first)
    def _():
        ref[at] = val

    @pl.when(jnp.logical_not(first))
    def _():
        ref[at] += val


def _acc_rows(ref, rows, first):
    for i, r in enumerate(rows):
        _acc(ref, r, first, at=(slice(i, i + 1), slice(None)))


def _ln0_fwd(x, g, b, mod, n_b, t_len):
    nt, row, vec, mods = _row_specs(t_len)

    def body(x_ref, g_ref, b_ref, mod_ref, x0_ref, h_ref):
        x0, h = _f_ln0(x_ref[...], g_ref[...], b_ref[...], mod_ref[1:2, :], mod_ref[0:1, :])
        x0_ref[...] = x0
        h_ref[...] = h.astype(BF16)

    return pl.pallas_call(
        body, name="ln0_fwd", grid=(n_b, nt), in_specs=[row, vec, vec, mods], out_specs=[row, row],
        out_shape=[jax.ShapeDtypeStruct(x.shape, F32), jax.ShapeDtypeStruct(x.shape, BF16)],
        compiler_params=_cparams(("parallel", "parallel")),
    )(x, g, b, mod)


def _ln0_bwd(x, g, b, mod, dx0, dh, n_b, t_len):
    nt, row, vec, mods = _row_specs(t_len)
    dmod_spec = pl.BlockSpec((None, 2, D_MODEL), lambda bb, i: (bb, 0, 0))

    def body(x_ref, g_ref, b_ref, mod_ref, dx0_ref, dh_ref, dx_ref, dg_ref, db_ref, dmod_ref):
        _, pull = jax.vjp(_f_ln0, x_ref[...], g_ref[...], b_ref[...], mod_ref[1:2, :], mod_ref[0:1, :])
        dx, dg, db, dsc, dsh = pull((dx0_ref[...], dh_ref[...]))
        dx_ref[...] = dx
        _acc(dg_ref, dg, _first_step())
        _acc(db_ref, db, _first_step())
        _acc_rows(dmod_ref, [dsh, dsc], pl.program_id(1) == 0)

    return pl.pallas_call(
        body, name="ln0_bwd", grid=(n_b, nt), in_specs=[row, vec, vec, mods, row, row],
        out_specs=[row, vec, vec, dmod_spec],
        out_shape=[jax.ShapeDtypeStruct(x.shape, F32), jax.ShapeDtypeStruct((1, D_MODEL), F32),
                   jax.ShapeDtypeStruct((1, D_MODEL), F32), jax.ShapeDtypeStruct((n_b, 2, D_MODEL), F32)],
        compiler_params=_cparams(("arbitrary", "arbitrary")),
    )(x, g, b, mod, dx0, dh)


def _ln1_fwd(x0, y, g, b, mod, n_b, t_len):
    nt, row, vec, mods = _row_specs(t_len)

    def body(x0_ref, y_ref, g_ref, b_ref, mod_ref, x1_ref, h_ref):
        x1, h = _f_ln1(x0_ref[...], y_ref[...], mod_ref[2:3, :], g_ref[...], b_ref[...],
                       mod_ref[4:5, :], mod_ref[3:4, :])
        x1_ref[...] = x1
        h_ref[...] = h.astype(BF16)

    return pl.pallas_call(
        body, name="ln1_fwd", grid=(n_b, nt), in_specs=[row, row, vec, vec, mods], out_specs=[row, row],
        out_shape=[jax.ShapeDtypeStruct(x0.shape, F32), jax.ShapeDtypeStruct(x0.shape, BF16)],
        compiler_params=_cparams(("parallel", "parallel")),
    )(x0, y, g, b, mod)


def _ln1_bwd(x0, y, g, b, mod, dx1, dh, n_b, t_len):
    nt, row, vec, mods = _row_specs(t_len)
    dmod_spec = pl.BlockSpec((None, 3, D_MODEL), lambda bb, i: (bb, 0, 0))

    def body(x0_ref, y_ref, g_ref, b_ref, mod_ref, dx1_ref, dh_ref, dx0_ref, dy_ref, dg_ref, db_ref, dmod_ref):
        _, pull = jax.vjp(_f_ln1, x0_ref[...], y_ref[...], mod_ref[2:3, :], g_ref[...], b_ref[...],
                          mod_ref[4:5, :], mod_ref[3:4, :])
        dx0, dy, dgt, dg, db, dsc, dsh = pull((dx1_ref[...], dh_ref[...]))
        dx0_ref[...] = dx0
        dy_ref[...] = dy.astype(BF16)
        _acc(dg_ref, dg, _first_step())
        _acc(db_ref, db, _first_step())
        _acc_rows(dmod_ref, [dgt, dsh, dsc], pl.program_id(1) == 0)

    return pl.pallas_call(
        body, name="ln1_bwd", grid=(n_b, nt), in_specs=[row, row, vec, vec, mods, row, row],
        out_specs=[row, row, vec, vec, dmod_spec],
        out_shape=[jax.ShapeDtypeStruct(x0.shape, F32), jax.ShapeDtypeStruct(x0.shape, BF16),
                   jax.ShapeDtypeStruct((1, D_MODEL), F32), jax.ShapeDtypeStruct((1, D_MODEL), F32),
                   jax.ShapeDtypeStruct((n_b, 3, D_MODEL), F32)],
        compiler_params=_cparams(("arbitrary", "arbitrary")),
    )(x0, y, g, b, mod, dx1, dh)


def _ln2_loss_bwd(x1, y2, g, b, mod, tgt, n_b, t_len):
    nt, row, vec, mods = _row_specs(t_len)
    one = pl.BlockSpec((1, 128), lambda bb, i: (0, 0))
    dmod_spec = pl.BlockSpec((None, 1, D_MODEL), lambda bb, i: (bb, 0, 0))

    def body(x1_ref, y2_ref, g_ref, b_ref, mod_ref, t_ref, loss_ref, dx1_ref, dy2_ref, dg_ref, db_ref, dgt_ref):
        loss, pull = jax.vjp(functools.partial(_f_ln2_loss, tgt=t_ref[...]), x1_ref[...], y2_ref[...],
                             mod_ref[5:6, :], g_ref[...], b_ref[...])
        dx1, dy2, dgt, dg, db = pull(jnp.ones((1, 1), F32))
        dx1_ref[...] = dx1
        dy2_ref[...] = dy2.astype(BF16)
        _acc(loss_ref, jnp.broadcast_to(loss, (1, 128)), _first_step())
        _acc(dg_ref, dg, _first_step())
        _acc(db_ref, db, _first_step())
        _acc(dgt_ref, dgt, pl.program_id(1) == 0)

    return pl.pallas_call(
        body, name="ln2_loss_bwd", grid=(n_b, nt), in_specs=[row, row, vec, vec, mods, row],
        out_specs=[one, row, row, vec, vec, dmod_spec],
        out_shape=[jax.ShapeDtypeStruct((1, 128), F32), jax.ShapeDtypeStruct(x1.shape, F32),
                   jax.ShapeDtypeStruct(x1.shape, BF16), jax.ShapeDtypeStruct((1, D_MODEL), F32),
                   jax.ShapeDtypeStruct((1, D_MODEL), F32), jax.ShapeDtypeStruct((n_b, 1, D_MODEL), F32)],
        compiler_params=_cparams(("arbitrary", "arbitrary")),
    )(x1, y2, g, b, mod, tgt)


def _shift_down(x, s):
    if s == 0:
        return x
    rows = lax.broadcasted_iota(jnp.int32, x.shape, 0)
    return jnp.where(rows >= s, pltpu.roll(x, s, 0), 0.0)


def _shift_up(x, s):
    if s == 0:
        return x
    t_len = x.shape[0]
    rows = lax.broadcasted_iota(jnp.int32, x.shape, 0)
    return jnp.where(rows < t_len - s, pltpu.roll(x, t_len - s, 0), 0.0)


def _conv(x, w):
    k_w = w.shape[0]
    out = w[k_w - 1:k_w, :] * x
    for k in range(k_w - 1):
        out = out + w[k:k + 1, :] * _shift_down(x, k_w - 1 - k)
    return out


def _conv_bwd(x, w, du):
    k_w = w.shape[0]
    dx = w[k_w - 1:k_w, :] * du
    dws = []
    for k in range(k_w):
        s = k_w - 1 - k
        if s:
            dx = dx + w[k:k + 1, :] * _shift_up(du, s)
        dws.append(jnp.sum(du * _shift_down(x, s), 0, keepdims=True))
    return dx, dws


def _dn_pre_fwd(proj, conv_w, n_b, t_len):
    n_ct = 3 * HEADS
    k_w = conv_w.shape[0]

    def body(x_ref, w_ref, o_ref):
        o_ref[...] = _silu(_conv(x_ref[...], w_ref[...]))

    return pl.pallas_call(
        body, name="dn_pre_fwd", grid=(n_ct, n_b),
        in_specs=[pl.BlockSpec((t_len, 128), lambda j, b: (b, _qkv_block(j))),
                  pl.BlockSpec((k_w, 128), lambda j, b: (0, j))],
        out_specs=pl.BlockSpec((t_len, 128), lambda j, b: (b, j)),
        out_shape=jax.ShapeDtypeStruct((n_b * t_len, n_ct * 128), F32),
        compiler_params=_cparams(("parallel", "parallel")),
    )(proj, conv_w)


def _dn_pre_bwd(proj, conv_w, dqkv, d_proj, n_b, t_len):
    n_ct = 3 * HEADS
    k_w = conv_w.shape[0]

    def body(x_ref, w_ref, d_ref, _, dx_ref, dw_ref):
        x, w = x_ref[...], w_ref[...]
        du = d_ref[...] * _dsilu(_conv(x, w))
        dx, dw = _conv_bwd(x, w, du)
        dx_ref[...] = dx.astype(BF16)
        _acc_rows(dw_ref, dw, pl.program_id(1) == 0)

    return pl.pallas_call(
        body, name="dn_pre_bwd", grid=(n_ct, n_b),
        in_specs=[pl.BlockSpec((t_len, 128), lambda j, b: (b, _qkv_block(j))),
                  pl.BlockSpec((k_w, 128), lambda j, b: (0, j)),
                  pl.BlockSpec((t_len, 128), lambda j, b: (b, j)), pl.BlockSpec(memory_space=pl.ANY)],
        out_specs=[pl.BlockSpec((t_len, 128), lambda j, b: (b, _qkv_block(j))),
                   pl.BlockSpec((k_w, 128), lambda j, b: (0, j))],
        out_shape=[jax.ShapeDtypeStruct(d_proj.shape, BF16), jax.ShapeDtypeStruct((k_w, n_ct * 128), F32)],
        input_output_aliases={3: 0},
        compiler_params=_cparams(("parallel", "arbitrary")),
    )(proj, conv_w, dqkv, d_proj)


FFN_TC = 256
FFN_NT = D_FF // FFN_TC


def _ffn_specs(t_len):
    blk = lambda off: pl.BlockSpec((t_len, FFN_TC), lambda j, b: (b, j + off))
    wblk = lambda off: pl.BlockSpec((3, FFN_TC), lambda j, b: (0, j + off))
    bblk = lambda off: pl.BlockSpec((1, FFN_TC), lambda j, b: (0, j + off))
    return [blk(0), blk(FFN_NT), wblk(0), wblk(FFN_NT), bblk(0), bblk(FFN_NT)]


def _ffn_act_fwd(up, conv_w, conv_b, n_b, t_len):
    def body(g_ref, v_ref, wg_ref, wv_ref, bg_ref, bv_ref, o_ref):
        ug = _conv(g_ref[...], wg_ref[...]) + bg_ref[...]
        uv = _conv(v_ref[...], wv_ref[...]) + bv_ref[...]
        o_ref[...] = (_silu(ug) * uv).astype(BF16)

    return pl.pallas_call(
        body, name="ffn_act_fwd", grid=(FFN_NT, n_b), in_specs=_ffn_specs(t_len),
        out_specs=pl.BlockSpec((t_len, FFN_TC), lambda j, b: (b, j)),
        out_shape=jax.ShapeDtypeStruct((n_b * t_len, D_FF), BF16),
        compiler_params=_cparams(("parallel", "parallel")),
    )(up, up, conv_w, conv_w, conv_b, conv_b)


def _ffn_act_bwd(up, conv_w, conv_b, da, n_b, t_len):
    def body(g_ref, v_ref, wg_ref, wv_ref, bg_ref, bv_ref, da_ref, dup_ref, dw_ref, db_ref):
        first = pl.program_id(1) == 0
        xg, xv, wg, wv = g_ref[...], v_ref[...], wg_ref[...], wv_ref[...]
        ug = _conv(xg, wg) + bg_ref[...]
        uv = _conv(xv, wv) + bv_ref[...]
        d_act = da_ref[...]
        sig = jax.nn.sigmoid(ug)
        d_v = d_act * (ug * sig)
        d_g = d_act * uv * (sig * (1.0 + ug * (1.0 - sig)))
        for slab, (x, w, du) in enumerate(((xg, wg, d_g), (xv, wv, d_v))):
            dx, dw = _conv_bwd(x, w, du)
            dup_ref[slab] = dx.astype(BF16)
            for k, dw_k in enumerate(dw):
                _acc(dw_ref, dw_k, first, at=(slab, slice(k, k + 1), slice(None)))
            _acc(db_ref, jnp.sum(du, 0, keepdims=True), first, at=(slab, slice(None), slice(None)))

    return pl.pallas_call(
        body, name="ffn_act_bwd", grid=(FFN_NT, n_b),
        in_specs=_ffn_specs(t_len) + [pl.BlockSpec((t_len, FFN_TC), lambda j, b: (b, j))],
        out_specs=[pl.BlockSpec((2, t_len, FFN_TC), lambda j, b: (0, b, j)),
                   pl.BlockSpec((2, 3, FFN_TC), lambda j, b: (0, 0, j)),
                   pl.BlockSpec((2, 1, FFN_TC), lambda j, b: (0, 0, j))],
        out_shape=[jax.ShapeDtypeStruct((2, n_b * t_len, D_FF), BF16),
                   jax.ShapeDtypeStruct((2, 3, D_FF), F32), jax.ShapeDtypeStruct((2, 1, D_FF), F32)],
        compiler_params=_cparams(("parallel", "arbitrary")),
    )(up, up, conv_w, conv_w, conv_b, conv_b, da)


NN = (((2,), (1,)), ((0,), (0,)))
NT = (((2,), (2,)), ((0,), (0,)))
TN = (((1,), (1,)), ((0,), (0,)))


def _iota3(shape, axis):
    return lax.broadcasted_iota(jnp.int32, shape, axis)


def _dg(a, b, dims):
    return lax.dot_general(a, b, dims, preferred_element_type=F32)


def _dot(a, b):
    return _dg(a, b, NN)


def _dot_nt(a, b):
    return _dg(a, b, NT)


def _dot_tn(a, b):
    return _dg(a, b, TN)


def _split(a):
    hi = a.astype(BF16)
    return hi, (a - hi.astype(F32)).astype(BF16)


def _dg3(a, b, dims):
    ah, al = _split(a)
    bh, bl = _split(b)
    return _dg(ah, bh, dims) + (_dg(ah, bl, dims) + _dg(al, bh, dims))


@jax.custom_vjp
def _dot3(a, b):
    return _dg3(a, b, NN)


def _dot3_fwd(a, b):
    return _dg3(a, b, NN), (a, b)


def _dot3_bwd(res, g):
    a, b = res
    return _dg3(g, b, NT), _dg3(a, g, TN)


_dot3.defvjp(_dot3_fwd, _dot3_bwd)


def _lower_ones(g_n, n):
    shape = (g_n, n, n)
    return jnp.where(_iota3(shape, 1) >= _iota3(shape, 2), 1.0, 0.0).astype(BF16)


@jax.custom_vjp
def _chunk_cumsum(x):
    hi, lo = _split(x)
    tri = _lower_ones(x.shape[0], x.shape[1])
    return _dg(tri, hi, NN) + _dg(tri, lo, NN)


def _chunk_cumsum_fwd(x):
    return _chunk_cumsum(x), None


def _chunk_cumsum_bwd(_, g):
    hi, lo = _split(g)
    tri = _lower_ones(g.shape[0], g.shape[1])
    return (_dg(tri, hi, TN) + _dg(tri, lo, TN),)


_chunk_cumsum.defvjp(_chunk_cumsum_fwd, _chunk_cumsum_bwd)


@jax.custom_vjp
def _unit_lower_inv(m):
    n = m.shape[1]
    p = -m
    a = jnp.where(_iota3(m.shape, 1) == _iota3(m.shape, 2), 1.0, 0.0) + p
    span = 2
    while span < n:
        p = _dg3(p, p, NN)
        a = a + _dg3(a, p, NN)
        span *= 2
    return a


def _unit_lower_inv_fwd(m):
    a = _unit_lower_inv(m)
    return a, a


def _unit_lower_inv_bwd(a, da):
    return (-_dg3(a, _dg3(da, a, NT), TN),)


_unit_lower_inv.defvjp(_unit_lower_inv_fwd, _unit_lower_inv_bwd)


def _rms_gate(o, gn, gate):
    return o * lax.rsqrt(jnp.mean(o * o, -1, keepdims=True) + EPS) * gn * _silu(gate)


def _dn_chains(q, k, v, z, small, s_in, a_log, dt_bias, gn):
    g_n, c_len = q.shape[0], q.shape[1]
    sq = (g_n, c_len, c_len)
    row, col = _iota3(sq, 1), _iota3(sq, 2)
    causal, strict, eye = row >= col, row > col, row == col
    qn = q * lax.rsqrt(jnp.sum(q * q, -1, keepdims=True) + EPS) * (HEAD_DIM ** -0.5)
    kn = k * lax.rsqrt(jnp.sum(k * k, -1, keepdims=True) + EPS)
    lane = _iota3(small.shape, 2)
    head = jnp.bitwise_and(_iota3(small.shape, 0), HEADS - 1)
    la_all = -jnp.exp(a_log) * _softplus(small + dt_bias)
    la_c = jnp.sum(jnp.where(lane == head, la_all, 0.0), 2, keepdims=True)
    beta = jnp.sum(jnp.where(lane == head + HEADS, jax.nn.sigmoid(small), 0.0), 2, keepdims=True)
    la_b = jnp.broadcast_to(la_c, sq)
    la_r = jnp.sum(jnp.where(eye, la_b, 0.0), 1, keepdims=True)
    g_c = jnp.sum(jnp.where(causal, jnp.broadcast_to(la_r, sq), 0.0), 2, keepdims=True)
    g_r = jnp.sum(jnp.where(row <= col, la_b, 0.0), 1, keepdims=True)
    g_last = jnp.sum(la_c, 1, keepdims=True)
    decay = jnp.exp(jnp.where(causal, g_c - g_r, -1e30))
    e_g = jnp.exp(g_c)
    kb = kn * beta
    m_low = jnp.where(strict, _dot_nt(kb, kn) * decay, 0.0)
    a_inv = _unit_lower_inv(m_low)
    u = _dot3(a_inv, v * beta)
    w = _dot3(a_inv, kb * e_g)
    attn = _dot_nt(qn, kn) * decay
    v_new = u - _dot(w, s_in)
    o = _dot(qn * e_g, s_in) + _dot(attn, v_new)
    s_out = s_in * jnp.exp(g_last) + _dot_tn(kn * jnp.exp(g_last - g_c), v_new)
    return _rms_gate(o, gn, z), s_out


def _gla_chains(q, k, v, gate, small, s_in, w2, b2, gn):
    g_n, c_len = q.shape[0], q.shape[1]
    sq, kk = (g_n, c_len, c_len), (g_n, GLA_KEY, GLA_KEY)
    causal = _iota3(sq, 1) >= _iota3(sq, 2)
    la = -_softplus(-(_dot(small, w2) + b2)) * (1.0 / 16.0)
    b = _chunk_cumsum(la)
    b_last = jnp.sum(jnp.where(_iota3(b.shape, 1) == c_len - 1, b, 0.0), 1, keepdims=True)
    q_dec = q * (GLA_KEY ** -0.5) * jnp.exp(b)
    attn = jnp.where(causal, _dot_nt(q_dec, k * jnp.exp(-b)), 0.0)
    o = _dot(q_dec, s_in) + _dot(attn, v)
    g_row = jnp.exp(b_last)
    g_col = jnp.sum(jnp.where(_iota3(kk, 1) == _iota3(kk, 2), jnp.broadcast_to(g_row, kk), 0.0), 2, keepdims=True)
    s_out = s_in * g_col + _dot_tn(k * jnp.exp(b_last - b), v)
    return _rms_gate(o, gn, gate), s_out


def _chunk_spec(n_b, width, col_block, n_c, reverse=False):
    if reverse:
        return pl.BlockSpec((n_b, CHUNK, width), lambda n: (0, n_c - 1 - n, col_block))
    return pl.BlockSpec((n_b, CHUNK, width), lambda n: (0, n, col_block))


def _hist_spec(n_b, d_k, n_c, reverse=False):
    if reverse:
        return pl.BlockSpec((None, n_b * HEADS, d_k, HEAD_DIM), lambda n: (n_c - 1 - n, 0, 0, 0))
    return pl.BlockSpec((None, n_b * HEADS, d_k, HEAD_DIM), lambda n: (n, 0, 0, 0))


def _stack_chains(ref, n_b, slices):
    return jnp.stack([ref[b, :, sl] for b in range(n_b) for sl in slices], axis=0)


def _per_chain(ref, n_b):
    return jnp.stack([ref[b] for b in range(n_b) for _ in range(HEADS)], axis=0)


def _unstack_chains(ref, val, n_b, slices, offset=0):
    for b in range(n_b):
        for h, sl in enumerate(slices):
            ref[b, :, slice(offset + sl.start, offset + sl.stop)] = val[b * HEADS + h].astype(ref.dtype)


def _gate_weights(w2_ref, b2_ref, n_b):
    w2 = jnp.stack([w2_ref[:, ks] for _ in range(n_b) for ks in GLA_KSL], axis=0)
    b2 = jnp.stack([b2_ref[:, ks] for _ in range(n_b) for ks in GLA_KSL], axis=0)
    return w2, b2


def _sum_heads(val, n_b):
    return [sum(val[b * HEADS + h] for h in range(HEADS)) for b in range(n_b)]


def _const_spec(shape):
    return pl.BlockSpec(shape, lambda n: (0,) * len(shape))


DN_SL = [slice(h * HEAD_DIM, (h + 1) * HEAD_DIM) for h in range(HEADS)]
GLA_KSL = [slice(h * GLA_KEY, (h + 1) * GLA_KEY) for h in range(HEADS)]


class Rider:
    def __init__(self, inputs, out_shapes, sems, first, last):
        self.inputs, self.out_shapes, self.sems, self.first, self.last = inputs, out_shapes, sems, first, last


def _with_rider(rider, n_in, n_out, n_scratch):
    if rider is None:
        return [], [], [], [], lambda refs: (refs, None)
    r_in, r_out, r_sem = len(rider.inputs), len(rider.out_shapes), len(rider.sems)

    def split(refs):
        own_in, rest = refs[:n_in], refs[n_in:]
        rid_in, rest = rest[:r_in], rest[r_in:]
        own_out, rest = rest[:n_out], rest[n_out:]
        rid_out, rest = rest[:r_out], rest[r_out:]
        own_scr, rid_sem = rest[:n_scratch], rest[n_scratch:]
        return own_in + own_out + own_scr, (rid_in, rid_out, rid_sem)

    return list(rider.inputs), [HBM_SPEC] * r_in, [HBM_SPEC] * r_out, list(rider.sems), split


def _ride(rider, parts, n_c):
    if rider is None:
        return None, None

    def first():
        pl.when(pl.program_id(0) == 0)(lambda: rider.first(*parts))

    def last():
        pl.when(pl.program_id(0) == n_c - 1)(lambda: rider.last(*parts))

    return first, last


def _dn_scan_fwd(qkv, proj, a_log, dt_bias, gn, n_b, t_len, rider=None):
    n_c = t_len // CHUNK
    spec = functools.partial(_chunk_spec, n_b, n_c=n_c)
    r_inputs, r_in_specs, r_out_specs, r_sems, split = _with_rider(rider, 8, 2, 1)

    def body(*refs):
        (q_ref, k_ref, v_ref, z_ref, sm_ref, al_ref, dt_ref, gn_ref, o_ref, hist_ref, s_ref), parts = split(refs)
        ride_first, ride_last = _ride(rider, parts, n_c)
        if rider is not None:
            ride_first()

        @pl.when(pl.program_id(0) == 0)
        def _():
            s_ref[...] = jnp.zeros_like(s_ref)

        s_in = s_ref[...]
        hist_ref[...] = s_in
        og, s_out = _dn_chains(*(_stack_chains(r, n_b, DN_SL) for r in (q_ref, k_ref, v_ref, z_ref)),
                               _per_chain(sm_ref, n_b), s_in, al_ref[...], dt_ref[...], gn_ref[...])
        _unstack_chains(o_ref, og, n_b, DN_SL)
        s_ref[...] = s_out
        if rider is not None:
            ride_last()

    qkv3, proj3 = qkv.reshape(n_b, t_len, -1), proj.reshape(n_b, t_len, -1)
    o, hist, *rider_outs = pl.pallas_call(
        body, name="dn_scan_fwd", grid=(n_c,),
        in_specs=[spec(512, 0), spec(512, 1), spec(512, 2), spec(512, OFF_Z // 512), spec(128, OFF_SMALL // 128),
                  _const_spec((1, 128)), _const_spec((1, 128)), _const_spec((1, 128))] + r_in_specs,
        out_specs=[spec(512, 0), _hist_spec(n_b, HEAD_DIM, n_c)] + r_out_specs,
        out_shape=[jax.ShapeDtypeStruct((n_b, t_len, 2 * 512), BF16),
                   jax.ShapeDtypeStruct((n_c, n_b * HEADS, HEAD_DIM, HEAD_DIM), F32)]
        + (list(rider.out_shapes) if rider else []),
        scratch_shapes=[pltpu.VMEM((n_b * HEADS, HEAD_DIM, HEAD_DIM), F32)] + r_sems,
        compiler_params=_cparams(("arbitrary",)),
    )(qkv3, qkv3, qkv3, proj3, proj3, a_log, dt_bias, gn, *r_inputs)
    return o, hist, rider_outs


def _dn_scan_bwd(qkv, proj, a_log, dt_bias, gn, hist, d_o, n_b, t_len, rider=None):
    n_c = t_len // CHUNK
    rev = functools.partial(_chunk_spec, n_b, n_c=n_c, reverse=True)
    r_inputs, r_in_specs, r_out_specs, r_sems, split = _with_rider(rider, 10, 6, 1)

    def body(*refs):
        (q_ref, k_ref, v_ref, z_ref, sm_ref, al_ref, dt_ref, gn_ref, hist_ref, do_ref,
         dqkv_ref, dz_ref, dsm_ref, dal_ref, ddt_ref, dgn_ref, ds_ref), parts = split(refs)
        ride_first, ride_last = _ride(rider, parts, n_c)
        if rider is not None:
            ride_first()
        first = pl.program_id(0) == 0

        @pl.when(first)
        def _():
            ds_ref[...] = jnp.zeros_like(ds_ref)

        _, pull = jax.vjp(_dn_chains, *(_stack_chains(r, n_b, DN_SL) for r in (q_ref, k_ref, v_ref, z_ref)),
                          _per_chain(sm_ref, n_b), hist_ref[...], al_ref[...], dt_ref[...], gn_ref[...])
        dq, dk, dv, dz, dsm, ds_in, dal, ddt, dgn = pull((_stack_chains(do_ref, n_b, DN_SL), ds_ref[...]))
        _unstack_chains(dqkv_ref, dq, n_b, DN_SL)
        _unstack_chains(dqkv_ref, dk, n_b, DN_SL, offset=512)
        _unstack_chains(dqkv_ref, dv, n_b, DN_SL, offset=1024)
        _unstack_chains(dz_ref, dz, n_b, DN_SL)
        ds_ref[...] = ds_in
        for b, dsm_b in enumerate(_sum_heads(dsm, n_b)):
            dsm_ref[b] = dsm_b
        _acc(dal_ref, dal, first)
        _acc(ddt_ref, ddt, first)
        _acc(dgn_ref, dgn, first)
        if rider is not None:
            ride_last()

    qkv3, proj3, do3 = (a.reshape(n_b, t_len, -1) for a in (qkv, proj, d_o))
    vec = jax.ShapeDtypeStruct((1, 128), F32)
    dqkv, d_proj, dsm, dal, ddt, dgn, *rider_outs = pl.pallas_call(
        body, name="dn_scan_bwd", grid=(n_c,),
        in_specs=[rev(512, 0), rev(512, 1), rev(512, 2), rev(512, OFF_Z // 512), rev(128, OFF_SMALL // 128),
                  _const_spec((1, 128)), _const_spec((1, 128)), _const_spec((1, 128)),
                  _hist_spec(n_b, HEAD_DIM, n_c, reverse=True), rev(512, 0)] + r_in_specs,
        out_specs=[rev(1536, 0), rev(512, OFF_Z // 512), rev(128, 0),
                   _const_spec((1, 128)), _const_spec((1, 128)), _const_spec((1, 128))] + r_out_specs,
        out_shape=[jax.ShapeDtypeStruct((n_b, t_len, 1536), F32), jax.ShapeDtypeStruct((n_b, t_len, PROJ_W), BF16),
                   jax.ShapeDtypeStruct((n_b, t_len, 128), F32), vec, vec, vec]
        + (list(rider.out_shapes) if rider else []),
        scratch_shapes=[pltpu.VMEM((n_b * HEADS, HEAD_DIM, HEAD_DIM), F32)] + r_sems,
        compiler_params=_cparams(("arbitrary",)),
    )(qkv3, qkv3, qkv3, proj3, proj3, a_log, dt_bias, gn, hist, do3, *r_inputs)
    return dqkv.reshape(n_b * t_len, 1536), d_proj, dsm, dal, ddt, dgn, rider_outs


def _gla_scan_fwd(proj, w2, b2, gn, o_mix, n_b, t_len):
    n_c = t_len // CHUNK
    spec = functools.partial(_chunk_spec, n_b, n_c=n_c)

    def body(q_ref, k_ref, v_ref, g_ref, sm_ref, w2_ref, b2_ref, gn_ref, _, o_ref, hist_ref, s_ref):
        @pl.when(pl.program_id(0) == 0)
        def _():
            s_ref[...] = jnp.zeros_like(s_ref)

        s_in = s_ref[...]
        hist_ref[...] = s_in
        og, s_out = _gla_chains(_stack_chains(q_ref, n_b, GLA_KSL), _stack_chains(k_ref, n_b, GLA_KSL),
                                _stack_chains(v_ref, n_b, DN_SL), _stack_chains(g_ref, n_b, DN_SL),
                                _per_chain(sm_ref, n_b), s_in, *_gate_weights(w2_ref, b2_ref, n_b), gn_ref[...])
        _unstack_chains(o_ref, og, n_b, DN_SL)
        s_ref[...] = s_out

    proj3 = proj.reshape(n_b, t_len, -1)
    o, hist = pl.pallas_call(
        body, name="gla_scan_fwd", grid=(n_c,),
        in_specs=[spec(256, OFF_GQ // 256), spec(256, OFF_GK // 256), spec(512, OFF_GV // 512),
                  spec(512, OFF_GG // 512), spec(128, OFF_SMALL // 128),
                  _const_spec((128, 256)), _const_spec((1, 256)), _const_spec((1, 128)),
                  pl.BlockSpec(memory_space=pl.ANY)],
        out_specs=[spec(512, 1), _hist_spec(n_b, GLA_KEY, n_c)],
        out_shape=[jax.ShapeDtypeStruct(o_mix.shape, BF16),
                   jax.ShapeDtypeStruct((n_c, n_b * HEADS, GLA_KEY, HEAD_DIM), F32)],
        input_output_aliases={8: 0},
        scratch_shapes=[pltpu.VMEM((n_b * HEADS, GLA_KEY, HEAD_DIM), F32)],
        compiler_params=_cparams(("arbitrary",)),
    )(proj3, proj3, proj3, proj3, proj3, w2, b2, gn, o_mix)
    return o.reshape(n_b * t_len, 2 * 512), hist


def _gla_scan_bwd(proj, w2, b2, gn, hist, d_o, dsm_dn, d_proj, n_b, t_len):
    n_c = t_len // CHUNK
    rev = functools.partial(_chunk_spec, n_b, n_c=n_c, reverse=True)

    def body(q_ref, k_ref, v_ref, g_ref, sm_ref, w2_ref, b2_ref, gn_ref, hist_ref, do_ref, dsm_dn_ref, _,
             dp_ref, dw2_ref, db2_ref, dgn_ref, ds_ref):
        first = pl.program_id(0) == 0

        @pl.when(first)
        def _():
            ds_ref[...] = jnp.zeros_like(ds_ref)

        _, pull = jax.vjp(_gla_chains, _stack_chains(q_ref, n_b, GLA_KSL), _stack_chains(k_ref, n_b, GLA_KSL),
                          _stack_chains(v_ref, n_b, DN_SL), _stack_chains(g_ref, n_b, DN_SL),
                          _per_chain(sm_ref, n_b), hist_ref[...], *_gate_weights(w2_ref, b2_ref, n_b), gn_ref[...])
        dq, dk, dv, dg, dsm, ds_in, dw2, db2, dgn = pull((_stack_chains(do_ref, n_b, DN_SL), ds_ref[...]))
        _unstack_chains(dp_ref, dq, n_b, GLA_KSL, offset=OFF_GQ)
        _unstack_chains(dp_ref, dk, n_b, GLA_KSL, offset=OFF_GK)
        _unstack_chains(dp_ref, dv, n_b, DN_SL, offset=OFF_GV)
        _unstack_chains(dp_ref, dg, n_b, DN_SL, offset=OFF_GG)
        ds_ref[...] = ds_in
        for b, dsm_b in enumerate(_sum_heads(dsm, n_b)):
            dp_ref[b, :, OFF_SMALL:OFF_SMALL + 128] = (dsm_b + dsm_dn_ref[b]).astype(BF16)
            dp_ref[b, :, OFF_SMALL + 128:GLA_W] = jnp.zeros((CHUNK, GLA_W - OFF_SMALL - 128), BF16)
        for h, ks in enumerate(GLA_KSL):
            _acc(dw2_ref, sum(dw2[b * HEADS + h] for b in range(n_b)), first, at=(slice(None), ks))
            _acc(db2_ref, sum(db2[b * HEADS + h] for b in range(n_b)), first, at=(slice(None), ks))
        _acc(dgn_ref, dgn, first)

    proj3, do3 = proj.reshape(n_b, t_len, -1), d_o.reshape(n_b, t_len, -1)
    return pl.pallas_call(
        body, name="gla_scan_bwd", grid=(n_c,),
        in_specs=[rev(256, OFF_GQ // 256), rev(256, OFF_GK // 256), rev(512, OFF_GV // 512), rev(512, OFF_GG // 512),
                  rev(128, OFF_SMALL // 128),
                  _const_spec((128, 256)), _const_spec((1, 256)), _const_spec((1, 128)),
                  _hist_spec(n_b, GLA_KEY, n_c, reverse=True), rev(512, 1), rev(128, 0),
                  pl.BlockSpec(memory_space=pl.ANY)],
        out_specs=[rev(GLA_W, 0), _const_spec((128, 256)), _const_spec((1, 256)), _const_spec((1, 128))],
        out_shape=[jax.ShapeDtypeStruct(d_proj.shape, BF16), jax.ShapeDtypeStruct((128, 256), F32),
                   jax.ShapeDtypeStruct((1, 256), F32), jax.ShapeDtypeStruct((1, 128), F32)],
        input_output_aliases={11: 0},
        scratch_shapes=[pltpu.VMEM((n_b * HEADS, GLA_KEY, HEAD_DIM), F32)],
        compiler_params=_cparams(("arbitrary",)),
    )(proj3, proj3, proj3, proj3, proj3, w2, b2, gn, hist, do3, dsm_dn, d_proj)


def _pad_w_in(w_in):
    zeros = jnp.zeros((w_in.shape[0], GLA_W - OFF_SMALL - 8 - GATE_RANK), w_in.dtype)
    return jnp.concatenate([w_in[:, 2056:3592], w_in[:, 2048:2056], w_in[:, 3592:3608], zeros,
                            w_in[:, 0:256], w_in[:, 1536:2048], w_in[:, 256:1536]], axis=1)


def _unpad_w_in(g):
    return jnp.concatenate([g[:, GLA_W:OFF_Z], g[:, OFF_Z + 512:PROJ_W], g[:, OFF_Z:OFF_Z + 512],
                            g[:, OFF_SMALL:OFF_SMALL + 8], g[:, 0:OFF_SMALL],
                            g[:, OFF_SMALL + 8:OFF_SMALL + 8 + GATE_RANK]], axis=1)


def _lane_vec(v, offset=0):
    return jnp.zeros((1, 128), F32).at[0, offset:offset + v.shape[0]].set(v)


def _local_step(x, tgt, mod, p, n_b, t_len, comm=None):
    row1 = lambda v: v.reshape(1, -1)
    a_log, dt_bias = _lane_vec(p["dn_a_log"]), _lane_vec(p["dn_dt_bias"])
    dn_gn, gla_gn = row1(p["dn_norm_g"]), row1(p["gla_norm_g"])
    w2 = jnp.zeros((128, 256), F32).at[8:8 + GATE_RANK].set(p["gla_w_gate2"])
    b2 = row1(p["gla_b_gate"])
    ln0_g, ln0_b, ln1_g, ln1_b, ln2_g, ln2_b = (row1(p[k]) for k in ("ln0_g", "ln0_b", "ln1_g", "ln1_b", "ln2_g", "ln2_b"))
    conv_b = row1(p["ffn_conv_b"])

    x0, h1 = _ln0_fwd(x, ln0_g, ln0_b, mod, n_b, t_len)
    proj = _mm(h1, p["w_in_p"], name="mm_proj")
    qkv = _dn_pre_fwd(proj, p["dn_conv"], n_b, t_len)
    o_half, hist_dn, landed = _dn_scan_fwd(qkv, proj, a_log, dt_bias, dn_gn, n_b, t_len,
                                           rider=comm.fwd_rider() if comm else None)
    if comm:
        p = {**p, **comm.weights_from(landed)}
    o_mix, hist_gla = _gla_scan_fwd(proj, w2, b2, gla_gn, o_half, n_b, t_len)
    y = _mm(o_mix, p["w_o"], name="mm_wo")
    x1, h2 = _ln1_fwd(x0, y, ln1_g, ln1_b, mod, n_b, t_len)
    up = _mm(h2, p["w_up"], name="mm_up")
    act = _ffn_act_fwd(up, p["ffn_conv"], conv_b, n_b, t_len)
    y2 = _mm(act, p["w_down"], name="mm_down")

    loss, dx1, dy2, g_ln2_g, g_ln2_b, dgt_f = _ln2_loss_bwd(x1, y2, ln2_g, ln2_b, mod, tgt, n_b, t_len)
    g_w_down = _mm(act, dy2, ta=True, name="mm_g_down")
    d_act = _mm(dy2, p["w_down"], tb=True, name="mm_d_act")
    d_up, g_ffn_conv, g_conv_b = _ffn_act_bwd(up, p["ffn_conv"], conv_b, d_act, n_b, t_len)
    g_w_up = _mm(h2, d_up, ta=True, out_slabs=N_CHIPS, name="mm_g_up")
    dh2 = _mm(d_up, p["w_up"], tb=True, name="mm_d_h2")
    dx0, dy, g_ln1_g, g_ln1_b, dmod_1 = _ln1_bwd(x0, y, ln1_g, ln1_b, mod, dx1, dh2, n_b, t_len)
    g_w_o = _mm(o_mix, dy, ta=True, name="mm_g_wo")
    d_o = _mm(dy, p["w_o"], tb=True, name="mm_d_o")
    dqkv, d_proj, dsm_dn, g_a_log, g_dt_bias, g_dn_gn, from_chips = _dn_scan_bwd(
        qkv, proj, a_log, dt_bias, dn_gn, hist_dn, d_o, n_b, t_len,
        rider=comm.bwd_rider(g_w_o, g_w_up, g_w_down) if comm else None)
    d_proj, g_w2, g_b2, g_gla_gn = _gla_scan_bwd(proj, w2, b2, gla_gn, hist_gla, d_o, dsm_dn, d_proj, n_b, t_len)
    d_proj, g_dn_conv = _dn_pre_bwd(proj, p["dn_conv"], dqkv, d_proj.reshape(n_b * t_len, PROJ_W), n_b, t_len)
    g_w_in_p = _mm(h1, d_proj, ta=True, name="mm_g_win")
    dh1 = _mm(d_proj, p["w_in_p"], tb=True, name="mm_d_h1")
    grad_x, g_ln0_g, g_ln0_b, dmod_0 = _ln0_bwd(x, ln0_g, ln0_b, mod, dx0, dh1, n_b, t_len)

    dmod = jnp.concatenate([dmod_0, dmod_1[:, 0:1], dmod_1[:, 1:3], dgt_f], axis=1)
    grads = {
        "ln0_g": g_ln0_g[0], "ln0_b": g_ln0_b[0], "w_in_p": g_w_in_p, "dn_conv": g_dn_conv,
        "dn_a_log": g_a_log[0, 0:HEADS], "dn_dt_bias": g_dt_bias[0, 0:HEADS], "dn_norm_g": g_dn_gn[0],
        "gla_w_gate2": g_w2[8:8 + GATE_RANK], "gla_b_gate": g_b2[0], "gla_norm_g": g_gla_gn[0],
        "w_o": g_w_o, "ln1_g": g_ln1_g[0], "ln1_b": g_ln1_b[0], "w_up": g_w_up,
        "ffn_conv": jnp.concatenate([g_ffn_conv[0], g_ffn_conv[1]], axis=1),
        "ffn_conv_b": jnp.concatenate([g_conv_b[0, 0], g_conv_b[1, 0]]), "w_down": g_w_down,
        "ln2_g": g_ln2_g[0], "ln2_b": g_ln2_b[0],
    }
    return loss, grad_x, grads, dmod, from_chips


def _ada_fwd(c_all, w_shard, b_shard):
    n_all, n_col = c_all.shape[0], w_shard.shape[1]
    tn = 512

    def body(c_ref, w_ref, b_ref, cond_ref, mod_ref):
        cond = _silu(c_ref[...])
        cond_ref[...] = cond
        mod_ref[...] = jnp.dot(cond.astype(BF16), w_ref[...].astype(BF16), preferred_element_type=F32) + b_ref[...]

    return pl.pallas_call(
        body, name="ada_fwd", grid=(n_col // tn,),
        in_specs=[pl.BlockSpec((n_all, D_MODEL), lambda j: (0, 0)), pl.BlockSpec((D_MODEL, tn), lambda j: (0, j)),
                  pl.BlockSpec((1, tn), lambda j: (0, j))],
        out_specs=[pl.BlockSpec((n_all, D_MODEL), lambda j: (0, 0)), pl.BlockSpec((n_all, tn), lambda j: (0, j))],
        out_shape=[jax.ShapeDtypeStruct((n_all, D_MODEL), F32), jax.ShapeDtypeStruct((n_all, n_col), F32)],
        compiler_params=_cparams(("arbitrary",)),
    )(c_all, w_shard, b_shard)


def _col_sum(a):
    def body(a_ref, o_ref):
        o_ref[...] = jnp.sum(a_ref[...], 0, keepdims=True)

    return pl.pallas_call(body, name="col_sum", out_shape=jax.ShapeDtypeStruct((1, a.shape[1]), F32))(a)


def _adamw(w, g, m, v, name):
    n_r, n_c = w.shape
    tr = _pick(n_r, (256, 64, 32, 16, 8))

    def body(w_ref, g_ref, m_ref, v_ref, d_ref, nm_ref, nv_ref):
        grad = g_ref[...]
        new_m = ADAM_B1 * m_ref[...] + (1.0 - ADAM_B1) * grad
        new_v = ADAM_B2 * v_ref[...] + (1.0 - ADAM_B2) * (grad * grad)
        m_hat = new_m / (1.0 - ADAM_B1 ** ADAM_STEP)
        v_hat = new_v / (1.0 - ADAM_B2 ** ADAM_STEP)
        d_ref[...] = -ADAM_LR * (m_hat / (jnp.sqrt(v_hat) + ADAM_EPS) + ADAM_WD * w_ref[...])
        nm_ref[...] = new_m
        nv_ref[...] = new_v

    blk = pl.BlockSpec((tr, n_c), lambda i: (i, 0))
    out = jax.ShapeDtypeStruct(w.shape, F32)
    return pl.pallas_call(
        body, name=name, grid=(n_r // tr,), in_specs=[blk] * 4, out_specs=[blk] * 3, out_shape=[out] * 3,
        compiler_params=_cparams(("parallel",)),
    )(w, g, m, v)


HBM_SPEC = pl.BlockSpec(memory_space=pltpu.HBM)
VMEM_SPEC = pl.BlockSpec(memory_space=pltpu.VMEM)
CHIP_FLIPS = ((1, 0), (0, 1), (1, 1))


def _place():
    return lax.axis_index("x"), lax.axis_index("y"), lax.axis_index("c")


def _flip(v, f):
    return 1 - v if f else v


def _all_gather8(slab, name):
    n_r, n_w = slab.shape

    def body(x_ref, o_ref, s_ref, send_sems, recv_sems, local_sem):
        x, y, c = _place()
        me = 4 * x + 2 * y + c
        mine = pltpu.make_async_copy(x_ref, o_ref.at[me], local_sem)
        mine.start()
        peers = [(_flip(x, k & 4), _flip(y, k & 2), _flip(c, k & 1)) for k in range(1, N_DEV)]
        sends = []
        for k, peer in enumerate(peers):
            cp = pltpu.make_async_remote_copy(src_ref=x_ref, dst_ref=o_ref.at[me], send_sem=send_sems.at[k],
                                              recv_sem=recv_sems.at[k], device_id=peer, device_id_type=MESH)
            cp.start()
            sends.append(cp)
        for k, (px, py, pc) in enumerate(peers):
            pltpu.make_async_remote_copy(src_ref=x_ref, dst_ref=o_ref.at[4 * px + 2 * py + pc],
                                         send_sem=send_sems.at[k], recv_sem=recv_sems.at[k],
                                         device_id=(px, py, pc), device_id_type=MESH).wait_recv()
        for cp in sends:
            cp.wait_send()
        mine.wait()
        total = o_ref[0]
        for d in range(1, N_DEV):
            total = total + o_ref[d]
        s_ref[...] = total

    return pl.pallas_call(
        body, name=name, in_specs=[VMEM_SPEC], out_specs=[VMEM_SPEC, VMEM_SPEC],
        out_shape=[jax.ShapeDtypeStruct((N_DEV, n_r, n_w), F32), jax.ShapeDtypeStruct((n_r, n_w), F32)],
        scratch_shapes=[pltpu.SemaphoreType.DMA((N_DEV - 1,)), pltpu.SemaphoreType.DMA((N_DEV - 1,)),
                        pltpu.SemaphoreType.DMA],
    )(slab)


def _gather_weights(shards):
    n_a = len(shards)

    def body(*refs):
        ins, outs, stage = refs[:n_a], refs[n_a:2 * n_a], refs[2 * n_a:3 * n_a]
        send_sems, recv_sems, local_sems = refs[3 * n_a:]
        x, y, c = _place()
        me_chip = 2 * x + y
        sibling = (x, y, 1 - c)
        chips = [(_flip(x, fx), _flip(y, fy)) for fx, fy in CHIP_FLIPS]
        stage_in = [pltpu.make_async_copy(ins[k], stage[k], local_sems.at[k]) for k in range(n_a)]
        for cp in stage_in:
            cp.start()

        def copy(k, slot, chip_of_block, half, to, src=None):
            dst = outs[k].at[chip_of_block, half]
            return pltpu.make_async_remote_copy(src_ref=dst if src is None else src, dst_ref=dst,
                                                send_sem=send_sems.at[k * 6 + slot], recv_sem=recv_sems.at[k * 6 + slot],
                                                device_id=to, device_id_type=MESH)

        first = [copy(k, r, me_chip, c, (*chips[r], c), src=ins[k].at[c]) for k in range(n_a) for r in range(3)]
        for cp in first:
            cp.start()
        stage_out = []
        for k in range(n_a):
            stage_in[k].wait()
            cp = pltpu.make_async_copy(stage[k], outs[k].at[me_chip], local_sems.at[n_a + k])
            cp.start()
            stage_out.append(cp)
        passed = []
        for k in range(n_a):
            for r, (px, py) in enumerate(chips):
                copy(k, r, 2 * px + py, c, (x, y, c)).wait_recv()
                fwd = copy(k, 3 + r, 2 * px + py, c, sibling)
                fwd.start()
                passed.append(fwd)
        for k in range(n_a):
            for r, (px, py) in enumerate(chips):
                copy(k, 3 + r, 2 * px + py, 1 - c, (x, y, c)).wait_recv()
        for cp in first + passed:
            cp.wait_send()
        for cp in stage_out:
            cp.wait()

    return pl.pallas_call(
        body, name="gather_weights", in_specs=[HBM_SPEC] * n_a, out_specs=[HBM_SPEC] * n_a,
        out_shape=[jax.ShapeDtypeStruct((N_CHIPS,) + s.shape, s.dtype) for s in shards],
        scratch_shapes=[pltpu.VMEM(s.shape, s.dtype) for s in shards]
        + [pltpu.SemaphoreType.DMA((6 * n_a,)), pltpu.SemaphoreType.DMA((6 * n_a,)),
           pltpu.SemaphoreType.DMA((2 * n_a,))],
        compiler_params=pltpu.CompilerParams(vmem_limit_bytes=VMEM_LIMIT),
    )(*shards)


def _gather_rider(shards):
    n_a = len(shards)

    def plan(ins, outs, sems):
        send_sems, recv_sems = sems
        x, y, c = _place()
        chips = [(_flip(x, fx), _flip(y, fy)) for fx, fy in CHIP_FLIPS]

        def copy(k, slot, chip_of_block, half, to, src=None):
            dst = outs[k].at[chip_of_block, half]
            return pltpu.make_async_remote_copy(src_ref=dst if src is None else src, dst_ref=dst,
                                                send_sem=send_sems.at[k * 6 + slot], recv_sem=recv_sems.at[k * 6 + slot],
                                                device_id=to, device_id_type=MESH)

        first = [copy(k, r, 2 * x + y, c, (*chips[r], c), src=ins[k].at[c]) for k in range(n_a) for r in range(3)]
        return copy, chips, first, (x, y, c)

    def first_step(ins, outs, sems):
        for cp in plan(ins, outs, sems)[2]:
            cp.start()

    def last_step(ins, outs, sems):
        copy, chips, first, (x, y, c) = plan(ins, outs, sems)
        passed = []
        for k in range(n_a):
            for r, (px, py) in enumerate(chips):
                copy(k, r, 2 * px + py, c, (x, y, c)).wait_recv()
                fwd = copy(k, 3 + r, 2 * px + py, c, (x, y, 1 - c))
                fwd.start()
                passed.append(fwd)
        for k in range(n_a):
            for r, (px, py) in enumerate(chips):
                copy(k, 3 + r, 2 * px + py, 1 - c, (x, y, c)).wait_recv()
        for cp in first + passed:
            cp.wait_send()

    return Rider(shards, [jax.ShapeDtypeStruct((N_CHIPS,) + s.shape, s.dtype) for s in shards],
                 [pltpu.SemaphoreType.DMA((6 * n_a,)), pltpu.SemaphoreType.DMA((6 * n_a,))], first_step, last_step)


def _place_own(gathered, shard, chip, name):
    _, _, n_h, n_c = gathered.shape
    th = _pick(n_h, (256, 176, 128))

    def body(sel_ref, s_ref, _, o_ref):
        o_ref[...] = s_ref[...]

    grid_spec = pltpu.PrefetchScalarGridSpec(
        num_scalar_prefetch=1, grid=(2, n_h // th),
        in_specs=[pl.BlockSpec((None, th, n_c), lambda hf, i, sel: (hf, i, 0)), pl.BlockSpec(memory_space=pl.ANY)],
        out_specs=pl.BlockSpec((None, None, th, n_c), lambda hf, i, sel: (sel[0], hf, i, 0)))
    return pl.pallas_call(
        body, name=name, grid_spec=grid_spec, out_shape=jax.ShapeDtypeStruct(gathered.shape, gathered.dtype),
        input_output_aliases={2: 0}, compiler_params=_cparams(("parallel", "parallel")),
    )(chip.reshape(1), shard, gathered)


def _rs_pair(parts, name):
    n_a = len(parts)

    def body(*refs):
        ins, outs = refs[:n_a], refs[n_a:2 * n_a]
        send_sems, recv_sems = refs[2 * n_a:]
        x, y, c = _place()
        cps = [pltpu.make_async_remote_copy(src_ref=ins[k].at[:, 1 - c], dst_ref=outs[k], send_sem=send_sems.at[k],
                                            recv_sem=recv_sems.at[k], device_id=(x, y, 1 - c), device_id_type=MESH)
               for k in range(n_a)]
        for cp in cps:
            cp.start()
        for cp in cps:
            cp.wait()

    return pl.pallas_call(
        body, name=name, in_specs=[HBM_SPEC] * n_a, out_specs=[HBM_SPEC] * n_a,
        out_shape=[jax.ShapeDtypeStruct((N_CHIPS,) + p.shape[2:], F32) for p in parts],
        scratch_shapes=[pltpu.SemaphoreType.DMA((n_a,)), pltpu.SemaphoreType.DMA((n_a,))],
    )(*parts)


def _chips_rider(sums):
    n_a = len(sums)

    def plan(ins, outs, sems):
        send_sems, recv_sems = sems
        x, y, c = _place()
        cps = []
        for k in range(n_a):
            for r, (fx, fy) in enumerate(CHIP_FLIPS):
                px, py = _flip(x, fx), _flip(y, fy)
                cps.append(pltpu.make_async_remote_copy(
                    src_ref=ins[k].at[2 * px + py], dst_ref=outs[k].at[r], send_sem=send_sems.at[3 * k + r],
                    recv_sem=recv_sems.at[3 * k + r], device_id=(px, py, c), device_id_type=MESH))
        return cps

    def first_step(ins, outs, sems):
        for cp in plan(ins, outs, sems):
            cp.start()

    def last_step(ins, outs, sems):
        for cp in plan(ins, outs, sems):
            cp.wait()

    return Rider(sums, [jax.ShapeDtypeStruct((3,) + s.shape[1:], s.dtype) for s in sums],
                 [pltpu.SemaphoreType.DMA((3 * n_a,)), pltpu.SemaphoreType.DMA((3 * n_a,))], first_step, last_step)


def _rs_chips(sums, name):
    rider = _chips_rider(sums)
    n_a = len(sums)

    def body(*refs):
        parts = (refs[:n_a], refs[n_a:2 * n_a], refs[2 * n_a:])
        rider.first(*parts)
        rider.last(*parts)

    return pl.pallas_call(
        body, name=name, in_specs=[HBM_SPEC] * n_a, out_specs=[HBM_SPEC] * n_a,
        out_shape=rider.out_shapes, scratch_shapes=rider.sems,
    )(*sums)


def _rs_share(bufs):
    n_a = len(bufs)

    def body(*refs):
        ins, outs = refs[:n_a], refs[n_a:2 * n_a]
        send_sems, recv_sems = refs[2 * n_a:]
        x, y, c = _place()
        sends = [pltpu.make_async_remote_copy(src_ref=ins[k].at[c], dst_ref=outs[k].at[c], send_sem=send_sems.at[k],
                                              recv_sem=recv_sems.at[k], device_id=(x, y, 1 - c), device_id_type=MESH)
                 for k in range(n_a)]
        for cp in sends:
            cp.start()
        for k in range(n_a):
            pltpu.make_async_remote_copy(src_ref=ins[k].at[c], dst_ref=outs[k].at[1 - c], send_sem=send_sems.at[k],
                                         recv_sem=recv_sems.at[k], device_id=(x, y, 1 - c),
                                         device_id_type=MESH).wait_recv()
        for cp in sends:
            cp.wait_send()

    return pl.pallas_call(
        body, name="rs_share", in_specs=[HBM_SPEC] * n_a, out_specs=[HBM_SPEC] * n_a,
        out_shape=[jax.ShapeDtypeStruct(s.shape, F32) for s in bufs],
        input_output_aliases={k: k for k in range(n_a)},
        scratch_shapes=[pltpu.SemaphoreType.DMA((n_a,)), pltpu.SemaphoreType.DMA((n_a,))],
    )(*bufs)


def _pair_add(part, recv, core, name):
    _, _, n_h, n_c = part.shape
    th = _pick(n_h, (256, 176, 128))

    def body(sel_ref, p_ref, r_ref, o_ref):
        o_ref[...] = (p_ref[...] + r_ref[...]).astype(BF16)

    grid_spec = pltpu.PrefetchScalarGridSpec(
        num_scalar_prefetch=1, grid=(N_CHIPS, n_h // th),
        in_specs=[pl.BlockSpec((None, None, th, n_c), lambda j, i, sel: (j, sel[0], i, 0)),
                  pl.BlockSpec((None, th, n_c), lambda j, i, sel: (j, i, 0))],
        out_specs=pl.BlockSpec((None, th, n_c), lambda j, i, sel: (j, i, 0)))
    return pl.pallas_call(
        body, name=name, grid_spec=grid_spec, out_shape=jax.ShapeDtypeStruct(recv.shape, BF16),
        compiler_params=_cparams(("parallel", "parallel")),
    )(core.reshape(1), part, recv)


def _chip_add(sums, recv, chip, core, name):
    _, n_h, n_c = sums.shape
    th = _pick(n_h, (256, 176, 128))

    def body(sel_ref, s_ref, r_ref, o_ref):
        total = s_ref[...].astype(F32)
        for r in range(3):
            total = total + r_ref[r].astype(F32)
        o_ref[...] = total

    grid_spec = pltpu.PrefetchScalarGridSpec(
        num_scalar_prefetch=1, grid=(n_h // th,),
        in_specs=[pl.BlockSpec((None, th, n_c), lambda i, sel: (sel[0], i, 0)),
                  pl.BlockSpec((3, th, n_c), lambda i, sel: (0, i, 0))],
        out_specs=pl.BlockSpec((None, th, n_c), lambda i, sel: (sel[1], i, 0)))
    return pl.pallas_call(
        body, name=name, grid_spec=grid_spec, out_shape=jax.ShapeDtypeStruct((2, n_h, n_c), F32),
        compiler_params=_cparams(("parallel",)),
    )(jnp.stack([chip, core]), sums, recv)


def _row_halves(a):
    return a.reshape(N_CHIPS, 2, -1, a.shape[-1])


def _by_cols(a, n_cols):
    return a.reshape(a.shape[0], N_CHIPS, n_cols).transpose(1, 0, 2)


class StepComm:
    REST = ("w_o", "w_up", "w_down")

    def __init__(self, core, chip, rest_shards):
        self.core, self.chip, self.shards = core, chip, rest_shards

    def fwd_rider(self):
        return _gather_rider(self.shards)

    def weights_from(self, landed):
        g_o, g_up, g_down = (_place_own(g, s, self.chip, "place_own_" + n)
                             for g, s, n in zip(landed, self.shards, self.REST))
        return {"w_o": g_o.reshape(-1, D_MODEL), "w_up": g_up.reshape(N_CHIPS, -1, g_up.shape[-1]),
                "w_down": g_down.reshape(-1, D_MODEL)}

    def _pair_sums(self, parts, names, tag):
        from_sibling = _rs_pair(parts, "rs_pair_" + tag)
        return [_pair_add(p, r, self.core, "pair_add_" + n) for p, r, n in zip(parts, from_sibling, names)]

    def bwd_rider(self, g_w_o, g_w_up, g_w_down):
        self.rest_sums = self._pair_sums([_row_halves(g) for g in (g_w_o, g_w_up, g_w_down)], self.REST, "rest")
        return _chips_rider(self.rest_sums)

    def finish(self, rest_from_chips, g_w_in_p, in_cols):
        in_sums = self._pair_sums([_row_halves(_by_cols(_unpad_w_in(g_w_in_p), in_cols))], ("w_in",), "w_in")
        in_from_chips = _rs_chips(in_sums, "rs_chips_w_in")
        halves = [_chip_add(s, r, self.chip, self.core, "chip_add_" + n)
                  for s, r, n in zip(in_sums + self.rest_sums, list(in_from_chips) + list(rest_from_chips),
                                     ("w_in",) + self.REST)]
        return [f.reshape(-1, f.shape[-1]) for f in _rs_share(halves)]


SLAB_W = 1024


def _pack(arrays, rows):
    flat = jnp.concatenate([a.reshape(-1).astype(F32) for a in arrays])
    return jnp.pad(flat, (0, rows * SLAB_W - flat.shape[0])).reshape(rows, SLAB_W)


def _unpack(flat, shapes):
    out, off = [], 0
    for s in shapes:
        n = 1
        for d in s:
            n *= d
        out.append(flat[off:off + n].reshape(s))
        off += n
    return out


def _rows_for(arrays_or_shapes):
    n = 0
    for a in arrays_or_shapes:
        s = a if isinstance(a, tuple) else a.shape
        k = 1
        for d in s:
            k *= d
        n += k
    return -(-n // (8 * SLAB_W)) * 8


def kernel(x, c, ln0_g, ln0_b, w_ada, b_ada, w_in, dn_conv, dn_a_log, dn_dt_bias, dn_norm_g, gla_w_gate2, gla_b_gate, gla_norm_g, w_o, ln1_g, ln1_b, ffn_w_up, ffn_conv, ffn_conv_b, ffn_w_down, ln2_g, ln2_b, loss_target, m_ln0_g, m_ln0_b, m_w_ada, m_b_ada, m_w_in, m_dn_conv, m_dn_a_log, m_dn_dt_bias, m_dn_norm_g, m_gla_w_gate2, m_gla_b_gate, m_gla_norm_g, m_w_o, m_ln1_g, m_ln1_b, m_ffn_w_up, m_ffn_conv, m_ffn_conv_b, m_ffn_w_down, m_ln2_g, m_ln2_b, v_ln0_g, v_ln0_b, v_w_ada, v_b_ada, v_w_in, v_dn_conv, v_dn_a_log, v_dn_dt_bias, v_dn_norm_g, v_gla_w_gate2, v_gla_b_gate, v_gla_norm_g, v_w_o, v_ln1_g, v_ln1_b, v_ffn_w_up, v_ffn_conv, v_ffn_conv_b, v_ffn_w_down, v_ln2_g, v_ln2_b):
    n_b, t_len, _ = x.shape
    xi, yi, ci = _place()
    chip = (2 * xi + yi).astype(jnp.int32)
    core = ci.astype(jnp.int32)
    me = 2 * chip + core
    n_all = N_DEV * n_b
    ada_cols = w_ada.shape[2]

    sharded_small = [dn_conv[0], gla_w_gate2[0], ffn_conv[0]]
    slab = _pack([c] + sharded_small, _rows_for([c] + sharded_small))
    gathered, _ = _all_gather8(slab, "gather_small")
    flat = gathered.reshape(N_DEV, -1)
    per_dev = [_unpack(flat[d], [c.shape] + [a.shape for a in sharded_small]) for d in range(N_DEV)]
    c_all = jnp.concatenate([per_dev[d][0] for d in range(N_DEV)], axis=0)
    dn_conv_f, gate2_f, ffn_conv_f = (jnp.concatenate([per_dev[2 * j][i] for j in range(N_CHIPS)], axis=1)
                                      for i in (1, 2, 3))

    b_ada_shard = lax.dynamic_slice(b_ada, (0, chip * ada_cols), (1, ada_cols))
    cond_all, mod_cols = _ada_fwd(c_all, w_ada[0], b_ada_shard)
    mod_g, _ = _all_gather8(mod_cols, "gather_mod")
    mod_full = jnp.concatenate([mod_g[2 * j] for j in range(N_CHIPS)], axis=1)
    mod = lax.dynamic_slice(mod_full, (me * n_b, 0), (n_b, 6 * D_MODEL)).reshape(n_b, 6, D_MODEL)

    halves = lambda a: a.astype(BF16).reshape(2, a.shape[0] // 2, a.shape[1])
    (g_in,) = _gather_weights([halves(w_in[0])])
    comm = StepComm(core, chip, [halves(w_o[0]), halves(ffn_w_up[0]), halves(ffn_w_down[0])])
    cols = lambda g: g.reshape(N_CHIPS, -1, g.shape[-1]).transpose(1, 0, 2).reshape(-1, N_CHIPS * g.shape[-1])
    params = {
        "w_in_p": _pad_w_in(cols(g_in)),
        "dn_conv": dn_conv_f, "dn_a_log": dn_a_log[0], "dn_dt_bias": dn_dt_bias[0], "dn_norm_g": dn_norm_g[0],
        "gla_w_gate2": gate2_f, "gla_b_gate": gla_b_gate[0], "gla_norm_g": gla_norm_g[0],
        "ln0_g": ln0_g, "ln0_b": ln0_b, "ln1_g": ln1_g[0], "ln1_b": ln1_b[0], "ln2_g": ln2_g[0], "ln2_b": ln2_b[0],
        "ffn_conv": ffn_conv_f, "ffn_conv_b": ffn_conv_b[0],
    }

    loss_row, grad_x, gp, dmod, from_chips = _local_step(
        x.reshape(n_b * t_len, D_MODEL), loss_target.reshape(n_b * t_len, D_MODEL), mod, params, n_b, t_len, comm)
    loss = lax.psum(loss_row[0, 0], ("x", "y", "c"))

    summed_names = ["ln0_g", "ln0_b", "dn_conv", "dn_a_log", "dn_dt_bias", "dn_norm_g", "gla_w_gate2", "gla_b_gate",
                    "gla_norm_g", "ln1_g", "ln1_b", "ffn_conv", "ffn_conv_b", "ln2_g", "ln2_b"]
    summed_parts = [gp[n] for n in summed_names]
    sum_rows = _rows_for(summed_parts)
    slab = jnp.concatenate([_pack(summed_parts, sum_rows), _pack([dmod], _rows_for([dmod]))], axis=0)
    gathered, total = _all_gather8(slab, "reduce_small")
    small_g = dict(zip(summed_names, _unpack(total.reshape(-1), [a.shape for a in summed_parts])))
    dmod_rows = n_b * 6 * D_MODEL // SLAB_W
    dmod_all = gathered[:, sum_rows:sum_rows + dmod_rows, :].reshape(n_all, 6 * D_MODEL)

    g_b_ada = _col_sum(dmod_all)
    dmod_cols = lax.dynamic_slice(dmod_all, (0, chip * ada_cols), (n_all, ada_cols))
    g_w_ada = _mm(cond_all, dmod_cols, ta=True, name="mm_g_ada")

    g_w_in, g_w_o, g_w_up, g_w_down = comm.finish(from_chips, gp["w_in_p"], w_in.shape[2])

    col_block = lambda a: lax.dynamic_slice(a, (0, chip * (a.shape[1] // N_CHIPS)), (a.shape[0], a.shape[1] // N_CHIPS))
    grads = {
        "ln0_g": small_g["ln0_g"], "ln0_b": small_g["ln0_b"], "w_ada": g_w_ada[None], "b_ada": g_b_ada,
        "w_in": g_w_in[None], "dn_conv": col_block(small_g["dn_conv"])[None], "dn_a_log": small_g["dn_a_log"][None],
        "dn_dt_bias": small_g["dn_dt_bias"][None], "dn_norm_g": small_g["dn_norm_g"][None],
        "gla_w_gate2": col_block(small_g["gla_w_gate2"])[None], "gla_b_gate": small_g["gla_b_gate"][None],
        "gla_norm_g": small_g["gla_norm_g"][None], "w_o": g_w_o[None], "ln1_g": small_g["ln1_g"][None],
        "ln1_b": small_g["ln1_b"][None], "ffn_w_up": g_w_up[None], "ffn_conv": col_block(small_g["ffn_conv"])[None],
        "ffn_conv_b": small_g["ffn_conv_b"][None], "ffn_w_down": g_w_down[None], "ln2_g": small_g["ln2_g"][None],
        "ln2_b": small_g["ln2_b"][None],
    }
    names = ["ln0_g", "ln0_b", "w_ada", "b_ada", "w_in", "dn_conv", "dn_a_log", "dn_dt_bias", "dn_norm_g",
             "gla_w_gate2", "gla_b_gate", "gla_norm_g", "w_o", "ln1_g", "ln1_b", "ffn_w_up", "ffn_conv", "ffn_conv_b",
             "ffn_w_down", "ln2_g", "ln2_b"]
    weights = dict(zip(names, [ln0_g, ln0_b, w_ada, b_ada, w_in, dn_conv, dn_a_log, dn_dt_bias, dn_norm_g, gla_w_gate2,
                               gla_b_gate, gla_norm_g, w_o, ln1_g, ln1_b, ffn_w_up, ffn_conv, ffn_conv_b, ffn_w_down,
                               ln2_g, ln2_b]))
    m_in = dict(zip(names, [m_ln0_g, m_ln0_b, m_w_ada, m_b_ada, m_w_in, m_dn_conv, m_dn_a_log, m_dn_dt_bias,
                            m_dn_norm_g, m_gla_w_gate2, m_gla_b_gate, m_gla_norm_g, m_w_o, m_ln1_g, m_ln1_b,
                            m_ffn_w_up, m_ffn_conv, m_ffn_conv_b, m_ffn_w_down, m_ln2_g, m_ln2_b]))
    v_in = dict(zip(names, [v_ln0_g, v_ln0_b, v_w_ada, v_b_ada, v_w_in, v_dn_conv, v_dn_a_log, v_dn_dt_bias,
                            v_dn_norm_g, v_gla_w_gate2, v_gla_b_gate, v_gla_norm_g, v_w_o, v_ln1_g, v_ln1_b,
                            v_ffn_w_up, v_ffn_conv, v_ffn_conv_b, v_ffn_w_down, v_ln2_g, v_ln2_b]))

    big = ("w_ada", "w_in", "w_o", "ffn_w_up", "ffn_w_down")
    delta, new_m, new_v = {}, {}, {}
    for n in big:
        d_n, m_n, v_n = _adamw(weights[n][0], grads[n][0], m_in[n][0], v_in[n][0], "adamw_" + n)
        delta[n], new_m[n], new_v[n] = d_n[None], m_n[None], v_n[None]
    small = [n for n in names if n not in big]
    shapes = [weights[n].shape for n in small]
    rows = _rows_for(shapes)
    d_s, m_s, v_s = _adamw(_pack([weights[n] for n in small], rows), _pack([grads[n] for n in small], rows),
                           _pack([m_in[n] for n in small], rows), _pack([v_in[n] for n in small], rows), "adamw_small")
    for out, slab_out in ((delta, d_s), (new_m, m_s), (new_v, v_s)):
        out.update(zip(small, _unpack(slab_out.reshape(-1), shapes)))

    return (loss, grad_x.reshape(x.shape), *[grads[n] for n in names], *[delta[n] for n in names],
            *[new_m[n] for n in names], *[new_v[n] for n in names])
```

```python
import functools

import jax
import jax.numpy as jnp
from jax import lax
from jax.experimental import pallas as pl
from jax.experimental.pallas import tpu as pltpu

F32 = jnp.float32
BF16 = jnp.bfloat16
MESH = pl.DeviceIdType.MESH

D_MODEL = 1024
HEADS = 4
HEAD_DIM = 128
GLA_KEY = 64
GATE_RANK = 16
CHUNK = 64
D_FF = 2816
ALPHA = 2.0 ** 0.25
EPS = 1e-6
N_CHIPS = 4
N_DEV = 8

PROJ_W = 3840
OFF_GQ, OFF_GK, OFF_GV, OFF_GG, OFF_SMALL, GLA_W = 0, 256, 512, 1024, 1536, 1792
OFF_Z = 2048
W_IN_COLS = 3608


def _qkv_block(j):
    return jnp.where(j < 2, GLA_W // 128 + j, (OFF_Z + 512) // 128 - 2 + j)

ADAM_LR, ADAM_B1, ADAM_B2, ADAM_EPS, ADAM_WD, ADAM_STEP = 0.001, 0.9, 0.999, 1e-08, 0.01, 10

VMEM_LIMIT = 56 * 1024 * 1024
ROW_TILE = 256


def _cparams(sem):
    return pltpu.CompilerParams(dimension_semantics=sem, vmem_limit_bytes=VMEM_LIMIT)


def _pick(n, prefs):
    for p in prefs:
        if n % p == 0:
            return p
    return n


def _mm(a, b, *, ta=False, tb=False, out_slabs=1, out_dtype=F32, name, rider=None):
    a_slabs = a.shape[0] if a.ndim == 3 else 1
    b_slabs = b.shape[0] if b.ndim == 3 else 1
    assert not (ta and a_slabs > 1)
    a2, b2 = a.shape[-2:], b.shape[-2:]
    if ta:
        k_dim, m_dim = a2
    else:
        m_dim, k_dim = a2[0], a2[1] * a_slabs
    n_dim = b2[0] if tb else b2[1] * b_slabs
    k_slabs = max(a_slabs, b_slabs if tb else 1)
    n_slabs = max(out_slabs, 1 if tb else b_slabs)
    tm = _pick(m_dim, (1024, 1408, 512, 256, 128))
    tn = _pick(n_dim // n_slabs, (1536, 1408, 1280, 1024, 768, 512, 384, 256, 128))
    tk = _pick(k_dim // k_slabs, (1408, 1280, 1024, 512, 256, 128))
    nk, nj = k_dim // tk, n_dim // tn
    nk_a, nk_b, nj_b, nj_o = nk // a_slabs, nk // b_slabs, nj // b_slabs, nj // out_slabs
    dims = (((0 if ta else 1,), (1 if tb else 0,)), ((), ()))

    grid = (m_dim // tm, nj, nk)
    r_inputs, r_in_specs, r_out_specs, r_sems, split = _with_rider(rider, 2, 1, 1)

    def body(*refs):
        (a_ref, b_ref, o_ref, acc_ref), parts = split(refs)
        ride_first, ride_last = _ride(rider, parts, grid)
        if rider is not None:
            ride_first()
        k = pl.program_id(2)

        @pl.when(k == 0)
        def _():
            acc_ref[...] = jnp.zeros_like(acc_ref)

        acc_ref[...] += lax.dot_general(a_ref[...].astype(BF16), b_ref[...].astype(BF16), dims,
                                        preferred_element_type=F32)

        @pl.when(k == nk - 1)
        def _():
            o_ref[...] = acc_ref[...].astype(o_ref.dtype)

        if rider is not None:
            ride_last()

    if ta:
        a_spec = pl.BlockSpec((tk, tm), lambda i, j, k: (k, i))
    elif a_slabs > 1:
        a_spec = pl.BlockSpec((None, tm, tk), lambda i, j, k: (k // nk_a, i, k % nk_a))
    else:
        a_spec = pl.BlockSpec((tm, tk), lambda i, j, k: (i, k))
    if tb and b_slabs > 1:
        b_spec = pl.BlockSpec((None, tn, tk), lambda i, j, k: (k // nk_b, j, k % nk_b))
    elif tb:
        b_spec = pl.BlockSpec((tn, tk), lambda i, j, k: (j, k))
    elif b_slabs > 1:
        b_spec = pl.BlockSpec((None, tk, tn), lambda i, j, k: (j // nj_b, k, j % nj_b))
    else:
        b_spec = pl.BlockSpec((tk, tn), lambda i, j, k: (k, j))
    if out_slabs > 1:
        o_spec = pl.BlockSpec((None, tm, tn), lambda i, j, k: (j // nj_o, i, j % nj_o))
        o_shape = (out_slabs, m_dim, n_dim // out_slabs)
    else:
        o_spec, o_shape = pl.BlockSpec((tm, tn), lambda i, j, k: (i, j)), (m_dim, n_dim)
    out, *rider_outs = pl.pallas_call(
        body, name=name, grid=grid,
        in_specs=[a_spec, b_spec] + r_in_specs, out_specs=[o_spec] + r_out_specs,
        out_shape=[jax.ShapeDtypeStruct(o_shape, out_dtype)] + (list(rider.out_shapes) if rider else []),
        scratch_shapes=[pltpu.VMEM((tm, tn), F32)] + r_sems,
        compiler_params=_cparams(("arbitrary",) * 3 if rider else ("parallel", "parallel", "arbitrary")),
    )(a, b, *r_inputs)
    return (out, rider_outs) if rider else out


def _ln(x, g, b):
    mu = jnp.mean(x, -1, keepdims=True)
    xc = x - mu
    var = jnp.mean(xc * xc, -1, keepdims=True)
    return xc * lax.rsqrt(var + EPS) * g + b


def _softplus(x):
    return jnp.maximum(x, 0.0) + jnp.log(1.0 + jnp.exp(-jnp.abs(x)))


def _silu(x):
    return x * jax.nn.sigmoid(x)


def _dsilu(x):
    s = jax.nn.sigmoid(x)
    return s * (1.0 + x * (1.0 - s))


def _f_ln0(x, g, b, sc, sh):
    x0 = _ln(x, g, b)
    return x0, x0 * (1.0 + sc) + sh


def _f_ln1(x0, y, gt, g, b, sc, sh):
    x1 = _ln(ALPHA * x0 + (1.0 + gt) * y, g, b)
    return x1, x1 * (1.0 + sc) + sh


def _f_ln2_loss(x1, y2, gt, g, b, tgt):
    x2 = _ln(ALPHA * x1 + (1.0 + gt) * y2, g, b)
    err = x2 - tgt
    per_row = jnp.sum(err * err, -1, keepdims=True) * (0.5 / D_MODEL)
    return jnp.sum(per_row, 0, keepdims=True)


def _row_specs(t_len):
    nt = t_len // ROW_TILE
    row = pl.BlockSpec((ROW_TILE, D_MODEL), lambda b, i: (b * nt + i, 0))
    vec = pl.BlockSpec((1, D_MODEL), lambda b, i: (0, 0))
    mod = pl.BlockSpec((None, 6, D_MODEL), lambda b, i: (b, 0, 0))
    return nt, row, vec, mod


def _first_step():
    return jnp.logical_and(pl.program_id(0) == 0, pl.program_id(1) == 0)


def _acc(ref, val, first, at=(Ellipsis,)):
    @pl.when(first)
    def _():
        ref[at] = val

    @pl.when(jnp.logical_not(first))
    def _():
        ref[at] += val


def _acc_rows(ref, rows, first):
    for i, r in enumerate(rows):
        _acc(ref, r, first, at=(slice(i, i + 1), slice(None)))


def _ln0_fwd(x, g, b, mod, n_b, t_len):
    nt, row, vec, mods = _row_specs(t_len)

    def body(x_ref, g_ref, b_ref, mod_ref, x0_ref, h_ref):
        x0, h = _f_ln0(x_ref[...], g_ref[...], b_ref[...], mod_ref[1:2, :], mod_ref[0:1, :])
        x0_ref[...] = x0
        h_ref[...] = h.astype(BF16)

    return pl.pallas_call(
        body, name="ln0_fwd", grid=(n_b, nt), in_specs=[row, vec, vec, mods], out_specs=[row, row],
        out_shape=[jax.ShapeDtypeStruct(x.shape, F32), jax.ShapeDtypeStruct(x.shape, BF16)],
        compiler_params=_cparams(("parallel", "parallel")),
    )(x, g, b, mod)


def _ln0_bwd(x, g, b, mod, dx0, dh, n_b, t_len):
    nt, row, vec, mods = _row_specs(t_len)
    dmod_spec = pl.BlockSpec((None, 2, D_MODEL), lambda bb, i: (bb, 0, 0))

    def body(x_ref, g_ref, b_ref, mod_ref, dx0_ref, dh_ref, dx_ref, dg_ref, db_ref, dmod_ref):
        _, pull = jax.vjp(_f_ln0, x_ref[...], g_ref[...], b_ref[...], mod_ref[1:2, :], mod_ref[0:1, :])
        dx, dg, db, dsc, dsh = pull((dx0_ref[...], dh_ref[...]))
        dx_ref[...] = dx
        _acc(dg_ref, dg, _first_step())
        _acc(db_ref, db, _first_step())
        _acc_rows(dmod_ref, [dsh, dsc], pl.program_id(1) == 0)

    return pl.pallas_call(
        body, name="ln0_bwd", grid=(n_b, nt), in_specs=[row, vec, vec, mods, row, row],
        out_specs=[row, vec, vec, dmod_spec],
        out_shape=[jax.ShapeDtypeStruct(x.shape, F32), jax.ShapeDtypeStruct((1, D_MODEL), F32),
                   jax.ShapeDtypeStruct((1, D_MODEL), F32), jax.ShapeDtypeStruct((n_b, 2, D_MODEL), F32)],
        compiler_params=_cparams(("arbitrary", "arbitrary")),
    )(x, g, b, mod, dx0, dh)


def _ln1_fwd(x0, y, g, b, mod, n_b, t_len):
    nt, row, vec, mods = _row_specs(t_len)

    def body(x0_ref, y_ref, g_ref, b_ref, mod_ref, x1_ref, h_ref):
        x1, h = _f_ln1(x0_ref[...], y_ref[...], mod_ref[2:3, :], g_ref[...], b_ref[...],
                       mod_ref[4:5, :], mod_ref[3:4, :])
        x1_ref[...] = x1
        h_ref[...] = h.astype(BF16)

    return pl.pallas_call(
        body, name="ln1_fwd", grid=(n_b, nt), in_specs=[row, row, vec, vec, mods], out_specs=[row, row],
        out_shape=[jax.ShapeDtypeStruct(x0.shape, F32), jax.ShapeDtypeStruct(x0.shape, BF16)],
        compiler_params=_cparams(("parallel", "parallel")),
    )(x0, y, g, b, mod)


def _ln1_bwd(x0, y, g, b, mod, dx1, dh, n_b, t_len):
    nt, row, vec, mods = _row_specs(t_len)
    dmod_spec = pl.BlockSpec((None, 3, D_MODEL), lambda bb, i: (bb, 0, 0))

    def body(x0_ref, y_ref, g_ref, b_ref, mod_ref, dx1_ref, dh_ref, dx0_ref, dy_ref, dg_ref, db_ref, dmod_ref):
        _, pull = jax.vjp(_f_ln1, x0_ref[...], y_ref[...], mod_ref[2:3, :], g_ref[...], b_ref[...],
                          mod_ref[4:5, :], mod_ref[3:4, :])
        dx0, dy, dgt, dg, db, dsc, dsh = pull((dx1_ref[...], dh_ref[...]))
        dx0_ref[...] = dx0
        dy_ref[...] = dy.astype(BF16)
        _acc(dg_ref, dg, _first_step())
        _acc(db_ref, db, _first_step())
        _acc_rows(dmod_ref, [dgt, dsh, dsc], pl.program_id(1) == 0)

    return pl.pallas_call(
        body, name="ln1_bwd", grid=(n_b, nt), in_specs=[row, row, vec, vec, mods, row, row],
        out_specs=[row, row, vec, vec, dmod_spec],
        out_shape=[jax.ShapeDtypeStruct(x0.shape, F32), jax.ShapeDtypeStruct(x0.shape, BF16),
                   jax.ShapeDtypeStruct((1, D_MODEL), F32), jax.ShapeDtypeStruct((1, D_MODEL), F32),
                   jax.ShapeDtypeStruct((n_b, 3, D_MODEL), F32)],
        compiler_params=_cparams(("arbitrary", "arbitrary")),
    )(x0, y, g, b, mod, dx1, dh)


def _ln2_loss_bwd(x1, y2, g, b, mod, tgt, n_b, t_len):
    nt, row, vec, mods = _row_specs(t_len)
    one = pl.BlockSpec((1, 128), lambda bb, i: (0, 0))
    dmod_spec = pl.BlockSpec((None, 1, D_MODEL), lambda bb, i: (bb, 0, 0))

    def body(x1_ref, y2_ref, g_ref, b_ref, mod_ref, t_ref, loss_ref, dx1_ref, dy2_ref, dg_ref, db_ref, dgt_ref):
        loss, pull = jax.vjp(functools.partial(_f_ln2_loss, tgt=t_ref[...]), x1_ref[...], y2_ref[...],
                             mod_ref[5:6, :], g_ref[...], b_ref[...])
        dx1, dy2, dgt, dg, db = pull(jnp.ones((1, 1), F32))
        dx1_ref[...] = dx1
        dy2_ref[...] = dy2.astype(BF16)
        _acc(loss_ref, jnp.broadcast_to(loss, (1, 128)), _first_step())
        _acc(dg_ref, dg, _first_step())
        _acc(db_ref, db, _first_step())
        _acc(dgt_ref, dgt, pl.program_id(1) == 0)

    return pl.pallas_call(
        body, name="ln2_loss_bwd", grid=(n_b, nt), in_specs=[row, row, vec, vec, mods, row],
        out_specs=[one, row, row, vec, vec, dmod_spec],
        out_shape=[jax.ShapeDtypeStruct((1, 128), F32), jax.ShapeDtypeStruct(x1.shape, F32),
                   jax.ShapeDtypeStruct(x1.shape, BF16), jax.ShapeDtypeStruct((1, D_MODEL), F32),
                   jax.ShapeDtypeStruct((1, D_MODEL), F32), jax.ShapeDtypeStruct((n_b, 1, D_MODEL), F32)],
        compiler_params=_cparams(("arbitrary", "arbitrary")),
    )(x1, y2, g, b, mod, tgt)


def _shift_down(x, s):
    if s == 0:
        return x
    rows = lax.broadcasted_iota(jnp.int32, x.shape, 0)
    return jnp.where(rows >= s, pltpu.roll(x, s, 0), 0.0)


def _shift_up(x, s):
    if s == 0:
        return x
    t_len = x.shape[0]
    rows = lax.broadcasted_iota(jnp.int32, x.shape, 0)
    return jnp.where(rows < t_len - s, pltpu.roll(x, t_len - s, 0), 0.0)


def _conv(x, w):
    k_w = w.shape[0]
    out = w[k_w - 1:k_w, :] * x
    for k in range(k_w - 1):
        out = out + w[k:k + 1, :] * _shift_down(x, k_w - 1 - k)
    return out


def _conv_bwd(x, w, du):
    k_w = w.shape[0]
    dx = w[k_w - 1:k_w, :] * du
    dws = []
    for k in range(k_w):
        s = k_w - 1 - k
        if s:
            dx = dx + w[k:k + 1, :] * _shift_up(du, s)
        dws.append(jnp.sum(du * _shift_down(x, s), 0, keepdims=True))
    return dx, dws


def _dn_pre_fwd(proj, conv_w, n_b, t_len):
    n_ct = 3 * HEADS
    k_w = conv_w.shape[0]

    def body(x_ref, w_ref, o_ref):
        o_ref[...] = _silu(_conv(x_ref[...], w_ref[...]))

    return pl.pallas_call(
        body, name="dn_pre_fwd", grid=(n_ct, n_b),
        in_specs=[pl.BlockSpec((t_len, 128), lambda j, b: (b, _qkv_block(j))),
                  pl.BlockSpec((k_w, 128), lambda j, b: (0, j))],
        out_specs=pl.BlockSpec((t_len, 128), lambda j, b: (b, j)),
        out_shape=jax.ShapeDtypeStruct((n_b * t_len, n_ct * 128), F32),
        compiler_params=_cparams(("parallel", "parallel")),
    )(proj, conv_w)


def _dn_pre_bwd(proj, conv_w, dqkv, d_proj, n_b, t_len):
    n_ct = 3 * HEADS
    k_w = conv_w.shape[0]

    def body(x_ref, w_ref, d_ref, _, dx_ref, dw_ref):
        x, w = x_ref[...], w_ref[...]
        du = d_ref[...] * _dsilu(_conv(x, w))
        dx, dw = _conv_bwd(x, w, du)
        dx_ref[...] = dx.astype(BF16)
        _acc_rows(dw_ref, dw, pl.program_id(1) == 0)

    return pl.pallas_call(
        body, name="dn_pre_bwd", grid=(n_ct, n_b),
        in_specs=[pl.BlockSpec((t_len, 128), lambda j, b: (b, _qkv_block(j))),
                  pl.BlockSpec((k_w, 128), lambda j, b: (0, j)),
                  pl.BlockSpec((t_len, 128), lambda j, b: (b, j)), pl.BlockSpec(memory_space=pl.ANY)],
        out_specs=[pl.BlockSpec((t_len, 128), lambda j, b: (b, _qkv_block(j))),
                   pl.BlockSpec((k_w, 128), lambda j, b: (0, j))],
        out_shape=[jax.ShapeDtypeStruct(d_proj.shape, BF16), jax.ShapeDtypeStruct((k_w, n_ct * 128), F32)],
        input_output_aliases={3: 0},
        compiler_params=_cparams(("parallel", "arbitrary")),
    )(proj, conv_w, dqkv, d_proj)


FFN_TC = 256
FFN_NT = D_FF // FFN_TC


def _ffn_specs(t_len):
    blk = lambda off: pl.BlockSpec((t_len, FFN_TC), lambda j, b: (b, j + off))
    wblk = lambda off: pl.BlockSpec((3, FFN_TC), lambda j, b: (0, j + off))
    bblk = lambda off: pl.BlockSpec((1, FFN_TC), lambda j, b: (0, j + off))
    return [blk(0), blk(FFN_NT), wblk(0), wblk(FFN_NT), bblk(0), bblk(FFN_NT)]


def _ffn_act_fwd(up, conv_w, conv_b, n_b, t_len):
    def body(g_ref, v_ref, wg_ref, wv_ref, bg_ref, bv_ref, o_ref):
        ug = _conv(g_ref[...], wg_ref[...]) + bg_ref[...]
        uv = _conv(v_ref[...], wv_ref[...]) + bv_ref[...]
        o_ref[...] = (_silu(ug) * uv).astype(BF16)

    return pl.pallas_call(
        body, name="ffn_act_fwd", grid=(FFN_NT, n_b), in_specs=_ffn_specs(t_len),
        out_specs=pl.BlockSpec((t_len, FFN_TC), lambda j, b: (b, j)),
        out_shape=jax.ShapeDtypeStruct((n_b * t_len, D_FF), BF16),
        compiler_params=_cparams(("parallel", "parallel")),
    )(up, up, conv_w, conv_w, conv_b, conv_b)


def _ffn_act_bwd(up, conv_w, conv_b, da, n_b, t_len):
    def body(g_ref, v_ref, wg_ref, wv_ref, bg_ref, bv_ref, da_ref, dup_ref, dw_ref, db_ref):
        first = pl.program_id(1) == 0
        xg, xv, wg, wv = g_ref[...], v_ref[...], wg_ref[...], wv_ref[...]
        ug = _conv(xg, wg) + bg_ref[...]
        uv = _conv(xv, wv) + bv_ref[...]
        d_act = da_ref[...]
        sig = jax.nn.sigmoid(ug)
        d_v = d_act * (ug * sig)
        d_g = d_act * uv * (sig * (1.0 + ug * (1.0 - sig)))
        for slab, (x, w, du) in enumerate(((xg, wg, d_g), (xv, wv, d_v))):
            dx, dw = _conv_bwd(x, w, du)
            dup_ref[slab] = dx.astype(BF16)
            for k, dw_k in enumerate(dw):
                _acc(dw_ref, dw_k, first, at=(slab, slice(k, k + 1), slice(None)))
            _acc(db_ref, jnp.sum(du, 0, keepdims=True), first, at=(slab, slice(None), slice(None)))

    return pl.pallas_call(
        body, name="ffn_act_bwd", grid=(FFN_NT, n_b),
        in_specs=_ffn_specs(t_len) + [pl.BlockSpec((t_len, FFN_TC), lambda j, b: (b, j))],
        out_specs=[pl.BlockSpec((2, t_len, FFN_TC), lambda j, b: (0, b, j)),
                   pl.BlockSpec((2, 3, FFN_TC), lambda j, b: (0, 0, j)),
                   pl.BlockSpec((2, 1, FFN_TC), lambda j, b: (0, 0, j))],
        out_shape=[jax.ShapeDtypeStruct((2, n_b * t_len, D_FF), BF16),
                   jax.ShapeDtypeStruct((2, 3, D_FF), F32), jax.ShapeDtypeStruct((2, 1, D_FF), F32)],
        compiler_params=_cparams(("parallel", "arbitrary")),
    )(up, up, conv_w, conv_w, conv_b, conv_b, da)


NN = (((2,), (1,)), ((0,), (0,)))
NT = (((2,), (2,)), ((0,), (0,)))
TN = (((1,), (1,)), ((0,), (0,)))


def _iota3(shape, axis):
    return lax.broadcasted_iota(jnp.int32, shape, axis)


def _dg(a, b, dims):
    return lax.dot_general(a, b, dims, preferred_element_type=F32)


def _dot(a, b):
    return _dg(a, b, NN)


def _dot_nt(a, b):
    return _dg(a, b, NT)


def _dot_tn(a, b):
    return _dg(a, b, TN)


def _split(a):
    hi = a.astype(BF16)
    return hi, (a - hi.astype(F32)).astype(BF16)


def _dg3(a, b, dims):
    ah, al = _split(a)
    bh, bl = _split(b)
    return _dg(ah, bh, dims) + (_dg(ah, bl, dims) + _dg(al, bh, dims))


@jax.custom_vjp
def _dot3(a, b):
    return _dg3(a, b, NN)


def _dot3_fwd(a, b):
    return _dg3(a, b, NN), (a, b)


def _dot3_bwd(res, g):
    a, b = res
    return _dg3(g, b, NT), _dg3(a, g, TN)


_dot3.defvjp(_dot3_fwd, _dot3_bwd)


def _lower_ones(g_n, n):
    shape = (g_n, n, n)
    return jnp.where(_iota3(shape, 1) >= _iota3(shape, 2), 1.0, 0.0).astype(BF16)


@jax.custom_vjp
def _chunk_cumsum(x):
    hi, lo = _split(x)
    tri = _lower_ones(x.shape[0], x.shape[1])
    return _dg(tri, hi, NN) + _dg(tri, lo, NN)


def _chunk_cumsum_fwd(x):
    return _chunk_cumsum(x), None


def _chunk_cumsum_bwd(_, g):
    hi, lo = _split(g)
    tri = _lower_ones(g.shape[0], g.shape[1])
    return (_dg(tri, hi, TN) + _dg(tri, lo, TN),)


_chunk_cumsum.defvjp(_chunk_cumsum_fwd, _chunk_cumsum_bwd)


@jax.custom_vjp
def _unit_lower_inv(m):
    n = m.shape[1]
    p = -m
    a = jnp.where(_iota3(m.shape, 1) == _iota3(m.shape, 2), 1.0, 0.0) + p
    span = 2
    while span < n:
        p = _dg3(p, p, NN)
        a = a + _dg3(a, p, NN)
        span *= 2
    return a


def _unit_lower_inv_fwd(m):
    a = _unit_lower_inv(m)
    return a, a


def _unit_lower_inv_bwd(a, da):
    return (-_dg3(a, _dg3(da, a, NT), TN),)


_unit_lower_inv.defvjp(_unit_lower_inv_fwd, _unit_lower_inv_bwd)


@jax.custom_vjp
def _saved_lower_inv(m, a):
    return a


def _saved_lower_inv_fwd(m, a):
    return a, a


def _saved_lower_inv_bwd(a, da):
    return _unit_lower_inv_bwd(a, da)[0], jnp.zeros_like(a)


_saved_lower_inv.defvjp(_saved_lower_inv_fwd, _saved_lower_inv_bwd)


def _rms_gate(o, gn, gate):
    return o * lax.rsqrt(jnp.mean(o * o, -1, keepdims=True) + EPS) * gn * _silu(gate)


def _dn_chains(q, k, v, z, small, s_in, a_log, dt_bias, gn, a_saved=None):
    g_n, c_len = q.shape[0], q.shape[1]
    sq = (g_n, c_len, c_len)
    row, col = _iota3(sq, 1), _iota3(sq, 2)
    causal, strict, eye = row >= col, row > col, row == col
    qn = q * lax.rsqrt(jnp.sum(q * q, -1, keepdims=True) + EPS) * (HEAD_DIM ** -0.5)
    kn = k * lax.rsqrt(jnp.sum(k * k, -1, keepdims=True) + EPS)
    lane = _iota3(small.shape, 2)
    head = jnp.bitwise_and(_iota3(small.shape, 0), HEADS - 1)
    la_all = -jnp.exp(a_log) * _softplus(small + dt_bias)
    la_c = jnp.sum(jnp.where(lane == head, la_all, 0.0), 2, keepdims=True)
    beta = jnp.sum(jnp.where(lane == head + HEADS, jax.nn.sigmoid(small), 0.0), 2, keepdims=True)
    la_b = jnp.broadcast_to(la_c, sq)
    la_r = jnp.sum(jnp.where(eye, la_b, 0.0), 1, keepdims=True)
    g_c = jnp.sum(jnp.where(causal, jnp.broadcast_to(la_r, sq), 0.0), 2, keepdims=True)
    g_r = jnp.sum(jnp.where(row <= col, la_b, 0.0), 1, keepdims=True)
    g_last = jnp.sum(la_c, 1, keepdims=True)
    decay = jnp.exp(jnp.where(causal, g_c - g_r, -1e30))
    e_g = jnp.exp(g_c)
    kb = kn * beta
    m_low = jnp.where(strict, _dot_nt(kb, kn) * decay, 0.0)
    a_inv = _unit_lower_inv(m_low) if a_saved is None else _saved_lower_inv(m_low, a_saved)
    u = _dot3(a_inv, v * beta)
    w = _dot3(a_inv, kb * e_g)
    attn = _dot_nt(qn, kn) * decay
    v_new = u - _dot(w, s_in)
    o = _dot(qn * e_g, s_in) + _dot(attn, v_new)
    s_out = s_in * jnp.exp(g_last) + _dot_tn(kn * jnp.exp(g_last - g_c), v_new)
    return _rms_gate(o, gn, z), s_out, a_inv


def _gla_chains(q, k, v, gate, small, s_in, w2, b2, gn):
    g_n, c_len = q.shape[0], q.shape[1]
    sq, kk = (g_n, c_len, c_len), (g_n, GLA_KEY, GLA_KEY)
    causal = _iota3(sq, 1) >= _iota3(sq, 2)
    la = -_softplus(-(_dot(small, w2) + b2)) * (1.0 / 16.0)
    b = _chunk_cumsum(la)
    b_last = jnp.sum(jnp.where(_iota3(b.shape, 1) == c_len - 1, b, 0.0), 1, keepdims=True)
    q_dec = q * (GLA_KEY ** -0.5) * jnp.exp(b)
    attn = jnp.where(causal, _dot_nt(q_dec, k * jnp.exp(-b)), 0.0)
    o = _dot(q_dec, s_in) + _dot(attn, v)
    g_row = jnp.exp(b_last)
    g_col = jnp.sum(jnp.where(_iota3(kk, 1) == _iota3(kk, 2), jnp.broadcast_to(g_row, kk), 0.0), 2, keepdims=True)
    s_out = s_in * g_col + _dot_tn(k * jnp.exp(b_last - b), v)
    return _rms_gate(o, gn, gate), s_out


def _chunk_spec(n_b, width, col_block, n_c, reverse=False):
    if reverse:
        return pl.BlockSpec((n_b, CHUNK, width), lambda n: (0, n_c - 1 - n, col_block))
    return pl.BlockSpec((n_b, CHUNK, width), lambda n: (0, n, col_block))


def _hist_spec(n_b, d_k, n_c, reverse=False):
    if reverse:
        return pl.BlockSpec((None, n_b * HEADS, d_k, HEAD_DIM), lambda n: (n_c - 1 - n, 0, 0, 0))
    return pl.BlockSpec((None, n_b * HEADS, d_k, HEAD_DIM), lambda n: (n, 0, 0, 0))


def _ainv_spec(n_b, n_c, reverse=False):
    if reverse:
        return pl.BlockSpec((None, n_b * HEADS, CHUNK, CHUNK), lambda n: (n_c - 1 - n, 0, 0, 0))
    return pl.BlockSpec((None, n_b * HEADS, CHUNK, CHUNK), lambda n: (n, 0, 0, 0))


def _stack_chains(ref, n_b, slices):
    return jnp.stack([ref[b, :, sl] for b in range(n_b) for sl in slices], axis=0)


def _per_chain(ref, n_b):
    return jnp.stack([ref[b] for b in range(n_b) for _ in range(HEADS)], axis=0)


def _unstack_chains(ref, val, n_b, slices, offset=0):
    for b in range(n_b):
        for h, sl in enumerate(slices):
            ref[b, :, slice(offset + sl.start, offset + sl.stop)] = val[b * HEADS + h].astype(ref.dtype)


def _gate_weights(w2_ref, b2_ref, n_b):
    w2 = jnp.stack([w2_ref[:, ks] for _ in range(n_b) for ks in GLA_KSL], axis=0)
    b2 = jnp.stack([b2_ref[:, ks] for _ in range(n_b) for ks in GLA_KSL], axis=0)
    return w2, b2


def _sum_heads(val, n_b):
    return [sum(val[b * HEADS + h] for h in range(HEADS)) for b in range(n_b)]


def _const_spec(shape):
    return pl.BlockSpec(shape, lambda n: (0,) * len(shape))


DN_SL = [slice(h * HEAD_DIM, (h + 1) * HEAD_DIM) for h in range(HEADS)]
GLA_KSL = [slice(h * GLA_KEY, (h + 1) * GLA_KEY) for h in range(HEADS)]


class Rider:
    def __init__(self, inputs, out_shapes, sems, first, last):
        self.inputs, self.out_shapes, self.sems, self.first, self.last = inputs, out_shapes, sems, first, last


def _with_rider(rider, n_in, n_out, n_scratch):
    if rider is None:
        return [], [], [], [], lambda refs: (refs, None)
    r_in, r_out, r_sem = len(rider.inputs), len(rider.out_shapes), len(rider.sems)

    def split(refs):
        own_in, rest = refs[:n_in], refs[n_in:]
        rid_in, rest = rest[:r_in], rest[r_in:]
        own_out, rest = rest[:n_out], rest[n_out:]
        rid_out, rest = rest[:r_out], rest[r_out:]
        own_scr, rid_sem = rest[:n_scratch], rest[n_scratch:]
        return own_in + own_out + own_scr, (rid_in, rid_out, rid_sem)

    return list(rider.inputs), [HBM_SPEC] * r_in, [HBM_SPEC] * r_out, list(rider.sems), split


def _ride(rider, parts, grid):
    if rider is None:
        return None, None
    grid = grid if isinstance(grid, tuple) else (grid,)

    def at(step_of):
        hit = pl.program_id(0) == step_of(grid[0])
        for axis in range(1, len(grid)):
            hit = jnp.logical_and(hit, pl.program_id(axis) == step_of(grid[axis]))
        return hit

    def first():
        pl.when(at(lambda n: 0))(lambda: rider.first(*parts))

    def last():
        pl.when(at(lambda n: n - 1))(lambda: rider.last(*parts))

    return first, last


def _dn_scan_fwd(qkv, proj, a_log, dt_bias, gn, n_b, t_len, rider=None):
    n_c = t_len // CHUNK
    spec = functools.partial(_chunk_spec, n_b, n_c=n_c)
    r_inputs, r_in_specs, r_out_specs, r_sems, split = _with_rider(rider, 8, 3, 1)

    def body(*refs):
        (q_ref, k_ref, v_ref, z_ref, sm_ref, al_ref, dt_ref, gn_ref,
         o_ref, hist_ref, ainv_ref, s_ref), parts = split(refs)
        ride_first, ride_last = _ride(rider, parts, n_c)
        if rider is not None:
            ride_first()

        @pl.when(pl.program_id(0) == 0)
        def _():
            s_ref[...] = jnp.zeros_like(s_ref)

        s_in = s_ref[...]
        hist_ref[...] = s_in
        og, s_out, a_inv = _dn_chains(*(_stack_chains(r, n_b, DN_SL) for r in (q_ref, k_ref, v_ref, z_ref)),
                                      _per_chain(sm_ref, n_b), s_in, al_ref[...], dt_ref[...], gn_ref[...])
        _unstack_chains(o_ref, og, n_b, DN_SL)
        s_ref[...] = s_out
        ainv_ref[...] = a_inv
        if rider is not None:
            ride_last()

    qkv3, proj3 = qkv.reshape(n_b, t_len, -1), proj.reshape(n_b, t_len, -1)
    o, hist, ainv, *rider_outs = pl.pallas_call(
        body, name="dn_scan_fwd", grid=(n_c,),
        in_specs=[spec(512, 0), spec(512, 1), spec(512, 2), spec(512, OFF_Z // 512), spec(128, OFF_SMALL // 128),
                  _const_spec((1, 128)), _const_spec((1, 128)), _const_spec((1, 128))] + r_in_specs,
        out_specs=[spec(512, 0), _hist_spec(n_b, HEAD_DIM, n_c), _ainv_spec(n_b, n_c)] + r_out_specs,
        out_shape=[jax.ShapeDtypeStruct((n_b, t_len, 2 * 512), BF16),
                   jax.ShapeDtypeStruct((n_c, n_b * HEADS, HEAD_DIM, HEAD_DIM), F32),
                   jax.ShapeDtypeStruct((n_c, n_b * HEADS, CHUNK, CHUNK), F32)]
        + (list(rider.out_shapes) if rider else []),
        scratch_shapes=[pltpu.VMEM((n_b * HEADS, HEAD_DIM, HEAD_DIM), F32)] + r_sems,
        compiler_params=_cparams(("arbitrary",)),
    )(qkv3, qkv3, qkv3, proj3, proj3, a_log, dt_bias, gn, *r_inputs)
    return o, (hist, ainv), rider_outs


def _dn_scan_bwd(qkv, proj, a_log, dt_bias, gn, hist, d_o, n_b, t_len, rider=None):
    n_c = t_len // CHUNK
    rev = functools.partial(_chunk_spec, n_b, n_c=n_c, reverse=True)
    r_inputs, r_in_specs, r_out_specs, r_sems, split = _with_rider(rider, 11, 6, 1)
    hist, ainv = hist

    def body(*refs):
        (q_ref, k_ref, v_ref, z_ref, sm_ref, al_ref, dt_ref, gn_ref, hist_ref, ainv_ref, do_ref,
         dqkv_ref, dz_ref, dsm_ref, dal_ref, ddt_ref, dgn_ref, ds_ref), parts = split(refs)
        ride_first, ride_last = _ride(rider, parts, n_c)
        if rider is not None:
            ride_first()
        first = pl.program_id(0) == 0

        @pl.when(first)
        def _():
            ds_ref[...] = jnp.zeros_like(ds_ref)

        chains = lambda *a: _dn_chains(*a, a_saved=ainv_ref[...])[:2]
        _, pull = jax.vjp(chains, *(_stack_chains(r, n_b, DN_SL) for r in (q_ref, k_ref, v_ref, z_ref)),
                          _per_chain(sm_ref, n_b), hist_ref[...], al_ref[...], dt_ref[...], gn_ref[...])
        dq, dk, dv, dz, dsm, ds_in, dal, ddt, dgn = pull((_stack_chains(do_ref, n_b, DN_SL), ds_ref[...]))
        _unstack_chains(dqkv_ref, dq, n_b, DN_SL)
        _unstack_chains(dqkv_ref, dk, n_b, DN_SL, offset=512)
        _unstack_chains(dqkv_ref, dv, n_b, DN_SL, offset=1024)
        _unstack_chains(dz_ref, dz, n_b, DN_SL)
        ds_ref[...] = ds_in
        for b, dsm_b in enumerate(_sum_heads(dsm, n_b)):
            dsm_ref[b] = dsm_b
        _acc(dal_ref, dal, first)
        _acc(ddt_ref, ddt, first)
        _acc(dgn_ref, dgn, first)
        if rider is not None:
            ride_last()

    qkv3, proj3, do3 = (a.reshape(n_b, t_len, -1) for a in (qkv, proj, d_o))
    vec = jax.ShapeDtypeStruct((1, 128), F32)
    dqkv, d_proj, dsm, dal, ddt, dgn, *rider_outs = pl.pallas_call(
        body, name="dn_scan_bwd", grid=(n_c,),
        in_specs=[rev(512, 0), rev(512, 1), rev(512, 2), rev(512, OFF_Z // 512), rev(128, OFF_SMALL // 128),
                  _const_spec((1, 128)), _const_spec((1, 128)), _const_spec((1, 128)),
                  _hist_spec(n_b, HEAD_DIM, n_c, reverse=True), _ainv_spec(n_b, n_c, reverse=True),
                  rev(512, 0)] + r_in_specs,
        out_specs=[rev(1536, 0), rev(512, OFF_Z // 512), rev(128, 0),
                   _const_spec((1, 128)), _const_spec((1, 128)), _const_spec((1, 128))] + r_out_specs,
        out_shape=[jax.ShapeDtypeStruct((n_b, t_len, 1536), F32), jax.ShapeDtypeStruct((n_b, t_len, PROJ_W), BF16),
                   jax.ShapeDtypeStruct((n_b, t_len, 128), F32), vec, vec, vec]
        + (list(rider.out_shapes) if rider else []),
        scratch_shapes=[pltpu.VMEM((n_b * HEADS, HEAD_DIM, HEAD_DIM), F32)] + r_sems,
        compiler_params=_cparams(("arbitrary",)),
    )(qkv3, qkv3, qkv3, proj3, proj3, a_log, dt_bias, gn, hist, ainv, do3, *r_inputs)
    return dqkv.reshape(n_b * t_len, 1536), d_proj, dsm, dal, ddt, dgn, rider_outs


def _gla_scan_fwd(proj, w2, b2, gn, o_mix, n_b, t_len):
    n_c = t_len // CHUNK
    spec = functools.partial(_chunk_spec, n_b, n_c=n_c)

    def body(q_ref, k_ref, v_ref, g_ref, sm_ref, w2_ref, b2_ref, gn_ref, _, o_ref, hist_ref, s_ref):
        @pl.when(pl.program_id(0) == 0)
        def _():
            s_ref[...] = jnp.zeros_like(s_ref)

        s_in = s_ref[...]
        hist_ref[...] = s_in
        og, s_out = _gla_chains(_stack_chains(q_ref, n_b, GLA_KSL), _stack_chains(k_ref, n_b, GLA_KSL),
                                _stack_chains(v_ref, n_b, DN_SL), _stack_chains(g_ref, n_b, DN_SL),
                                _per_chain(sm_ref, n_b), s_in, *_gate_weights(w2_ref, b2_ref, n_b), gn_ref[...])
        _unstack_chains(o_ref, og, n_b, DN_SL)
        s_ref[...] = s_out

    proj3 = proj.reshape(n_b, t_len, -1)
    o, hist = pl.pallas_call(
        body, name="gla_scan_fwd", grid=(n_c,),
        in_specs=[spec(256, OFF_GQ // 256), spec(256, OFF_GK // 256), spec(512, OFF_GV // 512),
                  spec(512, OFF_GG // 512), spec(128, OFF_SMALL // 128),
                  _const_spec((128, 256)), _const_spec((1, 256)), _const_spec((1, 128)),
                  pl.BlockSpec(memory_space=pl.ANY)],
        out_specs=[spec(512, 1), _hist_spec(n_b, GLA_KEY, n_c)],
        out_shape=[jax.ShapeDtypeStruct(o_mix.shape, BF16),
                   jax.ShapeDtypeStruct((n_c, n_b * HEADS, GLA_KEY, HEAD_DIM), F32)],
        input_output_aliases={8: 0},
        scratch_shapes=[pltpu.VMEM((n_b * HEADS, GLA_KEY, HEAD_DIM), F32)],
        compiler_params=_cparams(("arbitrary",)),
    )(proj3, proj3, proj3, proj3, proj3, w2, b2, gn, o_mix)
    return o.reshape(n_b * t_len, 2 * 512), hist


def _gla_scan_bwd(proj, w2, b2, gn, hist, d_o, dsm_dn, d_proj, n_b, t_len):
    n_c = t_len // CHUNK
    rev = functools.partial(_chunk_spec, n_b, n_c=n_c, reverse=True)

    def body(q_ref, k_ref, v_ref, g_ref, sm_ref, w2_ref, b2_ref, gn_ref, hist_ref, do_ref, dsm_dn_ref, _,
             dp_ref, dw2_ref, db2_ref, dgn_ref, ds_ref):
        first = pl.program_id(0) == 0

        @pl.when(first)
        def _():
            ds_ref[...] = jnp.zeros_like(ds_ref)

        _, pull = jax.vjp(_gla_chains, _stack_chains(q_ref, n_b, GLA_KSL), _stack_chains(k_ref, n_b, GLA_KSL),
                          _stack_chains(v_ref, n_b, DN_SL), _stack_chains(g_ref, n_b, DN_SL),
                          _per_chain(sm_ref, n_b), hist_ref[...], *_gate_weights(w2_ref, b2_ref, n_b), gn_ref[...])
        dq, dk, dv, dg, dsm, ds_in, dw2, db2, dgn = pull((_stack_chains(do_ref, n_b, DN_SL), ds_ref[...]))
        _unstack_chains(dp_ref, dq, n_b, GLA_KSL, offset=OFF_GQ)
        _unstack_chains(dp_ref, dk, n_b, GLA_KSL, offset=OFF_GK)
        _unstack_chains(dp_ref, dv, n_b, DN_SL, offset=OFF_GV)
        _unstack_chains(dp_ref, dg, n_b, DN_SL, offset=OFF_GG)
        ds_ref[...] = ds_in
        for b, dsm_b in enumerate(_sum_heads(dsm, n_b)):
            dp_ref[b, :, OFF_SMALL:OFF_SMALL + 128] = (dsm_b + dsm_dn_ref[b]).astype(BF16)
            dp_ref[b, :, OFF_SMALL + 128:GLA_W] = jnp.zeros((CHUNK, GLA_W - OFF_SMALL - 128), BF16)
        for h, ks in enumerate(GLA_KSL):
            _acc(dw2_ref, sum(dw2[b * HEADS + h] for b in range(n_b)), first, at=(slice(None), ks))
            _acc(db2_ref, sum(db2[b * HEADS + h] for b in range(n_b)), first, at=(slice(None), ks))
        _acc(dgn_ref, dgn, first)

    proj3, do3 = proj.reshape(n_b, t_len, -1), d_o.reshape(n_b, t_len, -1)
    return pl.pallas_call(
        body, name="gla_scan_bwd", grid=(n_c,),
        in_specs=[rev(256, OFF_GQ // 256), rev(256, OFF_GK // 256), rev(512, OFF_GV // 512), rev(512, OFF_GG // 512),
                  rev(128, OFF_SMALL // 128),
                  _const_spec((128, 256)), _const_spec((1, 256)), _const_spec((1, 128)),
                  _hist_spec(n_b, GLA_KEY, n_c, reverse=True), rev(512, 1), rev(128, 0),
                  pl.BlockSpec(memory_space=pl.ANY)],
        out_specs=[rev(GLA_W, 0), _const_spec((128, 256)), _const_spec((1, 256)), _const_spec((1, 128))],
        out_shape=[jax.ShapeDtypeStruct(d_proj.shape, BF16), jax.ShapeDtypeStruct((128, 256), F32),
                   jax.ShapeDtypeStruct((1, 256), F32), jax.ShapeDtypeStruct((1, 128), F32)],
        input_output_aliases={11: 0},
        scratch_shapes=[pltpu.VMEM((n_b * HEADS, GLA_KEY, HEAD_DIM), F32)],
        compiler_params=_cparams(("arbitrary",)),
    )(proj3, proj3, proj3, proj3, proj3, w2, b2, gn, hist, do3, dsm_dn, d_proj)


def _pad_w_in(w_in):
    zeros = jnp.zeros((w_in.shape[0], GLA_W - OFF_SMALL - 8 - GATE_RANK), w_in.dtype)
    return jnp.concatenate([w_in[:, 2056:3592], w_in[:, 2048:2056], w_in[:, 3592:3608], zeros,
                            w_in[:, 0:256], w_in[:, 1536:2048], w_in[:, 256:1536]], axis=1)


def _unpad_w_in(g):
    return jnp.concatenate([g[:, GLA_W:OFF_Z], g[:, OFF_Z + 512:PROJ_W], g[:, OFF_Z:OFF_Z + 512],
                            g[:, OFF_SMALL:OFF_SMALL + 8], g[:, 0:OFF_SMALL],
                            g[:, OFF_SMALL + 8:OFF_SMALL + 8 + GATE_RANK]], axis=1)


def _lane_vec(v, offset=0):
    return jnp.zeros((1, 128), F32).at[0, offset:offset + v.shape[0]].set(v)


def _local_step(x, tgt, mod, p, n_b, t_len, comm=None):
    row1 = lambda v: v.reshape(1, -1)
    a_log, dt_bias = _lane_vec(p["dn_a_log"]), _lane_vec(p["dn_dt_bias"])
    dn_gn, gla_gn = row1(p["dn_norm_g"]), row1(p["gla_norm_g"])
    w2 = jnp.zeros((128, 256), F32).at[8:8 + GATE_RANK].set(p["gla_w_gate2"])
    b2 = row1(p["gla_b_gate"])
    ln0_g, ln0_b, ln1_g, ln1_b, ln2_g, ln2_b = (row1(p[k]) for k in ("ln0_g", "ln0_b", "ln1_g", "ln1_b", "ln2_g", "ln2_b"))
    conv_b = row1(p["ffn_conv_b"])

    x0, h1 = _ln0_fwd(x, ln0_g, ln0_b, mod, n_b, t_len)
    proj = _mm(h1, p["w_in_p"], name="mm_proj")
    qkv = _dn_pre_fwd(proj, p["dn_conv"], n_b, t_len)
    o_half, hist_dn, landed = _dn_scan_fwd(qkv, proj, a_log, dt_bias, dn_gn, n_b, t_len,
                                           rider=comm.fwd_rider() if comm else None)
    if comm:
        p = {**p, **comm.weights_from(landed)}
    o_mix, hist_gla = _gla_scan_fwd(proj, w2, b2, gla_gn, o_half, n_b, t_len)
    y = _mm(o_mix, p["w_o"], name="mm_wo")
    x1, h2 = _ln1_fwd(x0, y, ln1_g, ln1_b, mod, n_b, t_len)
    up = _mm(h2, p["w_up"], name="mm_up")
    act = _ffn_act_fwd(up, p["ffn_conv"], conv_b, n_b, t_len)
    y2 = _mm(act, p["w_down"], name="mm_down")

    loss, dx1, dy2, g_ln2_g, g_ln2_b, dgt_f = _ln2_loss_bwd(x1, y2, ln2_g, ln2_b, mod, tgt, n_b, t_len)
    g_w_down = _mm(act, dy2, ta=True, name="mm_g_down")
    d_act = _mm(dy2, p["w_down"], tb=True, name="mm_d_act")
    d_up, g_ffn_conv, g_conv_b = _ffn_act_bwd(up, p["ffn_conv"], conv_b, d_act, n_b, t_len)
    g_w_up = _mm(h2, d_up, ta=True, out_slabs=N_CHIPS, name="mm_g_up")
    if comm:
        dh2, from_sibling = _mm(d_up, p["w_up"], tb=True, name="mm_d_h2", rider=comm.ffn_pair_rider(g_w_up, g_w_down))
    else:
        dh2 = _mm(d_up, p["w_up"], tb=True, name="mm_d_h2")
    dx0, dy, g_ln1_g, g_ln1_b, dmod_1 = _ln1_bwd(x0, y, ln1_g, ln1_b, mod, dx1, dh2, n_b, t_len)
    g_w_o = _mm(o_mix, dy, ta=True, name="mm_g_wo")
    d_o = _mm(dy, p["w_o"], tb=True, name="mm_d_o")
    dqkv, d_proj, dsm_dn, g_a_log, g_dt_bias, g_dn_gn, ffn_from_chips = _dn_scan_bwd(
        qkv, proj, a_log, dt_bias, dn_gn, hist_dn, d_o, n_b, t_len,
        rider=comm.ffn_chips_rider(from_sibling) if comm else None)
    d_proj, g_w2, g_b2, g_gla_gn = _gla_scan_bwd(proj, w2, b2, gla_gn, hist_gla, d_o, dsm_dn, d_proj, n_b, t_len)
    d_proj, g_dn_conv = _dn_pre_bwd(proj, p["dn_conv"], dqkv, d_proj.reshape(n_b * t_len, PROJ_W), n_b, t_len)
    g_w_in_p = _mm(h1, d_proj, ta=True, name="mm_g_win")
    if comm:
        dh1, tail_from_chips = _mm(d_proj, p["w_in_p"], tb=True, name="mm_d_h1",
                                   rider=comm.tail_chips_rider(g_w_in_p, g_w_o))
        from_chips = (ffn_from_chips, tail_from_chips)
    else:
        dh1, from_chips = _mm(d_proj, p["w_in_p"], tb=True, name="mm_d_h1"), None
    grad_x, g_ln0_g, g_ln0_b, dmod_0 = _ln0_bwd(x, ln0_g, ln0_b, mod, dx0, dh1, n_b, t_len)

    dmod = jnp.concatenate([dmod_0, dmod_1[:, 0:1], dmod_1[:, 1:3], dgt_f], axis=1)
    grads = {
        "ln0_g": g_ln0_g[0], "ln0_b": g_ln0_b[0], "w_in_p": g_w_in_p, "dn_conv": g_dn_conv,
        "dn_a_log": g_a_log[0, 0:HEADS], "dn_dt_bias": g_dt_bias[0, 0:HEADS], "dn_norm_g": g_dn_gn[0],
        "gla_w_gate2": g_w2[8:8 + GATE_RANK], "gla_b_gate": g_b2[0], "gla_norm_g": g_gla_gn[0],
        "w_o": g_w_o, "ln1_g": g_ln1_g[0], "ln1_b": g_ln1_b[0], "w_up": g_w_up,
        "ffn_conv": jnp.concatenate([g_ffn_conv[0], g_ffn_conv[1]], axis=1),
        "ffn_conv_b": jnp.concatenate([g_conv_b[0, 0], g_conv_b[1, 0]]), "w_down": g_w_down,
        "ln2_g": g_ln2_g[0], "ln2_b": g_ln2_b[0],
    }
    return loss, grad_x, grads, dmod, from_chips


def _ada_fwd(c_all, w_shard, b_shard):
    n_all, n_col = c_all.shape[0], w_shard.shape[1]
    tn = 512

    def body(c_ref, w_ref, b_ref, cond_ref, mod_ref):
        cond = _silu(c_ref[...])
        cond_ref[...] = cond
        mod_ref[...] = jnp.dot(cond.astype(BF16), w_ref[...].astype(BF16), preferred_element_type=F32) + b_ref[...]

    return pl.pallas_call(
        body, name="ada_fwd", grid=(n_col // tn,),
        in_specs=[pl.BlockSpec((n_all, D_MODEL), lambda j: (0, 0)), pl.BlockSpec((D_MODEL, tn), lambda j: (0, j)),
                  pl.BlockSpec((1, tn), lambda j: (0, j))],
        out_specs=[pl.BlockSpec((n_all, D_MODEL), lambda j: (0, 0)), pl.BlockSpec((n_all, tn), lambda j: (0, j))],
        out_shape=[jax.ShapeDtypeStruct((n_all, D_MODEL), F32), jax.ShapeDtypeStruct((n_all, n_col), F32)],
        compiler_params=_cparams(("arbitrary",)),
    )(c_all, w_shard, b_shard)


def _col_sum(a):
    def body(a_ref, o_ref):
        o_ref[...] = jnp.sum(a_ref[...], 0, keepdims=True)

    return pl.pallas_call(body, name="col_sum", out_shape=jax.ShapeDtypeStruct((1, a.shape[1]), F32))(a)


def _adamw(w, g, m, v, name):
    n_r, n_c = w.shape
    if n_r % 8 == 0:
        tr = _pick(n_r, (256, 64, 32, 16, 8))
        grid, blk = (n_r // tr,), pl.BlockSpec((tr, n_c), lambda i: (i, 0))
    else:
        tc = _pick(n_c, (256, 128))
        grid, blk = (n_c // tc,), pl.BlockSpec((n_r, tc), lambda i: (0, i))

    def body(w_ref, g_ref, m_ref, v_ref, d_ref, nm_ref, nv_ref):
        grad = g_ref[...]
        new_m = ADAM_B1 * m_ref[...] + (1.0 - ADAM_B1) * grad
        new_v = ADAM_B2 * v_ref[...] + (1.0 - ADAM_B2) * (grad * grad)
        m_hat = new_m / (1.0 - ADAM_B1 ** ADAM_STEP)
        v_hat = new_v / (1.0 - ADAM_B2 ** ADAM_STEP)
        d_ref[...] = -ADAM_LR * (m_hat / (jnp.sqrt(v_hat) + ADAM_EPS) + ADAM_WD * w_ref[...])
        nm_ref[...] = new_m
        nv_ref[...] = new_v

    out = jax.ShapeDtypeStruct(w.shape, F32)
    return pl.pallas_call(
        body, name=name, grid=grid, in_specs=[blk] * 4, out_specs=[blk] * 3, out_shape=[out] * 3,
        compiler_params=_cparams(("parallel",)),
    )(w, g, m, v)


HBM_SPEC = pl.BlockSpec(memory_space=pltpu.HBM)
VMEM_SPEC = pl.BlockSpec(memory_space=pltpu.VMEM)
CHIP_FLIPS = ((1, 0), (0, 1), (1, 1))


def _place():
    return lax.axis_index("x"), lax.axis_index("y"), lax.axis_index("c")


def _flip(v, f):
    return 1 - v if f else v


def _all_gather8(slab, name):
    n_r, n_w = slab.shape

    def body(x_ref, o_ref, s_ref, send_sems, recv_sems, local_sem):
        x, y, c = _place()
        me = 4 * x + 2 * y + c
        mine = pltpu.make_async_copy(x_ref, o_ref.at[me], local_sem)
        mine.start()
        peers = [(_flip(x, k & 4), _flip(y, k & 2), _flip(c, k & 1)) for k in range(1, N_DEV)]
        sends = []
        for k, peer in enumerate(peers):
            cp = pltpu.make_async_remote_copy(src_ref=x_ref, dst_ref=o_ref.at[me], send_sem=send_sems.at[k],
                                              recv_sem=recv_sems.at[k], device_id=peer, device_id_type=MESH)
            cp.start()
            sends.append(cp)
        for k, (px, py, pc) in enumerate(peers):
            pltpu.make_async_remote_copy(src_ref=x_ref, dst_ref=o_ref.at[4 * px + 2 * py + pc],
                                         send_sem=send_sems.at[k], recv_sem=recv_sems.at[k],
                                         device_id=(px, py, pc), device_id_type=MESH).wait_recv()
        for cp in sends:
            cp.wait_send()
        mine.wait()
        total = o_ref[0]
        for d in range(1, N_DEV):
            total = total + o_ref[d]
        s_ref[...] = total

    return pl.pallas_call(
        body, name=name, in_specs=[VMEM_SPEC], out_specs=[VMEM_SPEC, VMEM_SPEC],
        out_shape=[jax.ShapeDtypeStruct((N_DEV, n_r, n_w), F32), jax.ShapeDtypeStruct((n_r, n_w), F32)],
        scratch_shapes=[pltpu.SemaphoreType.DMA((N_DEV - 1,)), pltpu.SemaphoreType.DMA((N_DEV - 1,)),
                        pltpu.SemaphoreType.DMA],
    )(slab)


def _gather_weights(shards):
    n_a = len(shards)

    def body(*refs):
        ins, outs, stage = refs[:n_a], refs[n_a:2 * n_a], refs[2 * n_a:3 * n_a]
        send_sems, recv_sems, local_sems = refs[3 * n_a:]
        x, y, c = _place()
        me_chip = 2 * x + y
        sibling = (x, y, 1 - c)
        chips = [(_flip(x, fx), _flip(y, fy)) for fx, fy in CHIP_FLIPS]
        stage_in = [pltpu.make_async_copy(ins[k], stage[k], local_sems.at[k]) for k in range(n_a)]
        for cp in stage_in:
            cp.start()

        def copy(k, slot, chip_of_block, half, to, src=None):
            dst = outs[k].at[chip_of_block, half]
            return pltpu.make_async_remote_copy(src_ref=dst if src is None else src, dst_ref=dst,
                                                send_sem=send_sems.at[k * 6 + slot], recv_sem=recv_sems.at[k * 6 + slot],
                                                device_id=to, device_id_type=MESH)

        first = [copy(k, r, me_chip, c, (*chips[r], c), src=ins[k].at[c]) for k in range(n_a) for r in range(3)]
        for cp in first:
            cp.start()
        stage_out = []
        for k in range(n_a):
            stage_in[k].wait()
            cp = pltpu.make_async_copy(stage[k], outs[k].at[me_chip], local_sems.at[n_a + k])
            cp.start()
            stage_out.append(cp)
        passed = []
        for k in range(n_a):
            for r, (px, py) in enumerate(chips):
                copy(k, r, 2 * px + py, c, (x, y, c)).wait_recv()
                fwd = copy(k, 3 + r, 2 * px + py, c, sibling)
                fwd.start()
                passed.append(fwd)
        for k in range(n_a):
            for r, (px, py) in enumerate(chips):
                copy(k, 3 + r, 2 * px + py, 1 - c, (x, y, c)).wait_recv()
        for cp in first + passed:
            cp.wait_send()
        for cp in stage_out:
            cp.wait()

    return pl.pallas_call(
        body, name="gather_weights", in_specs=[HBM_SPEC] * n_a, out_specs=[HBM_SPEC] * n_a,
        out_shape=[jax.ShapeDtypeStruct((N_CHIPS,) + s.shape, s.dtype) for s in shards],
        scratch_shapes=[pltpu.VMEM(s.shape, s.dtype) for s in shards]
        + [pltpu.SemaphoreType.DMA((6 * n_a,)), pltpu.SemaphoreType.DMA((6 * n_a,)),
           pltpu.SemaphoreType.DMA((2 * n_a,))],
        compiler_params=pltpu.CompilerParams(vmem_limit_bytes=VMEM_LIMIT),
    )(*shards)


def _gather_rider(shards):
    n_a = len(shards)

    def plan(ins, outs, sems):
        send_sems, recv_sems = sems
        x, y, c = _place()
        chips = [(_flip(x, fx), _flip(y, fy)) for fx, fy in CHIP_FLIPS]

        def copy(k, slot, chip_of_block, half, to, src=None):
            dst = outs[k].at[chip_of_block, half]
            return pltpu.make_async_remote_copy(src_ref=dst if src is None else src, dst_ref=dst,
                                                send_sem=send_sems.at[k * 6 + slot], recv_sem=recv_sems.at[k * 6 + slot],
                                                device_id=to, device_id_type=MESH)

        first = [copy(k, r, 2 * x + y, c, (*chips[r], c), src=ins[k].at[c]) for k in range(n_a) for r in range(3)]
        return copy, chips, first, (x, y, c)

    def first_step(ins, outs, sems):
        for cp in plan(ins, outs, sems)[2]:
            cp.start()

    def last_step(ins, outs, sems):
        copy, chips, first, (x, y, c) = plan(ins, outs, sems)
        passed = []
        for k in range(n_a):
            for r, (px, py) in enumerate(chips):
                copy(k, r, 2 * px + py, c, (x, y, c)).wait_recv()
                fwd = copy(k, 3 + r, 2 * px + py, c, (x, y, 1 - c))
                fwd.start()
                passed.append(fwd)
        for k in range(n_a):
            for r, (px, py) in enumerate(chips):
                copy(k, 3 + r, 2 * px + py, 1 - c, (x, y, c)).wait_recv()
        for cp in first + passed:
            cp.wait_send()

    return Rider(shards, [jax.ShapeDtypeStruct((N_CHIPS,) + s.shape, s.dtype) for s in shards],
                 [pltpu.SemaphoreType.DMA((6 * n_a,)), pltpu.SemaphoreType.DMA((6 * n_a,))], first_step, last_step)


def _place_own(gathered, shard, chip, name):
    _, _, n_h, n_c = gathered.shape
    th = _pick(n_h, (256, 176, 128))

    def body(sel_ref, s_ref, _, o_ref):
        o_ref[...] = s_ref[...]

    grid_spec = pltpu.PrefetchScalarGridSpec(
        num_scalar_prefetch=1, grid=(2, n_h // th),
        in_specs=[pl.BlockSpec((None, th, n_c), lambda hf, i, sel: (hf, i, 0)), pl.BlockSpec(memory_space=pl.ANY)],
        out_specs=pl.BlockSpec((None, None, th, n_c), lambda hf, i, sel: (sel[0], hf, i, 0)))
    return pl.pallas_call(
        body, name=name, grid_spec=grid_spec, out_shape=jax.ShapeDtypeStruct(gathered.shape, gathered.dtype),
        input_output_aliases={2: 0}, compiler_params=_cparams(("parallel", "parallel")),
    )(chip.reshape(1), shard, gathered)


def _pair_rider(parts):
    n_a = len(parts)

    def plan(ins, outs, sems):
        send_sems, recv_sems = sems
        x, y, c = _place()
        return [pltpu.make_async_remote_copy(src_ref=ins[k].at[:, 1 - c], dst_ref=outs[k], send_sem=send_sems.at[k],
                                             recv_sem=recv_sems.at[k], device_id=(x, y, 1 - c), device_id_type=MESH)
                for k in range(n_a)]

    def first_step(ins, outs, sems):
        for cp in plan(ins, outs, sems):
            cp.start()

    def last_step(ins, outs, sems):
        for cp in plan(ins, outs, sems):
            cp.wait()

    return Rider(parts, [jax.ShapeDtypeStruct((N_CHIPS,) + p.shape[2:], F32) for p in parts],
                 [pltpu.SemaphoreType.DMA((n_a,)), pltpu.SemaphoreType.DMA((n_a,))], first_step, last_step)


def _alone(rider, name):
    n_a = len(rider.inputs)

    def body(*refs):
        parts = (refs[:n_a], refs[n_a:2 * n_a], refs[2 * n_a:])
        rider.first(*parts)
        rider.last(*parts)

    return pl.pallas_call(
        body, name=name, in_specs=[HBM_SPEC] * n_a, out_specs=[HBM_SPEC] * n_a,
        out_shape=rider.out_shapes, scratch_shapes=rider.sems,
    )(*rider.inputs)


def _chips_rider(sums):
    n_a = len(sums)

    def plan(ins, outs, sems):
        send_sems, recv_sems = sems
        x, y, c = _place()
        cps = []
        for k in range(n_a):
            for r, (fx, fy) in enumerate(CHIP_FLIPS):
                px, py = _flip(x, fx), _flip(y, fy)
                cps.append(pltpu.make_async_remote_copy(
                    src_ref=ins[k].at[2 * px + py], dst_ref=outs[k].at[r], send_sem=send_sems.at[3 * k + r],
                    recv_sem=recv_sems.at[3 * k + r], device_id=(px, py, c), device_id_type=MESH))
        return cps

    def first_step(ins, outs, sems):
        for cp in plan(ins, outs, sems):
            cp.start()

    def last_step(ins, outs, sems):
        for cp in plan(ins, outs, sems):
            cp.wait()

    return Rider(sums, [jax.ShapeDtypeStruct((3,) + s.shape[1:], s.dtype) for s in sums],
                 [pltpu.SemaphoreType.DMA((3 * n_a,)), pltpu.SemaphoreType.DMA((3 * n_a,))], first_step, last_step)


def _rs_share(bufs):
    n_a = len(bufs)

    def body(*refs):
        ins, outs = refs[:n_a], refs[n_a:2 * n_a]
        send_sems, recv_sems = refs[2 * n_a:]
        x, y, c = _place()
        sends = [pltpu.make_async_remote_copy(src_ref=ins[k].at[c], dst_ref=outs[k].at[c], send_sem=send_sems.at[k],
                                              recv_sem=recv_sems.at[k], device_id=(x, y, 1 - c), device_id_type=MESH)
                 for k in range(n_a)]
        for cp in sends:
            cp.start()
        for k in range(n_a):
            pltpu.make_async_remote_copy(src_ref=ins[k].at[c], dst_ref=outs[k].at[1 - c], send_sem=send_sems.at[k],
                                         recv_sem=recv_sems.at[k], device_id=(x, y, 1 - c),
                                         device_id_type=MESH).wait_recv()
        for cp in sends:
            cp.wait_send()

    return pl.pallas_call(
        body, name="rs_share", in_specs=[HBM_SPEC] * n_a, out_specs=[HBM_SPEC] * n_a,
        out_shape=[jax.ShapeDtypeStruct(s.shape, F32) for s in bufs],
        input_output_aliases={k: k for k in range(n_a)},
        scratch_shapes=[pltpu.SemaphoreType.DMA((n_a,)), pltpu.SemaphoreType.DMA((n_a,))],
    )(*bufs)


def _pair_add(part, recv, core, name):
    _, _, n_h, n_c = part.shape
    th = _pick(n_h, (256, 176, 128))

    def body(sel_ref, p_ref, r_ref, o_ref):
        o_ref[...] = (p_ref[...] + r_ref[...]).astype(BF16)

    grid_spec = pltpu.PrefetchScalarGridSpec(
        num_scalar_prefetch=1, grid=(N_CHIPS, n_h // th),
        in_specs=[pl.BlockSpec((None, None, th, n_c), lambda j, i, sel: (j, sel[0], i, 0)),
                  pl.BlockSpec((None, th, n_c), lambda j, i, sel: (j, i, 0))],
        out_specs=pl.BlockSpec((None, th, n_c), lambda j, i, sel: (j, i, 0)))
    return pl.pallas_call(
        body, name=name, grid_spec=grid_spec, out_shape=jax.ShapeDtypeStruct(recv.shape, BF16),
        compiler_params=_cparams(("parallel", "parallel")),
    )(core.reshape(1), part, recv)


def _chip_add(sums, recv, chip, core, name):
    _, n_h, n_c = sums.shape
    th = _pick(n_h, (256, 176, 128))

    def body(sel_ref, s_ref, r_ref, o_ref):
        total = s_ref[...].astype(F32)
        for r in range(3):
            total = total + r_ref[r].astype(F32)
        o_ref[...] = total

    grid_spec = pltpu.PrefetchScalarGridSpec(
        num_scalar_prefetch=1, grid=(n_h // th,),
        in_specs=[pl.BlockSpec((None, th, n_c), lambda i, sel: (sel[0], i, 0)),
                  pl.BlockSpec((3, th, n_c), lambda i, sel: (0, i, 0))],
        out_specs=pl.BlockSpec((None, th, n_c), lambda i, sel: (sel[1], i, 0)))
    return pl.pallas_call(
        body, name=name, grid_spec=grid_spec, out_shape=jax.ShapeDtypeStruct((2, n_h, n_c), F32),
        compiler_params=_cparams(("parallel",)),
    )(jnp.stack([chip, core]), sums, recv)


def _row_halves(a):
    return a.reshape(N_CHIPS, 2, -1, a.shape[-1])


def _by_cols(a, n_cols):
    return a.reshape(a.shape[0], N_CHIPS, n_cols).transpose(1, 0, 2)


class StepComm:
    REST = ("w_o", "w_up", "w_down")

    def __init__(self, core, chip, rest_shards, in_cols):
        self.core, self.chip, self.shards, self.in_cols = core, chip, rest_shards, in_cols

    def fwd_rider(self):
        return _gather_rider(self.shards)

    def weights_from(self, landed):
        g_o, g_up, g_down = (_place_own(g, s, self.chip, "place_own_" + n)
                             for g, s, n in zip(landed, self.shards, self.REST))
        return {"w_o": g_o.reshape(-1, D_MODEL), "w_up": g_up.reshape(N_CHIPS, -1, g_up.shape[-1]),
                "w_down": g_down.reshape(-1, D_MODEL)}

    def _add_pairs(self, parts, from_sibling, names):
        return [_pair_add(p, r, self.core, "pair_add_" + n) for p, r, n in zip(parts, from_sibling, names)]

    def ffn_pair_rider(self, g_w_up, g_w_down):
        self.ffn_parts = [_row_halves(g_w_up), _row_halves(g_w_down)]
        return _pair_rider(self.ffn_parts)

    def ffn_chips_rider(self, from_sibling):
        self.ffn_sums = self._add_pairs(self.ffn_parts, from_sibling, ("w_up", "w_down"))
        return _chips_rider(self.ffn_sums)

    def tail_chips_rider(self, g_w_in_p, g_w_o):
        parts = [_row_halves(_by_cols(_unpad_w_in(g_w_in_p), self.in_cols)), _row_halves(g_w_o)]
        self.tail_sums = self._add_pairs(parts, _alone(_pair_rider(parts), "rs_pair_tail"), ("w_in", "w_o"))
        return _chips_rider(self.tail_sums)

    def finish(self, ffn_from_chips, tail_from_chips):
        halves = [_chip_add(s, r, self.chip, self.core, "chip_add_" + n)
                  for s, r, n in zip(self.tail_sums + self.ffn_sums, list(tail_from_chips) + list(ffn_from_chips),
                                     ("w_in", "w_o", "w_up", "w_down"))]
        return [f.reshape(-1, f.shape[-1]) for f in _rs_share(halves)]


SLAB_W = 1024


def _pack(arrays, rows):
    flat = jnp.concatenate([a.reshape(-1).astype(F32) for a in arrays])
    return jnp.pad(flat, (0, rows * SLAB_W - flat.shape[0])).reshape(rows, SLAB_W)


def _unpack(flat, shapes):
    out, off = [], 0
    for s in shapes:
        n = 1
        for d in s:
            n *= d
        out.append(flat[off:off + n].reshape(s))
        off += n
    return out


def _rows_for(arrays_or_shapes):
    n = 0
    for a in arrays_or_shapes:
        s = a if isinstance(a, tuple) else a.shape
        k = 1
        for d in s:
            k *= d
        n += k
    return -(-n // (8 * SLAB_W)) * 8


def kernel(x, c, ln0_g, ln0_b, w_ada, b_ada, w_in, dn_conv, dn_a_log, dn_dt_bias, dn_norm_g, gla_w_gate2, gla_b_gate, gla_norm_g, w_o, ln1_g, ln1_b, ffn_w_up, ffn_conv, ffn_conv_b, ffn_w_down, ln2_g, ln2_b, loss_target, m_ln0_g, m_ln0_b, m_w_ada, m_b_ada, m_w_in, m_dn_conv, m_dn_a_log, m_dn_dt_bias, m_dn_norm_g, m_gla_w_gate2, m_gla_b_gate, m_gla_norm_g, m_w_o, m_ln1_g, m_ln1_b, m_ffn_w_up, m_ffn_conv, m_ffn_conv_b, m_ffn_w_down, m_ln2_g, m_ln2_b, v_ln0_g, v_ln0_b, v_w_ada, v_b_ada, v_w_in, v_dn_conv, v_dn_a_log, v_dn_dt_bias, v_dn_norm_g, v_gla_w_gate2, v_gla_b_gate, v_gla_norm_g, v_w_o, v_ln1_g, v_ln1_b, v_ffn_w_up, v_ffn_conv, v_ffn_conv_b, v_ffn_w_down, v_ln2_g, v_ln2_b):
    n_b, t_len, _ = x.shape
    xi, yi, ci = _place()
    chip = (2 * xi + yi).astype(jnp.int32)
    core = ci.astype(jnp.int32)
    me = 2 * chip + core
    n_all = N_DEV * n_b
    ada_cols = w_ada.shape[2]

    sharded_small = [dn_conv[0], gla_w_gate2[0], ffn_conv[0]]
    slab = _pack([c] + sharded_small, _rows_for([c] + sharded_small))
    gathered, _ = _all_gather8(slab, "gather_small")
    flat = gathered.reshape(N_DEV, -1)
    per_dev = [_unpack(flat[d], [c.shape] + [a.shape for a in sharded_small]) for d in range(N_DEV)]
    c_all = jnp.concatenate([per_dev[d][0] for d in range(N_DEV)], axis=0)
    dn_conv_f, gate2_f, ffn_conv_f = (jnp.concatenate([per_dev[2 * j][i] for j in range(N_CHIPS)], axis=1)
                                      for i in (1, 2, 3))

    b_ada_shard = lax.dynamic_slice(b_ada, (0, chip * ada_cols), (1, ada_cols))
    cond_all, mod_cols = _ada_fwd(c_all, w_ada[0], b_ada_shard)
    mod_g, _ = _all_gather8(mod_cols, "gather_mod")
    mod_full = jnp.concatenate([mod_g[2 * j] for j in range(N_CHIPS)], axis=1)
    mod = lax.dynamic_slice(mod_full, (me * n_b, 0), (n_b, 6 * D_MODEL)).reshape(n_b, 6, D_MODEL)

    halves = lambda a: a.astype(BF16).reshape(2, a.shape[0] // 2, a.shape[1])
    (g_in,) = _gather_weights([halves(w_in[0])])
    comm = StepComm(core, chip, [halves(w_o[0]), halves(ffn_w_up[0]), halves(ffn_w_down[0])], w_in.shape[2])
    cols = lambda g: g.reshape(N_CHIPS, -1, g.shape[-1]).transpose(1, 0, 2).reshape(-1, N_CHIPS * g.shape[-1])
    params = {
        "w_in_p": _pad_w_in(cols(g_in)),
        "dn_conv": dn_conv_f, "dn_a_log": dn_a_log[0], "dn_dt_bias": dn_dt_bias[0], "dn_norm_g": dn_norm_g[0],
        "gla_w_gate2": gate2_f, "gla_b_gate": gla_b_gate[0], "gla_norm_g": gla_norm_g[0],
        "ln0_g": ln0_g, "ln0_b": ln0_b, "ln1_g": ln1_g[0], "ln1_b": ln1_b[0], "ln2_g": ln2_g[0], "ln2_b": ln2_b[0],
        "ffn_conv": ffn_conv_f, "ffn_conv_b": ffn_conv_b[0],
    }

    loss_row, grad_x, gp, dmod, from_chips = _local_step(
        x.reshape(n_b * t_len, D_MODEL), loss_target.reshape(n_b * t_len, D_MODEL), mod, params, n_b, t_len, comm)
    loss = lax.psum(loss_row[0, 0], ("x", "y", "c"))

    summed_names = ["ln0_g", "ln0_b", "dn_conv", "dn_a_log", "dn_dt_bias", "dn_norm_g", "gla_w_gate2", "gla_b_gate",
                    "gla_norm_g", "ln1_g", "ln1_b", "ffn_conv", "ffn_conv_b", "ln2_g", "ln2_b"]
    summed_parts = [gp[n] for n in summed_names]
    sum_rows = _rows_for(summed_parts)
    slab = jnp.concatenate([_pack(summed_parts, sum_rows), _pack([dmod], _rows_for([dmod]))], axis=0)
    gathered, total = _all_gather8(slab, "reduce_small")
    small_g = dict(zip(summed_names, _unpack(total.reshape(-1), [a.shape for a in summed_parts])))
    dmod_rows = n_b * 6 * D_MODEL // SLAB_W
    dmod_all = gathered[:, sum_rows:sum_rows + dmod_rows, :].reshape(n_all, 6 * D_MODEL)

    g_b_ada = _col_sum(dmod_all)
    dmod_cols = lax.dynamic_slice(dmod_all, (0, chip * ada_cols), (n_all, ada_cols))
    g_w_ada = _mm(cond_all, dmod_cols, ta=True, name="mm_g_ada")

    g_w_in, g_w_o, g_w_up, g_w_down = comm.finish(*from_chips)

    col_block = lambda a: lax.dynamic_slice(a, (0, chip * (a.shape[1] // N_CHIPS)), (a.shape[0], a.shape[1] // N_CHIPS))
    grads = {
        "ln0_g": small_g["ln0_g"], "ln0_b": small_g["ln0_b"], "w_ada": g_w_ada[None], "b_ada": g_b_ada,
        "w_in": g_w_in[None], "dn_conv": col_block(small_g["dn_conv"])[None], "dn_a_log": small_g["dn_a_log"][None],
        "dn_dt_bias": small_g["dn_dt_bias"][None], "dn_norm_g": small_g["dn_norm_g"][None],
        "gla_w_gate2": col_block(small_g["gla_w_gate2"])[None], "gla_b_gate": small_g["gla_b_gate"][None],
        "gla_norm_g": small_g["gla_norm_g"][None], "w_o": g_w_o[None], "ln1_g": small_g["ln1_g"][None],
        "ln1_b": small_g["ln1_b"][None], "ffn_w_up": g_w_up[None], "ffn_conv": col_block(small_g["ffn_conv"])[None],
        "ffn_conv_b": small_g["ffn_conv_b"][None], "ffn_w_down": g_w_down[None], "ln2_g": small_g["ln2_g"][None],
        "ln2_b": small_g["ln2_b"][None],
    }
    names = ["ln0_g", "ln0_b", "w_ada", "b_ada", "w_in", "dn_conv", "dn_a_log", "dn_dt_bias", "dn_norm_g",
             "gla_w_gate2", "gla_b_gate", "gla_norm_g", "w_o", "ln1_g", "ln1_b", "ffn_w_up", "ffn_conv", "ffn_conv_b",
             "ffn_w_down", "ln2_g", "ln2_b"]
    weights = dict(zip(names, [ln0_g, ln0_b, w_ada, b_ada, w_in, dn_conv, dn_a_log, dn_dt_bias, dn_norm_g, gla_w_gate2,
                               gla_b_gate, gla_norm_g, w_o, ln1_g, ln1_b, ffn_w_up, ffn_conv, ffn_conv_b, ffn_w_down,
                               ln2_g, ln2_b]))
    m_in = dict(zip(names, [m_ln0_g, m_ln0_b, m_w_ada, m_b_ada, m_w_in, m_dn_conv, m_dn_a_log, m_dn_dt_bias,
                            m_dn_norm_g, m_gla_w_gate2, m_gla_b_gate, m_gla_norm_g, m_w_o, m_ln1_g, m_ln1_b,
                            m_ffn_w_up, m_ffn_conv, m_ffn_conv_b, m_ffn_w_down, m_ln2_g, m_ln2_b]))
    v_in = dict(zip(names, [v_ln0_g, v_ln0_b, v_w_ada, v_b_ada, v_w_in, v_dn_conv, v_dn_a_log, v_dn_dt_bias,
                            v_dn_norm_g, v_gla_w_gate2, v_gla_b_gate, v_gla_norm_g, v_w_o, v_ln1_g, v_ln1_b,
                            v_ffn_w_up, v_ffn_conv, v_ffn_conv_b, v_ffn_w_down, v_ln2_g, v_ln2_b]))

    big = ("w_ada", "w_in", "w_o", "ffn_w_up", "ffn_w_down")
    delta, new_m, new_v = {}, {}, {}
    for n in big:
        view = (lambda a: a.T) if n == "w_in" else (lambda a: a)
        d_n, m_n, v_n = _adamw(view(weights[n][0]), view(grads[n][0]), view(m_in[n][0]), view(v_in[n][0]), "adamw_" + n)
        delta[n], new_m[n], new_v[n] = view(d_n)[None], view(m_n)[None], view(v_n)[None]
    small = [n for n in names if n not in big]
    shapes = [weights[n].shape for n in small]
    rows = _rows_for(shapes)
    d_s, m_s, v_s = _adamw(_pack([weights[n] for n in small], rows), _pack([grads[n] for n in small], rows),
                           _pack([m_in[n] for n in small], rows), _pack([v_in[n] for n in small], rows), "adamw_small")
    for out, slab_out in ((delta, d_s), (new_m, m_s), (new_v, v_s)):
        out.update(zip(small, _unpack(slab_out.reshape(-1), shapes)))

    return (loss, grad_x.reshape(x.shape), *[grads[n] for n in names], *[delta[n] for n in names],
            *[new_m[n] for n in names], *[new_v[n] for n in names])
```

```python
import functools

import jax
import jax.numpy as jnp
from jax import lax
from jax.experimental import pallas as pl
from jax.experimental.pallas import tpu as pltpu

F32 = jnp.float32
BF16 = jnp.bfloat16
MESH = pl.DeviceIdType.MESH

D_MODEL = 1024
HEADS = 4
HEAD_DIM = 128
GLA_KEY = 64
GATE_RANK = 16
CHUNK = 64
D_FF = 2816
ALPHA = 2.0 ** 0.25
EPS = 1e-6
N_CHIPS = 4
N_DEV = 8

PROJ_W = 3840
OFF_GQ, OFF_GK, OFF_GV, OFF_GG, OFF_SMALL, GLA_W = 0, 256, 512, 1024, 1536, 1792
OFF_Z = 2048
W_IN_COLS = 3608


def _qkv_block(j):
    return jnp.where(j < 2, GLA_W // 128 + j, (OFF_Z + 512) // 128 - 2 + j)

ADAM_LR, ADAM_B1, ADAM_B2, ADAM_EPS, ADAM_WD, ADAM_STEP = 0.001, 0.9, 0.999, 1e-08, 0.01, 10

VMEM_LIMIT = 56 * 1024 * 1024
ROW_TILE = 256


def _cparams(sem):
    return pltpu.CompilerParams(dimension_semantics=sem, vmem_limit_bytes=VMEM_LIMIT)


def _pick(n, prefs):
    for p in prefs:
        if n % p == 0:
            return p
    return n


def _mm(a, b, *, ta=False, tb=False, out_slabs=1, out_dtype=F32, name, rider=None):
    a_slabs = a.shape[0] if a.ndim == 3 else 1
    b_slabs = b.shape[0] if b.ndim == 3 else 1
    assert not (ta and a_slabs > 1)
    a2, b2 = a.shape[-2:], b.shape[-2:]
    if ta:
        k_dim, m_dim = a2
    else:
        m_dim, k_dim = a2[0], a2[1] * a_slabs
    n_dim = b2[0] if tb else b2[1] * b_slabs
    k_slabs = max(a_slabs, b_slabs if tb else 1)
    n_slabs = max(out_slabs, 1 if tb else b_slabs)
    tm = _pick(m_dim, (1024, 1408, 512, 256, 128))
    tn = _pick(n_dim // n_slabs, (1536, 1408, 1280, 1024, 768, 512, 384, 256, 128))
    tk = _pick(k_dim // k_slabs, (1408, 1280, 1024, 512, 256, 128))
    nk, nj = k_dim // tk, n_dim // tn
    nk_a, nk_b, nj_b, nj_o = nk // a_slabs, nk // b_slabs, nj // b_slabs, nj // out_slabs
    dims = (((0 if ta else 1,), (1 if tb else 0,)), ((), ()))

    grid = (m_dim // tm, nj, nk)
    assert out_dtype == F32
    r_inputs, r_in_specs, r_out_specs, r_sems, split = _with_rider(rider, 2, 1, 0)

    def body(*refs):
        (a_ref, b_ref, o_ref), parts = split(refs)
        ride_first, ride_last = _ride(rider, parts, grid)
        if rider is not None:
            ride_first()
        prod = lax.dot_general(a_ref[...].astype(BF16), b_ref[...].astype(BF16), dims, preferred_element_type=F32)
        if nk == 1:
            o_ref[...] = prod
        else:
            _acc(o_ref, prod, pl.program_id(2) == 0)
        if rider is not None:
            ride_last()

    if ta:
        a_spec = pl.BlockSpec((tk, tm), lambda i, j, k: (k, i))
    elif a_slabs > 1:
        a_spec = pl.BlockSpec((None, tm, tk), lambda i, j, k: (k // nk_a, i, k % nk_a))
    else:
        a_spec = pl.BlockSpec((tm, tk), lambda i, j, k: (i, k))
    if tb and b_slabs > 1:
        b_spec = pl.BlockSpec((None, tn, tk), lambda i, j, k: (k // nk_b, j, k % nk_b))
    elif tb:
        b_spec = pl.BlockSpec((tn, tk), lambda i, j, k: (j, k))
    elif b_slabs > 1:
        b_spec = pl.BlockSpec((None, tk, tn), lambda i, j, k: (j // nj_b, k, j % nj_b))
    else:
        b_spec = pl.BlockSpec((tk, tn), lambda i, j, k: (k, j))
    if out_slabs > 1:
        o_spec = pl.BlockSpec((None, tm, tn), lambda i, j, k: (j // nj_o, i, j % nj_o))
        o_shape = (out_slabs, m_dim, n_dim // out_slabs)
    else:
        o_spec, o_shape = pl.BlockSpec((tm, tn), lambda i, j, k: (i, j)), (m_dim, n_dim)
    out, *rider_outs = pl.pallas_call(
        body, name=name, grid=grid,
        in_specs=[a_spec, b_spec] + r_in_specs, out_specs=[o_spec] + r_out_specs,
        out_shape=[jax.ShapeDtypeStruct(o_shape, out_dtype)] + (list(rider.out_shapes) if rider else []),
        scratch_shapes=r_sems,
        compiler_params=_cparams(("arbitrary",) * 3 if rider else ("parallel", "parallel", "arbitrary")),
    )(a, b, *r_inputs)
    return (out, rider_outs) if rider else out


def _ln(x, g, b):
    mu = jnp.mean(x, -1, keepdims=True)
    xc = x - mu
    var = jnp.mean(xc * xc, -1, keepdims=True)
    return xc * lax.rsqrt(var + EPS) * g + b


def _softplus(x):
    return jnp.maximum(x, 0.0) + jnp.log(1.0 + jnp.exp(-jnp.abs(x)))


def _silu(x):
    return x * jax.nn.sigmoid(x)


def _dsilu(x):
    s = jax.nn.sigmoid(x)
    return s * (1.0 + x * (1.0 - s))


def _f_ln0(x, g, b, sc, sh):
    x0 = _ln(x, g, b)
    return x0, x0 * (1.0 + sc) + sh


def _f_ln1(x0, y, gt, g, b, sc, sh):
    x1 = _ln(ALPHA * x0 + (1.0 + gt) * y, g, b)
    return x1, x1 * (1.0 + sc) + sh


def _f_ln2_loss(x1, y2, gt, g, b, tgt):
    x2 = _ln(ALPHA * x1 + (1.0 + gt) * y2, g, b)
    err = x2 - tgt
    per_row = jnp.sum(err * err, -1, keepdims=True) * (0.5 / D_MODEL)
    return jnp.sum(per_row, 0, keepdims=True)


def _row_specs(t_len):
    nt = t_len // ROW_TILE
    row = pl.BlockSpec((ROW_TILE, D_MODEL), lambda b, i: (b * nt + i, 0))
    vec = pl.BlockSpec((1, D_MODEL), lambda b, i: (0, 0))
    mod = pl.BlockSpec((None, 6, D_MODEL), lambda b, i: (b, 0, 0))
    return nt, row, vec, mod


def _first_step():
    return jnp.logical_and(pl.program_id(0) == 0, pl.program_id(1) == 0)


def _acc(ref, val, first, at=(Ellipsis,)):
    @pl.when(first)
    def _():
        ref[at] = val

    @pl.when(jnp.logical_not(first))
    def _():
        ref[at] += val


def _acc_rows(ref, rows, first):
    for i, r in enumerate(rows):
        _acc(ref, r, first, at=(slice(i, i + 1), slice(None)))


def _ln0_fwd(x, g, b, mod, n_b, t_len):
    nt, row, vec, mods = _row_specs(t_len)

    def body(x_ref, g_ref, b_ref, mod_ref, x0_ref, h_ref):
        x0, h = _f_ln0(x_ref[...], g_ref[...], b_ref[...], mod_ref[1:2, :], mod_ref[0:1, :])
        x0_ref[...] = x0
        h_ref[...] = h.astype(BF16)

    return pl.pallas_call(
        body, name="ln0_fwd", grid=(n_b, nt), in_specs=[row, vec, vec, mods], out_specs=[row, row],
        out_shape=[jax.ShapeDtypeStruct(x.shape, F32), jax.ShapeDtypeStruct(x.shape, BF16)],
        compiler_params=_cparams(("parallel", "parallel")),
    )(x, g, b, mod)


def _ln0_bwd(x, g, b, mod, dx0, dh, n_b, t_len):
    nt, row, vec, mods = _row_specs(t_len)
    dmod_spec = pl.BlockSpec((None, 2, D_MODEL), lambda bb, i: (bb, 0, 0))

    def body(x_ref, g_ref, b_ref, mod_ref, dx0_ref, dh_ref, dx_ref, dg_ref, db_ref, dmod_ref):
        _, pull = jax.vjp(_f_ln0, x_ref[...], g_ref[...], b_ref[...], mod_ref[1:2, :], mod_ref[0:1, :])
        dx, dg, db, dsc, dsh = pull((dx0_ref[...], dh_ref[...]))
        dx_ref[...] = dx
        _acc(dg_ref, dg, _first_step())
        _acc(db_ref, db, _first_step())
        _acc_rows(dmod_ref, [dsh, dsc], pl.program_id(1) == 0)

    return pl.pallas_call(
        body, name="ln0_bwd", grid=(n_b, nt), in_specs=[row, vec, vec, mods, row, row],
        out_specs=[row, vec, vec, dmod_spec],
        out_shape=[jax.ShapeDtypeStruct(x.shape, F32), jax.ShapeDtypeStruct((1, D_MODEL), F32),
                   jax.ShapeDtypeStruct((1, D_MODEL), F32), jax.ShapeDtypeStruct((n_b, 2, D_MODEL), F32)],
        compiler_params=_cparams(("arbitrary", "arbitrary")),
    )(x, g, b, mod, dx0, dh)


def _ln1_fwd(x0, y, g, b, mod, n_b, t_len):
    nt, row, vec, mods = _row_specs(t_len)

    def body(x0_ref, y_ref, g_ref, b_ref, mod_ref, x1_ref, h_ref):
        x1, h = _f_ln1(x0_ref[...], y_ref[...], mod_ref[2:3, :], g_ref[...], b_ref[...],
                       mod_ref[4:5, :], mod_ref[3:4, :])
        x1_ref[...] = x1
        h_ref[...] = h.astype(BF16)

    return pl.pallas_call(
        body, name="ln1_fwd", grid=(n_b, nt), in_specs=[row, row, vec, vec, mods], out_specs=[row, row],
        out_shape=[jax.ShapeDtypeStruct(x0.shape, F32), jax.ShapeDtypeStruct(x0.shape, BF16)],
        compiler_params=_cparams(("parallel", "parallel")),
    )(x0, y, g, b, mod)


def _ln1_bwd(x0, y, g, b, mod, dx1, dh, n_b, t_len):
    nt, row, vec, mods = _row_specs(t_len)
    dmod_spec = pl.BlockSpec((None, 3, D_MODEL), lambda bb, i: (bb, 0, 0))

    def body(x0_ref, y_ref, g_ref, b_ref, mod_ref, dx1_ref, dh_ref, dx0_ref, dy_ref, dg_ref, db_ref, dmod_ref):
        _, pull = jax.vjp(_f_ln1, x0_ref[...], y_ref[...], mod_ref[2:3, :], g_ref[...], b_ref[...],
                          mod_ref[4:5, :], mod_ref[3:4, :])
        dx0, dy, dgt, dg, db, dsc, dsh = pull((dx1_ref[...], dh_ref[...]))
        dx0_ref[...] = dx0
        dy_ref[...] = dy.astype(BF16)
        _acc(dg_ref, dg, _first_step())
        _acc(db_ref, db, _first_step())
        _acc_rows(dmod_ref, [dgt, dsh, dsc], pl.program_id(1) == 0)

    return pl.pallas_call(
        body, name="ln1_bwd", grid=(n_b, nt), in_specs=[row, row, vec, vec, mods, row, row],
        out_specs=[row, row, vec, vec, dmod_spec],
        out_shape=[jax.ShapeDtypeStruct(x0.shape, F32), jax.ShapeDtypeStruct(x0.shape, BF16),
                   jax.ShapeDtypeStruct((1, D_MODEL), F32), jax.ShapeDtypeStruct((1, D_MODEL), F32),
                   jax.ShapeDtypeStruct((n_b, 3, D_MODEL), F32)],
        compiler_params=_cparams(("arbitrary", "arbitrary")),
    )(x0, y, g, b, mod, dx1, dh)


def _ln2_loss_bwd(x1, y2, g, b, mod, tgt, n_b, t_len):
    nt, row, vec, mods = _row_specs(t_len)
    one = pl.BlockSpec((1, 128), lambda bb, i: (0, 0))
    dmod_spec = pl.BlockSpec((None, 1, D_MODEL), lambda bb, i: (bb, 0, 0))

    def body(x1_ref, y2_ref, g_ref, b_ref, mod_ref, t_ref, loss_ref, dx1_ref, dy2_ref, dg_ref, db_ref, dgt_ref):
        loss, pull = jax.vjp(functools.partial(_f_ln2_loss, tgt=t_ref[...]), x1_ref[...], y2_ref[...],
                             mod_ref[5:6, :], g_ref[...], b_ref[...])
        dx1, dy2, dgt, dg, db = pull(jnp.ones((1, 1), F32))
        dx1_ref[...] = dx1
        dy2_ref[...] = dy2.astype(BF16)
        _acc(loss_ref, jnp.broadcast_to(loss, (1, 128)), _first_step())
        _acc(dg_ref, dg, _first_step())
        _acc(db_ref, db, _first_step())
        _acc(dgt_ref, dgt, pl.program_id(1) == 0)

    return pl.pallas_call(
        body, name="ln2_loss_bwd", grid=(n_b, nt), in_specs=[row, row, vec, vec, mods, row],
        out_specs=[one, row, row, vec, vec, dmod_spec],
        out_shape=[jax.ShapeDtypeStruct((1, 128), F32), jax.ShapeDtypeStruct(x1.shape, F32),
                   jax.ShapeDtypeStruct(x1.shape, BF16), jax.ShapeDtypeStruct((1, D_MODEL), F32),
                   jax.ShapeDtypeStruct((1, D_MODEL), F32), jax.ShapeDtypeStruct((n_b, 1, D_MODEL), F32)],
        compiler_params=_cparams(("arbitrary", "arbitrary")),
    )(x1, y2, g, b, mod, tgt)


def _shift_down(x, s):
    if s == 0:
        return x
    rows = lax.broadcasted_iota(jnp.int32, x.shape, 0)
    return jnp.where(rows >= s, pltpu.roll(x, s, 0), 0.0)


def _shift_up(x, s):
    if s == 0:
        return x
    t_len = x.shape[0]
    rows = lax.broadcasted_iota(jnp.int32, x.shape, 0)
    return jnp.where(rows < t_len - s, pltpu.roll(x, t_len - s, 0), 0.0)


def _taps(x, k_w):
    return [_shift_down(x, k_w - 1 - k) for k in range(k_w)]


def _conv(taps, w):
    out = w[0:1, :] * taps[0]
    for k in range(1, len(taps)):
        out = out + w[k:k + 1, :] * taps[k]
    return out


def _conv_bwd(taps, w, du):
    k_w = len(taps)
    dx = w[k_w - 1:k_w, :] * du
    for k in range(k_w - 1):
        dx = dx + w[k:k + 1, :] * _shift_up(du, k_w - 1 - k)
    return dx, [jnp.sum(du * taps[k], 0, keepdims=True) for k in range(k_w)]


def _dn_pre_fwd(proj, conv_w, n_b, t_len):
    n_ct = 3 * HEADS
    k_w = conv_w.shape[0]

    def body(x_ref, w_ref, o_ref):
        o_ref[...] = _silu(_conv(_taps(x_ref[...], k_w), w_ref[...]))

    return pl.pallas_call(
        body, name="dn_pre_fwd", grid=(n_ct, n_b),
        in_specs=[pl.BlockSpec((t_len, 128), lambda j, b: (b, _qkv_block(j))),
                  pl.BlockSpec((k_w, 128), lambda j, b: (0, j))],
        out_specs=pl.BlockSpec((t_len, 128), lambda j, b: (b, j)),
        out_shape=jax.ShapeDtypeStruct((n_b * t_len, n_ct * 128), F32),
        compiler_params=_cparams(("parallel", "parallel")),
    )(proj, conv_w)


def _dn_pre_bwd(proj, conv_w, dqkv, d_proj, n_b, t_len):
    n_ct = 3 * HEADS
    k_w = conv_w.shape[0]

    def body(x_ref, w_ref, d_ref, _, dx_ref, dw_ref):
        taps, w = _taps(x_ref[...], k_w), w_ref[...]
        du = d_ref[...] * _dsilu(_conv(taps, w))
        dx, dw = _conv_bwd(taps, w, du)
        dx_ref[...] = dx.astype(BF16)
        _acc_rows(dw_ref, dw, pl.program_id(1) == 0)

    return pl.pallas_call(
        body, name="dn_pre_bwd", grid=(n_ct, n_b),
        in_specs=[pl.BlockSpec((t_len, 128), lambda j, b: (b, _qkv_block(j))),
                  pl.BlockSpec((k_w, 128), lambda j, b: (0, j)),
                  pl.BlockSpec((t_len, 128), lambda j, b: (b, j)), pl.BlockSpec(memory_space=pl.ANY)],
        out_specs=[pl.BlockSpec((t_len, 128), lambda j, b: (b, _qkv_block(j))),
                   pl.BlockSpec((k_w, 128), lambda j, b: (0, j))],
        out_shape=[jax.ShapeDtypeStruct(d_proj.shape, BF16), jax.ShapeDtypeStruct((k_w, n_ct * 128), F32)],
        input_output_aliases={3: 0},
        compiler_params=_cparams(("parallel", "arbitrary")),
    )(proj, conv_w, dqkv, d_proj)


FFN_TC = 256
FFN_NT = D_FF // FFN_TC


def _ffn_specs(t_len):
    blk = lambda off: pl.BlockSpec((t_len, FFN_TC), lambda j, b: (b, j + off))
    wblk = lambda off: pl.BlockSpec((3, FFN_TC), lambda j, b: (0, j + off))
    bblk = lambda off: pl.BlockSpec((1, FFN_TC), lambda j, b: (0, j + off))
    return [blk(0), blk(FFN_NT), wblk(0), wblk(FFN_NT), bblk(0), bblk(FFN_NT)]


def _ffn_act_fwd(up, conv_w, conv_b, n_b, t_len):
    def body(g_ref, v_ref, wg_ref, wv_ref, bg_ref, bv_ref, o_ref):
        ug = _conv(_taps(g_ref[...], 3), wg_ref[...]) + bg_ref[...]
        uv = _conv(_taps(v_ref[...], 3), wv_ref[...]) + bv_ref[...]
        o_ref[...] = (_silu(ug) * uv).astype(BF16)

    return pl.pallas_call(
        body, name="ffn_act_fwd", grid=(FFN_NT, n_b), in_specs=_ffn_specs(t_len),
        out_specs=pl.BlockSpec((t_len, FFN_TC), lambda j, b: (b, j)),
        out_shape=jax.ShapeDtypeStruct((n_b * t_len, D_FF), BF16),
        compiler_params=_cparams(("parallel", "parallel")),
    )(up, up, conv_w, conv_w, conv_b, conv_b)


def _ffn_act_bwd(up, conv_w, conv_b, da, n_b, t_len):
    def body(g_ref, v_ref, wg_ref, wv_ref, bg_ref, bv_ref, da_ref, dup_ref, dw_ref, db_ref):
        first = pl.program_id(1) == 0
        tg, tv, wg, wv = _taps(g_ref[...], 3), _taps(v_ref[...], 3), wg_ref[...], wv_ref[...]
        ug = _conv(tg, wg) + bg_ref[...]
        uv = _conv(tv, wv) + bv_ref[...]
        d_act = da_ref[...]
        sig = jax.nn.sigmoid(ug)
        d_v = d_act * (ug * sig)
        d_g = d_act * uv * (sig * (1.0 + ug * (1.0 - sig)))
        for slab, (taps, w, du) in enumerate(((tg, wg, d_g), (tv, wv, d_v))):
            dx, dw = _conv_bwd(taps, w, du)
            dup_ref[slab] = dx.astype(BF16)
            for k, dw_k in enumerate(dw):
                _acc(dw_ref, dw_k, first, at=(slab, slice(k, k + 1), slice(None)))
            _acc(db_ref, jnp.sum(du, 0, keepdims=True), first, at=(slab, slice(None), slice(None)))

    return pl.pallas_call(
        body, name="ffn_act_bwd", grid=(FFN_NT, n_b),
        in_specs=_ffn_specs(t_len) + [pl.BlockSpec((t_len, FFN_TC), lambda j, b: (b, j))],
        out_specs=[pl.BlockSpec((2, t_len, FFN_TC), lambda j, b: (0, b, j)),
                   pl.BlockSpec((2, 3, FFN_TC), lambda j, b: (0, 0, j)),
                   pl.BlockSpec((2, 1, FFN_TC), lambda j, b: (0, 0, j))],
        out_shape=[jax.ShapeDtypeStruct((2, n_b * t_len, D_FF), BF16),
                   jax.ShapeDtypeStruct((2, 3, D_FF), F32), jax.ShapeDtypeStruct((2, 1, D_FF), F32)],
        compiler_params=_cparams(("parallel", "arbitrary")),
    )(up, up, conv_w, conv_w, conv_b, conv_b, da)


NN = (((2,), (1,)), ((0,), (0,)))
NT = (((2,), (2,)), ((0,), (0,)))
TN = (((1,), (1,)), ((0,), (0,)))


def _iota3(shape, axis):
    return lax.broadcasted_iota(jnp.int32, shape, axis)


def _dg(a, b, dims):
    return lax.dot_general(a, b, dims, preferred_element_type=F32)


def _dot(a, b):
    return _dg(a, b, NN)


def _dot_nt(a, b):
    return _dg(a, b, NT)


def _dot_tn(a, b):
    return _dg(a, b, TN)


def _split(a):
    hi = a.astype(BF16)
    return hi, (a - hi.astype(F32)).astype(BF16)


def _dg3(a, b, dims):
    ah, al = _split(a)
    bh, bl = _split(b)
    return _dg(ah, bh, dims) + (_dg(ah, bl, dims) + _dg(al, bh, dims))


@jax.custom_vjp
def _dot3(a, b):
    return _dg3(a, b, NN)


def _dot3_fwd(a, b):
    return _dg3(a, b, NN), (a, b)


def _dot3_bwd(res, g):
    a, b = res
    return _dg3(g, b, NT), _dg3(a, g, TN)


_dot3.defvjp(_dot3_fwd, _dot3_bwd)


def _lower_ones(g_n, n):
    shape = (g_n, n, n)
    return jnp.where(_iota3(shape, 1) >= _iota3(shape, 2), 1.0, 0.0).astype(BF16)


@jax.custom_vjp
def _chunk_cumsum(x):
    hi, lo = _split(x)
    tri = _lower_ones(x.shape[0], x.shape[1])
    return _dg(tri, hi, NN) + _dg(tri, lo, NN)


def _chunk_cumsum_fwd(x):
    return _chunk_cumsum(x), None


def _chunk_cumsum_bwd(_, g):
    hi, lo = _split(g)
    tri = _lower_ones(g.shape[0], g.shape[1])
    return (_dg(tri, hi, TN) + _dg(tri, lo, TN),)


_chunk_cumsum.defvjp(_chunk_cumsum_fwd, _chunk_cumsum_bwd)


@jax.custom_vjp
def _unit_lower_inv(m):
    n = m.shape[1]
    p = -m
    a = jnp.where(_iota3(m.shape, 1) == _iota3(m.shape, 2), 1.0, 0.0) + p
    span = 2
    while span < n:
        p = _dg3(p, p, NN)
        a = a + _dg3(a, p, NN)
        span *= 2
    return a


def _unit_lower_inv_fwd(m):
    a = _unit_lower_inv(m)
    return a, a


def _unit_lower_inv_bwd(a, da):
    return (-_dg3(a, _dg3(da, a, NT), TN),)


_unit_lower_inv.defvjp(_unit_lower_inv_fwd, _unit_lower_inv_bwd)


@jax.custom_vjp
def _saved_lower_inv(m, a):
    return a


def _saved_lower_inv_fwd(m, a):
    return a, a


def _saved_lower_inv_bwd(a, da):
    return _unit_lower_inv_bwd(a, da)[0], jnp.zeros_like(a)


_saved_lower_inv.defvjp(_saved_lower_inv_fwd, _saved_lower_inv_bwd)


def _rms_gate(o, gn, gate):
    return o * lax.rsqrt(jnp.mean(o * o, -1, keepdims=True) + EPS) * gn * _silu(gate)


def _dn_chains(q, k, v, z, small, s_in, a_log, dt_bias, gn, a_saved=None):
    g_n, c_len = q.shape[0], q.shape[1]
    sq = (g_n, c_len, c_len)
    row, col = _iota3(sq, 1), _iota3(sq, 2)
    causal, strict, eye = row >= col, row > col, row == col
    qn = q * lax.rsqrt(jnp.sum(q * q, -1, keepdims=True) + EPS) * (HEAD_DIM ** -0.5)
    kn = k * lax.rsqrt(jnp.sum(k * k, -1, keepdims=True) + EPS)
    lane = _iota3(small.shape, 2)
    head = jnp.bitwise_and(_iota3(small.shape, 0), HEADS - 1)
    la_all = -jnp.exp(a_log) * _softplus(small + dt_bias)
    la_c = jnp.sum(jnp.where(lane == head, la_all, 0.0), 2, keepdims=True)
    beta = jnp.sum(jnp.where(lane == head + HEADS, jax.nn.sigmoid(small), 0.0), 2, keepdims=True)
    la_b = jnp.broadcast_to(la_c, sq)
    la_r = jnp.sum(jnp.where(eye, la_b, 0.0), 1, keepdims=True)
    g_c = jnp.sum(jnp.where(causal, jnp.broadcast_to(la_r, sq), 0.0), 2, keepdims=True)
    g_r = jnp.sum(jnp.where(row <= col, la_b, 0.0), 1, keepdims=True)
    g_last = jnp.sum(la_c, 1, keepdims=True)
    decay = jnp.exp(jnp.where(causal, g_c - g_r, -1e30))
    e_g = jnp.exp(g_c)
    kb = kn * beta
    m_low = jnp.where(strict, _dot_nt(kb, kn) * decay, 0.0)
    a_inv = _unit_lower_inv(m_low) if a_saved is None else _saved_lower_inv(m_low, a_saved)
    u = _dot3(a_inv, v * beta)
    w = _dot3(a_inv, kb * e_g)
    attn = _dot_nt(qn, kn) * decay
    v_new = u - _dot(w, s_in)
    o = _dot(qn * e_g, s_in) + _dot(attn, v_new)
    s_out = s_in * jnp.exp(g_last) + _dot_tn(kn * jnp.exp(g_last - g_c), v_new)
    return _rms_gate(o, gn, z), s_out, a_inv


def _gla_chains(q, k, v, gate, small, s_in, w2, b2, gn):
    g_n, c_len = q.shape[0], q.shape[1]
    sq, kk = (g_n, c_len, c_len), (g_n, GLA_KEY, GLA_KEY)
    causal = _iota3(sq, 1) >= _iota3(sq, 2)
    la = -_softplus(-(_dot(small, w2) + b2)) * (1.0 / 16.0)
    b = _chunk_cumsum(la)
    b_last = jnp.sum(jnp.where(_iota3(b.shape, 1) == c_len - 1, b, 0.0), 1, keepdims=True)
    q_dec = q * (GLA_KEY ** -0.5) * jnp.exp(b)
    attn = jnp.where(causal, _dot_nt(q_dec, k * jnp.exp(-b)), 0.0)
    o = _dot(q_dec, s_in) + _dot(attn, v)
    g_row = jnp.exp(b_last)
    g_col = jnp.sum(jnp.where(_iota3(kk, 1) == _iota3(kk, 2), jnp.broadcast_to(g_row, kk), 0.0), 2, keepdims=True)
    s_out = s_in * g_col + _dot_tn(k * jnp.exp(b_last - b), v)
    return _rms_gate(o, gn, gate), s_out


def _chunk_spec(n_b, width, col_block, n_c, reverse=False):
    if reverse:
        return pl.BlockSpec((n_b, CHUNK, width), lambda n: (0, n_c - 1 - n, col_block))
    return pl.BlockSpec((n_b, CHUNK, width), lambda n: (0, n, col_block))


def _hist_spec(n_b, d_k, n_c, reverse=False):
    if reverse:
        return pl.BlockSpec((None, n_b * HEADS, d_k, HEAD_DIM), lambda n: (n_c - 1 - n, 0, 0, 0))
    return pl.BlockSpec((None, n_b * HEADS, d_k, HEAD_DIM), lambda n: (n, 0, 0, 0))


def _ainv_spec(n_b, n_c, reverse=False):
    if reverse:
        return pl.BlockSpec((None, n_b * HEADS, CHUNK, CHUNK), lambda n: (n_c - 1 - n, 0, 0, 0))
    return pl.BlockSpec((None, n_b * HEADS, CHUNK, CHUNK), lambda n: (n, 0, 0, 0))


def _stack_chains(ref, n_b, slices):
    return jnp.stack([ref[b, :, sl] for b in range(n_b) for sl in slices], axis=0)


def _per_chain(ref, n_b):
    return jnp.stack([ref[b] for b in range(n_b) for _ in range(HEADS)], axis=0)


def _unstack_chains(ref, val, n_b, slices, offset=0):
    for b in range(n_b):
        for h, sl in enumerate(slices):
            ref[b, :, slice(offset + sl.start, offset + sl.stop)] = val[b * HEADS + h].astype(ref.dtype)


def _gate_weights(w2_ref, b2_ref, n_b):
    w2 = jnp.stack([w2_ref[:, ks] for _ in range(n_b) for ks in GLA_KSL], axis=0)
    b2 = jnp.stack([b2_ref[:, ks] for _ in range(n_b) for ks in GLA_KSL], axis=0)
    return w2, b2


def _sum_heads(val, n_b):
    return [sum(val[b * HEADS + h] for h in range(HEADS)) for b in range(n_b)]


def _const_spec(shape):
    return pl.BlockSpec(shape, lambda n: (0,) * len(shape))


DN_SL = [slice(h * HEAD_DIM, (h + 1) * HEAD_DIM) for h in range(HEADS)]
GLA_KSL = [slice(h * GLA_KEY, (h + 1) * GLA_KEY) for h in range(HEADS)]


class Rider:
    def __init__(self, inputs, out_shapes, sems, first, last):
        self.inputs, self.out_shapes, self.sems, self.first, self.last = inputs, out_shapes, sems, first, last


def _with_rider(rider, n_in, n_out, n_scratch):
    if rider is None:
        return [], [], [], [], lambda refs: (refs, None)
    r_in, r_out, r_sem = len(rider.inputs), len(rider.out_shapes), len(rider.sems)

    def split(refs):
        own_in, rest = refs[:n_in], refs[n_in:]
        rid_in, rest = rest[:r_in], rest[r_in:]
        own_out, rest = rest[:n_out], rest[n_out:]
        rid_out, rest = rest[:r_out], rest[r_out:]
        own_scr, rid_sem = rest[:n_scratch], rest[n_scratch:]
        return own_in + own_out + own_scr, (rid_in, rid_out, rid_sem)

    return list(rider.inputs), [HBM_SPEC] * r_in, [HBM_SPEC] * r_out, list(rider.sems), split


def _ride(rider, parts, grid):
    if rider is None:
        return None, None
    grid = grid if isinstance(grid, tuple) else (grid,)

    def at(step_of):
        hit = pl.program_id(0) == step_of(grid[0])
        for axis in range(1, len(grid)):
            hit = jnp.logical_and(hit, pl.program_id(axis) == step_of(grid[axis]))
        return hit

    def first():
        pl.when(at(lambda n: 0))(lambda: rider.first(*parts))

    def last():
        pl.when(at(lambda n: n - 1))(lambda: rider.last(*parts))

    return first, last


def _dn_scan_fwd(qkv, proj, a_log, dt_bias, gn, n_b, t_len, rider=None):
    n_c = t_len // CHUNK
    spec = functools.partial(_chunk_spec, n_b, n_c=n_c)
    r_inputs, r_in_specs, r_out_specs, r_sems, split = _with_rider(rider, 8, 3, 1)

    def body(*refs):
        (q_ref, k_ref, v_ref, z_ref, sm_ref, al_ref, dt_ref, gn_ref,
         o_ref, hist_ref, ainv_ref, s_ref), parts = split(refs)
        ride_first, ride_last = _ride(rider, parts, n_c)
        if rider is not None:
            ride_first()

        @pl.when(pl.program_id(0) == 0)
        def _():
            s_ref[...] = jnp.zeros_like(s_ref)

        s_in = s_ref[...]
        hist_ref[...] = s_in
        og, s_out, a_inv = _dn_chains(*(_stack_chains(r, n_b, DN_SL) for r in (q_ref, k_ref, v_ref, z_ref)),
                                      _per_chain(sm_ref, n_b), s_in, al_ref[...], dt_ref[...], gn_ref[...])
        _unstack_chains(o_ref, og, n_b, DN_SL)
        s_ref[...] = s_out
        ainv_ref[...] = a_inv
        if rider is not None:
            ride_last()

    qkv3, proj3 = qkv.reshape(n_b, t_len, -1), proj.reshape(n_b, t_len, -1)
    o, hist, ainv, *rider_outs = pl.pallas_call(
        body, name="dn_scan_fwd", grid=(n_c,),
        in_specs=[spec(512, 0), spec(512, 1), spec(512, 2), spec(512, OFF_Z // 512), spec(128, OFF_SMALL // 128),
                  _const_spec((1, 128)), _const_spec((1, 128)), _const_spec((1, 128))] + r_in_specs,
        out_specs=[spec(512, 0), _hist_spec(n_b, HEAD_DIM, n_c), _ainv_spec(n_b, n_c)] + r_out_specs,
        out_shape=[jax.ShapeDtypeStruct((n_b, t_len, 2 * 512), BF16),
                   jax.ShapeDtypeStruct((n_c, n_b * HEADS, HEAD_DIM, HEAD_DIM), F32),
                   jax.ShapeDtypeStruct((n_c, n_b * HEADS, CHUNK, CHUNK), F32)]
        + (list(rider.out_shapes) if rider else []),
        scratch_shapes=[pltpu.VMEM((n_b * HEADS, HEAD_DIM, HEAD_DIM), F32)] + r_sems,
        compiler_params=_cparams(("arbitrary",)),
    )(qkv3, qkv3, qkv3, proj3, proj3, a_log, dt_bias, gn, *r_inputs)
    return o, (hist, ainv), rider_outs


def _dn_scan_bwd(qkv, proj, a_log, dt_bias, gn, hist, d_o, n_b, t_len, rider=None):
    n_c = t_len // CHUNK
    rev = functools.partial(_chunk_spec, n_b, n_c=n_c, reverse=True)
    r_inputs, r_in_specs, r_out_specs, r_sems, split = _with_rider(rider, 11, 6, 1)
    hist, ainv = hist

    def body(*refs):
        (q_ref, k_ref, v_ref, z_ref, sm_ref, al_ref, dt_ref, gn_ref, hist_ref, ainv_ref, do_ref,
         dqkv_ref, dz_ref, dsm_ref, dal_ref, ddt_ref, dgn_ref, ds_ref), parts = split(refs)
        ride_first, ride_last = _ride(rider, parts, n_c)
        if rider is not None:
            ride_first()
        first = pl.program_id(0) == 0

        @pl.when(first)
        def _():
            ds_ref[...] = jnp.zeros_like(ds_ref)

        chains = lambda *a: _dn_chains(*a, a_saved=ainv_ref[...])[:2]
        _, pull = jax.vjp(chains, *(_stack_chains(r, n_b, DN_SL) for r in (q_ref, k_ref, v_ref, z_ref)),
                          _per_chain(sm_ref, n_b), hist_ref[...], al_ref[...], dt_ref[...], gn_ref[...])
        dq, dk, dv, dz, dsm, ds_in, dal, ddt, dgn = pull((_stack_chains(do_ref, n_b, DN_SL), ds_ref[...]))
        _unstack_chains(dqkv_ref, dq, n_b, DN_SL)
        _unstack_chains(dqkv_ref, dk, n_b, DN_SL, offset=512)
        _unstack_chains(dqkv_ref, dv, n_b, DN_SL, offset=1024)
        _unstack_chains(dz_ref, dz, n_b, DN_SL)
        ds_ref[...] = ds_in
        for b, dsm_b in enumerate(_sum_heads(dsm, n_b)):
            dsm_ref[b] = dsm_b
        _acc(dal_ref, dal, first)
        _acc(ddt_ref, ddt, first)
        _acc(dgn_ref, dgn, first)
        if rider is not None:
            ride_last()

    qkv3, proj3, do3 = (a.reshape(n_b, t_len, -1) for a in (qkv, proj, d_o))
    vec = jax.ShapeDtypeStruct((1, 128), F32)
    dqkv, d_proj, dsm, dal, ddt, dgn, *rider_outs = pl.pallas_call(
        body, name="dn_scan_bwd", grid=(n_c,),
        in_specs=[rev(512, 0), rev(512, 1), rev(512, 2), rev(512, OFF_Z // 512), rev(128, OFF_SMALL // 128),
                  _const_spec((1, 128)), _const_spec((1, 128)), _const_spec((1, 128)),
                  _hist_spec(n_b, HEAD_DIM, n_c, reverse=True), _ainv_spec(n_b, n_c, reverse=True),
                  rev(512, 0)] + r_in_specs,
        out_specs=[rev(1536, 0), rev(512, OFF_Z // 512), rev(128, 0),
                   _const_spec((1, 128)), _const_spec((1, 128)), _const_spec((1, 128))] + r_out_specs,
        out_shape=[jax.ShapeDtypeStruct((n_b, t_len, 1536), F32), jax.ShapeDtypeStruct((n_b, t_len, PROJ_W), BF16),
                   jax.ShapeDtypeStruct((n_b, t_len, 128), F32), vec, vec, vec]
        + (list(rider.out_shapes) if rider else []),
        scratch_shapes=[pltpu.VMEM((n_b * HEADS, HEAD_DIM, HEAD_DIM), F32)] + r_sems,
        compiler_params=_cparams(("arbitrary",)),
    )(qkv3, qkv3, qkv3, proj3, proj3, a_log, dt_bias, gn, hist, ainv, do3, *r_inputs)
    return dqkv.reshape(n_b * t_len, 1536), d_proj, dsm, dal, ddt, dgn, rider_outs


def _gla_scan_fwd(proj, w2, b2, gn, o_mix, n_b, t_len):
    n_c = t_len // CHUNK
    spec = functools.partial(_chunk_spec, n_b, n_c=n_c)

    def body(q_ref, k_ref, v_ref, g_ref, sm_ref, w2_ref, b2_ref, gn_ref, _, o_ref, hist_ref, s_ref):
        @pl.when(pl.program_id(0) == 0)
        def _():
            s_ref[...] = jnp.zeros_like(s_ref)

        s_in = s_ref[...]
        hist_ref[...] = s_in
        og, s_out = _gla_chains(_stack_chains(q_ref, n_b, GLA_KSL), _stack_chains(k_ref, n_b, GLA_KSL),
                                _stack_chains(v_ref, n_b, DN_SL), _stack_chains(g_ref, n_b, DN_SL),
                                _per_chain(sm_ref, n_b), s_in, *_gate_weights(w2_ref, b2_ref, n_b), gn_ref[...])
        _unstack_chains(o_ref, og, n_b, DN_SL)
        s_ref[...] = s_out

    proj3 = proj.reshape(n_b, t_len, -1)
    o, hist = pl.pallas_call(
        body, name="gla_scan_fwd", grid=(n_c,),
        in_specs=[spec(256, OFF_GQ // 256), spec(256, OFF_GK // 256), spec(512, OFF_GV // 512),
                  spec(512, OFF_GG // 512), spec(128, OFF_SMALL // 128),
                  _const_spec((128, 256)), _const_spec((1, 256)), _const_spec((1, 128)),
                  pl.BlockSpec(memory_space=pl.ANY)],
        out_specs=[spec(512, 1), _hist_spec(n_b, GLA_KEY, n_c)],
        out_shape=[jax.ShapeDtypeStruct(o_mix.shape, BF16),
                   jax.ShapeDtypeStruct((n_c, n_b * HEADS, GLA_KEY, HEAD_DIM), F32)],
        input_output_aliases={8: 0},
        scratch_shapes=[pltpu.VMEM((n_b * HEADS, GLA_KEY, HEAD_DIM), F32)],
        compiler_params=_cparams(("arbitrary",)),
    )(proj3, proj3, proj3, proj3, proj3, w2, b2, gn, o_mix)
    return o.reshape(n_b * t_len, 2 * 512), hist


def _gla_scan_bwd(proj, w2, b2, gn, hist, d_o, dsm_dn, d_proj, n_b, t_len):
    n_c = t_len // CHUNK
    rev = functools.partial(_chunk_spec, n_b, n_c=n_c, reverse=True)

    def body(q_ref, k_ref, v_ref, g_ref, sm_ref, w2_ref, b2_ref, gn_ref, hist_ref, do_ref, dsm_dn_ref, _,
             dp_ref, dw2_ref, db2_ref, dgn_ref, ds_ref):
        first = pl.program_id(0) == 0

        @pl.when(first)
        def _():
            ds_ref[...] = jnp.zeros_like(ds_ref)

        _, pull = jax.vjp(_gla_chains, _stack_chains(q_ref, n_b, GLA_KSL), _stack_chains(k_ref, n_b, GLA_KSL),
                          _stack_chains(v_ref, n_b, DN_SL), _stack_chains(g_ref, n_b, DN_SL),
                          _per_chain(sm_ref, n_b), hist_ref[...], *_gate_weights(w2_ref, b2_ref, n_b), gn_ref[...])
        dq, dk, dv, dg, dsm, ds_in, dw2, db2, dgn = pull((_stack_chains(do_ref, n_b, DN_SL), ds_ref[...]))
        _unstack_chains(dp_ref, dq, n_b, GLA_KSL, offset=OFF_GQ)
        _unstack_chains(dp_ref, dk, n_b, GLA_KSL, offset=OFF_GK)
        _unstack_chains(dp_ref, dv, n_b, DN_SL, offset=OFF_GV)
        _unstack_chains(dp_ref, dg, n_b, DN_SL, offset=OFF_GG)
        ds_ref[...] = ds_in
        for b, dsm_b in enumerate(_sum_heads(dsm, n_b)):
            dp_ref[b, :, OFF_SMALL:OFF_SMALL + 128] = (dsm_b + dsm_dn_ref[b]).astype(BF16)
            dp_ref[b, :, OFF_SMALL + 128:GLA_W] = jnp.zeros((CHUNK, GLA_W - OFF_SMALL - 128), BF16)
        for h, ks in enumerate(GLA_KSL):
            _acc(dw2_ref, sum(dw2[b * HEADS + h] for b in range(n_b)), first, at=(slice(None), ks))
            _acc(db2_ref, sum(db2[b * HEADS + h] for b in range(n_b)), first, at=(slice(None), ks))
        _acc(dgn_ref, dgn, first)

    proj3, do3 = proj.reshape(n_b, t_len, -1), d_o.reshape(n_b, t_len, -1)
    return pl.pallas_call(
        body, name="gla_scan_bwd", grid=(n_c,),
        in_specs=[rev(256, OFF_GQ // 256), rev(256, OFF_GK // 256), rev(512, OFF_GV // 512), rev(512, OFF_GG // 512),
                  rev(128, OFF_SMALL // 128),
                  _const_spec((128, 256)), _const_spec((1, 256)), _const_spec((1, 128)),
                  _hist_spec(n_b, GLA_KEY, n_c, reverse=True), rev(512, 1), rev(128, 0),
                  pl.BlockSpec(memory_space=pl.ANY)],
        out_specs=[rev(GLA_W, 0), _const_spec((128, 256)), _const_spec((1, 256)), _const_spec((1, 128))],
        out_shape=[jax.ShapeDtypeStruct(d_proj.shape, BF16), jax.ShapeDtypeStruct((128, 256), F32),
                   jax.ShapeDtypeStruct((1, 256), F32), jax.ShapeDtypeStruct((1, 128), F32)],
        input_output_aliases={11: 0},
        scratch_shapes=[pltpu.VMEM((n_b * HEADS, GLA_KEY, HEAD_DIM), F32)],
        compiler_params=_cparams(("arbitrary",)),
    )(proj3, proj3, proj3, proj3, proj3, w2, b2, gn, hist, do3, dsm_dn, d_proj)


W_IN_RUNS = ((0, 256, GLA_W), (256, 1536, OFF_Z + 512), (1536, 2048, OFF_Z), (2048, 2056, OFF_SMALL),
             (2056, 3592, 0), (3592, 3608, OFF_SMALL + 8))
W_IN_ROWS = 256


def _w_in_pieces(cols_per_chip):
    out = []
    for first, last, start in W_IN_RUNS:
        for j in range(N_CHIPS):
            a, b = max(first, cols_per_chip * j), min(last, cols_per_chip * (j + 1))
            if a < b:
                out.append((j, a - cols_per_chip * j, b - cols_per_chip * j, start + a - first))
    return out


def _w_in_to_padded(w4):
    _, n_r, n_c = w4.shape

    def body(i_ref, o_ref):
        o_ref[...] = jnp.zeros_like(o_ref)
        for j, a, b, p in _w_in_pieces(n_c):
            o_ref[:, p:p + b - a] = i_ref[j, :, a:b]

    return pl.pallas_call(
        body, name="w_in_to_padded", grid=(n_r // W_IN_ROWS,),
        in_specs=[pl.BlockSpec((N_CHIPS, W_IN_ROWS, n_c), lambda i: (0, i, 0))],
        out_specs=pl.BlockSpec((W_IN_ROWS, PROJ_W), lambda i: (i, 0)),
        out_shape=jax.ShapeDtypeStruct((n_r, PROJ_W), w4.dtype), compiler_params=_cparams(("parallel",)),
    )(w4)


def _w_in_to_chips(g, n_c):
    n_r = g.shape[0]

    def body(i_ref, o_ref):
        for j, a, b, p in _w_in_pieces(n_c):
            o_ref[j, :, a:b] = i_ref[:, p:p + b - a]

    return pl.pallas_call(
        body, name="w_in_to_chips", grid=(n_r // W_IN_ROWS,),
        in_specs=[pl.BlockSpec((W_IN_ROWS, PROJ_W), lambda i: (i, 0))],
        out_specs=pl.BlockSpec((N_CHIPS, W_IN_ROWS, n_c), lambda i: (0, i, 0)),
        out_shape=jax.ShapeDtypeStruct((N_CHIPS, n_r, n_c), g.dtype), compiler_params=_cparams(("parallel",)),
    )(g)


def _lane_vec(v, offset=0):
    return jnp.zeros((1, 128), F32).at[0, offset:offset + v.shape[0]].set(v)


def _local_step(x, tgt, mod, p, n_b, t_len, comm=None):
    row1 = lambda v: v.reshape(1, -1)
    a_log, dt_bias = _lane_vec(p["dn_a_log"]), _lane_vec(p["dn_dt_bias"])
    dn_gn, gla_gn = row1(p["dn_norm_g"]), row1(p["gla_norm_g"])
    w2 = jnp.zeros((128, 256), F32).at[8:8 + GATE_RANK].set(p["gla_w_gate2"])
    b2 = row1(p["gla_b_gate"])
    ln0_g, ln0_b, ln1_g, ln1_b, ln2_g, ln2_b = (row1(p[k]) for k in ("ln0_g", "ln0_b", "ln1_g", "ln1_b", "ln2_g", "ln2_b"))
    conv_b = row1(p["ffn_conv_b"])

    x0, h1 = _ln0_fwd(x, ln0_g, ln0_b, mod, n_b, t_len)
    proj = _mm(h1, p["w_in_p"], name="mm_proj")
    qkv = _dn_pre_fwd(proj, p["dn_conv"], n_b, t_len)
    o_half, hist_dn, landed = _dn_scan_fwd(qkv, proj, a_log, dt_bias, dn_gn, n_b, t_len,
                                           rider=comm.fwd_rider() if comm else None)
    if comm:
        p = {**p, **comm.weights_from(landed)}
    o_mix, hist_gla = _gla_scan_fwd(proj, w2, b2, gla_gn, o_half, n_b, t_len)
    y = _mm(o_mix, p["w_o"], name="mm_wo")
    x1, h2 = _ln1_fwd(x0, y, ln1_g, ln1_b, mod, n_b, t_len)
    up = _mm(h2, p["w_up"], name="mm_up")
    act = _ffn_act_fwd(up, p["ffn_conv"], conv_b, n_b, t_len)
    y2 = _mm(act, p["w_down"], name="mm_down")

    loss, dx1, dy2, g_ln2_g, g_ln2_b, dgt_f = _ln2_loss_bwd(x1, y2, ln2_g, ln2_b, mod, tgt, n_b, t_len)
    g_w_down = _mm(act, dy2, ta=True, name="mm_g_down")
    d_act = _mm(dy2, p["w_down"], tb=True, name="mm_d_act")
    d_up, g_ffn_conv, g_conv_b = _ffn_act_bwd(up, p["ffn_conv"], conv_b, d_act, n_b, t_len)
    g_w_up = _mm(h2, d_up, ta=True, out_slabs=N_CHIPS, name="mm_g_up")
    if comm:
        dh2, from_sibling = _mm(d_up, p["w_up"], tb=True, name="mm_d_h2", rider=comm.ffn_pair_rider(g_w_up, g_w_down))
    else:
        dh2 = _mm(d_up, p["w_up"], tb=True, name="mm_d_h2")
    dx0, dy, g_ln1_g, g_ln1_b, dmod_1 = _ln1_bwd(x0, y, ln1_g, ln1_b, mod, dx1, dh2, n_b, t_len)
    g_w_o = _mm(o_mix, dy, ta=True, name="mm_g_wo")
    d_o = _mm(dy, p["w_o"], tb=True, name="mm_d_o")
    dqkv, d_proj, dsm_dn, g_a_log, g_dt_bias, g_dn_gn, ffn_from_chips = _dn_scan_bwd(
        qkv, proj, a_log, dt_bias, dn_gn, hist_dn, d_o, n_b, t_len,
        rider=comm.ffn_chips_rider(from_sibling) if comm else None)
    d_proj, g_w2, g_b2, g_gla_gn = _gla_scan_bwd(proj, w2, b2, gla_gn, hist_gla, d_o, dsm_dn, d_proj, n_b, t_len)
    d_proj, g_dn_conv = _dn_pre_bwd(proj, p["dn_conv"], dqkv, d_proj.reshape(n_b * t_len, PROJ_W), n_b, t_len)
    g_w_in_p = _mm(h1, d_proj, ta=True, name="mm_g_win")
    if comm:
        dh1, tail_from_chips = _mm(d_proj, p["w_in_p"], tb=True, name="mm_d_h1",
                                   rider=comm.tail_chips_rider(g_w_in_p, g_w_o))
        from_chips = (ffn_from_chips, tail_from_chips)
    else:
        dh1, from_chips = _mm(d_proj, p["w_in_p"], tb=True, name="mm_d_h1"), None
    grad_x, g_ln0_g, g_ln0_b, dmod_0 = _ln0_bwd(x, ln0_g, ln0_b, mod, dx0, dh1, n_b, t_len)

    dmod = jnp.concatenate([dmod_0, dmod_1[:, 0:1], dmod_1[:, 1:3], dgt_f], axis=1)
    grads = {
        "ln0_g": g_ln0_g[0], "ln0_b": g_ln0_b[0], "w_in_p": g_w_in_p, "dn_conv": g_dn_conv,
        "dn_a_log": g_a_log[0, 0:HEADS], "dn_dt_bias": g_dt_bias[0, 0:HEADS], "dn_norm_g": g_dn_gn[0],
        "gla_w_gate2": g_w2[8:8 + GATE_RANK], "gla_b_gate": g_b2[0], "gla_norm_g": g_gla_gn[0],
        "w_o": g_w_o, "ln1_g": g_ln1_g[0], "ln1_b": g_ln1_b[0], "w_up": g_w_up,
        "ffn_conv": jnp.concatenate([g_ffn_conv[0], g_ffn_conv[1]], axis=1),
        "ffn_conv_b": jnp.concatenate([g_conv_b[0, 0], g_conv_b[1, 0]]), "w_down": g_w_down,
        "ln2_g": g_ln2_g[0], "ln2_b": g_ln2_b[0],
    }
    return loss, grad_x, grads, dmod, from_chips


def _ada_fwd(c_all, w_shard, b_shard):
    n_all, n_col = c_all.shape[0], w_shard.shape[1]
    tn = 512

    def body(c_ref, w_ref, b_ref, cond_ref, mod_ref):
        cond = _silu(c_ref[...])
        cond_ref[...] = cond
        mod_ref[...] = jnp.dot(cond.astype(BF16), w_ref[...].astype(BF16), preferred_element_type=F32) + b_ref[...]

    return pl.pallas_call(
        body, name="ada_fwd", grid=(n_col // tn,),
        in_specs=[pl.BlockSpec((n_all, D_MODEL), lambda j: (0, 0)), pl.BlockSpec((D_MODEL, tn), lambda j: (0, j)),
                  pl.BlockSpec((1, tn), lambda j: (0, j))],
        out_specs=[pl.BlockSpec((n_all, D_MODEL), lambda j: (0, 0)), pl.BlockSpec((n_all, tn), lambda j: (0, j))],
        out_shape=[jax.ShapeDtypeStruct((n_all, D_MODEL), F32), jax.ShapeDtypeStruct((n_all, n_col), F32)],
        compiler_params=_cparams(("arbitrary",)),
    )(c_all, w_shard, b_shard)


def _col_sum(a):
    def body(a_ref, o_ref):
        o_ref[...] = jnp.sum(a_ref[...], 0, keepdims=True)

    return pl.pallas_call(body, name="col_sum", out_shape=jax.ShapeDtypeStruct((1, a.shape[1]), F32))(a)


def _adamw(w, g, m, v, name):
    n_r, n_c = w.shape
    if n_r % 8 == 0:
        tr = _pick(n_r, (256, 64, 32, 16, 8))
        grid, blk = (n_r // tr,), pl.BlockSpec((tr, n_c), lambda i: (i, 0))
    else:
        tc = _pick(n_c, (256, 128))
        grid, blk = (n_c // tc,), pl.BlockSpec((n_r, tc), lambda i: (0, i))

    def body(w_ref, g_ref, m_ref, v_ref, d_ref, nm_ref, nv_ref):
        grad = g_ref[...]
        new_m = ADAM_B1 * m_ref[...] + (1.0 - ADAM_B1) * grad
        new_v = ADAM_B2 * v_ref[...] + (1.0 - ADAM_B2) * (grad * grad)
        m_hat = new_m / (1.0 - ADAM_B1 ** ADAM_STEP)
        v_hat = new_v / (1.0 - ADAM_B2 ** ADAM_STEP)
        d_ref[...] = -ADAM_LR * (m_hat / (jnp.sqrt(v_hat) + ADAM_EPS) + ADAM_WD * w_ref[...])
        nm_ref[...] = new_m
        nv_ref[...] = new_v

    out = jax.ShapeDtypeStruct(w.shape, F32)
    return pl.pallas_call(
        body, name=name, grid=grid, in_specs=[blk] * 4, out_specs=[blk] * 3, out_shape=[out] * 3,
        compiler_params=_cparams(("parallel",)),
    )(w, g, m, v)


HBM_SPEC = pl.BlockSpec(memory_space=pltpu.HBM)
VMEM_SPEC = pl.BlockSpec(memory_space=pltpu.VMEM)
CHIP_FLIPS = ((1, 0), (0, 1), (1, 1))


def _place():
    return lax.axis_index("x"), lax.axis_index("y"), lax.axis_index("c")


def _flip(v, f):
    return 1 - v if f else v


def _all_gather8(slab, name):
    n_r, n_w = slab.shape

    def body(x_ref, o_ref, s_ref, send_sems, recv_sems, local_sem):
        x, y, c = _place()
        me = 4 * x + 2 * y + c
        mine = pltpu.make_async_copy(x_ref, o_ref.at[me], local_sem)
        mine.start()
        peers = [(_flip(x, k & 4), _flip(y, k & 2), _flip(c, k & 1)) for k in range(1, N_DEV)]
        sends = []
        for k, peer in enumerate(peers):
            cp = pltpu.make_async_remote_copy(src_ref=x_ref, dst_ref=o_ref.at[me], send_sem=send_sems.at[k],
                                              recv_sem=recv_sems.at[k], device_id=peer, device_id_type=MESH)
            cp.start()
            sends.append(cp)
        for k, (px, py, pc) in enumerate(peers):
            pltpu.make_async_remote_copy(src_ref=x_ref, dst_ref=o_ref.at[4 * px + 2 * py + pc],
                                         send_sem=send_sems.at[k], recv_sem=recv_sems.at[k],
                                         device_id=(px, py, pc), device_id_type=MESH).wait_recv()
        for cp in sends:
            cp.wait_send()
        mine.wait()
        total = o_ref[0]
        for d in range(1, N_DEV):
            total = total + o_ref[d]
        s_ref[...] = total

    return pl.pallas_call(
        body, name=name, in_specs=[VMEM_SPEC], out_specs=[VMEM_SPEC, VMEM_SPEC],
        out_shape=[jax.ShapeDtypeStruct((N_DEV, n_r, n_w), F32), jax.ShapeDtypeStruct((n_r, n_w), F32)],
        scratch_shapes=[pltpu.SemaphoreType.DMA((N_DEV - 1,)), pltpu.SemaphoreType.DMA((N_DEV - 1,)),
                        pltpu.SemaphoreType.DMA],
    )(slab)


def _gather_weights(shards):
    n_a = len(shards)

    def body(*refs):
        ins, outs, stage = refs[:n_a], refs[n_a:2 * n_a], refs[2 * n_a:3 * n_a]
        send_sems, recv_sems, local_sems = refs[3 * n_a:]
        x, y, c = _place()
        me_chip = 2 * x + y
        sibling = (x, y, 1 - c)
        chips = [(_flip(x, fx), _flip(y, fy)) for fx, fy in CHIP_FLIPS]
        stage_in = [pltpu.make_async_copy(ins[k], stage[k], local_sems.at[k]) for k in range(n_a)]
        for cp in stage_in:
            cp.start()

        def copy(k, slot, chip_of_block, half, to, src=None):
            dst = outs[k].at[chip_of_block, half]
            return pltpu.make_async_remote_copy(src_ref=dst if src is None else src, dst_ref=dst,
                                                send_sem=send_sems.at[k * 6 + slot], recv_sem=recv_sems.at[k * 6 + slot],
                                                device_id=to, device_id_type=MESH)

        first = [copy(k, r, me_chip, c, (*chips[r], c), src=ins[k].at[c]) for k in range(n_a) for r in range(3)]
        for cp in first:
            cp.start()
        stage_out = []
        for k in range(n_a):
            stage_in[k].wait()
            cp = pltpu.make_async_copy(stage[k], outs[k].at[me_chip], local_sems.at[n_a + k])
            cp.start()
            stage_out.append(cp)
        passed = []
        for k in range(n_a):
            for r, (px, py) in enumerate(chips):
                copy(k, r, 2 * px + py, c, (x, y, c)).wait_recv()
                fwd = copy(k, 3 + r, 2 * px + py, c, sibling)
                fwd.start()
                passed.append(fwd)
        for k in range(n_a):
            for r, (px, py) in enumerate(chips):
                copy(k, 3 + r, 2 * px + py, 1 - c, (x, y, c)).wait_recv()
        for cp in first + passed:
            cp.wait_send()
        for cp in stage_out:
            cp.wait()

    return pl.pallas_call(
        body, name="gather_weights", in_specs=[HBM_SPEC] * n_a, out_specs=[HBM_SPEC] * n_a,
        out_shape=[jax.ShapeDtypeStruct((N_CHIPS,) + s.shape, s.dtype) for s in shards],
        scratch_shapes=[pltpu.VMEM(s.shape, s.dtype) for s in shards]
        + [pltpu.SemaphoreType.DMA((6 * n_a,)), pltpu.SemaphoreType.DMA((6 * n_a,)),
           pltpu.SemaphoreType.DMA((2 * n_a,))],
        compiler_params=pltpu.CompilerParams(vmem_limit_bytes=VMEM_LIMIT),
    )(*shards)


def _gather_rider(shards):
    n_a = len(shards)

    def plan(ins, outs, sems):
        send_sems, recv_sems = sems
        x, y, c = _place()
        chips = [(_flip(x, fx), _flip(y, fy)) for fx, fy in CHIP_FLIPS]

        def copy(k, slot, chip_of_block, half, to, src=None):
            dst = outs[k].at[chip_of_block, half]
            return pltpu.make_async_remote_copy(src_ref=dst if src is None else src, dst_ref=dst,
                                                send_sem=send_sems.at[k * 6 + slot], recv_sem=recv_sems.at[k * 6 + slot],
                                                device_id=to, device_id_type=MESH)

        first = [copy(k, r, 2 * x + y, c, (*chips[r], c), src=ins[k].at[c]) for k in range(n_a) for r in range(3)]
        return copy, chips, first, (x, y, c)

    def first_step(ins, outs, sems):
        for cp in plan(ins, outs, sems)[2]:
            cp.start()

    def last_step(ins, outs, sems):
        copy, chips, first, (x, y, c) = plan(ins, outs, sems)
        passed = []
        for k in range(n_a):
            for r, (px, py) in enumerate(chips):
                copy(k, r, 2 * px + py, c, (x, y, c)).wait_recv()
                fwd = copy(k, 3 + r, 2 * px + py, c, (x, y, 1 - c))
                fwd.start()
                passed.append(fwd)
        for k in range(n_a):
            for r, (px, py) in enumerate(chips):
                copy(k, 3 + r, 2 * px + py, 1 - c, (x, y, c)).wait_recv()
        for cp in first + passed:
            cp.wait_send()

    return Rider(shards, [jax.ShapeDtypeStruct((N_CHIPS,) + s.shape, s.dtype) for s in shards],
                 [pltpu.SemaphoreType.DMA((6 * n_a,)), pltpu.SemaphoreType.DMA((6 * n_a,))], first_step, last_step)


def _place_own(gathered, shard, chip, name):
    _, _, n_h, n_c = gathered.shape
    th = _pick(n_h, (256, 176, 128))

    def body(sel_ref, s_ref, _, o_ref):
        o_ref[...] = s_ref[...]

    grid_spec = pltpu.PrefetchScalarGridSpec(
        num_scalar_prefetch=1, grid=(2, n_h // th),
        in_specs=[pl.BlockSpec((None, th, n_c), lambda hf, i, sel: (hf, i, 0)), pl.BlockSpec(memory_space=pl.ANY)],
        out_specs=pl.BlockSpec((None, None, th, n_c), lambda hf, i, sel: (sel[0], hf, i, 0)))
    return pl.pallas_call(
        body, name=name, grid_spec=grid_spec, out_shape=jax.ShapeDtypeStruct(gathered.shape, gathered.dtype),
        input_output_aliases={2: 0}, compiler_params=_cparams(("parallel", "parallel")),
    )(chip.reshape(1), shard, gathered)


def _pair_rider(parts):
    n_a = len(parts)

    def plan(ins, outs, sems):
        send_sems, recv_sems = sems
        x, y, c = _place()
        return [pltpu.make_async_remote_copy(src_ref=ins[k].at[:, 1 - c], dst_ref=outs[k], send_sem=send_sems.at[k],
                                             recv_sem=recv_sems.at[k], device_id=(x, y, 1 - c), device_id_type=MESH)
                for k in range(n_a)]

    def first_step(ins, outs, sems):
        for cp in plan(ins, outs, sems):
            cp.start()

    def last_step(ins, outs, sems):
        for cp in plan(ins, outs, sems):
            cp.wait()

    return Rider(parts, [jax.ShapeDtypeStruct((N_CHIPS,) + p.shape[2:], F32) for p in parts],
                 [pltpu.SemaphoreType.DMA((n_a,)), pltpu.SemaphoreType.DMA((n_a,))], first_step, last_step)


def _alone(rider, name):
    n_a = len(rider.inputs)

    def body(*refs):
        parts = (refs[:n_a], refs[n_a:2 * n_a], refs[2 * n_a:])
        rider.first(*parts)
        rider.last(*parts)

    return pl.pallas_call(
        body, name=name, in_specs=[HBM_SPEC] * n_a, out_specs=[HBM_SPEC] * n_a,
        out_shape=rider.out_shapes, scratch_shapes=rider.sems,
    )(*rider.inputs)


def _chips_rider(sums):
    n_a = len(sums)

    def plan(ins, outs, sems):
        send_sems, recv_sems = sems
        x, y, c = _place()
        cps = []
        for k in range(n_a):
            for r, (fx, fy) in enumerate(CHIP_FLIPS):
                px, py = _flip(x, fx), _flip(y, fy)
                cps.append(pltpu.make_async_remote_copy(
                    src_ref=ins[k].at[2 * px + py], dst_ref=outs[k].at[r], send_sem=send_sems.at[3 * k + r],
                    recv_sem=recv_sems.at[3 * k + r], device_id=(px, py, c), device_id_type=MESH))
        return cps

    def first_step(ins, outs, sems):
        for cp in plan(ins, outs, sems):
            cp.start()

    def last_step(ins, outs, sems):
        for cp in plan(ins, outs, sems):
            cp.wait()

    return Rider(sums, [jax.ShapeDtypeStruct((3,) + s.shape[1:], s.dtype) for s in sums],
                 [pltpu.SemaphoreType.DMA((3 * n_a,)), pltpu.SemaphoreType.DMA((3 * n_a,))], first_step, last_step)


def _rs_share(bufs):
    n_a = len(bufs)

    def body(*refs):
        ins, outs = refs[:n_a], refs[n_a:2 * n_a]
        send_sems, recv_sems = refs[2 * n_a:]
        x, y, c = _place()
        sends = [pltpu.make_async_remote_copy(src_ref=ins[k].at[c], dst_ref=outs[k].at[c], send_sem=send_sems.at[k],
                                              recv_sem=recv_sems.at[k], device_id=(x, y, 1 - c), device_id_type=MESH)
                 for k in range(n_a)]
        for cp in sends:
            cp.start()
        for k in range(n_a):
            pltpu.make_async_remote_copy(src_ref=ins[k].at[c], dst_ref=outs[k].at[1 - c], send_sem=send_sems.at[k],
                                         recv_sem=recv_sems.at[k], device_id=(x, y, 1 - c),
                                         device_id_type=MESH).wait_recv()
        for cp in sends:
            cp.wait_send()

    return pl.pallas_call(
        body, name="rs_share", in_specs=[HBM_SPEC] * n_a, out_specs=[HBM_SPEC] * n_a,
        out_shape=[jax.ShapeDtypeStruct(s.shape, F32) for s in bufs],
        input_output_aliases={k: k for k in range(n_a)},
        scratch_shapes=[pltpu.SemaphoreType.DMA((n_a,)), pltpu.SemaphoreType.DMA((n_a,))],
    )(*bufs)


def _pair_add(part, recv, core, name):
    _, _, n_h, n_c = part.shape
    th = _pick(n_h, (256, 176, 128))

    def body(sel_ref, p_ref, r_ref, o_ref):
        o_ref[...] = (p_ref[...] + r_ref[...]).astype(BF16)

    grid_spec = pltpu.PrefetchScalarGridSpec(
        num_scalar_prefetch=1, grid=(N_CHIPS, n_h // th),
        in_specs=[pl.BlockSpec((None, None, th, n_c), lambda j, i, sel: (j, sel[0], i, 0)),
                  pl.BlockSpec((None, th, n_c), lambda j, i, sel: (j, i, 0))],
        out_specs=pl.BlockSpec((None, th, n_c), lambda j, i, sel: (j, i, 0)))
    return pl.pallas_call(
        body, name=name, grid_spec=grid_spec, out_shape=jax.ShapeDtypeStruct(recv.shape, BF16),
        compiler_params=_cparams(("parallel", "parallel")),
    )(core.reshape(1), part, recv)


def _chip_add(sums, recv, chip, core, name):
    _, n_h, n_c = sums.shape
    th = _pick(n_h, (256, 176, 128))

    def body(sel_ref, s_ref, r_ref, o_ref):
        total = s_ref[...].astype(F32)
        for r in range(3):
            total = total + r_ref[r].astype(F32)
        o_ref[...] = total

    grid_spec = pltpu.PrefetchScalarGridSpec(
        num_scalar_prefetch=1, grid=(n_h // th,),
        in_specs=[pl.BlockSpec((None, th, n_c), lambda i, sel: (sel[0], i, 0)),
                  pl.BlockSpec((3, th, n_c), lambda i, sel: (0, i, 0))],
        out_specs=pl.BlockSpec((None, th, n_c), lambda i, sel: (sel[1], i, 0)))
    return pl.pallas_call(
        body, name=name, grid_spec=grid_spec, out_shape=jax.ShapeDtypeStruct((2, n_h, n_c), F32),
        compiler_params=_cparams(("parallel",)),
    )(jnp.stack([chip, core]), sums, recv)


def _row_halves(a):
    return a.reshape(N_CHIPS, 2, -1, a.shape[-1])


class StepComm:
    REST = ("w_o", "w_up", "w_down")

    def __init__(self, core, chip, rest_shards, in_cols):
        self.core, self.chip, self.shards, self.in_cols = core, chip, rest_shards, in_cols

    def fwd_rider(self):
        return _gather_rider(self.shards)

    def weights_from(self, landed):
        g_o, g_up, g_down = (_place_own(g, s, self.chip, "place_own_" + n)
                             for g, s, n in zip(landed, self.shards, self.REST))
        return {"w_o": g_o.reshape(-1, D_MODEL), "w_up": g_up.reshape(N_CHIPS, -1, g_up.shape[-1]),
                "w_down": g_down.reshape(-1, D_MODEL)}

    def _add_pairs(self, parts, from_sibling, names):
        return [_pair_add(p, r, self.core, "pair_add_" + n) for p, r, n in zip(parts, from_sibling, names)]

    def ffn_pair_rider(self, g_w_up, g_w_down):
        self.ffn_parts = [_row_halves(g_w_up), _row_halves(g_w_down)]
        return _pair_rider(self.ffn_parts)

    def ffn_chips_rider(self, from_sibling):
        self.ffn_sums = self._add_pairs(self.ffn_parts, from_sibling, ("w_up", "w_down"))
        return _chips_rider(self.ffn_sums)

    def tail_chips_rider(self, g_w_in_p, g_w_o):
        parts = [_row_halves(_w_in_to_chips(g_w_in_p, self.in_cols)), _row_halves(g_w_o)]
        self.tail_sums = self._add_pairs(parts, _alone(_pair_rider(parts), "rs_pair_tail"), ("w_in", "w_o"))
        return _chips_rider(self.tail_sums)

    def finish(self, ffn_from_chips, tail_from_chips):
        halves = [_chip_add(s, r, self.chip, self.core, "chip_add_" + n)
                  for s, r, n in zip(self.tail_sums + self.ffn_sums, list(tail_from_chips) + list(ffn_from_chips),
                                     ("w_in", "w_o", "w_up", "w_down"))]
        return [f.reshape(-1, f.shape[-1]) for f in _rs_share(halves)]


SLAB_W = 1024


def _pack(arrays, rows):
    flat = jnp.concatenate([a.reshape(-1).astype(F32) for a in arrays])
    return jnp.pad(flat, (0, rows * SLAB_W - flat.shape[0])).reshape(rows, SLAB_W)


def _unpack(flat, shapes):
    out, off = [], 0
    for s in shapes:
        n = 1
        for d in s:
            n *= d
        out.append(flat[off:off + n].reshape(s))
        off += n
    return out


def _rows_for(arrays_or_shapes):
    n = 0
    for a in arrays_or_shapes:
        s = a if isinstance(a, tuple) else a.shape
        k = 1
        for d in s:
            k *= d
        n += k
    return -(-n // (8 * SLAB_W)) * 8


def kernel(x, c, ln0_g, ln0_b, w_ada, b_ada, w_in, dn_conv, dn_a_log, dn_dt_bias, dn_norm_g, gla_w_gate2, gla_b_gate, gla_norm_g, w_o, ln1_g, ln1_b, ffn_w_up, ffn_conv, ffn_conv_b, ffn_w_down, ln2_g, ln2_b, loss_target, m_ln0_g, m_ln0_b, m_w_ada, m_b_ada, m_w_in, m_dn_conv, m_dn_a_log, m_dn_dt_bias, m_dn_norm_g, m_gla_w_gate2, m_gla_b_gate, m_gla_norm_g, m_w_o, m_ln1_g, m_ln1_b, m_ffn_w_up, m_ffn_conv, m_ffn_conv_b, m_ffn_w_down, m_ln2_g, m_ln2_b, v_ln0_g, v_ln0_b, v_w_ada, v_b_ada, v_w_in, v_dn_conv, v_dn_a_log, v_dn_dt_bias, v_dn_norm_g, v_gla_w_gate2, v_gla_b_gate, v_gla_norm_g, v_w_o, v_ln1_g, v_ln1_b, v_ffn_w_up, v_ffn_conv, v_ffn_conv_b, v_ffn_w_down, v_ln2_g, v_ln2_b):
    n_b, t_len, _ = x.shape
    xi, yi, ci = _place()
    chip = (2 * xi + yi).astype(jnp.int32)
    core = ci.astype(jnp.int32)
    me = 2 * chip + core
    n_all = N_DEV * n_b
    ada_cols = w_ada.shape[2]

    sharded_small = [dn_conv[0], gla_w_gate2[0], ffn_conv[0]]
    slab = _pack([c] + sharded_small, _rows_for([c] + sharded_small))
    gathered, _ = _all_gather8(slab, "gather_small")
    flat = gathered.reshape(N_DEV, -1)
    per_dev = [_unpack(flat[d], [c.shape] + [a.shape for a in sharded_small]) for d in range(N_DEV)]
    c_all = jnp.concatenate([per_dev[d][0] for d in range(N_DEV)], axis=0)
    dn_conv_f, gate2_f, ffn_conv_f = (jnp.concatenate([per_dev[2 * j][i] for j in range(N_CHIPS)], axis=1)
                                      for i in (1, 2, 3))

    b_ada_shard = lax.dynamic_slice(b_ada, (0, chip * ada_cols), (1, ada_cols))
    cond_all, mod_cols = _ada_fwd(c_all, w_ada[0], b_ada_shard)
    mod_g, _ = _all_gather8(mod_cols, "gather_mod")
    mod_full = jnp.concatenate([mod_g[2 * j] for j in range(N_CHIPS)], axis=1)
    mod = lax.dynamic_slice(mod_full, (me * n_b, 0), (n_b, 6 * D_MODEL)).reshape(n_b, 6, D_MODEL)

    halves = lambda a: a.astype(BF16).reshape(2, a.shape[0] // 2, a.shape[1])
    (g_in,) = _gather_weights([halves(w_in[0])])
    comm = StepComm(core, chip, [halves(w_o[0]), halves(ffn_w_up[0]), halves(ffn_w_down[0])], w_in.shape[2])
    params = {
        "w_in_p": _w_in_to_padded(g_in.reshape(N_CHIPS, -1, g_in.shape[-1])),
        "dn_conv": dn_conv_f, "dn_a_log": dn_a_log[0], "dn_dt_bias": dn_dt_bias[0], "dn_norm_g": dn_norm_g[0],
        "gla_w_gate2": gate2_f, "gla_b_gate": gla_b_gate[0], "gla_norm_g": gla_norm_g[0],
        "ln0_g": ln0_g, "ln0_b": ln0_b, "ln1_g": ln1_g[0], "ln1_b": ln1_b[0], "ln2_g": ln2_g[0], "ln2_b": ln2_b[0],
        "ffn_conv": ffn_conv_f, "ffn_conv_b": ffn_conv_b[0],
    }

    loss_row, grad_x, gp, dmod, from_chips = _local_step(
        x.reshape(n_b * t_len, D_MODEL), loss_target.reshape(n_b * t_len, D_MODEL), mod, params, n_b, t_len, comm)
    loss = lax.psum(loss_row[0, 0], ("x", "y", "c"))

    summed_names = ["ln0_g", "ln0_b", "dn_conv", "dn_a_log", "dn_dt_bias", "dn_norm_g", "gla_w_gate2", "gla_b_gate",
                    "gla_norm_g", "ln1_g", "ln1_b", "ffn_conv", "ffn_conv_b", "ln2_g", "ln2_b"]
    summed_parts = [gp[n] for n in summed_names]
    sum_rows = _rows_for(summed_parts)
    slab = jnp.concatenate([_pack(summed_parts, sum_rows), _pack([dmod], _rows_for([dmod]))], axis=0)
    gathered, total = _all_gather8(slab, "reduce_small")
    small_g = dict(zip(summed_names, _unpack(total.reshape(-1), [a.shape for a in summed_parts])))
    dmod_rows = n_b * 6 * D_MODEL // SLAB_W
    dmod_all = gathered[:, sum_rows:sum_rows + dmod_rows, :].reshape(n_all, 6 * D_MODEL)

    g_b_ada = _col_sum(dmod_all)
    dmod_cols = lax.dynamic_slice(dmod_all, (0, chip * ada_cols), (n_all, ada_cols))
    g_w_ada = _mm(cond_all, dmod_cols, ta=True, name="mm_g_ada")

    g_w_in, g_w_o, g_w_up, g_w_down = comm.finish(*from_chips)

    col_block = lambda a: lax.dynamic_slice(a, (0, chip * (a.shape[1] // N_CHIPS)), (a.shape[0], a.shape[1] // N_CHIPS))
    grads = {
        "ln0_g": small_g["ln0_g"], "ln0_b": small_g["ln0_b"], "w_ada": g_w_ada[None], "b_ada": g_b_ada,
        "w_in": g_w_in[None], "dn_conv": col_block(small_g["dn_conv"])[None], "dn_a_log": small_g["dn_a_log"][None],
        "dn_dt_bias": small_g["dn_dt_bias"][None], "dn_norm_g": small_g["dn_norm_g"][None],
        "gla_w_gate2": col_block(small_g["gla_w_gate2"])[None], "gla_b_gate": small_g["gla_b_gate"][None],
        "gla_norm_g": small_g["gla_norm_g"][None], "w_o": g_w_o[None], "ln1_g": small_g["ln1_g"][None],
        "ln1_b": small_g["ln1_b"][None], "ffn_w_up": g_w_up[None], "ffn_conv": col_block(small_g["ffn_conv"])[None],
        "ffn_conv_b": small_g["ffn_conv_b"][None], "ffn_w_down": g_w_down[None], "ln2_g": small_g["ln2_g"][None],
        "ln2_b": small_g["ln2_b"][None],
    }
    names = ["ln0_g", "ln0_b", "w_ada", "b_ada", "w_in", "dn_conv", "dn_a_log", "dn_dt_bias", "dn_norm_g",
             "gla_w_gate2", "gla_b_gate", "gla_norm_g", "w_o", "ln1_g", "ln1_b", "ffn_w_up", "ffn_conv", "ffn_conv_b",
             "ffn_w_down", "ln2_g", "ln2_b"]
    weights = dict(zip(names, [ln0_g, ln0_b, w_ada, b_ada, w_in, dn_conv, dn_a_log, dn_dt_bias, dn_norm_g, gla_w_gate2,
                               gla_b_gate, gla_norm_g, w_o, ln1_g, ln1_b, ffn_w_up, ffn_conv, ffn_conv_b, ffn_w_down,
                               ln2_g, ln2_b]))
    m_in = dict(zip(names, [m_ln0_g, m_ln0_b, m_w_ada, m_b_ada, m_w_in, m_dn_conv, m_dn_a_log, m_dn_dt_bias,
                            m_dn_norm_g, m_gla_w_gate2, m_gla_b_gate, m_gla_norm_g, m_w_o, m_ln1_g, m_ln1_b,
                            m_ffn_w_up, m_ffn_conv, m_ffn_conv_b, m_ffn_w_down, m_ln2_g, m_ln2_b]))
    v_in = dict(zip(names, [v_ln0_g, v_ln0_b, v_w_ada, v_b_ada, v_w_in, v_dn_conv, v_dn_a_log, v_dn_dt_bias,
                            v_dn_norm_g, v_gla_w_gate2, v_gla_b_gate, v_gla_norm_g, v_w_o, v_ln1_g, v_ln1_b,
                            v_ffn_w_up, v_ffn_conv, v_ffn_conv_b, v_ffn_w_down, v_ln2_g, v_ln2_b]))

    big = ("w_ada", "w_in", "w_o", "ffn_w_up", "ffn_w_down")
    delta, new_m, new_v = {}, {}, {}
    for n in big:
        view = (lambda a: a.T) if n == "w_in" else (lambda a: a)
        d_n, m_n, v_n = _adamw(view(weights[n][0]), view(grads[n][0]), view(m_in[n][0]), view(v_in[n][0]), "adamw_" + n)
        delta[n], new_m[n], new_v[n] = view(d_n)[None], view(m_n)[None], view(v_n)[None]
    small = [n for n in names if n not in big]
    shapes = [weights[n].shape for n in small]
    rows = _rows_for(shapes)
    d_s, m_s, v_s = _adamw(_pack([weights[n] for n in small], rows), _pack([grads[n] for n in small], rows),
                           _pack([m_in[n] for n in small], rows), _pack([v_in[n] for n in small], rows), "adamw_small")
    for out, slab_out in ((delta, d_s), (new_m, m_s), (new_v, v_s)):
        out.update(zip(small, _unpack(slab_out.reshape(-1), shapes)))

    return (loss, grad_x.reshape(x.shape), *[grads[n] for n in names], *[delta[n] for n in names],
            *[new_m[n] for n in names], *[new_v[n] for n in names])
```

```python
import functools

import jax
import jax.numpy as jnp
from jax import lax
from jax.experimental import pallas as pl
from jax.experimental.pallas import tpu as pltpu

F32 = jnp.float32
BF16 = jnp.bfloat16
MESH = pl.DeviceIdType.MESH

D_MODEL = 1024
HEADS = 4
HEAD_DIM = 128
GLA_KEY = 64
GATE_RANK = 16
CHUNK = 64
D_FF = 2816
ALPHA = 2.0 ** 0.25
EPS = 1e-6
N_CHIPS = 4
N_DEV = 8

PROJ_W = 3840
OFF_GQ, OFF_GK, OFF_GV, OFF_GG, OFF_SMALL, GLA_W = 0, 256, 512, 1024, 1536, 1792
OFF_Z = 2048
W_IN_COLS = 3608


def _qkv_block(j):
    return jnp.where(j < 2, GLA_W // 128 + j, (OFF_Z + 512) // 128 - 2 + j)

ADAM_LR, ADAM_B1, ADAM_B2, ADAM_EPS, ADAM_WD, ADAM_STEP = 0.001, 0.9, 0.999, 1e-08, 0.01, 10

VMEM_LIMIT = 56 * 1024 * 1024
ROW_TILE = 512


def _cparams(sem):
    return pltpu.CompilerParams(dimension_semantics=sem, vmem_limit_bytes=VMEM_LIMIT)


def _pick(n, prefs):
    for p in prefs:
        if n % p == 0:
            return p
    return n


def _mm(a, b, *, ta=False, tb=False, out_slabs=1, out_dtype=F32, name, rider=None):
    a_slabs = a.shape[0] if a.ndim == 3 else 1
    b_slabs = b.shape[0] if b.ndim == 3 else 1
    assert not (ta and a_slabs > 1)
    a2, b2 = a.shape[-2:], b.shape[-2:]
    if ta:
        k_dim, m_dim = a2
    else:
        m_dim, k_dim = a2[0], a2[1] * a_slabs
    n_dim = b2[0] if tb else b2[1] * b_slabs
    k_slabs = max(a_slabs, b_slabs if tb else 1)
    n_slabs = max(out_slabs, 1 if tb else b_slabs)
    tm = _pick(m_dim, (1024, 1408, 512, 256, 128))
    tn = _pick(n_dim // n_slabs, (1536, 1408, 1280, 1024, 768, 512, 384, 256, 128))
    tk = _pick(k_dim // k_slabs, (1408, 1280, 1024, 512, 256, 128))
    nk, nj = k_dim // tk, n_dim // tn
    nk_a, nk_b, nj_b, nj_o = nk // a_slabs, nk // b_slabs, nj // b_slabs, nj // out_slabs
    dims = (((0 if ta else 1,), (1 if tb else 0,)), ((), ()))

    grid = (m_dim // tm, nj, nk)
    assert out_dtype == F32
    r_inputs, r_in_specs, r_out_specs, r_sems, split = _with_rider(rider, 2, 1, 0)

    def body(*refs):
        (a_ref, b_ref, o_ref), parts = split(refs)
        ride_first, ride_last = _ride(rider, parts, grid)
        if rider is not None:
            ride_first()
        prod = lax.dot_general(a_ref[...].astype(BF16), b_ref[...].astype(BF16), dims, preferred_element_type=F32)
        if nk == 1:
            o_ref[...] = prod
        else:
            _acc(o_ref, prod, pl.program_id(2) == 0)
        if rider is not None:
            ride_last()

    if ta:
        a_spec = pl.BlockSpec((tk, tm), lambda i, j, k: (k, i))
    elif a_slabs > 1:
        a_spec = pl.BlockSpec((None, tm, tk), lambda i, j, k: (k // nk_a, i, k % nk_a))
    else:
        a_spec = pl.BlockSpec((tm, tk), lambda i, j, k: (i, k))
    if tb and b_slabs > 1:
        b_spec = pl.BlockSpec((None, tn, tk), lambda i, j, k: (k // nk_b, j, k % nk_b))
    elif tb:
        b_spec = pl.BlockSpec((tn, tk), lambda i, j, k: (j, k))
    elif b_slabs > 1:
        b_spec = pl.BlockSpec((None, tk, tn), lambda i, j, k: (j // nj_b, k, j % nj_b))
    else:
        b_spec = pl.BlockSpec((tk, tn), lambda i, j, k: (k, j))
    if out_slabs > 1:
        o_spec = pl.BlockSpec((None, tm, tn), lambda i, j, k: (j // nj_o, i, j % nj_o))
        o_shape = (out_slabs, m_dim, n_dim // out_slabs)
    else:
        o_spec, o_shape = pl.BlockSpec((tm, tn), lambda i, j, k: (i, j)), (m_dim, n_dim)
    out, *rider_outs = pl.pallas_call(
        body, name=name, grid=grid,
        in_specs=[a_spec, b_spec] + r_in_specs, out_specs=[o_spec] + r_out_specs,
        out_shape=[jax.ShapeDtypeStruct(o_shape, out_dtype)] + (list(rider.out_shapes) if rider else []),
        scratch_shapes=r_sems,
        compiler_params=_cparams(("arbitrary",) * 3 if rider else ("parallel", "parallel", "arbitrary")),
    )(a, b, *r_inputs)
    return (out, rider_outs) if rider else out


def _ln(x, g, b):
    mu = jnp.mean(x, -1, keepdims=True)
    xc = x - mu
    var = jnp.mean(xc * xc, -1, keepdims=True)
    return xc * lax.rsqrt(var + EPS) * g + b


def _softplus(x):
    return jnp.maximum(x, 0.0) + jnp.log(1.0 + jnp.exp(-jnp.abs(x)))


def _silu(x):
    return x * jax.nn.sigmoid(x)


def _dsilu(x):
    s = jax.nn.sigmoid(x)
    return s * (1.0 + x * (1.0 - s))


def _f_ln0(x, g, b, sc, sh):
    x0 = _ln(x, g, b)
    return x0, x0 * (1.0 + sc) + sh


def _f_ln1(x0, y, gt, g, b, sc, sh):
    x1 = _ln(ALPHA * x0 + (1.0 + gt) * y, g, b)
    return x1, x1 * (1.0 + sc) + sh


def _f_ln2_loss(x1, y2, gt, g, b, tgt):
    x2 = _ln(ALPHA * x1 + (1.0 + gt) * y2, g, b)
    err = x2 - tgt
    per_row = jnp.sum(err * err, -1, keepdims=True) * (0.5 / D_MODEL)
    return jnp.sum(per_row, 0, keepdims=True)


def _row_specs(t_len):
    nt = t_len // ROW_TILE
    row = pl.BlockSpec((ROW_TILE, D_MODEL), lambda b, i: (b * nt + i, 0))
    vec = pl.BlockSpec((1, D_MODEL), lambda b, i: (0, 0))
    mod = pl.BlockSpec((None, 6, D_MODEL), lambda b, i: (b, 0, 0))
    return nt, row, vec, mod


def _first_step():
    return jnp.logical_and(pl.program_id(0) == 0, pl.program_id(1) == 0)


def _acc(ref, val, first, at=(Ellipsis,)):
    @pl.when(first)
    def _():
        ref[at] = val

    @pl.when(jnp.logical_not(first))
    def _():
        ref[at] += val


def _acc_rows(ref, rows, first):
    for i, r in enumerate(rows):
        _acc(ref, r, first, at=(slice(i, i + 1), slice(None)))


def _ln0_fwd(x, g, b, mod, n_b, t_len):
    nt, row, vec, mods = _row_specs(t_len)

    def body(x_ref, g_ref, b_ref, mod_ref, x0_ref, h_ref):
        x0, h = _f_ln0(x_ref[...], g_ref[...], b_ref[...], mod_ref[1:2, :], mod_ref[0:1, :])
        x0_ref[...] = x0
        h_ref[...] = h.astype(BF16)

    return pl.pallas_call(
        body, name="ln0_fwd", grid=(n_b, nt), in_specs=[row, vec, vec, mods], out_specs=[row, row],
        out_shape=[jax.ShapeDtypeStruct(x.shape, F32), jax.ShapeDtypeStruct(x.shape, BF16)],
        compiler_params=_cparams(("parallel", "parallel")),
    )(x, g, b, mod)


def _ln0_bwd(x, g, b, mod, dx0, dh, n_b, t_len):
    nt, row, vec, mods = _row_specs(t_len)
    dmod_spec = pl.BlockSpec((None, 2, D_MODEL), lambda bb, i: (bb, 0, 0))

    def body(x_ref, g_ref, b_ref, mod_ref, dx0_ref, dh_ref, dx_ref, dg_ref, db_ref, dmod_ref):
        _, pull = jax.vjp(_f_ln0, x_ref[...], g_ref[...], b_ref[...], mod_ref[1:2, :], mod_ref[0:1, :])
        dx, dg, db, dsc, dsh = pull((dx0_ref[...], dh_ref[...]))
        dx_ref[...] = dx
        _acc(dg_ref, dg, _first_step())
        _acc(db_ref, db, _first_step())
        _acc_rows(dmod_ref, [dsh, dsc], pl.program_id(1) == 0)

    return pl.pallas_call(
        body, name="ln0_bwd", grid=(n_b, nt), in_specs=[row, vec, vec, mods, row, row],
        out_specs=[row, vec, vec, dmod_spec],
        out_shape=[jax.ShapeDtypeStruct(x.shape, F32), jax.ShapeDtypeStruct((1, D_MODEL), F32),
                   jax.ShapeDtypeStruct((1, D_MODEL), F32), jax.ShapeDtypeStruct((n_b, 2, D_MODEL), F32)],
        compiler_params=_cparams(("arbitrary", "arbitrary")),
    )(x, g, b, mod, dx0, dh)


def _ln1_fwd(x0, y, g, b, mod, n_b, t_len):
    nt, row, vec, mods = _row_specs(t_len)

    def body(x0_ref, y_ref, g_ref, b_ref, mod_ref, x1_ref, h_ref):
        x1, h = _f_ln1(x0_ref[...], y_ref[...], mod_ref[2:3, :], g_ref[...], b_ref[...],
                       mod_ref[4:5, :], mod_ref[3:4, :])
        x1_ref[...] = x1
        h_ref[...] = h.astype(BF16)

    return pl.pallas_call(
        body, name="ln1_fwd", grid=(n_b, nt), in_specs=[row, row, vec, vec, mods], out_specs=[row, row],
        out_shape=[jax.ShapeDtypeStruct(x0.shape, F32), jax.ShapeDtypeStruct(x0.shape, BF16)],
        compiler_params=_cparams(("parallel", "parallel")),
    )(x0, y, g, b, mod)


def _ln1_bwd(x0, y, g, b, mod, dx1, dh, n_b, t_len):
    nt, row, vec, mods = _row_specs(t_len)
    dmod_spec = pl.BlockSpec((None, 3, D_MODEL), lambda bb, i: (bb, 0, 0))

    def body(x0_ref, y_ref, g_ref, b_ref, mod_ref, dx1_ref, dh_ref, dx0_ref, dy_ref, dg_ref, db_ref, dmod_ref):
        _, pull = jax.vjp(_f_ln1, x0_ref[...], y_ref[...], mod_ref[2:3, :], g_ref[...], b_ref[...],
                          mod_ref[4:5, :], mod_ref[3:4, :])
        dx0, dy, dgt, dg, db, dsc, dsh = pull((dx1_ref[...], dh_ref[...]))
        dx0_ref[...] = dx0
        dy_ref[...] = dy.astype(BF16)
        _acc(dg_ref, dg, _first_step())
        _acc(db_ref, db, _first_step())
        _acc_rows(dmod_ref, [dgt, dsh, dsc], pl.program_id(1) == 0)

    return pl.pallas_call(
        body, name="ln1_bwd", grid=(n_b, nt), in_specs=[row, row, vec, vec, mods, row, row],
        out_specs=[row, row, vec, vec, dmod_spec],
        out_shape=[jax.ShapeDtypeStruct(x0.shape, F32), jax.ShapeDtypeStruct(x0.shape, BF16),
                   jax.ShapeDtypeStruct((1, D_MODEL), F32), jax.ShapeDtypeStruct((1, D_MODEL), F32),
                   jax.ShapeDtypeStruct((n_b, 3, D_MODEL), F32)],
        compiler_params=_cparams(("arbitrary", "arbitrary")),
    )(x0, y, g, b, mod, dx1, dh)


def _ln2_loss_bwd(x1, y2, g, b, mod, tgt, n_b, t_len):
    nt, row, vec, mods = _row_specs(t_len)
    one = pl.BlockSpec((1, 128), lambda bb, i: (0, 0))
    dmod_spec = pl.BlockSpec((None, 1, D_MODEL), lambda bb, i: (bb, 0, 0))

    def body(x1_ref, y2_ref, g_ref, b_ref, mod_ref, t_ref, loss_ref, dx1_ref, dy2_ref, dg_ref, db_ref, dgt_ref):
        loss, pull = jax.vjp(functools.partial(_f_ln2_loss, tgt=t_ref[...]), x1_ref[...], y2_ref[...],
                             mod_ref[5:6, :], g_ref[...], b_ref[...])
        dx1, dy2, dgt, dg, db = pull(jnp.ones((1, 1), F32))
        dx1_ref[...] = dx1
        dy2_ref[...] = dy2.astype(BF16)
        _acc(loss_ref, jnp.broadcast_to(loss, (1, 128)), _first_step())
        _acc(dg_ref, dg, _first_step())
        _acc(db_ref, db, _first_step())
        _acc(dgt_ref, dgt, pl.program_id(1) == 0)

    return pl.pallas_call(
        body, name="ln2_loss_bwd", grid=(n_b, nt), in_specs=[row, row, vec, vec, mods, row],
        out_specs=[one, row, row, vec, vec, dmod_spec],
        out_shape=[jax.ShapeDtypeStruct((1, 128), F32), jax.ShapeDtypeStruct(x1.shape, F32),
                   jax.ShapeDtypeStruct(x1.shape, BF16), jax.ShapeDtypeStruct((1, D_MODEL), F32),
                   jax.ShapeDtypeStruct((1, D_MODEL), F32), jax.ShapeDtypeStruct((n_b, 1, D_MODEL), F32)],
        compiler_params=_cparams(("arbitrary", "arbitrary")),
    )(x1, y2, g, b, mod, tgt)


def _shift_down(x, s):
    if s == 0:
        return x
    rows = lax.broadcasted_iota(jnp.int32, x.shape, 0)
    return jnp.where(rows >= s, pltpu.roll(x, s, 0), 0.0)


def _shift_up(x, s):
    if s == 0:
        return x
    t_len = x.shape[0]
    rows = lax.broadcasted_iota(jnp.int32, x.shape, 0)
    return jnp.where(rows < t_len - s, pltpu.roll(x, t_len - s, 0), 0.0)


def _taps(x, k_w):
    return [_shift_down(x, k_w - 1 - k) for k in range(k_w)]


def _conv(taps, w):
    out = w[0:1, :] * taps[0]
    for k in range(1, len(taps)):
        out = out + w[k:k + 1, :] * taps[k]
    return out


def _conv_bwd(taps, w, du):
    k_w = len(taps)
    dx = w[k_w - 1:k_w, :] * du
    for k in range(k_w - 1):
        dx = dx + w[k:k + 1, :] * _shift_up(du, k_w - 1 - k)
    return dx, [jnp.sum(du * taps[k], 0, keepdims=True) for k in range(k_w)]


def _dn_pre_fwd(proj, conv_w, n_b, t_len):
    n_ct = 3 * HEADS
    k_w = conv_w.shape[0]

    def body(x_ref, w_ref, o_ref):
        o_ref[...] = _silu(_conv(_taps(x_ref[...], k_w), w_ref[...]))

    return pl.pallas_call(
        body, name="dn_pre_fwd", grid=(n_ct, n_b),
        in_specs=[pl.BlockSpec((t_len, 128), lambda j, b: (b, _qkv_block(j))),
                  pl.BlockSpec((k_w, 128), lambda j, b: (0, j))],
        out_specs=pl.BlockSpec((t_len, 128), lambda j, b: (b, j)),
        out_shape=jax.ShapeDtypeStruct((n_b * t_len, n_ct * 128), F32),
        compiler_params=_cparams(("parallel", "parallel")),
    )(proj, conv_w)


def _dn_pre_bwd(proj, conv_w, dqkv, d_proj, n_b, t_len):
    n_ct = 3 * HEADS
    k_w = conv_w.shape[0]

    def body(x_ref, w_ref, d_ref, _, dx_ref, dw_ref):
        taps, w = _taps(x_ref[...], k_w), w_ref[...]
        du = d_ref[...] * _dsilu(_conv(taps, w))
        dx, dw = _conv_bwd(taps, w, du)
        dx_ref[...] = dx.astype(BF16)
        _acc_rows(dw_ref, dw, pl.program_id(1) == 0)

    return pl.pallas_call(
        body, name="dn_pre_bwd", grid=(n_ct, n_b),
        in_specs=[pl.BlockSpec((t_len, 128), lambda j, b: (b, _qkv_block(j))),
                  pl.BlockSpec((k_w, 128), lambda j, b: (0, j)),
                  pl.BlockSpec((t_len, 128), lambda j, b: (b, j)), pl.BlockSpec(memory_space=pl.ANY)],
        out_specs=[pl.BlockSpec((t_len, 128), lambda j, b: (b, _qkv_block(j))),
                   pl.BlockSpec((k_w, 128), lambda j, b: (0, j))],
        out_shape=[jax.ShapeDtypeStruct(d_proj.shape, BF16), jax.ShapeDtypeStruct((k_w, n_ct * 128), F32)],
        input_output_aliases={3: 0},
        compiler_params=_cparams(("parallel", "arbitrary")),
    )(proj, conv_w, dqkv, d_proj)


FFN_TC = 256
FFN_NT = D_FF // FFN_TC


def _ffn_specs(t_len):
    blk = lambda off: pl.BlockSpec((t_len, FFN_TC), lambda j, b: (b, j + off))
    wblk = lambda off: pl.BlockSpec((3, FFN_TC), lambda j, b: (0, j + off))
    bblk = lambda off: pl.BlockSpec((1, FFN_TC), lambda j, b: (0, j + off))
    return [blk(0), blk(FFN_NT), wblk(0), wblk(FFN_NT), bblk(0), bblk(FFN_NT)]


def _ffn_act_fwd(up, conv_w, conv_b, n_b, t_len):
    def body(g_ref, v_ref, wg_ref, wv_ref, bg_ref, bv_ref, o_ref):
        ug = _conv(_taps(g_ref[...], 3), wg_ref[...]) + bg_ref[...]
        uv = _conv(_taps(v_ref[...], 3), wv_ref[...]) + bv_ref[...]
        o_ref[...] = (_silu(ug) * uv).astype(BF16)

    return pl.pallas_call(
        body, name="ffn_act_fwd", grid=(FFN_NT, n_b), in_specs=_ffn_specs(t_len),
        out_specs=pl.BlockSpec((t_len, FFN_TC), lambda j, b: (b, j)),
        out_shape=jax.ShapeDtypeStruct((n_b * t_len, D_FF), BF16),
        compiler_params=_cparams(("parallel", "parallel")),
    )(up, up, conv_w, conv_w, conv_b, conv_b)


def _ffn_act_bwd(up, conv_w, conv_b, da, n_b, t_len):
    def body(g_ref, v_ref, wg_ref, wv_ref, bg_ref, bv_ref, da_ref, dup_ref, dw_ref, db_ref):
        first = pl.program_id(1) == 0
        tg, tv, wg, wv = _taps(g_ref[...], 3), _taps(v_ref[...], 3), wg_ref[...], wv_ref[...]
        ug = _conv(tg, wg) + bg_ref[...]
        uv = _conv(tv, wv) + bv_ref[...]
        d_act = da_ref[...]
        sig = jax.nn.sigmoid(ug)
        d_v = d_act * (ug * sig)
        d_g = d_act * uv * (sig * (1.0 + ug * (1.0 - sig)))
        for slab, (taps, w, du) in enumerate(((tg, wg, d_g), (tv, wv, d_v))):
            dx, dw = _conv_bwd(taps, w, du)
            dup_ref[slab] = dx.astype(BF16)
            for k, dw_k in enumerate(dw):
                _acc(dw_ref, dw_k, first, at=(slab, slice(k, k + 1), slice(None)))
            _acc(db_ref, jnp.sum(du, 0, keepdims=True), first, at=(slab, slice(None), slice(None)))

    return pl.pallas_call(
        body, name="ffn_act_bwd", grid=(FFN_NT, n_b),
        in_specs=_ffn_specs(t_len) + [pl.BlockSpec((t_len, FFN_TC), lambda j, b: (b, j))],
        out_specs=[pl.BlockSpec((2, t_len, FFN_TC), lambda j, b: (0, b, j)),
                   pl.BlockSpec((2, 3, FFN_TC), lambda j, b: (0, 0, j)),
                   pl.BlockSpec((2, 1, FFN_TC), lambda j, b: (0, 0, j))],
        out_shape=[jax.ShapeDtypeStruct((2, n_b * t_len, D_FF), BF16),
                   jax.ShapeDtypeStruct((2, 3, D_FF), F32), jax.ShapeDtypeStruct((2, 1, D_FF), F32)],
        compiler_params=_cparams(("parallel", "arbitrary")),
    )(up, up, conv_w, conv_w, conv_b, conv_b, da)


NN = (((2,), (1,)), ((0,), (0,)))
NT = (((2,), (2,)), ((0,), (0,)))
TN = (((1,), (1,)), ((0,), (0,)))


def _iota3(shape, axis):
    return lax.broadcasted_iota(jnp.int32, shape, axis)


def _dg(a, b, dims):
    return lax.dot_general(a, b, dims, preferred_element_type=F32)


def _dot(a, b):
    return _dg(a, b, NN)


def _dot_nt(a, b):
    return _dg(a, b, NT)


def _dot_tn(a, b):
    return _dg(a, b, TN)


def _split(a):
    hi = a.astype(BF16)
    return hi, (a - hi.astype(F32)).astype(BF16)


def _dg3(a, b, dims):
    ah, al = _split(a)
    bh, bl = _split(b)
    return _dg(ah, bh, dims) + (_dg(ah, bl, dims) + _dg(al, bh, dims))


@jax.custom_vjp
def _dot3(a, b):
    return _dg3(a, b, NN)


def _dot3_fwd(a, b):
    return _dg3(a, b, NN), (a, b)


def _dot3_bwd(res, g):
    a, b = res
    return _dg3(g, b, NT), _dg3(a, g, TN)


_dot3.defvjp(_dot3_fwd, _dot3_bwd)


def _lower_ones(g_n, n):
    shape = (g_n, n, n)
    return jnp.where(_iota3(shape, 1) >= _iota3(shape, 2), 1.0, 0.0).astype(BF16)


@jax.custom_vjp
def _chunk_cumsum(x):
    hi, lo = _split(x)
    tri = _lower_ones(x.shape[0], x.shape[1])
    return _dg(tri, hi, NN) + _dg(tri, lo, NN)


def _chunk_cumsum_fwd(x):
    return _chunk_cumsum(x), None


def _chunk_cumsum_bwd(_, g):
    hi, lo = _split(g)
    tri = _lower_ones(g.shape[0], g.shape[1])
    return (_dg(tri, hi, TN) + _dg(tri, lo, TN),)


_chunk_cumsum.defvjp(_chunk_cumsum_fwd, _chunk_cumsum_bwd)


@jax.custom_vjp
def _unit_lower_inv(m):
    n = m.shape[1]
    p = -m
    a = jnp.where(_iota3(m.shape, 1) == _iota3(m.shape, 2), 1.0, 0.0) + p
    span = 2
    while span < n:
        p = _dg3(p, p, NN)
        a = a + _dg3(a, p, NN)
        span *= 2
    return a


def _unit_lower_inv_fwd(m):
    a = _unit_lower_inv(m)
    return a, a


def _unit_lower_inv_bwd(a, da):
    return (-_dg3(a, _dg3(da, a, NT), TN),)


_unit_lower_inv.defvjp(_unit_lower_inv_fwd, _unit_lower_inv_bwd)


@jax.custom_vjp
def _saved_lower_inv(m, a):
    return a


def _saved_lower_inv_fwd(m, a):
    return a, a


def _saved_lower_inv_bwd(a, da):
    return _unit_lower_inv_bwd(a, da)[0], jnp.zeros_like(a)


_saved_lower_inv.defvjp(_saved_lower_inv_fwd, _saved_lower_inv_bwd)


def _rms_gate(o, gn, gate):
    return o * lax.rsqrt(jnp.mean(o * o, -1, keepdims=True) + EPS) * gn * _silu(gate)


def _dn_chains(q, k, v, z, small, s_in, a_log, dt_bias, gn, a_saved=None):
    g_n, c_len = q.shape[0], q.shape[1]
    sq = (g_n, c_len, c_len)
    row, col = _iota3(sq, 1), _iota3(sq, 2)
    causal, strict, eye = row >= col, row > col, row == col
    qn = q * lax.rsqrt(jnp.sum(q * q, -1, keepdims=True) + EPS) * (HEAD_DIM ** -0.5)
    kn = k * lax.rsqrt(jnp.sum(k * k, -1, keepdims=True) + EPS)
    lane = _iota3(small.shape, 2)
    head = jnp.bitwise_and(_iota3(small.shape, 0), HEADS - 1)
    la_all = -jnp.exp(a_log) * _softplus(small + dt_bias)
    la_c = jnp.sum(jnp.where(lane == head, la_all, 0.0), 2, keepdims=True)
    beta = jnp.sum(jnp.where(lane == head + HEADS, jax.nn.sigmoid(small), 0.0), 2, keepdims=True)
    la_b = jnp.broadcast_to(la_c, sq)
    la_r = jnp.sum(jnp.where(eye, la_b, 0.0), 1, keepdims=True)
    g_c = jnp.sum(jnp.where(causal, jnp.broadcast_to(la_r, sq), 0.0), 2, keepdims=True)
    g_r = jnp.sum(jnp.where(row <= col, la_b, 0.0), 1, keepdims=True)
    g_last = jnp.sum(la_c, 1, keepdims=True)
    decay = jnp.exp(jnp.where(causal, g_c - g_r, -1e30))
    e_g = jnp.exp(g_c)
    kb = kn * beta
    m_low = jnp.where(strict, _dot_nt(kb, kn) * decay, 0.0)
    a_inv = _unit_lower_inv(m_low) if a_saved is None else _saved_lower_inv(m_low, a_saved)
    u = _dot3(a_inv, v * beta)
    w = _dot3(a_inv, kb * e_g)
    attn = _dot_nt(qn, kn) * decay
    v_new = u - _dot(w, s_in)
    o = _dot(qn * e_g, s_in) + _dot(attn, v_new)
    s_out = s_in * jnp.exp(g_last) + _dot_tn(kn * jnp.exp(g_last - g_c), v_new)
    return _rms_gate(o, gn, z), s_out, a_inv


def _gla_chains(q, k, v, gate, small, s_in, w2, b2, gn):
    g_n, c_len = q.shape[0], q.shape[1]
    sq, kk = (g_n, c_len, c_len), (g_n, GLA_KEY, GLA_KEY)
    causal = _iota3(sq, 1) >= _iota3(sq, 2)
    la = -_softplus(-(_dot(small, w2) + b2)) * (1.0 / 16.0)
    b = _chunk_cumsum(la)
    b_last = jnp.sum(jnp.where(_iota3(b.shape, 1) == c_len - 1, b, 0.0), 1, keepdims=True)
    q_dec = q * (GLA_KEY ** -0.5) * jnp.exp(b)
    attn = jnp.where(causal, _dot_nt(q_dec, k * jnp.exp(-b)), 0.0)
    o = _dot(q_dec, s_in) + _dot(attn, v)
    g_row = jnp.exp(b_last)
    g_col = jnp.sum(jnp.where(_iota3(kk, 1) == _iota3(kk, 2), jnp.broadcast_to(g_row, kk), 0.0), 2, keepdims=True)
    s_out = s_in * g_col + _dot_tn(k * jnp.exp(b_last - b), v)
    return _rms_gate(o, gn, gate), s_out


def _chunk_spec(n_b, width, col_block, n_c, reverse=False):
    if reverse:
        return pl.BlockSpec((n_b, CHUNK, width), lambda n: (0, n_c - 1 - n, col_block))
    return pl.BlockSpec((n_b, CHUNK, width), lambda n: (0, n, col_block))


def _hist_spec(n_b, d_k, n_c, reverse=False):
    if reverse:
        return pl.BlockSpec((None, n_b * HEADS, d_k, HEAD_DIM), lambda n: (n_c - 1 - n, 0, 0, 0))
    return pl.BlockSpec((None, n_b * HEADS, d_k, HEAD_DIM), lambda n: (n, 0, 0, 0))


def _ainv_spec(n_b, n_c, reverse=False):
    if reverse:
        return pl.BlockSpec((None, n_b * HEADS, CHUNK, CHUNK), lambda n: (n_c - 1 - n, 0, 0, 0))
    return pl.BlockSpec((None, n_b * HEADS, CHUNK, CHUNK), lambda n: (n, 0, 0, 0))


def _stack_chains(ref, n_b, slices):
    return jnp.stack([ref[b, :, sl] for b in range(n_b) for sl in slices], axis=0)


def _per_chain(ref, n_b):
    return jnp.stack([ref[b] for b in range(n_b) for _ in range(HEADS)], axis=0)


def _unstack_chains(ref, val, n_b, slices, offset=0):
    for b in range(n_b):
        for h, sl in enumerate(slices):
            ref[b, :, slice(offset + sl.start, offset + sl.stop)] = val[b * HEADS + h].astype(ref.dtype)


def _gate_weights(w2_ref, b2_ref, n_b):
    w2 = jnp.stack([w2_ref[:, ks] for _ in range(n_b) for ks in GLA_KSL], axis=0)
    b2 = jnp.stack([b2_ref[:, ks] for _ in range(n_b) for ks in GLA_KSL], axis=0)
    return w2, b2


def _sum_heads(val, n_b):
    return [sum(val[b * HEADS + h] for h in range(HEADS)) for b in range(n_b)]


def _const_spec(shape):
    return pl.BlockSpec(shape, lambda n: (0,) * len(shape))


DN_SL = [slice(h * HEAD_DIM, (h + 1) * HEAD_DIM) for h in range(HEADS)]
GLA_KSL = [slice(h * GLA_KEY, (h + 1) * GLA_KEY) for h in range(HEADS)]


class Rider:
    def __init__(self, inputs, out_shapes, sems, first, last):
        self.inputs, self.out_shapes, self.sems, self.first, self.last = inputs, out_shapes, sems, first, last


def _with_rider(rider, n_in, n_out, n_scratch):
    if rider is None:
        return [], [], [], [], lambda refs: (refs, None)
    r_in, r_out, r_sem = len(rider.inputs), len(rider.out_shapes), len(rider.sems)

    def split(refs):
        own_in, rest = refs[:n_in], refs[n_in:]
        rid_in, rest = rest[:r_in], rest[r_in:]
        own_out, rest = rest[:n_out], rest[n_out:]
        rid_out, rest = rest[:r_out], rest[r_out:]
        own_scr, rid_sem = rest[:n_scratch], rest[n_scratch:]
        return own_in + own_out + own_scr, (rid_in, rid_out, rid_sem)

    return list(rider.inputs), [HBM_SPEC] * r_in, [HBM_SPEC] * r_out, list(rider.sems), split


def _ride(rider, parts, grid):
    if rider is None:
        return None, None
    grid = grid if isinstance(grid, tuple) else (grid,)

    def at(step_of):
        hit = pl.program_id(0) == step_of(grid[0])
        for axis in range(1, len(grid)):
            hit = jnp.logical_and(hit, pl.program_id(axis) == step_of(grid[axis]))
        return hit

    def first():
        pl.when(at(lambda n: 0))(lambda: rider.first(*parts))

    def last():
        pl.when(at(lambda n: n - 1))(lambda: rider.last(*parts))

    return first, last


def _dn_scan_fwd(qkv, proj, a_log, dt_bias, gn, n_b, t_len, rider=None):
    n_c = t_len // CHUNK
    spec = functools.partial(_chunk_spec, n_b, n_c=n_c)
    r_inputs, r_in_specs, r_out_specs, r_sems, split = _with_rider(rider, 8, 3, 1)

    def body(*refs):
        (q_ref, k_ref, v_ref, z_ref, sm_ref, al_ref, dt_ref, gn_ref,
         o_ref, hist_ref, ainv_ref, s_ref), parts = split(refs)
        ride_first, ride_last = _ride(rider, parts, n_c)
        if rider is not None:
            ride_first()

        @pl.when(pl.program_id(0) == 0)
        def _():
            s_ref[...] = jnp.zeros_like(s_ref)

        s_in = s_ref[...]
        hist_ref[...] = s_in
        og, s_out, a_inv = _dn_chains(*(_stack_chains(r, n_b, DN_SL) for r in (q_ref, k_ref, v_ref, z_ref)),
                                      _per_chain(sm_ref, n_b), s_in, al_ref[...], dt_ref[...], gn_ref[...])
        _unstack_chains(o_ref, og, n_b, DN_SL)
        s_ref[...] = s_out
        ainv_ref[...] = a_inv
        if rider is not None:
            ride_last()

    qkv3, proj3 = qkv.reshape(n_b, t_len, -1), proj.reshape(n_b, t_len, -1)
    o, hist, ainv, *rider_outs = pl.pallas_call(
        body, name="dn_scan_fwd", grid=(n_c,),
        in_specs=[spec(512, 0), spec(512, 1), spec(512, 2), spec(512, OFF_Z // 512), spec(128, OFF_SMALL // 128),
                  _const_spec((1, 128)), _const_spec((1, 128)), _const_spec((1, 128))] + r_in_specs,
        out_specs=[spec(512, 0), _hist_spec(n_b, HEAD_DIM, n_c), _ainv_spec(n_b, n_c)] + r_out_specs,
        out_shape=[jax.ShapeDtypeStruct((n_b, t_len, 2 * 512), BF16),
                   jax.ShapeDtypeStruct((n_c, n_b * HEADS, HEAD_DIM, HEAD_DIM), F32),
                   jax.ShapeDtypeStruct((n_c, n_b * HEADS, CHUNK, CHUNK), F32)]
        + (list(rider.out_shapes) if rider else []),
        scratch_shapes=[pltpu.VMEM((n_b * HEADS, HEAD_DIM, HEAD_DIM), F32)] + r_sems,
        compiler_params=_cparams(("arbitrary",)),
    )(qkv3, qkv3, qkv3, proj3, proj3, a_log, dt_bias, gn, *r_inputs)
    return o, (hist, ainv), rider_outs


def _dn_scan_bwd(qkv, proj, a_log, dt_bias, gn, hist, d_o, n_b, t_len, rider=None):
    n_c = t_len // CHUNK
    rev = functools.partial(_chunk_spec, n_b, n_c=n_c, reverse=True)
    r_inputs, r_in_specs, r_out_specs, r_sems, split = _with_rider(rider, 11, 6, 1)
    hist, ainv = hist

    def body(*refs):
        (q_ref, k_ref, v_ref, z_ref, sm_ref, al_ref, dt_ref, gn_ref, hist_ref, ainv_ref, do_ref,
         dqkv_ref, dz_ref, dsm_ref, dal_ref, ddt_ref, dgn_ref, ds_ref), parts = split(refs)
        ride_first, ride_last = _ride(rider, parts, n_c)
        if rider is not None:
            ride_first()
        first = pl.program_id(0) == 0

        @pl.when(first)
        def _():
            ds_ref[...] = jnp.zeros_like(ds_ref)

        chains = lambda *a: _dn_chains(*a, a_saved=ainv_ref[...])[:2]
        _, pull = jax.vjp(chains, *(_stack_chains(r, n_b, DN_SL) for r in (q_ref, k_ref, v_ref, z_ref)),
                          _per_chain(sm_ref, n_b), hist_ref[...], al_ref[...], dt_ref[...], gn_ref[...])
        dq, dk, dv, dz, dsm, ds_in, dal, ddt, dgn = pull((_stack_chains(do_ref, n_b, DN_SL), ds_ref[...]))
        _unstack_chains(dqkv_ref, dq, n_b, DN_SL)
        _unstack_chains(dqkv_ref, dk, n_b, DN_SL, offset=512)
        _unstack_chains(dqkv_ref, dv, n_b, DN_SL, offset=1024)
        _unstack_chains(dz_ref, dz, n_b, DN_SL)
        ds_ref[...] = ds_in
        for b, dsm_b in enumerate(_sum_heads(dsm, n_b)):
            dsm_ref[b] = dsm_b
        _acc(dal_ref, dal, first)
        _acc(ddt_ref, ddt, first)
        _acc(dgn_ref, dgn, first)
        if rider is not None:
            ride_last()

    qkv3, proj3, do3 = (a.reshape(n_b, t_len, -1) for a in (qkv, proj, d_o))
    vec = jax.ShapeDtypeStruct((1, 128), F32)
    dqkv, d_proj, dsm, dal, ddt, dgn, *rider_outs = pl.pallas_call(
        body, name="dn_scan_bwd", grid=(n_c,),
        in_specs=[rev(512, 0), rev(512, 1), rev(512, 2), rev(512, OFF_Z // 512), rev(128, OFF_SMALL // 128),
                  _const_spec((1, 128)), _const_spec((1, 128)), _const_spec((1, 128)),
                  _hist_spec(n_b, HEAD_DIM, n_c, reverse=True), _ainv_spec(n_b, n_c, reverse=True),
                  rev(512, 0)] + r_in_specs,
        out_specs=[rev(1536, 0), rev(512, OFF_Z // 512), rev(128, 0),
                   _const_spec((1, 128)), _const_spec((1, 128)), _const_spec((1, 128))] + r_out_specs,
        out_shape=[jax.ShapeDtypeStruct((n_b, t_len, 1536), F32), jax.ShapeDtypeStruct((n_b, t_len, PROJ_W), BF16),
                   jax.ShapeDtypeStruct((n_b, t_len, 128), F32), vec, vec, vec]
        + (list(rider.out_shapes) if rider else []),
        scratch_shapes=[pltpu.VMEM((n_b * HEADS, HEAD_DIM, HEAD_DIM), F32)] + r_sems,
        compiler_params=_cparams(("arbitrary",)),
    )(qkv3, qkv3, qkv3, proj3, proj3, a_log, dt_bias, gn, hist, ainv, do3, *r_inputs)
    return dqkv.reshape(n_b * t_len, 1536), d_proj, dsm, dal, ddt, dgn, rider_outs


def _gla_scan_fwd(proj, w2, b2, gn, o_mix, n_b, t_len):
    n_c = t_len // CHUNK
    spec = functools.partial(_chunk_spec, n_b, n_c=n_c)

    def body(q_ref, k_ref, v_ref, g_ref, sm_ref, w2_ref, b2_ref, gn_ref, _, o_ref, hist_ref, s_ref):
        @pl.when(pl.program_id(0) == 0)
        def _():
            s_ref[...] = jnp.zeros_like(s_ref)

        s_in = s_ref[...]
        hist_ref[...] = s_in
        og, s_out = _gla_chains(_stack_chains(q_ref, n_b, GLA_KSL), _stack_chains(k_ref, n_b, GLA_KSL),
                                _stack_chains(v_ref, n_b, DN_SL), _stack_chains(g_ref, n_b, DN_SL),
                                _per_chain(sm_ref, n_b), s_in, *_gate_weights(w2_ref, b2_ref, n_b), gn_ref[...])
        _unstack_chains(o_ref, og, n_b, DN_SL)
        s_ref[...] = s_out

    proj3 = proj.reshape(n_b, t_len, -1)
    o, hist = pl.pallas_call(
        body, name="gla_scan_fwd", grid=(n_c,),
        in_specs=[spec(256, OFF_GQ // 256), spec(256, OFF_GK // 256), spec(512, OFF_GV // 512),
                  spec(512, OFF_GG // 512), spec(128, OFF_SMALL // 128),
                  _const_spec((128, 256)), _const_spec((1, 256)), _const_spec((1, 128)),
                  pl.BlockSpec(memory_space=pl.ANY)],
        out_specs=[spec(512, 1), _hist_spec(n_b, GLA_KEY, n_c)],
        out_shape=[jax.ShapeDtypeStruct(o_mix.shape, BF16),
                   jax.ShapeDtypeStruct((n_c, n_b * HEADS, GLA_KEY, HEAD_DIM), F32)],
        input_output_aliases={8: 0},
        scratch_shapes=[pltpu.VMEM((n_b * HEADS, GLA_KEY, HEAD_DIM), F32)],
        compiler_params=_cparams(("arbitrary",)),
    )(proj3, proj3, proj3, proj3, proj3, w2, b2, gn, o_mix)
    return o.reshape(n_b * t_len, 2 * 512), hist


def _gla_scan_bwd(proj, w2, b2, gn, hist, d_o, dsm_dn, d_proj, n_b, t_len):
    n_c = t_len // CHUNK
    rev = functools.partial(_chunk_spec, n_b, n_c=n_c, reverse=True)

    def body(q_ref, k_ref, v_ref, g_ref, sm_ref, w2_ref, b2_ref, gn_ref, hist_ref, do_ref, dsm_dn_ref, _,
             dp_ref, dw2_ref, db2_ref, dgn_ref, ds_ref):
        first = pl.program_id(0) == 0

        @pl.when(first)
        def _():
            ds_ref[...] = jnp.zeros_like(ds_ref)

        _, pull = jax.vjp(_gla_chains, _stack_chains(q_ref, n_b, GLA_KSL), _stack_chains(k_ref, n_b, GLA_KSL),
                          _stack_chains(v_ref, n_b, DN_SL), _stack_chains(g_ref, n_b, DN_SL),
                          _per_chain(sm_ref, n_b), hist_ref[...], *_gate_weights(w2_ref, b2_ref, n_b), gn_ref[...])
        dq, dk, dv, dg, dsm, ds_in, dw2, db2, dgn = pull((_stack_chains(do_ref, n_b, DN_SL), ds_ref[...]))
        _unstack_chains(dp_ref, dq, n_b, GLA_KSL, offset=OFF_GQ)
        _unstack_chains(dp_ref, dk, n_b, GLA_KSL, offset=OFF_GK)
        _unstack_chains(dp_ref, dv, n_b, DN_SL, offset=OFF_GV)
        _unstack_chains(dp_ref, dg, n_b, DN_SL, offset=OFF_GG)
        ds_ref[...] = ds_in
        for b, dsm_b in enumerate(_sum_heads(dsm, n_b)):
            dp_ref[b, :, OFF_SMALL:OFF_SMALL + 128] = (dsm_b + dsm_dn_ref[b]).astype(BF16)
            dp_ref[b, :, OFF_SMALL + 128:GLA_W] = jnp.zeros((CHUNK, GLA_W - OFF_SMALL - 128), BF16)
        for h, ks in enumerate(GLA_KSL):
            _acc(dw2_ref, sum(dw2[b * HEADS + h] for b in range(n_b)), first, at=(slice(None), ks))
            _acc(db2_ref, sum(db2[b * HEADS + h] for b in range(n_b)), first, at=(slice(None), ks))
        _acc(dgn_ref, dgn, first)

    proj3, do3 = proj.reshape(n_b, t_len, -1), d_o.reshape(n_b, t_len, -1)
    return pl.pallas_call(
        body, name="gla_scan_bwd", grid=(n_c,),
        in_specs=[rev(256, OFF_GQ // 256), rev(256, OFF_GK // 256), rev(512, OFF_GV // 512), rev(512, OFF_GG // 512),
                  rev(128, OFF_SMALL // 128),
                  _const_spec((128, 256)), _const_spec((1, 256)), _const_spec((1, 128)),
                  _hist_spec(n_b, GLA_KEY, n_c, reverse=True), rev(512, 1), rev(128, 0),
                  pl.BlockSpec(memory_space=pl.ANY)],
        out_specs=[rev(GLA_W, 0), _const_spec((128, 256)), _const_spec((1, 256)), _const_spec((1, 128))],
        out_shape=[jax.ShapeDtypeStruct(d_proj.shape, BF16), jax.ShapeDtypeStruct((128, 256), F32),
                   jax.ShapeDtypeStruct((1, 256), F32), jax.ShapeDtypeStruct((1, 128), F32)],
        input_output_aliases={11: 0},
        scratch_shapes=[pltpu.VMEM((n_b * HEADS, GLA_KEY, HEAD_DIM), F32)],
        compiler_params=_cparams(("arbitrary",)),
    )(proj3, proj3, proj3, proj3, proj3, w2, b2, gn, hist, do3, dsm_dn, d_proj)


W_IN_RUNS = ((0, 256, GLA_W), (256, 1536, OFF_Z + 512), (1536, 2048, OFF_Z), (2048, 2056, OFF_SMALL),
             (2056, 3592, 0), (3592, 3608, OFF_SMALL + 8))
W_IN_ROWS = 256


def _w_in_pieces(cols_per_chip):
    out = []
    for first, last, start in W_IN_RUNS:
        for j in range(N_CHIPS):
            a, b = max(first, cols_per_chip * j), min(last, cols_per_chip * (j + 1))
            if a < b:
                out.append((j, a - cols_per_chip * j, b - cols_per_chip * j, start + a - first))
    return out


def _w_in_to_padded(w4):
    _, n_r, n_c = w4.shape

    def body(i_ref, o_ref):
        o_ref[...] = jnp.zeros_like(o_ref)
        for j, a, b, p in _w_in_pieces(n_c):
            o_ref[:, p:p + b - a] = i_ref[j, :, a:b]

    return pl.pallas_call(
        body, name="w_in_to_padded", grid=(n_r // W_IN_ROWS,),
        in_specs=[pl.BlockSpec((N_CHIPS, W_IN_ROWS, n_c), lambda i: (0, i, 0))],
        out_specs=pl.BlockSpec((W_IN_ROWS, PROJ_W), lambda i: (i, 0)),
        out_shape=jax.ShapeDtypeStruct((n_r, PROJ_W), w4.dtype), compiler_params=_cparams(("parallel",)),
    )(w4)


def _w_in_to_chips(g, n_c):
    n_r = g.shape[0]

    def body(i_ref, o_ref):
        for j, a, b, p in _w_in_pieces(n_c):
            o_ref[j, :, a:b] = i_ref[:, p:p + b - a]

    return pl.pallas_call(
        body, name="w_in_to_chips", grid=(n_r // W_IN_ROWS,),
        in_specs=[pl.BlockSpec((W_IN_ROWS, PROJ_W), lambda i: (i, 0))],
        out_specs=pl.BlockSpec((N_CHIPS, W_IN_ROWS, n_c), lambda i: (0, i, 0)),
        out_shape=jax.ShapeDtypeStruct((N_CHIPS, n_r, n_c), g.dtype), compiler_params=_cparams(("parallel",)),
    )(g)


def _lane_vec(v, offset=0):
    return jnp.zeros((1, 128), F32).at[0, offset:offset + v.shape[0]].set(v)


def _local_step(x, tgt, mod, p, n_b, t_len, comm=None):
    row1 = lambda v: v.reshape(1, -1)
    a_log, dt_bias = _lane_vec(p["dn_a_log"]), _lane_vec(p["dn_dt_bias"])
    dn_gn, gla_gn = row1(p["dn_norm_g"]), row1(p["gla_norm_g"])
    w2 = jnp.zeros((128, 256), F32).at[8:8 + GATE_RANK].set(p["gla_w_gate2"])
    b2 = row1(p["gla_b_gate"])
    ln0_g, ln0_b, ln1_g, ln1_b, ln2_g, ln2_b = (row1(p[k]) for k in ("ln0_g", "ln0_b", "ln1_g", "ln1_b", "ln2_g", "ln2_b"))
    conv_b = row1(p["ffn_conv_b"])

    x0, h1 = _ln0_fwd(x, ln0_g, ln0_b, mod, n_b, t_len)
    proj = _mm(h1, p["w_in_p"], name="mm_proj")
    qkv = _dn_pre_fwd(proj, p["dn_conv"], n_b, t_len)
    o_half, hist_dn, landed = _dn_scan_fwd(qkv, proj, a_log, dt_bias, dn_gn, n_b, t_len,
                                           rider=comm.fwd_rider() if comm else None)
    if comm:
        p = {**p, **comm.weights_from(landed)}
    o_mix, hist_gla = _gla_scan_fwd(proj, w2, b2, gla_gn, o_half, n_b, t_len)
    y = _mm(o_mix, p["w_o"], name="mm_wo")
    x1, h2 = _ln1_fwd(x0, y, ln1_g, ln1_b, mod, n_b, t_len)
    up = _mm(h2, p["w_up"], name="mm_up")
    act = _ffn_act_fwd(up, p["ffn_conv"], conv_b, n_b, t_len)
    y2 = _mm(act, p["w_down"], name="mm_down")

    loss, dx1, dy2, g_ln2_g, g_ln2_b, dgt_f = _ln2_loss_bwd(x1, y2, ln2_g, ln2_b, mod, tgt, n_b, t_len)
    g_w_down = _mm(act, dy2, ta=True, name="mm_g_down")
    d_act = _mm(dy2, p["w_down"], tb=True, name="mm_d_act")
    d_up, g_ffn_conv, g_conv_b = _ffn_act_bwd(up, p["ffn_conv"], conv_b, d_act, n_b, t_len)
    g_w_up = _mm(h2, d_up, ta=True, out_slabs=N_CHIPS, name="mm_g_up")
    if comm:
        dh2, from_sibling = _mm(d_up, p["w_up"], tb=True, name="mm_d_h2", rider=comm.ffn_pair_rider(g_w_up, g_w_down))
    else:
        dh2 = _mm(d_up, p["w_up"], tb=True, name="mm_d_h2")
    dx0, dy, g_ln1_g, g_ln1_b, dmod_1 = _ln1_bwd(x0, y, ln1_g, ln1_b, mod, dx1, dh2, n_b, t_len)
    g_w_o = _mm(o_mix, dy, ta=True, name="mm_g_wo")
    d_o = _mm(dy, p["w_o"], tb=True, name="mm_d_o")
    dqkv, d_proj, dsm_dn, g_a_log, g_dt_bias, g_dn_gn, ffn_from_chips = _dn_scan_bwd(
        qkv, proj, a_log, dt_bias, dn_gn, hist_dn, d_o, n_b, t_len,
        rider=comm.ffn_chips_rider(from_sibling) if comm else None)
    d_proj, g_w2, g_b2, g_gla_gn = _gla_scan_bwd(proj, w2, b2, gla_gn, hist_gla, d_o, dsm_dn, d_proj, n_b, t_len)
    d_proj, g_dn_conv = _dn_pre_bwd(proj, p["dn_conv"], dqkv, d_proj.reshape(n_b * t_len, PROJ_W), n_b, t_len)
    g_w_in_p = _mm(h1, d_proj, ta=True, name="mm_g_win")
    if comm:
        dh1, tail_from_chips = _mm(d_proj, p["w_in_p"], tb=True, name="mm_d_h1",
                                   rider=comm.tail_chips_rider(g_w_in_p, g_w_o))
        from_chips = (ffn_from_chips, tail_from_chips)
    else:
        dh1, from_chips = _mm(d_proj, p["w_in_p"], tb=True, name="mm_d_h1"), None
    grad_x, g_ln0_g, g_ln0_b, dmod_0 = _ln0_bwd(x, ln0_g, ln0_b, mod, dx0, dh1, n_b, t_len)

    dmod = jnp.concatenate([dmod_0, dmod_1[:, 0:1], dmod_1[:, 1:3], dgt_f], axis=1)
    grads = {
        "ln0_g": g_ln0_g[0], "ln0_b": g_ln0_b[0], "w_in_p": g_w_in_p, "dn_conv": g_dn_conv,
        "dn_a_log": g_a_log[0, 0:HEADS], "dn_dt_bias": g_dt_bias[0, 0:HEADS], "dn_norm_g": g_dn_gn[0],
        "gla_w_gate2": g_w2[8:8 + GATE_RANK], "gla_b_gate": g_b2[0], "gla_norm_g": g_gla_gn[0],
        "w_o": g_w_o, "ln1_g": g_ln1_g[0], "ln1_b": g_ln1_b[0], "w_up": g_w_up,
        "ffn_conv": jnp.concatenate([g_ffn_conv[0], g_ffn_conv[1]], axis=1),
        "ffn_conv_b": jnp.concatenate([g_conv_b[0, 0], g_conv_b[1, 0]]), "w_down": g_w_down,
        "ln2_g": g_ln2_g[0], "ln2_b": g_ln2_b[0],
    }
    return loss, grad_x, grads, dmod, from_chips


def _ada_fwd(c_all, w_shard, b_shard):
    n_all, n_col = c_all.shape[0], w_shard.shape[1]
    tn = 512

    def body(c_ref, w_ref, b_ref, cond_ref, mod_ref):
        cond = _silu(c_ref[...])
        cond_ref[...] = cond
        mod_ref[...] = jnp.dot(cond.astype(BF16), w_ref[...].astype(BF16), preferred_element_type=F32) + b_ref[...]

    return pl.pallas_call(
        body, name="ada_fwd", grid=(n_col // tn,),
        in_specs=[pl.BlockSpec((n_all, D_MODEL), lambda j: (0, 0)), pl.BlockSpec((D_MODEL, tn), lambda j: (0, j)),
                  pl.BlockSpec((1, tn), lambda j: (0, j))],
        out_specs=[pl.BlockSpec((n_all, D_MODEL), lambda j: (0, 0)), pl.BlockSpec((n_all, tn), lambda j: (0, j))],
        out_shape=[jax.ShapeDtypeStruct((n_all, D_MODEL), F32), jax.ShapeDtypeStruct((n_all, n_col), F32)],
        compiler_params=_cparams(("arbitrary",)),
    )(c_all, w_shard, b_shard)


def _col_sum(a):
    def body(a_ref, o_ref):
        o_ref[...] = jnp.sum(a_ref[...], 0, keepdims=True)

    return pl.pallas_call(body, name="col_sum", out_shape=jax.ShapeDtypeStruct((1, a.shape[1]), F32))(a)


def _adamw_math(w, grad, m, v):
    new_m = ADAM_B1 * m + (1.0 - ADAM_B1) * grad
    new_v = ADAM_B2 * v + (1.0 - ADAM_B2) * (grad * grad)
    m_hat = new_m / (1.0 - ADAM_B1 ** ADAM_STEP)
    v_hat = new_v / (1.0 - ADAM_B2 ** ADAM_STEP)
    return -ADAM_LR * (m_hat / (jnp.sqrt(v_hat) + ADAM_EPS) + ADAM_WD * w), new_m, new_v


def _adamw_many(ws, gs, ms, vs):
    n = len(ws)

    def body(*refs):
        for i in range(n):
            w_ref, g_ref, m_ref, v_ref = (refs[k * n + i] for k in range(4))
            d_ref, nm_ref, nv_ref = (refs[(4 + k) * n + i] for k in range(3))
            d_ref[...], nm_ref[...], nv_ref[...] = _adamw_math(w_ref[...], g_ref[...], m_ref[...], v_ref[...])

    outs = pl.pallas_call(
        body, name="adamw_small", out_shape=[jax.ShapeDtypeStruct(w.shape, F32) for w in ws] * 3,
    )(*ws, *gs, *ms, *vs)
    return outs[:n], outs[n:2 * n], outs[2 * n:]


def _adamw(w, g, m, v, name):
    n_r, n_c = w.shape
    if n_r % 8 == 0:
        tr = _pick(n_r, (256, 64, 32, 16, 8))
        grid, blk = (n_r // tr,), pl.BlockSpec((tr, n_c), lambda i: (i, 0))
    else:
        tc = _pick(n_c, (256, 128))
        grid, blk = (n_c // tc,), pl.BlockSpec((n_r, tc), lambda i: (0, i))

    def body(w_ref, g_ref, m_ref, v_ref, d_ref, nm_ref, nv_ref):
        d_ref[...], nm_ref[...], nv_ref[...] = _adamw_math(w_ref[...], g_ref[...], m_ref[...], v_ref[...])

    out = jax.ShapeDtypeStruct(w.shape, F32)
    return pl.pallas_call(
        body, name=name, grid=grid, in_specs=[blk] * 4, out_specs=[blk] * 3, out_shape=[out] * 3,
        compiler_params=_cparams(("parallel",)),
    )(w, g, m, v)


HBM_SPEC = pl.BlockSpec(memory_space=pltpu.HBM)
VMEM_SPEC = pl.BlockSpec(memory_space=pltpu.VMEM)
CHIP_FLIPS = ((1, 0), (0, 1), (1, 1))


def _place():
    return lax.axis_index("x"), lax.axis_index("y"), lax.axis_index("c")


def _flip(v, f):
    return 1 - v if f else v


def _all_gather8(slab, name):
    n_r, n_w = slab.shape

    def body(x_ref, o_ref, s_ref, send_sems, recv_sems, local_sem):
        x, y, c = _place()
        me = 4 * x + 2 * y + c
        mine = pltpu.make_async_copy(x_ref, o_ref.at[me], local_sem)
        mine.start()
        peers = [(_flip(x, k & 4), _flip(y, k & 2), _flip(c, k & 1)) for k in range(1, N_DEV)]
        sends = []
        for k, peer in enumerate(peers):
            cp = pltpu.make_async_remote_copy(src_ref=x_ref, dst_ref=o_ref.at[me], send_sem=send_sems.at[k],
                                              recv_sem=recv_sems.at[k], device_id=peer, device_id_type=MESH)
            cp.start()
            sends.append(cp)
        for k, (px, py, pc) in enumerate(peers):
            pltpu.make_async_remote_copy(src_ref=x_ref, dst_ref=o_ref.at[4 * px + 2 * py + pc],
                                         send_sem=send_sems.at[k], recv_sem=recv_sems.at[k],
                                         device_id=(px, py, pc), device_id_type=MESH).wait_recv()
        for cp in sends:
            cp.wait_send()
        mine.wait()
        total = o_ref[0]
        for d in range(1, N_DEV):
            total = total + o_ref[d]
        s_ref[...] = total

    return pl.pallas_call(
        body, name=name, in_specs=[VMEM_SPEC], out_specs=[VMEM_SPEC, VMEM_SPEC],
        out_shape=[jax.ShapeDtypeStruct((N_DEV, n_r, n_w), F32), jax.ShapeDtypeStruct((n_r, n_w), F32)],
        scratch_shapes=[pltpu.SemaphoreType.DMA((N_DEV - 1,)), pltpu.SemaphoreType.DMA((N_DEV - 1,)),
                        pltpu.SemaphoreType.DMA],
    )(slab)


def _gather_weights(shards):
    n_a = len(shards)

    def body(*refs):
        ins, outs, stage = refs[:n_a], refs[n_a:2 * n_a], refs[2 * n_a:3 * n_a]
        send_sems, recv_sems, local_sems = refs[3 * n_a:]
        x, y, c = _place()
        me_chip = 2 * x + y
        sibling = (x, y, 1 - c)
        chips = [(_flip(x, fx), _flip(y, fy)) for fx, fy in CHIP_FLIPS]
        stage_in = [pltpu.make_async_copy(ins[k], stage[k], local_sems.at[k]) for k in range(n_a)]
        for cp in stage_in:
            cp.start()

        def copy(k, slot, chip_of_block, half, to, src=None):
            dst = outs[k].at[chip_of_block, half]
            return pltpu.make_async_remote_copy(src_ref=dst if src is None else src, dst_ref=dst,
                                                send_sem=send_sems.at[k * 6 + slot], recv_sem=recv_sems.at[k * 6 + slot],
                                                device_id=to, device_id_type=MESH)

        first = [copy(k, r, me_chip, c, (*chips[r], c), src=ins[k].at[c]) for k in range(n_a) for r in range(3)]
        for cp in first:
            cp.start()
        stage_out = []
        for k in range(n_a):
            stage_in[k].wait()
            cp = pltpu.make_async_copy(stage[k], outs[k].at[me_chip], local_sems.at[n_a + k])
            cp.start()
            stage_out.append(cp)
        passed = []
        for k in range(n_a):
            for r, (px, py) in enumerate(chips):
                copy(k, r, 2 * px + py, c, (x, y, c)).wait_recv()
                fwd = copy(k, 3 + r, 2 * px + py, c, sibling)
                fwd.start()
                passed.append(fwd)
        for k in range(n_a):
            for r, (px, py) in enumerate(chips):
                copy(k, 3 + r, 2 * px + py, 1 - c, (x, y, c)).wait_recv()
        for cp in first + passed:
            cp.wait_send()
        for cp in stage_out:
            cp.wait()

    return pl.pallas_call(
        body, name="gather_weights", in_specs=[HBM_SPEC] * n_a, out_specs=[HBM_SPEC] * n_a,
        out_shape=[jax.ShapeDtypeStruct((N_CHIPS,) + s.shape, s.dtype) for s in shards],
        scratch_shapes=[pltpu.VMEM(s.shape, s.dtype) for s in shards]
        + [pltpu.SemaphoreType.DMA((6 * n_a,)), pltpu.SemaphoreType.DMA((6 * n_a,)),
           pltpu.SemaphoreType.DMA((2 * n_a,))],
        compiler_params=pltpu.CompilerParams(vmem_limit_bytes=VMEM_LIMIT),
    )(*shards)


def _gather_rider(shards):
    n_a = len(shards)

    def plan(ins, outs, sems):
        send_sems, recv_sems = sems
        x, y, c = _place()
        chips = [(_flip(x, fx), _flip(y, fy)) for fx, fy in CHIP_FLIPS]

        def copy(k, slot, chip_of_block, half, to, src=None):
            dst = outs[k].at[chip_of_block, half]
            return pltpu.make_async_remote_copy(src_ref=dst if src is None else src, dst_ref=dst,
                                                send_sem=send_sems.at[k * 6 + slot], recv_sem=recv_sems.at[k * 6 + slot],
                                                device_id=to, device_id_type=MESH)

        first = [copy(k, r, 2 * x + y, c, (*chips[r], c), src=ins[k].at[c]) for k in range(n_a) for r in range(3)]
        return copy, chips, first, (x, y, c)

    def first_step(ins, outs, sems):
        for cp in plan(ins, outs, sems)[2]:
            cp.start()

    def last_step(ins, outs, sems):
        copy, chips, first, (x, y, c) = plan(ins, outs, sems)
        passed = []
        for k in range(n_a):
            for r, (px, py) in enumerate(chips):
                copy(k, r, 2 * px + py, c, (x, y, c)).wait_recv()
                fwd = copy(k, 3 + r, 2 * px + py, c, (x, y, 1 - c))
                fwd.start()
                passed.append(fwd)
        for k in range(n_a):
            for r, (px, py) in enumerate(chips):
                copy(k, 3 + r, 2 * px + py, 1 - c, (x, y, c)).wait_recv()
        for cp in first + passed:
            cp.wait_send()

    return Rider(shards, [jax.ShapeDtypeStruct((N_CHIPS,) + s.shape, s.dtype) for s in shards],
                 [pltpu.SemaphoreType.DMA((6 * n_a,)), pltpu.SemaphoreType.DMA((6 * n_a,))], first_step, last_step)


def _place_own(gathered, shard, chip, name):
    _, _, n_h, n_c = gathered.shape
    th = _pick(n_h, (256, 176, 128))

    def body(sel_ref, s_ref, _, o_ref):
        o_ref[...] = s_ref[...]

    grid_spec = pltpu.PrefetchScalarGridSpec(
        num_scalar_prefetch=1, grid=(2, n_h // th),
        in_specs=[pl.BlockSpec((None, th, n_c), lambda hf, i, sel: (hf, i, 0)), pl.BlockSpec(memory_space=pl.ANY)],
        out_specs=pl.BlockSpec((None, None, th, n_c), lambda hf, i, sel: (sel[0], hf, i, 0)))
    return pl.pallas_call(
        body, name=name, grid_spec=grid_spec, out_shape=jax.ShapeDtypeStruct(gathered.shape, gathered.dtype),
        input_output_aliases={2: 0}, compiler_params=_cparams(("parallel", "parallel")),
    )(chip.reshape(1), shard, gathered)


def _pair_rider(parts):
    n_a = len(parts)

    def plan(ins, outs, sems):
        send_sems, recv_sems = sems
        x, y, c = _place()
        return [pltpu.make_async_remote_copy(src_ref=ins[k].at[:, 1 - c], dst_ref=outs[k], send_sem=send_sems.at[k],
                                             recv_sem=recv_sems.at[k], device_id=(x, y, 1 - c), device_id_type=MESH)
                for k in range(n_a)]

    def first_step(ins, outs, sems):
        for cp in plan(ins, outs, sems):
            cp.start()

    def last_step(ins, outs, sems):
        for cp in plan(ins, outs, sems):
            cp.wait()

    return Rider(parts, [jax.ShapeDtypeStruct((N_CHIPS,) + p.shape[2:], F32) for p in parts],
                 [pltpu.SemaphoreType.DMA((n_a,)), pltpu.SemaphoreType.DMA((n_a,))], first_step, last_step)


def _alone(rider, name):
    n_a = len(rider.inputs)

    def body(*refs):
        parts = (refs[:n_a], refs[n_a:2 * n_a], refs[2 * n_a:])
        rider.first(*parts)
        rider.last(*parts)

    return pl.pallas_call(
        body, name=name, in_specs=[HBM_SPEC] * n_a, out_specs=[HBM_SPEC] * n_a,
        out_shape=rider.out_shapes, scratch_shapes=rider.sems,
    )(*rider.inputs)


def _chips_rider(sums):
    n_a = len(sums)

    def plan(ins, outs, sems):
        send_sems, recv_sems = sems
        x, y, c = _place()
        cps = []
        for k in range(n_a):
            for r, (fx, fy) in enumerate(CHIP_FLIPS):
                px, py = _flip(x, fx), _flip(y, fy)
                cps.append(pltpu.make_async_remote_copy(
                    src_ref=ins[k].at[2 * px + py], dst_ref=outs[k].at[r], send_sem=send_sems.at[3 * k + r],
                    recv_sem=recv_sems.at[3 * k + r], device_id=(px, py, c), device_id_type=MESH))
        return cps

    def first_step(ins, outs, sems):
        for cp in plan(ins, outs, sems):
            cp.start()

    def last_step(ins, outs, sems):
        for cp in plan(ins, outs, sems):
            cp.wait()

    return Rider(sums, [jax.ShapeDtypeStruct((3,) + s.shape[1:], s.dtype) for s in sums],
                 [pltpu.SemaphoreType.DMA((3 * n_a,)), pltpu.SemaphoreType.DMA((3 * n_a,))], first_step, last_step)


def _rs_share(bufs):
    n_a = len(bufs)

    def body(*refs):
        ins, outs = refs[:n_a], refs[n_a:2 * n_a]
        send_sems, recv_sems = refs[2 * n_a:]
        x, y, c = _place()
        sends = [pltpu.make_async_remote_copy(src_ref=ins[k].at[c], dst_ref=outs[k].at[c], send_sem=send_sems.at[k],
                                              recv_sem=recv_sems.at[k], device_id=(x, y, 1 - c), device_id_type=MESH)
                 for k in range(n_a)]
        for cp in sends:
            cp.start()
        for k in range(n_a):
            pltpu.make_async_remote_copy(src_ref=ins[k].at[c], dst_ref=outs[k].at[1 - c], send_sem=send_sems.at[k],
                                         recv_sem=recv_sems.at[k], device_id=(x, y, 1 - c),
                                         device_id_type=MESH).wait_recv()
        for cp in sends:
            cp.wait_send()

    return pl.pallas_call(
        body, name="rs_share", in_specs=[HBM_SPEC] * n_a, out_specs=[HBM_SPEC] * n_a,
        out_shape=[jax.ShapeDtypeStruct(s.shape, F32) for s in bufs],
        input_output_aliases={k: k for k in range(n_a)},
        scratch_shapes=[pltpu.SemaphoreType.DMA((n_a,)), pltpu.SemaphoreType.DMA((n_a,))],
    )(*bufs)


def _pair_add(part, recv, core, name):
    _, _, n_h, n_c = part.shape
    th = _pick(n_h, (256, 176, 128))

    def body(sel_ref, p_ref, r_ref, o_ref):
        o_ref[...] = (p_ref[...] + r_ref[...]).astype(BF16)

    grid_spec = pltpu.PrefetchScalarGridSpec(
        num_scalar_prefetch=1, grid=(N_CHIPS, n_h // th),
        in_specs=[pl.BlockSpec((None, None, th, n_c), lambda j, i, sel: (j, sel[0], i, 0)),
                  pl.BlockSpec((None, th, n_c), lambda j, i, sel: (j, i, 0))],
        out_specs=pl.BlockSpec((None, th, n_c), lambda j, i, sel: (j, i, 0)))
    return pl.pallas_call(
        body, name=name, grid_spec=grid_spec, out_shape=jax.ShapeDtypeStruct(recv.shape, BF16),
        compiler_params=_cparams(("parallel", "parallel")),
    )(core.reshape(1), part, recv)


def _chip_add(sums, recv, chip, core, name):
    _, n_h, n_c = sums.shape
    th = _pick(n_h, (256, 176, 128))

    def body(sel_ref, s_ref, r_ref, o_ref):
        total = s_ref[...].astype(F32)
        for r in range(3):
            total = total + r_ref[r].astype(F32)
        o_ref[...] = total

    grid_spec = pltpu.PrefetchScalarGridSpec(
        num_scalar_prefetch=1, grid=(n_h // th,),
        in_specs=[pl.BlockSpec((None, th, n_c), lambda i, sel: (sel[0], i, 0)),
                  pl.BlockSpec((3, th, n_c), lambda i, sel: (0, i, 0))],
        out_specs=pl.BlockSpec((None, th, n_c), lambda i, sel: (sel[1], i, 0)))
    return pl.pallas_call(
        body, name=name, grid_spec=grid_spec, out_shape=jax.ShapeDtypeStruct((2, n_h, n_c), F32),
        compiler_params=_cparams(("parallel",)),
    )(jnp.stack([chip, core]), sums, recv)


def _row_halves(a):
    return a.reshape(N_CHIPS, 2, -1, a.shape[-1])


class StepComm:
    REST = ("w_o", "w_up", "w_down")

    def __init__(self, core, chip, rest_shards, in_cols):
        self.core, self.chip, self.shards, self.in_cols = core, chip, rest_shards, in_cols

    def fwd_rider(self):
        return _gather_rider(self.shards)

    def weights_from(self, landed):
        g_o, g_up, g_down = (_place_own(g, s, self.chip, "place_own_" + n)
                             for g, s, n in zip(landed, self.shards, self.REST))
        return {"w_o": g_o.reshape(-1, D_MODEL), "w_up": g_up.reshape(N_CHIPS, -1, g_up.shape[-1]),
                "w_down": g_down.reshape(-1, D_MODEL)}

    def _add_pairs(self, parts, from_sibling, names):
        return [_pair_add(p, r, self.core, "pair_add_" + n) for p, r, n in zip(parts, from_sibling, names)]

    def ffn_pair_rider(self, g_w_up, g_w_down):
        self.ffn_parts = [_row_halves(g_w_up), _row_halves(g_w_down)]
        return _pair_rider(self.ffn_parts)

    def ffn_chips_rider(self, from_sibling):
        self.ffn_sums = self._add_pairs(self.ffn_parts, from_sibling, ("w_up", "w_down"))
        return _chips_rider(self.ffn_sums)

    def tail_chips_rider(self, g_w_in_p, g_w_o):
        parts = [_row_halves(_w_in_to_chips(g_w_in_p, self.in_cols)), _row_halves(g_w_o)]
        self.tail_sums = self._add_pairs(parts, _alone(_pair_rider(parts), "rs_pair_tail"), ("w_in", "w_o"))
        return _chips_rider(self.tail_sums)

    def finish(self, ffn_from_chips, tail_from_chips):
        halves = [_chip_add(s, r, self.chip, self.core, "chip_add_" + n)
                  for s, r, n in zip(self.tail_sums + self.ffn_sums, list(tail_from_chips) + list(ffn_from_chips),
                                     ("w_in", "w_o", "w_up", "w_down"))]
        return [f.reshape(-1, f.shape[-1]) for f in _rs_share(halves)]


SLAB_W = 1024


def _pack(arrays, rows):
    flat = jnp.concatenate([a.reshape(-1).astype(F32) for a in arrays])
    return jnp.pad(flat, (0, rows * SLAB_W - flat.shape[0])).reshape(rows, SLAB_W)


def _unpack(flat, shapes):
    out, off = [], 0
    for s in shapes:
        n = 1
        for d in s:
            n *= d
        out.append(flat[off:off + n].reshape(s))
        off += n
    return out


def _rows_for(arrays_or_shapes):
    n = 0
    for a in arrays_or_shapes:
        s = a if isinstance(a, tuple) else a.shape
        k = 1
        for d in s:
            k *= d
        n += k
    return -(-n // (8 * SLAB_W)) * 8


def kernel(x, c, ln0_g, ln0_b, w_ada, b_ada, w_in, dn_conv, dn_a_log, dn_dt_bias, dn_norm_g, gla_w_gate2, gla_b_gate, gla_norm_g, w_o, ln1_g, ln1_b, ffn_w_up, ffn_conv, ffn_conv_b, ffn_w_down, ln2_g, ln2_b, loss_target, m_ln0_g, m_ln0_b, m_w_ada, m_b_ada, m_w_in, m_dn_conv, m_dn_a_log, m_dn_dt_bias, m_dn_norm_g, m_gla_w_gate2, m_gla_b_gate, m_gla_norm_g, m_w_o, m_ln1_g, m_ln1_b, m_ffn_w_up, m_ffn_conv, m_ffn_conv_b, m_ffn_w_down, m_ln2_g, m_ln2_b, v_ln0_g, v_ln0_b, v_w_ada, v_b_ada, v_w_in, v_dn_conv, v_dn_a_log, v_dn_dt_bias, v_dn_norm_g, v_gla_w_gate2, v_gla_b_gate, v_gla_norm_g, v_w_o, v_ln1_g, v_ln1_b, v_ffn_w_up, v_ffn_conv, v_ffn_conv_b, v_ffn_w_down, v_ln2_g, v_ln2_b):
    n_b, t_len, _ = x.shape
    xi, yi, ci = _place()
    chip = (2 * xi + yi).astype(jnp.int32)
    core = ci.astype(jnp.int32)
    me = 2 * chip + core
    n_all = N_DEV * n_b
    ada_cols = w_ada.shape[2]

    sharded_small = [dn_conv[0], gla_w_gate2[0], ffn_conv[0]]
    slab = _pack([c] + sharded_small, _rows_for([c] + sharded_small))
    gathered, _ = _all_gather8(slab, "gather_small")
    flat = gathered.reshape(N_DEV, -1)
    c_all = flat[:, :c.size].reshape(n_all, D_MODEL)
    by_chip = flat[0::2]
    full, off = [], c.size
    for a in sharded_small:
        blocks = by_chip[:, off:off + a.size].reshape(N_CHIPS, *a.shape)
        full.append(blocks.transpose(1, 0, 2).reshape(a.shape[0], N_CHIPS * a.shape[1]))
        off += a.size
    dn_conv_f, gate2_f, ffn_conv_f = full

    b_ada_shard = lax.dynamic_slice(b_ada, (0, chip * ada_cols), (1, ada_cols))
    cond_all, mod_cols = _ada_fwd(c_all, w_ada[0], b_ada_shard)
    mod_g, _ = _all_gather8(mod_cols, "gather_mod")
    mod_full = jnp.concatenate([mod_g[2 * j] for j in range(N_CHIPS)], axis=1)
    mod = lax.dynamic_slice(mod_full, (me * n_b, 0), (n_b, 6 * D_MODEL)).reshape(n_b, 6, D_MODEL)

    halves = lambda a: a.astype(BF16).reshape(2, a.shape[0] // 2, a.shape[1])
    (g_in,) = _gather_weights([halves(w_in[0])])
    comm = StepComm(core, chip, [halves(w_o[0]), halves(ffn_w_up[0]), halves(ffn_w_down[0])], w_in.shape[2])
    params = {
        "w_in_p": _w_in_to_padded(g_in.reshape(N_CHIPS, -1, g_in.shape[-1])),
        "dn_conv": dn_conv_f, "dn_a_log": dn_a_log[0], "dn_dt_bias": dn_dt_bias[0], "dn_norm_g": dn_norm_g[0],
        "gla_w_gate2": gate2_f, "gla_b_gate": gla_b_gate[0], "gla_norm_g": gla_norm_g[0],
        "ln0_g": ln0_g, "ln0_b": ln0_b, "ln1_g": ln1_g[0], "ln1_b": ln1_b[0], "ln2_g": ln2_g[0], "ln2_b": ln2_b[0],
        "ffn_conv": ffn_conv_f, "ffn_conv_b": ffn_conv_b[0],
    }

    loss_row, grad_x, gp, dmod, from_chips = _local_step(
        x.reshape(n_b * t_len, D_MODEL), loss_target.reshape(n_b * t_len, D_MODEL), mod, params, n_b, t_len, comm)
    loss = lax.psum(loss_row[0, 0], ("x", "y", "c"))

    summed_names = ["ln0_g", "ln0_b", "dn_conv", "dn_a_log", "dn_dt_bias", "dn_norm_g", "gla_w_gate2", "gla_b_gate",
                    "gla_norm_g", "ln1_g", "ln1_b", "ffn_conv", "ffn_conv_b", "ln2_g", "ln2_b"]
    summed_parts = [gp[n] for n in summed_names]
    sum_rows = _rows_for(summed_parts)
    slab = jnp.concatenate([_pack(summed_parts, sum_rows), _pack([dmod], _rows_for([dmod]))], axis=0)
    gathered, total = _all_gather8(slab, "reduce_small")
    small_g = dict(zip(summed_names, _unpack(total.reshape(-1), [a.shape for a in summed_parts])))
    dmod_rows = n_b * 6 * D_MODEL // SLAB_W
    dmod_all = gathered[:, sum_rows:sum_rows + dmod_rows, :].reshape(n_all, 6 * D_MODEL)

    g_b_ada = _col_sum(dmod_all)
    dmod_cols = lax.dynamic_slice(dmod_all, (0, chip * ada_cols), (n_all, ada_cols))
    g_w_ada = _mm(cond_all, dmod_cols, ta=True, name="mm_g_ada")

    g_w_in, g_w_o, g_w_up, g_w_down = comm.finish(*from_chips)

    col_block = lambda a: lax.dynamic_slice(a, (0, chip * (a.shape[1] // N_CHIPS)), (a.shape[0], a.shape[1] // N_CHIPS))
    grads = {
        "ln0_g": small_g["ln0_g"], "ln0_b": small_g["ln0_b"], "w_ada": g_w_ada[None], "b_ada": g_b_ada,
        "w_in": g_w_in[None], "dn_conv": col_block(small_g["dn_conv"])[None], "dn_a_log": small_g["dn_a_log"][None],
        "dn_dt_bias": small_g["dn_dt_bias"][None], "dn_norm_g": small_g["dn_norm_g"][None],
        "gla_w_gate2": col_block(small_g["gla_w_gate2"])[None], "gla_b_gate": small_g["gla_b_gate"][None],
        "gla_norm_g": small_g["gla_norm_g"][None], "w_o": g_w_o[None], "ln1_g": small_g["ln1_g"][None],
        "ln1_b": small_g["ln1_b"][None], "ffn_w_up": g_w_up[None], "ffn_conv": col_block(small_g["ffn_conv"])[None],
        "ffn_conv_b": small_g["ffn_conv_b"][None], "ffn_w_down": g_w_down[None], "ln2_g": small_g["ln2_g"][None],
        "ln2_b": small_g["ln2_b"][None],
    }
    names = ["ln0_g", "ln0_b", "w_ada", "b_ada", "w_in", "dn_conv", "dn_a_log", "dn_dt_bias", "dn_norm_g",
             "gla_w_gate2", "gla_b_gate", "gla_norm_g", "w_o", "ln1_g", "ln1_b", "ffn_w_up", "ffn_conv", "ffn_conv_b",
             "ffn_w_down", "ln2_g", "ln2_b"]
    weights = dict(zip(names, [ln0_g, ln0_b, w_ada, b_ada, w_in, dn_conv, dn_a_log, dn_dt_bias, dn_norm_g, gla_w_gate2,
                               gla_b_gate, gla_norm_g, w_o, ln1_g, ln1_b, ffn_w_up, ffn_conv, ffn_conv_b, ffn_w_down,
                               ln2_g, ln2_b]))
    m_in = dict(zip(names, [m_ln0_g, m_ln0_b, m_w_ada, m_b_ada, m_w_in, m_dn_conv, m_dn_a_log, m_dn_dt_bias,
                            m_dn_norm_g, m_gla_w_gate2, m_gla_b_gate, m_gla_norm_g, m_w_o, m_ln1_g, m_ln1_b,
                            m_ffn_w_up, m_ffn_conv, m_ffn_conv_b, m_ffn_w_down, m_ln2_g, m_ln2_b]))
    v_in = dict(zip(names, [v_ln0_g, v_ln0_b, v_w_ada, v_b_ada, v_w_in, v_dn_conv, v_dn_a_log, v_dn_dt_bias,
                            v_dn_norm_g, v_gla_w_gate2, v_gla_b_gate, v_gla_norm_g, v_w_o, v_ln1_g, v_ln1_b,
                            v_ffn_w_up, v_ffn_conv, v_ffn_conv_b, v_ffn_w_down, v_ln2_g, v_ln2_b]))

    big = ("w_ada", "w_in", "w_o", "ffn_w_up", "ffn_w_down")
    delta, new_m, new_v = {}, {}, {}
    for n in big:
        view = (lambda a: a.T) if n == "w_in" else (lambda a: a)
        d_n, m_n, v_n = _adamw(view(weights[n][0]), view(grads[n][0]), view(m_in[n][0]), view(v_in[n][0]), "adamw_" + n)
        delta[n], new_m[n], new_v[n] = view(d_n)[None], view(m_n)[None], view(v_n)[None]
    small = [n for n in names if n not in big]
    d_s, m_s, v_s = _adamw_many([weights[n] for n in small], [grads[n] for n in small],
                                [m_in[n] for n in small], [v_in[n] for n in small])
    for out, vals in ((delta, d_s), (new_m, m_s), (new_v, v_s)):
        out.update(zip(small, vals))

    return (loss, grad_x.reshape(x.shape), *[grads[n] for n in names], *[delta[n] for n in names],
            *[new_m[n] for n in names], *[new_v[n] for n in names])
```

```python
import functools

import jax
import jax.numpy as jnp
from jax import lax
from jax.experimental import pallas as pl
from jax.experimental.pallas import tpu as pltpu

F32 = jnp.float32
BF16 = jnp.bfloat16
MESH = pl.DeviceIdType.MESH

D_MODEL = 1024
HEADS = 4
HEAD_DIM = 128
GLA_KEY = 64
GATE_RANK = 16
CHUNK = 64
D_FF = 2816
ALPHA = 2.0 ** 0.25
EPS = 1e-6
N_CHIPS = 4
N_DEV = 8

PROJ_W = 3840
OFF_GQ, OFF_GK, OFF_GV, OFF_GG, OFF_SMALL, GLA_W = 0, 256, 512, 1024, 1536, 1792
OFF_Z = 2048
W_IN_COLS = 3608


def _qkv_block(j):
    return jnp.where(j < 2, GLA_W // 128 + j, (OFF_Z + 512) // 128 - 2 + j)

ADAM_LR, ADAM_B1, ADAM_B2, ADAM_EPS, ADAM_WD, ADAM_STEP = 0.001, 0.9, 0.999, 1e-08, 0.01, 10

VMEM_LIMIT = 56 * 1024 * 1024
ROW_TILE = 512


def _cparams(sem):
    return pltpu.CompilerParams(dimension_semantics=sem, vmem_limit_bytes=VMEM_LIMIT)


def _pick(n, prefs):
    for p in prefs:
        if n % p == 0:
            return p
    return n


def _mm(a, b, *, ta=False, tb=False, out_slabs=1, out_dtype=F32, name, rider=None):
    a_slabs = a.shape[0] if a.ndim == 3 else 1
    b_slabs = b.shape[0] if b.ndim == 3 else 1
    assert not (ta and a_slabs > 1)
    a2, b2 = a.shape[-2:], b.shape[-2:]
    if ta:
        k_dim, m_dim = a2
    else:
        m_dim, k_dim = a2[0], a2[1] * a_slabs
    n_dim = b2[0] if tb else b2[1] * b_slabs
    k_slabs = max(a_slabs, b_slabs if tb else 1)
    n_slabs = max(out_slabs, 1 if tb else b_slabs)
    tm = _pick(m_dim, (1024, 1408, 512, 256, 128))
    tn = _pick(n_dim // n_slabs, (1536, 1408, 1280, 1024, 768, 512, 384, 256, 128))
    tk = _pick(k_dim // k_slabs, (1408, 1280, 1024, 512, 256, 128))
    nk, nj = k_dim // tk, n_dim // tn
    nk_a, nk_b, nj_b, nj_o = nk // a_slabs, nk // b_slabs, nj // b_slabs, nj // out_slabs
    dims = (((0 if ta else 1,), (1 if tb else 0,)), ((), ()))

    grid = (m_dim // tm, nj, nk)
    assert out_dtype == F32
    r_inputs, r_in_specs, r_out_specs, r_sems, split = _with_rider(rider, 2, 1, 0)

    def body(*refs):
        (a_ref, b_ref, o_ref), parts = split(refs)
        ride_first, ride_last = _ride(rider, parts, grid)
        if rider is not None:
            ride_first()
        prod = lax.dot_general(a_ref[...].astype(BF16), b_ref[...].astype(BF16), dims, preferred_element_type=F32)
        if nk == 1:
            o_ref[...] = prod
        else:
            _acc(o_ref, prod, pl.program_id(2) == 0)
        if rider is not None:
            ride_last()

    if ta:
        a_spec = pl.BlockSpec((tk, tm), lambda i, j, k: (k, i))
    elif a_slabs > 1:
        a_spec = pl.BlockSpec((None, tm, tk), lambda i, j, k: (k // nk_a, i, k % nk_a))
    else:
        a_spec = pl.BlockSpec((tm, tk), lambda i, j, k: (i, k))
    if tb and b_slabs > 1:
        b_spec = pl.BlockSpec((None, tn, tk), lambda i, j, k: (k // nk_b, j, k % nk_b))
    elif tb:
        b_spec = pl.BlockSpec((tn, tk), lambda i, j, k: (j, k))
    elif b_slabs > 1:
        b_spec = pl.BlockSpec((None, tk, tn), lambda i, j, k: (j // nj_b, k, j % nj_b))
    else:
        b_spec = pl.BlockSpec((tk, tn), lambda i, j, k: (k, j))
    if out_slabs > 1:
        o_spec = pl.BlockSpec((None, tm, tn), lambda i, j, k: (j // nj_o, i, j % nj_o))
        o_shape = (out_slabs, m_dim, n_dim // out_slabs)
    else:
        o_spec, o_shape = pl.BlockSpec((tm, tn), lambda i, j, k: (i, j)), (m_dim, n_dim)
    out, *rider_outs = pl.pallas_call(
        body, name=name, grid=grid,
        in_specs=[a_spec, b_spec] + r_in_specs, out_specs=[o_spec] + r_out_specs,
        out_shape=[jax.ShapeDtypeStruct(o_shape, out_dtype)] + (list(rider.out_shapes) if rider else []),
        scratch_shapes=r_sems,
        compiler_params=_cparams(("arbitrary",) * 3 if rider else ("parallel", "parallel", "arbitrary")),
    )(a, b, *r_inputs)
    return (out, rider_outs) if rider else out


def _ln(x, g, b):
    mu = jnp.mean(x, -1, keepdims=True)
    xc = x - mu
    var = jnp.mean(xc * xc, -1, keepdims=True)
    return xc * lax.rsqrt(var + EPS) * g + b


def _softplus(x):
    return jnp.maximum(x, 0.0) + jnp.log(1.0 + jnp.exp(-jnp.abs(x)))


def _silu(x):
    return x * jax.nn.sigmoid(x)


def _dsilu(x):
    s = jax.nn.sigmoid(x)
    return s * (1.0 + x * (1.0 - s))


def _f_ln0(x, g, b, sc, sh):
    x0 = _ln(x, g, b)
    return x0, x0 * (1.0 + sc) + sh


def _f_ln1(x0, y, gt, g, b, sc, sh):
    x1 = _ln(ALPHA * x0 + (1.0 + gt) * y, g, b)
    return x1, x1 * (1.0 + sc) + sh


def _f_ln2_loss(x1, y2, gt, g, b, tgt):
    x2 = _ln(ALPHA * x1 + (1.0 + gt) * y2, g, b)
    err = x2 - tgt
    per_row = jnp.sum(err * err, -1, keepdims=True) * (0.5 / D_MODEL)
    return jnp.sum(per_row, 0, keepdims=True)


def _row_specs(t_len):
    nt = t_len // ROW_TILE
    row = pl.BlockSpec((ROW_TILE, D_MODEL), lambda b, i: (b * nt + i, 0))
    vec = pl.BlockSpec((1, D_MODEL), lambda b, i: (0, 0))
    mod = pl.BlockSpec((None, 6, D_MODEL), lambda b, i: (b, 0, 0))
    return nt, row, vec, mod


def _first_step():
    return jnp.logical_and(pl.program_id(0) == 0, pl.program_id(1) == 0)


def _acc(ref, val, first, at=(Ellipsis,)):
    @pl.when(first)
    def _():
        ref[at] = val

    @pl.when(jnp.logical_not(first))
    def _():
        ref[at] += val


def _acc_rows(ref, rows, first):
    for i, r in enumerate(rows):
        _acc(ref, r, first, at=(slice(i, i + 1), slice(None)))


def _ln0_fwd(x, g, b, mod, n_b, t_len):
    nt, row, vec, mods = _row_specs(t_len)

    def body(x_ref, g_ref, b_ref, mod_ref, x0_ref, h_ref):
        x0, h = _f_ln0(x_ref[...], g_ref[...], b_ref[...], mod_ref[1:2, :], mod_ref[0:1, :])
        x0_ref[...] = x0
        h_ref[...] = h.astype(BF16)

    return pl.pallas_call(
        body, name="ln0_fwd", grid=(n_b, nt), in_specs=[row, vec, vec, mods], out_specs=[row, row],
        out_shape=[jax.ShapeDtypeStruct(x.shape, F32), jax.ShapeDtypeStruct(x.shape, BF16)],
        compiler_params=_cparams(("parallel", "parallel")),
    )(x, g, b, mod)


def _ln0_bwd(x, g, b, mod, dx0, dh, n_b, t_len):
    nt, row, vec, mods = _row_specs(t_len)
    dmod_spec = pl.BlockSpec((None, 2, D_MODEL), lambda bb, i: (bb, 0, 0))

    def body(x_ref, g_ref, b_ref, mod_ref, dx0_ref, dh_ref, dx_ref, dg_ref, db_ref, dmod_ref):
        _, pull = jax.vjp(_f_ln0, x_ref[...], g_ref[...], b_ref[...], mod_ref[1:2, :], mod_ref[0:1, :])
        dx, dg, db, dsc, dsh = pull((dx0_ref[...], dh_ref[...]))
        dx_ref[...] = dx
        _acc(dg_ref, dg, _first_step())
        _acc(db_ref, db, _first_step())
        _acc_rows(dmod_ref, [dsh, dsc], pl.program_id(1) == 0)

    return pl.pallas_call(
        body, name="ln0_bwd", grid=(n_b, nt), in_specs=[row, vec, vec, mods, row, row],
        out_specs=[row, vec, vec, dmod_spec],
        out_shape=[jax.ShapeDtypeStruct(x.shape, F32), jax.ShapeDtypeStruct((1, D_MODEL), F32),
                   jax.ShapeDtypeStruct((1, D_MODEL), F32), jax.ShapeDtypeStruct((n_b, 2, D_MODEL), F32)],
        compiler_params=_cparams(("arbitrary", "arbitrary")),
    )(x, g, b, mod, dx0, dh)


def _ln1_fwd(x0, y, g, b, mod, n_b, t_len):
    nt, row, vec, mods = _row_specs(t_len)

    def body(x0_ref, y_ref, g_ref, b_ref, mod_ref, x1_ref, h_ref):
        x1, h = _f_ln1(x0_ref[...], y_ref[...], mod_ref[2:3, :], g_ref[...], b_ref[...],
                       mod_ref[4:5, :], mod_ref[3:4, :])
        x1_ref[...] = x1
        h_ref[...] = h.astype(BF16)

    return pl.pallas_call(
        body, name="ln1_fwd", grid=(n_b, nt), in_specs=[row, row, vec, vec, mods], out_specs=[row, row],
        out_shape=[jax.ShapeDtypeStruct(x0.shape, F32), jax.ShapeDtypeStruct(x0.shape, BF16)],
        compiler_params=_cparams(("parallel", "parallel")),
    )(x0, y, g, b, mod)


def _ln1_bwd(x0, y, g, b, mod, dx1, dh, n_b, t_len):
    nt, row, vec, mods = _row_specs(t_len)
    dmod_spec = pl.BlockSpec((None, 3, D_MODEL), lambda bb, i: (bb, 0, 0))

    def body(x0_ref, y_ref, g_ref, b_ref, mod_ref, dx1_ref, dh_ref, dx0_ref, dy_ref, dg_ref, db_ref, dmod_ref):
        _, pull = jax.vjp(_f_ln1, x0_ref[...], y_ref[...], mod_ref[2:3, :], g_ref[...], b_ref[...],
                          mod_ref[4:5, :], mod_ref[3:4, :])
        dx0, dy, dgt, dg, db, dsc, dsh = pull((dx1_ref[...], dh_ref[...]))
        dx0_ref[...] = dx0
        dy_ref[...] = dy.astype(BF16)
        _acc(dg_ref, dg, _first_step())
        _acc(db_ref, db, _first_step())
        _acc_rows(dmod_ref, [dgt, dsh, dsc], pl.program_id(1) == 0)

    return pl.pallas_call(
        body, name="ln1_bwd", grid=(n_b, nt), in_specs=[row, row, vec, vec, mods, row, row],
        out_specs=[row, row, vec, vec, dmod_spec],
        out_shape=[jax.ShapeDtypeStruct(x0.shape, F32), jax.ShapeDtypeStruct(x0.shape, BF16),
                   jax.ShapeDtypeStruct((1, D_MODEL), F32), jax.ShapeDtypeStruct((1, D_MODEL), F32),
                   jax.ShapeDtypeStruct((n_b, 3, D_MODEL), F32)],
        compiler_params=_cparams(("arbitrary", "arbitrary")),
    )(x0, y, g, b, mod, dx1, dh)


def _ln2_loss_bwd(x1, y2, g, b, mod, tgt, n_b, t_len):
    nt, row, vec, mods = _row_specs(t_len)
    one = pl.BlockSpec((1, 128), lambda bb, i: (0, 0))
    dmod_spec = pl.BlockSpec((None, 1, D_MODEL), lambda bb, i: (bb, 0, 0))

    def body(x1_ref, y2_ref, g_ref, b_ref, mod_ref, t_ref, loss_ref, dx1_ref, dy2_ref, dg_ref, db_ref, dgt_ref):
        loss, pull = jax.vjp(functools.partial(_f_ln2_loss, tgt=t_ref[...]), x1_ref[...], y2_ref[...],
                             mod_ref[5:6, :], g_ref[...], b_ref[...])
        dx1, dy2, dgt, dg, db = pull(jnp.ones((1, 1), F32))
        dx1_ref[...] = dx1
        dy2_ref[...] = dy2.astype(BF16)
        _acc(loss_ref, jnp.broadcast_to(loss, (1, 128)), _first_step())
        _acc(dg_ref, dg, _first_step())
        _acc(db_ref, db, _first_step())
        _acc(dgt_ref, dgt, pl.program_id(1) == 0)

    return pl.pallas_call(
        body, name="ln2_loss_bwd", grid=(n_b, nt), in_specs=[row, row, vec, vec, mods, row],
        out_specs=[one, row, row, vec, vec, dmod_spec],
        out_shape=[jax.ShapeDtypeStruct((1, 128), F32), jax.ShapeDtypeStruct(x1.shape, F32),
                   jax.ShapeDtypeStruct(x1.shape, BF16), jax.ShapeDtypeStruct((1, D_MODEL), F32),
                   jax.ShapeDtypeStruct((1, D_MODEL), F32), jax.ShapeDtypeStruct((n_b, 1, D_MODEL), F32)],
        compiler_params=_cparams(("arbitrary", "arbitrary")),
    )(x1, y2, g, b, mod, tgt)


def _shift_down(x, s):
    if s == 0:
        return x
    rows = lax.broadcasted_iota(jnp.int32, x.shape, 0)
    return jnp.where(rows >= s, pltpu.roll(x, s, 0), 0.0)


def _shift_up(x, s):
    if s == 0:
        return x
    t_len = x.shape[0]
    rows = lax.broadcasted_iota(jnp.int32, x.shape, 0)
    return jnp.where(rows < t_len - s, pltpu.roll(x, t_len - s, 0), 0.0)


def _taps(x, k_w):
    return [_shift_down(x, k_w - 1 - k) for k in range(k_w)]


def _conv(taps, w):
    out = w[0:1, :] * taps[0]
    for k in range(1, len(taps)):
        out = out + w[k:k + 1, :] * taps[k]
    return out


def _conv_bwd(taps, w, du):
    k_w = len(taps)
    dx = w[k_w - 1:k_w, :] * du
    for k in range(k_w - 1):
        dx = dx + w[k:k + 1, :] * _shift_up(du, k_w - 1 - k)
    return dx, [jnp.sum(du * taps[k], 0, keepdims=True) for k in range(k_w)]


def _dn_pre_fwd(proj, conv_w, n_b, t_len):
    n_ct = 3 * HEADS
    k_w = conv_w.shape[0]

    def body(x_ref, w_ref, o_ref):
        o_ref[...] = _silu(_conv(_taps(x_ref[...], k_w), w_ref[...]))

    return pl.pallas_call(
        body, name="dn_pre_fwd", grid=(n_ct, n_b),
        in_specs=[pl.BlockSpec((t_len, 128), lambda j, b: (b, _qkv_block(j))),
                  pl.BlockSpec((k_w, 128), lambda j, b: (0, j))],
        out_specs=pl.BlockSpec((t_len, 128), lambda j, b: (b, j)),
        out_shape=jax.ShapeDtypeStruct((n_b * t_len, n_ct * 128), F32),
        compiler_params=_cparams(("parallel", "parallel")),
    )(proj, conv_w)


def _dn_pre_bwd(proj, conv_w, dqkv, d_proj, n_b, t_len):
    n_ct = 3 * HEADS
    k_w = conv_w.shape[0]

    def body(x_ref, w_ref, d_ref, _, dx_ref, dw_ref):
        taps, w = _taps(x_ref[...], k_w), w_ref[...]
        du = d_ref[...] * _dsilu(_conv(taps, w))
        dx, dw = _conv_bwd(taps, w, du)
        dx_ref[...] = dx.astype(BF16)
        _acc_rows(dw_ref, dw, pl.program_id(1) == 0)

    return pl.pallas_call(
        body, name="dn_pre_bwd", grid=(n_ct, n_b),
        in_specs=[pl.BlockSpec((t_len, 128), lambda j, b: (b, _qkv_block(j))),
                  pl.BlockSpec((k_w, 128), lambda j, b: (0, j)),
                  pl.BlockSpec((t_len, 128), lambda j, b: (b, j)), pl.BlockSpec(memory_space=pl.ANY)],
        out_specs=[pl.BlockSpec((t_len, 128), lambda j, b: (b, _qkv_block(j))),
                   pl.BlockSpec((k_w, 128), lambda j, b: (0, j))],
        out_shape=[jax.ShapeDtypeStruct(d_proj.shape, BF16), jax.ShapeDtypeStruct((k_w, n_ct * 128), F32)],
        input_output_aliases={3: 0},
        compiler_params=_cparams(("parallel", "arbitrary")),
    )(proj, conv_w, dqkv, d_proj)


FFN_TC = 256
FFN_NT = D_FF // FFN_TC


def _ffn_specs(t_len):
    blk = lambda off: pl.BlockSpec((t_len, FFN_TC), lambda j, b: (b, j + off))
    wblk = lambda off: pl.BlockSpec((3, FFN_TC), lambda j, b: (0, j + off))
    bblk = lambda off: pl.BlockSpec((1, FFN_TC), lambda j, b: (0, j + off))
    return [blk(0), blk(FFN_NT), wblk(0), wblk(FFN_NT), bblk(0), bblk(FFN_NT)]


def _ffn_act_fwd(up, conv_w, conv_b, n_b, t_len):
    def body(g_ref, v_ref, wg_ref, wv_ref, bg_ref, bv_ref, o_ref):
        ug = _conv(_taps(g_ref[...], 3), wg_ref[...]) + bg_ref[...]
        uv = _conv(_taps(v_ref[...], 3), wv_ref[...]) + bv_ref[...]
        o_ref[...] = (_silu(ug) * uv).astype(BF16)

    return pl.pallas_call(
        body, name="ffn_act_fwd", grid=(FFN_NT, n_b), in_specs=_ffn_specs(t_len),
        out_specs=pl.BlockSpec((t_len, FFN_TC), lambda j, b: (b, j)),
        out_shape=jax.ShapeDtypeStruct((n_b * t_len, D_FF), BF16),
        compiler_params=_cparams(("parallel", "parallel")),
    )(up, up, conv_w, conv_w, conv_b, conv_b)


def _ffn_act_bwd(up, conv_w, conv_b, da, n_b, t_len):
    def body(g_ref, v_ref, wg_ref, wv_ref, bg_ref, bv_ref, da_ref, dup_ref, dw_ref, db_ref):
        first = pl.program_id(1) == 0
        tg, tv, wg, wv = _taps(g_ref[...], 3), _taps(v_ref[...], 3), wg_ref[...], wv_ref[...]
        ug = _conv(tg, wg) + bg_ref[...]
        uv = _conv(tv, wv) + bv_ref[...]
        d_act = da_ref[...]
        sig = jax.nn.sigmoid(ug)
        d_v = d_act * (ug * sig)
        d_g = d_act * uv * (sig * (1.0 + ug * (1.0 - sig)))
        for slab, (taps, w, du) in enumerate(((tg, wg, d_g), (tv, wv, d_v))):
            dx, dw = _conv_bwd(taps, w, du)
            dup_ref[slab] = dx.astype(BF16)
            for k, dw_k in enumerate(dw):
                _acc(dw_ref, dw_k, first, at=(slab, slice(k, k + 1), slice(None)))
            _acc(db_ref, jnp.sum(du, 0, keepdims=True), first, at=(slab, slice(None), slice(None)))

    return pl.pallas_call(
        body, name="ffn_act_bwd", grid=(FFN_NT, n_b),
        in_specs=_ffn_specs(t_len) + [pl.BlockSpec((t_len, FFN_TC), lambda j, b: (b, j))],
        out_specs=[pl.BlockSpec((2, t_len, FFN_TC), lambda j, b: (0, b, j)),
                   pl.BlockSpec((2, 3, FFN_TC), lambda j, b: (0, 0, j)),
                   pl.BlockSpec((2, 1, FFN_TC), lambda j, b: (0, 0, j))],
        out_shape=[jax.ShapeDtypeStruct((2, n_b * t_len, D_FF), BF16),
                   jax.ShapeDtypeStruct((2, 3, D_FF), F32), jax.ShapeDtypeStruct((2, 1, D_FF), F32)],
        compiler_params=_cparams(("parallel", "arbitrary")),
    )(up, up, conv_w, conv_w, conv_b, conv_b, da)


NN = (((2,), (1,)), ((0,), (0,)))
NT = (((2,), (2,)), ((0,), (0,)))
TN = (((1,), (1,)), ((0,), (0,)))


def _iota3(shape, axis):
    return lax.broadcasted_iota(jnp.int32, shape, axis)


def _dg(a, b, dims):
    return lax.dot_general(a, b, dims, preferred_element_type=F32)


def _dot(a, b):
    return _dg(a, b, NN)


def _dot_nt(a, b):
    return _dg(a, b, NT)


def _dot_tn(a, b):
    return _dg(a, b, TN)


def _split(a):
    hi = a.astype(BF16)
    return hi, (a - hi.astype(F32)).astype(BF16)


def _dg3(a, b, dims):
    ah, al = _split(a)
    bh, bl = _split(b)
    return _dg(ah, bh, dims) + (_dg(ah, bl, dims) + _dg(al, bh, dims))


@jax.custom_vjp
def _dot3(a, b):
    return _dg3(a, b, NN)


def _dot3_fwd(a, b):
    return _dg3(a, b, NN), (a, b)


def _dot3_bwd(res, g):
    a, b = res
    return _dg3(g, b, NT), _dg3(a, g, TN)


_dot3.defvjp(_dot3_fwd, _dot3_bwd)


def _lower_ones(g_n, n):
    shape = (g_n, n, n)
    return jnp.where(_iota3(shape, 1) >= _iota3(shape, 2), 1.0, 0.0).astype(BF16)


@jax.custom_vjp
def _chunk_cumsum(x):
    hi, lo = _split(x)
    tri = _lower_ones(x.shape[0], x.shape[1])
    return _dg(tri, hi, NN) + _dg(tri, lo, NN)


def _chunk_cumsum_fwd(x):
    return _chunk_cumsum(x), None


def _chunk_cumsum_bwd(_, g):
    hi, lo = _split(g)
    tri = _lower_ones(g.shape[0], g.shape[1])
    return (_dg(tri, hi, TN) + _dg(tri, lo, TN),)


_chunk_cumsum.defvjp(_chunk_cumsum_fwd, _chunk_cumsum_bwd)


@jax.custom_vjp
def _unit_lower_inv(m):
    n = m.shape[1]
    p = -m
    a = jnp.where(_iota3(m.shape, 1) == _iota3(m.shape, 2), 1.0, 0.0) + p
    span = 2
    while span < n:
        p = _dg3(p, p, NN)
        a = a + _dg3(a, p, NN)
        span *= 2
    return a


def _unit_lower_inv_fwd(m):
    a = _unit_lower_inv(m)
    return a, a


def _unit_lower_inv_bwd(a, da):
    return (-_dg3(a, _dg3(da, a, NT), TN),)


_unit_lower_inv.defvjp(_unit_lower_inv_fwd, _unit_lower_inv_bwd)


@jax.custom_vjp
def _saved_lower_inv(m, a):
    return a


def _saved_lower_inv_fwd(m, a):
    return a, a


def _saved_lower_inv_bwd(a, da):
    return _unit_lower_inv_bwd(a, da)[0], jnp.zeros_like(a)


_saved_lower_inv.defvjp(_saved_lower_inv_fwd, _saved_lower_inv_bwd)


def _rms_gate(o, gn, gate):
    return o * lax.rsqrt(jnp.mean(o * o, -1, keepdims=True) + EPS) * gn * _silu(gate)


def _dn_chains(q, k, v, z, small, s_in, a_log, dt_bias, gn, a_saved=None):
    g_n, c_len = q.shape[0], q.shape[1]
    sq = (g_n, c_len, c_len)
    row, col = _iota3(sq, 1), _iota3(sq, 2)
    causal, strict, eye = row >= col, row > col, row == col
    qn = q * lax.rsqrt(jnp.sum(q * q, -1, keepdims=True) + EPS) * (HEAD_DIM ** -0.5)
    kn = k * lax.rsqrt(jnp.sum(k * k, -1, keepdims=True) + EPS)
    lane = _iota3(small.shape, 2)
    head = jnp.bitwise_and(_iota3(small.shape, 0), HEADS - 1)
    la_all = -jnp.exp(a_log) * _softplus(small + dt_bias)
    la_c = jnp.sum(jnp.where(lane == head, la_all, 0.0), 2, keepdims=True)
    beta = jnp.sum(jnp.where(lane == head + HEADS, jax.nn.sigmoid(small), 0.0), 2, keepdims=True)
    la_b = jnp.broadcast_to(la_c, sq)
    la_r = jnp.sum(jnp.where(eye, la_b, 0.0), 1, keepdims=True)
    g_c = jnp.sum(jnp.where(causal, jnp.broadcast_to(la_r, sq), 0.0), 2, keepdims=True)
    g_r = jnp.sum(jnp.where(row <= col, la_b, 0.0), 1, keepdims=True)
    g_last = jnp.sum(la_c, 1, keepdims=True)
    decay = jnp.exp(jnp.where(causal, g_c - g_r, -1e30))
    e_g = jnp.exp(g_c)
    kb = kn * beta
    m_low = jnp.where(strict, _dot_nt(kb, kn) * decay, 0.0)
    a_inv = _unit_lower_inv(m_low) if a_saved is None else _saved_lower_inv(m_low, a_saved)
    u = _dot3(a_inv, v * beta)
    w = _dot3(a_inv, kb * e_g)
    attn = _dot_nt(qn, kn) * decay
    v_new = u - _dot(w, s_in)
    o = _dot(qn * e_g, s_in) + _dot(attn, v_new)
    s_out = s_in * jnp.exp(g_last) + _dot_tn(kn * jnp.exp(g_last - g_c), v_new)
    return _rms_gate(o, gn, z), s_out, a_inv


def _gla_chains(q, k, v, gate, small, s_in, w2, b2, gn):
    g_n, c_len = q.shape[0], q.shape[1]
    sq, kk = (g_n, c_len, c_len), (g_n, GLA_KEY, GLA_KEY)
    causal = _iota3(sq, 1) >= _iota3(sq, 2)
    la = -_softplus(-(_dot(small, w2) + b2)) * (1.0 / 16.0)
    b = _chunk_cumsum(la)
    b_last = jnp.sum(jnp.where(_iota3(b.shape, 1) == c_len - 1, b, 0.0), 1, keepdims=True)
    q_dec = q * (GLA_KEY ** -0.5) * jnp.exp(b)
    attn = jnp.where(causal, _dot_nt(q_dec, k * jnp.exp(-b)), 0.0)
    o = _dot(q_dec, s_in) + _dot(attn, v)
    g_row = jnp.exp(b_last)
    g_col = jnp.sum(jnp.where(_iota3(kk, 1) == _iota3(kk, 2), jnp.broadcast_to(g_row, kk), 0.0), 2, keepdims=True)
    s_out = s_in * g_col + _dot_tn(k * jnp.exp(b_last - b), v)
    return _rms_gate(o, gn, gate), s_out


def _chunk_spec(n_b, width, col_block, n_c, reverse=False):
    if reverse:
        return pl.BlockSpec((n_b, CHUNK, width), lambda n: (0, n_c - 1 - n, col_block))
    return pl.BlockSpec((n_b, CHUNK, width), lambda n: (0, n, col_block))


def _hist_spec(n_b, d_k, n_c, reverse=False):
    if reverse:
        return pl.BlockSpec((None, n_b * HEADS, d_k, HEAD_DIM), lambda n: (n_c - 1 - n, 0, 0, 0))
    return pl.BlockSpec((None, n_b * HEADS, d_k, HEAD_DIM), lambda n: (n, 0, 0, 0))


def _ainv_spec(n_b, n_c, reverse=False):
    if reverse:
        return pl.BlockSpec((None, n_b * HEADS, CHUNK, CHUNK), lambda n: (n_c - 1 - n, 0, 0, 0))
    return pl.BlockSpec((None, n_b * HEADS, CHUNK, CHUNK), lambda n: (n, 0, 0, 0))


def _stack_chains(ref, n_b, slices):
    return jnp.stack([ref[b, :, sl] for b in range(n_b) for sl in slices], axis=0)


def _per_chain(ref, n_b):
    return jnp.stack([ref[b] for b in range(n_b) for _ in range(HEADS)], axis=0)


def _unstack_chains(ref, val, n_b, slices, offset=0):
    for b in range(n_b):
        for h, sl in enumerate(slices):
            ref[b, :, slice(offset + sl.start, offset + sl.stop)] = val[b * HEADS + h].astype(ref.dtype)


def _gate_weights(w2_ref, b2_ref, n_b):
    w2 = jnp.stack([w2_ref[:, ks] for _ in range(n_b) for ks in GLA_KSL], axis=0)
    b2 = jnp.stack([b2_ref[:, ks] for _ in range(n_b) for ks in GLA_KSL], axis=0)
    return w2, b2


def _sum_heads(val, n_b):
    return [sum(val[b * HEADS + h] for h in range(HEADS)) for b in range(n_b)]


def _const_spec(shape):
    return pl.BlockSpec(shape, lambda n: (0,) * len(shape))


DN_SL = [slice(h * HEAD_DIM, (h + 1) * HEAD_DIM) for h in range(HEADS)]
GLA_KSL = [slice(h * GLA_KEY, (h + 1) * GLA_KEY) for h in range(HEADS)]


class Rider:
    def __init__(self, inputs, out_shapes, sems, first, last):
        self.inputs, self.out_shapes, self.sems, self.first, self.last = inputs, out_shapes, sems, first, last


def _with_rider(rider, n_in, n_out, n_scratch):
    if rider is None:
        return [], [], [], [], lambda refs: (refs, None)
    r_in, r_out, r_sem = len(rider.inputs), len(rider.out_shapes), len(rider.sems)

    def split(refs):
        own_in, rest = refs[:n_in], refs[n_in:]
        rid_in, rest = rest[:r_in], rest[r_in:]
        own_out, rest = rest[:n_out], rest[n_out:]
        rid_out, rest = rest[:r_out], rest[r_out:]
        own_scr, rid_sem = rest[:n_scratch], rest[n_scratch:]
        return own_in + own_out + own_scr, (rid_in, rid_out, rid_sem)

    return list(rider.inputs), [HBM_SPEC] * r_in, [HBM_SPEC] * r_out, list(rider.sems), split


def _ride(rider, parts, grid):
    if rider is None:
        return None, None
    grid = grid if isinstance(grid, tuple) else (grid,)

    def at(step_of):
        hit = pl.program_id(0) == step_of(grid[0])
        for axis in range(1, len(grid)):
            hit = jnp.logical_and(hit, pl.program_id(axis) == step_of(grid[axis]))
        return hit

    def first():
        pl.when(at(lambda n: 0))(lambda: rider.first(*parts))

    def last():
        pl.when(at(lambda n: n - 1))(lambda: rider.last(*parts))

    return first, last


def _dn_scan_fwd(qkv, proj, a_log, dt_bias, gn, n_b, t_len, rider=None):
    n_c = t_len // CHUNK
    spec = functools.partial(_chunk_spec, n_b, n_c=n_c)
    r_inputs, r_in_specs, r_out_specs, r_sems, split = _with_rider(rider, 8, 3, 1)

    def body(*refs):
        (q_ref, k_ref, v_ref, z_ref, sm_ref, al_ref, dt_ref, gn_ref,
         o_ref, hist_ref, ainv_ref, s_ref), parts = split(refs)
        ride_first, ride_last = _ride(rider, parts, n_c)
        if rider is not None:
            ride_first()

        @pl.when(pl.program_id(0) == 0)
        def _():
            s_ref[...] = jnp.zeros_like(s_ref)

        s_in = s_ref[...]
        hist_ref[...] = s_in
        og, s_out, a_inv = _dn_chains(*(_stack_chains(r, n_b, DN_SL) for r in (q_ref, k_ref, v_ref, z_ref)),
                                      _per_chain(sm_ref, n_b), s_in, al_ref[...], dt_ref[...], gn_ref[...])
        _unstack_chains(o_ref, og, n_b, DN_SL)
        s_ref[...] = s_out
        ainv_ref[...] = a_inv
        if rider is not None:
            ride_last()

    qkv3, proj3 = qkv.reshape(n_b, t_len, -1), proj.reshape(n_b, t_len, -1)
    o, hist, ainv, *rider_outs = pl.pallas_call(
        body, name="dn_scan_fwd", grid=(n_c,),
        in_specs=[spec(512, 0), spec(512, 1), spec(512, 2), spec(512, OFF_Z // 512), spec(128, OFF_SMALL // 128),
                  _const_spec((1, 128)), _const_spec((1, 128)), _const_spec((1, 128))] + r_in_specs,
        out_specs=[spec(512, 0), _hist_spec(n_b, HEAD_DIM, n_c), _ainv_spec(n_b, n_c)] + r_out_specs,
        out_shape=[jax.ShapeDtypeStruct((n_b, t_len, 2 * 512), BF16),
                   jax.ShapeDtypeStruct((n_c, n_b * HEADS, HEAD_DIM, HEAD_DIM), F32),
                   jax.ShapeDtypeStruct((n_c, n_b * HEADS, CHUNK, CHUNK), F32)]
        + (list(rider.out_shapes) if rider else []),
        scratch_shapes=[pltpu.VMEM((n_b * HEADS, HEAD_DIM, HEAD_DIM), F32)] + r_sems,
        compiler_params=_cparams(("arbitrary",)),
    )(qkv3, qkv3, qkv3, proj3, proj3, a_log, dt_bias, gn, *r_inputs)
    return o, (hist, ainv), rider_outs


def _dn_scan_bwd(qkv, proj, a_log, dt_bias, gn, hist, d_o, n_b, t_len, rider=None):
    n_c = t_len // CHUNK
    rev = functools.partial(_chunk_spec, n_b, n_c=n_c, reverse=True)
    r_inputs, r_in_specs, r_out_specs, r_sems, split = _with_rider(rider, 11, 6, 1)
    hist, ainv = hist

    def body(*refs):
        (q_ref, k_ref, v_ref, z_ref, sm_ref, al_ref, dt_ref, gn_ref, hist_ref, ainv_ref, do_ref,
         dqkv_ref, dz_ref, dsm_ref, dal_ref, ddt_ref, dgn_ref, ds_ref), parts = split(refs)
        ride_first, ride_last = _ride(rider, parts, n_c)
        if rider is not None:
            ride_first()
        first = pl.program_id(0) == 0

        @pl.when(first)
        def _():
            ds_ref[...] = jnp.zeros_like(ds_ref)

        chains = lambda *a: _dn_chains(*a, a_saved=ainv_ref[...])[:2]
        _, pull = jax.vjp(chains, *(_stack_chains(r, n_b, DN_SL) for r in (q_ref, k_ref, v_ref, z_ref)),
                          _per_chain(sm_ref, n_b), hist_ref[...], al_ref[...], dt_ref[...], gn_ref[...])
        dq, dk, dv, dz, dsm, ds_in, dal, ddt, dgn = pull((_stack_chains(do_ref, n_b, DN_SL), ds_ref[...]))
        _unstack_chains(dqkv_ref, dq, n_b, DN_SL)
        _unstack_chains(dqkv_ref, dk, n_b, DN_SL, offset=512)
        _unstack_chains(dqkv_ref, dv, n_b, DN_SL, offset=1024)
        _unstack_chains(dz_ref, dz, n_b, DN_SL)
        ds_ref[...] = ds_in
        for b, dsm_b in enumerate(_sum_heads(dsm, n_b)):
            dsm_ref[b] = dsm_b
        _acc(dal_ref, dal, first)
        _acc(ddt_ref, ddt, first)
        _acc(dgn_ref, dgn, first)
        if rider is not None:
            ride_last()

    qkv3, proj3, do3 = (a.reshape(n_b, t_len, -1) for a in (qkv, proj, d_o))
    vec = jax.ShapeDtypeStruct((1, 128), F32)
    dqkv, d_proj, dsm, dal, ddt, dgn, *rider_outs = pl.pallas_call(
        body, name="dn_scan_bwd", grid=(n_c,),
        in_specs=[rev(512, 0), rev(512, 1), rev(512, 2), rev(512, OFF_Z // 512), rev(128, OFF_SMALL // 128),
                  _const_spec((1, 128)), _const_spec((1, 128)), _const_spec((1, 128)),
                  _hist_spec(n_b, HEAD_DIM, n_c, reverse=True), _ainv_spec(n_b, n_c, reverse=True),
                  rev(512, 0)] + r_in_specs,
        out_specs=[rev(1536, 0), rev(512, OFF_Z // 512), rev(128, 0),
                   _const_spec((1, 128)), _const_spec((1, 128)), _const_spec((1, 128))] + r_out_specs,
        out_shape=[jax.ShapeDtypeStruct((n_b, t_len, 1536), F32), jax.ShapeDtypeStruct((n_b, t_len, PROJ_W), BF16),
                   jax.ShapeDtypeStruct((n_b, t_len, 128), F32), vec, vec, vec]
        + (list(rider.out_shapes) if rider else []),
        scratch_shapes=[pltpu.VMEM((n_b * HEADS, HEAD_DIM, HEAD_DIM), F32)] + r_sems,
        compiler_params=_cparams(("arbitrary",)),
    )(qkv3, qkv3, qkv3, proj3, proj3, a_log, dt_bias, gn, hist, ainv, do3, *r_inputs)
    return dqkv.reshape(n_b * t_len, 1536), d_proj, dsm, dal, ddt, dgn, rider_outs


def _gla_scan_fwd(proj, w2, b2, gn, o_mix, n_b, t_len):
    n_c = t_len // CHUNK
    spec = functools.partial(_chunk_spec, n_b, n_c=n_c)

    def body(q_ref, k_ref, v_ref, g_ref, sm_ref, w2_ref, b2_ref, gn_ref, _, o_ref, hist_ref, s_ref):
        @pl.when(pl.program_id(0) == 0)
        def _():
            s_ref[...] = jnp.zeros_like(s_ref)

        s_in = s_ref[...]
        hist_ref[...] = s_in
        og, s_out = _gla_chains(_stack_chains(q_ref, n_b, GLA_KSL), _stack_chains(k_ref, n_b, GLA_KSL),
                                _stack_chains(v_ref, n_b, DN_SL), _stack_chains(g_ref, n_b, DN_SL),
                                _per_chain(sm_ref, n_b), s_in, *_gate_weights(w2_ref, b2_ref, n_b), gn_ref[...])
        _unstack_chains(o_ref, og, n_b, DN_SL)
        s_ref[...] = s_out

    proj3 = proj.reshape(n_b, t_len, -1)
    o, hist = pl.pallas_call(
        body, name="gla_scan_fwd", grid=(n_c,),
        in_specs=[spec(256, OFF_GQ // 256), spec(256, OFF_GK // 256), spec(512, OFF_GV // 512),
                  spec(512, OFF_GG // 512), spec(128, OFF_SMALL // 128),
                  _const_spec((128, 256)), _const_spec((1, 256)), _const_spec((1, 128)),
                  pl.BlockSpec(memory_space=pl.ANY)],
        out_specs=[spec(512, 1), _hist_spec(n_b, GLA_KEY, n_c)],
        out_shape=[jax.ShapeDtypeStruct(o_mix.shape, BF16),
                   jax.ShapeDtypeStruct((n_c, n_b * HEADS, GLA_KEY, HEAD_DIM), F32)],
        input_output_aliases={8: 0},
        scratch_shapes=[pltpu.VMEM((n_b * HEADS, GLA_KEY, HEAD_DIM), F32)],
        compiler_params=_cparams(("arbitrary",)),
    )(proj3, proj3, proj3, proj3, proj3, w2, b2, gn, o_mix)
    return o.reshape(n_b * t_len, 2 * 512), hist


def _gla_scan_bwd(proj, w2, b2, gn, hist, d_o, dsm_dn, d_proj, n_b, t_len):
    n_c = t_len // CHUNK
    rev = functools.partial(_chunk_spec, n_b, n_c=n_c, reverse=True)

    def body(q_ref, k_ref, v_ref, g_ref, sm_ref, w2_ref, b2_ref, gn_ref, hist_ref, do_ref, dsm_dn_ref, _,
             dp_ref, dw2_ref, db2_ref, dgn_ref, ds_ref):
        first = pl.program_id(0) == 0

        @pl.when(first)
        def _():
            ds_ref[...] = jnp.zeros_like(ds_ref)

        _, pull = jax.vjp(_gla_chains, _stack_chains(q_ref, n_b, GLA_KSL), _stack_chains(k_ref, n_b, GLA_KSL),
                          _stack_chains(v_ref, n_b, DN_SL), _stack_chains(g_ref, n_b, DN_SL),
                          _per_chain(sm_ref, n_b), hist_ref[...], *_gate_weights(w2_ref, b2_ref, n_b), gn_ref[...])
        dq, dk, dv, dg, dsm, ds_in, dw2, db2, dgn = pull((_stack_chains(do_ref, n_b, DN_SL), ds_ref[...]))
        _unstack_chains(dp_ref, dq, n_b, GLA_KSL, offset=OFF_GQ)
        _unstack_chains(dp_ref, dk, n_b, GLA_KSL, offset=OFF_GK)
        _unstack_chains(dp_ref, dv, n_b, DN_SL, offset=OFF_GV)
        _unstack_chains(dp_ref, dg, n_b, DN_SL, offset=OFF_GG)
        ds_ref[...] = ds_in
        for b, dsm_b in enumerate(_sum_heads(dsm, n_b)):
            dp_ref[b, :, OFF_SMALL:OFF_SMALL + 128] = (dsm_b + dsm_dn_ref[b]).astype(BF16)
            dp_ref[b, :, OFF_SMALL + 128:GLA_W] = jnp.zeros((CHUNK, GLA_W - OFF_SMALL - 128), BF16)
        for h, ks in enumerate(GLA_KSL):
            _acc(dw2_ref, sum(dw2[b * HEADS + h] for b in range(n_b)), first, at=(slice(None), ks))
            _acc(db2_ref, sum(db2[b * HEADS + h] for b in range(n_b)), first, at=(slice(None), ks))
        _acc(dgn_ref, dgn, first)

    proj3, do3 = proj.reshape(n_b, t_len, -1), d_o.reshape(n_b, t_len, -1)
    return pl.pallas_call(
        body, name="gla_scan_bwd", grid=(n_c,),
        in_specs=[rev(256, OFF_GQ // 256), rev(256, OFF_GK // 256), rev(512, OFF_GV // 512), rev(512, OFF_GG // 512),
                  rev(128, OFF_SMALL // 128),
                  _const_spec((128, 256)), _const_spec((1, 256)), _const_spec((1, 128)),
                  _hist_spec(n_b, GLA_KEY, n_c, reverse=True), rev(512, 1), rev(128, 0),
                  pl.BlockSpec(memory_space=pl.ANY)],
        out_specs=[rev(GLA_W, 0), _const_spec((128, 256)), _const_spec((1, 256)), _const_spec((1, 128))],
        out_shape=[jax.ShapeDtypeStruct(d_proj.shape, BF16), jax.ShapeDtypeStruct((128, 256), F32),
                   jax.ShapeDtypeStruct((1, 256), F32), jax.ShapeDtypeStruct((1, 128), F32)],
        input_output_aliases={11: 0},
        scratch_shapes=[pltpu.VMEM((n_b * HEADS, GLA_KEY, HEAD_DIM), F32)],
        compiler_params=_cparams(("arbitrary",)),
    )(proj3, proj3, proj3, proj3, proj3, w2, b2, gn, hist, do3, dsm_dn, d_proj)


W_IN_RUNS = ((0, 256, GLA_W), (256, 1536, OFF_Z + 512), (1536, 2048, OFF_Z), (2048, 2056, OFF_SMALL),
             (2056, 3592, 0), (3592, 3608, OFF_SMALL + 8))
W_IN_ROWS = 256


def _w_in_pieces(cols_per_chip):
    out = []
    for first, last, start in W_IN_RUNS:
        for j in range(N_CHIPS):
            a, b = max(first, cols_per_chip * j), min(last, cols_per_chip * (j + 1))
            if a < b:
                out.append((j, a - cols_per_chip * j, b - cols_per_chip * j, start + a - first))
    return out


def _w_in_to_padded(w4):
    _, n_r, n_c = w4.shape

    def body(i_ref, o_ref):
        o_ref[...] = jnp.zeros_like(o_ref)
        for j, a, b, p in _w_in_pieces(n_c):
            o_ref[:, p:p + b - a] = i_ref[j, :, a:b]

    return pl.pallas_call(
        body, name="w_in_to_padded", grid=(n_r // W_IN_ROWS,),
        in_specs=[pl.BlockSpec((N_CHIPS, W_IN_ROWS, n_c), lambda i: (0, i, 0))],
        out_specs=pl.BlockSpec((W_IN_ROWS, PROJ_W), lambda i: (i, 0)),
        out_shape=jax.ShapeDtypeStruct((n_r, PROJ_W), w4.dtype), compiler_params=_cparams(("parallel",)),
    )(w4)


def _w_in_to_chips(g, n_c):
    n_r = g.shape[0]

    def body(i_ref, o_ref):
        for j, a, b, p in _w_in_pieces(n_c):
            o_ref[j, :, a:b] = i_ref[:, p:p + b - a]

    return pl.pallas_call(
        body, name="w_in_to_chips", grid=(n_r // W_IN_ROWS,),
        in_specs=[pl.BlockSpec((W_IN_ROWS, PROJ_W), lambda i: (i, 0))],
        out_specs=pl.BlockSpec((N_CHIPS, W_IN_ROWS, n_c), lambda i: (0, i, 0)),
        out_shape=jax.ShapeDtypeStruct((N_CHIPS, n_r, n_c), g.dtype), compiler_params=_cparams(("parallel",)),
    )(g)


def _lane_vec(v, offset=0):
    return jnp.zeros((1, 128), F32).at[0, offset:offset + v.shape[0]].set(v)


def _local_step(x, tgt, mod, p, n_b, t_len, comm=None):
    row1 = lambda v: v.reshape(1, -1)
    a_log, dt_bias = _lane_vec(p["dn_a_log"]), _lane_vec(p["dn_dt_bias"])
    dn_gn, gla_gn = row1(p["dn_norm_g"]), row1(p["gla_norm_g"])
    w2 = jnp.zeros((128, 256), F32).at[8:8 + GATE_RANK].set(p["gla_w_gate2"])
    b2 = row1(p["gla_b_gate"])
    ln0_g, ln0_b, ln1_g, ln1_b, ln2_g, ln2_b = (row1(p[k]) for k in ("ln0_g", "ln0_b", "ln1_g", "ln1_b", "ln2_g", "ln2_b"))
    conv_b = row1(p["ffn_conv_b"])

    x0, h1 = _ln0_fwd(x, ln0_g, ln0_b, mod, n_b, t_len)
    proj = _mm(h1, p["w_in_p"], name="mm_proj")
    qkv = _dn_pre_fwd(proj, p["dn_conv"], n_b, t_len)
    o_half, hist_dn, landed = _dn_scan_fwd(qkv, proj, a_log, dt_bias, dn_gn, n_b, t_len,
                                           rider=comm.fwd_rider() if comm else None)
    if comm:
        p = {**p, **comm.weights_from(landed)}
    o_mix, hist_gla = _gla_scan_fwd(proj, w2, b2, gla_gn, o_half, n_b, t_len)
    y = _mm(o_mix, p["w_o"], name="mm_wo")
    x1, h2 = _ln1_fwd(x0, y, ln1_g, ln1_b, mod, n_b, t_len)
    up = _mm(h2, p["w_up"], name="mm_up")
    act = _ffn_act_fwd(up, p["ffn_conv"], conv_b, n_b, t_len)
    y2 = _mm(act, p["w_down"], name="mm_down")

    loss, dx1, dy2, g_ln2_g, g_ln2_b, dgt_f = _ln2_loss_bwd(x1, y2, ln2_g, ln2_b, mod, tgt, n_b, t_len)
    g_w_down = _mm(act, dy2, ta=True, name="mm_g_down")
    d_act = _mm(dy2, p["w_down"], tb=True, name="mm_d_act")
    d_up, g_ffn_conv, g_conv_b = _ffn_act_bwd(up, p["ffn_conv"], conv_b, d_act, n_b, t_len)
    g_w_up = _mm(h2, d_up, ta=True, out_slabs=N_CHIPS, name="mm_g_up")
    if comm:
        dh2, from_sibling = _mm(d_up, p["w_up"], tb=True, name="mm_d_h2", rider=comm.ffn_pair_rider(g_w_up, g_w_down))
    else:
        dh2 = _mm(d_up, p["w_up"], tb=True, name="mm_d_h2")
    dx0, dy, g_ln1_g, g_ln1_b, dmod_1 = _ln1_bwd(x0, y, ln1_g, ln1_b, mod, dx1, dh2, n_b, t_len)
    g_w_o = _mm(o_mix, dy, ta=True, name="mm_g_wo")
    d_o = _mm(dy, p["w_o"], tb=True, name="mm_d_o")
    dqkv, d_proj, dsm_dn, g_a_log, g_dt_bias, g_dn_gn, ffn_from_chips = _dn_scan_bwd(
        qkv, proj, a_log, dt_bias, dn_gn, hist_dn, d_o, n_b, t_len,
        rider=comm.ffn_chips_rider(from_sibling) if comm else None)
    d_proj, g_w2, g_b2, g_gla_gn = _gla_scan_bwd(proj, w2, b2, gla_gn, hist_gla, d_o, dsm_dn, d_proj, n_b, t_len)
    d_proj, g_dn_conv = _dn_pre_bwd(proj, p["dn_conv"], dqkv, d_proj.reshape(n_b * t_len, PROJ_W), n_b, t_len)
    g_w_in_p = _mm(h1, d_proj, ta=True, name="mm_g_win")
    if comm:
        dh1, tail_from_chips = _mm(d_proj, p["w_in_p"], tb=True, name="mm_d_h1",
                                   rider=comm.tail_chips_rider(g_w_in_p, g_w_o))
        from_chips = (ffn_from_chips, tail_from_chips)
    else:
        dh1, from_chips = _mm(d_proj, p["w_in_p"], tb=True, name="mm_d_h1"), None
    grad_x, g_ln0_g, g_ln0_b, dmod_0 = _ln0_bwd(x, ln0_g, ln0_b, mod, dx0, dh1, n_b, t_len)

    dmod = jnp.concatenate([dmod_0, dmod_1[:, 0:1], dmod_1[:, 1:3], dgt_f], axis=1)
    grads = {
        "ln0_g": g_ln0_g[0], "ln0_b": g_ln0_b[0], "w_in_p": g_w_in_p, "dn_conv": g_dn_conv,
        "dn_a_log": g_a_log[0, 0:HEADS], "dn_dt_bias": g_dt_bias[0, 0:HEADS], "dn_norm_g": g_dn_gn[0],
        "gla_w_gate2": g_w2[8:8 + GATE_RANK], "gla_b_gate": g_b2[0], "gla_norm_g": g_gla_gn[0],
        "w_o": g_w_o, "ln1_g": g_ln1_g[0], "ln1_b": g_ln1_b[0], "w_up": g_w_up,
        "ffn_conv": jnp.concatenate([g_ffn_conv[0], g_ffn_conv[1]], axis=1),
        "ffn_conv_b": jnp.concatenate([g_conv_b[0, 0], g_conv_b[1, 0]]), "w_down": g_w_down,
        "ln2_g": g_ln2_g[0], "ln2_b": g_ln2_b[0],
    }
    return loss, grad_x, grads, dmod, from_chips


def _ada_fwd(c_all, w_shard, b_shard):
    n_all, n_col = c_all.shape[0], w_shard.shape[1]
    tn = 512

    def body(c_ref, w_ref, b_ref, cond_ref, mod_ref):
        cond = _silu(c_ref[...])
        cond_ref[...] = cond
        mod_ref[...] = jnp.dot(cond.astype(BF16), w_ref[...].astype(BF16), preferred_element_type=F32) + b_ref[...]

    return pl.pallas_call(
        body, name="ada_fwd", grid=(n_col // tn,),
        in_specs=[pl.BlockSpec((n_all, D_MODEL), lambda j: (0, 0)), pl.BlockSpec((D_MODEL, tn), lambda j: (0, j)),
                  pl.BlockSpec((1, tn), lambda j: (0, j))],
        out_specs=[pl.BlockSpec((n_all, D_MODEL), lambda j: (0, 0)), pl.BlockSpec((n_all, tn), lambda j: (0, j))],
        out_shape=[jax.ShapeDtypeStruct((n_all, D_MODEL), F32), jax.ShapeDtypeStruct((n_all, n_col), F32)],
        compiler_params=_cparams(("arbitrary",)),
    )(c_all, w_shard, b_shard)


def _col_sum(a):
    def body(a_ref, o_ref):
        o_ref[...] = jnp.sum(a_ref[...], 0, keepdims=True)

    return pl.pallas_call(body, name="col_sum", out_shape=jax.ShapeDtypeStruct((1, a.shape[1]), F32))(a)


def _adamw_math(w, grad, m, v):
    new_m = ADAM_B1 * m + (1.0 - ADAM_B1) * grad
    new_v = ADAM_B2 * v + (1.0 - ADAM_B2) * (grad * grad)
    m_hat = new_m / (1.0 - ADAM_B1 ** ADAM_STEP)
    v_hat = new_v / (1.0 - ADAM_B2 ** ADAM_STEP)
    return -ADAM_LR * (m_hat / (jnp.sqrt(v_hat) + ADAM_EPS) + ADAM_WD * w), new_m, new_v


def _adamw_many(ws, gs, ms, vs):
    n = len(ws)

    def body(*refs):
        for i in range(n):
            w_ref, g_ref, m_ref, v_ref = (refs[k * n + i] for k in range(4))
            d_ref, nm_ref, nv_ref = (refs[(4 + k) * n + i] for k in range(3))
            d_ref[...], nm_ref[...], nv_ref[...] = _adamw_math(w_ref[...], g_ref[...], m_ref[...], v_ref[...])

    outs = pl.pallas_call(
        body, name="adamw_small", out_shape=[jax.ShapeDtypeStruct(w.shape, F32) for w in ws] * 3,
    )(*ws, *gs, *ms, *vs)
    return outs[:n], outs[n:2 * n], outs[2 * n:]


def _adamw(w, g, m, v, name):
    n_r, n_c = w.shape
    if n_r % 8 == 0:
        tr = _pick(n_r, (256, 64, 32, 16, 8))
        grid, blk = (n_r // tr,), pl.BlockSpec((tr, n_c), lambda i: (i, 0))
    else:
        tc = _pick(n_c, (256, 128))
        grid, blk = (n_c // tc,), pl.BlockSpec((n_r, tc), lambda i: (0, i))

    def body(w_ref, g_ref, m_ref, v_ref, d_ref, nm_ref, nv_ref):
        d_ref[...], nm_ref[...], nv_ref[...] = _adamw_math(w_ref[...], g_ref[...], m_ref[...], v_ref[...])

    out = jax.ShapeDtypeStruct(w.shape, F32)
    return pl.pallas_call(
        body, name=name, grid=grid, in_specs=[blk] * 4, out_specs=[blk] * 3, out_shape=[out] * 3,
        compiler_params=_cparams(("parallel",)),
    )(w, g, m, v)


HBM_SPEC = pl.BlockSpec(memory_space=pltpu.HBM)
VMEM_SPEC = pl.BlockSpec(memory_space=pltpu.VMEM)
CHIP_FLIPS = ((1, 0), (0, 1), (1, 1))


def _place():
    return lax.axis_index("x"), lax.axis_index("y"), lax.axis_index("c")


def _flip(v, f):
    return 1 - v if f else v


def _all_gather8(slab, name, rider=None):
    n_r, n_w = slab.shape
    r_inputs, r_in_specs, r_out_specs, r_sems, split = _with_rider(rider, 1, 2, 3)

    def body(*refs):
        (x_ref, o_ref, s_ref, send_sems, recv_sems, local_sem), parts = split(refs)
        if rider is not None:
            rider.first(*parts)
        x, y, c = _place()
        me = 4 * x + 2 * y + c
        mine = pltpu.make_async_copy(x_ref, o_ref.at[me], local_sem)
        mine.start()
        peers = [(_flip(x, k & 4), _flip(y, k & 2), _flip(c, k & 1)) for k in range(1, N_DEV)]
        sends = []
        for k, peer in enumerate(peers):
            cp = pltpu.make_async_remote_copy(src_ref=x_ref, dst_ref=o_ref.at[me], send_sem=send_sems.at[k],
                                              recv_sem=recv_sems.at[k], device_id=peer, device_id_type=MESH)
            cp.start()
            sends.append(cp)
        for k, (px, py, pc) in enumerate(peers):
            pltpu.make_async_remote_copy(src_ref=x_ref, dst_ref=o_ref.at[4 * px + 2 * py + pc],
                                         send_sem=send_sems.at[k], recv_sem=recv_sems.at[k],
                                         device_id=(px, py, pc), device_id_type=MESH).wait_recv()
        for cp in sends:
            cp.wait_send()
        mine.wait()
        total = o_ref[0]
        for d in range(1, N_DEV):
            total = total + o_ref[d]
        s_ref[...] = total
        if rider is not None:
            rider.last(*parts)

    gathered, total, *rider_outs = pl.pallas_call(
        body, name=name, in_specs=[VMEM_SPEC] + r_in_specs, out_specs=[VMEM_SPEC, VMEM_SPEC] + r_out_specs,
        out_shape=[jax.ShapeDtypeStruct((N_DEV, n_r, n_w), F32), jax.ShapeDtypeStruct((n_r, n_w), F32)]
        + (list(rider.out_shapes) if rider else []),
        scratch_shapes=[pltpu.SemaphoreType.DMA((N_DEV - 1,)), pltpu.SemaphoreType.DMA((N_DEV - 1,)),
                        pltpu.SemaphoreType.DMA] + r_sems,
    )(slab, *r_inputs)
    return (gathered, total, rider_outs) if rider else (gathered, total)


def _gather_rider(shards):
    n_a = len(shards)

    def plan(ins, outs, sems):
        send_sems, recv_sems = sems
        x, y, c = _place()
        chips = [(_flip(x, fx), _flip(y, fy)) for fx, fy in CHIP_FLIPS]

        def copy(k, slot, chip_of_block, half, to, src=None):
            dst = outs[k].at[chip_of_block, half]
            return pltpu.make_async_remote_copy(src_ref=dst if src is None else src, dst_ref=dst,
                                                send_sem=send_sems.at[k * 6 + slot], recv_sem=recv_sems.at[k * 6 + slot],
                                                device_id=to, device_id_type=MESH)

        first = [copy(k, r, 2 * x + y, c, (*chips[r], c), src=ins[k].at[c]) for k in range(n_a) for r in range(3)]
        return copy, chips, first, (x, y, c)

    def first_step(ins, outs, sems):
        for cp in plan(ins, outs, sems)[2]:
            cp.start()

    def last_step(ins, outs, sems):
        copy, chips, first, (x, y, c) = plan(ins, outs, sems)
        passed = []
        for k in range(n_a):
            for r, (px, py) in enumerate(chips):
                copy(k, r, 2 * px + py, c, (x, y, c)).wait_recv()
                fwd = copy(k, 3 + r, 2 * px + py, c, (x, y, 1 - c))
                fwd.start()
                passed.append(fwd)
        for k in range(n_a):
            for r, (px, py) in enumerate(chips):
                copy(k, 3 + r, 2 * px + py, 1 - c, (x, y, c)).wait_recv()
        for cp in first + passed:
            cp.wait_send()

    return Rider(shards, [jax.ShapeDtypeStruct((N_CHIPS,) + s.shape, s.dtype) for s in shards],
                 [pltpu.SemaphoreType.DMA((6 * n_a,)), pltpu.SemaphoreType.DMA((6 * n_a,))], first_step, last_step)


def _place_own(gathered, shard, chip, name):
    _, _, n_h, n_c = gathered.shape
    th = _pick(n_h, (256, 176, 128))

    def body(sel_ref, s_ref, _, o_ref):
        o_ref[...] = s_ref[...]

    grid_spec = pltpu.PrefetchScalarGridSpec(
        num_scalar_prefetch=1, grid=(2, n_h // th),
        in_specs=[pl.BlockSpec((None, th, n_c), lambda hf, i, sel: (hf, i, 0)), pl.BlockSpec(memory_space=pl.ANY)],
        out_specs=pl.BlockSpec((None, None, th, n_c), lambda hf, i, sel: (sel[0], hf, i, 0)))
    return pl.pallas_call(
        body, name=name, grid_spec=grid_spec, out_shape=jax.ShapeDtypeStruct(gathered.shape, gathered.dtype),
        input_output_aliases={2: 0}, compiler_params=_cparams(("parallel", "parallel")),
    )(chip.reshape(1), shard, gathered)


def _pair_rider(parts):
    n_a = len(parts)

    def plan(ins, outs, sems):
        send_sems, recv_sems = sems
        x, y, c = _place()
        return [pltpu.make_async_remote_copy(src_ref=ins[k].at[:, 1 - c], dst_ref=outs[k], send_sem=send_sems.at[k],
                                             recv_sem=recv_sems.at[k], device_id=(x, y, 1 - c), device_id_type=MESH)
                for k in range(n_a)]

    def first_step(ins, outs, sems):
        for cp in plan(ins, outs, sems):
            cp.start()

    def last_step(ins, outs, sems):
        for cp in plan(ins, outs, sems):
            cp.wait()

    return Rider(parts, [jax.ShapeDtypeStruct((N_CHIPS,) + p.shape[2:], F32) for p in parts],
                 [pltpu.SemaphoreType.DMA((n_a,)), pltpu.SemaphoreType.DMA((n_a,))], first_step, last_step)


def _alone(rider, name):
    n_a = len(rider.inputs)

    def body(*refs):
        parts = (refs[:n_a], refs[n_a:2 * n_a], refs[2 * n_a:])
        rider.first(*parts)
        rider.last(*parts)

    return pl.pallas_call(
        body, name=name, in_specs=[HBM_SPEC] * n_a, out_specs=[HBM_SPEC] * n_a,
        out_shape=rider.out_shapes, scratch_shapes=rider.sems,
    )(*rider.inputs)


def _chips_rider(sums):
    n_a = len(sums)

    def plan(ins, outs, sems):
        send_sems, recv_sems = sems
        x, y, c = _place()
        cps = []
        for k in range(n_a):
            for r, (fx, fy) in enumerate(CHIP_FLIPS):
                px, py = _flip(x, fx), _flip(y, fy)
                cps.append(pltpu.make_async_remote_copy(
                    src_ref=ins[k].at[2 * px + py], dst_ref=outs[k].at[r], send_sem=send_sems.at[3 * k + r],
                    recv_sem=recv_sems.at[3 * k + r], device_id=(px, py, c), device_id_type=MESH))
        return cps

    def first_step(ins, outs, sems):
        for cp in plan(ins, outs, sems):
            cp.start()

    def last_step(ins, outs, sems):
        for cp in plan(ins, outs, sems):
            cp.wait()

    return Rider(sums, [jax.ShapeDtypeStruct((3,) + s.shape[1:], s.dtype) for s in sums],
                 [pltpu.SemaphoreType.DMA((3 * n_a,)), pltpu.SemaphoreType.DMA((3 * n_a,))], first_step, last_step)


def _rs_share(bufs):
    n_a = len(bufs)

    def body(*refs):
        ins, outs = refs[:n_a], refs[n_a:2 * n_a]
        send_sems, recv_sems = refs[2 * n_a:]
        x, y, c = _place()
        sends = [pltpu.make_async_remote_copy(src_ref=ins[k].at[c], dst_ref=outs[k].at[c], send_sem=send_sems.at[k],
                                              recv_sem=recv_sems.at[k], device_id=(x, y, 1 - c), device_id_type=MESH)
                 for k in range(n_a)]
        for cp in sends:
            cp.start()
        for k in range(n_a):
            pltpu.make_async_remote_copy(src_ref=ins[k].at[c], dst_ref=outs[k].at[1 - c], send_sem=send_sems.at[k],
                                         recv_sem=recv_sems.at[k], device_id=(x, y, 1 - c),
                                         device_id_type=MESH).wait_recv()
        for cp in sends:
            cp.wait_send()

    return pl.pallas_call(
        body, name="rs_share", in_specs=[HBM_SPEC] * n_a, out_specs=[HBM_SPEC] * n_a,
        out_shape=[jax.ShapeDtypeStruct(s.shape, F32) for s in bufs],
        input_output_aliases={k: k for k in range(n_a)},
        scratch_shapes=[pltpu.SemaphoreType.DMA((n_a,)), pltpu.SemaphoreType.DMA((n_a,))],
    )(*bufs)


def _pair_add(part, recv, core, name):
    _, _, n_h, n_c = part.shape
    th = _pick(n_h, (256, 176, 128))

    def body(sel_ref, p_ref, r_ref, o_ref):
        o_ref[...] = (p_ref[...] + r_ref[...]).astype(BF16)

    grid_spec = pltpu.PrefetchScalarGridSpec(
        num_scalar_prefetch=1, grid=(N_CHIPS, n_h // th),
        in_specs=[pl.BlockSpec((None, None, th, n_c), lambda j, i, sel: (j, sel[0], i, 0)),
                  pl.BlockSpec((None, th, n_c), lambda j, i, sel: (j, i, 0))],
        out_specs=pl.BlockSpec((None, th, n_c), lambda j, i, sel: (j, i, 0)))
    return pl.pallas_call(
        body, name=name, grid_spec=grid_spec, out_shape=jax.ShapeDtypeStruct(recv.shape, BF16),
        compiler_params=_cparams(("parallel", "parallel")),
    )(core.reshape(1), part, recv)


def _chip_add(sums, recv, chip, core, name):
    _, n_h, n_c = sums.shape
    th = _pick(n_h, (256, 176, 128))

    def body(sel_ref, s_ref, r_ref, o_ref):
        total = s_ref[...].astype(F32)
        for r in range(3):
            total = total + r_ref[r].astype(F32)
        o_ref[...] = total

    grid_spec = pltpu.PrefetchScalarGridSpec(
        num_scalar_prefetch=1, grid=(n_h // th,),
        in_specs=[pl.BlockSpec((None, th, n_c), lambda i, sel: (sel[0], i, 0)),
                  pl.BlockSpec((3, th, n_c), lambda i, sel: (0, i, 0))],
        out_specs=pl.BlockSpec((None, th, n_c), lambda i, sel: (sel[1], i, 0)))
    return pl.pallas_call(
        body, name=name, grid_spec=grid_spec, out_shape=jax.ShapeDtypeStruct((2, n_h, n_c), F32),
        compiler_params=_cparams(("parallel",)),
    )(jnp.stack([chip, core]), sums, recv)


def _row_halves(a):
    return a.reshape(N_CHIPS, 2, -1, a.shape[-1])


class StepComm:
    REST = ("w_o", "w_up", "w_down")

    def __init__(self, core, chip, rest_shards, in_cols):
        self.core, self.chip, self.shards, self.in_cols = core, chip, rest_shards, in_cols

    def fwd_rider(self):
        return _gather_rider(self.shards)

    def weights_from(self, landed):
        g_o, g_up, g_down = (_place_own(g, s, self.chip, "place_own_" + n)
                             for g, s, n in zip(landed, self.shards, self.REST))
        return {"w_o": g_o.reshape(-1, D_MODEL), "w_up": g_up.reshape(N_CHIPS, -1, g_up.shape[-1]),
                "w_down": g_down.reshape(-1, D_MODEL)}

    def _add_pairs(self, parts, from_sibling, names):
        return [_pair_add(p, r, self.core, "pair_add_" + n) for p, r, n in zip(parts, from_sibling, names)]

    def ffn_pair_rider(self, g_w_up, g_w_down):
        self.ffn_parts = [_row_halves(g_w_up), _row_halves(g_w_down)]
        return _pair_rider(self.ffn_parts)

    def ffn_chips_rider(self, from_sibling):
        self.ffn_sums = self._add_pairs(self.ffn_parts, from_sibling, ("w_up", "w_down"))
        return _chips_rider(self.ffn_sums)

    def tail_chips_rider(self, g_w_in_p, g_w_o):
        parts = [_row_halves(_w_in_to_chips(g_w_in_p, self.in_cols)), _row_halves(g_w_o)]
        self.tail_sums = self._add_pairs(parts, _alone(_pair_rider(parts), "rs_pair_tail"), ("w_in", "w_o"))
        return _chips_rider(self.tail_sums)

    def finish(self, ffn_from_chips, tail_from_chips):
        halves = [_chip_add(s, r, self.chip, self.core, "chip_add_" + n)
                  for s, r, n in zip(self.tail_sums + self.ffn_sums, list(tail_from_chips) + list(ffn_from_chips),
                                     ("w_in", "w_o", "w_up", "w_down"))]
        return [f.reshape(-1, f.shape[-1]) for f in _rs_share(halves)]


SLAB_W = 1024


def _pack(arrays, rows):
    flat = jnp.concatenate([a.reshape(-1).astype(F32) for a in arrays])
    return jnp.pad(flat, (0, rows * SLAB_W - flat.shape[0])).reshape(rows, SLAB_W)


def _unpack(flat, shapes):
    out, off = [], 0
    for s in shapes:
        n = 1
        for d in s:
            n *= d
        out.append(flat[off:off + n].reshape(s))
        off += n
    return out


def _rows_for(arrays_or_shapes):
    n = 0
    for a in arrays_or_shapes:
        s = a if isinstance(a, tuple) else a.shape
        k = 1
        for d in s:
            k *= d
        n += k
    return -(-n // (8 * SLAB_W)) * 8


def kernel(x, c, ln0_g, ln0_b, w_ada, b_ada, w_in, dn_conv, dn_a_log, dn_dt_bias, dn_norm_g, gla_w_gate2, gla_b_gate, gla_norm_g, w_o, ln1_g, ln1_b, ffn_w_up, ffn_conv, ffn_conv_b, ffn_w_down, ln2_g, ln2_b, loss_target, m_ln0_g, m_ln0_b, m_w_ada, m_b_ada, m_w_in, m_dn_conv, m_dn_a_log, m_dn_dt_bias, m_dn_norm_g, m_gla_w_gate2, m_gla_b_gate, m_gla_norm_g, m_w_o, m_ln1_g, m_ln1_b, m_ffn_w_up, m_ffn_conv, m_ffn_conv_b, m_ffn_w_down, m_ln2_g, m_ln2_b, v_ln0_g, v_ln0_b, v_w_ada, v_b_ada, v_w_in, v_dn_conv, v_dn_a_log, v_dn_dt_bias, v_dn_norm_g, v_gla_w_gate2, v_gla_b_gate, v_gla_norm_g, v_w_o, v_ln1_g, v_ln1_b, v_ffn_w_up, v_ffn_conv, v_ffn_conv_b, v_ffn_w_down, v_ln2_g, v_ln2_b):
    n_b, t_len, _ = x.shape
    xi, yi, ci = _place()
    chip = (2 * xi + yi).astype(jnp.int32)
    core = ci.astype(jnp.int32)
    me = 2 * chip + core
    n_all = N_DEV * n_b
    ada_cols = w_ada.shape[2]

    halves = lambda a: a.astype(BF16).reshape(2, a.shape[0] // 2, a.shape[1])
    w_in_halves = halves(w_in[0])
    sharded_small = [dn_conv[0], gla_w_gate2[0], ffn_conv[0]]
    slab = _pack([c] + sharded_small, _rows_for([c] + sharded_small))
    gathered, _, (g_in,) = _all_gather8(slab, "gather_small", rider=_gather_rider([w_in_halves]))
    g_in = _place_own(g_in, w_in_halves, chip, "place_own_w_in")
    flat = gathered.reshape(N_DEV, -1)
    c_all = flat[:, :c.size].reshape(n_all, D_MODEL)
    by_chip = flat[0::2]
    full, off = [], c.size
    for a in sharded_small:
        blocks = by_chip[:, off:off + a.size].reshape(N_CHIPS, *a.shape)
        full.append(blocks.transpose(1, 0, 2).reshape(a.shape[0], N_CHIPS * a.shape[1]))
        off += a.size
    dn_conv_f, gate2_f, ffn_conv_f = full

    b_ada_shard = lax.dynamic_slice(b_ada, (0, chip * ada_cols), (1, ada_cols))
    cond_all, mod_cols = _ada_fwd(c_all, w_ada[0], b_ada_shard)
    mod_g, _ = _all_gather8(mod_cols, "gather_mod")
    mod_full = jnp.concatenate([mod_g[2 * j] for j in range(N_CHIPS)], axis=1)
    mod = lax.dynamic_slice(mod_full, (me * n_b, 0), (n_b, 6 * D_MODEL)).reshape(n_b, 6, D_MODEL)

    comm = StepComm(core, chip, [halves(w_o[0]), halves(ffn_w_up[0]), halves(ffn_w_down[0])], w_in.shape[2])
    params = {
        "w_in_p": _w_in_to_padded(g_in.reshape(N_CHIPS, -1, g_in.shape[-1])),
        "dn_conv": dn_conv_f, "dn_a_log": dn_a_log[0], "dn_dt_bias": dn_dt_bias[0], "dn_norm_g": dn_norm_g[0],
        "gla_w_gate2": gate2_f, "gla_b_gate": gla_b_gate[0], "gla_norm_g": gla_norm_g[0],
        "ln0_g": ln0_g, "ln0_b": ln0_b, "ln1_g": ln1_g[0], "ln1_b": ln1_b[0], "ln2_g": ln2_g[0], "ln2_b": ln2_b[0],
        "ffn_conv": ffn_conv_f, "ffn_conv_b": ffn_conv_b[0],
    }

    loss_row, grad_x, gp, dmod, from_chips = _local_step(
        x.reshape(n_b * t_len, D_MODEL), loss_target.reshape(n_b * t_len, D_MODEL), mod, params, n_b, t_len, comm)
    loss = lax.psum(loss_row[0, 0], ("x", "y", "c"))

    summed_names = ["ln0_g", "ln0_b", "dn_conv", "dn_a_log", "dn_dt_bias", "dn_norm_g", "gla_w_gate2", "gla_b_gate",
                    "gla_norm_g", "ln1_g", "ln1_b", "ffn_conv", "ffn_conv_b", "ln2_g", "ln2_b"]
    summed_parts = [gp[n] for n in summed_names]
    sum_rows = _rows_for(summed_parts)
    slab = jnp.concatenate([_pack(summed_parts, sum_rows), _pack([dmod], _rows_for([dmod]))], axis=0)
    gathered, total = _all_gather8(slab, "reduce_small")
    small_g = dict(zip(summed_names, _unpack(total.reshape(-1), [a.shape for a in summed_parts])))
    dmod_rows = n_b * 6 * D_MODEL // SLAB_W
    dmod_all = gathered[:, sum_rows:sum_rows + dmod_rows, :].reshape(n_all, 6 * D_MODEL)

    g_b_ada = _col_sum(dmod_all)
    dmod_cols = lax.dynamic_slice(dmod_all, (0, chip * ada_cols), (n_all, ada_cols))
    g_w_ada = _mm(cond_all, dmod_cols, ta=True, name="mm_g_ada")

    g_w_in, g_w_o, g_w_up, g_w_down = comm.finish(*from_chips)

    col_block = lambda a: lax.dynamic_slice(a, (0, chip * (a.shape[1] // N_CHIPS)), (a.shape[0], a.shape[1] // N_CHIPS))
    grads = {
        "ln0_g": small_g["ln0_g"], "ln0_b": small_g["ln0_b"], "w_ada": g_w_ada[None], "b_ada": g_b_ada,
        "w_in": g_w_in[None], "dn_conv": col_block(small_g["dn_conv"])[None], "dn_a_log": small_g["dn_a_log"][None],
        "dn_dt_bias": small_g["dn_dt_bias"][None], "dn_norm_g": small_g["dn_norm_g"][None],
        "gla_w_gate2": col_block(small_g["gla_w_gate2"])[None], "gla_b_gate": small_g["gla_b_gate"][None],
        "gla_norm_g": small_g["gla_norm_g"][None], "w_o": g_w_o[None], "ln1_g": small_g["ln1_g"][None],
        "ln1_b": small_g["ln1_b"][None], "ffn_w_up": g_w_up[None], "ffn_conv": col_block(small_g["ffn_conv"])[None],
        "ffn_conv_b": small_g["ffn_conv_b"][None], "ffn_w_down": g_w_down[None], "ln2_g": small_g["ln2_g"][None],
        "ln2_b": small_g["ln2_b"][None],
    }
    names = ["ln0_g", "ln0_b", "w_ada", "b_ada", "w_in", "dn_conv", "dn_a_log", "dn_dt_bias", "dn_norm_g",
             "gla_w_gate2", "gla_b_gate", "gla_norm_g", "w_o", "ln1_g", "ln1_b", "ffn_w_up", "ffn_conv", "ffn_conv_b",
             "ffn_w_down", "ln2_g", "ln2_b"]
    weights = dict(zip(names, [ln0_g, ln0_b, w_ada, b_ada, w_in, dn_conv, dn_a_log, dn_dt_bias, dn_norm_g, gla_w_gate2,
                               gla_b_gate, gla_norm_g, w_o, ln1_g, ln1_b, ffn_w_up, ffn_conv, ffn_conv_b, ffn_w_down,
                               ln2_g, ln2_b]))
    m_in = dict(zip(names, [m_ln0_g, m_ln0_b, m_w_ada, m_b_ada, m_w_in, m_dn_conv, m_dn_a_log, m_dn_dt_bias,
                            m_dn_norm_g, m_gla_w_gate2, m_gla_b_gate, m_gla_norm_g, m_w_o, m_ln1_g, m_ln1_b,
                            m_ffn_w_up, m_ffn_conv, m_ffn_conv_b, m_ffn_w_down, m_ln2_g, m_ln2_b]))
    v_in = dict(zip(names, [v_ln0_g, v_ln0_b, v_w_ada, v_b_ada, v_w_in, v_dn_conv, v_dn_a_log, v_dn_dt_bias,
                            v_dn_norm_g, v_gla_w_gate2, v_gla_b_gate, v_gla_norm_g, v_w_o, v_ln1_g, v_ln1_b,
                            v_ffn_w_up, v_ffn_conv, v_ffn_conv_b, v_ffn_w_down, v_ln2_g, v_ln2_b]))

    big = ("w_ada", "w_in", "w_o", "ffn_w_up", "ffn_w_down")
    delta, new_m, new_v = {}, {}, {}
    for n in big:
        view = (lambda a: a.T) if n == "w_in" else (lambda a: a)
        d_n, m_n, v_n = _adamw(view(weights[n][0]), view(grads[n][0]), view(m_in[n][0]), view(v_in[n][0]), "adamw_" + n)
        delta[n], new_m[n], new_v[n] = view(d_n)[None], view(m_n)[None], view(v_n)[None]
    small = [n for n in names if n not in big]
    d_s, m_s, v_s = _adamw_many([weights[n] for n in small], [grads[n] for n in small],
                                [m_in[n] for n in small], [v_in[n] for n in small])
    for out, vals in ((delta, d_s), (new_m, m_s), (new_v, v_s)):
        out.update(zip(small, vals))

    return (loss, grad_x.reshape(x.shape), *[grads[n] for n in names], *[delta[n] for n in names],
            *[new_m[n] for n in names], *[new_v[n] for n in names])
```

```python
import functools

import jax
import jax.numpy as jnp
from jax import lax
from jax.experimental import pallas as pl
from jax.experimental.pallas import tpu as pltpu

F32 = jnp.float32
BF16 = jnp.bfloat16
MESH = pl.DeviceIdType.MESH

D_MODEL = 1024
HEADS = 4
HEAD_DIM = 128
GLA_KEY = 64
GATE_RANK = 16
CHUNK = 64
D_FF = 2816
ALPHA = 2.0 ** 0.25
EPS = 1e-6
N_CHIPS = 4
N_DEV = 8

PROJ_W = 3840
OFF_GQ, OFF_GK, OFF_GV, OFF_GG, OFF_SMALL, GLA_W = 0, 256, 512, 1024, 1536, 1792
OFF_Z = 2048
W_IN_COLS = 3608


def _qkv_block(j):
    return jnp.where(j < 2, GLA_W // 128 + j, (OFF_Z + 512) // 128 - 2 + j)

ADAM_LR, ADAM_B1, ADAM_B2, ADAM_EPS, ADAM_WD, ADAM_STEP = 0.001, 0.9, 0.999, 1e-08, 0.01, 10

VMEM_LIMIT = 56 * 1024 * 1024
ROW_TILE = 512


def _cparams(sem):
    return pltpu.CompilerParams(dimension_semantics=sem, vmem_limit_bytes=VMEM_LIMIT)


def _pick(n, prefs):
    for p in prefs:
        if n % p == 0:
            return p
    return n


def _mm(a, b, *, ta=False, tb=False, out_slabs=1, out_dtype=F32, name, rider=None):
    a_slabs = a.shape[0] if a.ndim == 3 else 1
    b_slabs = b.shape[0] if b.ndim == 3 else 1
    assert not (ta and a_slabs > 1)
    a2, b2 = a.shape[-2:], b.shape[-2:]
    if ta:
        k_dim, m_dim = a2
    else:
        m_dim, k_dim = a2[0], a2[1] * a_slabs
    n_dim = b2[0] if tb else b2[1] * b_slabs
    k_slabs = max(a_slabs, b_slabs if tb else 1)
    n_slabs = max(out_slabs, 1 if tb else b_slabs)
    tm = _pick(m_dim, (1024, 1408, 512, 256, 128))
    tn = _pick(n_dim // n_slabs, (1536, 1408, 1280, 1024, 768, 512, 384, 256, 128))
    tk = _pick(k_dim // k_slabs, (1408, 1280, 1024, 512, 256, 128))
    nk, nj = k_dim // tk, n_dim // tn
    nk_a, nk_b, nj_b, nj_o = nk // a_slabs, nk // b_slabs, nj // b_slabs, nj // out_slabs
    dims = (((0 if ta else 1,), (1 if tb else 0,)), ((), ()))

    grid = (m_dim // tm, nj, nk)
    assert out_dtype == F32
    r_inputs, r_in_specs, r_out_specs, r_sems, split = _with_rider(rider, 2, 1, 0)

    def body(*refs):
        (a_ref, b_ref, o_ref), parts = split(refs)
        ride_first, ride_last = _ride(rider, parts, grid)
        if rider is not None:
            ride_first()
        prod = lax.dot_general(a_ref[...].astype(BF16), b_ref[...].astype(BF16), dims, preferred_element_type=F32)
        if nk == 1:
            o_ref[...] = prod
        else:
            _acc(o_ref, prod, pl.program_id(2) == 0)
        if rider is not None:
            ride_last()

    if ta:
        a_spec = pl.BlockSpec((tk, tm), lambda i, j, k: (k, i))
    elif a_slabs > 1:
        a_spec = pl.BlockSpec((None, tm, tk), lambda i, j, k: (k // nk_a, i, k % nk_a))
    else:
        a_spec = pl.BlockSpec((tm, tk), lambda i, j, k: (i, k))
    if tb and b_slabs > 1:
        b_spec = pl.BlockSpec((None, tn, tk), lambda i, j, k: (k // nk_b, j, k % nk_b))
    elif tb:
        b_spec = pl.BlockSpec((tn, tk), lambda i, j, k: (j, k))
    elif b_slabs > 1:
        b_spec = pl.BlockSpec((None, tk, tn), lambda i, j, k: (j // nj_b, k, j % nj_b))
    else:
        b_spec = pl.BlockSpec((tk, tn), lambda i, j, k: (k, j))
    if out_slabs > 1:
        o_spec = pl.BlockSpec((None, tm, tn), lambda i, j, k: (j // nj_o, i, j % nj_o))
        o_shape = (out_slabs, m_dim, n_dim // out_slabs)
    else:
        o_spec, o_shape = pl.BlockSpec((tm, tn), lambda i, j, k: (i, j)), (m_dim, n_dim)
    out, *rider_outs = pl.pallas_call(
        body, name=name, grid=grid,
        in_specs=[a_spec, b_spec] + r_in_specs, out_specs=[o_spec] + r_out_specs,
        out_shape=[jax.ShapeDtypeStruct(o_shape, out_dtype)] + (list(rider.out_shapes) if rider else []),
        scratch_shapes=r_sems,
        compiler_params=_cparams(("arbitrary",) * 3 if rider else ("parallel", "parallel", "arbitrary")),
    )(a, b, *r_inputs)
    return (out, rider_outs) if rider else out


def _ln(x, g, b):
    mu = jnp.mean(x, -1, keepdims=True)
    xc = x - mu
    var = jnp.mean(xc * xc, -1, keepdims=True)
    return xc * lax.rsqrt(var + EPS) * g + b


def _softplus(x):
    return jnp.maximum(x, 0.0) + jnp.log(1.0 + jnp.exp(-jnp.abs(x)))


def _silu(x):
    return x * jax.nn.sigmoid(x)


def _dsilu(x):
    s = jax.nn.sigmoid(x)
    return s * (1.0 + x * (1.0 - s))


def _f_ln0(x, g, b, sc, sh):
    x0 = _ln(x, g, b)
    return x0, x0 * (1.0 + sc) + sh


def _f_ln1(x0, y, gt, g, b, sc, sh):
    x1 = _ln(ALPHA * x0 + (1.0 + gt) * y, g, b)
    return x1, x1 * (1.0 + sc) + sh


def _f_ln2_loss(x1, y2, gt, g, b, tgt):
    x2 = _ln(ALPHA * x1 + (1.0 + gt) * y2, g, b)
    err = x2 - tgt
    per_row = jnp.sum(err * err, -1, keepdims=True) * (0.5 / D_MODEL)
    return jnp.sum(per_row, 0, keepdims=True)


def _row_specs(t_len):
    nt = t_len // ROW_TILE
    row = pl.BlockSpec((ROW_TILE, D_MODEL), lambda b, i: (b * nt + i, 0))
    vec = pl.BlockSpec((1, D_MODEL), lambda b, i: (0, 0))
    mod = pl.BlockSpec((None, 6, D_MODEL), lambda b, i: (b, 0, 0))
    return nt, row, vec, mod


def _first_step():
    return jnp.logical_and(pl.program_id(0) == 0, pl.program_id(1) == 0)


def _acc(ref, val, first, at=(Ellipsis,)):
    @pl.when(first)
    def _():
        ref[at] = val

    @pl.when(jnp.logical_not(first))
    def _():
        ref[at] += val


def _acc_rows(ref, rows, first):
    for i, r in enumerate(rows):
        _acc(ref, r, first, at=(slice(i, i + 1), slice(None)))


def _ln0_fwd(x, g, b, mod, n_b, t_len):
    nt, row, vec, mods = _row_specs(t_len)

    def body(x_ref, g_ref, b_ref, mod_ref, x0_ref, h_ref):
        x0, h = _f_ln0(x_ref[...], g_ref[...], b_ref[...], mod_ref[1:2, :], mod_ref[0:1, :])
        x0_ref[...] = x0
        h_ref[...] = h.astype(BF16)

    return pl.pallas_call(
        body, name="ln0_fwd", grid=(n_b, nt), in_specs=[row, vec, vec, mods], out_specs=[row, row],
        out_shape=[jax.ShapeDtypeStruct(x.shape, F32), jax.ShapeDtypeStruct(x.shape, BF16)],
        compiler_params=_cparams(("parallel", "parallel")),
    )(x, g, b, mod)


def _ln0_bwd(x, g, b, mod, dx0, dh, n_b, t_len):
    nt, row, vec, mods = _row_specs(t_len)
    dmod_spec = pl.BlockSpec((None, 2, D_MODEL), lambda bb, i: (bb, 0, 0))

    def body(x_ref, g_ref, b_ref, mod_ref, dx0_ref, dh_ref, dx_ref, dg_ref, db_ref, dmod_ref):
        _, pull = jax.vjp(_f_ln0, x_ref[...], g_ref[...], b_ref[...], mod_ref[1:2, :], mod_ref[0:1, :])
        dx, dg, db, dsc, dsh = pull((dx0_ref[...], dh_ref[...]))
        dx_ref[...] = dx
        _acc(dg_ref, dg, _first_step())
        _acc(db_ref, db, _first_step())
        _acc_rows(dmod_ref, [dsh, dsc], pl.program_id(1) == 0)

    return pl.pallas_call(
        body, name="ln0_bwd", grid=(n_b, nt), in_specs=[row, vec, vec, mods, row, row],
        out_specs=[row, vec, vec, dmod_spec],
        out_shape=[jax.ShapeDtypeStruct(x.shape, F32), jax.ShapeDtypeStruct((1, D_MODEL), F32),
                   jax.ShapeDtypeStruct((1, D_MODEL), F32), jax.ShapeDtypeStruct((n_b, 2, D_MODEL), F32)],
        compiler_params=_cparams(("arbitrary", "arbitrary")),
    )(x, g, b, mod, dx0, dh)


def _ln1_fwd(x0, y, g, b, mod, n_b, t_len):
    nt, row, vec, mods = _row_specs(t_len)

    def body(x0_ref, y_ref, g_ref, b_ref, mod_ref, x1_ref, h_ref):
        x1, h = _f_ln1(x0_ref[...], y_ref[...], mod_ref[2:3, :], g_ref[...], b_ref[...],
                       mod_ref[4:5, :], mod_ref[3:4, :])
        x1_ref[...] = x1
        h_ref[...] = h.astype(BF16)

    return pl.pallas_call(
        body, name="ln1_fwd", grid=(n_b, nt), in_specs=[row, row, vec, vec, mods], out_specs=[row, row],
        out_shape=[jax.ShapeDtypeStruct(x0.shape, F32), jax.ShapeDtypeStruct(x0.shape, BF16)],
        compiler_params=_cparams(("parallel", "parallel")),
    )(x0, y, g, b, mod)


def _ln1_bwd(x0, y, g, b, mod, dx1, dh, n_b, t_len):
    nt, row, vec, mods = _row_specs(t_len)
    dmod_spec = pl.BlockSpec((None, 3, D_MODEL), lambda bb, i: (bb, 0, 0))

    def body(x0_ref, y_ref, g_ref, b_ref, mod_ref, dx1_ref, dh_ref, dx0_ref, dy_ref, dg_ref, db_ref, dmod_ref):
        _, pull = jax.vjp(_f_ln1, x0_ref[...], y_ref[...], mod_ref[2:3, :], g_ref[...], b_ref[...],
                          mod_ref[4:5, :], mod_ref[3:4, :])
        dx0, dy, dgt, dg, db, dsc, dsh = pull((dx1_ref[...], dh_ref[...]))
        dx0_ref[...] = dx0
        dy_ref[...] = dy.astype(BF16)
        _acc(dg_ref, dg, _first_step())
        _acc(db_ref, db, _first_step())
        _acc_rows(dmod_ref, [dgt, dsh, dsc], pl.program_id(1) == 0)

    return pl.pallas_call(
        body, name="ln1_bwd", grid=(n_b, nt), in_specs=[row, row, vec, vec, mods, row, row],
        out_specs=[row, row, vec, vec, dmod_spec],
        out_shape=[jax.ShapeDtypeStruct(x0.shape, F32), jax.ShapeDtypeStruct(x0.shape, BF16),
                   jax.ShapeDtypeStruct((1, D_MODEL), F32), jax.ShapeDtypeStruct((1, D_MODEL), F32),
                   jax.ShapeDtypeStruct((n_b, 3, D_MODEL), F32)],
        compiler_params=_cparams(("arbitrary", "arbitrary")),
    )(x0, y, g, b, mod, dx1, dh)


def _ln2_loss_bwd(x1, y2, g, b, mod, tgt, n_b, t_len):
    nt, row, vec, mods = _row_specs(t_len)
    one = pl.BlockSpec((1, 128), lambda bb, i: (0, 0))
    dmod_spec = pl.BlockSpec((None, 1, D_MODEL), lambda bb, i: (bb, 0, 0))

    def body(x1_ref, y2_ref, g_ref, b_ref, mod_ref, t_ref, loss_ref, dx1_ref, dy2_ref, dg_ref, db_ref, dgt_ref):
        loss, pull = jax.vjp(functools.partial(_f_ln2_loss, tgt=t_ref[...]), x1_ref[...], y2_ref[...],
                             mod_ref[5:6, :], g_ref[...], b_ref[...])
        dx1, dy2, dgt, dg, db = pull(jnp.ones((1, 1), F32))
        dx1_ref[...] = dx1
        dy2_ref[...] = dy2.astype(BF16)
        _acc(loss_ref, jnp.broadcast_to(loss, (1, 128)), _first_step())
        _acc(dg_ref, dg, _first_step())
        _acc(db_ref, db, _first_step())
        _acc(dgt_ref, dgt, pl.program_id(1) == 0)

    return pl.pallas_call(
        body, name="ln2_loss_bwd", grid=(n_b, nt), in_specs=[row, row, vec, vec, mods, row],
        out_specs=[one, row, row, vec, vec, dmod_spec],
        out_shape=[jax.ShapeDtypeStruct((1, 128), F32), jax.ShapeDtypeStruct(x1.shape, F32),
                   jax.ShapeDtypeStruct(x1.shape, BF16), jax.ShapeDtypeStruct((1, D_MODEL), F32),
                   jax.ShapeDtypeStruct((1, D_MODEL), F32), jax.ShapeDtypeStruct((n_b, 1, D_MODEL), F32)],
        compiler_params=_cparams(("arbitrary", "arbitrary")),
    )(x1, y2, g, b, mod, tgt)


def _shift_down(x, s):
    if s == 0:
        return x
    rows = lax.broadcasted_iota(jnp.int32, x.shape, 0)
    return jnp.where(rows >= s, pltpu.roll(x, s, 0), 0.0)


def _shift_up(x, s):
    if s == 0:
        return x
    t_len = x.shape[0]
    rows = lax.broadcasted_iota(jnp.int32, x.shape, 0)
    return jnp.where(rows < t_len - s, pltpu.roll(x, t_len - s, 0), 0.0)


def _taps(x, k_w):
    return [_shift_down(x, k_w - 1 - k) for k in range(k_w)]


def _conv(taps, w):
    out = w[0:1, :] * taps[0]
    for k in range(1, len(taps)):
        out = out + w[k:k + 1, :] * taps[k]
    return out


def _conv_bwd(taps, w, du):
    k_w = len(taps)
    dx = w[k_w - 1:k_w, :] * du
    for k in range(k_w - 1):
        dx = dx + w[k:k + 1, :] * _shift_up(du, k_w - 1 - k)
    return dx, [jnp.sum(du * taps[k], 0, keepdims=True) for k in range(k_w)]


def _dn_pre_fwd(proj, conv_w, n_b, t_len):
    n_ct = 3 * HEADS
    k_w = conv_w.shape[0]

    def body(x_ref, w_ref, o_ref):
        o_ref[...] = _silu(_conv(_taps(x_ref[...], k_w), w_ref[...]))

    return pl.pallas_call(
        body, name="dn_pre_fwd", grid=(n_ct, n_b),
        in_specs=[pl.BlockSpec((t_len, 128), lambda j, b: (b, _qkv_block(j))),
                  pl.BlockSpec((k_w, 128), lambda j, b: (0, j))],
        out_specs=pl.BlockSpec((t_len, 128), lambda j, b: (b, j)),
        out_shape=jax.ShapeDtypeStruct((n_b * t_len, n_ct * 128), F32),
        compiler_params=_cparams(("parallel", "parallel")),
    )(proj, conv_w)


def _dn_pre_bwd(proj, conv_w, dqkv, d_proj, n_b, t_len):
    n_ct = 3 * HEADS
    k_w = conv_w.shape[0]

    def body(x_ref, w_ref, d_ref, _, dx_ref, dw_ref):
        taps, w = _taps(x_ref[...], k_w), w_ref[...]
        du = d_ref[...] * _dsilu(_conv(taps, w))
        dx, dw = _conv_bwd(taps, w, du)
        dx_ref[...] = dx.astype(BF16)
        _acc_rows(dw_ref, dw, pl.program_id(1) == 0)

    return pl.pallas_call(
        body, name="dn_pre_bwd", grid=(n_ct, n_b),
        in_specs=[pl.BlockSpec((t_len, 128), lambda j, b: (b, _qkv_block(j))),
                  pl.BlockSpec((k_w, 128), lambda j, b: (0, j)),
                  pl.BlockSpec((t_len, 128), lambda j, b: (b, j)), pl.BlockSpec(memory_space=pl.ANY)],
        out_specs=[pl.BlockSpec((t_len, 128), lambda j, b: (b, _qkv_block(j))),
                   pl.BlockSpec((k_w, 128), lambda j, b: (0, j))],
        out_shape=[jax.ShapeDtypeStruct(d_proj.shape, BF16), jax.ShapeDtypeStruct((k_w, n_ct * 128), F32)],
        input_output_aliases={3: 0},
        compiler_params=_cparams(("parallel", "arbitrary")),
    )(proj, conv_w, dqkv, d_proj)


FFN_TC = 256
FFN_NT = D_FF // FFN_TC


def _ffn_specs(t_len):
    blk = lambda off: pl.BlockSpec((t_len, FFN_TC), lambda j, b: (b, j + off))
    wblk = lambda off: pl.BlockSpec((3, FFN_TC), lambda j, b: (0, j + off))
    bblk = lambda off: pl.BlockSpec((1, FFN_TC), lambda j, b: (0, j + off))
    return [blk(0), blk(FFN_NT), wblk(0), wblk(FFN_NT), bblk(0), bblk(FFN_NT)]


def _ffn_act_fwd(up, conv_w, conv_b, n_b, t_len):
    def body(g_ref, v_ref, wg_ref, wv_ref, bg_ref, bv_ref, o_ref):
        ug = _conv(_taps(g_ref[...], 3), wg_ref[...]) + bg_ref[...]
        uv = _conv(_taps(v_ref[...], 3), wv_ref[...]) + bv_ref[...]
        o_ref[...] = (_silu(ug) * uv).astype(BF16)

    return pl.pallas_call(
        body, name="ffn_act_fwd", grid=(FFN_NT, n_b), in_specs=_ffn_specs(t_len),
        out_specs=pl.BlockSpec((t_len, FFN_TC), lambda j, b: (b, j)),
        out_shape=jax.ShapeDtypeStruct((n_b * t_len, D_FF), BF16),
        compiler_params=_cparams(("parallel", "parallel")),
    )(up, up, conv_w, conv_w, conv_b, conv_b)


def _ffn_act_bwd(up, conv_w, conv_b, da, n_b, t_len):
    def body(g_ref, v_ref, wg_ref, wv_ref, bg_ref, bv_ref, da_ref, dup_ref, dw_ref, db_ref):
        first = pl.program_id(1) == 0
        tg, tv, wg, wv = _taps(g_ref[...], 3), _taps(v_ref[...], 3), wg_ref[...], wv_ref[...]
        ug = _conv(tg, wg) + bg_ref[...]
        uv = _conv(tv, wv) + bv_ref[...]
        d_act = da_ref[...]
        sig = jax.nn.sigmoid(ug)
        d_v = d_act * (ug * sig)
        d_g = d_act * uv * (sig * (1.0 + ug * (1.0 - sig)))
        for slab, (taps, w, du) in enumerate(((tg, wg, d_g), (tv, wv, d_v))):
            dx, dw = _conv_bwd(taps, w, du)
            dup_ref[slab] = dx.astype(BF16)
            for k, dw_k in enumerate(dw):
                _acc(dw_ref, dw_k, first, at=(slab, slice(k, k + 1), slice(None)))
            _acc(db_ref, jnp.sum(du, 0, keepdims=True), first, at=(slab, slice(None), slice(None)))

    return pl.pallas_call(
        body, name="ffn_act_bwd", grid=(FFN_NT, n_b),
        in_specs=_ffn_specs(t_len) + [pl.BlockSpec((t_len, FFN_TC), lambda j, b: (b, j))],
        out_specs=[pl.BlockSpec((2, t_len, FFN_TC), lambda j, b: (0, b, j)),
                   pl.BlockSpec((2, 3, FFN_TC), lambda j, b: (0, 0, j)),
                   pl.BlockSpec((2, 1, FFN_TC), lambda j, b: (0, 0, j))],
        out_shape=[jax.ShapeDtypeStruct((2, n_b * t_len, D_FF), BF16),
                   jax.ShapeDtypeStruct((2, 3, D_FF), F32), jax.ShapeDtypeStruct((2, 1, D_FF), F32)],
        compiler_params=_cparams(("parallel", "arbitrary")),
    )(up, up, conv_w, conv_w, conv_b, conv_b, da)


NN = (((2,), (1,)), ((0,), (0,)))
NT = (((2,), (2,)), ((0,), (0,)))
TN = (((1,), (1,)), ((0,), (0,)))


def _iota3(shape, axis):
    return lax.broadcasted_iota(jnp.int32, shape, axis)


def _dg(a, b, dims):
    return lax.dot_general(a, b, dims, preferred_element_type=F32)


def _dot(a, b):
    return _dg(a, b, NN)


def _dot_nt(a, b):
    return _dg(a, b, NT)


def _dot_tn(a, b):
    return _dg(a, b, TN)


def _split(a):
    hi = a.astype(BF16)
    return hi, (a - hi.astype(F32)).astype(BF16)


def _dg3(a, b, dims):
    ah, al = _split(a)
    bh, bl = _split(b)
    return _dg(ah, bh, dims) + (_dg(ah, bl, dims) + _dg(al, bh, dims))


@jax.custom_vjp
def _dot3(a, b):
    return _dg3(a, b, NN)


def _dot3_fwd(a, b):
    return _dg3(a, b, NN), (a, b)


def _dot3_bwd(res, g):
    a, b = res
    return _dg3(g, b, NT), _dg3(a, g, TN)


_dot3.defvjp(_dot3_fwd, _dot3_bwd)


def _lower_ones(g_n, n):
    shape = (g_n, n, n)
    return jnp.where(_iota3(shape, 1) >= _iota3(shape, 2), 1.0, 0.0).astype(BF16)


@jax.custom_vjp
def _chunk_cumsum(x):
    hi, lo = _split(x)
    tri = _lower_ones(x.shape[0], x.shape[1])
    return _dg(tri, hi, NN) + _dg(tri, lo, NN)


def _chunk_cumsum_fwd(x):
    return _chunk_cumsum(x), None


def _chunk_cumsum_bwd(_, g):
    hi, lo = _split(g)
    tri = _lower_ones(g.shape[0], g.shape[1])
    return (_dg(tri, hi, TN) + _dg(tri, lo, TN),)


_chunk_cumsum.defvjp(_chunk_cumsum_fwd, _chunk_cumsum_bwd)


@jax.custom_vjp
def _unit_lower_inv(m):
    n = m.shape[1]
    p = -m
    a = jnp.where(_iota3(m.shape, 1) == _iota3(m.shape, 2), 1.0, 0.0) + p
    span = 2
    while span < n:
        p = _dg3(p, p, NN)
        a = a + _dg3(a, p, NN)
        span *= 2
    return a


def _unit_lower_inv_fwd(m):
    a = _unit_lower_inv(m)
    return a, a


def _unit_lower_inv_bwd(a, da):
    return (-_dg3(a, _dg3(da, a, NT), TN),)


_unit_lower_inv.defvjp(_unit_lower_inv_fwd, _unit_lower_inv_bwd)


@jax.custom_vjp
def _saved_lower_inv(m, a):
    return a


def _saved_lower_inv_fwd(m, a):
    return a, a


def _saved_lower_inv_bwd(a, da):
    return _unit_lower_inv_bwd(a, da)[0], jnp.zeros_like(a)


_saved_lower_inv.defvjp(_saved_lower_inv_fwd, _saved_lower_inv_bwd)


def _rms_gate(o, gn, gate):
    return o * lax.rsqrt(jnp.mean(o * o, -1, keepdims=True) + EPS) * gn * _silu(gate)


def _dn_chains(q, k, v, z, small, s_in, a_log, dt_bias, gn, a_saved=None):
    g_n, c_len = q.shape[0], q.shape[1]
    sq = (g_n, c_len, c_len)
    row, col = _iota3(sq, 1), _iota3(sq, 2)
    causal, strict, eye = row >= col, row > col, row == col
    qn = q * lax.rsqrt(jnp.sum(q * q, -1, keepdims=True) + EPS) * (HEAD_DIM ** -0.5)
    kn = k * lax.rsqrt(jnp.sum(k * k, -1, keepdims=True) + EPS)
    lane = _iota3(small.shape, 2)
    head = jnp.bitwise_and(_iota3(small.shape, 0), HEADS - 1)
    la_all = -jnp.exp(a_log) * _softplus(small + dt_bias)
    la_c = jnp.sum(jnp.where(lane == head, la_all, 0.0), 2, keepdims=True)
    beta = jnp.sum(jnp.where(lane == head + HEADS, jax.nn.sigmoid(small), 0.0), 2, keepdims=True)
    la_b = jnp.broadcast_to(la_c, sq)
    la_r = jnp.sum(jnp.where(eye, la_b, 0.0), 1, keepdims=True)
    g_c = jnp.sum(jnp.where(causal, jnp.broadcast_to(la_r, sq), 0.0), 2, keepdims=True)
    g_r = jnp.sum(jnp.where(row <= col, la_b, 0.0), 1, keepdims=True)
    g_last = jnp.sum(la_c, 1, keepdims=True)
    decay = jnp.exp(jnp.where(causal, g_c - g_r, -1e30))
    e_g = jnp.exp(g_c)
    kb = kn * beta
    m_low = jnp.where(strict, _dot_nt(kb, kn) * decay, 0.0)
    a_inv = _unit_lower_inv(m_low) if a_saved is None else _saved_lower_inv(m_low, a_saved)
    u = _dot3(a_inv, v * beta)
    w = _dot3(a_inv, kb * e_g)
    attn = _dot_nt(qn, kn) * decay
    v_new = u - _dot(w, s_in)
    o = _dot(qn * e_g, s_in) + _dot(attn, v_new)
    s_out = s_in * jnp.exp(g_last) + _dot_tn(kn * jnp.exp(g_last - g_c), v_new)
    return _rms_gate(o, gn, z), s_out, a_inv


def _gla_chains(q, k, v, gate, small, s_in, w2, b2, gn):
    g_n, c_len = q.shape[0], q.shape[1]
    sq, kk = (g_n, c_len, c_len), (g_n, GLA_KEY, GLA_KEY)
    causal = _iota3(sq, 1) >= _iota3(sq, 2)
    la = -_softplus(-(_dot(small, w2) + b2)) * (1.0 / 16.0)
    b = _chunk_cumsum(la)
    b_last = jnp.sum(jnp.where(_iota3(b.shape, 1) == c_len - 1, b, 0.0), 1, keepdims=True)
    q_dec = q * (GLA_KEY ** -0.5) * jnp.exp(b)
    attn = jnp.where(causal, _dot_nt(q_dec, k * jnp.exp(-b)), 0.0)
    o = _dot(q_dec, s_in) + _dot(attn, v)
    g_row = jnp.exp(b_last)
    g_col = jnp.sum(jnp.where(_iota3(kk, 1) == _iota3(kk, 2), jnp.broadcast_to(g_row, kk), 0.0), 2, keepdims=True)
    s_out = s_in * g_col + _dot_tn(k * jnp.exp(b_last - b), v)
    return _rms_gate(o, gn, gate), s_out


def _chunk_spec(n_b, width, col_block, n_c, reverse=False):
    if reverse:
        return pl.BlockSpec((n_b, CHUNK, width), lambda n: (0, n_c - 1 - n, col_block))
    return pl.BlockSpec((n_b, CHUNK, width), lambda n: (0, n, col_block))


def _hist_spec(n_b, d_k, n_c, reverse=False):
    if reverse:
        return pl.BlockSpec((None, n_b * HEADS, d_k, HEAD_DIM), lambda n: (n_c - 1 - n, 0, 0, 0))
    return pl.BlockSpec((None, n_b * HEADS, d_k, HEAD_DIM), lambda n: (n, 0, 0, 0))


def _ainv_spec(n_b, n_c, reverse=False):
    if reverse:
        return pl.BlockSpec((None, n_b * HEADS, CHUNK, CHUNK), lambda n: (n_c - 1 - n, 0, 0, 0))
    return pl.BlockSpec((None, n_b * HEADS, CHUNK, CHUNK), lambda n: (n, 0, 0, 0))


def _stack_chains(ref, n_b, slices):
    return jnp.stack([ref[b, :, sl] for b in range(n_b) for sl in slices], axis=0)


def _per_chain(ref, n_b):
    return jnp.stack([ref[b] for b in range(n_b) for _ in range(HEADS)], axis=0)


def _unstack_chains(ref, val, n_b, slices, offset=0):
    for b in range(n_b):
        for h, sl in enumerate(slices):
            ref[b, :, slice(offset + sl.start, offset + sl.stop)] = val[b * HEADS + h].astype(ref.dtype)


def _gate_weights(w2_ref, b2_ref, n_b):
    w2 = jnp.stack([w2_ref[:, ks] for _ in range(n_b) for ks in GLA_KSL], axis=0)
    b2 = jnp.stack([b2_ref[:, ks] for _ in range(n_b) for ks in GLA_KSL], axis=0)
    return w2, b2


def _sum_heads(val, n_b):
    return [sum(val[b * HEADS + h] for h in range(HEADS)) for b in range(n_b)]


def _const_spec(shape):
    return pl.BlockSpec(shape, lambda n: (0,) * len(shape))


DN_SL = [slice(h * HEAD_DIM, (h + 1) * HEAD_DIM) for h in range(HEADS)]
GLA_KSL = [slice(h * GLA_KEY, (h + 1) * GLA_KEY) for h in range(HEADS)]


class Rider:
    def __init__(self, inputs, out_shapes, sems, first, last):
        self.inputs, self.out_shapes, self.sems, self.first, self.last = inputs, out_shapes, sems, first, last


def _with_rider(rider, n_in, n_out, n_scratch):
    if rider is None:
        return [], [], [], [], lambda refs: (refs, None)
    r_in, r_out, r_sem = len(rider.inputs), len(rider.out_shapes), len(rider.sems)

    def split(refs):
        own_in, rest = refs[:n_in], refs[n_in:]
        rid_in, rest = rest[:r_in], rest[r_in:]
        own_out, rest = rest[:n_out], rest[n_out:]
        rid_out, rest = rest[:r_out], rest[r_out:]
        own_scr, rid_sem = rest[:n_scratch], rest[n_scratch:]
        return own_in + own_out + own_scr, (rid_in, rid_out, rid_sem)

    return list(rider.inputs), [HBM_SPEC] * r_in, [HBM_SPEC] * r_out, list(rider.sems), split


def _ride(rider, parts, grid):
    if rider is None:
        return None, None
    grid = grid if isinstance(grid, tuple) else (grid,)

    def at(step_of):
        hit = pl.program_id(0) == step_of(grid[0])
        for axis in range(1, len(grid)):
            hit = jnp.logical_and(hit, pl.program_id(axis) == step_of(grid[axis]))
        return hit

    def first():
        pl.when(at(lambda n: 0))(lambda: rider.first(*parts))

    def last():
        pl.when(at(lambda n: n - 1))(lambda: rider.last(*parts))

    return first, last


def _dn_scan_fwd(qkv, proj, a_log, dt_bias, gn, n_b, t_len, rider=None):
    n_c = t_len // CHUNK
    spec = functools.partial(_chunk_spec, n_b, n_c=n_c)
    r_inputs, r_in_specs, r_out_specs, r_sems, split = _with_rider(rider, 8, 3, 1)

    def body(*refs):
        (q_ref, k_ref, v_ref, z_ref, sm_ref, al_ref, dt_ref, gn_ref,
         o_ref, hist_ref, ainv_ref, s_ref), parts = split(refs)
        ride_first, ride_last = _ride(rider, parts, n_c)
        if rider is not None:
            ride_first()

        @pl.when(pl.program_id(0) == 0)
        def _():
            s_ref[...] = jnp.zeros_like(s_ref)

        s_in = s_ref[...]
        hist_ref[...] = s_in
        og, s_out, a_inv = _dn_chains(*(_stack_chains(r, n_b, DN_SL) for r in (q_ref, k_ref, v_ref, z_ref)),
                                      _per_chain(sm_ref, n_b), s_in, al_ref[...], dt_ref[...], gn_ref[...])
        _unstack_chains(o_ref, og, n_b, DN_SL)
        s_ref[...] = s_out
        ainv_ref[...] = a_inv
        if rider is not None:
            ride_last()

    qkv3, proj3 = qkv.reshape(n_b, t_len, -1), proj.reshape(n_b, t_len, -1)
    o, hist, ainv, *rider_outs = pl.pallas_call(
        body, name="dn_scan_fwd", grid=(n_c,),
        in_specs=[spec(512, 0), spec(512, 1), spec(512, 2), spec(512, OFF_Z // 512), spec(128, OFF_SMALL // 128),
                  _const_spec((1, 128)), _const_spec((1, 128)), _const_spec((1, 128))] + r_in_specs,
        out_specs=[spec(512, 0), _hist_spec(n_b, HEAD_DIM, n_c), _ainv_spec(n_b, n_c)] + r_out_specs,
        out_shape=[jax.ShapeDtypeStruct((n_b, t_len, 2 * 512), BF16),
                   jax.ShapeDtypeStruct((n_c, n_b * HEADS, HEAD_DIM, HEAD_DIM), F32),
                   jax.ShapeDtypeStruct((n_c, n_b * HEADS, CHUNK, CHUNK), F32)]
        + (list(rider.out_shapes) if rider else []),
        scratch_shapes=[pltpu.VMEM((n_b * HEADS, HEAD_DIM, HEAD_DIM), F32)] + r_sems,
        compiler_params=_cparams(("arbitrary",)),
    )(qkv3, qkv3, qkv3, proj3, proj3, a_log, dt_bias, gn, *r_inputs)
    return o, (hist, ainv), rider_outs


def _dn_scan_bwd(qkv, proj, a_log, dt_bias, gn, hist, d_o, n_b, t_len, rider=None):
    n_c = t_len // CHUNK
    rev = functools.partial(_chunk_spec, n_b, n_c=n_c, reverse=True)
    r_inputs, r_in_specs, r_out_specs, r_sems, split = _with_rider(rider, 11, 6, 1)
    hist, ainv = hist

    def body(*refs):
        (q_ref, k_ref, v_ref, z_ref, sm_ref, al_ref, dt_ref, gn_ref, hist_ref, ainv_ref, do_ref,
         dqkv_ref, dz_ref, dsm_ref, dal_ref, ddt_ref, dgn_ref, ds_ref), parts = split(refs)
        ride_first, ride_last = _ride(rider, parts, n_c)
        if rider is not None:
            ride_first()
        first = pl.program_id(0) == 0

        @pl.when(first)
        def _():
            ds_ref[...] = jnp.zeros_like(ds_ref)

        chains = lambda *a: _dn_chains(*a, a_saved=ainv_ref[...])[:2]
        _, pull = jax.vjp(chains, *(_stack_chains(r, n_b, DN_SL) for r in (q_ref, k_ref, v_ref, z_ref)),
                          _per_chain(sm_ref, n_b), hist_ref[...], al_ref[...], dt_ref[...], gn_ref[...])
        dq, dk, dv, dz, dsm, ds_in, dal, ddt, dgn = pull((_stack_chains(do_ref, n_b, DN_SL), ds_ref[...]))
        _unstack_chains(dqkv_ref, dq, n_b, DN_SL)
        _unstack_chains(dqkv_ref, dk, n_b, DN_SL, offset=512)
        _unstack_chains(dqkv_ref, dv, n_b, DN_SL, offset=1024)
        _unstack_chains(dz_ref, dz, n_b, DN_SL)
        ds_ref[...] = ds_in
        for b, dsm_b in enumerate(_sum_heads(dsm, n_b)):
            dsm_ref[b] = dsm_b
        _acc(dal_ref, dal, first)
        _acc(ddt_ref, ddt, first)
        _acc(dgn_ref, dgn, first)
        if rider is not None:
            ride_last()

    qkv3, proj3, do3 = (a.reshape(n_b, t_len, -1) for a in (qkv, proj, d_o))
    vec = jax.ShapeDtypeStruct((1, 128), F32)
    dqkv, d_proj, dsm, dal, ddt, dgn, *rider_outs = pl.pallas_call(
        body, name="dn_scan_bwd", grid=(n_c,),
        in_specs=[rev(512, 0), rev(512, 1), rev(512, 2), rev(512, OFF_Z // 512), rev(128, OFF_SMALL // 128),
                  _const_spec((1, 128)), _const_spec((1, 128)), _const_spec((1, 128)),
                  _hist_spec(n_b, HEAD_DIM, n_c, reverse=True), _ainv_spec(n_b, n_c, reverse=True),
                  rev(512, 0)] + r_in_specs,
        out_specs=[rev(1536, 0), rev(512, OFF_Z // 512), rev(128, 0),
                   _const_spec((1, 128)), _const_spec((1, 128)), _const_spec((1, 128))] + r_out_specs,
        out_shape=[jax.ShapeDtypeStruct((n_b, t_len, 1536), F32), jax.ShapeDtypeStruct((n_b, t_len, PROJ_W), BF16),
                   jax.ShapeDtypeStruct((n_b, t_len, 128), F32), vec, vec, vec]
        + (list(rider.out_shapes) if rider else []),
        scratch_shapes=[pltpu.VMEM((n_b * HEADS, HEAD_DIM, HEAD_DIM), F32)] + r_sems,
        compiler_params=_cparams(("arbitrary",)),
    )(qkv3, qkv3, qkv3, proj3, proj3, a_log, dt_bias, gn, hist, ainv, do3, *r_inputs)
    return dqkv.reshape(n_b * t_len, 1536), d_proj, dsm, dal, ddt, dgn, rider_outs


def _gla_scan_fwd(proj, w2, b2, gn, o_mix, n_b, t_len):
    n_c = t_len // CHUNK
    spec = functools.partial(_chunk_spec, n_b, n_c=n_c)

    def body(q_ref, k_ref, v_ref, g_ref, sm_ref, w2_ref, b2_ref, gn_ref, _, o_ref, hist_ref, s_ref):
        @pl.when(pl.program_id(0) == 0)
        def _():
            s_ref[...] = jnp.zeros_like(s_ref)

        s_in = s_ref[...]
        hist_ref[...] = s_in
        og, s_out = _gla_chains(_stack_chains(q_ref, n_b, GLA_KSL), _stack_chains(k_ref, n_b, GLA_KSL),
                                _stack_chains(v_ref, n_b, DN_SL), _stack_chains(g_ref, n_b, DN_SL),
                                _per_chain(sm_ref, n_b), s_in, *_gate_weights(w2_ref, b2_ref, n_b), gn_ref[...])
        _unstack_chains(o_ref, og, n_b, DN_SL)
        s_ref[...] = s_out

    proj3 = proj.reshape(n_b, t_len, -1)
    o, hist = pl.pallas_call(
        body, name="gla_scan_fwd", grid=(n_c,),
        in_specs=[spec(256, OFF_GQ // 256), spec(256, OFF_GK // 256), spec(512, OFF_GV // 512),
                  spec(512, OFF_GG // 512), spec(128, OFF_SMALL // 128),
                  _const_spec((128, 256)), _const_spec((1, 256)), _const_spec((1, 128)),
                  pl.BlockSpec(memory_space=pl.ANY)],
        out_specs=[spec(512, 1), _hist_spec(n_b, GLA_KEY, n_c)],
        out_shape=[jax.ShapeDtypeStruct(o_mix.shape, BF16),
                   jax.ShapeDtypeStruct((n_c, n_b * HEADS, GLA_KEY, HEAD_DIM), F32)],
        input_output_aliases={8: 0},
        scratch_shapes=[pltpu.VMEM((n_b * HEADS, GLA_KEY, HEAD_DIM), F32)],
        compiler_params=_cparams(("arbitrary",)),
    )(proj3, proj3, proj3, proj3, proj3, w2, b2, gn, o_mix)
    return o.reshape(n_b * t_len, 2 * 512), hist


def _gla_scan_bwd(proj, w2, b2, gn, hist, d_o, dsm_dn, d_proj, n_b, t_len):
    n_c = t_len // CHUNK
    rev = functools.partial(_chunk_spec, n_b, n_c=n_c, reverse=True)

    def body(q_ref, k_ref, v_ref, g_ref, sm_ref, w2_ref, b2_ref, gn_ref, hist_ref, do_ref, dsm_dn_ref, _,
             dp_ref, dw2_ref, db2_ref, dgn_ref, ds_ref):
        first = pl.program_id(0) == 0

        @pl.when(first)
        def _():
            ds_ref[...] = jnp.zeros_like(ds_ref)

        _, pull = jax.vjp(_gla_chains, _stack_chains(q_ref, n_b, GLA_KSL), _stack_chains(k_ref, n_b, GLA_KSL),
                          _stack_chains(v_ref, n_b, DN_SL), _stack_chains(g_ref, n_b, DN_SL),
                          _per_chain(sm_ref, n_b), hist_ref[...], *_gate_weights(w2_ref, b2_ref, n_b), gn_ref[...])
        dq, dk, dv, dg, dsm, ds_in, dw2, db2, dgn = pull((_stack_chains(do_ref, n_b, DN_SL), ds_ref[...]))
        _unstack_chains(dp_ref, dq, n_b, GLA_KSL, offset=OFF_GQ)
        _unstack_chains(dp_ref, dk, n_b, GLA_KSL, offset=OFF_GK)
        _unstack_chains(dp_ref, dv, n_b, DN_SL, offset=OFF_GV)
        _unstack_chains(dp_ref, dg, n_b, DN_SL, offset=OFF_GG)
        ds_ref[...] = ds_in
        for b, dsm_b in enumerate(_sum_heads(dsm, n_b)):
            dp_ref[b, :, OFF_SMALL:OFF_SMALL + 128] = (dsm_b + dsm_dn_ref[b]).astype(BF16)
            dp_ref[b, :, OFF_SMALL + 128:GLA_W] = jnp.zeros((CHUNK, GLA_W - OFF_SMALL - 128), BF16)
        for h, ks in enumerate(GLA_KSL):
            _acc(dw2_ref, sum(dw2[b * HEADS + h] for b in range(n_b)), first, at=(slice(None), ks))
            _acc(db2_ref, sum(db2[b * HEADS + h] for b in range(n_b)), first, at=(slice(None), ks))
        _acc(dgn_ref, dgn, first)

    proj3, do3 = proj.reshape(n_b, t_len, -1), d_o.reshape(n_b, t_len, -1)
    return pl.pallas_call(
        body, name="gla_scan_bwd", grid=(n_c,),
        in_specs=[rev(256, OFF_GQ // 256), rev(256, OFF_GK // 256), rev(512, OFF_GV // 512), rev(512, OFF_GG // 512),
                  rev(128, OFF_SMALL // 128),
                  _const_spec((128, 256)), _const_spec((1, 256)), _const_spec((1, 128)),
                  _hist_spec(n_b, GLA_KEY, n_c, reverse=True), rev(512, 1), rev(128, 0),
                  pl.BlockSpec(memory_space=pl.ANY)],
        out_specs=[rev(GLA_W, 0), _const_spec((128, 256)), _const_spec((1, 256)), _const_spec((1, 128))],
        out_shape=[jax.ShapeDtypeStruct(d_proj.shape, BF16), jax.ShapeDtypeStruct((128, 256), F32),
                   jax.ShapeDtypeStruct((1, 256), F32), jax.ShapeDtypeStruct((1, 128), F32)],
        input_output_aliases={11: 0},
        scratch_shapes=[pltpu.VMEM((n_b * HEADS, GLA_KEY, HEAD_DIM), F32)],
        compiler_params=_cparams(("arbitrary",)),
    )(proj3, proj3, proj3, proj3, proj3, w2, b2, gn, hist, do3, dsm_dn, d_proj)


W_IN_RUNS = ((0, 256, GLA_W), (256, 1536, OFF_Z + 512), (1536, 2048, OFF_Z), (2048, 2056, OFF_SMALL),
             (2056, 3592, 0), (3592, 3608, OFF_SMALL + 8))
W_IN_ROWS = 256


def _w_in_pieces(cols_per_chip):
    out = []
    for first, last, start in W_IN_RUNS:
        for j in range(N_CHIPS):
            a, b = max(first, cols_per_chip * j), min(last, cols_per_chip * (j + 1))
            if a < b:
                out.append((j, a - cols_per_chip * j, b - cols_per_chip * j, start + a - first))
    return out


def _w_in_to_padded(w4):
    _, n_r, n_c = w4.shape

    def body(i_ref, o_ref):
        o_ref[...] = jnp.zeros_like(o_ref)
        for j, a, b, p in _w_in_pieces(n_c):
            o_ref[:, p:p + b - a] = i_ref[j, :, a:b]

    return pl.pallas_call(
        body, name="w_in_to_padded", grid=(n_r // W_IN_ROWS,),
        in_specs=[pl.BlockSpec((N_CHIPS, W_IN_ROWS, n_c), lambda i: (0, i, 0))],
        out_specs=pl.BlockSpec((W_IN_ROWS, PROJ_W), lambda i: (i, 0)),
        out_shape=jax.ShapeDtypeStruct((n_r, PROJ_W), w4.dtype), compiler_params=_cparams(("parallel",)),
    )(w4)


def _w_in_to_chips(g, n_c):
    n_r = g.shape[0]

    def body(i_ref, o_ref):
        for j, a, b, p in _w_in_pieces(n_c):
            o_ref[j, :, a:b] = i_ref[:, p:p + b - a]

    return pl.pallas_call(
        body, name="w_in_to_chips", grid=(n_r // W_IN_ROWS,),
        in_specs=[pl.BlockSpec((W_IN_ROWS, PROJ_W), lambda i: (i, 0))],
        out_specs=pl.BlockSpec((N_CHIPS, W_IN_ROWS, n_c), lambda i: (0, i, 0)),
        out_shape=jax.ShapeDtypeStruct((N_CHIPS, n_r, n_c), g.dtype), compiler_params=_cparams(("parallel",)),
    )(g)


def _lane_vec(v, offset=0):
    return jnp.zeros((1, 128), F32).at[0, offset:offset + v.shape[0]].set(v)


def _local_step(x, tgt, mod, p, n_b, t_len, comm=None):
    row1 = lambda v: v.reshape(1, -1)
    a_log, dt_bias = _lane_vec(p["dn_a_log"]), _lane_vec(p["dn_dt_bias"])
    dn_gn, gla_gn = row1(p["dn_norm_g"]), row1(p["gla_norm_g"])
    w2 = jnp.zeros((128, 256), F32).at[8:8 + GATE_RANK].set(p["gla_w_gate2"])
    b2 = row1(p["gla_b_gate"])
    ln0_g, ln0_b, ln1_g, ln1_b, ln2_g, ln2_b = (row1(p[k]) for k in ("ln0_g", "ln0_b", "ln1_g", "ln1_b", "ln2_g", "ln2_b"))
    conv_b = row1(p["ffn_conv_b"])

    x0, h1 = _ln0_fwd(x, ln0_g, ln0_b, mod, n_b, t_len)
    proj = _mm(h1, p["w_in_p"], name="mm_proj")
    qkv = _dn_pre_fwd(proj, p["dn_conv"], n_b, t_len)
    o_half, hist_dn, landed = _dn_scan_fwd(qkv, proj, a_log, dt_bias, dn_gn, n_b, t_len,
                                           rider=comm.fwd_rider() if comm else None)
    if comm:
        p = {**p, **comm.weights_from(landed)}
    o_mix, hist_gla = _gla_scan_fwd(proj, w2, b2, gla_gn, o_half, n_b, t_len)
    y = _mm(o_mix, p["w_o"], name="mm_wo")
    x1, h2 = _ln1_fwd(x0, y, ln1_g, ln1_b, mod, n_b, t_len)
    up = _mm(h2, p["w_up"], name="mm_up")
    act = _ffn_act_fwd(up, p["ffn_conv"], conv_b, n_b, t_len)
    y2 = _mm(act, p["w_down"], name="mm_down")

    loss, dx1, dy2, g_ln2_g, g_ln2_b, dgt_f = _ln2_loss_bwd(x1, y2, ln2_g, ln2_b, mod, tgt, n_b, t_len)
    g_w_down = _mm(act, dy2, ta=True, name="mm_g_down")
    d_act = _mm(dy2, p["w_down"], tb=True, name="mm_d_act")
    d_up, g_ffn_conv, g_conv_b = _ffn_act_bwd(up, p["ffn_conv"], conv_b, d_act, n_b, t_len)
    g_w_up = _mm(h2, d_up, ta=True, out_slabs=N_CHIPS, name="mm_g_up")
    if comm:
        dh2, from_sibling = _mm(d_up, p["w_up"], tb=True, name="mm_d_h2", rider=comm.ffn_pair_rider(g_w_up, g_w_down))
    else:
        dh2 = _mm(d_up, p["w_up"], tb=True, name="mm_d_h2")
    dx0, dy, g_ln1_g, g_ln1_b, dmod_1 = _ln1_bwd(x0, y, ln1_g, ln1_b, mod, dx1, dh2, n_b, t_len)
    g_w_o = _mm(o_mix, dy, ta=True, name="mm_g_wo")
    d_o = _mm(dy, p["w_o"], tb=True, name="mm_d_o")
    dqkv, d_proj, dsm_dn, g_a_log, g_dt_bias, g_dn_gn, ffn_from_chips = _dn_scan_bwd(
        qkv, proj, a_log, dt_bias, dn_gn, hist_dn, d_o, n_b, t_len,
        rider=comm.ffn_chips_rider(from_sibling) if comm else None)
    d_proj, g_w2, g_b2, g_gla_gn = _gla_scan_bwd(proj, w2, b2, gla_gn, hist_gla, d_o, dsm_dn, d_proj, n_b, t_len)
    d_proj, g_dn_conv = _dn_pre_bwd(proj, p["dn_conv"], dqkv, d_proj.reshape(n_b * t_len, PROJ_W), n_b, t_len)
    g_w_in_p = _mm(h1, d_proj, ta=True, name="mm_g_win")
    if comm:
        dh1, tail_from_chips = _mm(d_proj, p["w_in_p"], tb=True, name="mm_d_h1",
                                   rider=comm.tail_chips_rider(g_w_in_p, g_w_o))
        from_chips = (ffn_from_chips, tail_from_chips)
    else:
        dh1, from_chips = _mm(d_proj, p["w_in_p"], tb=True, name="mm_d_h1"), None
    grad_x, g_ln0_g, g_ln0_b, dmod_0 = _ln0_bwd(x, ln0_g, ln0_b, mod, dx0, dh1, n_b, t_len)

    dmod = jnp.concatenate([dmod_0, dmod_1[:, 0:1], dmod_1[:, 1:3], dgt_f], axis=1)
    grads = {
        "ln0_g": g_ln0_g[0], "ln0_b": g_ln0_b[0], "w_in_p": g_w_in_p, "dn_conv": g_dn_conv,
        "dn_a_log": g_a_log[0, 0:HEADS], "dn_dt_bias": g_dt_bias[0, 0:HEADS], "dn_norm_g": g_dn_gn[0],
        "gla_w_gate2": g_w2[8:8 + GATE_RANK], "gla_b_gate": g_b2[0], "gla_norm_g": g_gla_gn[0],
        "w_o": g_w_o, "ln1_g": g_ln1_g[0], "ln1_b": g_ln1_b[0], "w_up": g_w_up,
        "ffn_conv": jnp.concatenate([g_ffn_conv[0], g_ffn_conv[1]], axis=1),
        "ffn_conv_b": jnp.concatenate([g_conv_b[0, 0], g_conv_b[1, 0]]), "w_down": g_w_down,
        "ln2_g": g_ln2_g[0], "ln2_b": g_ln2_b[0],
    }
    return loss, grad_x, grads, dmod, from_chips


def _ada_fwd(c_all, w_shard, b_shard):
    n_all, n_col = c_all.shape[0], w_shard.shape[1]
    tn = 512

    def body(c_ref, w_ref, b_ref, cond_ref, mod_ref):
        cond = _silu(c_ref[...])
        cond_ref[...] = cond
        mod_ref[...] = jnp.dot(cond.astype(BF16), w_ref[...].astype(BF16), preferred_element_type=F32) + b_ref[...]

    return pl.pallas_call(
        body, name="ada_fwd", grid=(n_col // tn,),
        in_specs=[pl.BlockSpec((n_all, D_MODEL), lambda j: (0, 0)), pl.BlockSpec((D_MODEL, tn), lambda j: (0, j)),
                  pl.BlockSpec((1, tn), lambda j: (0, j))],
        out_specs=[pl.BlockSpec((n_all, D_MODEL), lambda j: (0, 0)), pl.BlockSpec((n_all, tn), lambda j: (0, j))],
        out_shape=[jax.ShapeDtypeStruct((n_all, D_MODEL), F32), jax.ShapeDtypeStruct((n_all, n_col), F32)],
        compiler_params=_cparams(("arbitrary",)),
    )(c_all, w_shard, b_shard)


def _col_sum(a):
    def body(a_ref, o_ref):
        o_ref[...] = jnp.sum(a_ref[...], 0, keepdims=True)

    return pl.pallas_call(body, name="col_sum", out_shape=jax.ShapeDtypeStruct((1, a.shape[1]), F32))(a)


def _adamw_math(w, grad, m, v):
    new_m = ADAM_B1 * m + (1.0 - ADAM_B1) * grad
    new_v = ADAM_B2 * v + (1.0 - ADAM_B2) * (grad * grad)
    m_hat = new_m / (1.0 - ADAM_B1 ** ADAM_STEP)
    v_hat = new_v / (1.0 - ADAM_B2 ** ADAM_STEP)
    return -ADAM_LR * (m_hat / (jnp.sqrt(v_hat) + ADAM_EPS) + ADAM_WD * w), new_m, new_v


def _adamw_many(ws, gs, ms, vs):
    n = len(ws)

    def body(*refs):
        for i in range(n):
            w_ref, g_ref, m_ref, v_ref = (refs[k * n + i] for k in range(4))
            d_ref, nm_ref, nv_ref = (refs[(4 + k) * n + i] for k in range(3))
            d_ref[...], nm_ref[...], nv_ref[...] = _adamw_math(w_ref[...], g_ref[...], m_ref[...], v_ref[...])

    outs = pl.pallas_call(
        body, name="adamw_small", out_shape=[jax.ShapeDtypeStruct(w.shape, F32) for w in ws] * 3,
    )(*ws, *gs, *ms, *vs)
    return outs[:n], outs[n:2 * n], outs[2 * n:]


def _adamw(w, g, m, v, name):
    if w.ndim == 3:
        tr = _pick(w.shape[0], (82, 64, 32, 16, 8))
        grid, blk = (w.shape[0] // tr,), pl.BlockSpec((tr,) + w.shape[1:], lambda i: (i, 0, 0))
    else:
        n_r, n_c = w.shape
        tr = _pick(n_r, (256, 64, 32, 16, 8))
        grid, blk = (n_r // tr,), pl.BlockSpec((tr, n_c), lambda i: (i, 0))

    def body(w_ref, g_ref, m_ref, v_ref, d_ref, nm_ref, nv_ref):
        d_ref[...], nm_ref[...], nv_ref[...] = _adamw_math(w_ref[...], g_ref[...], m_ref[...], v_ref[...])

    out = jax.ShapeDtypeStruct(w.shape, F32)
    return pl.pallas_call(
        body, name=name, grid=grid, in_specs=[blk] * 4, out_specs=[blk] * 3, out_shape=[out] * 3,
        compiler_params=_cparams(("parallel",)),
    )(w, g, m, v)


HBM_SPEC = pl.BlockSpec(memory_space=pltpu.HBM)
VMEM_SPEC = pl.BlockSpec(memory_space=pltpu.VMEM)
CHIP_FLIPS = ((1, 0), (0, 1), (1, 1))


def _place():
    return lax.axis_index("x"), lax.axis_index("y"), lax.axis_index("c")


def _flip(v, f):
    return 1 - v if f else v


def _all_gather8(slab, name, rider=None):
    n_r, n_w = slab.shape
    r_inputs, r_in_specs, r_out_specs, r_sems, split = _with_rider(rider, 1, 2, 3)

    def body(*refs):
        (x_ref, o_ref, s_ref, send_sems, recv_sems, local_sem), parts = split(refs)
        if rider is not None:
            rider.first(*parts)
        x, y, c = _place()
        me = 4 * x + 2 * y + c
        mine = pltpu.make_async_copy(x_ref, o_ref.at[me], local_sem)
        mine.start()
        peers = [(_flip(x, k & 4), _flip(y, k & 2), _flip(c, k & 1)) for k in range(1, N_DEV)]
        sends = []
        for k, peer in enumerate(peers):
            cp = pltpu.make_async_remote_copy(src_ref=x_ref, dst_ref=o_ref.at[me], send_sem=send_sems.at[k],
                                              recv_sem=recv_sems.at[k], device_id=peer, device_id_type=MESH)
            cp.start()
            sends.append(cp)
        for k, (px, py, pc) in enumerate(peers):
            pltpu.make_async_remote_copy(src_ref=x_ref, dst_ref=o_ref.at[4 * px + 2 * py + pc],
                                         send_sem=send_sems.at[k], recv_sem=recv_sems.at[k],
                                         device_id=(px, py, pc), device_id_type=MESH).wait_recv()
        for cp in sends:
            cp.wait_send()
        mine.wait()
        total = o_ref[0]
        for d in range(1, N_DEV):
            total = total + o_ref[d]
        s_ref[...] = total
        if rider is not None:
            rider.last(*parts)

    gathered, total, *rider_outs = pl.pallas_call(
        body, name=name, in_specs=[VMEM_SPEC] + r_in_specs, out_specs=[VMEM_SPEC, VMEM_SPEC] + r_out_specs,
        out_shape=[jax.ShapeDtypeStruct((N_DEV, n_r, n_w), F32), jax.ShapeDtypeStruct((n_r, n_w), F32)]
        + (list(rider.out_shapes) if rider else []),
        scratch_shapes=[pltpu.SemaphoreType.DMA((N_DEV - 1,)), pltpu.SemaphoreType.DMA((N_DEV - 1,)),
                        pltpu.SemaphoreType.DMA] + r_sems,
    )(slab, *r_inputs)
    return (gathered, total, rider_outs) if rider else (gathered, total)


def _gather_rider(shards):
    n_a = len(shards)

    def plan(ins, outs, sems):
        send_sems, recv_sems = sems
        x, y, c = _place()
        chips = [(_flip(x, fx), _flip(y, fy)) for fx, fy in CHIP_FLIPS]

        def copy(k, slot, chip_of_block, half, to, src=None):
            dst = outs[k].at[chip_of_block, half]
            return pltpu.make_async_remote_copy(src_ref=dst if src is None else src, dst_ref=dst,
                                                send_sem=send_sems.at[k * 6 + slot], recv_sem=recv_sems.at[k * 6 + slot],
                                                device_id=to, device_id_type=MESH)

        first = [copy(k, r, 2 * x + y, c, (*chips[r], c), src=ins[k].at[c]) for k in range(n_a) for r in range(3)]
        return copy, chips, first, (x, y, c)

    def first_step(ins, outs, sems):
        for cp in plan(ins, outs, sems)[2]:
            cp.start()

    def last_step(ins, outs, sems):
        copy, chips, first, (x, y, c) = plan(ins, outs, sems)
        passed = []
        for k in range(n_a):
            for r, (px, py) in enumerate(chips):
                copy(k, r, 2 * px + py, c, (x, y, c)).wait_recv()
                fwd = copy(k, 3 + r, 2 * px + py, c, (x, y, 1 - c))
                fwd.start()
                passed.append(fwd)
        for k in range(n_a):
            for r, (px, py) in enumerate(chips):
                copy(k, 3 + r, 2 * px + py, 1 - c, (x, y, c)).wait_recv()
        for cp in first + passed:
            cp.wait_send()

    return Rider(shards, [jax.ShapeDtypeStruct((N_CHIPS,) + s.shape, s.dtype) for s in shards],
                 [pltpu.SemaphoreType.DMA((6 * n_a,)), pltpu.SemaphoreType.DMA((6 * n_a,))], first_step, last_step)


def _place_own(gathered, shard, chip, name):
    _, _, n_h, n_c = gathered.shape
    th = _pick(n_h, (256, 176, 128))

    def body(sel_ref, s_ref, _, o_ref):
        o_ref[...] = s_ref[...]

    grid_spec = pltpu.PrefetchScalarGridSpec(
        num_scalar_prefetch=1, grid=(2, n_h // th),
        in_specs=[pl.BlockSpec((None, th, n_c), lambda hf, i, sel: (hf, i, 0)), pl.BlockSpec(memory_space=pl.ANY)],
        out_specs=pl.BlockSpec((None, None, th, n_c), lambda hf, i, sel: (sel[0], hf, i, 0)))
    return pl.pallas_call(
        body, name=name, grid_spec=grid_spec, out_shape=jax.ShapeDtypeStruct(gathered.shape, gathered.dtype),
        input_output_aliases={2: 0}, compiler_params=_cparams(("parallel", "parallel")),
    )(chip.reshape(1), shard, gathered)


def _pair_rider(parts):
    n_a = len(parts)

    def plan(ins, outs, sems):
        send_sems, recv_sems = sems
        x, y, c = _place()
        return [pltpu.make_async_remote_copy(src_ref=ins[k].at[:, 1 - c], dst_ref=outs[k], send_sem=send_sems.at[k],
                                             recv_sem=recv_sems.at[k], device_id=(x, y, 1 - c), device_id_type=MESH)
                for k in range(n_a)]

    def first_step(ins, outs, sems):
        for cp in plan(ins, outs, sems):
            cp.start()

    def last_step(ins, outs, sems):
        for cp in plan(ins, outs, sems):
            cp.wait()

    return Rider(parts, [jax.ShapeDtypeStruct((N_CHIPS,) + p.shape[2:], F32) for p in parts],
                 [pltpu.SemaphoreType.DMA((n_a,)), pltpu.SemaphoreType.DMA((n_a,))], first_step, last_step)


def _alone(rider, name):
    n_a = len(rider.inputs)

    def body(*refs):
        parts = (refs[:n_a], refs[n_a:2 * n_a], refs[2 * n_a:])
        rider.first(*parts)
        rider.last(*parts)

    return pl.pallas_call(
        body, name=name, in_specs=[HBM_SPEC] * n_a, out_specs=[HBM_SPEC] * n_a,
        out_shape=rider.out_shapes, scratch_shapes=rider.sems,
    )(*rider.inputs)


def _chips_rider(sums):
    n_a = len(sums)

    def plan(ins, outs, sems):
        send_sems, recv_sems = sems
        x, y, c = _place()
        cps = []
        for k in range(n_a):
            for r, (fx, fy) in enumerate(CHIP_FLIPS):
                px, py = _flip(x, fx), _flip(y, fy)
                cps.append(pltpu.make_async_remote_copy(
                    src_ref=ins[k].at[2 * px + py], dst_ref=outs[k].at[r], send_sem=send_sems.at[3 * k + r],
                    recv_sem=recv_sems.at[3 * k + r], device_id=(px, py, c), device_id_type=MESH))
        return cps

    def first_step(ins, outs, sems):
        for cp in plan(ins, outs, sems):
            cp.start()

    def last_step(ins, outs, sems):
        for cp in plan(ins, outs, sems):
            cp.wait()

    return Rider(sums, [jax.ShapeDtypeStruct((3,) + s.shape[1:], s.dtype) for s in sums],
                 [pltpu.SemaphoreType.DMA((3 * n_a,)), pltpu.SemaphoreType.DMA((3 * n_a,))], first_step, last_step)


def _rs_share(bufs):
    n_a = len(bufs)

    def body(*refs):
        ins, outs = refs[:n_a], refs[n_a:2 * n_a]
        send_sems, recv_sems = refs[2 * n_a:]
        x, y, c = _place()
        sends = [pltpu.make_async_remote_copy(src_ref=ins[k].at[c], dst_ref=outs[k].at[c], send_sem=send_sems.at[k],
                                              recv_sem=recv_sems.at[k], device_id=(x, y, 1 - c), device_id_type=MESH)
                 for k in range(n_a)]
        for cp in sends:
            cp.start()
        for k in range(n_a):
            pltpu.make_async_remote_copy(src_ref=ins[k].at[c], dst_ref=outs[k].at[1 - c], send_sem=send_sems.at[k],
                                         recv_sem=recv_sems.at[k], device_id=(x, y, 1 - c),
                                         device_id_type=MESH).wait_recv()
        for cp in sends:
            cp.wait_send()

    return pl.pallas_call(
        body, name="rs_share", in_specs=[HBM_SPEC] * n_a, out_specs=[HBM_SPEC] * n_a,
        out_shape=[jax.ShapeDtypeStruct(s.shape, F32) for s in bufs],
        input_output_aliases={k: k for k in range(n_a)},
        scratch_shapes=[pltpu.SemaphoreType.DMA((n_a,)), pltpu.SemaphoreType.DMA((n_a,))],
    )(*bufs)


def _pair_add(part, recv, core, name):
    _, _, n_h, n_c = part.shape
    th = _pick(n_h, (256, 176, 128))

    def body(sel_ref, p_ref, r_ref, o_ref):
        o_ref[...] = (p_ref[...] + r_ref[...]).astype(BF16)

    grid_spec = pltpu.PrefetchScalarGridSpec(
        num_scalar_prefetch=1, grid=(N_CHIPS, n_h // th),
        in_specs=[pl.BlockSpec((None, None, th, n_c), lambda j, i, sel: (j, sel[0], i, 0)),
                  pl.BlockSpec((None, th, n_c), lambda j, i, sel: (j, i, 0))],
        out_specs=pl.BlockSpec((None, th, n_c), lambda j, i, sel: (j, i, 0)))
    return pl.pallas_call(
        body, name=name, grid_spec=grid_spec, out_shape=jax.ShapeDtypeStruct(recv.shape, BF16),
        compiler_params=_cparams(("parallel", "parallel")),
    )(core.reshape(1), part, recv)


def _chip_add(sums, recv, chip, core, name):
    _, n_h, n_c = sums.shape
    th = _pick(n_h, (256, 176, 128))

    def body(sel_ref, s_ref, r_ref, o_ref):
        total = s_ref[...].astype(F32)
        for r in range(3):
            total = total + r_ref[r].astype(F32)
        o_ref[...] = total

    grid_spec = pltpu.PrefetchScalarGridSpec(
        num_scalar_prefetch=1, grid=(n_h // th,),
        in_specs=[pl.BlockSpec((None, th, n_c), lambda i, sel: (sel[0], i, 0)),
                  pl.BlockSpec((3, th, n_c), lambda i, sel: (0, i, 0))],
        out_specs=pl.BlockSpec((None, th, n_c), lambda i, sel: (sel[1], i, 0)))
    return pl.pallas_call(
        body, name=name, grid_spec=grid_spec, out_shape=jax.ShapeDtypeStruct((2, n_h, n_c), F32),
        compiler_params=_cparams(("parallel",)),
    )(jnp.stack([chip, core]), sums, recv)


def _row_halves(a):
    return a.reshape(N_CHIPS, 2, -1, a.shape[-1])


class StepComm:
    REST = ("w_o", "w_up", "w_down")

    def __init__(self, core, chip, rest_shards, in_cols):
        self.core, self.chip, self.shards, self.in_cols = core, chip, rest_shards, in_cols

    def fwd_rider(self):
        return _gather_rider(self.shards)

    def weights_from(self, landed):
        g_o, g_up, g_down = (_place_own(g, s, self.chip, "place_own_" + n)
                             for g, s, n in zip(landed, self.shards, self.REST))
        return {"w_o": g_o.reshape(-1, D_MODEL), "w_up": g_up.reshape(N_CHIPS, -1, g_up.shape[-1]),
                "w_down": g_down.reshape(-1, D_MODEL)}

    def _add_pairs(self, parts, from_sibling, names):
        return [_pair_add(p, r, self.core, "pair_add_" + n) for p, r, n in zip(parts, from_sibling, names)]

    def ffn_pair_rider(self, g_w_up, g_w_down):
        self.ffn_parts = [_row_halves(g_w_up), _row_halves(g_w_down)]
        return _pair_rider(self.ffn_parts)

    def ffn_chips_rider(self, from_sibling):
        self.ffn_sums = self._add_pairs(self.ffn_parts, from_sibling, ("w_up", "w_down"))
        return _chips_rider(self.ffn_sums)

    def tail_chips_rider(self, g_w_in_p, g_w_o):
        parts = [_row_halves(_w_in_to_chips(g_w_in_p, self.in_cols)), _row_halves(g_w_o)]
        self.tail_sums = self._add_pairs(parts, _alone(_pair_rider(parts), "rs_pair_tail"), ("w_in", "w_o"))
        return _chips_rider(self.tail_sums)

    def finish(self, ffn_from_chips, tail_from_chips):
        halves = [_chip_add(s, r, self.chip, self.core, "chip_add_" + n)
                  for s, r, n in zip(self.tail_sums + self.ffn_sums, list(tail_from_chips) + list(ffn_from_chips),
                                     ("w_in", "w_o", "w_up", "w_down"))]
        return [f.reshape(-1, f.shape[-1]) for f in _rs_share(halves)]


SLAB_W = 1024


def _pack(arrays, rows):
    flat = jnp.concatenate([a.reshape(-1).astype(F32) for a in arrays])
    return jnp.pad(flat, (0, rows * SLAB_W - flat.shape[0])).reshape(rows, SLAB_W)


def _unpack(flat, shapes):
    out, off = [], 0
    for s in shapes:
        n = 1
        for d in s:
            n *= d
        out.append(flat[off:off + n].reshape(s))
        off += n
    return out


def _rows_for(arrays_or_shapes):
    n = 0
    for a in arrays_or_shapes:
        s = a if isinstance(a, tuple) else a.shape
        k = 1
        for d in s:
            k *= d
        n += k
    return -(-n // (8 * SLAB_W)) * 8


def kernel(x, c, ln0_g, ln0_b, w_ada, b_ada, w_in, dn_conv, dn_a_log, dn_dt_bias, dn_norm_g, gla_w_gate2, gla_b_gate, gla_norm_g, w_o, ln1_g, ln1_b, ffn_w_up, ffn_conv, ffn_conv_b, ffn_w_down, ln2_g, ln2_b, loss_target, m_ln0_g, m_ln0_b, m_w_ada, m_b_ada, m_w_in, m_dn_conv, m_dn_a_log, m_dn_dt_bias, m_dn_norm_g, m_gla_w_gate2, m_gla_b_gate, m_gla_norm_g, m_w_o, m_ln1_g, m_ln1_b, m_ffn_w_up, m_ffn_conv, m_ffn_conv_b, m_ffn_w_down, m_ln2_g, m_ln2_b, v_ln0_g, v_ln0_b, v_w_ada, v_b_ada, v_w_in, v_dn_conv, v_dn_a_log, v_dn_dt_bias, v_dn_norm_g, v_gla_w_gate2, v_gla_b_gate, v_gla_norm_g, v_w_o, v_ln1_g, v_ln1_b, v_ffn_w_up, v_ffn_conv, v_ffn_conv_b, v_ffn_w_down, v_ln2_g, v_ln2_b):
    n_b, t_len, _ = x.shape
    xi, yi, ci = _place()
    chip = (2 * xi + yi).astype(jnp.int32)
    core = ci.astype(jnp.int32)
    me = 2 * chip + core
    n_all = N_DEV * n_b
    ada_cols = w_ada.shape[2]

    halves = lambda a: a.astype(BF16).reshape(2, a.shape[0] // 2, a.shape[1])
    w_in_halves = halves(w_in[0])
    sharded_small = [dn_conv[0], gla_w_gate2[0], ffn_conv[0]]
    slab = _pack([c] + sharded_small, _rows_for([c] + sharded_small))
    gathered, _, (g_in,) = _all_gather8(slab, "gather_small", rider=_gather_rider([w_in_halves]))
    g_in = _place_own(g_in, w_in_halves, chip, "place_own_w_in")
    flat = gathered.reshape(N_DEV, -1)
    c_all = flat[:, :c.size].reshape(n_all, D_MODEL)
    by_chip = flat[0::2]
    full, off = [], c.size
    for a in sharded_small:
        blocks = by_chip[:, off:off + a.size].reshape(N_CHIPS, *a.shape)
        full.append(blocks.transpose(1, 0, 2).reshape(a.shape[0], N_CHIPS * a.shape[1]))
        off += a.size
    dn_conv_f, gate2_f, ffn_conv_f = full

    b_ada_shard = lax.dynamic_slice(b_ada, (0, chip * ada_cols), (1, ada_cols))
    cond_all, mod_cols = _ada_fwd(c_all, w_ada[0], b_ada_shard)
    mod_g, _ = _all_gather8(mod_cols, "gather_mod")
    mod_full = jnp.concatenate([mod_g[2 * j] for j in range(N_CHIPS)], axis=1)
    mod = lax.dynamic_slice(mod_full, (me * n_b, 0), (n_b, 6 * D_MODEL)).reshape(n_b, 6, D_MODEL)

    comm = StepComm(core, chip, [halves(w_o[0]), halves(ffn_w_up[0]), halves(ffn_w_down[0])], w_in.shape[2])
    params = {
        "w_in_p": _w_in_to_padded(g_in.reshape(N_CHIPS, -1, g_in.shape[-1])),
        "dn_conv": dn_conv_f, "dn_a_log": dn_a_log[0], "dn_dt_bias": dn_dt_bias[0], "dn_norm_g": dn_norm_g[0],
        "gla_w_gate2": gate2_f, "gla_b_gate": gla_b_gate[0], "gla_norm_g": gla_norm_g[0],
        "ln0_g": ln0_g, "ln0_b": ln0_b, "ln1_g": ln1_g[0], "ln1_b": ln1_b[0], "ln2_g": ln2_g[0], "ln2_b": ln2_b[0],
        "ffn_conv": ffn_conv_f, "ffn_conv_b": ffn_conv_b[0],
    }

    loss_row, grad_x, gp, dmod, from_chips = _local_step(
        x.reshape(n_b * t_len, D_MODEL), loss_target.reshape(n_b * t_len, D_MODEL), mod, params, n_b, t_len, comm)
    loss = lax.psum(loss_row[0, 0], ("x", "y", "c"))

    summed_names = ["ln0_g", "ln0_b", "dn_conv", "dn_a_log", "dn_dt_bias", "dn_norm_g", "gla_w_gate2", "gla_b_gate",
                    "gla_norm_g", "ln1_g", "ln1_b", "ffn_conv", "ffn_conv_b", "ln2_g", "ln2_b"]
    summed_parts = [gp[n] for n in summed_names]
    sum_rows = _rows_for(summed_parts)
    slab = jnp.concatenate([_pack(summed_parts, sum_rows), _pack([dmod], _rows_for([dmod]))], axis=0)
    gathered, total = _all_gather8(slab, "reduce_small")
    small_g = dict(zip(summed_names, _unpack(total.reshape(-1), [a.shape for a in summed_parts])))
    dmod_rows = n_b * 6 * D_MODEL // SLAB_W
    dmod_all = gathered[:, sum_rows:sum_rows + dmod_rows, :].reshape(n_all, 6 * D_MODEL)

    g_b_ada = _col_sum(dmod_all)
    dmod_cols = lax.dynamic_slice(dmod_all, (0, chip * ada_cols), (n_all, ada_cols))
    g_w_ada = _mm(cond_all, dmod_cols, ta=True, name="mm_g_ada")

    g_w_in, g_w_o, g_w_up, g_w_down = comm.finish(*from_chips)

    col_block = lambda a: lax.dynamic_slice(a, (0, chip * (a.shape[1] // N_CHIPS)), (a.shape[0], a.shape[1] // N_CHIPS))
    grads = {
        "ln0_g": small_g["ln0_g"], "ln0_b": small_g["ln0_b"], "w_ada": g_w_ada[None], "b_ada": g_b_ada,
        "w_in": g_w_in[None], "dn_conv": col_block(small_g["dn_conv"])[None], "dn_a_log": small_g["dn_a_log"][None],
        "dn_dt_bias": small_g["dn_dt_bias"][None], "dn_norm_g": small_g["dn_norm_g"][None],
        "gla_w_gate2": col_block(small_g["gla_w_gate2"])[None], "gla_b_gate": small_g["gla_b_gate"][None],
        "gla_norm_g": small_g["gla_norm_g"][None], "w_o": g_w_o[None], "ln1_g": small_g["ln1_g"][None],
        "ln1_b": small_g["ln1_b"][None], "ffn_w_up": g_w_up[None], "ffn_conv": col_block(small_g["ffn_conv"])[None],
        "ffn_conv_b": small_g["ffn_conv_b"][None], "ffn_w_down": g_w_down[None], "ln2_g": small_g["ln2_g"][None],
        "ln2_b": small_g["ln2_b"][None],
    }
    names = ["ln0_g", "ln0_b", "w_ada", "b_ada", "w_in", "dn_conv", "dn_a_log", "dn_dt_bias", "dn_norm_g",
             "gla_w_gate2", "gla_b_gate", "gla_norm_g", "w_o", "ln1_g", "ln1_b", "ffn_w_up", "ffn_conv", "ffn_conv_b",
             "ffn_w_down", "ln2_g", "ln2_b"]
    weights = dict(zip(names, [ln0_g, ln0_b, w_ada, b_ada, w_in, dn_conv, dn_a_log, dn_dt_bias, dn_norm_g, gla_w_gate2,
                               gla_b_gate, gla_norm_g, w_o, ln1_g, ln1_b, ffn_w_up, ffn_conv, ffn_conv_b, ffn_w_down,
                               ln2_g, ln2_b]))
    m_in = dict(zip(names, [m_ln0_g, m_ln0_b, m_w_ada, m_b_ada, m_w_in, m_dn_conv, m_dn_a_log, m_dn_dt_bias,
                            m_dn_norm_g, m_gla_w_gate2, m_gla_b_gate, m_gla_norm_g, m_w_o, m_ln1_g, m_ln1_b,
                            m_ffn_w_up, m_ffn_conv, m_ffn_conv_b, m_ffn_w_down, m_ln2_g, m_ln2_b]))
    v_in = dict(zip(names, [v_ln0_g, v_ln0_b, v_w_ada, v_b_ada, v_w_in, v_dn_conv, v_dn_a_log, v_dn_dt_bias,
                            v_dn_norm_g, v_gla_w_gate2, v_gla_b_gate, v_gla_norm_g, v_w_o, v_ln1_g, v_ln1_b,
                            v_ffn_w_up, v_ffn_conv, v_ffn_conv_b, v_ffn_w_down, v_ln2_g, v_ln2_b]))

    big = ("w_ada", "w_in", "w_o", "ffn_w_up", "ffn_w_down")
    delta, new_m, new_v = {}, {}, {}
    for n in big:
        if n == "w_in":
            view = lambda a: a.T.reshape(a.shape[1], 8, a.shape[0] // 8)
            back = lambda a: a.reshape(a.shape[0], -1).T
        else:
            view = back = lambda a: a
        d_n, m_n, v_n = _adamw(view(weights[n][0]), view(grads[n][0]), view(m_in[n][0]), view(v_in[n][0]), "adamw_" + n)
        delta[n], new_m[n], new_v[n] = back(d_n)[None], back(m_n)[None], back(v_n)[None]
    small = [n for n in names if n not in big]
    d_s, m_s, v_s = _adamw_many([weights[n] for n in small], [grads[n] for n in small],
                                [m_in[n] for n in small], [v_in[n] for n in small])
    for out, vals in ((delta, d_s), (new_m, m_s), (new_v, v_s)):
        out.update(zip(small, vals))

    return (loss, grad_x.reshape(x.shape), *[grads[n] for n in names], *[delta[n] for n in names],
            *[new_m[n] for n in names], *[new_v[n] for n in names])
```

```python
import functools

import jax
import jax.numpy as jnp
from jax import lax
from jax.experimental import pallas as pl
from jax.experimental.pallas import tpu as pltpu

F32 = jnp.float32
BF16 = jnp.bfloat16
MESH = pl.DeviceIdType.MESH

D_MODEL = 1024
HEADS = 4
HEAD_DIM = 128
GLA_KEY = 64
GATE_RANK = 16
CHUNK = 64
D_FF = 2816
ALPHA = 2.0 ** 0.25
EPS = 1e-6
N_CHIPS = 4
N_DEV = 8

PROJ_W = 3840
OFF_GQ, OFF_GK, OFF_GV, OFF_GG, OFF_SMALL, GLA_W = 0, 256, 512, 1024, 1536, 1792
OFF_Z = 2048
W_IN_COLS = 3608


def _qkv_block(j):
    return jnp.where(j < 2, GLA_W // 128 + j, (OFF_Z + 512) // 128 - 2 + j)

ADAM_LR, ADAM_B1, ADAM_B2, ADAM_EPS, ADAM_WD, ADAM_STEP = 0.001, 0.9, 0.999, 1e-08, 0.01, 10

VMEM_LIMIT = 56 * 1024 * 1024
ROW_TILE = 512


def _cparams(sem):
    return pltpu.CompilerParams(dimension_semantics=sem, vmem_limit_bytes=VMEM_LIMIT)


def _pick(n, prefs):
    for p in prefs:
        if n % p == 0:
            return p
    return n


def _mm(a, b, *, ta=False, tb=False, out_slabs=1, out_dtype=F32, name, rider=None):
    a_slabs = a.shape[0] if a.ndim == 3 else 1
    b_slabs = b.shape[0] if b.ndim == 3 else 1
    assert not (ta and a_slabs > 1)
    a2, b2 = a.shape[-2:], b.shape[-2:]
    if ta:
        k_dim, m_dim = a2
    else:
        m_dim, k_dim = a2[0], a2[1] * a_slabs
    n_dim = b2[0] if tb else b2[1] * b_slabs
    k_slabs = max(a_slabs, b_slabs if tb else 1)
    n_slabs = max(out_slabs, 1 if tb else b_slabs)
    tm = _pick(m_dim, (1024, 1408, 512, 256, 128))
    tn = _pick(n_dim // n_slabs, (1536, 1408, 1280, 1024, 768, 512, 384, 256, 128))
    tk = _pick(k_dim // k_slabs, (1408, 1280, 1024, 512, 256, 128))
    nk, nj = k_dim // tk, n_dim // tn
    nk_a, nk_b, nj_b, nj_o = nk // a_slabs, nk // b_slabs, nj // b_slabs, nj // out_slabs
    dims = (((0 if ta else 1,), (1 if tb else 0,)), ((), ()))

    grid = (m_dim // tm, nj, nk)
    assert out_dtype == F32
    r_inputs, r_in_specs, r_out_specs, r_sems, split = _with_rider(rider, 2, 1, 0)

    def body(*refs):
        (a_ref, b_ref, o_ref), parts = split(refs)
        ride_first, ride_last = _ride(rider, parts, grid)
        if rider is not None:
            ride_first()
        prod = lax.dot_general(a_ref[...].astype(BF16), b_ref[...].astype(BF16), dims, preferred_element_type=F32)
        if nk == 1:
            o_ref[...] = prod
        else:
            _acc(o_ref, prod, pl.program_id(2) == 0)
        if rider is not None:
            ride_last()

    if ta:
        a_spec = pl.BlockSpec((tk, tm), lambda i, j, k: (k, i))
    elif a_slabs > 1:
        a_spec = pl.BlockSpec((None, tm, tk), lambda i, j, k: (k // nk_a, i, k % nk_a))
    else:
        a_spec = pl.BlockSpec((tm, tk), lambda i, j, k: (i, k))
    if tb and b_slabs > 1:
        b_spec = pl.BlockSpec((None, tn, tk), lambda i, j, k: (k // nk_b, j, k % nk_b))
    elif tb:
        b_spec = pl.BlockSpec((tn, tk), lambda i, j, k: (j, k))
    elif b_slabs > 1:
        b_spec = pl.BlockSpec((None, tk, tn), lambda i, j, k: (j // nj_b, k, j % nj_b))
    else:
        b_spec = pl.BlockSpec((tk, tn), lambda i, j, k: (k, j))
    if out_slabs > 1:
        o_spec = pl.BlockSpec((None, tm, tn), lambda i, j, k: (j // nj_o, i, j % nj_o))
        o_shape = (out_slabs, m_dim, n_dim // out_slabs)
    else:
        o_spec, o_shape = pl.BlockSpec((tm, tn), lambda i, j, k: (i, j)), (m_dim, n_dim)
    out, *rider_outs = pl.pallas_call(
        body, name=name, grid=grid,
        in_specs=[a_spec, b_spec] + r_in_specs, out_specs=[o_spec] + r_out_specs,
        out_shape=[jax.ShapeDtypeStruct(o_shape, out_dtype)] + (list(rider.out_shapes) if rider else []),
        scratch_shapes=r_sems,
        compiler_params=_cparams(("arbitrary",) * 3 if rider else ("parallel", "parallel", "arbitrary")),
    )(a, b, *r_inputs)
    return (out, rider_outs) if rider else out


def _ln(x, g, b):
    mu = jnp.mean(x, -1, keepdims=True)
    xc = x - mu
    var = jnp.mean(xc * xc, -1, keepdims=True)
    return xc * lax.rsqrt(var + EPS) * g + b


def _softplus(x):
    return jnp.maximum(x, 0.0) + jnp.log(1.0 + jnp.exp(-jnp.abs(x)))


def _silu(x):
    return x * jax.nn.sigmoid(x)


def _dsilu(x):
    s = jax.nn.sigmoid(x)
    return s * (1.0 + x * (1.0 - s))


def _f_ln0(x, g, b, sc, sh):
    x0 = _ln(x, g, b)
    return x0, x0 * (1.0 + sc) + sh


def _f_ln1(x0, y, gt, g, b, sc, sh):
    x1 = _ln(ALPHA * x0 + (1.0 + gt) * y, g, b)
    return x1, x1 * (1.0 + sc) + sh


def _f_ln2_loss(x1, y2, gt, g, b, tgt):
    x2 = _ln(ALPHA * x1 + (1.0 + gt) * y2, g, b)
    err = x2 - tgt
    per_row = jnp.sum(err * err, -1, keepdims=True) * (0.5 / D_MODEL)
    return jnp.sum(per_row, 0, keepdims=True)


def _row_specs(t_len):
    nt = t_len // ROW_TILE
    row = pl.BlockSpec((ROW_TILE, D_MODEL), lambda b, i: (b * nt + i, 0))
    vec = pl.BlockSpec((1, D_MODEL), lambda b, i: (0, 0))
    mod = pl.BlockSpec((None, 6, D_MODEL), lambda b, i: (b, 0, 0))
    return nt, row, vec, mod


def _first_step():
    return jnp.logical_and(pl.program_id(0) == 0, pl.program_id(1) == 0)


def _acc(ref, val, first, at=(Ellipsis,)):
    @pl.when(first)
    def _():
        ref[at] = val

    @pl.when(jnp.logical_not(first))
    def _():
        ref[at] += val


def _acc_rows(ref, rows, first):
    for i, r in enumerate(rows):
        _acc(ref, r, first, at=(slice(i, i + 1), slice(None)))


def _ln0_fwd(x, g, b, mod, n_b, t_len):
    nt, row, vec, mods = _row_specs(t_len)

    def body(x_ref, g_ref, b_ref, mod_ref, x0_ref, h_ref):
        x0, h = _f_ln0(x_ref[...], g_ref[...], b_ref[...], mod_ref[1:2, :], mod_ref[0:1, :])
        x0_ref[...] = x0
        h_ref[...] = h.astype(BF16)

    return pl.pallas_call(
        body, name="ln0_fwd", grid=(n_b, nt), in_specs=[row, vec, vec, mods], out_specs=[row, row],
        out_shape=[jax.ShapeDtypeStruct(x.shape, F32), jax.ShapeDtypeStruct(x.shape, BF16)],
        compiler_params=_cparams(("parallel", "parallel")),
    )(x, g, b, mod)


def _ln0_bwd(x, g, b, mod, dx0, dh, n_b, t_len):
    nt, row, vec, mods = _row_specs(t_len)
    dmod_spec = pl.BlockSpec((None, 2, D_MODEL), lambda bb, i: (bb, 0, 0))

    def body(x_ref, g_ref, b_ref, mod_ref, dx0_ref, dh_ref, dx_ref, dg_ref, db_ref, dmod_ref):
        _, pull = jax.vjp(_f_ln0, x_ref[...], g_ref[...], b_ref[...], mod_ref[1:2, :], mod_ref[0:1, :])
        dx, dg, db, dsc, dsh = pull((dx0_ref[...], dh_ref[...]))
        dx_ref[...] = dx
        _acc(dg_ref, dg, _first_step())
        _acc(db_ref, db, _first_step())
        _acc_rows(dmod_ref, [dsh, dsc], pl.program_id(1) == 0)

    return pl.pallas_call(
        body, name="ln0_bwd", grid=(n_b, nt), in_specs=[row, vec, vec, mods, row, row],
        out_specs=[row, vec, vec, dmod_spec],
        out_shape=[jax.ShapeDtypeStruct(x.shape, F32), jax.ShapeDtypeStruct((1, D_MODEL), F32),
                   jax.ShapeDtypeStruct((1, D_MODEL), F32), jax.ShapeDtypeStruct((n_b, 2, D_MODEL), F32)],
        compiler_params=_cparams(("arbitrary", "arbitrary")),
    )(x, g, b, mod, dx0, dh)


def _ln1_fwd(x0, y, g, b, mod, n_b, t_len):
    nt, row, vec, mods = _row_specs(t_len)

    def body(x0_ref, y_ref, g_ref, b_ref, mod_ref, x1_ref, h_ref):
        x1, h = _f_ln1(x0_ref[...], y_ref[...], mod_ref[2:3, :], g_ref[...], b_ref[...],
                       mod_ref[4:5, :], mod_ref[3:4, :])
        x1_ref[...] = x1
        h_ref[...] = h.astype(BF16)

    return pl.pallas_call(
        body, name="ln1_fwd", grid=(n_b, nt), in_specs=[row, row, vec, vec, mods], out_specs=[row, row],
        out_shape=[jax.ShapeDtypeStruct(x0.shape, F32), jax.ShapeDtypeStruct(x0.shape, BF16)],
        compiler_params=_cparams(("parallel", "parallel")),
    )(x0, y, g, b, mod)


def _ln1_bwd(x0, y, g, b, mod, dx1, dh, n_b, t_len):
    nt, row, vec, mods = _row_specs(t_len)
    dmod_spec = pl.BlockSpec((None, 3, D_MODEL), lambda bb, i: (bb, 0, 0))

    def body(x0_ref, y_ref, g_ref, b_ref, mod_ref, dx1_ref, dh_ref, dx0_ref, dy_ref, dg_ref, db_ref, dmod_ref):
        _, pull = jax.vjp(_f_ln1, x0_ref[...], y_ref[...], mod_ref[2:3, :], g_ref[...], b_ref[...],
                          mod_ref[4:5, :], mod_ref[3:4, :])
        dx0, dy, dgt, dg, db, dsc, dsh = pull((dx1_ref[...], dh_ref[...]))
        dx0_ref[...] = dx0
        dy_ref[...] = dy.astype(BF16)
        _acc(dg_ref, dg, _first_step())
        _acc(db_ref, db, _first_step())
        _acc_rows(dmod_ref, [dgt, dsh, dsc], pl.program_id(1) == 0)

    return pl.pallas_call(
        body, name="ln1_bwd", grid=(n_b, nt), in_specs=[row, row, vec, vec, mods, row, row],
        out_specs=[row, row, vec, vec, dmod_spec],
        out_shape=[jax.ShapeDtypeStruct(x0.shape, F32), jax.ShapeDtypeStruct(x0.shape, BF16),
                   jax.ShapeDtypeStruct((1, D_MODEL), F32), jax.ShapeDtypeStruct((1, D_MODEL), F32),
                   jax.ShapeDtypeStruct((n_b, 3, D_MODEL), F32)],
        compiler_params=_cparams(("arbitrary", "arbitrary")),
    )(x0, y, g, b, mod, dx1, dh)


def _ln2_loss_bwd(x1, y2, g, b, mod, tgt, n_b, t_len):
    nt, row, vec, mods = _row_specs(t_len)
    one = pl.BlockSpec((1, 128), lambda bb, i: (0, 0))
    dmod_spec = pl.BlockSpec((None, 1, D_MODEL), lambda bb, i: (bb, 0, 0))

    def body(x1_ref, y2_ref, g_ref, b_ref, mod_ref, t_ref, loss_ref, dx1_ref, dy2_ref, dg_ref, db_ref, dgt_ref):
        loss, pull = jax.vjp(functools.partial(_f_ln2_loss, tgt=t_ref[...]), x1_ref[...], y2_ref[...],
                             mod_ref[5:6, :], g_ref[...], b_ref[...])
        dx1, dy2, dgt, dg, db = pull(jnp.ones((1, 1), F32))
        dx1_ref[...] = dx1
        dy2_ref[...] = dy2.astype(BF16)
        _acc(loss_ref, jnp.broadcast_to(loss, (1, 128)), _first_step())
        _acc(dg_ref, dg, _first_step())
        _acc(db_ref, db, _first_step())
        _acc(dgt_ref, dgt, pl.program_id(1) == 0)

    return pl.pallas_call(
        body, name="ln2_loss_bwd", grid=(n_b, nt), in_specs=[row, row, vec, vec, mods, row],
        out_specs=[one, row, row, vec, vec, dmod_spec],
        out_shape=[jax.ShapeDtypeStruct((1, 128), F32), jax.ShapeDtypeStruct(x1.shape, F32),
                   jax.ShapeDtypeStruct(x1.shape, BF16), jax.ShapeDtypeStruct((1, D_MODEL), F32),
                   jax.ShapeDtypeStruct((1, D_MODEL), F32), jax.ShapeDtypeStruct((n_b, 1, D_MODEL), F32)],
        compiler_params=_cparams(("arbitrary", "arbitrary")),
    )(x1, y2, g, b, mod, tgt)


def _shift_down(x, s):
    if s == 0:
        return x
    rows = lax.broadcasted_iota(jnp.int32, x.shape, 0)
    return jnp.where(rows >= s, pltpu.roll(x, s, 0), 0.0)


def _shift_up(x, s):
    if s == 0:
        return x
    t_len = x.shape[0]
    rows = lax.broadcasted_iota(jnp.int32, x.shape, 0)
    return jnp.where(rows < t_len - s, pltpu.roll(x, t_len - s, 0), 0.0)


def _taps(x, k_w):
    return [_shift_down(x, k_w - 1 - k) for k in range(k_w)]


def _conv(taps, w):
    out = w[0:1, :] * taps[0]
    for k in range(1, len(taps)):
        out = out + w[k:k + 1, :] * taps[k]
    return out


def _conv_bwd(taps, w, du):
    k_w = len(taps)
    dx = w[k_w - 1:k_w, :] * du
    for k in range(k_w - 1):
        dx = dx + w[k:k + 1, :] * _shift_up(du, k_w - 1 - k)
    return dx, [jnp.sum(du * taps[k], 0, keepdims=True) for k in range(k_w)]


def _dn_pre_fwd(proj, conv_w, n_b, t_len):
    n_ct = 3 * HEADS
    k_w = conv_w.shape[0]

    def body(x_ref, w_ref, o_ref):
        o_ref[...] = _silu(_conv(_taps(x_ref[...], k_w), w_ref[...]))

    return pl.pallas_call(
        body, name="dn_pre_fwd", grid=(n_ct, n_b),
        in_specs=[pl.BlockSpec((t_len, 128), lambda j, b: (b, _qkv_block(j))),
                  pl.BlockSpec((k_w, 128), lambda j, b: (0, j))],
        out_specs=pl.BlockSpec((t_len, 128), lambda j, b: (b, j)),
        out_shape=jax.ShapeDtypeStruct((n_b * t_len, n_ct * 128), F32),
        compiler_params=_cparams(("parallel", "parallel")),
    )(proj, conv_w)


def _dn_pre_bwd(proj, conv_w, dqkv, d_proj, n_b, t_len):
    n_ct = 3 * HEADS
    k_w = conv_w.shape[0]

    def body(x_ref, w_ref, d_ref, _, dx_ref, dw_ref):
        taps, w = _taps(x_ref[...], k_w), w_ref[...]
        du = d_ref[...] * _dsilu(_conv(taps, w))
        dx, dw = _conv_bwd(taps, w, du)
        dx_ref[...] = dx.astype(BF16)
        _acc_rows(dw_ref, dw, pl.program_id(1) == 0)

    return pl.pallas_call(
        body, name="dn_pre_bwd", grid=(n_ct, n_b),
        in_specs=[pl.BlockSpec((t_len, 128), lambda j, b: (b, _qkv_block(j))),
                  pl.BlockSpec((k_w, 128), lambda j, b: (0, j)),
                  pl.BlockSpec((t_len, 128), lambda j, b: (b, j)), pl.BlockSpec(memory_space=pl.ANY)],
        out_specs=[pl.BlockSpec((t_len, 128), lambda j, b: (b, _qkv_block(j))),
                   pl.BlockSpec((k_w, 128), lambda j, b: (0, j))],
        out_shape=[jax.ShapeDtypeStruct(d_proj.shape, BF16), jax.ShapeDtypeStruct((k_w, n_ct * 128), F32)],
        input_output_aliases={3: 0},
        compiler_params=_cparams(("parallel", "arbitrary")),
    )(proj, conv_w, dqkv, d_proj)


FFN_TC = 256
FFN_NT = D_FF // FFN_TC


def _ffn_specs(t_len):
    blk = lambda off: pl.BlockSpec((t_len, FFN_TC), lambda j, b: (b, j + off))
    wblk = lambda off: pl.BlockSpec((3, FFN_TC), lambda j, b: (0, j + off))
    bblk = lambda off: pl.BlockSpec((1, FFN_TC), lambda j, b: (0, j + off))
    return [blk(0), blk(FFN_NT), wblk(0), wblk(FFN_NT), bblk(0), bblk(FFN_NT)]


def _ffn_act_fwd(up, conv_w, conv_b, n_b, t_len):
    def body(g_ref, v_ref, wg_ref, wv_ref, bg_ref, bv_ref, o_ref):
        ug = _conv(_taps(g_ref[...], 3), wg_ref[...]) + bg_ref[...]
        uv = _conv(_taps(v_ref[...], 3), wv_ref[...]) + bv_ref[...]
        o_ref[...] = (_silu(ug) * uv).astype(BF16)

    return pl.pallas_call(
        body, name="ffn_act_fwd", grid=(FFN_NT, n_b), in_specs=_ffn_specs(t_len),
        out_specs=pl.BlockSpec((t_len, FFN_TC), lambda j, b: (b, j)),
        out_shape=jax.ShapeDtypeStruct((n_b * t_len, D_FF), BF16),
        compiler_params=_cparams(("parallel", "parallel")),
    )(up, up, conv_w, conv_w, conv_b, conv_b)


def _ffn_act_bwd(up, conv_w, conv_b, da, n_b, t_len):
    def body(g_ref, v_ref, wg_ref, wv_ref, bg_ref, bv_ref, da_ref, dup_ref, dw_ref, db_ref):
        first = pl.program_id(1) == 0
        tg, tv, wg, wv = _taps(g_ref[...], 3), _taps(v_ref[...], 3), wg_ref[...], wv_ref[...]
        ug = _conv(tg, wg) + bg_ref[...]
        uv = _conv(tv, wv) + bv_ref[...]
        d_act = da_ref[...]
        sig = jax.nn.sigmoid(ug)
        d_v = d_act * (ug * sig)
        d_g = d_act * uv * (sig * (1.0 + ug * (1.0 - sig)))
        for slab, (taps, w, du) in enumerate(((tg, wg, d_g), (tv, wv, d_v))):
            dx, dw = _conv_bwd(taps, w, du)
            dup_ref[slab] = dx.astype(BF16)
            for k, dw_k in enumerate(dw):
                _acc(dw_ref, dw_k, first, at=(slab, slice(k, k + 1), slice(None)))
            _acc(db_ref, jnp.sum(du, 0, keepdims=True), first, at=(slab, slice(None), slice(None)))

    return pl.pallas_call(
        body, name="ffn_act_bwd", grid=(FFN_NT, n_b),
        in_specs=_ffn_specs(t_len) + [pl.BlockSpec((t_len, FFN_TC), lambda j, b: (b, j))],
        out_specs=[pl.BlockSpec((2, t_len, FFN_TC), lambda j, b: (0, b, j)),
                   pl.BlockSpec((2, 3, FFN_TC), lambda j, b: (0, 0, j)),
                   pl.BlockSpec((2, 1, FFN_TC), lambda j, b: (0, 0, j))],
        out_shape=[jax.ShapeDtypeStruct((2, n_b * t_len, D_FF), BF16),
                   jax.ShapeDtypeStruct((2, 3, D_FF), F32), jax.ShapeDtypeStruct((2, 1, D_FF), F32)],
        compiler_params=_cparams(("parallel", "arbitrary")),
    )(up, up, conv_w, conv_w, conv_b, conv_b, da)


NN = (((2,), (1,)), ((0,), (0,)))
NT = (((2,), (2,)), ((0,), (0,)))
TN = (((1,), (1,)), ((0,), (0,)))


def _iota3(shape, axis):
    return lax.broadcasted_iota(jnp.int32, shape, axis)


def _dg(a, b, dims):
    return lax.dot_general(a, b, dims, preferred_element_type=F32)


def _dot(a, b):
    return _dg(a, b, NN)


def _dot_nt(a, b):
    return _dg(a, b, NT)


def _dot_tn(a, b):
    return _dg(a, b, TN)


def _split(a):
    hi = a.astype(BF16)
    return hi, (a - hi.astype(F32)).astype(BF16)


def _dg3(a, b, dims):
    ah, al = _split(a)
    bh, bl = _split(b)
    return _dg(ah, bh, dims) + (_dg(ah, bl, dims) + _dg(al, bh, dims))


@jax.custom_vjp
def _dot3(a, b):
    return _dg3(a, b, NN)


def _dot3_fwd(a, b):
    return _dg3(a, b, NN), (a, b)


def _dot3_bwd(res, g):
    a, b = res
    return _dg3(g, b, NT), _dg3(a, g, TN)


_dot3.defvjp(_dot3_fwd, _dot3_bwd)


def _lower_ones(g_n, n):
    shape = (g_n, n, n)
    return jnp.where(_iota3(shape, 1) >= _iota3(shape, 2), 1.0, 0.0).astype(BF16)


@jax.custom_vjp
def _chunk_cumsum(x):
    hi, lo = _split(x)
    tri = _lower_ones(x.shape[0], x.shape[1])
    return _dg(tri, hi, NN) + _dg(tri, lo, NN)


def _chunk_cumsum_fwd(x):
    return _chunk_cumsum(x), None


def _chunk_cumsum_bwd(_, g):
    hi, lo = _split(g)
    tri = _lower_ones(g.shape[0], g.shape[1])
    return (_dg(tri, hi, TN) + _dg(tri, lo, TN),)


_chunk_cumsum.defvjp(_chunk_cumsum_fwd, _chunk_cumsum_bwd)


@jax.custom_vjp
def _unit_lower_inv(m):
    n = m.shape[1]
    p = -m
    a = jnp.where(_iota3(m.shape, 1) == _iota3(m.shape, 2), 1.0, 0.0) + p
    span = 2
    while span < n:
        p = _dg3(p, p, NN)
        a = a + _dg3(a, p, NN)
        span *= 2
    return a


def _unit_lower_inv_fwd(m):
    a = _unit_lower_inv(m)
    return a, a


def _unit_lower_inv_bwd(a, da):
    return (-_dg3(a, _dg3(da, a, NT), TN),)


_unit_lower_inv.defvjp(_unit_lower_inv_fwd, _unit_lower_inv_bwd)


@jax.custom_vjp
def _saved_lower_inv(m, a):
    return a


def _saved_lower_inv_fwd(m, a):
    return a, a


def _saved_lower_inv_bwd(a, da):
    return _unit_lower_inv_bwd(a, da)[0], jnp.zeros_like(a)


_saved_lower_inv.defvjp(_saved_lower_inv_fwd, _saved_lower_inv_bwd)


def _rms_gate(o, gn, gate):
    return o * lax.rsqrt(jnp.mean(o * o, -1, keepdims=True) + EPS) * gn * _silu(gate)


def _dn_chains(q, k, v, z, small, s_in, a_log, dt_bias, gn, a_saved=None):
    g_n, c_len = q.shape[0], q.shape[1]
    sq = (g_n, c_len, c_len)
    row, col = _iota3(sq, 1), _iota3(sq, 2)
    causal, strict, eye = row >= col, row > col, row == col
    qn = q * lax.rsqrt(jnp.sum(q * q, -1, keepdims=True) + EPS) * (HEAD_DIM ** -0.5)
    kn = k * lax.rsqrt(jnp.sum(k * k, -1, keepdims=True) + EPS)
    lane = _iota3(small.shape, 2)
    head = jnp.bitwise_and(_iota3(small.shape, 0), HEADS - 1)
    la_all = -jnp.exp(a_log) * _softplus(small + dt_bias)
    la_c = jnp.sum(jnp.where(lane == head, la_all, 0.0), 2, keepdims=True)
    beta = jnp.sum(jnp.where(lane == head + HEADS, jax.nn.sigmoid(small), 0.0), 2, keepdims=True)
    la_b = jnp.broadcast_to(la_c, sq)
    la_r = jnp.sum(jnp.where(eye, la_b, 0.0), 1, keepdims=True)
    g_c = jnp.sum(jnp.where(causal, jnp.broadcast_to(la_r, sq), 0.0), 2, keepdims=True)
    g_r = jnp.sum(jnp.where(row <= col, la_b, 0.0), 1, keepdims=True)
    g_last = jnp.sum(la_c, 1, keepdims=True)
    decay = jnp.exp(jnp.where(causal, g_c - g_r, -1e30))
    e_g = jnp.exp(g_c)
    kb = kn * beta
    m_low = jnp.where(strict, _dot_nt(kb, kn) * decay, 0.0)
    a_inv = _unit_lower_inv(m_low) if a_saved is None else _saved_lower_inv(m_low, a_saved)
    u = _dot3(a_inv, v * beta)
    w = _dot3(a_inv, kb * e_g)
    attn = _dot_nt(qn, kn) * decay
    v_new = u - _dot(w, s_in)
    o = _dot(qn * e_g, s_in) + _dot(attn, v_new)
    s_out = s_in * jnp.exp(g_last) + _dot_tn(kn * jnp.exp(g_last - g_c), v_new)
    return _rms_gate(o, gn, z), s_out, a_inv


def _gla_chains(q, k, v, gate, small, s_in, w2, b2, gn):
    g_n, c_len = q.shape[0], q.shape[1]
    sq, kk = (g_n, c_len, c_len), (g_n, GLA_KEY, GLA_KEY)
    causal = _iota3(sq, 1) >= _iota3(sq, 2)
    la = -_softplus(-(_dot(small, w2) + b2)) * (1.0 / 16.0)
    b = _chunk_cumsum(la)
    b_last = jnp.sum(jnp.where(_iota3(b.shape, 1) == c_len - 1, b, 0.0), 1, keepdims=True)
    q_dec = q * (GLA_KEY ** -0.5) * jnp.exp(b)
    attn = jnp.where(causal, _dot_nt(q_dec, k * jnp.exp(-b)), 0.0)
    o = _dot(q_dec, s_in) + _dot(attn, v)
    g_row = jnp.exp(b_last)
    g_col = jnp.sum(jnp.where(_iota3(kk, 1) == _iota3(kk, 2), jnp.broadcast_to(g_row, kk), 0.0), 2, keepdims=True)
    s_out = s_in * g_col + _dot_tn(k * jnp.exp(b_last - b), v)
    return _rms_gate(o, gn, gate), s_out


def _chunk_spec(n_b, width, col_block, n_c, reverse=False):
    if reverse:
        return pl.BlockSpec((n_b, CHUNK, width), lambda n: (0, n_c - 1 - n, col_block))
    return pl.BlockSpec((n_b, CHUNK, width), lambda n: (0, n, col_block))


def _hist_spec(n_b, d_k, n_c, reverse=False):
    if reverse:
        return pl.BlockSpec((None, n_b * HEADS, d_k, HEAD_DIM), lambda n: (n_c - 1 - n, 0, 0, 0))
    return pl.BlockSpec((None, n_b * HEADS, d_k, HEAD_DIM), lambda n: (n, 0, 0, 0))


def _ainv_spec(n_b, n_c, reverse=False):
    if reverse:
        return pl.BlockSpec((None, n_b * HEADS, CHUNK, CHUNK), lambda n: (n_c - 1 - n, 0, 0, 0))
    return pl.BlockSpec((None, n_b * HEADS, CHUNK, CHUNK), lambda n: (n, 0, 0, 0))


def _stack_chains(ref, n_b, slices):
    return jnp.stack([ref[b, :, sl] for b in range(n_b) for sl in slices], axis=0)


def _per_chain(ref, n_b):
    return jnp.stack([ref[b] for b in range(n_b) for _ in range(HEADS)], axis=0)


def _unstack_chains(ref, val, n_b, slices, offset=0):
    for b in range(n_b):
        for h, sl in enumerate(slices):
            ref[b, :, slice(offset + sl.start, offset + sl.stop)] = val[b * HEADS + h].astype(ref.dtype)


def _gate_weights(w2_ref, b2_ref, n_b):
    w2 = jnp.stack([w2_ref[:, ks] for _ in range(n_b) for ks in GLA_KSL], axis=0)
    b2 = jnp.stack([b2_ref[:, ks] for _ in range(n_b) for ks in GLA_KSL], axis=0)
    return w2, b2


def _sum_heads(val, n_b):
    return [sum(val[b * HEADS + h] for h in range(HEADS)) for b in range(n_b)]


def _const_spec(shape):
    return pl.BlockSpec(shape, lambda n: (0,) * len(shape))


DN_SL = [slice(h * HEAD_DIM, (h + 1) * HEAD_DIM) for h in range(HEADS)]
GLA_KSL = [slice(h * GLA_KEY, (h + 1) * GLA_KEY) for h in range(HEADS)]


class Rider:
    def __init__(self, inputs, out_shapes, sems, first, last):
        self.inputs, self.out_shapes, self.sems, self.first, self.last = inputs, out_shapes, sems, first, last


def _with_rider(rider, n_in, n_out, n_scratch):
    if rider is None:
        return [], [], [], [], lambda refs: (refs, None)
    r_in, r_out, r_sem = len(rider.inputs), len(rider.out_shapes), len(rider.sems)

    def split(refs):
        own_in, rest = refs[:n_in], refs[n_in:]
        rid_in, rest = rest[:r_in], rest[r_in:]
        own_out, rest = rest[:n_out], rest[n_out:]
        rid_out, rest = rest[:r_out], rest[r_out:]
        own_scr, rid_sem = rest[:n_scratch], rest[n_scratch:]
        return own_in + own_out + own_scr, (rid_in, rid_out, rid_sem)

    return list(rider.inputs), [HBM_SPEC] * r_in, [HBM_SPEC] * r_out, list(rider.sems), split


def _ride(rider, parts, grid):
    if rider is None:
        return None, None
    grid = grid if isinstance(grid, tuple) else (grid,)

    def at(step_of):
        hit = pl.program_id(0) == step_of(grid[0])
        for axis in range(1, len(grid)):
            hit = jnp.logical_and(hit, pl.program_id(axis) == step_of(grid[axis]))
        return hit

    def first():
        pl.when(at(lambda n: 0))(lambda: rider.first(*parts))

    def last():
        pl.when(at(lambda n: n - 1))(lambda: rider.last(*parts))

    return first, last


def _dn_scan_fwd(qkv, proj, a_log, dt_bias, gn, n_b, t_len, rider=None):
    n_c = t_len // CHUNK
    spec = functools.partial(_chunk_spec, n_b, n_c=n_c)
    r_inputs, r_in_specs, r_out_specs, r_sems, split = _with_rider(rider, 8, 3, 1)

    def body(*refs):
        (q_ref, k_ref, v_ref, z_ref, sm_ref, al_ref, dt_ref, gn_ref,
         o_ref, hist_ref, ainv_ref, s_ref), parts = split(refs)
        ride_first, ride_last = _ride(rider, parts, n_c)
        if rider is not None:
            ride_first()

        @pl.when(pl.program_id(0) == 0)
        def _():
            s_ref[...] = jnp.zeros_like(s_ref)

        s_in = s_ref[...]
        hist_ref[...] = s_in
        og, s_out, a_inv = _dn_chains(*(_stack_chains(r, n_b, DN_SL) for r in (q_ref, k_ref, v_ref, z_ref)),
                                      _per_chain(sm_ref, n_b), s_in, al_ref[...], dt_ref[...], gn_ref[...])
        _unstack_chains(o_ref, og, n_b, DN_SL)
        s_ref[...] = s_out
        ainv_ref[...] = a_inv
        if rider is not None:
            ride_last()

    qkv3, proj3 = qkv.reshape(n_b, t_len, -1), proj.reshape(n_b, t_len, -1)
    o, hist, ainv, *rider_outs = pl.pallas_call(
        body, name="dn_scan_fwd", grid=(n_c,),
        in_specs=[spec(512, 0), spec(512, 1), spec(512, 2), spec(512, OFF_Z // 512), spec(128, OFF_SMALL // 128),
                  _const_spec((1, 128)), _const_spec((1, 128)), _const_spec((1, 128))] + r_in_specs,
        out_specs=[spec(512, 0), _hist_spec(n_b, HEAD_DIM, n_c), _ainv_spec(n_b, n_c)] + r_out_specs,
        out_shape=[jax.ShapeDtypeStruct((n_b, t_len, 2 * 512), BF16),
                   jax.ShapeDtypeStruct((n_c, n_b * HEADS, HEAD_DIM, HEAD_DIM), F32),
                   jax.ShapeDtypeStruct((n_c, n_b * HEADS, CHUNK, CHUNK), F32)]
        + (list(rider.out_shapes) if rider else []),
        scratch_shapes=[pltpu.VMEM((n_b * HEADS, HEAD_DIM, HEAD_DIM), F32)] + r_sems,
        compiler_params=_cparams(("arbitrary",)),
    )(qkv3, qkv3, qkv3, proj3, proj3, a_log, dt_bias, gn, *r_inputs)
    return o, (hist, ainv), rider_outs


def _dn_scan_bwd(qkv, proj, a_log, dt_bias, gn, hist, d_o, n_b, t_len, rider=None):
    n_c = t_len // CHUNK
    rev = functools.partial(_chunk_spec, n_b, n_c=n_c, reverse=True)
    r_inputs, r_in_specs, r_out_specs, r_sems, split = _with_rider(rider, 11, 6, 1)
    hist, ainv = hist

    def body(*refs):
        (q_ref, k_ref, v_ref, z_ref, sm_ref, al_ref, dt_ref, gn_ref, hist_ref, ainv_ref, do_ref,
         dqkv_ref, dz_ref, dsm_ref, dal_ref, ddt_ref, dgn_ref, ds_ref), parts = split(refs)
        ride_first, ride_last = _ride(rider, parts, n_c)
        if rider is not None:
            ride_first()
        first = pl.program_id(0) == 0

        @pl.when(first)
        def _():
            ds_ref[...] = jnp.zeros_like(ds_ref)

        chains = lambda *a: _dn_chains(*a, a_saved=ainv_ref[...])[:2]
        _, pull = jax.vjp(chains, *(_stack_chains(r, n_b, DN_SL) for r in (q_ref, k_ref, v_ref, z_ref)),
                          _per_chain(sm_ref, n_b), hist_ref[...], al_ref[...], dt_ref[...], gn_ref[...])
        dq, dk, dv, dz, dsm, ds_in, dal, ddt, dgn = pull((_stack_chains(do_ref, n_b, DN_SL), ds_ref[...]))
        _unstack_chains(dqkv_ref, dq, n_b, DN_SL)
        _unstack_chains(dqkv_ref, dk, n_b, DN_SL, offset=512)
        _unstack_chains(dqkv_ref, dv, n_b, DN_SL, offset=1024)
        _unstack_chains(dz_ref, dz, n_b, DN_SL)
        ds_ref[...] = ds_in
        for b, dsm_b in enumerate(_sum_heads(dsm, n_b)):
            dsm_ref[b] = dsm_b
        _acc(dal_ref, dal, first)
        _acc(ddt_ref, ddt, first)
        _acc(dgn_ref, dgn, first)
        if rider is not None:
            ride_last()

    qkv3, proj3, do3 = (a.reshape(n_b, t_len, -1) for a in (qkv, proj, d_o))
    vec = jax.ShapeDtypeStruct((1, 128), F32)
    dqkv, d_proj, dsm, dal, ddt, dgn, *rider_outs = pl.pallas_call(
        body, name="dn_scan_bwd", grid=(n_c,),
        in_specs=[rev(512, 0), rev(512, 1), rev(512, 2), rev(512, OFF_Z // 512), rev(128, OFF_SMALL // 128),
                  _const_spec((1, 128)), _const_spec((1, 128)), _const_spec((1, 128)),
                  _hist_spec(n_b, HEAD_DIM, n_c, reverse=True), _ainv_spec(n_b, n_c, reverse=True),
                  rev(512, 0)] + r_in_specs,
        out_specs=[rev(1536, 0), rev(512, OFF_Z // 512), rev(128, 0),
                   _const_spec((1, 128)), _const_spec((1, 128)), _const_spec((1, 128))] + r_out_specs,
        out_shape=[jax.ShapeDtypeStruct((n_b, t_len, 1536), F32), jax.ShapeDtypeStruct((n_b, t_len, PROJ_W), BF16),
                   jax.ShapeDtypeStruct((n_b, t_len, 128), F32), vec, vec, vec]
        + (list(rider.out_shapes) if rider else []),
        scratch_shapes=[pltpu.VMEM((n_b * HEADS, HEAD_DIM, HEAD_DIM), F32)] + r_sems,
        compiler_params=_cparams(("arbitrary",)),
    )(qkv3, qkv3, qkv3, proj3, proj3, a_log, dt_bias, gn, hist, ainv, do3, *r_inputs)
    return dqkv.reshape(n_b * t_len, 1536), d_proj, dsm, dal, ddt, dgn, rider_outs


def _gla_scan_fwd(proj, w2, b2, gn, o_mix, n_b, t_len):
    n_c = t_len // CHUNK
    spec = functools.partial(_chunk_spec, n_b, n_c=n_c)

    def body(q_ref, k_ref, v_ref, g_ref, sm_ref, w2_ref, b2_ref, gn_ref, _, o_ref, hist_ref, s_ref):
        @pl.when(pl.program_id(0) == 0)
        def _():
            s_ref[...] = jnp.zeros_like(s_ref)

        s_in = s_ref[...]
        hist_ref[...] = s_in
        og, s_out = _gla_chains(_stack_chains(q_ref, n_b, GLA_KSL), _stack_chains(k_ref, n_b, GLA_KSL),
                                _stack_chains(v_ref, n_b, DN_SL), _stack_chains(g_ref, n_b, DN_SL),
                                _per_chain(sm_ref, n_b), s_in, *_gate_weights(w2_ref, b2_ref, n_b), gn_ref[...])
        _unstack_chains(o_ref, og, n_b, DN_SL)
        s_ref[...] = s_out

    proj3 = proj.reshape(n_b, t_len, -1)
    o, hist = pl.pallas_call(
        body, name="gla_scan_fwd", grid=(n_c,),
        in_specs=[spec(256, OFF_GQ // 256), spec(256, OFF_GK // 256), spec(512, OFF_GV // 512),
                  spec(512, OFF_GG // 512), spec(128, OFF_SMALL // 128),
                  _const_spec((128, 256)), _const_spec((1, 256)), _const_spec((1, 128)),
                  pl.BlockSpec(memory_space=pl.ANY)],
        out_specs=[spec(512, 1), _hist_spec(n_b, GLA_KEY, n_c)],
        out_shape=[jax.ShapeDtypeStruct(o_mix.shape, BF16),
                   jax.ShapeDtypeStruct((n_c, n_b * HEADS, GLA_KEY, HEAD_DIM), F32)],
        input_output_aliases={8: 0},
        scratch_shapes=[pltpu.VMEM((n_b * HEADS, GLA_KEY, HEAD_DIM), F32)],
        compiler_params=_cparams(("arbitrary",)),
    )(proj3, proj3, proj3, proj3, proj3, w2, b2, gn, o_mix)
    return o.reshape(n_b * t_len, 2 * 512), hist


def _gla_scan_bwd(proj, w2, b2, gn, hist, d_o, dsm_dn, d_proj, n_b, t_len):
    n_c = t_len // CHUNK
    rev = functools.partial(_chunk_spec, n_b, n_c=n_c, reverse=True)

    def body(q_ref, k_ref, v_ref, g_ref, sm_ref, w2_ref, b2_ref, gn_ref, hist_ref, do_ref, dsm_dn_ref, _,
             dp_ref, dw2_ref, db2_ref, dgn_ref, ds_ref):
        first = pl.program_id(0) == 0

        @pl.when(first)
        def _():
            ds_ref[...] = jnp.zeros_like(ds_ref)

        _, pull = jax.vjp(_gla_chains, _stack_chains(q_ref, n_b, GLA_KSL), _stack_chains(k_ref, n_b, GLA_KSL),
                          _stack_chains(v_ref, n_b, DN_SL), _stack_chains(g_ref, n_b, DN_SL),
                          _per_chain(sm_ref, n_b), hist_ref[...], *_gate_weights(w2_ref, b2_ref, n_b), gn_ref[...])
        dq, dk, dv, dg, dsm, ds_in, dw2, db2, dgn = pull((_stack_chains(do_ref, n_b, DN_SL), ds_ref[...]))
        _unstack_chains(dp_ref, dq, n_b, GLA_KSL, offset=OFF_GQ)
        _unstack_chains(dp_ref, dk, n_b, GLA_KSL, offset=OFF_GK)
        _unstack_chains(dp_ref, dv, n_b, DN_SL, offset=OFF_GV)
        _unstack_chains(dp_ref, dg, n_b, DN_SL, offset=OFF_GG)
        ds_ref[...] = ds_in
        for b, dsm_b in enumerate(_sum_heads(dsm, n_b)):
            dp_ref[b, :, OFF_SMALL:OFF_SMALL + 128] = (dsm_b + dsm_dn_ref[b]).astype(BF16)
            dp_ref[b, :, OFF_SMALL + 128:GLA_W] = jnp.zeros((CHUNK, GLA_W - OFF_SMALL - 128), BF16)
        for h, ks in enumerate(GLA_KSL):
            _acc(dw2_ref, sum(dw2[b * HEADS + h] for b in range(n_b)), first, at=(slice(None), ks))
            _acc(db2_ref, sum(db2[b * HEADS + h] for b in range(n_b)), first, at=(slice(None), ks))
        _acc(dgn_ref, dgn, first)

    proj3, do3 = proj.reshape(n_b, t_len, -1), d_o.reshape(n_b, t_len, -1)
    return pl.pallas_call(
        body, name="gla_scan_bwd", grid=(n_c,),
        in_specs=[rev(256, OFF_GQ // 256), rev(256, OFF_GK // 256), rev(512, OFF_GV // 512), rev(512, OFF_GG // 512),
                  rev(128, OFF_SMALL // 128),
                  _const_spec((128, 256)), _const_spec((1, 256)), _const_spec((1, 128)),
                  _hist_spec(n_b, GLA_KEY, n_c, reverse=True), rev(512, 1), rev(128, 0),
                  pl.BlockSpec(memory_space=pl.ANY)],
        out_specs=[rev(GLA_W, 0), _const_spec((128, 256)), _const_spec((1, 256)), _const_spec((1, 128))],
        out_shape=[jax.ShapeDtypeStruct(d_proj.shape, BF16), jax.ShapeDtypeStruct((128, 256), F32),
                   jax.ShapeDtypeStruct((1, 256), F32), jax.ShapeDtypeStruct((1, 128), F32)],
        input_output_aliases={11: 0},
        scratch_shapes=[pltpu.VMEM((n_b * HEADS, GLA_KEY, HEAD_DIM), F32)],
        compiler_params=_cparams(("arbitrary",)),
    )(proj3, proj3, proj3, proj3, proj3, w2, b2, gn, hist, do3, dsm_dn, d_proj)


W_IN_RUNS = ((0, 256, GLA_W), (256, 1536, OFF_Z + 512), (1536, 2048, OFF_Z), (2048, 2056, OFF_SMALL),
             (2056, 3592, 0), (3592, 3608, OFF_SMALL + 8))
W_IN_ROWS = 256


def _w_in_pieces(cols_per_chip):
    out = []
    for first, last, start in W_IN_RUNS:
        for j in range(N_CHIPS):
            a, b = max(first, cols_per_chip * j), min(last, cols_per_chip * (j + 1))
            if a < b:
                out.append((j, a - cols_per_chip * j, b - cols_per_chip * j, start + a - first))
    return out


def _w_in_to_padded(w4):
    _, n_r, n_c = w4.shape

    def body(i_ref, o_ref):
        o_ref[...] = jnp.zeros_like(o_ref)
        for j, a, b, p in _w_in_pieces(n_c):
            o_ref[:, p:p + b - a] = i_ref[j, :, a:b]

    return pl.pallas_call(
        body, name="w_in_to_padded", grid=(n_r // W_IN_ROWS,),
        in_specs=[pl.BlockSpec((N_CHIPS, W_IN_ROWS, n_c), lambda i: (0, i, 0))],
        out_specs=pl.BlockSpec((W_IN_ROWS, PROJ_W), lambda i: (i, 0)),
        out_shape=jax.ShapeDtypeStruct((n_r, PROJ_W), w4.dtype), compiler_params=_cparams(("parallel",)),
    )(w4)


def _w_in_to_chips(g, n_c):
    n_r = g.shape[0]

    def body(i_ref, o_ref):
        for j, a, b, p in _w_in_pieces(n_c):
            o_ref[j, :, a:b] = i_ref[:, p:p + b - a]

    return pl.pallas_call(
        body, name="w_in_to_chips", grid=(n_r // W_IN_ROWS,),
        in_specs=[pl.BlockSpec((W_IN_ROWS, PROJ_W), lambda i: (i, 0))],
        out_specs=pl.BlockSpec((N_CHIPS, W_IN_ROWS, n_c), lambda i: (0, i, 0)),
        out_shape=jax.ShapeDtypeStruct((N_CHIPS, n_r, n_c), g.dtype), compiler_params=_cparams(("parallel",)),
    )(g)


def _lane_vec(v, offset=0):
    return jnp.zeros((1, 128), F32).at[0, offset:offset + v.shape[0]].set(v)


def _local_step(x, tgt, mod, p, n_b, t_len, comm=None):
    row1 = lambda v: v.reshape(1, -1)
    a_log, dt_bias = _lane_vec(p["dn_a_log"]), _lane_vec(p["dn_dt_bias"])
    dn_gn, gla_gn = row1(p["dn_norm_g"]), row1(p["gla_norm_g"])
    w2 = jnp.zeros((128, 256), F32).at[8:8 + GATE_RANK].set(p["gla_w_gate2"])
    b2 = row1(p["gla_b_gate"])
    ln0_g, ln0_b, ln1_g, ln1_b, ln2_g, ln2_b = (row1(p[k]) for k in ("ln0_g", "ln0_b", "ln1_g", "ln1_b", "ln2_g", "ln2_b"))
    conv_b = row1(p["ffn_conv_b"])

    x0, h1 = _ln0_fwd(x, ln0_g, ln0_b, mod, n_b, t_len)
    proj = _mm(h1, p["w_in_p"], name="mm_proj")
    qkv = _dn_pre_fwd(proj, p["dn_conv"], n_b, t_len)
    o_half, hist_dn, landed = _dn_scan_fwd(qkv, proj, a_log, dt_bias, dn_gn, n_b, t_len,
                                           rider=comm.fwd_rider() if comm else None)
    if comm:
        p = {**p, **comm.weights_from(landed)}
    o_mix, hist_gla = _gla_scan_fwd(proj, w2, b2, gla_gn, o_half, n_b, t_len)
    y = _mm(o_mix, p["w_o"], name="mm_wo")
    x1, h2 = _ln1_fwd(x0, y, ln1_g, ln1_b, mod, n_b, t_len)
    up = _mm(h2, p["w_up"], name="mm_up")
    act = _ffn_act_fwd(up, p["ffn_conv"], conv_b, n_b, t_len)
    y2 = _mm(act, p["w_down"], name="mm_down")

    loss, dx1, dy2, g_ln2_g, g_ln2_b, dgt_f = _ln2_loss_bwd(x1, y2, ln2_g, ln2_b, mod, tgt, n_b, t_len)
    g_w_down = _mm(act, dy2, ta=True, name="mm_g_down")
    d_act = _mm(dy2, p["w_down"], tb=True, name="mm_d_act")
    d_up, g_ffn_conv, g_conv_b = _ffn_act_bwd(up, p["ffn_conv"], conv_b, d_act, n_b, t_len)
    g_w_up = _mm(h2, d_up, ta=True, out_slabs=N_CHIPS, name="mm_g_up")
    if comm:
        dh2, from_sibling = _mm(d_up, p["w_up"], tb=True, name="mm_d_h2", rider=comm.ffn_pair_rider(g_w_up, g_w_down))
    else:
        dh2 = _mm(d_up, p["w_up"], tb=True, name="mm_d_h2")
    dx0, dy, g_ln1_g, g_ln1_b, dmod_1 = _ln1_bwd(x0, y, ln1_g, ln1_b, mod, dx1, dh2, n_b, t_len)
    g_w_o = _mm(o_mix, dy, ta=True, name="mm_g_wo")
    d_o = _mm(dy, p["w_o"], tb=True, name="mm_d_o")
    dqkv, d_proj, dsm_dn, g_a_log, g_dt_bias, g_dn_gn, ffn_from_chips = _dn_scan_bwd(
        qkv, proj, a_log, dt_bias, dn_gn, hist_dn, d_o, n_b, t_len,
        rider=comm.ffn_chips_rider(from_sibling) if comm else None)
    d_proj, g_w2, g_b2, g_gla_gn = _gla_scan_bwd(proj, w2, b2, gla_gn, hist_gla, d_o, dsm_dn, d_proj, n_b, t_len)
    d_proj, g_dn_conv = _dn_pre_bwd(proj, p["dn_conv"], dqkv, d_proj.reshape(n_b * t_len, PROJ_W), n_b, t_len)
    g_w_in_p = _mm(h1, d_proj, ta=True, name="mm_g_win")
    if comm:
        dh1, tail_from_chips = _mm(d_proj, p["w_in_p"], tb=True, name="mm_d_h1",
                                   rider=comm.tail_chips_rider(g_w_in_p, g_w_o))
        from_chips = (ffn_from_chips, tail_from_chips)
    else:
        dh1, from_chips = _mm(d_proj, p["w_in_p"], tb=True, name="mm_d_h1"), None
    grad_x, g_ln0_g, g_ln0_b, dmod_0 = _ln0_bwd(x, ln0_g, ln0_b, mod, dx0, dh1, n_b, t_len)

    dmod = jnp.concatenate([dmod_0, dmod_1[:, 0:1], dmod_1[:, 1:3], dgt_f], axis=1)
    grads = {
        "ln0_g": g_ln0_g[0], "ln0_b": g_ln0_b[0], "w_in_p": g_w_in_p, "dn_conv": g_dn_conv,
        "dn_a_log": g_a_log[0, 0:HEADS], "dn_dt_bias": g_dt_bias[0, 0:HEADS], "dn_norm_g": g_dn_gn[0],
        "gla_w_gate2": g_w2[8:8 + GATE_RANK], "gla_b_gate": g_b2[0], "gla_norm_g": g_gla_gn[0],
        "w_o": g_w_o, "ln1_g": g_ln1_g[0], "ln1_b": g_ln1_b[0], "w_up": g_w_up,
        "ffn_conv": jnp.concatenate([g_ffn_conv[0], g_ffn_conv[1]], axis=1),
        "ffn_conv_b": jnp.concatenate([g_conv_b[0, 0], g_conv_b[1, 0]]), "w_down": g_w_down,
        "ln2_g": g_ln2_g[0], "ln2_b": g_ln2_b[0],
    }
    return loss, grad_x, grads, dmod, from_chips


def _ada_fwd(c_all, w_shard, b_shard):
    n_all, n_col = c_all.shape[0], w_shard.shape[1]
    tn = 512

    def body(c_ref, w_ref, b_ref, cond_ref, mod_ref):
        cond = _silu(c_ref[...])
        cond_ref[...] = cond
        mod_ref[...] = jnp.dot(cond.astype(BF16), w_ref[...].astype(BF16), preferred_element_type=F32) + b_ref[...]

    return pl.pallas_call(
        body, name="ada_fwd", grid=(n_col // tn,),
        in_specs=[pl.BlockSpec((n_all, D_MODEL), lambda j: (0, 0)), pl.BlockSpec((D_MODEL, tn), lambda j: (0, j)),
                  pl.BlockSpec((1, tn), lambda j: (0, j))],
        out_specs=[pl.BlockSpec((n_all, D_MODEL), lambda j: (0, 0)), pl.BlockSpec((n_all, tn), lambda j: (0, j))],
        out_shape=[jax.ShapeDtypeStruct((n_all, D_MODEL), F32), jax.ShapeDtypeStruct((n_all, n_col), F32)],
        compiler_params=_cparams(("arbitrary",)),
    )(c_all, w_shard, b_shard)


def _col_sum(a):
    def body(a_ref, o_ref):
        o_ref[...] = jnp.sum(a_ref[...], 0, keepdims=True)

    return pl.pallas_call(body, name="col_sum", out_shape=jax.ShapeDtypeStruct((1, a.shape[1]), F32))(a)


def _adamw_math(w, grad, m, v):
    new_m = ADAM_B1 * m + (1.0 - ADAM_B1) * grad
    new_v = ADAM_B2 * v + (1.0 - ADAM_B2) * (grad * grad)
    m_hat = new_m / (1.0 - ADAM_B1 ** ADAM_STEP)
    v_hat = new_v / (1.0 - ADAM_B2 ** ADAM_STEP)
    return -ADAM_LR * (m_hat / (jnp.sqrt(v_hat) + ADAM_EPS) + ADAM_WD * w), new_m, new_v


def _adamw_many(ws, gs, ms, vs):
    n = len(ws)

    def body(*refs):
        for i in range(n):
            w_ref, g_ref, m_ref, v_ref = (refs[k * n + i] for k in range(4))
            d_ref, nm_ref, nv_ref = (refs[(4 + k) * n + i] for k in range(3))
            d_ref[...], nm_ref[...], nv_ref[...] = _adamw_math(w_ref[...], g_ref[...], m_ref[...], v_ref[...])

    outs = pl.pallas_call(
        body, name="adamw_small", out_shape=[jax.ShapeDtypeStruct(w.shape, F32) for w in ws] * 3,
    )(*ws, *gs, *ms, *vs)
    return outs[:n], outs[n:2 * n], outs[2 * n:]


def _adamw(w, g, m, v, name, rider=None):
    n_r, n_c = w.shape
    if n_r % 8 == 0:
        tr = _pick(n_r, (256, 64, 32, 16, 8))
        grid, blk = (n_r // tr,), pl.BlockSpec((tr, n_c), lambda i: (i, 0))
    else:
        tc = _pick(n_c, (256, 128))
        grid, blk = (n_c // tc,), pl.BlockSpec((n_r, tc), lambda i: (0, i))
    r_inputs, r_in_specs, r_out_specs, r_sems, split = _with_rider(rider, 4, 3, 0)

    def body(*refs):
        (w_ref, g_ref, m_ref, v_ref, d_ref, nm_ref, nv_ref), parts = split(refs)
        ride_first, ride_last = _ride(rider, parts, grid)
        if rider is not None:
            ride_first()
        d_ref[...], nm_ref[...], nv_ref[...] = _adamw_math(w_ref[...], g_ref[...], m_ref[...], v_ref[...])
        if rider is not None:
            ride_last()

    out = jax.ShapeDtypeStruct(w.shape, F32)
    d_n, m_n, v_n, *rider_outs = pl.pallas_call(
        body, name=name, grid=grid, in_specs=[blk] * 4 + r_in_specs, out_specs=[blk] * 3 + r_out_specs,
        out_shape=[out] * 3 + (list(rider.out_shapes) if rider else []), scratch_shapes=r_sems,
        compiler_params=_cparams(("arbitrary",) if rider else ("parallel",)),
    )(w, g, m, v, *r_inputs)
    return ((d_n, m_n, v_n), rider_outs) if rider else (d_n, m_n, v_n)


HBM_SPEC = pl.BlockSpec(memory_space=pltpu.HBM)
VMEM_SPEC = pl.BlockSpec(memory_space=pltpu.VMEM)
CHIP_FLIPS = ((1, 0), (0, 1), (1, 1))


def _place():
    return lax.axis_index("x"), lax.axis_index("y"), lax.axis_index("c")


def _flip(v, f):
    return 1 - v if f else v


def _all_gather8(slab, name, rider=None):
    n_r, n_w = slab.shape
    r_inputs, r_in_specs, r_out_specs, r_sems, split = _with_rider(rider, 1, 2, 3)

    def body(*refs):
        (x_ref, o_ref, s_ref, send_sems, recv_sems, local_sem), parts = split(refs)
        if rider is not None:
            rider.first(*parts)
        x, y, c = _place()
        me = 4 * x + 2 * y + c
        mine = pltpu.make_async_copy(x_ref, o_ref.at[me], local_sem)
        mine.start()
        peers = [(_flip(x, k & 4), _flip(y, k & 2), _flip(c, k & 1)) for k in range(1, N_DEV)]
        sends = []
        for k, peer in enumerate(peers):
            cp = pltpu.make_async_remote_copy(src_ref=x_ref, dst_ref=o_ref.at[me], send_sem=send_sems.at[k],
                                              recv_sem=recv_sems.at[k], device_id=peer, device_id_type=MESH)
            cp.start()
            sends.append(cp)
        for k, (px, py, pc) in enumerate(peers):
            pltpu.make_async_remote_copy(src_ref=x_ref, dst_ref=o_ref.at[4 * px + 2 * py + pc],
                                         send_sem=send_sems.at[k], recv_sem=recv_sems.at[k],
                                         device_id=(px, py, pc), device_id_type=MESH).wait_recv()
        for cp in sends:
            cp.wait_send()
        mine.wait()
        total = o_ref[0]
        for d in range(1, N_DEV):
            total = total + o_ref[d]
        s_ref[...] = total
        if rider is not None:
            rider.last(*parts)

    gathered, total, *rider_outs = pl.pallas_call(
        body, name=name, in_specs=[VMEM_SPEC] + r_in_specs, out_specs=[VMEM_SPEC, VMEM_SPEC] + r_out_specs,
        out_shape=[jax.ShapeDtypeStruct((N_DEV, n_r, n_w), F32), jax.ShapeDtypeStruct((n_r, n_w), F32)]
        + (list(rider.out_shapes) if rider else []),
        scratch_shapes=[pltpu.SemaphoreType.DMA((N_DEV - 1,)), pltpu.SemaphoreType.DMA((N_DEV - 1,)),
                        pltpu.SemaphoreType.DMA] + r_sems,
    )(slab, *r_inputs)
    return (gathered, total, rider_outs) if rider else (gathered, total)


def _gather8_rider(slab):
    def plan(ins, outs, sems):
        send_sems, recv_sems, local_sem = sems
        x, y, c = _place()
        me = 4 * x + 2 * y + c
        peers = [(_flip(x, k & 4), _flip(y, k & 2), _flip(c, k & 1)) for k in range(1, N_DEV)]
        mine = pltpu.make_async_copy(ins[0], outs[0].at[me], local_sem.at[0])
        sends = [pltpu.make_async_remote_copy(src_ref=ins[0], dst_ref=outs[0].at[me], send_sem=send_sems.at[k],
                                              recv_sem=recv_sems.at[k], device_id=peer, device_id_type=MESH)
                 for k, peer in enumerate(peers)]
        recvs = [pltpu.make_async_remote_copy(src_ref=ins[0], dst_ref=outs[0].at[4 * px + 2 * py + pc],
                                              send_sem=send_sems.at[k], recv_sem=recv_sems.at[k],
                                              device_id=(px, py, pc), device_id_type=MESH)
                 for k, (px, py, pc) in enumerate(peers)]
        return mine, sends, recvs

    def first_step(ins, outs, sems):
        mine, sends, _ = plan(ins, outs, sems)
        mine.start()
        for cp in sends:
            cp.start()

    def last_step(ins, outs, sems):
        mine, sends, recvs = plan(ins, outs, sems)
        for cp in recvs:
            cp.wait_recv()
        for cp in sends:
            cp.wait_send()
        mine.wait()

    return Rider([slab], [jax.ShapeDtypeStruct((N_DEV,) + slab.shape, F32)],
                 [pltpu.SemaphoreType.DMA((N_DEV - 1,)), pltpu.SemaphoreType.DMA((N_DEV - 1,)),
                  pltpu.SemaphoreType.DMA((1,))], first_step, last_step)


def _sum8(gathered):
    def body(g_ref, o_ref):
        total = g_ref[0]
        for d in range(1, N_DEV):
            total = total + g_ref[d]
        o_ref[...] = total

    return pl.pallas_call(body, name="sum8", out_shape=jax.ShapeDtypeStruct(gathered.shape[1:], F32))(gathered)


def _gather_rider(shards):
    n_a = len(shards)

    def plan(ins, outs, sems):
        send_sems, recv_sems = sems
        x, y, c = _place()
        chips = [(_flip(x, fx), _flip(y, fy)) for fx, fy in CHIP_FLIPS]

        def copy(k, slot, chip_of_block, half, to, src=None):
            dst = outs[k].at[chip_of_block, half]
            return pltpu.make_async_remote_copy(src_ref=dst if src is None else src, dst_ref=dst,
                                                send_sem=send_sems.at[k * 6 + slot], recv_sem=recv_sems.at[k * 6 + slot],
                                                device_id=to, device_id_type=MESH)

        first = [copy(k, r, 2 * x + y, c, (*chips[r], c), src=ins[k].at[c]) for k in range(n_a) for r in range(3)]
        return copy, chips, first, (x, y, c)

    def first_step(ins, outs, sems):
        for cp in plan(ins, outs, sems)[2]:
            cp.start()

    def last_step(ins, outs, sems):
        copy, chips, first, (x, y, c) = plan(ins, outs, sems)
        passed = []
        for k in range(n_a):
            for r, (px, py) in enumerate(chips):
                copy(k, r, 2 * px + py, c, (x, y, c)).wait_recv()
                fwd = copy(k, 3 + r, 2 * px + py, c, (x, y, 1 - c))
                fwd.start()
                passed.append(fwd)
        for k in range(n_a):
            for r, (px, py) in enumerate(chips):
                copy(k, 3 + r, 2 * px + py, 1 - c, (x, y, c)).wait_recv()
        for cp in first + passed:
            cp.wait_send()

    return Rider(shards, [jax.ShapeDtypeStruct((N_CHIPS,) + s.shape, s.dtype) for s in shards],
                 [pltpu.SemaphoreType.DMA((6 * n_a,)), pltpu.SemaphoreType.DMA((6 * n_a,))], first_step, last_step)


def _place_own(gathered, shard, chip, name):
    _, _, n_h, n_c = gathered.shape
    th = _pick(n_h, (256, 176, 128))

    def body(sel_ref, s_ref, _, o_ref):
        o_ref[...] = s_ref[...]

    grid_spec = pltpu.PrefetchScalarGridSpec(
        num_scalar_prefetch=1, grid=(2, n_h // th),
        in_specs=[pl.BlockSpec((None, th, n_c), lambda hf, i, sel: (hf, i, 0)), pl.BlockSpec(memory_space=pl.ANY)],
        out_specs=pl.BlockSpec((None, None, th, n_c), lambda hf, i, sel: (sel[0], hf, i, 0)))
    return pl.pallas_call(
        body, name=name, grid_spec=grid_spec, out_shape=jax.ShapeDtypeStruct(gathered.shape, gathered.dtype),
        input_output_aliases={2: 0}, compiler_params=_cparams(("parallel", "parallel")),
    )(chip.reshape(1), shard, gathered)


def _pair_rider(parts):
    n_a = len(parts)

    def plan(ins, outs, sems):
        send_sems, recv_sems = sems
        x, y, c = _place()
        return [pltpu.make_async_remote_copy(src_ref=ins[k].at[:, 1 - c], dst_ref=outs[k], send_sem=send_sems.at[k],
                                             recv_sem=recv_sems.at[k], device_id=(x, y, 1 - c), device_id_type=MESH)
                for k in range(n_a)]

    def first_step(ins, outs, sems):
        for cp in plan(ins, outs, sems):
            cp.start()

    def last_step(ins, outs, sems):
        for cp in plan(ins, outs, sems):
            cp.wait()

    return Rider(parts, [jax.ShapeDtypeStruct((N_CHIPS,) + p.shape[2:], F32) for p in parts],
                 [pltpu.SemaphoreType.DMA((n_a,)), pltpu.SemaphoreType.DMA((n_a,))], first_step, last_step)


def _alone(rider, name):
    n_a = len(rider.inputs)

    def body(*refs):
        parts = (refs[:n_a], refs[n_a:2 * n_a], refs[2 * n_a:])
        rider.first(*parts)
        rider.last(*parts)

    return pl.pallas_call(
        body, name=name, in_specs=[HBM_SPEC] * n_a, out_specs=[HBM_SPEC] * n_a,
        out_shape=rider.out_shapes, scratch_shapes=rider.sems,
    )(*rider.inputs)


def _chips_rider(sums):
    n_a = len(sums)

    def plan(ins, outs, sems):
        send_sems, recv_sems = sems
        x, y, c = _place()
        cps = []
        for k in range(n_a):
            for r, (fx, fy) in enumerate(CHIP_FLIPS):
                px, py = _flip(x, fx), _flip(y, fy)
                cps.append(pltpu.make_async_remote_copy(
                    src_ref=ins[k].at[2 * px + py], dst_ref=outs[k].at[r], send_sem=send_sems.at[3 * k + r],
                    recv_sem=recv_sems.at[3 * k + r], device_id=(px, py, c), device_id_type=MESH))
        return cps

    def first_step(ins, outs, sems):
        for cp in plan(ins, outs, sems):
            cp.start()

    def last_step(ins, outs, sems):
        for cp in plan(ins, outs, sems):
            cp.wait()

    return Rider(sums, [jax.ShapeDtypeStruct((3,) + s.shape[1:], s.dtype) for s in sums],
                 [pltpu.SemaphoreType.DMA((3 * n_a,)), pltpu.SemaphoreType.DMA((3 * n_a,))], first_step, last_step)


def _rs_share(bufs):
    n_a = len(bufs)

    def body(*refs):
        ins, outs = refs[:n_a], refs[n_a:2 * n_a]
        send_sems, recv_sems = refs[2 * n_a:]
        x, y, c = _place()
        sends = [pltpu.make_async_remote_copy(src_ref=ins[k].at[c], dst_ref=outs[k].at[c], send_sem=send_sems.at[k],
                                              recv_sem=recv_sems.at[k], device_id=(x, y, 1 - c), device_id_type=MESH)
                 for k in range(n_a)]
        for cp in sends:
            cp.start()
        for k in range(n_a):
            pltpu.make_async_remote_copy(src_ref=ins[k].at[c], dst_ref=outs[k].at[1 - c], send_sem=send_sems.at[k],
                                         recv_sem=recv_sems.at[k], device_id=(x, y, 1 - c),
                                         device_id_type=MESH).wait_recv()
        for cp in sends:
            cp.wait_send()

    return pl.pallas_call(
        body, name="rs_share", in_specs=[HBM_SPEC] * n_a, out_specs=[HBM_SPEC] * n_a,
        out_shape=[jax.ShapeDtypeStruct(s.shape, F32) for s in bufs],
        input_output_aliases={k: k for k in range(n_a)},
        scratch_shapes=[pltpu.SemaphoreType.DMA((n_a,)), pltpu.SemaphoreType.DMA((n_a,))],
    )(*bufs)


def _pair_add(part, recv, core, name):
    _, _, n_h, n_c = part.shape
    th = _pick(n_h, (256, 176, 128))

    def body(sel_ref, p_ref, r_ref, o_ref):
        o_ref[...] = (p_ref[...] + r_ref[...]).astype(BF16)

    grid_spec = pltpu.PrefetchScalarGridSpec(
        num_scalar_prefetch=1, grid=(N_CHIPS, n_h // th),
        in_specs=[pl.BlockSpec((None, None, th, n_c), lambda j, i, sel: (j, sel[0], i, 0)),
                  pl.BlockSpec((None, th, n_c), lambda j, i, sel: (j, i, 0))],
        out_specs=pl.BlockSpec((None, th, n_c), lambda j, i, sel: (j, i, 0)))
    return pl.pallas_call(
        body, name=name, grid_spec=grid_spec, out_shape=jax.ShapeDtypeStruct(recv.shape, BF16),
        compiler_params=_cparams(("parallel", "parallel")),
    )(core.reshape(1), part, recv)


def _chip_add(sums, recv, chip, core, name):
    _, n_h, n_c = sums.shape
    th = _pick(n_h, (256, 176, 128))

    def body(sel_ref, s_ref, r_ref, o_ref):
        total = s_ref[...].astype(F32)
        for r in range(3):
            total = total + r_ref[r].astype(F32)
        o_ref[...] = total

    grid_spec = pltpu.PrefetchScalarGridSpec(
        num_scalar_prefetch=1, grid=(n_h // th,),
        in_specs=[pl.BlockSpec((None, th, n_c), lambda i, sel: (sel[0], i, 0)),
                  pl.BlockSpec((3, th, n_c), lambda i, sel: (0, i, 0))],
        out_specs=pl.BlockSpec((None, th, n_c), lambda i, sel: (sel[1], i, 0)))
    return pl.pallas_call(
        body, name=name, grid_spec=grid_spec, out_shape=jax.ShapeDtypeStruct((2, n_h, n_c), F32),
        compiler_params=_cparams(("parallel",)),
    )(jnp.stack([chip, core]), sums, recv)


def _row_halves(a):
    return a.reshape(N_CHIPS, 2, -1, a.shape[-1])


class StepComm:
    REST = ("w_o", "w_up", "w_down")

    def __init__(self, core, chip, rest_shards, in_cols):
        self.core, self.chip, self.shards, self.in_cols = core, chip, rest_shards, in_cols

    def fwd_rider(self):
        return _gather_rider(self.shards)

    def weights_from(self, landed):
        g_o, g_up, g_down = (_place_own(g, s, self.chip, "place_own_" + n)
                             for g, s, n in zip(landed, self.shards, self.REST))
        return {"w_o": g_o.reshape(-1, D_MODEL), "w_up": g_up.reshape(N_CHIPS, -1, g_up.shape[-1]),
                "w_down": g_down.reshape(-1, D_MODEL)}

    def _add_pairs(self, parts, from_sibling, names):
        return [_pair_add(p, r, self.core, "pair_add_" + n) for p, r, n in zip(parts, from_sibling, names)]

    def ffn_pair_rider(self, g_w_up, g_w_down):
        self.ffn_parts = [_row_halves(g_w_up), _row_halves(g_w_down)]
        return _pair_rider(self.ffn_parts)

    def ffn_chips_rider(self, from_sibling):
        self.ffn_sums = self._add_pairs(self.ffn_parts, from_sibling, ("w_up", "w_down"))
        return _chips_rider(self.ffn_sums)

    def tail_chips_rider(self, g_w_in_p, g_w_o):
        parts = [_row_halves(_w_in_to_chips(g_w_in_p, self.in_cols)), _row_halves(g_w_o)]
        self.tail_sums = self._add_pairs(parts, _alone(_pair_rider(parts), "rs_pair_tail"), ("w_in", "w_o"))
        return _chips_rider(self.tail_sums)

    def finish(self, ffn_from_chips, tail_from_chips):
        halves = [_chip_add(s, r, self.chip, self.core, "chip_add_" + n)
                  for s, r, n in zip(self.tail_sums + self.ffn_sums, list(tail_from_chips) + list(ffn_from_chips),
                                     ("w_in", "w_o", "w_up", "w_down"))]
        return [f.reshape(-1, f.shape[-1]) for f in _rs_share(halves)]


SLAB_W = 1024


def _pack(arrays, rows):
    flat = jnp.concatenate([a.reshape(-1).astype(F32) for a in arrays])
    return jnp.pad(flat, (0, rows * SLAB_W - flat.shape[0])).reshape(rows, SLAB_W)


def _unpack(flat, shapes):
    out, off = [], 0
    for s in shapes:
        n = 1
        for d in s:
            n *= d
        out.append(flat[off:off + n].reshape(s))
        off += n
    return out


def _rows_for(arrays_or_shapes):
    n = 0
    for a in arrays_or_shapes:
        s = a if isinstance(a, tuple) else a.shape
        k = 1
        for d in s:
            k *= d
        n += k
    return -(-n // (8 * SLAB_W)) * 8


def kernel(x, c, ln0_g, ln0_b, w_ada, b_ada, w_in, dn_conv, dn_a_log, dn_dt_bias, dn_norm_g, gla_w_gate2, gla_b_gate, gla_norm_g, w_o, ln1_g, ln1_b, ffn_w_up, ffn_conv, ffn_conv_b, ffn_w_down, ln2_g, ln2_b, loss_target, m_ln0_g, m_ln0_b, m_w_ada, m_b_ada, m_w_in, m_dn_conv, m_dn_a_log, m_dn_dt_bias, m_dn_norm_g, m_gla_w_gate2, m_gla_b_gate, m_gla_norm_g, m_w_o, m_ln1_g, m_ln1_b, m_ffn_w_up, m_ffn_conv, m_ffn_conv_b, m_ffn_w_down, m_ln2_g, m_ln2_b, v_ln0_g, v_ln0_b, v_w_ada, v_b_ada, v_w_in, v_dn_conv, v_dn_a_log, v_dn_dt_bias, v_dn_norm_g, v_gla_w_gate2, v_gla_b_gate, v_gla_norm_g, v_w_o, v_ln1_g, v_ln1_b, v_ffn_w_up, v_ffn_conv, v_ffn_conv_b, v_ffn_w_down, v_ln2_g, v_ln2_b):
    n_b, t_len, _ = x.shape
    xi, yi, ci = _place()
    chip = (2 * xi + yi).astype(jnp.int32)
    core = ci.astype(jnp.int32)
    me = 2 * chip + core
    n_all = N_DEV * n_b
    ada_cols = w_ada.shape[2]

    halves = lambda a: a.astype(BF16).reshape(2, a.shape[0] // 2, a.shape[1])
    w_in_halves = halves(w_in[0])
    sharded_small = [dn_conv[0], gla_w_gate2[0], ffn_conv[0]]
    slab = _pack([c] + sharded_small, _rows_for([c] + sharded_small))
    gathered, _, (g_in,) = _all_gather8(slab, "gather_small", rider=_gather_rider([w_in_halves]))
    g_in = _place_own(g_in, w_in_halves, chip, "place_own_w_in")
    flat = gathered.reshape(N_DEV, -1)
    c_all = flat[:, :c.size].reshape(n_all, D_MODEL)
    by_chip = flat[0::2]
    full, off = [], c.size
    for a in sharded_small:
        blocks = by_chip[:, off:off + a.size].reshape(N_CHIPS, *a.shape)
        full.append(blocks.transpose(1, 0, 2).reshape(a.shape[0], N_CHIPS * a.shape[1]))
        off += a.size
    dn_conv_f, gate2_f, ffn_conv_f = full

    b_ada_shard = lax.dynamic_slice(b_ada, (0, chip * ada_cols), (1, ada_cols))
    cond_all, mod_cols = _ada_fwd(c_all, w_ada[0], b_ada_shard)
    mod_g, _ = _all_gather8(mod_cols, "gather_mod")
    mod_full = jnp.concatenate([mod_g[2 * j] for j in range(N_CHIPS)], axis=1)
    mod = lax.dynamic_slice(mod_full, (me * n_b, 0), (n_b, 6 * D_MODEL)).reshape(n_b, 6, D_MODEL)

    comm = StepComm(core, chip, [halves(w_o[0]), halves(ffn_w_up[0]), halves(ffn_w_down[0])], w_in.shape[2])
    params = {
        "w_in_p": _w_in_to_padded(g_in.reshape(N_CHIPS, -1, g_in.shape[-1])),
        "dn_conv": dn_conv_f, "dn_a_log": dn_a_log[0], "dn_dt_bias": dn_dt_bias[0], "dn_norm_g": dn_norm_g[0],
        "gla_w_gate2": gate2_f, "gla_b_gate": gla_b_gate[0], "gla_norm_g": gla_norm_g[0],
        "ln0_g": ln0_g, "ln0_b": ln0_b, "ln1_g": ln1_g[0], "ln1_b": ln1_b[0], "ln2_g": ln2_g[0], "ln2_b": ln2_b[0],
        "ffn_conv": ffn_conv_f, "ffn_conv_b": ffn_conv_b[0],
    }

    loss_row, grad_x, gp, dmod, from_chips = _local_step(
        x.reshape(n_b * t_len, D_MODEL), loss_target.reshape(n_b * t_len, D_MODEL), mod, params, n_b, t_len, comm)
    names = ["ln0_g", "ln0_b", "w_ada", "b_ada", "w_in", "dn_conv", "dn_a_log", "dn_dt_bias", "dn_norm_g",
             "gla_w_gate2", "gla_b_gate", "gla_norm_g", "w_o", "ln1_g", "ln1_b", "ffn_w_up", "ffn_conv", "ffn_conv_b",
             "ffn_w_down", "ln2_g", "ln2_b"]
    weights = dict(zip(names, [ln0_g, ln0_b, w_ada, b_ada, w_in, dn_conv, dn_a_log, dn_dt_bias, dn_norm_g, gla_w_gate2,
                               gla_b_gate, gla_norm_g, w_o, ln1_g, ln1_b, ffn_w_up, ffn_conv, ffn_conv_b, ffn_w_down,
                               ln2_g, ln2_b]))
    m_in = dict(zip(names, [m_ln0_g, m_ln0_b, m_w_ada, m_b_ada, m_w_in, m_dn_conv, m_dn_a_log, m_dn_dt_bias,
                            m_dn_norm_g, m_gla_w_gate2, m_gla_b_gate, m_gla_norm_g, m_w_o, m_ln1_g, m_ln1_b,
                            m_ffn_w_up, m_ffn_conv, m_ffn_conv_b, m_ffn_w_down, m_ln2_g, m_ln2_b]))
    v_in = dict(zip(names, [v_ln0_g, v_ln0_b, v_w_ada, v_b_ada, v_w_in, v_dn_conv, v_dn_a_log, v_dn_dt_bias,
                            v_dn_norm_g, v_gla_w_gate2, v_gla_b_gate, v_gla_norm_g, v_w_o, v_ln1_g, v_ln1_b,
                            v_ffn_w_up, v_ffn_conv, v_ffn_conv_b, v_ffn_w_down, v_ln2_g, v_ln2_b]))
    grads, delta, new_m, new_v = {}, {}, {}, {}

    def adamw_big(n, grad, rider=None):
        view = (lambda a: a.T) if n == "w_in" else (lambda a: a)
        outs = _adamw(view(weights[n][0]), view(grad), view(m_in[n][0]), view(v_in[n][0]), "adamw_" + n, rider=rider)
        (d_n, m_n, v_n), landed = outs if rider else (outs, None)
        grads[n], delta[n], new_m[n], new_v[n] = grad[None], view(d_n)[None], view(m_n)[None], view(v_n)[None]
        return landed

    g_w_in, g_w_o, g_w_up, g_w_down = comm.finish(*from_chips)

    summed_names = ["loss", "ln0_g", "ln0_b", "dn_conv", "dn_a_log", "dn_dt_bias", "dn_norm_g", "gla_w_gate2",
                    "gla_b_gate", "gla_norm_g", "ln1_g", "ln1_b", "ffn_conv", "ffn_conv_b", "ln2_g", "ln2_b"]
    summed_parts = [loss_row[0, 0:1]] + [gp[n] for n in summed_names[1:]]
    sum_rows = _rows_for(summed_parts)
    slab = jnp.concatenate([_pack(summed_parts, sum_rows), _pack([dmod], _rows_for([dmod]))], axis=0)
    (gathered,) = adamw_big("ffn_w_up", g_w_up, rider=_gather8_rider(slab))
    small_g = dict(zip(summed_names, _unpack(_sum8(gathered).reshape(-1), [a.shape for a in summed_parts])))
    loss = small_g["loss"][0]
    dmod_rows = n_b * 6 * D_MODEL // SLAB_W
    dmod_all = gathered[:, sum_rows:sum_rows + dmod_rows, :].reshape(n_all, 6 * D_MODEL)
    for n, grad in (("ffn_w_down", g_w_down), ("w_o", g_w_o), ("w_in", g_w_in)):
        adamw_big(n, grad)

    g_b_ada = _col_sum(dmod_all)
    dmod_cols = lax.dynamic_slice(dmod_all, (0, chip * ada_cols), (n_all, ada_cols))
    adamw_big("w_ada", _mm(cond_all, dmod_cols, ta=True, name="mm_g_ada"))

    col_block = lambda a: lax.dynamic_slice(a, (0, chip * (a.shape[1] // N_CHIPS)), (a.shape[0], a.shape[1] // N_CHIPS))
    grads.update({
        "ln0_g": small_g["ln0_g"], "ln0_b": small_g["ln0_b"], "b_ada": g_b_ada,
        "dn_conv": col_block(small_g["dn_conv"])[None], "dn_a_log": small_g["dn_a_log"][None],
        "dn_dt_bias": small_g["dn_dt_bias"][None], "dn_norm_g": small_g["dn_norm_g"][None],
        "gla_w_gate2": col_block(small_g["gla_w_gate2"])[None], "gla_b_gate": small_g["gla_b_gate"][None],
        "gla_norm_g": small_g["gla_norm_g"][None], "ln1_g": small_g["ln1_g"][None],
        "ln1_b": small_g["ln1_b"][None], "ffn_conv": col_block(small_g["ffn_conv"])[None],
        "ffn_conv_b": small_g["ffn_conv_b"][None], "ln2_g": small_g["ln2_g"][None], "ln2_b": small_g["ln2_b"][None],
    })
    small = [n for n in names if n not in delta]
    d_s, m_s, v_s = _adamw_many([weights[n] for n in small], [grads[n] for n in small],
                                [m_in[n] for n in small], [v_in[n] for n in small])
    for out, vals in ((delta, d_s), (new_m, m_s), (new_v, v_s)):
        out.update(zip(small, vals))

    return (loss, grad_x.reshape(x.shape), *[grads[n] for n in names], *[delta[n] for n in names],
            *[new_m[n] for n in names], *[new_v[n] for n in names])
```

```python
import functools

import jax
import jax.numpy as jnp
from jax import lax
from jax.experimental import pallas as pl
from jax.experimental.pallas import tpu as pltpu

F32 = jnp.float32
BF16 = jnp.bfloat16
MESH = pl.DeviceIdType.MESH

D_MODEL = 1024
HEADS = 4
HEAD_DIM = 128
GLA_KEY = 64
GATE_RANK = 16
CHUNK = 64
D_FF = 2816
ALPHA = 2.0 ** 0.25
EPS = 1e-6
N_CHIPS = 4
N_DEV = 8

PROJ_W = 3840
OFF_GQ, OFF_GK, OFF_GV, OFF_GG, OFF_SMALL, GLA_W = 0, 256, 512, 1024, 1536, 1792
OFF_Z = 2048
W_IN_COLS = 3608


def _qkv_block(j):
    return jnp.where(j < 2, GLA_W // 128 + j, (OFF_Z + 512) // 128 - 2 + j)

ADAM_LR, ADAM_B1, ADAM_B2, ADAM_EPS, ADAM_WD, ADAM_STEP = 0.001, 0.9, 0.999, 1e-08, 0.01, 10

VMEM_LIMIT = 56 * 1024 * 1024
ROW_TILE = 512


def _cparams(sem):
    return pltpu.CompilerParams(dimension_semantics=sem, vmem_limit_bytes=VMEM_LIMIT)


def _pick(n, prefs):
    for p in prefs:
        if n % p == 0:
            return p
    return n


def _mm(a, b, *, ta=False, tb=False, out_slabs=1, out_dtype=F32, name, rider=None):
    a_slabs = a.shape[0] if a.ndim == 3 else 1
    b_slabs = b.shape[0] if b.ndim == 3 else 1
    assert not (ta and a_slabs > 1)
    a2, b2 = a.shape[-2:], b.shape[-2:]
    if ta:
        k_dim, m_dim = a2
    else:
        m_dim, k_dim = a2[0], a2[1] * a_slabs
    n_dim = b2[0] if tb else b2[1] * b_slabs
    k_slabs = max(a_slabs, b_slabs if tb else 1)
    n_slabs = max(out_slabs, 1 if tb else b_slabs)
    tm = _pick(m_dim, (1024, 1408, 512, 256, 128))
    tn = _pick(n_dim // n_slabs, (1536, 1408, 1280, 1024, 768, 512, 384, 256, 128))
    tk = _pick(k_dim // k_slabs, (1408, 1280, 1024, 512, 256, 128))
    nk, nj = k_dim // tk, n_dim // tn
    nk_a, nk_b, nj_b, nj_o = nk // a_slabs, nk // b_slabs, nj // b_slabs, nj // out_slabs
    dims = (((0 if ta else 1,), (1 if tb else 0,)), ((), ()))

    grid = (m_dim // tm, nj, nk)
    assert out_dtype == F32
    r_inputs, r_in_specs, r_out_specs, r_sems, split = _with_rider(rider, 2, 1, 0)

    def body(*refs):
        (a_ref, b_ref, o_ref), parts = split(refs)
        ride_first, ride_last = _ride(rider, parts, grid)
        if rider is not None:
            ride_first()
        prod = lax.dot_general(a_ref[...].astype(BF16), b_ref[...].astype(BF16), dims, preferred_element_type=F32)
        if nk == 1:
            o_ref[...] = prod
        else:
            _acc(o_ref, prod, pl.program_id(2) == 0)
        if rider is not None:
            ride_last()

    if ta:
        a_spec = pl.BlockSpec((tk, tm), lambda i, j, k: (k, i))
    elif a_slabs > 1:
        a_spec = pl.BlockSpec((None, tm, tk), lambda i, j, k: (k // nk_a, i, k % nk_a))
    else:
        a_spec = pl.BlockSpec((tm, tk), lambda i, j, k: (i, k))
    if tb and b_slabs > 1:
        b_spec = pl.BlockSpec((None, tn, tk), lambda i, j, k: (k // nk_b, j, k % nk_b))
    elif tb:
        b_spec = pl.BlockSpec((tn, tk), lambda i, j, k: (j, k))
    elif b_slabs > 1:
        b_spec = pl.BlockSpec((None, tk, tn), lambda i, j, k: (j // nj_b, k, j % nj_b))
    else:
        b_spec = pl.BlockSpec((tk, tn), lambda i, j, k: (k, j))
    if out_slabs > 1:
        o_spec = pl.BlockSpec((None, tm, tn), lambda i, j, k: (j // nj_o, i, j % nj_o))
        o_shape = (out_slabs, m_dim, n_dim // out_slabs)
    else:
        o_spec, o_shape = pl.BlockSpec((tm, tn), lambda i, j, k: (i, j)), (m_dim, n_dim)
    out, *rider_outs = pl.pallas_call(
        body, name=name, grid=grid,
        in_specs=[a_spec, b_spec] + r_in_specs, out_specs=[o_spec] + r_out_specs,
        out_shape=[jax.ShapeDtypeStruct(o_shape, out_dtype)] + (list(rider.out_shapes) if rider else []),
        scratch_shapes=r_sems,
        compiler_params=_cparams(("arbitrary",) * 3 if rider else ("parallel", "parallel", "arbitrary")),
    )(a, b, *r_inputs)
    return (out, rider_outs) if rider else out


def _ln(x, g, b):
    mu = jnp.mean(x, -1, keepdims=True)
    xc = x - mu
    var = jnp.mean(xc * xc, -1, keepdims=True)
    return xc * lax.rsqrt(var + EPS) * g + b


def _softplus(x):
    return jnp.maximum(x, 0.0) + jnp.log(1.0 + jnp.exp(-jnp.abs(x)))


def _silu(x):
    return x * jax.nn.sigmoid(x)


def _dsilu(x):
    s = jax.nn.sigmoid(x)
    return s * (1.0 + x * (1.0 - s))


def _f_ln0(x, g, b, sc, sh):
    x0 = _ln(x, g, b)
    return x0, x0 * (1.0 + sc) + sh


def _f_ln1(x0, y, gt, g, b, sc, sh):
    x1 = _ln(ALPHA * x0 + (1.0 + gt) * y, g, b)
    return x1, x1 * (1.0 + sc) + sh


def _f_ln2_loss(x1, y2, gt, g, b, tgt):
    x2 = _ln(ALPHA * x1 + (1.0 + gt) * y2, g, b)
    err = x2 - tgt
    per_row = jnp.sum(err * err, -1, keepdims=True) * (0.5 / D_MODEL)
    return jnp.sum(per_row, 0, keepdims=True)


def _row_specs(t_len):
    nt = t_len // ROW_TILE
    row = pl.BlockSpec((ROW_TILE, D_MODEL), lambda b, i: (b * nt + i, 0))
    vec = pl.BlockSpec((1, D_MODEL), lambda b, i: (0, 0))
    mod = pl.BlockSpec((None, 6, D_MODEL), lambda b, i: (b, 0, 0))
    return nt, row, vec, mod


def _first_step():
    return jnp.logical_and(pl.program_id(0) == 0, pl.program_id(1) == 0)


def _acc(ref, val, first, at=(Ellipsis,)):
    @pl.when(first)
    def _():
        ref[at] = val

    @pl.when(jnp.logical_not(first))
    def _():
        ref[at] += val


def _acc_rows(ref, rows, first):
    for i, r in enumerate(rows):
        _acc(ref, r, first, at=(slice(i, i + 1), slice(None)))


def _ln0_fwd(x, g, b, mod, n_b, t_len):
    nt, row, vec, mods = _row_specs(t_len)

    def body(x_ref, g_ref, b_ref, mod_ref, x0_ref, h_ref):
        x0, h = _f_ln0(x_ref[...], g_ref[...], b_ref[...], mod_ref[1:2, :], mod_ref[0:1, :])
        x0_ref[...] = x0
        h_ref[...] = h.astype(BF16)

    return pl.pallas_call(
        body, name="ln0_fwd", grid=(n_b, nt), in_specs=[row, vec, vec, mods], out_specs=[row, row],
        out_shape=[jax.ShapeDtypeStruct(x.shape, F32), jax.ShapeDtypeStruct(x.shape, BF16)],
        compiler_params=_cparams(("parallel", "parallel")),
    )(x, g, b, mod)


def _ln0_bwd(x, g, b, mod, dx0, dh, n_b, t_len):
    nt, row, vec, mods = _row_specs(t_len)
    dmod_spec = pl.BlockSpec((None, 2, D_MODEL), lambda bb, i: (bb, 0, 0))

    def body(x_ref, g_ref, b_ref, mod_ref, dx0_ref, dh_ref, dx_ref, dg_ref, db_ref, dmod_ref):
        _, pull = jax.vjp(_f_ln0, x_ref[...], g_ref[...], b_ref[...], mod_ref[1:2, :], mod_ref[0:1, :])
        dx, dg, db, dsc, dsh = pull((dx0_ref[...], dh_ref[...]))
        dx_ref[...] = dx
        _acc(dg_ref, dg, _first_step())
        _acc(db_ref, db, _first_step())
        _acc_rows(dmod_ref, [dsh, dsc], pl.program_id(1) == 0)

    return pl.pallas_call(
        body, name="ln0_bwd", grid=(n_b, nt), in_specs=[row, vec, vec, mods, row, row],
        out_specs=[row, vec, vec, dmod_spec],
        out_shape=[jax.ShapeDtypeStruct(x.shape, F32), jax.ShapeDtypeStruct((1, D_MODEL), F32),
                   jax.ShapeDtypeStruct((1, D_MODEL), F32), jax.ShapeDtypeStruct((n_b, 2, D_MODEL), F32)],
        compiler_params=_cparams(("arbitrary", "arbitrary")),
    )(x, g, b, mod, dx0, dh)


def _ln1_fwd(x0, y, g, b, mod, n_b, t_len):
    nt, row, vec, mods = _row_specs(t_len)

    def body(x0_ref, y_ref, g_ref, b_ref, mod_ref, x1_ref, h_ref):
        x1, h = _f_ln1(x0_ref[...], y_ref[...], mod_ref[2:3, :], g_ref[...], b_ref[...],
                       mod_ref[4:5, :], mod_ref[3:4, :])
        x1_ref[...] = x1
        h_ref[...] = h.astype(BF16)

    return pl.pallas_call(
        body, name="ln1_fwd", grid=(n_b, nt), in_specs=[row, row, vec, vec, mods], out_specs=[row, row],
        out_shape=[jax.ShapeDtypeStruct(x0.shape, F32), jax.ShapeDtypeStruct(x0.shape, BF16)],
        compiler_params=_cparams(("parallel", "parallel")),
    )(x0, y, g, b, mod)


def _ln1_bwd(x0, y, g, b, mod, dx1, dh, n_b, t_len):
    nt, row, vec, mods = _row_specs(t_len)
    dmod_spec = pl.BlockSpec((None, 3, D_MODEL), lambda bb, i: (bb, 0, 0))

    def body(x0_ref, y_ref, g_ref, b_ref, mod_ref, dx1_ref, dh_ref, dx0_ref, dy_ref, dg_ref, db_ref, dmod_ref):
        _, pull = jax.vjp(_f_ln1, x0_ref[...], y_ref[...], mod_ref[2:3, :], g_ref[...], b_ref[...],
                          mod_ref[4:5, :], mod_ref[3:4, :])
        dx0, dy, dgt, dg, db, dsc, dsh = pull((dx1_ref[...], dh_ref[...]))
        dx0_ref[...] = dx0
        dy_ref[...] = dy.astype(BF16)
        _acc(dg_ref, dg, _first_step())
        _acc(db_ref, db, _first_step())
        _acc_rows(dmod_ref, [dgt, dsh, dsc], pl.program_id(1) == 0)

    return pl.pallas_call(
        body, name="ln1_bwd", grid=(n_b, nt), in_specs=[row, row, vec, vec, mods, row, row],
        out_specs=[row, row, vec, vec, dmod_spec],
        out_shape=[jax.ShapeDtypeStruct(x0.shape, F32), jax.ShapeDtypeStruct(x0.shape, BF16),
                   jax.ShapeDtypeStruct((1, D_MODEL), F32), jax.ShapeDtypeStruct((1, D_MODEL), F32),
                   jax.ShapeDtypeStruct((n_b, 3, D_MODEL), F32)],
        compiler_params=_cparams(("arbitrary", "arbitrary")),
    )(x0, y, g, b, mod, dx1, dh)


def _ln2_loss_bwd(x1, y2, g, b, mod, tgt, n_b, t_len):
    nt, row, vec, mods = _row_specs(t_len)
    one = pl.BlockSpec((1, 128), lambda bb, i: (0, 0))
    dmod_spec = pl.BlockSpec((None, 1, D_MODEL), lambda bb, i: (bb, 0, 0))

    def body(x1_ref, y2_ref, g_ref, b_ref, mod_ref, t_ref, loss_ref, dx1_ref, dy2_ref, dg_ref, db_ref, dgt_ref):
        loss, pull = jax.vjp(functools.partial(_f_ln2_loss, tgt=t_ref[...]), x1_ref[...], y2_ref[...],
                             mod_ref[5:6, :], g_ref[...], b_ref[...])
        dx1, dy2, dgt, dg, db = pull(jnp.ones((1, 1), F32))
        dx1_ref[...] = dx1
        dy2_ref[...] = dy2.astype(BF16)
        _acc(loss_ref, jnp.broadcast_to(loss, (1, 128)), _first_step())
        _acc(dg_ref, dg, _first_step())
        _acc(db_ref, db, _first_step())
        _acc(dgt_ref, dgt, pl.program_id(1) == 0)

    return pl.pallas_call(
        body, name="ln2_loss_bwd", grid=(n_b, nt), in_specs=[row, row, vec, vec, mods, row],
        out_specs=[one, row, row, vec, vec, dmod_spec],
        out_shape=[jax.ShapeDtypeStruct((1, 128), F32), jax.ShapeDtypeStruct(x1.shape, F32),
                   jax.ShapeDtypeStruct(x1.shape, BF16), jax.ShapeDtypeStruct((1, D_MODEL), F32),
                   jax.ShapeDtypeStruct((1, D_MODEL), F32), jax.ShapeDtypeStruct((n_b, 1, D_MODEL), F32)],
        compiler_params=_cparams(("arbitrary", "arbitrary")),
    )(x1, y2, g, b, mod, tgt)


def _shift_down(x, s):
    if s == 0:
        return x
    rows = lax.broadcasted_iota(jnp.int32, x.shape, 0)
    return jnp.where(rows >= s, pltpu.roll(x, s, 0), 0.0)


def _shift_up(x, s):
    if s == 0:
        return x
    t_len = x.shape[0]
    rows = lax.broadcasted_iota(jnp.int32, x.shape, 0)
    return jnp.where(rows < t_len - s, pltpu.roll(x, t_len - s, 0), 0.0)


def _taps(x, k_w):
    return [_shift_down(x, k_w - 1 - k) for k in range(k_w)]


def _conv(taps, w):
    out = w[0:1, :] * taps[0]
    for k in range(1, len(taps)):
        out = out + w[k:k + 1, :] * taps[k]
    return out


def _conv_bwd(taps, w, du):
    k_w = len(taps)
    dx = w[k_w - 1:k_w, :] * du
    for k in range(k_w - 1):
        dx = dx + w[k:k + 1, :] * _shift_up(du, k_w - 1 - k)
    return dx, [jnp.sum(du * taps[k], 0, keepdims=True) for k in range(k_w)]


def _dn_pre_fwd(proj, conv_w, n_b, t_len):
    n_ct = 3 * HEADS
    k_w = conv_w.shape[0]

    def body(x_ref, w_ref, o_ref):
        o_ref[...] = _silu(_conv(_taps(x_ref[...], k_w), w_ref[...]))

    return pl.pallas_call(
        body, name="dn_pre_fwd", grid=(n_ct, n_b),
        in_specs=[pl.BlockSpec((t_len, 128), lambda j, b: (b, _qkv_block(j))),
                  pl.BlockSpec((k_w, 128), lambda j, b: (0, j))],
        out_specs=pl.BlockSpec((t_len, 128), lambda j, b: (b, j)),
        out_shape=jax.ShapeDtypeStruct((n_b * t_len, n_ct * 128), F32),
        compiler_params=_cparams(("parallel", "parallel")),
    )(proj, conv_w)


def _dn_pre_bwd(proj, conv_w, dqkv, d_proj, n_b, t_len):
    n_ct = 3 * HEADS
    k_w = conv_w.shape[0]

    def body(x_ref, w_ref, d_ref, _, dx_ref, dw_ref):
        taps, w = _taps(x_ref[...], k_w), w_ref[...]
        du = d_ref[...] * _dsilu(_conv(taps, w))
        dx, dw = _conv_bwd(taps, w, du)
        dx_ref[...] = dx.astype(BF16)
        _acc_rows(dw_ref, dw, pl.program_id(1) == 0)

    return pl.pallas_call(
        body, name="dn_pre_bwd", grid=(n_ct, n_b),
        in_specs=[pl.BlockSpec((t_len, 128), lambda j, b: (b, _qkv_block(j))),
                  pl.BlockSpec((k_w, 128), lambda j, b: (0, j)),
                  pl.BlockSpec((t_len, 128), lambda j, b: (b, j)), pl.BlockSpec(memory_space=pl.ANY)],
        out_specs=[pl.BlockSpec((t_len, 128), lambda j, b: (b, _qkv_block(j))),
                   pl.BlockSpec((k_w, 128), lambda j, b: (0, j))],
        out_shape=[jax.ShapeDtypeStruct(d_proj.shape, BF16), jax.ShapeDtypeStruct((k_w, n_ct * 128), F32)],
        input_output_aliases={3: 0},
        compiler_params=_cparams(("parallel", "arbitrary")),
    )(proj, conv_w, dqkv, d_proj)


FFN_TC = 256
FFN_NT = D_FF // FFN_TC


def _ffn_specs(t_len):
    blk = lambda off: pl.BlockSpec((t_len, FFN_TC), lambda j, b: (b, j + off))
    wblk = lambda off: pl.BlockSpec((3, FFN_TC), lambda j, b: (0, j + off))
    bblk = lambda off: pl.BlockSpec((1, FFN_TC), lambda j, b: (0, j + off))
    return [blk(0), blk(FFN_NT), wblk(0), wblk(FFN_NT), bblk(0), bblk(FFN_NT)]


def _ffn_act_fwd(up, conv_w, conv_b, n_b, t_len):
    def body(g_ref, v_ref, wg_ref, wv_ref, bg_ref, bv_ref, o_ref):
        ug = _conv(_taps(g_ref[...], 3), wg_ref[...]) + bg_ref[...]
        uv = _conv(_taps(v_ref[...], 3), wv_ref[...]) + bv_ref[...]
        o_ref[...] = (_silu(ug) * uv).astype(BF16)

    return pl.pallas_call(
        body, name="ffn_act_fwd", grid=(FFN_NT, n_b), in_specs=_ffn_specs(t_len),
        out_specs=pl.BlockSpec((t_len, FFN_TC), lambda j, b: (b, j)),
        out_shape=jax.ShapeDtypeStruct((n_b * t_len, D_FF), BF16),
        compiler_params=_cparams(("parallel", "parallel")),
    )(up, up, conv_w, conv_w, conv_b, conv_b)


def _ffn_act_bwd(up, conv_w, conv_b, da, n_b, t_len):
    def body(g_ref, v_ref, wg_ref, wv_ref, bg_ref, bv_ref, da_ref, dup_ref, dw_ref, db_ref):
        first = pl.program_id(1) == 0
        tg, tv, wg, wv = _taps(g_ref[...], 3), _taps(v_ref[...], 3), wg_ref[...], wv_ref[...]
        ug = _conv(tg, wg) + bg_ref[...]
        uv = _conv(tv, wv) + bv_ref[...]
        d_act = da_ref[...]
        sig = jax.nn.sigmoid(ug)
        d_v = d_act * (ug * sig)
        d_g = d_act * uv * (sig * (1.0 + ug * (1.0 - sig)))
        for slab, (taps, w, du) in enumerate(((tg, wg, d_g), (tv, wv, d_v))):
            dx, dw = _conv_bwd(taps, w, du)
            dup_ref[slab] = dx.astype(BF16)
            for k, dw_k in enumerate(dw):
                _acc(dw_ref, dw_k, first, at=(slab, slice(k, k + 1), slice(None)))
            _acc(db_ref, jnp.sum(du, 0, keepdims=True), first, at=(slab, slice(None), slice(None)))

    return pl.pallas_call(
        body, name="ffn_act_bwd", grid=(FFN_NT, n_b),
        in_specs=_ffn_specs(t_len) + [pl.BlockSpec((t_len, FFN_TC), lambda j, b: (b, j))],
        out_specs=[pl.BlockSpec((2, t_len, FFN_TC), lambda j, b: (0, b, j)),
                   pl.BlockSpec((2, 3, FFN_TC), lambda j, b: (0, 0, j)),
                   pl.BlockSpec((2, 1, FFN_TC), lambda j, b: (0, 0, j))],
        out_shape=[jax.ShapeDtypeStruct((2, n_b * t_len, D_FF), BF16),
                   jax.ShapeDtypeStruct((2, 3, D_FF), F32), jax.ShapeDtypeStruct((2, 1, D_FF), F32)],
        compiler_params=_cparams(("parallel", "arbitrary")),
    )(up, up, conv_w, conv_w, conv_b, conv_b, da)


NN = (((2,), (1,)), ((0,), (0,)))
NT = (((2,), (2,)), ((0,), (0,)))
TN = (((1,), (1,)), ((0,), (0,)))


def _iota3(shape, axis):
    return lax.broadcasted_iota(jnp.int32, shape, axis)


def _dg(a, b, dims):
    return lax.dot_general(a, b, dims, preferred_element_type=F32)


def _dot(a, b):
    return _dg(a, b, NN)


def _dot_nt(a, b):
    return _dg(a, b, NT)


def _dot_tn(a, b):
    return _dg(a, b, TN)


def _split(a):
    hi = a.astype(BF16)
    return hi, (a - hi.astype(F32)).astype(BF16)


def _dg3(a, b, dims):
    ah, al = _split(a)
    bh, bl = _split(b)
    return _dg(ah, bh, dims) + (_dg(ah, bl, dims) + _dg(al, bh, dims))


@jax.custom_vjp
def _dot3(a, b):
    return _dg3(a, b, NN)


def _dot3_fwd(a, b):
    return _dg3(a, b, NN), (a, b)


def _dot3_bwd(res, g):
    a, b = res
    return _dg3(g, b, NT), _dg3(a, g, TN)


_dot3.defvjp(_dot3_fwd, _dot3_bwd)


def _lower_ones(g_n, n):
    shape = (g_n, n, n)
    return jnp.where(_iota3(shape, 1) >= _iota3(shape, 2), 1.0, 0.0).astype(BF16)


@jax.custom_vjp
def _chunk_cumsum(x):
    hi, lo = _split(x)
    tri = _lower_ones(x.shape[0], x.shape[1])
    return _dg(tri, hi, NN) + _dg(tri, lo, NN)


def _chunk_cumsum_fwd(x):
    return _chunk_cumsum(x), None


def _chunk_cumsum_bwd(_, g):
    hi, lo = _split(g)
    tri = _lower_ones(g.shape[0], g.shape[1])
    return (_dg(tri, hi, TN) + _dg(tri, lo, TN),)


_chunk_cumsum.defvjp(_chunk_cumsum_fwd, _chunk_cumsum_bwd)


@jax.custom_vjp
def _unit_lower_inv(m):
    n = m.shape[1]
    p = -m
    a = jnp.where(_iota3(m.shape, 1) == _iota3(m.shape, 2), 1.0, 0.0) + p
    span = 2
    while span < n:
        p = _dg3(p, p, NN)
        a = a + _dg3(a, p, NN)
        span *= 2
    return a


def _unit_lower_inv_fwd(m):
    a = _unit_lower_inv(m)
    return a, a


def _unit_lower_inv_bwd(a, da):
    return (-_dg3(a, _dg3(da, a, NT), TN),)


_unit_lower_inv.defvjp(_unit_lower_inv_fwd, _unit_lower_inv_bwd)


@jax.custom_vjp
def _saved_lower_inv(m, a):
    return a


def _saved_lower_inv_fwd(m, a):
    return a, a


def _saved_lower_inv_bwd(a, da):
    return _unit_lower_inv_bwd(a, da)[0], jnp.zeros_like(a)


_saved_lower_inv.defvjp(_saved_lower_inv_fwd, _saved_lower_inv_bwd)


def _rms_gate(o, gn, gate):
    return o * lax.rsqrt(jnp.mean(o * o, -1, keepdims=True) + EPS) * gn * _silu(gate)


def _dn_chains(q, k, v, z, small, s_in, a_log, dt_bias, gn, a_saved=None):
    g_n, c_len = q.shape[0], q.shape[1]
    sq = (g_n, c_len, c_len)
    row, col = _iota3(sq, 1), _iota3(sq, 2)
    causal, strict, eye = row >= col, row > col, row == col
    qn = q * lax.rsqrt(jnp.sum(q * q, -1, keepdims=True) + EPS) * (HEAD_DIM ** -0.5)
    kn = k * lax.rsqrt(jnp.sum(k * k, -1, keepdims=True) + EPS)
    lane = _iota3(small.shape, 2)
    head = jnp.bitwise_and(_iota3(small.shape, 0), HEADS - 1)
    la_all = -jnp.exp(a_log) * _softplus(small + dt_bias)
    la_c = jnp.sum(jnp.where(lane == head, la_all, 0.0), 2, keepdims=True)
    beta = jnp.sum(jnp.where(lane == head + HEADS, jax.nn.sigmoid(small), 0.0), 2, keepdims=True)
    la_b = jnp.broadcast_to(la_c, sq)
    la_r = jnp.sum(jnp.where(eye, la_b, 0.0), 1, keepdims=True)
    g_c = jnp.sum(jnp.where(causal, jnp.broadcast_to(la_r, sq), 0.0), 2, keepdims=True)
    g_r = jnp.sum(jnp.where(row <= col, la_b, 0.0), 1, keepdims=True)
    g_last = jnp.sum(la_c, 1, keepdims=True)
    decay = jnp.exp(jnp.where(causal, g_c - g_r, -1e30))
    e_g = jnp.exp(g_c)
    kb = kn * beta
    m_low = jnp.where(strict, _dot_nt(kb, kn) * decay, 0.0)
    a_inv = _unit_lower_inv(m_low) if a_saved is None else _saved_lower_inv(m_low, a_saved)
    u = _dot3(a_inv, v * beta)
    w = _dot3(a_inv, kb * e_g)
    attn = _dot_nt(qn, kn) * decay
    v_new = u - _dot(w, s_in)
    o = _dot(qn * e_g, s_in) + _dot(attn, v_new)
    s_out = s_in * jnp.exp(g_last) + _dot_tn(kn * jnp.exp(g_last - g_c), v_new)
    return _rms_gate(o, gn, z), s_out, a_inv


def _gla_chains(q, k, v, gate, small, s_in, w2, b2, gn):
    g_n, c_len = q.shape[0], q.shape[1]
    sq, kk = (g_n, c_len, c_len), (g_n, GLA_KEY, GLA_KEY)
    causal = _iota3(sq, 1) >= _iota3(sq, 2)
    la = -_softplus(-(_dot(small, w2) + b2)) * (1.0 / 16.0)
    b = _chunk_cumsum(la)
    b_last = jnp.sum(jnp.where(_iota3(b.shape, 1) == c_len - 1, b, 0.0), 1, keepdims=True)
    q_dec = q * (GLA_KEY ** -0.5) * jnp.exp(b)
    attn = jnp.where(causal, _dot_nt(q_dec, k * jnp.exp(-b)), 0.0)
    o = _dot(q_dec, s_in) + _dot(attn, v)
    g_row = jnp.exp(b_last)
    g_col = jnp.sum(jnp.where(_iota3(kk, 1) == _iota3(kk, 2), jnp.broadcast_to(g_row, kk), 0.0), 2, keepdims=True)
    s_out = s_in * g_col + _dot_tn(k * jnp.exp(b_last - b), v)
    return _rms_gate(o, gn, gate), s_out


def _chunk_spec(n_b, width, col_block, n_c, reverse=False):
    if reverse:
        return pl.BlockSpec((n_b, CHUNK, width), lambda n: (0, n_c - 1 - n, col_block))
    return pl.BlockSpec((n_b, CHUNK, width), lambda n: (0, n, col_block))


def _hist_spec(n_b, d_k, n_c, reverse=False):
    if reverse:
        return pl.BlockSpec((None, n_b * HEADS, d_k, HEAD_DIM), lambda n: (n_c - 1 - n, 0, 0, 0))
    return pl.BlockSpec((None, n_b * HEADS, d_k, HEAD_DIM), lambda n: (n, 0, 0, 0))


def _ainv_spec(n_b, n_c, reverse=False):
    if reverse:
        return pl.BlockSpec((None, n_b * HEADS, CHUNK, CHUNK), lambda n: (n_c - 1 - n, 0, 0, 0))
    return pl.BlockSpec((None, n_b * HEADS, CHUNK, CHUNK), lambda n: (n, 0, 0, 0))


def _stack_chains(ref, n_b, slices):
    return jnp.stack([ref[b, :, sl] for b in range(n_b) for sl in slices], axis=0)


def _per_chain(ref, n_b):
    return jnp.stack([ref[b] for b in range(n_b) for _ in range(HEADS)], axis=0)


def _unstack_chains(ref, val, n_b, slices, offset=0):
    for b in range(n_b):
        for h, sl in enumerate(slices):
            ref[b, :, slice(offset + sl.start, offset + sl.stop)] = val[b * HEADS + h].astype(ref.dtype)


def _gate_weights(w2_ref, b2_ref, n_b):
    w2 = jnp.stack([w2_ref[:, ks] for _ in range(n_b) for ks in GLA_KSL], axis=0)
    b2 = jnp.stack([b2_ref[:, ks] for _ in range(n_b) for ks in GLA_KSL], axis=0)
    return w2, b2


def _sum_heads(val, n_b):
    return [sum(val[b * HEADS + h] for h in range(HEADS)) for b in range(n_b)]


def _const_spec(shape):
    return pl.BlockSpec(shape, lambda n: (0,) * len(shape))


DN_SL = [slice(h * HEAD_DIM, (h + 1) * HEAD_DIM) for h in range(HEADS)]
GLA_KSL = [slice(h * GLA_KEY, (h + 1) * GLA_KEY) for h in range(HEADS)]


class Rider:
    def __init__(self, inputs, out_shapes, sems, first, last):
        self.inputs, self.out_shapes, self.sems, self.first, self.last = inputs, out_shapes, sems, first, last


def _with_rider(rider, n_in, n_out, n_scratch):
    if rider is None:
        return [], [], [], [], lambda refs: (refs, None)
    r_in, r_out, r_sem = len(rider.inputs), len(rider.out_shapes), len(rider.sems)

    def split(refs):
        own_in, rest = refs[:n_in], refs[n_in:]
        rid_in, rest = rest[:r_in], rest[r_in:]
        own_out, rest = rest[:n_out], rest[n_out:]
        rid_out, rest = rest[:r_out], rest[r_out:]
        own_scr, rid_sem = rest[:n_scratch], rest[n_scratch:]
        return own_in + own_out + own_scr, (rid_in, rid_out, rid_sem)

    return list(rider.inputs), [HBM_SPEC] * r_in, [HBM_SPEC] * r_out, list(rider.sems), split


def _ride(rider, parts, grid):
    if rider is None:
        return None, None
    grid = grid if isinstance(grid, tuple) else (grid,)

    def at(step_of):
        hit = pl.program_id(0) == step_of(grid[0])
        for axis in range(1, len(grid)):
            hit = jnp.logical_and(hit, pl.program_id(axis) == step_of(grid[axis]))
        return hit

    def first():
        pl.when(at(lambda n: 0))(lambda: rider.first(*parts))

    def last():
        pl.when(at(lambda n: n - 1))(lambda: rider.last(*parts))

    return first, last


def _dn_scan_fwd(qkv, proj, a_log, dt_bias, gn, n_b, t_len, rider=None):
    n_c = t_len // CHUNK
    spec = functools.partial(_chunk_spec, n_b, n_c=n_c)
    r_inputs, r_in_specs, r_out_specs, r_sems, split = _with_rider(rider, 8, 3, 1)

    def body(*refs):
        (q_ref, k_ref, v_ref, z_ref, sm_ref, al_ref, dt_ref, gn_ref,
         o_ref, hist_ref, ainv_ref, s_ref), parts = split(refs)
        ride_first, ride_last = _ride(rider, parts, n_c)
        if rider is not None:
            ride_first()

        @pl.when(pl.program_id(0) == 0)
        def _():
            s_ref[...] = jnp.zeros_like(s_ref)

        s_in = s_ref[...]
        hist_ref[...] = s_in
        og, s_out, a_inv = _dn_chains(*(_stack_chains(r, n_b, DN_SL) for r in (q_ref, k_ref, v_ref, z_ref)),
                                      _per_chain(sm_ref, n_b), s_in, al_ref[...], dt_ref[...], gn_ref[...])
        _unstack_chains(o_ref, og, n_b, DN_SL)
        s_ref[...] = s_out
        ainv_ref[...] = a_inv
        if rider is not None:
            ride_last()

    qkv3, proj3 = qkv.reshape(n_b, t_len, -1), proj.reshape(n_b, t_len, -1)
    o, hist, ainv, *rider_outs = pl.pallas_call(
        body, name="dn_scan_fwd", grid=(n_c,),
        in_specs=[spec(512, 0), spec(512, 1), spec(512, 2), spec(512, OFF_Z // 512), spec(128, OFF_SMALL // 128),
                  _const_spec((1, 128)), _const_spec((1, 128)), _const_spec((1, 128))] + r_in_specs,
        out_specs=[spec(512, 0), _hist_spec(n_b, HEAD_DIM, n_c), _ainv_spec(n_b, n_c)] + r_out_specs,
        out_shape=[jax.ShapeDtypeStruct((n_b, t_len, 2 * 512), BF16),
                   jax.ShapeDtypeStruct((n_c, n_b * HEADS, HEAD_DIM, HEAD_DIM), F32),
                   jax.ShapeDtypeStruct((n_c, n_b * HEADS, CHUNK, CHUNK), F32)]
        + (list(rider.out_shapes) if rider else []),
        scratch_shapes=[pltpu.VMEM((n_b * HEADS, HEAD_DIM, HEAD_DIM), F32)] + r_sems,
        compiler_params=_cparams(("arbitrary",)),
    )(qkv3, qkv3, qkv3, proj3, proj3, a_log, dt_bias, gn, *r_inputs)
    return o, (hist, ainv), rider_outs


def _dn_scan_bwd(qkv, proj, a_log, dt_bias, gn, hist, d_o, n_b, t_len, rider=None):
    n_c = t_len // CHUNK
    rev = functools.partial(_chunk_spec, n_b, n_c=n_c, reverse=True)
    r_inputs, r_in_specs, r_out_specs, r_sems, split = _with_rider(rider, 11, 6, 1)
    hist, ainv = hist

    def body(*refs):
        (q_ref, k_ref, v_ref, z_ref, sm_ref, al_ref, dt_ref, gn_ref, hist_ref, ainv_ref, do_ref,
         dqkv_ref, dz_ref, dsm_ref, dal_ref, ddt_ref, dgn_ref, ds_ref), parts = split(refs)
        ride_first, ride_last = _ride(rider, parts, n_c)
        if rider is not None:
            ride_first()
        first = pl.program_id(0) == 0

        @pl.when(first)
        def _():
            ds_ref[...] = jnp.zeros_like(ds_ref)

        chains = lambda *a: _dn_chains(*a, a_saved=ainv_ref[...])[:2]
        _, pull = jax.vjp(chains, *(_stack_chains(r, n_b, DN_SL) for r in (q_ref, k_ref, v_ref, z_ref)),
                          _per_chain(sm_ref, n_b), hist_ref[...], al_ref[...], dt_ref[...], gn_ref[...])
        dq, dk, dv, dz, dsm, ds_in, dal, ddt, dgn = pull((_stack_chains(do_ref, n_b, DN_SL), ds_ref[...]))
        _unstack_chains(dqkv_ref, dq, n_b, DN_SL)
        _unstack_chains(dqkv_ref, dk, n_b, DN_SL, offset=512)
        _unstack_chains(dqkv_ref, dv, n_b, DN_SL, offset=1024)
        _unstack_chains(dz_ref, dz, n_b, DN_SL)
        ds_ref[...] = ds_in
        for b, dsm_b in enumerate(_sum_heads(dsm, n_b)):
            dsm_ref[b] = dsm_b
        _acc(dal_ref, dal, first)
        _acc(ddt_ref, ddt, first)
        _acc(dgn_ref, dgn, first)
        if rider is not None:
            ride_last()

    qkv3, proj3, do3 = (a.reshape(n_b, t_len, -1) for a in (qkv, proj, d_o))
    vec = jax.ShapeDtypeStruct((1, 128), F32)
    dqkv, d_proj, dsm, dal, ddt, dgn, *rider_outs = pl.pallas_call(
        body, name="dn_scan_bwd", grid=(n_c,),
        in_specs=[rev(512, 0), rev(512, 1), rev(512, 2), rev(512, OFF_Z // 512), rev(128, OFF_SMALL // 128),
                  _const_spec((1, 128)), _const_spec((1, 128)), _const_spec((1, 128)),
                  _hist_spec(n_b, HEAD_DIM, n_c, reverse=True), _ainv_spec(n_b, n_c, reverse=True),
                  rev(512, 0)] + r_in_specs,
        out_specs=[rev(1536, 0), rev(512, OFF_Z // 512), rev(128, 0),
                   _const_spec((1, 128)), _const_spec((1, 128)), _const_spec((1, 128))] + r_out_specs,
        out_shape=[jax.ShapeDtypeStruct((n_b, t_len, 1536), F32), jax.ShapeDtypeStruct((n_b, t_len, PROJ_W), BF16),
                   jax.ShapeDtypeStruct((n_b, t_len, 128), F32), vec, vec, vec]
        + (list(rider.out_shapes) if rider else []),
        scratch_shapes=[pltpu.VMEM((n_b * HEADS, HEAD_DIM, HEAD_DIM), F32)] + r_sems,
        compiler_params=_cparams(("arbitrary",)),
    )(qkv3, qkv3, qkv3, proj3, proj3, a_log, dt_bias, gn, hist, ainv, do3, *r_inputs)
    return dqkv.reshape(n_b * t_len, 1536), d_proj, dsm, dal, ddt, dgn, rider_outs


def _gla_scan_fwd(proj, w2, b2, gn, o_mix, n_b, t_len):
    n_c = t_len // CHUNK
    spec = functools.partial(_chunk_spec, n_b, n_c=n_c)

    def body(q_ref, k_ref, v_ref, g_ref, sm_ref, w2_ref, b2_ref, gn_ref, _, o_ref, hist_ref, s_ref):
        @pl.when(pl.program_id(0) == 0)
        def _():
            s_ref[...] = jnp.zeros_like(s_ref)

        s_in = s_ref[...]
        hist_ref[...] = s_in
        og, s_out = _gla_chains(_stack_chains(q_ref, n_b, GLA_KSL), _stack_chains(k_ref, n_b, GLA_KSL),
                                _stack_chains(v_ref, n_b, DN_SL), _stack_chains(g_ref, n_b, DN_SL),
                                _per_chain(sm_ref, n_b), s_in, *_gate_weights(w2_ref, b2_ref, n_b), gn_ref[...])
        _unstack_chains(o_ref, og, n_b, DN_SL)
        s_ref[...] = s_out

    proj3 = proj.reshape(n_b, t_len, -1)
    o, hist = pl.pallas_call(
        body, name="gla_scan_fwd", grid=(n_c,),
        in_specs=[spec(256, OFF_GQ // 256), spec(256, OFF_GK // 256), spec(512, OFF_GV // 512),
                  spec(512, OFF_GG // 512), spec(128, OFF_SMALL // 128),
                  _const_spec((128, 256)), _const_spec((1, 256)), _const_spec((1, 128)),
                  pl.BlockSpec(memory_space=pl.ANY)],
        out_specs=[spec(512, 1), _hist_spec(n_b, GLA_KEY, n_c)],
        out_shape=[jax.ShapeDtypeStruct(o_mix.shape, BF16),
                   jax.ShapeDtypeStruct((n_c, n_b * HEADS, GLA_KEY, HEAD_DIM), F32)],
        input_output_aliases={8: 0},
        scratch_shapes=[pltpu.VMEM((n_b * HEADS, GLA_KEY, HEAD_DIM), F32)],
        compiler_params=_cparams(("arbitrary",)),
    )(proj3, proj3, proj3, proj3, proj3, w2, b2, gn, o_mix)
    return o.reshape(n_b * t_len, 2 * 512), hist


def _gla_scan_bwd(proj, w2, b2, gn, hist, d_o, dsm_dn, d_proj, n_b, t_len):
    n_c = t_len // CHUNK
    rev = functools.partial(_chunk_spec, n_b, n_c=n_c, reverse=True)

    def body(q_ref, k_ref, v_ref, g_ref, sm_ref, w2_ref, b2_ref, gn_ref, hist_ref, do_ref, dsm_dn_ref, _,
             dp_ref, dw2_ref, db2_ref, dgn_ref, ds_ref):
        first = pl.program_id(0) == 0

        @pl.when(first)
        def _():
            ds_ref[...] = jnp.zeros_like(ds_ref)

        _, pull = jax.vjp(_gla_chains, _stack_chains(q_ref, n_b, GLA_KSL), _stack_chains(k_ref, n_b, GLA_KSL),
                          _stack_chains(v_ref, n_b, DN_SL), _stack_chains(g_ref, n_b, DN_SL),
                          _per_chain(sm_ref, n_b), hist_ref[...], *_gate_weights(w2_ref, b2_ref, n_b), gn_ref[...])
        dq, dk, dv, dg, dsm, ds_in, dw2, db2, dgn = pull((_stack_chains(do_ref, n_b, DN_SL), ds_ref[...]))
        _unstack_chains(dp_ref, dq, n_b, GLA_KSL, offset=OFF_GQ)
        _unstack_chains(dp_ref, dk, n_b, GLA_KSL, offset=OFF_GK)
        _unstack_chains(dp_ref, dv, n_b, DN_SL, offset=OFF_GV)
        _unstack_chains(dp_ref, dg, n_b, DN_SL, offset=OFF_GG)
        ds_ref[...] = ds_in
        for b, dsm_b in enumerate(_sum_heads(dsm, n_b)):
            dp_ref[b, :, OFF_SMALL:OFF_SMALL + 128] = (dsm_b + dsm_dn_ref[b]).astype(BF16)
            dp_ref[b, :, OFF_SMALL + 128:GLA_W] = jnp.zeros((CHUNK, GLA_W - OFF_SMALL - 128), BF16)
        for h, ks in enumerate(GLA_KSL):
            _acc(dw2_ref, sum(dw2[b * HEADS + h] for b in range(n_b)), first, at=(slice(None), ks))
            _acc(db2_ref, sum(db2[b * HEADS + h] for b in range(n_b)), first, at=(slice(None), ks))
        _acc(dgn_ref, dgn, first)

    proj3, do3 = proj.reshape(n_b, t_len, -1), d_o.reshape(n_b, t_len, -1)
    return pl.pallas_call(
        body, name="gla_scan_bwd", grid=(n_c,),
        in_specs=[rev(256, OFF_GQ // 256), rev(256, OFF_GK // 256), rev(512, OFF_GV // 512), rev(512, OFF_GG // 512),
                  rev(128, OFF_SMALL // 128),
                  _const_spec((128, 256)), _const_spec((1, 256)), _const_spec((1, 128)),
                  _hist_spec(n_b, GLA_KEY, n_c, reverse=True), rev(512, 1), rev(128, 0),
                  pl.BlockSpec(memory_space=pl.ANY)],
        out_specs=[rev(GLA_W, 0), _const_spec((128, 256)), _const_spec((1, 256)), _const_spec((1, 128))],
        out_shape=[jax.ShapeDtypeStruct(d_proj.shape, BF16), jax.ShapeDtypeStruct((128, 256), F32),
                   jax.ShapeDtypeStruct((1, 256), F32), jax.ShapeDtypeStruct((1, 128), F32)],
        input_output_aliases={11: 0},
        scratch_shapes=[pltpu.VMEM((n_b * HEADS, GLA_KEY, HEAD_DIM), F32)],
        compiler_params=_cparams(("arbitrary",)),
    )(proj3, proj3, proj3, proj3, proj3, w2, b2, gn, hist, do3, dsm_dn, d_proj)


W_IN_RUNS = ((0, 256, GLA_W), (256, 1536, OFF_Z + 512), (1536, 2048, OFF_Z), (2048, 2056, OFF_SMALL),
             (2056, 3592, 0), (3592, 3608, OFF_SMALL + 8))
W_IN_ROWS = 256


def _w_in_pieces(cols_per_chip):
    out = []
    for first, last, start in W_IN_RUNS:
        for j in range(N_CHIPS):
            a, b = max(first, cols_per_chip * j), min(last, cols_per_chip * (j + 1))
            if a < b:
                out.append((j, a - cols_per_chip * j, b - cols_per_chip * j, start + a - first))
    return out


def _w_in_to_padded(w4):
    _, n_r, n_c = w4.shape

    def body(i_ref, o_ref):
        o_ref[...] = jnp.zeros_like(o_ref)
        for j, a, b, p in _w_in_pieces(n_c):
            o_ref[:, p:p + b - a] = i_ref[j, :, a:b]

    return pl.pallas_call(
        body, name="w_in_to_padded", grid=(n_r // W_IN_ROWS,),
        in_specs=[pl.BlockSpec((N_CHIPS, W_IN_ROWS, n_c), lambda i: (0, i, 0))],
        out_specs=pl.BlockSpec((W_IN_ROWS, PROJ_W), lambda i: (i, 0)),
        out_shape=jax.ShapeDtypeStruct((n_r, PROJ_W), w4.dtype), compiler_params=_cparams(("parallel",)),
    )(w4)


def _w_in_to_chips(g, n_c):
    n_r = g.shape[0]

    def body(i_ref, o_ref):
        for j, a, b, p in _w_in_pieces(n_c):
            o_ref[j, :, a:b] = i_ref[:, p:p + b - a]

    return pl.pallas_call(
        body, name="w_in_to_chips", grid=(n_r // W_IN_ROWS,),
        in_specs=[pl.BlockSpec((W_IN_ROWS, PROJ_W), lambda i: (i, 0))],
        out_specs=pl.BlockSpec((N_CHIPS, W_IN_ROWS, n_c), lambda i: (0, i, 0)),
        out_shape=jax.ShapeDtypeStruct((N_CHIPS, n_r, n_c), g.dtype), compiler_params=_cparams(("parallel",)),
    )(g)


def _lane_vec(v, offset=0):
    return jnp.zeros((1, 128), F32).at[0, offset:offset + v.shape[0]].set(v)


def _local_step(x, tgt, mod, p, n_b, t_len, comm=None):
    row1 = lambda v: v.reshape(1, -1)
    a_log, dt_bias = _lane_vec(p["dn_a_log"]), _lane_vec(p["dn_dt_bias"])
    dn_gn, gla_gn = row1(p["dn_norm_g"]), row1(p["gla_norm_g"])
    w2 = jnp.zeros((128, 256), F32).at[8:8 + GATE_RANK].set(p["gla_w_gate2"])
    b2 = row1(p["gla_b_gate"])
    ln0_g, ln0_b, ln1_g, ln1_b, ln2_g, ln2_b = (row1(p[k]) for k in ("ln0_g", "ln0_b", "ln1_g", "ln1_b", "ln2_g", "ln2_b"))
    conv_b = row1(p["ffn_conv_b"])

    x0, h1 = _ln0_fwd(x, ln0_g, ln0_b, mod, n_b, t_len)
    proj = _mm(h1, p["w_in_p"], name="mm_proj")
    qkv = _dn_pre_fwd(proj, p["dn_conv"], n_b, t_len)
    o_half, hist_dn, landed = _dn_scan_fwd(qkv, proj, a_log, dt_bias, dn_gn, n_b, t_len,
                                           rider=comm.fwd_rider() if comm else None)
    if comm:
        p = {**p, **comm.weights_from(landed)}
    o_mix, hist_gla = _gla_scan_fwd(proj, w2, b2, gla_gn, o_half, n_b, t_len)
    y = _mm(o_mix, p["w_o"], name="mm_wo")
    x1, h2 = _ln1_fwd(x0, y, ln1_g, ln1_b, mod, n_b, t_len)
    up = _mm(h2, p["w_up"], name="mm_up")
    act = _ffn_act_fwd(up, p["ffn_conv"], conv_b, n_b, t_len)
    y2 = _mm(act, p["w_down"], name="mm_down")

    loss, dx1, dy2, g_ln2_g, g_ln2_b, dgt_f = _ln2_loss_bwd(x1, y2, ln2_g, ln2_b, mod, tgt, n_b, t_len)
    g_w_down = _mm(act, dy2, ta=True, name="mm_g_down")
    d_act = _mm(dy2, p["w_down"], tb=True, name="mm_d_act")
    d_up, g_ffn_conv, g_conv_b = _ffn_act_bwd(up, p["ffn_conv"], conv_b, d_act, n_b, t_len)
    g_w_up = _mm(h2, d_up, ta=True, out_slabs=N_CHIPS, name="mm_g_up")
    if comm:
        dh2, from_sibling = _mm(d_up, p["w_up"], tb=True, name="mm_d_h2", rider=comm.ffn_pair_rider(g_w_up, g_w_down))
    else:
        dh2 = _mm(d_up, p["w_up"], tb=True, name="mm_d_h2")
    dx0, dy, g_ln1_g, g_ln1_b, dmod_1 = _ln1_bwd(x0, y, ln1_g, ln1_b, mod, dx1, dh2, n_b, t_len)
    g_w_o = _mm(o_mix, dy, ta=True, name="mm_g_wo")
    d_o = _mm(dy, p["w_o"], tb=True, name="mm_d_o")
    dqkv, d_proj, dsm_dn, g_a_log, g_dt_bias, g_dn_gn, ffn_from_chips = _dn_scan_bwd(
        qkv, proj, a_log, dt_bias, dn_gn, hist_dn, d_o, n_b, t_len,
        rider=comm.ffn_chips_rider(from_sibling) if comm else None)
    d_proj, g_w2, g_b2, g_gla_gn = _gla_scan_bwd(proj, w2, b2, gla_gn, hist_gla, d_o, dsm_dn, d_proj, n_b, t_len)
    d_proj, g_dn_conv = _dn_pre_bwd(proj, p["dn_conv"], dqkv, d_proj.reshape(n_b * t_len, PROJ_W), n_b, t_len)
    g_w_in_p = _mm(h1, d_proj, ta=True, name="mm_g_win")
    if comm:
        dh1, tail_from_chips = _mm(d_proj, p["w_in_p"], tb=True, name="mm_d_h1",
                                   rider=comm.tail_chips_rider(g_w_in_p, g_w_o))
        from_chips = (ffn_from_chips, tail_from_chips)
    else:
        dh1, from_chips = _mm(d_proj, p["w_in_p"], tb=True, name="mm_d_h1"), None
    grad_x, g_ln0_g, g_ln0_b, dmod_0 = _ln0_bwd(x, ln0_g, ln0_b, mod, dx0, dh1, n_b, t_len)

    dmod = jnp.concatenate([dmod_0, dmod_1[:, 0:1], dmod_1[:, 1:3], dgt_f], axis=1)
    grads = {
        "ln0_g": g_ln0_g[0], "ln0_b": g_ln0_b[0], "w_in_p": g_w_in_p, "dn_conv": g_dn_conv,
        "dn_a_log": g_a_log[0, 0:HEADS], "dn_dt_bias": g_dt_bias[0, 0:HEADS], "dn_norm_g": g_dn_gn[0],
        "gla_w_gate2": g_w2[8:8 + GATE_RANK], "gla_b_gate": g_b2[0], "gla_norm_g": g_gla_gn[0],
        "w_o": g_w_o, "ln1_g": g_ln1_g[0], "ln1_b": g_ln1_b[0], "w_up": g_w_up,
        "ffn_conv": jnp.concatenate([g_ffn_conv[0], g_ffn_conv[1]], axis=1),
        "ffn_conv_b": jnp.concatenate([g_conv_b[0, 0], g_conv_b[1, 0]]), "w_down": g_w_down,
        "ln2_g": g_ln2_g[0], "ln2_b": g_ln2_b[0],
    }
    return loss, grad_x, grads, dmod, from_chips


def _ada_fwd(c_all, w_shard, b_shard):
    n_all, n_col = c_all.shape[0], w_shard.shape[1]
    tn = 512

    def body(c_ref, w_ref, b_ref, cond_ref, mod_ref):
        cond = _silu(c_ref[...])
        cond_ref[...] = cond
        mod_ref[...] = jnp.dot(cond.astype(BF16), w_ref[...].astype(BF16), preferred_element_type=F32) + b_ref[...]

    return pl.pallas_call(
        body, name="ada_fwd", grid=(n_col // tn,),
        in_specs=[pl.BlockSpec((n_all, D_MODEL), lambda j: (0, 0)), pl.BlockSpec((D_MODEL, tn), lambda j: (0, j)),
                  pl.BlockSpec((1, tn), lambda j: (0, j))],
        out_specs=[pl.BlockSpec((n_all, D_MODEL), lambda j: (0, 0)), pl.BlockSpec((n_all, tn), lambda j: (0, j))],
        out_shape=[jax.ShapeDtypeStruct((n_all, D_MODEL), F32), jax.ShapeDtypeStruct((n_all, n_col), F32)],
        compiler_params=_cparams(("arbitrary",)),
    )(c_all, w_shard, b_shard)


def _col_sum(a):
    def body(a_ref, o_ref):
        o_ref[...] = jnp.sum(a_ref[...], 0, keepdims=True)

    return pl.pallas_call(body, name="col_sum", out_shape=jax.ShapeDtypeStruct((1, a.shape[1]), F32))(a)


def _adamw_math(w, grad, m, v):
    new_m = ADAM_B1 * m + (1.0 - ADAM_B1) * grad
    new_v = ADAM_B2 * v + (1.0 - ADAM_B2) * (grad * grad)
    m_hat = new_m / (1.0 - ADAM_B1 ** ADAM_STEP)
    v_hat = new_v / (1.0 - ADAM_B2 ** ADAM_STEP)
    return -ADAM_LR * (m_hat / (jnp.sqrt(v_hat) + ADAM_EPS) + ADAM_WD * w), new_m, new_v


def _adamw_many(ws, gs, ms, vs):
    n = len(ws)

    def body(*refs):
        for i in range(n):
            w_ref, g_ref, m_ref, v_ref = (refs[k * n + i] for k in range(4))
            d_ref, nm_ref, nv_ref = (refs[(4 + k) * n + i] for k in range(3))
            d_ref[...], nm_ref[...], nv_ref[...] = _adamw_math(w_ref[...], g_ref[...], m_ref[...], v_ref[...])

    outs = pl.pallas_call(
        body, name="adamw_small", out_shape=[jax.ShapeDtypeStruct(w.shape, F32) for w in ws] * 3,
    )(*ws, *gs, *ms, *vs)
    return outs[:n], outs[n:2 * n], outs[2 * n:]


def _adamw(w, g, m, v, name):
    n_r, n_c = w.shape
    if n_r % 8 == 0:
        tr = _pick(n_r, (256, 64, 32, 16, 8))
        grid, blk = (n_r // tr,), pl.BlockSpec((tr, n_c), lambda i: (i, 0))
    else:
        tc = _pick(n_c, (256, 128))
        grid, blk = (n_c // tc,), pl.BlockSpec((n_r, tc), lambda i: (0, i))

    def body(w_ref, g_ref, m_ref, v_ref, go_ref, d_ref, nm_ref, nv_ref):
        grad = g_ref[...]
        go_ref[...] = grad
        d_ref[...], nm_ref[...], nv_ref[...] = _adamw_math(w_ref[...], grad, m_ref[...], v_ref[...])

    out = jax.ShapeDtypeStruct(w.shape, F32)
    return pl.pallas_call(
        body, name=name, grid=grid, in_specs=[blk] * 4, out_specs=[blk] * 4, out_shape=[out] * 4,
        compiler_params=_cparams(("parallel",)),
    )(w, g, m, v)


HBM_SPEC = pl.BlockSpec(memory_space=pltpu.HBM)
VMEM_SPEC = pl.BlockSpec(memory_space=pltpu.VMEM)
CHIP_FLIPS = ((1, 0), (0, 1), (1, 1))


def _place():
    return lax.axis_index("x"), lax.axis_index("y"), lax.axis_index("c")


def _flip(v, f):
    return 1 - v if f else v


def _all_gather8(slab, name, rider=None):
    n_r, n_w = slab.shape
    r_inputs, r_in_specs, r_out_specs, r_sems, split = _with_rider(rider, 1, 2, 3)

    def body(*refs):
        (x_ref, o_ref, s_ref, send_sems, recv_sems, local_sem), parts = split(refs)
        if rider is not None:
            rider.first(*parts)
        x, y, c = _place()
        me = 4 * x + 2 * y + c
        mine = pltpu.make_async_copy(x_ref, o_ref.at[me], local_sem)
        mine.start()
        peers = [(_flip(x, k & 4), _flip(y, k & 2), _flip(c, k & 1)) for k in range(1, N_DEV)]
        sends = []
        for k, peer in enumerate(peers):
            cp = pltpu.make_async_remote_copy(src_ref=x_ref, dst_ref=o_ref.at[me], send_sem=send_sems.at[k],
                                              recv_sem=recv_sems.at[k], device_id=peer, device_id_type=MESH)
            cp.start()
            sends.append(cp)
        for k, (px, py, pc) in enumerate(peers):
            pltpu.make_async_remote_copy(src_ref=x_ref, dst_ref=o_ref.at[4 * px + 2 * py + pc],
                                         send_sem=send_sems.at[k], recv_sem=recv_sems.at[k],
                                         device_id=(px, py, pc), device_id_type=MESH).wait_recv()
        for cp in sends:
            cp.wait_send()
        mine.wait()
        total = o_ref[0]
        for d in range(1, N_DEV):
            total = total + o_ref[d]
        s_ref[...] = total
        if rider is not None:
            rider.last(*parts)

    gathered, total, *rider_outs = pl.pallas_call(
        body, name=name, in_specs=[VMEM_SPEC] + r_in_specs, out_specs=[VMEM_SPEC, VMEM_SPEC] + r_out_specs,
        out_shape=[jax.ShapeDtypeStruct((N_DEV, n_r, n_w), F32), jax.ShapeDtypeStruct((n_r, n_w), F32)]
        + (list(rider.out_shapes) if rider else []),
        scratch_shapes=[pltpu.SemaphoreType.DMA((N_DEV - 1,)), pltpu.SemaphoreType.DMA((N_DEV - 1,)),
                        pltpu.SemaphoreType.DMA] + r_sems,
    )(slab, *r_inputs)
    return (gathered, total, rider_outs) if rider else (gathered, total)


def _gather_rider(shards):
    n_a = len(shards)

    def plan(ins, outs, sems):
        send_sems, recv_sems = sems
        x, y, c = _place()
        chips = [(_flip(x, fx), _flip(y, fy)) for fx, fy in CHIP_FLIPS]

        def copy(k, slot, chip_of_block, half, to, src=None):
            dst = outs[k].at[chip_of_block, half]
            return pltpu.make_async_remote_copy(src_ref=dst if src is None else src, dst_ref=dst,
                                                send_sem=send_sems.at[k * 6 + slot], recv_sem=recv_sems.at[k * 6 + slot],
                                                device_id=to, device_id_type=MESH)

        first = [copy(k, r, 2 * x + y, c, (*chips[r], c), src=ins[k].at[c]) for k in range(n_a) for r in range(3)]
        return copy, chips, first, (x, y, c)

    def first_step(ins, outs, sems):
        for cp in plan(ins, outs, sems)[2]:
            cp.start()

    def last_step(ins, outs, sems):
        copy, chips, first, (x, y, c) = plan(ins, outs, sems)
        passed = []
        for k in range(n_a):
            for r, (px, py) in enumerate(chips):
                copy(k, r, 2 * px + py, c, (x, y, c)).wait_recv()
                fwd = copy(k, 3 + r, 2 * px + py, c, (x, y, 1 - c))
                fwd.start()
                passed.append(fwd)
        for k in range(n_a):
            for r, (px, py) in enumerate(chips):
                copy(k, 3 + r, 2 * px + py, 1 - c, (x, y, c)).wait_recv()
        for cp in first + passed:
            cp.wait_send()

    return Rider(shards, [jax.ShapeDtypeStruct((N_CHIPS,) + s.shape, s.dtype) for s in shards],
                 [pltpu.SemaphoreType.DMA((6 * n_a,)), pltpu.SemaphoreType.DMA((6 * n_a,))], first_step, last_step)


def _place_own(gathered, shard, chip, name):
    _, _, n_h, n_c = gathered.shape
    th = _pick(n_h, (256, 176, 128))

    def body(sel_ref, s_ref, _, o_ref):
        o_ref[...] = s_ref[...]

    grid_spec = pltpu.PrefetchScalarGridSpec(
        num_scalar_prefetch=1, grid=(2, n_h // th),
        in_specs=[pl.BlockSpec((None, th, n_c), lambda hf, i, sel: (hf, i, 0)), pl.BlockSpec(memory_space=pl.ANY)],
        out_specs=pl.BlockSpec((None, None, th, n_c), lambda hf, i, sel: (sel[0], hf, i, 0)))
    return pl.pallas_call(
        body, name=name, grid_spec=grid_spec, out_shape=jax.ShapeDtypeStruct(gathered.shape, gathered.dtype),
        input_output_aliases={2: 0}, compiler_params=_cparams(("parallel", "parallel")),
    )(chip.reshape(1), shard, gathered)


def _pair_rider(parts):
    n_a = len(parts)

    def plan(ins, outs, sems):
        send_sems, recv_sems = sems
        x, y, c = _place()
        return [pltpu.make_async_remote_copy(src_ref=ins[k].at[:, 1 - c], dst_ref=outs[k], send_sem=send_sems.at[k],
                                             recv_sem=recv_sems.at[k], device_id=(x, y, 1 - c), device_id_type=MESH)
                for k in range(n_a)]

    def first_step(ins, outs, sems):
        for cp in plan(ins, outs, sems):
            cp.start()

    def last_step(ins, outs, sems):
        for cp in plan(ins, outs, sems):
            cp.wait()

    return Rider(parts, [jax.ShapeDtypeStruct((N_CHIPS,) + p.shape[2:], F32) for p in parts],
                 [pltpu.SemaphoreType.DMA((n_a,)), pltpu.SemaphoreType.DMA((n_a,))], first_step, last_step)


def _alone(rider, name):
    n_a = len(rider.inputs)

    def body(*refs):
        parts = (refs[:n_a], refs[n_a:2 * n_a], refs[2 * n_a:])
        rider.first(*parts)
        rider.last(*parts)

    return pl.pallas_call(
        body, name=name, in_specs=[HBM_SPEC] * n_a, out_specs=[HBM_SPEC] * n_a,
        out_shape=rider.out_shapes, scratch_shapes=rider.sems,
    )(*rider.inputs)


def _chips_rider(sums):
    n_a = len(sums)

    def plan(ins, outs, sems):
        send_sems, recv_sems = sems
        x, y, c = _place()
        cps = []
        for k in range(n_a):
            for r, (fx, fy) in enumerate(CHIP_FLIPS):
                px, py = _flip(x, fx), _flip(y, fy)
                cps.append(pltpu.make_async_remote_copy(
                    src_ref=ins[k].at[2 * px + py], dst_ref=outs[k].at[r], send_sem=send_sems.at[3 * k + r],
                    recv_sem=recv_sems.at[3 * k + r], device_id=(px, py, c), device_id_type=MESH))
        return cps

    def first_step(ins, outs, sems):
        for cp in plan(ins, outs, sems):
            cp.start()

    def last_step(ins, outs, sems):
        for cp in plan(ins, outs, sems):
            cp.wait()

    return Rider(sums, [jax.ShapeDtypeStruct((3,) + s.shape[1:], s.dtype) for s in sums],
                 [pltpu.SemaphoreType.DMA((3 * n_a,)), pltpu.SemaphoreType.DMA((3 * n_a,))], first_step, last_step)


def _rs_share(bufs):
    n_a = len(bufs)

    def body(*refs):
        ins, outs = refs[:n_a], refs[n_a:2 * n_a]
        send_sems, recv_sems = refs[2 * n_a:]
        x, y, c = _place()
        sends = [pltpu.make_async_remote_copy(src_ref=ins[k].at[c], dst_ref=outs[k].at[c], send_sem=send_sems.at[k],
                                              recv_sem=recv_sems.at[k], device_id=(x, y, 1 - c), device_id_type=MESH)
                 for k in range(n_a)]
        for cp in sends:
            cp.start()
        for k in range(n_a):
            pltpu.make_async_remote_copy(src_ref=ins[k].at[c], dst_ref=outs[k].at[1 - c], send_sem=send_sems.at[k],
                                         recv_sem=recv_sems.at[k], device_id=(x, y, 1 - c),
                                         device_id_type=MESH).wait_recv()
        for cp in sends:
            cp.wait_send()

    return pl.pallas_call(
        body, name="rs_share", in_specs=[HBM_SPEC] * n_a, out_specs=[HBM_SPEC] * n_a,
        out_shape=[jax.ShapeDtypeStruct(s.shape, F32) for s in bufs],
        input_output_aliases={k: k for k in range(n_a)},
        scratch_shapes=[pltpu.SemaphoreType.DMA((n_a,)), pltpu.SemaphoreType.DMA((n_a,))],
    )(*bufs)


def _pair_add(part, recv, core, name):
    _, _, n_h, n_c = part.shape
    th = _pick(n_h, (256, 176, 128))

    def body(sel_ref, p_ref, r_ref, o_ref):
        o_ref[...] = (p_ref[...] + r_ref[...]).astype(BF16)

    grid_spec = pltpu.PrefetchScalarGridSpec(
        num_scalar_prefetch=1, grid=(N_CHIPS, n_h // th),
        in_specs=[pl.BlockSpec((None, None, th, n_c), lambda j, i, sel: (j, sel[0], i, 0)),
                  pl.BlockSpec((None, th, n_c), lambda j, i, sel: (j, i, 0))],
        out_specs=pl.BlockSpec((None, th, n_c), lambda j, i, sel: (j, i, 0)))
    return pl.pallas_call(
        body, name=name, grid_spec=grid_spec, out_shape=jax.ShapeDtypeStruct(recv.shape, BF16),
        compiler_params=_cparams(("parallel", "parallel")),
    )(core.reshape(1), part, recv)


def _chip_add(sums, recv, chip, core, name):
    _, n_h, n_c = sums.shape
    th = _pick(n_h, (256, 176, 128))

    def body(sel_ref, s_ref, r_ref, o_ref):
        total = s_ref[...].astype(F32)
        for r in range(3):
            total = total + r_ref[r].astype(F32)
        o_ref[...] = total

    grid_spec = pltpu.PrefetchScalarGridSpec(
        num_scalar_prefetch=1, grid=(n_h // th,),
        in_specs=[pl.BlockSpec((None, th, n_c), lambda i, sel: (sel[0], i, 0)),
                  pl.BlockSpec((3, th, n_c), lambda i, sel: (0, i, 0))],
        out_specs=pl.BlockSpec((None, th, n_c), lambda i, sel: (sel[1], i, 0)))
    return pl.pallas_call(
        body, name=name, grid_spec=grid_spec, out_shape=jax.ShapeDtypeStruct((2, n_h, n_c), F32),
        compiler_params=_cparams(("parallel",)),
    )(jnp.stack([chip, core]), sums, recv)


def _row_halves(a):
    return a.reshape(N_CHIPS, 2, -1, a.shape[-1])


class StepComm:
    REST = ("w_o", "w_up", "w_down")

    def __init__(self, core, chip, rest_shards, in_cols):
        self.core, self.chip, self.shards, self.in_cols = core, chip, rest_shards, in_cols

    def fwd_rider(self):
        return _gather_rider(self.shards)

    def weights_from(self, landed):
        g_o, g_up, g_down = (_place_own(g, s, self.chip, "place_own_" + n)
                             for g, s, n in zip(landed, self.shards, self.REST))
        return {"w_o": g_o.reshape(-1, D_MODEL), "w_up": g_up.reshape(N_CHIPS, -1, g_up.shape[-1]),
                "w_down": g_down.reshape(-1, D_MODEL)}

    def _add_pairs(self, parts, from_sibling, names):
        return [_pair_add(p, r, self.core, "pair_add_" + n) for p, r, n in zip(parts, from_sibling, names)]

    def ffn_pair_rider(self, g_w_up, g_w_down):
        self.ffn_parts = [_row_halves(g_w_up), _row_halves(g_w_down)]
        return _pair_rider(self.ffn_parts)

    def ffn_chips_rider(self, from_sibling):
        self.ffn_sums = self._add_pairs(self.ffn_parts, from_sibling, ("w_up", "w_down"))
        return _chips_rider(self.ffn_sums)

    def tail_chips_rider(self, g_w_in_p, g_w_o):
        parts = [_row_halves(_w_in_to_chips(g_w_in_p, self.in_cols)), _row_halves(g_w_o)]
        self.tail_sums = self._add_pairs(parts, _alone(_pair_rider(parts), "rs_pair_tail"), ("w_in", "w_o"))
        return _chips_rider(self.tail_sums)

    def finish(self, ffn_from_chips, tail_from_chips):
        halves = [_chip_add(s, r, self.chip, self.core, "chip_add_" + n)
                  for s, r, n in zip(self.tail_sums + self.ffn_sums, list(tail_from_chips) + list(ffn_from_chips),
                                     ("w_in", "w_o", "w_up", "w_down"))]
        return [f.reshape(-1, f.shape[-1]) for f in _rs_share(halves)]


SLAB_W = 1024


def _pack(arrays, rows):
    flat = jnp.concatenate([a.reshape(-1).astype(F32) for a in arrays])
    return jnp.pad(flat, (0, rows * SLAB_W - flat.shape[0])).reshape(rows, SLAB_W)


def _unpack(flat, shapes):
    out, off = [], 0
    for s in shapes:
        n = 1
        for d in s:
            n *= d
        out.append(flat[off:off + n].reshape(s))
        off += n
    return out


def _rows_for(arrays_or_shapes):
    n = 0
    for a in arrays_or_shapes:
        s = a if isinstance(a, tuple) else a.shape
        k = 1
        for d in s:
            k *= d
        n += k
    return -(-n // (8 * SLAB_W)) * 8


def kernel(x, c, ln0_g, ln0_b, w_ada, b_ada, w_in, dn_conv, dn_a_log, dn_dt_bias, dn_norm_g, gla_w_gate2, gla_b_gate, gla_norm_g, w_o, ln1_g, ln1_b, ffn_w_up, ffn_conv, ffn_conv_b, ffn_w_down, ln2_g, ln2_b, loss_target, m_ln0_g, m_ln0_b, m_w_ada, m_b_ada, m_w_in, m_dn_conv, m_dn_a_log, m_dn_dt_bias, m_dn_norm_g, m_gla_w_gate2, m_gla_b_gate, m_gla_norm_g, m_w_o, m_ln1_g, m_ln1_b, m_ffn_w_up, m_ffn_conv, m_ffn_conv_b, m_ffn_w_down, m_ln2_g, m_ln2_b, v_ln0_g, v_ln0_b, v_w_ada, v_b_ada, v_w_in, v_dn_conv, v_dn_a_log, v_dn_dt_bias, v_dn_norm_g, v_gla_w_gate2, v_gla_b_gate, v_gla_norm_g, v_w_o, v_ln1_g, v_ln1_b, v_ffn_w_up, v_ffn_conv, v_ffn_conv_b, v_ffn_w_down, v_ln2_g, v_ln2_b):
    n_b, t_len, _ = x.shape
    xi, yi, ci = _place()
    chip = (2 * xi + yi).astype(jnp.int32)
    core = ci.astype(jnp.int32)
    me = 2 * chip + core
    n_all = N_DEV * n_b
    ada_cols = w_ada.shape[2]

    halves = lambda a: a.astype(BF16).reshape(2, a.shape[0] // 2, a.shape[1])
    w_in_halves = halves(w_in[0])
    sharded_small = [dn_conv[0], gla_w_gate2[0], ffn_conv[0]]
    slab = _pack([c] + sharded_small, _rows_for([c] + sharded_small))
    gathered, _, (g_in,) = _all_gather8(slab, "gather_small", rider=_gather_rider([w_in_halves]))
    g_in = _place_own(g_in, w_in_halves, chip, "place_own_w_in")
    flat = gathered.reshape(N_DEV, -1)
    c_all = flat[:, :c.size].reshape(n_all, D_MODEL)
    by_chip = flat[0::2]
    full, off = [], c.size
    for a in sharded_small:
        blocks = by_chip[:, off:off + a.size].reshape(N_CHIPS, *a.shape)
        full.append(blocks.transpose(1, 0, 2).reshape(a.shape[0], N_CHIPS * a.shape[1]))
        off += a.size
    dn_conv_f, gate2_f, ffn_conv_f = full

    b_ada_shard = lax.dynamic_slice(b_ada, (0, chip * ada_cols), (1, ada_cols))
    cond_all, mod_cols = _ada_fwd(c_all, w_ada[0], b_ada_shard)
    mod_g, _ = _all_gather8(mod_cols, "gather_mod")
    mod_full = jnp.concatenate([mod_g[2 * j] for j in range(N_CHIPS)], axis=1)
    mod = lax.dynamic_slice(mod_full, (me * n_b, 0), (n_b, 6 * D_MODEL)).reshape(n_b, 6, D_MODEL)

    comm = StepComm(core, chip, [halves(w_o[0]), halves(ffn_w_up[0]), halves(ffn_w_down[0])], w_in.shape[2])
    params = {
        "w_in_p": _w_in_to_padded(g_in.reshape(N_CHIPS, -1, g_in.shape[-1])),
        "dn_conv": dn_conv_f, "dn_a_log": dn_a_log[0], "dn_dt_bias": dn_dt_bias[0], "dn_norm_g": dn_norm_g[0],
        "gla_w_gate2": gate2_f, "gla_b_gate": gla_b_gate[0], "gla_norm_g": gla_norm_g[0],
        "ln0_g": ln0_g, "ln0_b": ln0_b, "ln1_g": ln1_g[0], "ln1_b": ln1_b[0], "ln2_g": ln2_g[0], "ln2_b": ln2_b[0],
        "ffn_conv": ffn_conv_f, "ffn_conv_b": ffn_conv_b[0],
    }

    loss_row, grad_x, gp, dmod, from_chips = _local_step(
        x.reshape(n_b * t_len, D_MODEL), loss_target.reshape(n_b * t_len, D_MODEL), mod, params, n_b, t_len, comm)
    names = ["ln0_g", "ln0_b", "w_ada", "b_ada", "w_in", "dn_conv", "dn_a_log", "dn_dt_bias", "dn_norm_g",
             "gla_w_gate2", "gla_b_gate", "gla_norm_g", "w_o", "ln1_g", "ln1_b", "ffn_w_up", "ffn_conv", "ffn_conv_b",
             "ffn_w_down", "ln2_g", "ln2_b"]
    weights = dict(zip(names, [ln0_g, ln0_b, w_ada, b_ada, w_in, dn_conv, dn_a_log, dn_dt_bias, dn_norm_g, gla_w_gate2,
                               gla_b_gate, gla_norm_g, w_o, ln1_g, ln1_b, ffn_w_up, ffn_conv, ffn_conv_b, ffn_w_down,
                               ln2_g, ln2_b]))
    m_in = dict(zip(names, [m_ln0_g, m_ln0_b, m_w_ada, m_b_ada, m_w_in, m_dn_conv, m_dn_a_log, m_dn_dt_bias,
                            m_dn_norm_g, m_gla_w_gate2, m_gla_b_gate, m_gla_norm_g, m_w_o, m_ln1_g, m_ln1_b,
                            m_ffn_w_up, m_ffn_conv, m_ffn_conv_b, m_ffn_w_down, m_ln2_g, m_ln2_b]))
    v_in = dict(zip(names, [v_ln0_g, v_ln0_b, v_w_ada, v_b_ada, v_w_in, v_dn_conv, v_dn_a_log, v_dn_dt_bias,
                            v_dn_norm_g, v_gla_w_gate2, v_gla_b_gate, v_gla_norm_g, v_w_o, v_ln1_g, v_ln1_b,
                            v_ffn_w_up, v_ffn_conv, v_ffn_conv_b, v_ffn_w_down, v_ln2_g, v_ln2_b]))
    grads, delta, new_m, new_v = {}, {}, {}, {}

    def adamw_big(n, grad):
        view = (lambda a: a.T) if n == "w_in" else (lambda a: a)
        outs = _adamw(view(weights[n][0]), view(grad), view(m_in[n][0]), view(v_in[n][0]), "adamw_" + n)
        grads[n], delta[n], new_m[n], new_v[n] = (view(a)[None] for a in outs)

    g_w_in, g_w_o, g_w_up, g_w_down = comm.finish(*from_chips)

    summed_names = ["loss", "ln0_g", "ln0_b", "dn_conv", "dn_a_log", "dn_dt_bias", "dn_norm_g", "gla_w_gate2",
                    "gla_b_gate", "gla_norm_g", "ln1_g", "ln1_b", "ffn_conv", "ffn_conv_b", "ln2_g", "ln2_b"]
    summed_parts = [loss_row[0, 0:1]] + [gp[n] for n in summed_names[1:]]
    sum_rows = _rows_for(summed_parts)
    slab = jnp.concatenate([_pack(summed_parts, sum_rows), _pack([dmod], _rows_for([dmod]))], axis=0)
    gathered, total = _all_gather8(slab, "reduce_small")
    small_g = dict(zip(summed_names, _unpack(total.reshape(-1), [a.shape for a in summed_parts])))
    loss = small_g["loss"][0]
    dmod_rows = n_b * 6 * D_MODEL // SLAB_W
    dmod_all = gathered[:, sum_rows:sum_rows + dmod_rows, :].reshape(n_all, 6 * D_MODEL)
    for n, grad in (("ffn_w_up", g_w_up), ("ffn_w_down", g_w_down), ("w_o", g_w_o), ("w_in", g_w_in)):
        adamw_big(n, grad)

    g_b_ada = _col_sum(dmod_all)
    dmod_cols = lax.dynamic_slice(dmod_all, (0, chip * ada_cols), (n_all, ada_cols))
    adamw_big("w_ada", _mm(cond_all, dmod_cols, ta=True, name="mm_g_ada"))

    col_block = lambda a: lax.dynamic_slice(a, (0, chip * (a.shape[1] // N_CHIPS)), (a.shape[0], a.shape[1] // N_CHIPS))
    grads.update({
        "ln0_g": small_g["ln0_g"], "ln0_b": small_g["ln0_b"], "b_ada": g_b_ada,
        "dn_conv": col_block(small_g["dn_conv"])[None], "dn_a_log": small_g["dn_a_log"][None],
        "dn_dt_bias": small_g["dn_dt_bias"][None], "dn_norm_g": small_g["dn_norm_g"][None],
        "gla_w_gate2": col_block(small_g["gla_w_gate2"])[None], "gla_b_gate": small_g["gla_b_gate"][None],
        "gla_norm_g": small_g["gla_norm_g"][None], "ln1_g": small_g["ln1_g"][None],
        "ln1_b": small_g["ln1_b"][None], "ffn_conv": col_block(small_g["ffn_conv"])[None],
        "ffn_conv_b": small_g["ffn_conv_b"][None], "ln2_g": small_g["ln2_g"][None], "ln2_b": small_g["ln2_b"][None],
    })
    small = [n for n in names if n not in delta]
    d_s, m_s, v_s = _adamw_many([weights[n] for n in small], [grads[n] for n in small],
                                [m_in[n] for n in small], [v_in[n] for n in small])
    for out, vals in ((delta, d_s), (new_m, m_s), (new_v, v_s)):
        out.update(zip(small, vals))

    return (loss, grad_x.reshape(x.shape), *[grads[n] for n in names], *[delta[n] for n in names],
            *[new_m[n] for n in names], *[new_v[n] for n in names])
```

```python
import functools

import jax
import jax.numpy as jnp
from jax import lax
from jax.experimental import pallas as pl
from jax.experimental.pallas import tpu as pltpu

F32 = jnp.float32
BF16 = jnp.bfloat16
MESH = pl.DeviceIdType.MESH

D_MODEL = 1024
HEADS = 4
HEAD_DIM = 128
GLA_KEY = 64
GATE_RANK = 16
CHUNK = 64
D_FF = 2816
ALPHA = 2.0 ** 0.25
EPS = 1e-6
N_CHIPS = 4
N_DEV = 8

PROJ_W = 3840
OFF_GQ, OFF_GK, OFF_GV, OFF_GG, OFF_SMALL, GLA_W = 0, 256, 512, 1024, 1536, 1792
OFF_Z = 2048
W_IN_COLS = 3608


def _qkv_block(j):
    return jnp.where(j < 2, GLA_W // 128 + j, (OFF_Z + 512) // 128 - 2 + j)

ADAM_LR, ADAM_B1, ADAM_B2, ADAM_EPS, ADAM_WD, ADAM_STEP = 0.001, 0.9, 0.999, 1e-08, 0.01, 10

VMEM_LIMIT = 56 * 1024 * 1024
ROW_TILE = 512


def _cparams(sem):
    return pltpu.CompilerParams(dimension_semantics=sem, vmem_limit_bytes=VMEM_LIMIT)


def _pick(n, prefs):
    for p in prefs:
        if n % p == 0:
            return p
    return n


def _mm(a, b, *, ta=False, tb=False, out_slabs=1, out_dtype=F32, name, rider=None):
    a_slabs = a.shape[0] if a.ndim == 3 else 1
    b_slabs = b.shape[0] if b.ndim == 3 else 1
    assert not (ta and a_slabs > 1)
    a2, b2 = a.shape[-2:], b.shape[-2:]
    if ta:
        k_dim, m_dim = a2
    else:
        m_dim, k_dim = a2[0], a2[1] * a_slabs
    n_dim = b2[0] if tb else b2[1] * b_slabs
    k_slabs = max(a_slabs, b_slabs if tb else 1)
    n_slabs = max(out_slabs, 1 if tb else b_slabs)
    tm = _pick(m_dim, (1024, 1408, 512, 256, 128))
    tn = _pick(n_dim // n_slabs, (1536, 1408, 1280, 1024, 768, 512, 384, 256, 128))
    tk = _pick(k_dim // k_slabs, (1408, 1280, 1024, 512, 256, 128))
    nk, nj = k_dim // tk, n_dim // tn
    nk_a, nk_b, nj_b, nj_o = nk // a_slabs, nk // b_slabs, nj // b_slabs, nj // out_slabs
    dims = (((0 if ta else 1,), (1 if tb else 0,)), ((), ()))

    grid = (m_dim // tm, nj, nk)
    assert out_dtype == F32
    r_inputs, r_in_specs, r_out_specs, r_sems, split = _with_rider(rider, 2, 1, 0)

    def body(*refs):
        (a_ref, b_ref, o_ref), parts = split(refs)
        ride_first, ride_last = _ride(rider, parts, grid)
        if rider is not None:
            ride_first()
        prod = lax.dot_general(a_ref[...].astype(BF16), b_ref[...].astype(BF16), dims, preferred_element_type=F32)
        if nk == 1:
            o_ref[...] = prod
        else:
            _acc(o_ref, prod, pl.program_id(2) == 0)
        if rider is not None:
            ride_last()

    if ta:
        a_spec = pl.BlockSpec((tk, tm), lambda i, j, k: (k, i))
    elif a_slabs > 1:
        a_spec = pl.BlockSpec((None, tm, tk), lambda i, j, k: (k // nk_a, i, k % nk_a))
    else:
        a_spec = pl.BlockSpec((tm, tk), lambda i, j, k: (i, k))
    if tb and b_slabs > 1:
        b_spec = pl.BlockSpec((None, tn, tk), lambda i, j, k: (k // nk_b, j, k % nk_b))
    elif tb:
        b_spec = pl.BlockSpec((tn, tk), lambda i, j, k: (j, k))
    elif b_slabs > 1:
        b_spec = pl.BlockSpec((None, tk, tn), lambda i, j, k: (j // nj_b, k, j % nj_b))
    else:
        b_spec = pl.BlockSpec((tk, tn), lambda i, j, k: (k, j))
    if out_slabs > 1:
        o_spec = pl.BlockSpec((None, tm, tn), lambda i, j, k: (j // nj_o, i, j % nj_o))
        o_shape = (out_slabs, m_dim, n_dim // out_slabs)
    else:
        o_spec, o_shape = pl.BlockSpec((tm, tn), lambda i, j, k: (i, j)), (m_dim, n_dim)
    out, *rider_outs = pl.pallas_call(
        body, name=name, grid=grid,
        in_specs=[a_spec, b_spec] + r_in_specs, out_specs=[o_spec] + r_out_specs,
        out_shape=[jax.ShapeDtypeStruct(o_shape, out_dtype)] + (list(rider.out_shapes) if rider else []),
        scratch_shapes=r_sems,
        compiler_params=_cparams(("arbitrary",) * 3 if rider else ("parallel", "parallel", "arbitrary")),
    )(a, b, *r_inputs)
    return (out, rider_outs) if rider else out


def _ln(x, g, b):
    mu = jnp.mean(x, -1, keepdims=True)
    xc = x - mu
    var = jnp.mean(xc * xc, -1, keepdims=True)
    return xc * lax.rsqrt(var + EPS) * g + b


def _softplus(x):
    return jnp.maximum(x, 0.0) + jnp.log(1.0 + jnp.exp(-jnp.abs(x)))


def _silu(x):
    return x * jax.nn.sigmoid(x)


def _dsilu(x):
    s = jax.nn.sigmoid(x)
    return s * (1.0 + x * (1.0 - s))


def _f_ln0(x, g, b, sc, sh):
    x0 = _ln(x, g, b)
    return x0, x0 * (1.0 + sc) + sh


def _f_ln1(x0, y, gt, g, b, sc, sh):
    x1 = _ln(ALPHA * x0 + (1.0 + gt) * y, g, b)
    return x1, x1 * (1.0 + sc) + sh


def _f_ln2_loss(x1, y2, gt, g, b, tgt):
    x2 = _ln(ALPHA * x1 + (1.0 + gt) * y2, g, b)
    err = x2 - tgt
    per_row = jnp.sum(err * err, -1, keepdims=True) * (0.5 / D_MODEL)
    return jnp.sum(per_row, 0, keepdims=True)


def _row_specs(t_len):
    nt = t_len // ROW_TILE
    row = pl.BlockSpec((ROW_TILE, D_MODEL), lambda b, i: (b * nt + i, 0))
    vec = pl.BlockSpec((1, D_MODEL), lambda b, i: (0, 0))
    mod = pl.BlockSpec((None, 6, D_MODEL), lambda b, i: (b, 0, 0))
    return nt, row, vec, mod


def _first_step():
    return jnp.logical_and(pl.program_id(0) == 0, pl.program_id(1) == 0)


def _acc(ref, val, first, at=(Ellipsis,)):
    @pl.when(first)
    def _():
        ref[at] = val

    @pl.when(jnp.logical_not(first))
    def _():
        ref[at] += val


def _acc_rows(ref, rows, first):
    for i, r in enumerate(rows):
        _acc(ref, r, first, at=(slice(i, i + 1), slice(None)))


def _ln0_fwd(x, g, b, mod, n_b, t_len):
    nt, row, vec, mods = _row_specs(t_len)

    def body(x_ref, g_ref, b_ref, mod_ref, x0_ref, h_ref):
        x0, h = _f_ln0(x_ref[...], g_ref[...], b_ref[...], mod_ref[1:2, :], mod_ref[0:1, :])
        x0_ref[...] = x0
        h_ref[...] = h.astype(BF16)

    return pl.pallas_call(
        body, name="ln0_fwd", grid=(n_b, nt), in_specs=[row, vec, vec, mods], out_specs=[row, row],
        out_shape=[jax.ShapeDtypeStruct(x.shape, F32), jax.ShapeDtypeStruct(x.shape, BF16)],
        compiler_params=_cparams(("parallel", "parallel")),
    )(x, g, b, mod)


def _ln0_bwd(x, g, b, mod, dx0, dh, n_b, t_len):
    nt, row, vec, mods = _row_specs(t_len)
    dmod_spec = pl.BlockSpec((None, 2, D_MODEL), lambda bb, i: (bb, 0, 0))

    def body(x_ref, g_ref, b_ref, mod_ref, dx0_ref, dh_ref, dx_ref, dg_ref, db_ref, dmod_ref):
        _, pull = jax.vjp(_f_ln0, x_ref[...], g_ref[...], b_ref[...], mod_ref[1:2, :], mod_ref[0:1, :])
        dx, dg, db, dsc, dsh = pull((dx0_ref[...], dh_ref[...]))
        dx_ref[...] = dx
        _acc(dg_ref, dg, _first_step())
        _acc(db_ref, db, _first_step())
        _acc_rows(dmod_ref, [dsh, dsc], pl.program_id(1) == 0)

    return pl.pallas_call(
        body, name="ln0_bwd", grid=(n_b, nt), in_specs=[row, vec, vec, mods, row, row],
        out_specs=[row, vec, vec, dmod_spec],
        out_shape=[jax.ShapeDtypeStruct(x.shape, F32), jax.ShapeDtypeStruct((1, D_MODEL), F32),
                   jax.ShapeDtypeStruct((1, D_MODEL), F32), jax.ShapeDtypeStruct((n_b, 2, D_MODEL), F32)],
        compiler_params=_cparams(("arbitrary", "arbitrary")),
    )(x, g, b, mod, dx0, dh)


def _ln1_fwd(x0, y, g, b, mod, n_b, t_len):
    nt, row, vec, mods = _row_specs(t_len)

    def body(x0_ref, y_ref, g_ref, b_ref, mod_ref, x1_ref, h_ref):
        x1, h = _f_ln1(x0_ref[...], y_ref[...], mod_ref[2:3, :], g_ref[...], b_ref[...],
                       mod_ref[4:5, :], mod_ref[3:4, :])
        x1_ref[...] = x1
        h_ref[...] = h.astype(BF16)

    return pl.pallas_call(
        body, name="ln1_fwd", grid=(n_b, nt), in_specs=[row, row, vec, vec, mods], out_specs=[row, row],
        out_shape=[jax.ShapeDtypeStruct(x0.shape, F32), jax.ShapeDtypeStruct(x0.shape, BF16)],
        compiler_params=_cparams(("parallel", "parallel")),
    )(x0, y, g, b, mod)


def _ln1_bwd(x0, y, g, b, mod, dx1, dh, n_b, t_len):
    nt, row, vec, mods = _row_specs(t_len)
    dmod_spec = pl.BlockSpec((None, 3, D_MODEL), lambda bb, i: (bb, 0, 0))

    def body(x0_ref, y_ref, g_ref, b_ref, mod_ref, dx1_ref, dh_ref, dx0_ref, dy_ref, dg_ref, db_ref, dmod_ref):
        _, pull = jax.vjp(_f_ln1, x0_ref[...], y_ref[...], mod_ref[2:3, :], g_ref[...], b_ref[...],
                          mod_ref[4:5, :], mod_ref[3:4, :])
        dx0, dy, dgt, dg, db, dsc, dsh = pull((dx1_ref[...], dh_ref[...]))
        dx0_ref[...] = dx0
        dy_ref[...] = dy.astype(BF16)
        _acc(dg_ref, dg, _first_step())
        _acc(db_ref, db, _first_step())
        _acc_rows(dmod_ref, [dgt, dsh, dsc], pl.program_id(1) == 0)

    return pl.pallas_call(
        body, name="ln1_bwd", grid=(n_b, nt), in_specs=[row, row, vec, vec, mods, row, row],
        out_specs=[row, row, vec, vec, dmod_spec],
        out_shape=[jax.ShapeDtypeStruct(x0.shape, F32), jax.ShapeDtypeStruct(x0.shape, BF16),
                   jax.ShapeDtypeStruct((1, D_MODEL), F32), jax.ShapeDtypeStruct((1, D_MODEL), F32),
                   jax.ShapeDtypeStruct((n_b, 3, D_MODEL), F32)],
        compiler_params=_cparams(("arbitrary", "arbitrary")),
    )(x0, y, g, b, mod, dx1, dh)


def _ln2_loss_bwd(x1, y2, g, b, mod, tgt, n_b, t_len):
    nt, row, vec, mods = _row_specs(t_len)
    one = pl.BlockSpec((1, 128), lambda bb, i: (0, 0))
    dmod_spec = pl.BlockSpec((None, 1, D_MODEL), lambda bb, i: (bb, 0, 0))

    def body(x1_ref, y2_ref, g_ref, b_ref, mod_ref, t_ref, loss_ref, dx1_ref, dy2_ref, dg_ref, db_ref, dgt_ref):
        loss, pull = jax.vjp(functools.partial(_f_ln2_loss, tgt=t_ref[...]), x1_ref[...], y2_ref[...],
                             mod_ref[5:6, :], g_ref[...], b_ref[...])
        dx1, dy2, dgt, dg, db = pull(jnp.ones((1, 1), F32))
        dx1_ref[...] = dx1
        dy2_ref[...] = dy2.astype(BF16)
        _acc(loss_ref, jnp.broadcast_to(loss, (1, 128)), _first_step())
        _acc(dg_ref, dg, _first_step())
        _acc(db_ref, db, _first_step())
        _acc(dgt_ref, dgt, pl.program_id(1) == 0)

    return pl.pallas_call(
        body, name="ln2_loss_bwd", grid=(n_b, nt), in_specs=[row, row, vec, vec, mods, row],
        out_specs=[one, row, row, vec, vec, dmod_spec],
        out_shape=[jax.ShapeDtypeStruct((1, 128), F32), jax.ShapeDtypeStruct(x1.shape, F32),
                   jax.ShapeDtypeStruct(x1.shape, BF16), jax.ShapeDtypeStruct((1, D_MODEL), F32),
                   jax.ShapeDtypeStruct((1, D_MODEL), F32), jax.ShapeDtypeStruct((n_b, 1, D_MODEL), F32)],
        compiler_params=_cparams(("arbitrary", "arbitrary")),
    )(x1, y2, g, b, mod, tgt)


def _shift_down(x, s):
    if s == 0:
        return x
    rows = lax.broadcasted_iota(jnp.int32, x.shape, 0)
    return jnp.where(rows >= s, pltpu.roll(x, s, 0), 0.0)


def _shift_up(x, s):
    if s == 0:
        return x
    t_len = x.shape[0]
    rows = lax.broadcasted_iota(jnp.int32, x.shape, 0)
    return jnp.where(rows < t_len - s, pltpu.roll(x, t_len - s, 0), 0.0)


def _taps(x, k_w):
    return [_shift_down(x, k_w - 1 - k) for k in range(k_w)]


def _conv(taps, w):
    out = w[0:1, :] * taps[0]
    for k in range(1, len(taps)):
        out = out + w[k:k + 1, :] * taps[k]
    return out


def _conv_bwd(taps, w, du):
    k_w = len(taps)
    dx = w[k_w - 1:k_w, :] * du
    for k in range(k_w - 1):
        dx = dx + w[k:k + 1, :] * _shift_up(du, k_w - 1 - k)
    return dx, [jnp.sum(du * taps[k], 0, keepdims=True) for k in range(k_w)]


def _dn_pre_fwd(proj, conv_w, n_b, t_len):
    n_ct = 3 * HEADS
    k_w = conv_w.shape[0]

    def body(x_ref, w_ref, o_ref):
        o_ref[...] = _silu(_conv(_taps(x_ref[...], k_w), w_ref[...]))

    return pl.pallas_call(
        body, name="dn_pre_fwd", grid=(n_ct, n_b),
        in_specs=[pl.BlockSpec((t_len, 128), lambda j, b: (b, _qkv_block(j))),
                  pl.BlockSpec((k_w, 128), lambda j, b: (0, j))],
        out_specs=pl.BlockSpec((t_len, 128), lambda j, b: (b, j)),
        out_shape=jax.ShapeDtypeStruct((n_b * t_len, n_ct * 128), F32),
        compiler_params=_cparams(("parallel", "parallel")),
    )(proj, conv_w)


def _dn_pre_bwd(proj, conv_w, dqkv, d_proj, n_b, t_len):
    n_ct = 3 * HEADS
    k_w = conv_w.shape[0]

    def body(x_ref, w_ref, d_ref, _, dx_ref, dw_ref):
        taps, w = _taps(x_ref[...], k_w), w_ref[...]
        du = d_ref[...] * _dsilu(_conv(taps, w))
        dx, dw = _conv_bwd(taps, w, du)
        dx_ref[...] = dx.astype(BF16)
        _acc_rows(dw_ref, dw, pl.program_id(1) == 0)

    return pl.pallas_call(
        body, name="dn_pre_bwd", grid=(n_ct, n_b),
        in_specs=[pl.BlockSpec((t_len, 128), lambda j, b: (b, _qkv_block(j))),
                  pl.BlockSpec((k_w, 128), lambda j, b: (0, j)),
                  pl.BlockSpec((t_len, 128), lambda j, b: (b, j)), pl.BlockSpec(memory_space=pl.ANY)],
        out_specs=[pl.BlockSpec((t_len, 128), lambda j, b: (b, _qkv_block(j))),
                   pl.BlockSpec((k_w, 128), lambda j, b: (0, j))],
        out_shape=[jax.ShapeDtypeStruct(d_proj.shape, BF16), jax.ShapeDtypeStruct((k_w, n_ct * 128), F32)],
        input_output_aliases={3: 0},
        compiler_params=_cparams(("parallel", "arbitrary")),
    )(proj, conv_w, dqkv, d_proj)


FFN_TC = 256
FFN_NT = D_FF // FFN_TC


def _ffn_specs(t_len):
    blk = lambda off: pl.BlockSpec((t_len, FFN_TC), lambda j, b: (b, j + off))
    wblk = lambda off: pl.BlockSpec((3, FFN_TC), lambda j, b: (0, j + off))
    bblk = lambda off: pl.BlockSpec((1, FFN_TC), lambda j, b: (0, j + off))
    return [blk(0), blk(FFN_NT), wblk(0), wblk(FFN_NT), bblk(0), bblk(FFN_NT)]


def _ffn_act_fwd(up, conv_w, conv_b, n_b, t_len):
    def body(g_ref, v_ref, wg_ref, wv_ref, bg_ref, bv_ref, o_ref):
        ug = _conv(_taps(g_ref[...], 3), wg_ref[...]) + bg_ref[...]
        uv = _conv(_taps(v_ref[...], 3), wv_ref[...]) + bv_ref[...]
        o_ref[...] = (_silu(ug) * uv).astype(BF16)

    return pl.pallas_call(
        body, name="ffn_act_fwd", grid=(FFN_NT, n_b), in_specs=_ffn_specs(t_len),
        out_specs=pl.BlockSpec((t_len, FFN_TC), lambda j, b: (b, j)),
        out_shape=jax.ShapeDtypeStruct((n_b * t_len, D_FF), BF16),
        compiler_params=_cparams(("parallel", "parallel")),
    )(up, up, conv_w, conv_w, conv_b, conv_b)


def _ffn_act_bwd(up, conv_w, conv_b, da, n_b, t_len):
    def body(g_ref, v_ref, wg_ref, wv_ref, bg_ref, bv_ref, da_ref, dup_ref, dw_ref, db_ref):
        first = pl.program_id(1) == 0
        tg, tv, wg, wv = _taps(g_ref[...], 3), _taps(v_ref[...], 3), wg_ref[...], wv_ref[...]
        ug = _conv(tg, wg) + bg_ref[...]
        uv = _conv(tv, wv) + bv_ref[...]
        d_act = da_ref[...]
        sig = jax.nn.sigmoid(ug)
        d_v = d_act * (ug * sig)
        d_g = d_act * uv * (sig * (1.0 + ug * (1.0 - sig)))
        for slab, (taps, w, du) in enumerate(((tg, wg, d_g), (tv, wv, d_v))):
            dx, dw = _conv_bwd(taps, w, du)
            dup_ref[slab] = dx.astype(BF16)
            for k, dw_k in enumerate(dw):
                _acc(dw_ref, dw_k, first, at=(slab, slice(k, k + 1), slice(None)))
            _acc(db_ref, jnp.sum(du, 0, keepdims=True), first, at=(slab, slice(None), slice(None)))

    return pl.pallas_call(
        body, name="ffn_act_bwd", grid=(FFN_NT, n_b),
        in_specs=_ffn_specs(t_len) + [pl.BlockSpec((t_len, FFN_TC), lambda j, b: (b, j))],
        out_specs=[pl.BlockSpec((2, t_len, FFN_TC), lambda j, b: (0, b, j)),
                   pl.BlockSpec((2, 3, FFN_TC), lambda j, b: (0, 0, j)),
                   pl.BlockSpec((2, 1, FFN_TC), lambda j, b: (0, 0, j))],
        out_shape=[jax.ShapeDtypeStruct((2, n_b * t_len, D_FF), BF16),
                   jax.ShapeDtypeStruct((2, 3, D_FF), F32), jax.ShapeDtypeStruct((2, 1, D_FF), F32)],
        compiler_params=_cparams(("parallel", "arbitrary")),
    )(up, up, conv_w, conv_w, conv_b, conv_b, da)


NN = (((2,), (1,)), ((0,), (0,)))
NT = (((2,), (2,)), ((0,), (0,)))
TN = (((1,), (1,)), ((0,), (0,)))


def _iota3(shape, axis):
    return lax.broadcasted_iota(jnp.int32, shape, axis)


def _dg(a, b, dims):
    return lax.dot_general(a, b, dims, preferred_element_type=F32)


def _dot(a, b):
    return _dg(a, b, NN)


def _dot_nt(a, b):
    return _dg(a, b, NT)


def _dot_tn(a, b):
    return _dg(a, b, TN)


def _split(a):
    hi = a.astype(BF16)
    return hi, (a - hi.astype(F32)).astype(BF16)


def _dg3(a, b, dims):
    ah, al = _split(a)
    bh, bl = _split(b)
    return _dg(ah, bh, dims) + (_dg(ah, bl, dims) + _dg(al, bh, dims))


@jax.custom_vjp
def _dot3(a, b):
    return _dg3(a, b, NN)


def _dot3_fwd(a, b):
    return _dg3(a, b, NN), (a, b)


def _dot3_bwd(res, g):
    a, b = res
    return _dg3(g, b, NT), _dg3(a, g, TN)


_dot3.defvjp(_dot3_fwd, _dot3_bwd)


def _lower_ones(g_n, n):
    shape = (g_n, n, n)
    return jnp.where(_iota3(shape, 1) >= _iota3(shape, 2), 1.0, 0.0).astype(BF16)


@jax.custom_vjp
def _chunk_cumsum(x):
    hi, lo = _split(x)
    tri = _lower_ones(x.shape[0], x.shape[1])
    return _dg(tri, hi, NN) + _dg(tri, lo, NN)


def _chunk_cumsum_fwd(x):
    return _chunk_cumsum(x), None


def _chunk_cumsum_bwd(_, g):
    hi, lo = _split(g)
    tri = _lower_ones(g.shape[0], g.shape[1])
    return (_dg(tri, hi, TN) + _dg(tri, lo, TN),)


_chunk_cumsum.defvjp(_chunk_cumsum_fwd, _chunk_cumsum_bwd)


@jax.custom_vjp
def _unit_lower_inv(m):
    n = m.shape[1]
    p = -m
    a = jnp.where(_iota3(m.shape, 1) == _iota3(m.shape, 2), 1.0, 0.0) + p
    span = 2
    while span < n:
        p = _dg3(p, p, NN)
        a = a + _dg3(a, p, NN)
        span *= 2
    return a


def _unit_lower_inv_fwd(m):
    a = _unit_lower_inv(m)
    return a, a


def _unit_lower_inv_bwd(a, da):
    return (-_dg3(a, _dg3(da, a, NT), TN),)


_unit_lower_inv.defvjp(_unit_lower_inv_fwd, _unit_lower_inv_bwd)


@jax.custom_vjp
def _saved_lower_inv(m, a):
    return a


def _saved_lower_inv_fwd(m, a):
    return a, a


def _saved_lower_inv_bwd(a, da):
    return _unit_lower_inv_bwd(a, da)[0], jnp.zeros_like(a)


_saved_lower_inv.defvjp(_saved_lower_inv_fwd, _saved_lower_inv_bwd)


def _rms_gate(o, gn, gate):
    return o * lax.rsqrt(jnp.mean(o * o, -1, keepdims=True) + EPS) * gn * _silu(gate)


def _dn_chains(q, k, v, z, small, s_in, a_log, dt_bias, gn, a_saved=None):
    g_n, c_len = q.shape[0], q.shape[1]
    sq = (g_n, c_len, c_len)
    row, col = _iota3(sq, 1), _iota3(sq, 2)
    causal, strict, eye = row >= col, row > col, row == col
    qn = q * lax.rsqrt(jnp.sum(q * q, -1, keepdims=True) + EPS) * (HEAD_DIM ** -0.5)
    kn = k * lax.rsqrt(jnp.sum(k * k, -1, keepdims=True) + EPS)
    lane = _iota3(small.shape, 2)
    head = jnp.bitwise_and(_iota3(small.shape, 0), HEADS - 1)
    la_all = -jnp.exp(a_log) * _softplus(small + dt_bias)
    la_c = jnp.sum(jnp.where(lane == head, la_all, 0.0), 2, keepdims=True)
    beta = jnp.sum(jnp.where(lane == head + HEADS, jax.nn.sigmoid(small), 0.0), 2, keepdims=True)
    la_b = jnp.broadcast_to(la_c, sq)
    la_r = jnp.sum(jnp.where(eye, la_b, 0.0), 1, keepdims=True)
    g_c = jnp.sum(jnp.where(causal, jnp.broadcast_to(la_r, sq), 0.0), 2, keepdims=True)
    g_r = jnp.sum(jnp.where(row <= col, la_b, 0.0), 1, keepdims=True)
    g_last = jnp.sum(la_c, 1, keepdims=True)
    decay = jnp.exp(jnp.where(causal, g_c - g_r, -1e30))
    e_g = jnp.exp(g_c)
    kb = kn * beta
    m_low = jnp.where(strict, _dot_nt(kb, kn) * decay, 0.0)
    a_inv = _unit_lower_inv(m_low) if a_saved is None else _saved_lower_inv(m_low, a_saved)
    u = _dot3(a_inv, v * beta)
    w = _dot3(a_inv, kb * e_g)
    attn = _dot_nt(qn, kn) * decay
    v_new = u - _dot(w, s_in)
    o = _dot(qn * e_g, s_in) + _dot(attn, v_new)
    s_out = s_in * jnp.exp(g_last) + _dot_tn(kn * jnp.exp(g_last - g_c), v_new)
    return _rms_gate(o, gn, z), s_out, a_inv


def _gla_chains(q, k, v, gate, small, s_in, w2, b2, gn):
    g_n, c_len = q.shape[0], q.shape[1]
    sq, kk = (g_n, c_len, c_len), (g_n, GLA_KEY, GLA_KEY)
    causal = _iota3(sq, 1) >= _iota3(sq, 2)
    la = -_softplus(-(_dot(small, w2) + b2)) * (1.0 / 16.0)
    b = _chunk_cumsum(la)
    b_last = jnp.sum(jnp.where(_iota3(b.shape, 1) == c_len - 1, b, 0.0), 1, keepdims=True)
    q_dec = q * (GLA_KEY ** -0.5) * jnp.exp(b)
    attn = jnp.where(causal, _dot_nt(q_dec, k * jnp.exp(-b)), 0.0)
    o = _dot(q_dec, s_in) + _dot(attn, v)
    g_row = jnp.exp(b_last)
    g_col = jnp.sum(jnp.where(_iota3(kk, 1) == _iota3(kk, 2), jnp.broadcast_to(g_row, kk), 0.0), 2, keepdims=True)
    s_out = s_in * g_col + _dot_tn(k * jnp.exp(b_last - b), v)
    return _rms_gate(o, gn, gate), s_out


def _chunk_spec(n_b, width, col_block, n_c, reverse=False):
    if reverse:
        return pl.BlockSpec((n_b, CHUNK, width), lambda n: (0, n_c - 1 - n, col_block))
    return pl.BlockSpec((n_b, CHUNK, width), lambda n: (0, n, col_block))


def _hist_spec(n_b, d_k, n_c, reverse=False):
    if reverse:
        return pl.BlockSpec((None, n_b * HEADS, d_k, HEAD_DIM), lambda n: (n_c - 1 - n, 0, 0, 0))
    return pl.BlockSpec((None, n_b * HEADS, d_k, HEAD_DIM), lambda n: (n, 0, 0, 0))


def _ainv_spec(n_b, n_c, reverse=False):
    if reverse:
        return pl.BlockSpec((None, n_b * HEADS, CHUNK, CHUNK), lambda n: (n_c - 1 - n, 0, 0, 0))
    return pl.BlockSpec((None, n_b * HEADS, CHUNK, CHUNK), lambda n: (n, 0, 0, 0))


def _stack_chains(ref, n_b, slices):
    return jnp.stack([ref[b, :, sl] for b in range(n_b) for sl in slices], axis=0)


def _per_chain(ref, n_b):
    return jnp.stack([ref[b] for b in range(n_b) for _ in range(HEADS)], axis=0)


def _unstack_chains(ref, val, n_b, slices, offset=0):
    for b in range(n_b):
        for h, sl in enumerate(slices):
            ref[b, :, slice(offset + sl.start, offset + sl.stop)] = val[b * HEADS + h].astype(ref.dtype)


def _gate_weights(w2_ref, b2_ref, n_b):
    w2 = jnp.stack([w2_ref[:, ks] for _ in range(n_b) for ks in GLA_KSL], axis=0)
    b2 = jnp.stack([b2_ref[:, ks] for _ in range(n_b) for ks in GLA_KSL], axis=0)
    return w2, b2


def _sum_heads(val, n_b):
    return [sum(val[b * HEADS + h] for h in range(HEADS)) for b in range(n_b)]


def _const_spec(shape):
    return pl.BlockSpec(shape, lambda n: (0,) * len(shape))


DN_SL = [slice(h * HEAD_DIM, (h + 1) * HEAD_DIM) for h in range(HEADS)]
GLA_KSL = [slice(h * GLA_KEY, (h + 1) * GLA_KEY) for h in range(HEADS)]


class Rider:
    def __init__(self, inputs, out_shapes, sems, first, last):
        self.inputs, self.out_shapes, self.sems, self.first, self.last = inputs, out_shapes, sems, first, last


def _with_rider(rider, n_in, n_out, n_scratch):
    if rider is None:
        return [], [], [], [], lambda refs: (refs, None)
    r_in, r_out, r_sem = len(rider.inputs), len(rider.out_shapes), len(rider.sems)

    def split(refs):
        own_in, rest = refs[:n_in], refs[n_in:]
        rid_in, rest = rest[:r_in], rest[r_in:]
        own_out, rest = rest[:n_out], rest[n_out:]
        rid_out, rest = rest[:r_out], rest[r_out:]
        own_scr, rid_sem = rest[:n_scratch], rest[n_scratch:]
        return own_in + own_out + own_scr, (rid_in, rid_out, rid_sem)

    return list(rider.inputs), [HBM_SPEC] * r_in, [HBM_SPEC] * r_out, list(rider.sems), split


def _ride(rider, parts, grid):
    if rider is None:
        return None, None
    grid = grid if isinstance(grid, tuple) else (grid,)

    def at(step_of):
        hit = pl.program_id(0) == step_of(grid[0])
        for axis in range(1, len(grid)):
            hit = jnp.logical_and(hit, pl.program_id(axis) == step_of(grid[axis]))
        return hit

    def first():
        pl.when(at(lambda n: 0))(lambda: rider.first(*parts))

    def last():
        pl.when(at(lambda n: n - 1))(lambda: rider.last(*parts))

    return first, last


def _dn_scan_fwd(qkv, proj, a_log, dt_bias, gn, n_b, t_len, rider=None):
    n_c = t_len // CHUNK
    spec = functools.partial(_chunk_spec, n_b, n_c=n_c)
    r_inputs, r_in_specs, r_out_specs, r_sems, split = _with_rider(rider, 8, 3, 1)

    def body(*refs):
        (q_ref, k_ref, v_ref, z_ref, sm_ref, al_ref, dt_ref, gn_ref,
         o_ref, hist_ref, ainv_ref, s_ref), parts = split(refs)
        ride_first, ride_last = _ride(rider, parts, n_c)
        if rider is not None:
            ride_first()

        @pl.when(pl.program_id(0) == 0)
        def _():
            s_ref[...] = jnp.zeros_like(s_ref)

        s_in = s_ref[...]
        hist_ref[...] = s_in
        og, s_out, a_inv = _dn_chains(*(_stack_chains(r, n_b, DN_SL) for r in (q_ref, k_ref, v_ref, z_ref)),
                                      _per_chain(sm_ref, n_b), s_in, al_ref[...], dt_ref[...], gn_ref[...])
        _unstack_chains(o_ref, og, n_b, DN_SL)
        s_ref[...] = s_out
        ainv_ref[...] = a_inv
        if rider is not None:
            ride_last()

    qkv3, proj3 = qkv.reshape(n_b, t_len, -1), proj.reshape(n_b, t_len, -1)
    o, hist, ainv, *rider_outs = pl.pallas_call(
        body, name="dn_scan_fwd", grid=(n_c,),
        in_specs=[spec(512, 0), spec(512, 1), spec(512, 2), spec(512, OFF_Z // 512), spec(128, OFF_SMALL // 128),
                  _const_spec((1, 128)), _const_spec((1, 128)), _const_spec((1, 128))] + r_in_specs,
        out_specs=[spec(512, 0), _hist_spec(n_b, HEAD_DIM, n_c), _ainv_spec(n_b, n_c)] + r_out_specs,
        out_shape=[jax.ShapeDtypeStruct((n_b, t_len, 2 * 512), BF16),
                   jax.ShapeDtypeStruct((n_c, n_b * HEADS, HEAD_DIM, HEAD_DIM), F32),
                   jax.ShapeDtypeStruct((n_c, n_b * HEADS, CHUNK, CHUNK), F32)]
        + (list(rider.out_shapes) if rider else []),
        scratch_shapes=[pltpu.VMEM((n_b * HEADS, HEAD_DIM, HEAD_DIM), F32)] + r_sems,
        compiler_params=_cparams(("arbitrary",)),
    )(qkv3, qkv3, qkv3, proj3, proj3, a_log, dt_bias, gn, *r_inputs)
    return o, (hist, ainv), rider_outs


def _dn_scan_bwd(qkv, proj, a_log, dt_bias, gn, hist, d_o, n_b, t_len, rider=None):
    n_c = t_len // CHUNK
    rev = functools.partial(_chunk_spec, n_b, n_c=n_c, reverse=True)
    r_inputs, r_in_specs, r_out_specs, r_sems, split = _with_rider(rider, 11, 6, 1)
    hist, ainv = hist

    def body(*refs):
        (q_ref, k_ref, v_ref, z_ref, sm_ref, al_ref, dt_ref, gn_ref, hist_ref, ainv_ref, do_ref,
         dqkv_ref, dz_ref, dsm_ref, dal_ref, ddt_ref, dgn_ref, ds_ref), parts = split(refs)
        ride_first, ride_last = _ride(rider, parts, n_c)
        if rider is not None:
            ride_first()
        first = pl.program_id(0) == 0

        @pl.when(first)
        def _():
            ds_ref[...] = jnp.zeros_like(ds_ref)

        chains = lambda *a: _dn_chains(*a, a_saved=ainv_ref[...])[:2]
        _, pull = jax.vjp(chains, *(_stack_chains(r, n_b, DN_SL) for r in (q_ref, k_ref, v_ref, z_ref)),
                          _per_chain(sm_ref, n_b), hist_ref[...], al_ref[...], dt_ref[...], gn_ref[...])
        dq, dk, dv, dz, dsm, ds_in, dal, ddt, dgn = pull((_stack_chains(do_ref, n_b, DN_SL), ds_ref[...]))
        _unstack_chains(dqkv_ref, dq, n_b, DN_SL)
        _unstack_chains(dqkv_ref, dk, n_b, DN_SL, offset=512)
        _unstack_chains(dqkv_ref, dv, n_b, DN_SL, offset=1024)
        _unstack_chains(dz_ref, dz, n_b, DN_SL)
        ds_ref[...] = ds_in
        for b, dsm_b in enumerate(_sum_heads(dsm, n_b)):
            dsm_ref[b] = dsm_b
        _acc(dal_ref, dal, first)
        _acc(ddt_ref, ddt, first)
        _acc(dgn_ref, dgn, first)
        if rider is not None:
            ride_last()

    qkv3, proj3, do3 = (a.reshape(n_b, t_len, -1) for a in (qkv, proj, d_o))
    vec = jax.ShapeDtypeStruct((1, 128), F32)
    dqkv, d_proj, dsm, dal, ddt, dgn, *rider_outs = pl.pallas_call(
        body, name="dn_scan_bwd", grid=(n_c,),
        in_specs=[rev(512, 0), rev(512, 1), rev(512, 2), rev(512, OFF_Z // 512), rev(128, OFF_SMALL // 128),
                  _const_spec((1, 128)), _const_spec((1, 128)), _const_spec((1, 128)),
                  _hist_spec(n_b, HEAD_DIM, n_c, reverse=True), _ainv_spec(n_b, n_c, reverse=True),
                  rev(512, 0)] + r_in_specs,
        out_specs=[rev(1536, 0), rev(512, OFF_Z // 512), rev(128, 0),
                   _const_spec((1, 128)), _const_spec((1, 128)), _const_spec((1, 128))] + r_out_specs,
        out_shape=[jax.ShapeDtypeStruct((n_b, t_len, 1536), F32), jax.ShapeDtypeStruct((n_b, t_len, PROJ_W), BF16),
                   jax.ShapeDtypeStruct((n_b, t_len, 128), F32), vec, vec, vec]
        + (list(rider.out_shapes) if rider else []),
        scratch_shapes=[pltpu.VMEM((n_b * HEADS, HEAD_DIM, HEAD_DIM), F32)] + r_sems,
        compiler_params=_cparams(("arbitrary",)),
    )(qkv3, qkv3, qkv3, proj3, proj3, a_log, dt_bias, gn, hist, ainv, do3, *r_inputs)
    return dqkv.reshape(n_b * t_len, 1536), d_proj, dsm, dal, ddt, dgn, rider_outs


def _gla_scan_fwd(proj, w2, b2, gn, o_mix, n_b, t_len):
    n_c = t_len // CHUNK
    spec = functools.partial(_chunk_spec, n_b, n_c=n_c)

    def body(q_ref, k_ref, v_ref, g_ref, sm_ref, w2_ref, b2_ref, gn_ref, _, o_ref, hist_ref, s_ref):
        @pl.when(pl.program_id(0) == 0)
        def _():
            s_ref[...] = jnp.zeros_like(s_ref)

        s_in = s_ref[...]
        hist_ref[...] = s_in
        og, s_out = _gla_chains(_stack_chains(q_ref, n_b, GLA_KSL), _stack_chains(k_ref, n_b, GLA_KSL),
                                _stack_chains(v_ref, n_b, DN_SL), _stack_chains(g_ref, n_b, DN_SL),
                                _per_chain(sm_ref, n_b), s_in, *_gate_weights(w2_ref, b2_ref, n_b), gn_ref[...])
        _unstack_chains(o_ref, og, n_b, DN_SL)
        s_ref[...] = s_out

    proj3 = proj.reshape(n_b, t_len, -1)
    o, hist = pl.pallas_call(
        body, name="gla_scan_fwd", grid=(n_c,),
        in_specs=[spec(256, OFF_GQ // 256), spec(256, OFF_GK // 256), spec(512, OFF_GV // 512),
                  spec(512, OFF_GG // 512), spec(128, OFF_SMALL // 128),
                  _const_spec((128, 256)), _const_spec((1, 256)), _const_spec((1, 128)),
                  pl.BlockSpec(memory_space=pl.ANY)],
        out_specs=[spec(512, 1), _hist_spec(n_b, GLA_KEY, n_c)],
        out_shape=[jax.ShapeDtypeStruct(o_mix.shape, BF16),
                   jax.ShapeDtypeStruct((n_c, n_b * HEADS, GLA_KEY, HEAD_DIM), F32)],
        input_output_aliases={8: 0},
        scratch_shapes=[pltpu.VMEM((n_b * HEADS, GLA_KEY, HEAD_DIM), F32)],
        compiler_params=_cparams(("arbitrary",)),
    )(proj3, proj3, proj3, proj3, proj3, w2, b2, gn, o_mix)
    return o.reshape(n_b * t_len, 2 * 512), hist


def _gla_scan_bwd(proj, w2, b2, gn, hist, d_o, dsm_dn, d_proj, n_b, t_len):
    n_c = t_len // CHUNK
    rev = functools.partial(_chunk_spec, n_b, n_c=n_c, reverse=True)

    def body(q_ref, k_ref, v_ref, g_ref, sm_ref, w2_ref, b2_ref, gn_ref, hist_ref, do_ref, dsm_dn_ref, _,
             dp_ref, dw2_ref, db2_ref, dgn_ref, ds_ref):
        first = pl.program_id(0) == 0

        @pl.when(first)
        def _():
            ds_ref[...] = jnp.zeros_like(ds_ref)

        _, pull = jax.vjp(_gla_chains, _stack_chains(q_ref, n_b, GLA_KSL), _stack_chains(k_ref, n_b, GLA_KSL),
                          _stack_chains(v_ref, n_b, DN_SL), _stack_chains(g_ref, n_b, DN_SL),
                          _per_chain(sm_ref, n_b), hist_ref[...], *_gate_weights(w2_ref, b2_ref, n_b), gn_ref[...])
        dq, dk, dv, dg, dsm, ds_in, dw2, db2, dgn = pull((_stack_chains(do_ref, n_b, DN_SL), ds_ref[...]))
        _unstack_chains(dp_ref, dq, n_b, GLA_KSL, offset=OFF_GQ)
        _unstack_chains(dp_ref, dk, n_b, GLA_KSL, offset=OFF_GK)
        _unstack_chains(dp_ref, dv, n_b, DN_SL, offset=OFF_GV)
        _unstack_chains(dp_ref, dg, n_b, DN_SL, offset=OFF_GG)
        ds_ref[...] = ds_in
        for b, dsm_b in enumerate(_sum_heads(dsm, n_b)):
            dp_ref[b, :, OFF_SMALL:OFF_SMALL + 128] = (dsm_b + dsm_dn_ref[b]).astype(BF16)
            dp_ref[b, :, OFF_SMALL + 128:GLA_W] = jnp.zeros((CHUNK, GLA_W - OFF_SMALL - 128), BF16)
        for h, ks in enumerate(GLA_KSL):
            _acc(dw2_ref, sum(dw2[b * HEADS + h] for b in range(n_b)), first, at=(slice(None), ks))
            _acc(db2_ref, sum(db2[b * HEADS + h] for b in range(n_b)), first, at=(slice(None), ks))
        _acc(dgn_ref, dgn, first)

    proj3, do3 = proj.reshape(n_b, t_len, -1), d_o.reshape(n_b, t_len, -1)
    return pl.pallas_call(
        body, name="gla_scan_bwd", grid=(n_c,),
        in_specs=[rev(256, OFF_GQ // 256), rev(256, OFF_GK // 256), rev(512, OFF_GV // 512), rev(512, OFF_GG // 512),
                  rev(128, OFF_SMALL // 128),
                  _const_spec((128, 256)), _const_spec((1, 256)), _const_spec((1, 128)),
                  _hist_spec(n_b, GLA_KEY, n_c, reverse=True), rev(512, 1), rev(128, 0),
                  pl.BlockSpec(memory_space=pl.ANY)],
        out_specs=[rev(GLA_W, 0), _const_spec((128, 256)), _const_spec((1, 256)), _const_spec((1, 128))],
        out_shape=[jax.ShapeDtypeStruct(d_proj.shape, BF16), jax.ShapeDtypeStruct((128, 256), F32),
                   jax.ShapeDtypeStruct((1, 256), F32), jax.ShapeDtypeStruct((1, 128), F32)],
        input_output_aliases={11: 0},
        scratch_shapes=[pltpu.VMEM((n_b * HEADS, GLA_KEY, HEAD_DIM), F32)],
        compiler_params=_cparams(("arbitrary",)),
    )(proj3, proj3, proj3, proj3, proj3, w2, b2, gn, hist, do3, dsm_dn, d_proj)


W_IN_RUNS = ((0, 256, GLA_W), (256, 1536, OFF_Z + 512), (1536, 2048, OFF_Z), (2048, 2056, OFF_SMALL),
             (2056, 3592, 0), (3592, 3608, OFF_SMALL + 8))
W_IN_ROWS = 256


def _w_in_pieces(cols_per_chip):
    out = []
    for first, last, start in W_IN_RUNS:
        for j in range(N_CHIPS):
            a, b = max(first, cols_per_chip * j), min(last, cols_per_chip * (j + 1))
            if a < b:
                out.append((j, a - cols_per_chip * j, b - cols_per_chip * j, start + a - first))
    return out


def _w_in_to_padded(w4):
    _, n_r, n_c = w4.shape

    def body(i_ref, o_ref):
        o_ref[...] = jnp.zeros_like(o_ref)
        for j, a, b, p in _w_in_pieces(n_c):
            o_ref[:, p:p + b - a] = i_ref[j, :, a:b]

    return pl.pallas_call(
        body, name="w_in_to_padded", grid=(n_r // W_IN_ROWS,),
        in_specs=[pl.BlockSpec((N_CHIPS, W_IN_ROWS, n_c), lambda i: (0, i, 0))],
        out_specs=pl.BlockSpec((W_IN_ROWS, PROJ_W), lambda i: (i, 0)),
        out_shape=jax.ShapeDtypeStruct((n_r, PROJ_W), w4.dtype), compiler_params=_cparams(("parallel",)),
    )(w4)


def _w_in_to_chips(g, n_c):
    n_r = g.shape[0]

    def body(i_ref, o_ref):
        for j, a, b, p in _w_in_pieces(n_c):
            o_ref[j, :, a:b] = i_ref[:, p:p + b - a]

    return pl.pallas_call(
        body, name="w_in_to_chips", grid=(n_r // W_IN_ROWS,),
        in_specs=[pl.BlockSpec((W_IN_ROWS, PROJ_W), lambda i: (i, 0))],
        out_specs=pl.BlockSpec((N_CHIPS, W_IN_ROWS, n_c), lambda i: (0, i, 0)),
        out_shape=jax.ShapeDtypeStruct((N_CHIPS, n_r, n_c), g.dtype), compiler_params=_cparams(("parallel",)),
    )(g)


def _lane_vec(v, offset=0):
    return jnp.zeros((1, 128), F32).at[0, offset:offset + v.shape[0]].set(v)


def _local_step(x, tgt, mod, p, n_b, t_len, comm=None):
    row1 = lambda v: v.reshape(1, -1)
    a_log, dt_bias = _lane_vec(p["dn_a_log"]), _lane_vec(p["dn_dt_bias"])
    dn_gn, gla_gn = row1(p["dn_norm_g"]), row1(p["gla_norm_g"])
    w2 = jnp.zeros((128, 256), F32).at[8:8 + GATE_RANK].set(p["gla_w_gate2"])
    b2 = row1(p["gla_b_gate"])
    ln0_g, ln0_b, ln1_g, ln1_b, ln2_g, ln2_b = (row1(p[k]) for k in ("ln0_g", "ln0_b", "ln1_g", "ln1_b", "ln2_g", "ln2_b"))
    conv_b = row1(p["ffn_conv_b"])

    x0, h1 = _ln0_fwd(x, ln0_g, ln0_b, mod, n_b, t_len)
    proj = _mm(h1, p["w_in_p"], name="mm_proj")
    qkv = _dn_pre_fwd(proj, p["dn_conv"], n_b, t_len)
    o_half, hist_dn, landed = _dn_scan_fwd(qkv, proj, a_log, dt_bias, dn_gn, n_b, t_len,
                                           rider=comm.fwd_rider() if comm else None)
    if comm:
        p = {**p, **comm.weights_from(landed)}
    o_mix, hist_gla = _gla_scan_fwd(proj, w2, b2, gla_gn, o_half, n_b, t_len)
    y = _mm(o_mix, p["w_o"], name="mm_wo")
    x1, h2 = _ln1_fwd(x0, y, ln1_g, ln1_b, mod, n_b, t_len)
    up = _mm(h2, p["w_up"], name="mm_up")
    act = _ffn_act_fwd(up, p["ffn_conv"], conv_b, n_b, t_len)
    y2 = _mm(act, p["w_down"], name="mm_down")

    loss, dx1, dy2, g_ln2_g, g_ln2_b, dgt_f = _ln2_loss_bwd(x1, y2, ln2_g, ln2_b, mod, tgt, n_b, t_len)
    g_w_down = _mm(act, dy2, ta=True, name="mm_g_down")
    d_act = _mm(dy2, p["w_down"], tb=True, name="mm_d_act")
    d_up, g_ffn_conv, g_conv_b = _ffn_act_bwd(up, p["ffn_conv"], conv_b, d_act, n_b, t_len)
    g_w_up = _mm(h2, d_up, ta=True, out_slabs=N_CHIPS, name="mm_g_up")
    if comm:
        dh2, from_sibling = _mm(d_up, p["w_up"], tb=True, name="mm_d_h2", rider=comm.ffn_pair_rider(g_w_up, g_w_down))
    else:
        dh2 = _mm(d_up, p["w_up"], tb=True, name="mm_d_h2")
    dx0, dy, g_ln1_g, g_ln1_b, dmod_1 = _ln1_bwd(x0, y, ln1_g, ln1_b, mod, dx1, dh2, n_b, t_len)
    g_w_o = _mm(o_mix, dy, ta=True, name="mm_g_wo")
    d_o = _mm(dy, p["w_o"], tb=True, name="mm_d_o")
    dqkv, d_proj, dsm_dn, g_a_log, g_dt_bias, g_dn_gn, ffn_from_chips = _dn_scan_bwd(
        qkv, proj, a_log, dt_bias, dn_gn, hist_dn, d_o, n_b, t_len,
        rider=comm.ffn_chips_rider(from_sibling) if comm else None)
    d_proj, g_w2, g_b2, g_gla_gn = _gla_scan_bwd(proj, w2, b2, gla_gn, hist_gla, d_o, dsm_dn, d_proj, n_b, t_len)
    d_proj, g_dn_conv = _dn_pre_bwd(proj, p["dn_conv"], dqkv, d_proj.reshape(n_b * t_len, PROJ_W), n_b, t_len)
    g_w_in_p = _mm(h1, d_proj, ta=True, name="mm_g_win")
    if comm:
        dh1, tail_from_chips = _mm(d_proj, p["w_in_p"], tb=True, name="mm_d_h1",
                                   rider=comm.tail_chips_rider(g_w_in_p, g_w_o))
        from_chips = (ffn_from_chips, tail_from_chips)
    else:
        dh1, from_chips = _mm(d_proj, p["w_in_p"], tb=True, name="mm_d_h1"), None
    grad_x, g_ln0_g, g_ln0_b, dmod_0 = _ln0_bwd(x, ln0_g, ln0_b, mod, dx0, dh1, n_b, t_len)

    dmod = jnp.concatenate([dmod_0, dmod_1[:, 0:1], dmod_1[:, 1:3], dgt_f], axis=1)
    grads = {
        "ln0_g": g_ln0_g[0], "ln0_b": g_ln0_b[0], "w_in_p": g_w_in_p, "dn_conv": g_dn_conv,
        "dn_a_log": g_a_log[0, 0:HEADS], "dn_dt_bias": g_dt_bias[0, 0:HEADS], "dn_norm_g": g_dn_gn[0],
        "gla_w_gate2": g_w2[8:8 + GATE_RANK], "gla_b_gate": g_b2[0], "gla_norm_g": g_gla_gn[0],
        "w_o": g_w_o, "ln1_g": g_ln1_g[0], "ln1_b": g_ln1_b[0], "w_up": g_w_up,
        "ffn_conv": jnp.concatenate([g_ffn_conv[0], g_ffn_conv[1]], axis=1),
        "ffn_conv_b": jnp.concatenate([g_conv_b[0, 0], g_conv_b[1, 0]]), "w_down": g_w_down,
        "ln2_g": g_ln2_g[0], "ln2_b": g_ln2_b[0],
    }
    return loss, grad_x, grads, dmod, from_chips


def _ada_fwd(c_all, w_shard, b_shard):
    n_all, n_col = c_all.shape[0], w_shard.shape[1]
    tn = 512

    def body(c_ref, w_ref, b_ref, cond_ref, mod_ref):
        cond = _silu(c_ref[...])
        cond_ref[...] = cond
        mod_ref[...] = jnp.dot(cond.astype(BF16), w_ref[...].astype(BF16), preferred_element_type=F32) + b_ref[...]

    return pl.pallas_call(
        body, name="ada_fwd", grid=(n_col // tn,),
        in_specs=[pl.BlockSpec((n_all, D_MODEL), lambda j: (0, 0)), pl.BlockSpec((D_MODEL, tn), lambda j: (0, j)),
                  pl.BlockSpec((1, tn), lambda j: (0, j))],
        out_specs=[pl.BlockSpec((n_all, D_MODEL), lambda j: (0, 0)), pl.BlockSpec((n_all, tn), lambda j: (0, j))],
        out_shape=[jax.ShapeDtypeStruct((n_all, D_MODEL), F32), jax.ShapeDtypeStruct((n_all, n_col), F32)],
        compiler_params=_cparams(("arbitrary",)),
    )(c_all, w_shard, b_shard)


def _col_sum(a):
    def body(a_ref, o_ref):
        o_ref[...] = jnp.sum(a_ref[...], 0, keepdims=True)

    return pl.pallas_call(body, name="col_sum", out_shape=jax.ShapeDtypeStruct((1, a.shape[1]), F32))(a)


def _adamw_math(w, grad, m, v):
    new_m = ADAM_B1 * m + (1.0 - ADAM_B1) * grad
    new_v = ADAM_B2 * v + (1.0 - ADAM_B2) * (grad * grad)
    m_hat = new_m / (1.0 - ADAM_B1 ** ADAM_STEP)
    v_hat = new_v / (1.0 - ADAM_B2 ** ADAM_STEP)
    return -ADAM_LR * (m_hat / (jnp.sqrt(v_hat) + ADAM_EPS) + ADAM_WD * w), new_m, new_v


def _adamw_many(ws, gs, ms, vs):
    n = len(ws)

    def body(*refs):
        for i in range(n):
            w_ref, g_ref, m_ref, v_ref = (refs[k * n + i] for k in range(4))
            d_ref, nm_ref, nv_ref = (refs[(4 + k) * n + i] for k in range(3))
            d_ref[...], nm_ref[...], nv_ref[...] = _adamw_math(w_ref[...], g_ref[...], m_ref[...], v_ref[...])

    outs = pl.pallas_call(
        body, name="adamw_small", out_shape=[jax.ShapeDtypeStruct(w.shape, F32) for w in ws] * 3,
    )(*ws, *gs, *ms, *vs)
    return outs[:n], outs[n:2 * n], outs[2 * n:]


def _adamw(w, g, m, v, name):
    n_r, n_c = w.shape
    if n_r % 8 == 0:
        tr = _pick(n_r, (256, 64, 32, 16, 8))
        grid, blk = (n_r // tr,), pl.BlockSpec((tr, n_c), lambda i: (i, 0))
    else:
        tc = _pick(n_c, (256, 128))
        grid, blk = (n_c // tc,), pl.BlockSpec((n_r, tc), lambda i: (0, i))

    def body(w_ref, g_ref, m_ref, v_ref, go_ref, d_ref, nm_ref, nv_ref):
        grad = g_ref[...]
        go_ref[...] = grad
        d_ref[...], nm_ref[...], nv_ref[...] = _adamw_math(w_ref[...], grad, m_ref[...], v_ref[...])

    out = jax.ShapeDtypeStruct(w.shape, F32)
    return pl.pallas_call(
        body, name=name, grid=grid, in_specs=[blk] * 4, out_specs=[blk] * 4, out_shape=[out] * 4,
        compiler_params=_cparams(("parallel",)),
    )(w, g, m, v)


HBM_SPEC = pl.BlockSpec(memory_space=pltpu.HBM)
VMEM_SPEC = pl.BlockSpec(memory_space=pltpu.VMEM)
CHIP_FLIPS = ((1, 0), (0, 1), (1, 1))


def _place():
    return lax.axis_index("x"), lax.axis_index("y"), lax.axis_index("c")


def _flip(v, f):
    return 1 - v if f else v


def _all_gather8(slab, name, rider=None):
    n_r, n_w = slab.shape
    r_inputs, r_in_specs, r_out_specs, r_sems, split = _with_rider(rider, 1, 2, 3)

    def body(*refs):
        (x_ref, o_ref, s_ref, send_sems, recv_sems, local_sem), parts = split(refs)
        if rider is not None:
            rider.first(*parts)
        x, y, c = _place()
        me = 4 * x + 2 * y + c
        mine = pltpu.make_async_copy(x_ref, o_ref.at[me], local_sem)
        mine.start()
        peers = [(_flip(x, k & 4), _flip(y, k & 2), _flip(c, k & 1)) for k in range(1, N_DEV)]
        sends = []
        for k, peer in enumerate(peers):
            cp = pltpu.make_async_remote_copy(src_ref=x_ref, dst_ref=o_ref.at[me], send_sem=send_sems.at[k],
                                              recv_sem=recv_sems.at[k], device_id=peer, device_id_type=MESH)
            cp.start()
            sends.append(cp)
        for k, (px, py, pc) in enumerate(peers):
            pltpu.make_async_remote_copy(src_ref=x_ref, dst_ref=o_ref.at[4 * px + 2 * py + pc],
                                         send_sem=send_sems.at[k], recv_sem=recv_sems.at[k],
                                         device_id=(px, py, pc), device_id_type=MESH).wait_recv()
        for cp in sends:
            cp.wait_send()
        mine.wait()
        total = o_ref[0]
        for d in range(1, N_DEV):
            total = total + o_ref[d]
        s_ref[...] = total
        if rider is not None:
            rider.last(*parts)

    gathered, total, *rider_outs = pl.pallas_call(
        body, name=name, in_specs=[VMEM_SPEC] + r_in_specs, out_specs=[VMEM_SPEC, VMEM_SPEC] + r_out_specs,
        out_shape=[jax.ShapeDtypeStruct((N_DEV, n_r, n_w), F32), jax.ShapeDtypeStruct((n_r, n_w), F32)]
        + (list(rider.out_shapes) if rider else []),
        scratch_shapes=[pltpu.SemaphoreType.DMA((N_DEV - 1,)), pltpu.SemaphoreType.DMA((N_DEV - 1,)),
                        pltpu.SemaphoreType.DMA] + r_sems,
    )(slab, *r_inputs)
    return (gathered, total, rider_outs) if rider else (gathered, total)


SEQ_ROWS = 8


def _prologue(slab, w_ada_shard, b_shard, rider):
    n_r, n_w = slab.shape
    n_col = w_ada_shard.shape[1]
    r_inputs, r_in_specs, r_out_specs, r_sems, split = _with_rider(rider, 3, 3, 6)

    def body(*refs):
        (x_ref, w_ref, b_ref, g_ref, cond_ref, modr_ref, modp_ref, s1, r1, s2, r2, lsem), parts = split(refs)
        rider.first(*parts)
        x, y, c = _place()
        me = 4 * x + 2 * y + c
        peers = [(_flip(x, k & 4), _flip(y, k & 2), _flip(c, k & 1)) for k in range(1, N_DEV)]
        ids = [4 * px + 2 * py + pc for px, py, pc in peers]

        def exchange(src_of, dst, send_sems, recv_sems, own_sem):
            mine = pltpu.make_async_copy(src_of(me), dst.at[me], own_sem)
            mine.start()
            sends = [pltpu.make_async_remote_copy(src_ref=src_of(ids[k]), dst_ref=dst.at[me], send_sem=send_sems.at[k],
                                                  recv_sem=recv_sems.at[k], device_id=peers[k], device_id_type=MESH)
                     for k in range(N_DEV - 1)]
            for cp in sends:
                cp.start()
            for k in range(N_DEV - 1):
                pltpu.make_async_remote_copy(src_ref=src_of(ids[k]), dst_ref=dst.at[ids[k]], send_sem=send_sems.at[k],
                                             recv_sem=recv_sems.at[k], device_id=peers[k],
                                             device_id_type=MESH).wait_recv()
            for cp in sends:
                cp.wait_send()
            mine.wait()

        exchange(lambda d: x_ref, g_ref, s1, r1, lsem.at[0])
        cond = _silu(g_ref[:, 0:SEQ_ROWS, :].reshape(N_DEV * SEQ_ROWS, n_w))
        cond_ref[...] = cond
        modp_ref[...] = jnp.dot(cond.astype(BF16), w_ref[...].astype(BF16), preferred_element_type=F32) + b_ref[...]
        exchange(lambda d: modp_ref.at[pl.ds(pl.multiple_of(d * SEQ_ROWS, SEQ_ROWS), SEQ_ROWS)], modr_ref, s2, r2,
                 lsem.at[1])
        rider.last(*parts)

    sem7 = pltpu.SemaphoreType.DMA((N_DEV - 1,))
    gathered, cond, mod_recv, *rider_outs = pl.pallas_call(
        body, name="prologue", in_specs=[VMEM_SPEC] * 3 + r_in_specs, out_specs=[VMEM_SPEC] * 3 + r_out_specs,
        out_shape=[jax.ShapeDtypeStruct((N_DEV, n_r, n_w), F32), jax.ShapeDtypeStruct((N_DEV * SEQ_ROWS, n_w), F32),
                   jax.ShapeDtypeStruct((N_DEV, SEQ_ROWS, n_col), F32)] + list(rider.out_shapes),
        scratch_shapes=[pltpu.VMEM((N_DEV * SEQ_ROWS, n_col), F32), sem7, sem7, sem7, sem7,
                        pltpu.SemaphoreType.DMA((2,))] + r_sems,
        compiler_params=pltpu.CompilerParams(vmem_limit_bytes=VMEM_LIMIT),
    )(slab, w_ada_shard, b_shard, *r_inputs)
    return gathered, cond, mod_recv, rider_outs


def _gather_rider(shards):
    n_a = len(shards)

    def plan(ins, outs, sems):
        send_sems, recv_sems = sems
        x, y, c = _place()
        chips = [(_flip(x, fx), _flip(y, fy)) for fx, fy in CHIP_FLIPS]

        def copy(k, slot, chip_of_block, half, to, src=None):
            dst = outs[k].at[chip_of_block, half]
            return pltpu.make_async_remote_copy(src_ref=dst if src is None else src, dst_ref=dst,
                                                send_sem=send_sems.at[k * 6 + slot], recv_sem=recv_sems.at[k * 6 + slot],
                                                device_id=to, device_id_type=MESH)

        first = [copy(k, r, 2 * x + y, c, (*chips[r], c), src=ins[k].at[c]) for k in range(n_a) for r in range(3)]
        return copy, chips, first, (x, y, c)

    def first_step(ins, outs, sems):
        for cp in plan(ins, outs, sems)[2]:
            cp.start()

    def last_step(ins, outs, sems):
        copy, chips, first, (x, y, c) = plan(ins, outs, sems)
        passed = []
        for k in range(n_a):
            for r, (px, py) in enumerate(chips):
                copy(k, r, 2 * px + py, c, (x, y, c)).wait_recv()
                fwd = copy(k, 3 + r, 2 * px + py, c, (x, y, 1 - c))
                fwd.start()
                passed.append(fwd)
        for k in range(n_a):
            for r, (px, py) in enumerate(chips):
                copy(k, 3 + r, 2 * px + py, 1 - c, (x, y, c)).wait_recv()
        for cp in first + passed:
            cp.wait_send()

    return Rider(shards, [jax.ShapeDtypeStruct((N_CHIPS,) + s.shape, s.dtype) for s in shards],
                 [pltpu.SemaphoreType.DMA((6 * n_a,)), pltpu.SemaphoreType.DMA((6 * n_a,))], first_step, last_step)


def _place_own(gathered, shard, chip, name):
    _, _, n_h, n_c = gathered.shape
    th = _pick(n_h, (256, 176, 128))

    def body(sel_ref, s_ref, _, o_ref):
        o_ref[...] = s_ref[...]

    grid_spec = pltpu.PrefetchScalarGridSpec(
        num_scalar_prefetch=1, grid=(2, n_h // th),
        in_specs=[pl.BlockSpec((None, th, n_c), lambda hf, i, sel: (hf, i, 0)), pl.BlockSpec(memory_space=pl.ANY)],
        out_specs=pl.BlockSpec((None, None, th, n_c), lambda hf, i, sel: (sel[0], hf, i, 0)))
    return pl.pallas_call(
        body, name=name, grid_spec=grid_spec, out_shape=jax.ShapeDtypeStruct(gathered.shape, gathered.dtype),
        input_output_aliases={2: 0}, compiler_params=_cparams(("parallel", "parallel")),
    )(chip.reshape(1), shard, gathered)


def _pair_rider(parts):
    n_a = len(parts)

    def plan(ins, outs, sems):
        send_sems, recv_sems = sems
        x, y, c = _place()
        return [pltpu.make_async_remote_copy(src_ref=ins[k].at[:, 1 - c], dst_ref=outs[k], send_sem=send_sems.at[k],
                                             recv_sem=recv_sems.at[k], device_id=(x, y, 1 - c), device_id_type=MESH)
                for k in range(n_a)]

    def first_step(ins, outs, sems):
        for cp in plan(ins, outs, sems):
            cp.start()

    def last_step(ins, outs, sems):
        for cp in plan(ins, outs, sems):
            cp.wait()

    return Rider(parts, [jax.ShapeDtypeStruct((N_CHIPS,) + p.shape[2:], F32) for p in parts],
                 [pltpu.SemaphoreType.DMA((n_a,)), pltpu.SemaphoreType.DMA((n_a,))], first_step, last_step)


def _alone(rider, name):
    n_a = len(rider.inputs)

    def body(*refs):
        parts = (refs[:n_a], refs[n_a:2 * n_a], refs[2 * n_a:])
        rider.first(*parts)
        rider.last(*parts)

    return pl.pallas_call(
        body, name=name, in_specs=[HBM_SPEC] * n_a, out_specs=[HBM_SPEC] * n_a,
        out_shape=rider.out_shapes, scratch_shapes=rider.sems,
    )(*rider.inputs)


def _chips_rider(sums):
    n_a = len(sums)

    def plan(ins, outs, sems):
        send_sems, recv_sems = sems
        x, y, c = _place()
        cps = []
        for k in range(n_a):
            for r, (fx, fy) in enumerate(CHIP_FLIPS):
                px, py = _flip(x, fx), _flip(y, fy)
                cps.append(pltpu.make_async_remote_copy(
                    src_ref=ins[k].at[2 * px + py], dst_ref=outs[k].at[r], send_sem=send_sems.at[3 * k + r],
                    recv_sem=recv_sems.at[3 * k + r], device_id=(px, py, c), device_id_type=MESH))
        return cps

    def first_step(ins, outs, sems):
        for cp in plan(ins, outs, sems):
            cp.start()

    def last_step(ins, outs, sems):
        for cp in plan(ins, outs, sems):
            cp.wait()

    return Rider(sums, [jax.ShapeDtypeStruct((3,) + s.shape[1:], s.dtype) for s in sums],
                 [pltpu.SemaphoreType.DMA((3 * n_a,)), pltpu.SemaphoreType.DMA((3 * n_a,))], first_step, last_step)


def _rs_share(bufs):
    n_a = len(bufs)

    def body(*refs):
        ins, outs = refs[:n_a], refs[n_a:2 * n_a]
        send_sems, recv_sems = refs[2 * n_a:]
        x, y, c = _place()
        sends = [pltpu.make_async_remote_copy(src_ref=ins[k].at[c], dst_ref=outs[k].at[c], send_sem=send_sems.at[k],
                                              recv_sem=recv_sems.at[k], device_id=(x, y, 1 - c), device_id_type=MESH)
                 for k in range(n_a)]
        for cp in sends:
            cp.start()
        for k in range(n_a):
            pltpu.make_async_remote_copy(src_ref=ins[k].at[c], dst_ref=outs[k].at[1 - c], send_sem=send_sems.at[k],
                                         recv_sem=recv_sems.at[k], device_id=(x, y, 1 - c),
                                         device_id_type=MESH).wait_recv()
        for cp in sends:
            cp.wait_send()

    return pl.pallas_call(
        body, name="rs_share", in_specs=[HBM_SPEC] * n_a, out_specs=[HBM_SPEC] * n_a,
        out_shape=[jax.ShapeDtypeStruct(s.shape, F32) for s in bufs],
        input_output_aliases={k: k for k in range(n_a)},
        scratch_shapes=[pltpu.SemaphoreType.DMA((n_a,)), pltpu.SemaphoreType.DMA((n_a,))],
    )(*bufs)


def _pair_add(part, recv, core, name):
    _, _, n_h, n_c = part.shape
    th = _pick(n_h, (256, 176, 128))

    def body(sel_ref, p_ref, r_ref, o_ref):
        o_ref[...] = (p_ref[...] + r_ref[...]).astype(BF16)

    grid_spec = pltpu.PrefetchScalarGridSpec(
        num_scalar_prefetch=1, grid=(N_CHIPS, n_h // th),
        in_specs=[pl.BlockSpec((None, None, th, n_c), lambda j, i, sel: (j, sel[0], i, 0)),
                  pl.BlockSpec((None, th, n_c), lambda j, i, sel: (j, i, 0))],
        out_specs=pl.BlockSpec((None, th, n_c), lambda j, i, sel: (j, i, 0)))
    return pl.pallas_call(
        body, name=name, grid_spec=grid_spec, out_shape=jax.ShapeDtypeStruct(recv.shape, BF16),
        compiler_params=_cparams(("parallel", "parallel")),
    )(core.reshape(1), part, recv)


def _chip_add(sums, recv, chip, core, name):
    _, n_h, n_c = sums.shape
    th = _pick(n_h, (256, 176, 128))

    def body(sel_ref, s_ref, r_ref, o_ref):
        total = s_ref[...].astype(F32)
        for r in range(3):
            total = total + r_ref[r].astype(F32)
        o_ref[...] = total

    grid_spec = pltpu.PrefetchScalarGridSpec(
        num_scalar_prefetch=1, grid=(n_h // th,),
        in_specs=[pl.BlockSpec((None, th, n_c), lambda i, sel: (sel[0], i, 0)),
                  pl.BlockSpec((3, th, n_c), lambda i, sel: (0, i, 0))],
        out_specs=pl.BlockSpec((None, th, n_c), lambda i, sel: (sel[1], i, 0)))
    return pl.pallas_call(
        body, name=name, grid_spec=grid_spec, out_shape=jax.ShapeDtypeStruct((2, n_h, n_c), F32),
        compiler_params=_cparams(("parallel",)),
    )(jnp.stack([chip, core]), sums, recv)


def _row_halves(a):
    return a.reshape(N_CHIPS, 2, -1, a.shape[-1])


class StepComm:
    REST = ("w_o", "w_up", "w_down")

    def __init__(self, core, chip, rest_shards, in_cols):
        self.core, self.chip, self.shards, self.in_cols = core, chip, rest_shards, in_cols

    def fwd_rider(self):
        return _gather_rider(self.shards)

    def weights_from(self, landed):
        g_o, g_up, g_down = (_place_own(g, s, self.chip, "place_own_" + n)
                             for g, s, n in zip(landed, self.shards, self.REST))
        return {"w_o": g_o.reshape(-1, D_MODEL), "w_up": g_up.reshape(N_CHIPS, -1, g_up.shape[-1]),
                "w_down": g_down.reshape(-1, D_MODEL)}

    def _add_pairs(self, parts, from_sibling, names):
        return [_pair_add(p, r, self.core, "pair_add_" + n) for p, r, n in zip(parts, from_sibling, names)]

    def ffn_pair_rider(self, g_w_up, g_w_down):
        self.ffn_parts = [_row_halves(g_w_up), _row_halves(g_w_down)]
        return _pair_rider(self.ffn_parts)

    def ffn_chips_rider(self, from_sibling):
        self.ffn_sums = self._add_pairs(self.ffn_parts, from_sibling, ("w_up", "w_down"))
        return _chips_rider(self.ffn_sums)

    def tail_chips_rider(self, g_w_in_p, g_w_o):
        parts = [_row_halves(_w_in_to_chips(g_w_in_p, self.in_cols)), _row_halves(g_w_o)]
        self.tail_sums = self._add_pairs(parts, _alone(_pair_rider(parts), "rs_pair_tail"), ("w_in", "w_o"))
        return _chips_rider(self.tail_sums)

    def finish(self, ffn_from_chips, tail_from_chips):
        halves = [_chip_add(s, r, self.chip, self.core, "chip_add_" + n)
                  for s, r, n in zip(self.tail_sums + self.ffn_sums, list(tail_from_chips) + list(ffn_from_chips),
                                     ("w_in", "w_o", "w_up", "w_down"))]
        return [f.reshape(-1, f.shape[-1]) for f in _rs_share(halves)]


SLAB_W = 1024


def _pack(arrays, rows):
    flat = jnp.concatenate([a.reshape(-1).astype(F32) for a in arrays])
    return jnp.pad(flat, (0, rows * SLAB_W - flat.shape[0])).reshape(rows, SLAB_W)


def _unpack(flat, shapes):
    out, off = [], 0
    for s in shapes:
        n = 1
        for d in s:
            n *= d
        out.append(flat[off:off + n].reshape(s))
        off += n
    return out


def _rows_for(arrays_or_shapes):
    n = 0
    for a in arrays_or_shapes:
        s = a if isinstance(a, tuple) else a.shape
        k = 1
        for d in s:
            k *= d
        n += k
    return -(-n // (8 * SLAB_W)) * 8


def kernel(x, c, ln0_g, ln0_b, w_ada, b_ada, w_in, dn_conv, dn_a_log, dn_dt_bias, dn_norm_g, gla_w_gate2, gla_b_gate, gla_norm_g, w_o, ln1_g, ln1_b, ffn_w_up, ffn_conv, ffn_conv_b, ffn_w_down, ln2_g, ln2_b, loss_target, m_ln0_g, m_ln0_b, m_w_ada, m_b_ada, m_w_in, m_dn_conv, m_dn_a_log, m_dn_dt_bias, m_dn_norm_g, m_gla_w_gate2, m_gla_b_gate, m_gla_norm_g, m_w_o, m_ln1_g, m_ln1_b, m_ffn_w_up, m_ffn_conv, m_ffn_conv_b, m_ffn_w_down, m_ln2_g, m_ln2_b, v_ln0_g, v_ln0_b, v_w_ada, v_b_ada, v_w_in, v_dn_conv, v_dn_a_log, v_dn_dt_bias, v_dn_norm_g, v_gla_w_gate2, v_gla_b_gate, v_gla_norm_g, v_w_o, v_ln1_g, v_ln1_b, v_ffn_w_up, v_ffn_conv, v_ffn_conv_b, v_ffn_w_down, v_ln2_g, v_ln2_b):
    n_b, t_len, _ = x.shape
    xi, yi, ci = _place()
    chip = (2 * xi + yi).astype(jnp.int32)
    core = ci.astype(jnp.int32)
    me = 2 * chip + core
    n_all = N_DEV * n_b
    ada_cols = w_ada.shape[2]

    halves = lambda a: a.astype(BF16).reshape(2, a.shape[0] // 2, a.shape[1])
    w_in_halves = halves(w_in[0])
    sharded_small = [dn_conv[0], gla_w_gate2[0], ffn_conv[0]]
    slab = jnp.concatenate([_pack([c], SEQ_ROWS), _pack(sharded_small, _rows_for(sharded_small))], axis=0)
    b_ada_shard = lax.dynamic_slice(b_ada, (0, chip * ada_cols), (1, ada_cols))
    gathered, cond_pad, mod_recv, (g_in,) = _prologue(slab, w_ada[0], b_ada_shard, _gather_rider([w_in_halves]))
    g_in = _place_own(g_in, w_in_halves, chip, "place_own_w_in")
    cond_all = cond_pad.reshape(N_DEV, SEQ_ROWS, D_MODEL)[:, :n_b].reshape(n_all, D_MODEL)
    by_chip = gathered.reshape(N_DEV, -1)[0::2]
    full, off = [], SEQ_ROWS * SLAB_W
    for a in sharded_small:
        blocks = by_chip[:, off:off + a.size].reshape(N_CHIPS, *a.shape)
        full.append(blocks.transpose(1, 0, 2).reshape(a.shape[0], N_CHIPS * a.shape[1]))
        off += a.size
    dn_conv_f, gate2_f, ffn_conv_f = full
    mod = mod_recv[0::2, :n_b].transpose(1, 0, 2).reshape(n_b, 6, D_MODEL)

    comm = StepComm(core, chip, [halves(w_o[0]), halves(ffn_w_up[0]), halves(ffn_w_down[0])], w_in.shape[2])
    params = {
        "w_in_p": _w_in_to_padded(g_in.reshape(N_CHIPS, -1, g_in.shape[-1])),
        "dn_conv": dn_conv_f, "dn_a_log": dn_a_log[0], "dn_dt_bias": dn_dt_bias[0], "dn_norm_g": dn_norm_g[0],
        "gla_w_gate2": gate2_f, "gla_b_gate": gla_b_gate[0], "gla_norm_g": gla_norm_g[0],
        "ln0_g": ln0_g, "ln0_b": ln0_b, "ln1_g": ln1_g[0], "ln1_b": ln1_b[0], "ln2_g": ln2_g[0], "ln2_b": ln2_b[0],
        "ffn_conv": ffn_conv_f, "ffn_conv_b": ffn_conv_b[0],
    }

    loss_row, grad_x, gp, dmod, from_chips = _local_step(
        x.reshape(n_b * t_len, D_MODEL), loss_target.reshape(n_b * t_len, D_MODEL), mod, params, n_b, t_len, comm)
    names = ["ln0_g", "ln0_b", "w_ada", "b_ada", "w_in", "dn_conv", "dn_a_log", "dn_dt_bias", "dn_norm_g",
             "gla_w_gate2", "gla_b_gate", "gla_norm_g", "w_o", "ln1_g", "ln1_b", "ffn_w_up", "ffn_conv", "ffn_conv_b",
             "ffn_w_down", "ln2_g", "ln2_b"]
    weights = dict(zip(names, [ln0_g, ln0_b, w_ada, b_ada, w_in, dn_conv, dn_a_log, dn_dt_bias, dn_norm_g, gla_w_gate2,
                               gla_b_gate, gla_norm_g, w_o, ln1_g, ln1_b, ffn_w_up, ffn_conv, ffn_conv_b, ffn_w_down,
                               ln2_g, ln2_b]))
    m_in = dict(zip(names, [m_ln0_g, m_ln0_b, m_w_ada, m_b_ada, m_w_in, m_dn_conv, m_dn_a_log, m_dn_dt_bias,
                            m_dn_norm_g, m_gla_w_gate2, m_gla_b_gate, m_gla_norm_g, m_w_o, m_ln1_g, m_ln1_b,
                            m_ffn_w_up, m_ffn_conv, m_ffn_conv_b, m_ffn_w_down, m_ln2_g, m_ln2_b]))
    v_in = dict(zip(names, [v_ln0_g, v_ln0_b, v_w_ada, v_b_ada, v_w_in, v_dn_conv, v_dn_a_log, v_dn_dt_bias,
                            v_dn_norm_g, v_gla_w_gate2, v_gla_b_gate, v_gla_norm_g, v_w_o, v_ln1_g, v_ln1_b,
                            v_ffn_w_up, v_ffn_conv, v_ffn_conv_b, v_ffn_w_down, v_ln2_g, v_ln2_b]))
    grads, delta, new_m, new_v = {}, {}, {}, {}

    def adamw_big(n, grad):
        view = (lambda a: a.T) if n == "w_in" else (lambda a: a)
        outs = _adamw(view(weights[n][0]), view(grad), view(m_in[n][0]), view(v_in[n][0]), "adamw_" + n)
        grads[n], delta[n], new_m[n], new_v[n] = (view(a)[None] for a in outs)

    g_w_in, g_w_o, g_w_up, g_w_down = comm.finish(*from_chips)

    summed_names = ["loss", "ln0_g", "ln0_b", "dn_conv", "dn_a_log", "dn_dt_bias", "dn_norm_g", "gla_w_gate2",
                    "gla_b_gate", "gla_norm_g", "ln1_g", "ln1_b", "ffn_conv", "ffn_conv_b", "ln2_g", "ln2_b"]
    summed_parts = [loss_row[0, 0:1]] + [gp[n] for n in summed_names[1:]]
    sum_rows = _rows_for(summed_parts)
    slab = jnp.concatenate([_pack(summed_parts, sum_rows), _pack([dmod], _rows_for([dmod]))], axis=0)
    gathered, total = _all_gather8(slab, "reduce_small")
    small_g = dict(zip(summed_names, _unpack(total.reshape(-1), [a.shape for a in summed_parts])))
    loss = small_g["loss"][0]
    dmod_rows = n_b * 6 * D_MODEL // SLAB_W
    dmod_all = gathered[:, sum_rows:sum_rows + dmod_rows, :].reshape(n_all, 6 * D_MODEL)
    for n, grad in (("ffn_w_up", g_w_up), ("ffn_w_down", g_w_down), ("w_o", g_w_o), ("w_in", g_w_in)):
        adamw_big(n, grad)

    g_b_ada = _col_sum(dmod_all)
    dmod_cols = lax.dynamic_slice(dmod_all, (0, chip * ada_cols), (n_all, ada_cols))
    adamw_big("w_ada", _mm(cond_all, dmod_cols, ta=True, name="mm_g_ada"))

    col_block = lambda a: lax.dynamic_slice(a, (0, chip * (a.shape[1] // N_CHIPS)), (a.shape[0], a.shape[1] // N_CHIPS))
    grads.update({
        "ln0_g": small_g["ln0_g"], "ln0_b": small_g["ln0_b"], "b_ada": g_b_ada,
        "dn_conv": col_block(small_g["dn_conv"])[None], "dn_a_log": small_g["dn_a_log"][None],
        "dn_dt_bias": small_g["dn_dt_bias"][None], "dn_norm_g": small_g["dn_norm_g"][None],
        "gla_w_gate2": col_block(small_g["gla_w_gate2"])[None], "gla_b_gate": small_g["gla_b_gate"][None],
        "gla_norm_g": small_g["gla_norm_g"][None], "ln1_g": small_g["ln1_g"][None],
        "ln1_b": small_g["ln1_b"][None], "ffn_conv": col_block(small_g["ffn_conv"])[None],
        "ffn_conv_b": small_g["ffn_conv_b"][None], "ln2_g": small_g["ln2_g"][None], "ln2_b": small_g["ln2_b"][None],
    })
    small = [n for n in names if n not in delta]
    d_s, m_s, v_s = _adamw_many([weights[n] for n in small], [grads[n] for n in small],
                                [m_in[n] for n in small], [v_in[n] for n in small])
    for out, vals in ((delta, d_s), (new_m, m_s), (new_v, v_s)):
        out.update(zip(small, vals))

    return (loss, grad_x.reshape(x.shape), *[grads[n] for n in names], *[delta[n] for n in names],
            *[new_m[n] for n in names], *[new_v[n] for n in names])
```

```python
import functools

import jax
import jax.numpy as jnp
from jax import lax
from jax.experimental import pallas as pl
from jax.experimental.pallas import tpu as pltpu

F32 = jnp.float32
BF16 = jnp.bfloat16
MESH = pl.DeviceIdType.MESH

D_MODEL = 1024
HEADS = 4
HEAD_DIM = 128
GLA_KEY = 64
GATE_RANK = 16
CHUNK = 64
D_FF = 2816
ALPHA = 2.0 ** 0.25
EPS = 1e-6
N_CHIPS = 4
N_DEV = 8

PROJ_W = 3840
OFF_GQ, OFF_GK, OFF_GV, OFF_GG, OFF_SMALL, GLA_W = 0, 256, 512, 1024, 1536, 1792
OFF_Z = 2048
W_IN_COLS = 3608


def _qkv_block(j):
    return jnp.where(j < 2, GLA_W // 128 + j, (OFF_Z + 512) // 128 - 2 + j)

ADAM_LR, ADAM_B1, ADAM_B2, ADAM_EPS, ADAM_WD, ADAM_STEP = 0.001, 0.9, 0.999, 1e-08, 0.01, 10

VMEM_LIMIT = 56 * 1024 * 1024
ROW_TILE = 512


def _cparams(sem):
    return pltpu.CompilerParams(dimension_semantics=sem, vmem_limit_bytes=VMEM_LIMIT)


def _pick(n, prefs):
    for p in prefs:
        if n % p == 0:
            return p
    return n


def _mm(a, b, *, ta=False, tb=False, out_slabs=1, out_dtype=F32, name, rider=None):
    a_slabs = a.shape[0] if a.ndim == 3 else 1
    b_slabs = b.shape[0] if b.ndim == 3 else 1
    assert not (ta and a_slabs > 1)
    a2, b2 = a.shape[-2:], b.shape[-2:]
    if ta:
        k_dim, m_dim = a2
    else:
        m_dim, k_dim = a2[0], a2[1] * a_slabs
    n_dim = b2[0] if tb else b2[1] * b_slabs
    k_slabs = max(a_slabs, b_slabs if tb else 1)
    n_slabs = max(out_slabs, 1 if tb else b_slabs)
    tm = _pick(m_dim, (1024, 1408, 512, 256, 128))
    tn = _pick(n_dim // n_slabs, (1536, 1408, 1280, 1024, 768, 512, 384, 256, 128))
    tk = _pick(k_dim // k_slabs, (1408, 1280, 1024, 512, 256, 128))
    nk, nj = k_dim // tk, n_dim // tn
    nk_a, nk_b, nj_b, nj_o = nk // a_slabs, nk // b_slabs, nj // b_slabs, nj // out_slabs
    dims = (((0 if ta else 1,), (1 if tb else 0,)), ((), ()))

    grid = (m_dim // tm, nj, nk)
    assert out_dtype == F32
    r_inputs, r_in_specs, r_out_specs, r_sems, split = _with_rider(rider, 2, 1, 0)

    def body(*refs):
        (a_ref, b_ref, o_ref), parts = split(refs)
        ride_first, ride_last = _ride(rider, parts, grid)
        if rider is not None:
            ride_first()
        prod = lax.dot_general(a_ref[...].astype(BF16), b_ref[...].astype(BF16), dims, preferred_element_type=F32)
        if nk == 1:
            o_ref[...] = prod
        else:
            _acc(o_ref, prod, pl.program_id(2) == 0)
        if rider is not None:
            ride_last()

    if ta:
        a_spec = pl.BlockSpec((tk, tm), lambda i, j, k: (k, i))
    elif a_slabs > 1:
        a_spec = pl.BlockSpec((None, tm, tk), lambda i, j, k: (k // nk_a, i, k % nk_a))
    else:
        a_spec = pl.BlockSpec((tm, tk), lambda i, j, k: (i, k))
    if tb and b_slabs > 1:
        b_spec = pl.BlockSpec((None, tn, tk), lambda i, j, k: (k // nk_b, j, k % nk_b))
    elif tb:
        b_spec = pl.BlockSpec((tn, tk), lambda i, j, k: (j, k))
    elif b_slabs > 1:
        b_spec = pl.BlockSpec((None, tk, tn), lambda i, j, k: (j // nj_b, k, j % nj_b))
    else:
        b_spec = pl.BlockSpec((tk, tn), lambda i, j, k: (k, j))
    if out_slabs > 1:
        o_spec = pl.BlockSpec((None, tm, tn), lambda i, j, k: (j // nj_o, i, j % nj_o))
        o_shape = (out_slabs, m_dim, n_dim // out_slabs)
    else:
        o_spec, o_shape = pl.BlockSpec((tm, tn), lambda i, j, k: (i, j)), (m_dim, n_dim)
    out, *rider_outs = pl.pallas_call(
        body, name=name, grid=grid,
        in_specs=[a_spec, b_spec] + r_in_specs, out_specs=[o_spec] + r_out_specs,
        out_shape=[jax.ShapeDtypeStruct(o_shape, out_dtype)] + (list(rider.out_shapes) if rider else []),
        scratch_shapes=r_sems,
        compiler_params=_cparams(("arbitrary",) * 3 if rider else ("parallel", "parallel", "arbitrary")),
    )(a, b, *r_inputs)
    return (out, rider_outs) if rider else out


def _ln(x, g, b):
    mu = jnp.mean(x, -1, keepdims=True)
    xc = x - mu
    var = jnp.mean(xc * xc, -1, keepdims=True)
    return xc * lax.rsqrt(var + EPS) * g + b


def _softplus(x):
    return jnp.maximum(x, 0.0) + jnp.log(1.0 + jnp.exp(-jnp.abs(x)))


def _silu(x):
    return x * jax.nn.sigmoid(x)


def _dsilu(x):
    s = jax.nn.sigmoid(x)
    return s * (1.0 + x * (1.0 - s))


def _f_ln0(x, g, b, sc, sh):
    x0 = _ln(x, g, b)
    return x0, x0 * (1.0 + sc) + sh


def _f_ln1(x0, y, gt, g, b, sc, sh):
    x1 = _ln(ALPHA * x0 + (1.0 + gt) * y, g, b)
    return x1, x1 * (1.0 + sc) + sh


def _f_ln2_loss(x1, y2, gt, g, b, tgt):
    x2 = _ln(ALPHA * x1 + (1.0 + gt) * y2, g, b)
    err = x2 - tgt
    per_row = jnp.sum(err * err, -1, keepdims=True) * (0.5 / D_MODEL)
    return jnp.sum(per_row, 0, keepdims=True)


def _row_specs(t_len):
    nt = t_len // ROW_TILE
    row = pl.BlockSpec((ROW_TILE, D_MODEL), lambda b, i: (b * nt + i, 0))
    vec = pl.BlockSpec((1, D_MODEL), lambda b, i: (0, 0))
    mod = pl.BlockSpec((None, 6, D_MODEL), lambda b, i: (b, 0, 0))
    return nt, row, vec, mod


def _first_step():
    return jnp.logical_and(pl.program_id(0) == 0, pl.program_id(1) == 0)


def _acc(ref, val, first, at=(Ellipsis,)):
    @pl.when(first)
    def _():
        ref[at] = val

    @pl.when(jnp.logical_not(first))
    def _():
        ref[at] += val


def _acc_rows(ref, rows, first):
    for i, r in enumerate(rows):
        _acc(ref, r, first, at=(slice(i, i + 1), slice(None)))


def _ln0_fwd(x, g, b, mod, n_b, t_len):
    nt, row, vec, mods = _row_specs(t_len)

    def body(x_ref, g_ref, b_ref, mod_ref, x0_ref, h_ref):
        x0, h = _f_ln0(x_ref[...], g_ref[...], b_ref[...], mod_ref[1:2, :], mod_ref[0:1, :])
        x0_ref[...] = x0
        h_ref[...] = h.astype(BF16)

    return pl.pallas_call(
        body, name="ln0_fwd", grid=(n_b, nt), in_specs=[row, vec, vec, mods], out_specs=[row, row],
        out_shape=[jax.ShapeDtypeStruct(x.shape, F32), jax.ShapeDtypeStruct(x.shape, BF16)],
        compiler_params=_cparams(("parallel", "parallel")),
    )(x, g, b, mod)


def _ln0_bwd(x, g, b, mod, dx0, dh, n_b, t_len):
    nt, row, vec, mods = _row_specs(t_len)
    dmod_spec = pl.BlockSpec((None, 2, D_MODEL), lambda bb, i: (bb, 0, 0))

    def body(x_ref, g_ref, b_ref, mod_ref, dx0_ref, dh_ref, dx_ref, dg_ref, db_ref, dmod_ref):
        _, pull = jax.vjp(_f_ln0, x_ref[...], g_ref[...], b_ref[...], mod_ref[1:2, :], mod_ref[0:1, :])
        dx, dg, db, dsc, dsh = pull((dx0_ref[...], dh_ref[...]))
        dx_ref[...] = dx
        _acc(dg_ref, dg, _first_step())
        _acc(db_ref, db, _first_step())
        _acc_rows(dmod_ref, [dsh, dsc], pl.program_id(1) == 0)

    return pl.pallas_call(
        body, name="ln0_bwd", grid=(n_b, nt), in_specs=[row, vec, vec, mods, row, row],
        out_specs=[row, vec, vec, dmod_spec],
        out_shape=[jax.ShapeDtypeStruct(x.shape, F32), jax.ShapeDtypeStruct((1, D_MODEL), F32),
                   jax.ShapeDtypeStruct((1, D_MODEL), F32), jax.ShapeDtypeStruct((n_b, 2, D_MODEL), F32)],
        compiler_params=_cparams(("arbitrary", "arbitrary")),
    )(x, g, b, mod, dx0, dh)


def _ln1_fwd(x0, y, g, b, mod, n_b, t_len):
    nt, row, vec, mods = _row_specs(t_len)

    def body(x0_ref, y_ref, g_ref, b_ref, mod_ref, x1_ref, h_ref):
        x1, h = _f_ln1(x0_ref[...], y_ref[...], mod_ref[2:3, :], g_ref[...], b_ref[...],
                       mod_ref[4:5, :], mod_ref[3:4, :])
        x1_ref[...] = x1
        h_ref[...] = h.astype(BF16)

    return pl.pallas_call(
        body, name="ln1_fwd", grid=(n_b, nt), in_specs=[row, row, vec, vec, mods], out_specs=[row, row],
        out_shape=[jax.ShapeDtypeStruct(x0.shape, F32), jax.ShapeDtypeStruct(x0.shape, BF16)],
        compiler_params=_cparams(("parallel", "parallel")),
    )(x0, y, g, b, mod)


def _ln1_bwd(x0, y, g, b, mod, dx1, dh, n_b, t_len):
    nt, row, vec, mods = _row_specs(t_len)
    dmod_spec = pl.BlockSpec((None, 3, D_MODEL), lambda bb, i: (bb, 0, 0))

    def body(x0_ref, y_ref, g_ref, b_ref, mod_ref, dx1_ref, dh_ref, dx0_ref, dy_ref, dg_ref, db_ref, dmod_ref):
        _, pull = jax.vjp(_f_ln1, x0_ref[...], y_ref[...], mod_ref[2:3, :], g_ref[...], b_ref[...],
                          mod_ref[4:5, :], mod_ref[3:4, :])
        dx0, dy, dgt, dg, db, dsc, dsh = pull((dx1_ref[...], dh_ref[...]))
        dx0_ref[...] = dx0
        dy_ref[...] = dy.astype(BF16)
        _acc(dg_ref, dg, _first_step())
        _acc(db_ref, db, _first_step())
        _acc_rows(dmod_ref, [dgt, dsh, dsc], pl.program_id(1) == 0)

    return pl.pallas_call(
        body, name="ln1_bwd", grid=(n_b, nt), in_specs=[row, row, vec, vec, mods, row, row],
        out_specs=[row, row, vec, vec, dmod_spec],
        out_shape=[jax.ShapeDtypeStruct(x0.shape, F32), jax.ShapeDtypeStruct(x0.shape, BF16),
                   jax.ShapeDtypeStruct((1, D_MODEL), F32), jax.ShapeDtypeStruct((1, D_MODEL), F32),
                   jax.ShapeDtypeStruct((n_b, 3, D_MODEL), F32)],
        compiler_params=_cparams(("arbitrary", "arbitrary")),
    )(x0, y, g, b, mod, dx1, dh)


def _ln2_loss_bwd(x1, y2, g, b, mod, tgt, n_b, t_len):
    nt, row, vec, mods = _row_specs(t_len)
    one = pl.BlockSpec((1, 128), lambda bb, i: (0, 0))
    dmod_spec = pl.BlockSpec((None, 1, D_MODEL), lambda bb, i: (bb, 0, 0))

    def body(x1_ref, y2_ref, g_ref, b_ref, mod_ref, t_ref, loss_ref, dx1_ref, dy2_ref, dg_ref, db_ref, dgt_ref):
        loss, pull = jax.vjp(functools.partial(_f_ln2_loss, tgt=t_ref[...]), x1_ref[...], y2_ref[...],
                             mod_ref[5:6, :], g_ref[...], b_ref[...])
        dx1, dy2, dgt, dg, db = pull(jnp.ones((1, 1), F32))
        dx1_ref[...] = dx1
        dy2_ref[...] = dy2.astype(BF16)
        _acc(loss_ref, jnp.broadcast_to(loss, (1, 128)), _first_step())
        _acc(dg_ref, dg, _first_step())
        _acc(db_ref, db, _first_step())
        _acc(dgt_ref, dgt, pl.program_id(1) == 0)

    return pl.pallas_call(
        body, name="ln2_loss_bwd", grid=(n_b, nt), in_specs=[row, row, vec, vec, mods, row],
        out_specs=[one, row, row, vec, vec, dmod_spec],
        out_shape=[jax.ShapeDtypeStruct((1, 128), F32), jax.ShapeDtypeStruct(x1.shape, F32),
                   jax.ShapeDtypeStruct(x1.shape, BF16), jax.ShapeDtypeStruct((1, D_MODEL), F32),
                   jax.ShapeDtypeStruct((1, D_MODEL), F32), jax.ShapeDtypeStruct((n_b, 1, D_MODEL), F32)],
        compiler_params=_cparams(("arbitrary", "arbitrary")),
    )(x1, y2, g, b, mod, tgt)


def _shift_down(x, s):
    if s == 0:
        return x
    rows = lax.broadcasted_iota(jnp.int32, x.shape, 0)
    return jnp.where(rows >= s, pltpu.roll(x, s, 0), 0.0)


def _shift_up(x, s):
    if s == 0:
        return x
    t_len = x.shape[0]
    rows = lax.broadcasted_iota(jnp.int32, x.shape, 0)
    return jnp.where(rows < t_len - s, pltpu.roll(x, t_len - s, 0), 0.0)


def _taps(x, k_w):
    return [_shift_down(x, k_w - 1 - k) for k in range(k_w)]


def _conv(taps, w):
    out = w[0:1, :] * taps[0]
    for k in range(1, len(taps)):
        out = out + w[k:k + 1, :] * taps[k]
    return out


def _conv_bwd(taps, w, du):
    k_w = len(taps)
    dx = w[k_w - 1:k_w, :] * du
    for k in range(k_w - 1):
        dx = dx + w[k:k + 1, :] * _shift_up(du, k_w - 1 - k)
    return dx, [jnp.sum(du * taps[k], 0, keepdims=True) for k in range(k_w)]


def _dn_pre_fwd(proj, conv_w, n_b, t_len):
    n_ct = 3 * HEADS
    k_w = conv_w.shape[0]

    def body(x_ref, w_ref, o_ref):
        o_ref[...] = _silu(_conv(_taps(x_ref[...], k_w), w_ref[...]))

    return pl.pallas_call(
        body, name="dn_pre_fwd", grid=(n_ct, n_b),
        in_specs=[pl.BlockSpec((t_len, 128), lambda j, b: (b, _qkv_block(j))),
                  pl.BlockSpec((k_w, 128), lambda j, b: (0, j))],
        out_specs=pl.BlockSpec((t_len, 128), lambda j, b: (b, j)),
        out_shape=jax.ShapeDtypeStruct((n_b * t_len, n_ct * 128), F32),
        compiler_params=_cparams(("parallel", "parallel")),
    )(proj, conv_w)


def _dn_pre_bwd(proj, conv_w, dqkv, d_proj, n_b, t_len):
    n_ct = 3 * HEADS
    k_w = conv_w.shape[0]

    def body(x_ref, w_ref, d_ref, _, dx_ref, dw_ref):
        taps, w = _taps(x_ref[...], k_w), w_ref[...]
        du = d_ref[...] * _dsilu(_conv(taps, w))
        dx, dw = _conv_bwd(taps, w, du)
        dx_ref[...] = dx.astype(BF16)
        _acc_rows(dw_ref, dw, pl.program_id(1) == 0)

    return pl.pallas_call(
        body, name="dn_pre_bwd", grid=(n_ct, n_b),
        in_specs=[pl.BlockSpec((t_len, 128), lambda j, b: (b, _qkv_block(j))),
                  pl.BlockSpec((k_w, 128), lambda j, b: (0, j)),
                  pl.BlockSpec((t_len, 128), lambda j, b: (b, j)), pl.BlockSpec(memory_space=pl.ANY)],
        out_specs=[pl.BlockSpec((t_len, 128), lambda j, b: (b, _qkv_block(j))),
                   pl.BlockSpec((k_w, 128), lambda j, b: (0, j))],
        out_shape=[jax.ShapeDtypeStruct(d_proj.shape, BF16), jax.ShapeDtypeStruct((k_w, n_ct * 128), F32)],
        input_output_aliases={3: 0},
        compiler_params=_cparams(("parallel", "arbitrary")),
    )(proj, conv_w, dqkv, d_proj)


FFN_TC = 256
FFN_NT = D_FF // FFN_TC


def _ffn_specs(t_len):
    blk = lambda off: pl.BlockSpec((t_len, FFN_TC), lambda j, b: (b, j + off))
    wblk = lambda off: pl.BlockSpec((3, FFN_TC), lambda j, b: (0, j + off))
    bblk = lambda off: pl.BlockSpec((1, FFN_TC), lambda j, b: (0, j + off))
    return [blk(0), blk(FFN_NT), wblk(0), wblk(FFN_NT), bblk(0), bblk(FFN_NT)]


def _ffn_act_fwd(up, conv_w, conv_b, n_b, t_len):
    def body(g_ref, v_ref, wg_ref, wv_ref, bg_ref, bv_ref, o_ref):
        ug = _conv(_taps(g_ref[...], 3), wg_ref[...]) + bg_ref[...]
        uv = _conv(_taps(v_ref[...], 3), wv_ref[...]) + bv_ref[...]
        o_ref[...] = (_silu(ug) * uv).astype(BF16)

    return pl.pallas_call(
        body, name="ffn_act_fwd", grid=(FFN_NT, n_b), in_specs=_ffn_specs(t_len),
        out_specs=pl.BlockSpec((t_len, FFN_TC), lambda j, b: (b, j)),
        out_shape=jax.ShapeDtypeStruct((n_b * t_len, D_FF), BF16),
        compiler_params=_cparams(("parallel", "parallel")),
    )(up, up, conv_w, conv_w, conv_b, conv_b)


def _ffn_act_bwd(up, conv_w, conv_b, da, n_b, t_len):
    def body(g_ref, v_ref, wg_ref, wv_ref, bg_ref, bv_ref, da_ref, dup_ref, dw_ref, db_ref):
        first = pl.program_id(1) == 0
        tg, tv, wg, wv = _taps(g_ref[...], 3), _taps(v_ref[...], 3), wg_ref[...], wv_ref[...]
        ug = _conv(tg, wg) + bg_ref[...]
        uv = _conv(tv, wv) + bv_ref[...]
        d_act = da_ref[...]
        sig = jax.nn.sigmoid(ug)
        d_v = d_act * (ug * sig)
        d_g = d_act * uv * (sig * (1.0 + ug * (1.0 - sig)))
        for slab, (taps, w, du) in enumerate(((tg, wg, d_g), (tv, wv, d_v))):
            dx, dw = _conv_bwd(taps, w, du)
            dup_ref[slab] = dx.astype(BF16)
            for k, dw_k in enumerate(dw):
                _acc(dw_ref, dw_k, first, at=(slab, slice(k, k + 1), slice(None)))
            _acc(db_ref, jnp.sum(du, 0, keepdims=True), first, at=(slab, slice(None), slice(None)))

    return pl.pallas_call(
        body, name="ffn_act_bwd", grid=(FFN_NT, n_b),
        in_specs=_ffn_specs(t_len) + [pl.BlockSpec((t_len, FFN_TC), lambda j, b: (b, j))],
        out_specs=[pl.BlockSpec((2, t_len, FFN_TC), lambda j, b: (0, b, j)),
                   pl.BlockSpec((2, 3, FFN_TC), lambda j, b: (0, 0, j)),
                   pl.BlockSpec((2, 1, FFN_TC), lambda j, b: (0, 0, j))],
        out_shape=[jax.ShapeDtypeStruct((2, n_b * t_len, D_FF), BF16),
                   jax.ShapeDtypeStruct((2, 3, D_FF), F32), jax.ShapeDtypeStruct((2, 1, D_FF), F32)],
        compiler_params=_cparams(("parallel", "arbitrary")),
    )(up, up, conv_w, conv_w, conv_b, conv_b, da)


NN = (((2,), (1,)), ((0,), (0,)))
NT = (((2,), (2,)), ((0,), (0,)))
TN = (((1,), (1,)), ((0,), (0,)))


def _iota3(shape, axis):
    return lax.broadcasted_iota(jnp.int32, shape, axis)


def _dg(a, b, dims):
    return lax.dot_general(a, b, dims, preferred_element_type=F32)


def _dot(a, b):
    return _dg(a, b, NN)


def _dot_nt(a, b):
    return _dg(a, b, NT)


def _dot_tn(a, b):
    return _dg(a, b, TN)


def _split(a):
    hi = a.astype(BF16)
    return hi, (a - hi.astype(F32)).astype(BF16)


def _dg3(a, b, dims):
    ah, al = _split(a)
    bh, bl = _split(b)
    return _dg(ah, bh, dims) + (_dg(ah, bl, dims) + _dg(al, bh, dims))


@jax.custom_vjp
def _dot3(a, b):
    return _dg3(a, b, NN)


def _dot3_fwd(a, b):
    return _dg3(a, b, NN), (a, b)


def _dot3_bwd(res, g):
    a, b = res
    return _dg3(g, b, NT), _dg3(a, g, TN)


_dot3.defvjp(_dot3_fwd, _dot3_bwd)


def _lower_ones(g_n, n):
    shape = (g_n, n, n)
    return jnp.where(_iota3(shape, 1) >= _iota3(shape, 2), 1.0, 0.0).astype(BF16)


@jax.custom_vjp
def _chunk_cumsum(x):
    hi, lo = _split(x)
    tri = _lower_ones(x.shape[0], x.shape[1])
    return _dg(tri, hi, NN) + _dg(tri, lo, NN)


def _chunk_cumsum_fwd(x):
    return _chunk_cumsum(x), None


def _chunk_cumsum_bwd(_, g):
    hi, lo = _split(g)
    tri = _lower_ones(g.shape[0], g.shape[1])
    return (_dg(tri, hi, TN) + _dg(tri, lo, TN),)


_chunk_cumsum.defvjp(_chunk_cumsum_fwd, _chunk_cumsum_bwd)


@jax.custom_vjp
def _unit_lower_inv(m):
    n = m.shape[1]
    p = -m
    a = jnp.where(_iota3(m.shape, 1) == _iota3(m.shape, 2), 1.0, 0.0) + p
    span = 2
    while span < n:
        p = _dg3(p, p, NN)
        a = a + _dg3(a, p, NN)
        span *= 2
    return a


def _unit_lower_inv_fwd(m):
    a = _unit_lower_inv(m)
    return a, a


def _unit_lower_inv_bwd(a, da):
    return (-_dg3(a, _dg3(da, a, NT), TN),)


_unit_lower_inv.defvjp(_unit_lower_inv_fwd, _unit_lower_inv_bwd)


@jax.custom_vjp
def _saved_lower_inv(m, a):
    return a


def _saved_lower_inv_fwd(m, a):
    return a, a


def _saved_lower_inv_bwd(a, da):
    return _unit_lower_inv_bwd(a, da)[0], jnp.zeros_like(a)


_saved_lower_inv.defvjp(_saved_lower_inv_fwd, _saved_lower_inv_bwd)


def _rms_gate(o, gn, gate):
    return o * lax.rsqrt(jnp.mean(o * o, -1, keepdims=True) + EPS) * gn * _silu(gate)


def _dn_chains(q, k, v, z, small, s_in, a_log, dt_bias, gn, a_saved=None):
    g_n, c_len = q.shape[0], q.shape[1]
    sq = (g_n, c_len, c_len)
    row, col = _iota3(sq, 1), _iota3(sq, 2)
    causal, strict, eye = row >= col, row > col, row == col
    qn = q * lax.rsqrt(jnp.sum(q * q, -1, keepdims=True) + EPS) * (HEAD_DIM ** -0.5)
    kn = k * lax.rsqrt(jnp.sum(k * k, -1, keepdims=True) + EPS)
    lane = _iota3(small.shape, 2)
    head = jnp.bitwise_and(_iota3(small.shape, 0), HEADS - 1)
    la_all = -jnp.exp(a_log) * _softplus(small + dt_bias)
    la_c = jnp.sum(jnp.where(lane == head, la_all, 0.0), 2, keepdims=True)
    beta = jnp.sum(jnp.where(lane == head + HEADS, jax.nn.sigmoid(small), 0.0), 2, keepdims=True)
    la_b = jnp.broadcast_to(la_c, sq)
    la_r = jnp.sum(jnp.where(eye, la_b, 0.0), 1, keepdims=True)
    g_c = jnp.sum(jnp.where(causal, jnp.broadcast_to(la_r, sq), 0.0), 2, keepdims=True)
    g_r = jnp.sum(jnp.where(row <= col, la_b, 0.0), 1, keepdims=True)
    g_last = jnp.sum(la_c, 1, keepdims=True)
    decay = jnp.exp(jnp.where(causal, g_c - g_r, -1e30))
    e_g = jnp.exp(g_c)
    kb = kn * beta
    m_low = jnp.where(strict, _dot_nt(kb, kn) * decay, 0.0)
    a_inv = _unit_lower_inv(m_low) if a_saved is None else _saved_lower_inv(m_low, a_saved)
    u = _dot3(a_inv, v * beta)
    w = _dot3(a_inv, kb * e_g)
    attn = _dot_nt(qn, kn) * decay
    v_new = u - _dot(w, s_in)
    o = _dot(qn * e_g, s_in) + _dot(attn, v_new)
    s_out = s_in * jnp.exp(g_last) + _dot_tn(kn * jnp.exp(g_last - g_c), v_new)
    return _rms_gate(o, gn, z), s_out, a_inv


def _gla_chains(q, k, v, gate, small, s_in, w2, b2, gn):
    g_n, c_len = q.shape[0], q.shape[1]
    sq, kk = (g_n, c_len, c_len), (g_n, GLA_KEY, GLA_KEY)
    causal = _iota3(sq, 1) >= _iota3(sq, 2)
    la = -_softplus(-(_dot(small, w2) + b2)) * (1.0 / 16.0)
    b = _chunk_cumsum(la)
    b_last = jnp.sum(jnp.where(_iota3(b.shape, 1) == c_len - 1, b, 0.0), 1, keepdims=True)
    q_dec = q * (GLA_KEY ** -0.5) * jnp.exp(b)
    attn = jnp.where(causal, _dot_nt(q_dec, k * jnp.exp(-b)), 0.0)
    o = _dot(q_dec, s_in) + _dot(attn, v)
    g_row = jnp.exp(b_last)
    g_col = jnp.sum(jnp.where(_iota3(kk, 1) == _iota3(kk, 2), jnp.broadcast_to(g_row, kk), 0.0), 2, keepdims=True)
    s_out = s_in * g_col + _dot_tn(k * jnp.exp(b_last - b), v)
    return _rms_gate(o, gn, gate), s_out


def _chunk_spec(n_b, width, col_block, n_c, reverse=False):
    if reverse:
        return pl.BlockSpec((n_b, CHUNK, width), lambda n: (0, n_c - 1 - n, col_block))
    return pl.BlockSpec((n_b, CHUNK, width), lambda n: (0, n, col_block))


def _hist_spec(n_b, d_k, n_c, reverse=False):
    if reverse:
        return pl.BlockSpec((None, n_b * HEADS, d_k, HEAD_DIM), lambda n: (n_c - 1 - n, 0, 0, 0))
    return pl.BlockSpec((None, n_b * HEADS, d_k, HEAD_DIM), lambda n: (n, 0, 0, 0))


def _ainv_spec(n_b, n_c, reverse=False):
    if reverse:
        return pl.BlockSpec((None, n_b * HEADS, CHUNK, CHUNK), lambda n: (n_c - 1 - n, 0, 0, 0))
    return pl.BlockSpec((None, n_b * HEADS, CHUNK, CHUNK), lambda n: (n, 0, 0, 0))


def _stack_chains(ref, n_b, slices):
    return jnp.stack([ref[b, :, sl] for b in range(n_b) for sl in slices], axis=0)


def _per_chain(ref, n_b):
    return jnp.stack([ref[b] for b in range(n_b) for _ in range(HEADS)], axis=0)


def _unstack_chains(ref, val, n_b, slices, offset=0):
    for b in range(n_b):
        for h, sl in enumerate(slices):
            ref[b, :, slice(offset + sl.start, offset + sl.stop)] = val[b * HEADS + h].astype(ref.dtype)


def _gate_weights(w2_ref, b2_ref, n_b):
    w2 = jnp.stack([w2_ref[:, ks] for _ in range(n_b) for ks in GLA_KSL], axis=0)
    b2 = jnp.stack([b2_ref[:, ks] for _ in range(n_b) for ks in GLA_KSL], axis=0)
    return w2, b2


def _sum_heads(val, n_b):
    return [sum(val[b * HEADS + h] for h in range(HEADS)) for b in range(n_b)]


def _const_spec(shape):
    return pl.BlockSpec(shape, lambda n: (0,) * len(shape))


DN_SL = [slice(h * HEAD_DIM, (h + 1) * HEAD_DIM) for h in range(HEADS)]
GLA_KSL = [slice(h * GLA_KEY, (h + 1) * GLA_KEY) for h in range(HEADS)]


class Rider:
    def __init__(self, inputs, out_shapes, sems, first, last):
        self.inputs, self.out_shapes, self.sems, self.first, self.last = inputs, out_shapes, sems, first, last


def _with_rider(rider, n_in, n_out, n_scratch):
    if rider is None:
        return [], [], [], [], lambda refs: (refs, None)
    r_in, r_out, r_sem = len(rider.inputs), len(rider.out_shapes), len(rider.sems)

    def split(refs):
        own_in, rest = refs[:n_in], refs[n_in:]
        rid_in, rest = rest[:r_in], rest[r_in:]
        own_out, rest = rest[:n_out], rest[n_out:]
        rid_out, rest = rest[:r_out], rest[r_out:]
        own_scr, rid_sem = rest[:n_scratch], rest[n_scratch:]
        return own_in + own_out + own_scr, (rid_in, rid_out, rid_sem)

    return list(rider.inputs), [HBM_SPEC] * r_in, [HBM_SPEC] * r_out, list(rider.sems), split


def _ride(rider, parts, grid):
    if rider is None:
        return None, None
    grid = grid if isinstance(grid, tuple) else (grid,)

    def at(step_of):
        hit = pl.program_id(0) == step_of(grid[0])
        for axis in range(1, len(grid)):
            hit = jnp.logical_and(hit, pl.program_id(axis) == step_of(grid[axis]))
        return hit

    def first():
        pl.when(at(lambda n: 0))(lambda: rider.first(*parts))

    def last():
        pl.when(at(lambda n: n - 1))(lambda: rider.last(*parts))

    return first, last


def _dn_scan_fwd(qkv, proj, a_log, dt_bias, gn, n_b, t_len, rider=None):
    n_c = t_len // CHUNK
    spec = functools.partial(_chunk_spec, n_b, n_c=n_c)
    r_inputs, r_in_specs, r_out_specs, r_sems, split = _with_rider(rider, 8, 3, 1)

    def body(*refs):
        (q_ref, k_ref, v_ref, z_ref, sm_ref, al_ref, dt_ref, gn_ref,
         o_ref, hist_ref, ainv_ref, s_ref), parts = split(refs)
        ride_first, ride_last = _ride(rider, parts, n_c)
        if rider is not None:
            ride_first()

        @pl.when(pl.program_id(0) == 0)
        def _():
            s_ref[...] = jnp.zeros_like(s_ref)

        s_in = s_ref[...]
        hist_ref[...] = s_in
        og, s_out, a_inv = _dn_chains(*(_stack_chains(r, n_b, DN_SL) for r in (q_ref, k_ref, v_ref, z_ref)),
                                      _per_chain(sm_ref, n_b), s_in, al_ref[...], dt_ref[...], gn_ref[...])
        _unstack_chains(o_ref, og, n_b, DN_SL)
        s_ref[...] = s_out
        ainv_ref[...] = a_inv
        if rider is not None:
            ride_last()

    qkv3, proj3 = qkv.reshape(n_b, t_len, -1), proj.reshape(n_b, t_len, -1)
    o, hist, ainv, *rider_outs = pl.pallas_call(
        body, name="dn_scan_fwd", grid=(n_c,),
        in_specs=[spec(512, 0), spec(512, 1), spec(512, 2), spec(512, OFF_Z // 512), spec(128, OFF_SMALL // 128),
                  _const_spec((1, 128)), _const_spec((1, 128)), _const_spec((1, 128))] + r_in_specs,
        out_specs=[spec(512, 0), _hist_spec(n_b, HEAD_DIM, n_c), _ainv_spec(n_b, n_c)] + r_out_specs,
        out_shape=[jax.ShapeDtypeStruct((n_b, t_len, 2 * 512), BF16),
                   jax.ShapeDtypeStruct((n_c, n_b * HEADS, HEAD_DIM, HEAD_DIM), F32),
                   jax.ShapeDtypeStruct((n_c, n_b * HEADS, CHUNK, CHUNK), F32)]
        + (list(rider.out_shapes) if rider else []),
        scratch_shapes=[pltpu.VMEM((n_b * HEADS, HEAD_DIM, HEAD_DIM), F32)] + r_sems,
        compiler_params=_cparams(("arbitrary",)),
    )(qkv3, qkv3, qkv3, proj3, proj3, a_log, dt_bias, gn, *r_inputs)
    return o, (hist, ainv), rider_outs


def _dn_scan_bwd(qkv, proj, a_log, dt_bias, gn, hist, d_o, n_b, t_len, rider=None):
    n_c = t_len // CHUNK
    rev = functools.partial(_chunk_spec, n_b, n_c=n_c, reverse=True)
    r_inputs, r_in_specs, r_out_specs, r_sems, split = _with_rider(rider, 11, 6, 1)
    hist, ainv = hist

    def body(*refs):
        (q_ref, k_ref, v_ref, z_ref, sm_ref, al_ref, dt_ref, gn_ref, hist_ref, ainv_ref, do_ref,
         dqkv_ref, dz_ref, dsm_ref, dal_ref, ddt_ref, dgn_ref, ds_ref), parts = split(refs)
        ride_first, ride_last = _ride(rider, parts, n_c)
        if rider is not None:
            ride_first()
        first = pl.program_id(0) == 0

        @pl.when(first)
        def _():
            ds_ref[...] = jnp.zeros_like(ds_ref)

        chains = lambda *a: _dn_chains(*a, a_saved=ainv_ref[...])[:2]
        _, pull = jax.vjp(chains, *(_stack_chains(r, n_b, DN_SL) for r in (q_ref, k_ref, v_ref, z_ref)),
                          _per_chain(sm_ref, n_b), hist_ref[...], al_ref[...], dt_ref[...], gn_ref[...])
        dq, dk, dv, dz, dsm, ds_in, dal, ddt, dgn = pull((_stack_chains(do_ref, n_b, DN_SL), ds_ref[...]))
        _unstack_chains(dqkv_ref, dq, n_b, DN_SL)
        _unstack_chains(dqkv_ref, dk, n_b, DN_SL, offset=512)
        _unstack_chains(dqkv_ref, dv, n_b, DN_SL, offset=1024)
        _unstack_chains(dz_ref, dz, n_b, DN_SL)
        ds_ref[...] = ds_in
        for b, dsm_b in enumerate(_sum_heads(dsm, n_b)):
            dsm_ref[b] = dsm_b
        _acc(dal_ref, dal, first)
        _acc(ddt_ref, ddt, first)
        _acc(dgn_ref, dgn, first)
        if rider is not None:
            ride_last()

    qkv3, proj3, do3 = (a.reshape(n_b, t_len, -1) for a in (qkv, proj, d_o))
    vec = jax.ShapeDtypeStruct((1, 128), F32)
    dqkv, d_proj, dsm, dal, ddt, dgn, *rider_outs = pl.pallas_call(
        body, name="dn_scan_bwd", grid=(n_c,),
        in_specs=[rev(512, 0), rev(512, 1), rev(512, 2), rev(512, OFF_Z // 512), rev(128, OFF_SMALL // 128),
                  _const_spec((1, 128)), _const_spec((1, 128)), _const_spec((1, 128)),
                  _hist_spec(n_b, HEAD_DIM, n_c, reverse=True), _ainv_spec(n_b, n_c, reverse=True),
                  rev(512, 0)] + r_in_specs,
        out_specs=[rev(1536, 0), rev(512, OFF_Z // 512), rev(128, 0),
                   _const_spec((1, 128)), _const_spec((1, 128)), _const_spec((1, 128))] + r_out_specs,
        out_shape=[jax.ShapeDtypeStruct((n_b, t_len, 1536), F32), jax.ShapeDtypeStruct((n_b, t_len, PROJ_W), BF16),
                   jax.ShapeDtypeStruct((n_b, t_len, 128), F32), vec, vec, vec]
        + (list(rider.out_shapes) if rider else []),
        scratch_shapes=[pltpu.VMEM((n_b * HEADS, HEAD_DIM, HEAD_DIM), F32)] + r_sems,
        compiler_params=_cparams(("arbitrary",)),
    )(qkv3, qkv3, qkv3, proj3, proj3, a_log, dt_bias, gn, hist, ainv, do3, *r_inputs)
    return dqkv.reshape(n_b * t_len, 1536), d_proj, dsm, dal, ddt, dgn, rider_outs


def _gla_scan_fwd(proj, w2, b2, gn, o_mix, n_b, t_len):
    n_c = t_len // CHUNK
    spec = functools.partial(_chunk_spec, n_b, n_c=n_c)

    def body(q_ref, k_ref, v_ref, g_ref, sm_ref, w2_ref, b2_ref, gn_ref, _, o_ref, hist_ref, s_ref):
        @pl.when(pl.program_id(0) == 0)
        def _():
            s_ref[...] = jnp.zeros_like(s_ref)

        s_in = s_ref[...]
        hist_ref[...] = s_in
        og, s_out = _gla_chains(_stack_chains(q_ref, n_b, GLA_KSL), _stack_chains(k_ref, n_b, GLA_KSL),
                                _stack_chains(v_ref, n_b, DN_SL), _stack_chains(g_ref, n_b, DN_SL),
                                _per_chain(sm_ref, n_b), s_in, *_gate_weights(w2_ref, b2_ref, n_b), gn_ref[...])
        _unstack_chains(o_ref, og, n_b, DN_SL)
        s_ref[...] = s_out

    proj3 = proj.reshape(n_b, t_len, -1)
    o, hist = pl.pallas_call(
        body, name="gla_scan_fwd", grid=(n_c,),
        in_specs=[spec(256, OFF_GQ // 256), spec(256, OFF_GK // 256), spec(512, OFF_GV // 512),
                  spec(512, OFF_GG // 512), spec(128, OFF_SMALL // 128),
                  _const_spec((128, 256)), _const_spec((1, 256)), _const_spec((1, 128)),
                  pl.BlockSpec(memory_space=pl.ANY)],
        out_specs=[spec(512, 1), _hist_spec(n_b, GLA_KEY, n_c)],
        out_shape=[jax.ShapeDtypeStruct(o_mix.shape, BF16),
                   jax.ShapeDtypeStruct((n_c, n_b * HEADS, GLA_KEY, HEAD_DIM), F32)],
        input_output_aliases={8: 0},
        scratch_shapes=[pltpu.VMEM((n_b * HEADS, GLA_KEY, HEAD_DIM), F32)],
        compiler_params=_cparams(("arbitrary",)),
    )(proj3, proj3, proj3, proj3, proj3, w2, b2, gn, o_mix)
    return o.reshape(n_b * t_len, 2 * 512), hist


def _gla_scan_bwd(proj, w2, b2, gn, hist, d_o, dsm_dn, d_proj, n_b, t_len):
    n_c = t_len // CHUNK
    rev = functools.partial(_chunk_spec, n_b, n_c=n_c, reverse=True)

    def body(q_ref, k_ref, v_ref, g_ref, sm_ref, w2_ref, b2_ref, gn_ref, hist_ref, do_ref, dsm_dn_ref, _,
             dp_ref, dw2_ref, db2_ref, dgn_ref, ds_ref):
        first = pl.program_id(0) == 0

        @pl.when(first)
        def _():
            ds_ref[...] = jnp.zeros_like(ds_ref)

        _, pull = jax.vjp(_gla_chains, _stack_chains(q_ref, n_b, GLA_KSL), _stack_chains(k_ref, n_b, GLA_KSL),
                          _stack_chains(v_ref, n_b, DN_SL), _stack_chains(g_ref, n_b, DN_SL),
                          _per_chain(sm_ref, n_b), hist_ref[...], *_gate_weights(w2_ref, b2_ref, n_b), gn_ref[...])
        dq, dk, dv, dg, dsm, ds_in, dw2, db2, dgn = pull((_stack_chains(do_ref, n_b, DN_SL), ds_ref[...]))
        _unstack_chains(dp_ref, dq, n_b, GLA_KSL, offset=OFF_GQ)
        _unstack_chains(dp_ref, dk, n_b, GLA_KSL, offset=OFF_GK)
        _unstack_chains(dp_ref, dv, n_b, DN_SL, offset=OFF_GV)
        _unstack_chains(dp_ref, dg, n_b, DN_SL, offset=OFF_GG)
        ds_ref[...] = ds_in
        for b, dsm_b in enumerate(_sum_heads(dsm, n_b)):
            dp_ref[b, :, OFF_SMALL:OFF_SMALL + 128] = (dsm_b + dsm_dn_ref[b]).astype(BF16)
            dp_ref[b, :, OFF_SMALL + 128:GLA_W] = jnp.zeros((CHUNK, GLA_W - OFF_SMALL - 128), BF16)
        for h, ks in enumerate(GLA_KSL):
            _acc(dw2_ref, sum(dw2[b * HEADS + h] for b in range(n_b)), first, at=(slice(None), ks))
            _acc(db2_ref, sum(db2[b * HEADS + h] for b in range(n_b)), first, at=(slice(None), ks))
        _acc(dgn_ref, dgn, first)

    proj3, do3 = proj.reshape(n_b, t_len, -1), d_o.reshape(n_b, t_len, -1)
    return pl.pallas_call(
        body, name="gla_scan_bwd", grid=(n_c,),
        in_specs=[rev(256, OFF_GQ // 256), rev(256, OFF_GK // 256), rev(512, OFF_GV // 512), rev(512, OFF_GG // 512),
                  rev(128, OFF_SMALL // 128),
                  _const_spec((128, 256)), _const_spec((1, 256)), _const_spec((1, 128)),
                  _hist_spec(n_b, GLA_KEY, n_c, reverse=True), rev(512, 1), rev(128, 0),
                  pl.BlockSpec(memory_space=pl.ANY)],
        out_specs=[rev(GLA_W, 0), _const_spec((128, 256)), _const_spec((1, 256)), _const_spec((1, 128))],
        out_shape=[jax.ShapeDtypeStruct(d_proj.shape, BF16), jax.ShapeDtypeStruct((128, 256), F32),
                   jax.ShapeDtypeStruct((1, 256), F32), jax.ShapeDtypeStruct((1, 128), F32)],
        input_output_aliases={11: 0},
        scratch_shapes=[pltpu.VMEM((n_b * HEADS, GLA_KEY, HEAD_DIM), F32)],
        compiler_params=_cparams(("arbitrary",)),
    )(proj3, proj3, proj3, proj3, proj3, w2, b2, gn, hist, do3, dsm_dn, d_proj)


W_IN_RUNS = ((0, 256, GLA_W), (256, 1536, OFF_Z + 512), (1536, 2048, OFF_Z), (2048, 2056, OFF_SMALL),
             (2056, 3592, 0), (3592, 3608, OFF_SMALL + 8))
W_IN_ROWS = 256


def _w_in_pieces(cols_per_chip):
    out = []
    for first, last, start in W_IN_RUNS:
        for j in range(N_CHIPS):
            a, b = max(first, cols_per_chip * j), min(last, cols_per_chip * (j + 1))
            if a < b:
                out.append((j, a - cols_per_chip * j, b - cols_per_chip * j, start + a - first))
    return out


def _w_in_to_padded(w4):
    _, n_r, n_c = w4.shape

    def body(i_ref, o_ref):
        o_ref[...] = jnp.zeros_like(o_ref)
        for j, a, b, p in _w_in_pieces(n_c):
            o_ref[:, p:p + b - a] = i_ref[j, :, a:b]

    return pl.pallas_call(
        body, name="w_in_to_padded", grid=(n_r // W_IN_ROWS,),
        in_specs=[pl.BlockSpec((N_CHIPS, W_IN_ROWS, n_c), lambda i: (0, i, 0))],
        out_specs=pl.BlockSpec((W_IN_ROWS, PROJ_W), lambda i: (i, 0)),
        out_shape=jax.ShapeDtypeStruct((n_r, PROJ_W), w4.dtype), compiler_params=_cparams(("parallel",)),
    )(w4)


def _w_in_to_chips(g, n_c):
    n_r = g.shape[0]

    def body(i_ref, o_ref):
        for j, a, b, p in _w_in_pieces(n_c):
            o_ref[j, :, a:b] = i_ref[:, p:p + b - a]

    return pl.pallas_call(
        body, name="w_in_to_chips", grid=(n_r // W_IN_ROWS,),
        in_specs=[pl.BlockSpec((W_IN_ROWS, PROJ_W), lambda i: (i, 0))],
        out_specs=pl.BlockSpec((N_CHIPS, W_IN_ROWS, n_c), lambda i: (0, i, 0)),
        out_shape=jax.ShapeDtypeStruct((N_CHIPS, n_r, n_c), g.dtype), compiler_params=_cparams(("parallel",)),
    )(g)


def _lane_vec(v, offset=0):
    return jnp.zeros((1, 128), F32).at[0, offset:offset + v.shape[0]].set(v)


def _local_step(x, tgt, mod, p, n_b, t_len, comm=None):
    row1 = lambda v: v.reshape(1, -1)
    a_log, dt_bias = _lane_vec(p["dn_a_log"]), _lane_vec(p["dn_dt_bias"])
    dn_gn, gla_gn = row1(p["dn_norm_g"]), row1(p["gla_norm_g"])
    w2 = jnp.zeros((128, 256), F32).at[8:8 + GATE_RANK].set(p["gla_w_gate2"])
    b2 = row1(p["gla_b_gate"])
    ln0_g, ln0_b, ln1_g, ln1_b, ln2_g, ln2_b = (row1(p[k]) for k in ("ln0_g", "ln0_b", "ln1_g", "ln1_b", "ln2_g", "ln2_b"))
    conv_b = row1(p["ffn_conv_b"])

    x0, h1 = _ln0_fwd(x, ln0_g, ln0_b, mod, n_b, t_len)
    proj = _mm(h1, p["w_in_p"], name="mm_proj")
    qkv = _dn_pre_fwd(proj, p["dn_conv"], n_b, t_len)
    o_half, hist_dn, landed = _dn_scan_fwd(qkv, proj, a_log, dt_bias, dn_gn, n_b, t_len,
                                           rider=comm.fwd_rider() if comm else None)
    if comm:
        p = {**p, **comm.weights_from(landed)}
    o_mix, hist_gla = _gla_scan_fwd(proj, w2, b2, gla_gn, o_half, n_b, t_len)
    y = _mm(o_mix, p["w_o"], name="mm_wo")
    x1, h2 = _ln1_fwd(x0, y, ln1_g, ln1_b, mod, n_b, t_len)
    up = _mm(h2, p["w_up"], name="mm_up")
    act = _ffn_act_fwd(up, p["ffn_conv"], conv_b, n_b, t_len)
    y2 = _mm(act, p["w_down"], name="mm_down")

    loss, dx1, dy2, g_ln2_g, g_ln2_b, dgt_f = _ln2_loss_bwd(x1, y2, ln2_g, ln2_b, mod, tgt, n_b, t_len)
    g_w_down = _mm(act, dy2, ta=True, name="mm_g_down")
    d_act = _mm(dy2, p["w_down"], tb=True, name="mm_d_act")
    d_up, g_ffn_conv, g_conv_b = _ffn_act_bwd(up, p["ffn_conv"], conv_b, d_act, n_b, t_len)
    g_w_up = _mm(h2, d_up, ta=True, out_slabs=N_CHIPS, name="mm_g_up")
    if comm:
        dh2, from_sibling = _mm(d_up, p["w_up"], tb=True, name="mm_d_h2", rider=comm.ffn_pair_rider(g_w_up, g_w_down))
    else:
        dh2 = _mm(d_up, p["w_up"], tb=True, name="mm_d_h2")
    dx0, dy, g_ln1_g, g_ln1_b, dmod_1 = _ln1_bwd(x0, y, ln1_g, ln1_b, mod, dx1, dh2, n_b, t_len)
    g_w_o = _mm(o_mix, dy, ta=True, name="mm_g_wo")
    d_o = _mm(dy, p["w_o"], tb=True, name="mm_d_o")
    dqkv, d_proj, dsm_dn, g_a_log, g_dt_bias, g_dn_gn, ffn_from_chips = _dn_scan_bwd(
        qkv, proj, a_log, dt_bias, dn_gn, hist_dn, d_o, n_b, t_len,
        rider=comm.ffn_chips_rider(from_sibling) if comm else None)
    d_proj, g_w2, g_b2, g_gla_gn = _gla_scan_bwd(proj, w2, b2, gla_gn, hist_gla, d_o, dsm_dn, d_proj, n_b, t_len)
    d_proj, g_dn_conv = _dn_pre_bwd(proj, p["dn_conv"], dqkv, d_proj.reshape(n_b * t_len, PROJ_W), n_b, t_len)
    g_w_in_p = _mm(h1, d_proj, ta=True, name="mm_g_win")
    if comm:
        dh1, tail_from_chips = _mm(d_proj, p["w_in_p"], tb=True, name="mm_d_h1",
                                   rider=comm.tail_chips_rider(g_w_in_p, g_w_o))
        from_chips = (ffn_from_chips, tail_from_chips)
    else:
        dh1, from_chips = _mm(d_proj, p["w_in_p"], tb=True, name="mm_d_h1"), None
    grad_x, g_ln0_g, g_ln0_b, dmod_0 = _ln0_bwd(x, ln0_g, ln0_b, mod, dx0, dh1, n_b, t_len)

    dmod = jnp.concatenate([dmod_0, dmod_1[:, 0:1], dmod_1[:, 1:3], dgt_f], axis=1)
    grads = {
        "ln0_g": g_ln0_g[0], "ln0_b": g_ln0_b[0], "w_in_p": g_w_in_p, "dn_conv": g_dn_conv,
        "dn_a_log": g_a_log[0, 0:HEADS], "dn_dt_bias": g_dt_bias[0, 0:HEADS], "dn_norm_g": g_dn_gn[0],
        "gla_w_gate2": g_w2[8:8 + GATE_RANK], "gla_b_gate": g_b2[0], "gla_norm_g": g_gla_gn[0],
        "w_o": g_w_o, "ln1_g": g_ln1_g[0], "ln1_b": g_ln1_b[0], "w_up": g_w_up,
        "ffn_conv": jnp.concatenate([g_ffn_conv[0], g_ffn_conv[1]], axis=1),
        "ffn_conv_b": jnp.concatenate([g_conv_b[0, 0], g_conv_b[1, 0]]), "w_down": g_w_down,
        "ln2_g": g_ln2_g[0], "ln2_b": g_ln2_b[0],
    }
    return loss, grad_x, grads, dmod, from_chips


def _col_sum(a):
    def body(a_ref, o_ref):
        o_ref[...] = jnp.sum(a_ref[...], 0, keepdims=True)

    return pl.pallas_call(body, name="col_sum", out_shape=jax.ShapeDtypeStruct((1, a.shape[1]), F32))(a)


def _adamw_math(w, grad, m, v):
    new_m = ADAM_B1 * m + (1.0 - ADAM_B1) * grad
    new_v = ADAM_B2 * v + (1.0 - ADAM_B2) * (grad * grad)
    m_hat = new_m / (1.0 - ADAM_B1 ** ADAM_STEP)
    v_hat = new_v / (1.0 - ADAM_B2 ** ADAM_STEP)
    return -ADAM_LR * (m_hat / (jnp.sqrt(v_hat) + ADAM_EPS) + ADAM_WD * w), new_m, new_v


def _adamw_many(ws, gs, ms, vs):
    n = len(ws)

    def body(*refs):
        for i in range(n):
            w_ref, g_ref, m_ref, v_ref = (refs[k * n + i] for k in range(4))
            d_ref, nm_ref, nv_ref = (refs[(4 + k) * n + i] for k in range(3))
            d_ref[...], nm_ref[...], nv_ref[...] = _adamw_math(w_ref[...], g_ref[...], m_ref[...], v_ref[...])

    outs = pl.pallas_call(
        body, name="adamw_small", out_shape=[jax.ShapeDtypeStruct(w.shape, F32) for w in ws] * 3,
    )(*ws, *gs, *ms, *vs)
    return outs[:n], outs[n:2 * n], outs[2 * n:]


def _adamw(w, g, m, v, name):
    n_r, n_c = w.shape
    if n_r % 8 == 0:
        tr = _pick(n_r, (256, 64, 32, 16, 8))
        grid, blk = (n_r // tr,), pl.BlockSpec((tr, n_c), lambda i: (i, 0))
    else:
        tc = _pick(n_c, (256, 128))
        grid, blk = (n_c // tc,), pl.BlockSpec((n_r, tc), lambda i: (0, i))

    def body(w_ref, g_ref, m_ref, v_ref, d_ref, nm_ref, nv_ref):
        d_ref[...], nm_ref[...], nv_ref[...] = _adamw_math(w_ref[...], g_ref[...], m_ref[...], v_ref[...])

    out = jax.ShapeDtypeStruct(w.shape, F32)
    return pl.pallas_call(
        body, name=name, grid=grid, in_specs=[blk] * 4, out_specs=[blk] * 3, out_shape=[out] * 3,
        compiler_params=_cparams(("parallel",)),
    )(w, g, m, v)


HBM_SPEC = pl.BlockSpec(memory_space=pltpu.HBM)
VMEM_SPEC = pl.BlockSpec(memory_space=pltpu.VMEM)
CHIP_FLIPS = ((1, 0), (0, 1), (1, 1))


def _place():
    return lax.axis_index("x"), lax.axis_index("y"), lax.axis_index("c")


def _flip(v, f):
    return 1 - v if f else v


def _all_gather8(slab, name):
    n_r, n_w = slab.shape

    def body(x_ref, o_ref, s_ref, send_sems, recv_sems, local_sem):
        x, y, c = _place()
        me = 4 * x + 2 * y + c
        mine = pltpu.make_async_copy(x_ref, o_ref.at[me], local_sem)
        mine.start()
        peers = [(_flip(x, k & 4), _flip(y, k & 2), _flip(c, k & 1)) for k in range(1, N_DEV)]
        sends = []
        for k, peer in enumerate(peers):
            cp = pltpu.make_async_remote_copy(src_ref=x_ref, dst_ref=o_ref.at[me], send_sem=send_sems.at[k],
                                              recv_sem=recv_sems.at[k], device_id=peer, device_id_type=MESH)
            cp.start()
            sends.append(cp)
        for k, (px, py, pc) in enumerate(peers):
            pltpu.make_async_remote_copy(src_ref=x_ref, dst_ref=o_ref.at[4 * px + 2 * py + pc],
                                         send_sem=send_sems.at[k], recv_sem=recv_sems.at[k],
                                         device_id=(px, py, pc), device_id_type=MESH).wait_recv()
        for cp in sends:
            cp.wait_send()
        mine.wait()
        total = o_ref[0]
        for d in range(1, N_DEV):
            total = total + o_ref[d]
        s_ref[...] = total

    return pl.pallas_call(
        body, name=name, in_specs=[VMEM_SPEC], out_specs=[VMEM_SPEC, VMEM_SPEC],
        out_shape=[jax.ShapeDtypeStruct((N_DEV, n_r, n_w), F32), jax.ShapeDtypeStruct((n_r, n_w), F32)],
        scratch_shapes=[pltpu.SemaphoreType.DMA((N_DEV - 1,)), pltpu.SemaphoreType.DMA((N_DEV - 1,)),
                        pltpu.SemaphoreType.DMA],
    )(slab)


SEQ_ROWS = 8


def _prologue(slab, w_ada_shard, b_shard, rider):
    n_r, n_w = slab.shape
    n_col = w_ada_shard.shape[1]
    r_inputs, r_in_specs, r_out_specs, r_sems, split = _with_rider(rider, 3, 3, 6)

    def body(*refs):
        (x_ref, w_ref, b_ref, g_ref, cond_ref, modr_ref, modp_ref, s1, r1, s2, r2, lsem), parts = split(refs)
        rider.first(*parts)
        x, y, c = _place()
        me = 4 * x + 2 * y + c
        peers = [(_flip(x, k & 4), _flip(y, k & 2), _flip(c, k & 1)) for k in range(1, N_DEV)]
        ids = [4 * px + 2 * py + pc for px, py, pc in peers]

        def exchange(src_of, dst, send_sems, recv_sems, own_sem):
            mine = pltpu.make_async_copy(src_of(me), dst.at[me], own_sem)
            mine.start()
            sends = [pltpu.make_async_remote_copy(src_ref=src_of(ids[k]), dst_ref=dst.at[me], send_sem=send_sems.at[k],
                                                  recv_sem=recv_sems.at[k], device_id=peers[k], device_id_type=MESH)
                     for k in range(N_DEV - 1)]
            for cp in sends:
                cp.start()
            for k in range(N_DEV - 1):
                pltpu.make_async_remote_copy(src_ref=src_of(ids[k]), dst_ref=dst.at[ids[k]], send_sem=send_sems.at[k],
                                             recv_sem=recv_sems.at[k], device_id=peers[k],
                                             device_id_type=MESH).wait_recv()
            for cp in sends:
                cp.wait_send()
            mine.wait()

        exchange(lambda d: x_ref, g_ref, s1, r1, lsem.at[0])
        cond = _silu(g_ref[:, 0:SEQ_ROWS, :].reshape(N_DEV * SEQ_ROWS, n_w))
        cond_ref[...] = cond
        modp_ref[...] = jnp.dot(cond.astype(BF16), w_ref[...].astype(BF16), preferred_element_type=F32) + b_ref[...]
        exchange(lambda d: modp_ref.at[pl.ds(pl.multiple_of(d * SEQ_ROWS, SEQ_ROWS), SEQ_ROWS)], modr_ref, s2, r2,
                 lsem.at[1])
        rider.last(*parts)

    sem7 = pltpu.SemaphoreType.DMA((N_DEV - 1,))
    gathered, cond, mod_recv, *rider_outs = pl.pallas_call(
        body, name="prologue", in_specs=[VMEM_SPEC] * 3 + r_in_specs, out_specs=[VMEM_SPEC] * 3 + r_out_specs,
        out_shape=[jax.ShapeDtypeStruct((N_DEV, n_r, n_w), F32), jax.ShapeDtypeStruct((N_DEV * SEQ_ROWS, n_w), F32),
                   jax.ShapeDtypeStruct((N_DEV, SEQ_ROWS, n_col), F32)] + list(rider.out_shapes),
        scratch_shapes=[pltpu.VMEM((N_DEV * SEQ_ROWS, n_col), F32), sem7, sem7, sem7, sem7,
                        pltpu.SemaphoreType.DMA((2,))] + r_sems,
        compiler_params=pltpu.CompilerParams(vmem_limit_bytes=VMEM_LIMIT),
    )(slab, w_ada_shard, b_shard, *r_inputs)
    return gathered, cond, mod_recv, rider_outs


def _gather_rider(shards):
    n_a = len(shards)

    def plan(ins, outs, sems):
        send_sems, recv_sems = sems
        x, y, c = _place()
        chips = [(_flip(x, fx), _flip(y, fy)) for fx, fy in CHIP_FLIPS]

        def copy(k, slot, chip_of_block, half, to, src=None):
            dst = outs[k].at[chip_of_block, half]
            return pltpu.make_async_remote_copy(src_ref=dst if src is None else src, dst_ref=dst,
                                                send_sem=send_sems.at[k * 6 + slot], recv_sem=recv_sems.at[k * 6 + slot],
                                                device_id=to, device_id_type=MESH)

        first = [copy(k, r, 2 * x + y, c, (*chips[r], c), src=ins[k].at[c]) for k in range(n_a) for r in range(3)]
        return copy, chips, first, (x, y, c)

    def first_step(ins, outs, sems):
        for cp in plan(ins, outs, sems)[2]:
            cp.start()

    def last_step(ins, outs, sems):
        copy, chips, first, (x, y, c) = plan(ins, outs, sems)
        passed = []
        for k in range(n_a):
            for r, (px, py) in enumerate(chips):
                copy(k, r, 2 * px + py, c, (x, y, c)).wait_recv()
                fwd = copy(k, 3 + r, 2 * px + py, c, (x, y, 1 - c))
                fwd.start()
                passed.append(fwd)
        for k in range(n_a):
            for r, (px, py) in enumerate(chips):
                copy(k, 3 + r, 2 * px + py, 1 - c, (x, y, c)).wait_recv()
        for cp in first + passed:
            cp.wait_send()

    return Rider(shards, [jax.ShapeDtypeStruct((N_CHIPS,) + s.shape, s.dtype) for s in shards],
                 [pltpu.SemaphoreType.DMA((6 * n_a,)), pltpu.SemaphoreType.DMA((6 * n_a,))], first_step, last_step)


def _place_own(gathered, shard, chip, name):
    _, _, n_h, n_c = gathered.shape
    th = _pick(n_h, (256, 176, 128))

    def body(sel_ref, s_ref, _, o_ref):
        o_ref[...] = s_ref[...]

    grid_spec = pltpu.PrefetchScalarGridSpec(
        num_scalar_prefetch=1, grid=(2, n_h // th),
        in_specs=[pl.BlockSpec((None, th, n_c), lambda hf, i, sel: (hf, i, 0)), pl.BlockSpec(memory_space=pl.ANY)],
        out_specs=pl.BlockSpec((None, None, th, n_c), lambda hf, i, sel: (sel[0], hf, i, 0)))
    return pl.pallas_call(
        body, name=name, grid_spec=grid_spec, out_shape=jax.ShapeDtypeStruct(gathered.shape, gathered.dtype),
        input_output_aliases={2: 0}, compiler_params=_cparams(("parallel", "parallel")),
    )(chip.reshape(1), shard, gathered)


def _pair_rider(parts):
    n_a = len(parts)

    def plan(ins, outs, sems):
        send_sems, recv_sems = sems
        x, y, c = _place()
        return [pltpu.make_async_remote_copy(src_ref=ins[k].at[:, 1 - c], dst_ref=outs[k], send_sem=send_sems.at[k],
                                             recv_sem=recv_sems.at[k], device_id=(x, y, 1 - c), device_id_type=MESH)
                for k in range(n_a)]

    def first_step(ins, outs, sems):
        for cp in plan(ins, outs, sems):
            cp.start()

    def last_step(ins, outs, sems):
        for cp in plan(ins, outs, sems):
            cp.wait()

    return Rider(parts, [jax.ShapeDtypeStruct((N_CHIPS,) + p.shape[2:], F32) for p in parts],
                 [pltpu.SemaphoreType.DMA((n_a,)), pltpu.SemaphoreType.DMA((n_a,))], first_step, last_step)


def _alone(rider, name):
    n_a = len(rider.inputs)

    def body(*refs):
        parts = (refs[:n_a], refs[n_a:2 * n_a], refs[2 * n_a:])
        rider.first(*parts)
        rider.last(*parts)

    return pl.pallas_call(
        body, name=name, in_specs=[HBM_SPEC] * n_a, out_specs=[HBM_SPEC] * n_a,
        out_shape=rider.out_shapes, scratch_shapes=rider.sems,
    )(*rider.inputs)


def _chips_rider(sums):
    n_a = len(sums)

    def plan(ins, outs, sems):
        send_sems, recv_sems = sems
        x, y, c = _place()
        cps = []
        for k in range(n_a):
            for r, (fx, fy) in enumerate(CHIP_FLIPS):
                px, py = _flip(x, fx), _flip(y, fy)
                cps.append(pltpu.make_async_remote_copy(
                    src_ref=ins[k].at[2 * px + py], dst_ref=outs[k].at[r], send_sem=send_sems.at[3 * k + r],
                    recv_sem=recv_sems.at[3 * k + r], device_id=(px, py, c), device_id_type=MESH))
        return cps

    def first_step(ins, outs, sems):
        for cp in plan(ins, outs, sems):
            cp.start()

    def last_step(ins, outs, sems):
        for cp in plan(ins, outs, sems):
            cp.wait()

    return Rider(sums, [jax.ShapeDtypeStruct((3,) + s.shape[1:], s.dtype) for s in sums],
                 [pltpu.SemaphoreType.DMA((3 * n_a,)), pltpu.SemaphoreType.DMA((3 * n_a,))], first_step, last_step)


def _rs_share(bufs):
    n_a = len(bufs)

    def body(*refs):
        ins, outs = refs[:n_a], refs[n_a:2 * n_a]
        send_sems, recv_sems = refs[2 * n_a:]
        x, y, c = _place()
        sends = [pltpu.make_async_remote_copy(src_ref=ins[k].at[c], dst_ref=outs[k].at[c], send_sem=send_sems.at[k],
                                              recv_sem=recv_sems.at[k], device_id=(x, y, 1 - c), device_id_type=MESH)
                 for k in range(n_a)]
        for cp in sends:
            cp.start()
        for k in range(n_a):
            pltpu.make_async_remote_copy(src_ref=ins[k].at[c], dst_ref=outs[k].at[1 - c], send_sem=send_sems.at[k],
                                         recv_sem=recv_sems.at[k], device_id=(x, y, 1 - c),
                                         device_id_type=MESH).wait_recv()
        for cp in sends:
            cp.wait_send()

    return pl.pallas_call(
        body, name="rs_share", in_specs=[HBM_SPEC] * n_a, out_specs=[HBM_SPEC] * n_a,
        out_shape=[jax.ShapeDtypeStruct(s.shape, F32) for s in bufs],
        input_output_aliases={k: k for k in range(n_a)},
        scratch_shapes=[pltpu.SemaphoreType.DMA((n_a,)), pltpu.SemaphoreType.DMA((n_a,))],
    )(*bufs)


def _pair_add(part, recv, core, name):
    _, _, n_h, n_c = part.shape
    th = _pick(n_h, (256, 176, 128))

    def body(sel_ref, p_ref, r_ref, o_ref):
        o_ref[...] = (p_ref[...] + r_ref[...]).astype(BF16)

    grid_spec = pltpu.PrefetchScalarGridSpec(
        num_scalar_prefetch=1, grid=(N_CHIPS, n_h // th),
        in_specs=[pl.BlockSpec((None, None, th, n_c), lambda j, i, sel: (j, sel[0], i, 0)),
                  pl.BlockSpec((None, th, n_c), lambda j, i, sel: (j, i, 0))],
        out_specs=pl.BlockSpec((None, th, n_c), lambda j, i, sel: (j, i, 0)))
    return pl.pallas_call(
        body, name=name, grid_spec=grid_spec, out_shape=jax.ShapeDtypeStruct(recv.shape, BF16),
        compiler_params=_cparams(("parallel", "parallel")),
    )(core.reshape(1), part, recv)


def _chip_add(sums, recv, chip, core, name):
    _, n_h, n_c = sums.shape
    th = _pick(n_h, (256, 176, 128))

    def body(sel_ref, s_ref, r_ref, o_ref):
        total = s_ref[...].astype(F32)
        for r in range(3):
            total = total + r_ref[r].astype(F32)
        o_ref[...] = total

    grid_spec = pltpu.PrefetchScalarGridSpec(
        num_scalar_prefetch=1, grid=(n_h // th,),
        in_specs=[pl.BlockSpec((None, th, n_c), lambda i, sel: (sel[0], i, 0)),
                  pl.BlockSpec((3, th, n_c), lambda i, sel: (0, i, 0))],
        out_specs=pl.BlockSpec((None, th, n_c), lambda i, sel: (sel[1], i, 0)))
    return pl.pallas_call(
        body, name=name, grid_spec=grid_spec, out_shape=jax.ShapeDtypeStruct((2, n_h, n_c), F32),
        compiler_params=_cparams(("parallel",)),
    )(jnp.stack([chip, core]), sums, recv)


def _row_halves(a):
    return a.reshape(N_CHIPS, 2, -1, a.shape[-1])


class StepComm:
    REST = ("w_o", "w_up", "w_down")

    def __init__(self, core, chip, rest_shards, in_cols):
        self.core, self.chip, self.shards, self.in_cols = core, chip, rest_shards, in_cols

    def fwd_rider(self):
        return _gather_rider(self.shards)

    def weights_from(self, landed):
        g_o, g_up, g_down = (_place_own(g, s, self.chip, "place_own_" + n)
                             for g, s, n in zip(landed, self.shards, self.REST))
        return {"w_o": g_o.reshape(-1, D_MODEL), "w_up": g_up.reshape(N_CHIPS, -1, g_up.shape[-1]),
                "w_down": g_down.reshape(-1, D_MODEL)}

    def _add_pairs(self, parts, from_sibling, names):
        return [_pair_add(p, r, self.core, "pair_add_" + n) for p, r, n in zip(parts, from_sibling, names)]

    def ffn_pair_rider(self, g_w_up, g_w_down):
        self.ffn_parts = [_row_halves(g_w_up), _row_halves(g_w_down)]
        return _pair_rider(self.ffn_parts)

    def ffn_chips_rider(self, from_sibling):
        self.ffn_sums = self._add_pairs(self.ffn_parts, from_sibling, ("w_up", "w_down"))
        return _chips_rider(self.ffn_sums)

    def tail_chips_rider(self, g_w_in_p, g_w_o):
        parts = [_row_halves(_w_in_to_chips(g_w_in_p, self.in_cols)), _row_halves(g_w_o)]
        self.tail_sums = self._add_pairs(parts, _alone(_pair_rider(parts), "rs_pair_tail"), ("w_in", "w_o"))
        return _chips_rider(self.tail_sums)

    def finish(self, ffn_from_chips, tail_from_chips):
        halves = [_chip_add(s, r, self.chip, self.core, "chip_add_" + n)
                  for s, r, n in zip(self.tail_sums + self.ffn_sums, list(tail_from_chips) + list(ffn_from_chips),
                                     ("w_in", "w_o", "w_up", "w_down"))]
        return [f.reshape(-1, f.shape[-1]) for f in _rs_share(halves)]


SLAB_W = 1024


def _pack(arrays, rows):
    flat = jnp.concatenate([a.reshape(-1).astype(F32) for a in arrays])
    return jnp.pad(flat, (0, rows * SLAB_W - flat.shape[0])).reshape(rows, SLAB_W)


def _unpack(flat, shapes):
    out, off = [], 0
    for s in shapes:
        n = 1
        for d in s:
            n *= d
        out.append(flat[off:off + n].reshape(s))
        off += n
    return out


def _rows_for(arrays_or_shapes):
    n = 0
    for a in arrays_or_shapes:
        s = a if isinstance(a, tuple) else a.shape
        k = 1
        for d in s:
            k *= d
        n += k
    return -(-n // (8 * SLAB_W)) * 8


def kernel(x, c, ln0_g, ln0_b, w_ada, b_ada, w_in, dn_conv, dn_a_log, dn_dt_bias, dn_norm_g, gla_w_gate2, gla_b_gate, gla_norm_g, w_o, ln1_g, ln1_b, ffn_w_up, ffn_conv, ffn_conv_b, ffn_w_down, ln2_g, ln2_b, loss_target, m_ln0_g, m_ln0_b, m_w_ada, m_b_ada, m_w_in, m_dn_conv, m_dn_a_log, m_dn_dt_bias, m_dn_norm_g, m_gla_w_gate2, m_gla_b_gate, m_gla_norm_g, m_w_o, m_ln1_g, m_ln1_b, m_ffn_w_up, m_ffn_conv, m_ffn_conv_b, m_ffn_w_down, m_ln2_g, m_ln2_b, v_ln0_g, v_ln0_b, v_w_ada, v_b_ada, v_w_in, v_dn_conv, v_dn_a_log, v_dn_dt_bias, v_dn_norm_g, v_gla_w_gate2, v_gla_b_gate, v_gla_norm_g, v_w_o, v_ln1_g, v_ln1_b, v_ffn_w_up, v_ffn_conv, v_ffn_conv_b, v_ffn_w_down, v_ln2_g, v_ln2_b):
    n_b, t_len, _ = x.shape
    xi, yi, ci = _place()
    chip = (2 * xi + yi).astype(jnp.int32)
    core = ci.astype(jnp.int32)
    n_all = N_DEV * n_b
    ada_cols = w_ada.shape[2]

    halves = lambda a: a.astype(BF16).reshape(2, a.shape[0] // 2, a.shape[1])
    w_in_halves = halves(w_in[0])
    sharded_small = [dn_conv[0], gla_w_gate2[0], ffn_conv[0]]
    slab = jnp.concatenate([_pack([c], SEQ_ROWS), _pack(sharded_small, _rows_for(sharded_small))], axis=0)
    b_ada_shard = lax.dynamic_slice(b_ada, (0, chip * ada_cols), (1, ada_cols))
    gathered, cond_pad, mod_recv, (g_in,) = _prologue(slab, w_ada[0], b_ada_shard, _gather_rider([w_in_halves]))
    g_in = _place_own(g_in, w_in_halves, chip, "place_own_w_in")
    cond_all = cond_pad.reshape(N_DEV, SEQ_ROWS, D_MODEL)[:, :n_b].reshape(n_all, D_MODEL)
    by_chip = gathered.reshape(N_DEV, -1)[0::2]
    full, off = [], SEQ_ROWS * SLAB_W
    for a in sharded_small:
        blocks = by_chip[:, off:off + a.size].reshape(N_CHIPS, *a.shape)
        full.append(blocks.transpose(1, 0, 2).reshape(a.shape[0], N_CHIPS * a.shape[1]))
        off += a.size
    dn_conv_f, gate2_f, ffn_conv_f = full
    mod = mod_recv[0::2, :n_b].transpose(1, 0, 2).reshape(n_b, 6, D_MODEL)

    comm = StepComm(core, chip, [halves(w_o[0]), halves(ffn_w_up[0]), halves(ffn_w_down[0])], w_in.shape[2])
    params = {
        "w_in_p": _w_in_to_padded(g_in.reshape(N_CHIPS, -1, g_in.shape[-1])),
        "dn_conv": dn_conv_f, "dn_a_log": dn_a_log[0], "dn_dt_bias": dn_dt_bias[0], "dn_norm_g": dn_norm_g[0],
        "gla_w_gate2": gate2_f, "gla_b_gate": gla_b_gate[0], "gla_norm_g": gla_norm_g[0],
        "ln0_g": ln0_g, "ln0_b": ln0_b, "ln1_g": ln1_g[0], "ln1_b": ln1_b[0], "ln2_g": ln2_g[0], "ln2_b": ln2_b[0],
        "ffn_conv": ffn_conv_f, "ffn_conv_b": ffn_conv_b[0],
    }

    loss_row, grad_x, gp, dmod, from_chips = _local_step(
        x.reshape(n_b * t_len, D_MODEL), loss_target.reshape(n_b * t_len, D_MODEL), mod, params, n_b, t_len, comm)
    names = ["ln0_g", "ln0_b", "w_ada", "b_ada", "w_in", "dn_conv", "dn_a_log", "dn_dt_bias", "dn_norm_g",
             "gla_w_gate2", "gla_b_gate", "gla_norm_g", "w_o", "ln1_g", "ln1_b", "ffn_w_up", "ffn_conv", "ffn_conv_b",
             "ffn_w_down", "ln2_g", "ln2_b"]
    weights = dict(zip(names, [ln0_g, ln0_b, w_ada, b_ada, w_in, dn_conv, dn_a_log, dn_dt_bias, dn_norm_g, gla_w_gate2,
                               gla_b_gate, gla_norm_g, w_o, ln1_g, ln1_b, ffn_w_up, ffn_conv, ffn_conv_b, ffn_w_down,
                               ln2_g, ln2_b]))
    m_in = dict(zip(names, [m_ln0_g, m_ln0_b, m_w_ada, m_b_ada, m_w_in, m_dn_conv, m_dn_a_log, m_dn_dt_bias,
                            m_dn_norm_g, m_gla_w_gate2, m_gla_b_gate, m_gla_norm_g, m_w_o, m_ln1_g, m_ln1_b,
                            m_ffn_w_up, m_ffn_conv, m_ffn_conv_b, m_ffn_w_down, m_ln2_g, m_ln2_b]))
    v_in = dict(zip(names, [v_ln0_g, v_ln0_b, v_w_ada, v_b_ada, v_w_in, v_dn_conv, v_dn_a_log, v_dn_dt_bias,
                            v_dn_norm_g, v_gla_w_gate2, v_gla_b_gate, v_gla_norm_g, v_w_o, v_ln1_g, v_ln1_b,
                            v_ffn_w_up, v_ffn_conv, v_ffn_conv_b, v_ffn_w_down, v_ln2_g, v_ln2_b]))
    grads, delta, new_m, new_v = {}, {}, {}, {}

    def adamw_big(n, grad):
        view = (lambda a: a.T) if n == "w_in" else (lambda a: a)
        outs = _adamw(view(weights[n][0]), view(grad), view(m_in[n][0]), view(v_in[n][0]), "adamw_" + n)
        grads[n] = grad[None]
        delta[n], new_m[n], new_v[n] = (view(a)[None] for a in outs)

    g_w_in, g_w_o, g_w_up, g_w_down = comm.finish(*from_chips)

    summed_names = ["loss", "ln0_g", "ln0_b", "dn_conv", "dn_a_log", "dn_dt_bias", "dn_norm_g", "gla_w_gate2",
                    "gla_b_gate", "gla_norm_g", "ln1_g", "ln1_b", "ffn_conv", "ffn_conv_b", "ln2_g", "ln2_b"]
    summed_parts = [loss_row[0, 0:1]] + [gp[n] for n in summed_names[1:]]
    sum_rows = _rows_for(summed_parts)
    slab = jnp.concatenate([_pack(summed_parts, sum_rows), _pack([dmod], _rows_for([dmod]))], axis=0)
    gathered, total = _all_gather8(slab, "reduce_small")
    small_g = dict(zip(summed_names, _unpack(total.reshape(-1), [a.shape for a in summed_parts])))
    loss = small_g["loss"][0]
    dmod_rows = n_b * 6 * D_MODEL // SLAB_W
    dmod_all = gathered[:, sum_rows:sum_rows + dmod_rows, :].reshape(n_all, 6 * D_MODEL)
    for n, grad in (("ffn_w_up", g_w_up), ("ffn_w_down", g_w_down), ("w_o", g_w_o), ("w_in", g_w_in)):
        adamw_big(n, grad)

    g_b_ada = _col_sum(dmod_all)
    dmod_cols = lax.dynamic_slice(dmod_all, (0, chip * ada_cols), (n_all, ada_cols))
    adamw_big("w_ada", _mm(cond_all, dmod_cols, ta=True, name="mm_g_ada"))

    col_block = lambda a: lax.dynamic_slice(a, (0, chip * (a.shape[1] // N_CHIPS)), (a.shape[0], a.shape[1] // N_CHIPS))
    grads.update({
        "ln0_g": small_g["ln0_g"], "ln0_b": small_g["ln0_b"], "b_ada": g_b_ada,
        "dn_conv": col_block(small_g["dn_conv"])[None], "dn_a_log": small_g["dn_a_log"][None],
        "dn_dt_bias": small_g["dn_dt_bias"][None], "dn_norm_g": small_g["dn_norm_g"][None],
        "gla_w_gate2": col_block(small_g["gla_w_gate2"])[None], "gla_b_gate": small_g["gla_b_gate"][None],
        "gla_norm_g": small_g["gla_norm_g"][None], "ln1_g": small_g["ln1_g"][None],
        "ln1_b": small_g["ln1_b"][None], "ffn_conv": col_block(small_g["ffn_conv"])[None],
        "ffn_conv_b": small_g["ffn_conv_b"][None], "ln2_g": small_g["ln2_g"][None], "ln2_b": small_g["ln2_b"][None],
    })
    small = [n for n in names if n not in delta]
    d_s, m_s, v_s = _adamw_many([weights[n] for n in small], [grads[n] for n in small],
                                [m_in[n] for n in small], [v_in[n] for n in small])
    for out, vals in ((delta, d_s), (new_m, m_s), (new_v, v_s)):
        out.update(zip(small, vals))

    return (loss, grad_x.reshape(x.shape), *[grads[n] for n in names], *[delta[n] for n in names],
            *[new_m[n] for n in names], *[new_v[n] for n in names])
```

```python
import functools

import jax
import jax.numpy as jnp
from jax import lax
from jax.experimental import pallas as pl
from jax.experimental.pallas import tpu as pltpu

F32 = jnp.float32
BF16 = jnp.bfloat16
MESH = pl.DeviceIdType.MESH

D_MODEL = 1024
HEADS = 4
HEAD_DIM = 128
GLA_KEY = 64
GATE_RANK = 16
CHUNK = 64
D_FF = 2816
ALPHA = 2.0 ** 0.25
EPS = 1e-6
N_CHIPS = 4
N_DEV = 8

PROJ_W = 3840
OFF_GQ, OFF_GK, OFF_GV, OFF_GG, OFF_SMALL, GLA_W = 0, 256, 512, 1024, 1536, 1792
OFF_Z = 2048
W_IN_COLS = 3608


def _qkv_block(j):
    return jnp.where(j < 2, GLA_W // 128 + j, (OFF_Z + 512) // 128 - 2 + j)

ADAM_LR, ADAM_B1, ADAM_B2, ADAM_EPS, ADAM_WD, ADAM_STEP = 0.001, 0.9, 0.999, 1e-08, 0.01, 10

VMEM_LIMIT = 56 * 1024 * 1024
ROW_TILE = 512


def _cparams(sem):
    return pltpu.CompilerParams(dimension_semantics=sem, vmem_limit_bytes=VMEM_LIMIT)


def _pick(n, prefs):
    for p in prefs:
        if n % p == 0:
            return p
    return n


def _mm(a, b, *, ta=False, tb=False, out_slabs=1, out_dtype=F32, name, rider=None):
    a_slabs = a.shape[0] if a.ndim == 3 else 1
    b_slabs = b.shape[0] if b.ndim == 3 else 1
    assert not (ta and a_slabs > 1)
    a2, b2 = a.shape[-2:], b.shape[-2:]
    if ta:
        k_dim, m_dim = a2
    else:
        m_dim, k_dim = a2[0], a2[1] * a_slabs
    n_dim = b2[0] if tb else b2[1] * b_slabs
    k_slabs = max(a_slabs, b_slabs if tb else 1)
    n_slabs = max(out_slabs, 1 if tb else b_slabs)
    tm = _pick(m_dim, (1024, 1408, 512, 256, 128))
    tn = _pick(n_dim // n_slabs, (1536, 1408, 1280, 1024, 768, 512, 384, 256, 128))
    tk = _pick(k_dim // k_slabs, (1408, 1280, 1024, 512, 256, 128))
    nk, nj = k_dim // tk, n_dim // tn
    nk_a, nk_b, nj_b, nj_o = nk // a_slabs, nk // b_slabs, nj // b_slabs, nj // out_slabs
    dims = (((0 if ta else 1,), (1 if tb else 0,)), ((), ()))

    grid = (m_dim // tm, nj, nk)
    assert out_dtype == F32
    r_inputs, r_in_specs, r_out_specs, r_sems, split = _with_rider(rider, 2, 1, 0)

    def body(*refs):
        (a_ref, b_ref, o_ref), parts = split(refs)
        ride_first, ride_last = _ride(rider, parts, grid)
        if rider is not None:
            ride_first()
        prod = lax.dot_general(a_ref[...].astype(BF16), b_ref[...].astype(BF16), dims, preferred_element_type=F32)
        if nk == 1:
            o_ref[...] = prod
        else:
            _acc(o_ref, prod, pl.program_id(2) == 0)
        if rider is not None:
            ride_last()

    if ta:
        a_spec = pl.BlockSpec((tk, tm), lambda i, j, k: (k, i))
    elif a_slabs > 1:
        a_spec = pl.BlockSpec((None, tm, tk), lambda i, j, k: (k // nk_a, i, k % nk_a))
    else:
        a_spec = pl.BlockSpec((tm, tk), lambda i, j, k: (i, k))
    if tb and b_slabs > 1:
        b_spec = pl.BlockSpec((None, tn, tk), lambda i, j, k: (k // nk_b, j, k % nk_b))
    elif tb:
        b_spec = pl.BlockSpec((tn, tk), lambda i, j, k: (j, k))
    elif b_slabs > 1:
        b_spec = pl.BlockSpec((None, tk, tn), lambda i, j, k: (j // nj_b, k, j % nj_b))
    else:
        b_spec = pl.BlockSpec((tk, tn), lambda i, j, k: (k, j))
    if out_slabs > 1:
        o_spec = pl.BlockSpec((None, tm, tn), lambda i, j, k: (j // nj_o, i, j % nj_o))
        o_shape = (out_slabs, m_dim, n_dim // out_slabs)
    else:
        o_spec, o_shape = pl.BlockSpec((tm, tn), lambda i, j, k: (i, j)), (m_dim, n_dim)
    out, *rider_outs = pl.pallas_call(
        body, name=name, grid=grid,
        in_specs=[a_spec, b_spec] + r_in_specs, out_specs=[o_spec] + r_out_specs,
        out_shape=[jax.ShapeDtypeStruct(o_shape, out_dtype)] + (list(rider.out_shapes) if rider else []),
        scratch_shapes=r_sems,
        compiler_params=_cparams(("arbitrary",) * 3 if rider else ("parallel", "parallel", "arbitrary")),
    )(a, b, *r_inputs)
    return (out, rider_outs) if rider else out


def _ln(x, g, b):
    mu = jnp.mean(x, -1, keepdims=True)
    xc = x - mu
    var = jnp.mean(xc * xc, -1, keepdims=True)
    return xc * lax.rsqrt(var + EPS) * g + b


def _softplus(x):
    return jnp.maximum(x, 0.0) + jnp.log(1.0 + jnp.exp(-jnp.abs(x)))


def _silu(x):
    return x * jax.nn.sigmoid(x)


def _dsilu(x):
    s = jax.nn.sigmoid(x)
    return s * (1.0 + x * (1.0 - s))


def _f_ln0(x, g, b, sc, sh):
    x0 = _ln(x, g, b)
    return x0, x0 * (1.0 + sc) + sh


def _f_ln1(x0, y, gt, g, b, sc, sh):
    x1 = _ln(ALPHA * x0 + (1.0 + gt) * y, g, b)
    return x1, x1 * (1.0 + sc) + sh


def _f_ln2_loss(x1, y2, gt, g, b, tgt):
    x2 = _ln(ALPHA * x1 + (1.0 + gt) * y2, g, b)
    err = x2 - tgt
    per_row = jnp.sum(err * err, -1, keepdims=True) * (0.5 / D_MODEL)
    return jnp.sum(per_row, 0, keepdims=True)


def _row_specs(t_len):
    nt = t_len // ROW_TILE
    row = pl.BlockSpec((ROW_TILE, D_MODEL), lambda b, i: (b * nt + i, 0))
    vec = pl.BlockSpec((1, D_MODEL), lambda b, i: (0, 0))
    mod = pl.BlockSpec((None, 6, D_MODEL), lambda b, i: (b, 0, 0))
    return nt, row, vec, mod


def _first_step():
    return jnp.logical_and(pl.program_id(0) == 0, pl.program_id(1) == 0)


def _acc(ref, val, first, at=(Ellipsis,)):
    @pl.when(first)
    def _():
        ref[at] = val

    @pl.when(jnp.logical_not(first))
    def _():
        ref[at] += val


def _acc_rows(ref, rows, first):
    for i, r in enumerate(rows):
        _acc(ref, r, first, at=(slice(i, i + 1), slice(None)))


def _ln0_fwd(x, g, b, mod, n_b, t_len):
    nt, row, vec, mods = _row_specs(t_len)

    def body(x_ref, g_ref, b_ref, mod_ref, x0_ref, h_ref):
        x0, h = _f_ln0(x_ref[...], g_ref[...], b_ref[...], mod_ref[1:2, :], mod_ref[0:1, :])
        x0_ref[...] = x0
        h_ref[...] = h.astype(BF16)

    return pl.pallas_call(
        body, name="ln0_fwd", grid=(n_b, nt), in_specs=[row, vec, vec, mods], out_specs=[row, row],
        out_shape=[jax.ShapeDtypeStruct(x.shape, F32), jax.ShapeDtypeStruct(x.shape, BF16)],
        compiler_params=_cparams(("parallel", "parallel")),
    )(x, g, b, mod)


def _ln0_bwd(x, g, b, mod, dx0, dh, n_b, t_len):
    nt, row, vec, mods = _row_specs(t_len)
    dmod_spec = pl.BlockSpec((None, 2, D_MODEL), lambda bb, i: (bb, 0, 0))

    def body(x_ref, g_ref, b_ref, mod_ref, dx0_ref, dh_ref, dx_ref, dg_ref, db_ref, dmod_ref):
        _, pull = jax.vjp(_f_ln0, x_ref[...], g_ref[...], b_ref[...], mod_ref[1:2, :], mod_ref[0:1, :])
        dx, dg, db, dsc, dsh = pull((dx0_ref[...], dh_ref[...]))
        dx_ref[...] = dx
        _acc(dg_ref, dg, _first_step())
        _acc(db_ref, db, _first_step())
        _acc_rows(dmod_ref, [dsh, dsc], pl.program_id(1) == 0)

    return pl.pallas_call(
        body, name="ln0_bwd", grid=(n_b, nt), in_specs=[row, vec, vec, mods, row, row],
        out_specs=[row, vec, vec, dmod_spec],
        out_shape=[jax.ShapeDtypeStruct(x.shape, F32), jax.ShapeDtypeStruct((1, D_MODEL), F32),
                   jax.ShapeDtypeStruct((1, D_MODEL), F32), jax.ShapeDtypeStruct((n_b, 2, D_MODEL), F32)],
        compiler_params=_cparams(("arbitrary", "arbitrary")),
    )(x, g, b, mod, dx0, dh)


def _ln1_fwd(x0, y, g, b, mod, n_b, t_len):
    nt, row, vec, mods = _row_specs(t_len)

    def body(x0_ref, y_ref, g_ref, b_ref, mod_ref, x1_ref, h_ref):
        x1, h = _f_ln1(x0_ref[...], y_ref[...], mod_ref[2:3, :], g_ref[...], b_ref[...],
                       mod_ref[4:5, :], mod_ref[3:4, :])
        x1_ref[...] = x1
        h_ref[...] = h.astype(BF16)

    return pl.pallas_call(
        body, name="ln1_fwd", grid=(n_b, nt), in_specs=[row, row, vec, vec, mods], out_specs=[row, row],
        out_shape=[jax.ShapeDtypeStruct(x0.shape, F32), jax.ShapeDtypeStruct(x0.shape, BF16)],
        compiler_params=_cparams(("parallel", "parallel")),
    )(x0, y, g, b, mod)


def _ln1_bwd(x0, y, g, b, mod, dx1, dh, n_b, t_len):
    nt, row, vec, mods = _row_specs(t_len)
    dmod_spec = pl.BlockSpec((None, 3, D_MODEL), lambda bb, i: (bb, 0, 0))

    def body(x0_ref, y_ref, g_ref, b_ref, mod_ref, dx1_ref, dh_ref, dx0_ref, dy_ref, dg_ref, db_ref, dmod_ref):
        _, pull = jax.vjp(_f_ln1, x0_ref[...], y_ref[...], mod_ref[2:3, :], g_ref[...], b_ref[...],
                          mod_ref[4:5, :], mod_ref[3:4, :])
        dx0, dy, dgt, dg, db, dsc, dsh = pull((dx1_ref[...], dh_ref[...]))
        dx0_ref[...] = dx0
        dy_ref[...] = dy.astype(BF16)
        _acc(dg_ref, dg, _first_step())
        _acc(db_ref, db, _first_step())
        _acc_rows(dmod_ref, [dgt, dsh, dsc], pl.program_id(1) == 0)

    return pl.pallas_call(
        body, name="ln1_bwd", grid=(n_b, nt), in_specs=[row, row, vec, vec, mods, row, row],
        out_specs=[row, row, vec, vec, dmod_spec],
        out_shape=[jax.ShapeDtypeStruct(x0.shape, F32), jax.ShapeDtypeStruct(x0.shape, BF16),
                   jax.ShapeDtypeStruct((1, D_MODEL), F32), jax.ShapeDtypeStruct((1, D_MODEL), F32),
                   jax.ShapeDtypeStruct((n_b, 3, D_MODEL), F32)],
        compiler_params=_cparams(("arbitrary", "arbitrary")),
    )(x0, y, g, b, mod, dx1, dh)


def _ln2_loss_bwd(x1, y2, g, b, mod, tgt, n_b, t_len):
    nt, row, vec, mods = _row_specs(t_len)
    one = pl.BlockSpec((1, 128), lambda bb, i: (0, 0))
    dmod_spec = pl.BlockSpec((None, 1, D_MODEL), lambda bb, i: (bb, 0, 0))

    def body(x1_ref, y2_ref, g_ref, b_ref, mod_ref, t_ref, loss_ref, dx1_ref, dy2_ref, dg_ref, db_ref, dgt_ref):
        loss, pull = jax.vjp(functools.partial(_f_ln2_loss, tgt=t_ref[...]), x1_ref[...], y2_ref[...],
                             mod_ref[5:6, :], g_ref[...], b_ref[...])
        dx1, dy2, dgt, dg, db = pull(jnp.ones((1, 1), F32))
        dx1_ref[...] = dx1
        dy2_ref[...] = dy2.astype(BF16)
        _acc(loss_ref, jnp.broadcast_to(loss, (1, 128)), _first_step())
        _acc(dg_ref, dg, _first_step())
        _acc(db_ref, db, _first_step())
        _acc(dgt_ref, dgt, pl.program_id(1) == 0)

    return pl.pallas_call(
        body, name="ln2_loss_bwd", grid=(n_b, nt), in_specs=[row, row, vec, vec, mods, row],
        out_specs=[one, row, row, vec, vec, dmod_spec],
        out_shape=[jax.ShapeDtypeStruct((1, 128), F32), jax.ShapeDtypeStruct(x1.shape, F32),
                   jax.ShapeDtypeStruct(x1.shape, BF16), jax.ShapeDtypeStruct((1, D_MODEL), F32),
                   jax.ShapeDtypeStruct((1, D_MODEL), F32), jax.ShapeDtypeStruct((n_b, 1, D_MODEL), F32)],
        compiler_params=_cparams(("arbitrary", "arbitrary")),
    )(x1, y2, g, b, mod, tgt)


def _shift_down(x, s):
    if s == 0:
        return x
    rows = lax.broadcasted_iota(jnp.int32, x.shape, 0)
    return jnp.where(rows >= s, pltpu.roll(x, s, 0), 0.0)


def _shift_up(x, s):
    if s == 0:
        return x
    t_len = x.shape[0]
    rows = lax.broadcasted_iota(jnp.int32, x.shape, 0)
    return jnp.where(rows < t_len - s, pltpu.roll(x, t_len - s, 0), 0.0)


def _taps(x, k_w):
    return [_shift_down(x, k_w - 1 - k) for k in range(k_w)]


def _conv(taps, w):
    out = w[0:1, :] * taps[0]
    for k in range(1, len(taps)):
        out = out + w[k:k + 1, :] * taps[k]
    return out


def _conv_bwd(taps, w, du):
    k_w = len(taps)
    dx = w[k_w - 1:k_w, :] * du
    for k in range(k_w - 1):
        dx = dx + w[k:k + 1, :] * _shift_up(du, k_w - 1 - k)
    return dx, [jnp.sum(du * taps[k], 0, keepdims=True) for k in range(k_w)]


def _dn_pre_fwd(proj, conv_w, n_b, t_len):
    n_ct = 3 * HEADS
    k_w = conv_w.shape[0]

    def body(x_ref, w_ref, o_ref):
        o_ref[...] = _silu(_conv(_taps(x_ref[...], k_w), w_ref[...]))

    return pl.pallas_call(
        body, name="dn_pre_fwd", grid=(n_ct, n_b),
        in_specs=[pl.BlockSpec((t_len, 128), lambda j, b: (b, _qkv_block(j))),
                  pl.BlockSpec((k_w, 128), lambda j, b: (0, j))],
        out_specs=pl.BlockSpec((t_len, 128), lambda j, b: (b, j)),
        out_shape=jax.ShapeDtypeStruct((n_b * t_len, n_ct * 128), F32),
        compiler_params=_cparams(("parallel", "parallel")),
    )(proj, conv_w)


def _dn_pre_bwd(proj, conv_w, dqkv, d_proj, n_b, t_len):
    n_ct = 3 * HEADS
    k_w = conv_w.shape[0]

    def body(x_ref, w_ref, d_ref, _, dx_ref, dw_ref):
        taps, w = _taps(x_ref[...], k_w), w_ref[...]
        du = d_ref[...] * _dsilu(_conv(taps, w))
        dx, dw = _conv_bwd(taps, w, du)
        dx_ref[...] = dx.astype(BF16)
        _acc_rows(dw_ref, dw, pl.program_id(1) == 0)

    return pl.pallas_call(
        body, name="dn_pre_bwd", grid=(n_ct, n_b),
        in_specs=[pl.BlockSpec((t_len, 128), lambda j, b: (b, _qkv_block(j))),
                  pl.BlockSpec((k_w, 128), lambda j, b: (0, j)),
                  pl.BlockSpec((t_len, 128), lambda j, b: (b, j)), pl.BlockSpec(memory_space=pl.ANY)],
        out_specs=[pl.BlockSpec((t_len, 128), lambda j, b: (b, _qkv_block(j))),
                   pl.BlockSpec((k_w, 128), lambda j, b: (0, j))],
        out_shape=[jax.ShapeDtypeStruct(d_proj.shape, BF16), jax.ShapeDtypeStruct((k_w, n_ct * 128), F32)],
        input_output_aliases={3: 0},
        compiler_params=_cparams(("parallel", "arbitrary")),
    )(proj, conv_w, dqkv, d_proj)


FFN_TC = 256
FFN_NT = D_FF // FFN_TC


def _ffn_specs(t_len):
    blk = lambda off: pl.BlockSpec((t_len, FFN_TC), lambda j, b: (b, j + off))
    wblk = lambda off: pl.BlockSpec((3, FFN_TC), lambda j, b: (0, j + off))
    bblk = lambda off: pl.BlockSpec((1, FFN_TC), lambda j, b: (0, j + off))
    return [blk(0), blk(FFN_NT), wblk(0), wblk(FFN_NT), bblk(0), bblk(FFN_NT)]


def _ffn_act_fwd(up, conv_w, conv_b, n_b, t_len):
    def body(g_ref, v_ref, wg_ref, wv_ref, bg_ref, bv_ref, o_ref):
        ug = _conv(_taps(g_ref[...], 3), wg_ref[...]) + bg_ref[...]
        uv = _conv(_taps(v_ref[...], 3), wv_ref[...]) + bv_ref[...]
        o_ref[...] = (_silu(ug) * uv).astype(BF16)

    return pl.pallas_call(
        body, name="ffn_act_fwd", grid=(FFN_NT, n_b), in_specs=_ffn_specs(t_len),
        out_specs=pl.BlockSpec((t_len, FFN_TC), lambda j, b: (b, j)),
        out_shape=jax.ShapeDtypeStruct((n_b * t_len, D_FF), BF16),
        compiler_params=_cparams(("parallel", "parallel")),
    )(up, up, conv_w, conv_w, conv_b, conv_b)


def _ffn_act_bwd(up, conv_w, conv_b, da, n_b, t_len):
    def body(g_ref, v_ref, wg_ref, wv_ref, bg_ref, bv_ref, da_ref, dup_ref, dw_ref, db_ref):
        first = pl.program_id(1) == 0
        tg, tv, wg, wv = _taps(g_ref[...], 3), _taps(v_ref[...], 3), wg_ref[...], wv_ref[...]
        ug = _conv(tg, wg) + bg_ref[...]
        uv = _conv(tv, wv) + bv_ref[...]
        d_act = da_ref[...]
        sig = jax.nn.sigmoid(ug)
        d_v = d_act * (ug * sig)
        d_g = d_act * uv * (sig * (1.0 + ug * (1.0 - sig)))
        for slab, (taps, w, du) in enumerate(((tg, wg, d_g), (tv, wv, d_v))):
            dx, dw = _conv_bwd(taps, w, du)
            dup_ref[slab] = dx.astype(BF16)
            for k, dw_k in enumerate(dw):
                _acc(dw_ref, dw_k, first, at=(slab, slice(k, k + 1), slice(None)))
            _acc(db_ref, jnp.sum(du, 0, keepdims=True), first, at=(slab, slice(None), slice(None)))

    return pl.pallas_call(
        body, name="ffn_act_bwd", grid=(FFN_NT, n_b),
        in_specs=_ffn_specs(t_len) + [pl.BlockSpec((t_len, FFN_TC), lambda j, b: (b, j))],
        out_specs=[pl.BlockSpec((2, t_len, FFN_TC), lambda j, b: (0, b, j)),
                   pl.BlockSpec((2, 3, FFN_TC), lambda j, b: (0, 0, j)),
                   pl.BlockSpec((2, 1, FFN_TC), lambda j, b: (0, 0, j))],
        out_shape=[jax.ShapeDtypeStruct((2, n_b * t_len, D_FF), BF16),
                   jax.ShapeDtypeStruct((2, 3, D_FF), F32), jax.ShapeDtypeStruct((2, 1, D_FF), F32)],
        compiler_params=_cparams(("parallel", "arbitrary")),
    )(up, up, conv_w, conv_w, conv_b, conv_b, da)


NN = (((2,), (1,)), ((0,), (0,)))
NT = (((2,), (2,)), ((0,), (0,)))
TN = (((1,), (1,)), ((0,), (0,)))


def _iota3(shape, axis):
    return lax.broadcasted_iota(jnp.int32, shape, axis)


def _dg(a, b, dims):
    return lax.dot_general(a, b, dims, preferred_element_type=F32)


def _dot(a, b):
    return _dg(a, b, NN)


def _dot_nt(a, b):
    return _dg(a, b, NT)


def _dot_tn(a, b):
    return _dg(a, b, TN)


def _split(a):
    hi = a.astype(BF16)
    return hi, (a - hi.astype(F32)).astype(BF16)


def _dg3(a, b, dims):
    ah, al = _split(a)
    bh, bl = _split(b)
    return _dg(ah, bh, dims) + (_dg(ah, bl, dims) + _dg(al, bh, dims))


@jax.custom_vjp
def _dot3(a, b):
    return _dg3(a, b, NN)


def _dot3_fwd(a, b):
    return _dg3(a, b, NN), (a, b)


def _dot3_bwd(res, g):
    a, b = res
    return _dg3(g, b, NT), _dg3(a, g, TN)


_dot3.defvjp(_dot3_fwd, _dot3_bwd)


def _lower_ones(g_n, n):
    shape = (g_n, n, n)
    return jnp.where(_iota3(shape, 1) >= _iota3(shape, 2), 1.0, 0.0).astype(BF16)


@jax.custom_vjp
def _chunk_cumsum(x):
    hi, lo = _split(x)
    tri = _lower_ones(x.shape[0], x.shape[1])
    return _dg(tri, hi, NN) + _dg(tri, lo, NN)


def _chunk_cumsum_fwd(x):
    return _chunk_cumsum(x), None


def _chunk_cumsum_bwd(_, g):
    hi, lo = _split(g)
    tri = _lower_ones(g.shape[0], g.shape[1])
    return (_dg(tri, hi, TN) + _dg(tri, lo, TN),)


_chunk_cumsum.defvjp(_chunk_cumsum_fwd, _chunk_cumsum_bwd)


@jax.custom_vjp
def _unit_lower_inv(m):
    n = m.shape[1]
    p = -m
    a = jnp.where(_iota3(m.shape, 1) == _iota3(m.shape, 2), 1.0, 0.0) + p
    span = 2
    while span < n:
        p = _dg3(p, p, NN)
        a = a + _dg3(a, p, NN)
        span *= 2
    return a


def _unit_lower_inv_fwd(m):
    a = _unit_lower_inv(m)
    return a, a


def _unit_lower_inv_bwd(a, da):
    return (-_dg3(a, _dg3(da, a, NT), TN),)


_unit_lower_inv.defvjp(_unit_lower_inv_fwd, _unit_lower_inv_bwd)


@jax.custom_vjp
def _saved_lower_inv(m, a):
    return a


def _saved_lower_inv_fwd(m, a):
    return a, a


def _saved_lower_inv_bwd(a, da):
    return _unit_lower_inv_bwd(a, da)[0], jnp.zeros_like(a)


_saved_lower_inv.defvjp(_saved_lower_inv_fwd, _saved_lower_inv_bwd)


def _rms_gate(o, gn, gate):
    return o * lax.rsqrt(jnp.mean(o * o, -1, keepdims=True) + EPS) * gn * _silu(gate)


def _dn_chains(q, k, v, z, small, s_in, a_log, dt_bias, gn, a_saved=None):
    g_n, c_len = q.shape[0], q.shape[1]
    sq = (g_n, c_len, c_len)
    row, col = _iota3(sq, 1), _iota3(sq, 2)
    causal, strict, eye = row >= col, row > col, row == col
    qn = q * lax.rsqrt(jnp.sum(q * q, -1, keepdims=True) + EPS) * (HEAD_DIM ** -0.5)
    kn = k * lax.rsqrt(jnp.sum(k * k, -1, keepdims=True) + EPS)
    lane = _iota3(small.shape, 2)
    head = jnp.bitwise_and(_iota3(small.shape, 0), HEADS - 1)
    la_all = -jnp.exp(a_log) * _softplus(small + dt_bias)
    la_c = jnp.sum(jnp.where(lane == head, la_all, 0.0), 2, keepdims=True)
    beta = jnp.sum(jnp.where(lane == head + HEADS, jax.nn.sigmoid(small), 0.0), 2, keepdims=True)
    la_b = jnp.broadcast_to(la_c, sq)
    la_r = jnp.sum(jnp.where(eye, la_b, 0.0), 1, keepdims=True)
    g_c = jnp.sum(jnp.where(causal, jnp.broadcast_to(la_r, sq), 0.0), 2, keepdims=True)
    g_r = jnp.sum(jnp.where(row <= col, la_b, 0.0), 1, keepdims=True)
    g_last = jnp.sum(la_c, 1, keepdims=True)
    decay = jnp.exp(jnp.where(causal, g_c - g_r, -1e30))
    e_g = jnp.exp(g_c)
    kb = kn * beta
    m_low = jnp.where(strict, _dot_nt(kb, kn) * decay, 0.0)
    a_inv = _unit_lower_inv(m_low) if a_saved is None else _saved_lower_inv(m_low, a_saved)
    u = _dot3(a_inv, v * beta)
    w = _dot3(a_inv, kb * e_g)
    attn = _dot_nt(qn, kn) * decay
    v_new = u - _dot(w, s_in)
    o = _dot(qn * e_g, s_in) + _dot(attn, v_new)
    s_out = s_in * jnp.exp(g_last) + _dot_tn(kn * jnp.exp(g_last - g_c), v_new)
    return _rms_gate(o, gn, z), s_out, a_inv


def _gla_chains(q, k, v, gate, small, s_in, w2, b2, gn):
    g_n, c_len = q.shape[0], q.shape[1]
    sq, kk = (g_n, c_len, c_len), (g_n, GLA_KEY, GLA_KEY)
    causal = _iota3(sq, 1) >= _iota3(sq, 2)
    la = -_softplus(-(_dot(small, w2) + b2)) * (1.0 / 16.0)
    b = _chunk_cumsum(la)
    b_last = jnp.sum(jnp.where(_iota3(b.shape, 1) == c_len - 1, b, 0.0), 1, keepdims=True)
    q_dec = q * (GLA_KEY ** -0.5) * jnp.exp(b)
    attn = jnp.where(causal, _dot_nt(q_dec, k * jnp.exp(-b)), 0.0)
    o = _dot(q_dec, s_in) + _dot(attn, v)
    g_row = jnp.exp(b_last)
    g_col = jnp.sum(jnp.where(_iota3(kk, 1) == _iota3(kk, 2), jnp.broadcast_to(g_row, kk), 0.0), 2, keepdims=True)
    s_out = s_in * g_col + _dot_tn(k * jnp.exp(b_last - b), v)
    return _rms_gate(o, gn, gate), s_out


def _chunk_spec(n_b, width, col_block, n_c, reverse=False):
    if reverse:
        return pl.BlockSpec((n_b, CHUNK, width), lambda n: (0, n_c - 1 - n, col_block))
    return pl.BlockSpec((n_b, CHUNK, width), lambda n: (0, n, col_block))


def _hist_spec(n_b, d_k, n_c, reverse=False):
    if reverse:
        return pl.BlockSpec((None, n_b * HEADS, d_k, HEAD_DIM), lambda n: (n_c - 1 - n, 0, 0, 0))
    return pl.BlockSpec((None, n_b * HEADS, d_k, HEAD_DIM), lambda n: (n, 0, 0, 0))


def _ainv_spec(n_b, n_c, reverse=False):
    if reverse:
        return pl.BlockSpec((None, n_b * HEADS, CHUNK, CHUNK), lambda n: (n_c - 1 - n, 0, 0, 0))
    return pl.BlockSpec((None, n_b * HEADS, CHUNK, CHUNK), lambda n: (n, 0, 0, 0))


def _stack_chains(ref, n_b, slices):
    return jnp.stack([ref[b, :, sl] for b in range(n_b) for sl in slices], axis=0)


def _per_chain(ref, n_b):
    return jnp.stack([ref[b] for b in range(n_b) for _ in range(HEADS)], axis=0)


def _unstack_chains(ref, val, n_b, slices, offset=0):
    for b in range(n_b):
        for h, sl in enumerate(slices):
            ref[b, :, slice(offset + sl.start, offset + sl.stop)] = val[b * HEADS + h].astype(ref.dtype)


def _gate_weights(w2_ref, b2_ref, n_b):
    w2 = jnp.stack([w2_ref[:, ks] for _ in range(n_b) for ks in GLA_KSL], axis=0)
    b2 = jnp.stack([b2_ref[:, ks] for _ in range(n_b) for ks in GLA_KSL], axis=0)
    return w2, b2


def _sum_heads(val, n_b):
    return [sum(val[b * HEADS + h] for h in range(HEADS)) for b in range(n_b)]


def _const_spec(shape):
    return pl.BlockSpec(shape, lambda n: (0,) * len(shape))


DN_SL = [slice(h * HEAD_DIM, (h + 1) * HEAD_DIM) for h in range(HEADS)]
GLA_KSL = [slice(h * GLA_KEY, (h + 1) * GLA_KEY) for h in range(HEADS)]


class Rider:
    def __init__(self, inputs, out_shapes, sems, first, last):
        self.inputs, self.out_shapes, self.sems, self.first, self.last = inputs, out_shapes, sems, first, last


def _with_rider(rider, n_in, n_out, n_scratch):
    if rider is None:
        return [], [], [], [], lambda refs: (refs, None)
    r_in, r_out, r_sem = len(rider.inputs), len(rider.out_shapes), len(rider.sems)

    def split(refs):
        own_in, rest = refs[:n_in], refs[n_in:]
        rid_in, rest = rest[:r_in], rest[r_in:]
        own_out, rest = rest[:n_out], rest[n_out:]
        rid_out, rest = rest[:r_out], rest[r_out:]
        own_scr, rid_sem = rest[:n_scratch], rest[n_scratch:]
        return own_in + own_out + own_scr, (rid_in, rid_out, rid_sem)

    return list(rider.inputs), [HBM_SPEC] * r_in, [HBM_SPEC] * r_out, list(rider.sems), split


def _ride(rider, parts, grid):
    if rider is None:
        return None, None
    grid = grid if isinstance(grid, tuple) else (grid,)

    def at(step_of):
        hit = pl.program_id(0) == step_of(grid[0])
        for axis in range(1, len(grid)):
            hit = jnp.logical_and(hit, pl.program_id(axis) == step_of(grid[axis]))
        return hit

    def first():
        pl.when(at(lambda n: 0))(lambda: rider.first(*parts))

    def last():
        pl.when(at(lambda n: n - 1))(lambda: rider.last(*parts))

    return first, last


def _dn_scan_fwd(qkv, proj, a_log, dt_bias, gn, n_b, t_len, rider=None):
    n_c = t_len // CHUNK
    spec = functools.partial(_chunk_spec, n_b, n_c=n_c)
    r_inputs, r_in_specs, r_out_specs, r_sems, split = _with_rider(rider, 8, 3, 1)

    def body(*refs):
        (q_ref, k_ref, v_ref, z_ref, sm_ref, al_ref, dt_ref, gn_ref,
         o_ref, hist_ref, ainv_ref, s_ref), parts = split(refs)
        ride_first, ride_last = _ride(rider, parts, n_c)
        if rider is not None:
            ride_first()

        @pl.when(pl.program_id(0) == 0)
        def _():
            s_ref[...] = jnp.zeros_like(s_ref)

        s_in = s_ref[...]
        hist_ref[...] = s_in
        og, s_out, a_inv = _dn_chains(*(_stack_chains(r, n_b, DN_SL) for r in (q_ref, k_ref, v_ref, z_ref)),
                                      _per_chain(sm_ref, n_b), s_in, al_ref[...], dt_ref[...], gn_ref[...])
        _unstack_chains(o_ref, og, n_b, DN_SL)
        s_ref[...] = s_out
        ainv_ref[...] = a_inv
        if rider is not None:
            ride_last()

    qkv3, proj3 = qkv.reshape(n_b, t_len, -1), proj.reshape(n_b, t_len, -1)
    o, hist, ainv, *rider_outs = pl.pallas_call(
        body, name="dn_scan_fwd", grid=(n_c,),
        in_specs=[spec(512, 0), spec(512, 1), spec(512, 2), spec(512, OFF_Z // 512), spec(128, OFF_SMALL // 128),
                  _const_spec((1, 128)), _const_spec((1, 128)), _const_spec((1, 128))] + r_in_specs,
        out_specs=[spec(512, 0), _hist_spec(n_b, HEAD_DIM, n_c), _ainv_spec(n_b, n_c)] + r_out_specs,
        out_shape=[jax.ShapeDtypeStruct((n_b, t_len, 2 * 512), BF16),
                   jax.ShapeDtypeStruct((n_c, n_b * HEADS, HEAD_DIM, HEAD_DIM), F32),
                   jax.ShapeDtypeStruct((n_c, n_b * HEADS, CHUNK, CHUNK), F32)]
        + (list(rider.out_shapes) if rider else []),
        scratch_shapes=[pltpu.VMEM((n_b * HEADS, HEAD_DIM, HEAD_DIM), F32)] + r_sems,
        compiler_params=_cparams(("arbitrary",)),
    )(qkv3, qkv3, qkv3, proj3, proj3, a_log, dt_bias, gn, *r_inputs)
    return o, (hist, ainv), rider_outs


SCAN_BWD_W = OFF_Z + 512


def _scan_bwd(qkv, proj, dn_params, gla_params, hist_dn, hist_gla, d_o, n_b, t_len, rider=None):
    n_c = t_len // CHUNK
    rev = functools.partial(_chunk_spec, n_b, n_c=n_c, reverse=True)
    r_inputs, r_in_specs, r_out_specs, r_sems, split = _with_rider(rider, 19, 8, 2)
    (hist, ainv), do_gla_sl = hist_dn, [slice(512 + sl.start, 512 + sl.stop) for sl in DN_SL]

    def body(*refs):
        (q_ref, k_ref, v_ref, z_ref, sm_ref, gq_ref, gk_ref, gv_ref, gg_ref,
         al_ref, dt_ref, dgn_in_ref, w2_ref, b2_ref, ggn_in_ref, hist_ref, ainv_ref, ghist_ref, do_ref,
         dqkv_ref, dp_ref, dal_ref, ddt_ref, dgn_ref, dw2_ref, db2_ref, dggn_ref, ds_ref, gds_ref), parts = split(refs)
        ride_first, ride_last = _ride(rider, parts, n_c)
        if rider is not None:
            ride_first()
        first = pl.program_id(0) == 0

        @pl.when(first)
        def _():
            ds_ref[...] = jnp.zeros_like(ds_ref)
            gds_ref[...] = jnp.zeros_like(gds_ref)

        small = _per_chain(sm_ref, n_b)
        chains = lambda *a: _dn_chains(*a, a_saved=ainv_ref[...])[:2]
        _, pull = jax.vjp(chains, *(_stack_chains(r, n_b, DN_SL) for r in (q_ref, k_ref, v_ref, z_ref)),
                          small, hist_ref[...], al_ref[...], dt_ref[...], dgn_in_ref[...])
        dq, dk, dv, dz, dsm_dn, ds_in, dal, ddt, dgn = pull((_stack_chains(do_ref, n_b, DN_SL), ds_ref[...]))
        _, gpull = jax.vjp(_gla_chains, _stack_chains(gq_ref, n_b, GLA_KSL), _stack_chains(gk_ref, n_b, GLA_KSL),
                           _stack_chains(gv_ref, n_b, DN_SL), _stack_chains(gg_ref, n_b, DN_SL),
                           small, ghist_ref[...], *_gate_weights(w2_ref, b2_ref, n_b), ggn_in_ref[...])
        gq, gk, gv, gg, dsm_gla, gds_in, dw2, db2, dggn = gpull((_stack_chains(do_ref, n_b, do_gla_sl), gds_ref[...]))

        _unstack_chains(dqkv_ref, dq, n_b, DN_SL)
        _unstack_chains(dqkv_ref, dk, n_b, DN_SL, offset=512)
        _unstack_chains(dqkv_ref, dv, n_b, DN_SL, offset=1024)
        _unstack_chains(dp_ref, dz, n_b, DN_SL, offset=OFF_Z)
        _unstack_chains(dp_ref, gq, n_b, GLA_KSL, offset=OFF_GQ)
        _unstack_chains(dp_ref, gk, n_b, GLA_KSL, offset=OFF_GK)
        _unstack_chains(dp_ref, gv, n_b, DN_SL, offset=OFF_GV)
        _unstack_chains(dp_ref, gg, n_b, DN_SL, offset=OFF_GG)
        ds_ref[...] = ds_in
        gds_ref[...] = gds_in
        for b, (s_dn, s_gla) in enumerate(zip(_sum_heads(dsm_dn, n_b), _sum_heads(dsm_gla, n_b))):
            dp_ref[b, :, OFF_SMALL:OFF_SMALL + 128] = (s_dn + s_gla).astype(BF16)
            dp_ref[b, :, OFF_SMALL + 128:GLA_W] = jnp.zeros((CHUNK, GLA_W - OFF_SMALL - 128), BF16)
        _acc(dal_ref, dal, first)
        _acc(ddt_ref, ddt, first)
        _acc(dgn_ref, dgn, first)
        for h, ks in enumerate(GLA_KSL):
            _acc(dw2_ref, sum(dw2[b * HEADS + h] for b in range(n_b)), first, at=(slice(None), ks))
            _acc(db2_ref, sum(db2[b * HEADS + h] for b in range(n_b)), first, at=(slice(None), ks))
        _acc(dggn_ref, dggn, first)
        if rider is not None:
            ride_last()

    qkv3, proj3, do3 = (a.reshape(n_b, t_len, -1) for a in (qkv, proj, d_o))
    vec = jax.ShapeDtypeStruct((1, 128), F32)
    dqkv, d_proj, dal, ddt, dgn, dw2, db2, dggn, *rider_outs = pl.pallas_call(
        body, name="scan_bwd", grid=(n_c,),
        in_specs=[rev(512, 0), rev(512, 1), rev(512, 2), rev(512, OFF_Z // 512), rev(128, OFF_SMALL // 128),
                  rev(256, OFF_GQ // 256), rev(256, OFF_GK // 256), rev(512, OFF_GV // 512), rev(512, OFF_GG // 512),
                  _const_spec((1, 128)), _const_spec((1, 128)), _const_spec((1, 128)),
                  _const_spec((128, 256)), _const_spec((1, 256)), _const_spec((1, 128)),
                  _hist_spec(n_b, HEAD_DIM, n_c, reverse=True), _ainv_spec(n_b, n_c, reverse=True),
                  _hist_spec(n_b, GLA_KEY, n_c, reverse=True), rev(2 * 512, 0)] + r_in_specs,
        out_specs=[rev(1536, 0), rev(SCAN_BWD_W, 0), _const_spec((1, 128)), _const_spec((1, 128)),
                   _const_spec((1, 128)), _const_spec((128, 256)), _const_spec((1, 256)), _const_spec((1, 128))]
        + r_out_specs,
        out_shape=[jax.ShapeDtypeStruct((n_b, t_len, 1536), F32), jax.ShapeDtypeStruct((n_b, t_len, PROJ_W), BF16),
                   vec, vec, vec, jax.ShapeDtypeStruct((128, 256), F32), jax.ShapeDtypeStruct((1, 256), F32), vec]
        + (list(rider.out_shapes) if rider else []),
        scratch_shapes=[pltpu.VMEM((n_b * HEADS, HEAD_DIM, HEAD_DIM), F32),
                        pltpu.VMEM((n_b * HEADS, GLA_KEY, HEAD_DIM), F32)] + r_sems,
        compiler_params=_cparams(("arbitrary",)),
    )(qkv3, qkv3, qkv3, proj3, proj3, proj3, proj3, proj3, proj3, *dn_params, *gla_params, hist, ainv, hist_gla, do3,
      *r_inputs)
    return dqkv.reshape(n_b * t_len, 1536), d_proj, (dal, ddt, dgn), (dw2, db2, dggn), rider_outs


def _gla_scan_fwd(proj, w2, b2, gn, o_mix, n_b, t_len):
    n_c = t_len // CHUNK
    spec = functools.partial(_chunk_spec, n_b, n_c=n_c)

    def body(q_ref, k_ref, v_ref, g_ref, sm_ref, w2_ref, b2_ref, gn_ref, _, o_ref, hist_ref, s_ref):
        @pl.when(pl.program_id(0) == 0)
        def _():
            s_ref[...] = jnp.zeros_like(s_ref)

        s_in = s_ref[...]
        hist_ref[...] = s_in
        og, s_out = _gla_chains(_stack_chains(q_ref, n_b, GLA_KSL), _stack_chains(k_ref, n_b, GLA_KSL),
                                _stack_chains(v_ref, n_b, DN_SL), _stack_chains(g_ref, n_b, DN_SL),
                                _per_chain(sm_ref, n_b), s_in, *_gate_weights(w2_ref, b2_ref, n_b), gn_ref[...])
        _unstack_chains(o_ref, og, n_b, DN_SL)
        s_ref[...] = s_out

    proj3 = proj.reshape(n_b, t_len, -1)
    o, hist = pl.pallas_call(
        body, name="gla_scan_fwd", grid=(n_c,),
        in_specs=[spec(256, OFF_GQ // 256), spec(256, OFF_GK // 256), spec(512, OFF_GV // 512),
                  spec(512, OFF_GG // 512), spec(128, OFF_SMALL // 128),
                  _const_spec((128, 256)), _const_spec((1, 256)), _const_spec((1, 128)),
                  pl.BlockSpec(memory_space=pl.ANY)],
        out_specs=[spec(512, 1), _hist_spec(n_b, GLA_KEY, n_c)],
        out_shape=[jax.ShapeDtypeStruct(o_mix.shape, BF16),
                   jax.ShapeDtypeStruct((n_c, n_b * HEADS, GLA_KEY, HEAD_DIM), F32)],
        input_output_aliases={8: 0},
        scratch_shapes=[pltpu.VMEM((n_b * HEADS, GLA_KEY, HEAD_DIM), F32)],
        compiler_params=_cparams(("arbitrary",)),
    )(proj3, proj3, proj3, proj3, proj3, w2, b2, gn, o_mix)
    return o.reshape(n_b * t_len, 2 * 512), hist


W_IN_RUNS = ((0, 256, GLA_W), (256, 1536, OFF_Z + 512), (1536, 2048, OFF_Z), (2048, 2056, OFF_SMALL),
             (2056, 3592, 0), (3592, 3608, OFF_SMALL + 8))
W_IN_ROWS = 256


def _w_in_pieces(cols_per_chip):
    out = []
    for first, last, start in W_IN_RUNS:
        for j in range(N_CHIPS):
            a, b = max(first, cols_per_chip * j), min(last, cols_per_chip * (j + 1))
            if a < b:
                out.append((j, a - cols_per_chip * j, b - cols_per_chip * j, start + a - first))
    return out


def _w_in_to_padded(w4):
    _, n_r, n_c = w4.shape

    def body(i_ref, o_ref):
        o_ref[...] = jnp.zeros_like(o_ref)
        for j, a, b, p in _w_in_pieces(n_c):
            o_ref[:, p:p + b - a] = i_ref[j, :, a:b]

    return pl.pallas_call(
        body, name="w_in_to_padded", grid=(n_r // W_IN_ROWS,),
        in_specs=[pl.BlockSpec((N_CHIPS, W_IN_ROWS, n_c), lambda i: (0, i, 0))],
        out_specs=pl.BlockSpec((W_IN_ROWS, PROJ_W), lambda i: (i, 0)),
        out_shape=jax.ShapeDtypeStruct((n_r, PROJ_W), w4.dtype), compiler_params=_cparams(("parallel",)),
    )(w4)


def _w_in_to_chips(g, n_c):
    n_r = g.shape[0]

    def body(i_ref, o_ref):
        for j, a, b, p in _w_in_pieces(n_c):
            o_ref[j, :, a:b] = i_ref[:, p:p + b - a]

    return pl.pallas_call(
        body, name="w_in_to_chips", grid=(n_r // W_IN_ROWS,),
        in_specs=[pl.BlockSpec((W_IN_ROWS, PROJ_W), lambda i: (i, 0))],
        out_specs=pl.BlockSpec((N_CHIPS, W_IN_ROWS, n_c), lambda i: (0, i, 0)),
        out_shape=jax.ShapeDtypeStruct((N_CHIPS, n_r, n_c), g.dtype), compiler_params=_cparams(("parallel",)),
    )(g)


def _lane_vec(v, offset=0):
    return jnp.zeros((1, 128), F32).at[0, offset:offset + v.shape[0]].set(v)


def _local_step(x, tgt, mod, p, n_b, t_len, comm=None):
    row1 = lambda v: v.reshape(1, -1)
    a_log, dt_bias = _lane_vec(p["dn_a_log"]), _lane_vec(p["dn_dt_bias"])
    dn_gn, gla_gn = row1(p["dn_norm_g"]), row1(p["gla_norm_g"])
    w2 = jnp.zeros((128, 256), F32).at[8:8 + GATE_RANK].set(p["gla_w_gate2"])
    b2 = row1(p["gla_b_gate"])
    ln0_g, ln0_b, ln1_g, ln1_b, ln2_g, ln2_b = (row1(p[k]) for k in ("ln0_g", "ln0_b", "ln1_g", "ln1_b", "ln2_g", "ln2_b"))
    conv_b = row1(p["ffn_conv_b"])

    x0, h1 = _ln0_fwd(x, ln0_g, ln0_b, mod, n_b, t_len)
    proj = _mm(h1, p["w_in_p"], name="mm_proj")
    qkv = _dn_pre_fwd(proj, p["dn_conv"], n_b, t_len)
    o_half, hist_dn, landed = _dn_scan_fwd(qkv, proj, a_log, dt_bias, dn_gn, n_b, t_len,
                                           rider=comm.fwd_rider() if comm else None)
    if comm:
        p = {**p, **comm.weights_from(landed)}
    o_mix, hist_gla = _gla_scan_fwd(proj, w2, b2, gla_gn, o_half, n_b, t_len)
    y = _mm(o_mix, p["w_o"], name="mm_wo")
    x1, h2 = _ln1_fwd(x0, y, ln1_g, ln1_b, mod, n_b, t_len)
    up = _mm(h2, p["w_up"], name="mm_up")
    act = _ffn_act_fwd(up, p["ffn_conv"], conv_b, n_b, t_len)
    y2 = _mm(act, p["w_down"], name="mm_down")

    loss, dx1, dy2, g_ln2_g, g_ln2_b, dgt_f = _ln2_loss_bwd(x1, y2, ln2_g, ln2_b, mod, tgt, n_b, t_len)
    g_w_down = _mm(act, dy2, ta=True, name="mm_g_down")
    d_act = _mm(dy2, p["w_down"], tb=True, name="mm_d_act")
    d_up, g_ffn_conv, g_conv_b = _ffn_act_bwd(up, p["ffn_conv"], conv_b, d_act, n_b, t_len)
    g_w_up = _mm(h2, d_up, ta=True, out_slabs=N_CHIPS, name="mm_g_up")
    if comm:
        dh2, from_sibling = _mm(d_up, p["w_up"], tb=True, name="mm_d_h2", rider=comm.ffn_pair_rider(g_w_up, g_w_down))
    else:
        dh2 = _mm(d_up, p["w_up"], tb=True, name="mm_d_h2")
    dx0, dy, g_ln1_g, g_ln1_b, dmod_1 = _ln1_bwd(x0, y, ln1_g, ln1_b, mod, dx1, dh2, n_b, t_len)
    g_w_o = _mm(o_mix, dy, ta=True, name="mm_g_wo")
    d_o = _mm(dy, p["w_o"], tb=True, name="mm_d_o")
    dqkv, d_proj, (g_a_log, g_dt_bias, g_dn_gn), (g_w2, g_b2, g_gla_gn), ffn_from_chips = _scan_bwd(
        qkv, proj, (a_log, dt_bias, dn_gn), (w2, b2, gla_gn), hist_dn, hist_gla, d_o, n_b, t_len,
        rider=comm.ffn_chips_rider(from_sibling) if comm else None)
    d_proj, g_dn_conv = _dn_pre_bwd(proj, p["dn_conv"], dqkv, d_proj.reshape(n_b * t_len, PROJ_W), n_b, t_len)
    g_w_in_p = _mm(h1, d_proj, ta=True, name="mm_g_win")
    if comm:
        dh1, tail_from_chips = _mm(d_proj, p["w_in_p"], tb=True, name="mm_d_h1",
                                   rider=comm.tail_chips_rider(g_w_in_p, g_w_o))
        from_chips = (ffn_from_chips, tail_from_chips)
    else:
        dh1, from_chips = _mm(d_proj, p["w_in_p"], tb=True, name="mm_d_h1"), None
    grad_x, g_ln0_g, g_ln0_b, dmod_0 = _ln0_bwd(x, ln0_g, ln0_b, mod, dx0, dh1, n_b, t_len)

    dmod = jnp.concatenate([dmod_0, dmod_1[:, 0:1], dmod_1[:, 1:3], dgt_f], axis=1)
    grads = {
        "ln0_g": g_ln0_g[0], "ln0_b": g_ln0_b[0], "w_in_p": g_w_in_p, "dn_conv": g_dn_conv,
        "dn_a_log": g_a_log[0, 0:HEADS], "dn_dt_bias": g_dt_bias[0, 0:HEADS], "dn_norm_g": g_dn_gn[0],
        "gla_w_gate2": g_w2[8:8 + GATE_RANK], "gla_b_gate": g_b2[0], "gla_norm_g": g_gla_gn[0],
        "w_o": g_w_o, "ln1_g": g_ln1_g[0], "ln1_b": g_ln1_b[0], "w_up": g_w_up,
        "ffn_conv": jnp.concatenate([g_ffn_conv[0], g_ffn_conv[1]], axis=1),
        "ffn_conv_b": jnp.concatenate([g_conv_b[0, 0], g_conv_b[1, 0]]), "w_down": g_w_down,
        "ln2_g": g_ln2_g[0], "ln2_b": g_ln2_b[0],
    }
    return loss, grad_x, grads, dmod, from_chips


def _col_sum(a):
    def body(a_ref, o_ref):
        o_ref[...] = jnp.sum(a_ref[...], 0, keepdims=True)

    return pl.pallas_call(body, name="col_sum", out_shape=jax.ShapeDtypeStruct((1, a.shape[1]), F32))(a)


def _adamw_math(w, grad, m, v):
    new_m = ADAM_B1 * m + (1.0 - ADAM_B1) * grad
    new_v = ADAM_B2 * v + (1.0 - ADAM_B2) * (grad * grad)
    m_hat = new_m / (1.0 - ADAM_B1 ** ADAM_STEP)
    v_hat = new_v / (1.0 - ADAM_B2 ** ADAM_STEP)
    return -ADAM_LR * (m_hat / (jnp.sqrt(v_hat) + ADAM_EPS) + ADAM_WD * w), new_m, new_v


def _adamw_many(ws, gs, ms, vs):
    n = len(ws)

    def body(*refs):
        for i in range(n):
            w_ref, g_ref, m_ref, v_ref = (refs[k * n + i] for k in range(4))
            d_ref, nm_ref, nv_ref = (refs[(4 + k) * n + i] for k in range(3))
            d_ref[...], nm_ref[...], nv_ref[...] = _adamw_math(w_ref[...], g_ref[...], m_ref[...], v_ref[...])

    outs = pl.pallas_call(
        body, name="adamw_small", out_shape=[jax.ShapeDtypeStruct(w.shape, F32) for w in ws] * 3,
    )(*ws, *gs, *ms, *vs)
    return outs[:n], outs[n:2 * n], outs[2 * n:]


def _adamw(w, g, m, v, name):
    n_r, n_c = w.shape
    if n_r % 8 == 0:
        tr = _pick(n_r, (256, 64, 32, 16, 8))
        grid, blk = (n_r // tr,), pl.BlockSpec((tr, n_c), lambda i: (i, 0))
    else:
        tc = _pick(n_c, (256, 128))
        grid, blk = (n_c // tc,), pl.BlockSpec((n_r, tc), lambda i: (0, i))

    def body(w_ref, g_ref, m_ref, v_ref, d_ref, nm_ref, nv_ref):
        d_ref[...], nm_ref[...], nv_ref[...] = _adamw_math(w_ref[...], g_ref[...], m_ref[...], v_ref[...])

    out = jax.ShapeDtypeStruct(w.shape, F32)
    return pl.pallas_call(
        body, name=name, grid=grid, in_specs=[blk] * 4, out_specs=[blk] * 3, out_shape=[out] * 3,
        compiler_params=_cparams(("parallel",)),
    )(w, g, m, v)


HBM_SPEC = pl.BlockSpec(memory_space=pltpu.HBM)
VMEM_SPEC = pl.BlockSpec(memory_space=pltpu.VMEM)
CHIP_FLIPS = ((1, 0), (0, 1), (1, 1))


def _place():
    return lax.axis_index("x"), lax.axis_index("y"), lax.axis_index("c")


def _flip(v, f):
    return 1 - v if f else v


def _all_gather8(slab, name):
    n_r, n_w = slab.shape

    def body(x_ref, o_ref, s_ref, send_sems, recv_sems, local_sem):
        x, y, c = _place()
        me = 4 * x + 2 * y + c
        mine = pltpu.make_async_copy(x_ref, o_ref.at[me], local_sem)
        mine.start()
        peers = [(_flip(x, k & 4), _flip(y, k & 2), _flip(c, k & 1)) for k in range(1, N_DEV)]
        sends = []
        for k, peer in enumerate(peers):
            cp = pltpu.make_async_remote_copy(src_ref=x_ref, dst_ref=o_ref.at[me], send_sem=send_sems.at[k],
                                              recv_sem=recv_sems.at[k], device_id=peer, device_id_type=MESH)
            cp.start()
            sends.append(cp)
        for k, (px, py, pc) in enumerate(peers):
            pltpu.make_async_remote_copy(src_ref=x_ref, dst_ref=o_ref.at[4 * px + 2 * py + pc],
                                         send_sem=send_sems.at[k], recv_sem=recv_sems.at[k],
                                         device_id=(px, py, pc), device_id_type=MESH).wait_recv()
        for cp in sends:
            cp.wait_send()
        mine.wait()
        total = o_ref[0]
        for d in range(1, N_DEV):
            total = total + o_ref[d]
        s_ref[...] = total

    return pl.pallas_call(
        body, name=name, in_specs=[VMEM_SPEC], out_specs=[VMEM_SPEC, VMEM_SPEC],
        out_shape=[jax.ShapeDtypeStruct((N_DEV, n_r, n_w), F32), jax.ShapeDtypeStruct((n_r, n_w), F32)],
        scratch_shapes=[pltpu.SemaphoreType.DMA((N_DEV - 1,)), pltpu.SemaphoreType.DMA((N_DEV - 1,)),
                        pltpu.SemaphoreType.DMA],
    )(slab)


SEQ_ROWS = 8


def _prologue(slab, w_ada_shard, b_shard, rider):
    n_r, n_w = slab.shape
    n_col = w_ada_shard.shape[1]
    r_inputs, r_in_specs, r_out_specs, r_sems, split = _with_rider(rider, 3, 3, 6)

    def body(*refs):
        (x_ref, w_ref, b_ref, g_ref, cond_ref, modr_ref, modp_ref, s1, r1, s2, r2, lsem), parts = split(refs)
        rider.first(*parts)
        x, y, c = _place()
        me = 4 * x + 2 * y + c
        peers = [(_flip(x, k & 4), _flip(y, k & 2), _flip(c, k & 1)) for k in range(1, N_DEV)]
        ids = [4 * px + 2 * py + pc for px, py, pc in peers]

        def exchange(src_of, dst, send_sems, recv_sems, own_sem):
            mine = pltpu.make_async_copy(src_of(me), dst.at[me], own_sem)
            mine.start()
            sends = [pltpu.make_async_remote_copy(src_ref=src_of(ids[k]), dst_ref=dst.at[me], send_sem=send_sems.at[k],
                                                  recv_sem=recv_sems.at[k], device_id=peers[k], device_id_type=MESH)
                     for k in range(N_DEV - 1)]
            for cp in sends:
                cp.start()
            for k in range(N_DEV - 1):
                pltpu.make_async_remote_copy(src_ref=src_of(ids[k]), dst_ref=dst.at[ids[k]], send_sem=send_sems.at[k],
                                             recv_sem=recv_sems.at[k], device_id=peers[k],
                                             device_id_type=MESH).wait_recv()
            for cp in sends:
                cp.wait_send()
            mine.wait()

        exchange(lambda d: x_ref, g_ref, s1, r1, lsem.at[0])
        cond = _silu(g_ref[:, 0:SEQ_ROWS, :].reshape(N_DEV * SEQ_ROWS, n_w))
        cond_ref[...] = cond
        modp_ref[...] = jnp.dot(cond.astype(BF16), w_ref[...].astype(BF16), preferred_element_type=F32) + b_ref[...]
        exchange(lambda d: modp_ref.at[pl.ds(pl.multiple_of(d * SEQ_ROWS, SEQ_ROWS), SEQ_ROWS)], modr_ref, s2, r2,
                 lsem.at[1])
        rider.last(*parts)

    sem7 = pltpu.SemaphoreType.DMA((N_DEV - 1,))
    gathered, cond, mod_recv, *rider_outs = pl.pallas_call(
        body, name="prologue", in_specs=[VMEM_SPEC] * 3 + r_in_specs, out_specs=[VMEM_SPEC] * 3 + r_out_specs,
        out_shape=[jax.ShapeDtypeStruct((N_DEV, n_r, n_w), F32), jax.ShapeDtypeStruct((N_DEV * SEQ_ROWS, n_w), F32),
                   jax.ShapeDtypeStruct((N_DEV, SEQ_ROWS, n_col), F32)] + list(rider.out_shapes),
        scratch_shapes=[pltpu.VMEM((N_DEV * SEQ_ROWS, n_col), F32), sem7, sem7, sem7, sem7,
                        pltpu.SemaphoreType.DMA((2,))] + r_sems,
        compiler_params=pltpu.CompilerParams(vmem_limit_bytes=VMEM_LIMIT),
    )(slab, w_ada_shard, b_shard, *r_inputs)
    return gathered, cond, mod_recv, rider_outs


def _gather_rider(shards):
    n_a = len(shards)

    def plan(ins, outs, sems):
        send_sems, recv_sems = sems
        x, y, c = _place()
        chips = [(_flip(x, fx), _flip(y, fy)) for fx, fy in CHIP_FLIPS]

        def copy(k, slot, chip_of_block, half, to, src=None):
            dst = outs[k].at[chip_of_block, half]
            return pltpu.make_async_remote_copy(src_ref=dst if src is None else src, dst_ref=dst,
                                                send_sem=send_sems.at[k * 6 + slot], recv_sem=recv_sems.at[k * 6 + slot],
                                                device_id=to, device_id_type=MESH)

        first = [copy(k, r, 2 * x + y, c, (*chips[r], c), src=ins[k].at[c]) for k in range(n_a) for r in range(3)]
        return copy, chips, first, (x, y, c)

    def first_step(ins, outs, sems):
        for cp in plan(ins, outs, sems)[2]:
            cp.start()

    def last_step(ins, outs, sems):
        copy, chips, first, (x, y, c) = plan(ins, outs, sems)
        passed = []
        for k in range(n_a):
            for r, (px, py) in enumerate(chips):
                copy(k, r, 2 * px + py, c, (x, y, c)).wait_recv()
                fwd = copy(k, 3 + r, 2 * px + py, c, (x, y, 1 - c))
                fwd.start()
                passed.append(fwd)
        for k in range(n_a):
            for r, (px, py) in enumerate(chips):
                copy(k, 3 + r, 2 * px + py, 1 - c, (x, y, c)).wait_recv()
        for cp in first + passed:
            cp.wait_send()

    return Rider(shards, [jax.ShapeDtypeStruct((N_CHIPS,) + s.shape, s.dtype) for s in shards],
                 [pltpu.SemaphoreType.DMA((6 * n_a,)), pltpu.SemaphoreType.DMA((6 * n_a,))], first_step, last_step)


def _place_own(gathered, shard, chip, name):
    _, _, n_h, n_c = gathered.shape
    th = _pick(n_h, (256, 176, 128))

    def body(sel_ref, s_ref, _, o_ref):
        o_ref[...] = s_ref[...]

    grid_spec = pltpu.PrefetchScalarGridSpec(
        num_scalar_prefetch=1, grid=(2, n_h // th),
        in_specs=[pl.BlockSpec((None, th, n_c), lambda hf, i, sel: (hf, i, 0)), pl.BlockSpec(memory_space=pl.ANY)],
        out_specs=pl.BlockSpec((None, None, th, n_c), lambda hf, i, sel: (sel[0], hf, i, 0)))
    return pl.pallas_call(
        body, name=name, grid_spec=grid_spec, out_shape=jax.ShapeDtypeStruct(gathered.shape, gathered.dtype),
        input_output_aliases={2: 0}, compiler_params=_cparams(("parallel", "parallel")),
    )(chip.reshape(1), shard, gathered)


def _pair_rider(parts):
    n_a = len(parts)

    def plan(ins, outs, sems):
        send_sems, recv_sems = sems
        x, y, c = _place()
        return [pltpu.make_async_remote_copy(src_ref=ins[k].at[:, 1 - c], dst_ref=outs[k], send_sem=send_sems.at[k],
                                             recv_sem=recv_sems.at[k], device_id=(x, y, 1 - c), device_id_type=MESH)
                for k in range(n_a)]

    def first_step(ins, outs, sems):
        for cp in plan(ins, outs, sems):
            cp.start()

    def last_step(ins, outs, sems):
        for cp in plan(ins, outs, sems):
            cp.wait()

    return Rider(parts, [jax.ShapeDtypeStruct((N_CHIPS,) + p.shape[2:], F32) for p in parts],
                 [pltpu.SemaphoreType.DMA((n_a,)), pltpu.SemaphoreType.DMA((n_a,))], first_step, last_step)


def _alone(rider, name):
    n_a = len(rider.inputs)

    def body(*refs):
        parts = (refs[:n_a], refs[n_a:2 * n_a], refs[2 * n_a:])
        rider.first(*parts)
        rider.last(*parts)

    return pl.pallas_call(
        body, name=name, in_specs=[HBM_SPEC] * n_a, out_specs=[HBM_SPEC] * n_a,
        out_shape=rider.out_shapes, scratch_shapes=rider.sems,
    )(*rider.inputs)


def _chips_rider(sums):
    n_a = len(sums)

    def plan(ins, outs, sems):
        send_sems, recv_sems = sems
        x, y, c = _place()
        cps = []
        for k in range(n_a):
            for r, (fx, fy) in enumerate(CHIP_FLIPS):
                px, py = _flip(x, fx), _flip(y, fy)
                cps.append(pltpu.make_async_remote_copy(
                    src_ref=ins[k].at[2 * px + py], dst_ref=outs[k].at[r], send_sem=send_sems.at[3 * k + r],
                    recv_sem=recv_sems.at[3 * k + r], device_id=(px, py, c), device_id_type=MESH))
        return cps

    def first_step(ins, outs, sems):
        for cp in plan(ins, outs, sems):
            cp.start()

    def last_step(ins, outs, sems):
        for cp in plan(ins, outs, sems):
            cp.wait()

    return Rider(sums, [jax.ShapeDtypeStruct((3,) + s.shape[1:], s.dtype) for s in sums],
                 [pltpu.SemaphoreType.DMA((3 * n_a,)), pltpu.SemaphoreType.DMA((3 * n_a,))], first_step, last_step)


def _rs_share(bufs):
    n_a = len(bufs)

    def body(*refs):
        ins, outs = refs[:n_a], refs[n_a:2 * n_a]
        send_sems, recv_sems = refs[2 * n_a:]
        x, y, c = _place()
        sends = [pltpu.make_async_remote_copy(src_ref=ins[k].at[c], dst_ref=outs[k].at[c], send_sem=send_sems.at[k],
                                              recv_sem=recv_sems.at[k], device_id=(x, y, 1 - c), device_id_type=MESH)
                 for k in range(n_a)]
        for cp in sends:
            cp.start()
        for k in range(n_a):
            pltpu.make_async_remote_copy(src_ref=ins[k].at[c], dst_ref=outs[k].at[1 - c], send_sem=send_sems.at[k],
                                         recv_sem=recv_sems.at[k], device_id=(x, y, 1 - c),
                                         device_id_type=MESH).wait_recv()
        for cp in sends:
            cp.wait_send()

    return pl.pallas_call(
        body, name="rs_share", in_specs=[HBM_SPEC] * n_a, out_specs=[HBM_SPEC] * n_a,
        out_shape=[jax.ShapeDtypeStruct(s.shape, F32) for s in bufs],
        input_output_aliases={k: k for k in range(n_a)},
        scratch_shapes=[pltpu.SemaphoreType.DMA((n_a,)), pltpu.SemaphoreType.DMA((n_a,))],
    )(*bufs)


def _pair_add(part, recv, core, name):
    _, _, n_h, n_c = part.shape
    th = _pick(n_h, (256, 176, 128))

    def body(sel_ref, p_ref, r_ref, o_ref):
        o_ref[...] = (p_ref[...] + r_ref[...]).astype(BF16)

    grid_spec = pltpu.PrefetchScalarGridSpec(
        num_scalar_prefetch=1, grid=(N_CHIPS, n_h // th),
        in_specs=[pl.BlockSpec((None, None, th, n_c), lambda j, i, sel: (j, sel[0], i, 0)),
                  pl.BlockSpec((None, th, n_c), lambda j, i, sel: (j, i, 0))],
        out_specs=pl.BlockSpec((None, th, n_c), lambda j, i, sel: (j, i, 0)))
    return pl.pallas_call(
        body, name=name, grid_spec=grid_spec, out_shape=jax.ShapeDtypeStruct(recv.shape, BF16),
        compiler_params=_cparams(("parallel", "parallel")),
    )(core.reshape(1), part, recv)


def _chip_add(sums, recv, chip, core, name):
    _, n_h, n_c = sums.shape
    th = _pick(n_h, (256, 176, 128))

    def body(sel_ref, s_ref, r_ref, o_ref):
        total = s_ref[...].astype(F32)
        for r in range(3):
            total = total + r_ref[r].astype(F32)
        o_ref[...] = total

    grid_spec = pltpu.PrefetchScalarGridSpec(
        num_scalar_prefetch=1, grid=(n_h // th,),
        in_specs=[pl.BlockSpec((None, th, n_c), lambda i, sel: (sel[0], i, 0)),
                  pl.BlockSpec((3, th, n_c), lambda i, sel: (0, i, 0))],
        out_specs=pl.BlockSpec((None, th, n_c), lambda i, sel: (sel[1], i, 0)))
    return pl.pallas_call(
        body, name=name, grid_spec=grid_spec, out_shape=jax.ShapeDtypeStruct((2, n_h, n_c), F32),
        compiler_params=_cparams(("parallel",)),
    )(jnp.stack([chip, core]), sums, recv)


def _row_halves(a):
    return a.reshape(N_CHIPS, 2, -1, a.shape[-1])


class StepComm:
    REST = ("w_o", "w_up", "w_down")

    def __init__(self, core, chip, rest_shards, in_cols):
        self.core, self.chip, self.shards, self.in_cols = core, chip, rest_shards, in_cols

    def fwd_rider(self):
        return _gather_rider(self.shards)

    def weights_from(self, landed):
        g_o, g_up, g_down = (_place_own(g, s, self.chip, "place_own_" + n)
                             for g, s, n in zip(landed, self.shards, self.REST))
        return {"w_o": g_o.reshape(-1, D_MODEL), "w_up": g_up.reshape(N_CHIPS, -1, g_up.shape[-1]),
                "w_down": g_down.reshape(-1, D_MODEL)}

    def _add_pairs(self, parts, from_sibling, names):
        return [_pair_add(p, r, self.core, "pair_add_" + n) for p, r, n in zip(parts, from_sibling, names)]

    def ffn_pair_rider(self, g_w_up, g_w_down):
        self.ffn_parts = [_row_halves(g_w_up), _row_halves(g_w_down)]
        return _pair_rider(self.ffn_parts)

    def ffn_chips_rider(self, from_sibling):
        self.ffn_sums = self._add_pairs(self.ffn_parts, from_sibling, ("w_up", "w_down"))
        return _chips_rider(self.ffn_sums)

    def tail_chips_rider(self, g_w_in_p, g_w_o):
        parts = [_row_halves(_w_in_to_chips(g_w_in_p, self.in_cols)), _row_halves(g_w_o)]
        self.tail_sums = self._add_pairs(parts, _alone(_pair_rider(parts), "rs_pair_tail"), ("w_in", "w_o"))
        return _chips_rider(self.tail_sums)

    def finish(self, ffn_from_chips, tail_from_chips):
        halves = [_chip_add(s, r, self.chip, self.core, "chip_add_" + n)
                  for s, r, n in zip(self.tail_sums + self.ffn_sums, list(tail_from_chips) + list(ffn_from_chips),
                                     ("w_in", "w_o", "w_up", "w_down"))]
        return [f.reshape(-1, f.shape[-1]) for f in _rs_share(halves)]


SLAB_W = 1024


def _pack(arrays, rows):
    flat = jnp.concatenate([a.reshape(-1).astype(F32) for a in arrays])
    return jnp.pad(flat, (0, rows * SLAB_W - flat.shape[0])).reshape(rows, SLAB_W)


def _unpack(flat, shapes):
    out, off = [], 0
    for s in shapes:
        n = 1
        for d in s:
            n *= d
        out.append(flat[off:off + n].reshape(s))
        off += n
    return out


def _rows_for(arrays_or_shapes):
    n = 0
    for a in arrays_or_shapes:
        s = a if isinstance(a, tuple) else a.shape
        k = 1
        for d in s:
            k *= d
        n += k
    return -(-n // (8 * SLAB_W)) * 8


def kernel(x, c, ln0_g, ln0_b, w_ada, b_ada, w_in, dn_conv, dn_a_log, dn_dt_bias, dn_norm_g, gla_w_gate2, gla_b_gate, gla_norm_g, w_o, ln1_g, ln1_b, ffn_w_up, ffn_conv, ffn_conv_b, ffn_w_down, ln2_g, ln2_b, loss_target, m_ln0_g, m_ln0_b, m_w_ada, m_b_ada, m_w_in, m_dn_conv, m_dn_a_log, m_dn_dt_bias, m_dn_norm_g, m_gla_w_gate2, m_gla_b_gate, m_gla_norm_g, m_w_o, m_ln1_g, m_ln1_b, m_ffn_w_up, m_ffn_conv, m_ffn_conv_b, m_ffn_w_down, m_ln2_g, m_ln2_b, v_ln0_g, v_ln0_b, v_w_ada, v_b_ada, v_w_in, v_dn_conv, v_dn_a_log, v_dn_dt_bias, v_dn_norm_g, v_gla_w_gate2, v_gla_b_gate, v_gla_norm_g, v_w_o, v_ln1_g, v_ln1_b, v_ffn_w_up, v_ffn_conv, v_ffn_conv_b, v_ffn_w_down, v_ln2_g, v_ln2_b):
    n_b, t_len, _ = x.shape
    xi, yi, ci = _place()
    chip = (2 * xi + yi).astype(jnp.int32)
    core = ci.astype(jnp.int32)
    n_all = N_DEV * n_b
    ada_cols = w_ada.shape[2]

    halves = lambda a: a.astype(BF16).reshape(2, a.shape[0] // 2, a.shape[1])
    w_in_halves = halves(w_in[0])
    sharded_small = [dn_conv[0], gla_w_gate2[0], ffn_conv[0]]
    slab = jnp.concatenate([_pack([c], SEQ_ROWS), _pack(sharded_small, _rows_for(sharded_small))], axis=0)
    b_ada_shard = lax.dynamic_slice(b_ada, (0, chip * ada_cols), (1, ada_cols))
    gathered, cond_pad, mod_recv, (g_in,) = _prologue(slab, w_ada[0], b_ada_shard, _gather_rider([w_in_halves]))
    g_in = _place_own(g_in, w_in_halves, chip, "place_own_w_in")
    cond_all = cond_pad.reshape(N_DEV, SEQ_ROWS, D_MODEL)[:, :n_b].reshape(n_all, D_MODEL)
    by_chip = gathered.reshape(N_DEV, -1)[0::2]
    full, off = [], SEQ_ROWS * SLAB_W
    for a in sharded_small:
        blocks = by_chip[:, off:off + a.size].reshape(N_CHIPS, *a.shape)
        full.append(blocks.transpose(1, 0, 2).reshape(a.shape[0], N_CHIPS * a.shape[1]))
        off += a.size
    dn_conv_f, gate2_f, ffn_conv_f = full
    mod = mod_recv[0::2, :n_b].transpose(1, 0, 2).reshape(n_b, 6, D_MODEL)

    comm = StepComm(core, chip, [halves(w_o[0]), halves(ffn_w_up[0]), halves(ffn_w_down[0])], w_in.shape[2])
    params = {
        "w_in_p": _w_in_to_padded(g_in.reshape(N_CHIPS, -1, g_in.shape[-1])),
        "dn_conv": dn_conv_f, "dn_a_log": dn_a_log[0], "dn_dt_bias": dn_dt_bias[0], "dn_norm_g": dn_norm_g[0],
        "gla_w_gate2": gate2_f, "gla_b_gate": gla_b_gate[0], "gla_norm_g": gla_norm_g[0],
        "ln0_g": ln0_g, "ln0_b": ln0_b, "ln1_g": ln1_g[0], "ln1_b": ln1_b[0], "ln2_g": ln2_g[0], "ln2_b": ln2_b[0],
        "ffn_conv": ffn_conv_f, "ffn_conv_b": ffn_conv_b[0],
    }

    loss_row, grad_x, gp, dmod, from_chips = _local_step(
        x.reshape(n_b * t_len, D_MODEL), loss_target.reshape(n_b * t_len, D_MODEL), mod, params, n_b, t_len, comm)
    names = ["ln0_g", "ln0_b", "w_ada", "b_ada", "w_in", "dn_conv", "dn_a_log", "dn_dt_bias", "dn_norm_g",
             "gla_w_gate2", "gla_b_gate", "gla_norm_g", "w_o", "ln1_g", "ln1_b", "ffn_w_up", "ffn_conv", "ffn_conv_b",
             "ffn_w_down", "ln2_g", "ln2_b"]
    weights = dict(zip(names, [ln0_g, ln0_b, w_ada, b_ada, w_in, dn_conv, dn_a_log, dn_dt_bias, dn_norm_g, gla_w_gate2,
                               gla_b_gate, gla_norm_g, w_o, ln1_g, ln1_b, ffn_w_up, ffn_conv, ffn_conv_b, ffn_w_down,
                               ln2_g, ln2_b]))
    m_in = dict(zip(names, [m_ln0_g, m_ln0_b, m_w_ada, m_b_ada, m_w_in, m_dn_conv, m_dn_a_log, m_dn_dt_bias,
                            m_dn_norm_g, m_gla_w_gate2, m_gla_b_gate, m_gla_norm_g, m_w_o, m_ln1_g, m_ln1_b,
                            m_ffn_w_up, m_ffn_conv, m_ffn_conv_b, m_ffn_w_down, m_ln2_g, m_ln2_b]))
    v_in = dict(zip(names, [v_ln0_g, v_ln0_b, v_w_ada, v_b_ada, v_w_in, v_dn_conv, v_dn_a_log, v_dn_dt_bias,
                            v_dn_norm_g, v_gla_w_gate2, v_gla_b_gate, v_gla_norm_g, v_w_o, v_ln1_g, v_ln1_b,
                            v_ffn_w_up, v_ffn_conv, v_ffn_conv_b, v_ffn_w_down, v_ln2_g, v_ln2_b]))
    grads, delta, new_m, new_v = {}, {}, {}, {}

    def adamw_big(n, grad):
        view = (lambda a: a.T) if n == "w_in" else (lambda a: a)
        outs = _adamw(view(weights[n][0]), view(grad), view(m_in[n][0]), view(v_in[n][0]), "adamw_" + n)
        grads[n] = grad[None]
        delta[n], new_m[n], new_v[n] = (view(a)[None] for a in outs)

    g_w_in, g_w_o, g_w_up, g_w_down = comm.finish(*from_chips)

    summed_names = ["loss", "ln0_g", "ln0_b", "dn_conv", "dn_a_log", "dn_dt_bias", "dn_norm_g", "gla_w_gate2",
                    "gla_b_gate", "gla_norm_g", "ln1_g", "ln1_b", "ffn_conv", "ffn_conv_b", "ln2_g", "ln2_b"]
    summed_parts = [loss_row[0, 0:1]] + [gp[n] for n in summed_names[1:]]
    sum_rows = _rows_for(summed_parts)
    slab = jnp.concatenate([_pack(summed_parts, sum_rows), _pack([dmod], _rows_for([dmod]))], axis=0)
    gathered, total = _all_gather8(slab, "reduce_small")
    small_g = dict(zip(summed_names, _unpack(total.reshape(-1), [a.shape for a in summed_parts])))
    loss = small_g["loss"][0]
    dmod_rows = n_b * 6 * D_MODEL // SLAB_W
    dmod_all = gathered[:, sum_rows:sum_rows + dmod_rows, :].reshape(n_all, 6 * D_MODEL)
    for n, grad in (("ffn_w_up", g_w_up), ("ffn_w_down", g_w_down), ("w_o", g_w_o), ("w_in", g_w_in)):
        adamw_big(n, grad)

    g_b_ada = _col_sum(dmod_all)
    dmod_cols = lax.dynamic_slice(dmod_all, (0, chip * ada_cols), (n_all, ada_cols))
    adamw_big("w_ada", _mm(cond_all, dmod_cols, ta=True, name="mm_g_ada"))

    col_block = lambda a: lax.dynamic_slice(a, (0, chip * (a.shape[1] // N_CHIPS)), (a.shape[0], a.shape[1] // N_CHIPS))
    grads.update({
        "ln0_g": small_g["ln0_g"], "ln0_b": small_g["ln0_b"], "b_ada": g_b_ada,
        "dn_conv": col_block(small_g["dn_conv"])[None], "dn_a_log": small_g["dn_a_log"][None],
        "dn_dt_bias": small_g["dn_dt_bias"][None], "dn_norm_g": small_g["dn_norm_g"][None],
        "gla_w_gate2": col_block(small_g["gla_w_gate2"])[None], "gla_b_gate": small_g["gla_b_gate"][None],
        "gla_norm_g": small_g["gla_norm_g"][None], "ln1_g": small_g["ln1_g"][None],
        "ln1_b": small_g["ln1_b"][None], "ffn_conv": col_block(small_g["ffn_conv"])[None],
        "ffn_conv_b": small_g["ffn_conv_b"][None], "ln2_g": small_g["ln2_g"][None], "ln2_b": small_g["ln2_b"][None],
    })
    small = [n for n in names if n not in delta]
    d_s, m_s, v_s = _adamw_many([weights[n] for n in small], [grads[n] for n in small],
                                [m_in[n] for n in small], [v_in[n] for n in small])
    for out, vals in ((delta, d_s), (new_m, m_s), (new_v, v_s)):
        out.update(zip(small, vals))

    return (loss, grad_x.reshape(x.shape), *[grads[n] for n in names], *[delta[n] for n in names],
            *[new_m[n] for n in names], *[new_v[n] for n in names])
```

```python
import functools

import jax
import jax.numpy as jnp
from jax import lax
from jax.experimental import pallas as pl
from jax.experimental.pallas import tpu as pltpu

F32 = jnp.float32
BF16 = jnp.bfloat16
MESH = pl.DeviceIdType.MESH

D_MODEL = 1024
HEADS = 4
HEAD_DIM = 128
GLA_KEY = 64
GATE_RANK = 16
CHUNK = 64
D_FF = 2816
ALPHA = 2.0 ** 0.25
EPS = 1e-6
N_CHIPS = 4
N_DEV = 8

PROJ_W = 3840
OFF_GQ, OFF_GK, OFF_GV, OFF_GG, OFF_SMALL, GLA_W = 0, 256, 512, 1024, 1536, 1792
OFF_Z = 2048
W_IN_COLS = 3608


def _qkv_block(j):
    return jnp.where(j < 2, GLA_W // 128 + j, (OFF_Z + 512) // 128 - 2 + j)

ADAM_LR, ADAM_B1, ADAM_B2, ADAM_EPS, ADAM_WD, ADAM_STEP = 0.001, 0.9, 0.999, 1e-08, 0.01, 10

VMEM_LIMIT = 56 * 1024 * 1024
ROW_TILE = 512


def _cparams(sem):
    return pltpu.CompilerParams(dimension_semantics=sem, vmem_limit_bytes=VMEM_LIMIT)


def _pick(n, prefs):
    for p in prefs:
        if n % p == 0:
            return p
    return n


def _mm(a, b, *, ta=False, tb=False, out_slabs=1, out_dtype=F32, name, rider=None):
    a_slabs = a.shape[0] if a.ndim == 3 else 1
    b_slabs = b.shape[0] if b.ndim == 3 else 1
    assert not (ta and a_slabs > 1)
    a2, b2 = a.shape[-2:], b.shape[-2:]
    if ta:
        k_dim, m_dim = a2
    else:
        m_dim, k_dim = a2[0], a2[1] * a_slabs
    n_dim = b2[0] if tb else b2[1] * b_slabs
    k_slabs = max(a_slabs, b_slabs if tb else 1)
    n_slabs = max(out_slabs, 1 if tb else b_slabs)
    tm = _pick(m_dim, (1024, 1408, 512, 256, 128))
    tn = _pick(n_dim // n_slabs, (1536, 1408, 1280, 1024, 768, 512, 384, 256, 128))
    tk = _pick(k_dim // k_slabs, (1408, 1280, 1024, 512, 256, 128))
    nk, nj = k_dim // tk, n_dim // tn
    nk_a, nk_b, nj_b, nj_o = nk // a_slabs, nk // b_slabs, nj // b_slabs, nj // out_slabs
    dims = (((0 if ta else 1,), (1 if tb else 0,)), ((), ()))

    grid = (m_dim // tm, nj, nk)
    assert out_dtype == F32
    r_inputs, r_in_specs, r_out_specs, r_sems, split = _with_rider(rider, 2, 1, 0)

    def body(*refs):
        (a_ref, b_ref, o_ref), parts = split(refs)
        ride_first, ride_last = _ride(rider, parts, grid)
        if rider is not None:
            ride_first()
        prod = lax.dot_general(a_ref[...].astype(BF16), b_ref[...].astype(BF16), dims, preferred_element_type=F32)
        if nk == 1:
            o_ref[...] = prod
        else:
            _acc(o_ref, prod, pl.program_id(2) == 0)
        if rider is not None:
            ride_last()

    if ta:
        a_spec = pl.BlockSpec((tk, tm), lambda i, j, k: (k, i))
    elif a_slabs > 1:
        a_spec = pl.BlockSpec((None, tm, tk), lambda i, j, k: (k // nk_a, i, k % nk_a))
    else:
        a_spec = pl.BlockSpec((tm, tk), lambda i, j, k: (i, k))
    if tb and b_slabs > 1:
        b_spec = pl.BlockSpec((None, tn, tk), lambda i, j, k: (k // nk_b, j, k % nk_b))
    elif tb:
        b_spec = pl.BlockSpec((tn, tk), lambda i, j, k: (j, k))
    elif b_slabs > 1:
        b_spec = pl.BlockSpec((None, tk, tn), lambda i, j, k: (j // nj_b, k, j % nj_b))
    else:
        b_spec = pl.BlockSpec((tk, tn), lambda i, j, k: (k, j))
    if out_slabs > 1:
        o_spec = pl.BlockSpec((None, tm, tn), lambda i, j, k: (j // nj_o, i, j % nj_o))
        o_shape = (out_slabs, m_dim, n_dim // out_slabs)
    else:
        o_spec, o_shape = pl.BlockSpec((tm, tn), lambda i, j, k: (i, j)), (m_dim, n_dim)
    out, *rider_outs = pl.pallas_call(
        body, name=name, grid=grid,
        in_specs=[a_spec, b_spec] + r_in_specs, out_specs=[o_spec] + r_out_specs,
        out_shape=[jax.ShapeDtypeStruct(o_shape, out_dtype)] + (list(rider.out_shapes) if rider else []),
        scratch_shapes=r_sems,
        compiler_params=_cparams(("arbitrary",) * 3 if rider else ("parallel", "parallel", "arbitrary")),
    )(a, b, *r_inputs)
    return (out, rider_outs) if rider else out


def _ln(x, g, b):
    mu = jnp.mean(x, -1, keepdims=True)
    xc = x - mu
    var = jnp.mean(xc * xc, -1, keepdims=True)
    return xc * lax.rsqrt(var + EPS) * g + b


def _softplus(x):
    return jnp.maximum(x, 0.0) + jnp.log(1.0 + jnp.exp(-jnp.abs(x)))


def _silu(x):
    return x * jax.nn.sigmoid(x)


def _dsilu(x):
    s = jax.nn.sigmoid(x)
    return s * (1.0 + x * (1.0 - s))


def _f_ln0(x, g, b, sc, sh):
    x0 = _ln(x, g, b)
    return x0, x0 * (1.0 + sc) + sh


def _f_ln1(x0, y, gt, g, b, sc, sh):
    x1 = _ln(ALPHA * x0 + (1.0 + gt) * y, g, b)
    return x1, x1 * (1.0 + sc) + sh


def _f_ln2_loss(x1, y2, gt, g, b, tgt):
    x2 = _ln(ALPHA * x1 + (1.0 + gt) * y2, g, b)
    err = x2 - tgt
    per_row = jnp.sum(err * err, -1, keepdims=True) * (0.5 / D_MODEL)
    return jnp.sum(per_row, 0, keepdims=True)


def _row_specs(t_len):
    nt = t_len // ROW_TILE
    row = pl.BlockSpec((ROW_TILE, D_MODEL), lambda b, i: (b * nt + i, 0))
    vec = pl.BlockSpec((1, D_MODEL), lambda b, i: (0, 0))
    mod = pl.BlockSpec((None, 6, D_MODEL), lambda b, i: (b, 0, 0))
    return nt, row, vec, mod


def _first_step():
    return jnp.logical_and(pl.program_id(0) == 0, pl.program_id(1) == 0)


def _acc(ref, val, first, at=(Ellipsis,)):
    @pl.when(first)
    def _():
        ref[at] = val

    @pl.when(jnp.logical_not(first))
    def _():
        ref[at] += val


def _acc_rows(ref, rows, first):
    for i, r in enumerate(rows):
        _acc(ref, r, first, at=(slice(i, i + 1), slice(None)))


def _ln0_fwd(x, g, b, mod, n_b, t_len):
    nt, row, vec, mods = _row_specs(t_len)

    def body(x_ref, g_ref, b_ref, mod_ref, x0_ref, h_ref):
        x0, h = _f_ln0(x_ref[...], g_ref[...], b_ref[...], mod_ref[1:2, :], mod_ref[0:1, :])
        x0_ref[...] = x0
        h_ref[...] = h.astype(BF16)

    return pl.pallas_call(
        body, name="ln0_fwd", grid=(n_b, nt), in_specs=[row, vec, vec, mods], out_specs=[row, row],
        out_shape=[jax.ShapeDtypeStruct(x.shape, F32), jax.ShapeDtypeStruct(x.shape, BF16)],
        compiler_params=_cparams(("parallel", "parallel")),
    )(x, g, b, mod)


def _ln0_bwd(x, g, b, mod, dx0, dh, n_b, t_len):
    nt, row, vec, mods = _row_specs(t_len)
    dmod_spec = pl.BlockSpec((None, 2, D_MODEL), lambda bb, i: (bb, 0, 0))

    def body(x_ref, g_ref, b_ref, mod_ref, dx0_ref, dh_ref, dx_ref, dg_ref, db_ref, dmod_ref):
        _, pull = jax.vjp(_f_ln0, x_ref[...], g_ref[...], b_ref[...], mod_ref[1:2, :], mod_ref[0:1, :])
        dx, dg, db, dsc, dsh = pull((dx0_ref[...], dh_ref[...]))
        dx_ref[...] = dx
        _acc(dg_ref, dg, _first_step())
        _acc(db_ref, db, _first_step())
        _acc_rows(dmod_ref, [dsh, dsc], pl.program_id(1) == 0)

    return pl.pallas_call(
        body, name="ln0_bwd", grid=(n_b, nt), in_specs=[row, vec, vec, mods, row, row],
        out_specs=[row, vec, vec, dmod_spec],
        out_shape=[jax.ShapeDtypeStruct(x.shape, F32), jax.ShapeDtypeStruct((1, D_MODEL), F32),
                   jax.ShapeDtypeStruct((1, D_MODEL), F32), jax.ShapeDtypeStruct((n_b, 2, D_MODEL), F32)],
        compiler_params=_cparams(("arbitrary", "arbitrary")),
    )(x, g, b, mod, dx0, dh)


def _ln1_fwd(x0, y, g, b, mod, n_b, t_len):
    nt, row, vec, mods = _row_specs(t_len)

    def body(x0_ref, y_ref, g_ref, b_ref, mod_ref, x1_ref, h_ref):
        x1, h = _f_ln1(x0_ref[...], y_ref[...], mod_ref[2:3, :], g_ref[...], b_ref[...],
                       mod_ref[4:5, :], mod_ref[3:4, :])
        x1_ref[...] = x1
        h_ref[...] = h.astype(BF16)

    return pl.pallas_call(
        body, name="ln1_fwd", grid=(n_b, nt), in_specs=[row, row, vec, vec, mods], out_specs=[row, row],
        out_shape=[jax.ShapeDtypeStruct(x0.shape, F32), jax.ShapeDtypeStruct(x0.shape, BF16)],
        compiler_params=_cparams(("parallel", "parallel")),
    )(x0, y, g, b, mod)


def _ln1_bwd(x0, y, g, b, mod, dx1, dh, n_b, t_len):
    nt, row, vec, mods = _row_specs(t_len)
    dmod_spec = pl.BlockSpec((None, 3, D_MODEL), lambda bb, i: (bb, 0, 0))

    def body(x0_ref, y_ref, g_ref, b_ref, mod_ref, dx1_ref, dh_ref, dx0_ref, dy_ref, dg_ref, db_ref, dmod_ref):
        _, pull = jax.vjp(_f_ln1, x0_ref[...], y_ref[...], mod_ref[2:3, :], g_ref[...], b_ref[...],
                          mod_ref[4:5, :], mod_ref[3:4, :])
        dx0, dy, dgt, dg, db, dsc, dsh = pull((dx1_ref[...], dh_ref[...]))
        dx0_ref[...] = dx0
        dy_ref[...] = dy.astype(BF16)
        _acc(dg_ref, dg, _first_step())
        _acc(db_ref, db, _first_step())
        _acc_rows(dmod_ref, [dgt, dsh, dsc], pl.program_id(1) == 0)

    return pl.pallas_call(
        body, name="ln1_bwd", grid=(n_b, nt), in_specs=[row, row, vec, vec, mods, row, row],
        out_specs=[row, row, vec, vec, dmod_spec],
        out_shape=[jax.ShapeDtypeStruct(x0.shape, F32), jax.ShapeDtypeStruct(x0.shape, BF16),
                   jax.ShapeDtypeStruct((1, D_MODEL), F32), jax.ShapeDtypeStruct((1, D_MODEL), F32),
                   jax.ShapeDtypeStruct((n_b, 3, D_MODEL), F32)],
        compiler_params=_cparams(("arbitrary", "arbitrary")),
    )(x0, y, g, b, mod, dx1, dh)


def _ln2_loss_bwd(x1, y2, g, b, mod, tgt, n_b, t_len):
    nt, row, vec, mods = _row_specs(t_len)
    one = pl.BlockSpec((1, 128), lambda bb, i: (0, 0))
    dmod_spec = pl.BlockSpec((None, 1, D_MODEL), lambda bb, i: (bb, 0, 0))

    def body(x1_ref, y2_ref, g_ref, b_ref, mod_ref, t_ref, loss_ref, dx1_ref, dy2_ref, dg_ref, db_ref, dgt_ref):
        loss, pull = jax.vjp(functools.partial(_f_ln2_loss, tgt=t_ref[...]), x1_ref[...], y2_ref[...],
                             mod_ref[5:6, :], g_ref[...], b_ref[...])
        dx1, dy2, dgt, dg, db = pull(jnp.ones((1, 1), F32))
        dx1_ref[...] = dx1
        dy2_ref[...] = dy2.astype(BF16)
        _acc(loss_ref, jnp.broadcast_to(loss, (1, 128)), _first_step())
        _acc(dg_ref, dg, _first_step())
        _acc(db_ref, db, _first_step())
        _acc(dgt_ref, dgt, pl.program_id(1) == 0)

    return pl.pallas_call(
        body, name="ln2_loss_bwd", grid=(n_b, nt), in_specs=[row, row, vec, vec, mods, row],
        out_specs=[one, row, row, vec, vec, dmod_spec],
        out_shape=[jax.ShapeDtypeStruct((1, 128), F32), jax.ShapeDtypeStruct(x1.shape, F32),
                   jax.ShapeDtypeStruct(x1.shape, BF16), jax.ShapeDtypeStruct((1, D_MODEL), F32),
                   jax.ShapeDtypeStruct((1, D_MODEL), F32), jax.ShapeDtypeStruct((n_b, 1, D_MODEL), F32)],
        compiler_params=_cparams(("arbitrary", "arbitrary")),
    )(x1, y2, g, b, mod, tgt)


def _shift_down(x, s):
    if s == 0:
        return x
    rows = lax.broadcasted_iota(jnp.int32, x.shape, 0)
    return jnp.where(rows >= s, pltpu.roll(x, s, 0), 0.0)


def _shift_up(x, s):
    if s == 0:
        return x
    t_len = x.shape[0]
    rows = lax.broadcasted_iota(jnp.int32, x.shape, 0)
    return jnp.where(rows < t_len - s, pltpu.roll(x, t_len - s, 0), 0.0)


def _taps(x, k_w):
    return [_shift_down(x, k_w - 1 - k) for k in range(k_w)]


def _conv(taps, w):
    out = w[0:1, :] * taps[0]
    for k in range(1, len(taps)):
        out = out + w[k:k + 1, :] * taps[k]
    return out


def _conv_bwd(taps, w, du):
    k_w = len(taps)
    dx = w[k_w - 1:k_w, :] * du
    for k in range(k_w - 1):
        dx = dx + w[k:k + 1, :] * _shift_up(du, k_w - 1 - k)
    return dx, [jnp.sum(du * taps[k], 0, keepdims=True) for k in range(k_w)]


def _dn_pre_fwd(proj, conv_w, n_b, t_len):
    n_ct = 3 * HEADS
    k_w = conv_w.shape[0]

    def body(x_ref, w_ref, o_ref):
        o_ref[...] = _silu(_conv(_taps(x_ref[...], k_w), w_ref[...]))

    return pl.pallas_call(
        body, name="dn_pre_fwd", grid=(n_ct, n_b),
        in_specs=[pl.BlockSpec((t_len, 128), lambda j, b: (b, _qkv_block(j))),
                  pl.BlockSpec((k_w, 128), lambda j, b: (0, j))],
        out_specs=pl.BlockSpec((t_len, 128), lambda j, b: (b, j)),
        out_shape=jax.ShapeDtypeStruct((n_b * t_len, n_ct * 128), F32),
        compiler_params=_cparams(("parallel", "parallel")),
    )(proj, conv_w)


def _dn_pre_bwd(proj, conv_w, dqkv, d_proj, n_b, t_len):
    n_ct = 3 * HEADS
    k_w = conv_w.shape[0]

    def body(x_ref, w_ref, d_ref, _, dx_ref, dw_ref):
        taps, w = _taps(x_ref[...], k_w), w_ref[...]
        du = d_ref[...] * _dsilu(_conv(taps, w))
        dx, dw = _conv_bwd(taps, w, du)
        dx_ref[...] = dx.astype(BF16)
        _acc_rows(dw_ref, dw, pl.program_id(1) == 0)

    return pl.pallas_call(
        body, name="dn_pre_bwd", grid=(n_ct, n_b),
        in_specs=[pl.BlockSpec((t_len, 128), lambda j, b: (b, _qkv_block(j))),
                  pl.BlockSpec((k_w, 128), lambda j, b: (0, j)),
                  pl.BlockSpec((t_len, 128), lambda j, b: (b, j)), pl.BlockSpec(memory_space=pl.ANY)],
        out_specs=[pl.BlockSpec((t_len, 128), lambda j, b: (b, _qkv_block(j))),
                   pl.BlockSpec((k_w, 128), lambda j, b: (0, j))],
        out_shape=[jax.ShapeDtypeStruct(d_proj.shape, BF16), jax.ShapeDtypeStruct((k_w, n_ct * 128), F32)],
        input_output_aliases={3: 0},
        compiler_params=_cparams(("parallel", "arbitrary")),
    )(proj, conv_w, dqkv, d_proj)


FFN_TC = 256
FFN_NT = D_FF // FFN_TC


def _ffn_specs(t_len):
    blk = lambda off: pl.BlockSpec((t_len, FFN_TC), lambda j, b: (b, j + off))
    wblk = lambda off: pl.BlockSpec((3, FFN_TC), lambda j, b: (0, j + off))
    bblk = lambda off: pl.BlockSpec((1, FFN_TC), lambda j, b: (0, j + off))
    return [blk(0), blk(FFN_NT), wblk(0), wblk(FFN_NT), bblk(0), bblk(FFN_NT)]


def _ffn_act_fwd(up, conv_w, conv_b, n_b, t_len):
    def body(g_ref, v_ref, wg_ref, wv_ref, bg_ref, bv_ref, o_ref):
        ug = _conv(_taps(g_ref[...], 3), wg_ref[...]) + bg_ref[...]
        uv = _conv(_taps(v_ref[...], 3), wv_ref[...]) + bv_ref[...]
        o_ref[...] = (_silu(ug) * uv).astype(BF16)

    return pl.pallas_call(
        body, name="ffn_act_fwd", grid=(FFN_NT, n_b), in_specs=_ffn_specs(t_len),
        out_specs=pl.BlockSpec((t_len, FFN_TC), lambda j, b: (b, j)),
        out_shape=jax.ShapeDtypeStruct((n_b * t_len, D_FF), BF16),
        compiler_params=_cparams(("parallel", "parallel")),
    )(up, up, conv_w, conv_w, conv_b, conv_b)


def _ffn_act_bwd(up, conv_w, conv_b, da, n_b, t_len):
    def body(g_ref, v_ref, wg_ref, wv_ref, bg_ref, bv_ref, da_ref, dup_ref, dw_ref, db_ref):
        first = pl.program_id(1) == 0
        tg, tv, wg, wv = _taps(g_ref[...], 3), _taps(v_ref[...], 3), wg_ref[...], wv_ref[...]
        ug = _conv(tg, wg) + bg_ref[...]
        uv = _conv(tv, wv) + bv_ref[...]
        d_act = da_ref[...]
        sig = jax.nn.sigmoid(ug)
        d_v = d_act * (ug * sig)
        d_g = d_act * uv * (sig * (1.0 + ug * (1.0 - sig)))
        for slab, (taps, w, du) in enumerate(((tg, wg, d_g), (tv, wv, d_v))):
            dx, dw = _conv_bwd(taps, w, du)
            dup_ref[slab] = dx.astype(BF16)
            for k, dw_k in enumerate(dw):
                _acc(dw_ref, dw_k, first, at=(slab, slice(k, k + 1), slice(None)))
            _acc(db_ref, jnp.sum(du, 0, keepdims=True), first, at=(slab, slice(None), slice(None)))

    return pl.pallas_call(
        body, name="ffn_act_bwd", grid=(FFN_NT, n_b),
        in_specs=_ffn_specs(t_len) + [pl.BlockSpec((t_len, FFN_TC), lambda j, b: (b, j))],
        out_specs=[pl.BlockSpec((2, t_len, FFN_TC), lambda j, b: (0, b, j)),
                   pl.BlockSpec((2, 3, FFN_TC), lambda j, b: (0, 0, j)),
                   pl.BlockSpec((2, 1, FFN_TC), lambda j, b: (0, 0, j))],
        out_shape=[jax.ShapeDtypeStruct((2, n_b * t_len, D_FF), BF16),
                   jax.ShapeDtypeStruct((2, 3, D_FF), F32), jax.ShapeDtypeStruct((2, 1, D_FF), F32)],
        compiler_params=_cparams(("parallel", "arbitrary")),
    )(up, up, conv_w, conv_w, conv_b, conv_b, da)


NN = (((2,), (1,)), ((0,), (0,)))
NT = (((2,), (2,)), ((0,), (0,)))
TN = (((1,), (1,)), ((0,), (0,)))


def _iota3(shape, axis):
    return lax.broadcasted_iota(jnp.int32, shape, axis)


def _dg(a, b, dims):
    return lax.dot_general(a, b, dims, preferred_element_type=F32)


def _dot(a, b):
    return _dg(a, b, NN)


def _dot_nt(a, b):
    return _dg(a, b, NT)


def _dot_tn(a, b):
    return _dg(a, b, TN)


def _split(a):
    hi = a.astype(BF16)
    return hi, (a - hi.astype(F32)).astype(BF16)


def _dg3(a, b, dims):
    ah, al = _split(a)
    bh, bl = _split(b)
    return _dg(ah, bh, dims) + (_dg(ah, bl, dims) + _dg(al, bh, dims))


@jax.custom_vjp
def _dot3(a, b):
    return _dg3(a, b, NN)


def _dot3_fwd(a, b):
    return _dg3(a, b, NN), (a, b)


def _dot3_bwd(res, g):
    a, b = res
    return _dg3(g, b, NT), _dg3(a, g, TN)


_dot3.defvjp(_dot3_fwd, _dot3_bwd)


def _lower_ones(g_n, n):
    shape = (g_n, n, n)
    return jnp.where(_iota3(shape, 1) >= _iota3(shape, 2), 1.0, 0.0).astype(BF16)


@jax.custom_vjp
def _chunk_cumsum(x):
    hi, lo = _split(x)
    tri = _lower_ones(x.shape[0], x.shape[1])
    return _dg(tri, hi, NN) + _dg(tri, lo, NN)


def _chunk_cumsum_fwd(x):
    return _chunk_cumsum(x), None


def _chunk_cumsum_bwd(_, g):
    hi, lo = _split(g)
    tri = _lower_ones(g.shape[0], g.shape[1])
    return (_dg(tri, hi, TN) + _dg(tri, lo, TN),)


_chunk_cumsum.defvjp(_chunk_cumsum_fwd, _chunk_cumsum_bwd)


@jax.custom_vjp
def _unit_lower_inv(m):
    n = m.shape[1]
    p = -m
    a = jnp.where(_iota3(m.shape, 1) == _iota3(m.shape, 2), 1.0, 0.0) + p
    span = 2
    while span < n:
        p = _dg3(p, p, NN)
        a = a + _dg3(a, p, NN)
        span *= 2
    return a


def _unit_lower_inv_fwd(m):
    a = _unit_lower_inv(m)
    return a, a


def _unit_lower_inv_bwd(a, da):
    return (-_dg3(a, _dg3(da, a, NT), TN),)


_unit_lower_inv.defvjp(_unit_lower_inv_fwd, _unit_lower_inv_bwd)


@jax.custom_vjp
def _saved_lower_inv(m, a):
    return a


def _saved_lower_inv_fwd(m, a):
    return a, a


def _saved_lower_inv_bwd(a, da):
    return _unit_lower_inv_bwd(a, da)[0], jnp.zeros_like(a)


_saved_lower_inv.defvjp(_saved_lower_inv_fwd, _saved_lower_inv_bwd)


def _rms_gate(o, gn, gate):
    return o * lax.rsqrt(jnp.mean(o * o, -1, keepdims=True) + EPS) * gn * _silu(gate)


def _dn_chains(q, k, v, z, small, s_in, a_log, dt_bias, gn, a_saved=None):
    prep = _dn_prepare(q, k, v, small, a_log, dt_bias, a_saved)
    og, s_out = _dn_advance(prep[:-1], z, s_in, gn)
    return og, s_out, prep[-1]


def _dn_prepare(q, k, v, small, a_log, dt_bias, a_saved=None):
    g_n, c_len = q.shape[0], q.shape[1]
    sq = (g_n, c_len, c_len)
    row, col = _iota3(sq, 1), _iota3(sq, 2)
    causal, strict, eye = row >= col, row > col, row == col
    qn = q * lax.rsqrt(jnp.sum(q * q, -1, keepdims=True) + EPS) * (HEAD_DIM ** -0.5)
    kn = k * lax.rsqrt(jnp.sum(k * k, -1, keepdims=True) + EPS)
    lane = _iota3(small.shape, 2)
    head = jnp.bitwise_and(_iota3(small.shape, 0), HEADS - 1)
    la_all = -jnp.exp(a_log) * _softplus(small + dt_bias)
    la_c = jnp.sum(jnp.where(lane == head, la_all, 0.0), 2, keepdims=True)
    beta = jnp.sum(jnp.where(lane == head + HEADS, jax.nn.sigmoid(small), 0.0), 2, keepdims=True)
    la_b = jnp.broadcast_to(la_c, sq)
    la_r = jnp.sum(jnp.where(eye, la_b, 0.0), 1, keepdims=True)
    g_c = jnp.sum(jnp.where(causal, jnp.broadcast_to(la_r, sq), 0.0), 2, keepdims=True)
    g_r = jnp.sum(jnp.where(row <= col, la_b, 0.0), 1, keepdims=True)
    g_last = jnp.sum(la_c, 1, keepdims=True)
    decay = jnp.exp(jnp.where(causal, g_c - g_r, -1e30))
    e_g = jnp.exp(g_c)
    kb = kn * beta
    m_low = jnp.where(strict, _dot_nt(kb, kn) * decay, 0.0)
    a_inv = _unit_lower_inv(m_low) if a_saved is None else _saved_lower_inv(m_low, a_saved)
    u = _dot3(a_inv, v * beta)
    w = _dot3(a_inv, kb * e_g)
    attn = _dot_nt(qn, kn) * decay
    return u, w, attn, qn * e_g, kn * jnp.exp(g_last - g_c), jnp.exp(g_last), a_inv


def _dn_advance(prep, z, s_in, gn):
    u, w, attn, q_dec, k_dec, g_chunk = prep
    v_new = u - _dot(w, s_in)
    o = _dot(q_dec, s_in) + _dot(attn, v_new)
    s_out = s_in * g_chunk + _dot_tn(k_dec, v_new)
    return _rms_gate(o, gn, z), s_out


def _gla_chains(q, k, v, gate, small, s_in, w2, b2, gn):
    g_n, c_len = q.shape[0], q.shape[1]
    sq, kk = (g_n, c_len, c_len), (g_n, GLA_KEY, GLA_KEY)
    causal = _iota3(sq, 1) >= _iota3(sq, 2)
    la = -_softplus(-(_dot(small, w2) + b2)) * (1.0 / 16.0)
    b = _chunk_cumsum(la)
    b_last = jnp.sum(jnp.where(_iota3(b.shape, 1) == c_len - 1, b, 0.0), 1, keepdims=True)
    q_dec = q * (GLA_KEY ** -0.5) * jnp.exp(b)
    attn = jnp.where(causal, _dot_nt(q_dec, k * jnp.exp(-b)), 0.0)
    o = _dot(q_dec, s_in) + _dot(attn, v)
    g_row = jnp.exp(b_last)
    g_col = jnp.sum(jnp.where(_iota3(kk, 1) == _iota3(kk, 2), jnp.broadcast_to(g_row, kk), 0.0), 2, keepdims=True)
    s_out = s_in * g_col + _dot_tn(k * jnp.exp(b_last - b), v)
    return _rms_gate(o, gn, gate), s_out


def _chunk_spec(n_b, width, col_block, n_c, reverse=False):
    if reverse:
        return pl.BlockSpec((n_b, CHUNK, width), lambda n: (0, n_c - 1 - n, col_block))
    return pl.BlockSpec((n_b, CHUNK, width), lambda n: (0, n, col_block))


def _hist_spec(n_b, d_k, n_c, reverse=False):
    if reverse:
        return pl.BlockSpec((None, n_b * HEADS, d_k, HEAD_DIM), lambda n: (n_c - 1 - n, 0, 0, 0))
    return pl.BlockSpec((None, n_b * HEADS, d_k, HEAD_DIM), lambda n: (n, 0, 0, 0))


def _ainv_spec(n_b, n_c, reverse=False):
    if reverse:
        return pl.BlockSpec((None, n_b * HEADS, CHUNK, CHUNK), lambda n: (n_c - 1 - n, 0, 0, 0))
    return pl.BlockSpec((None, n_b * HEADS, CHUNK, CHUNK), lambda n: (n, 0, 0, 0))


def _stack_chains(ref, n_b, slices):
    return jnp.stack([ref[b, :, sl] for b in range(n_b) for sl in slices], axis=0)


def _per_chain(ref, n_b):
    return jnp.stack([ref[b] for b in range(n_b) for _ in range(HEADS)], axis=0)


def _unstack_chains(ref, val, n_b, slices, offset=0):
    for b in range(n_b):
        for h, sl in enumerate(slices):
            ref[b, :, slice(offset + sl.start, offset + sl.stop)] = val[b * HEADS + h].astype(ref.dtype)


def _gate_weights(w2_ref, b2_ref, n_b):
    w2 = jnp.stack([w2_ref[:, ks] for _ in range(n_b) for ks in GLA_KSL], axis=0)
    b2 = jnp.stack([b2_ref[:, ks] for _ in range(n_b) for ks in GLA_KSL], axis=0)
    return w2, b2


def _sum_heads(val, n_b):
    return [sum(val[b * HEADS + h] for h in range(HEADS)) for b in range(n_b)]


def _const_spec(shape):
    return pl.BlockSpec(shape, lambda n: (0,) * len(shape))


DN_SL = [slice(h * HEAD_DIM, (h + 1) * HEAD_DIM) for h in range(HEADS)]
GLA_KSL = [slice(h * GLA_KEY, (h + 1) * GLA_KEY) for h in range(HEADS)]


class Rider:
    def __init__(self, inputs, out_shapes, sems, first, last):
        self.inputs, self.out_shapes, self.sems, self.first, self.last = inputs, out_shapes, sems, first, last


def _with_rider(rider, n_in, n_out, n_scratch):
    if rider is None:
        return [], [], [], [], lambda refs: (refs, None)
    r_in, r_out, r_sem = len(rider.inputs), len(rider.out_shapes), len(rider.sems)

    def split(refs):
        own_in, rest = refs[:n_in], refs[n_in:]
        rid_in, rest = rest[:r_in], rest[r_in:]
        own_out, rest = rest[:n_out], rest[n_out:]
        rid_out, rest = rest[:r_out], rest[r_out:]
        own_scr, rid_sem = rest[:n_scratch], rest[n_scratch:]
        return own_in + own_out + own_scr, (rid_in, rid_out, rid_sem)

    return list(rider.inputs), [HBM_SPEC] * r_in, [HBM_SPEC] * r_out, list(rider.sems), split


def _ride(rider, parts, grid):
    if rider is None:
        return None, None
    grid = grid if isinstance(grid, tuple) else (grid,)

    def at(step_of):
        hit = pl.program_id(0) == step_of(grid[0])
        for axis in range(1, len(grid)):
            hit = jnp.logical_and(hit, pl.program_id(axis) == step_of(grid[axis]))
        return hit

    def first():
        pl.when(at(lambda n: 0))(lambda: rider.first(*parts))

    def last():
        pl.when(at(lambda n: n - 1))(lambda: rider.last(*parts))

    return first, last


FWD_CHUNKS = 4


def _dn_scan_fwd(qkv, proj, a_log, dt_bias, gn, n_b, t_len, rider=None):
    n_c, n_g, rows = t_len // CHUNK, n_b * HEADS, FWD_CHUNKS * CHUNK
    n_s = n_c // FWD_CHUNKS
    spec = lambda width, col_block: pl.BlockSpec((n_b, rows, width), lambda n: (0, n, col_block))
    kept = lambda d0, d1: pl.BlockSpec((FWD_CHUNKS, n_g, d0, d1), lambda n: (n, 0, 0, 0))
    r_inputs, r_in_specs, r_out_specs, r_sems, split = _with_rider(rider, 8, 3, 1)
    chunk_rows = [slice(j * CHUNK, (j + 1) * CHUNK) for j in range(FWD_CHUNKS)]

    def body(*refs):
        (q_ref, k_ref, v_ref, z_ref, sm_ref, al_ref, dt_ref, gn_ref,
         o_ref, hist_ref, ainv_ref, s_ref), parts = split(refs)
        ride_first, ride_last = _ride(rider, parts, n_s)
        if rider is not None:
            ride_first()

        @pl.when(pl.program_id(0) == 0)
        def _():
            s_ref[...] = jnp.zeros_like(s_ref)

        def stack(ref, slices):
            return jnp.stack([ref[b, rs, sl] for rs in chunk_rows for b in range(n_b) for sl in slices], axis=0)

        prep = _dn_prepare(stack(q_ref, DN_SL), stack(k_ref, DN_SL), stack(v_ref, DN_SL),
                           stack(sm_ref, [slice(None)] * HEADS), al_ref[...], dt_ref[...])
        z, state = stack(z_ref, DN_SL), s_ref[...]
        for j, rs in enumerate(chunk_rows):
            mine = slice(j * n_g, (j + 1) * n_g)
            hist_ref[j] = state
            ainv_ref[j] = prep[-1][mine]
            og, state = _dn_advance(tuple(a[mine] for a in prep[:-1]), z[mine], state, gn_ref[...])
            for b in range(n_b):
                for h, sl in enumerate(DN_SL):
                    o_ref[b, rs, sl] = og[b * HEADS + h].astype(BF16)
        s_ref[...] = state
        if rider is not None:
            ride_last()

    qkv3, proj3 = qkv.reshape(n_b, t_len, -1), proj.reshape(n_b, t_len, -1)
    o, hist, ainv, *rider_outs = pl.pallas_call(
        body, name="dn_scan_fwd", grid=(n_s,),
        in_specs=[spec(512, 0), spec(512, 1), spec(512, 2), spec(512, OFF_Z // 512), spec(128, OFF_SMALL // 128),
                  _const_spec((1, 128)), _const_spec((1, 128)), _const_spec((1, 128))] + r_in_specs,
        out_specs=[spec(512, 0), kept(HEAD_DIM, HEAD_DIM), kept(CHUNK, CHUNK)] + r_out_specs,
        out_shape=[jax.ShapeDtypeStruct((n_b, t_len, 2 * 512), BF16),
                   jax.ShapeDtypeStruct((n_c, n_b * HEADS, HEAD_DIM, HEAD_DIM), F32),
                   jax.ShapeDtypeStruct((n_c, n_b * HEADS, CHUNK, CHUNK), F32)]
        + (list(rider.out_shapes) if rider else []),
        scratch_shapes=[pltpu.VMEM((n_b * HEADS, HEAD_DIM, HEAD_DIM), F32)] + r_sems,
        compiler_params=_cparams(("arbitrary",)),
    )(qkv3, qkv3, qkv3, proj3, proj3, a_log, dt_bias, gn, *r_inputs)
    return o, (hist, ainv), rider_outs


SCAN_BWD_W = OFF_Z + 512


def _scan_bwd(qkv, proj, dn_params, gla_params, hist_dn, hist_gla, d_o, n_b, t_len, rider=None):
    n_c = t_len // CHUNK
    rev = functools.partial(_chunk_spec, n_b, n_c=n_c, reverse=True)
    r_inputs, r_in_specs, r_out_specs, r_sems, split = _with_rider(rider, 19, 8, 2)
    (hist, ainv), do_gla_sl = hist_dn, [slice(512 + sl.start, 512 + sl.stop) for sl in DN_SL]

    def body(*refs):
        (q_ref, k_ref, v_ref, z_ref, sm_ref, gq_ref, gk_ref, gv_ref, gg_ref,
         al_ref, dt_ref, dgn_in_ref, w2_ref, b2_ref, ggn_in_ref, hist_ref, ainv_ref, ghist_ref, do_ref,
         dqkv_ref, dp_ref, dal_ref, ddt_ref, dgn_ref, dw2_ref, db2_ref, dggn_ref, ds_ref, gds_ref), parts = split(refs)
        ride_first, ride_last = _ride(rider, parts, n_c)
        if rider is not None:
            ride_first()
        first = pl.program_id(0) == 0

        @pl.when(first)
        def _():
            ds_ref[...] = jnp.zeros_like(ds_ref)
            gds_ref[...] = jnp.zeros_like(gds_ref)

        small = _per_chain(sm_ref, n_b)
        chains = lambda *a: _dn_chains(*a, a_saved=ainv_ref[...])[:2]
        _, pull = jax.vjp(chains, *(_stack_chains(r, n_b, DN_SL) for r in (q_ref, k_ref, v_ref, z_ref)),
                          small, hist_ref[...], al_ref[...], dt_ref[...], dgn_in_ref[...])
        dq, dk, dv, dz, dsm_dn, ds_in, dal, ddt, dgn = pull((_stack_chains(do_ref, n_b, DN_SL), ds_ref[...]))
        _, gpull = jax.vjp(_gla_chains, _stack_chains(gq_ref, n_b, GLA_KSL), _stack_chains(gk_ref, n_b, GLA_KSL),
                           _stack_chains(gv_ref, n_b, DN_SL), _stack_chains(gg_ref, n_b, DN_SL),
                           small, ghist_ref[...], *_gate_weights(w2_ref, b2_ref, n_b), ggn_in_ref[...])
        gq, gk, gv, gg, dsm_gla, gds_in, dw2, db2, dggn = gpull((_stack_chains(do_ref, n_b, do_gla_sl), gds_ref[...]))

        _unstack_chains(dqkv_ref, dq, n_b, DN_SL)
        _unstack_chains(dqkv_ref, dk, n_b, DN_SL, offset=512)
        _unstack_chains(dqkv_ref, dv, n_b, DN_SL, offset=1024)
        _unstack_chains(dp_ref, dz, n_b, DN_SL, offset=OFF_Z)
        _unstack_chains(dp_ref, gq, n_b, GLA_KSL, offset=OFF_GQ)
        _unstack_chains(dp_ref, gk, n_b, GLA_KSL, offset=OFF_GK)
        _unstack_chains(dp_ref, gv, n_b, DN_SL, offset=OFF_GV)
        _unstack_chains(dp_ref, gg, n_b, DN_SL, offset=OFF_GG)
        ds_ref[...] = ds_in
        gds_ref[...] = gds_in
        for b, (s_dn, s_gla) in enumerate(zip(_sum_heads(dsm_dn, n_b), _sum_heads(dsm_gla, n_b))):
            dp_ref[b, :, OFF_SMALL:OFF_SMALL + 128] = (s_dn + s_gla).astype(BF16)
            dp_ref[b, :, OFF_SMALL + 128:GLA_W] = jnp.zeros((CHUNK, GLA_W - OFF_SMALL - 128), BF16)
        _acc(dal_ref, dal, first)
        _acc(ddt_ref, ddt, first)
        _acc(dgn_ref, dgn, first)
        for h, ks in enumerate(GLA_KSL):
            _acc(dw2_ref, sum(dw2[b * HEADS + h] for b in range(n_b)), first, at=(slice(None), ks))
            _acc(db2_ref, sum(db2[b * HEADS + h] for b in range(n_b)), first, at=(slice(None), ks))
        _acc(dggn_ref, dggn, first)
        if rider is not None:
            ride_last()

    qkv3, proj3, do3 = (a.reshape(n_b, t_len, -1) for a in (qkv, proj, d_o))
    vec = jax.ShapeDtypeStruct((1, 128), F32)
    dqkv, d_proj, dal, ddt, dgn, dw2, db2, dggn, *rider_outs = pl.pallas_call(
        body, name="scan_bwd", grid=(n_c,),
        in_specs=[rev(512, 0), rev(512, 1), rev(512, 2), rev(512, OFF_Z // 512), rev(128, OFF_SMALL // 128),
                  rev(256, OFF_GQ // 256), rev(256, OFF_GK // 256), rev(512, OFF_GV // 512), rev(512, OFF_GG // 512),
                  _const_spec((1, 128)), _const_spec((1, 128)), _const_spec((1, 128)),
                  _const_spec((128, 256)), _const_spec((1, 256)), _const_spec((1, 128)),
                  _hist_spec(n_b, HEAD_DIM, n_c, reverse=True), _ainv_spec(n_b, n_c, reverse=True),
                  _hist_spec(n_b, GLA_KEY, n_c, reverse=True), rev(2 * 512, 0)] + r_in_specs,
        out_specs=[rev(1536, 0), rev(SCAN_BWD_W, 0), _const_spec((1, 128)), _const_spec((1, 128)),
                   _const_spec((1, 128)), _const_spec((128, 256)), _const_spec((1, 256)), _const_spec((1, 128))]
        + r_out_specs,
        out_shape=[jax.ShapeDtypeStruct((n_b, t_len, 1536), F32), jax.ShapeDtypeStruct((n_b, t_len, PROJ_W), BF16),
                   vec, vec, vec, jax.ShapeDtypeStruct((128, 256), F32), jax.ShapeDtypeStruct((1, 256), F32), vec]
        + (list(rider.out_shapes) if rider else []),
        scratch_shapes=[pltpu.VMEM((n_b * HEADS, HEAD_DIM, HEAD_DIM), F32),
                        pltpu.VMEM((n_b * HEADS, GLA_KEY, HEAD_DIM), F32)] + r_sems,
        compiler_params=_cparams(("arbitrary",)),
    )(qkv3, qkv3, qkv3, proj3, proj3, proj3, proj3, proj3, proj3, *dn_params, *gla_params, hist, ainv, hist_gla, do3,
      *r_inputs)
    return dqkv.reshape(n_b * t_len, 1536), d_proj, (dal, ddt, dgn), (dw2, db2, dggn), rider_outs


def _gla_scan_fwd(proj, w2, b2, gn, o_mix, n_b, t_len):
    n_c = t_len // CHUNK
    spec = functools.partial(_chunk_spec, n_b, n_c=n_c)

    def body(q_ref, k_ref, v_ref, g_ref, sm_ref, w2_ref, b2_ref, gn_ref, _, o_ref, hist_ref, s_ref):
        @pl.when(pl.program_id(0) == 0)
        def _():
            s_ref[...] = jnp.zeros_like(s_ref)

        s_in = s_ref[...]
        hist_ref[...] = s_in
        og, s_out = _gla_chains(_stack_chains(q_ref, n_b, GLA_KSL), _stack_chains(k_ref, n_b, GLA_KSL),
                                _stack_chains(v_ref, n_b, DN_SL), _stack_chains(g_ref, n_b, DN_SL),
                                _per_chain(sm_ref, n_b), s_in, *_gate_weights(w2_ref, b2_ref, n_b), gn_ref[...])
        _unstack_chains(o_ref, og, n_b, DN_SL)
        s_ref[...] = s_out

    proj3 = proj.reshape(n_b, t_len, -1)
    o, hist = pl.pallas_call(
        body, name="gla_scan_fwd", grid=(n_c,),
        in_specs=[spec(256, OFF_GQ // 256), spec(256, OFF_GK // 256), spec(512, OFF_GV // 512),
                  spec(512, OFF_GG // 512), spec(128, OFF_SMALL // 128),
                  _const_spec((128, 256)), _const_spec((1, 256)), _const_spec((1, 128)),
                  pl.BlockSpec(memory_space=pl.ANY)],
        out_specs=[spec(512, 1), _hist_spec(n_b, GLA_KEY, n_c)],
        out_shape=[jax.ShapeDtypeStruct(o_mix.shape, BF16),
                   jax.ShapeDtypeStruct((n_c, n_b * HEADS, GLA_KEY, HEAD_DIM), F32)],
        input_output_aliases={8: 0},
        scratch_shapes=[pltpu.VMEM((n_b * HEADS, GLA_KEY, HEAD_DIM), F32)],
        compiler_params=_cparams(("arbitrary",)),
    )(proj3, proj3, proj3, proj3, proj3, w2, b2, gn, o_mix)
    return o.reshape(n_b * t_len, 2 * 512), hist


W_IN_RUNS = ((0, 256, GLA_W), (256, 1536, OFF_Z + 512), (1536, 2048, OFF_Z), (2048, 2056, OFF_SMALL),
             (2056, 3592, 0), (3592, 3608, OFF_SMALL + 8))
W_IN_ROWS = 256


def _w_in_pieces(cols_per_chip):
    out = []
    for first, last, start in W_IN_RUNS:
        for j in range(N_CHIPS):
            a, b = max(first, cols_per_chip * j), min(last, cols_per_chip * (j + 1))
            if a < b:
                out.append((j, a - cols_per_chip * j, b - cols_per_chip * j, start + a - first))
    return out


def _w_in_to_padded(w4):
    _, n_r, n_c = w4.shape

    def body(i_ref, o_ref):
        o_ref[...] = jnp.zeros_like(o_ref)
        for j, a, b, p in _w_in_pieces(n_c):
            o_ref[:, p:p + b - a] = i_ref[j, :, a:b]

    return pl.pallas_call(
        body, name="w_in_to_padded", grid=(n_r // W_IN_ROWS,),
        in_specs=[pl.BlockSpec((N_CHIPS, W_IN_ROWS, n_c), lambda i: (0, i, 0))],
        out_specs=pl.BlockSpec((W_IN_ROWS, PROJ_W), lambda i: (i, 0)),
        out_shape=jax.ShapeDtypeStruct((n_r, PROJ_W), w4.dtype), compiler_params=_cparams(("parallel",)),
    )(w4)


def _w_in_to_chips(g, n_c):
    n_r = g.shape[0]

    def body(i_ref, o_ref):
        for j, a, b, p in _w_in_pieces(n_c):
            o_ref[j, :, a:b] = i_ref[:, p:p + b - a]

    return pl.pallas_call(
        body, name="w_in_to_chips", grid=(n_r // W_IN_ROWS,),
        in_specs=[pl.BlockSpec((W_IN_ROWS, PROJ_W), lambda i: (i, 0))],
        out_specs=pl.BlockSpec((N_CHIPS, W_IN_ROWS, n_c), lambda i: (0, i, 0)),
        out_shape=jax.ShapeDtypeStruct((N_CHIPS, n_r, n_c), g.dtype), compiler_params=_cparams(("parallel",)),
    )(g)


def _lane_vec(v, offset=0):
    return jnp.zeros((1, 128), F32).at[0, offset:offset + v.shape[0]].set(v)


def _local_step(x, tgt, mod, p, n_b, t_len, comm=None):
    row1 = lambda v: v.reshape(1, -1)
    a_log, dt_bias = _lane_vec(p["dn_a_log"]), _lane_vec(p["dn_dt_bias"])
    dn_gn, gla_gn = row1(p["dn_norm_g"]), row1(p["gla_norm_g"])
    w2 = jnp.zeros((128, 256), F32).at[8:8 + GATE_RANK].set(p["gla_w_gate2"])
    b2 = row1(p["gla_b_gate"])
    ln0_g, ln0_b, ln1_g, ln1_b, ln2_g, ln2_b = (row1(p[k]) for k in ("ln0_g", "ln0_b", "ln1_g", "ln1_b", "ln2_g", "ln2_b"))
    conv_b = row1(p["ffn_conv_b"])

    x0, h1 = _ln0_fwd(x, ln0_g, ln0_b, mod, n_b, t_len)
    if comm:
        proj, landed_proj = _mm(h1, p["w_in_p"], name="mm_proj", rider=comm.proj_rider())
    else:
        proj = _mm(h1, p["w_in_p"], name="mm_proj")
    qkv = _dn_pre_fwd(proj, p["dn_conv"], n_b, t_len)
    o_half, hist_dn, landed_scan = _dn_scan_fwd(qkv, proj, a_log, dt_bias, dn_gn, n_b, t_len,
                                                rider=comm.scan_rider() if comm else None)
    if comm:
        p = {**p, **comm.weights_from(landed_proj, landed_scan)}
    o_mix, hist_gla = _gla_scan_fwd(proj, w2, b2, gla_gn, o_half, n_b, t_len)
    y = _mm(o_mix, p["w_o"], name="mm_wo")
    x1, h2 = _ln1_fwd(x0, y, ln1_g, ln1_b, mod, n_b, t_len)
    up = _mm(h2, p["w_up"], name="mm_up")
    act = _ffn_act_fwd(up, p["ffn_conv"], conv_b, n_b, t_len)
    y2 = _mm(act, p["w_down"], name="mm_down")

    loss, dx1, dy2, g_ln2_g, g_ln2_b, dgt_f = _ln2_loss_bwd(x1, y2, ln2_g, ln2_b, mod, tgt, n_b, t_len)
    g_w_down = _mm(act, dy2, ta=True, name="mm_g_down")
    d_act = _mm(dy2, p["w_down"], tb=True, name="mm_d_act")
    d_up, g_ffn_conv, g_conv_b = _ffn_act_bwd(up, p["ffn_conv"], conv_b, d_act, n_b, t_len)
    g_w_up = _mm(h2, d_up, ta=True, out_slabs=N_CHIPS, name="mm_g_up")
    if comm:
        dh2, from_sibling = _mm(d_up, p["w_up"], tb=True, name="mm_d_h2", rider=comm.ffn_pair_rider(g_w_up, g_w_down))
    else:
        dh2 = _mm(d_up, p["w_up"], tb=True, name="mm_d_h2")
    dx0, dy, g_ln1_g, g_ln1_b, dmod_1 = _ln1_bwd(x0, y, ln1_g, ln1_b, mod, dx1, dh2, n_b, t_len)
    g_w_o = _mm(o_mix, dy, ta=True, name="mm_g_wo")
    d_o = _mm(dy, p["w_o"], tb=True, name="mm_d_o")
    dqkv, d_proj, (g_a_log, g_dt_bias, g_dn_gn), (g_w2, g_b2, g_gla_gn), ffn_from_chips = _scan_bwd(
        qkv, proj, (a_log, dt_bias, dn_gn), (w2, b2, gla_gn), hist_dn, hist_gla, d_o, n_b, t_len,
        rider=comm.ffn_chips_rider(from_sibling) if comm else None)
    d_proj, g_dn_conv = _dn_pre_bwd(proj, p["dn_conv"], dqkv, d_proj.reshape(n_b * t_len, PROJ_W), n_b, t_len)
    g_w_in_p = _mm(h1, d_proj, ta=True, name="mm_g_win")
    if comm:
        dh1, tail_from_chips = _mm(d_proj, p["w_in_p"], tb=True, name="mm_d_h1",
                                   rider=comm.tail_chips_rider(g_w_in_p, g_w_o))
        from_chips = (ffn_from_chips, tail_from_chips)
    else:
        dh1, from_chips = _mm(d_proj, p["w_in_p"], tb=True, name="mm_d_h1"), None
    grad_x, g_ln0_g, g_ln0_b, dmod_0 = _ln0_bwd(x, ln0_g, ln0_b, mod, dx0, dh1, n_b, t_len)

    dmod = jnp.concatenate([dmod_0, dmod_1[:, 0:1], dmod_1[:, 1:3], dgt_f], axis=1)
    grads = {
        "ln0_g": g_ln0_g[0], "ln0_b": g_ln0_b[0], "w_in_p": g_w_in_p, "dn_conv": g_dn_conv,
        "dn_a_log": g_a_log[0, 0:HEADS], "dn_dt_bias": g_dt_bias[0, 0:HEADS], "dn_norm_g": g_dn_gn[0],
        "gla_w_gate2": g_w2[8:8 + GATE_RANK], "gla_b_gate": g_b2[0], "gla_norm_g": g_gla_gn[0],
        "w_o": g_w_o, "ln1_g": g_ln1_g[0], "ln1_b": g_ln1_b[0], "w_up": g_w_up,
        "ffn_conv": jnp.concatenate([g_ffn_conv[0], g_ffn_conv[1]], axis=1),
        "ffn_conv_b": jnp.concatenate([g_conv_b[0, 0], g_conv_b[1, 0]]), "w_down": g_w_down,
        "ln2_g": g_ln2_g[0], "ln2_b": g_ln2_b[0],
    }
    return loss, grad_x, grads, dmod, from_chips


def _col_sum(a):
    def body(a_ref, o_ref):
        o_ref[...] = jnp.sum(a_ref[...], 0, keepdims=True)

    return pl.pallas_call(body, name="col_sum", out_shape=jax.ShapeDtypeStruct((1, a.shape[1]), F32))(a)


def _adamw_math(w, grad, m, v):
    new_m = ADAM_B1 * m + (1.0 - ADAM_B1) * grad
    new_v = ADAM_B2 * v + (1.0 - ADAM_B2) * (grad * grad)
    m_hat = new_m / (1.0 - ADAM_B1 ** ADAM_STEP)
    v_hat = new_v / (1.0 - ADAM_B2 ** ADAM_STEP)
    return -ADAM_LR * (m_hat / (jnp.sqrt(v_hat) + ADAM_EPS) + ADAM_WD * w), new_m, new_v


def _adamw_many(ws, gs, ms, vs):
    n = len(ws)

    def body(*refs):
        for i in range(n):
            w_ref, g_ref, m_ref, v_ref = (refs[k * n + i] for k in range(4))
            d_ref, nm_ref, nv_ref = (refs[(4 + k) * n + i] for k in range(3))
            d_ref[...], nm_ref[...], nv_ref[...] = _adamw_math(w_ref[...], g_ref[...], m_ref[...], v_ref[...])

    outs = pl.pallas_call(
        body, name="adamw_small", out_shape=[jax.ShapeDtypeStruct(w.shape, F32) for w in ws] * 3,
    )(*ws, *gs, *ms, *vs)
    return outs[:n], outs[n:2 * n], outs[2 * n:]


def _adamw(w, g, m, v, name):
    n_r, n_c = w.shape
    if n_r % 8 == 0:
        tr = _pick(n_r, (256, 64, 32, 16, 8))
        grid, blk = (n_r // tr,), pl.BlockSpec((tr, n_c), lambda i: (i, 0))
    else:
        tc = _pick(n_c, (256, 128))
        grid, blk = (n_c // tc,), pl.BlockSpec((n_r, tc), lambda i: (0, i))

    def body(w_ref, g_ref, m_ref, v_ref, d_ref, nm_ref, nv_ref):
        d_ref[...], nm_ref[...], nv_ref[...] = _adamw_math(w_ref[...], g_ref[...], m_ref[...], v_ref[...])

    out = jax.ShapeDtypeStruct(w.shape, F32)
    return pl.pallas_call(
        body, name=name, grid=grid, in_specs=[blk] * 4, out_specs=[blk] * 3, out_shape=[out] * 3,
        compiler_params=_cparams(("parallel",)),
    )(w, g, m, v)


HBM_SPEC = pl.BlockSpec(memory_space=pltpu.HBM)
VMEM_SPEC = pl.BlockSpec(memory_space=pltpu.VMEM)
CHIP_FLIPS = ((1, 0), (0, 1), (1, 1))


def _place():
    return lax.axis_index("x"), lax.axis_index("y"), lax.axis_index("c")


def _flip(v, f):
    return 1 - v if f else v


def _all_gather8(slab, name):
    n_r, n_w = slab.shape

    def body(x_ref, o_ref, s_ref, send_sems, recv_sems, local_sem):
        x, y, c = _place()
        me = 4 * x + 2 * y + c
        mine = pltpu.make_async_copy(x_ref, o_ref.at[me], local_sem)
        mine.start()
        peers = [(_flip(x, k & 4), _flip(y, k & 2), _flip(c, k & 1)) for k in range(1, N_DEV)]
        sends = []
        for k, peer in enumerate(peers):
            cp = pltpu.make_async_remote_copy(src_ref=x_ref, dst_ref=o_ref.at[me], send_sem=send_sems.at[k],
                                              recv_sem=recv_sems.at[k], device_id=peer, device_id_type=MESH)
            cp.start()
            sends.append(cp)
        for k, (px, py, pc) in enumerate(peers):
            pltpu.make_async_remote_copy(src_ref=x_ref, dst_ref=o_ref.at[4 * px + 2 * py + pc],
                                         send_sem=send_sems.at[k], recv_sem=recv_sems.at[k],
                                         device_id=(px, py, pc), device_id_type=MESH).wait_recv()
        for cp in sends:
            cp.wait_send()
        mine.wait()
        total = o_ref[0]
        for d in range(1, N_DEV):
            total = total + o_ref[d]
        s_ref[...] = total

    return pl.pallas_call(
        body, name=name, in_specs=[VMEM_SPEC], out_specs=[VMEM_SPEC, VMEM_SPEC],
        out_shape=[jax.ShapeDtypeStruct((N_DEV, n_r, n_w), F32), jax.ShapeDtypeStruct((n_r, n_w), F32)],
        scratch_shapes=[pltpu.SemaphoreType.DMA((N_DEV - 1,)), pltpu.SemaphoreType.DMA((N_DEV - 1,)),
                        pltpu.SemaphoreType.DMA],
    )(slab)


SEQ_ROWS = 8


def _prologue(slab, w_ada_shard, b_shard, rider):
    n_r, n_w = slab.shape
    n_col = w_ada_shard.shape[1]
    r_inputs, r_in_specs, r_out_specs, r_sems, split = _with_rider(rider, 3, 3, 6)

    def body(*refs):
        (x_ref, w_ref, b_ref, g_ref, cond_ref, modr_ref, modp_ref, s1, r1, s2, r2, lsem), parts = split(refs)
        rider.first(*parts)
        x, y, c = _place()
        me = 4 * x + 2 * y + c
        peers = [(_flip(x, k & 4), _flip(y, k & 2), _flip(c, k & 1)) for k in range(1, N_DEV)]
        ids = [4 * px + 2 * py + pc for px, py, pc in peers]

        def exchange(src_of, dst, send_sems, recv_sems, own_sem):
            mine = pltpu.make_async_copy(src_of(me), dst.at[me], own_sem)
            mine.start()
            sends = [pltpu.make_async_remote_copy(src_ref=src_of(ids[k]), dst_ref=dst.at[me], send_sem=send_sems.at[k],
                                                  recv_sem=recv_sems.at[k], device_id=peers[k], device_id_type=MESH)
                     for k in range(N_DEV - 1)]
            for cp in sends:
                cp.start()
            for k in range(N_DEV - 1):
                pltpu.make_async_remote_copy(src_ref=src_of(ids[k]), dst_ref=dst.at[ids[k]], send_sem=send_sems.at[k],
                                             recv_sem=recv_sems.at[k], device_id=peers[k],
                                             device_id_type=MESH).wait_recv()
            for cp in sends:
                cp.wait_send()
            mine.wait()

        exchange(lambda d: x_ref, g_ref, s1, r1, lsem.at[0])
        cond = _silu(g_ref[:, 0:SEQ_ROWS, :].reshape(N_DEV * SEQ_ROWS, n_w))
        cond_ref[...] = cond
        modp_ref[...] = jnp.dot(cond.astype(BF16), w_ref[...].astype(BF16), preferred_element_type=F32) + b_ref[...]
        exchange(lambda d: modp_ref.at[pl.ds(pl.multiple_of(d * SEQ_ROWS, SEQ_ROWS), SEQ_ROWS)], modr_ref, s2, r2,
                 lsem.at[1])
        rider.last(*parts)

    sem7 = pltpu.SemaphoreType.DMA((N_DEV - 1,))
    gathered, cond, mod_recv, *rider_outs = pl.pallas_call(
        body, name="prologue", in_specs=[VMEM_SPEC] * 3 + r_in_specs, out_specs=[VMEM_SPEC] * 3 + r_out_specs,
        out_shape=[jax.ShapeDtypeStruct((N_DEV, n_r, n_w), F32), jax.ShapeDtypeStruct((N_DEV * SEQ_ROWS, n_w), F32),
                   jax.ShapeDtypeStruct((N_DEV, SEQ_ROWS, n_col), F32)] + list(rider.out_shapes),
        scratch_shapes=[pltpu.VMEM((N_DEV * SEQ_ROWS, n_col), F32), sem7, sem7, sem7, sem7,
                        pltpu.SemaphoreType.DMA((2,))] + r_sems,
        compiler_params=pltpu.CompilerParams(vmem_limit_bytes=VMEM_LIMIT),
    )(slab, w_ada_shard, b_shard, *r_inputs)
    return gathered, cond, mod_recv, rider_outs


def _gather_rider(shards):
    n_a = len(shards)

    def plan(ins, outs, sems):
        send_sems, recv_sems = sems
        x, y, c = _place()
        chips = [(_flip(x, fx), _flip(y, fy)) for fx, fy in CHIP_FLIPS]

        def copy(k, slot, chip_of_block, half, to, src=None):
            dst = outs[k].at[chip_of_block, half]
            return pltpu.make_async_remote_copy(src_ref=dst if src is None else src, dst_ref=dst,
                                                send_sem=send_sems.at[k * 6 + slot], recv_sem=recv_sems.at[k * 6 + slot],
                                                device_id=to, device_id_type=MESH)

        first = [copy(k, r, 2 * x + y, c, (*chips[r], c), src=ins[k].at[c]) for k in range(n_a) for r in range(3)]
        return copy, chips, first, (x, y, c)

    def first_step(ins, outs, sems):
        for cp in plan(ins, outs, sems)[2]:
            cp.start()

    def last_step(ins, outs, sems):
        copy, chips, first, (x, y, c) = plan(ins, outs, sems)
        passed = []
        for k in range(n_a):
            for r, (px, py) in enumerate(chips):
                copy(k, r, 2 * px + py, c, (x, y, c)).wait_recv()
                fwd = copy(k, 3 + r, 2 * px + py, c, (x, y, 1 - c))
                fwd.start()
                passed.append(fwd)
        for k in range(n_a):
            for r, (px, py) in enumerate(chips):
                copy(k, 3 + r, 2 * px + py, 1 - c, (x, y, c)).wait_recv()
        for cp in first + passed:
            cp.wait_send()

    return Rider(shards, [jax.ShapeDtypeStruct((N_CHIPS,) + s.shape, s.dtype) for s in shards],
                 [pltpu.SemaphoreType.DMA((6 * n_a,)), pltpu.SemaphoreType.DMA((6 * n_a,))], first_step, last_step)


def _place_own(gathered, shard, chip, name):
    _, _, n_h, n_c = gathered.shape
    th = _pick(n_h, (256, 176, 128))

    def body(sel_ref, s_ref, _, o_ref):
        o_ref[...] = s_ref[...]

    grid_spec = pltpu.PrefetchScalarGridSpec(
        num_scalar_prefetch=1, grid=(2, n_h // th),
        in_specs=[pl.BlockSpec((None, th, n_c), lambda hf, i, sel: (hf, i, 0)), pl.BlockSpec(memory_space=pl.ANY)],
        out_specs=pl.BlockSpec((None, None, th, n_c), lambda hf, i, sel: (sel[0], hf, i, 0)))
    return pl.pallas_call(
        body, name=name, grid_spec=grid_spec, out_shape=jax.ShapeDtypeStruct(gathered.shape, gathered.dtype),
        input_output_aliases={2: 0}, compiler_params=_cparams(("parallel", "parallel")),
    )(chip.reshape(1), shard, gathered)


def _pair_rider(parts):
    n_a = len(parts)

    def plan(ins, outs, sems):
        send_sems, recv_sems = sems
        x, y, c = _place()
        return [pltpu.make_async_remote_copy(src_ref=ins[k].at[:, 1 - c], dst_ref=outs[k], send_sem=send_sems.at[k],
                                             recv_sem=recv_sems.at[k], device_id=(x, y, 1 - c), device_id_type=MESH)
                for k in range(n_a)]

    def first_step(ins, outs, sems):
        for cp in plan(ins, outs, sems):
            cp.start()

    def last_step(ins, outs, sems):
        for cp in plan(ins, outs, sems):
            cp.wait()

    return Rider(parts, [jax.ShapeDtypeStruct((N_CHIPS,) + p.shape[2:], F32) for p in parts],
                 [pltpu.SemaphoreType.DMA((n_a,)), pltpu.SemaphoreType.DMA((n_a,))], first_step, last_step)


def _alone(rider, name):
    n_a = len(rider.inputs)

    def body(*refs):
        parts = (refs[:n_a], refs[n_a:2 * n_a], refs[2 * n_a:])
        rider.first(*parts)
        rider.last(*parts)

    return pl.pallas_call(
        body, name=name, in_specs=[HBM_SPEC] * n_a, out_specs=[HBM_SPEC] * n_a,
        out_shape=rider.out_shapes, scratch_shapes=rider.sems,
    )(*rider.inputs)


def _chips_rider(sums):
    n_a = len(sums)

    def plan(ins, outs, sems):
        send_sems, recv_sems = sems
        x, y, c = _place()
        cps = []
        for k in range(n_a):
            for r, (fx, fy) in enumerate(CHIP_FLIPS):
                px, py = _flip(x, fx), _flip(y, fy)
                cps.append(pltpu.make_async_remote_copy(
                    src_ref=ins[k].at[2 * px + py], dst_ref=outs[k].at[r], send_sem=send_sems.at[3 * k + r],
                    recv_sem=recv_sems.at[3 * k + r], device_id=(px, py, c), device_id_type=MESH))
        return cps

    def first_step(ins, outs, sems):
        for cp in plan(ins, outs, sems):
            cp.start()

    def last_step(ins, outs, sems):
        for cp in plan(ins, outs, sems):
            cp.wait()

    return Rider(sums, [jax.ShapeDtypeStruct((3,) + s.shape[1:], s.dtype) for s in sums],
                 [pltpu.SemaphoreType.DMA((3 * n_a,)), pltpu.SemaphoreType.DMA((3 * n_a,))], first_step, last_step)


def _rs_share(bufs):
    n_a = len(bufs)

    def body(*refs):
        ins, outs = refs[:n_a], refs[n_a:2 * n_a]
        send_sems, recv_sems = refs[2 * n_a:]
        x, y, c = _place()
        sends = [pltpu.make_async_remote_copy(src_ref=ins[k].at[c], dst_ref=outs[k].at[c], send_sem=send_sems.at[k],
                                              recv_sem=recv_sems.at[k], device_id=(x, y, 1 - c), device_id_type=MESH)
                 for k in range(n_a)]
        for cp in sends:
            cp.start()
        for k in range(n_a):
            pltpu.make_async_remote_copy(src_ref=ins[k].at[c], dst_ref=outs[k].at[1 - c], send_sem=send_sems.at[k],
                                         recv_sem=recv_sems.at[k], device_id=(x, y, 1 - c),
                                         device_id_type=MESH).wait_recv()
        for cp in sends:
            cp.wait_send()

    return pl.pallas_call(
        body, name="rs_share", in_specs=[HBM_SPEC] * n_a, out_specs=[HBM_SPEC] * n_a,
        out_shape=[jax.ShapeDtypeStruct(s.shape, F32) for s in bufs],
        input_output_aliases={k: k for k in range(n_a)},
        scratch_shapes=[pltpu.SemaphoreType.DMA((n_a,)), pltpu.SemaphoreType.DMA((n_a,))],
    )(*bufs)


def _pair_add(part, recv, core, name):
    _, _, n_h, n_c = part.shape
    th = _pick(n_h, (256, 176, 128))

    def body(sel_ref, p_ref, r_ref, o_ref):
        o_ref[...] = (p_ref[...] + r_ref[...]).astype(BF16)

    grid_spec = pltpu.PrefetchScalarGridSpec(
        num_scalar_prefetch=1, grid=(N_CHIPS, n_h // th),
        in_specs=[pl.BlockSpec((None, None, th, n_c), lambda j, i, sel: (j, sel[0], i, 0)),
                  pl.BlockSpec((None, th, n_c), lambda j, i, sel: (j, i, 0))],
        out_specs=pl.BlockSpec((None, th, n_c), lambda j, i, sel: (j, i, 0)))
    return pl.pallas_call(
        body, name=name, grid_spec=grid_spec, out_shape=jax.ShapeDtypeStruct(recv.shape, BF16),
        compiler_params=_cparams(("parallel", "parallel")),
    )(core.reshape(1), part, recv)


def _chip_add(sums, recv, chip, core, name):
    _, n_h, n_c = sums.shape
    th = _pick(n_h, (256, 176, 128))

    def body(sel_ref, s_ref, r_ref, o_ref):
        total = s_ref[...].astype(F32)
        for r in range(3):
            total = total + r_ref[r].astype(F32)
        o_ref[...] = total

    grid_spec = pltpu.PrefetchScalarGridSpec(
        num_scalar_prefetch=1, grid=(n_h // th,),
        in_specs=[pl.BlockSpec((None, th, n_c), lambda i, sel: (sel[0], i, 0)),
                  pl.BlockSpec((3, th, n_c), lambda i, sel: (0, i, 0))],
        out_specs=pl.BlockSpec((None, th, n_c), lambda i, sel: (sel[1], i, 0)))
    return pl.pallas_call(
        body, name=name, grid_spec=grid_spec, out_shape=jax.ShapeDtypeStruct((2, n_h, n_c), F32),
        compiler_params=_cparams(("parallel",)),
    )(jnp.stack([chip, core]), sums, recv)


def _row_halves(a):
    return a.reshape(N_CHIPS, 2, -1, a.shape[-1])


class StepComm:
    REST = ("w_o", "w_up", "w_down")

    def __init__(self, core, chip, rest_shards, in_cols):
        self.core, self.chip, self.shards, self.in_cols = core, chip, rest_shards, in_cols

    def proj_rider(self):
        return _gather_rider([self.shards[0], self.shards[2]])

    def scan_rider(self):
        return _gather_rider([self.shards[1]])

    def weights_from(self, landed_proj, landed_scan):
        landed = (landed_proj[0], landed_scan[0], landed_proj[1])
        g_o, g_up, g_down = (_place_own(g, s, self.chip, "place_own_" + n)
                             for g, s, n in zip(landed, self.shards, self.REST))
        return {"w_o": g_o.reshape(-1, D_MODEL), "w_up": g_up.reshape(N_CHIPS, -1, g_up.shape[-1]),
                "w_down": g_down.reshape(-1, D_MODEL)}

    def _add_pairs(self, parts, from_sibling, names):
        return [_pair_add(p, r, self.core, "pair_add_" + n) for p, r, n in zip(parts, from_sibling, names)]

    def ffn_pair_rider(self, g_w_up, g_w_down):
        self.ffn_parts = [_row_halves(g_w_up), _row_halves(g_w_down)]
        return _pair_rider(self.ffn_parts)

    def ffn_chips_rider(self, from_sibling):
        self.ffn_sums = self._add_pairs(self.ffn_parts, from_sibling, ("w_up", "w_down"))
        return _chips_rider(self.ffn_sums)

    def tail_chips_rider(self, g_w_in_p, g_w_o):
        parts = [_row_halves(_w_in_to_chips(g_w_in_p, self.in_cols)), _row_halves(g_w_o)]
        self.tail_sums = self._add_pairs(parts, _alone(_pair_rider(parts), "rs_pair_tail"), ("w_in", "w_o"))
        return _chips_rider(self.tail_sums)

    def finish(self, ffn_from_chips, tail_from_chips):
        halves = [_chip_add(s, r, self.chip, self.core, "chip_add_" + n)
                  for s, r, n in zip(self.tail_sums + self.ffn_sums, list(tail_from_chips) + list(ffn_from_chips),
                                     ("w_in", "w_o", "w_up", "w_down"))]
        return [f.reshape(-1, f.shape[-1]) for f in _rs_share(halves)]


SLAB_W = 1024


def _pack(arrays, rows):
    flat = jnp.concatenate([a.reshape(-1).astype(F32) for a in arrays])
    return jnp.pad(flat, (0, rows * SLAB_W - flat.shape[0])).reshape(rows, SLAB_W)


def _unpack(flat, shapes):
    out, off = [], 0
    for s in shapes:
        n = 1
        for d in s:
            n *= d
        out.append(flat[off:off + n].reshape(s))
        off += n
    return out


def _rows_for(arrays_or_shapes):
    n = 0
    for a in arrays_or_shapes:
        s = a if isinstance(a, tuple) else a.shape
        k = 1
        for d in s:
            k *= d
        n += k
    return -(-n // (8 * SLAB_W)) * 8


def kernel(x, c, ln0_g, ln0_b, w_ada, b_ada, w_in, dn_conv, dn_a_log, dn_dt_bias, dn_norm_g, gla_w_gate2, gla_b_gate, gla_norm_g, w_o, ln1_g, ln1_b, ffn_w_up, ffn_conv, ffn_conv_b, ffn_w_down, ln2_g, ln2_b, loss_target, m_ln0_g, m_ln0_b, m_w_ada, m_b_ada, m_w_in, m_dn_conv, m_dn_a_log, m_dn_dt_bias, m_dn_norm_g, m_gla_w_gate2, m_gla_b_gate, m_gla_norm_g, m_w_o, m_ln1_g, m_ln1_b, m_ffn_w_up, m_ffn_conv, m_ffn_conv_b, m_ffn_w_down, m_ln2_g, m_ln2_b, v_ln0_g, v_ln0_b, v_w_ada, v_b_ada, v_w_in, v_dn_conv, v_dn_a_log, v_dn_dt_bias, v_dn_norm_g, v_gla_w_gate2, v_gla_b_gate, v_gla_norm_g, v_w_o, v_ln1_g, v_ln1_b, v_ffn_w_up, v_ffn_conv, v_ffn_conv_b, v_ffn_w_down, v_ln2_g, v_ln2_b):
    n_b, t_len, _ = x.shape
    xi, yi, ci = _place()
    chip = (2 * xi + yi).astype(jnp.int32)
    core = ci.astype(jnp.int32)
    n_all = N_DEV * n_b
    ada_cols = w_ada.shape[2]

    halves = lambda a: a.astype(BF16).reshape(2, a.shape[0] // 2, a.shape[1])
    w_in_halves = halves(w_in[0])
    sharded_small = [dn_conv[0], gla_w_gate2[0], ffn_conv[0]]
    slab = jnp.concatenate([_pack([c], SEQ_ROWS), _pack(sharded_small, _rows_for(sharded_small))], axis=0)
    b_ada_shard = lax.dynamic_slice(b_ada, (0, chip * ada_cols), (1, ada_cols))
    gathered, cond_pad, mod_recv, (g_in,) = _prologue(slab, w_ada[0], b_ada_shard, _gather_rider([w_in_halves]))
    g_in = _place_own(g_in, w_in_halves, chip, "place_own_w_in")
    cond_all = cond_pad.reshape(N_DEV, SEQ_ROWS, D_MODEL)[:, :n_b].reshape(n_all, D_MODEL)
    by_chip = gathered.reshape(N_DEV, -1)[0::2]
    full, off = [], SEQ_ROWS * SLAB_W
    for a in sharded_small:
        blocks = by_chip[:, off:off + a.size].reshape(N_CHIPS, *a.shape)
        full.append(blocks.transpose(1, 0, 2).reshape(a.shape[0], N_CHIPS * a.shape[1]))
        off += a.size
    dn_conv_f, gate2_f, ffn_conv_f = full
    mod = mod_recv[0::2, :n_b].transpose(1, 0, 2).reshape(n_b, 6, D_MODEL)

    comm = StepComm(core, chip, [halves(w_o[0]), halves(ffn_w_up[0]), halves(ffn_w_down[0])], w_in.shape[2])
    params = {
        "w_in_p": _w_in_to_padded(g_in.reshape(N_CHIPS, -1, g_in.shape[-1])),
        "dn_conv": dn_conv_f, "dn_a_log": dn_a_log[0], "dn_dt_bias": dn_dt_bias[0], "dn_norm_g": dn_norm_g[0],
        "gla_w_gate2": gate2_f, "gla_b_gate": gla_b_gate[0], "gla_norm_g": gla_norm_g[0],
        "ln0_g": ln0_g, "ln0_b": ln0_b, "ln1_g": ln1_g[0], "ln1_b": ln1_b[0], "ln2_g": ln2_g[0], "ln2_b": ln2_b[0],
        "ffn_conv": ffn_conv_f, "ffn_conv_b": ffn_conv_b[0],
    }

    loss_row, grad_x, gp, dmod, from_chips = _local_step(
        x.reshape(n_b * t_len, D_MODEL), loss_target.reshape(n_b * t_len, D_MODEL), mod, params, n_b, t_len, comm)
    names = ["ln0_g", "ln0_b", "w_ada", "b_ada", "w_in", "dn_conv", "dn_a_log", "dn_dt_bias", "dn_norm_g",
             "gla_w_gate2", "gla_b_gate", "gla_norm_g", "w_o", "ln1_g", "ln1_b", "ffn_w_up", "ffn_conv", "ffn_conv_b",
             "ffn_w_down", "ln2_g", "ln2_b"]
    weights = dict(zip(names, [ln0_g, ln0_b, w_ada, b_ada, w_in, dn_conv, dn_a_log, dn_dt_bias, dn_norm_g, gla_w_gate2,
                               gla_b_gate, gla_norm_g, w_o, ln1_g, ln1_b, ffn_w_up, ffn_conv, ffn_conv_b, ffn_w_down,
                               ln2_g, ln2_b]))
    m_in = dict(zip(names, [m_ln0_g, m_ln0_b, m_w_ada, m_b_ada, m_w_in, m_dn_conv, m_dn_a_log, m_dn_dt_bias,
                            m_dn_norm_g, m_gla_w_gate2, m_gla_b_gate, m_gla_norm_g, m_w_o, m_ln1_g, m_ln1_b,
                            m_ffn_w_up, m_ffn_conv, m_ffn_conv_b, m_ffn_w_down, m_ln2_g, m_ln2_b]))
    v_in = dict(zip(names, [v_ln0_g, v_ln0_b, v_w_ada, v_b_ada, v_w_in, v_dn_conv, v_dn_a_log, v_dn_dt_bias,
                            v_dn_norm_g, v_gla_w_gate2, v_gla_b_gate, v_gla_norm_g, v_w_o, v_ln1_g, v_ln1_b,
                            v_ffn_w_up, v_ffn_conv, v_ffn_conv_b, v_ffn_w_down, v_ln2_g, v_ln2_b]))
    grads, delta, new_m, new_v = {}, {}, {}, {}

    def adamw_big(n, grad):
        view = (lambda a: a.T) if n == "w_in" else (lambda a: a)
        outs = _adamw(view(weights[n][0]), view(grad), view(m_in[n][0]), view(v_in[n][0]), "adamw_" + n)
        grads[n] = grad[None]
        delta[n], new_m[n], new_v[n] = (view(a)[None] for a in outs)

    g_w_in, g_w_o, g_w_up, g_w_down = comm.finish(*from_chips)

    summed_names = ["loss", "ln0_g", "ln0_b", "dn_conv", "dn_a_log", "dn_dt_bias", "dn_norm_g", "gla_w_gate2",
                    "gla_b_gate", "gla_norm_g", "ln1_g", "ln1_b", "ffn_conv", "ffn_conv_b", "ln2_g", "ln2_b"]
    summed_parts = [loss_row[0, 0:1]] + [gp[n] for n in summed_names[1:]]
    sum_rows = _rows_for(summed_parts)
    slab = jnp.concatenate([_pack(summed_parts, sum_rows), _pack([dmod], _rows_for([dmod]))], axis=0)
    gathered, total = _all_gather8(slab, "reduce_small")
    small_g = dict(zip(summed_names, _unpack(total.reshape(-1), [a.shape for a in summed_parts])))
    loss = small_g["loss"][0]
    dmod_rows = n_b * 6 * D_MODEL // SLAB_W
    dmod_all = gathered[:, sum_rows:sum_rows + dmod_rows, :].reshape(n_all, 6 * D_MODEL)
    for n, grad in (("ffn_w_up", g_w_up), ("ffn_w_down", g_w_down), ("w_o", g_w_o), ("w_in", g_w_in)):
        adamw_big(n, grad)

    g_b_ada = _col_sum(dmod_all)
    dmod_cols = lax.dynamic_slice(dmod_all, (0, chip * ada_cols), (n_all, ada_cols))
    adamw_big("w_ada", _mm(cond_all, dmod_cols, ta=True, name="mm_g_ada"))

    col_block = lambda a: lax.dynamic_slice(a, (0, chip * (a.shape[1] // N_CHIPS)), (a.shape[0], a.shape[1] // N_CHIPS))
    grads.update({
        "ln0_g": small_g["ln0_g"], "ln0_b": small_g["ln0_b"], "b_ada": g_b_ada,
        "dn_conv": col_block(small_g["dn_conv"])[None], "dn_a_log": small_g["dn_a_log"][None],
        "dn_dt_bias": small_g["dn_dt_bias"][None], "dn_norm_g": small_g["dn_norm_g"][None],
        "gla_w_gate2": col_block(small_g["gla_w_gate2"])[None], "gla_b_gate": small_g["gla_b_gate"][None],
        "gla_norm_g": small_g["gla_norm_g"][None], "ln1_g": small_g["ln1_g"][None],
        "ln1_b": small_g["ln1_b"][None], "ffn_conv": col_block(small_g["ffn_conv"])[None],
        "ffn_conv_b": small_g["ffn_conv_b"][None], "ln2_g": small_g["ln2_g"][None], "ln2_b": small_g["ln2_b"][None],
    })
    small = [n for n in names if n not in delta]
    d_s, m_s, v_s = _adamw_many([weights[n] for n in small], [grads[n] for n in small],
                                [m_in[n] for n in small], [v_in[n] for n in small])
    for out, vals in ((delta, d_s), (new_m, m_s), (new_v, v_s)):
        out.update(zip(small, vals))

    return (loss, grad_x.reshape(x.shape), *[grads[n] for n in names], *[delta[n] for n in names],
            *[new_m[n] for n in names], *[new_v[n] for n in names])
```

```python
import functools

import jax
import jax.numpy as jnp
from jax import lax
from jax.experimental import pallas as pl
from jax.experimental.pallas import tpu as pltpu

F32 = jnp.float32
BF16 = jnp.bfloat16
MESH = pl.DeviceIdType.MESH

D_MODEL = 1024
HEADS = 4
HEAD_DIM = 128
GLA_KEY = 64
GATE_RANK = 16
CHUNK = 64
D_FF = 2816
ALPHA = 2.0 ** 0.25
EPS = 1e-6
N_CHIPS = 4
N_DEV = 8

PROJ_W = 3840
OFF_GQ, OFF_GK, OFF_GV, OFF_GG, OFF_SMALL, GLA_W = 0, 256, 512, 1024, 1536, 1792
OFF_Z = 2048
W_IN_COLS = 3608


def _qkv_block(j):
    return jnp.where(j < 2, GLA_W // 128 + j, (OFF_Z + 512) // 128 - 2 + j)

ADAM_LR, ADAM_B1, ADAM_B2, ADAM_EPS, ADAM_WD, ADAM_STEP = 0.001, 0.9, 0.999, 1e-08, 0.01, 10

VMEM_LIMIT = 56 * 1024 * 1024
ROW_TILE = 512


def _cparams(sem):
    return pltpu.CompilerParams(dimension_semantics=sem, vmem_limit_bytes=VMEM_LIMIT)


def _pick(n, prefs):
    for p in prefs:
        if n % p == 0:
            return p
    return n


def _mm(a, b, *, ta=False, tb=False, out_slabs=1, out_dtype=F32, name, rider=None):
    a_slabs = a.shape[0] if a.ndim == 3 else 1
    b_slabs = b.shape[0] if b.ndim == 3 else 1
    assert not (ta and a_slabs > 1)
    a2, b2 = a.shape[-2:], b.shape[-2:]
    if ta:
        k_dim, m_dim = a2
    else:
        m_dim, k_dim = a2[0], a2[1] * a_slabs
    n_dim = b2[0] if tb else b2[1] * b_slabs
    k_slabs = max(a_slabs, b_slabs if tb else 1)
    n_slabs = max(out_slabs, 1 if tb else b_slabs)
    tm = _pick(m_dim, (1024, 1408, 512, 256, 128))
    tn = _pick(n_dim // n_slabs, (1536, 1408, 1280, 1024, 768, 512, 384, 256, 128))
    tk = _pick(k_dim // k_slabs, (1408, 1280, 1024, 512, 256, 128))
    nk, nj = k_dim // tk, n_dim // tn
    nk_a, nk_b, nj_b, nj_o = nk // a_slabs, nk // b_slabs, nj // b_slabs, nj // out_slabs
    dims = (((0 if ta else 1,), (1 if tb else 0,)), ((), ()))

    grid = (m_dim // tm, nj, nk)
    assert out_dtype == F32
    r_inputs, r_in_specs, r_out_specs, r_sems, split = _with_rider(rider, 2, 1, 0)

    def body(*refs):
        (a_ref, b_ref, o_ref), parts = split(refs)
        ride_first, ride_last = _ride(rider, parts, grid)
        if rider is not None:
            ride_first()
        prod = lax.dot_general(a_ref[...].astype(BF16), b_ref[...].astype(BF16), dims, preferred_element_type=F32)
        if nk == 1:
            o_ref[...] = prod
        else:
            _acc(o_ref, prod, pl.program_id(2) == 0)
        if rider is not None:
            ride_last()

    if ta:
        a_spec = pl.BlockSpec((tk, tm), lambda i, j, k: (k, i))
    elif a_slabs > 1:
        a_spec = pl.BlockSpec((None, tm, tk), lambda i, j, k: (k // nk_a, i, k % nk_a))
    else:
        a_spec = pl.BlockSpec((tm, tk), lambda i, j, k: (i, k))
    if tb and b_slabs > 1:
        b_spec = pl.BlockSpec((None, tn, tk), lambda i, j, k: (k // nk_b, j, k % nk_b))
    elif tb:
        b_spec = pl.BlockSpec((tn, tk), lambda i, j, k: (j, k))
    elif b_slabs > 1:
        b_spec = pl.BlockSpec((None, tk, tn), lambda i, j, k: (j // nj_b, k, j % nj_b))
    else:
        b_spec = pl.BlockSpec((tk, tn), lambda i, j, k: (k, j))
    if out_slabs > 1:
        o_spec = pl.BlockSpec((None, tm, tn), lambda i, j, k: (j // nj_o, i, j % nj_o))
        o_shape = (out_slabs, m_dim, n_dim // out_slabs)
    else:
        o_spec, o_shape = pl.BlockSpec((tm, tn), lambda i, j, k: (i, j)), (m_dim, n_dim)
    out, *rider_outs = pl.pallas_call(
        body, name=name, grid=grid,
        in_specs=[a_spec, b_spec] + r_in_specs, out_specs=[o_spec] + r_out_specs,
        out_shape=[jax.ShapeDtypeStruct(o_shape, out_dtype)] + (list(rider.out_shapes) if rider else []),
        scratch_shapes=r_sems,
        compiler_params=_cparams(("arbitrary",) * 3 if rider else ("parallel", "parallel", "arbitrary")),
    )(a, b, *r_inputs)
    return (out, rider_outs) if rider else out


def _ln(x, g, b):
    mu = jnp.mean(x, -1, keepdims=True)
    xc = x - mu
    var = jnp.mean(xc * xc, -1, keepdims=True)
    return xc * lax.rsqrt(var + EPS) * g + b


def _softplus(x):
    return jnp.maximum(x, 0.0) + jnp.log(1.0 + jnp.exp(-jnp.abs(x)))


def _silu(x):
    return x * jax.nn.sigmoid(x)


def _dsilu(x):
    s = jax.nn.sigmoid(x)
    return s * (1.0 + x * (1.0 - s))


def _f_ln0(x, g, b, sc, sh):
    x0 = _ln(x, g, b)
    return x0, x0 * (1.0 + sc) + sh


def _f_ln1(x0, y, gt, g, b, sc, sh):
    x1 = _ln(ALPHA * x0 + (1.0 + gt) * y, g, b)
    return x1, x1 * (1.0 + sc) + sh


def _f_ln2_loss(x1, y2, gt, g, b, tgt):
    x2 = _ln(ALPHA * x1 + (1.0 + gt) * y2, g, b)
    err = x2 - tgt
    per_row = jnp.sum(err * err, -1, keepdims=True) * (0.5 / D_MODEL)
    return jnp.sum(per_row, 0, keepdims=True)


def _row_specs(t_len):
    nt = t_len // ROW_TILE
    row = pl.BlockSpec((ROW_TILE, D_MODEL), lambda b, i: (b * nt + i, 0))
    vec = pl.BlockSpec((1, D_MODEL), lambda b, i: (0, 0))
    mod = pl.BlockSpec((None, 6, D_MODEL), lambda b, i: (b, 0, 0))
    return nt, row, vec, mod


def _first_step():
    return jnp.logical_and(pl.program_id(0) == 0, pl.program_id(1) == 0)


def _acc(ref, val, first, at=(Ellipsis,)):
    @pl.when(first)
    def _():
        ref[at] = val

    @pl.when(jnp.logical_not(first))
    def _():
        ref[at] += val


def _acc_rows(ref, rows, first):
    for i, r in enumerate(rows):
        _acc(ref, r, first, at=(slice(i, i + 1), slice(None)))


def _ln0_fwd(x, g, b, mod, n_b, t_len):
    nt, row, vec, mods = _row_specs(t_len)

    def body(x_ref, g_ref, b_ref, mod_ref, x0_ref, h_ref):
        x0, h = _f_ln0(x_ref[...], g_ref[...], b_ref[...], mod_ref[1:2, :], mod_ref[0:1, :])
        x0_ref[...] = x0
        h_ref[...] = h.astype(BF16)

    return pl.pallas_call(
        body, name="ln0_fwd", grid=(n_b, nt), in_specs=[row, vec, vec, mods], out_specs=[row, row],
        out_shape=[jax.ShapeDtypeStruct(x.shape, F32), jax.ShapeDtypeStruct(x.shape, BF16)],
        compiler_params=_cparams(("parallel", "parallel")),
    )(x, g, b, mod)


def _ln0_bwd(x, g, b, mod, dx0, dh, n_b, t_len):
    nt, row, vec, mods = _row_specs(t_len)
    dmod_spec = pl.BlockSpec((None, 2, D_MODEL), lambda bb, i: (bb, 0, 0))

    def body(x_ref, g_ref, b_ref, mod_ref, dx0_ref, dh_ref, dx_ref, dg_ref, db_ref, dmod_ref):
        _, pull = jax.vjp(_f_ln0, x_ref[...], g_ref[...], b_ref[...], mod_ref[1:2, :], mod_ref[0:1, :])
        dx, dg, db, dsc, dsh = pull((dx0_ref[...], dh_ref[...]))
        dx_ref[...] = dx
        _acc(dg_ref, dg, _first_step())
        _acc(db_ref, db, _first_step())
        _acc_rows(dmod_ref, [dsh, dsc], pl.program_id(1) == 0)

    return pl.pallas_call(
        body, name="ln0_bwd", grid=(n_b, nt), in_specs=[row, vec, vec, mods, row, row],
        out_specs=[row, vec, vec, dmod_spec],
        out_shape=[jax.ShapeDtypeStruct(x.shape, F32), jax.ShapeDtypeStruct((1, D_MODEL), F32),
                   jax.ShapeDtypeStruct((1, D_MODEL), F32), jax.ShapeDtypeStruct((n_b, 2, D_MODEL), F32)],
        compiler_params=_cparams(("arbitrary", "arbitrary")),
    )(x, g, b, mod, dx0, dh)


def _ln1_fwd(x0, y, g, b, mod, n_b, t_len):
    nt, row, vec, mods = _row_specs(t_len)

    def body(x0_ref, y_ref, g_ref, b_ref, mod_ref, x1_ref, h_ref):
        x1, h = _f_ln1(x0_ref[...], y_ref[...], mod_ref[2:3, :], g_ref[...], b_ref[...],
                       mod_ref[4:5, :], mod_ref[3:4, :])
        x1_ref[...] = x1
        h_ref[...] = h.astype(BF16)

    return pl.pallas_call(
        body, name="ln1_fwd", grid=(n_b, nt), in_specs=[row, row, vec, vec, mods], out_specs=[row, row],
        out_shape=[jax.ShapeDtypeStruct(x0.shape, F32), jax.ShapeDtypeStruct(x0.shape, BF16)],
        compiler_params=_cparams(("parallel", "parallel")),
    )(x0, y, g, b, mod)


def _ln1_bwd(x0, y, g, b, mod, dx1, dh, n_b, t_len):
    nt, row, vec, mods = _row_specs(t_len)
    dmod_spec = pl.BlockSpec((None, 3, D_MODEL), lambda bb, i: (bb, 0, 0))

    def body(x0_ref, y_ref, g_ref, b_ref, mod_ref, dx1_ref, dh_ref, dx0_ref, dy_ref, dg_ref, db_ref, dmod_ref):
        _, pull = jax.vjp(_f_ln1, x0_ref[...], y_ref[...], mod_ref[2:3, :], g_ref[...], b_ref[...],
                          mod_ref[4:5, :], mod_ref[3:4, :])
        dx0, dy, dgt, dg, db, dsc, dsh = pull((dx1_ref[...], dh_ref[...]))
        dx0_ref[...] = dx0
        dy_ref[...] = dy.astype(BF16)
        _acc(dg_ref, dg, _first_step())
        _acc(db_ref, db, _first_step())
        _acc_rows(dmod_ref, [dgt, dsh, dsc], pl.program_id(1) == 0)

    return pl.pallas_call(
        body, name="ln1_bwd", grid=(n_b, nt), in_specs=[row, row, vec, vec, mods, row, row],
        out_specs=[row, row, vec, vec, dmod_spec],
        out_shape=[jax.ShapeDtypeStruct(x0.shape, F32), jax.ShapeDtypeStruct(x0.shape, BF16),
                   jax.ShapeDtypeStruct((1, D_MODEL), F32), jax.ShapeDtypeStruct((1, D_MODEL), F32),
                   jax.ShapeDtypeStruct((n_b, 3, D_MODEL), F32)],
        compiler_params=_cparams(("arbitrary", "arbitrary")),
    )(x0, y, g, b, mod, dx1, dh)


def _ln2_loss_bwd(x1, y2, g, b, mod, tgt, n_b, t_len):
    nt, row, vec, mods = _row_specs(t_len)
    one = pl.BlockSpec((1, 128), lambda bb, i: (0, 0))
    dmod_spec = pl.BlockSpec((None, 1, D_MODEL), lambda bb, i: (bb, 0, 0))

    def body(x1_ref, y2_ref, g_ref, b_ref, mod_ref, t_ref, loss_ref, dx1_ref, dy2_ref, dg_ref, db_ref, dgt_ref):
        loss, pull = jax.vjp(functools.partial(_f_ln2_loss, tgt=t_ref[...]), x1_ref[...], y2_ref[...],
                             mod_ref[5:6, :], g_ref[...], b_ref[...])
        dx1, dy2, dgt, dg, db = pull(jnp.ones((1, 1), F32))
        dx1_ref[...] = dx1
        dy2_ref[...] = dy2.astype(BF16)
        _acc(loss_ref, jnp.broadcast_to(loss, (1, 128)), _first_step())
        _acc(dg_ref, dg, _first_step())
        _acc(db_ref, db, _first_step())
        _acc(dgt_ref, dgt, pl.program_id(1) == 0)

    return pl.pallas_call(
        body, name="ln2_loss_bwd", grid=(n_b, nt), in_specs=[row, row, vec, vec, mods, row],
        out_specs=[one, row, row, vec, vec, dmod_spec],
        out_shape=[jax.ShapeDtypeStruct((1, 128), F32), jax.ShapeDtypeStruct(x1.shape, F32),
                   jax.ShapeDtypeStruct(x1.shape, BF16), jax.ShapeDtypeStruct((1, D_MODEL), F32),
                   jax.ShapeDtypeStruct((1, D_MODEL), F32), jax.ShapeDtypeStruct((n_b, 1, D_MODEL), F32)],
        compiler_params=_cparams(("arbitrary", "arbitrary")),
    )(x1, y2, g, b, mod, tgt)


def _shift_down(x, s):
    if s == 0:
        return x
    rows = lax.broadcasted_iota(jnp.int32, x.shape, 0)
    return jnp.where(rows >= s, pltpu.roll(x, s, 0), 0.0)


def _shift_up(x, s):
    if s == 0:
        return x
    t_len = x.shape[0]
    rows = lax.broadcasted_iota(jnp.int32, x.shape, 0)
    return jnp.where(rows < t_len - s, pltpu.roll(x, t_len - s, 0), 0.0)


def _taps(x, k_w):
    return [_shift_down(x, k_w - 1 - k) for k in range(k_w)]


def _conv(taps, w):
    out = w[0:1, :] * taps[0]
    for k in range(1, len(taps)):
        out = out + w[k:k + 1, :] * taps[k]
    return out


def _conv_bwd(taps, w, du):
    k_w = len(taps)
    dx = w[k_w - 1:k_w, :] * du
    for k in range(k_w - 1):
        dx = dx + w[k:k + 1, :] * _shift_up(du, k_w - 1 - k)
    return dx, [jnp.sum(du * taps[k], 0, keepdims=True) for k in range(k_w)]


def _dn_pre_fwd(proj, conv_w, n_b, t_len):
    n_ct = 3 * HEADS
    k_w = conv_w.shape[0]

    def body(x_ref, w_ref, o_ref):
        o_ref[...] = _silu(_conv(_taps(x_ref[...], k_w), w_ref[...]))

    return pl.pallas_call(
        body, name="dn_pre_fwd", grid=(n_ct, n_b),
        in_specs=[pl.BlockSpec((t_len, 128), lambda j, b: (b, _qkv_block(j))),
                  pl.BlockSpec((k_w, 128), lambda j, b: (0, j))],
        out_specs=pl.BlockSpec((t_len, 128), lambda j, b: (b, j)),
        out_shape=jax.ShapeDtypeStruct((n_b * t_len, n_ct * 128), F32),
        compiler_params=_cparams(("parallel", "parallel")),
    )(proj, conv_w)


def _dn_pre_bwd(proj, conv_w, dqkv, d_proj, n_b, t_len):
    n_ct = 3 * HEADS
    k_w = conv_w.shape[0]

    def body(x_ref, w_ref, d_ref, _, dx_ref, dw_ref):
        taps, w = _taps(x_ref[...], k_w), w_ref[...]
        du = d_ref[...] * _dsilu(_conv(taps, w))
        dx, dw = _conv_bwd(taps, w, du)
        dx_ref[...] = dx.astype(BF16)
        _acc_rows(dw_ref, dw, pl.program_id(1) == 0)

    return pl.pallas_call(
        body, name="dn_pre_bwd", grid=(n_ct, n_b),
        in_specs=[pl.BlockSpec((t_len, 128), lambda j, b: (b, _qkv_block(j))),
                  pl.BlockSpec((k_w, 128), lambda j, b: (0, j)),
                  pl.BlockSpec((t_len, 128), lambda j, b: (b, j)), pl.BlockSpec(memory_space=pl.ANY)],
        out_specs=[pl.BlockSpec((t_len, 128), lambda j, b: (b, _qkv_block(j))),
                   pl.BlockSpec((k_w, 128), lambda j, b: (0, j))],
        out_shape=[jax.ShapeDtypeStruct(d_proj.shape, BF16), jax.ShapeDtypeStruct((k_w, n_ct * 128), F32)],
        input_output_aliases={3: 0},
        compiler_params=_cparams(("parallel", "arbitrary")),
    )(proj, conv_w, dqkv, d_proj)


FFN_TC = 256
FFN_NT = D_FF // FFN_TC


def _ffn_specs(t_len):
    blk = lambda off: pl.BlockSpec((t_len, FFN_TC), lambda j, b: (b, j + off))
    wblk = lambda off: pl.BlockSpec((3, FFN_TC), lambda j, b: (0, j + off))
    bblk = lambda off: pl.BlockSpec((1, FFN_TC), lambda j, b: (0, j + off))
    return [blk(0), blk(FFN_NT), wblk(0), wblk(FFN_NT), bblk(0), bblk(FFN_NT)]


def _ffn_act_fwd(up, conv_w, conv_b, n_b, t_len):
    def body(g_ref, v_ref, wg_ref, wv_ref, bg_ref, bv_ref, o_ref):
        ug = _conv(_taps(g_ref[...], 3), wg_ref[...]) + bg_ref[...]
        uv = _conv(_taps(v_ref[...], 3), wv_ref[...]) + bv_ref[...]
        o_ref[...] = (_silu(ug) * uv).astype(BF16)

    return pl.pallas_call(
        body, name="ffn_act_fwd", grid=(FFN_NT, n_b), in_specs=_ffn_specs(t_len),
        out_specs=pl.BlockSpec((t_len, FFN_TC), lambda j, b: (b, j)),
        out_shape=jax.ShapeDtypeStruct((n_b * t_len, D_FF), BF16),
        compiler_params=_cparams(("parallel", "parallel")),
    )(up, up, conv_w, conv_w, conv_b, conv_b)


def _ffn_act_bwd(up, conv_w, conv_b, da, n_b, t_len):
    def body(g_ref, v_ref, wg_ref, wv_ref, bg_ref, bv_ref, da_ref, dup_ref, dw_ref, db_ref):
        first = pl.program_id(1) == 0
        tg, tv, wg, wv = _taps(g_ref[...], 3), _taps(v_ref[...], 3), wg_ref[...], wv_ref[...]
        ug = _conv(tg, wg) + bg_ref[...]
        uv = _conv(tv, wv) + bv_ref[...]
        d_act = da_ref[...]
        sig = jax.nn.sigmoid(ug)
        d_v = d_act * (ug * sig)
        d_g = d_act * uv * (sig * (1.0 + ug * (1.0 - sig)))
        for slab, (taps, w, du) in enumerate(((tg, wg, d_g), (tv, wv, d_v))):
            dx, dw = _conv_bwd(taps, w, du)
            dup_ref[slab] = dx.astype(BF16)
            for k, dw_k in enumerate(dw):
                _acc(dw_ref, dw_k, first, at=(slab, slice(k, k + 1), slice(None)))
            _acc(db_ref, jnp.sum(du, 0, keepdims=True), first, at=(slab, slice(None), slice(None)))

    return pl.pallas_call(
        body, name="ffn_act_bwd", grid=(FFN_NT, n_b),
        in_specs=_ffn_specs(t_len) + [pl.BlockSpec((t_len, FFN_TC), lambda j, b: (b, j))],
        out_specs=[pl.BlockSpec((2, t_len, FFN_TC), lambda j, b: (0, b, j)),
                   pl.BlockSpec((2, 3, FFN_TC), lambda j, b: (0, 0, j)),
                   pl.BlockSpec((2, 1, FFN_TC), lambda j, b: (0, 0, j))],
        out_shape=[jax.ShapeDtypeStruct((2, n_b * t_len, D_FF), BF16),
                   jax.ShapeDtypeStruct((2, 3, D_FF), F32), jax.ShapeDtypeStruct((2, 1, D_FF), F32)],
        compiler_params=_cparams(("parallel", "arbitrary")),
    )(up, up, conv_w, conv_w, conv_b, conv_b, da)


NN = (((2,), (1,)), ((0,), (0,)))
NT = (((2,), (2,)), ((0,), (0,)))
TN = (((1,), (1,)), ((0,), (0,)))


def _iota3(shape, axis):
    return lax.broadcasted_iota(jnp.int32, shape, axis)


def _dg(a, b, dims):
    return lax.dot_general(a, b, dims, preferred_element_type=F32)


def _dot(a, b):
    return _dg(a, b, NN)


def _dot_nt(a, b):
    return _dg(a, b, NT)


def _dot_tn(a, b):
    return _dg(a, b, TN)


def _split(a):
    hi = a.astype(BF16)
    return hi, (a - hi.astype(F32)).astype(BF16)


def _dg3(a, b, dims):
    ah, al = _split(a)
    bh, bl = _split(b)
    return _dg(ah, bh, dims) + (_dg(ah, bl, dims) + _dg(al, bh, dims))


@jax.custom_vjp
def _dot3(a, b):
    return _dg3(a, b, NN)


def _dot3_fwd(a, b):
    return _dg3(a, b, NN), (a, b)


def _dot3_bwd(res, g):
    a, b = res
    return _dg3(g, b, NT), _dg3(a, g, TN)


_dot3.defvjp(_dot3_fwd, _dot3_bwd)


def _lower_ones(g_n, n):
    shape = (g_n, n, n)
    return jnp.where(_iota3(shape, 1) >= _iota3(shape, 2), 1.0, 0.0).astype(BF16)


@jax.custom_vjp
def _chunk_cumsum(x):
    hi, lo = _split(x)
    tri = _lower_ones(x.shape[0], x.shape[1])
    return _dg(tri, hi, NN) + _dg(tri, lo, NN)


def _chunk_cumsum_fwd(x):
    return _chunk_cumsum(x), None


def _chunk_cumsum_bwd(_, g):
    hi, lo = _split(g)
    tri = _lower_ones(g.shape[0], g.shape[1])
    return (_dg(tri, hi, TN) + _dg(tri, lo, TN),)


_chunk_cumsum.defvjp(_chunk_cumsum_fwd, _chunk_cumsum_bwd)


@jax.custom_vjp
def _unit_lower_inv(m):
    n = m.shape[1]
    p = -m
    a = jnp.where(_iota3(m.shape, 1) == _iota3(m.shape, 2), 1.0, 0.0) + p
    span = 2
    while span < n:
        p = _dg3(p, p, NN)
        a = a + _dg3(a, p, NN)
        span *= 2
    return a


def _unit_lower_inv_fwd(m):
    a = _unit_lower_inv(m)
    return a, a


def _unit_lower_inv_bwd(a, da):
    return (-_dg3(a, _dg3(da, a, NT), TN),)


_unit_lower_inv.defvjp(_unit_lower_inv_fwd, _unit_lower_inv_bwd)


@jax.custom_vjp
def _saved_lower_inv(m, a):
    return a


def _saved_lower_inv_fwd(m, a):
    return a, a


def _saved_lower_inv_bwd(a, da):
    return _unit_lower_inv_bwd(a, da)[0], jnp.zeros_like(a)


_saved_lower_inv.defvjp(_saved_lower_inv_fwd, _saved_lower_inv_bwd)


def _rms_gate(o, gn, gate):
    return o * lax.rsqrt(jnp.mean(o * o, -1, keepdims=True) + EPS) * gn * _silu(gate)


def _dn_chains(q, k, v, z, small, s_in, a_log, dt_bias, gn, a_saved=None):
    prep = _dn_prepare(q, k, v, small, a_log, dt_bias, a_saved)
    og, s_out = _dn_advance(prep[:-1], z, s_in, gn)
    return og, s_out, prep[-1]


def _dn_prepare(q, k, v, small, a_log, dt_bias, a_saved=None):
    g_n, c_len = q.shape[0], q.shape[1]
    sq = (g_n, c_len, c_len)
    row, col = _iota3(sq, 1), _iota3(sq, 2)
    causal, strict, eye = row >= col, row > col, row == col
    qn = q * lax.rsqrt(jnp.sum(q * q, -1, keepdims=True) + EPS) * (HEAD_DIM ** -0.5)
    kn = k * lax.rsqrt(jnp.sum(k * k, -1, keepdims=True) + EPS)
    lane = _iota3(small.shape, 2)
    head = jnp.bitwise_and(_iota3(small.shape, 0), HEADS - 1)
    la_all = -jnp.exp(a_log) * _softplus(small + dt_bias)
    la_c = jnp.sum(jnp.where(lane == head, la_all, 0.0), 2, keepdims=True)
    beta = jnp.sum(jnp.where(lane == head + HEADS, jax.nn.sigmoid(small), 0.0), 2, keepdims=True)
    la_b = jnp.broadcast_to(la_c, sq)
    la_r = jnp.sum(jnp.where(eye, la_b, 0.0), 1, keepdims=True)
    g_c = jnp.sum(jnp.where(causal, jnp.broadcast_to(la_r, sq), 0.0), 2, keepdims=True)
    g_r = jnp.sum(jnp.where(row <= col, la_b, 0.0), 1, keepdims=True)
    g_last = jnp.sum(la_c, 1, keepdims=True)
    decay = jnp.exp(jnp.where(causal, g_c - g_r, -1e30))
    e_g = jnp.exp(g_c)
    kb = kn * beta
    m_low = jnp.where(strict, _dot_nt(kb, kn) * decay, 0.0)
    a_inv = _unit_lower_inv(m_low) if a_saved is None else _saved_lower_inv(m_low, a_saved)
    u = _dot3(a_inv, v * beta)
    w = _dot3(a_inv, kb * e_g)
    attn = _dot_nt(qn, kn) * decay
    return u, w, attn, qn * e_g, kn * jnp.exp(g_last - g_c), jnp.exp(g_last), a_inv


def _dn_advance(prep, z, s_in, gn):
    u, w, attn, q_dec, k_dec, g_chunk = prep
    v_new = u - _dot(w, s_in)
    o = _dot(q_dec, s_in) + _dot(attn, v_new)
    s_out = s_in * g_chunk + _dot_tn(k_dec, v_new)
    return _rms_gate(o, gn, z), s_out


def _gla_chains(q, k, v, gate, small, s_in, w2, b2, gn):
    g_n, c_len = q.shape[0], q.shape[1]
    sq, kk = (g_n, c_len, c_len), (g_n, GLA_KEY, GLA_KEY)
    causal = _iota3(sq, 1) >= _iota3(sq, 2)
    la = -_softplus(-(_dot(small, w2) + b2)) * (1.0 / 16.0)
    b = _chunk_cumsum(la)
    b_last = jnp.sum(jnp.where(_iota3(b.shape, 1) == c_len - 1, b, 0.0), 1, keepdims=True)
    q_dec = q * (GLA_KEY ** -0.5) * jnp.exp(b)
    attn = jnp.where(causal, _dot_nt(q_dec, k * jnp.exp(-b)), 0.0)
    o = _dot(q_dec, s_in) + _dot(attn, v)
    g_row = jnp.exp(b_last)
    g_col = jnp.sum(jnp.where(_iota3(kk, 1) == _iota3(kk, 2), jnp.broadcast_to(g_row, kk), 0.0), 2, keepdims=True)
    s_out = s_in * g_col + _dot_tn(k * jnp.exp(b_last - b), v)
    return _rms_gate(o, gn, gate), s_out


def _chunk_spec(n_b, width, col_block, n_c, reverse=False):
    if reverse:
        return pl.BlockSpec((n_b, CHUNK, width), lambda n: (0, n_c - 1 - n, col_block))
    return pl.BlockSpec((n_b, CHUNK, width), lambda n: (0, n, col_block))


def _hist_spec(n_b, d_k, n_c, reverse=False):
    if reverse:
        return pl.BlockSpec((None, n_b * HEADS, d_k, HEAD_DIM), lambda n: (n_c - 1 - n, 0, 0, 0))
    return pl.BlockSpec((None, n_b * HEADS, d_k, HEAD_DIM), lambda n: (n, 0, 0, 0))


def _ainv_spec(n_b, n_c, reverse=False):
    if reverse:
        return pl.BlockSpec((None, n_b * HEADS, CHUNK, CHUNK), lambda n: (n_c - 1 - n, 0, 0, 0))
    return pl.BlockSpec((None, n_b * HEADS, CHUNK, CHUNK), lambda n: (n, 0, 0, 0))


def _stack_chains(ref, n_b, slices):
    return jnp.stack([ref[b, :, sl] for b in range(n_b) for sl in slices], axis=0)


def _per_chain(ref, n_b):
    return jnp.stack([ref[b] for b in range(n_b) for _ in range(HEADS)], axis=0)


def _unstack_chains(ref, val, n_b, slices, offset=0, rows=slice(None), base=0):
    for b in range(n_b):
        for h, sl in enumerate(slices):
            ref[b, rows, slice(offset + sl.start, offset + sl.stop)] = val[base + b * HEADS + h].astype(ref.dtype)


def _gate_weights(w2_ref, b2_ref, n_b):
    w2 = jnp.stack([w2_ref[:, ks] for _ in range(n_b) for ks in GLA_KSL], axis=0)
    b2 = jnp.stack([b2_ref[:, ks] for _ in range(n_b) for ks in GLA_KSL], axis=0)
    return w2, b2


def _sum_heads(val, n_b):
    return [sum(val[b * HEADS + h] for h in range(HEADS)) for b in range(n_b)]


def _const_spec(shape):
    return pl.BlockSpec(shape, lambda n: (0,) * len(shape))


DN_SL = [slice(h * HEAD_DIM, (h + 1) * HEAD_DIM) for h in range(HEADS)]
GLA_KSL = [slice(h * GLA_KEY, (h + 1) * GLA_KEY) for h in range(HEADS)]


class Rider:
    def __init__(self, inputs, out_shapes, sems, first, last):
        self.inputs, self.out_shapes, self.sems, self.first, self.last = inputs, out_shapes, sems, first, last


def _with_rider(rider, n_in, n_out, n_scratch):
    if rider is None:
        return [], [], [], [], lambda refs: (refs, None)
    r_in, r_out, r_sem = len(rider.inputs), len(rider.out_shapes), len(rider.sems)

    def split(refs):
        own_in, rest = refs[:n_in], refs[n_in:]
        rid_in, rest = rest[:r_in], rest[r_in:]
        own_out, rest = rest[:n_out], rest[n_out:]
        rid_out, rest = rest[:r_out], rest[r_out:]
        own_scr, rid_sem = rest[:n_scratch], rest[n_scratch:]
        return own_in + own_out + own_scr, (rid_in, rid_out, rid_sem)

    return list(rider.inputs), [HBM_SPEC] * r_in, [HBM_SPEC] * r_out, list(rider.sems), split


def _ride(rider, parts, grid):
    if rider is None:
        return None, None
    grid = grid if isinstance(grid, tuple) else (grid,)

    def at(step_of):
        hit = pl.program_id(0) == step_of(grid[0])
        for axis in range(1, len(grid)):
            hit = jnp.logical_and(hit, pl.program_id(axis) == step_of(grid[axis]))
        return hit

    def first():
        pl.when(at(lambda n: 0))(lambda: rider.first(*parts))

    def last():
        pl.when(at(lambda n: n - 1))(lambda: rider.last(*parts))

    return first, last


FWD_CHUNKS = 4


def _dn_scan_fwd(qkv, proj, a_log, dt_bias, gn, n_b, t_len, rider=None):
    n_c, n_g, rows = t_len // CHUNK, n_b * HEADS, FWD_CHUNKS * CHUNK
    n_s = n_c // FWD_CHUNKS
    spec = lambda width, col_block: pl.BlockSpec((n_b, rows, width), lambda n: (0, n, col_block))
    kept = lambda d0, d1: pl.BlockSpec((FWD_CHUNKS, n_g, d0, d1), lambda n: (n, 0, 0, 0))
    r_inputs, r_in_specs, r_out_specs, r_sems, split = _with_rider(rider, 8, 3, 1)
    chunk_rows = [slice(j * CHUNK, (j + 1) * CHUNK) for j in range(FWD_CHUNKS)]

    def body(*refs):
        (q_ref, k_ref, v_ref, z_ref, sm_ref, al_ref, dt_ref, gn_ref,
         o_ref, hist_ref, ainv_ref, s_ref), parts = split(refs)
        ride_first, ride_last = _ride(rider, parts, n_s)
        if rider is not None:
            ride_first()

        @pl.when(pl.program_id(0) == 0)
        def _():
            s_ref[...] = jnp.zeros_like(s_ref)

        def stack(ref, slices):
            return jnp.stack([ref[b, rs, sl] for rs in chunk_rows for b in range(n_b) for sl in slices], axis=0)

        prep = _dn_prepare(stack(q_ref, DN_SL), stack(k_ref, DN_SL), stack(v_ref, DN_SL),
                           stack(sm_ref, [slice(None)] * HEADS), al_ref[...], dt_ref[...])
        z, state = stack(z_ref, DN_SL), s_ref[...]
        for j, rs in enumerate(chunk_rows):
            mine = slice(j * n_g, (j + 1) * n_g)
            hist_ref[j] = state
            ainv_ref[j] = prep[-1][mine]
            og, state = _dn_advance(tuple(a[mine] for a in prep[:-1]), z[mine], state, gn_ref[...])
            for b in range(n_b):
                for h, sl in enumerate(DN_SL):
                    o_ref[b, rs, sl] = og[b * HEADS + h].astype(BF16)
        s_ref[...] = state
        if rider is not None:
            ride_last()

    qkv3, proj3 = qkv.reshape(n_b, t_len, -1), proj.reshape(n_b, t_len, -1)
    o, hist, ainv, *rider_outs = pl.pallas_call(
        body, name="dn_scan_fwd", grid=(n_s,),
        in_specs=[spec(512, 0), spec(512, 1), spec(512, 2), spec(512, OFF_Z // 512), spec(128, OFF_SMALL // 128),
                  _const_spec((1, 128)), _const_spec((1, 128)), _const_spec((1, 128))] + r_in_specs,
        out_specs=[spec(512, 0), kept(HEAD_DIM, HEAD_DIM), kept(CHUNK, CHUNK)] + r_out_specs,
        out_shape=[jax.ShapeDtypeStruct((n_b, t_len, 2 * 512), BF16),
                   jax.ShapeDtypeStruct((n_c, n_b * HEADS, HEAD_DIM, HEAD_DIM), F32),
                   jax.ShapeDtypeStruct((n_c, n_b * HEADS, CHUNK, CHUNK), F32)]
        + (list(rider.out_shapes) if rider else []),
        scratch_shapes=[pltpu.VMEM((n_b * HEADS, HEAD_DIM, HEAD_DIM), F32)] + r_sems,
        compiler_params=_cparams(("arbitrary",)),
    )(qkv3, qkv3, qkv3, proj3, proj3, a_log, dt_bias, gn, *r_inputs)
    return o, (hist, ainv), rider_outs


SCAN_BWD_W = OFF_Z + 512
BWD_CHUNKS = 2


def _scan_bwd(qkv, proj, dn_params, gla_params, hist_dn, hist_gla, d_o, n_b, t_len, rider=None):
    n_c, n_g, rows = t_len // CHUNK, n_b * HEADS, BWD_CHUNKS * CHUNK
    n_s = n_c // BWD_CHUNKS
    rev = lambda width, col_block: pl.BlockSpec((n_b, rows, width), lambda n: (0, n_s - 1 - n, col_block))
    kept = lambda d0, d1: pl.BlockSpec((BWD_CHUNKS, n_g, d0, d1), lambda n: (n_s - 1 - n, 0, 0, 0))
    r_inputs, r_in_specs, r_out_specs, r_sems, split = _with_rider(rider, 19, 8, 2)
    (hist, ainv), do_gla_sl = hist_dn, [slice(512 + sl.start, 512 + sl.stop) for sl in DN_SL]
    chunk_rows = [slice(j * CHUNK, (j + 1) * CHUNK) for j in range(BWD_CHUNKS)]

    def body(*refs):
        (q_ref, k_ref, v_ref, z_ref, sm_ref, gq_ref, gk_ref, gv_ref, gg_ref,
         al_ref, dt_ref, dgn_in_ref, w2_ref, b2_ref, ggn_in_ref, hist_ref, ainv_ref, ghist_ref, do_ref,
         dqkv_ref, dp_ref, dal_ref, ddt_ref, dgn_ref, dw2_ref, db2_ref, dggn_ref, ds_ref, gds_ref), parts = split(refs)
        ride_first, ride_last = _ride(rider, parts, n_s)
        if rider is not None:
            ride_first()
        first = pl.program_id(0) == 0

        @pl.when(first)
        def _():
            ds_ref[...] = jnp.zeros_like(ds_ref)
            gds_ref[...] = jnp.zeros_like(gds_ref)

        def stack(ref, slices, chunks=chunk_rows):
            return jnp.stack([ref[b, rs, sl] for rs in chunks for b in range(n_b) for sl in slices], axis=0)

        a_saved = jnp.concatenate([ainv_ref[j] for j in range(BWD_CHUNKS)], axis=0)
        prepare = lambda *a: _dn_prepare(*a, a_saved=a_saved)[:-1]
        prep, pull_prep = jax.vjp(prepare, stack(q_ref, DN_SL), stack(k_ref, DN_SL), stack(v_ref, DN_SL),
                                  stack(sm_ref, [slice(None)] * HEADS), al_ref[...], dt_ref[...])
        z, d_out, gn = stack(z_ref, DN_SL), stack(do_ref, DN_SL), dgn_in_ref[...]
        d_prep, dz, d_state, dgn = [None] * BWD_CHUNKS, [None] * BWD_CHUNKS, ds_ref[...], jnp.zeros((1, 128), F32)
        for j in reversed(range(BWD_CHUNKS)):
            mine = slice(j * n_g, (j + 1) * n_g)
            _, pull = jax.vjp(_dn_advance, tuple(a[mine] for a in prep), z[mine], hist_ref[j], gn)
            d_prep[j], dz[j], d_state, dgn_j = pull((d_out[mine], d_state))
            dgn = dgn + dgn_j
        ds_ref[...] = d_state
        dq, dk, dv, dsm_dn, dal, ddt = pull_prep(tuple(jnp.concatenate([d[i] for d in d_prep], axis=0)
                                                       for i in range(len(prep))))

        w2s, b2s = _gate_weights(w2_ref, b2_ref, n_b)
        d_gstate, dggn = gds_ref[...], jnp.zeros((1, 128), F32)
        dw2, db2, g_res = jnp.zeros_like(w2s), jnp.zeros_like(b2s), [None] * BWD_CHUNKS
        for j in reversed(range(BWD_CHUNKS)):
            one = [chunk_rows[j]]
            _, gpull = jax.vjp(_gla_chains, stack(gq_ref, GLA_KSL, one), stack(gk_ref, GLA_KSL, one),
                               stack(gv_ref, DN_SL, one), stack(gg_ref, DN_SL, one),
                               stack(sm_ref, [slice(None)] * HEADS, one), ghist_ref[j], w2s, b2s, ggn_in_ref[...])
            *g_res[j], d_gstate, dw2_j, db2_j, dggn_j = gpull((stack(do_ref, do_gla_sl, one), d_gstate))
            dw2, db2, dggn = dw2 + dw2_j, db2 + db2_j, dggn + dggn_j
        gds_ref[...] = d_gstate

        for j, rs in enumerate(chunk_rows):
            at = dict(rows=rs, base=j * n_g)
            _unstack_chains(dqkv_ref, dq, n_b, DN_SL, **at)
            _unstack_chains(dqkv_ref, dk, n_b, DN_SL, offset=512, **at)
            _unstack_chains(dqkv_ref, dv, n_b, DN_SL, offset=1024, **at)
            _unstack_chains(dp_ref, dz[j], n_b, DN_SL, offset=OFF_Z, rows=rs)
            gq, gk, gv, gg, dsm_gla = g_res[j]
            _unstack_chains(dp_ref, gq, n_b, GLA_KSL, offset=OFF_GQ, rows=rs)
            _unstack_chains(dp_ref, gk, n_b, GLA_KSL, offset=OFF_GK, rows=rs)
            _unstack_chains(dp_ref, gv, n_b, DN_SL, offset=OFF_GV, rows=rs)
            _unstack_chains(dp_ref, gg, n_b, DN_SL, offset=OFF_GG, rows=rs)
            for b, (s_dn, s_gla) in enumerate(zip(_sum_heads(dsm_dn[j * n_g:(j + 1) * n_g], n_b),
                                                  _sum_heads(dsm_gla, n_b))):
                dp_ref[b, rs, OFF_SMALL:OFF_SMALL + 128] = (s_dn + s_gla).astype(BF16)
                dp_ref[b, rs, OFF_SMALL + 128:GLA_W] = jnp.zeros((CHUNK, GLA_W - OFF_SMALL - 128), BF16)
        _acc(dal_ref, dal, first)
        _acc(ddt_ref, ddt, first)
        _acc(dgn_ref, dgn, first)
        for h, ks in enumerate(GLA_KSL):
            _acc(dw2_ref, sum(dw2[b * HEADS + h] for b in range(n_b)), first, at=(slice(None), ks))
            _acc(db2_ref, sum(db2[b * HEADS + h] for b in range(n_b)), first, at=(slice(None), ks))
        _acc(dggn_ref, dggn, first)
        if rider is not None:
            ride_last()

    qkv3, proj3, do3 = (a.reshape(n_b, t_len, -1) for a in (qkv, proj, d_o))
    vec = jax.ShapeDtypeStruct((1, 128), F32)
    dqkv, d_proj, dal, ddt, dgn, dw2, db2, dggn, *rider_outs = pl.pallas_call(
        body, name="scan_bwd", grid=(n_s,),
        in_specs=[rev(512, 0), rev(512, 1), rev(512, 2), rev(512, OFF_Z // 512), rev(128, OFF_SMALL // 128),
                  rev(256, OFF_GQ // 256), rev(256, OFF_GK // 256), rev(512, OFF_GV // 512), rev(512, OFF_GG // 512),
                  _const_spec((1, 128)), _const_spec((1, 128)), _const_spec((1, 128)),
                  _const_spec((128, 256)), _const_spec((1, 256)), _const_spec((1, 128)),
                  kept(HEAD_DIM, HEAD_DIM), kept(CHUNK, CHUNK), kept(GLA_KEY, HEAD_DIM), rev(2 * 512, 0)] + r_in_specs,
        out_specs=[rev(1536, 0), rev(SCAN_BWD_W, 0), _const_spec((1, 128)), _const_spec((1, 128)),
                   _const_spec((1, 128)), _const_spec((128, 256)), _const_spec((1, 256)), _const_spec((1, 128))]
        + r_out_specs,
        out_shape=[jax.ShapeDtypeStruct((n_b, t_len, 1536), F32), jax.ShapeDtypeStruct((n_b, t_len, PROJ_W), BF16),
                   vec, vec, vec, jax.ShapeDtypeStruct((128, 256), F32), jax.ShapeDtypeStruct((1, 256), F32), vec]
        + (list(rider.out_shapes) if rider else []),
        scratch_shapes=[pltpu.VMEM((n_b * HEADS, HEAD_DIM, HEAD_DIM), F32),
                        pltpu.VMEM((n_b * HEADS, GLA_KEY, HEAD_DIM), F32)] + r_sems,
        compiler_params=_cparams(("arbitrary",)),
    )(qkv3, qkv3, qkv3, proj3, proj3, proj3, proj3, proj3, proj3, *dn_params, *gla_params, hist, ainv, hist_gla, do3,
      *r_inputs)
    return dqkv.reshape(n_b * t_len, 1536), d_proj, (dal, ddt, dgn), (dw2, db2, dggn), rider_outs


def _gla_scan_fwd(proj, w2, b2, gn, o_mix, n_b, t_len):
    n_c = t_len // CHUNK
    spec = functools.partial(_chunk_spec, n_b, n_c=n_c)

    def body(q_ref, k_ref, v_ref, g_ref, sm_ref, w2_ref, b2_ref, gn_ref, _, o_ref, hist_ref, s_ref):
        @pl.when(pl.program_id(0) == 0)
        def _():
            s_ref[...] = jnp.zeros_like(s_ref)

        s_in = s_ref[...]
        hist_ref[...] = s_in
        og, s_out = _gla_chains(_stack_chains(q_ref, n_b, GLA_KSL), _stack_chains(k_ref, n_b, GLA_KSL),
                                _stack_chains(v_ref, n_b, DN_SL), _stack_chains(g_ref, n_b, DN_SL),
                                _per_chain(sm_ref, n_b), s_in, *_gate_weights(w2_ref, b2_ref, n_b), gn_ref[...])
        _unstack_chains(o_ref, og, n_b, DN_SL)
        s_ref[...] = s_out

    proj3 = proj.reshape(n_b, t_len, -1)
    o, hist = pl.pallas_call(
        body, name="gla_scan_fwd", grid=(n_c,),
        in_specs=[spec(256, OFF_GQ // 256), spec(256, OFF_GK // 256), spec(512, OFF_GV // 512),
                  spec(512, OFF_GG // 512), spec(128, OFF_SMALL // 128),
                  _const_spec((128, 256)), _const_spec((1, 256)), _const_spec((1, 128)),
                  pl.BlockSpec(memory_space=pl.ANY)],
        out_specs=[spec(512, 1), _hist_spec(n_b, GLA_KEY, n_c)],
        out_shape=[jax.ShapeDtypeStruct(o_mix.shape, BF16),
                   jax.ShapeDtypeStruct((n_c, n_b * HEADS, GLA_KEY, HEAD_DIM), F32)],
        input_output_aliases={8: 0},
        scratch_shapes=[pltpu.VMEM((n_b * HEADS, GLA_KEY, HEAD_DIM), F32)],
        compiler_params=_cparams(("arbitrary",)),
    )(proj3, proj3, proj3, proj3, proj3, w2, b2, gn, o_mix)
    return o.reshape(n_b * t_len, 2 * 512), hist


W_IN_RUNS = ((0, 256, GLA_W), (256, 1536, OFF_Z + 512), (1536, 2048, OFF_Z), (2048, 2056, OFF_SMALL),
             (2056, 3592, 0), (3592, 3608, OFF_SMALL + 8))
W_IN_ROWS = 256


def _w_in_pieces(cols_per_chip):
    out = []
    for first, last, start in W_IN_RUNS:
        for j in range(N_CHIPS):
            a, b = max(first, cols_per_chip * j), min(last, cols_per_chip * (j + 1))
            if a < b:
                out.append((j, a - cols_per_chip * j, b - cols_per_chip * j, start + a - first))
    return out


def _w_in_to_padded(w4):
    _, n_r, n_c = w4.shape

    def body(i_ref, o_ref):
        o_ref[...] = jnp.zeros_like(o_ref)
        for j, a, b, p in _w_in_pieces(n_c):
            o_ref[:, p:p + b - a] = i_ref[j, :, a:b]

    return pl.pallas_call(
        body, name="w_in_to_padded", grid=(n_r // W_IN_ROWS,),
        in_specs=[pl.BlockSpec((N_CHIPS, W_IN_ROWS, n_c), lambda i: (0, i, 0))],
        out_specs=pl.BlockSpec((W_IN_ROWS, PROJ_W), lambda i: (i, 0)),
        out_shape=jax.ShapeDtypeStruct((n_r, PROJ_W), w4.dtype), compiler_params=_cparams(("parallel",)),
    )(w4)


def _w_in_to_chips(g, n_c):
    n_r = g.shape[0]

    def body(i_ref, o_ref):
        for j, a, b, p in _w_in_pieces(n_c):
            o_ref[j, :, a:b] = i_ref[:, p:p + b - a]

    return pl.pallas_call(
        body, name="w_in_to_chips", grid=(n_r // W_IN_ROWS,),
        in_specs=[pl.BlockSpec((W_IN_ROWS, PROJ_W), lambda i: (i, 0))],
        out_specs=pl.BlockSpec((N_CHIPS, W_IN_ROWS, n_c), lambda i: (0, i, 0)),
        out_shape=jax.ShapeDtypeStruct((N_CHIPS, n_r, n_c), g.dtype), compiler_params=_cparams(("parallel",)),
    )(g)


def _lane_vec(v, offset=0):
    return jnp.zeros((1, 128), F32).at[0, offset:offset + v.shape[0]].set(v)


def _local_step(x, tgt, mod, p, n_b, t_len, comm=None):
    row1 = lambda v: v.reshape(1, -1)
    a_log, dt_bias = _lane_vec(p["dn_a_log"]), _lane_vec(p["dn_dt_bias"])
    dn_gn, gla_gn = row1(p["dn_norm_g"]), row1(p["gla_norm_g"])
    w2 = jnp.zeros((128, 256), F32).at[8:8 + GATE_RANK].set(p["gla_w_gate2"])
    b2 = row1(p["gla_b_gate"])
    ln0_g, ln0_b, ln1_g, ln1_b, ln2_g, ln2_b = (row1(p[k]) for k in ("ln0_g", "ln0_b", "ln1_g", "ln1_b", "ln2_g", "ln2_b"))
    conv_b = row1(p["ffn_conv_b"])

    x0, h1 = _ln0_fwd(x, ln0_g, ln0_b, mod, n_b, t_len)
    if comm:
        proj, landed_proj = _mm(h1, p["w_in_p"], name="mm_proj", rider=comm.proj_rider())
    else:
        proj = _mm(h1, p["w_in_p"], name="mm_proj")
    qkv = _dn_pre_fwd(proj, p["dn_conv"], n_b, t_len)
    o_half, hist_dn, landed_scan = _dn_scan_fwd(qkv, proj, a_log, dt_bias, dn_gn, n_b, t_len,
                                                rider=comm.scan_rider() if comm else None)
    if comm:
        p = {**p, **comm.weights_from(landed_proj, landed_scan)}
    o_mix, hist_gla = _gla_scan_fwd(proj, w2, b2, gla_gn, o_half, n_b, t_len)
    y = _mm(o_mix, p["w_o"], name="mm_wo")
    x1, h2 = _ln1_fwd(x0, y, ln1_g, ln1_b, mod, n_b, t_len)
    up = _mm(h2, p["w_up"], name="mm_up")
    act = _ffn_act_fwd(up, p["ffn_conv"], conv_b, n_b, t_len)
    y2 = _mm(act, p["w_down"], name="mm_down")

    loss, dx1, dy2, g_ln2_g, g_ln2_b, dgt_f = _ln2_loss_bwd(x1, y2, ln2_g, ln2_b, mod, tgt, n_b, t_len)
    g_w_down = _mm(act, dy2, ta=True, name="mm_g_down")
    d_act = _mm(dy2, p["w_down"], tb=True, name="mm_d_act")
    d_up, g_ffn_conv, g_conv_b = _ffn_act_bwd(up, p["ffn_conv"], conv_b, d_act, n_b, t_len)
    g_w_up = _mm(h2, d_up, ta=True, out_slabs=N_CHIPS, name="mm_g_up")
    if comm:
        dh2, from_sibling = _mm(d_up, p["w_up"], tb=True, name="mm_d_h2", rider=comm.ffn_pair_rider(g_w_up, g_w_down))
    else:
        dh2 = _mm(d_up, p["w_up"], tb=True, name="mm_d_h2")
    dx0, dy, g_ln1_g, g_ln1_b, dmod_1 = _ln1_bwd(x0, y, ln1_g, ln1_b, mod, dx1, dh2, n_b, t_len)
    g_w_o = _mm(o_mix, dy, ta=True, name="mm_g_wo")
    d_o = _mm(dy, p["w_o"], tb=True, name="mm_d_o")
    dqkv, d_proj, (g_a_log, g_dt_bias, g_dn_gn), (g_w2, g_b2, g_gla_gn), ffn_from_chips = _scan_bwd(
        qkv, proj, (a_log, dt_bias, dn_gn), (w2, b2, gla_gn), hist_dn, hist_gla, d_o, n_b, t_len,
        rider=comm.ffn_chips_rider(from_sibling) if comm else None)
    d_proj, g_dn_conv = _dn_pre_bwd(proj, p["dn_conv"], dqkv, d_proj.reshape(n_b * t_len, PROJ_W), n_b, t_len)
    g_w_in_p = _mm(h1, d_proj, ta=True, name="mm_g_win")
    if comm:
        dh1, tail_from_chips = _mm(d_proj, p["w_in_p"], tb=True, name="mm_d_h1",
                                   rider=comm.tail_chips_rider(g_w_in_p, g_w_o))
        from_chips = (ffn_from_chips, tail_from_chips)
    else:
        dh1, from_chips = _mm(d_proj, p["w_in_p"], tb=True, name="mm_d_h1"), None
    grad_x, g_ln0_g, g_ln0_b, dmod_0 = _ln0_bwd(x, ln0_g, ln0_b, mod, dx0, dh1, n_b, t_len)

    dmod = jnp.concatenate([dmod_0, dmod_1[:, 0:1], dmod_1[:, 1:3], dgt_f], axis=1)
    grads = {
        "ln0_g": g_ln0_g[0], "ln0_b": g_ln0_b[0], "w_in_p": g_w_in_p, "dn_conv": g_dn_conv,
        "dn_a_log": g_a_log[0, 0:HEADS], "dn_dt_bias": g_dt_bias[0, 0:HEADS], "dn_norm_g": g_dn_gn[0],
        "gla_w_gate2": g_w2[8:8 + GATE_RANK], "gla_b_gate": g_b2[0], "gla_norm_g": g_gla_gn[0],
        "w_o": g_w_o, "ln1_g": g_ln1_g[0], "ln1_b": g_ln1_b[0], "w_up": g_w_up,
        "ffn_conv": jnp.concatenate([g_ffn_conv[0], g_ffn_conv[1]], axis=1),
        "ffn_conv_b": jnp.concatenate([g_conv_b[0, 0], g_conv_b[1, 0]]), "w_down": g_w_down,
        "ln2_g": g_ln2_g[0], "ln2_b": g_ln2_b[0],
    }
    return loss, grad_x, grads, dmod, from_chips


def _col_sum(a):
    def body(a_ref, o_ref):
        o_ref[...] = jnp.sum(a_ref[...], 0, keepdims=True)

    return pl.pallas_call(body, name="col_sum", out_shape=jax.ShapeDtypeStruct((1, a.shape[1]), F32))(a)


def _adamw_math(w, grad, m, v):
    new_m = ADAM_B1 * m + (1.0 - ADAM_B1) * grad
    new_v = ADAM_B2 * v + (1.0 - ADAM_B2) * (grad * grad)
    m_hat = new_m / (1.0 - ADAM_B1 ** ADAM_STEP)
    v_hat = new_v / (1.0 - ADAM_B2 ** ADAM_STEP)
    return -ADAM_LR * (m_hat / (jnp.sqrt(v_hat) + ADAM_EPS) + ADAM_WD * w), new_m, new_v


def _adamw_many(ws, gs, ms, vs):
    n = len(ws)

    def body(*refs):
        for i in range(n):
            w_ref, g_ref, m_ref, v_ref = (refs[k * n + i] for k in range(4))
            d_ref, nm_ref, nv_ref = (refs[(4 + k) * n + i] for k in range(3))
            d_ref[...], nm_ref[...], nv_ref[...] = _adamw_math(w_ref[...], g_ref[...], m_ref[...], v_ref[...])

    outs = pl.pallas_call(
        body, name="adamw_small", out_shape=[jax.ShapeDtypeStruct(w.shape, F32) for w in ws] * 3,
    )(*ws, *gs, *ms, *vs)
    return outs[:n], outs[n:2 * n], outs[2 * n:]


def _adamw(w, g, m, v, name):
    n_r, n_c = w.shape
    if n_r % 8 == 0:
        tr = _pick(n_r, (256, 64, 32, 16, 8))
        grid, blk = (n_r // tr,), pl.BlockSpec((tr, n_c), lambda i: (i, 0))
    else:
        tc = _pick(n_c, (256, 128))
        grid, blk = (n_c // tc,), pl.BlockSpec((n_r, tc), lambda i: (0, i))

    def body(w_ref, g_ref, m_ref, v_ref, d_ref, nm_ref, nv_ref):
        d_ref[...], nm_ref[...], nv_ref[...] = _adamw_math(w_ref[...], g_ref[...], m_ref[...], v_ref[...])

    out = jax.ShapeDtypeStruct(w.shape, F32)
    return pl.pallas_call(
        body, name=name, grid=grid, in_specs=[blk] * 4, out_specs=[blk] * 3, out_shape=[out] * 3,
        compiler_params=_cparams(("parallel",)),
    )(w, g, m, v)


HBM_SPEC = pl.BlockSpec(memory_space=pltpu.HBM)
VMEM_SPEC = pl.BlockSpec(memory_space=pltpu.VMEM)
CHIP_FLIPS = ((1, 0), (0, 1), (1, 1))


def _place():
    return lax.axis_index("x"), lax.axis_index("y"), lax.axis_index("c")


def _flip(v, f):
    return 1 - v if f else v


def _all_gather8(slab, name):
    n_r, n_w = slab.shape

    def body(x_ref, o_ref, s_ref, send_sems, recv_sems, local_sem):
        x, y, c = _place()
        me = 4 * x + 2 * y + c
        mine = pltpu.make_async_copy(x_ref, o_ref.at[me], local_sem)
        mine.start()
        peers = [(_flip(x, k & 4), _flip(y, k & 2), _flip(c, k & 1)) for k in range(1, N_DEV)]
        sends = []
        for k, peer in enumerate(peers):
            cp = pltpu.make_async_remote_copy(src_ref=x_ref, dst_ref=o_ref.at[me], send_sem=send_sems.at[k],
                                              recv_sem=recv_sems.at[k], device_id=peer, device_id_type=MESH)
            cp.start()
            sends.append(cp)
        for k, (px, py, pc) in enumerate(peers):
            pltpu.make_async_remote_copy(src_ref=x_ref, dst_ref=o_ref.at[4 * px + 2 * py + pc],
                                         send_sem=send_sems.at[k], recv_sem=recv_sems.at[k],
                                         device_id=(px, py, pc), device_id_type=MESH).wait_recv()
        for cp in sends:
            cp.wait_send()
        mine.wait()
        total = o_ref[0]
        for d in range(1, N_DEV):
            total = total + o_ref[d]
        s_ref[...] = total

    return pl.pallas_call(
        body, name=name, in_specs=[VMEM_SPEC], out_specs=[VMEM_SPEC, VMEM_SPEC],
        out_shape=[jax.ShapeDtypeStruct((N_DEV, n_r, n_w), F32), jax.ShapeDtypeStruct((n_r, n_w), F32)],
        scratch_shapes=[pltpu.SemaphoreType.DMA((N_DEV - 1,)), pltpu.SemaphoreType.DMA((N_DEV - 1,)),
                        pltpu.SemaphoreType.DMA],
    )(slab)


SEQ_ROWS = 8


def _prologue(slab, w_ada_shard, b_shard, rider):
    n_r, n_w = slab.shape
    n_col = w_ada_shard.shape[1]
    r_inputs, r_in_specs, r_out_specs, r_sems, split = _with_rider(rider, 3, 3, 6)

    def body(*refs):
        (x_ref, w_ref, b_ref, g_ref, cond_ref, modr_ref, modp_ref, s1, r1, s2, r2, lsem), parts = split(refs)
        rider.first(*parts)
        x, y, c = _place()
        me = 4 * x + 2 * y + c
        peers = [(_flip(x, k & 4), _flip(y, k & 2), _flip(c, k & 1)) for k in range(1, N_DEV)]
        ids = [4 * px + 2 * py + pc for px, py, pc in peers]

        def exchange(src_of, dst, send_sems, recv_sems, own_sem):
            mine = pltpu.make_async_copy(src_of(me), dst.at[me], own_sem)
            mine.start()
            sends = [pltpu.make_async_remote_copy(src_ref=src_of(ids[k]), dst_ref=dst.at[me], send_sem=send_sems.at[k],
                                                  recv_sem=recv_sems.at[k], device_id=peers[k], device_id_type=MESH)
                     for k in range(N_DEV - 1)]
            for cp in sends:
                cp.start()
            for k in range(N_DEV - 1):
                pltpu.make_async_remote_copy(src_ref=src_of(ids[k]), dst_ref=dst.at[ids[k]], send_sem=send_sems.at[k],
                                             recv_sem=recv_sems.at[k], device_id=peers[k],
                                             device_id_type=MESH).wait_recv()
            for cp in sends:
                cp.wait_send()
            mine.wait()

        exchange(lambda d: x_ref, g_ref, s1, r1, lsem.at[0])
        cond = _silu(g_ref[:, 0:SEQ_ROWS, :].reshape(N_DEV * SEQ_ROWS, n_w))
        cond_ref[...] = cond
        modp_ref[...] = jnp.dot(cond.astype(BF16), w_ref[...].astype(BF16), preferred_element_type=F32) + b_ref[...]
        exchange(lambda d: modp_ref.at[pl.ds(pl.multiple_of(d * SEQ_ROWS, SEQ_ROWS), SEQ_ROWS)], modr_ref, s2, r2,
                 lsem.at[1])
        rider.last(*parts)

    sem7 = pltpu.SemaphoreType.DMA((N_DEV - 1,))
    gathered, cond, mod_recv, *rider_outs = pl.pallas_call(
        body, name="prologue", in_specs=[VMEM_SPEC] * 3 + r_in_specs, out_specs=[VMEM_SPEC] * 3 + r_out_specs,
        out_shape=[jax.ShapeDtypeStruct((N_DEV, n_r, n_w), F32), jax.ShapeDtypeStruct((N_DEV * SEQ_ROWS, n_w), F32),
                   jax.ShapeDtypeStruct((N_DEV, SEQ_ROWS, n_col), F32)] + list(rider.out_shapes),
        scratch_shapes=[pltpu.VMEM((N_DEV * SEQ_ROWS, n_col), F32), sem7, sem7, sem7, sem7,
                        pltpu.SemaphoreType.DMA((2,))] + r_sems,
        compiler_params=pltpu.CompilerParams(vmem_limit_bytes=VMEM_LIMIT),
    )(slab, w_ada_shard, b_shard, *r_inputs)
    return gathered, cond, mod_recv, rider_outs


def _gather_rider(shards):
    n_a = len(shards)

    def plan(ins, outs, sems):
        send_sems, recv_sems = sems
        x, y, c = _place()
        chips = [(_flip(x, fx), _flip(y, fy)) for fx, fy in CHIP_FLIPS]

        def copy(k, slot, chip_of_block, half, to, src=None):
            dst = outs[k].at[chip_of_block, half]
            return pltpu.make_async_remote_copy(src_ref=dst if src is None else src, dst_ref=dst,
                                                send_sem=send_sems.at[k * 6 + slot], recv_sem=recv_sems.at[k * 6 + slot],
                                                device_id=to, device_id_type=MESH)

        first = [copy(k, r, 2 * x + y, c, (*chips[r], c), src=ins[k].at[c]) for k in range(n_a) for r in range(3)]
        return copy, chips, first, (x, y, c)

    def first_step(ins, outs, sems):
        for cp in plan(ins, outs, sems)[2]:
            cp.start()

    def last_step(ins, outs, sems):
        copy, chips, first, (x, y, c) = plan(ins, outs, sems)
        passed = []
        for k in range(n_a):
            for r, (px, py) in enumerate(chips):
                copy(k, r, 2 * px + py, c, (x, y, c)).wait_recv()
                fwd = copy(k, 3 + r, 2 * px + py, c, (x, y, 1 - c))
                fwd.start()
                passed.append(fwd)
        for k in range(n_a):
            for r, (px, py) in enumerate(chips):
                copy(k, 3 + r, 2 * px + py, 1 - c, (x, y, c)).wait_recv()
        for cp in first + passed:
            cp.wait_send()

    return Rider(shards, [jax.ShapeDtypeStruct((N_CHIPS,) + s.shape, s.dtype) for s in shards],
                 [pltpu.SemaphoreType.DMA((6 * n_a,)), pltpu.SemaphoreType.DMA((6 * n_a,))], first_step, last_step)


def _place_own(gathered, shard, chip, name):
    _, _, n_h, n_c = gathered.shape
    th = _pick(n_h, (256, 176, 128))

    def body(sel_ref, s_ref, _, o_ref):
        o_ref[...] = s_ref[...]

    grid_spec = pltpu.PrefetchScalarGridSpec(
        num_scalar_prefetch=1, grid=(2, n_h // th),
        in_specs=[pl.BlockSpec((None, th, n_c), lambda hf, i, sel: (hf, i, 0)), pl.BlockSpec(memory_space=pl.ANY)],
        out_specs=pl.BlockSpec((None, None, th, n_c), lambda hf, i, sel: (sel[0], hf, i, 0)))
    return pl.pallas_call(
        body, name=name, grid_spec=grid_spec, out_shape=jax.ShapeDtypeStruct(gathered.shape, gathered.dtype),
        input_output_aliases={2: 0}, compiler_params=_cparams(("parallel", "parallel")),
    )(chip.reshape(1), shard, gathered)


def _pair_rider(parts):
    n_a = len(parts)

    def plan(ins, outs, sems):
        send_sems, recv_sems = sems
        x, y, c = _place()
        return [pltpu.make_async_remote_copy(src_ref=ins[k].at[:, 1 - c], dst_ref=outs[k], send_sem=send_sems.at[k],
                                             recv_sem=recv_sems.at[k], device_id=(x, y, 1 - c), device_id_type=MESH)
                for k in range(n_a)]

    def first_step(ins, outs, sems):
        for cp in plan(ins, outs, sems):
            cp.start()

    def last_step(ins, outs, sems):
        for cp in plan(ins, outs, sems):
            cp.wait()

    return Rider(parts, [jax.ShapeDtypeStruct((N_CHIPS,) + p.shape[2:], F32) for p in parts],
                 [pltpu.SemaphoreType.DMA((n_a,)), pltpu.SemaphoreType.DMA((n_a,))], first_step, last_step)


def _alone(rider, name):
    n_a = len(rider.inputs)

    def body(*refs):
        parts = (refs[:n_a], refs[n_a:2 * n_a], refs[2 * n_a:])
        rider.first(*parts)
        rider.last(*parts)

    return pl.pallas_call(
        body, name=name, in_specs=[HBM_SPEC] * n_a, out_specs=[HBM_SPEC] * n_a,
        out_shape=rider.out_shapes, scratch_shapes=rider.sems,
    )(*rider.inputs)


def _chips_rider(sums):
    n_a = len(sums)

    def plan(ins, outs, sems):
        send_sems, recv_sems = sems
        x, y, c = _place()
        cps = []
        for k in range(n_a):
            for r, (fx, fy) in enumerate(CHIP_FLIPS):
                px, py = _flip(x, fx), _flip(y, fy)
                cps.append(pltpu.make_async_remote_copy(
                    src_ref=ins[k].at[2 * px + py], dst_ref=outs[k].at[r], send_sem=send_sems.at[3 * k + r],
                    recv_sem=recv_sems.at[3 * k + r], device_id=(px, py, c), device_id_type=MESH))
        return cps

    def first_step(ins, outs, sems):
        for cp in plan(ins, outs, sems):
            cp.start()

    def last_step(ins, outs, sems):
        for cp in plan(ins, outs, sems):
            cp.wait()

    return Rider(sums, [jax.ShapeDtypeStruct((3,) + s.shape[1:], s.dtype) for s in sums],
                 [pltpu.SemaphoreType.DMA((3 * n_a,)), pltpu.SemaphoreType.DMA((3 * n_a,))], first_step, last_step)


def _rs_share(bufs):
    n_a = len(bufs)

    def body(*refs):
        ins, outs = refs[:n_a], refs[n_a:2 * n_a]
        send_sems, recv_sems = refs[2 * n_a:]
        x, y, c = _place()
        sends = [pltpu.make_async_remote_copy(src_ref=ins[k].at[c], dst_ref=outs[k].at[c], send_sem=send_sems.at[k],
                                              recv_sem=recv_sems.at[k], device_id=(x, y, 1 - c), device_id_type=MESH)
                 for k in range(n_a)]
        for cp in sends:
            cp.start()
        for k in range(n_a):
            pltpu.make_async_remote_copy(src_ref=ins[k].at[c], dst_ref=outs[k].at[1 - c], send_sem=send_sems.at[k],
                                         recv_sem=recv_sems.at[k], device_id=(x, y, 1 - c),
                                         device_id_type=MESH).wait_recv()
        for cp in sends:
            cp.wait_send()

    return pl.pallas_call(
        body, name="rs_share", in_specs=[HBM_SPEC] * n_a, out_specs=[HBM_SPEC] * n_a,
        out_shape=[jax.ShapeDtypeStruct(s.shape, F32) for s in bufs],
        input_output_aliases={k: k for k in range(n_a)},
        scratch_shapes=[pltpu.SemaphoreType.DMA((n_a,)), pltpu.SemaphoreType.DMA((n_a,))],
    )(*bufs)


def _pair_add(part, recv, core, name):
    _, _, n_h, n_c = part.shape
    th = _pick(n_h, (256, 176, 128))

    def body(sel_ref, p_ref, r_ref, o_ref):
        o_ref[...] = (p_ref[...] + r_ref[...]).astype(BF16)

    grid_spec = pltpu.PrefetchScalarGridSpec(
        num_scalar_prefetch=1, grid=(N_CHIPS, n_h // th),
        in_specs=[pl.BlockSpec((None, None, th, n_c), lambda j, i, sel: (j, sel[0], i, 0)),
                  pl.BlockSpec((None, th, n_c), lambda j, i, sel: (j, i, 0))],
        out_specs=pl.BlockSpec((None, th, n_c), lambda j, i, sel: (j, i, 0)))
    return pl.pallas_call(
        body, name=name, grid_spec=grid_spec, out_shape=jax.ShapeDtypeStruct(recv.shape, BF16),
        compiler_params=_cparams(("parallel", "parallel")),
    )(core.reshape(1), part, recv)


def _chip_add(sums, recv, chip, core, name):
    _, n_h, n_c = sums.shape
    th = _pick(n_h, (256, 176, 128))

    def body(sel_ref, s_ref, r_ref, o_ref):
        total = s_ref[...].astype(F32)
        for r in range(3):
            total = total + r_ref[r].astype(F32)
        o_ref[...] = total

    grid_spec = pltpu.PrefetchScalarGridSpec(
        num_scalar_prefetch=1, grid=(n_h // th,),
        in_specs=[pl.BlockSpec((None, th, n_c), lambda i, sel: (sel[0], i, 0)),
                  pl.BlockSpec((3, th, n_c), lambda i, sel: (0, i, 0))],
        out_specs=pl.BlockSpec((None, th, n_c), lambda i, sel: (sel[1], i, 0)))
    return pl.pallas_call(
        body, name=name, grid_spec=grid_spec, out_shape=jax.ShapeDtypeStruct((2, n_h, n_c), F32),
        compiler_params=_cparams(("parallel",)),
    )(jnp.stack([chip, core]), sums, recv)


def _row_halves(a):
    return a.reshape(N_CHIPS, 2, -1, a.shape[-1])


class StepComm:
    REST = ("w_o", "w_up", "w_down")

    def __init__(self, core, chip, rest_shards, in_cols):
        self.core, self.chip, self.shards, self.in_cols = core, chip, rest_shards, in_cols

    def proj_rider(self):
        return _gather_rider([self.shards[0], self.shards[2]])

    def scan_rider(self):
        return _gather_rider([self.shards[1]])

    def weights_from(self, landed_proj, landed_scan):
        landed = (landed_proj[0], landed_scan[0], landed_proj[1])
        g_o, g_up, g_down = (_place_own(g, s, self.chip, "place_own_" + n)
                             for g, s, n in zip(landed, self.shards, self.REST))
        return {"w_o": g_o.reshape(-1, D_MODEL), "w_up": g_up.reshape(N_CHIPS, -1, g_up.shape[-1]),
                "w_down": g_down.reshape(-1, D_MODEL)}

    def _add_pairs(self, parts, from_sibling, names):
        return [_pair_add(p, r, self.core, "pair_add_" + n) for p, r, n in zip(parts, from_sibling, names)]

    def ffn_pair_rider(self, g_w_up, g_w_down):
        self.ffn_parts = [_row_halves(g_w_up), _row_halves(g_w_down)]
        return _pair_rider(self.ffn_parts)

    def ffn_chips_rider(self, from_sibling):
        self.ffn_sums = self._add_pairs(self.ffn_parts, from_sibling, ("w_up", "w_down"))
        return _chips_rider(self.ffn_sums)

    def tail_chips_rider(self, g_w_in_p, g_w_o):
        parts = [_row_halves(_w_in_to_chips(g_w_in_p, self.in_cols)), _row_halves(g_w_o)]
        self.tail_sums = self._add_pairs(parts, _alone(_pair_rider(parts), "rs_pair_tail"), ("w_in", "w_o"))
        return _chips_rider(self.tail_sums)

    def finish(self, ffn_from_chips, tail_from_chips):
        halves = [_chip_add(s, r, self.chip, self.core, "chip_add_" + n)
                  for s, r, n in zip(self.tail_sums + self.ffn_sums, list(tail_from_chips) + list(ffn_from_chips),
                                     ("w_in", "w_o", "w_up", "w_down"))]
        return [f.reshape(-1, f.shape[-1]) for f in _rs_share(halves)]


SLAB_W = 1024


def _pack(arrays, rows):
    flat = jnp.concatenate([a.reshape(-1).astype(F32) for a in arrays])
    return jnp.pad(flat, (0, rows * SLAB_W - flat.shape[0])).reshape(rows, SLAB_W)


def _unpack(flat, shapes):
    out, off = [], 0
    for s in shapes:
        n = 1
        for d in s:
            n *= d
        out.append(flat[off:off + n].reshape(s))
        off += n
    return out


def _rows_for(arrays_or_shapes):
    n = 0
    for a in arrays_or_shapes:
        s = a if isinstance(a, tuple) else a.shape
        k = 1
        for d in s:
            k *= d
        n += k
    return -(-n // (8 * SLAB_W)) * 8


def kernel(x, c, ln0_g, ln0_b, w_ada, b_ada, w_in, dn_conv, dn_a_log, dn_dt_bias, dn_norm_g, gla_w_gate2, gla_b_gate, gla_norm_g, w_o, ln1_g, ln1_b, ffn_w_up, ffn_conv, ffn_conv_b, ffn_w_down, ln2_g, ln2_b, loss_target, m_ln0_g, m_ln0_b, m_w_ada, m_b_ada, m_w_in, m_dn_conv, m_dn_a_log, m_dn_dt_bias, m_dn_norm_g, m_gla_w_gate2, m_gla_b_gate, m_gla_norm_g, m_w_o, m_ln1_g, m_ln1_b, m_ffn_w_up, m_ffn_conv, m_ffn_conv_b, m_ffn_w_down, m_ln2_g, m_ln2_b, v_ln0_g, v_ln0_b, v_w_ada, v_b_ada, v_w_in, v_dn_conv, v_dn_a_log, v_dn_dt_bias, v_dn_norm_g, v_gla_w_gate2, v_gla_b_gate, v_gla_norm_g, v_w_o, v_ln1_g, v_ln1_b, v_ffn_w_up, v_ffn_conv, v_ffn_conv_b, v_ffn_w_down, v_ln2_g, v_ln2_b):
    n_b, t_len, _ = x.shape
    xi, yi, ci = _place()
    chip = (2 * xi + yi).astype(jnp.int32)
    core = ci.astype(jnp.int32)
    n_all = N_DEV * n_b
    ada_cols = w_ada.shape[2]

    halves = lambda a: a.astype(BF16).reshape(2, a.shape[0] // 2, a.shape[1])
    w_in_halves = halves(w_in[0])
    sharded_small = [dn_conv[0], gla_w_gate2[0], ffn_conv[0]]
    slab = jnp.concatenate([_pack([c], SEQ_ROWS), _pack(sharded_small, _rows_for(sharded_small))], axis=0)
    b_ada_shard = lax.dynamic_slice(b_ada, (0, chip * ada_cols), (1, ada_cols))
    gathered, cond_pad, mod_recv, (g_in,) = _prologue(slab, w_ada[0], b_ada_shard, _gather_rider([w_in_halves]))
    g_in = _place_own(g_in, w_in_halves, chip, "place_own_w_in")
    cond_all = cond_pad.reshape(N_DEV, SEQ_ROWS, D_MODEL)[:, :n_b].reshape(n_all, D_MODEL)
    by_chip = gathered.reshape(N_DEV, -1)[0::2]
    full, off = [], SEQ_ROWS * SLAB_W
    for a in sharded_small:
        blocks = by_chip[:, off:off + a.size].reshape(N_CHIPS, *a.shape)
        full.append(blocks.transpose(1, 0, 2).reshape(a.shape[0], N_CHIPS * a.shape[1]))
        off += a.size
    dn_conv_f, gate2_f, ffn_conv_f = full
    mod = mod_recv[0::2, :n_b].transpose(1, 0, 2).reshape(n_b, 6, D_MODEL)

    comm = StepComm(core, chip, [halves(w_o[0]), halves(ffn_w_up[0]), halves(ffn_w_down[0])], w_in.shape[2])
    params = {
        "w_in_p": _w_in_to_padded(g_in.reshape(N_CHIPS, -1, g_in.shape[-1])),
        "dn_conv": dn_conv_f, "dn_a_log": dn_a_log[0], "dn_dt_bias": dn_dt_bias[0], "dn_norm_g": dn_norm_g[0],
        "gla_w_gate2": gate2_f, "gla_b_gate": gla_b_gate[0], "gla_norm_g": gla_norm_g[0],
        "ln0_g": ln0_g, "ln0_b": ln0_b, "ln1_g": ln1_g[0], "ln1_b": ln1_b[0], "ln2_g": ln2_g[0], "ln2_b": ln2_b[0],
        "ffn_conv": ffn_conv_f, "ffn_conv_b": ffn_conv_b[0],
    }

    loss_row, grad_x, gp, dmod, from_chips = _local_step(
        x.reshape(n_b * t_len, D_MODEL), loss_target.reshape(n_b * t_len, D_MODEL), mod, params, n_b, t_len, comm)
    names = ["ln0_g", "ln0_b", "w_ada", "b_ada", "w_in", "dn_conv", "dn_a_log", "dn_dt_bias", "dn_norm_g",
             "gla_w_gate2", "gla_b_gate", "gla_norm_g", "w_o", "ln1_g", "ln1_b", "ffn_w_up", "ffn_conv", "ffn_conv_b",
             "ffn_w_down", "ln2_g", "ln2_b"]
    weights = dict(zip(names, [ln0_g, ln0_b, w_ada, b_ada, w_in, dn_conv, dn_a_log, dn_dt_bias, dn_norm_g, gla_w_gate2,
                               gla_b_gate, gla_norm_g, w_o, ln1_g, ln1_b, ffn_w_up, ffn_conv, ffn_conv_b, ffn_w_down,
                               ln2_g, ln2_b]))
    m_in = dict(zip(names, [m_ln0_g, m_ln0_b, m_w_ada, m_b_ada, m_w_in, m_dn_conv, m_dn_a_log, m_dn_dt_bias,
                            m_dn_norm_g, m_gla_w_gate2, m_gla_b_gate, m_gla_norm_g, m_w_o, m_ln1_g, m_ln1_b,
                            m_ffn_w_up, m_ffn_conv, m_ffn_conv_b, m_ffn_w_down, m_ln2_g, m_ln2_b]))
    v_in = dict(zip(names, [v_ln0_g, v_ln0_b, v_w_ada, v_b_ada, v_w_in, v_dn_conv, v_dn_a_log, v_dn_dt_bias,
                            v_dn_norm_g, v_gla_w_gate2, v_gla_b_gate, v_gla_norm_g, v_w_o, v_ln1_g, v_ln1_b,
                            v_ffn_w_up, v_ffn_conv, v_ffn_conv_b, v_ffn_w_down, v_ln2_g, v_ln2_b]))
    grads, delta, new_m, new_v = {}, {}, {}, {}

    def adamw_big(n, grad):
        view = (lambda a: a.T) if n == "w_in" else (lambda a: a)
        outs = _adamw(view(weights[n][0]), view(grad), view(m_in[n][0]), view(v_in[n][0]), "adamw_" + n)
        grads[n] = grad[None]
        delta[n], new_m[n], new_v[n] = (view(a)[None] for a in outs)

    g_w_in, g_w_o, g_w_up, g_w_down = comm.finish(*from_chips)

    summed_names = ["loss", "ln0_g", "ln0_b", "dn_conv", "dn_a_log", "dn_dt_bias", "dn_norm_g", "gla_w_gate2",
                    "gla_b_gate", "gla_norm_g", "ln1_g", "ln1_b", "ffn_conv", "ffn_conv_b", "ln2_g", "ln2_b"]
    summed_parts = [loss_row[0, 0:1]] + [gp[n] for n in summed_names[1:]]
    sum_rows = _rows_for(summed_parts)
    slab = jnp.concatenate([_pack(summed_parts, sum_rows), _pack([dmod], _rows_for([dmod]))], axis=0)
    gathered, total = _all_gather8(slab, "reduce_small")
    small_g = dict(zip(summed_names, _unpack(total.reshape(-1), [a.shape for a in summed_parts])))
    loss = small_g["loss"][0]
    dmod_rows = n_b * 6 * D_MODEL // SLAB_W
    dmod_all = gathered[:, sum_rows:sum_rows + dmod_rows, :].reshape(n_all, 6 * D_MODEL)
    for n, grad in (("ffn_w_up", g_w_up), ("ffn_w_down", g_w_down), ("w_o", g_w_o), ("w_in", g_w_in)):
        adamw_big(n, grad)

    g_b_ada = _col_sum(dmod_all)
    dmod_cols = lax.dynamic_slice(dmod_all, (0, chip * ada_cols), (n_all, ada_cols))
    adamw_big("w_ada", _mm(cond_all, dmod_cols, ta=True, name="mm_g_ada"))

    col_block = lambda a: lax.dynamic_slice(a, (0, chip * (a.shape[1] // N_CHIPS)), (a.shape[0], a.shape[1] // N_CHIPS))
    grads.update({
        "ln0_g": small_g["ln0_g"], "ln0_b": small_g["ln0_b"], "b_ada": g_b_ada,
        "dn_conv": col_block(small_g["dn_conv"])[None], "dn_a_log": small_g["dn_a_log"][None],
        "dn_dt_bias": small_g["dn_dt_bias"][None], "dn_norm_g": small_g["dn_norm_g"][None],
        "gla_w_gate2": col_block(small_g["gla_w_gate2"])[None], "gla_b_gate": small_g["gla_b_gate"][None],
        "gla_norm_g": small_g["gla_norm_g"][None], "ln1_g": small_g["ln1_g"][None],
        "ln1_b": small_g["ln1_b"][None], "ffn_conv": col_block(small_g["ffn_conv"])[None],
        "ffn_conv_b": small_g["ffn_conv_b"][None], "ln2_g": small_g["ln2_g"][None], "ln2_b": small_g["ln2_b"][None],
    })
    small = [n for n in names if n not in delta]
    d_s, m_s, v_s = _adamw_many([weights[n] for n in small], [grads[n] for n in small],
                                [m_in[n] for n in small], [v_in[n] for n in small])
    for out, vals in ((delta, d_s), (new_m, m_s), (new_v, v_s)):
        out.update(zip(small, vals))

    return (loss, grad_x.reshape(x.shape), *[grads[n] for n in names], *[delta[n] for n in names],
            *[new_m[n] for n in names], *[new_v[n] for n in names])
```

```python
import functools

import jax
import jax.numpy as jnp
from jax import lax
from jax.experimental import pallas as pl
from jax.experimental.pallas import tpu as pltpu

F32 = jnp.float32
BF16 = jnp.bfloat16
MESH = pl.DeviceIdType.MESH

D_MODEL = 1024
HEADS = 4
HEAD_DIM = 128
GLA_KEY = 64
GATE_RANK = 16
CHUNK = 64
D_FF = 2816
ALPHA = 2.0 ** 0.25
EPS = 1e-6
N_CHIPS = 4
N_DEV = 8

PROJ_W = 3840
OFF_GQ, OFF_GK, OFF_GV, OFF_GG, OFF_SMALL, GLA_W = 0, 256, 512, 1024, 1536, 1792
OFF_Z = 2048
W_IN_COLS = 3608


def _qkv_block(j):
    return jnp.where(j < 2, GLA_W // 128 + j, (OFF_Z + 512) // 128 - 2 + j)

ADAM_LR, ADAM_B1, ADAM_B2, ADAM_EPS, ADAM_WD, ADAM_STEP = 0.001, 0.9, 0.999, 1e-08, 0.01, 10

VMEM_LIMIT = 56 * 1024 * 1024
ROW_TILE = 512


def _cparams(sem):
    return pltpu.CompilerParams(dimension_semantics=sem, vmem_limit_bytes=VMEM_LIMIT)


def _pick(n, prefs):
    for p in prefs:
        if n % p == 0:
            return p
    return n


def _mm(a, b, *, ta=False, tb=False, out_slabs=1, out_dtype=F32, name, rider=None):
    a_slabs = a.shape[0] if a.ndim == 3 else 1
    b_slabs = b.shape[0] if b.ndim == 3 else 1
    assert not (ta and a_slabs > 1)
    a2, b2 = a.shape[-2:], b.shape[-2:]
    if ta:
        k_dim, m_dim = a2
    else:
        m_dim, k_dim = a2[0], a2[1] * a_slabs
    n_dim = b2[0] if tb else b2[1] * b_slabs
    k_slabs = max(a_slabs, b_slabs if tb else 1)
    n_slabs = max(out_slabs, 1 if tb else b_slabs)
    tm = _pick(m_dim, (1024, 1408, 512, 256, 128))
    tn = _pick(n_dim // n_slabs, (1536, 1408, 1280, 1024, 768, 512, 384, 256, 128))
    tk = _pick(k_dim // k_slabs, (1408, 1280, 1024, 512, 256, 128))
    nk, nj = k_dim // tk, n_dim // tn
    nk_a, nk_b, nj_b, nj_o = nk // a_slabs, nk // b_slabs, nj // b_slabs, nj // out_slabs
    dims = (((0 if ta else 1,), (1 if tb else 0,)), ((), ()))

    grid = (m_dim // tm, nj, nk)
    assert out_dtype == F32
    r_inputs, r_in_specs, r_out_specs, r_sems, split = _with_rider(rider, 2, 1, 0)

    def body(*refs):
        (a_ref, b_ref, o_ref), parts = split(refs)
        ride_first, ride_last = _ride(rider, parts, grid)
        if rider is not None:
            ride_first()
        prod = lax.dot_general(a_ref[...].astype(BF16), b_ref[...].astype(BF16), dims, preferred_element_type=F32)
        if nk == 1:
            o_ref[...] = prod
        else:
            _acc(o_ref, prod, pl.program_id(2) == 0)
        if rider is not None:
            ride_last()

    if ta:
        a_spec = pl.BlockSpec((tk, tm), lambda i, j, k: (k, i))
    elif a_slabs > 1:
        a_spec = pl.BlockSpec((None, tm, tk), lambda i, j, k: (k // nk_a, i, k % nk_a))
    else:
        a_spec = pl.BlockSpec((tm, tk), lambda i, j, k: (i, k))
    if tb and b_slabs > 1:
        b_spec = pl.BlockSpec((None, tn, tk), lambda i, j, k: (k // nk_b, j, k % nk_b))
    elif tb:
        b_spec = pl.BlockSpec((tn, tk), lambda i, j, k: (j, k))
    elif b_slabs > 1:
        b_spec = pl.BlockSpec((None, tk, tn), lambda i, j, k: (j // nj_b, k, j % nj_b))
    else:
        b_spec = pl.BlockSpec((tk, tn), lambda i, j, k: (k, j))
    if out_slabs > 1:
        o_spec = pl.BlockSpec((None, tm, tn), lambda i, j, k: (j // nj_o, i, j % nj_o))
        o_shape = (out_slabs, m_dim, n_dim // out_slabs)
    else:
        o_spec, o_shape = pl.BlockSpec((tm, tn), lambda i, j, k: (i, j)), (m_dim, n_dim)
    out, *rider_outs = pl.pallas_call(
        body, name=name, grid=grid,
        in_specs=[a_spec, b_spec] + r_in_specs, out_specs=[o_spec] + r_out_specs,
        out_shape=[jax.ShapeDtypeStruct(o_shape, out_dtype)] + (list(rider.out_shapes) if rider else []),
        scratch_shapes=r_sems,
        compiler_params=_cparams(("arbitrary",) * 3 if rider else ("parallel", "parallel", "arbitrary")),
    )(a, b, *r_inputs)
    return (out, rider_outs) if rider else out


def _ln(x, g, b):
    mu = jnp.mean(x, -1, keepdims=True)
    xc = x - mu
    var = jnp.mean(xc * xc, -1, keepdims=True)
    return xc * lax.rsqrt(var + EPS) * g + b


def _softplus(x):
    return jnp.maximum(x, 0.0) + jnp.log(1.0 + jnp.exp(-jnp.abs(x)))


def _silu(x):
    return x * jax.nn.sigmoid(x)


def _dsilu(x):
    s = jax.nn.sigmoid(x)
    return s * (1.0 + x * (1.0 - s))


def _f_ln0(x, g, b, sc, sh):
    x0 = _ln(x, g, b)
    return x0, x0 * (1.0 + sc) + sh


def _f_ln1(x0, y, gt, g, b, sc, sh):
    x1 = _ln(ALPHA * x0 + (1.0 + gt) * y, g, b)
    return x1, x1 * (1.0 + sc) + sh


def _f_ln2_loss(x1, y2, gt, g, b, tgt):
    x2 = _ln(ALPHA * x1 + (1.0 + gt) * y2, g, b)
    err = x2 - tgt
    per_row = jnp.sum(err * err, -1, keepdims=True) * (0.5 / D_MODEL)
    return jnp.sum(per_row, 0, keepdims=True)


def _row_specs(t_len):
    nt = t_len // ROW_TILE
    row = pl.BlockSpec((ROW_TILE, D_MODEL), lambda b, i: (b * nt + i, 0))
    vec = pl.BlockSpec((1, D_MODEL), lambda b, i: (0, 0))
    mod = pl.BlockSpec((None, 6, D_MODEL), lambda b, i: (b, 0, 0))
    return nt, row, vec, mod


def _first_step():
    return jnp.logical_and(pl.program_id(0) == 0, pl.program_id(1) == 0)


def _acc(ref, val, first, at=(Ellipsis,)):
    @pl.when(first)
    def _():
        ref[at] = val

    @pl.when(jnp.logical_not(first))
    def _():
        ref[at] += val


def _acc_rows(ref, rows, first):
    for i, r in enumerate(rows):
        _acc(ref, r, first, at=(slice(i, i + 1), slice(None)))


def _ln0_fwd(x, g, b, mod, n_b, t_len):
    nt, row, vec, mods = _row_specs(t_len)

    def body(x_ref, g_ref, b_ref, mod_ref, x0_ref, h_ref):
        x0, h = _f_ln0(x_ref[...], g_ref[...], b_ref[...], mod_ref[1:2, :], mod_ref[0:1, :])
        x0_ref[...] = x0
        h_ref[...] = h.astype(BF16)

    return pl.pallas_call(
        body, name="ln0_fwd", grid=(n_b, nt), in_specs=[row, vec, vec, mods], out_specs=[row, row],
        out_shape=[jax.ShapeDtypeStruct(x.shape, F32), jax.ShapeDtypeStruct(x.shape, BF16)],
        compiler_params=_cparams(("parallel", "parallel")),
    )(x, g, b, mod)


def _ln0_bwd(x, g, b, mod, dx0, dh, n_b, t_len):
    nt, row, vec, mods = _row_specs(t_len)
    dmod_spec = pl.BlockSpec((None, 2, D_MODEL), lambda bb, i: (bb, 0, 0))

    def body(x_ref, g_ref, b_ref, mod_ref, dx0_ref, dh_ref, dx_ref, dg_ref, db_ref, dmod_ref):
        _, pull = jax.vjp(_f_ln0, x_ref[...], g_ref[...], b_ref[...], mod_ref[1:2, :], mod_ref[0:1, :])
        dx, dg, db, dsc, dsh = pull((dx0_ref[...], dh_ref[...]))
        dx_ref[...] = dx
        _acc(dg_ref, dg, _first_step())
        _acc(db_ref, db, _first_step())
        _acc_rows(dmod_ref, [dsh, dsc], pl.program_id(1) == 0)

    return pl.pallas_call(
        body, name="ln0_bwd", grid=(n_b, nt), in_specs=[row, vec, vec, mods, row, row],
        out_specs=[row, vec, vec, dmod_spec],
        out_shape=[jax.ShapeDtypeStruct(x.shape, F32), jax.ShapeDtypeStruct((1, D_MODEL), F32),
                   jax.ShapeDtypeStruct((1, D_MODEL), F32), jax.ShapeDtypeStruct((n_b, 2, D_MODEL), F32)],
        compiler_params=_cparams(("arbitrary", "arbitrary")),
    )(x, g, b, mod, dx0, dh)


def _ln1_fwd(x0, y, g, b, mod, n_b, t_len):
    nt, row, vec, mods = _row_specs(t_len)

    def body(x0_ref, y_ref, g_ref, b_ref, mod_ref, x1_ref, h_ref):
        x1, h = _f_ln1(x0_ref[...], y_ref[...], mod_ref[2:3, :], g_ref[...], b_ref[...],
                       mod_ref[4:5, :], mod_ref[3:4, :])
        x1_ref[...] = x1
        h_ref[...] = h.astype(BF16)

    return pl.pallas_call(
        body, name="ln1_fwd", grid=(n_b, nt), in_specs=[row, row, vec, vec, mods], out_specs=[row, row],
        out_shape=[jax.ShapeDtypeStruct(x0.shape, F32), jax.ShapeDtypeStruct(x0.shape, BF16)],
        compiler_params=_cparams(("parallel", "parallel")),
    )(x0, y, g, b, mod)


def _ln1_bwd(x0, y, g, b, mod, dx1, dh, n_b, t_len):
    nt, row, vec, mods = _row_specs(t_len)
    dmod_spec = pl.BlockSpec((None, 3, D_MODEL), lambda bb, i: (bb, 0, 0))

    def body(x0_ref, y_ref, g_ref, b_ref, mod_ref, dx1_ref, dh_ref, dx0_ref, dy_ref, dg_ref, db_ref, dmod_ref):
        _, pull = jax.vjp(_f_ln1, x0_ref[...], y_ref[...], mod_ref[2:3, :], g_ref[...], b_ref[...],
                          mod_ref[4:5, :], mod_ref[3:4, :])
        dx0, dy, dgt, dg, db, dsc, dsh = pull((dx1_ref[...], dh_ref[...]))
        dx0_ref[...] = dx0
        dy_ref[...] = dy.astype(BF16)
        _acc(dg_ref, dg, _first_step())
        _acc(db_ref, db, _first_step())
        _acc_rows(dmod_ref, [dgt, dsh, dsc], pl.program_id(1) == 0)

    return pl.pallas_call(
        body, name="ln1_bwd", grid=(n_b, nt), in_specs=[row, row, vec, vec, mods, row, row],
        out_specs=[row, row, vec, vec, dmod_spec],
        out_shape=[jax.ShapeDtypeStruct(x0.shape, F32), jax.ShapeDtypeStruct(x0.shape, BF16),
                   jax.ShapeDtypeStruct((1, D_MODEL), F32), jax.ShapeDtypeStruct((1, D_MODEL), F32),
                   jax.ShapeDtypeStruct((n_b, 3, D_MODEL), F32)],
        compiler_params=_cparams(("arbitrary", "arbitrary")),
    )(x0, y, g, b, mod, dx1, dh)


def _ln2_loss_bwd(x1, y2, g, b, mod, tgt, n_b, t_len):
    nt, row, vec, mods = _row_specs(t_len)
    one = pl.BlockSpec((1, 128), lambda bb, i: (0, 0))
    dmod_spec = pl.BlockSpec((None, 1, D_MODEL), lambda bb, i: (bb, 0, 0))

    def body(x1_ref, y2_ref, g_ref, b_ref, mod_ref, t_ref, loss_ref, dx1_ref, dy2_ref, dg_ref, db_ref, dgt_ref):
        loss, pull = jax.vjp(functools.partial(_f_ln2_loss, tgt=t_ref[...]), x1_ref[...], y2_ref[...],
                             mod_ref[5:6, :], g_ref[...], b_ref[...])
        dx1, dy2, dgt, dg, db = pull(jnp.ones((1, 1), F32))
        dx1_ref[...] = dx1
        dy2_ref[...] = dy2.astype(BF16)
        _acc(loss_ref, jnp.broadcast_to(loss, (1, 128)), _first_step())
        _acc(dg_ref, dg, _first_step())
        _acc(db_ref, db, _first_step())
        _acc(dgt_ref, dgt, pl.program_id(1) == 0)

    return pl.pallas_call(
        body, name="ln2_loss_bwd", grid=(n_b, nt), in_specs=[row, row, vec, vec, mods, row],
        out_specs=[one, row, row, vec, vec, dmod_spec],
        out_shape=[jax.ShapeDtypeStruct((1, 128), F32), jax.ShapeDtypeStruct(x1.shape, F32),
                   jax.ShapeDtypeStruct(x1.shape, BF16), jax.ShapeDtypeStruct((1, D_MODEL), F32),
                   jax.ShapeDtypeStruct((1, D_MODEL), F32), jax.ShapeDtypeStruct((n_b, 1, D_MODEL), F32)],
        compiler_params=_cparams(("arbitrary", "arbitrary")),
    )(x1, y2, g, b, mod, tgt)


def _shift_down(x, s):
    if s == 0:
        return x
    rows = lax.broadcasted_iota(jnp.int32, x.shape, 0)
    return jnp.where(rows >= s, pltpu.roll(x, s, 0), 0.0)


def _shift_up(x, s):
    if s == 0:
        return x
    t_len = x.shape[0]
    rows = lax.broadcasted_iota(jnp.int32, x.shape, 0)
    return jnp.where(rows < t_len - s, pltpu.roll(x, t_len - s, 0), 0.0)


def _taps(x, k_w):
    return [_shift_down(x, k_w - 1 - k) for k in range(k_w)]


def _conv(taps, w):
    out = w[0:1, :] * taps[0]
    for k in range(1, len(taps)):
        out = out + w[k:k + 1, :] * taps[k]
    return out


def _conv_bwd(taps, w, du):
    k_w = len(taps)
    dx = w[k_w - 1:k_w, :] * du
    for k in range(k_w - 1):
        dx = dx + w[k:k + 1, :] * _shift_up(du, k_w - 1 - k)
    return dx, [jnp.sum(du * taps[k], 0, keepdims=True) for k in range(k_w)]


def _dn_pre_fwd(proj, conv_w, n_b, t_len):
    n_ct = 3 * HEADS
    k_w = conv_w.shape[0]

    def body(x_ref, w_ref, o_ref):
        o_ref[...] = _silu(_conv(_taps(x_ref[...], k_w), w_ref[...]))

    return pl.pallas_call(
        body, name="dn_pre_fwd", grid=(n_ct, n_b),
        in_specs=[pl.BlockSpec((t_len, 128), lambda j, b: (b, _qkv_block(j))),
                  pl.BlockSpec((k_w, 128), lambda j, b: (0, j))],
        out_specs=pl.BlockSpec((t_len, 128), lambda j, b: (b, j)),
        out_shape=jax.ShapeDtypeStruct((n_b * t_len, n_ct * 128), F32),
        compiler_params=_cparams(("parallel", "parallel")),
    )(proj, conv_w)


def _dn_pre_bwd(proj, conv_w, dqkv, d_proj, n_b, t_len):
    n_ct = 3 * HEADS
    k_w = conv_w.shape[0]

    def body(x_ref, w_ref, d_ref, _, dx_ref, dw_ref):
        taps, w = _taps(x_ref[...], k_w), w_ref[...]
        du = d_ref[...] * _dsilu(_conv(taps, w))
        dx, dw = _conv_bwd(taps, w, du)
        dx_ref[...] = dx.astype(BF16)
        _acc_rows(dw_ref, dw, pl.program_id(1) == 0)

    return pl.pallas_call(
        body, name="dn_pre_bwd", grid=(n_ct, n_b),
        in_specs=[pl.BlockSpec((t_len, 128), lambda j, b: (b, _qkv_block(j))),
                  pl.BlockSpec((k_w, 128), lambda j, b: (0, j)),
                  pl.BlockSpec((t_len, 128), lambda j, b: (b, j)), pl.BlockSpec(memory_space=pl.ANY)],
        out_specs=[pl.BlockSpec((t_len, 128), lambda j, b: (b, _qkv_block(j))),
                   pl.BlockSpec((k_w, 128), lambda j, b: (0, j))],
        out_shape=[jax.ShapeDtypeStruct(d_proj.shape, BF16), jax.ShapeDtypeStruct((k_w, n_ct * 128), F32)],
        input_output_aliases={3: 0},
        compiler_params=_cparams(("parallel", "arbitrary")),
    )(proj, conv_w, dqkv, d_proj)


FFN_TC = 256
FFN_NT = D_FF // FFN_TC


def _ffn_specs(t_len):
    blk = lambda off: pl.BlockSpec((t_len, FFN_TC), lambda j, b: (b, j + off))
    wblk = lambda off: pl.BlockSpec((3, FFN_TC), lambda j, b: (0, j + off))
    bblk = lambda off: pl.BlockSpec((1, FFN_TC), lambda j, b: (0, j + off))
    return [blk(0), blk(FFN_NT), wblk(0), wblk(FFN_NT), bblk(0), bblk(FFN_NT)]


def _ffn_act_fwd(up, conv_w, conv_b, n_b, t_len):
    def body(g_ref, v_ref, wg_ref, wv_ref, bg_ref, bv_ref, o_ref):
        ug = _conv(_taps(g_ref[...], 3), wg_ref[...]) + bg_ref[...]
        uv = _conv(_taps(v_ref[...], 3), wv_ref[...]) + bv_ref[...]
        o_ref[...] = (_silu(ug) * uv).astype(BF16)

    return pl.pallas_call(
        body, name="ffn_act_fwd", grid=(FFN_NT, n_b), in_specs=_ffn_specs(t_len),
        out_specs=pl.BlockSpec((t_len, FFN_TC), lambda j, b: (b, j)),
        out_shape=jax.ShapeDtypeStruct((n_b * t_len, D_FF), BF16),
        compiler_params=_cparams(("parallel", "parallel")),
    )(up, up, conv_w, conv_w, conv_b, conv_b)


def _ffn_act_bwd(up, conv_w, conv_b, da, n_b, t_len):
    def body(g_ref, v_ref, wg_ref, wv_ref, bg_ref, bv_ref, da_ref, dup_ref, dw_ref, db_ref):
        first = pl.program_id(1) == 0
        tg, tv, wg, wv = _taps(g_ref[...], 3), _taps(v_ref[...], 3), wg_ref[...], wv_ref[...]
        ug = _conv(tg, wg) + bg_ref[...]
        uv = _conv(tv, wv) + bv_ref[...]
        d_act = da_ref[...]
        sig = jax.nn.sigmoid(ug)
        d_v = d_act * (ug * sig)
        d_g = d_act * uv * (sig * (1.0 + ug * (1.0 - sig)))
        for slab, (taps, w, du) in enumerate(((tg, wg, d_g), (tv, wv, d_v))):
            dx, dw = _conv_bwd(taps, w, du)
            dup_ref[slab] = dx.astype(BF16)
            for k, dw_k in enumerate(dw):
                _acc(dw_ref, dw_k, first, at=(slab, slice(k, k + 1), slice(None)))
            _acc(db_ref, jnp.sum(du, 0, keepdims=True), first, at=(slab, slice(None), slice(None)))

    return pl.pallas_call(
        body, name="ffn_act_bwd", grid=(FFN_NT, n_b),
        in_specs=_ffn_specs(t_len) + [pl.BlockSpec((t_len, FFN_TC), lambda j, b: (b, j))],
        out_specs=[pl.BlockSpec((2, t_len, FFN_TC), lambda j, b: (0, b, j)),
                   pl.BlockSpec((2, 3, FFN_TC), lambda j, b: (0, 0, j)),
                   pl.BlockSpec((2, 1, FFN_TC), lambda j, b: (0, 0, j))],
        out_shape=[jax.ShapeDtypeStruct((2, n_b * t_len, D_FF), BF16),
                   jax.ShapeDtypeStruct((2, 3, D_FF), F32), jax.ShapeDtypeStruct((2, 1, D_FF), F32)],
        compiler_params=_cparams(("parallel", "arbitrary")),
    )(up, up, conv_w, conv_w, conv_b, conv_b, da)


NN = (((2,), (1,)), ((0,), (0,)))
NT = (((2,), (2,)), ((0,), (0,)))
TN = (((1,), (1,)), ((0,), (0,)))


def _iota3(shape, axis):
    return lax.broadcasted_iota(jnp.int32, shape, axis)


def _dg(a, b, dims):
    return lax.dot_general(a, b, dims, preferred_element_type=F32)


def _dot(a, b):
    return _dg(a, b, NN)


def _dot_nt(a, b):
    return _dg(a, b, NT)


def _dot_tn(a, b):
    return _dg(a, b, TN)


def _split(a):
    hi = a.astype(BF16)
    return hi, (a - hi.astype(F32)).astype(BF16)


def _dg3(a, b, dims):
    ah, al = _split(a)
    bh, bl = _split(b)
    return _dg(ah, bh, dims) + (_dg(ah, bl, dims) + _dg(al, bh, dims))


@jax.custom_vjp
def _dot3(a, b):
    return _dg3(a, b, NN)


def _dot3_fwd(a, b):
    return _dg3(a, b, NN), (a, b)


def _dot3_bwd(res, g):
    a, b = res
    return _dg3(g, b, NT), _dg3(a, g, TN)


_dot3.defvjp(_dot3_fwd, _dot3_bwd)


def _lower_ones(g_n, n):
    shape = (g_n, n, n)
    return jnp.where(_iota3(shape, 1) >= _iota3(shape, 2), 1.0, 0.0).astype(BF16)


@jax.custom_vjp
def _chunk_cumsum(x):
    hi, lo = _split(x)
    tri = _lower_ones(x.shape[0], x.shape[1])
    return _dg(tri, hi, NN) + _dg(tri, lo, NN)


def _chunk_cumsum_fwd(x):
    return _chunk_cumsum(x), None


def _chunk_cumsum_bwd(_, g):
    hi, lo = _split(g)
    tri = _lower_ones(g.shape[0], g.shape[1])
    return (_dg(tri, hi, TN) + _dg(tri, lo, TN),)


_chunk_cumsum.defvjp(_chunk_cumsum_fwd, _chunk_cumsum_bwd)


@jax.custom_vjp
def _unit_lower_inv(m):
    n = m.shape[1]
    p = -m
    a = jnp.where(_iota3(m.shape, 1) == _iota3(m.shape, 2), 1.0, 0.0) + p
    span = 2
    while span < n:
        p = _dg3(p, p, NN)
        a = a + _dg3(a, p, NN)
        span *= 2
    return a


def _unit_lower_inv_fwd(m):
    a = _unit_lower_inv(m)
    return a, a


def _unit_lower_inv_bwd(a, da):
    return (-_dg3(a, _dg3(da, a, NT), TN),)


_unit_lower_inv.defvjp(_unit_lower_inv_fwd, _unit_lower_inv_bwd)


@jax.custom_vjp
def _saved_lower_inv(m, a):
    return a


def _saved_lower_inv_fwd(m, a):
    return a, a


def _saved_lower_inv_bwd(a, da):
    return _unit_lower_inv_bwd(a, da)[0], jnp.zeros_like(a)


_saved_lower_inv.defvjp(_saved_lower_inv_fwd, _saved_lower_inv_bwd)


def _rms_gate(o, gn, gate):
    return o * lax.rsqrt(jnp.mean(o * o, -1, keepdims=True) + EPS) * gn * _silu(gate)


def _dn_chains(q, k, v, z, small, s_in, a_log, dt_bias, gn, a_saved=None):
    prep = _dn_prepare(q, k, v, small, a_log, dt_bias, a_saved)
    og, s_out = _dn_advance(prep[:-1], z, s_in, gn)
    return og, s_out, prep[-1]


def _dn_prepare(q, k, v, small, a_log, dt_bias, a_saved=None):
    g_n, c_len = q.shape[0], q.shape[1]
    sq = (g_n, c_len, c_len)
    row, col = _iota3(sq, 1), _iota3(sq, 2)
    causal, strict, eye = row >= col, row > col, row == col
    qn = q * lax.rsqrt(jnp.sum(q * q, -1, keepdims=True) + EPS) * (HEAD_DIM ** -0.5)
    kn = k * lax.rsqrt(jnp.sum(k * k, -1, keepdims=True) + EPS)
    lane = _iota3(small.shape, 2)
    head = jnp.bitwise_and(_iota3(small.shape, 0), HEADS - 1)
    la_all = -jnp.exp(a_log) * _softplus(small + dt_bias)
    la_c = jnp.sum(jnp.where(lane == head, la_all, 0.0), 2, keepdims=True)
    beta = jnp.sum(jnp.where(lane == head + HEADS, jax.nn.sigmoid(small), 0.0), 2, keepdims=True)
    la_b = jnp.broadcast_to(la_c, sq)
    la_r = jnp.sum(jnp.where(eye, la_b, 0.0), 1, keepdims=True)
    g_c = jnp.sum(jnp.where(causal, jnp.broadcast_to(la_r, sq), 0.0), 2, keepdims=True)
    g_r = jnp.sum(jnp.where(row <= col, la_b, 0.0), 1, keepdims=True)
    g_last = jnp.sum(la_c, 1, keepdims=True)
    decay = jnp.exp(jnp.where(causal, g_c - g_r, -1e30))
    e_g = jnp.exp(g_c)
    kb = kn * beta
    m_low = jnp.where(strict, _dot_nt(kb, kn) * decay, 0.0)
    a_inv = _unit_lower_inv(m_low) if a_saved is None else _saved_lower_inv(m_low, a_saved)
    u = _dot3(a_inv, v * beta)
    w = _dot3(a_inv, kb * e_g)
    attn = _dot_nt(qn, kn) * decay
    return u, w, attn, qn * e_g, kn * jnp.exp(g_last - g_c), jnp.exp(g_last), a_inv


def _dn_advance(prep, z, s_in, gn):
    u, w, attn, q_dec, k_dec, g_chunk = prep
    v_new = u - _dot(w, s_in)
    o = _dot(q_dec, s_in) + _dot(attn, v_new)
    s_out = s_in * g_chunk + _dot_tn(k_dec, v_new)
    return _rms_gate(o, gn, z), s_out


def _gla_chains(q, k, v, gate, small, s_in, w2, b2, gn):
    g_n, c_len = q.shape[0], q.shape[1]
    sq, kk = (g_n, c_len, c_len), (g_n, GLA_KEY, GLA_KEY)
    causal = _iota3(sq, 1) >= _iota3(sq, 2)
    la = -_softplus(-(_dot(small, w2) + b2)) * (1.0 / 16.0)
    b = _chunk_cumsum(la)
    b_last = jnp.sum(jnp.where(_iota3(b.shape, 1) == c_len - 1, b, 0.0), 1, keepdims=True)
    q_dec = q * (GLA_KEY ** -0.5) * jnp.exp(b)
    attn = jnp.where(causal, _dot_nt(q_dec, k * jnp.exp(-b)), 0.0)
    o = _dot(q_dec, s_in) + _dot(attn, v)
    g_row = jnp.exp(b_last)
    g_col = jnp.sum(jnp.where(_iota3(kk, 1) == _iota3(kk, 2), jnp.broadcast_to(g_row, kk), 0.0), 2, keepdims=True)
    s_out = s_in * g_col + _dot_tn(k * jnp.exp(b_last - b), v)
    return _rms_gate(o, gn, gate), s_out


def _chunk_spec(n_b, width, col_block, n_c, reverse=False):
    if reverse:
        return pl.BlockSpec((n_b, CHUNK, width), lambda n: (0, n_c - 1 - n, col_block))
    return pl.BlockSpec((n_b, CHUNK, width), lambda n: (0, n, col_block))


def _hist_spec(n_b, d_k, n_c, reverse=False):
    if reverse:
        return pl.BlockSpec((None, n_b * HEADS, d_k, HEAD_DIM), lambda n: (n_c - 1 - n, 0, 0, 0))
    return pl.BlockSpec((None, n_b * HEADS, d_k, HEAD_DIM), lambda n: (n, 0, 0, 0))


def _ainv_spec(n_b, n_c, reverse=False):
    if reverse:
        return pl.BlockSpec((None, n_b * HEADS, CHUNK, CHUNK), lambda n: (n_c - 1 - n, 0, 0, 0))
    return pl.BlockSpec((None, n_b * HEADS, CHUNK, CHUNK), lambda n: (n, 0, 0, 0))


def _stack_chains(ref, n_b, slices):
    return jnp.stack([ref[b, :, sl] for b in range(n_b) for sl in slices], axis=0)


def _per_chain(ref, n_b):
    return jnp.stack([ref[b] for b in range(n_b) for _ in range(HEADS)], axis=0)


def _unstack_chains(ref, val, n_b, slices, offset=0):
    for b in range(n_b):
        for h, sl in enumerate(slices):
            ref[b, :, slice(offset + sl.start, offset + sl.stop)] = val[b * HEADS + h].astype(ref.dtype)


def _gate_weights(w2_ref, b2_ref, n_b):
    w2 = jnp.stack([w2_ref[:, ks] for _ in range(n_b) for ks in GLA_KSL], axis=0)
    b2 = jnp.stack([b2_ref[:, ks] for _ in range(n_b) for ks in GLA_KSL], axis=0)
    return w2, b2


def _sum_heads(val, n_b):
    return [sum(val[b * HEADS + h] for h in range(HEADS)) for b in range(n_b)]


def _const_spec(shape):
    return pl.BlockSpec(shape, lambda n: (0,) * len(shape))


DN_SL = [slice(h * HEAD_DIM, (h + 1) * HEAD_DIM) for h in range(HEADS)]
GLA_KSL = [slice(h * GLA_KEY, (h + 1) * GLA_KEY) for h in range(HEADS)]


class Rider:
    def __init__(self, inputs, out_shapes, sems, first, last):
        self.inputs, self.out_shapes, self.sems, self.first, self.last = inputs, out_shapes, sems, first, last


def _with_rider(rider, n_in, n_out, n_scratch):
    if rider is None:
        return [], [], [], [], lambda refs: (refs, None)
    r_in, r_out, r_sem = len(rider.inputs), len(rider.out_shapes), len(rider.sems)

    def split(refs):
        own_in, rest = refs[:n_in], refs[n_in:]
        rid_in, rest = rest[:r_in], rest[r_in:]
        own_out, rest = rest[:n_out], rest[n_out:]
        rid_out, rest = rest[:r_out], rest[r_out:]
        own_scr, rid_sem = rest[:n_scratch], rest[n_scratch:]
        return own_in + own_out + own_scr, (rid_in, rid_out, rid_sem)

    return list(rider.inputs), [HBM_SPEC] * r_in, [HBM_SPEC] * r_out, list(rider.sems), split


def _ride(rider, parts, grid):
    if rider is None:
        return None, None
    grid = grid if isinstance(grid, tuple) else (grid,)

    def at(step_of):
        hit = pl.program_id(0) == step_of(grid[0])
        for axis in range(1, len(grid)):
            hit = jnp.logical_and(hit, pl.program_id(axis) == step_of(grid[axis]))
        return hit

    def first():
        pl.when(at(lambda n: 0))(lambda: rider.first(*parts))

    def last():
        pl.when(at(lambda n: n - 1))(lambda: rider.last(*parts))

    return first, last


FWD_CHUNKS = 4


def _dn_scan_fwd(qkv, proj, a_log, dt_bias, gn, n_b, t_len, rider=None):
    n_c, n_g, rows = t_len // CHUNK, n_b * HEADS, FWD_CHUNKS * CHUNK
    n_s = n_c // FWD_CHUNKS
    spec = lambda width, col_block: pl.BlockSpec((n_b, rows, width), lambda n: (0, n, col_block))
    kept = lambda d0, d1: pl.BlockSpec((FWD_CHUNKS, n_g, d0, d1), lambda n: (n, 0, 0, 0))
    r_inputs, r_in_specs, r_out_specs, r_sems, split = _with_rider(rider, 8, 3, 1)
    chunk_rows = [slice(j * CHUNK, (j + 1) * CHUNK) for j in range(FWD_CHUNKS)]

    def body(*refs):
        (q_ref, k_ref, v_ref, z_ref, sm_ref, al_ref, dt_ref, gn_ref,
         o_ref, hist_ref, ainv_ref, s_ref), parts = split(refs)
        ride_first, ride_last = _ride(rider, parts, n_s)
        if rider is not None:
            ride_first()

        @pl.when(pl.program_id(0) == 0)
        def _():
            s_ref[...] = jnp.zeros_like(s_ref)

        def stack(ref, slices):
            return jnp.stack([ref[b, rs, sl] for rs in chunk_rows for b in range(n_b) for sl in slices], axis=0)

        prep = _dn_prepare(stack(q_ref, DN_SL), stack(k_ref, DN_SL), stack(v_ref, DN_SL),
                           stack(sm_ref, [slice(None)] * HEADS), al_ref[...], dt_ref[...])
        z, state = stack(z_ref, DN_SL), s_ref[...]
        for j, rs in enumerate(chunk_rows):
            mine = slice(j * n_g, (j + 1) * n_g)
            hist_ref[j] = state
            ainv_ref[j] = prep[-1][mine]
            og, state = _dn_advance(tuple(a[mine] for a in prep[:-1]), z[mine], state, gn_ref[...])
            for b in range(n_b):
                for h, sl in enumerate(DN_SL):
                    o_ref[b, rs, sl] = og[b * HEADS + h].astype(BF16)
        s_ref[...] = state
        if rider is not None:
            ride_last()

    qkv3, proj3 = qkv.reshape(n_b, t_len, -1), proj.reshape(n_b, t_len, -1)
    o, hist, ainv, *rider_outs = pl.pallas_call(
        body, name="dn_scan_fwd", grid=(n_s,),
        in_specs=[spec(512, 0), spec(512, 1), spec(512, 2), spec(512, OFF_Z // 512), spec(128, OFF_SMALL // 128),
                  _const_spec((1, 128)), _const_spec((1, 128)), _const_spec((1, 128))] + r_in_specs,
        out_specs=[spec(512, 0), kept(HEAD_DIM, HEAD_DIM), kept(CHUNK, CHUNK)] + r_out_specs,
        out_shape=[jax.ShapeDtypeStruct((n_b, t_len, 2 * 512), BF16),
                   jax.ShapeDtypeStruct((n_c, n_b * HEADS, HEAD_DIM, HEAD_DIM), F32),
                   jax.ShapeDtypeStruct((n_c, n_b * HEADS, CHUNK, CHUNK), F32)]
        + (list(rider.out_shapes) if rider else []),
        scratch_shapes=[pltpu.VMEM((n_b * HEADS, HEAD_DIM, HEAD_DIM), F32)] + r_sems,
        compiler_params=_cparams(("arbitrary",)),
    )(qkv3, qkv3, qkv3, proj3, proj3, a_log, dt_bias, gn, *r_inputs)
    return o, (hist, ainv), rider_outs


SCAN_BWD_W = OFF_Z + 512


def _scan_bwd(qkv, proj, dn_params, gla_params, hist_dn, hist_gla, d_o, n_b, t_len, rider=None):
    n_c = t_len // CHUNK
    rev = functools.partial(_chunk_spec, n_b, n_c=n_c, reverse=True)
    r_inputs, r_in_specs, r_out_specs, r_sems, split = _with_rider(rider, 19, 8, 2)
    (hist, ainv), do_gla_sl = hist_dn, [slice(512 + sl.start, 512 + sl.stop) for sl in DN_SL]

    def body(*refs):
        (q_ref, k_ref, v_ref, z_ref, sm_ref, gq_ref, gk_ref, gv_ref, gg_ref,
         al_ref, dt_ref, dgn_in_ref, w2_ref, b2_ref, ggn_in_ref, hist_ref, ainv_ref, ghist_ref, do_ref,
         dqkv_ref, dp_ref, dal_ref, ddt_ref, dgn_ref, dw2_ref, db2_ref, dggn_ref, ds_ref, gds_ref), parts = split(refs)
        ride_first, ride_last = _ride(rider, parts, n_c)
        if rider is not None:
            ride_first()
        first = pl.program_id(0) == 0

        @pl.when(first)
        def _():
            ds_ref[...] = jnp.zeros_like(ds_ref)
            gds_ref[...] = jnp.zeros_like(gds_ref)

        small = _per_chain(sm_ref, n_b)
        chains = lambda *a: _dn_chains(*a, a_saved=ainv_ref[...])[:2]
        _, pull = jax.vjp(chains, *(_stack_chains(r, n_b, DN_SL) for r in (q_ref, k_ref, v_ref, z_ref)),
                          small, hist_ref[...], al_ref[...], dt_ref[...], dgn_in_ref[...])
        dq, dk, dv, dz, dsm_dn, ds_in, dal, ddt, dgn = pull((_stack_chains(do_ref, n_b, DN_SL), ds_ref[...]))
        _, gpull = jax.vjp(_gla_chains, _stack_chains(gq_ref, n_b, GLA_KSL), _stack_chains(gk_ref, n_b, GLA_KSL),
                           _stack_chains(gv_ref, n_b, DN_SL), _stack_chains(gg_ref, n_b, DN_SL),
                           small, ghist_ref[...], *_gate_weights(w2_ref, b2_ref, n_b), ggn_in_ref[...])
        gq, gk, gv, gg, dsm_gla, gds_in, dw2, db2, dggn = gpull((_stack_chains(do_ref, n_b, do_gla_sl), gds_ref[...]))

        _unstack_chains(dqkv_ref, dq, n_b, DN_SL)
        _unstack_chains(dqkv_ref, dk, n_b, DN_SL, offset=512)
        _unstack_chains(dqkv_ref, dv, n_b, DN_SL, offset=1024)
        _unstack_chains(dp_ref, dz, n_b, DN_SL, offset=OFF_Z)
        _unstack_chains(dp_ref, gq, n_b, GLA_KSL, offset=OFF_GQ)
        _unstack_chains(dp_ref, gk, n_b, GLA_KSL, offset=OFF_GK)
        _unstack_chains(dp_ref, gv, n_b, DN_SL, offset=OFF_GV)
        _unstack_chains(dp_ref, gg, n_b, DN_SL, offset=OFF_GG)
        ds_ref[...] = ds_in
        gds_ref[...] = gds_in
        for b, (s_dn, s_gla) in enumerate(zip(_sum_heads(dsm_dn, n_b), _sum_heads(dsm_gla, n_b))):
            dp_ref[b, :, OFF_SMALL:OFF_SMALL + 128] = (s_dn + s_gla).astype(BF16)
            dp_ref[b, :, OFF_SMALL + 128:GLA_W] = jnp.zeros((CHUNK, GLA_W - OFF_SMALL - 128), BF16)
        _acc(dal_ref, dal, first)
        _acc(ddt_ref, ddt, first)
        _acc(dgn_ref, dgn, first)
        for h, ks in enumerate(GLA_KSL):
            _acc(dw2_ref, sum(dw2[b * HEADS + h] for b in range(n_b)), first, at=(slice(None), ks))
            _acc(db2_ref, sum(db2[b * HEADS + h] for b in range(n_b)), first, at=(slice(None), ks))
        _acc(dggn_ref, dggn, first)
        if rider is not None:
            ride_last()

    qkv3, proj3, do3 = (a.reshape(n_b, t_len, -1) for a in (qkv, proj, d_o))
    vec = jax.ShapeDtypeStruct((1, 128), F32)
    dqkv, d_proj, dal, ddt, dgn, dw2, db2, dggn, *rider_outs = pl.pallas_call(
        body, name="scan_bwd", grid=(n_c,),
        in_specs=[rev(512, 0), rev(512, 1), rev(512, 2), rev(512, OFF_Z // 512), rev(128, OFF_SMALL // 128),
                  rev(256, OFF_GQ // 256), rev(256, OFF_GK // 256), rev(512, OFF_GV // 512), rev(512, OFF_GG // 512),
                  _const_spec((1, 128)), _const_spec((1, 128)), _const_spec((1, 128)),
                  _const_spec((128, 256)), _const_spec((1, 256)), _const_spec((1, 128)),
                  _hist_spec(n_b, HEAD_DIM, n_c, reverse=True), _ainv_spec(n_b, n_c, reverse=True),
                  _hist_spec(n_b, GLA_KEY, n_c, reverse=True), rev(2 * 512, 0)] + r_in_specs,
        out_specs=[rev(1536, 0), rev(SCAN_BWD_W, 0), _const_spec((1, 128)), _const_spec((1, 128)),
                   _const_spec((1, 128)), _const_spec((128, 256)), _const_spec((1, 256)), _const_spec((1, 128))]
        + r_out_specs,
        out_shape=[jax.ShapeDtypeStruct((n_b, t_len, 1536), F32), jax.ShapeDtypeStruct((n_b, t_len, PROJ_W), BF16),
                   vec, vec, vec, jax.ShapeDtypeStruct((128, 256), F32), jax.ShapeDtypeStruct((1, 256), F32), vec]
        + (list(rider.out_shapes) if rider else []),
        scratch_shapes=[pltpu.VMEM((n_b * HEADS, HEAD_DIM, HEAD_DIM), F32),
                        pltpu.VMEM((n_b * HEADS, GLA_KEY, HEAD_DIM), F32)] + r_sems,
        compiler_params=_cparams(("arbitrary",)),
    )(qkv3, qkv3, qkv3, proj3, proj3, proj3, proj3, proj3, proj3, *dn_params, *gla_params, hist, ainv, hist_gla, do3,
      *r_inputs)
    return dqkv.reshape(n_b * t_len, 1536), d_proj, (dal, ddt, dgn), (dw2, db2, dggn), rider_outs


def _gla_scan_fwd(proj, w2, b2, gn, o_mix, n_b, t_len):
    n_c = t_len // CHUNK
    spec = functools.partial(_chunk_spec, n_b, n_c=n_c)

    def body(q_ref, k_ref, v_ref, g_ref, sm_ref, w2_ref, b2_ref, gn_ref, _, o_ref, hist_ref, s_ref):
        @pl.when(pl.program_id(0) == 0)
        def _():
            s_ref[...] = jnp.zeros_like(s_ref)

        s_in = s_ref[...]
        hist_ref[...] = s_in
        og, s_out = _gla_chains(_stack_chains(q_ref, n_b, GLA_KSL), _stack_chains(k_ref, n_b, GLA_KSL),
                                _stack_chains(v_ref, n_b, DN_SL), _stack_chains(g_ref, n_b, DN_SL),
                                _per_chain(sm_ref, n_b), s_in, *_gate_weights(w2_ref, b2_ref, n_b), gn_ref[...])
        _unstack_chains(o_ref, og, n_b, DN_SL)
        s_ref[...] = s_out

    proj3 = proj.reshape(n_b, t_len, -1)
    o, hist = pl.pallas_call(
        body, name="gla_scan_fwd", grid=(n_c,),
        in_specs=[spec(256, OFF_GQ // 256), spec(256, OFF_GK // 256), spec(512, OFF_GV // 512),
                  spec(512, OFF_GG // 512), spec(128, OFF_SMALL // 128),
                  _const_spec((128, 256)), _const_spec((1, 256)), _const_spec((1, 128)),
                  pl.BlockSpec(memory_space=pl.ANY)],
        out_specs=[spec(512, 1), _hist_spec(n_b, GLA_KEY, n_c)],
        out_shape=[jax.ShapeDtypeStruct(o_mix.shape, BF16),
                   jax.ShapeDtypeStruct((n_c, n_b * HEADS, GLA_KEY, HEAD_DIM), F32)],
        input_output_aliases={8: 0},
        scratch_shapes=[pltpu.VMEM((n_b * HEADS, GLA_KEY, HEAD_DIM), F32)],
        compiler_params=_cparams(("arbitrary",)),
    )(proj3, proj3, proj3, proj3, proj3, w2, b2, gn, o_mix)
    return o.reshape(n_b * t_len, 2 * 512), hist


W_IN_RUNS = ((0, 256, GLA_W), (256, 1536, OFF_Z + 512), (1536, 2048, OFF_Z), (2048, 2056, OFF_SMALL),
             (2056, 3592, 0), (3592, 3608, OFF_SMALL + 8))
W_IN_ROWS = 256


def _w_in_pieces(cols_per_chip):
    out = []
    for first, last, start in W_IN_RUNS:
        for j in range(N_CHIPS):
            a, b = max(first, cols_per_chip * j), min(last, cols_per_chip * (j + 1))
            if a < b:
                out.append((j, a - cols_per_chip * j, b - cols_per_chip * j, start + a - first))
    return out


def _w_in_to_padded(w4):
    _, n_r, n_c = w4.shape

    def body(i_ref, o_ref):
        o_ref[...] = jnp.zeros_like(o_ref)
        for j, a, b, p in _w_in_pieces(n_c):
            o_ref[:, p:p + b - a] = i_ref[j, :, a:b]

    return pl.pallas_call(
        body, name="w_in_to_padded", grid=(n_r // W_IN_ROWS,),
        in_specs=[pl.BlockSpec((N_CHIPS, W_IN_ROWS, n_c), lambda i: (0, i, 0))],
        out_specs=pl.BlockSpec((W_IN_ROWS, PROJ_W), lambda i: (i, 0)),
        out_shape=jax.ShapeDtypeStruct((n_r, PROJ_W), w4.dtype), compiler_params=_cparams(("parallel",)),
    )(w4)


def _w_in_to_chips(g, n_c):
    n_r = g.shape[0]

    def body(i_ref, o_ref):
        for j, a, b, p in _w_in_pieces(n_c):
            o_ref[j, :, a:b] = i_ref[:, p:p + b - a]

    return pl.pallas_call(
        body, name="w_in_to_chips", grid=(n_r // W_IN_ROWS,),
        in_specs=[pl.BlockSpec((W_IN_ROWS, PROJ_W), lambda i: (i, 0))],
        out_specs=pl.BlockSpec((N_CHIPS, W_IN_ROWS, n_c), lambda i: (0, i, 0)),
        out_shape=jax.ShapeDtypeStruct((N_CHIPS, n_r, n_c), g.dtype), compiler_params=_cparams(("parallel",)),
    )(g)


def _lane_vec(v, offset=0):
    return jnp.zeros((1, 128), F32).at[0, offset:offset + v.shape[0]].set(v)


def _local_step(x, tgt, mod, p, n_b, t_len, comm=None):
    row1 = lambda v: v.reshape(1, -1)
    a_log, dt_bias = _lane_vec(p["dn_a_log"]), _lane_vec(p["dn_dt_bias"])
    dn_gn, gla_gn = row1(p["dn_norm_g"]), row1(p["gla_norm_g"])
    w2 = jnp.zeros((128, 256), F32).at[8:8 + GATE_RANK].set(p["gla_w_gate2"])
    b2 = row1(p["gla_b_gate"])
    ln0_g, ln0_b, ln1_g, ln1_b, ln2_g, ln2_b = (row1(p[k]) for k in ("ln0_g", "ln0_b", "ln1_g", "ln1_b", "ln2_g", "ln2_b"))
    conv_b = row1(p["ffn_conv_b"])

    x0, h1 = _ln0_fwd(x, ln0_g, ln0_b, mod, n_b, t_len)
    if comm:
        proj, landed_proj = _mm(h1, p["w_in_p"], name="mm_proj", rider=comm.proj_rider())
    else:
        proj = _mm(h1, p["w_in_p"], name="mm_proj")
    qkv = _dn_pre_fwd(proj, p["dn_conv"], n_b, t_len)
    o_half, hist_dn, landed_scan = _dn_scan_fwd(qkv, proj, a_log, dt_bias, dn_gn, n_b, t_len,
                                                rider=comm.scan_rider() if comm else None)
    if comm:
        p = {**p, **comm.weights_from(landed_proj, landed_scan)}
    o_mix, hist_gla = _gla_scan_fwd(proj, w2, b2, gla_gn, o_half, n_b, t_len)
    y = _mm(o_mix, p["w_o"], name="mm_wo")
    x1, h2 = _ln1_fwd(x0, y, ln1_g, ln1_b, mod, n_b, t_len)
    up = _mm(h2, p["w_up"], name="mm_up")
    act = _ffn_act_fwd(up, p["ffn_conv"], conv_b, n_b, t_len)
    y2 = _mm(act, p["w_down"], name="mm_down")

    loss, dx1, dy2, g_ln2_g, g_ln2_b, dgt_f = _ln2_loss_bwd(x1, y2, ln2_g, ln2_b, mod, tgt, n_b, t_len)
    g_w_down = _mm(act, dy2, ta=True, name="mm_g_down")
    d_act = _mm(dy2, p["w_down"], tb=True, name="mm_d_act")
    d_up, g_ffn_conv, g_conv_b = _ffn_act_bwd(up, p["ffn_conv"], conv_b, d_act, n_b, t_len)
    g_w_up = _mm(h2, d_up, ta=True, out_slabs=N_CHIPS, name="mm_g_up")
    if comm:
        dh2, from_sibling = _mm(d_up, p["w_up"], tb=True, name="mm_d_h2", rider=comm.ffn_pair_rider(g_w_up, g_w_down))
    else:
        dh2 = _mm(d_up, p["w_up"], tb=True, name="mm_d_h2")
    dx0, dy, g_ln1_g, g_ln1_b, dmod_1 = _ln1_bwd(x0, y, ln1_g, ln1_b, mod, dx1, dh2, n_b, t_len)
    g_w_o = _mm(o_mix, dy, ta=True, name="mm_g_wo")
    if comm:
        d_o, wo_from_sibling = _mm(dy, p["w_o"], tb=True, name="mm_d_o", rider=comm.wo_pair_rider(g_w_o))
    else:
        d_o = _mm(dy, p["w_o"], tb=True, name="mm_d_o")
    dqkv, d_proj, (g_a_log, g_dt_bias, g_dn_gn), (g_w2, g_b2, g_gla_gn), scan_from_chips = _scan_bwd(
        qkv, proj, (a_log, dt_bias, dn_gn), (w2, b2, gla_gn), hist_dn, hist_gla, d_o, n_b, t_len,
        rider=comm.scan_chips_rider(from_sibling, wo_from_sibling) if comm else None)
    d_proj, g_dn_conv = _dn_pre_bwd(proj, p["dn_conv"], dqkv, d_proj.reshape(n_b * t_len, PROJ_W), n_b, t_len)
    g_w_in_p = _mm(h1, d_proj, ta=True, name="mm_g_win")
    if comm:
        dh1, tail_from_chips = _mm(d_proj, p["w_in_p"], tb=True, name="mm_d_h1",
                                   rider=comm.tail_chips_rider(g_w_in_p))
        from_chips = (scan_from_chips, tail_from_chips)
    else:
        dh1, from_chips = _mm(d_proj, p["w_in_p"], tb=True, name="mm_d_h1"), None
    grad_x, g_ln0_g, g_ln0_b, dmod_0 = _ln0_bwd(x, ln0_g, ln0_b, mod, dx0, dh1, n_b, t_len)

    dmod = jnp.concatenate([dmod_0, dmod_1[:, 0:1], dmod_1[:, 1:3], dgt_f], axis=1)
    grads = {
        "ln0_g": g_ln0_g[0], "ln0_b": g_ln0_b[0], "w_in_p": g_w_in_p, "dn_conv": g_dn_conv,
        "dn_a_log": g_a_log[0, 0:HEADS], "dn_dt_bias": g_dt_bias[0, 0:HEADS], "dn_norm_g": g_dn_gn[0],
        "gla_w_gate2": g_w2[8:8 + GATE_RANK], "gla_b_gate": g_b2[0], "gla_norm_g": g_gla_gn[0],
        "w_o": g_w_o, "ln1_g": g_ln1_g[0], "ln1_b": g_ln1_b[0], "w_up": g_w_up,
        "ffn_conv": jnp.concatenate([g_ffn_conv[0], g_ffn_conv[1]], axis=1),
        "ffn_conv_b": jnp.concatenate([g_conv_b[0, 0], g_conv_b[1, 0]]), "w_down": g_w_down,
        "ln2_g": g_ln2_g[0], "ln2_b": g_ln2_b[0],
    }
    return loss, grad_x, grads, dmod, from_chips


def _col_sum(a):
    def body(a_ref, o_ref):
        o_ref[...] = jnp.sum(a_ref[...], 0, keepdims=True)

    return pl.pallas_call(body, name="col_sum", out_shape=jax.ShapeDtypeStruct((1, a.shape[1]), F32))(a)


def _adamw_math(w, grad, m, v):
    new_m = ADAM_B1 * m + (1.0 - ADAM_B1) * grad
    new_v = ADAM_B2 * v + (1.0 - ADAM_B2) * (grad * grad)
    m_hat = new_m / (1.0 - ADAM_B1 ** ADAM_STEP)
    v_hat = new_v / (1.0 - ADAM_B2 ** ADAM_STEP)
    return -ADAM_LR * (m_hat / (jnp.sqrt(v_hat) + ADAM_EPS) + ADAM_WD * w), new_m, new_v


def _adamw_many(ws, gs, ms, vs):
    n = len(ws)

    def body(*refs):
        for i in range(n):
            w_ref, g_ref, m_ref, v_ref = (refs[k * n + i] for k in range(4))
            d_ref, nm_ref, nv_ref = (refs[(4 + k) * n + i] for k in range(3))
            d_ref[...], nm_ref[...], nv_ref[...] = _adamw_math(w_ref[...], g_ref[...], m_ref[...], v_ref[...])

    outs = pl.pallas_call(
        body, name="adamw_small", out_shape=[jax.ShapeDtypeStruct(w.shape, F32) for w in ws] * 3,
    )(*ws, *gs, *ms, *vs)
    return outs[:n], outs[n:2 * n], outs[2 * n:]


def _adamw(w, g, m, v, name):
    n_r, n_c = w.shape
    if n_r % 8 == 0:
        tr = _pick(n_r, (256, 64, 32, 16, 8))
        grid, blk = (n_r // tr,), pl.BlockSpec((tr, n_c), lambda i: (i, 0))
    else:
        tc = _pick(n_c, (256, 128))
        grid, blk = (n_c // tc,), pl.BlockSpec((n_r, tc), lambda i: (0, i))

    def body(w_ref, g_ref, m_ref, v_ref, d_ref, nm_ref, nv_ref):
        d_ref[...], nm_ref[...], nv_ref[...] = _adamw_math(w_ref[...], g_ref[...], m_ref[...], v_ref[...])

    out = jax.ShapeDtypeStruct(w.shape, F32)
    return pl.pallas_call(
        body, name=name, grid=grid, in_specs=[blk] * 4, out_specs=[blk] * 3, out_shape=[out] * 3,
        compiler_params=_cparams(("parallel",)),
    )(w, g, m, v)


HBM_SPEC = pl.BlockSpec(memory_space=pltpu.HBM)
VMEM_SPEC = pl.BlockSpec(memory_space=pltpu.VMEM)
CHIP_FLIPS = ((1, 0), (0, 1), (1, 1))


def _place():
    return lax.axis_index("x"), lax.axis_index("y"), lax.axis_index("c")


def _flip(v, f):
    return 1 - v if f else v


def _all_gather8(slab, name):
    n_r, n_w = slab.shape

    def body(x_ref, o_ref, s_ref, send_sems, recv_sems, local_sem):
        x, y, c = _place()
        me = 4 * x + 2 * y + c
        mine = pltpu.make_async_copy(x_ref, o_ref.at[me], local_sem)
        mine.start()
        peers = [(_flip(x, k & 4), _flip(y, k & 2), _flip(c, k & 1)) for k in range(1, N_DEV)]
        sends = []
        for k, peer in enumerate(peers):
            cp = pltpu.make_async_remote_copy(src_ref=x_ref, dst_ref=o_ref.at[me], send_sem=send_sems.at[k],
                                              recv_sem=recv_sems.at[k], device_id=peer, device_id_type=MESH)
            cp.start()
            sends.append(cp)
        for k, (px, py, pc) in enumerate(peers):
            pltpu.make_async_remote_copy(src_ref=x_ref, dst_ref=o_ref.at[4 * px + 2 * py + pc],
                                         send_sem=send_sems.at[k], recv_sem=recv_sems.at[k],
                                         device_id=(px, py, pc), device_id_type=MESH).wait_recv()
        for cp in sends:
            cp.wait_send()
        mine.wait()
        total = o_ref[0]
        for d in range(1, N_DEV):
            total = total + o_ref[d]
        s_ref[...] = total

    return pl.pallas_call(
        body, name=name, in_specs=[VMEM_SPEC], out_specs=[VMEM_SPEC, VMEM_SPEC],
        out_shape=[jax.ShapeDtypeStruct((N_DEV, n_r, n_w), F32), jax.ShapeDtypeStruct((n_r, n_w), F32)],
        scratch_shapes=[pltpu.SemaphoreType.DMA((N_DEV - 1,)), pltpu.SemaphoreType.DMA((N_DEV - 1,)),
                        pltpu.SemaphoreType.DMA],
    )(slab)


SEQ_ROWS = 8


def _prologue(slab, w_ada_shard, b_shard, rider):
    n_r, n_w = slab.shape
    n_col = w_ada_shard.shape[1]
    r_inputs, r_in_specs, r_out_specs, r_sems, split = _with_rider(rider, 3, 3, 6)

    def body(*refs):
        (x_ref, w_ref, b_ref, g_ref, cond_ref, modr_ref, modp_ref, s1, r1, s2, r2, lsem), parts = split(refs)
        rider.first(*parts)
        x, y, c = _place()
        me = 4 * x + 2 * y + c
        peers = [(_flip(x, k & 4), _flip(y, k & 2), _flip(c, k & 1)) for k in range(1, N_DEV)]
        ids = [4 * px + 2 * py + pc for px, py, pc in peers]

        def exchange(src_of, dst, send_sems, recv_sems, own_sem):
            mine = pltpu.make_async_copy(src_of(me), dst.at[me], own_sem)
            mine.start()
            sends = [pltpu.make_async_remote_copy(src_ref=src_of(ids[k]), dst_ref=dst.at[me], send_sem=send_sems.at[k],
                                                  recv_sem=recv_sems.at[k], device_id=peers[k], device_id_type=MESH)
                     for k in range(N_DEV - 1)]
            for cp in sends:
                cp.start()
            for k in range(N_DEV - 1):
                pltpu.make_async_remote_copy(src_ref=src_of(ids[k]), dst_ref=dst.at[ids[k]], send_sem=send_sems.at[k],
                                             recv_sem=recv_sems.at[k], device_id=peers[k],
                                             device_id_type=MESH).wait_recv()
            for cp in sends:
                cp.wait_send()
            mine.wait()

        exchange(lambda d: x_ref, g_ref, s1, r1, lsem.at[0])
        cond = _silu(g_ref[:, 0:SEQ_ROWS, :].reshape(N_DEV * SEQ_ROWS, n_w))
        cond_ref[...] = cond
        modp_ref[...] = jnp.dot(cond.astype(BF16), w_ref[...].astype(BF16), preferred_element_type=F32) + b_ref[...]
        exchange(lambda d: modp_ref.at[pl.ds(pl.multiple_of(d * SEQ_ROWS, SEQ_ROWS), SEQ_ROWS)], modr_ref, s2, r2,
                 lsem.at[1])
        rider.last(*parts)

    sem7 = pltpu.SemaphoreType.DMA((N_DEV - 1,))
    gathered, cond, mod_recv, *rider_outs = pl.pallas_call(
        body, name="prologue", in_specs=[VMEM_SPEC] * 3 + r_in_specs, out_specs=[VMEM_SPEC] * 3 + r_out_specs,
        out_shape=[jax.ShapeDtypeStruct((N_DEV, n_r, n_w), F32), jax.ShapeDtypeStruct((N_DEV * SEQ_ROWS, n_w), F32),
                   jax.ShapeDtypeStruct((N_DEV, SEQ_ROWS, n_col), F32)] + list(rider.out_shapes),
        scratch_shapes=[pltpu.VMEM((N_DEV * SEQ_ROWS, n_col), F32), sem7, sem7, sem7, sem7,
                        pltpu.SemaphoreType.DMA((2,))] + r_sems,
        compiler_params=pltpu.CompilerParams(vmem_limit_bytes=VMEM_LIMIT),
    )(slab, w_ada_shard, b_shard, *r_inputs)
    return gathered, cond, mod_recv, rider_outs


def _gather_rider(shards):
    n_a = len(shards)

    def plan(ins, outs, sems):
        send_sems, recv_sems = sems
        x, y, c = _place()
        chips = [(_flip(x, fx), _flip(y, fy)) for fx, fy in CHIP_FLIPS]

        def copy(k, slot, chip_of_block, half, to, src=None):
            dst = outs[k].at[chip_of_block, half]
            return pltpu.make_async_remote_copy(src_ref=dst if src is None else src, dst_ref=dst,
                                                send_sem=send_sems.at[k * 6 + slot], recv_sem=recv_sems.at[k * 6 + slot],
                                                device_id=to, device_id_type=MESH)

        first = [copy(k, r, 2 * x + y, c, (*chips[r], c), src=ins[k].at[c]) for k in range(n_a) for r in range(3)]
        return copy, chips, first, (x, y, c)

    def first_step(ins, outs, sems):
        for cp in plan(ins, outs, sems)[2]:
            cp.start()

    def last_step(ins, outs, sems):
        copy, chips, first, (x, y, c) = plan(ins, outs, sems)
        passed = []
        for k in range(n_a):
            for r, (px, py) in enumerate(chips):
                copy(k, r, 2 * px + py, c, (x, y, c)).wait_recv()
                fwd = copy(k, 3 + r, 2 * px + py, c, (x, y, 1 - c))
                fwd.start()
                passed.append(fwd)
        for k in range(n_a):
            for r, (px, py) in enumerate(chips):
                copy(k, 3 + r, 2 * px + py, 1 - c, (x, y, c)).wait_recv()
        for cp in first + passed:
            cp.wait_send()

    return Rider(shards, [jax.ShapeDtypeStruct((N_CHIPS,) + s.shape, s.dtype) for s in shards],
                 [pltpu.SemaphoreType.DMA((6 * n_a,)), pltpu.SemaphoreType.DMA((6 * n_a,))], first_step, last_step)


def _place_own(gathered, shard, chip, name):
    _, _, n_h, n_c = gathered.shape
    th = _pick(n_h, (256, 176, 128))

    def body(sel_ref, s_ref, _, o_ref):
        o_ref[...] = s_ref[...]

    grid_spec = pltpu.PrefetchScalarGridSpec(
        num_scalar_prefetch=1, grid=(2, n_h // th),
        in_specs=[pl.BlockSpec((None, th, n_c), lambda hf, i, sel: (hf, i, 0)), pl.BlockSpec(memory_space=pl.ANY)],
        out_specs=pl.BlockSpec((None, None, th, n_c), lambda hf, i, sel: (sel[0], hf, i, 0)))
    return pl.pallas_call(
        body, name=name, grid_spec=grid_spec, out_shape=jax.ShapeDtypeStruct(gathered.shape, gathered.dtype),
        input_output_aliases={2: 0}, compiler_params=_cparams(("parallel", "parallel")),
    )(chip.reshape(1), shard, gathered)


def _pair_rider(parts):
    n_a = len(parts)

    def plan(ins, outs, sems):
        send_sems, recv_sems = sems
        x, y, c = _place()
        return [pltpu.make_async_remote_copy(src_ref=ins[k].at[:, 1 - c], dst_ref=outs[k], send_sem=send_sems.at[k],
                                             recv_sem=recv_sems.at[k], device_id=(x, y, 1 - c), device_id_type=MESH)
                for k in range(n_a)]

    def first_step(ins, outs, sems):
        for cp in plan(ins, outs, sems):
            cp.start()

    def last_step(ins, outs, sems):
        for cp in plan(ins, outs, sems):
            cp.wait()

    return Rider(parts, [jax.ShapeDtypeStruct((N_CHIPS,) + p.shape[2:], F32) for p in parts],
                 [pltpu.SemaphoreType.DMA((n_a,)), pltpu.SemaphoreType.DMA((n_a,))], first_step, last_step)


def _alone(rider, name):
    n_a = len(rider.inputs)

    def body(*refs):
        parts = (refs[:n_a], refs[n_a:2 * n_a], refs[2 * n_a:])
        rider.first(*parts)
        rider.last(*parts)

    return pl.pallas_call(
        body, name=name, in_specs=[HBM_SPEC] * n_a, out_specs=[HBM_SPEC] * n_a,
        out_shape=rider.out_shapes, scratch_shapes=rider.sems,
    )(*rider.inputs)


def _chips_rider(sums):
    n_a = len(sums)

    def plan(ins, outs, sems):
        send_sems, recv_sems = sems
        x, y, c = _place()
        cps = []
        for k in range(n_a):
            for r, (fx, fy) in enumerate(CHIP_FLIPS):
                px, py = _flip(x, fx), _flip(y, fy)
                cps.append(pltpu.make_async_remote_copy(
                    src_ref=ins[k].at[2 * px + py], dst_ref=outs[k].at[r], send_sem=send_sems.at[3 * k + r],
                    recv_sem=recv_sems.at[3 * k + r], device_id=(px, py, c), device_id_type=MESH))
        return cps

    def first_step(ins, outs, sems):
        for cp in plan(ins, outs, sems):
            cp.start()

    def last_step(ins, outs, sems):
        for cp in plan(ins, outs, sems):
            cp.wait()

    return Rider(sums, [jax.ShapeDtypeStruct((3,) + s.shape[1:], s.dtype) for s in sums],
                 [pltpu.SemaphoreType.DMA((3 * n_a,)), pltpu.SemaphoreType.DMA((3 * n_a,))], first_step, last_step)


def _rs_share(bufs):
    n_a = len(bufs)

    def body(*refs):
        ins, outs = refs[:n_a], refs[n_a:2 * n_a]
        send_sems, recv_sems = refs[2 * n_a:]
        x, y, c = _place()
        sends = [pltpu.make_async_remote_copy(src_ref=ins[k].at[c], dst_ref=outs[k].at[c], send_sem=send_sems.at[k],
                                              recv_sem=recv_sems.at[k], device_id=(x, y, 1 - c), device_id_type=MESH)
                 for k in range(n_a)]
        for cp in sends:
            cp.start()
        for k in range(n_a):
            pltpu.make_async_remote_copy(src_ref=ins[k].at[c], dst_ref=outs[k].at[1 - c], send_sem=send_sems.at[k],
                                         recv_sem=recv_sems.at[k], device_id=(x, y, 1 - c),
                                         device_id_type=MESH).wait_recv()
        for cp in sends:
            cp.wait_send()

    return pl.pallas_call(
        body, name="rs_share", in_specs=[HBM_SPEC] * n_a, out_specs=[HBM_SPEC] * n_a,
        out_shape=[jax.ShapeDtypeStruct(s.shape, F32) for s in bufs],
        input_output_aliases={k: k for k in range(n_a)},
        scratch_shapes=[pltpu.SemaphoreType.DMA((n_a,)), pltpu.SemaphoreType.DMA((n_a,))],
    )(*bufs)


def _pair_add(part, recv, core, name):
    _, _, n_h, n_c = part.shape
    th = _pick(n_h, (256, 176, 128))

    def body(sel_ref, p_ref, r_ref, o_ref):
        o_ref[...] = (p_ref[...] + r_ref[...]).astype(BF16)

    grid_spec = pltpu.PrefetchScalarGridSpec(
        num_scalar_prefetch=1, grid=(N_CHIPS, n_h // th),
        in_specs=[pl.BlockSpec((None, None, th, n_c), lambda j, i, sel: (j, sel[0], i, 0)),
                  pl.BlockSpec((None, th, n_c), lambda j, i, sel: (j, i, 0))],
        out_specs=pl.BlockSpec((None, th, n_c), lambda j, i, sel: (j, i, 0)))
    return pl.pallas_call(
        body, name=name, grid_spec=grid_spec, out_shape=jax.ShapeDtypeStruct(recv.shape, BF16),
        compiler_params=_cparams(("parallel", "parallel")),
    )(core.reshape(1), part, recv)


def _chip_add(sums, recv, chip, core, name):
    _, n_h, n_c = sums.shape
    th = _pick(n_h, (256, 176, 128))

    def body(sel_ref, s_ref, r_ref, o_ref):
        total = s_ref[...].astype(F32)
        for r in range(3):
            total = total + r_ref[r].astype(F32)
        o_ref[...] = total

    grid_spec = pltpu.PrefetchScalarGridSpec(
        num_scalar_prefetch=1, grid=(n_h // th,),
        in_specs=[pl.BlockSpec((None, th, n_c), lambda i, sel: (sel[0], i, 0)),
                  pl.BlockSpec((3, th, n_c), lambda i, sel: (0, i, 0))],
        out_specs=pl.BlockSpec((None, th, n_c), lambda i, sel: (sel[1], i, 0)))
    return pl.pallas_call(
        body, name=name, grid_spec=grid_spec, out_shape=jax.ShapeDtypeStruct((2, n_h, n_c), F32),
        compiler_params=_cparams(("parallel",)),
    )(jnp.stack([chip, core]), sums, recv)


def _row_halves(a):
    return a.reshape(N_CHIPS, 2, -1, a.shape[-1])


class StepComm:
    REST = ("w_o", "w_up", "w_down")

    def __init__(self, core, chip, rest_shards, in_cols):
        self.core, self.chip, self.shards, self.in_cols = core, chip, rest_shards, in_cols

    def proj_rider(self):
        return _gather_rider([self.shards[0], self.shards[2]])

    def scan_rider(self):
        return _gather_rider([self.shards[1]])

    def weights_from(self, landed_proj, landed_scan):
        landed = (landed_proj[0], landed_scan[0], landed_proj[1])
        g_o, g_up, g_down = (_place_own(g, s, self.chip, "place_own_" + n)
                             for g, s, n in zip(landed, self.shards, self.REST))
        return {"w_o": g_o.reshape(-1, D_MODEL), "w_up": g_up.reshape(N_CHIPS, -1, g_up.shape[-1]),
                "w_down": g_down.reshape(-1, D_MODEL)}

    def _add_pairs(self, parts, from_sibling, names):
        return [_pair_add(p, r, self.core, "pair_add_" + n) for p, r, n in zip(parts, from_sibling, names)]

    def ffn_pair_rider(self, g_w_up, g_w_down):
        self.ffn_parts = [_row_halves(g_w_up), _row_halves(g_w_down)]
        return _pair_rider(self.ffn_parts)

    def wo_pair_rider(self, g_w_o):
        self.wo_parts = [_row_halves(g_w_o)]
        return _pair_rider(self.wo_parts)

    def scan_chips_rider(self, ffn_from_sibling, wo_from_sibling):
        self.scan_sums = self._add_pairs(self.wo_parts + self.ffn_parts, list(wo_from_sibling) + list(ffn_from_sibling),
                                         ("w_o", "w_up", "w_down"))
        return _chips_rider(self.scan_sums)

    def tail_chips_rider(self, g_w_in_p):
        parts = [_row_halves(_w_in_to_chips(g_w_in_p, self.in_cols))]
        self.tail_sums = self._add_pairs(parts, _alone(_pair_rider(parts), "rs_pair_tail"), ("w_in",))
        return _chips_rider(self.tail_sums)

    def finish(self, scan_from_chips, tail_from_chips):
        halves = [_chip_add(s, r, self.chip, self.core, "chip_add_" + n)
                  for s, r, n in zip(self.tail_sums + self.scan_sums, list(tail_from_chips) + list(scan_from_chips),
                                     ("w_in", "w_o", "w_up", "w_down"))]
        return [f.reshape(-1, f.shape[-1]) for f in _rs_share(halves)]


SLAB_W = 1024


def _pack(arrays, rows):
    flat = jnp.concatenate([a.reshape(-1).astype(F32) for a in arrays])
    return jnp.pad(flat, (0, rows * SLAB_W - flat.shape[0])).reshape(rows, SLAB_W)


def _unpack(flat, shapes):
    out, off = [], 0
    for s in shapes:
        n = 1
        for d in s:
            n *= d
        out.append(flat[off:off + n].reshape(s))
        off += n
    return out


def _rows_for(arrays_or_shapes):
    n = 0
    for a in arrays_or_shapes:
        s = a if isinstance(a, tuple) else a.shape
        k = 1
        for d in s:
            k *= d
        n += k
    return -(-n // (8 * SLAB_W)) * 8


def kernel(x, c, ln0_g, ln0_b, w_ada, b_ada, w_in, dn_conv, dn_a_log, dn_dt_bias, dn_norm_g, gla_w_gate2, gla_b_gate, gla_norm_g, w_o, ln1_g, ln1_b, ffn_w_up, ffn_conv, ffn_conv_b, ffn_w_down, ln2_g, ln2_b, loss_target, m_ln0_g, m_ln0_b, m_w_ada, m_b_ada, m_w_in, m_dn_conv, m_dn_a_log, m_dn_dt_bias, m_dn_norm_g, m_gla_w_gate2, m_gla_b_gate, m_gla_norm_g, m_w_o, m_ln1_g, m_ln1_b, m_ffn_w_up, m_ffn_conv, m_ffn_conv_b, m_ffn_w_down, m_ln2_g, m_ln2_b, v_ln0_g, v_ln0_b, v_w_ada, v_b_ada, v_w_in, v_dn_conv, v_dn_a_log, v_dn_dt_bias, v_dn_norm_g, v_gla_w_gate2, v_gla_b_gate, v_gla_norm_g, v_w_o, v_ln1_g, v_ln1_b, v_ffn_w_up, v_ffn_conv, v_ffn_conv_b, v_ffn_w_down, v_ln2_g, v_ln2_b):
    n_b, t_len, _ = x.shape
    xi, yi, ci = _place()
    chip = (2 * xi + yi).astype(jnp.int32)
    core = ci.astype(jnp.int32)
    n_all = N_DEV * n_b
    ada_cols = w_ada.shape[2]

    halves = lambda a: a.astype(BF16).reshape(2, a.shape[0] // 2, a.shape[1])
    w_in_halves = halves(w_in[0])
    sharded_small = [dn_conv[0], gla_w_gate2[0], ffn_conv[0]]
    slab = jnp.concatenate([_pack([c], SEQ_ROWS), _pack(sharded_small, _rows_for(sharded_small))], axis=0)
    b_ada_shard = lax.dynamic_slice(b_ada, (0, chip * ada_cols), (1, ada_cols))
    gathered, cond_pad, mod_recv, (g_in,) = _prologue(slab, w_ada[0], b_ada_shard, _gather_rider([w_in_halves]))
    g_in = _place_own(g_in, w_in_halves, chip, "place_own_w_in")
    cond_all = cond_pad.reshape(N_DEV, SEQ_ROWS, D_MODEL)[:, :n_b].reshape(n_all, D_MODEL)
    by_chip = gathered.reshape(N_DEV, -1)[0::2]
    full, off = [], SEQ_ROWS * SLAB_W
    for a in sharded_small:
        blocks = by_chip[:, off:off + a.size].reshape(N_CHIPS, *a.shape)
        full.append(blocks.transpose(1, 0, 2).reshape(a.shape[0], N_CHIPS * a.shape[1]))
        off += a.size
    dn_conv_f, gate2_f, ffn_conv_f = full
    mod = mod_recv[0::2, :n_b].transpose(1, 0, 2).reshape(n_b, 6, D_MODEL)

    comm = StepComm(core, chip, [halves(w_o[0]), halves(ffn_w_up[0]), halves(ffn_w_down[0])], w_in.shape[2])
    params = {
        "w_in_p": _w_in_to_padded(g_in.reshape(N_CHIPS, -1, g_in.shape[-1])),
        "dn_conv": dn_conv_f, "dn_a_log": dn_a_log[0], "dn_dt_bias": dn_dt_bias[0], "dn_norm_g": dn_norm_g[0],
        "gla_w_gate2": gate2_f, "gla_b_gate": gla_b_gate[0], "gla_norm_g": gla_norm_g[0],
        "ln0_g": ln0_g, "ln0_b": ln0_b, "ln1_g": ln1_g[0], "ln1_b": ln1_b[0], "ln2_g": ln2_g[0], "ln2_b": ln2_b[0],
        "ffn_conv": ffn_conv_f, "ffn_conv_b": ffn_conv_b[0],
    }

    loss_row, grad_x, gp, dmod, from_chips = _local_step(
        x.reshape(n_b * t_len, D_MODEL), loss_target.reshape(n_b * t_len, D_MODEL), mod, params, n_b, t_len, comm)
    names = ["ln0_g", "ln0_b", "w_ada", "b_ada", "w_in", "dn_conv", "dn_a_log", "dn_dt_bias", "dn_norm_g",
             "gla_w_gate2", "gla_b_gate", "gla_norm_g", "w_o", "ln1_g", "ln1_b", "ffn_w_up", "ffn_conv", "ffn_conv_b",
             "ffn_w_down", "ln2_g", "ln2_b"]
    weights = dict(zip(names, [ln0_g, ln0_b, w_ada, b_ada, w_in, dn_conv, dn_a_log, dn_dt_bias, dn_norm_g, gla_w_gate2,
                               gla_b_gate, gla_norm_g, w_o, ln1_g, ln1_b, ffn_w_up, ffn_conv, ffn_conv_b, ffn_w_down,
                               ln2_g, ln2_b]))
    m_in = dict(zip(names, [m_ln0_g, m_ln0_b, m_w_ada, m_b_ada, m_w_in, m_dn_conv, m_dn_a_log, m_dn_dt_bias,
                            m_dn_norm_g, m_gla_w_gate2, m_gla_b_gate, m_gla_norm_g, m_w_o, m_ln1_g, m_ln1_b,
                            m_ffn_w_up, m_ffn_conv, m_ffn_conv_b, m_ffn_w_down, m_ln2_g, m_ln2_b]))
    v_in = dict(zip(names, [v_ln0_g, v_ln0_b, v_w_ada, v_b_ada, v_w_in, v_dn_conv, v_dn_a_log, v_dn_dt_bias,
                            v_dn_norm_g, v_gla_w_gate2, v_gla_b_gate, v_gla_norm_g, v_w_o, v_ln1_g, v_ln1_b,
                            v_ffn_w_up, v_ffn_conv, v_ffn_conv_b, v_ffn_w_down, v_ln2_g, v_ln2_b]))
    grads, delta, new_m, new_v = {}, {}, {}, {}

    def adamw_big(n, grad):
        view = (lambda a: a.T) if n == "w_in" else (lambda a: a)
        outs = _adamw(view(weights[n][0]), view(grad), view(m_in[n][0]), view(v_in[n][0]), "adamw_" + n)
        grads[n] = grad[None]
        delta[n], new_m[n], new_v[n] = (view(a)[None] for a in outs)

    g_w_in, g_w_o, g_w_up, g_w_down = comm.finish(*from_chips)

    summed_names = ["loss", "ln0_g", "ln0_b", "dn_conv", "dn_a_log", "dn_dt_bias", "dn_norm_g", "gla_w_gate2",
                    "gla_b_gate", "gla_norm_g", "ln1_g", "ln1_b", "ffn_conv", "ffn_conv_b", "ln2_g", "ln2_b"]
    summed_parts = [loss_row[0, 0:1]] + [gp[n] for n in summed_names[1:]]
    sum_rows = _rows_for(summed_parts)
    slab = jnp.concatenate([_pack(summed_parts, sum_rows), _pack([dmod], _rows_for([dmod]))], axis=0)
    gathered, total = _all_gather8(slab, "reduce_small")
    small_g = dict(zip(summed_names, _unpack(total.reshape(-1), [a.shape for a in summed_parts])))
    loss = small_g["loss"][0]
    dmod_rows = n_b * 6 * D_MODEL // SLAB_W
    dmod_all = gathered[:, sum_rows:sum_rows + dmod_rows, :].reshape(n_all, 6 * D_MODEL)
    for n, grad in (("ffn_w_up", g_w_up), ("ffn_w_down", g_w_down), ("w_o", g_w_o), ("w_in", g_w_in)):
        adamw_big(n, grad)

    g_b_ada = _col_sum(dmod_all)
    dmod_cols = lax.dynamic_slice(dmod_all, (0, chip * ada_cols), (n_all, ada_cols))
    adamw_big("w_ada", _mm(cond_all, dmod_cols, ta=True, name="mm_g_ada"))

    col_block = lambda a: lax.dynamic_slice(a, (0, chip * (a.shape[1] // N_CHIPS)), (a.shape[0], a.shape[1] // N_CHIPS))
    grads.update({
        "ln0_g": small_g["ln0_g"], "ln0_b": small_g["ln0_b"], "b_ada": g_b_ada,
        "dn_conv": col_block(small_g["dn_conv"])[None], "dn_a_log": small_g["dn_a_log"][None],
        "dn_dt_bias": small_g["dn_dt_bias"][None], "dn_norm_g": small_g["dn_norm_g"][None],
        "gla_w_gate2": col_block(small_g["gla_w_gate2"])[None], "gla_b_gate": small_g["gla_b_gate"][None],
        "gla_norm_g": small_g["gla_norm_g"][None], "ln1_g": small_g["ln1_g"][None],
        "ln1_b": small_g["ln1_b"][None], "ffn_conv": col_block(small_g["ffn_conv"])[None],
        "ffn_conv_b": small_g["ffn_conv_b"][None], "ln2_g": small_g["ln2_g"][None], "ln2_b": small_g["ln2_b"][None],
    })
    small = [n for n in names if n not in delta]
    d_s, m_s, v_s = _adamw_many([weights[n] for n in small], [grads[n] for n in small],
                                [m_in[n] for n in small], [v_in[n] for n in small])
    for out, vals in ((delta, d_s), (new_m, m_s), (new_v, v_s)):
        out.update(zip(small, vals))

    return (loss, grad_x.reshape(x.shape), *[grads[n] for n in names], *[delta[n] for n in names],
            *[new_m[n] for n in names], *[new_v[n] for n in names])
```

```python
import functools

import jax
import jax.numpy as jnp
from jax import lax
from jax.experimental import pallas as pl
from jax.experimental.pallas import tpu as pltpu

F32 = jnp.float32
BF16 = jnp.bfloat16
MESH = pl.DeviceIdType.MESH

D_MODEL = 1024
HEADS = 4
HEAD_DIM = 128
GLA_KEY = 64
GATE_RANK = 16
CHUNK = 64
D_FF = 2816
ALPHA = 2.0 ** 0.25
EPS = 1e-6
N_CHIPS = 4
N_DEV = 8

PROJ_W = 3840
OFF_GQ, OFF_GK, OFF_GV, OFF_GG, OFF_SMALL, GLA_W = 0, 256, 512, 1024, 1536, 1792
OFF_Z = 2048
W_IN_COLS = 3608


def _qkv_block(j):
    return jnp.where(j < 2, GLA_W // 128 + j, (OFF_Z + 512) // 128 - 2 + j)

ADAM_LR, ADAM_B1, ADAM_B2, ADAM_EPS, ADAM_WD, ADAM_STEP = 0.001, 0.9, 0.999, 1e-08, 0.01, 10

VMEM_LIMIT = 56 * 1024 * 1024
ROW_TILE = 512


def _cparams(sem):
    return pltpu.CompilerParams(dimension_semantics=sem, vmem_limit_bytes=VMEM_LIMIT)


def _pick(n, prefs):
    for p in prefs:
        if n % p == 0:
            return p
    return n


def _mm(a, b, *, ta=False, tb=False, out_slabs=1, out_dtype=F32, name, rider=None):
    a_slabs = a.shape[0] if a.ndim == 3 else 1
    b_slabs = b.shape[0] if b.ndim == 3 else 1
    assert not (ta and a_slabs > 1)
    a2, b2 = a.shape[-2:], b.shape[-2:]
    if ta:
        k_dim, m_dim = a2
    else:
        m_dim, k_dim = a2[0], a2[1] * a_slabs
    n_dim = b2[0] if tb else b2[1] * b_slabs
    k_slabs = max(a_slabs, b_slabs if tb else 1)
    n_slabs = max(out_slabs, 1 if tb else b_slabs)
    tm = _pick(m_dim, (1024, 1408, 512, 256, 128))
    tn = _pick(n_dim // n_slabs, (1536, 1408, 1280, 1024, 768, 512, 384, 256, 128))
    tk = _pick(k_dim // k_slabs, (1408, 1280, 1024, 512, 256, 128))
    nk, nj = k_dim // tk, n_dim // tn
    nk_a, nk_b, nj_b, nj_o = nk // a_slabs, nk // b_slabs, nj // b_slabs, nj // out_slabs
    dims = (((0 if ta else 1,), (1 if tb else 0,)), ((), ()))

    grid = (m_dim // tm, nj, nk)
    assert out_dtype == F32
    r_inputs, r_in_specs, r_out_specs, r_sems, split = _with_rider(rider, 2, 1, 0)

    def body(*refs):
        (a_ref, b_ref, o_ref), parts = split(refs)
        ride_first, ride_last = _ride(rider, parts, grid)
        if rider is not None:
            ride_first()
        prod = lax.dot_general(a_ref[...].astype(BF16), b_ref[...].astype(BF16), dims, preferred_element_type=F32)
        if nk == 1:
            o_ref[...] = prod
        else:
            _acc(o_ref, prod, pl.program_id(2) == 0)
        if rider is not None:
            ride_last()

    if ta:
        a_spec = pl.BlockSpec((tk, tm), lambda i, j, k: (k, i))
    elif a_slabs > 1:
        a_spec = pl.BlockSpec((None, tm, tk), lambda i, j, k: (k // nk_a, i, k % nk_a))
    else:
        a_spec = pl.BlockSpec((tm, tk), lambda i, j, k: (i, k))
    if tb and b_slabs > 1:
        b_spec = pl.BlockSpec((None, tn, tk), lambda i, j, k: (k // nk_b, j, k % nk_b))
    elif tb:
        b_spec = pl.BlockSpec((tn, tk), lambda i, j, k: (j, k))
    elif b_slabs > 1:
        b_spec = pl.BlockSpec((None, tk, tn), lambda i, j, k: (j // nj_b, k, j % nj_b))
    else:
        b_spec = pl.BlockSpec((tk, tn), lambda i, j, k: (k, j))
    if out_slabs > 1:
        o_spec = pl.BlockSpec((None, tm, tn), lambda i, j, k: (j // nj_o, i, j % nj_o))
        o_shape = (out_slabs, m_dim, n_dim // out_slabs)
    else:
        o_spec, o_shape = pl.BlockSpec((tm, tn), lambda i, j, k: (i, j)), (m_dim, n_dim)
    out, *rider_outs = pl.pallas_call(
        body, name=name, grid=grid,
        in_specs=[a_spec, b_spec] + r_in_specs, out_specs=[o_spec] + r_out_specs,
        out_shape=[jax.ShapeDtypeStruct(o_shape, out_dtype)] + (list(rider.out_shapes) if rider else []),
        scratch_shapes=r_sems,
        compiler_params=_cparams(("parallel", "parallel", "arbitrary")),
    )(a, b, *r_inputs)
    return (out, rider_outs) if rider else out


def _ln(x, g, b):
    mu = jnp.mean(x, -1, keepdims=True)
    xc = x - mu
    var = jnp.mean(xc * xc, -1, keepdims=True)
    return xc * lax.rsqrt(var + EPS) * g + b


def _softplus(x):
    return jnp.maximum(x, 0.0) + jnp.log(1.0 + jnp.exp(-jnp.abs(x)))


def _silu(x):
    return x * jax.nn.sigmoid(x)


def _dsilu(x):
    s = jax.nn.sigmoid(x)
    return s * (1.0 + x * (1.0 - s))


def _f_ln0(x, g, b, sc, sh):
    x0 = _ln(x, g, b)
    return x0, x0 * (1.0 + sc) + sh


def _f_ln1(x0, y, gt, g, b, sc, sh):
    x1 = _ln(ALPHA * x0 + (1.0 + gt) * y, g, b)
    return x1, x1 * (1.0 + sc) + sh


def _f_ln2_loss(x1, y2, gt, g, b, tgt):
    x2 = _ln(ALPHA * x1 + (1.0 + gt) * y2, g, b)
    err = x2 - tgt
    per_row = jnp.sum(err * err, -1, keepdims=True) * (0.5 / D_MODEL)
    return jnp.sum(per_row, 0, keepdims=True)


def _row_specs(t_len):
    nt = t_len // ROW_TILE
    row = pl.BlockSpec((ROW_TILE, D_MODEL), lambda b, i: (b * nt + i, 0))
    vec = pl.BlockSpec((1, D_MODEL), lambda b, i: (0, 0))
    mod = pl.BlockSpec((None, 6, D_MODEL), lambda b, i: (b, 0, 0))
    return nt, row, vec, mod


def _first_step():
    return jnp.logical_and(pl.program_id(0) == 0, pl.program_id(1) == 0)


def _acc(ref, val, first, at=(Ellipsis,)):
    @pl.when(first)
    def _():
        ref[at] = val

    @pl.when(jnp.logical_not(first))
    def _():
        ref[at] += val


def _acc_rows(ref, rows, first):
    for i, r in enumerate(rows):
        _acc(ref, r, first, at=(slice(i, i + 1), slice(None)))


def _ln0_fwd(x, g, b, mod, n_b, t_len):
    nt, row, vec, mods = _row_specs(t_len)

    def body(x_ref, g_ref, b_ref, mod_ref, x0_ref, h_ref):
        x0, h = _f_ln0(x_ref[...], g_ref[...], b_ref[...], mod_ref[1:2, :], mod_ref[0:1, :])
        x0_ref[...] = x0
        h_ref[...] = h.astype(BF16)

    return pl.pallas_call(
        body, name="ln0_fwd", grid=(n_b, nt), in_specs=[row, vec, vec, mods], out_specs=[row, row],
        out_shape=[jax.ShapeDtypeStruct(x.shape, F32), jax.ShapeDtypeStruct(x.shape, BF16)],
        compiler_params=_cparams(("parallel", "parallel")),
    )(x, g, b, mod)


def _ln0_bwd(x, g, b, mod, dx0, dh, n_b, t_len):
    nt, row, vec, mods = _row_specs(t_len)
    dmod_spec = pl.BlockSpec((None, 2, D_MODEL), lambda bb, i: (bb, 0, 0))

    def body(x_ref, g_ref, b_ref, mod_ref, dx0_ref, dh_ref, dx_ref, dg_ref, db_ref, dmod_ref):
        _, pull = jax.vjp(_f_ln0, x_ref[...], g_ref[...], b_ref[...], mod_ref[1:2, :], mod_ref[0:1, :])
        dx, dg, db, dsc, dsh = pull((dx0_ref[...], dh_ref[...]))
        dx_ref[...] = dx
        _acc(dg_ref, dg, _first_step())
        _acc(db_ref, db, _first_step())
        _acc_rows(dmod_ref, [dsh, dsc], pl.program_id(1) == 0)

    return pl.pallas_call(
        body, name="ln0_bwd", grid=(n_b, nt), in_specs=[row, vec, vec, mods, row, row],
        out_specs=[row, vec, vec, dmod_spec],
        out_shape=[jax.ShapeDtypeStruct(x.shape, F32), jax.ShapeDtypeStruct((1, D_MODEL), F32),
                   jax.ShapeDtypeStruct((1, D_MODEL), F32), jax.ShapeDtypeStruct((n_b, 2, D_MODEL), F32)],
        compiler_params=_cparams(("arbitrary", "arbitrary")),
    )(x, g, b, mod, dx0, dh)


def _ln1_fwd(x0, y, g, b, mod, n_b, t_len):
    nt, row, vec, mods = _row_specs(t_len)

    def body(x0_ref, y_ref, g_ref, b_ref, mod_ref, x1_ref, h_ref):
        x1, h = _f_ln1(x0_ref[...], y_ref[...], mod_ref[2:3, :], g_ref[...], b_ref[...],
                       mod_ref[4:5, :], mod_ref[3:4, :])
        x1_ref[...] = x1
        h_ref[...] = h.astype(BF16)

    return pl.pallas_call(
        body, name="ln1_fwd", grid=(n_b, nt), in_specs=[row, row, vec, vec, mods], out_specs=[row, row],
        out_shape=[jax.ShapeDtypeStruct(x0.shape, F32), jax.ShapeDtypeStruct(x0.shape, BF16)],
        compiler_params=_cparams(("parallel", "parallel")),
    )(x0, y, g, b, mod)


def _ln1_bwd(x0, y, g, b, mod, dx1, dh, n_b, t_len):
    nt, row, vec, mods = _row_specs(t_len)
    dmod_spec = pl.BlockSpec((None, 3, D_MODEL), lambda bb, i: (bb, 0, 0))

    def body(x0_ref, y_ref, g_ref, b_ref, mod_ref, dx1_ref, dh_ref, dx0_ref, dy_ref, dg_ref, db_ref, dmod_ref):
        _, pull = jax.vjp(_f_ln1, x0_ref[...], y_ref[...], mod_ref[2:3, :], g_ref[...], b_ref[...],
                          mod_ref[4:5, :], mod_ref[3:4, :])
        dx0, dy, dgt, dg, db, dsc, dsh = pull((dx1_ref[...], dh_ref[...]))
        dx0_ref[...] = dx0
        dy_ref[...] = dy.astype(BF16)
        _acc(dg_ref, dg, _first_step())
        _acc(db_ref, db, _first_step())
        _acc_rows(dmod_ref, [dgt, dsh, dsc], pl.program_id(1) == 0)

    return pl.pallas_call(
        body, name="ln1_bwd", grid=(n_b, nt), in_specs=[row, row, vec, vec, mods, row, row],
        out_specs=[row, row, vec, vec, dmod_spec],
        out_shape=[jax.ShapeDtypeStruct(x0.shape, F32), jax.ShapeDtypeStruct(x0.shape, BF16),
                   jax.ShapeDtypeStruct((1, D_MODEL), F32), jax.ShapeDtypeStruct((1, D_MODEL), F32),
                   jax.ShapeDtypeStruct((n_b, 3, D_MODEL), F32)],
        compiler_params=_cparams(("arbitrary", "arbitrary")),
    )(x0, y, g, b, mod, dx1, dh)


def _ln2_loss_bwd(x1, y2, g, b, mod, tgt, n_b, t_len):
    nt, row, vec, mods = _row_specs(t_len)
    one = pl.BlockSpec((1, 128), lambda bb, i: (0, 0))
    dmod_spec = pl.BlockSpec((None, 1, D_MODEL), lambda bb, i: (bb, 0, 0))

    def body(x1_ref, y2_ref, g_ref, b_ref, mod_ref, t_ref, loss_ref, dx1_ref, dy2_ref, dg_ref, db_ref, dgt_ref):
        loss, pull = jax.vjp(functools.partial(_f_ln2_loss, tgt=t_ref[...]), x1_ref[...], y2_ref[...],
                             mod_ref[5:6, :], g_ref[...], b_ref[...])
        dx1, dy2, dgt, dg, db = pull(jnp.ones((1, 1), F32))
        dx1_ref[...] = dx1
        dy2_ref[...] = dy2.astype(BF16)
        _acc(loss_ref, jnp.broadcast_to(loss, (1, 128)), _first_step())
        _acc(dg_ref, dg, _first_step())
        _acc(db_ref, db, _first_step())
        _acc(dgt_ref, dgt, pl.program_id(1) == 0)

    return pl.pallas_call(
        body, name="ln2_loss_bwd", grid=(n_b, nt), in_specs=[row, row, vec, vec, mods, row],
        out_specs=[one, row, row, vec, vec, dmod_spec],
        out_shape=[jax.ShapeDtypeStruct((1, 128), F32), jax.ShapeDtypeStruct(x1.shape, F32),
                   jax.ShapeDtypeStruct(x1.shape, BF16), jax.ShapeDtypeStruct((1, D_MODEL), F32),
                   jax.ShapeDtypeStruct((1, D_MODEL), F32), jax.ShapeDtypeStruct((n_b, 1, D_MODEL), F32)],
        compiler_params=_cparams(("arbitrary", "arbitrary")),
    )(x1, y2, g, b, mod, tgt)


def _shift_down(x, s):
    if s == 0:
        return x
    rows = lax.broadcasted_iota(jnp.int32, x.shape, 0)
    return jnp.where(rows >= s, pltpu.roll(x, s, 0), 0.0)


def _shift_up(x, s):
    if s == 0:
        return x
    t_len = x.shape[0]
    rows = lax.broadcasted_iota(jnp.int32, x.shape, 0)
    return jnp.where(rows < t_len - s, pltpu.roll(x, t_len - s, 0), 0.0)


def _taps(x, k_w):
    return [_shift_down(x, k_w - 1 - k) for k in range(k_w)]


def _conv(taps, w):
    out = w[0:1, :] * taps[0]
    for k in range(1, len(taps)):
        out = out + w[k:k + 1, :] * taps[k]
    return out


def _conv_bwd(taps, w, du):
    k_w = len(taps)
    dx = w[k_w - 1:k_w, :] * du
    for k in range(k_w - 1):
        dx = dx + w[k:k + 1, :] * _shift_up(du, k_w - 1 - k)
    return dx, [jnp.sum(du * taps[k], 0, keepdims=True) for k in range(k_w)]


def _dn_pre_fwd(proj, conv_w, n_b, t_len):
    n_ct = 3 * HEADS
    k_w = conv_w.shape[0]

    def body(x_ref, w_ref, o_ref):
        o_ref[...] = _silu(_conv(_taps(x_ref[...], k_w), w_ref[...]))

    return pl.pallas_call(
        body, name="dn_pre_fwd", grid=(n_ct, n_b),
        in_specs=[pl.BlockSpec((t_len, 128), lambda j, b: (b, _qkv_block(j))),
                  pl.BlockSpec((k_w, 128), lambda j, b: (0, j))],
        out_specs=pl.BlockSpec((t_len, 128), lambda j, b: (b, j)),
        out_shape=jax.ShapeDtypeStruct((n_b * t_len, n_ct * 128), F32),
        compiler_params=_cparams(("parallel", "parallel")),
    )(proj, conv_w)


def _dn_pre_bwd(proj, conv_w, dqkv, d_proj, n_b, t_len):
    n_ct = 3 * HEADS
    k_w = conv_w.shape[0]

    def body(x_ref, w_ref, d_ref, _, dx_ref, dw_ref):
        taps, w = _taps(x_ref[...], k_w), w_ref[...]
        du = d_ref[...] * _dsilu(_conv(taps, w))
        dx, dw = _conv_bwd(taps, w, du)
        dx_ref[...] = dx.astype(BF16)
        _acc_rows(dw_ref, dw, pl.program_id(1) == 0)

    return pl.pallas_call(
        body, name="dn_pre_bwd", grid=(n_ct, n_b),
        in_specs=[pl.BlockSpec((t_len, 128), lambda j, b: (b, _qkv_block(j))),
                  pl.BlockSpec((k_w, 128), lambda j, b: (0, j)),
                  pl.BlockSpec((t_len, 128), lambda j, b: (b, j)), pl.BlockSpec(memory_space=pl.ANY)],
        out_specs=[pl.BlockSpec((t_len, 128), lambda j, b: (b, _qkv_block(j))),
                   pl.BlockSpec((k_w, 128), lambda j, b: (0, j))],
        out_shape=[jax.ShapeDtypeStruct(d_proj.shape, BF16), jax.ShapeDtypeStruct((k_w, n_ct * 128), F32)],
        input_output_aliases={3: 0},
        compiler_params=_cparams(("parallel", "arbitrary")),
    )(proj, conv_w, dqkv, d_proj)


FFN_TC = 256
FFN_NT = D_FF // FFN_TC


def _ffn_specs(t_len):
    blk = lambda off: pl.BlockSpec((t_len, FFN_TC), lambda j, b: (b, j + off))
    wblk = lambda off: pl.BlockSpec((3, FFN_TC), lambda j, b: (0, j + off))
    bblk = lambda off: pl.BlockSpec((1, FFN_TC), lambda j, b: (0, j + off))
    return [blk(0), blk(FFN_NT), wblk(0), wblk(FFN_NT), bblk(0), bblk(FFN_NT)]


def _ffn_act_fwd(up, conv_w, conv_b, n_b, t_len):
    def body(g_ref, v_ref, wg_ref, wv_ref, bg_ref, bv_ref, o_ref):
        ug = _conv(_taps(g_ref[...], 3), wg_ref[...]) + bg_ref[...]
        uv = _conv(_taps(v_ref[...], 3), wv_ref[...]) + bv_ref[...]
        o_ref[...] = (_silu(ug) * uv).astype(BF16)

    return pl.pallas_call(
        body, name="ffn_act_fwd", grid=(FFN_NT, n_b), in_specs=_ffn_specs(t_len),
        out_specs=pl.BlockSpec((t_len, FFN_TC), lambda j, b: (b, j)),
        out_shape=jax.ShapeDtypeStruct((n_b * t_len, D_FF), BF16),
        compiler_params=_cparams(("parallel", "parallel")),
    )(up, up, conv_w, conv_w, conv_b, conv_b)


def _ffn_act_bwd(up, conv_w, conv_b, da, n_b, t_len):
    def body(g_ref, v_ref, wg_ref, wv_ref, bg_ref, bv_ref, da_ref, dup_ref, dw_ref, db_ref):
        first = pl.program_id(1) == 0
        tg, tv, wg, wv = _taps(g_ref[...], 3), _taps(v_ref[...], 3), wg_ref[...], wv_ref[...]
        ug = _conv(tg, wg) + bg_ref[...]
        uv = _conv(tv, wv) + bv_ref[...]
        d_act = da_ref[...]
        sig = jax.nn.sigmoid(ug)
        d_v = d_act * (ug * sig)
        d_g = d_act * uv * (sig * (1.0 + ug * (1.0 - sig)))
        for slab, (taps, w, du) in enumerate(((tg, wg, d_g), (tv, wv, d_v))):
            dx, dw = _conv_bwd(taps, w, du)
            dup_ref[slab] = dx.astype(BF16)
            for k, dw_k in enumerate(dw):
                _acc(dw_ref, dw_k, first, at=(slab, slice(k, k + 1), slice(None)))
            _acc(db_ref, jnp.sum(du, 0, keepdims=True), first, at=(slab, slice(None), slice(None)))

    return pl.pallas_call(
        body, name="ffn_act_bwd", grid=(FFN_NT, n_b),
        in_specs=_ffn_specs(t_len) + [pl.BlockSpec((t_len, FFN_TC), lambda j, b: (b, j))],
        out_specs=[pl.BlockSpec((2, t_len, FFN_TC), lambda j, b: (0, b, j)),
                   pl.BlockSpec((2, 3, FFN_TC), lambda j, b: (0, 0, j)),
                   pl.BlockSpec((2, 1, FFN_TC), lambda j, b: (0, 0, j))],
        out_shape=[jax.ShapeDtypeStruct((2, n_b * t_len, D_FF), BF16),
                   jax.ShapeDtypeStruct((2, 3, D_FF), F32), jax.ShapeDtypeStruct((2, 1, D_FF), F32)],
        compiler_params=_cparams(("parallel", "arbitrary")),
    )(up, up, conv_w, conv_w, conv_b, conv_b, da)


NN = (((2,), (1,)), ((0,), (0,)))
NT = (((2,), (2,)), ((0,), (0,)))
TN = (((1,), (1,)), ((0,), (0,)))


def _iota3(shape, axis):
    return lax.broadcasted_iota(jnp.int32, shape, axis)


def _dg(a, b, dims):
    return lax.dot_general(a, b, dims, preferred_element_type=F32)


def _dot(a, b):
    return _dg(a, b, NN)


def _dot_nt(a, b):
    return _dg(a, b, NT)


def _dot_tn(a, b):
    return _dg(a, b, TN)


def _split(a):
    hi = a.astype(BF16)
    return hi, (a - hi.astype(F32)).astype(BF16)


def _dg3(a, b, dims):
    ah, al = _split(a)
    bh, bl = _split(b)
    return _dg(ah, bh, dims) + (_dg(ah, bl, dims) + _dg(al, bh, dims))


@jax.custom_vjp
def _dot3(a, b):
    return _dg3(a, b, NN)


def _dot3_fwd(a, b):
    return _dg3(a, b, NN), (a, b)


def _dot3_bwd(res, g):
    a, b = res
    return _dg3(g, b, NT), _dg3(a, g, TN)


_dot3.defvjp(_dot3_fwd, _dot3_bwd)


def _lower_ones(g_n, n):
    shape = (g_n, n, n)
    return jnp.where(_iota3(shape, 1) >= _iota3(shape, 2), 1.0, 0.0).astype(BF16)


@jax.custom_vjp
def _chunk_cumsum(x):
    hi, lo = _split(x)
    tri = _lower_ones(x.shape[0], x.shape[1])
    return _dg(tri, hi, NN) + _dg(tri, lo, NN)


def _chunk_cumsum_fwd(x):
    return _chunk_cumsum(x), None


def _chunk_cumsum_bwd(_, g):
    hi, lo = _split(g)
    tri = _lower_ones(g.shape[0], g.shape[1])
    return (_dg(tri, hi, TN) + _dg(tri, lo, TN),)


_chunk_cumsum.defvjp(_chunk_cumsum_fwd, _chunk_cumsum_bwd)


@jax.custom_vjp
def _unit_lower_inv(m):
    n = m.shape[1]
    p = -m
    a = jnp.where(_iota3(m.shape, 1) == _iota3(m.shape, 2), 1.0, 0.0) + p
    span = 2
    while span < n:
        p = _dg3(p, p, NN)
        a = a + _dg3(a, p, NN)
        span *= 2
    return a


def _unit_lower_inv_fwd(m):
    a = _unit_lower_inv(m)
    return a, a


def _unit_lower_inv_bwd(a, da):
    return (-_dg3(a, _dg3(da, a, NT), TN),)


_unit_lower_inv.defvjp(_unit_lower_inv_fwd, _unit_lower_inv_bwd)


@jax.custom_vjp
def _saved_lower_inv(m, a):
    return a


def _saved_lower_inv_fwd(m, a):
    return a, a


def _saved_lower_inv_bwd(a, da):
    return _unit_lower_inv_bwd(a, da)[0], jnp.zeros_like(a)


_saved_lower_inv.defvjp(_saved_lower_inv_fwd, _saved_lower_inv_bwd)


def _rms_gate(o, gn, gate):
    return o * lax.rsqrt(jnp.mean(o * o, -1, keepdims=True) + EPS) * gn * _silu(gate)


def _dn_chains(q, k, v, z, small, s_in, a_log, dt_bias, gn, a_saved=None):
    prep = _dn_prepare(q, k, v, small, a_log, dt_bias, a_saved)
    og, s_out = _dn_advance(prep[:-1], z, s_in, gn)
    return og, s_out, prep[-1]


def _dn_prepare(q, k, v, small, a_log, dt_bias, a_saved=None):
    g_n, c_len = q.shape[0], q.shape[1]
    sq = (g_n, c_len, c_len)
    row, col = _iota3(sq, 1), _iota3(sq, 2)
    causal, strict, eye = row >= col, row > col, row == col
    qn = q * lax.rsqrt(jnp.sum(q * q, -1, keepdims=True) + EPS) * (HEAD_DIM ** -0.5)
    kn = k * lax.rsqrt(jnp.sum(k * k, -1, keepdims=True) + EPS)
    lane = _iota3(small.shape, 2)
    head = jnp.bitwise_and(_iota3(small.shape, 0), HEADS - 1)
    la_all = -jnp.exp(a_log) * _softplus(small + dt_bias)
    la_c = jnp.sum(jnp.where(lane == head, la_all, 0.0), 2, keepdims=True)
    beta = jnp.sum(jnp.where(lane == head + HEADS, jax.nn.sigmoid(small), 0.0), 2, keepdims=True)
    la_b = jnp.broadcast_to(la_c, sq)
    la_r = jnp.sum(jnp.where(eye, la_b, 0.0), 1, keepdims=True)
    g_c = jnp.sum(jnp.where(causal, jnp.broadcast_to(la_r, sq), 0.0), 2, keepdims=True)
    g_r = jnp.sum(jnp.where(row <= col, la_b, 0.0), 1, keepdims=True)
    g_last = jnp.sum(la_c, 1, keepdims=True)
    decay = jnp.exp(jnp.where(causal, g_c - g_r, -1e30))
    e_g = jnp.exp(g_c)
    kb = kn * beta
    m_low = jnp.where(strict, _dot_nt(kb, kn) * decay, 0.0)
    a_inv = _unit_lower_inv(m_low) if a_saved is None else _saved_lower_inv(m_low, a_saved)
    u = _dot3(a_inv, v * beta)
    w = _dot3(a_inv, kb * e_g)
    attn = _dot_nt(qn, kn) * decay
    return u, w, attn, qn * e_g, kn * jnp.exp(g_last - g_c), jnp.exp(g_last), a_inv


def _dn_advance(prep, z, s_in, gn):
    u, w, attn, q_dec, k_dec, g_chunk = prep
    v_new = u - _dot(w, s_in)
    o = _dot(q_dec, s_in) + _dot(attn, v_new)
    s_out = s_in * g_chunk + _dot_tn(k_dec, v_new)
    return _rms_gate(o, gn, z), s_out


def _gla_chains(q, k, v, gate, small, s_in, w2, b2, gn):
    g_n, c_len = q.shape[0], q.shape[1]
    sq, kk = (g_n, c_len, c_len), (g_n, GLA_KEY, GLA_KEY)
    causal = _iota3(sq, 1) >= _iota3(sq, 2)
    la = -_softplus(-(_dot(small, w2) + b2)) * (1.0 / 16.0)
    b = _chunk_cumsum(la)
    b_last = jnp.sum(jnp.where(_iota3(b.shape, 1) == c_len - 1, b, 0.0), 1, keepdims=True)
    q_dec = q * (GLA_KEY ** -0.5) * jnp.exp(b)
    attn = jnp.where(causal, _dot_nt(q_dec, k * jnp.exp(-b)), 0.0)
    o = _dot(q_dec, s_in) + _dot(attn, v)
    g_row = jnp.exp(b_last)
    g_col = jnp.sum(jnp.where(_iota3(kk, 1) == _iota3(kk, 2), jnp.broadcast_to(g_row, kk), 0.0), 2, keepdims=True)
    s_out = s_in * g_col + _dot_tn(k * jnp.exp(b_last - b), v)
    return _rms_gate(o, gn, gate), s_out


def _chunk_spec(n_b, width, col_block, n_c, reverse=False):
    if reverse:
        return pl.BlockSpec((n_b, CHUNK, width), lambda n: (0, n_c - 1 - n, col_block))
    return pl.BlockSpec((n_b, CHUNK, width), lambda n: (0, n, col_block))


def _hist_spec(n_b, d_k, n_c, reverse=False):
    if reverse:
        return pl.BlockSpec((None, n_b * HEADS, d_k, HEAD_DIM), lambda n: (n_c - 1 - n, 0, 0, 0))
    return pl.BlockSpec((None, n_b * HEADS, d_k, HEAD_DIM), lambda n: (n, 0, 0, 0))


def _ainv_spec(n_b, n_c, reverse=False):
    if reverse:
        return pl.BlockSpec((None, n_b * HEADS, CHUNK, CHUNK), lambda n: (n_c - 1 - n, 0, 0, 0))
    return pl.BlockSpec((None, n_b * HEADS, CHUNK, CHUNK), lambda n: (n, 0, 0, 0))


def _stack_chains(ref, n_b, slices):
    return jnp.stack([ref[b, :, sl] for b in range(n_b) for sl in slices], axis=0)


def _per_chain(ref, n_b):
    return jnp.stack([ref[b] for b in range(n_b) for _ in range(HEADS)], axis=0)


def _unstack_chains(ref, val, n_b, slices, offset=0):
    for b in range(n_b):
        for h, sl in enumerate(slices):
            ref[b, :, slice(offset + sl.start, offset + sl.stop)] = val[b * HEADS + h].astype(ref.dtype)


def _gate_weights(w2_ref, b2_ref, n_b):
    w2 = jnp.stack([w2_ref[:, ks] for _ in range(n_b) for ks in GLA_KSL], axis=0)
    b2 = jnp.stack([b2_ref[:, ks] for _ in range(n_b) for ks in GLA_KSL], axis=0)
    return w2, b2


def _sum_heads(val, n_b):
    return [sum(val[b * HEADS + h] for h in range(HEADS)) for b in range(n_b)]


def _const_spec(shape):
    return pl.BlockSpec(shape, lambda n: (0,) * len(shape))


DN_SL = [slice(h * HEAD_DIM, (h + 1) * HEAD_DIM) for h in range(HEADS)]
GLA_KSL = [slice(h * GLA_KEY, (h + 1) * GLA_KEY) for h in range(HEADS)]


class Rider:
    def __init__(self, inputs, out_shapes, sems, first, last):
        self.inputs, self.out_shapes, self.sems, self.first, self.last = inputs, out_shapes, sems, first, last


def _with_rider(rider, n_in, n_out, n_scratch):
    if rider is None:
        return [], [], [], [], lambda refs: (refs, None)
    r_in, r_out, r_sem = len(rider.inputs), len(rider.out_shapes), len(rider.sems)

    def split(refs):
        own_in, rest = refs[:n_in], refs[n_in:]
        rid_in, rest = rest[:r_in], rest[r_in:]
        own_out, rest = rest[:n_out], rest[n_out:]
        rid_out, rest = rest[:r_out], rest[r_out:]
        own_scr, rid_sem = rest[:n_scratch], rest[n_scratch:]
        return own_in + own_out + own_scr, (rid_in, rid_out, rid_sem)

    return list(rider.inputs), [HBM_SPEC] * r_in, [HBM_SPEC] * r_out, list(rider.sems), split


def _ride(rider, parts, grid):
    if rider is None:
        return None, None
    grid = grid if isinstance(grid, tuple) else (grid,)

    def at(step_of):
        hit = pl.program_id(0) == step_of(grid[0])
        for axis in range(1, len(grid)):
            hit = jnp.logical_and(hit, pl.program_id(axis) == step_of(grid[axis]))
        return hit

    def first():
        pl.when(at(lambda n: 0))(lambda: rider.first(*parts))

    def last():
        pl.when(at(lambda n: n - 1))(lambda: rider.last(*parts))

    return first, last


FWD_CHUNKS = 4


def _dn_scan_fwd(qkv, proj, a_log, dt_bias, gn, n_b, t_len, rider=None):
    n_c, n_g, rows = t_len // CHUNK, n_b * HEADS, FWD_CHUNKS * CHUNK
    n_s = n_c // FWD_CHUNKS
    spec = lambda width, col_block: pl.BlockSpec((n_b, rows, width), lambda n: (0, n, col_block))
    kept = lambda d0, d1: pl.BlockSpec((FWD_CHUNKS, n_g, d0, d1), lambda n: (n, 0, 0, 0))
    r_inputs, r_in_specs, r_out_specs, r_sems, split = _with_rider(rider, 8, 3, 1)
    chunk_rows = [slice(j * CHUNK, (j + 1) * CHUNK) for j in range(FWD_CHUNKS)]

    def body(*refs):
        (q_ref, k_ref, v_ref, z_ref, sm_ref, al_ref, dt_ref, gn_ref,
         o_ref, hist_ref, ainv_ref, s_ref), parts = split(refs)
        ride_first, ride_last = _ride(rider, parts, n_s)
        if rider is not None:
            ride_first()

        @pl.when(pl.program_id(0) == 0)
        def _():
            s_ref[...] = jnp.zeros_like(s_ref)

        def stack(ref, slices):
            return jnp.stack([ref[b, rs, sl] for rs in chunk_rows for b in range(n_b) for sl in slices], axis=0)

        prep = _dn_prepare(stack(q_ref, DN_SL), stack(k_ref, DN_SL), stack(v_ref, DN_SL),
                           stack(sm_ref, [slice(None)] * HEADS), al_ref[...], dt_ref[...])
        z, state = stack(z_ref, DN_SL), s_ref[...]
        for j, rs in enumerate(chunk_rows):
            mine = slice(j * n_g, (j + 1) * n_g)
            hist_ref[j] = state
            ainv_ref[j] = prep[-1][mine]
            og, state = _dn_advance(tuple(a[mine] for a in prep[:-1]), z[mine], state, gn_ref[...])
            for b in range(n_b):
                for h, sl in enumerate(DN_SL):
                    o_ref[b, rs, sl] = og[b * HEADS + h].astype(BF16)
        s_ref[...] = state
        if rider is not None:
            ride_last()

    qkv3, proj3 = qkv.reshape(n_b, t_len, -1), proj.reshape(n_b, t_len, -1)
    o, hist, ainv, *rider_outs = pl.pallas_call(
        body, name="dn_scan_fwd", grid=(n_s,),
        in_specs=[spec(512, 0), spec(512, 1), spec(512, 2), spec(512, OFF_Z // 512), spec(128, OFF_SMALL // 128),
                  _const_spec((1, 128)), _const_spec((1, 128)), _const_spec((1, 128))] + r_in_specs,
        out_specs=[spec(512, 0), kept(HEAD_DIM, HEAD_DIM), kept(CHUNK, CHUNK)] + r_out_specs,
        out_shape=[jax.ShapeDtypeStruct((n_b, t_len, 2 * 512), BF16),
                   jax.ShapeDtypeStruct((n_c, n_b * HEADS, HEAD_DIM, HEAD_DIM), F32),
                   jax.ShapeDtypeStruct((n_c, n_b * HEADS, CHUNK, CHUNK), F32)]
        + (list(rider.out_shapes) if rider else []),
        scratch_shapes=[pltpu.VMEM((n_b * HEADS, HEAD_DIM, HEAD_DIM), F32)] + r_sems,
        compiler_params=_cparams(("arbitrary",)),
    )(qkv3, qkv3, qkv3, proj3, proj3, a_log, dt_bias, gn, *r_inputs)
    return o, (hist, ainv), rider_outs


SCAN_BWD_W = OFF_Z + 512


def _scan_bwd(qkv, proj, dn_params, gla_params, hist_dn, hist_gla, d_o, n_b, t_len, rider=None):
    n_c = t_len // CHUNK
    rev = functools.partial(_chunk_spec, n_b, n_c=n_c, reverse=True)
    r_inputs, r_in_specs, r_out_specs, r_sems, split = _with_rider(rider, 19, 8, 2)
    (hist, ainv), do_gla_sl = hist_dn, [slice(512 + sl.start, 512 + sl.stop) for sl in DN_SL]

    def body(*refs):
        (q_ref, k_ref, v_ref, z_ref, sm_ref, gq_ref, gk_ref, gv_ref, gg_ref,
         al_ref, dt_ref, dgn_in_ref, w2_ref, b2_ref, ggn_in_ref, hist_ref, ainv_ref, ghist_ref, do_ref,
         dqkv_ref, dp_ref, dal_ref, ddt_ref, dgn_ref, dw2_ref, db2_ref, dggn_ref, ds_ref, gds_ref), parts = split(refs)
        ride_first, ride_last = _ride(rider, parts, n_c)
        if rider is not None:
            ride_first()
        first = pl.program_id(0) == 0

        @pl.when(first)
        def _():
            ds_ref[...] = jnp.zeros_like(ds_ref)
            gds_ref[...] = jnp.zeros_like(gds_ref)

        small = _per_chain(sm_ref, n_b)
        chains = lambda *a: _dn_chains(*a, a_saved=ainv_ref[...])[:2]
        _, pull = jax.vjp(chains, *(_stack_chains(r, n_b, DN_SL) for r in (q_ref, k_ref, v_ref, z_ref)),
                          small, hist_ref[...], al_ref[...], dt_ref[...], dgn_in_ref[...])
        dq, dk, dv, dz, dsm_dn, ds_in, dal, ddt, dgn = pull((_stack_chains(do_ref, n_b, DN_SL), ds_ref[...]))
        _, gpull = jax.vjp(_gla_chains, _stack_chains(gq_ref, n_b, GLA_KSL), _stack_chains(gk_ref, n_b, GLA_KSL),
                           _stack_chains(gv_ref, n_b, DN_SL), _stack_chains(gg_ref, n_b, DN_SL),
                           small, ghist_ref[...], *_gate_weights(w2_ref, b2_ref, n_b), ggn_in_ref[...])
        gq, gk, gv, gg, dsm_gla, gds_in, dw2, db2, dggn = gpull((_stack_chains(do_ref, n_b, do_gla_sl), gds_ref[...]))

        _unstack_chains(dqkv_ref, dq, n_b, DN_SL)
        _unstack_chains(dqkv_ref, dk, n_b, DN_SL, offset=512)
        _unstack_chains(dqkv_ref, dv, n_b, DN_SL, offset=1024)
        _unstack_chains(dp_ref, dz, n_b, DN_SL, offset=OFF_Z)
        _unstack_chains(dp_ref, gq, n_b, GLA_KSL, offset=OFF_GQ)
        _unstack_chains(dp_ref, gk, n_b, GLA_KSL, offset=OFF_GK)
        _unstack_chains(dp_ref, gv, n_b, DN_SL, offset=OFF_GV)
        _unstack_chains(dp_ref, gg, n_b, DN_SL, offset=OFF_GG)
        ds_ref[...] = ds_in
        gds_ref[...] = gds_in
        for b, (s_dn, s_gla) in enumerate(zip(_sum_heads(dsm_dn, n_b), _sum_heads(dsm_gla, n_b))):
            dp_ref[b, :, OFF_SMALL:OFF_SMALL + 128] = (s_dn + s_gla).astype(BF16)
            dp_ref[b, :, OFF_SMALL + 128:GLA_W] = jnp.zeros((CHUNK, GLA_W - OFF_SMALL - 128), BF16)
        _acc(dal_ref, dal, first)
        _acc(ddt_ref, ddt, first)
        _acc(dgn_ref, dgn, first)
        for h, ks in enumerate(GLA_KSL):
            _acc(dw2_ref, sum(dw2[b * HEADS + h] for b in range(n_b)), first, at=(slice(None), ks))
            _acc(db2_ref, sum(db2[b * HEADS + h] for b in range(n_b)), first, at=(slice(None), ks))
        _acc(dggn_ref, dggn, first)
        if rider is not None:
            ride_last()

    qkv3, proj3, do3 = (a.reshape(n_b, t_len, -1) for a in (qkv, proj, d_o))
    vec = jax.ShapeDtypeStruct((1, 128), F32)
    dqkv, d_proj, dal, ddt, dgn, dw2, db2, dggn, *rider_outs = pl.pallas_call(
        body, name="scan_bwd", grid=(n_c,),
        in_specs=[rev(512, 0), rev(512, 1), rev(512, 2), rev(512, OFF_Z // 512), rev(128, OFF_SMALL // 128),
                  rev(256, OFF_GQ // 256), rev(256, OFF_GK // 256), rev(512, OFF_GV // 512), rev(512, OFF_GG // 512),
                  _const_spec((1, 128)), _const_spec((1, 128)), _const_spec((1, 128)),
                  _const_spec((128, 256)), _const_spec((1, 256)), _const_spec((1, 128)),
                  _hist_spec(n_b, HEAD_DIM, n_c, reverse=True), _ainv_spec(n_b, n_c, reverse=True),
                  _hist_spec(n_b, GLA_KEY, n_c, reverse=True), rev(2 * 512, 0)] + r_in_specs,
        out_specs=[rev(1536, 0), rev(SCAN_BWD_W, 0), _const_spec((1, 128)), _const_spec((1, 128)),
                   _const_spec((1, 128)), _const_spec((128, 256)), _const_spec((1, 256)), _const_spec((1, 128))]
        + r_out_specs,
        out_shape=[jax.ShapeDtypeStruct((n_b, t_len, 1536), F32), jax.ShapeDtypeStruct((n_b, t_len, PROJ_W), BF16),
                   vec, vec, vec, jax.ShapeDtypeStruct((128, 256), F32), jax.ShapeDtypeStruct((1, 256), F32), vec]
        + (list(rider.out_shapes) if rider else []),
        scratch_shapes=[pltpu.VMEM((n_b * HEADS, HEAD_DIM, HEAD_DIM), F32),
                        pltpu.VMEM((n_b * HEADS, GLA_KEY, HEAD_DIM), F32)] + r_sems,
        compiler_params=_cparams(("arbitrary",)),
    )(qkv3, qkv3, qkv3, proj3, proj3, proj3, proj3, proj3, proj3, *dn_params, *gla_params, hist, ainv, hist_gla, do3,
      *r_inputs)
    return dqkv.reshape(n_b * t_len, 1536), d_proj, (dal, ddt, dgn), (dw2, db2, dggn), rider_outs


def _gla_scan_fwd(proj, w2, b2, gn, o_mix, n_b, t_len):
    n_c = t_len // CHUNK
    spec = functools.partial(_chunk_spec, n_b, n_c=n_c)

    def body(q_ref, k_ref, v_ref, g_ref, sm_ref, w2_ref, b2_ref, gn_ref, _, o_ref, hist_ref, s_ref):
        @pl.when(pl.program_id(0) == 0)
        def _():
            s_ref[...] = jnp.zeros_like(s_ref)

        s_in = s_ref[...]
        hist_ref[...] = s_in
        og, s_out = _gla_chains(_stack_chains(q_ref, n_b, GLA_KSL), _stack_chains(k_ref, n_b, GLA_KSL),
                                _stack_chains(v_ref, n_b, DN_SL), _stack_chains(g_ref, n_b, DN_SL),
                                _per_chain(sm_ref, n_b), s_in, *_gate_weights(w2_ref, b2_ref, n_b), gn_ref[...])
        _unstack_chains(o_ref, og, n_b, DN_SL)
        s_ref[...] = s_out

    proj3 = proj.reshape(n_b, t_len, -1)
    o, hist = pl.pallas_call(
        body, name="gla_scan_fwd", grid=(n_c,),
        in_specs=[spec(256, OFF_GQ // 256), spec(256, OFF_GK // 256), spec(512, OFF_GV // 512),
                  spec(512, OFF_GG // 512), spec(128, OFF_SMALL // 128),
                  _const_spec((128, 256)), _const_spec((1, 256)), _const_spec((1, 128)),
                  pl.BlockSpec(memory_space=pl.ANY)],
        out_specs=[spec(512, 1), _hist_spec(n_b, GLA_KEY, n_c)],
        out_shape=[jax.ShapeDtypeStruct(o_mix.shape, BF16),
                   jax.ShapeDtypeStruct((n_c, n_b * HEADS, GLA_KEY, HEAD_DIM), F32)],
        input_output_aliases={8: 0},
        scratch_shapes=[pltpu.VMEM((n_b * HEADS, GLA_KEY, HEAD_DIM), F32)],
        compiler_params=_cparams(("arbitrary",)),
    )(proj3, proj3, proj3, proj3, proj3, w2, b2, gn, o_mix)
    return o.reshape(n_b * t_len, 2 * 512), hist


W_IN_RUNS = ((0, 256, GLA_W), (256, 1536, OFF_Z + 512), (1536, 2048, OFF_Z), (2048, 2056, OFF_SMALL),
             (2056, 3592, 0), (3592, 3608, OFF_SMALL + 8))
W_IN_ROWS = 256


def _w_in_pieces(cols_per_chip):
    out = []
    for first, last, start in W_IN_RUNS:
        for j in range(N_CHIPS):
            a, b = max(first, cols_per_chip * j), min(last, cols_per_chip * (j + 1))
            if a < b:
                out.append((j, a - cols_per_chip * j, b - cols_per_chip * j, start + a - first))
    return out


def _w_in_to_padded(w4):
    _, n_r, n_c = w4.shape

    def body(i_ref, o_ref):
        o_ref[...] = jnp.zeros_like(o_ref)
        for j, a, b, p in _w_in_pieces(n_c):
            o_ref[:, p:p + b - a] = i_ref[j, :, a:b]

    return pl.pallas_call(
        body, name="w_in_to_padded", grid=(n_r // W_IN_ROWS,),
        in_specs=[pl.BlockSpec((N_CHIPS, W_IN_ROWS, n_c), lambda i: (0, i, 0))],
        out_specs=pl.BlockSpec((W_IN_ROWS, PROJ_W), lambda i: (i, 0)),
        out_shape=jax.ShapeDtypeStruct((n_r, PROJ_W), w4.dtype), compiler_params=_cparams(("parallel",)),
    )(w4)


def _w_in_to_chips(g, n_c):
    n_r = g.shape[0]

    def body(i_ref, o_ref):
        for j, a, b, p in _w_in_pieces(n_c):
            o_ref[j, :, a:b] = i_ref[:, p:p + b - a]

    return pl.pallas_call(
        body, name="w_in_to_chips", grid=(n_r // W_IN_ROWS,),
        in_specs=[pl.BlockSpec((W_IN_ROWS, PROJ_W), lambda i: (i, 0))],
        out_specs=pl.BlockSpec((N_CHIPS, W_IN_ROWS, n_c), lambda i: (0, i, 0)),
        out_shape=jax.ShapeDtypeStruct((N_CHIPS, n_r, n_c), g.dtype), compiler_params=_cparams(("parallel",)),
    )(g)


def _lane_vec(v, offset=0):
    return jnp.zeros((1, 128), F32).at[0, offset:offset + v.shape[0]].set(v)


def _local_step(x, tgt, mod, p, n_b, t_len, comm=None):
    row1 = lambda v: v.reshape(1, -1)
    a_log, dt_bias = _lane_vec(p["dn_a_log"]), _lane_vec(p["dn_dt_bias"])
    dn_gn, gla_gn = row1(p["dn_norm_g"]), row1(p["gla_norm_g"])
    w2 = jnp.zeros((128, 256), F32).at[8:8 + GATE_RANK].set(p["gla_w_gate2"])
    b2 = row1(p["gla_b_gate"])
    ln0_g, ln0_b, ln1_g, ln1_b, ln2_g, ln2_b = (row1(p[k]) for k in ("ln0_g", "ln0_b", "ln1_g", "ln1_b", "ln2_g", "ln2_b"))
    conv_b = row1(p["ffn_conv_b"])

    x0, h1 = _ln0_fwd(x, ln0_g, ln0_b, mod, n_b, t_len)
    if comm:
        proj, landed_proj = _mm(h1, p["w_in_p"], name="mm_proj", rider=comm.proj_rider())
    else:
        proj = _mm(h1, p["w_in_p"], name="mm_proj")
    qkv = _dn_pre_fwd(proj, p["dn_conv"], n_b, t_len)
    o_half, hist_dn, landed_scan = _dn_scan_fwd(qkv, proj, a_log, dt_bias, dn_gn, n_b, t_len,
                                                rider=comm.scan_rider() if comm else None)
    if comm:
        p = {**p, **comm.weights_from(landed_proj, landed_scan)}
    o_mix, hist_gla = _gla_scan_fwd(proj, w2, b2, gla_gn, o_half, n_b, t_len)
    y = _mm(o_mix, p["w_o"], name="mm_wo")
    x1, h2 = _ln1_fwd(x0, y, ln1_g, ln1_b, mod, n_b, t_len)
    up = _mm(h2, p["w_up"], name="mm_up")
    act = _ffn_act_fwd(up, p["ffn_conv"], conv_b, n_b, t_len)
    y2 = _mm(act, p["w_down"], name="mm_down")

    loss, dx1, dy2, g_ln2_g, g_ln2_b, dgt_f = _ln2_loss_bwd(x1, y2, ln2_g, ln2_b, mod, tgt, n_b, t_len)
    g_w_down = _mm(act, dy2, ta=True, name="mm_g_down")
    d_act = _mm(dy2, p["w_down"], tb=True, name="mm_d_act")
    d_up, g_ffn_conv, g_conv_b = _ffn_act_bwd(up, p["ffn_conv"], conv_b, d_act, n_b, t_len)
    g_w_up = _mm(h2, d_up, ta=True, out_slabs=N_CHIPS, name="mm_g_up")
    if comm:
        dh2, from_sibling = _mm(d_up, p["w_up"], tb=True, name="mm_d_h2", rider=comm.ffn_pair_rider(g_w_up, g_w_down))
    else:
        dh2 = _mm(d_up, p["w_up"], tb=True, name="mm_d_h2")
    dx0, dy, g_ln1_g, g_ln1_b, dmod_1 = _ln1_bwd(x0, y, ln1_g, ln1_b, mod, dx1, dh2, n_b, t_len)
    g_w_o = _mm(o_mix, dy, ta=True, name="mm_g_wo")
    if comm:
        d_o, wo_from_sibling = _mm(dy, p["w_o"], tb=True, name="mm_d_o", rider=comm.wo_pair_rider(g_w_o))
    else:
        d_o = _mm(dy, p["w_o"], tb=True, name="mm_d_o")
    dqkv, d_proj, (g_a_log, g_dt_bias, g_dn_gn), (g_w2, g_b2, g_gla_gn), scan_from_chips = _scan_bwd(
        qkv, proj, (a_log, dt_bias, dn_gn), (w2, b2, gla_gn), hist_dn, hist_gla, d_o, n_b, t_len,
        rider=comm.scan_chips_rider(from_sibling, wo_from_sibling) if comm else None)
    d_proj, g_dn_conv = _dn_pre_bwd(proj, p["dn_conv"], dqkv, d_proj.reshape(n_b * t_len, PROJ_W), n_b, t_len)
    g_w_in_p = _mm(h1, d_proj, ta=True, name="mm_g_win")
    if comm:
        dh1, tail_from_chips = _mm(d_proj, p["w_in_p"], tb=True, name="mm_d_h1",
                                   rider=comm.tail_chips_rider(g_w_in_p))
        from_chips = (scan_from_chips, tail_from_chips)
    else:
        dh1, from_chips = _mm(d_proj, p["w_in_p"], tb=True, name="mm_d_h1"), None
    grad_x, g_ln0_g, g_ln0_b, dmod_0 = _ln0_bwd(x, ln0_g, ln0_b, mod, dx0, dh1, n_b, t_len)

    dmod = jnp.concatenate([dmod_0, dmod_1[:, 0:1], dmod_1[:, 1:3], dgt_f], axis=1)
    grads = {
        "ln0_g": g_ln0_g[0], "ln0_b": g_ln0_b[0], "w_in_p": g_w_in_p, "dn_conv": g_dn_conv,
        "dn_a_log": g_a_log[0, 0:HEADS], "dn_dt_bias": g_dt_bias[0, 0:HEADS], "dn_norm_g": g_dn_gn[0],
        "gla_w_gate2": g_w2[8:8 + GATE_RANK], "gla_b_gate": g_b2[0], "gla_norm_g": g_gla_gn[0],
        "w_o": g_w_o, "ln1_g": g_ln1_g[0], "ln1_b": g_ln1_b[0], "w_up": g_w_up,
        "ffn_conv": jnp.concatenate([g_ffn_conv[0], g_ffn_conv[1]], axis=1),
        "ffn_conv_b": jnp.concatenate([g_conv_b[0, 0], g_conv_b[1, 0]]), "w_down": g_w_down,
        "ln2_g": g_ln2_g[0], "ln2_b": g_ln2_b[0],
    }
    return loss, grad_x, grads, dmod, from_chips


def _col_sum(a):
    def body(a_ref, o_ref):
        o_ref[...] = jnp.sum(a_ref[...], 0, keepdims=True)

    return pl.pallas_call(body, name="col_sum", out_shape=jax.ShapeDtypeStruct((1, a.shape[1]), F32))(a)


def _adamw_math(w, grad, m, v):
    new_m = ADAM_B1 * m + (1.0 - ADAM_B1) * grad
    new_v = ADAM_B2 * v + (1.0 - ADAM_B2) * (grad * grad)
    m_hat = new_m / (1.0 - ADAM_B1 ** ADAM_STEP)
    v_hat = new_v / (1.0 - ADAM_B2 ** ADAM_STEP)
    return -ADAM_LR * (m_hat / (jnp.sqrt(v_hat) + ADAM_EPS) + ADAM_WD * w), new_m, new_v


def _adamw_many(ws, gs, ms, vs):
    n = len(ws)

    def body(*refs):
        for i in range(n):
            w_ref, g_ref, m_ref, v_ref = (refs[k * n + i] for k in range(4))
            d_ref, nm_ref, nv_ref = (refs[(4 + k) * n + i] for k in range(3))
            d_ref[...], nm_ref[...], nv_ref[...] = _adamw_math(w_ref[...], g_ref[...], m_ref[...], v_ref[...])

    outs = pl.pallas_call(
        body, name="adamw_small", out_shape=[jax.ShapeDtypeStruct(w.shape, F32) for w in ws] * 3,
    )(*ws, *gs, *ms, *vs)
    return outs[:n], outs[n:2 * n], outs[2 * n:]


def _adamw(w, g, m, v, name):
    n_r, n_c = w.shape
    if n_r % 8 == 0:
        tr = _pick(n_r, (256, 64, 32, 16, 8))
        grid, blk = (n_r // tr,), pl.BlockSpec((tr, n_c), lambda i: (i, 0))
    else:
        tc = _pick(n_c, (256, 128))
        grid, blk = (n_c // tc,), pl.BlockSpec((n_r, tc), lambda i: (0, i))

    def body(w_ref, g_ref, m_ref, v_ref, d_ref, nm_ref, nv_ref):
        d_ref[...], nm_ref[...], nv_ref[...] = _adamw_math(w_ref[...], g_ref[...], m_ref[...], v_ref[...])

    out = jax.ShapeDtypeStruct(w.shape, F32)
    return pl.pallas_call(
        body, name=name, grid=grid, in_specs=[blk] * 4, out_specs=[blk] * 3, out_shape=[out] * 3,
        compiler_params=_cparams(("parallel",)),
    )(w, g, m, v)


HBM_SPEC = pl.BlockSpec(memory_space=pltpu.HBM)
VMEM_SPEC = pl.BlockSpec(memory_space=pltpu.VMEM)
CHIP_FLIPS = ((1, 0), (0, 1), (1, 1))


def _place():
    return lax.axis_index("x"), lax.axis_index("y"), lax.axis_index("c")


def _flip(v, f):
    return 1 - v if f else v


def _all_gather8(slab, name):
    n_r, n_w = slab.shape

    def body(x_ref, o_ref, s_ref, send_sems, recv_sems, local_sem):
        x, y, c = _place()
        me = 4 * x + 2 * y + c
        mine = pltpu.make_async_copy(x_ref, o_ref.at[me], local_sem)
        mine.start()
        peers = [(_flip(x, k & 4), _flip(y, k & 2), _flip(c, k & 1)) for k in range(1, N_DEV)]
        sends = []
        for k, peer in enumerate(peers):
            cp = pltpu.make_async_remote_copy(src_ref=x_ref, dst_ref=o_ref.at[me], send_sem=send_sems.at[k],
                                              recv_sem=recv_sems.at[k], device_id=peer, device_id_type=MESH)
            cp.start()
            sends.append(cp)
        for k, (px, py, pc) in enumerate(peers):
            pltpu.make_async_remote_copy(src_ref=x_ref, dst_ref=o_ref.at[4 * px + 2 * py + pc],
                                         send_sem=send_sems.at[k], recv_sem=recv_sems.at[k],
                                         device_id=(px, py, pc), device_id_type=MESH).wait_recv()
        for cp in sends:
            cp.wait_send()
        mine.wait()
        total = o_ref[0]
        for d in range(1, N_DEV):
            total = total + o_ref[d]
        s_ref[...] = total

    return pl.pallas_call(
        body, name=name, in_specs=[VMEM_SPEC], out_specs=[VMEM_SPEC, VMEM_SPEC],
        out_shape=[jax.ShapeDtypeStruct((N_DEV, n_r, n_w), F32), jax.ShapeDtypeStruct((n_r, n_w), F32)],
        scratch_shapes=[pltpu.SemaphoreType.DMA((N_DEV - 1,)), pltpu.SemaphoreType.DMA((N_DEV - 1,)),
                        pltpu.SemaphoreType.DMA],
    )(slab)


SEQ_ROWS = 8


def _prologue(slab, w_ada_shard, b_shard, rider):
    n_r, n_w = slab.shape
    n_col = w_ada_shard.shape[1]
    r_inputs, r_in_specs, r_out_specs, r_sems, split = _with_rider(rider, 3, 3, 6)

    def body(*refs):
        (x_ref, w_ref, b_ref, g_ref, cond_ref, modr_ref, modp_ref, s1, r1, s2, r2, lsem), parts = split(refs)
        rider.first(*parts)
        x, y, c = _place()
        me = 4 * x + 2 * y + c
        peers = [(_flip(x, k & 4), _flip(y, k & 2), _flip(c, k & 1)) for k in range(1, N_DEV)]
        ids = [4 * px + 2 * py + pc for px, py, pc in peers]

        def exchange(src_of, dst, send_sems, recv_sems, own_sem):
            mine = pltpu.make_async_copy(src_of(me), dst.at[me], own_sem)
            mine.start()
            sends = [pltpu.make_async_remote_copy(src_ref=src_of(ids[k]), dst_ref=dst.at[me], send_sem=send_sems.at[k],
                                                  recv_sem=recv_sems.at[k], device_id=peers[k], device_id_type=MESH)
                     for k in range(N_DEV - 1)]
            for cp in sends:
                cp.start()
            for k in range(N_DEV - 1):
                pltpu.make_async_remote_copy(src_ref=src_of(ids[k]), dst_ref=dst.at[ids[k]], send_sem=send_sems.at[k],
                                             recv_sem=recv_sems.at[k], device_id=peers[k],
                                             device_id_type=MESH).wait_recv()
            for cp in sends:
                cp.wait_send()
            mine.wait()

        exchange(lambda d: x_ref, g_ref, s1, r1, lsem.at[0])
        cond = _silu(g_ref[:, 0:SEQ_ROWS, :].reshape(N_DEV * SEQ_ROWS, n_w))
        cond_ref[...] = cond
        modp_ref[...] = jnp.dot(cond.astype(BF16), w_ref[...].astype(BF16), preferred_element_type=F32) + b_ref[...]
        exchange(lambda d: modp_ref.at[pl.ds(pl.multiple_of(d * SEQ_ROWS, SEQ_ROWS), SEQ_ROWS)], modr_ref, s2, r2,
                 lsem.at[1])
        rider.last(*parts)

    sem7 = pltpu.SemaphoreType.DMA((N_DEV - 1,))
    gathered, cond, mod_recv, *rider_outs = pl.pallas_call(
        body, name="prologue", in_specs=[VMEM_SPEC] * 3 + r_in_specs, out_specs=[VMEM_SPEC] * 3 + r_out_specs,
        out_shape=[jax.ShapeDtypeStruct((N_DEV, n_r, n_w), F32), jax.ShapeDtypeStruct((N_DEV * SEQ_ROWS, n_w), F32),
                   jax.ShapeDtypeStruct((N_DEV, SEQ_ROWS, n_col), F32)] + list(rider.out_shapes),
        scratch_shapes=[pltpu.VMEM((N_DEV * SEQ_ROWS, n_col), F32), sem7, sem7, sem7, sem7,
                        pltpu.SemaphoreType.DMA((2,))] + r_sems,
        compiler_params=pltpu.CompilerParams(vmem_limit_bytes=VMEM_LIMIT),
    )(slab, w_ada_shard, b_shard, *r_inputs)
    return gathered, cond, mod_recv, rider_outs


def _gather_rider(shards):
    n_a = len(shards)

    def plan(ins, outs, sems):
        send_sems, recv_sems = sems
        x, y, c = _place()
        chips = [(_flip(x, fx), _flip(y, fy)) for fx, fy in CHIP_FLIPS]

        def copy(k, slot, chip_of_block, half, to, src=None):
            dst = outs[k].at[chip_of_block, half]
            return pltpu.make_async_remote_copy(src_ref=dst if src is None else src, dst_ref=dst,
                                                send_sem=send_sems.at[k * 6 + slot], recv_sem=recv_sems.at[k * 6 + slot],
                                                device_id=to, device_id_type=MESH)

        first = [copy(k, r, 2 * x + y, c, (*chips[r], c), src=ins[k].at[c]) for k in range(n_a) for r in range(3)]
        return copy, chips, first, (x, y, c)

    def first_step(ins, outs, sems):
        for cp in plan(ins, outs, sems)[2]:
            cp.start()

    def last_step(ins, outs, sems):
        copy, chips, first, (x, y, c) = plan(ins, outs, sems)
        passed = []
        for k in range(n_a):
            for r, (px, py) in enumerate(chips):
                copy(k, r, 2 * px + py, c, (x, y, c)).wait_recv()
                fwd = copy(k, 3 + r, 2 * px + py, c, (x, y, 1 - c))
                fwd.start()
                passed.append(fwd)
        for k in range(n_a):
            for r, (px, py) in enumerate(chips):
                copy(k, 3 + r, 2 * px + py, 1 - c, (x, y, c)).wait_recv()
        for cp in first + passed:
            cp.wait_send()

    return Rider(shards, [jax.ShapeDtypeStruct((N_CHIPS,) + s.shape, s.dtype) for s in shards],
                 [pltpu.SemaphoreType.DMA((6 * n_a,)), pltpu.SemaphoreType.DMA((6 * n_a,))], first_step, last_step)


def _place_own(gathered, shard, chip, name):
    _, _, n_h, n_c = gathered.shape
    th = _pick(n_h, (256, 176, 128))

    def body(sel_ref, s_ref, _, o_ref):
        o_ref[...] = s_ref[...]

    grid_spec = pltpu.PrefetchScalarGridSpec(
        num_scalar_prefetch=1, grid=(2, n_h // th),
        in_specs=[pl.BlockSpec((None, th, n_c), lambda hf, i, sel: (hf, i, 0)), pl.BlockSpec(memory_space=pl.ANY)],
        out_specs=pl.BlockSpec((None, None, th, n_c), lambda hf, i, sel: (sel[0], hf, i, 0)))
    return pl.pallas_call(
        body, name=name, grid_spec=grid_spec, out_shape=jax.ShapeDtypeStruct(gathered.shape, gathered.dtype),
        input_output_aliases={2: 0}, compiler_params=_cparams(("parallel", "parallel")),
    )(chip.reshape(1), shard, gathered)


def _pair_rider(parts):
    n_a = len(parts)

    def plan(ins, outs, sems):
        send_sems, recv_sems = sems
        x, y, c = _place()
        return [pltpu.make_async_remote_copy(src_ref=ins[k].at[:, 1 - c], dst_ref=outs[k], send_sem=send_sems.at[k],
                                             recv_sem=recv_sems.at[k], device_id=(x, y, 1 - c), device_id_type=MESH)
                for k in range(n_a)]

    def first_step(ins, outs, sems):
        for cp in plan(ins, outs, sems):
            cp.start()

    def last_step(ins, outs, sems):
        for cp in plan(ins, outs, sems):
            cp.wait()

    return Rider(parts, [jax.ShapeDtypeStruct((N_CHIPS,) + p.shape[2:], F32) for p in parts],
                 [pltpu.SemaphoreType.DMA((n_a,)), pltpu.SemaphoreType.DMA((n_a,))], first_step, last_step)


def _alone(rider, name):
    n_a = len(rider.inputs)

    def body(*refs):
        parts = (refs[:n_a], refs[n_a:2 * n_a], refs[2 * n_a:])
        rider.first(*parts)
        rider.last(*parts)

    return pl.pallas_call(
        body, name=name, in_specs=[HBM_SPEC] * n_a, out_specs=[HBM_SPEC] * n_a,
        out_shape=rider.out_shapes, scratch_shapes=rider.sems,
    )(*rider.inputs)


def _chips_rider(sums):
    n_a = len(sums)

    def plan(ins, outs, sems):
        send_sems, recv_sems = sems
        x, y, c = _place()
        cps = []
        for k in range(n_a):
            for r, (fx, fy) in enumerate(CHIP_FLIPS):
                px, py = _flip(x, fx), _flip(y, fy)
                cps.append(pltpu.make_async_remote_copy(
                    src_ref=ins[k].at[2 * px + py], dst_ref=outs[k].at[r], send_sem=send_sems.at[3 * k + r],
                    recv_sem=recv_sems.at[3 * k + r], device_id=(px, py, c), device_id_type=MESH))
        return cps

    def first_step(ins, outs, sems):
        for cp in plan(ins, outs, sems):
            cp.start()

    def last_step(ins, outs, sems):
        for cp in plan(ins, outs, sems):
            cp.wait()

    return Rider(sums, [jax.ShapeDtypeStruct((3,) + s.shape[1:], s.dtype) for s in sums],
                 [pltpu.SemaphoreType.DMA((3 * n_a,)), pltpu.SemaphoreType.DMA((3 * n_a,))], first_step, last_step)


def _rs_share(bufs):
    n_a = len(bufs)

    def body(*refs):
        ins, outs = refs[:n_a], refs[n_a:2 * n_a]
        send_sems, recv_sems = refs[2 * n_a:]
        x, y, c = _place()
        sends = [pltpu.make_async_remote_copy(src_ref=ins[k].at[c], dst_ref=outs[k].at[c], send_sem=send_sems.at[k],
                                              recv_sem=recv_sems.at[k], device_id=(x, y, 1 - c), device_id_type=MESH)
                 for k in range(n_a)]
        for cp in sends:
            cp.start()
        for k in range(n_a):
            pltpu.make_async_remote_copy(src_ref=ins[k].at[c], dst_ref=outs[k].at[1 - c], send_sem=send_sems.at[k],
                                         recv_sem=recv_sems.at[k], device_id=(x, y, 1 - c),
                                         device_id_type=MESH).wait_recv()
        for cp in sends:
            cp.wait_send()

    return pl.pallas_call(
        body, name="rs_share", in_specs=[HBM_SPEC] * n_a, out_specs=[HBM_SPEC] * n_a,
        out_shape=[jax.ShapeDtypeStruct(s.shape, F32) for s in bufs],
        input_output_aliases={k: k for k in range(n_a)},
        scratch_shapes=[pltpu.SemaphoreType.DMA((n_a,)), pltpu.SemaphoreType.DMA((n_a,))],
    )(*bufs)


def _pair_add(part, recv, core, name):
    _, _, n_h, n_c = part.shape
    th = _pick(n_h, (256, 176, 128))

    def body(sel_ref, p_ref, r_ref, o_ref):
        o_ref[...] = (p_ref[...] + r_ref[...]).astype(BF16)

    grid_spec = pltpu.PrefetchScalarGridSpec(
        num_scalar_prefetch=1, grid=(N_CHIPS, n_h // th),
        in_specs=[pl.BlockSpec((None, None, th, n_c), lambda j, i, sel: (j, sel[0], i, 0)),
                  pl.BlockSpec((None, th, n_c), lambda j, i, sel: (j, i, 0))],
        out_specs=pl.BlockSpec((None, th, n_c), lambda j, i, sel: (j, i, 0)))
    return pl.pallas_call(
        body, name=name, grid_spec=grid_spec, out_shape=jax.ShapeDtypeStruct(recv.shape, BF16),
        compiler_params=_cparams(("parallel", "parallel")),
    )(core.reshape(1), part, recv)


def _chip_add(sums, recv, chip, core, name):
    _, n_h, n_c = sums.shape
    th = _pick(n_h, (256, 176, 128))

    def body(sel_ref, s_ref, r_ref, o_ref):
        total = s_ref[...].astype(F32)
        for r in range(3):
            total = total + r_ref[r].astype(F32)
        o_ref[...] = total

    grid_spec = pltpu.PrefetchScalarGridSpec(
        num_scalar_prefetch=1, grid=(n_h // th,),
        in_specs=[pl.BlockSpec((None, th, n_c), lambda i, sel: (sel[0], i, 0)),
                  pl.BlockSpec((3, th, n_c), lambda i, sel: (0, i, 0))],
        out_specs=pl.BlockSpec((None, th, n_c), lambda i, sel: (sel[1], i, 0)))
    return pl.pallas_call(
        body, name=name, grid_spec=grid_spec, out_shape=jax.ShapeDtypeStruct((2, n_h, n_c), F32),
        compiler_params=_cparams(("parallel",)),
    )(jnp.stack([chip, core]), sums, recv)


def _row_halves(a):
    return a.reshape(N_CHIPS, 2, -1, a.shape[-1])


class StepComm:
    REST = ("w_o", "w_up", "w_down")

    def __init__(self, core, chip, rest_shards, in_cols):
        self.core, self.chip, self.shards, self.in_cols = core, chip, rest_shards, in_cols

    def proj_rider(self):
        return _gather_rider([self.shards[0], self.shards[2]])

    def scan_rider(self):
        return _gather_rider([self.shards[1]])

    def weights_from(self, landed_proj, landed_scan):
        landed = (landed_proj[0], landed_scan[0], landed_proj[1])
        g_o, g_up, g_down = (_place_own(g, s, self.chip, "place_own_" + n)
                             for g, s, n in zip(landed, self.shards, self.REST))
        return {"w_o": g_o.reshape(-1, D_MODEL), "w_up": g_up.reshape(N_CHIPS, -1, g_up.shape[-1]),
                "w_down": g_down.reshape(-1, D_MODEL)}

    def _add_pairs(self, parts, from_sibling, names):
        return [_pair_add(p, r, self.core, "pair_add_" + n) for p, r, n in zip(parts, from_sibling, names)]

    def ffn_pair_rider(self, g_w_up, g_w_down):
        self.ffn_parts = [_row_halves(g_w_up), _row_halves(g_w_down)]
        return _pair_rider(self.ffn_parts)

    def wo_pair_rider(self, g_w_o):
        self.wo_parts = [_row_halves(g_w_o)]
        return _pair_rider(self.wo_parts)

    def scan_chips_rider(self, ffn_from_sibling, wo_from_sibling):
        self.scan_sums = self._add_pairs(self.wo_parts + self.ffn_parts, list(wo_from_sibling) + list(ffn_from_sibling),
                                         ("w_o", "w_up", "w_down"))
        return _chips_rider(self.scan_sums)

    def tail_chips_rider(self, g_w_in_p):
        parts = [_row_halves(_w_in_to_chips(g_w_in_p, self.in_cols))]
        self.tail_sums = self._add_pairs(parts, _alone(_pair_rider(parts), "rs_pair_tail"), ("w_in",))
        return _chips_rider(self.tail_sums)

    def finish(self, scan_from_chips, tail_from_chips):
        halves = [_chip_add(s, r, self.chip, self.core, "chip_add_" + n)
                  for s, r, n in zip(self.tail_sums + self.scan_sums, list(tail_from_chips) + list(scan_from_chips),
                                     ("w_in", "w_o", "w_up", "w_down"))]
        return [f.reshape(-1, f.shape[-1]) for f in _rs_share(halves)]


SLAB_W = 1024


def _pack(arrays, rows):
    flat = jnp.concatenate([a.reshape(-1).astype(F32) for a in arrays])
    return jnp.pad(flat, (0, rows * SLAB_W - flat.shape[0])).reshape(rows, SLAB_W)


def _unpack(flat, shapes):
    out, off = [], 0
    for s in shapes:
        n = 1
        for d in s:
            n *= d
        out.append(flat[off:off + n].reshape(s))
        off += n
    return out


def _rows_for(arrays_or_shapes):
    n = 0
    for a in arrays_or_shapes:
        s = a if isinstance(a, tuple) else a.shape
        k = 1
        for d in s:
            k *= d
        n += k
    return -(-n // (8 * SLAB_W)) * 8


def kernel(x, c, ln0_g, ln0_b, w_ada, b_ada, w_in, dn_conv, dn_a_log, dn_dt_bias, dn_norm_g, gla_w_gate2, gla_b_gate, gla_norm_g, w_o, ln1_g, ln1_b, ffn_w_up, ffn_conv, ffn_conv_b, ffn_w_down, ln2_g, ln2_b, loss_target, m_ln0_g, m_ln0_b, m_w_ada, m_b_ada, m_w_in, m_dn_conv, m_dn_a_log, m_dn_dt_bias, m_dn_norm_g, m_gla_w_gate2, m_gla_b_gate, m_gla_norm_g, m_w_o, m_ln1_g, m_ln1_b, m_ffn_w_up, m_ffn_conv, m_ffn_conv_b, m_ffn_w_down, m_ln2_g, m_ln2_b, v_ln0_g, v_ln0_b, v_w_ada, v_b_ada, v_w_in, v_dn_conv, v_dn_a_log, v_dn_dt_bias, v_dn_norm_g, v_gla_w_gate2, v_gla_b_gate, v_gla_norm_g, v_w_o, v_ln1_g, v_ln1_b, v_ffn_w_up, v_ffn_conv, v_ffn_conv_b, v_ffn_w_down, v_ln2_g, v_ln2_b):
    n_b, t_len, _ = x.shape
    xi, yi, ci = _place()
    chip = (2 * xi + yi).astype(jnp.int32)
    core = ci.astype(jnp.int32)
    n_all = N_DEV * n_b
    ada_cols = w_ada.shape[2]

    halves = lambda a: a.astype(BF16).reshape(2, a.shape[0] // 2, a.shape[1])
    w_in_halves = halves(w_in[0])
    sharded_small = [dn_conv[0], gla_w_gate2[0], ffn_conv[0]]
    slab = jnp.concatenate([_pack([c], SEQ_ROWS), _pack(sharded_small, _rows_for(sharded_small))], axis=0)
    b_ada_shard = lax.dynamic_slice(b_ada, (0, chip * ada_cols), (1, ada_cols))
    gathered, cond_pad, mod_recv, (g_in,) = _prologue(slab, w_ada[0], b_ada_shard, _gather_rider([w_in_halves]))
    g_in = _place_own(g_in, w_in_halves, chip, "place_own_w_in")
    cond_all = cond_pad.reshape(N_DEV, SEQ_ROWS, D_MODEL)[:, :n_b].reshape(n_all, D_MODEL)
    by_chip = gathered.reshape(N_DEV, -1)[0::2]
    full, off = [], SEQ_ROWS * SLAB_W
    for a in sharded_small:
        blocks = by_chip[:, off:off + a.size].reshape(N_CHIPS, *a.shape)
        full.append(blocks.transpose(1, 0, 2).reshape(a.shape[0], N_CHIPS * a.shape[1]))
        off += a.size
    dn_conv_f, gate2_f, ffn_conv_f = full
    mod = mod_recv[0::2, :n_b].transpose(1, 0, 2).reshape(n_b, 6, D_MODEL)

    comm = StepComm(core, chip, [halves(w_o[0]), halves(ffn_w_up[0]), halves(ffn_w_down[0])], w_in.shape[2])
    params = {
        "w_in_p": _w_in_to_padded(g_in.reshape(N_CHIPS, -1, g_in.shape[-1])),
        "dn_conv": dn_conv_f, "dn_a_log": dn_a_log[0], "dn_dt_bias": dn_dt_bias[0], "dn_norm_g": dn_norm_g[0],
        "gla_w_gate2": gate2_f, "gla_b_gate": gla_b_gate[0], "gla_norm_g": gla_norm_g[0],
        "ln0_g": ln0_g, "ln0_b": ln0_b, "ln1_g": ln1_g[0], "ln1_b": ln1_b[0], "ln2_g": ln2_g[0], "ln2_b": ln2_b[0],
        "ffn_conv": ffn_conv_f, "ffn_conv_b": ffn_conv_b[0],
    }

    loss_row, grad_x, gp, dmod, from_chips = _local_step(
        x.reshape(n_b * t_len, D_MODEL), loss_target.reshape(n_b * t_len, D_MODEL), mod, params, n_b, t_len, comm)
    names = ["ln0_g", "ln0_b", "w_ada", "b_ada", "w_in", "dn_conv", "dn_a_log", "dn_dt_bias", "dn_norm_g",
             "gla_w_gate2", "gla_b_gate", "gla_norm_g", "w_o", "ln1_g", "ln1_b", "ffn_w_up", "ffn_conv", "ffn_conv_b",
             "ffn_w_down", "ln2_g", "ln2_b"]
    weights = dict(zip(names, [ln0_g, ln0_b, w_ada, b_ada, w_in, dn_conv, dn_a_log, dn_dt_bias, dn_norm_g, gla_w_gate2,
                               gla_b_gate, gla_norm_g, w_o, ln1_g, ln1_b, ffn_w_up, ffn_conv, ffn_conv_b, ffn_w_down,
                               ln2_g, ln2_b]))
    m_in = dict(zip(names, [m_ln0_g, m_ln0_b, m_w_ada, m_b_ada, m_w_in, m_dn_conv, m_dn_a_log, m_dn_dt_bias,
                            m_dn_norm_g, m_gla_w_gate2, m_gla_b_gate, m_gla_norm_g, m_w_o, m_ln1_g, m_ln1_b,
                            m_ffn_w_up, m_ffn_conv, m_ffn_conv_b, m_ffn_w_down, m_ln2_g, m_ln2_b]))
    v_in = dict(zip(names, [v_ln0_g, v_ln0_b, v_w_ada, v_b_ada, v_w_in, v_dn_conv, v_dn_a_log, v_dn_dt_bias,
                            v_dn_norm_g, v_gla_w_gate2, v_gla_b_gate, v_gla_norm_g, v_w_o, v_ln1_g, v_ln1_b,
                            v_ffn_w_up, v_ffn_conv, v_ffn_conv_b, v_ffn_w_down, v_ln2_g, v_ln2_b]))
    grads, delta, new_m, new_v = {}, {}, {}, {}

    def adamw_big(n, grad):
        view = (lambda a: a.T) if n == "w_in" else (lambda a: a)
        outs = _adamw(view(weights[n][0]), view(grad), view(m_in[n][0]), view(v_in[n][0]), "adamw_" + n)
        grads[n] = grad[None]
        delta[n], new_m[n], new_v[n] = (view(a)[None] for a in outs)

    g_w_in, g_w_o, g_w_up, g_w_down = comm.finish(*from_chips)

    summed_names = ["loss", "ln0_g", "ln0_b", "dn_conv", "dn_a_log", "dn_dt_bias", "dn_norm_g", "gla_w_gate2",
                    "gla_b_gate", "gla_norm_g", "ln1_g", "ln1_b", "ffn_conv", "ffn_conv_b", "ln2_g", "ln2_b"]
    summed_parts = [loss_row[0, 0:1]] + [gp[n] for n in summed_names[1:]]
    sum_rows = _rows_for(summed_parts)
    slab = jnp.concatenate([_pack(summed_parts, sum_rows), _pack([dmod], _rows_for([dmod]))], axis=0)
    gathered, total = _all_gather8(slab, "reduce_small")
    small_g = dict(zip(summed_names, _unpack(total.reshape(-1), [a.shape for a in summed_parts])))
    loss = small_g["loss"][0]
    dmod_rows = n_b * 6 * D_MODEL // SLAB_W
    dmod_all = gathered[:, sum_rows:sum_rows + dmod_rows, :].reshape(n_all, 6 * D_MODEL)
    for n, grad in (("ffn_w_up", g_w_up), ("ffn_w_down", g_w_down), ("w_o", g_w_o), ("w_in", g_w_in)):
        adamw_big(n, grad)

    g_b_ada = _col_sum(dmod_all)
    dmod_cols = lax.dynamic_slice(dmod_all, (0, chip * ada_cols), (n_all, ada_cols))
    adamw_big("w_ada", _mm(cond_all, dmod_cols, ta=True, name="mm_g_ada"))

    col_block = lambda a: lax.dynamic_slice(a, (0, chip * (a.shape[1] // N_CHIPS)), (a.shape[0], a.shape[1] // N_CHIPS))
    grads.update({
        "ln0_g": small_g["ln0_g"], "ln0_b": small_g["ln0_b"], "b_ada": g_b_ada,
        "dn_conv": col_block(small_g["dn_conv"])[None], "dn_a_log": small_g["dn_a_log"][None],
        "dn_dt_bias": small_g["dn_dt_bias"][None], "dn_norm_g": small_g["dn_norm_g"][None],
        "gla_w_gate2": col_block(small_g["gla_w_gate2"])[None], "gla_b_gate": small_g["gla_b_gate"][None],
        "gla_norm_g": small_g["gla_norm_g"][None], "ln1_g": small_g["ln1_g"][None],
        "ln1_b": small_g["ln1_b"][None], "ffn_conv": col_block(small_g["ffn_conv"])[None],
        "ffn_conv_b": small_g["ffn_conv_b"][None], "ln2_g": small_g["ln2_g"][None], "ln2_b": small_g["ln2_b"][None],
    })
    small = [n for n in names if n not in delta]
    d_s, m_s, v_s = _adamw_many([weights[n] for n in small], [grads[n] for n in small],
                                [m_in[n] for n in small], [v_in[n] for n in small])
    for out, vals in ((delta, d_s), (new_m, m_s), (new_v, v_s)):
        out.update(zip(small, vals))

    return (loss, grad_x.reshape(x.shape), *[grads[n] for n in names], *[delta[n] for n in names],
            *[new_m[n] for n in names], *[new_v[n] for n in names])
```

```python
import functools

import jax
import jax.numpy as jnp
from jax import lax
from jax.experimental import pallas as pl
from jax.experimental.pallas import tpu as pltpu

F32 = jnp.float32
BF16 = jnp.bfloat16
MESH = pl.DeviceIdType.MESH

D_MODEL = 1024
HEADS = 4
HEAD_DIM = 128
GLA_KEY = 64
GATE_RANK = 16
CHUNK = 64
D_FF = 2816
ALPHA = 2.0 ** 0.25
EPS = 1e-6
N_CHIPS = 4
N_DEV = 8

PROJ_W = 3840
OFF_GQ, OFF_GK, OFF_GV, OFF_GG, OFF_SMALL, GLA_W = 0, 256, 512, 1024, 1536, 1792
OFF_Z = 2048
W_IN_COLS = 3608


def _qkv_block(j):
    return jnp.where(j < 2, GLA_W // 128 + j, (OFF_Z + 512) // 128 - 2 + j)

ADAM_LR, ADAM_B1, ADAM_B2, ADAM_EPS, ADAM_WD, ADAM_STEP = 0.001, 0.9, 0.999, 1e-08, 0.01, 10

VMEM_LIMIT = 56 * 1024 * 1024
ROW_TILE = 512


def _cparams(sem):
    return pltpu.CompilerParams(dimension_semantics=sem, vmem_limit_bytes=VMEM_LIMIT)


def _pick(n, prefs):
    for p in prefs:
        if n % p == 0:
            return p
    return n


def _mm(a, b, *, ta=False, tb=False, out_slabs=1, out_dtype=F32, name, rider=None):
    a_slabs = a.shape[0] if a.ndim == 3 else 1
    b_slabs = b.shape[0] if b.ndim == 3 else 1
    assert not (ta and a_slabs > 1)
    a2, b2 = a.shape[-2:], b.shape[-2:]
    if ta:
        k_dim, m_dim = a2
    else:
        m_dim, k_dim = a2[0], a2[1] * a_slabs
    n_dim = b2[0] if tb else b2[1] * b_slabs
    k_slabs = max(a_slabs, b_slabs if tb else 1)
    n_slabs = max(out_slabs, 1 if tb else b_slabs)
    tm = _pick(m_dim, (1024, 1408, 512, 256, 128))
    tn = _pick(n_dim // n_slabs, (1536, 1408, 1280, 1024, 768, 512, 384, 256, 128))
    tk = _pick(k_dim // k_slabs, (1408, 1280, 1024, 512, 256, 128))
    nk, nj = k_dim // tk, n_dim // tn
    nk_a, nk_b, nj_b, nj_o = nk // a_slabs, nk // b_slabs, nj // b_slabs, nj // out_slabs
    dims = (((0 if ta else 1,), (1 if tb else 0,)), ((), ()))

    grid = (m_dim // tm, nj, nk)
    assert out_dtype == F32
    r_inputs, r_in_specs, r_out_specs, r_sems, split = _with_rider(rider, 2, 1, 0)

    def body(*refs):
        (a_ref, b_ref, o_ref), parts = split(refs)
        ride_first, ride_last = _ride(rider, parts, grid)
        if rider is not None:
            ride_first()
        prod = lax.dot_general(a_ref[...].astype(BF16), b_ref[...].astype(BF16), dims, preferred_element_type=F32)
        if nk == 1:
            o_ref[...] = prod
        else:
            _acc(o_ref, prod, pl.program_id(2) == 0)
        if rider is not None:
            ride_last()

    if ta:
        a_spec = pl.BlockSpec((tk, tm), lambda i, j, k: (k, i))
    elif a_slabs > 1:
        a_spec = pl.BlockSpec((None, tm, tk), lambda i, j, k: (k // nk_a, i, k % nk_a))
    else:
        a_spec = pl.BlockSpec((tm, tk), lambda i, j, k: (i, k))
    if tb and b_slabs > 1:
        b_spec = pl.BlockSpec((None, tn, tk), lambda i, j, k: (k // nk_b, j, k % nk_b))
    elif tb:
        b_spec = pl.BlockSpec((tn, tk), lambda i, j, k: (j, k))
    elif b_slabs > 1:
        b_spec = pl.BlockSpec((None, tk, tn), lambda i, j, k: (j // nj_b, k, j % nj_b))
    else:
        b_spec = pl.BlockSpec((tk, tn), lambda i, j, k: (k, j))
    if out_slabs > 1:
        o_spec = pl.BlockSpec((None, tm, tn), lambda i, j, k: (j // nj_o, i, j % nj_o))
        o_shape = (out_slabs, m_dim, n_dim // out_slabs)
    else:
        o_spec, o_shape = pl.BlockSpec((tm, tn), lambda i, j, k: (i, j)), (m_dim, n_dim)
    out, *rider_outs = pl.pallas_call(
        body, name=name, grid=grid,
        in_specs=[a_spec, b_spec] + r_in_specs, out_specs=[o_spec] + r_out_specs,
        out_shape=[jax.ShapeDtypeStruct(o_shape, out_dtype)] + (list(rider.out_shapes) if rider else []),
        scratch_shapes=r_sems,
        compiler_params=_cparams(("arbitrary",) * 3 if rider else ("parallel", "parallel", "arbitrary")),
    )(a, b, *r_inputs)
    return (out, rider_outs) if rider else out


def _ln(x, g, b):
    mu = jnp.mean(x, -1, keepdims=True)
    xc = x - mu
    var = jnp.mean(xc * xc, -1, keepdims=True)
    return xc * lax.rsqrt(var + EPS) * g + b


def _softplus(x):
    return jnp.maximum(x, 0.0) + jnp.log(1.0 + jnp.exp(-jnp.abs(x)))


def _silu(x):
    return x * jax.nn.sigmoid(x)


def _dsilu(x):
    s = jax.nn.sigmoid(x)
    return s * (1.0 + x * (1.0 - s))


def _f_ln0(x, g, b, sc, sh):
    x0 = _ln(x, g, b)
    return x0, x0 * (1.0 + sc) + sh


def _f_ln1(x0, y, gt, g, b, sc, sh):
    x1 = _ln(ALPHA * x0 + (1.0 + gt) * y, g, b)
    return x1, x1 * (1.0 + sc) + sh


def _f_ln2_loss(x1, y2, gt, g, b, tgt):
    x2 = _ln(ALPHA * x1 + (1.0 + gt) * y2, g, b)
    err = x2 - tgt
    per_row = jnp.sum(err * err, -1, keepdims=True) * (0.5 / D_MODEL)
    return jnp.sum(per_row, 0, keepdims=True)


def _row_specs(t_len):
    nt = t_len // ROW_TILE
    row = pl.BlockSpec((ROW_TILE, D_MODEL), lambda b, i: (b * nt + i, 0))
    vec = pl.BlockSpec((1, D_MODEL), lambda b, i: (0, 0))
    mod = pl.BlockSpec((None, 6, D_MODEL), lambda b, i: (b, 0, 0))
    return nt, row, vec, mod


def _first_step():
    return jnp.logical_and(pl.program_id(0) == 0, pl.program_id(1) == 0)


def _acc(ref, val, first, at=(Ellipsis,)):
    @pl.when(first)
    def _():
        ref[at] = val

    @pl.when(jnp.logical_not(first))
    def _():
        ref[at] += val


def _acc_rows(ref, rows, first):
    for i, r in enumerate(rows):
        _acc(ref, r, first, at=(slice(i, i + 1), slice(None)))


def _ln0_fwd(x, g, b, mod, n_b, t_len):
    nt, row, vec, mods = _row_specs(t_len)

    def body(x_ref, g_ref, b_ref, mod_ref, x0_ref, h_ref):
        x0, h = _f_ln0(x_ref[...], g_ref[...], b_ref[...], mod_ref[1:2, :], mod_ref[0:1, :])
        x0_ref[...] = x0
        h_ref[...] = h.astype(BF16)

    return pl.pallas_call(
        body, name="ln0_fwd", grid=(n_b, nt), in_specs=[row, vec, vec, mods], out_specs=[row, row],
        out_shape=[jax.ShapeDtypeStruct(x.shape, F32), jax.ShapeDtypeStruct(x.shape, BF16)],
        compiler_params=_cparams(("parallel", "parallel")),
    )(x, g, b, mod)


def _ln0_bwd(x, g, b, mod, dx0, dh, n_b, t_len):
    nt, row, vec, mods = _row_specs(t_len)
    dmod_spec = pl.BlockSpec((None, 2, D_MODEL), lambda bb, i: (bb, 0, 0))

    def body(x_ref, g_ref, b_ref, mod_ref, dx0_ref, dh_ref, dx_ref, dg_ref, db_ref, dmod_ref):
        _, pull = jax.vjp(_f_ln0, x_ref[...], g_ref[...], b_ref[...], mod_ref[1:2, :], mod_ref[0:1, :])
        dx, dg, db, dsc, dsh = pull((dx0_ref[...], dh_ref[...]))
        dx_ref[...] = dx
        _acc(dg_ref, dg, _first_step())
        _acc(db_ref, db, _first_step())
        _acc_rows(dmod_ref, [dsh, dsc], pl.program_id(1) == 0)

    return pl.pallas_call(
        body, name="ln0_bwd", grid=(n_b, nt), in_specs=[row, vec, vec, mods, row, row],
        out_specs=[row, vec, vec, dmod_spec],
        out_shape=[jax.ShapeDtypeStruct(x.shape, F32), jax.ShapeDtypeStruct((1, D_MODEL), F32),
                   jax.ShapeDtypeStruct((1, D_MODEL), F32), jax.ShapeDtypeStruct((n_b, 2, D_MODEL), F32)],
        compiler_params=_cparams(("arbitrary", "arbitrary")),
    )(x, g, b, mod, dx0, dh)


def _ln1_fwd(x0, y, g, b, mod, n_b, t_len):
    nt, row, vec, mods = _row_specs(t_len)

    def body(x0_ref, y_ref, g_ref, b_ref, mod_ref, x1_ref, h_ref):
        x1, h = _f_ln1(x0_ref[...], y_ref[...], mod_ref[2:3, :], g_ref[...], b_ref[...],
                       mod_ref[4:5, :], mod_ref[3:4, :])
        x1_ref[...] = x1
        h_ref[...] = h.astype(BF16)

    return pl.pallas_call(
        body, name="ln1_fwd", grid=(n_b, nt), in_specs=[row, row, vec, vec, mods], out_specs=[row, row],
        out_shape=[jax.ShapeDtypeStruct(x0.shape, F32), jax.ShapeDtypeStruct(x0.shape, BF16)],
        compiler_params=_cparams(("parallel", "parallel")),
    )(x0, y, g, b, mod)


def _ln1_bwd(x0, y, g, b, mod, dx1, dh, n_b, t_len):
    nt, row, vec, mods = _row_specs(t_len)
    dmod_spec = pl.BlockSpec((None, 3, D_MODEL), lambda bb, i: (bb, 0, 0))

    def body(x0_ref, y_ref, g_ref, b_ref, mod_ref, dx1_ref, dh_ref, dx0_ref, dy_ref, dg_ref, db_ref, dmod_ref):
        _, pull = jax.vjp(_f_ln1, x0_ref[...], y_ref[...], mod_ref[2:3, :], g_ref[...], b_ref[...],
                          mod_ref[4:5, :], mod_ref[3:4, :])
        dx0, dy, dgt, dg, db, dsc, dsh = pull((dx1_ref[...], dh_ref[...]))
        dx0_ref[...] = dx0
        dy_ref[...] = dy.astype(BF16)
        _acc(dg_ref, dg, _first_step())
        _acc(db_ref, db, _first_step())
        _acc_rows(dmod_ref, [dgt, dsh, dsc], pl.program_id(1) == 0)

    return pl.pallas_call(
        body, name="ln1_bwd", grid=(n_b, nt), in_specs=[row, row, vec, vec, mods, row, row],
        out_specs=[row, row, vec, vec, dmod_spec],
        out_shape=[jax.ShapeDtypeStruct(x0.shape, F32), jax.ShapeDtypeStruct(x0.shape, BF16),
                   jax.ShapeDtypeStruct((1, D_MODEL), F32), jax.ShapeDtypeStruct((1, D_MODEL), F32),
                   jax.ShapeDtypeStruct((n_b, 3, D_MODEL), F32)],
        compiler_params=_cparams(("arbitrary", "arbitrary")),
    )(x0, y, g, b, mod, dx1, dh)


def _ln2_loss_bwd(x1, y2, g, b, mod, tgt, n_b, t_len):
    nt, row, vec, mods = _row_specs(t_len)
    one = pl.BlockSpec((1, 128), lambda bb, i: (0, 0))
    dmod_spec = pl.BlockSpec((None, 1, D_MODEL), lambda bb, i: (bb, 0, 0))

    def body(x1_ref, y2_ref, g_ref, b_ref, mod_ref, t_ref, loss_ref, dx1_ref, dy2_ref, dg_ref, db_ref, dgt_ref):
        loss, pull = jax.vjp(functools.partial(_f_ln2_loss, tgt=t_ref[...]), x1_ref[...], y2_ref[...],
                             mod_ref[5:6, :], g_ref[...], b_ref[...])
        dx1, dy2, dgt, dg, db = pull(jnp.ones((1, 1), F32))
        dx1_ref[...] = dx1
        dy2_ref[...] = dy2.astype(BF16)
        _acc(loss_ref, jnp.broadcast_to(loss, (1, 128)), _first_step())
        _acc(dg_ref, dg, _first_step())
        _acc(db_ref, db, _first_step())
        _acc(dgt_ref, dgt, pl.program_id(1) == 0)

    return pl.pallas_call(
        body, name="ln2_loss_bwd", grid=(n_b, nt), in_specs=[row, row, vec, vec, mods, row],
        out_specs=[one, row, row, vec, vec, dmod_spec],
        out_shape=[jax.ShapeDtypeStruct((1, 128), F32), jax.ShapeDtypeStruct(x1.shape, F32),
                   jax.ShapeDtypeStruct(x1.shape, BF16), jax.ShapeDtypeStruct((1, D_MODEL), F32),
                   jax.ShapeDtypeStruct((1, D_MODEL), F32), jax.ShapeDtypeStruct((n_b, 1, D_MODEL), F32)],
        compiler_params=_cparams(("arbitrary", "arbitrary")),
    )(x1, y2, g, b, mod, tgt)


def _shift_down(x, s):
    if s == 0:
        return x
    rows = lax.broadcasted_iota(jnp.int32, x.shape, 0)
    return jnp.where(rows >= s, pltpu.roll(x, s, 0), 0.0)


def _shift_up(x, s):
    if s == 0:
        return x
    t_len = x.shape[0]
    rows = lax.broadcasted_iota(jnp.int32, x.shape, 0)
    return jnp.where(rows < t_len - s, pltpu.roll(x, t_len - s, 0), 0.0)


def _taps(x, k_w):
    return [_shift_down(x, k_w - 1 - k) for k in range(k_w)]


def _conv(taps, w):
    out = w[0:1, :] * taps[0]
    for k in range(1, len(taps)):
        out = out + w[k:k + 1, :] * taps[k]
    return out


def _conv_bwd(taps, w, du):
    k_w = len(taps)
    dx = w[k_w - 1:k_w, :] * du
    for k in range(k_w - 1):
        dx = dx + w[k:k + 1, :] * _shift_up(du, k_w - 1 - k)
    return dx, [jnp.sum(du * taps[k], 0, keepdims=True) for k in range(k_w)]


def _dn_pre_fwd(proj, conv_w, n_b, t_len):
    n_ct = 3 * HEADS
    k_w = conv_w.shape[0]

    def body(x_ref, w_ref, o_ref):
        o_ref[...] = _silu(_conv(_taps(x_ref[...], k_w), w_ref[...]))

    return pl.pallas_call(
        body, name="dn_pre_fwd", grid=(n_ct, n_b),
        in_specs=[pl.BlockSpec((t_len, 128), lambda j, b: (b, _qkv_block(j))),
                  pl.BlockSpec((k_w, 128), lambda j, b: (0, j))],
        out_specs=pl.BlockSpec((t_len, 128), lambda j, b: (b, j)),
        out_shape=jax.ShapeDtypeStruct((n_b * t_len, n_ct * 128), F32),
        compiler_params=_cparams(("parallel", "parallel")),
    )(proj, conv_w)


def _dn_pre_bwd(proj, conv_w, dqkv, d_proj, n_b, t_len):
    n_ct = 3 * HEADS
    k_w = conv_w.shape[0]

    def body(x_ref, w_ref, d_ref, _, dx_ref, dw_ref):
        taps, w = _taps(x_ref[...], k_w), w_ref[...]
        du = d_ref[...] * _dsilu(_conv(taps, w))
        dx, dw = _conv_bwd(taps, w, du)
        dx_ref[...] = dx.astype(BF16)
        _acc_rows(dw_ref, dw, pl.program_id(1) == 0)

    return pl.pallas_call(
        body, name="dn_pre_bwd", grid=(n_ct, n_b),
        in_specs=[pl.BlockSpec((t_len, 128), lambda j, b: (b, _qkv_block(j))),
                  pl.BlockSpec((k_w, 128), lambda j, b: (0, j)),
                  pl.BlockSpec((t_len, 128), lambda j, b: (b, j)), pl.BlockSpec(memory_space=pl.ANY)],
        out_specs=[pl.BlockSpec((t_len, 128), lambda j, b: (b, _qkv_block(j))),
                   pl.BlockSpec((k_w, 128), lambda j, b: (0, j))],
        out_shape=[jax.ShapeDtypeStruct(d_proj.shape, BF16), jax.ShapeDtypeStruct((k_w, n_ct * 128), F32)],
        input_output_aliases={3: 0},
        compiler_params=_cparams(("parallel", "arbitrary")),
    )(proj, conv_w, dqkv, d_proj)


FFN_TC = 256
FFN_NT = D_FF // FFN_TC


def _ffn_specs(t_len):
    blk = lambda off: pl.BlockSpec((t_len, FFN_TC), lambda j, b: (b, j + off))
    wblk = lambda off: pl.BlockSpec((3, FFN_TC), lambda j, b: (0, j + off))
    bblk = lambda off: pl.BlockSpec((1, FFN_TC), lambda j, b: (0, j + off))
    return [blk(0), blk(FFN_NT), wblk(0), wblk(FFN_NT), bblk(0), bblk(FFN_NT)]


def _ffn_act_fwd(up, conv_w, conv_b, n_b, t_len):
    def body(g_ref, v_ref, wg_ref, wv_ref, bg_ref, bv_ref, o_ref):
        ug = _conv(_taps(g_ref[...], 3), wg_ref[...]) + bg_ref[...]
        uv = _conv(_taps(v_ref[...], 3), wv_ref[...]) + bv_ref[...]
        o_ref[...] = (_silu(ug) * uv).astype(BF16)

    return pl.pallas_call(
        body, name="ffn_act_fwd", grid=(FFN_NT, n_b), in_specs=_ffn_specs(t_len),
        out_specs=pl.BlockSpec((t_len, FFN_TC), lambda j, b: (b, j)),
        out_shape=jax.ShapeDtypeStruct((n_b * t_len, D_FF), BF16),
        compiler_params=_cparams(("parallel", "parallel")),
    )(up, up, conv_w, conv_w, conv_b, conv_b)


def _ffn_act_bwd(up, conv_w, conv_b, da, n_b, t_len):
    def body(g_ref, v_ref, wg_ref, wv_ref, bg_ref, bv_ref, da_ref, dup_ref, dw_ref, db_ref):
        first = pl.program_id(1) == 0
        tg, tv, wg, wv = _taps(g_ref[...], 3), _taps(v_ref[...], 3), wg_ref[...], wv_ref[...]
        ug = _conv(tg, wg) + bg_ref[...]
        uv = _conv(tv, wv) + bv_ref[...]
        d_act = da_ref[...]
        sig = jax.nn.sigmoid(ug)
        d_v = d_act * (ug * sig)
        d_g = d_act * uv * (sig * (1.0 + ug * (1.0 - sig)))
        for slab, (taps, w, du) in enumerate(((tg, wg, d_g), (tv, wv, d_v))):
            dx, dw = _conv_bwd(taps, w, du)
            dup_ref[slab] = dx.astype(BF16)
            for k, dw_k in enumerate(dw):
                _acc(dw_ref, dw_k, first, at=(slab, slice(k, k + 1), slice(None)))
            _acc(db_ref, jnp.sum(du, 0, keepdims=True), first, at=(slab, slice(None), slice(None)))

    return pl.pallas_call(
        body, name="ffn_act_bwd", grid=(FFN_NT, n_b),
        in_specs=_ffn_specs(t_len) + [pl.BlockSpec((t_len, FFN_TC), lambda j, b: (b, j))],
        out_specs=[pl.BlockSpec((2, t_len, FFN_TC), lambda j, b: (0, b, j)),
                   pl.BlockSpec((2, 3, FFN_TC), lambda j, b: (0, 0, j)),
                   pl.BlockSpec((2, 1, FFN_TC), lambda j, b: (0, 0, j))],
        out_shape=[jax.ShapeDtypeStruct((2, n_b * t_len, D_FF), BF16),
                   jax.ShapeDtypeStruct((2, 3, D_FF), F32), jax.ShapeDtypeStruct((2, 1, D_FF), F32)],
        compiler_params=_cparams(("parallel", "arbitrary")),
    )(up, up, conv_w, conv_w, conv_b, conv_b, da)


NN = (((2,), (1,)), ((0,), (0,)))
NT = (((2,), (2,)), ((0,), (0,)))
TN = (((1,), (1,)), ((0,), (0,)))


def _iota3(shape, axis):
    return lax.broadcasted_iota(jnp.int32, shape, axis)


def _dg(a, b, dims):
    return lax.dot_general(a, b, dims, preferred_element_type=F32)


def _dot(a, b):
    return _dg(a, b, NN)


def _dot_nt(a, b):
    return _dg(a, b, NT)


def _dot_tn(a, b):
    return _dg(a, b, TN)


def _split(a):
    hi = a.astype(BF16)
    return hi, (a - hi.astype(F32)).astype(BF16)


def _dg3(a, b, dims):
    ah, al = _split(a)
    bh, bl = _split(b)
    return _dg(ah, bh, dims) + (_dg(ah, bl, dims) + _dg(al, bh, dims))


@jax.custom_vjp
def _dot3(a, b):
    return _dg3(a, b, NN)


def _dot3_fwd(a, b):
    return _dg3(a, b, NN), (a, b)


def _dot3_bwd(res, g):
    a, b = res
    return _dg3(g, b, NT), _dg3(a, g, TN)


_dot3.defvjp(_dot3_fwd, _dot3_bwd)


def _lower_ones(g_n, n):
    shape = (g_n, n, n)
    return jnp.where(_iota3(shape, 1) >= _iota3(shape, 2), 1.0, 0.0).astype(BF16)


@jax.custom_vjp
def _chunk_cumsum(x):
    hi, lo = _split(x)
    tri = _lower_ones(x.shape[0], x.shape[1])
    return _dg(tri, hi, NN) + _dg(tri, lo, NN)


def _chunk_cumsum_fwd(x):
    return _chunk_cumsum(x), None


def _chunk_cumsum_bwd(_, g):
    hi, lo = _split(g)
    tri = _lower_ones(g.shape[0], g.shape[1])
    return (_dg(tri, hi, TN) + _dg(tri, lo, TN),)


_chunk_cumsum.defvjp(_chunk_cumsum_fwd, _chunk_cumsum_bwd)


@jax.custom_vjp
def _unit_lower_inv(m):
    n = m.shape[1]
    p = -m
    a = jnp.where(_iota3(m.shape, 1) == _iota3(m.shape, 2), 1.0, 0.0) + p
    span = 2
    while span < n:
        p = _dg3(p, p, NN)
        a = a + _dg3(a, p, NN)
        span *= 2
    return a


def _unit_lower_inv_fwd(m):
    a = _unit_lower_inv(m)
    return a, a


def _unit_lower_inv_bwd(a, da):
    return (-_dg3(a, _dg3(da, a, NT), TN),)


_unit_lower_inv.defvjp(_unit_lower_inv_fwd, _unit_lower_inv_bwd)


@jax.custom_vjp
def _saved_lower_inv(m, a):
    return a


def _saved_lower_inv_fwd(m, a):
    return a, a


def _saved_lower_inv_bwd(a, da):
    return _unit_lower_inv_bwd(a, da)[0], jnp.zeros_like(a)


_saved_lower_inv.defvjp(_saved_lower_inv_fwd, _saved_lower_inv_bwd)


def _rms_gate(o, gn, gate):
    return o * lax.rsqrt(jnp.mean(o * o, -1, keepdims=True) + EPS) * gn * _silu(gate)


def _dn_chains(q, k, v, z, small, s_in, a_log, dt_bias, gn, a_saved=None):
    prep = _dn_prepare(q, k, v, small, a_log, dt_bias, a_saved)
    og, s_out = _dn_advance(prep[:-1], z, s_in, gn)
    return og, s_out, prep[-1]


def _dn_prepare(q, k, v, small, a_log, dt_bias, a_saved=None):
    g_n, c_len = q.shape[0], q.shape[1]
    sq = (g_n, c_len, c_len)
    row, col = _iota3(sq, 1), _iota3(sq, 2)
    causal, strict, eye = row >= col, row > col, row == col
    qn = q * lax.rsqrt(jnp.sum(q * q, -1, keepdims=True) + EPS) * (HEAD_DIM ** -0.5)
    kn = k * lax.rsqrt(jnp.sum(k * k, -1, keepdims=True) + EPS)
    lane = _iota3(small.shape, 2)
    head = jnp.bitwise_and(_iota3(small.shape, 0), HEADS - 1)
    la_all = -jnp.exp(a_log) * _softplus(small + dt_bias)
    la_c = jnp.sum(jnp.where(lane == head, la_all, 0.0), 2, keepdims=True)
    beta = jnp.sum(jnp.where(lane == head + HEADS, jax.nn.sigmoid(small), 0.0), 2, keepdims=True)
    la_b = jnp.broadcast_to(la_c, sq)
    la_r = jnp.sum(jnp.where(eye, la_b, 0.0), 1, keepdims=True)
    g_c = jnp.sum(jnp.where(causal, jnp.broadcast_to(la_r, sq), 0.0), 2, keepdims=True)
    g_r = jnp.sum(jnp.where(row <= col, la_b, 0.0), 1, keepdims=True)
    g_last = jnp.sum(la_c, 1, keepdims=True)
    decay = jnp.exp(jnp.where(causal, g_c - g_r, -1e30))
    e_g = jnp.exp(g_c)
    kb = kn * beta
    m_low = jnp.where(strict, _dot_nt(kb, kn) * decay, 0.0)
    a_inv = _unit_lower_inv(m_low) if a_saved is None else _saved_lower_inv(m_low, a_saved)
    u = _dot3(a_inv, v * beta)
    w = _dot3(a_inv, kb * e_g)
    attn = _dot_nt(qn, kn) * decay
    return u, w, attn, qn * e_g, kn * jnp.exp(g_last - g_c), jnp.exp(g_last), a_inv


def _dn_advance(prep, z, s_in, gn):
    u, w, attn, q_dec, k_dec, g_chunk = prep
    v_new = u - _dot(w, s_in)
    o = _dot(q_dec, s_in) + _dot(attn, v_new)
    s_out = s_in * g_chunk + _dot_tn(k_dec, v_new)
    return _rms_gate(o, gn, z), s_out


def _gla_chains(q, k, v, gate, small, s_in, w2, b2, gn):
    g_n, c_len = q.shape[0], q.shape[1]
    sq, kk = (g_n, c_len, c_len), (g_n, GLA_KEY, GLA_KEY)
    causal = _iota3(sq, 1) >= _iota3(sq, 2)
    la = -_softplus(-(_dot(small, w2) + b2)) * (1.0 / 16.0)
    b = _chunk_cumsum(la)
    b_last = jnp.sum(jnp.where(_iota3(b.shape, 1) == c_len - 1, b, 0.0), 1, keepdims=True)
    q_dec = q * (GLA_KEY ** -0.5) * jnp.exp(b)
    attn = jnp.where(causal, _dot_nt(q_dec, k * jnp.exp(-b)), 0.0)
    o = _dot(q_dec, s_in) + _dot(attn, v)
    g_row = jnp.exp(b_last)
    g_col = jnp.sum(jnp.where(_iota3(kk, 1) == _iota3(kk, 2), jnp.broadcast_to(g_row, kk), 0.0), 2, keepdims=True)
    s_out = s_in * g_col + _dot_tn(k * jnp.exp(b_last - b), v)
    return _rms_gate(o, gn, gate), s_out


def _chunk_spec(n_b, width, col_block, n_c, reverse=False):
    if reverse:
        return pl.BlockSpec((n_b, CHUNK, width), lambda n: (0, n_c - 1 - n, col_block))
    return pl.BlockSpec((n_b, CHUNK, width), lambda n: (0, n, col_block))


def _hist_spec(n_b, d_k, n_c, reverse=False):
    if reverse:
        return pl.BlockSpec((None, n_b * HEADS, d_k, HEAD_DIM), lambda n: (n_c - 1 - n, 0, 0, 0))
    return pl.BlockSpec((None, n_b * HEADS, d_k, HEAD_DIM), lambda n: (n, 0, 0, 0))


def _ainv_spec(n_b, n_c, reverse=False):
    if reverse:
        return pl.BlockSpec((None, n_b * HEADS, CHUNK, CHUNK), lambda n: (n_c - 1 - n, 0, 0, 0))
    return pl.BlockSpec((None, n_b * HEADS, CHUNK, CHUNK), lambda n: (n, 0, 0, 0))


def _stack_chains(ref, n_b, slices):
    return jnp.stack([ref[b, :, sl] for b in range(n_b) for sl in slices], axis=0)


def _per_chain(ref, n_b):
    return jnp.stack([ref[b] for b in range(n_b) for _ in range(HEADS)], axis=0)


def _unstack_chains(ref, val, n_b, slices, offset=0):
    for b in range(n_b):
        for h, sl in enumerate(slices):
            ref[b, :, slice(offset + sl.start, offset + sl.stop)] = val[b * HEADS + h].astype(ref.dtype)


def _gate_weights(w2_ref, b2_ref, n_b):
    w2 = jnp.stack([w2_ref[:, ks] for _ in range(n_b) for ks in GLA_KSL], axis=0)
    b2 = jnp.stack([b2_ref[:, ks] for _ in range(n_b) for ks in GLA_KSL], axis=0)
    return w2, b2


def _sum_heads(val, n_b):
    return [sum(val[b * HEADS + h] for h in range(HEADS)) for b in range(n_b)]


def _const_spec(shape):
    return pl.BlockSpec(shape, lambda n: (0,) * len(shape))


DN_SL = [slice(h * HEAD_DIM, (h + 1) * HEAD_DIM) for h in range(HEADS)]
GLA_KSL = [slice(h * GLA_KEY, (h + 1) * GLA_KEY) for h in range(HEADS)]


class Rider:
    def __init__(self, inputs, out_shapes, sems, first, last):
        self.inputs, self.out_shapes, self.sems, self.first, self.last = inputs, out_shapes, sems, first, last


def _with_rider(rider, n_in, n_out, n_scratch):
    if rider is None:
        return [], [], [], [], lambda refs: (refs, None)
    r_in, r_out, r_sem = len(rider.inputs), len(rider.out_shapes), len(rider.sems)

    def split(refs):
        own_in, rest = refs[:n_in], refs[n_in:]
        rid_in, rest = rest[:r_in], rest[r_in:]
        own_out, rest = rest[:n_out], rest[n_out:]
        rid_out, rest = rest[:r_out], rest[r_out:]
        own_scr, rid_sem = rest[:n_scratch], rest[n_scratch:]
        return own_in + own_out + own_scr, (rid_in, rid_out, rid_sem)

    return list(rider.inputs), [HBM_SPEC] * r_in, [HBM_SPEC] * r_out, list(rider.sems), split


def _ride(rider, parts, grid):
    if rider is None:
        return None, None
    grid = grid if isinstance(grid, tuple) else (grid,)

    def at(step_of):
        hit = pl.program_id(0) == step_of(grid[0])
        for axis in range(1, len(grid)):
            hit = jnp.logical_and(hit, pl.program_id(axis) == step_of(grid[axis]))
        return hit

    def first():
        pl.when(at(lambda n: 0))(lambda: rider.first(*parts))

    def last():
        pl.when(at(lambda n: n - 1))(lambda: rider.last(*parts))

    return first, last


FWD_CHUNKS = 4


def _dn_scan_fwd(qkv, proj, a_log, dt_bias, gn, n_b, t_len, rider=None):
    n_c, n_g, rows = t_len // CHUNK, n_b * HEADS, FWD_CHUNKS * CHUNK
    n_s = n_c // FWD_CHUNKS
    spec = lambda width, col_block: pl.BlockSpec((n_b, rows, width), lambda n: (0, n, col_block))
    kept = lambda d0, d1: pl.BlockSpec((FWD_CHUNKS, n_g, d0, d1), lambda n: (n, 0, 0, 0))
    r_inputs, r_in_specs, r_out_specs, r_sems, split = _with_rider(rider, 8, 3, 1)
    chunk_rows = [slice(j * CHUNK, (j + 1) * CHUNK) for j in range(FWD_CHUNKS)]

    def body(*refs):
        (q_ref, k_ref, v_ref, z_ref, sm_ref, al_ref, dt_ref, gn_ref,
         o_ref, hist_ref, ainv_ref, s_ref), parts = split(refs)
        ride_first, ride_last = _ride(rider, parts, n_s)
        if rider is not None:
            ride_first()

        @pl.when(pl.program_id(0) == 0)
        def _():
            s_ref[...] = jnp.zeros_like(s_ref)

        def stack(ref, slices):
            return jnp.stack([ref[b, rs, sl] for rs in chunk_rows for b in range(n_b) for sl in slices], axis=0)

        prep = _dn_prepare(stack(q_ref, DN_SL), stack(k_ref, DN_SL), stack(v_ref, DN_SL),
                           stack(sm_ref, [slice(None)] * HEADS), al_ref[...], dt_ref[...])
        z, state = stack(z_ref, DN_SL), s_ref[...]
        for j, rs in enumerate(chunk_rows):
            mine = slice(j * n_g, (j + 1) * n_g)
            hist_ref[j] = state
            ainv_ref[j] = prep[-1][mine]
            og, state = _dn_advance(tuple(a[mine] for a in prep[:-1]), z[mine], state, gn_ref[...])
            for b in range(n_b):
                for h, sl in enumerate(DN_SL):
                    o_ref[b, rs, sl] = og[b * HEADS + h].astype(BF16)
        s_ref[...] = state
        if rider is not None:
            ride_last()

    qkv3, proj3 = qkv.reshape(n_b, t_len, -1), proj.reshape(n_b, t_len, -1)
    o, hist, ainv, *rider_outs = pl.pallas_call(
        body, name="dn_scan_fwd", grid=(n_s,),
        in_specs=[spec(512, 0), spec(512, 1), spec(512, 2), spec(512, OFF_Z // 512), spec(128, OFF_SMALL // 128),
                  _const_spec((1, 128)), _const_spec((1, 128)), _const_spec((1, 128))] + r_in_specs,
        out_specs=[spec(512, 0), kept(HEAD_DIM, HEAD_DIM), kept(CHUNK, CHUNK)] + r_out_specs,
        out_shape=[jax.ShapeDtypeStruct((n_b, t_len, 2 * 512), BF16),
                   jax.ShapeDtypeStruct((n_c, n_b * HEADS, HEAD_DIM, HEAD_DIM), F32),
                   jax.ShapeDtypeStruct((n_c, n_b * HEADS, CHUNK, CHUNK), F32)]
        + (list(rider.out_shapes) if rider else []),
        scratch_shapes=[pltpu.VMEM((n_b * HEADS, HEAD_DIM, HEAD_DIM), F32)] + r_sems,
        compiler_params=_cparams(("arbitrary",)),
    )(qkv3, qkv3, qkv3, proj3, proj3, a_log, dt_bias, gn, *r_inputs)
    return o, (hist, ainv), rider_outs


SCAN_BWD_W = OFF_Z + 512


def _scan_bwd(qkv, proj, dn_params, gla_params, hist_dn, hist_gla, d_o, n_b, t_len, rider=None):
    n_c = t_len // CHUNK
    rev = functools.partial(_chunk_spec, n_b, n_c=n_c, reverse=True)
    r_inputs, r_in_specs, r_out_specs, r_sems, split = _with_rider(rider, 19, 8, 2)
    (hist, ainv), do_gla_sl = hist_dn, [slice(512 + sl.start, 512 + sl.stop) for sl in DN_SL]

    def body(*refs):
        (q_ref, k_ref, v_ref, z_ref, sm_ref, gq_ref, gk_ref, gv_ref, gg_ref,
         al_ref, dt_ref, dgn_in_ref, w2_ref, b2_ref, ggn_in_ref, hist_ref, ainv_ref, ghist_ref, do_ref,
         dqkv_ref, dp_ref, dal_ref, ddt_ref, dgn_ref, dw2_ref, db2_ref, dggn_ref, ds_ref, gds_ref), parts = split(refs)
        ride_first, ride_last = _ride(rider, parts, n_c)
        if rider is not None:
            ride_first()
        first = pl.program_id(0) == 0

        @pl.when(first)
        def _():
            ds_ref[...] = jnp.zeros_like(ds_ref)
            gds_ref[...] = jnp.zeros_like(gds_ref)

        small = _per_chain(sm_ref, n_b)
        chains = lambda *a: _dn_chains(*a, a_saved=ainv_ref[...])[:2]
        _, pull = jax.vjp(chains, *(_stack_chains(r, n_b, DN_SL) for r in (q_ref, k_ref, v_ref, z_ref)),
                          small, hist_ref[...], al_ref[...], dt_ref[...], dgn_in_ref[...])
        dq, dk, dv, dz, dsm_dn, ds_in, dal, ddt, dgn = pull((_stack_chains(do_ref, n_b, DN_SL), ds_ref[...]))
        _, gpull = jax.vjp(_gla_chains, _stack_chains(gq_ref, n_b, GLA_KSL), _stack_chains(gk_ref, n_b, GLA_KSL),
                           _stack_chains(gv_ref, n_b, DN_SL), _stack_chains(gg_ref, n_b, DN_SL),
                           small, ghist_ref[...], *_gate_weights(w2_ref, b2_ref, n_b), ggn_in_ref[...])
        gq, gk, gv, gg, dsm_gla, gds_in, dw2, db2, dggn = gpull((_stack_chains(do_ref, n_b, do_gla_sl), gds_ref[...]))

        _unstack_chains(dqkv_ref, dq, n_b, DN_SL)
        _unstack_chains(dqkv_ref, dk, n_b, DN_SL, offset=512)
        _unstack_chains(dqkv_ref, dv, n_b, DN_SL, offset=1024)
        _unstack_chains(dp_ref, dz, n_b, DN_SL, offset=OFF_Z)
        _unstack_chains(dp_ref, gq, n_b, GLA_KSL, offset=OFF_GQ)
        _unstack_chains(dp_ref, gk, n_b, GLA_KSL, offset=OFF_GK)
        _unstack_chains(dp_ref, gv, n_b, DN_SL, offset=OFF_GV)
        _unstack_chains(dp_ref, gg, n_b, DN_SL, offset=OFF_GG)
        ds_ref[...] = ds_in
        gds_ref[...] = gds_in
        for b, (s_dn, s_gla) in enumerate(zip(_sum_heads(dsm_dn, n_b), _sum_heads(dsm_gla, n_b))):
            dp_ref[b, :, OFF_SMALL:OFF_SMALL + 128] = (s_dn + s_gla).astype(BF16)
            dp_ref[b, :, OFF_SMALL + 128:GLA_W] = jnp.zeros((CHUNK, GLA_W - OFF_SMALL - 128), BF16)
        _acc(dal_ref, dal, first)
        _acc(ddt_ref, ddt, first)
        _acc(dgn_ref, dgn, first)
        for h, ks in enumerate(GLA_KSL):
            _acc(dw2_ref, sum(dw2[b * HEADS + h] for b in range(n_b)), first, at=(slice(None), ks))
            _acc(db2_ref, sum(db2[b * HEADS + h] for b in range(n_b)), first, at=(slice(None), ks))
        _acc(dggn_ref, dggn, first)
        if rider is not None:
            ride_last()

    qkv3, proj3, do3 = (a.reshape(n_b, t_len, -1) for a in (qkv, proj, d_o))
    vec = jax.ShapeDtypeStruct((1, 128), F32)
    dqkv, d_proj, dal, ddt, dgn, dw2, db2, dggn, *rider_outs = pl.pallas_call(
        body, name="scan_bwd", grid=(n_c,),
        in_specs=[rev(512, 0), rev(512, 1), rev(512, 2), rev(512, OFF_Z // 512), rev(128, OFF_SMALL // 128),
                  rev(256, OFF_GQ // 256), rev(256, OFF_GK // 256), rev(512, OFF_GV // 512), rev(512, OFF_GG // 512),
                  _const_spec((1, 128)), _const_spec((1, 128)), _const_spec((1, 128)),
                  _const_spec((128, 256)), _const_spec((1, 256)), _const_spec((1, 128)),
                  _hist_spec(n_b, HEAD_DIM, n_c, reverse=True), _ainv_spec(n_b, n_c, reverse=True),
                  _hist_spec(n_b, GLA_KEY, n_c, reverse=True), rev(2 * 512, 0)] + r_in_specs,
        out_specs=[rev(1536, 0), rev(SCAN_BWD_W, 0), _const_spec((1, 128)), _const_spec((1, 128)),
                   _const_spec((1, 128)), _const_spec((128, 256)), _const_spec((1, 256)), _const_spec((1, 128))]
        + r_out_specs,
        out_shape=[jax.ShapeDtypeStruct((n_b, t_len, 1536), F32), jax.ShapeDtypeStruct((n_b, t_len, PROJ_W), BF16),
                   vec, vec, vec, jax.ShapeDtypeStruct((128, 256), F32), jax.ShapeDtypeStruct((1, 256), F32), vec]
        + (list(rider.out_shapes) if rider else []),
        scratch_shapes=[pltpu.VMEM((n_b * HEADS, HEAD_DIM, HEAD_DIM), F32),
                        pltpu.VMEM((n_b * HEADS, GLA_KEY, HEAD_DIM), F32)] + r_sems,
        compiler_params=_cparams(("arbitrary",)),
    )(qkv3, qkv3, qkv3, proj3, proj3, proj3, proj3, proj3, proj3, *dn_params, *gla_params, hist, ainv, hist_gla, do3,
      *r_inputs)
    return dqkv.reshape(n_b * t_len, 1536), d_proj, (dal, ddt, dgn), (dw2, db2, dggn), rider_outs


def _gla_scan_fwd(proj, w2, b2, gn, o_mix, n_b, t_len):
    n_c = t_len // CHUNK
    spec = functools.partial(_chunk_spec, n_b, n_c=n_c)

    def body(q_ref, k_ref, v_ref, g_ref, sm_ref, w2_ref, b2_ref, gn_ref, _, o_ref, hist_ref, s_ref):
        @pl.when(pl.program_id(0) == 0)
        def _():
            s_ref[...] = jnp.zeros_like(s_ref)

        s_in = s_ref[...]
        hist_ref[...] = s_in
        og, s_out = _gla_chains(_stack_chains(q_ref, n_b, GLA_KSL), _stack_chains(k_ref, n_b, GLA_KSL),
                                _stack_chains(v_ref, n_b, DN_SL), _stack_chains(g_ref, n_b, DN_SL),
                                _per_chain(sm_ref, n_b), s_in, *_gate_weights(w2_ref, b2_ref, n_b), gn_ref[...])
        _unstack_chains(o_ref, og, n_b, DN_SL)
        s_ref[...] = s_out

    proj3 = proj.reshape(n_b, t_len, -1)
    o, hist = pl.pallas_call(
        body, name="gla_scan_fwd", grid=(n_c,),
        in_specs=[spec(256, OFF_GQ // 256), spec(256, OFF_GK // 256), spec(512, OFF_GV // 512),
                  spec(512, OFF_GG // 512), spec(128, OFF_SMALL // 128),
                  _const_spec((128, 256)), _const_spec((1, 256)), _const_spec((1, 128)),
                  pl.BlockSpec(memory_space=pl.ANY)],
        out_specs=[spec(512, 1), _hist_spec(n_b, GLA_KEY, n_c)],
        out_shape=[jax.ShapeDtypeStruct(o_mix.shape, BF16),
                   jax.ShapeDtypeStruct((n_c, n_b * HEADS, GLA_KEY, HEAD_DIM), F32)],
        input_output_aliases={8: 0},
        scratch_shapes=[pltpu.VMEM((n_b * HEADS, GLA_KEY, HEAD_DIM), F32)],
        compiler_params=_cparams(("arbitrary",)),
    )(proj3, proj3, proj3, proj3, proj3, w2, b2, gn, o_mix)
    return o.reshape(n_b * t_len, 2 * 512), hist


W_IN_RUNS = ((0, 256, GLA_W), (256, 1536, OFF_Z + 512), (1536, 2048, OFF_Z), (2048, 2056, OFF_SMALL),
             (2056, 3592, 0), (3592, 3608, OFF_SMALL + 8))
W_IN_ROWS = 256


def _w_in_pieces(cols_per_chip):
    out = []
    for first, last, start in W_IN_RUNS:
        for j in range(N_CHIPS):
            a, b = max(first, cols_per_chip * j), min(last, cols_per_chip * (j + 1))
            if a < b:
                out.append((j, a - cols_per_chip * j, b - cols_per_chip * j, start + a - first))
    return out


def _w_in_to_padded(w4):
    _, n_r, n_c = w4.shape

    def body(i_ref, o_ref):
        o_ref[...] = jnp.zeros_like(o_ref)
        for j, a, b, p in _w_in_pieces(n_c):
            o_ref[:, p:p + b - a] = i_ref[j, :, a:b]

    return pl.pallas_call(
        body, name="w_in_to_padded", grid=(n_r // W_IN_ROWS,),
        in_specs=[pl.BlockSpec((N_CHIPS, W_IN_ROWS, n_c), lambda i: (0, i, 0))],
        out_specs=pl.BlockSpec((W_IN_ROWS, PROJ_W), lambda i: (i, 0)),
        out_shape=jax.ShapeDtypeStruct((n_r, PROJ_W), w4.dtype), compiler_params=_cparams(("parallel",)),
    )(w4)


def _w_in_to_chips(g, n_c):
    n_r = g.shape[0]

    def body(i_ref, o_ref):
        for j, a, b, p in _w_in_pieces(n_c):
            o_ref[j, :, a:b] = i_ref[:, p:p + b - a]

    return pl.pallas_call(
        body, name="w_in_to_chips", grid=(n_r // W_IN_ROWS,),
        in_specs=[pl.BlockSpec((W_IN_ROWS, PROJ_W), lambda i: (i, 0))],
        out_specs=pl.BlockSpec((N_CHIPS, W_IN_ROWS, n_c), lambda i: (0, i, 0)),
        out_shape=jax.ShapeDtypeStruct((N_CHIPS, n_r, n_c), g.dtype), compiler_params=_cparams(("parallel",)),
    )(g)


def _lane_vec(v, offset=0):
    return jnp.zeros((1, 128), F32).at[0, offset:offset + v.shape[0]].set(v)


def _local_step(x, tgt, mod, p, n_b, t_len, comm=None):
    row1 = lambda v: v.reshape(1, -1)
    a_log, dt_bias = _lane_vec(p["dn_a_log"]), _lane_vec(p["dn_dt_bias"])
    dn_gn, gla_gn = row1(p["dn_norm_g"]), row1(p["gla_norm_g"])
    w2 = jnp.zeros((128, 256), F32).at[8:8 + GATE_RANK].set(p["gla_w_gate2"])
    b2 = row1(p["gla_b_gate"])
    ln0_g, ln0_b, ln1_g, ln1_b, ln2_g, ln2_b = (row1(p[k]) for k in ("ln0_g", "ln0_b", "ln1_g", "ln1_b", "ln2_g", "ln2_b"))
    conv_b = row1(p["ffn_conv_b"])

    x0, h1 = _ln0_fwd(x, ln0_g, ln0_b, mod, n_b, t_len)
    if comm:
        proj, landed_proj = _mm(h1, p["w_in_p"], name="mm_proj", rider=comm.proj_rider())
    else:
        proj = _mm(h1, p["w_in_p"], name="mm_proj")
    qkv = _dn_pre_fwd(proj, p["dn_conv"], n_b, t_len)
    o_half, hist_dn, landed_scan = _dn_scan_fwd(qkv, proj, a_log, dt_bias, dn_gn, n_b, t_len,
                                                rider=comm.scan_rider() if comm else None)
    if comm:
        p = {**p, **comm.weights_from(landed_proj, landed_scan)}
    o_mix, hist_gla = _gla_scan_fwd(proj, w2, b2, gla_gn, o_half, n_b, t_len)
    y = _mm(o_mix, p["w_o"], name="mm_wo")
    x1, h2 = _ln1_fwd(x0, y, ln1_g, ln1_b, mod, n_b, t_len)
    up = _mm(h2, p["w_up"], name="mm_up")
    act = _ffn_act_fwd(up, p["ffn_conv"], conv_b, n_b, t_len)
    y2 = _mm(act, p["w_down"], name="mm_down")

    loss, dx1, dy2, g_ln2_g, g_ln2_b, dgt_f = _ln2_loss_bwd(x1, y2, ln2_g, ln2_b, mod, tgt, n_b, t_len)
    g_w_down = _mm(act, dy2, ta=True, name="mm_g_down")
    d_act = _mm(dy2, p["w_down"], tb=True, name="mm_d_act")
    d_up, g_ffn_conv, g_conv_b = _ffn_act_bwd(up, p["ffn_conv"], conv_b, d_act, n_b, t_len)
    g_w_up = _mm(h2, d_up, ta=True, out_slabs=N_CHIPS, name="mm_g_up")
    if comm:
        dh2, from_sibling = _mm(d_up, p["w_up"], tb=True, name="mm_d_h2", rider=comm.ffn_pair_rider(g_w_up, g_w_down))
    else:
        dh2 = _mm(d_up, p["w_up"], tb=True, name="mm_d_h2")
    dx0, dy, g_ln1_g, g_ln1_b, dmod_1 = _ln1_bwd(x0, y, ln1_g, ln1_b, mod, dx1, dh2, n_b, t_len)
    g_w_o = _mm(o_mix, dy, ta=True, name="mm_g_wo")
    if comm:
        d_o, wo_from_sibling = _mm(dy, p["w_o"], tb=True, name="mm_d_o", rider=comm.wo_pair_rider(g_w_o))
    else:
        d_o = _mm(dy, p["w_o"], tb=True, name="mm_d_o")
    dqkv, d_proj, (g_a_log, g_dt_bias, g_dn_gn), (g_w2, g_b2, g_gla_gn), scan_from_chips = _scan_bwd(
        qkv, proj, (a_log, dt_bias, dn_gn), (w2, b2, gla_gn), hist_dn, hist_gla, d_o, n_b, t_len,
        rider=comm.scan_chips_rider(from_sibling, wo_from_sibling) if comm else None)
    d_proj, g_dn_conv = _dn_pre_bwd(proj, p["dn_conv"], dqkv, d_proj.reshape(n_b * t_len, PROJ_W), n_b, t_len)
    g_w_in_p = _mm(h1, d_proj, ta=True, name="mm_g_win")
    if comm:
        dh1, tail_from_chips = _mm(d_proj, p["w_in_p"], tb=True, name="mm_d_h1",
                                   rider=comm.tail_chips_rider(g_w_in_p))
        from_chips = (scan_from_chips, tail_from_chips)
    else:
        dh1, from_chips = _mm(d_proj, p["w_in_p"], tb=True, name="mm_d_h1"), None
    grad_x, g_ln0_g, g_ln0_b, dmod_0 = _ln0_bwd(x, ln0_g, ln0_b, mod, dx0, dh1, n_b, t_len)

    dmod = jnp.concatenate([dmod_0, dmod_1[:, 0:1], dmod_1[:, 1:3], dgt_f], axis=1)
    grads = {
        "ln0_g": g_ln0_g[0], "ln0_b": g_ln0_b[0], "w_in_p": g_w_in_p, "dn_conv": g_dn_conv,
        "dn_a_log": g_a_log[0, 0:HEADS], "dn_dt_bias": g_dt_bias[0, 0:HEADS], "dn_norm_g": g_dn_gn[0],
        "gla_w_gate2": g_w2[8:8 + GATE_RANK], "gla_b_gate": g_b2[0], "gla_norm_g": g_gla_gn[0],
        "w_o": g_w_o, "ln1_g": g_ln1_g[0], "ln1_b": g_ln1_b[0], "w_up": g_w_up,
        "ffn_conv": jnp.concatenate([g_ffn_conv[0], g_ffn_conv[1]], axis=1),
        "ffn_conv_b": jnp.concatenate([g_conv_b[0, 0], g_conv_b[1, 0]]), "w_down": g_w_down,
        "ln2_g": g_ln2_g[0], "ln2_b": g_ln2_b[0],
    }
    return loss, grad_x, grads, dmod, from_chips


def _col_sum(a):
    def body(a_ref, o_ref):
        o_ref[...] = jnp.sum(a_ref[...], 0, keepdims=True)

    return pl.pallas_call(body, name="col_sum", out_shape=jax.ShapeDtypeStruct((1, a.shape[1]), F32))(a)


def _adamw_math(w, grad, m, v):
    new_m = ADAM_B1 * m + (1.0 - ADAM_B1) * grad
    new_v = ADAM_B2 * v + (1.0 - ADAM_B2) * (grad * grad)
    m_hat = new_m / (1.0 - ADAM_B1 ** ADAM_STEP)
    v_hat = new_v / (1.0 - ADAM_B2 ** ADAM_STEP)
    return -ADAM_LR * (m_hat / (jnp.sqrt(v_hat) + ADAM_EPS) + ADAM_WD * w), new_m, new_v


def _adamw_many(ws, gs, ms, vs):
    n = len(ws)

    def body(*refs):
        for i in range(n):
            w_ref, g_ref, m_ref, v_ref = (refs[k * n + i] for k in range(4))
            d_ref, nm_ref, nv_ref = (refs[(4 + k) * n + i] for k in range(3))
            d_ref[...], nm_ref[...], nv_ref[...] = _adamw_math(w_ref[...], g_ref[...], m_ref[...], v_ref[...])

    outs = pl.pallas_call(
        body, name="adamw_small", out_shape=[jax.ShapeDtypeStruct(w.shape, F32) for w in ws] * 3,
    )(*ws, *gs, *ms, *vs)
    return outs[:n], outs[n:2 * n], outs[2 * n:]


def _adamw(w, g, m, v, name):
    n_r, n_c = w.shape
    if n_r % 8 == 0:
        tr = _pick(n_r, (256, 64, 32, 16, 8))
        grid, blk = (n_r // tr,), pl.BlockSpec((tr, n_c), lambda i: (i, 0))
    else:
        tc = _pick(n_c, (256, 128))
        grid, blk = (n_c // tc,), pl.BlockSpec((n_r, tc), lambda i: (0, i))

    def body(w_ref, g_ref, m_ref, v_ref, d_ref, nm_ref, nv_ref):
        d_ref[...], nm_ref[...], nv_ref[...] = _adamw_math(w_ref[...], g_ref[...], m_ref[...], v_ref[...])

    out = jax.ShapeDtypeStruct(w.shape, F32)
    return pl.pallas_call(
        body, name=name, grid=grid, in_specs=[blk] * 4, out_specs=[blk] * 3, out_shape=[out] * 3,
        compiler_params=_cparams(("parallel",)),
    )(w, g, m, v)


HBM_SPEC = pl.BlockSpec(memory_space=pltpu.HBM)
VMEM_SPEC = pl.BlockSpec(memory_space=pltpu.VMEM)
CHIP_FLIPS = ((1, 0), (0, 1), (1, 1))


def _place():
    return lax.axis_index("x"), lax.axis_index("y"), lax.axis_index("c")


def _flip(v, f):
    return 1 - v if f else v


def _all_gather8(slab, name):
    n_r, n_w = slab.shape

    def body(x_ref, o_ref, s_ref, send_sems, recv_sems, local_sem):
        x, y, c = _place()
        chips = [(_flip(x, fx), _flip(y, fy)) for fx, fy in CHIP_FLIPS]

        def copy(k, block, to, src=None):
            slot = o_ref.at[4 * block[0] + 2 * block[1] + block[2]]
            return pltpu.make_async_remote_copy(src_ref=slot if src is None else src, dst_ref=slot,
                                                send_sem=send_sems.at[k], recv_sem=recv_sems.at[k],
                                                device_id=to, device_id_type=MESH)

        mine = pltpu.make_async_copy(x_ref, o_ref.at[4 * x + 2 * y + c], local_sem)
        mine.start()
        first = [copy(0, (x, y, c), (x, y, 1 - c), src=x_ref)]
        first += [copy(1 + j, (x, y, c), (*chip, c), src=x_ref) for j, chip in enumerate(chips)]
        for cp in first:
            cp.start()
        passed = [copy(4 + j, (*chip, c), (x, y, 1 - c)) for j, chip in enumerate(chips)]
        for j, chip in enumerate(chips):
            copy(1 + j, (*chip, c), (x, y, c)).wait_recv()
            passed[j].start()
        copy(0, (x, y, 1 - c), (x, y, c)).wait_recv()
        for j, chip in enumerate(chips):
            copy(4 + j, (*chip, 1 - c), (x, y, c)).wait_recv()
        for cp in first + passed:
            cp.wait_send()
        mine.wait()
        total = o_ref[0]
        for d in range(1, N_DEV):
            total = total + o_ref[d]
        s_ref[...] = total

    return pl.pallas_call(
        body, name=name, in_specs=[VMEM_SPEC], out_specs=[VMEM_SPEC, VMEM_SPEC],
        out_shape=[jax.ShapeDtypeStruct((N_DEV, n_r, n_w), F32), jax.ShapeDtypeStruct((n_r, n_w), F32)],
        scratch_shapes=[pltpu.SemaphoreType.DMA((N_DEV - 1,)), pltpu.SemaphoreType.DMA((N_DEV - 1,)),
                        pltpu.SemaphoreType.DMA],
    )(slab)


SEQ_ROWS = 8


def _prologue(slab, w_ada_shard, b_shard, rider):
    n_r, n_w = slab.shape
    n_col = w_ada_shard.shape[1]
    r_inputs, r_in_specs, r_out_specs, r_sems, split = _with_rider(rider, 3, 3, 6)

    def body(*refs):
        (x_ref, w_ref, b_ref, g_ref, cond_ref, modr_ref, modp_ref, s1, r1, s2, r2, lsem), parts = split(refs)
        rider.first(*parts)
        x, y, c = _place()
        me = 4 * x + 2 * y + c
        peers = [(_flip(x, k & 4), _flip(y, k & 2), _flip(c, k & 1)) for k in range(1, N_DEV)]
        ids = [4 * px + 2 * py + pc for px, py, pc in peers]

        def exchange(src_of, dst, send_sems, recv_sems, own_sem):
            mine = pltpu.make_async_copy(src_of(me), dst.at[me], own_sem)
            mine.start()
            sends = [pltpu.make_async_remote_copy(src_ref=src_of(ids[k]), dst_ref=dst.at[me], send_sem=send_sems.at[k],
                                                  recv_sem=recv_sems.at[k], device_id=peers[k], device_id_type=MESH)
                     for k in range(N_DEV - 1)]
            for cp in sends:
                cp.start()
            for k in range(N_DEV - 1):
                pltpu.make_async_remote_copy(src_ref=src_of(ids[k]), dst_ref=dst.at[ids[k]], send_sem=send_sems.at[k],
                                             recv_sem=recv_sems.at[k], device_id=peers[k],
                                             device_id_type=MESH).wait_recv()
            for cp in sends:
                cp.wait_send()
            mine.wait()

        exchange(lambda d: x_ref, g_ref, s1, r1, lsem.at[0])
        cond = _silu(g_ref[:, 0:SEQ_ROWS, :].reshape(N_DEV * SEQ_ROWS, n_w))
        cond_ref[...] = cond
        modp_ref[...] = jnp.dot(cond.astype(BF16), w_ref[...].astype(BF16), preferred_element_type=F32) + b_ref[...]
        exchange(lambda d: modp_ref.at[pl.ds(pl.multiple_of(d * SEQ_ROWS, SEQ_ROWS), SEQ_ROWS)], modr_ref, s2, r2,
                 lsem.at[1])
        rider.last(*parts)

    sem7 = pltpu.SemaphoreType.DMA((N_DEV - 1,))
    gathered, cond, mod_recv, *rider_outs = pl.pallas_call(
        body, name="prologue", in_specs=[VMEM_SPEC] * 3 + r_in_specs, out_specs=[VMEM_SPEC] * 3 + r_out_specs,
        out_shape=[jax.ShapeDtypeStruct((N_DEV, n_r, n_w), F32), jax.ShapeDtypeStruct((N_DEV * SEQ_ROWS, n_w), F32),
                   jax.ShapeDtypeStruct((N_DEV, SEQ_ROWS, n_col), F32)] + list(rider.out_shapes),
        scratch_shapes=[pltpu.VMEM((N_DEV * SEQ_ROWS, n_col), F32), sem7, sem7, sem7, sem7,
                        pltpu.SemaphoreType.DMA((2,))] + r_sems,
        compiler_params=pltpu.CompilerParams(vmem_limit_bytes=VMEM_LIMIT),
    )(slab, w_ada_shard, b_shard, *r_inputs)
    return gathered, cond, mod_recv, rider_outs


def _gather_rider(shards):
    n_a = len(shards)

    def plan(ins, outs, sems):
        send_sems, recv_sems = sems
        x, y, c = _place()
        chips = [(_flip(x, fx), _flip(y, fy)) for fx, fy in CHIP_FLIPS]

        def copy(k, slot, chip_of_block, half, to, src=None):
            dst = outs[k].at[chip_of_block, half]
            return pltpu.make_async_remote_copy(src_ref=dst if src is None else src, dst_ref=dst,
                                                send_sem=send_sems.at[k * 6 + slot], recv_sem=recv_sems.at[k * 6 + slot],
                                                device_id=to, device_id_type=MESH)

        first = [copy(k, r, 2 * x + y, c, (*chips[r], c), src=ins[k].at[c]) for k in range(n_a) for r in range(3)]
        return copy, chips, first, (x, y, c)

    def first_step(ins, outs, sems):
        for cp in plan(ins, outs, sems)[2]:
            cp.start()

    def last_step(ins, outs, sems):
        copy, chips, first, (x, y, c) = plan(ins, outs, sems)
        passed = []
        for k in range(n_a):
            for r, (px, py) in enumerate(chips):
                copy(k, r, 2 * px + py, c, (x, y, c)).wait_recv()
                fwd = copy(k, 3 + r, 2 * px + py, c, (x, y, 1 - c))
                fwd.start()
                passed.append(fwd)
        for k in range(n_a):
            for r, (px, py) in enumerate(chips):
                copy(k, 3 + r, 2 * px + py, 1 - c, (x, y, c)).wait_recv()
        for cp in first + passed:
            cp.wait_send()

    return Rider(shards, [jax.ShapeDtypeStruct((N_CHIPS,) + s.shape, s.dtype) for s in shards],
                 [pltpu.SemaphoreType.DMA((6 * n_a,)), pltpu.SemaphoreType.DMA((6 * n_a,))], first_step, last_step)


def _place_own(gathered, shard, chip, name):
    _, _, n_h, n_c = gathered.shape
    th = _pick(n_h, (256, 176, 128))

    def body(sel_ref, s_ref, _, o_ref):
        o_ref[...] = s_ref[...]

    grid_spec = pltpu.PrefetchScalarGridSpec(
        num_scalar_prefetch=1, grid=(2, n_h // th),
        in_specs=[pl.BlockSpec((None, th, n_c), lambda hf, i, sel: (hf, i, 0)), pl.BlockSpec(memory_space=pl.ANY)],
        out_specs=pl.BlockSpec((None, None, th, n_c), lambda hf, i, sel: (sel[0], hf, i, 0)))
    return pl.pallas_call(
        body, name=name, grid_spec=grid_spec, out_shape=jax.ShapeDtypeStruct(gathered.shape, gathered.dtype),
        input_output_aliases={2: 0}, compiler_params=_cparams(("parallel", "parallel")),
    )(chip.reshape(1), shard, gathered)


def _pair_rider(parts):
    n_a = len(parts)

    def plan(ins, outs, sems):
        send_sems, recv_sems = sems
        x, y, c = _place()
        return [pltpu.make_async_remote_copy(src_ref=ins[k].at[:, 1 - c], dst_ref=outs[k], send_sem=send_sems.at[k],
                                             recv_sem=recv_sems.at[k], device_id=(x, y, 1 - c), device_id_type=MESH)
                for k in range(n_a)]

    def first_step(ins, outs, sems):
        for cp in plan(ins, outs, sems):
            cp.start()

    def last_step(ins, outs, sems):
        for cp in plan(ins, outs, sems):
            cp.wait()

    return Rider(parts, [jax.ShapeDtypeStruct((N_CHIPS,) + p.shape[2:], F32) for p in parts],
                 [pltpu.SemaphoreType.DMA((n_a,)), pltpu.SemaphoreType.DMA((n_a,))], first_step, last_step)


def _alone(rider, name):
    n_a = len(rider.inputs)

    def body(*refs):
        parts = (refs[:n_a], refs[n_a:2 * n_a], refs[2 * n_a:])
        rider.first(*parts)
        rider.last(*parts)

    return pl.pallas_call(
        body, name=name, in_specs=[HBM_SPEC] * n_a, out_specs=[HBM_SPEC] * n_a,
        out_shape=rider.out_shapes, scratch_shapes=rider.sems,
    )(*rider.inputs)


def _chips_rider(sums):
    n_a = len(sums)

    def plan(ins, outs, sems):
        send_sems, recv_sems = sems
        x, y, c = _place()
        cps = []
        for k in range(n_a):
            for r, (fx, fy) in enumerate(CHIP_FLIPS):
                px, py = _flip(x, fx), _flip(y, fy)
                cps.append(pltpu.make_async_remote_copy(
                    src_ref=ins[k].at[2 * px + py], dst_ref=outs[k].at[r], send_sem=send_sems.at[3 * k + r],
                    recv_sem=recv_sems.at[3 * k + r], device_id=(px, py, c), device_id_type=MESH))
        return cps

    def first_step(ins, outs, sems):
        for cp in plan(ins, outs, sems):
            cp.start()

    def last_step(ins, outs, sems):
        for cp in plan(ins, outs, sems):
            cp.wait()

    return Rider(sums, [jax.ShapeDtypeStruct((3,) + s.shape[1:], s.dtype) for s in sums],
                 [pltpu.SemaphoreType.DMA((3 * n_a,)), pltpu.SemaphoreType.DMA((3 * n_a,))], first_step, last_step)


def _rs_share(bufs):
    n_a = len(bufs)

    def body(*refs):
        ins, outs = refs[:n_a], refs[n_a:2 * n_a]
        send_sems, recv_sems = refs[2 * n_a:]
        x, y, c = _place()
        sends = [pltpu.make_async_remote_copy(src_ref=ins[k].at[c], dst_ref=outs[k].at[c], send_sem=send_sems.at[k],
                                              recv_sem=recv_sems.at[k], device_id=(x, y, 1 - c), device_id_type=MESH)
                 for k in range(n_a)]
        for cp in sends:
            cp.start()
        for k in range(n_a):
            pltpu.make_async_remote_copy(src_ref=ins[k].at[c], dst_ref=outs[k].at[1 - c], send_sem=send_sems.at[k],
                                         recv_sem=recv_sems.at[k], device_id=(x, y, 1 - c),
                                         device_id_type=MESH).wait_recv()
        for cp in sends:
            cp.wait_send()

    return pl.pallas_call(
        body, name="rs_share", in_specs=[HBM_SPEC] * n_a, out_specs=[HBM_SPEC] * n_a,
        out_shape=[jax.ShapeDtypeStruct(s.shape, F32) for s in bufs],
        input_output_aliases={k: k for k in range(n_a)},
        scratch_shapes=[pltpu.SemaphoreType.DMA((n_a,)), pltpu.SemaphoreType.DMA((n_a,))],
    )(*bufs)


def _pair_add(part, recv, core, name):
    _, _, n_h, n_c = part.shape
    th = _pick(n_h, (256, 176, 128))

    def body(sel_ref, p_ref, r_ref, o_ref):
        o_ref[...] = (p_ref[...] + r_ref[...]).astype(BF16)

    grid_spec = pltpu.PrefetchScalarGridSpec(
        num_scalar_prefetch=1, grid=(N_CHIPS, n_h // th),
        in_specs=[pl.BlockSpec((None, None, th, n_c), lambda j, i, sel: (j, sel[0], i, 0)),
                  pl.BlockSpec((None, th, n_c), lambda j, i, sel: (j, i, 0))],
        out_specs=pl.BlockSpec((None, th, n_c), lambda j, i, sel: (j, i, 0)))
    return pl.pallas_call(
        body, name=name, grid_spec=grid_spec, out_shape=jax.ShapeDtypeStruct(recv.shape, BF16),
        compiler_params=_cparams(("parallel", "parallel")),
    )(core.reshape(1), part, recv)


def _chip_add(sums, recv, chip, core, name):
    _, n_h, n_c = sums.shape
    th = _pick(n_h, (256, 176, 128))

    def body(sel_ref, s_ref, r_ref, o_ref):
        total = s_ref[...].astype(F32)
        for r in range(3):
            total = total + r_ref[r].astype(F32)
        o_ref[...] = total

    grid_spec = pltpu.PrefetchScalarGridSpec(
        num_scalar_prefetch=1, grid=(n_h // th,),
        in_specs=[pl.BlockSpec((None, th, n_c), lambda i, sel: (sel[0], i, 0)),
                  pl.BlockSpec((3, th, n_c), lambda i, sel: (0, i, 0))],
        out_specs=pl.BlockSpec((None, th, n_c), lambda i, sel: (sel[1], i, 0)))
    return pl.pallas_call(
        body, name=name, grid_spec=grid_spec, out_shape=jax.ShapeDtypeStruct((2, n_h, n_c), F32),
        compiler_params=_cparams(("parallel",)),
    )(jnp.stack([chip, core]), sums, recv)


def _row_halves(a):
    return a.reshape(N_CHIPS, 2, -1, a.shape[-1])


class StepComm:
    REST = ("w_o", "w_up", "w_down")

    def __init__(self, core, chip, rest_shards, in_cols):
        self.core, self.chip, self.shards, self.in_cols = core, chip, rest_shards, in_cols

    def proj_rider(self):
        return _gather_rider([self.shards[0], self.shards[2]])

    def scan_rider(self):
        return _gather_rider([self.shards[1]])

    def weights_from(self, landed_proj, landed_scan):
        landed = (landed_proj[0], landed_scan[0], landed_proj[1])
        g_o, g_up, g_down = (_place_own(g, s, self.chip, "place_own_" + n)
                             for g, s, n in zip(landed, self.shards, self.REST))
        return {"w_o": g_o.reshape(-1, D_MODEL), "w_up": g_up.reshape(N_CHIPS, -1, g_up.shape[-1]),
                "w_down": g_down.reshape(-1, D_MODEL)}

    def _add_pairs(self, parts, from_sibling, names):
        return [_pair_add(p, r, self.core, "pair_add_" + n) for p, r, n in zip(parts, from_sibling, names)]

    def ffn_pair_rider(self, g_w_up, g_w_down):
        self.ffn_parts = [_row_halves(g_w_up), _row_halves(g_w_down)]
        return _pair_rider(self.ffn_parts)

    def wo_pair_rider(self, g_w_o):
        self.wo_parts = [_row_halves(g_w_o)]
        return _pair_rider(self.wo_parts)

    def scan_chips_rider(self, ffn_from_sibling, wo_from_sibling):
        self.scan_sums = self._add_pairs(self.wo_parts + self.ffn_parts, list(wo_from_sibling) + list(ffn_from_sibling),
                                         ("w_o", "w_up", "w_down"))
        return _chips_rider(self.scan_sums)

    def tail_chips_rider(self, g_w_in_p):
        parts = [_row_halves(_w_in_to_chips(g_w_in_p, self.in_cols))]
        self.tail_sums = self._add_pairs(parts, _alone(_pair_rider(parts), "rs_pair_tail"), ("w_in",))
        return _chips_rider(self.tail_sums)

    def finish(self, scan_from_chips, tail_from_chips):
        halves = [_chip_add(s, r, self.chip, self.core, "chip_add_" + n)
                  for s, r, n in zip(self.tail_sums + self.scan_sums, list(tail_from_chips) + list(scan_from_chips),
                                     ("w_in", "w_o", "w_up", "w_down"))]
        return [f.reshape(-1, f.shape[-1]) for f in _rs_share(halves)]


SLAB_W = 1024


def _pack(arrays, rows):
    flat = jnp.concatenate([a.reshape(-1).astype(F32) for a in arrays])
    return jnp.pad(flat, (0, rows * SLAB_W - flat.shape[0])).reshape(rows, SLAB_W)


def _unpack(flat, shapes):
    out, off = [], 0
    for s in shapes:
        n = 1
        for d in s:
            n *= d
        out.append(flat[off:off + n].reshape(s))
        off += n
    return out


def _rows_for(arrays_or_shapes):
    n = 0
    for a in arrays_or_shapes:
        s = a if isinstance(a, tuple) else a.shape
        k = 1
        for d in s:
            k *= d
        n += k
    return -(-n // (8 * SLAB_W)) * 8


def kernel(x, c, ln0_g, ln0_b, w_ada, b_ada, w_in, dn_conv, dn_a_log, dn_dt_bias, dn_norm_g, gla_w_gate2, gla_b_gate, gla_norm_g, w_o, ln1_g, ln1_b, ffn_w_up, ffn_conv, ffn_conv_b, ffn_w_down, ln2_g, ln2_b, loss_target, m_ln0_g, m_ln0_b, m_w_ada, m_b_ada, m_w_in, m_dn_conv, m_dn_a_log, m_dn_dt_bias, m_dn_norm_g, m_gla_w_gate2, m_gla_b_gate, m_gla_norm_g, m_w_o, m_ln1_g, m_ln1_b, m_ffn_w_up, m_ffn_conv, m_ffn_conv_b, m_ffn_w_down, m_ln2_g, m_ln2_b, v_ln0_g, v_ln0_b, v_w_ada, v_b_ada, v_w_in, v_dn_conv, v_dn_a_log, v_dn_dt_bias, v_dn_norm_g, v_gla_w_gate2, v_gla_b_gate, v_gla_norm_g, v_w_o, v_ln1_g, v_ln1_b, v_ffn_w_up, v_ffn_conv, v_ffn_conv_b, v_ffn_w_down, v_ln2_g, v_ln2_b):
    n_b, t_len, _ = x.shape
    xi, yi, ci = _place()
    chip = (2 * xi + yi).astype(jnp.int32)
    core = ci.astype(jnp.int32)
    n_all = N_DEV * n_b
    ada_cols = w_ada.shape[2]

    halves = lambda a: a.astype(BF16).reshape(2, a.shape[0] // 2, a.shape[1])
    w_in_halves = halves(w_in[0])
    sharded_small = [dn_conv[0], gla_w_gate2[0], ffn_conv[0]]
    slab = jnp.concatenate([_pack([c], SEQ_ROWS), _pack(sharded_small, _rows_for(sharded_small))], axis=0)
    b_ada_shard = lax.dynamic_slice(b_ada, (0, chip * ada_cols), (1, ada_cols))
    gathered, cond_pad, mod_recv, (g_in,) = _prologue(slab, w_ada[0], b_ada_shard, _gather_rider([w_in_halves]))
    g_in = _place_own(g_in, w_in_halves, chip, "place_own_w_in")
    cond_all = cond_pad.reshape(N_DEV, SEQ_ROWS, D_MODEL)[:, :n_b].reshape(n_all, D_MODEL)
    by_chip = gathered.reshape(N_DEV, -1)[0::2]
    full, off = [], SEQ_ROWS * SLAB_W
    for a in sharded_small:
        blocks = by_chip[:, off:off + a.size].reshape(N_CHIPS, *a.shape)
        full.append(blocks.transpose(1, 0, 2).reshape(a.shape[0], N_CHIPS * a.shape[1]))
        off += a.size
    dn_conv_f, gate2_f, ffn_conv_f = full
    mod = mod_recv[0::2, :n_b].transpose(1, 0, 2).reshape(n_b, 6, D_MODEL)

    comm = StepComm(core, chip, [halves(w_o[0]), halves(ffn_w_up[0]), halves(ffn_w_down[0])], w_in.shape[2])
    params = {
        "w_in_p": _w_in_to_padded(g_in.reshape(N_CHIPS, -1, g_in.shape[-1])),
        "dn_conv": dn_conv_f, "dn_a_log": dn_a_log[0], "dn_dt_bias": dn_dt_bias[0], "dn_norm_g": dn_norm_g[0],
        "gla_w_gate2": gate2_f, "gla_b_gate": gla_b_gate[0], "gla_norm_g": gla_norm_g[0],
        "ln0_g": ln0_g, "ln0_b": ln0_b, "ln1_g": ln1_g[0], "ln1_b": ln1_b[0], "ln2_g": ln2_g[0], "ln2_b": ln2_b[0],
        "ffn_conv": ffn_conv_f, "ffn_conv_b": ffn_conv_b[0],
    }

    loss_row, grad_x, gp, dmod, from_chips = _local_step(
        x.reshape(n_b * t_len, D_MODEL), loss_target.reshape(n_b * t_len, D_MODEL), mod, params, n_b, t_len, comm)
    names = ["ln0_g", "ln0_b", "w_ada", "b_ada", "w_in", "dn_conv", "dn_a_log", "dn_dt_bias", "dn_norm_g",
             "gla_w_gate2", "gla_b_gate", "gla_norm_g", "w_o", "ln1_g", "ln1_b", "ffn_w_up", "ffn_conv", "ffn_conv_b",
             "ffn_w_down", "ln2_g", "ln2_b"]
    weights = dict(zip(names, [ln0_g, ln0_b, w_ada, b_ada, w_in, dn_conv, dn_a_log, dn_dt_bias, dn_norm_g, gla_w_gate2,
                               gla_b_gate, gla_norm_g, w_o, ln1_g, ln1_b, ffn_w_up, ffn_conv, ffn_conv_b, ffn_w_down,
                               ln2_g, ln2_b]))
    m_in = dict(zip(names, [m_ln0_g, m_ln0_b, m_w_ada, m_b_ada, m_w_in, m_dn_conv, m_dn_a_log, m_dn_dt_bias,
                            m_dn_norm_g, m_gla_w_gate2, m_gla_b_gate, m_gla_norm_g, m_w_o, m_ln1_g, m_ln1_b,
                            m_ffn_w_up, m_ffn_conv, m_ffn_conv_b, m_ffn_w_down, m_ln2_g, m_ln2_b]))
    v_in = dict(zip(names, [v_ln0_g, v_ln0_b, v_w_ada, v_b_ada, v_w_in, v_dn_conv, v_dn_a_log, v_dn_dt_bias,
                            v_dn_norm_g, v_gla_w_gate2, v_gla_b_gate, v_gla_norm_g, v_w_o, v_ln1_g, v_ln1_b,
                            v_ffn_w_up, v_ffn_conv, v_ffn_conv_b, v_ffn_w_down, v_ln2_g, v_ln2_b]))
    grads, delta, new_m, new_v = {}, {}, {}, {}

    def adamw_big(n, grad):
        view = (lambda a: a.T) if n == "w_in" else (lambda a: a)
        outs = _adamw(view(weights[n][0]), view(grad), view(m_in[n][0]), view(v_in[n][0]), "adamw_" + n)
        grads[n] = grad[None]
        delta[n], new_m[n], new_v[n] = (view(a)[None] for a in outs)

    g_w_in, g_w_o, g_w_up, g_w_down = comm.finish(*from_chips)

    summed_names = ["loss", "ln0_g", "ln0_b", "dn_conv", "dn_a_log", "dn_dt_bias", "dn_norm_g", "gla_w_gate2",
                    "gla_b_gate", "gla_norm_g", "ln1_g", "ln1_b", "ffn_conv", "ffn_conv_b", "ln2_g", "ln2_b"]
    summed_parts = [loss_row[0, 0:1]] + [gp[n] for n in summed_names[1:]]
    sum_rows = _rows_for(summed_parts)
    slab = jnp.concatenate([_pack(summed_parts, sum_rows), _pack([dmod], _rows_for([dmod]))], axis=0)
    gathered, total = _all_gather8(slab, "reduce_small")
    small_g = dict(zip(summed_names, _unpack(total.reshape(-1), [a.shape for a in summed_parts])))
    loss = small_g["loss"][0]
    dmod_rows = n_b * 6 * D_MODEL // SLAB_W
    dmod_all = gathered[:, sum_rows:sum_rows + dmod_rows, :].reshape(n_all, 6 * D_MODEL)
    for n, grad in (("ffn_w_up", g_w_up), ("ffn_w_down", g_w_down), ("w_o", g_w_o), ("w_in", g_w_in)):
        adamw_big(n, grad)

    g_b_ada = _col_sum(dmod_all)
    dmod_cols = lax.dynamic_slice(dmod_all, (0, chip * ada_cols), (n_all, ada_cols))
    adamw_big("w_ada", _mm(cond_all, dmod_cols, ta=True, name="mm_g_ada"))

    col_block = lambda a: lax.dynamic_slice(a, (0, chip * (a.shape[1] // N_CHIPS)), (a.shape[0], a.shape[1] // N_CHIPS))
    grads.update({
        "ln0_g": small_g["ln0_g"], "ln0_b": small_g["ln0_b"], "b_ada": g_b_ada,
        "dn_conv": col_block(small_g["dn_conv"])[None], "dn_a_log": small_g["dn_a_log"][None],
        "dn_dt_bias": small_g["dn_dt_bias"][None], "dn_norm_g": small_g["dn_norm_g"][None],
        "gla_w_gate2": col_block(small_g["gla_w_gate2"])[None], "gla_b_gate": small_g["gla_b_gate"][None],
        "gla_norm_g": small_g["gla_norm_g"][None], "ln1_g": small_g["ln1_g"][None],
        "ln1_b": small_g["ln1_b"][None], "ffn_conv": col_block(small_g["ffn_conv"])[None],
        "ffn_conv_b": small_g["ffn_conv_b"][None], "ln2_g": small_g["ln2_g"][None], "ln2_b": small_g["ln2_b"][None],
    })
    small = [n for n in names if n not in delta]
    d_s, m_s, v_s = _adamw_many([weights[n] for n in small], [grads[n] for n in small],
                                [m_in[n] for n in small], [v_in[n] for n in small])
    for out, vals in ((delta, d_s), (new_m, m_s), (new_v, v_s)):
        out.update(zip(small, vals))

    return (loss, grad_x.reshape(x.shape), *[grads[n] for n in names], *[delta[n] for n in names],
            *[new_m[n] for n in names], *[new_v[n] for n in names])
```

```python
import functools

import jax
import jax.numpy as jnp
from jax import lax
from jax.experimental import pallas as pl
from jax.experimental.pallas import tpu as pltpu

F32 = jnp.float32
BF16 = jnp.bfloat16
MESH = pl.DeviceIdType.MESH

D_MODEL = 1024
HEADS = 4
HEAD_DIM = 128
GLA_KEY = 64
GATE_RANK = 16
CHUNK = 64
D_FF = 2816
ALPHA = 2.0 ** 0.25
EPS = 1e-6
N_CHIPS = 4
N_DEV = 8

PROJ_W = 3840
OFF_GQ, OFF_GK, OFF_GV, OFF_GG, OFF_SMALL, GLA_W = 0, 256, 512, 1024, 1536, 1792
OFF_Z = 2048
W_IN_COLS = 3608


def _qkv_block(j):
    return jnp.where(j < 2, GLA_W // 128 + j, (OFF_Z + 512) // 128 - 2 + j)

ADAM_LR, ADAM_B1, ADAM_B2, ADAM_EPS, ADAM_WD, ADAM_STEP = 0.001, 0.9, 0.999, 1e-08, 0.01, 10

VMEM_LIMIT = 56 * 1024 * 1024
ROW_TILE = 512


def _cparams(sem):
    return pltpu.CompilerParams(dimension_semantics=sem, vmem_limit_bytes=VMEM_LIMIT)


def _pick(n, prefs):
    for p in prefs:
        if n % p == 0:
            return p
    return n


def _mm(a, b, *, ta=False, tb=False, out_slabs=1, out_dtype=F32, name, rider=None):
    a_slabs = a.shape[0] if a.ndim == 3 else 1
    b_slabs = b.shape[0] if b.ndim == 3 else 1
    assert not (ta and a_slabs > 1)
    a2, b2 = a.shape[-2:], b.shape[-2:]
    if ta:
        k_dim, m_dim = a2
    else:
        m_dim, k_dim = a2[0], a2[1] * a_slabs
    n_dim = b2[0] if tb else b2[1] * b_slabs
    k_slabs = max(a_slabs, b_slabs if tb else 1)
    n_slabs = max(out_slabs, 1 if tb else b_slabs)
    tm = _pick(m_dim, (1024, 1408, 512, 256, 128))
    tn = _pick(n_dim // n_slabs, (1536, 1408, 1280, 1024, 768, 512, 384, 256, 128))
    tk = _pick(k_dim // k_slabs, (1408, 1280, 1024, 512, 256, 128))
    nk, nj = k_dim // tk, n_dim // tn
    nk_a, nk_b, nj_b, nj_o = nk // a_slabs, nk // b_slabs, nj // b_slabs, nj // out_slabs
    dims = (((0 if ta else 1,), (1 if tb else 0,)), ((), ()))

    grid = (m_dim // tm, nj, nk)
    assert out_dtype == F32
    r_inputs, r_in_specs, r_out_specs, r_sems, split = _with_rider(rider, 2, 1, 0)

    def body(*refs):
        (a_ref, b_ref, o_ref), parts = split(refs)
        ride_first, ride_last = _ride(rider, parts, grid)
        if rider is not None:
            ride_first()
        prod = lax.dot_general(a_ref[...].astype(BF16), b_ref[...].astype(BF16), dims, preferred_element_type=F32)
        if nk == 1:
            o_ref[...] = prod
        else:
            _acc(o_ref, prod, pl.program_id(2) == 0)
        if rider is not None:
            ride_last()

    if ta:
        a_spec = pl.BlockSpec((tk, tm), lambda i, j, k: (k, i))
    elif a_slabs > 1:
        a_spec = pl.BlockSpec((None, tm, tk), lambda i, j, k: (k // nk_a, i, k % nk_a))
    else:
        a_spec = pl.BlockSpec((tm, tk), lambda i, j, k: (i, k))
    if tb and b_slabs > 1:
        b_spec = pl.BlockSpec((None, tn, tk), lambda i, j, k: (k // nk_b, j, k % nk_b))
    elif tb:
        b_spec = pl.BlockSpec((tn, tk), lambda i, j, k: (j, k))
    elif b_slabs > 1:
        b_spec = pl.BlockSpec((None, tk, tn), lambda i, j, k: (j // nj_b, k, j % nj_b))
    else:
        b_spec = pl.BlockSpec((tk, tn), lambda i, j, k: (k, j))
    if out_slabs > 1:
        o_spec = pl.BlockSpec((None, tm, tn), lambda i, j, k: (j // nj_o, i, j % nj_o))
        o_shape = (out_slabs, m_dim, n_dim // out_slabs)
    else:
        o_spec, o_shape = pl.BlockSpec((tm, tn), lambda i, j, k: (i, j)), (m_dim, n_dim)
    out, *rider_outs = pl.pallas_call(
        body, name=name, grid=grid,
        in_specs=[a_spec, b_spec] + r_in_specs, out_specs=[o_spec] + r_out_specs,
        out_shape=[jax.ShapeDtypeStruct(o_shape, out_dtype)] + (list(rider.out_shapes) if rider else []),
        scratch_shapes=r_sems,
        compiler_params=_cparams(("arbitrary",) * 3 if rider else ("parallel", "parallel", "arbitrary")),
    )(a, b, *r_inputs)
    return (out, rider_outs) if rider else out


def _ln(x, g, b):
    mu = jnp.mean(x, -1, keepdims=True)
    xc = x - mu
    var = jnp.mean(xc * xc, -1, keepdims=True)
    return xc * lax.rsqrt(var + EPS) * g + b


def _softplus(x):
    return jnp.maximum(x, 0.0) + jnp.log(1.0 + jnp.exp(-jnp.abs(x)))


def _silu(x):
    return x * jax.nn.sigmoid(x)


def _dsilu(x):
    s = jax.nn.sigmoid(x)
    return s * (1.0 + x * (1.0 - s))


def _f_ln0(x, g, b, sc, sh):
    x0 = _ln(x, g, b)
    return x0, x0 * (1.0 + sc) + sh


def _f_ln1(x0, y, gt, g, b, sc, sh):
    x1 = _ln(ALPHA * x0 + (1.0 + gt) * y, g, b)
    return x1, x1 * (1.0 + sc) + sh


def _f_ln2_loss(x1, y2, gt, g, b, tgt):
    x2 = _ln(ALPHA * x1 + (1.0 + gt) * y2, g, b)
    err = x2 - tgt
    per_row = jnp.sum(err * err, -1, keepdims=True) * (0.5 / D_MODEL)
    return jnp.sum(per_row, 0, keepdims=True)


def _row_specs(t_len):
    nt = t_len // ROW_TILE
    row = pl.BlockSpec((ROW_TILE, D_MODEL), lambda b, i: (b * nt + i, 0))
    vec = pl.BlockSpec((1, D_MODEL), lambda b, i: (0, 0))
    mod = pl.BlockSpec((None, 6, D_MODEL), lambda b, i: (b, 0, 0))
    return nt, row, vec, mod


def _first_step():
    return jnp.logical_and(pl.program_id(0) == 0, pl.program_id(1) == 0)


def _acc(ref, val, first, at=(Ellipsis,)):
    @pl.when(first)
    def _():
        ref[at] = val

    @pl.when(jnp.logical_not(first))
    def _():
        ref[at] += val


def _acc_rows(ref, rows, first):
    for i, r in enumerate(rows):
        _acc(ref, r, first, at=(slice(i, i + 1), slice(None)))


def _ln0_fwd(x, g, b, mod, n_b, t_len):
    nt, row, vec, mods = _row_specs(t_len)

    def body(x_ref, g_ref, b_ref, mod_ref, x0_ref, h_ref):
        x0, h = _f_ln0(x_ref[...], g_ref[...], b_ref[...], mod_ref[1:2, :], mod_ref[0:1, :])
        x0_ref[...] = x0
        h_ref[...] = h.astype(BF16)

    return pl.pallas_call(
        body, name="ln0_fwd", grid=(n_b, nt), in_specs=[row, vec, vec, mods], out_specs=[row, row],
        out_shape=[jax.ShapeDtypeStruct(x.shape, F32), jax.ShapeDtypeStruct(x.shape, BF16)],
        compiler_params=_cparams(("parallel", "parallel")),
    )(x, g, b, mod)


def _ln0_bwd(x, g, b, mod, dx0, dh, n_b, t_len):
    nt, row, vec, mods = _row_specs(t_len)
    dmod_spec = pl.BlockSpec((None, 2, D_MODEL), lambda bb, i: (bb, 0, 0))

    def body(x_ref, g_ref, b_ref, mod_ref, dx0_ref, dh_ref, dx_ref, dg_ref, db_ref, dmod_ref):
        _, pull = jax.vjp(_f_ln0, x_ref[...], g_ref[...], b_ref[...], mod_ref[1:2, :], mod_ref[0:1, :])
        dx, dg, db, dsc, dsh = pull((dx0_ref[...], dh_ref[...]))
        dx_ref[...] = dx
        _acc(dg_ref, dg, _first_step())
        _acc(db_ref, db, _first_step())
        _acc_rows(dmod_ref, [dsh, dsc], pl.program_id(1) == 0)

    return pl.pallas_call(
        body, name="ln0_bwd", grid=(n_b, nt), in_specs=[row, vec, vec, mods, row, row],
        out_specs=[row, vec, vec, dmod_spec],
        out_shape=[jax.ShapeDtypeStruct(x.shape, F32), jax.ShapeDtypeStruct((1, D_MODEL), F32),
                   jax.ShapeDtypeStruct((1, D_MODEL), F32), jax.ShapeDtypeStruct((n_b, 2, D_MODEL), F32)],
        compiler_params=_cparams(("arbitrary", "arbitrary")),
    )(x, g, b, mod, dx0, dh)


def _ln1_fwd(x0, y, g, b, mod, n_b, t_len):
    nt, row, vec, mods = _row_specs(t_len)

    def body(x0_ref, y_ref, g_ref, b_ref, mod_ref, x1_ref, h_ref):
        x1, h = _f_ln1(x0_ref[...], y_ref[...], mod_ref[2:3, :], g_ref[...], b_ref[...],
                       mod_ref[4:5, :], mod_ref[3:4, :])
        x1_ref[...] = x1
        h_ref[...] = h.astype(BF16)

    return pl.pallas_call(
        body, name="ln1_fwd", grid=(n_b, nt), in_specs=[row, row, vec, vec, mods], out_specs=[row, row],
        out_shape=[jax.ShapeDtypeStruct(x0.shape, F32), jax.ShapeDtypeStruct(x0.shape, BF16)],
        compiler_params=_cparams(("parallel", "parallel")),
    )(x0, y, g, b, mod)


def _ln1_bwd(x0, y, g, b, mod, dx1, dh, n_b, t_len):
    nt, row, vec, mods = _row_specs(t_len)
    dmod_spec = pl.BlockSpec((None, 3, D_MODEL), lambda bb, i: (bb, 0, 0))

    def body(x0_ref, y_ref, g_ref, b_ref, mod_ref, dx1_ref, dh_ref, dx0_ref, dy_ref, dg_ref, db_ref, dmod_ref):
        _, pull = jax.vjp(_f_ln1, x0_ref[...], y_ref[...], mod_ref[2:3, :], g_ref[...], b_ref[...],
                          mod_ref[4:5, :], mod_ref[3:4, :])
        dx0, dy, dgt, dg, db, dsc, dsh = pull((dx1_ref[...], dh_ref[...]))
        dx0_ref[...] = dx0
        dy_ref[...] = dy.astype(BF16)
        _acc(dg_ref, dg, _first_step())
        _acc(db_ref, db, _first_step())
        _acc_rows(dmod_ref, [dgt, dsh, dsc], pl.program_id(1) == 0)

    return pl.pallas_call(
        body, name="ln1_bwd", grid=(n_b, nt), in_specs=[row, row, vec, vec, mods, row, row],
        out_specs=[row, row, vec, vec, dmod_spec],
        out_shape=[jax.ShapeDtypeStruct(x0.shape, F32), jax.ShapeDtypeStruct(x0.shape, BF16),
                   jax.ShapeDtypeStruct((1, D_MODEL), F32), jax.ShapeDtypeStruct((1, D_MODEL), F32),
                   jax.ShapeDtypeStruct((n_b, 3, D_MODEL), F32)],
        compiler_params=_cparams(("arbitrary", "arbitrary")),
    )(x0, y, g, b, mod, dx1, dh)


def _ln2_loss_bwd(x1, y2, g, b, mod, tgt, n_b, t_len):
    nt, row, vec, mods = _row_specs(t_len)
    one = pl.BlockSpec((1, 128), lambda bb, i: (0, 0))
    dmod_spec = pl.BlockSpec((None, 1, D_MODEL), lambda bb, i: (bb, 0, 0))

    def body(x1_ref, y2_ref, g_ref, b_ref, mod_ref, t_ref, loss_ref, dx1_ref, dy2_ref, dg_ref, db_ref, dgt_ref):
        loss, pull = jax.vjp(functools.partial(_f_ln2_loss, tgt=t_ref[...]), x1_ref[...], y2_ref[...],
                             mod_ref[5:6, :], g_ref[...], b_ref[...])
        dx1, dy2, dgt, dg, db = pull(jnp.ones((1, 1), F32))
        dx1_ref[...] = dx1
        dy2_ref[...] = dy2.astype(BF16)
        _acc(loss_ref, jnp.broadcast_to(loss, (1, 128)), _first_step())
        _acc(dg_ref, dg, _first_step())
        _acc(db_ref, db, _first_step())
        _acc(dgt_ref, dgt, pl.program_id(1) == 0)

    return pl.pallas_call(
        body, name="ln2_loss_bwd", grid=(n_b, nt), in_specs=[row, row, vec, vec, mods, row],
        out_specs=[one, row, row, vec, vec, dmod_spec],
        out_shape=[jax.ShapeDtypeStruct((1, 128), F32), jax.ShapeDtypeStruct(x1.shape, F32),
                   jax.ShapeDtypeStruct(x1.shape, BF16), jax.ShapeDtypeStruct((1, D_MODEL), F32),
                   jax.ShapeDtypeStruct((1, D_MODEL), F32), jax.ShapeDtypeStruct((n_b, 1, D_MODEL), F32)],
        compiler_params=_cparams(("arbitrary", "arbitrary")),
    )(x1, y2, g, b, mod, tgt)


def _shift_down(x, s):
    if s == 0:
        return x
    rows = lax.broadcasted_iota(jnp.int32, x.shape, 0)
    return jnp.where(rows >= s, pltpu.roll(x, s, 0), 0.0)


def _shift_up(x, s):
    if s == 0:
        return x
    t_len = x.shape[0]
    rows = lax.broadcasted_iota(jnp.int32, x.shape, 0)
    return jnp.where(rows < t_len - s, pltpu.roll(x, t_len - s, 0), 0.0)


def _taps(x, k_w):
    return [_shift_down(x, k_w - 1 - k) for k in range(k_w)]


def _conv(taps, w):
    out = w[0:1, :] * taps[0]
    for k in range(1, len(taps)):
        out = out + w[k:k + 1, :] * taps[k]
    return out


def _conv_bwd(taps, w, du):
    k_w = len(taps)
    dx = w[k_w - 1:k_w, :] * du
    for k in range(k_w - 1):
        dx = dx + w[k:k + 1, :] * _shift_up(du, k_w - 1 - k)
    return dx, [jnp.sum(du * taps[k], 0, keepdims=True) for k in range(k_w)]


def _dn_pre_fwd(proj, conv_w, n_b, t_len):
    n_ct = 3 * HEADS
    k_w = conv_w.shape[0]

    def body(x_ref, w_ref, o_ref):
        o_ref[...] = _silu(_conv(_taps(x_ref[...], k_w), w_ref[...]))

    return pl.pallas_call(
        body, name="dn_pre_fwd", grid=(n_ct, n_b),
        in_specs=[pl.BlockSpec((t_len, 128), lambda j, b: (b, _qkv_block(j))),
                  pl.BlockSpec((k_w, 128), lambda j, b: (0, j))],
        out_specs=pl.BlockSpec((t_len, 128), lambda j, b: (b, j)),
        out_shape=jax.ShapeDtypeStruct((n_b * t_len, n_ct * 128), F32),
        compiler_params=_cparams(("parallel", "parallel")),
    )(proj, conv_w)


def _dn_pre_bwd(proj, conv_w, dqkv, d_proj, n_b, t_len):
    n_ct = 3 * HEADS
    k_w = conv_w.shape[0]

    def body(x_ref, w_ref, d_ref, _, dx_ref, dw_ref):
        taps, w = _taps(x_ref[...], k_w), w_ref[...]
        du = d_ref[...] * _dsilu(_conv(taps, w))
        dx, dw = _conv_bwd(taps, w, du)
        dx_ref[...] = dx.astype(BF16)
        _acc_rows(dw_ref, dw, pl.program_id(1) == 0)

    return pl.pallas_call(
        body, name="dn_pre_bwd", grid=(n_ct, n_b),
        in_specs=[pl.BlockSpec((t_len, 128), lambda j, b: (b, _qkv_block(j))),
                  pl.BlockSpec((k_w, 128), lambda j, b: (0, j)),
                  pl.BlockSpec((t_len, 128), lambda j, b: (b, j)), pl.BlockSpec(memory_space=pl.ANY)],
        out_specs=[pl.BlockSpec((t_len, 128), lambda j, b: (b, _qkv_block(j))),
                   pl.BlockSpec((k_w, 128), lambda j, b: (0, j))],
        out_shape=[jax.ShapeDtypeStruct(d_proj.shape, BF16), jax.ShapeDtypeStruct((k_w, n_ct * 128), F32)],
        input_output_aliases={3: 0},
        compiler_params=_cparams(("parallel", "arbitrary")),
    )(proj, conv_w, dqkv, d_proj)


FFN_TC = 256
FFN_NT = D_FF // FFN_TC


def _ffn_specs(t_len):
    blk = lambda off: pl.BlockSpec((t_len, FFN_TC), lambda j, b: (b, j + off))
    wblk = lambda off: pl.BlockSpec((3, FFN_TC), lambda j, b: (0, j + off))
    bblk = lambda off: pl.BlockSpec((1, FFN_TC), lambda j, b: (0, j + off))
    return [blk(0), blk(FFN_NT), wblk(0), wblk(FFN_NT), bblk(0), bblk(FFN_NT)]


def _ffn_act_fwd(up, conv_w, conv_b, n_b, t_len):
    def body(g_ref, v_ref, wg_ref, wv_ref, bg_ref, bv_ref, o_ref):
        ug = _conv(_taps(g_ref[...], 3), wg_ref[...]) + bg_ref[...]
        uv = _conv(_taps(v_ref[...], 3), wv_ref[...]) + bv_ref[...]
        o_ref[...] = (_silu(ug) * uv).astype(BF16)

    return pl.pallas_call(
        body, name="ffn_act_fwd", grid=(FFN_NT, n_b), in_specs=_ffn_specs(t_len),
        out_specs=pl.BlockSpec((t_len, FFN_TC), lambda j, b: (b, j)),
        out_shape=jax.ShapeDtypeStruct((n_b * t_len, D_FF), BF16),
        compiler_params=_cparams(("parallel", "parallel")),
    )(up, up, conv_w, conv_w, conv_b, conv_b)


def _ffn_act_bwd(up, conv_w, conv_b, da, n_b, t_len):
    def body(g_ref, v_ref, wg_ref, wv_ref, bg_ref, bv_ref, da_ref, dup_ref, dw_ref, db_ref):
        first = pl.program_id(1) == 0
        tg, tv, wg, wv = _taps(g_ref[...], 3), _taps(v_ref[...], 3), wg_ref[...], wv_ref[...]
        ug = _conv(tg, wg) + bg_ref[...]
        uv = _conv(tv, wv) + bv_ref[...]
        d_act = da_ref[...]
        sig = jax.nn.sigmoid(ug)
        d_v = d_act * (ug * sig)
        d_g = d_act * uv * (sig * (1.0 + ug * (1.0 - sig)))
        for slab, (taps, w, du) in enumerate(((tg, wg, d_g), (tv, wv, d_v))):
            dx, dw = _conv_bwd(taps, w, du)
            dup_ref[slab] = dx.astype(BF16)
            for k, dw_k in enumerate(dw):
                _acc(dw_ref, dw_k, first, at=(slab, slice(k, k + 1), slice(None)))
            _acc(db_ref, jnp.sum(du, 0, keepdims=True), first, at=(slab, slice(None), slice(None)))

    return pl.pallas_call(
        body, name="ffn_act_bwd", grid=(FFN_NT, n_b),
        in_specs=_ffn_specs(t_len) + [pl.BlockSpec((t_len, FFN_TC), lambda j, b: (b, j))],
        out_specs=[pl.BlockSpec((2, t_len, FFN_TC), lambda j, b: (0, b, j)),
                   pl.BlockSpec((2, 3, FFN_TC), lambda j, b: (0, 0, j)),
                   pl.BlockSpec((2, 1, FFN_TC), lambda j, b: (0, 0, j))],
        out_shape=[jax.ShapeDtypeStruct((2, n_b * t_len, D_FF), BF16),
                   jax.ShapeDtypeStruct((2, 3, D_FF), F32), jax.ShapeDtypeStruct((2, 1, D_FF), F32)],
        compiler_params=_cparams(("parallel", "arbitrary")),
    )(up, up, conv_w, conv_w, conv_b, conv_b, da)


NN = (((2,), (1,)), ((0,), (0,)))
NT = (((2,), (2,)), ((0,), (0,)))
TN = (((1,), (1,)), ((0,), (0,)))


def _iota3(shape, axis):
    return lax.broadcasted_iota(jnp.int32, shape, axis)


def _dg(a, b, dims):
    return lax.dot_general(a, b, dims, preferred_element_type=F32)


def _dot(a, b):
    return _dg(a, b, NN)


def _dot_nt(a, b):
    return _dg(a, b, NT)


def _dot_tn(a, b):
    return _dg(a, b, TN)


def _split(a):
    hi = a.astype(BF16)
    return hi, (a - hi.astype(F32)).astype(BF16)


def _dg3(a, b, dims):
    ah, al = _split(a)
    bh, bl = _split(b)
    return _dg(ah, bh, dims) + (_dg(ah, bl, dims) + _dg(al, bh, dims))


@jax.custom_vjp
def _dot3(a, b):
    return _dg3(a, b, NN)


def _dot3_fwd(a, b):
    return _dg3(a, b, NN), (a, b)


def _dot3_bwd(res, g):
    a, b = res
    return _dg3(g, b, NT), _dg3(a, g, TN)


_dot3.defvjp(_dot3_fwd, _dot3_bwd)


def _lower_ones(g_n, n):
    shape = (g_n, n, n)
    return jnp.where(_iota3(shape, 1) >= _iota3(shape, 2), 1.0, 0.0).astype(BF16)


@jax.custom_vjp
def _chunk_cumsum(x):
    hi, lo = _split(x)
    tri = _lower_ones(x.shape[0], x.shape[1])
    return _dg(tri, hi, NN) + _dg(tri, lo, NN)


def _chunk_cumsum_fwd(x):
    return _chunk_cumsum(x), None


def _chunk_cumsum_bwd(_, g):
    hi, lo = _split(g)
    tri = _lower_ones(g.shape[0], g.shape[1])
    return (_dg(tri, hi, TN) + _dg(tri, lo, TN),)


_chunk_cumsum.defvjp(_chunk_cumsum_fwd, _chunk_cumsum_bwd)


@jax.custom_vjp
def _unit_lower_inv(m):
    n = m.shape[1]
    p = -m
    a = jnp.where(_iota3(m.shape, 1) == _iota3(m.shape, 2), 1.0, 0.0) + p
    span = 2
    while span < n:
        p = _dg3(p, p, NN)
        a = a + _dg3(a, p, NN)
        span *= 2
    return a


def _unit_lower_inv_fwd(m):
    a = _unit_lower_inv(m)
    return a, a


def _unit_lower_inv_bwd(a, da):
    return (-_dg3(a, _dg3(da, a, NT), TN),)


_unit_lower_inv.defvjp(_unit_lower_inv_fwd, _unit_lower_inv_bwd)


@jax.custom_vjp
def _saved_lower_inv(m, a):
    return a


def _saved_lower_inv_fwd(m, a):
    return a, a


def _saved_lower_inv_bwd(a, da):
    return _unit_lower_inv_bwd(a, da)[0], jnp.zeros_like(a)


_saved_lower_inv.defvjp(_saved_lower_inv_fwd, _saved_lower_inv_bwd)


def _rms_gate(o, gn, gate):
    return o * lax.rsqrt(jnp.mean(o * o, -1, keepdims=True) + EPS) * gn * _silu(gate)


def _dn_chains(q, k, v, z, small, s_in, a_log, dt_bias, gn, a_saved=None):
    prep = _dn_prepare(q, k, v, small, a_log, dt_bias, a_saved)
    og, s_out = _dn_advance(prep[:-1], z, s_in, gn)
    return og, s_out, prep[-1]


def _dn_prepare(q, k, v, small, a_log, dt_bias, a_saved=None):
    g_n, c_len = q.shape[0], q.shape[1]
    sq = (g_n, c_len, c_len)
    row, col = _iota3(sq, 1), _iota3(sq, 2)
    causal, strict, eye = row >= col, row > col, row == col
    qn = q * lax.rsqrt(jnp.sum(q * q, -1, keepdims=True) + EPS) * (HEAD_DIM ** -0.5)
    kn = k * lax.rsqrt(jnp.sum(k * k, -1, keepdims=True) + EPS)
    lane = _iota3(small.shape, 2)
    head = jnp.bitwise_and(_iota3(small.shape, 0), HEADS - 1)
    la_all = -jnp.exp(a_log) * _softplus(small + dt_bias)
    la_c = jnp.sum(jnp.where(lane == head, la_all, 0.0), 2, keepdims=True)
    beta = jnp.sum(jnp.where(lane == head + HEADS, jax.nn.sigmoid(small), 0.0), 2, keepdims=True)
    la_b = jnp.broadcast_to(la_c, sq)
    la_r = jnp.sum(jnp.where(eye, la_b, 0.0), 1, keepdims=True)
    g_c = jnp.sum(jnp.where(causal, jnp.broadcast_to(la_r, sq), 0.0), 2, keepdims=True)
    g_r = jnp.sum(jnp.where(row <= col, la_b, 0.0), 1, keepdims=True)
    g_last = jnp.sum(la_c, 1, keepdims=True)
    decay = jnp.exp(jnp.where(causal, g_c - g_r, -1e30))
    e_g = jnp.exp(g_c)
    kb = kn * beta
    m_low = jnp.where(strict, _dot_nt(kb, kn) * decay, 0.0)
    a_inv = _unit_lower_inv(m_low) if a_saved is None else _saved_lower_inv(m_low, a_saved)
    u = _dot3(a_inv, v * beta)
    w = _dot3(a_inv, kb * e_g)
    attn = _dot_nt(qn, kn) * decay
    return u, w, attn, qn * e_g, kn * jnp.exp(g_last - g_c), jnp.exp(g_last), a_inv


def _dn_advance(prep, z, s_in, gn):
    u, w, attn, q_dec, k_dec, g_chunk = prep
    v_new = u - _dot(w, s_in)
    o = _dot(q_dec, s_in) + _dot(attn, v_new)
    s_out = s_in * g_chunk + _dot_tn(k_dec, v_new)
    return _rms_gate(o, gn, z), s_out


def _gla_chains(q, k, v, gate, small, s_in, w2, b2, gn):
    g_n, c_len = q.shape[0], q.shape[1]
    sq, kk = (g_n, c_len, c_len), (g_n, GLA_KEY, GLA_KEY)
    causal = _iota3(sq, 1) >= _iota3(sq, 2)
    la = -_softplus(-(_dot(small, w2) + b2)) * (1.0 / 16.0)
    b = _chunk_cumsum(la)
    b_last = jnp.sum(jnp.where(_iota3(b.shape, 1) == c_len - 1, b, 0.0), 1, keepdims=True)
    q_dec = q * (GLA_KEY ** -0.5) * jnp.exp(b)
    attn = jnp.where(causal, _dot_nt(q_dec, k * jnp.exp(-b)), 0.0)
    o = _dot(q_dec, s_in) + _dot(attn, v)
    g_row = jnp.exp(b_last)
    g_col = jnp.sum(jnp.where(_iota3(kk, 1) == _iota3(kk, 2), jnp.broadcast_to(g_row, kk), 0.0), 2, keepdims=True)
    s_out = s_in * g_col + _dot_tn(k * jnp.exp(b_last - b), v)
    return _rms_gate(o, gn, gate), s_out


def _chunk_spec(n_b, width, col_block, n_c, reverse=False):
    if reverse:
        return pl.BlockSpec((n_b, CHUNK, width), lambda n: (0, n_c - 1 - n, col_block))
    return pl.BlockSpec((n_b, CHUNK, width), lambda n: (0, n, col_block))


def _hist_spec(n_b, d_k, n_c, reverse=False):
    if reverse:
        return pl.BlockSpec((None, n_b * HEADS, d_k, HEAD_DIM), lambda n: (n_c - 1 - n, 0, 0, 0))
    return pl.BlockSpec((None, n_b * HEADS, d_k, HEAD_DIM), lambda n: (n, 0, 0, 0))


def _ainv_spec(n_b, n_c, reverse=False):
    if reverse:
        return pl.BlockSpec((None, n_b * HEADS, CHUNK, CHUNK), lambda n: (n_c - 1 - n, 0, 0, 0))
    return pl.BlockSpec((None, n_b * HEADS, CHUNK, CHUNK), lambda n: (n, 0, 0, 0))


def _stack_chains(ref, n_b, slices):
    return jnp.stack([ref[b, :, sl] for b in range(n_b) for sl in slices], axis=0)


def _per_chain(ref, n_b):
    return jnp.stack([ref[b] for b in range(n_b) for _ in range(HEADS)], axis=0)


def _unstack_chains(ref, val, n_b, slices, offset=0):
    for b in range(n_b):
        for h, sl in enumerate(slices):
            ref[b, :, slice(offset + sl.start, offset + sl.stop)] = val[b * HEADS + h].astype(ref.dtype)


def _gate_weights(w2_ref, b2_ref, n_b):
    w2 = jnp.stack([w2_ref[:, ks] for _ in range(n_b) for ks in GLA_KSL], axis=0)
    b2 = jnp.stack([b2_ref[:, ks] for _ in range(n_b) for ks in GLA_KSL], axis=0)
    return w2, b2


def _sum_heads(val, n_b):
    return [sum(val[b * HEADS + h] for h in range(HEADS)) for b in range(n_b)]


def _const_spec(shape):
    return pl.BlockSpec(shape, lambda n: (0,) * len(shape))


DN_SL = [slice(h * HEAD_DIM, (h + 1) * HEAD_DIM) for h in range(HEADS)]
GLA_KSL = [slice(h * GLA_KEY, (h + 1) * GLA_KEY) for h in range(HEADS)]


class Rider:
    def __init__(self, inputs, out_shapes, sems, first, last):
        self.inputs, self.out_shapes, self.sems, self.first, self.last = inputs, out_shapes, sems, first, last


def _with_rider(rider, n_in, n_out, n_scratch):
    if rider is None:
        return [], [], [], [], lambda refs: (refs, None)
    r_in, r_out, r_sem = len(rider.inputs), len(rider.out_shapes), len(rider.sems)

    def split(refs):
        own_in, rest = refs[:n_in], refs[n_in:]
        rid_in, rest = rest[:r_in], rest[r_in:]
        own_out, rest = rest[:n_out], rest[n_out:]
        rid_out, rest = rest[:r_out], rest[r_out:]
        own_scr, rid_sem = rest[:n_scratch], rest[n_scratch:]
        return own_in + own_out + own_scr, (rid_in, rid_out, rid_sem)

    return list(rider.inputs), [HBM_SPEC] * r_in, [HBM_SPEC] * r_out, list(rider.sems), split


def _ride(rider, parts, grid):
    if rider is None:
        return None, None
    grid = grid if isinstance(grid, tuple) else (grid,)

    def at(step_of):
        hit = pl.program_id(0) == step_of(grid[0])
        for axis in range(1, len(grid)):
            hit = jnp.logical_and(hit, pl.program_id(axis) == step_of(grid[axis]))
        return hit

    def first():
        pl.when(at(lambda n: 0))(lambda: rider.first(*parts))

    def last():
        pl.when(at(lambda n: n - 1))(lambda: rider.last(*parts))

    return first, last


FWD_CHUNKS = 4


def _dn_scan_fwd(qkv, proj, a_log, dt_bias, gn, n_b, t_len, rider=None):
    n_c, n_g, rows = t_len // CHUNK, n_b * HEADS, FWD_CHUNKS * CHUNK
    n_s = n_c // FWD_CHUNKS
    spec = lambda width, col_block: pl.BlockSpec((n_b, rows, width), lambda n: (0, n, col_block))
    kept = lambda d0, d1: pl.BlockSpec((FWD_CHUNKS, n_g, d0, d1), lambda n: (n, 0, 0, 0))
    r_inputs, r_in_specs, r_out_specs, r_sems, split = _with_rider(rider, 8, 3, 1)
    chunk_rows = [slice(j * CHUNK, (j + 1) * CHUNK) for j in range(FWD_CHUNKS)]

    def body(*refs):
        (q_ref, k_ref, v_ref, z_ref, sm_ref, al_ref, dt_ref, gn_ref,
         o_ref, hist_ref, ainv_ref, s_ref), parts = split(refs)
        ride_first, ride_last = _ride(rider, parts, n_s)
        if rider is not None:
            ride_first()

        @pl.when(pl.program_id(0) == 0)
        def _():
            s_ref[...] = jnp.zeros_like(s_ref)

        def stack(ref, slices):
            return jnp.stack([ref[b, rs, sl] for rs in chunk_rows for b in range(n_b) for sl in slices], axis=0)

        prep = _dn_prepare(stack(q_ref, DN_SL), stack(k_ref, DN_SL), stack(v_ref, DN_SL),
                           stack(sm_ref, [slice(None)] * HEADS), al_ref[...], dt_ref[...])
        z, state = stack(z_ref, DN_SL), s_ref[...]
        for j, rs in enumerate(chunk_rows):
            mine = slice(j * n_g, (j + 1) * n_g)
            hist_ref[j] = state
            ainv_ref[j] = prep[-1][mine]
            og, state = _dn_advance(tuple(a[mine] for a in prep[:-1]), z[mine], state, gn_ref[...])
            for b in range(n_b):
                for h, sl in enumerate(DN_SL):
                    o_ref[b, rs, sl] = og[b * HEADS + h].astype(BF16)
        s_ref[...] = state
        if rider is not None:
            ride_last()

    qkv3, proj3 = qkv.reshape(n_b, t_len, -1), proj.reshape(n_b, t_len, -1)
    o, hist, ainv, *rider_outs = pl.pallas_call(
        body, name="dn_scan_fwd", grid=(n_s,),
        in_specs=[spec(512, 0), spec(512, 1), spec(512, 2), spec(512, OFF_Z // 512), spec(128, OFF_SMALL // 128),
                  _const_spec((1, 128)), _const_spec((1, 128)), _const_spec((1, 128))] + r_in_specs,
        out_specs=[spec(512, 0), kept(HEAD_DIM, HEAD_DIM), kept(CHUNK, CHUNK)] + r_out_specs,
        out_shape=[jax.ShapeDtypeStruct((n_b, t_len, 2 * 512), BF16),
                   jax.ShapeDtypeStruct((n_c, n_b * HEADS, HEAD_DIM, HEAD_DIM), F32),
                   jax.ShapeDtypeStruct((n_c, n_b * HEADS, CHUNK, CHUNK), F32)]
        + (list(rider.out_shapes) if rider else []),
        scratch_shapes=[pltpu.VMEM((n_b * HEADS, HEAD_DIM, HEAD_DIM), F32)] + r_sems,
        compiler_params=_cparams(("arbitrary",)),
    )(qkv3, qkv3, qkv3, proj3, proj3, a_log, dt_bias, gn, *r_inputs)
    return o, (hist, ainv), rider_outs


SCAN_BWD_W = OFF_Z + 512


def _scan_bwd(qkv, proj, dn_params, gla_params, hist_dn, hist_gla, d_o, n_b, t_len, rider=None):
    n_c = t_len // CHUNK
    rev = functools.partial(_chunk_spec, n_b, n_c=n_c, reverse=True)
    r_inputs, r_in_specs, r_out_specs, r_sems, split = _with_rider(rider, 19, 8, 2)
    (hist, ainv), do_gla_sl = hist_dn, [slice(512 + sl.start, 512 + sl.stop) for sl in DN_SL]

    def body(*refs):
        (q_ref, k_ref, v_ref, z_ref, sm_ref, gq_ref, gk_ref, gv_ref, gg_ref,
         al_ref, dt_ref, dgn_in_ref, w2_ref, b2_ref, ggn_in_ref, hist_ref, ainv_ref, ghist_ref, do_ref,
         dqkv_ref, dp_ref, dal_ref, ddt_ref, dgn_ref, dw2_ref, db2_ref, dggn_ref, ds_ref, gds_ref), parts = split(refs)
        ride_first, ride_last = _ride(rider, parts, n_c)
        if rider is not None:
            ride_first()
        first = pl.program_id(0) == 0

        @pl.when(first)
        def _():
            ds_ref[...] = jnp.zeros_like(ds_ref)
            gds_ref[...] = jnp.zeros_like(gds_ref)

        small = _per_chain(sm_ref, n_b)
        chains = lambda *a: _dn_chains(*a, a_saved=ainv_ref[...])[:2]
        _, pull = jax.vjp(chains, *(_stack_chains(r, n_b, DN_SL) for r in (q_ref, k_ref, v_ref, z_ref)),
                          small, hist_ref[...], al_ref[...], dt_ref[...], dgn_in_ref[...])
        dq, dk, dv, dz, dsm_dn, ds_in, dal, ddt, dgn = pull((_stack_chains(do_ref, n_b, DN_SL), ds_ref[...]))
        _, gpull = jax.vjp(_gla_chains, _stack_chains(gq_ref, n_b, GLA_KSL), _stack_chains(gk_ref, n_b, GLA_KSL),
                           _stack_chains(gv_ref, n_b, DN_SL), _stack_chains(gg_ref, n_b, DN_SL),
                           small, ghist_ref[...], *_gate_weights(w2_ref, b2_ref, n_b), ggn_in_ref[...])
        gq, gk, gv, gg, dsm_gla, gds_in, dw2, db2, dggn = gpull((_stack_chains(do_ref, n_b, do_gla_sl), gds_ref[...]))

        _unstack_chains(dqkv_ref, dq, n_b, DN_SL)
        _unstack_chains(dqkv_ref, dk, n_b, DN_SL, offset=512)
        _unstack_chains(dqkv_ref, dv, n_b, DN_SL, offset=1024)
        _unstack_chains(dp_ref, dz, n_b, DN_SL, offset=OFF_Z)
        _unstack_chains(dp_ref, gq, n_b, GLA_KSL, offset=OFF_GQ)
        _unstack_chains(dp_ref, gk, n_b, GLA_KSL, offset=OFF_GK)
        _unstack_chains(dp_ref, gv, n_b, DN_SL, offset=OFF_GV)
        _unstack_chains(dp_ref, gg, n_b, DN_SL, offset=OFF_GG)
        ds_ref[...] = ds_in
        gds_ref[...] = gds_in
        for b, (s_dn, s_gla) in enumerate(zip(_sum_heads(dsm_dn, n_b), _sum_heads(dsm_gla, n_b))):
            dp_ref[b, :, OFF_SMALL:OFF_SMALL + 128] = (s_dn + s_gla).astype(BF16)
            dp_ref[b, :, OFF_SMALL + 128:GLA_W] = jnp.zeros((CHUNK, GLA_W - OFF_SMALL - 128), BF16)
        _acc(dal_ref, dal, first)
        _acc(ddt_ref, ddt, first)
        _acc(dgn_ref, dgn, first)
        for h, ks in enumerate(GLA_KSL):
            _acc(dw2_ref, sum(dw2[b * HEADS + h] for b in range(n_b)), first, at=(slice(None), ks))
            _acc(db2_ref, sum(db2[b * HEADS + h] for b in range(n_b)), first, at=(slice(None), ks))
        _acc(dggn_ref, dggn, first)
        if rider is not None:
            ride_last()

    qkv3, proj3, do3 = (a.reshape(n_b, t_len, -1) for a in (qkv, proj, d_o))
    vec = jax.ShapeDtypeStruct((1, 128), F32)
    dqkv, d_proj, dal, ddt, dgn, dw2, db2, dggn, *rider_outs = pl.pallas_call(
        body, name="scan_bwd", grid=(n_c,),
        in_specs=[rev(512, 0), rev(512, 1), rev(512, 2), rev(512, OFF_Z // 512), rev(128, OFF_SMALL // 128),
                  rev(256, OFF_GQ // 256), rev(256, OFF_GK // 256), rev(512, OFF_GV // 512), rev(512, OFF_GG // 512),
                  _const_spec((1, 128)), _const_spec((1, 128)), _const_spec((1, 128)),
                  _const_spec((128, 256)), _const_spec((1, 256)), _const_spec((1, 128)),
                  _hist_spec(n_b, HEAD_DIM, n_c, reverse=True), _ainv_spec(n_b, n_c, reverse=True),
                  _hist_spec(n_b, GLA_KEY, n_c, reverse=True), rev(2 * 512, 0)] + r_in_specs,
        out_specs=[rev(1536, 0), rev(SCAN_BWD_W, 0), _const_spec((1, 128)), _const_spec((1, 128)),
                   _const_spec((1, 128)), _const_spec((128, 256)), _const_spec((1, 256)), _const_spec((1, 128))]
        + r_out_specs,
        out_shape=[jax.ShapeDtypeStruct((n_b, t_len, 1536), F32), jax.ShapeDtypeStruct((n_b, t_len, PROJ_W), BF16),
                   vec, vec, vec, jax.ShapeDtypeStruct((128, 256), F32), jax.ShapeDtypeStruct((1, 256), F32), vec]
        + (list(rider.out_shapes) if rider else []),
        scratch_shapes=[pltpu.VMEM((n_b * HEADS, HEAD_DIM, HEAD_DIM), F32),
                        pltpu.VMEM((n_b * HEADS, GLA_KEY, HEAD_DIM), F32)] + r_sems,
        compiler_params=_cparams(("arbitrary",)),
    )(qkv3, qkv3, qkv3, proj3, proj3, proj3, proj3, proj3, proj3, *dn_params, *gla_params, hist, ainv, hist_gla, do3,
      *r_inputs)
    return dqkv.reshape(n_b * t_len, 1536), d_proj, (dal, ddt, dgn), (dw2, db2, dggn), rider_outs


def _gla_scan_fwd(proj, w2, b2, gn, o_mix, n_b, t_len):
    n_c = t_len // CHUNK
    spec = functools.partial(_chunk_spec, n_b, n_c=n_c)

    def body(q_ref, k_ref, v_ref, g_ref, sm_ref, w2_ref, b2_ref, gn_ref, _, o_ref, hist_ref, s_ref):
        @pl.when(pl.program_id(0) == 0)
        def _():
            s_ref[...] = jnp.zeros_like(s_ref)

        s_in = s_ref[...]
        hist_ref[...] = s_in
        og, s_out = _gla_chains(_stack_chains(q_ref, n_b, GLA_KSL), _stack_chains(k_ref, n_b, GLA_KSL),
                                _stack_chains(v_ref, n_b, DN_SL), _stack_chains(g_ref, n_b, DN_SL),
                                _per_chain(sm_ref, n_b), s_in, *_gate_weights(w2_ref, b2_ref, n_b), gn_ref[...])
        _unstack_chains(o_ref, og, n_b, DN_SL)
        s_ref[...] = s_out

    proj3 = proj.reshape(n_b, t_len, -1)
    o, hist = pl.pallas_call(
        body, name="gla_scan_fwd", grid=(n_c,),
        in_specs=[spec(256, OFF_GQ // 256), spec(256, OFF_GK // 256), spec(512, OFF_GV // 512),
                  spec(512, OFF_GG // 512), spec(128, OFF_SMALL // 128),
                  _const_spec((128, 256)), _const_spec((1, 256)), _const_spec((1, 128)),
                  pl.BlockSpec(memory_space=pl.ANY)],
        out_specs=[spec(512, 1), _hist_spec(n_b, GLA_KEY, n_c)],
        out_shape=[jax.ShapeDtypeStruct(o_mix.shape, BF16),
                   jax.ShapeDtypeStruct((n_c, n_b * HEADS, GLA_KEY, HEAD_DIM), F32)],
        input_output_aliases={8: 0},
        scratch_shapes=[pltpu.VMEM((n_b * HEADS, GLA_KEY, HEAD_DIM), F32)],
        compiler_params=_cparams(("arbitrary",)),
    )(proj3, proj3, proj3, proj3, proj3, w2, b2, gn, o_mix)
    return o.reshape(n_b * t_len, 2 * 512), hist


W_IN_RUNS = ((0, 256, GLA_W), (256, 1536, OFF_Z + 512), (1536, 2048, OFF_Z), (2048, 2056, OFF_SMALL),
             (2056, 3592, 0), (3592, 3608, OFF_SMALL + 8))
W_IN_ROWS = 256


def _w_in_pieces(cols_per_chip):
    out = []
    for first, last, start in W_IN_RUNS:
        for j in range(N_CHIPS):
            a, b = max(first, cols_per_chip * j), min(last, cols_per_chip * (j + 1))
            if a < b:
                out.append((j, a - cols_per_chip * j, b - cols_per_chip * j, start + a - first))
    return out


def _w_in_to_padded(w4):
    _, n_r, n_c = w4.shape

    def body(i_ref, o_ref):
        o_ref[...] = jnp.zeros_like(o_ref)
        for j, a, b, p in _w_in_pieces(n_c):
            o_ref[:, p:p + b - a] = i_ref[j, :, a:b]

    return pl.pallas_call(
        body, name="w_in_to_padded", grid=(n_r // W_IN_ROWS,),
        in_specs=[pl.BlockSpec((N_CHIPS, W_IN_ROWS, n_c), lambda i: (0, i, 0))],
        out_specs=pl.BlockSpec((W_IN_ROWS, PROJ_W), lambda i: (i, 0)),
        out_shape=jax.ShapeDtypeStruct((n_r, PROJ_W), w4.dtype), compiler_params=_cparams(("parallel",)),
    )(w4)


def _w_in_to_chips(g, n_c):
    n_r = g.shape[0]

    def body(i_ref, o_ref):
        for j, a, b, p in _w_in_pieces(n_c):
            o_ref[j, :, a:b] = i_ref[:, p:p + b - a]

    return pl.pallas_call(
        body, name="w_in_to_chips", grid=(n_r // W_IN_ROWS,),
        in_specs=[pl.BlockSpec((W_IN_ROWS, PROJ_W), lambda i: (i, 0))],
        out_specs=pl.BlockSpec((N_CHIPS, W_IN_ROWS, n_c), lambda i: (0, i, 0)),
        out_shape=jax.ShapeDtypeStruct((N_CHIPS, n_r, n_c), g.dtype), compiler_params=_cparams(("parallel",)),
    )(g)


def _lane_vec(v, offset=0):
    return jnp.zeros((1, 128), F32).at[0, offset:offset + v.shape[0]].set(v)


def _local_step(x, tgt, mod, p, n_b, t_len, comm=None):
    row1 = lambda v: v.reshape(1, -1)
    a_log, dt_bias = _lane_vec(p["dn_a_log"]), _lane_vec(p["dn_dt_bias"])
    dn_gn, gla_gn = row1(p["dn_norm_g"]), row1(p["gla_norm_g"])
    w2 = jnp.zeros((128, 256), F32).at[8:8 + GATE_RANK].set(p["gla_w_gate2"])
    b2 = row1(p["gla_b_gate"])
    ln0_g, ln0_b, ln1_g, ln1_b, ln2_g, ln2_b = (row1(p[k]) for k in ("ln0_g", "ln0_b", "ln1_g", "ln1_b", "ln2_g", "ln2_b"))
    conv_b = row1(p["ffn_conv_b"])

    x0, h1 = _ln0_fwd(x, ln0_g, ln0_b, mod, n_b, t_len)
    if comm:
        proj, landed_proj = _mm(h1, p["w_in_p"], name="mm_proj", rider=comm.proj_rider())
    else:
        proj = _mm(h1, p["w_in_p"], name="mm_proj")
    qkv = _dn_pre_fwd(proj, p["dn_conv"], n_b, t_len)
    o_half, hist_dn, landed_scan = _dn_scan_fwd(qkv, proj, a_log, dt_bias, dn_gn, n_b, t_len,
                                                rider=comm.scan_rider() if comm else None)
    if comm:
        p = {**p, **comm.weights_from(landed_proj, landed_scan)}
    o_mix, hist_gla = _gla_scan_fwd(proj, w2, b2, gla_gn, o_half, n_b, t_len)
    y = _mm(o_mix, p["w_o"], name="mm_wo")
    x1, h2 = _ln1_fwd(x0, y, ln1_g, ln1_b, mod, n_b, t_len)
    up = _mm(h2, p["w_up"], name="mm_up")
    act = _ffn_act_fwd(up, p["ffn_conv"], conv_b, n_b, t_len)
    y2 = _mm(act, p["w_down"], name="mm_down")

    loss, dx1, dy2, g_ln2_g, g_ln2_b, dgt_f = _ln2_loss_bwd(x1, y2, ln2_g, ln2_b, mod, tgt, n_b, t_len)
    g_w_down = _mm(act, dy2, ta=True, name="mm_g_down")
    d_act = _mm(dy2, p["w_down"], tb=True, name="mm_d_act")
    d_up, g_ffn_conv, g_conv_b = _ffn_act_bwd(up, p["ffn_conv"], conv_b, d_act, n_b, t_len)
    g_w_up = _mm(h2, d_up, ta=True, out_slabs=N_CHIPS, name="mm_g_up")
    if comm:
        dh2, from_sibling = _mm(d_up, p["w_up"], tb=True, name="mm_d_h2", rider=comm.ffn_pair_rider(g_w_up, g_w_down))
    else:
        dh2 = _mm(d_up, p["w_up"], tb=True, name="mm_d_h2")
    dx0, dy, g_ln1_g, g_ln1_b, dmod_1 = _ln1_bwd(x0, y, ln1_g, ln1_b, mod, dx1, dh2, n_b, t_len)
    g_w_o = _mm(o_mix, dy, ta=True, name="mm_g_wo")
    if comm:
        d_o, wo_from_sibling = _mm(dy, p["w_o"], tb=True, name="mm_d_o", rider=comm.wo_pair_rider(g_w_o))
    else:
        d_o = _mm(dy, p["w_o"], tb=True, name="mm_d_o")
    dqkv, d_proj, (g_a_log, g_dt_bias, g_dn_gn), (g_w2, g_b2, g_gla_gn), scan_from_chips = _scan_bwd(
        qkv, proj, (a_log, dt_bias, dn_gn), (w2, b2, gla_gn), hist_dn, hist_gla, d_o, n_b, t_len,
        rider=comm.scan_chips_rider(from_sibling, wo_from_sibling) if comm else None)
    d_proj, g_dn_conv = _dn_pre_bwd(proj, p["dn_conv"], dqkv, d_proj.reshape(n_b * t_len, PROJ_W), n_b, t_len)
    g_w_in_p = _mm(h1, d_proj, ta=True, name="mm_g_win")
    if comm:
        dh1, tail_from_chips = _mm(d_proj, p["w_in_p"], tb=True, name="mm_d_h1",
                                   rider=comm.tail_chips_rider(g_w_in_p))
        from_chips = (scan_from_chips, tail_from_chips)
    else:
        dh1, from_chips = _mm(d_proj, p["w_in_p"], tb=True, name="mm_d_h1"), None
    grad_x, g_ln0_g, g_ln0_b, dmod_0 = _ln0_bwd(x, ln0_g, ln0_b, mod, dx0, dh1, n_b, t_len)

    dmod = jnp.concatenate([dmod_0, dmod_1[:, 0:1], dmod_1[:, 1:3], dgt_f], axis=1)
    grads = {
        "ln0_g": g_ln0_g[0], "ln0_b": g_ln0_b[0], "w_in_p": g_w_in_p, "dn_conv": g_dn_conv,
        "dn_a_log": g_a_log[0, 0:HEADS], "dn_dt_bias": g_dt_bias[0, 0:HEADS], "dn_norm_g": g_dn_gn[0],
        "gla_w_gate2": g_w2[8:8 + GATE_RANK], "gla_b_gate": g_b2[0], "gla_norm_g": g_gla_gn[0],
        "w_o": g_w_o, "ln1_g": g_ln1_g[0], "ln1_b": g_ln1_b[0], "w_up": g_w_up,
        "ffn_conv": jnp.concatenate([g_ffn_conv[0], g_ffn_conv[1]], axis=1),
        "ffn_conv_b": jnp.concatenate([g_conv_b[0, 0], g_conv_b[1, 0]]), "w_down": g_w_down,
        "ln2_g": g_ln2_g[0], "ln2_b": g_ln2_b[0],
    }
    return loss, grad_x, grads, dmod, from_chips


def _col_sum(a):
    def body(a_ref, o_ref):
        o_ref[...] = jnp.sum(a_ref[...], 0, keepdims=True)

    return pl.pallas_call(body, name="col_sum", out_shape=jax.ShapeDtypeStruct((1, a.shape[1]), F32))(a)


def _adamw_math(w, grad, m, v):
    new_m = ADAM_B1 * m + (1.0 - ADAM_B1) * grad
    new_v = ADAM_B2 * v + (1.0 - ADAM_B2) * (grad * grad)
    m_hat = new_m / (1.0 - ADAM_B1 ** ADAM_STEP)
    v_hat = new_v / (1.0 - ADAM_B2 ** ADAM_STEP)
    return -ADAM_LR * (m_hat / (jnp.sqrt(v_hat) + ADAM_EPS) + ADAM_WD * w), new_m, new_v


def _adamw_many(ws, gs, ms, vs):
    n = len(ws)

    def body(*refs):
        for i in range(n):
            w_ref, g_ref, m_ref, v_ref = (refs[k * n + i] for k in range(4))
            d_ref, nm_ref, nv_ref = (refs[(4 + k) * n + i] for k in range(3))
            d_ref[...], nm_ref[...], nv_ref[...] = _adamw_math(w_ref[...], g_ref[...], m_ref[...], v_ref[...])

    outs = pl.pallas_call(
        body, name="adamw_small", out_shape=[jax.ShapeDtypeStruct(w.shape, F32) for w in ws] * 3,
    )(*ws, *gs, *ms, *vs)
    return outs[:n], outs[n:2 * n], outs[2 * n:]


def _adamw(w, g, m, v, name):
    n_r, n_c = w.shape
    if n_r % 8 == 0:
        tr = _pick(n_r, (256, 64, 32, 16, 8))
        grid, blk = (n_r // tr,), pl.BlockSpec((tr, n_c), lambda i: (i, 0))
    else:
        tc = _pick(n_c, (256, 128))
        grid, blk = (n_c // tc,), pl.BlockSpec((n_r, tc), lambda i: (0, i))

    def body(w_ref, g_ref, m_ref, v_ref, d_ref, nm_ref, nv_ref):
        d_ref[...], nm_ref[...], nv_ref[...] = _adamw_math(w_ref[...], g_ref[...], m_ref[...], v_ref[...])

    out = jax.ShapeDtypeStruct(w.shape, F32)
    return pl.pallas_call(
        body, name=name, grid=grid, in_specs=[blk] * 4, out_specs=[blk] * 3, out_shape=[out] * 3,
        compiler_params=_cparams(("parallel",)),
    )(w, g, m, v)


HBM_SPEC = pl.BlockSpec(memory_space=pltpu.HBM)
VMEM_SPEC = pl.BlockSpec(memory_space=pltpu.VMEM)
CHIP_FLIPS = ((1, 0), (0, 1), (1, 1))


def _place():
    return lax.axis_index("x"), lax.axis_index("y"), lax.axis_index("c")


def _flip(v, f):
    return 1 - v if f else v


def _all_gather8(slab, name):
    n_r, n_w = slab.shape

    def body(x_ref, o_ref, s_ref, send_sems, recv_sems, local_sem):
        x, y, c = _place()
        chips = [(_flip(x, fx), _flip(y, fy)) for fx, fy in CHIP_FLIPS]

        def copy(k, block, to, src=None):
            slot = o_ref.at[4 * block[0] + 2 * block[1] + block[2]]
            return pltpu.make_async_remote_copy(src_ref=slot if src is None else src, dst_ref=slot,
                                                send_sem=send_sems.at[k], recv_sem=recv_sems.at[k],
                                                device_id=to, device_id_type=MESH)

        mine = pltpu.make_async_copy(x_ref, o_ref.at[4 * x + 2 * y + c], local_sem)
        mine.start()
        first = [copy(0, (x, y, c), (x, y, 1 - c), src=x_ref)]
        first += [copy(1 + j, (x, y, c), (*chip, c), src=x_ref) for j, chip in enumerate(chips)]
        for cp in first:
            cp.start()
        passed = [copy(4 + j, (*chip, c), (x, y, 1 - c)) for j, chip in enumerate(chips)]
        for j, chip in enumerate(chips):
            copy(1 + j, (*chip, c), (x, y, c)).wait_recv()
            passed[j].start()
        copy(0, (x, y, 1 - c), (x, y, c)).wait_recv()
        for j, chip in enumerate(chips):
            copy(4 + j, (*chip, 1 - c), (x, y, c)).wait_recv()
        for cp in first + passed:
            cp.wait_send()
        mine.wait()
        total = o_ref[0]
        for d in range(1, N_DEV):
            total = total + o_ref[d]
        s_ref[...] = total

    return pl.pallas_call(
        body, name=name, in_specs=[VMEM_SPEC], out_specs=[VMEM_SPEC, VMEM_SPEC],
        out_shape=[jax.ShapeDtypeStruct((N_DEV, n_r, n_w), F32), jax.ShapeDtypeStruct((n_r, n_w), F32)],
        scratch_shapes=[pltpu.SemaphoreType.DMA((N_DEV - 1,)), pltpu.SemaphoreType.DMA((N_DEV - 1,)),
                        pltpu.SemaphoreType.DMA],
    )(slab)


SEQ_ROWS = 8


def _prologue(slab, w_ada_shard, b_shard, rider):
    n_r, n_w = slab.shape
    n_col = w_ada_shard.shape[1]
    r_inputs, r_in_specs, r_out_specs, r_sems, split = _with_rider(rider, 3, 3, 6)

    def body(*refs):
        (x_ref, w_ref, b_ref, g_ref, cond_ref, modr_ref, modp_ref, s1, r1, s2, r2, lsem), parts = split(refs)
        rider.first(*parts)
        x, y, c = _place()
        me = 4 * x + 2 * y + c
        peers = [(_flip(x, k & 4), _flip(y, k & 2), _flip(c, k & 1)) for k in range(1, N_DEV)]
        ids = [4 * px + 2 * py + pc for px, py, pc in peers]

        def exchange(src_of, dst, send_sems, recv_sems, own_sem):
            mine = pltpu.make_async_copy(src_of(me), dst.at[me], own_sem)
            mine.start()
            sends = [pltpu.make_async_remote_copy(src_ref=src_of(ids[k]), dst_ref=dst.at[me], send_sem=send_sems.at[k],
                                                  recv_sem=recv_sems.at[k], device_id=peers[k], device_id_type=MESH)
                     for k in range(N_DEV - 1)]
            for cp in sends:
                cp.start()
            for k in range(N_DEV - 1):
                pltpu.make_async_remote_copy(src_ref=src_of(ids[k]), dst_ref=dst.at[ids[k]], send_sem=send_sems.at[k],
                                             recv_sem=recv_sems.at[k], device_id=peers[k],
                                             device_id_type=MESH).wait_recv()
            for cp in sends:
                cp.wait_send()
            mine.wait()

        exchange(lambda d: x_ref, g_ref, s1, r1, lsem.at[0])
        cond = _silu(g_ref[:, 0:SEQ_ROWS, :].reshape(N_DEV * SEQ_ROWS, n_w))
        cond_ref[...] = cond
        modp_ref[...] = jnp.dot(cond.astype(BF16), w_ref[...].astype(BF16), preferred_element_type=F32) + b_ref[...]
        rows_of = lambda d: modp_ref.at[pl.ds(pl.multiple_of(d * SEQ_ROWS, SEQ_ROWS), SEQ_ROWS)]
        mine = pltpu.make_async_copy(rows_of(me), modr_ref.at[me], lsem.at[1])
        mine.start()
        chips = [(_flip(x, fx), _flip(y, fy)) for fx, fy in CHIP_FLIPS]
        sends = [pltpu.make_async_remote_copy(src_ref=rows_of(4 * px + 2 * py + c), dst_ref=modr_ref.at[me],
                                              send_sem=s2.at[j], recv_sem=r2.at[j], device_id=(px, py, c),
                                              device_id_type=MESH) for j, (px, py) in enumerate(chips)]
        for cp in sends:
            cp.start()
        for j, (px, py) in enumerate(chips):
            pltpu.make_async_remote_copy(src_ref=rows_of(me), dst_ref=modr_ref.at[4 * px + 2 * py + c], send_sem=s2.at[j],
                                         recv_sem=r2.at[j], device_id=(px, py, c), device_id_type=MESH).wait_recv()
        for cp in sends:
            cp.wait_send()
        mine.wait()
        rider.last(*parts)

    sem7 = pltpu.SemaphoreType.DMA((N_DEV - 1,))
    gathered, cond, mod_recv, *rider_outs = pl.pallas_call(
        body, name="prologue", in_specs=[VMEM_SPEC] * 3 + r_in_specs, out_specs=[VMEM_SPEC] * 3 + r_out_specs,
        out_shape=[jax.ShapeDtypeStruct((N_DEV, n_r, n_w), F32), jax.ShapeDtypeStruct((N_DEV * SEQ_ROWS, n_w), F32),
                   jax.ShapeDtypeStruct((N_DEV, SEQ_ROWS, n_col), F32)] + list(rider.out_shapes),
        scratch_shapes=[pltpu.VMEM((N_DEV * SEQ_ROWS, n_col), F32), sem7, sem7, sem7, sem7,
                        pltpu.SemaphoreType.DMA((2,))] + r_sems,
        compiler_params=pltpu.CompilerParams(vmem_limit_bytes=VMEM_LIMIT),
    )(slab, w_ada_shard, b_shard, *r_inputs)
    return gathered, cond, mod_recv, rider_outs


def _gather_rider(shards):
    n_a = len(shards)

    def plan(ins, outs, sems):
        send_sems, recv_sems = sems
        x, y, c = _place()
        chips = [(_flip(x, fx), _flip(y, fy)) for fx, fy in CHIP_FLIPS]

        def copy(k, slot, chip_of_block, half, to, src=None):
            dst = outs[k].at[chip_of_block, half]
            return pltpu.make_async_remote_copy(src_ref=dst if src is None else src, dst_ref=dst,
                                                send_sem=send_sems.at[k * 6 + slot], recv_sem=recv_sems.at[k * 6 + slot],
                                                device_id=to, device_id_type=MESH)

        first = [copy(k, r, 2 * x + y, c, (*chips[r], c), src=ins[k].at[c]) for k in range(n_a) for r in range(3)]
        return copy, chips, first, (x, y, c)

    def first_step(ins, outs, sems):
        for cp in plan(ins, outs, sems)[2]:
            cp.start()

    def last_step(ins, outs, sems):
        copy, chips, first, (x, y, c) = plan(ins, outs, sems)
        passed = []
        for k in range(n_a):
            for r, (px, py) in enumerate(chips):
                copy(k, r, 2 * px + py, c, (x, y, c)).wait_recv()
                fwd = copy(k, 3 + r, 2 * px + py, c, (x, y, 1 - c))
                fwd.start()
                passed.append(fwd)
        for k in range(n_a):
            for r, (px, py) in enumerate(chips):
                copy(k, 3 + r, 2 * px + py, 1 - c, (x, y, c)).wait_recv()
        for cp in first + passed:
            cp.wait_send()

    return Rider(shards, [jax.ShapeDtypeStruct((N_CHIPS,) + s.shape, s.dtype) for s in shards],
                 [pltpu.SemaphoreType.DMA((6 * n_a,)), pltpu.SemaphoreType.DMA((6 * n_a,))], first_step, last_step)


def _place_own(gathered, shard, chip, name):
    _, _, n_h, n_c = gathered.shape
    th = _pick(n_h, (256, 176, 128))

    def body(sel_ref, s_ref, _, o_ref):
        o_ref[...] = s_ref[...]

    grid_spec = pltpu.PrefetchScalarGridSpec(
        num_scalar_prefetch=1, grid=(2, n_h // th),
        in_specs=[pl.BlockSpec((None, th, n_c), lambda hf, i, sel: (hf, i, 0)), pl.BlockSpec(memory_space=pl.ANY)],
        out_specs=pl.BlockSpec((None, None, th, n_c), lambda hf, i, sel: (sel[0], hf, i, 0)))
    return pl.pallas_call(
        body, name=name, grid_spec=grid_spec, out_shape=jax.ShapeDtypeStruct(gathered.shape, gathered.dtype),
        input_output_aliases={2: 0}, compiler_params=_cparams(("parallel", "parallel")),
    )(chip.reshape(1), shard, gathered)


def _pair_rider(parts):
    n_a = len(parts)

    def plan(ins, outs, sems):
        send_sems, recv_sems = sems
        x, y, c = _place()
        return [pltpu.make_async_remote_copy(src_ref=ins[k].at[:, 1 - c], dst_ref=outs[k], send_sem=send_sems.at[k],
                                             recv_sem=recv_sems.at[k], device_id=(x, y, 1 - c), device_id_type=MESH)
                for k in range(n_a)]

    def first_step(ins, outs, sems):
        for cp in plan(ins, outs, sems):
            cp.start()

    def last_step(ins, outs, sems):
        for cp in plan(ins, outs, sems):
            cp.wait()

    return Rider(parts, [jax.ShapeDtypeStruct((N_CHIPS,) + p.shape[2:], F32) for p in parts],
                 [pltpu.SemaphoreType.DMA((n_a,)), pltpu.SemaphoreType.DMA((n_a,))], first_step, last_step)


def _alone(rider, name):
    n_a = len(rider.inputs)

    def body(*refs):
        parts = (refs[:n_a], refs[n_a:2 * n_a], refs[2 * n_a:])
        rider.first(*parts)
        rider.last(*parts)

    return pl.pallas_call(
        body, name=name, in_specs=[HBM_SPEC] * n_a, out_specs=[HBM_SPEC] * n_a,
        out_shape=rider.out_shapes, scratch_shapes=rider.sems,
    )(*rider.inputs)


def _chips_rider(sums):
    n_a = len(sums)

    def plan(ins, outs, sems):
        send_sems, recv_sems = sems
        x, y, c = _place()
        cps = []
        for k in range(n_a):
            for r, (fx, fy) in enumerate(CHIP_FLIPS):
                px, py = _flip(x, fx), _flip(y, fy)
                cps.append(pltpu.make_async_remote_copy(
                    src_ref=ins[k].at[2 * px + py], dst_ref=outs[k].at[r], send_sem=send_sems.at[3 * k + r],
                    recv_sem=recv_sems.at[3 * k + r], device_id=(px, py, c), device_id_type=MESH))
        return cps

    def first_step(ins, outs, sems):
        for cp in plan(ins, outs, sems):
            cp.start()

    def last_step(ins, outs, sems):
        for cp in plan(ins, outs, sems):
            cp.wait()

    return Rider(sums, [jax.ShapeDtypeStruct((3,) + s.shape[1:], s.dtype) for s in sums],
                 [pltpu.SemaphoreType.DMA((3 * n_a,)), pltpu.SemaphoreType.DMA((3 * n_a,))], first_step, last_step)


def _rs_share(bufs):
    n_a = len(bufs)

    def body(*refs):
        ins, outs = refs[:n_a], refs[n_a:2 * n_a]
        send_sems, recv_sems = refs[2 * n_a:]
        x, y, c = _place()
        sends = [pltpu.make_async_remote_copy(src_ref=ins[k].at[c], dst_ref=outs[k].at[c], send_sem=send_sems.at[k],
                                              recv_sem=recv_sems.at[k], device_id=(x, y, 1 - c), device_id_type=MESH)
                 for k in range(n_a)]
        for cp in sends:
            cp.start()
        for k in range(n_a):
            pltpu.make_async_remote_copy(src_ref=ins[k].at[c], dst_ref=outs[k].at[1 - c], send_sem=send_sems.at[k],
                                         recv_sem=recv_sems.at[k], device_id=(x, y, 1 - c),
                                         device_id_type=MESH).wait_recv()
        for cp in sends:
            cp.wait_send()

    return pl.pallas_call(
        body, name="rs_share", in_specs=[HBM_SPEC] * n_a, out_specs=[HBM_SPEC] * n_a,
        out_shape=[jax.ShapeDtypeStruct(s.shape, F32) for s in bufs],
        input_output_aliases={k: k for k in range(n_a)},
        scratch_shapes=[pltpu.SemaphoreType.DMA((n_a,)), pltpu.SemaphoreType.DMA((n_a,))],
    )(*bufs)


def _pair_add(part, recv, core, name):
    _, _, n_h, n_c = part.shape
    th = _pick(n_h, (256, 176, 128))

    def body(sel_ref, p_ref, r_ref, o_ref):
        o_ref[...] = (p_ref[...] + r_ref[...]).astype(BF16)

    grid_spec = pltpu.PrefetchScalarGridSpec(
        num_scalar_prefetch=1, grid=(N_CHIPS, n_h // th),
        in_specs=[pl.BlockSpec((None, None, th, n_c), lambda j, i, sel: (j, sel[0], i, 0)),
                  pl.BlockSpec((None, th, n_c), lambda j, i, sel: (j, i, 0))],
        out_specs=pl.BlockSpec((None, th, n_c), lambda j, i, sel: (j, i, 0)))
    return pl.pallas_call(
        body, name=name, grid_spec=grid_spec, out_shape=jax.ShapeDtypeStruct(recv.shape, BF16),
        compiler_params=_cparams(("parallel", "parallel")),
    )(core.reshape(1), part, recv)


def _chip_add(sums, recv, chip, core, name):
    _, n_h, n_c = sums.shape
    th = _pick(n_h, (256, 176, 128))

    def body(sel_ref, s_ref, r_ref, o_ref):
        total = s_ref[...].astype(F32)
        for r in range(3):
            total = total + r_ref[r].astype(F32)
        o_ref[...] = total

    grid_spec = pltpu.PrefetchScalarGridSpec(
        num_scalar_prefetch=1, grid=(n_h // th,),
        in_specs=[pl.BlockSpec((None, th, n_c), lambda i, sel: (sel[0], i, 0)),
                  pl.BlockSpec((3, th, n_c), lambda i, sel: (0, i, 0))],
        out_specs=pl.BlockSpec((None, th, n_c), lambda i, sel: (sel[1], i, 0)))
    return pl.pallas_call(
        body, name=name, grid_spec=grid_spec, out_shape=jax.ShapeDtypeStruct((2, n_h, n_c), F32),
        compiler_params=_cparams(("parallel",)),
    )(jnp.stack([chip, core]), sums, recv)


def _row_halves(a):
    return a.reshape(N_CHIPS, 2, -1, a.shape[-1])


class StepComm:
    REST = ("w_o", "w_up", "w_down")

    def __init__(self, core, chip, rest_shards, in_cols):
        self.core, self.chip, self.shards, self.in_cols = core, chip, rest_shards, in_cols

    def proj_rider(self):
        return _gather_rider([self.shards[0], self.shards[2]])

    def scan_rider(self):
        return _gather_rider([self.shards[1]])

    def weights_from(self, landed_proj, landed_scan):
        landed = (landed_proj[0], landed_scan[0], landed_proj[1])
        g_o, g_up, g_down = (_place_own(g, s, self.chip, "place_own_" + n)
                             for g, s, n in zip(landed, self.shards, self.REST))
        return {"w_o": g_o.reshape(-1, D_MODEL), "w_up": g_up.reshape(N_CHIPS, -1, g_up.shape[-1]),
                "w_down": g_down.reshape(-1, D_MODEL)}

    def _add_pairs(self, parts, from_sibling, names):
        return [_pair_add(p, r, self.core, "pair_add_" + n) for p, r, n in zip(parts, from_sibling, names)]

    def ffn_pair_rider(self, g_w_up, g_w_down):
        self.ffn_parts = [_row_halves(g_w_up), _row_halves(g_w_down)]
        return _pair_rider(self.ffn_parts)

    def wo_pair_rider(self, g_w_o):
        self.wo_parts = [_row_halves(g_w_o)]
        return _pair_rider(self.wo_parts)

    def scan_chips_rider(self, ffn_from_sibling, wo_from_sibling):
        self.scan_sums = self._add_pairs(self.wo_parts + self.ffn_parts, list(wo_from_sibling) + list(ffn_from_sibling),
                                         ("w_o", "w_up", "w_down"))
        return _chips_rider(self.scan_sums)

    def tail_chips_rider(self, g_w_in_p):
        parts = [_row_halves(_w_in_to_chips(g_w_in_p, self.in_cols))]
        self.tail_sums = self._add_pairs(parts, _alone(_pair_rider(parts), "rs_pair_tail"), ("w_in",))
        return _chips_rider(self.tail_sums)

    def finish(self, scan_from_chips, tail_from_chips):
        halves = [_chip_add(s, r, self.chip, self.core, "chip_add_" + n)
                  for s, r, n in zip(self.tail_sums + self.scan_sums, list(tail_from_chips) + list(scan_from_chips),
                                     ("w_in", "w_o", "w_up", "w_down"))]
        return [f.reshape(-1, f.shape[-1]) for f in _rs_share(halves)]


SLAB_W = 1024


def _pack(arrays, rows):
    flat = jnp.concatenate([a.reshape(-1).astype(F32) for a in arrays])
    return jnp.pad(flat, (0, rows * SLAB_W - flat.shape[0])).reshape(rows, SLAB_W)


def _unpack(flat, shapes):
    out, off = [], 0
    for s in shapes:
        n = 1
        for d in s:
            n *= d
        out.append(flat[off:off + n].reshape(s))
        off += n
    return out


def _rows_for(arrays_or_shapes):
    n = 0
    for a in arrays_or_shapes:
        s = a if isinstance(a, tuple) else a.shape
        k = 1
        for d in s:
            k *= d
        n += k
    return -(-n // (8 * SLAB_W)) * 8


def kernel(x, c, ln0_g, ln0_b, w_ada, b_ada, w_in, dn_conv, dn_a_log, dn_dt_bias, dn_norm_g, gla_w_gate2, gla_b_gate, gla_norm_g, w_o, ln1_g, ln1_b, ffn_w_up, ffn_conv, ffn_conv_b, ffn_w_down, ln2_g, ln2_b, loss_target, m_ln0_g, m_ln0_b, m_w_ada, m_b_ada, m_w_in, m_dn_conv, m_dn_a_log, m_dn_dt_bias, m_dn_norm_g, m_gla_w_gate2, m_gla_b_gate, m_gla_norm_g, m_w_o, m_ln1_g, m_ln1_b, m_ffn_w_up, m_ffn_conv, m_ffn_conv_b, m_ffn_w_down, m_ln2_g, m_ln2_b, v_ln0_g, v_ln0_b, v_w_ada, v_b_ada, v_w_in, v_dn_conv, v_dn_a_log, v_dn_dt_bias, v_dn_norm_g, v_gla_w_gate2, v_gla_b_gate, v_gla_norm_g, v_w_o, v_ln1_g, v_ln1_b, v_ffn_w_up, v_ffn_conv, v_ffn_conv_b, v_ffn_w_down, v_ln2_g, v_ln2_b):
    n_b, t_len, _ = x.shape
    xi, yi, ci = _place()
    chip = (2 * xi + yi).astype(jnp.int32)
    core = ci.astype(jnp.int32)
    n_all = N_DEV * n_b
    ada_cols = w_ada.shape[2]

    halves = lambda a: a.astype(BF16).reshape(2, a.shape[0] // 2, a.shape[1])
    w_in_halves = halves(w_in[0])
    sharded_small = [dn_conv[0], gla_w_gate2[0], ffn_conv[0]]
    slab = jnp.concatenate([_pack([c], SEQ_ROWS), _pack(sharded_small, _rows_for(sharded_small))], axis=0)
    b_ada_shard = lax.dynamic_slice(b_ada, (0, chip * ada_cols), (1, ada_cols))
    gathered, cond_pad, mod_recv, (g_in,) = _prologue(slab, w_ada[0], b_ada_shard, _gather_rider([w_in_halves]))
    g_in = _place_own(g_in, w_in_halves, chip, "place_own_w_in")
    cond_all = cond_pad.reshape(N_DEV, SEQ_ROWS, D_MODEL)[:, :n_b].reshape(n_all, D_MODEL)
    by_chip = gathered.reshape(N_DEV, -1)[0::2]
    full, off = [], SEQ_ROWS * SLAB_W
    for a in sharded_small:
        blocks = by_chip[:, off:off + a.size].reshape(N_CHIPS, *a.shape)
        full.append(blocks.transpose(1, 0, 2).reshape(a.shape[0], N_CHIPS * a.shape[1]))
        off += a.size
    dn_conv_f, gate2_f, ffn_conv_f = full
    same_core = lax.dynamic_index_in_dim(mod_recv.reshape(N_CHIPS, 2, SEQ_ROWS, -1), core, axis=1, keepdims=False)
    mod = same_core[:, :n_b].transpose(1, 0, 2).reshape(n_b, 6, D_MODEL)

    comm = StepComm(core, chip, [halves(w_o[0]), halves(ffn_w_up[0]), halves(ffn_w_down[0])], w_in.shape[2])
    params = {
        "w_in_p": _w_in_to_padded(g_in.reshape(N_CHIPS, -1, g_in.shape[-1])),
        "dn_conv": dn_conv_f, "dn_a_log": dn_a_log[0], "dn_dt_bias": dn_dt_bias[0], "dn_norm_g": dn_norm_g[0],
        "gla_w_gate2": gate2_f, "gla_b_gate": gla_b_gate[0], "gla_norm_g": gla_norm_g[0],
        "ln0_g": ln0_g, "ln0_b": ln0_b, "ln1_g": ln1_g[0], "ln1_b": ln1_b[0], "ln2_g": ln2_g[0], "ln2_b": ln2_b[0],
        "ffn_conv": ffn_conv_f, "ffn_conv_b": ffn_conv_b[0],
    }

    loss_row, grad_x, gp, dmod, from_chips = _local_step(
        x.reshape(n_b * t_len, D_MODEL), loss_target.reshape(n_b * t_len, D_MODEL), mod, params, n_b, t_len, comm)
    names = ["ln0_g", "ln0_b", "w_ada", "b_ada", "w_in", "dn_conv", "dn_a_log", "dn_dt_bias", "dn_norm_g",
             "gla_w_gate2", "gla_b_gate", "gla_norm_g", "w_o", "ln1_g", "ln1_b", "ffn_w_up", "ffn_conv", "ffn_conv_b",
             "ffn_w_down", "ln2_g", "ln2_b"]
    weights = dict(zip(names, [ln0_g, ln0_b, w_ada, b_ada, w_in, dn_conv, dn_a_log, dn_dt_bias, dn_norm_g, gla_w_gate2,
                               gla_b_gate, gla_norm_g, w_o, ln1_g, ln1_b, ffn_w_up, ffn_conv, ffn_conv_b, ffn_w_down,
                               ln2_g, ln2_b]))
    m_in = dict(zip(names, [m_ln0_g, m_ln0_b, m_w_ada, m_b_ada, m_w_in, m_dn_conv, m_dn_a_log, m_dn_dt_bias,
                            m_dn_norm_g, m_gla_w_gate2, m_gla_b_gate, m_gla_norm_g, m_w_o, m_ln1_g, m_ln1_b,
                            m_ffn_w_up, m_ffn_conv, m_ffn_conv_b, m_ffn_w_down, m_ln2_g, m_ln2_b]))
    v_in = dict(zip(names, [v_ln0_g, v_ln0_b, v_w_ada, v_b_ada, v_w_in, v_dn_conv, v_dn_a_log, v_dn_dt_bias,
                            v_dn_norm_g, v_gla_w_gate2, v_gla_b_gate, v_gla_norm_g, v_w_o, v_ln1_g, v_ln1_b,
                            v_ffn_w_up, v_ffn_conv, v_ffn_conv_b, v_ffn_w_down, v_ln2_g, v_ln2_b]))
    grads, delta, new_m, new_v = {}, {}, {}, {}

    def adamw_big(n, grad):
        view = (lambda a: a.T) if n == "w_in" else (lambda a: a)
        outs = _adamw(view(weights[n][0]), view(grad), view(m_in[n][0]), view(v_in[n][0]), "adamw_" + n)
        grads[n] = grad[None]
        delta[n], new_m[n], new_v[n] = (view(a)[None] for a in outs)

    g_w_in, g_w_o, g_w_up, g_w_down = comm.finish(*from_chips)

    summed_names = ["loss", "ln0_g", "ln0_b", "dn_conv", "dn_a_log", "dn_dt_bias", "dn_norm_g", "gla_w_gate2",
                    "gla_b_gate", "gla_norm_g", "ln1_g", "ln1_b", "ffn_conv", "ffn_conv_b", "ln2_g", "ln2_b"]
    summed_parts = [loss_row[0, 0:1]] + [gp[n] for n in summed_names[1:]]
    sum_rows = _rows_for(summed_parts)
    slab = jnp.concatenate([_pack(summed_parts, sum_rows), _pack([dmod], _rows_for([dmod]))], axis=0)
    gathered, total = _all_gather8(slab, "reduce_small")
    small_g = dict(zip(summed_names, _unpack(total.reshape(-1), [a.shape for a in summed_parts])))
    loss = small_g["loss"][0]
    dmod_rows = n_b * 6 * D_MODEL // SLAB_W
    dmod_all = gathered[:, sum_rows:sum_rows + dmod_rows, :].reshape(n_all, 6 * D_MODEL)
    for n, grad in (("ffn_w_up", g_w_up), ("ffn_w_down", g_w_down), ("w_o", g_w_o), ("w_in", g_w_in)):
        adamw_big(n, grad)

    g_b_ada = _col_sum(dmod_all)
    dmod_cols = lax.dynamic_slice(dmod_all, (0, chip * ada_cols), (n_all, ada_cols))
    adamw_big("w_ada", _mm(cond_all, dmod_cols, ta=True, name="mm_g_ada"))

    col_block = lambda a: lax.dynamic_slice(a, (0, chip * (a.shape[1] // N_CHIPS)), (a.shape[0], a.shape[1] // N_CHIPS))
    grads.update({
        "ln0_g": small_g["ln0_g"], "ln0_b": small_g["ln0_b"], "b_ada": g_b_ada,
        "dn_conv": col_block(small_g["dn_conv"])[None], "dn_a_log": small_g["dn_a_log"][None],
        "dn_dt_bias": small_g["dn_dt_bias"][None], "dn_norm_g": small_g["dn_norm_g"][None],
        "gla_w_gate2": col_block(small_g["gla_w_gate2"])[None], "gla_b_gate": small_g["gla_b_gate"][None],
        "gla_norm_g": small_g["gla_norm_g"][None], "ln1_g": small_g["ln1_g"][None],
        "ln1_b": small_g["ln1_b"][None], "ffn_conv": col_block(small_g["ffn_conv"])[None],
        "ffn_conv_b": small_g["ffn_conv_b"][None], "ln2_g": small_g["ln2_g"][None], "ln2_b": small_g["ln2_b"][None],
    })
    small = [n for n in names if n not in delta]
    d_s, m_s, v_s = _adamw_many([weights[n] for n in small], [grads[n] for n in small],
                                [m_in[n] for n in small], [v_in[n] for n in small])
    for out, vals in ((delta, d_s), (new_m, m_s), (new_v, v_s)):
        out.update(zip(small, vals))

    return (loss, grad_x.reshape(x.shape), *[grads[n] for n in names], *[delta[n] for n in names],
            *[new_m[n] for n in names], *[new_v[n] for n in names])
```
